```python
import math
import jax, jax.numpy as jnp
from jax import lax
import numpy as np

D_MODEL = 1024
BATCH = 8
SEQ = 8192
DEPTH = 1

N_Q_HEADS = 8
N_KV_HEADS = 2
HEAD_DIM = 64
WINDOW = 128
ATTN_BLOCK = WINDOW
ROPE_THETA = 10000.0
MLSTM_HEADS = 4
MLSTM_HEAD_DIM = 128
MLSTM_CHUNK = 64
CONV_WIDTH = 4
D_FF = -(-8 * D_MODEL // (3 * 256)) * 256
NORM_EPS = 1e-6

ATTN_Q_WIDTH = N_Q_HEADS * HEAD_DIM
ATTN_KV_WIDTH = N_KV_HEADS * HEAD_DIM
MLSTM_WIDTH = MLSTM_HEADS * MLSTM_HEAD_DIM
IN_SPLIT_SIZES = (ATTN_Q_WIDTH, ATTN_KV_WIDTH, ATTN_KV_WIDTH,
                  MLSTM_WIDTH, MLSTM_WIDTH, MLSTM_WIDTH, MLSTM_WIDTH,
                  MLSTM_HEADS, MLSTM_HEADS, D_MODEL, D_MODEL)
IN_WIDTH = sum(IN_SPLIT_SIZES)

kernel_name = 'hybrid_swa_sink_mlstm_gated_block'


def rms_norm(x, g):
    xf = x.astype(jnp.float32)
    y = xf * lax.rsqrt(jnp.mean(xf * xf, axis=-1, keepdims=True) + NORM_EPS)
    return (y * g.astype(jnp.float32)).astype(x.dtype)


def modulate(h, shift, scale):
    return h * (1 + scale[:, None, :]) + shift[:, None, :]


def rope(t, positions):
    half = HEAD_DIM // 2
    inv_freq = ROPE_THETA ** (-2.0 * jnp.arange(half, dtype=jnp.float32) / HEAD_DIM)
    ang = positions.astype(jnp.float32)[..., None] * inv_freq
    cos = jnp.cos(ang)[:, :, None, :]
    sin = jnp.sin(ang)[:, :, None, :]
    tf = t.astype(jnp.float32)
    t1, t2 = tf[..., :half], tf[..., half:]
    return jnp.concatenate([t1 * cos - t2 * sin, t2 * cos + t1 * sin], axis=-1).astype(t.dtype)


def causal_depthwise_conv(u, w, b):
    S = u.shape[1]
    up = jnp.pad(u, ((0, 0), (CONV_WIDTH - 1, 0), (0, 0)))
    out = b
    for j in range(CONV_WIDTH):
        out = out + up[:, j:j + S] * w[j]
    return out


def sliding_window_attention(q, k, v, sinks):
    B, S = q.shape[0], q.shape[1]
    nb = S // ATTN_BLOCK
    g = N_Q_HEADS // N_KV_HEADS
    qb = q.astype(jnp.float32).reshape(B, nb, ATTN_BLOCK, N_KV_HEADS, g, HEAD_DIM) * (HEAD_DIM ** -0.5)

    def band(t):
        t = t.astype(jnp.float32).reshape(B, nb, ATTN_BLOCK, N_KV_HEADS, HEAD_DIM)
        prev = jnp.pad(t, ((0, 0), (1, 0), (0, 0), (0, 0), (0, 0)))[:, :-1]
        return jnp.concatenate([prev, t], axis=2)

    kb, vb = band(k), band(v)
    s = jnp.einsum('bnqhgd,bnkhd->bnhgqk', qb, kb)
    qi = jnp.arange(ATTN_BLOCK)[:, None]
    kj = jnp.arange(2 * ATTN_BLOCK)[None, :]
    rel = kj - ATTN_BLOCK
    in_band = (rel <= qi) & (qi - rel < WINDOW)
    blk = jnp.arange(nb)[:, None, None]
    mask = in_band[None] & ((blk > 0) | (kj[None] >= ATTN_BLOCK))
    s = jnp.where(mask[None, :, None, None], s, -jnp.inf)
    sink = sinks.astype(jnp.float32).reshape(N_KV_HEADS, g)[None, None, :, :, None, None]
    m = jnp.maximum(jnp.max(s, axis=-1, keepdims=True), sink)
    p = jnp.exp(s - m)
    p = p / (jnp.sum(p, axis=-1, keepdims=True) + jnp.exp(sink - m))
    o = jnp.einsum('bnhgqk,bnkhd->bnqhgd', p, vb)
    return o.reshape(B, S, N_Q_HEADS * HEAD_DIM)


def mlstm_chunkwise(q, k, v, i_pre, f_pre):
    B, S, H, D = q.shape
    L = MLSTM_CHUNK
    nc = S // L

    def chunks(t):
        return t.astype(jnp.float32).reshape(B, nc, L, H, D).transpose(0, 3, 1, 2, 4)

    qc = chunks(q)
    kc = chunks(k) * (D ** -0.5)
    vc = chunks(v)
    ig = i_pre.astype(jnp.float32).reshape(B, nc, L, H).transpose(0, 3, 1, 2)
    logf = jax.nn.log_sigmoid(f_pre.astype(jnp.float32)).reshape(B, nc, L, H).transpose(0, 3, 1, 2)
    b = jnp.cumsum(logf, axis=-1)
    b_last = b[..., -1]
    a = b_last[..., None] - b + ig

    def step(carry, inp):
        C, n, m = carry
        k_c, v_c, a_c, bl_c = inp
        m_new = jnp.maximum(bl_c + m, jnp.max(a_c, axis=-1))
        decay = jnp.exp(bl_c + m - m_new)
        kw = k_c * jnp.exp(a_c - m_new[..., None])[..., None]
        C_new = decay[..., None, None] * C + jnp.einsum('bhld,bhle->bhde', kw, v_c)
        n_new = decay[..., None] * n + jnp.sum(kw, axis=-2)
        return (C_new, n_new, m_new), (C, n, m)

    init = (jnp.zeros((B, H, D, D), jnp.float32), jnp.zeros((B, H, D), jnp.float32),
            jnp.zeros((B, H), jnp.float32))
    xs = (kc.transpose(2, 0, 1, 3, 4), vc.transpose(2, 0, 1, 3, 4),
          a.transpose(2, 0, 1, 3), b_last.transpose(2, 0, 1))
    _, (C_prev, n_prev, m_prev) = lax.scan(step, init, xs)
    C_prev = C_prev.transpose(1, 2, 0, 3, 4)
    n_prev = n_prev.transpose(1, 2, 0, 3)
    m_prev = m_prev.transpose(1, 2, 0)

    causal = jnp.tril(jnp.ones((L, L), dtype=bool))
    d_mat = jnp.where(causal, b[..., :, None] - b[..., None, :] + ig[..., None, :], -jnp.inf)
    inter = b + m_prev[..., None]
    m_t = jnp.maximum(inter, jnp.max(d_mat, axis=-1))
    w_intra = jnp.exp(d_mat - m_t[..., None])
    w_inter = jnp.exp(inter - m_t)
    scores = jnp.einsum('bhctd,bhcsd->bhcts', qc, kc) * w_intra
    num = jnp.einsum('bhcts,bhcsd->bhctd', scores, vc) + \
        w_inter[..., None] * jnp.einsum('bhctd,bhcde->bhcte', qc, C_prev)
    den = jnp.sum(scores, axis=-1) + w_inter * jnp.einsum('bhctd,bhcd->bhct', qc, n_prev)
    h = num / jnp.maximum(jnp.abs(den), jnp.exp(-m_t))[..., None]
    return h.transpose(0, 2, 3, 1, 4).reshape(B, S, H, D)


def head_layer_norm(h, w):
    mu = jnp.mean(h, axis=-1, keepdims=True)
    var = jnp.mean(jnp.square(h - mu), axis=-1, keepdims=True)
    y = (h - mu) * lax.rsqrt(var + NORM_EPS)
    return y.reshape(h.shape[0], h.shape[1], -1) * w.astype(jnp.float32)


def _fwd_setup_inputs(seed: int = 0) -> dict:
    key = jax.random.key(seed)
    ks = jax.random.split(key, 24)
    f32 = jnp.float32

    def dense(k, shape, fan_in, scale=1.0):
        return jax.random.normal(k, shape, f32) * (scale * fan_in ** -0.5)

    def gain(k, width):
        return 1.0 + 0.05 * jax.random.normal(k, (DEPTH, width), f32)

    x = jax.random.normal(ks[0], (BATCH, SEQ, D_MODEL), f32)
    c = jax.random.normal(ks[1], (BATCH, D_MODEL), f32)
    offsets = jax.random.randint(ks[2], (BATCH, 1), 0, 4096, dtype=jnp.int32)
    positions = (offsets + jnp.arange(SEQ, dtype=jnp.int32)[None, :]).astype(jnp.int32)
    w_ada = dense(ks[3], (DEPTH, D_MODEL, 6 * D_MODEL), D_MODEL, 0.5)
    b_ada = 0.02 * jax.random.normal(ks[4], (DEPTH, 6 * D_MODEL), f32)
    g_pre_mix = gain(ks[5], D_MODEL)
    g_post_mix = gain(ks[6], D_MODEL)
    w_in = dense(ks[7], (DEPTH, D_MODEL, IN_WIDTH), D_MODEL)
    b_i = 0.1 * jax.random.normal(ks[8], (DEPTH, MLSTM_HEADS), f32)
    b_f = jnp.linspace(3.0, 6.0, MLSTM_HEADS, dtype=f32)[None, :] + \
        0.1 * jax.random.normal(ks[9], (DEPTH, MLSTM_HEADS), f32)
    b_if = jnp.concatenate([b_i, b_f], axis=-1)
    conv_w = dense(ks[10], (DEPTH, CONV_WIDTH, 2 * MLSTM_WIDTH), CONV_WIDTH)
    conv_b = 0.02 * jax.random.normal(ks[11], (DEPTH, 2 * MLSTM_WIDTH), f32)
    attn_sinks = 0.5 * jax.random.normal(ks[12], (DEPTH, N_Q_HEADS), f32)
    mlstm_norm_w = gain(ks[13], MLSTM_WIDTH)
    w_branch_attn = dense(ks[14], (DEPTH, ATTN_Q_WIDTH, D_MODEL), ATTN_Q_WIDTH)
    w_branch_mlstm = dense(ks[15], (DEPTH, MLSTM_WIDTH, D_MODEL), MLSTM_WIDTH)
    w_out = dense(ks[16], (DEPTH, D_MODEL, D_MODEL), D_MODEL)
    g_pre_ffn = gain(ks[17], D_MODEL)
    g_post_ffn = gain(ks[18], D_MODEL)
    w_ffn_gate = dense(ks[19], (DEPTH, D_MODEL, D_FF), D_MODEL)
    w_ffn_up = dense(ks[20], (DEPTH, D_MODEL, D_FF), D_MODEL)
    w_ffn_down = dense(ks[21], (DEPTH, D_FF, D_MODEL), D_FF)
    return {'x': x, 'c': c, 'positions': positions, 'w_ada': w_ada, 'b_ada': b_ada,
            'g_pre_mix': g_pre_mix, 'g_post_mix': g_post_mix, 'w_in': w_in, 'b_if': b_if,
            'conv_w': conv_w, 'conv_b': conv_b, 'attn_sinks': attn_sinks,
            'mlstm_norm_w': mlstm_norm_w, 'w_branch_attn': w_branch_attn,
            'w_branch_mlstm': w_branch_mlstm, 'w_out': w_out, 'g_pre_ffn': g_pre_ffn,
            'g_post_ffn': g_post_ffn, 'w_ffn_gate': w_ffn_gate, 'w_ffn_up': w_ffn_up,
            'w_ffn_down': w_ffn_down}


def _fwd_reference(x, c, positions, w_ada, b_ada, g_pre_mix, g_post_mix, w_in, b_if, conv_w, conv_b,
              attn_sinks, mlstm_norm_w, w_branch_attn, w_branch_mlstm, w_out, g_pre_ffn,
              g_post_ffn, w_ffn_gate, w_ffn_up, w_ffn_down):
    B, S, _ = x.shape
    split_points = np.cumsum(IN_SPLIT_SIZES)[:-1].tolist()
    for l in range(DEPTH):
        mod = c @ w_ada[l] + b_ada[l]
        shift_m, scale_m, gate_m, shift_f, scale_f, gate_f = jnp.split(mod, 6, axis=-1)

        h = modulate(rms_norm(x, g_pre_mix[l]), shift_m, scale_m)
        proj = h @ w_in[l]
        q_a, k_a, v_a, q_m, k_m, v_m, o_m, i_m, f_m, g_a, g_m = jnp.split(proj, split_points, axis=-1)

        q_a = rope(q_a.reshape(B, S, N_Q_HEADS, HEAD_DIM), positions)
        k_a = rope(k_a.reshape(B, S, N_KV_HEADS, HEAD_DIM), positions)
        v_a = v_a.reshape(B, S, N_KV_HEADS, HEAD_DIM)
        y_a = sliding_window_attention(q_a, k_a, v_a, attn_sinks[l]).astype(x.dtype)

        qk_m = jax.nn.silu(causal_depthwise_conv(jnp.concatenate([q_m, k_m], axis=-1), conv_w[l], conv_b[l]))
        q_m, k_m = jnp.split(qk_m, 2, axis=-1)
        i_pre = i_m + b_if[l][:MLSTM_HEADS]
        f_pre = f_m + b_if[l][MLSTM_HEADS:]
        h_m = mlstm_chunkwise(q_m.reshape(B, S, MLSTM_HEADS, MLSTM_HEAD_DIM),
                              k_m.reshape(B, S, MLSTM_HEADS, MLSTM_HEAD_DIM),
                              v_m.reshape(B, S, MLSTM_HEADS, MLSTM_HEAD_DIM), i_pre, f_pre)
        y_m = (jax.nn.sigmoid(o_m.astype(jnp.float32)) * head_layer_norm(h_m, mlstm_norm_w[l])).astype(x.dtype)

        merged = jax.nn.sigmoid(g_a) * (y_a @ w_branch_attn[l]) + jax.nn.sigmoid(g_m) * (y_m @ w_branch_mlstm[l])
        mix = merged @ w_out[l]
        x = x + gate_m[:, None, :] * rms_norm(mix, g_post_mix[l])

        h2 = modulate(rms_norm(x, g_pre_ffn[l]), shift_f, scale_f)
        ff = (jax.nn.silu(h2 @ w_ffn_gate[l]) * (h2 @ w_ffn_up[l])) @ w_ffn_down[l]
        x = x + gate_f[:, None, :] * rms_norm(ff, g_post_ffn[l])
    return x


import jax as _jax
import jax.numpy as _jnp

TWIN_FORMAT = 'train_step'
FWD_PARAMS = ['x', 'c', 'positions', 'w_ada', 'b_ada', 'g_pre_mix', 'g_post_mix', 'w_in', 'b_if', 'conv_w', 'conv_b', 'attn_sinks', 'mlstm_norm_w', 'w_branch_attn', 'w_branch_mlstm', 'w_out', 'g_pre_ffn', 'g_post_ffn', 'w_ffn_gate', 'w_ffn_up', 'w_ffn_down']
TWIN_WEIGHTS = ['w_ada', 'b_ada', 'g_pre_mix', 'g_post_mix', 'w_in', 'b_if', 'conv_w', 'conv_b', 'attn_sinks', 'mlstm_norm_w', 'w_branch_attn', 'w_branch_mlstm', 'w_out', 'g_pre_ffn', 'g_post_ffn', 'w_ffn_gate', 'w_ffn_up', 'w_ffn_down']
TWIN_DIFF_INPUT = 'x'
TWIN_INPUTS = ['x', 'c', 'positions', 'w_ada', 'b_ada', 'g_pre_mix', 'g_post_mix', 'w_in', 'b_if', 'conv_w', 'conv_b', 'attn_sinks', 'mlstm_norm_w', 'w_branch_attn', 'w_branch_mlstm', 'w_out', 'g_pre_ffn', 'g_post_ffn', 'w_ffn_gate', 'w_ffn_up', 'w_ffn_down', 'loss_target', 'm_w_ada', 'm_b_ada', 'm_g_pre_mix', 'm_g_post_mix', 'm_w_in', 'm_b_if', 'm_conv_w', 'm_conv_b', 'm_attn_sinks', 'm_mlstm_norm_w', 'm_w_branch_attn', 'm_w_branch_mlstm', 'm_w_out', 'm_g_pre_ffn', 'm_g_post_ffn', 'm_w_ffn_gate', 'm_w_ffn_up', 'm_w_ffn_down', 'v_w_ada', 'v_b_ada', 'v_g_pre_mix', 'v_g_post_mix', 'v_w_in', 'v_b_if', 'v_conv_w', 'v_conv_b', 'v_attn_sinks', 'v_mlstm_norm_w', 'v_w_branch_attn', 'v_w_branch_mlstm', 'v_w_out', 'v_g_pre_ffn', 'v_g_post_ffn', 'v_w_ffn_gate', 'v_w_ffn_up', 'v_w_ffn_down']
TWIN_OUTPUTS = ['loss', 'grad_x', 'grad_w_ada', 'grad_b_ada', 'grad_g_pre_mix', 'grad_g_post_mix', 'grad_w_in', 'grad_b_if', 'grad_conv_w', 'grad_conv_b', 'grad_attn_sinks', 'grad_mlstm_norm_w', 'grad_w_branch_attn', 'grad_w_branch_mlstm', 'grad_w_out', 'grad_g_pre_ffn', 'grad_g_post_ffn', 'grad_w_ffn_gate', 'grad_w_ffn_up', 'grad_w_ffn_down', 'delta_w_ada', 'delta_b_ada', 'delta_g_pre_mix', 'delta_g_post_mix', 'delta_w_in', 'delta_b_if', 'delta_conv_w', 'delta_conv_b', 'delta_attn_sinks', 'delta_mlstm_norm_w', 'delta_w_branch_attn', 'delta_w_branch_mlstm', 'delta_w_out', 'delta_g_pre_ffn', 'delta_g_post_ffn', 'delta_w_ffn_gate', 'delta_w_ffn_up', 'delta_w_ffn_down', 'new_m_w_ada', 'new_m_b_ada', 'new_m_g_pre_mix', 'new_m_g_post_mix', 'new_m_w_in', 'new_m_b_if', 'new_m_conv_w', 'new_m_conv_b', 'new_m_attn_sinks', 'new_m_mlstm_norm_w', 'new_m_w_branch_attn', 'new_m_w_branch_mlstm', 'new_m_w_out', 'new_m_g_pre_ffn', 'new_m_g_post_ffn', 'new_m_w_ffn_gate', 'new_m_w_ffn_up', 'new_m_w_ffn_down', 'new_v_w_ada', 'new_v_b_ada', 'new_v_g_pre_mix', 'new_v_g_post_mix', 'new_v_w_in', 'new_v_b_if', 'new_v_conv_w', 'new_v_conv_b', 'new_v_attn_sinks', 'new_v_mlstm_norm_w', 'new_v_w_branch_attn', 'new_v_w_branch_mlstm', 'new_v_w_out', 'new_v_g_pre_ffn', 'new_v_g_post_ffn', 'new_v_w_ffn_gate', 'new_v_w_ffn_up', 'new_v_w_ffn_down']
TWIN_LEAF_KINDS = {'loss': 'loss', 'grad_x': 'grad_x', 'grad_w_ada': 'grad_w', 'grad_b_ada': 'grad_w', 'grad_g_pre_mix': 'grad_w', 'grad_g_post_mix': 'grad_w', 'grad_w_in': 'grad_w', 'grad_b_if': 'grad_w', 'grad_conv_w': 'grad_w', 'grad_conv_b': 'grad_w', 'grad_attn_sinks': 'grad_w', 'grad_mlstm_norm_w': 'grad_w', 'grad_w_branch_attn': 'grad_w', 'grad_w_branch_mlstm': 'grad_w', 'grad_w_out': 'grad_w', 'grad_g_pre_ffn': 'grad_w', 'grad_g_post_ffn': 'grad_w', 'grad_w_ffn_gate': 'grad_w', 'grad_w_ffn_up': 'grad_w', 'grad_w_ffn_down': 'grad_w', 'delta_w_ada': 'delta_w', 'delta_b_ada': 'delta_w', 'delta_g_pre_mix': 'delta_w', 'delta_g_post_mix': 'delta_w', 'delta_w_in': 'delta_w', 'delta_b_if': 'delta_w', 'delta_conv_w': 'delta_w', 'delta_conv_b': 'delta_w', 'delta_attn_sinks': 'delta_w', 'delta_mlstm_norm_w': 'delta_w', 'delta_w_branch_attn': 'delta_w', 'delta_w_branch_mlstm': 'delta_w', 'delta_w_out': 'delta_w', 'delta_g_pre_ffn': 'delta_w', 'delta_g_post_ffn': 'delta_w', 'delta_w_ffn_gate': 'delta_w', 'delta_w_ffn_up': 'delta_w', 'delta_w_ffn_down': 'delta_w', 'new_m_w_ada': 'new_m', 'new_m_b_ada': 'new_m', 'new_m_g_pre_mix': 'new_m', 'new_m_g_post_mix': 'new_m', 'new_m_w_in': 'new_m', 'new_m_b_if': 'new_m', 'new_m_conv_w': 'new_m', 'new_m_conv_b': 'new_m', 'new_m_attn_sinks': 'new_m', 'new_m_mlstm_norm_w': 'new_m', 'new_m_w_branch_attn': 'new_m', 'new_m_w_branch_mlstm': 'new_m', 'new_m_w_out': 'new_m', 'new_m_g_pre_ffn': 'new_m', 'new_m_g_post_ffn': 'new_m', 'new_m_w_ffn_gate': 'new_m', 'new_m_w_ffn_up': 'new_m', 'new_m_w_ffn_down': 'new_m', 'new_v_w_ada': 'new_v', 'new_v_b_ada': 'new_v', 'new_v_g_pre_mix': 'new_v', 'new_v_g_post_mix': 'new_v', 'new_v_w_in': 'new_v', 'new_v_b_if': 'new_v', 'new_v_conv_w': 'new_v', 'new_v_conv_b': 'new_v', 'new_v_attn_sinks': 'new_v', 'new_v_mlstm_norm_w': 'new_v', 'new_v_w_branch_attn': 'new_v', 'new_v_w_branch_mlstm': 'new_v', 'new_v_w_out': 'new_v', 'new_v_g_pre_ffn': 'new_v', 'new_v_g_post_ffn': 'new_v', 'new_v_w_ffn_gate': 'new_v', 'new_v_w_ffn_up': 'new_v', 'new_v_w_ffn_down': 'new_v'}


def _forward(args):
    return _fwd_reference(*[args[k] for k in FWD_PARAMS])


def _output_shape():
    def fwd():
        inp = _fwd_setup_inputs(0)
        return _fwd_reference(*[inp[k] for k in FWD_PARAMS])
    out = _jax.eval_shape(fwd)
    return out.shape, out.dtype

N_MICROBATCH = 1
ADAM_LR = 0.001
ADAM_B1 = 0.9
ADAM_B2 = 0.999
ADAM_EPS = 1e-08
ADAM_WD = 0.01
ADAM_STEP = 10
PER_EXAMPLE_BATCH_AXIS = {'x': 0, 'c': 0, 'positions': 0, 'loss_target': 0}
SHARED_INPUTS = []
_WEIGHT_DTYPES = {'w_ada': _jnp.float32, 'b_ada': _jnp.float32, 'g_pre_mix': _jnp.float32, 'g_post_mix': _jnp.float32, 'w_in': _jnp.float32, 'b_if': _jnp.float32, 'conv_w': _jnp.float32, 'conv_b': _jnp.float32, 'attn_sinks': _jnp.float32, 'mlstm_norm_w': _jnp.float32, 'w_branch_attn': _jnp.float32, 'w_branch_mlstm': _jnp.float32, 'w_out': _jnp.float32, 'g_pre_ffn': _jnp.float32, 'g_post_ffn': _jnp.float32, 'w_ffn_gate': _jnp.float32, 'w_ffn_up': _jnp.float32, 'w_ffn_down': _jnp.float32}
MOMENT_SCALE = {'w_ada': 9.373172e+00, 'b_ada': 9.200925e+00, 'g_pre_mix': 5.956527e-01, 'g_post_mix': 2.043120e+01, 'w_in': 2.487690e+00, 'b_if': 4.498281e+00, 'conv_w': 1.221603e-01, 'conv_b': 1.265877e-01, 'attn_sinks': 1.364482e-01, 'mlstm_norm_w': 4.684173e+00, 'w_branch_attn': 3.921988e+00, 'w_branch_mlstm': 3.439380e+00, 'w_out': 5.117246e+00, 'g_pre_ffn': 1.018654e+00, 'g_post_ffn': 1.784223e+01, 'w_ffn_gate': 6.113751e-01, 'w_ffn_up': 8.262339e-01, 'w_ffn_down': 1.401896e+00}


def _to_microbatches(a, axis):
    t = _jnp.moveaxis(a, axis, 0)
    t = t.reshape((N_MICROBATCH, t.shape[0] // N_MICROBATCH) + t.shape[1:])
    return _jnp.moveaxis(t, 1, axis + 1)


def setup_inputs(seed: int = 0) -> dict:
    inp = _fwd_setup_inputs(seed)
    key = _jax.random.fold_in(_jax.random.key(seed), 7919)
    shape, _ = _output_shape()
    out = dict(inp)
    out["loss_target"] = _jax.random.normal(_jax.random.fold_in(key, 0), shape, _jnp.float32)
    for i, name in enumerate(TWIN_WEIGHTS):
        w = inp[name].astype(_jnp.float32)
        if MOMENT_SCALE is None:
            s = _jnp.sqrt(_jnp.mean(_jnp.square(w)) + 1e-30)
        else:
            s = MOMENT_SCALE[name]
        km, kv = _jax.random.split(_jax.random.fold_in(key, i + 1))
        out[name] = w
        out["m_" + name] = s * _jax.random.normal(km, w.shape, _jnp.float32)
        out["v_" + name] = (s * s) * _jax.random.uniform(kv, w.shape, _jnp.float32, 0.5, 1.5)
    if N_MICROBATCH > 1:
        for name, axis in PER_EXAMPLE_BATCH_AXIS.items():
            out[name] = _to_microbatches(out[name], axis)
    return {'x': out['x'], 'c': out['c'], 'positions': out['positions'], 'w_ada': out['w_ada'], 'b_ada': out['b_ada'], 'g_pre_mix': out['g_pre_mix'], 'g_post_mix': out['g_post_mix'], 'w_in': out['w_in'], 'b_if': out['b_if'], 'conv_w': out['conv_w'], 'conv_b': out['conv_b'], 'attn_sinks': out['attn_sinks'], 'mlstm_norm_w': out['mlstm_norm_w'], 'w_branch_attn': out['w_branch_attn'], 'w_branch_mlstm': out['w_branch_mlstm'], 'w_out': out['w_out'], 'g_pre_ffn': out['g_pre_ffn'], 'g_post_ffn': out['g_post_ffn'], 'w_ffn_gate': out['w_ffn_gate'], 'w_ffn_up': out['w_ffn_up'], 'w_ffn_down': out['w_ffn_down'], 'loss_target': out['loss_target'], 'm_w_ada': out['m_w_ada'], 'm_b_ada': out['m_b_ada'], 'm_g_pre_mix': out['m_g_pre_mix'], 'm_g_post_mix': out['m_g_post_mix'], 'm_w_in': out['m_w_in'], 'm_b_if': out['m_b_if'], 'm_conv_w': out['m_conv_w'], 'm_conv_b': out['m_conv_b'], 'm_attn_sinks': out['m_attn_sinks'], 'm_mlstm_norm_w': out['m_mlstm_norm_w'], 'm_w_branch_attn': out['m_w_branch_attn'], 'm_w_branch_mlstm': out['m_w_branch_mlstm'], 'm_w_out': out['m_w_out'], 'm_g_pre_ffn': out['m_g_pre_ffn'], 'm_g_post_ffn': out['m_g_post_ffn'], 'm_w_ffn_gate': out['m_w_ffn_gate'], 'm_w_ffn_up': out['m_w_ffn_up'], 'm_w_ffn_down': out['m_w_ffn_down'], 'v_w_ada': out['v_w_ada'], 'v_b_ada': out['v_b_ada'], 'v_g_pre_mix': out['v_g_pre_mix'], 'v_g_post_mix': out['v_g_post_mix'], 'v_w_in': out['v_w_in'], 'v_b_if': out['v_b_if'], 'v_conv_w': out['v_conv_w'], 'v_conv_b': out['v_conv_b'], 'v_attn_sinks': out['v_attn_sinks'], 'v_mlstm_norm_w': out['v_mlstm_norm_w'], 'v_w_branch_attn': out['v_w_branch_attn'], 'v_w_branch_mlstm': out['v_w_branch_mlstm'], 'v_w_out': out['v_w_out'], 'v_g_pre_ffn': out['v_g_pre_ffn'], 'v_g_post_ffn': out['v_g_post_ffn'], 'v_w_ffn_gate': out['v_w_ffn_gate'], 'v_w_ffn_up': out['v_w_ffn_up'], 'v_w_ffn_down': out['v_w_ffn_down']}


def _loss(weights, diff, rest, loss_target):
    with _jax.named_scope("forward"):
        args = {**rest, TWIN_DIFF_INPUT: diff, **{k: w.astype(_WEIGHT_DTYPES[k]) for k, w in weights.items()}}
        y = _forward(args)
    with _jax.named_scope("loss_head"):
        err = _jnp.square(y.astype(_jnp.float32) - loss_target)
        return 0.5 * _jnp.sum(_jnp.mean(err, axis=-1)) if err.ndim else 0.5 * err


def _adamw(w, g, m, v):
    m = ADAM_B1 * m + (1.0 - ADAM_B1) * g
    v = ADAM_B2 * v + (1.0 - ADAM_B2) * _jnp.square(g)
    m_hat = m / (1.0 - ADAM_B1 ** ADAM_STEP)
    v_hat = v / (1.0 - ADAM_B2 ** ADAM_STEP)
    delta = -ADAM_LR * (m_hat / (_jnp.sqrt(v_hat) + ADAM_EPS) + ADAM_WD * w)
    return delta, m, v


def reference(x, c, positions, w_ada, b_ada, g_pre_mix, g_post_mix, w_in, b_if, conv_w, conv_b, attn_sinks, mlstm_norm_w, w_branch_attn, w_branch_mlstm, w_out, g_pre_ffn, g_post_ffn, w_ffn_gate, w_ffn_up, w_ffn_down, loss_target, m_w_ada, m_b_ada, m_g_pre_mix, m_g_post_mix, m_w_in, m_b_if, m_conv_w, m_conv_b, m_attn_sinks, m_mlstm_norm_w, m_w_branch_attn, m_w_branch_mlstm, m_w_out, m_g_pre_ffn, m_g_post_ffn, m_w_ffn_gate, m_w_ffn_up, m_w_ffn_down, v_w_ada, v_b_ada, v_g_pre_mix, v_g_post_mix, v_w_in, v_b_if, v_conv_w, v_conv_b, v_attn_sinks, v_mlstm_norm_w, v_w_branch_attn, v_w_branch_mlstm, v_w_out, v_g_pre_ffn, v_g_post_ffn, v_w_ffn_gate, v_w_ffn_up, v_w_ffn_down):
    given = dict(x=x, c=c, positions=positions, w_ada=w_ada, b_ada=b_ada, g_pre_mix=g_pre_mix, g_post_mix=g_post_mix, w_in=w_in, b_if=b_if, conv_w=conv_w, conv_b=conv_b, attn_sinks=attn_sinks, mlstm_norm_w=mlstm_norm_w, w_branch_attn=w_branch_attn, w_branch_mlstm=w_branch_mlstm, w_out=w_out, g_pre_ffn=g_pre_ffn, g_post_ffn=g_post_ffn, w_ffn_gate=w_ffn_gate, w_ffn_up=w_ffn_up, w_ffn_down=w_ffn_down, loss_target=loss_target, m_w_ada=m_w_ada, m_b_ada=m_b_ada, m_g_pre_mix=m_g_pre_mix, m_g_post_mix=m_g_post_mix, m_w_in=m_w_in, m_b_if=m_b_if, m_conv_w=m_conv_w, m_conv_b=m_conv_b, m_attn_sinks=m_attn_sinks, m_mlstm_norm_w=m_mlstm_norm_w, m_w_branch_attn=m_w_branch_attn, m_w_branch_mlstm=m_w_branch_mlstm, m_w_out=m_w_out, m_g_pre_ffn=m_g_pre_ffn, m_g_post_ffn=m_g_post_ffn, m_w_ffn_gate=m_w_ffn_gate, m_w_ffn_up=m_w_ffn_up, m_w_ffn_down=m_w_ffn_down, v_w_ada=v_w_ada, v_b_ada=v_b_ada, v_g_pre_mix=v_g_pre_mix, v_g_post_mix=v_g_post_mix, v_w_in=v_w_in, v_b_if=v_b_if, v_conv_w=v_conv_w, v_conv_b=v_conv_b, v_attn_sinks=v_attn_sinks, v_mlstm_norm_w=v_mlstm_norm_w, v_w_branch_attn=v_w_branch_attn, v_w_branch_mlstm=v_w_branch_mlstm, v_w_out=v_w_out, v_g_pre_ffn=v_g_pre_ffn, v_g_post_ffn=v_g_post_ffn, v_w_ffn_gate=v_w_ffn_gate, v_w_ffn_up=v_w_ffn_up, v_w_ffn_down=v_w_ffn_down)
    weights = {n: given[n] for n in TWIN_WEIGHTS}
    shared = {n: given[n] for n in SHARED_INPUTS}
    per_example = {n: given[n] for n in ['x', 'c', 'positions']}
    grad_fn = _jax.value_and_grad(_loss, argnums=(0, 1))

    def one_microbatch(ex, loss_target):
        ex = dict(ex)
        diff = ex.pop(TWIN_DIFF_INPUT)
        return grad_fn(weights, diff, {**shared, **ex}, loss_target)

    if N_MICROBATCH == 1:
        loss, (grad_w, grad_x) = one_microbatch(per_example, given["loss_target"])
    else:
        def body(carry, xs):
            loss_sum, grad_sum = carry
            l_k, (gw_k, gx_k) = one_microbatch(xs[0], xs[1])
            with _jax.named_scope("update"):
                return (loss_sum + l_k, _jax.tree.map(_jnp.add, grad_sum, gw_k)), gx_k

        init = (_jnp.zeros((), _jnp.float32), _jax.tree.map(_jnp.zeros_like, weights))
        (loss, grad_w), grad_x = _jax.lax.scan(body, init, (per_example, given["loss_target"]))
    with _jax.named_scope("update"):
        delta_w, new_m, new_v = {}, {}, {}
        for n in TWIN_WEIGHTS:
            delta_w[n], new_m[n], new_v[n] = _adamw(weights[n], grad_w[n], given["m_" + n], given["v_" + n])
    return (loss, grad_x, *[grad_w[n] for n in TWIN_WEIGHTS], *[delta_w[n] for n in TWIN_WEIGHTS],
            *[new_m[n] for n in TWIN_WEIGHTS], *[new_v[n] for n in TWIN_WEIGHTS])
```

```python
import functools

import jax
import jax.numpy as jnp
from jax import lax
from jax.experimental import pallas as pl
from jax.experimental.pallas import tpu as pltpu

F32, BF16 = jnp.float32, jnp.bfloat16
MESH = pl.DeviceIdType.MESH

D_MODEL = 1024
N_Q_HEADS, N_KV_HEADS, HEAD_DIM, WINDOW = 8, 2, 64, 128
ROPE_THETA = 10000.0
MLSTM_HEADS, MLSTM_HEAD_DIM, MLSTM_CHUNK, CONV_WIDTH = 4, 128, 64, 4
D_FF = 2816
NORM_EPS = 1e-6
ADAM_LR, ADAM_B1, ADAM_B2, ADAM_EPS, ADAM_WD, ADAM_STEP = 0.001, 0.9, 0.999, 1e-08, 0.01, 10

VMEM_LIMIT = 56 * 1024 * 1024
ROW_TILE = 256
MM_TM = 512
ATTN_BLK = WINDOW
STEP_ROWS = 2 * MLSTM_CHUNK
NEG_INF = float("-inf")


def _params(sem):
    return pltpu.CompilerParams(dimension_semantics=sem, vmem_limit_bytes=VMEM_LIMIT)


def _sds(shape, dtype):
    return jax.ShapeDtypeStruct(shape, dtype)


def _sigmoid(x):
    return 1.0 / (1.0 + jnp.exp(-x))


def _dot(a, b, ca, cb):
    return lax.dot_general(a, b, (((ca,), (cb,)), ((), ())), preferred_element_type=F32)


def _mm(name, prods, extras, epi, out_dtypes, tn, nt=False, tm=MM_TM):
    flat = [ab for p in prods for ab in p]
    counts = [len(p) for p in prods]
    M = flat[0][0].shape[0]
    N = flat[0][1].shape[0 if nt else 1]
    tm = min(tm, M)
    n_in = 2 * len(flat) + len(extras)

    def body(*refs):
        ins, outs = refs[:n_in], refs[n_in:]
        k, ps = 0, []
        for cnt in counts:
            acc = None
            for _ in range(cnt):
                d = _dot(ins[k][...], ins[k + 1][...], 1, 1 if nt else 0)
                acc = d if acc is None else acc + d
                k += 2
            ps.append(acc)
        res = epi(ps, [r[...] for r in ins[k:]])
        for o, r in zip(outs, res):
            o[...] = r.astype(o.dtype)

    in_specs, args = [], []
    for a, b in flat:
        K = a.shape[1]
        in_specs.append(pl.BlockSpec((tm, K), lambda i, j: (i, 0)))
        in_specs.append(pl.BlockSpec((tn, K), lambda i, j: (j, 0)) if nt
                        else pl.BlockSpec((K, tn), lambda i, j: (0, j)))
        args += [a, b]
    for e in extras:
        e, off = e if isinstance(e, tuple) else (e, 0)
        if e.shape[0] == 1:
            in_specs.append(pl.BlockSpec((1, tn), lambda i, j, off=off: (0, j + off)))
        else:
            in_specs.append(pl.BlockSpec((tm, tn), lambda i, j, off=off: (i, j + off)))
        args.append(e)
    outs = pl.pallas_call(
        body, name=name, grid=(M // tm, N // tn), in_specs=in_specs,
        out_specs=[pl.BlockSpec((tm, tn), lambda i, j: (i, j)) for _ in out_dtypes],
        out_shape=[_sds((M, N), dt) for dt in out_dtypes],
        compiler_params=_params(("parallel", "parallel")))(*args)
    return outs


def _mm_tn(name, a, b, out_dtype, tk, tn, tt):
    T, Ka = a.shape
    N = b.shape[1]
    tt = min(tt, T)
    steps = T // tt

    def body(a_ref, b_ref, o_ref, acc):
        t = pl.program_id(2)

        @pl.when(t == 0)
        def _():
            acc[...] = jnp.zeros_like(acc)

        acc[...] += _dot(a_ref[...], b_ref[...], 0, 0)

        @pl.when(t == steps - 1)
        def _():
            o_ref[...] = acc[...].astype(o_ref.dtype)

    return pl.pallas_call(
        body, name=name, grid=(Ka // tk, N // tn, steps),
        in_specs=[pl.BlockSpec((tt, tk), lambda i, j, t: (t, i)),
                  pl.BlockSpec((tt, tn), lambda i, j, t: (t, j))],
        out_specs=pl.BlockSpec((tk, tn), lambda i, j, t: (i, j)),
        out_shape=_sds((Ka, N), out_dtype),
        scratch_shapes=[pltpu.VMEM((tk, tn), F32)],
        compiler_params=_params(("parallel", "parallel", "arbitrary")))(a, b)


def _first(ps, es):
    return (ps[0],)


def _rows(name, body, ins, out_shapes, T, tr=ROW_TILE):
    tr = min(tr, T)

    def spec(shape):
        if shape[0] == T:
            return pl.BlockSpec((tr,) + tuple(shape[1:]), lambda i: (i,) + (0,) * (len(shape) - 1))
        return pl.BlockSpec(tuple(shape), lambda i: (0,) * len(shape))

    return pl.pallas_call(
        body, name=name, grid=(T // tr,),
        in_specs=[spec(a.shape) for a in ins], out_specs=[spec(s.shape) for s in out_shapes],
        out_shape=out_shapes, compiler_params=_params(("arbitrary",)))(*ins)


def _rms(x):
    r = lax.rsqrt(jnp.mean(x * x, axis=-1, keepdims=True) + NORM_EPS)
    return x * r, r


def _rms_bwd(dxn, xn, r):
    return r * (dxn - xn * jnp.mean(dxn * xn, axis=-1, keepdims=True))


def _colsum(v):
    return jnp.sum(v, axis=0, keepdims=True)


def _pre_norm(x, g, sc, sh):
    T = x.shape[0]

    def body(x_ref, g_ref, sc_ref, sh_ref, h_ref):
        xn, _ = _rms(x_ref[...])
        h_ref[...] = (xn * g_ref[...] * (1.0 + sc_ref[...]) + sh_ref[...]).astype(BF16)

    return _rows("pre_norm", body, [x, g, sc, sh], [_sds((T, D_MODEL), BF16)], T)[0]


def _res_norm(x, mix, gate, gpost, g2, sc2, sh2):
    T = x.shape[0]

    def body(x_ref, mix_ref, gate_ref, gp_ref, g2_ref, sc_ref, sh_ref, x1_ref, h2_ref):
        mh, _ = _rms(mix_ref[...])
        x1 = x_ref[...] + gate_ref[...] * (mh * gp_ref[...])
        x1_ref[...] = x1
        xn, _ = _rms(x1)
        h2_ref[...] = (xn * g2_ref[...] * (1.0 + sc_ref[...]) + sh_ref[...]).astype(BF16)

    return _rows("res_norm", body, [x, mix, gate, gpost, g2, sc2, sh2],
                 [_sds((T, D_MODEL), F32), _sds((T, D_MODEL), BF16)], T)


def _final_loss(x1, ff, tgt, gate, gpost):
    T = x1.shape[0]

    def body(x1_ref, ff_ref, t_ref, gate_ref, gp_ref, dy_ref, dff_ref, acc_ref, loss_ref):
        @pl.when(pl.program_id(0) == 0)
        def _():
            acc_ref[...] = jnp.zeros_like(acc_ref)
            loss_ref[...] = jnp.zeros_like(loss_ref)

        fh, r = _rms(ff_ref[...])
        gate, gp = gate_ref[...], gp_ref[...]
        e = x1_ref[...] + gate * (fh * gp) - t_ref[...]
        loss_ref[...] += 0.5 * jnp.sum(jnp.mean(e * e, axis=-1, keepdims=True))
        dy = e * (1.0 / D_MODEL)
        dy_ref[...] = dy
        acc_ref[0:1, :] += _colsum(dy * fh * gp)
        acc_ref[1:2, :] += _colsum(dy * gate * fh)
        dff_ref[...] = _rms_bwd(dy * gate * gp, fh, r).astype(BF16)

    return _rows("final_loss", body, [x1, ff, tgt, gate, gpost],
                 [_sds((T, D_MODEL), F32), _sds((T, D_MODEL), BF16),
                  _sds((8, D_MODEL), F32), _sds((1, 128), F32)], T)


def _res_norm_bwd(x1, x, mix, dh2, dy, sc2, gate, g2, gpost):
    T = x.shape[0]

    def body(x1_ref, x_ref, mix_ref, dh_ref, dy_ref, sc_ref, gate_ref, g2_ref, gp_ref,
             dx1_ref, dmix_ref, acc_ref):
        @pl.when(pl.program_id(0) == 0)
        def _():
            acc_ref[...] = jnp.zeros_like(acc_ref)

        xn, r1 = _rms(x1_ref[...])
        dh, sc, g2 = dh_ref[...], sc_ref[...], g2_ref[...]
        acc_ref[0:1, :] += _colsum(dh * xn * g2)
        acc_ref[1:2, :] += _colsum(dh)
        acc_ref[2:3, :] += _colsum(dh * (1.0 + sc) * xn)
        dx1 = dy_ref[...] + _rms_bwd(dh * (1.0 + sc) * g2, xn, r1)
        dx1_ref[...] = dx1
        mh, rm = _rms(mix_ref[...])
        gate, gp = gate_ref[...], gp_ref[...]
        acc_ref[3:4, :] += _colsum(dx1 * mh * gp)
        acc_ref[4:5, :] += _colsum(dx1 * gate * mh)
        dmix_ref[...] = _rms_bwd(dx1 * gate * gp, mh, rm).astype(BF16)

    return _rows("res_norm_bwd", body, [x1, x, mix, dh2, dy, sc2, gate, g2, gpost],
                 [_sds((T, D_MODEL), F32), _sds((T, D_MODEL), BF16), _sds((8, D_MODEL), F32)], T)


def _pre_norm_bwd(x, dh, dx1, g, sc):
    T = x.shape[0]

    def body(x_ref, dh_ref, dx1_ref, g_ref, sc_ref, dx_ref, acc_ref):
        @pl.when(pl.program_id(0) == 0)
        def _():
            acc_ref[...] = jnp.zeros_like(acc_ref)

        xn, r = _rms(x_ref[...])
        dh, sc, g = dh_ref[...], sc_ref[...], g_ref[...]
        acc_ref[0:1, :] += _colsum(dh * xn * g)
        acc_ref[1:2, :] += _colsum(dh)
        acc_ref[2:3, :] += _colsum(dh * (1.0 + sc) * xn)
        dx_ref[...] = dx1_ref[...] + _rms_bwd(dh * (1.0 + sc) * g, xn, r)

    return _rows("pre_norm_bwd", body, [x, dh, dx1, g, sc],
                 [_sds((T, D_MODEL), F32), _sds((8, D_MODEL), F32)], T)


def _rope_tables(pos_col, inv_freq):
    T = pos_col.shape[0]

    def body(p_ref, f_ref, c_ref, s_ref):
        ang = p_ref[...].astype(F32) * f_ref[...]
        lane = lax.broadcasted_iota(jnp.int32, ang.shape, 1)
        c_ref[...] = jnp.cos(ang)
        s_ref[...] = jnp.where(lane % HEAD_DIM < HEAD_DIM // 2, -1.0, 1.0) * jnp.sin(ang)

    return _rows("rope_tables", body, [pos_col, inv_freq],
                 [_sds((T, 128), F32), _sds((T, 128), F32)], T, tr=512)


def _swap_halves(t):
    W = t.shape[1]
    lane = lax.broadcasted_iota(jnp.int32, t.shape, 1)
    half = HEAD_DIM // 2
    return jnp.where(lane % HEAD_DIM < half, pltpu.roll(t, W - half, 1), pltpu.roll(t, half, 1))


def _widen(c, W):
    return c if W == 128 else jnp.concatenate([c] * (W // 128), axis=1)


def _rope(t, c, s):
    W = t.shape[1]
    return t * _widen(c, W) + _swap_halves(t) * _widen(s, W)


def _unrope(dy, c, s):
    W = dy.shape[1]
    return dy * _widen(c, W) + _swap_halves(dy * _widen(s, W))


def _attn_mask(n):
    qi = lax.broadcasted_iota(jnp.int32, (ATTN_BLK, 2 * ATTN_BLK), 0)
    kj = lax.broadcasted_iota(jnp.int32, (ATTN_BLK, 2 * ATTN_BLK), 1)
    rel = kj - ATTN_BLK
    return (rel <= qi) & (qi - rel < WINDOW) & ((n > 0) | (kj >= ATTN_BLK))


def _attn_load(cur, prv, cc, sc, cp, sp):
    x, xp = cur[...], prv[...]
    q = _rope(x[:, :512], cc[...], sc[...]) * (HEAD_DIM ** -0.5)
    k = jnp.concatenate([_rope(xp[:, 512:640], cp[...], sp[...]),
                         _rope(x[:, 512:640], cc[...], sc[...])], axis=0)
    v = jnp.concatenate([xp[:, 640:768], x[:, 640:768]], axis=0)
    return q, k, v


def _head_operands(q, k, v, head):
    pair, a, kv = head // 2, head % 2, head // (N_Q_HEADS // N_KV_HEADS)
    lane_q = lax.broadcasted_iota(jnp.int32, (ATTN_BLK, 128), 1)
    lane_k = lax.broadcasted_iota(jnp.int32, (2 * ATTN_BLK, 128), 1)
    sel_q = (lane_q // HEAD_DIM) == a
    sel_k = (lane_k // HEAD_DIM) == a
    qh = jnp.where(sel_q, q[:, 128 * pair:128 * pair + 128], 0.0)
    ku = k if a == kv else pltpu.roll(k, HEAD_DIM, 1)
    vu = v if a == kv else pltpu.roll(v, HEAD_DIM, 1)
    return qh, jnp.where(sel_k, ku, 0.0), jnp.where(sel_k, vu, 0.0), sel_q, a != kv


def _attn_probs(qh, kh, mask, sink):
    s = _dot(qh.astype(BF16), kh.astype(BF16), 1, 1)
    s = jnp.where(mask, s, NEG_INF)
    m = jnp.maximum(jnp.max(s, axis=-1, keepdims=True), sink)
    p = jnp.exp(s - m)
    es = jnp.exp(sink - m)
    rl = 1.0 / (jnp.sum(p, axis=-1, keepdims=True) + es)
    return p, es, rl


def _attn_specs(nb):
    blk = lambda w: pl.BlockSpec((ATTN_BLK, w), lambda n: (n, 0))
    prv = lambda w: pl.BlockSpec((ATTN_BLK, w), lambda n: (jnp.maximum(n - 1, 0), 0))
    return [blk(768), prv(768), blk(128), blk(128), prv(128), prv(128),
            pl.BlockSpec(memory_space=pltpu.SMEM)]


def _attn_fwd(pa, cos, sin, sinks):
    T = pa.shape[0]
    nb = T // ATTN_BLK

    def body(cur, prv, cc, sc, cp, sp, snk, y_ref):
        n = pl.program_id(0)
        q, k, v = _attn_load(cur, prv, cc, sc, cp, sp)
        mask = _attn_mask(n)
        for pair in range(N_Q_HEADS // 2):
            o = None
            for a in range(2):
                head = 2 * pair + a
                qh, kh, vh, _, _ = _head_operands(q, k, v, head)
                p, _, rl = _attn_probs(qh, kh, mask, snk[0, head])
                oh = _dot(p.astype(BF16), vh.astype(BF16), 1, 0) * rl
                o = oh if o is None else o + oh
            y_ref[:, 128 * pair:128 * pair + 128] = o.astype(BF16)

    return pl.pallas_call(
        body, name="attn_fwd", grid=(nb,), in_specs=_attn_specs(nb),
        out_specs=pl.BlockSpec((ATTN_BLK, 512), lambda n: (n, 0)),
        out_shape=_sds((T, 512), BF16), compiler_params=_params(("parallel",)))(
            pa, pa, cos, sin, cos, sin, sinks)


def _attn_bwd(pa, cos, sin, sinks, dy):
    T = pa.shape[0]
    nb = T // ATTN_BLK

    def body(cur, prv, cc, sc, cp, sp, snk, dy_ref, dq_ref, dcur_ref, dprv_ref, dsink_ref):
        n = pl.program_id(0)

        @pl.when(n == 0)
        def _():
            dsink_ref[...] = jnp.zeros_like(dsink_ref)

        q, k, v = _attn_load(cur, prv, cc, sc, cp, sp)
        mask = _attn_mask(n)
        dk = jnp.zeros((2 * ATTN_BLK, 128), F32)
        dv = jnp.zeros((2 * ATTN_BLK, 128), F32)
        for pair in range(N_Q_HEADS // 2):
            dqp = None
            for a in range(2):
                head = 2 * pair + a
                qh, kh, vh, sel_q, rolled = _head_operands(q, k, v, head)
                p, es, rl = _attn_probs(qh, kh, mask, snk[0, head])
                pn = p * rl
                do = jnp.where(sel_q, dy_ref[:, 128 * pair:128 * pair + 128], 0.0).astype(BF16)
                dp = _dot(do, vh.astype(BF16), 1, 1)
                delta = jnp.sum(pn * dp, axis=-1, keepdims=True)
                ds = (pn * (dp - delta)).astype(BF16)
                dsink_ref[head:head + 1, :] += -jnp.sum(es * rl * delta)
                dqh = _dot(ds, kh.astype(BF16), 1, 0)
                dqp = dqh if dqp is None else dqp + dqh
                dkh = _dot(ds, qh.astype(BF16), 0, 0)
                dvh = _dot(pn.astype(BF16), do, 0, 0)
                if rolled:
                    dkh, dvh = pltpu.roll(dkh, HEAD_DIM, 1), pltpu.roll(dvh, HEAD_DIM, 1)
                dk, dv = dk + dkh, dv + dvh
            dq_ref[:, 128 * pair:128 * pair + 128] = _unrope(
                dqp * (HEAD_DIM ** -0.5), cc[...], sc[...]).astype(BF16)
        dcur_ref[:, 0:128] = dk[ATTN_BLK:]
        dcur_ref[:, 128:256] = dv[ATTN_BLK:]
        dprv_ref[:, 0:128] = dk[:ATTN_BLK]
        dprv_ref[:, 128:256] = dv[:ATTN_BLK]

    blk = lambda w: pl.BlockSpec((ATTN_BLK, w), lambda n: (n, 0))
    return pl.pallas_call(
        body, name="attn_bwd", grid=(nb,), in_specs=_attn_specs(nb) + [blk(512)],
        out_specs=[blk(512), blk(256), blk(256), pl.BlockSpec((8, 128), lambda n: (0, 0))],
        out_shape=[_sds((T, 512), BF16), _sds((T, 256), F32), _sds((T, 256), F32),
                   _sds((8, 128), F32)],
        compiler_params=_params(("arbitrary",)))(pa, pa, cos, sin, cos, sin, sinks, dy)


def _attn_kv_combine(dcur, dprv, cos, sin):
    T = dcur.shape[0]
    nb = T // ATTN_BLK

    def body(c_ref, p_ref, cc, sc, o_ref):
        n = pl.program_id(0)
        t = c_ref[...] + jnp.where(n < nb - 1, p_ref[...], 0.0)
        o_ref[:, 0:128] = _unrope(t[:, 0:128], cc[...], sc[...]).astype(BF16)
        o_ref[:, 128:256] = t[:, 128:256].astype(BF16)

    blk = lambda w: pl.BlockSpec((ATTN_BLK, w), lambda n: (n, 0))
    nxt = pl.BlockSpec((ATTN_BLK, 256), lambda n: (jnp.minimum(n + 1, nb - 1), 0))
    return pl.pallas_call(
        body, name="attn_kv_combine", grid=(nb,), in_specs=[blk(256), nxt, blk(128), blk(128)],
        out_specs=blk(256), out_shape=_sds((T, 256), BF16),
        compiler_params=_params(("parallel",)))(dcur, dprv, cos, sin)


CONV_COLS = 2 * MLSTM_HEADS * MLSTM_HEAD_DIM


def _conv_pre(cur_ref, halo_ref, w_ref, b_ref, i, tr):
    xx = jnp.concatenate([jnp.where(i > 0, halo_ref[...], 0.0), cur_ref[...]], axis=0)
    taps = [(pltpu.roll(xx, CONV_WIDTH - 1 - j, 0) if j < CONV_WIDTH - 1 else xx)[8:8 + tr]
            for j in range(CONV_WIDTH)]
    pre = b_ref[...]
    for j in range(CONV_WIDTH):
        pre = pre + taps[j] * w_ref[j:j + 1, :]
    return pre, taps


def _conv_specs(T, tr):
    return [pl.BlockSpec((tr, CONV_COLS), lambda i: (i, 0)),
            pl.BlockSpec((8, CONV_COLS), lambda i: (jnp.maximum(i * (tr // 8) - 1, 0), 0)),
            pl.BlockSpec((CONV_WIDTH, CONV_COLS), lambda i: (0, 0)),
            pl.BlockSpec((1, CONV_COLS), lambda i: (0, 0))]


def _conv_fwd(pm, w, b):
    T = pm.shape[0]
    tr = min(ROW_TILE, T)

    def body(cur_ref, halo_ref, w_ref, b_ref, o_ref):
        pre, _ = _conv_pre(cur_ref, halo_ref, w_ref, b_ref, pl.program_id(0), tr)
        o_ref[...] = pre * _sigmoid(pre)

    return pl.pallas_call(
        body, name="conv_fwd", grid=(T // tr,), in_specs=_conv_specs(T, tr),
        out_specs=pl.BlockSpec((tr, CONV_COLS), lambda i: (i, 0)),
        out_shape=_sds((T, CONV_COLS), F32), compiler_params=_params(("parallel",)))(pm, pm, w, b)


def _conv_bwd_pre(pm, w, b, dqk):
    T = pm.shape[0]
    tr = min(ROW_TILE, T)

    def body(cur_ref, halo_ref, w_ref, b_ref, d_ref, dpre_ref, acc_ref):
        i = pl.program_id(0)

        @pl.when(i == 0)
        def _():
            acc_ref[...] = jnp.zeros_like(acc_ref)

        pre, taps = _conv_pre(cur_ref, halo_ref, w_ref, b_ref, i, tr)
        sg = _sigmoid(pre)
        dpre = d_ref[...] * (sg * (1.0 + pre * (1.0 - sg)))
        dpre_ref[...] = dpre
        for j in range(CONV_WIDTH):
            acc_ref[j:j + 1, :] += _colsum(dpre * taps[j])
        acc_ref[CONV_WIDTH:CONV_WIDTH + 1, :] += _colsum(dpre)

    return pl.pallas_call(
        body, name="conv_bwd_pre", grid=(T // tr,),
        in_specs=_conv_specs(T, tr) + [pl.BlockSpec((tr, CONV_COLS), lambda i: (i, 0))],
        out_specs=[pl.BlockSpec((tr, CONV_COLS), lambda i: (i, 0)),
                   pl.BlockSpec((8, CONV_COLS), lambda i: (0, 0))],
        out_shape=[_sds((T, CONV_COLS), F32), _sds((8, CONV_COLS), F32)],
        compiler_params=_params(("arbitrary",)))(pm, pm, w, b, dqk)


def _conv_bwd_in(dpre, w):
    T = dpre.shape[0]
    tr = min(ROW_TILE, T)
    nt = T // tr

    def body(cur_ref, halo_ref, w_ref, o_ref):
        i = pl.program_id(0)
        yy = jnp.concatenate([cur_ref[...], jnp.where(i < nt - 1, halo_ref[...], 0.0)], axis=0)
        du = cur_ref[...] * w_ref[CONV_WIDTH - 1:CONV_WIDTH, :]
        for j in range(CONV_WIDTH - 1):
            k = CONV_WIDTH - 1 - j
            du = du + pltpu.roll(yy, tr + 8 - k, 0)[:tr] * w_ref[j:j + 1, :]
        o_ref[...] = du.astype(BF16)

    return pl.pallas_call(
        body, name="conv_bwd_in", grid=(nt,),
        in_specs=[pl.BlockSpec((tr, CONV_COLS), lambda i: (i, 0)),
                  pl.BlockSpec((8, CONV_COLS),
                               lambda i: (jnp.minimum((i + 1) * (tr // 8), T // 8 - 1), 0)),
                  pl.BlockSpec((CONV_WIDTH, CONV_COLS), lambda i: (0, 0))],
        out_specs=pl.BlockSpec((tr, CONV_COLS), lambda i: (i, 0)),
        out_shape=_sds((T, CONV_COLS), BF16), compiler_params=_params(("parallel",)))(dpre, dpre, w)


def _log_sigmoid(x):
    return jnp.minimum(x, 0.0) - jnp.log1p(jnp.exp(-jnp.abs(x)))


def _chunk_cumsum(x, axis):
    idx = lax.broadcasted_iota(jnp.int32, x.shape, axis) % MLSTM_CHUNK
    k = 1
    while k < MLSTM_CHUNK:
        x = x + jnp.where(idx >= k, pltpu.roll(x, k, axis), 0.0)
        k *= 2
    return x


def _chunk_rev_cumsum(x, axis):
    n = x.shape[axis]
    idx = lax.broadcasted_iota(jnp.int32, x.shape, axis) % MLSTM_CHUNK
    k = 1
    while k < MLSTM_CHUNK:
        x = x + jnp.where(idx < MLSTM_CHUNK - k, pltpu.roll(x, n - k, axis), 0.0)
        k *= 2
    return x


def _mlstm_gates(gc_ref, bc_ref, gr_ref, br_ref):
    gc = gc_ref[...] + bc_ref[...]
    gr = gr_ref[...] + br_ref[...]
    return gc, _chunk_cumsum(_log_sigmoid(gc), 0), gr, _chunk_cumsum(_log_sigmoid(gr), 1)


def _mlstm_head(hd, q_ref, k_ref, v_ref, gc, bc, gr, br, c_prev, n_prev, m_prev):
    L = MLSTM_CHUNK
    D = MLSTM_HEAD_DIM
    q = q_ref[:, D * hd:D * hd + D]
    ks = k_ref[:, D * hd:D * hd + D] * (D ** -0.5)
    v = v_ref[:, D * hd:D * hd + D]
    qb, kb, vb = q.astype(BF16), ks.astype(BF16), v.astype(BF16)
    b_col = bc[:, MLSTM_HEADS + hd:MLSTM_HEADS + hd + 1]
    i_col = gc[:, hd:hd + 1]
    b_row = br[MLSTM_HEADS + hd:MLSTM_HEADS + hd + 1, :]
    i_row = gr[hd:hd + 1, :]
    t = lax.broadcasted_iota(jnp.int32, (2 * L, 2 * L), 0)
    s = lax.broadcasted_iota(jnp.int32, (2 * L, 2 * L), 1)
    mask = (t // L == s // L) & (s <= t)
    d = jnp.where(mask, b_col - b_row + i_row, NEG_INF)
    row = lax.broadcasted_iota(jnp.int32, (2 * L, 1), 0)
    inter = b_col + jnp.where(row < L, m_prev[0], m_prev[1])
    m_t = jnp.maximum(inter, jnp.max(d, axis=-1, keepdims=True))
    w_intra = jnp.exp(d - m_t)
    w_inter = jnp.exp(inter - m_t)
    sc = _dot(qb, kb, 1, 1) * w_intra
    qc = jnp.concatenate([_dot(qb[:L], c_prev[0].astype(BF16), 1, 0),
                          _dot(qb[L:], c_prev[1].astype(BF16), 1, 0)], axis=0)
    qn = jnp.concatenate([jnp.sum(q[:L] * n_prev[0], axis=-1, keepdims=True),
                          jnp.sum(q[L:] * n_prev[1], axis=-1, keepdims=True)], axis=0)
    num = _dot(sc.astype(BF16), vb, 1, 0) + w_inter * qc
    den = jnp.sum(sc, axis=-1, keepdims=True) + w_inter * qn
    floor = jnp.exp(-m_t)
    return dict(q=q, ks=ks, v=v, qb=qb, kb=kb, vb=vb, b_col=b_col, i_col=i_col, w_intra=w_intra,
                w_inter=w_inter, sc=sc, qc=qc, qn=qn, num=num, den=den, floor=floor)


def _mlstm_update(f, ch, c, n, m):
    L = MLSTM_CHUNK
    rows = slice(L * ch, L * ch + L)
    b_col = f["b_col"][rows]
    g_last = b_col[L - 1:L]
    a_col = g_last - b_col + f["i_col"][rows]
    m_new = jnp.maximum(g_last + m, jnp.max(a_col, axis=0, keepdims=True))
    decay = jnp.exp(g_last + m - m_new)
    e_a = jnp.exp(a_col - m_new)
    kw = f["ks"][rows] * e_a
    c_new = decay * c + _dot(kw.astype(BF16), f["vb"][rows], 0, 0)
    n_new = decay * n + _colsum(kw)
    return c_new, n_new, m_new, decay, e_a, kw


def _mlstm_specs(T, order):
    blk = lambda w, col: pl.BlockSpec((STEP_ROWS, w), lambda s: (order(s), col))
    return [blk(512, 0), blk(512, 1), blk(512, 2), blk(128, 0),
            pl.BlockSpec((1, 128), lambda s: (0, 0)),
            pl.BlockSpec((8, STEP_ROWS), lambda s: (0, order(s))),
            pl.BlockSpec((8, 128), lambda s: (0, 0))]


def _mlstm_fwd(qk, pm, gcol, bcol, grow, brow):
    T = qk.shape[0]
    steps = T // STEP_ROWS
    H = MLSTM_HEADS

    def body(q_ref, k_ref, v_ref, gc_ref, bc_ref, gr_ref, br_ref, h_ref, cs_ref, nm_ref, c_st, nm_st):
        @pl.when(pl.program_id(0) == 0)
        def _():
            c_st[...] = jnp.zeros_like(c_st)
            nm_st[...] = jnp.zeros_like(nm_st)

        gc, bc, gr, br = _mlstm_gates(gc_ref, bc_ref, gr_ref, br_ref)
        for hd in range(H):
            c0, n0, m0 = c_st[hd], nm_st[hd:hd + 1, :], nm_st[H + hd:H + hd + 1, 0:1]
            D = MLSTM_HEAD_DIM
            pre = dict(ks=k_ref[:, D * hd:D * hd + D] * (D ** -0.5),
                       vb=v_ref[:, D * hd:D * hd + D].astype(BF16),
                       b_col=bc[:, H + hd:H + hd + 1], i_col=gc[:, hd:hd + 1])
            c1, n1, m1, _, _, _ = _mlstm_update(pre, 0, c0, n0, m0)
            c2, n2, m2, _, _, _ = _mlstm_update(pre, 1, c1, n1, m1)
            f = _mlstm_head(hd, q_ref, k_ref, v_ref, gc, bc, gr, br, (c0, c1), (n0, n1), (m0, m1))
            h_ref[:, D * hd:D * hd + D] = f["num"] / jnp.maximum(jnp.abs(f["den"]), f["floor"])
            cs_ref[0, hd], cs_ref[1, hd] = c0, c1
            nm_ref[0, hd:hd + 1, :], nm_ref[1, hd:hd + 1, :] = n0, n1
            nm_ref[0, H + hd:H + hd + 1, :] = jnp.broadcast_to(m0, (1, 128))
            nm_ref[1, H + hd:H + hd + 1, :] = jnp.broadcast_to(m1, (1, 128))
            c_st[hd] = c2
            nm_st[hd:hd + 1, :] = n2
            nm_st[H + hd:H + hd + 1, :] = jnp.broadcast_to(m2, (1, 128))

    return pl.pallas_call(
        body, name="mlstm_fwd", grid=(steps,), in_specs=_mlstm_specs(T, lambda s: s),
        out_specs=[pl.BlockSpec((STEP_ROWS, 512), lambda s: (s, 0)),
                   pl.BlockSpec((2, H, 128, 128), lambda s: (s, 0, 0, 0)),
                   pl.BlockSpec((2, 8, 128), lambda s: (s, 0, 0))],
        out_shape=[_sds((T, 512), F32), _sds((2 * steps, H, 128, 128), F32),
                   _sds((2 * steps, 8, 128), F32)],
        scratch_shapes=[pltpu.VMEM((H, 128, 128), F32), pltpu.VMEM((8, 128), F32)],
        compiler_params=_params(("arbitrary",)))(qk, qk, pm, gcol, bcol, grow, brow)


def _mlstm_bwd(qk, pm, gcol, bcol, grow, brow, cs, nm, dh):
    T = qk.shape[0]
    steps = T // STEP_ROWS
    H, L, D = MLSTM_HEADS, MLSTM_CHUNK, MLSTM_HEAD_DIM
    rev = lambda s: steps - 1 - s

    def body(q_ref, k_ref, v_ref, gc_ref, bc_ref, gr_ref, br_ref, cs_ref, nm_ref, dh_ref,
             dqk_ref, dv_ref, dgc_ref, dgr_ref, dc_st, dn_st):
        @pl.when(pl.program_id(0) == 0)
        def _():
            dc_st[...] = jnp.zeros_like(dc_st)
            dn_st[...] = jnp.zeros_like(dn_st)

        gc, bc, gr, br = _mlstm_gates(gc_ref, bc_ref, gr_ref, br_ref)
        lane = lax.broadcasted_iota(jnp.int32, (STEP_ROWS, 128), 1)
        sub = lax.broadcasted_iota(jnp.int32, (8, STEP_ROWS), 0)
        row = lax.broadcasted_iota(jnp.int32, (STEP_ROWS, 1), 0)
        dgc = jnp.zeros((STEP_ROWS, 128), F32)
        dgr = jnp.zeros((8, STEP_ROWS), F32)
        for hd in range(H):
            c_prev = (cs_ref[0, hd], cs_ref[1, hd])
            n_prev = (nm_ref[0, hd:hd + 1, :], nm_ref[1, hd:hd + 1, :])
            m_prev = (nm_ref[0, H + hd:H + hd + 1, 0:1], nm_ref[1, H + hd:H + hd + 1, 0:1])
            f = _mlstm_head(hd, q_ref, k_ref, v_ref, gc, bc, gr, br, c_prev, n_prev, m_prev)
            dh_h = dh_ref[:, D * hd:D * hd + D]
            big = jnp.abs(f["den"]) > f["floor"]
            rden = 1.0 / jnp.where(big, jnp.abs(f["den"]), f["floor"])
            dnum = dh_h * rden
            hdh = jnp.sum(f["num"] * dnum, axis=-1, keepdims=True)
            dden = jnp.where(big, -hdh * rden * jnp.sign(f["den"]), 0.0)
            dnum_b = dnum.astype(BF16)
            dsc = _dot(dnum_b, f["vb"], 1, 1) + dden
            g = dsc * f["sc"]
            dv = _dot(f["sc"].astype(BF16), dnum_b, 0, 0)
            dqk_ = (dsc * f["w_intra"]).astype(BF16)
            dq = _dot(dqk_, f["kb"], 1, 0)
            dks = _dot(dqk_, f["qb"], 0, 0)
            wdn = f["w_inter"] * dnum
            wdn_b = wdn.astype(BF16)
            u = jnp.sum(f["qc"] * wdn, axis=-1, keepdims=True) + f["w_inter"] * dden * f["qn"]
            dq_i, dks_s, dv_s, z_all, dg_rows = [], [None, None], [None, None], [None, None], [None, None]
            dcn, dnn = dc_st[hd], dn_st[hd:hd + 1, :]
            for ch in (1, 0):
                rows = slice(L * ch, L * ch + L)
                _, _, _, decay, e_a, kw = _mlstm_update(f, ch, c_prev[ch], n_prev[ch], m_prev[ch])
                dcv = _dot(f["vb"][rows], dcn.astype(BF16), 1, 1)
                dkw = dcv + dnn
                dks_s[ch] = e_a * dkw
                dv_s[ch] = _dot(kw.astype(BF16), dcn.astype(BF16), 1, 0)
                z = e_a * jnp.sum(f["ks"][rows] * dkw, axis=-1, keepdims=True)
                dg = jnp.sum(z) + jnp.sum(decay) * (jnp.sum(c_prev[ch] * dcn) + jnp.sum(n_prev[ch] * dnn))
                z_all[ch], dg_rows[ch] = z, dg
                wd = (f["w_inter"] * dden)[rows]
                dcn = decay * dcn + _dot(f["qb"][rows], wdn_b[rows], 0, 0)
                dnn = decay * dnn + _colsum(wd * f["q"][rows])
            dc_st[hd] = dcn
            dn_st[hd:hd + 1, :] = dnn
            dq_int = jnp.concatenate(
                [_dot(wdn_b[:L], c_prev[0].astype(BF16), 1, 1) + (f["w_inter"] * dden)[:L] * n_prev[0],
                 _dot(wdn_b[L:], c_prev[1].astype(BF16), 1, 1) + (f["w_inter"] * dden)[L:] * n_prev[1]],
                axis=0)
            dq = dq + dq_int
            dks = dks + jnp.concatenate(dks_s, axis=0)
            dv = dv + jnp.concatenate(dv_s, axis=0)
            z = jnp.concatenate(z_all, axis=0)
            dg_col = jnp.where(row == L - 1, dg_rows[0], 0.0) + jnp.where(row == 2 * L - 1, dg_rows[1], 0.0)
            db_col = jnp.sum(g, axis=-1, keepdims=True) + u - z + dg_col
            g_row = jnp.sum(g, axis=0, keepdims=True)
            dgc = dgc + jnp.where(lane == hd, z, 0.0) + jnp.where(lane == H + hd, db_col, 0.0)
            dgr = dgr + jnp.where(sub == hd, g_row, 0.0) - jnp.where(sub == H + hd, g_row, 0.0)
            dqk_ref[:, D * hd:D * hd + D] = dq
            dqk_ref[:, H * D + D * hd:H * D + D * hd + D] = dks * (D ** -0.5)
            dv_ref[:, D * hd:D * hd + D] = dv.astype(BF16)
        dgc_ref[...] = dgc
        dgr_ref[...] = dgr

    return pl.pallas_call(
        body, name="mlstm_bwd", grid=(steps,),
        in_specs=_mlstm_specs(T, rev) + [
            pl.BlockSpec((2, H, 128, 128), lambda s: (rev(s), 0, 0, 0)),
            pl.BlockSpec((2, 8, 128), lambda s: (rev(s), 0, 0)),
            pl.BlockSpec((STEP_ROWS, 512), lambda s: (rev(s), 0))],
        out_specs=[pl.BlockSpec((STEP_ROWS, 1024), lambda s: (rev(s), 0)),
                   pl.BlockSpec((STEP_ROWS, 512), lambda s: (rev(s), 0)),
                   pl.BlockSpec((STEP_ROWS, 128), lambda s: (rev(s), 0)),
                   pl.BlockSpec((8, STEP_ROWS), lambda s: (0, rev(s)))],
        out_shape=[_sds((T, 1024), F32), _sds((T, 512), BF16), _sds((T, 128), F32), _sds((8, T), F32)],
        scratch_shapes=[pltpu.VMEM((H, 128, 128), F32), pltpu.VMEM((8, 128), F32)],
        compiler_params=_params(("arbitrary",)))(qk, qk, pm, gcol, bcol, grow, brow, cs, nm, dh)


def _gate_bwd(dgc, dgr_t, gcol, bcol):
    T = dgc.shape[0]

    def body(a_ref, b_ref, g_ref, bias_ref, o_ref, acc_ref):
        @pl.when(pl.program_id(0) == 0)
        def _():
            acc_ref[...] = jnp.zeros_like(acc_ref)

        d = a_ref[...] + b_ref[...]
        lane = lax.broadcasted_iota(jnp.int32, d.shape, 1)
        is_f = (lane >= MLSTM_HEADS) & (lane < 2 * MLSTM_HEADS)
        dlogf = _chunk_rev_cumsum(jnp.where(is_f, d, 0.0), 0)
        out = jnp.where(is_f, dlogf * _sigmoid(-(g_ref[...] + bias_ref[...])), d)
        o_ref[...] = out.astype(BF16)
        acc_ref[0:1, :] += _colsum(out)

    return _rows("gate_bwd", body, [dgc, dgr_t, gcol, bcol],
                 [_sds((T, 128), BF16), _sds((8, 128), F32)], T)


def _head_norm(h, mu_axis=-1):
    mu = jnp.mean(h, axis=-1, keepdims=True)
    hc = h - mu
    r = lax.rsqrt(jnp.mean(hc * hc, axis=-1, keepdims=True) + NORM_EPS)
    return hc * r, r


def _mlstm_out(hm, pm, w):
    T = hm.shape[0]
    D = MLSTM_HEAD_DIM

    def body(h_ref, o_ref, w_ref, y_ref):
        for hd in range(MLSTM_HEADS):
            cols = slice(D * hd, D * hd + D)
            hn, _ = _head_norm(h_ref[:, cols])
            y_ref[:, cols] = (_sigmoid(o_ref[:, cols]) * hn * w_ref[:, cols]).astype(BF16)

    tr = min(ROW_TILE, T)
    return pl.pallas_call(
        body, name="mlstm_out", grid=(T // tr,),
        in_specs=[pl.BlockSpec((tr, 512), lambda i: (i, 0)), pl.BlockSpec((tr, 512), lambda i: (i, 3)),
                  pl.BlockSpec((1, 512), lambda i: (0, 0))],
        out_specs=pl.BlockSpec((tr, 512), lambda i: (i, 0)), out_shape=_sds((T, 512), BF16),
        compiler_params=_params(("parallel",)))(hm, pm, w)


def _mlstm_out_bwd(hm, pm, w, dy):
    T = hm.shape[0]
    D = MLSTM_HEAD_DIM
    tr = min(ROW_TILE, T)

    def body(h_ref, o_ref, w_ref, dy_ref, dh_ref, do_ref, acc_ref):
        @pl.when(pl.program_id(0) == 0)
        def _():
            acc_ref[...] = jnp.zeros_like(acc_ref)

        for hd in range(MLSTM_HEADS):
            cols = slice(D * hd, D * hd + D)
            hn, r = _head_norm(h_ref[:, cols])
            sg = _sigmoid(o_ref[:, cols])
            dy, w = dy_ref[:, cols], w_ref[:, cols]
            do_ref[:, cols] = (dy * hn * w * sg * (1.0 - sg)).astype(BF16)
            dyn = dy * sg
            acc_ref[0:1, cols] += _colsum(dyn * hn)
            dhn = dyn * w
            dh_ref[:, cols] = r * (dhn - jnp.mean(dhn, axis=-1, keepdims=True)
                                   - hn * jnp.mean(dhn * hn, axis=-1, keepdims=True))

    return pl.pallas_call(
        body, name="mlstm_out_bwd", grid=(T // tr,),
        in_specs=[pl.BlockSpec((tr, 512), lambda i: (i, 0)), pl.BlockSpec((tr, 512), lambda i: (i, 3)),
                  pl.BlockSpec((1, 512), lambda i: (0, 0)), pl.BlockSpec((tr, 512), lambda i: (i, 0))],
        out_specs=[pl.BlockSpec((tr, 512), lambda i: (i, 0)), pl.BlockSpec((tr, 512), lambda i: (i, 0)),
                   pl.BlockSpec((8, 512), lambda i: (0, 0))],
        out_shape=[_sds((T, 512), F32), _sds((T, 512), BF16), _sds((8, 512), F32)],
        compiler_params=_params(("arbitrary",)))(hm, pm, w, dy)


def _adamw(name, w, g, m, v, tr=64):
    R, C = w.shape
    tr = min(tr, R)
    c1 = 1.0 - ADAM_B1 ** ADAM_STEP
    c2 = 1.0 - ADAM_B2 ** ADAM_STEP

    def body(w_ref, g_ref, m_ref, v_ref, d_ref, mo_ref, vo_ref):
        g = g_ref[...]
        m = ADAM_B1 * m_ref[...] + (1.0 - ADAM_B1) * g
        v = ADAM_B2 * v_ref[...] + (1.0 - ADAM_B2) * (g * g)
        mo_ref[...] = m
        vo_ref[...] = v
        d_ref[...] = -ADAM_LR * ((m / c1) / (jnp.sqrt(v / c2) + ADAM_EPS) + ADAM_WD * w_ref[...])

    spec = pl.BlockSpec((tr, C), lambda i: (i, 0))
    return pl.pallas_call(
        body, name=name, grid=(R // tr,), in_specs=[spec] * 4, out_specs=[spec] * 3,
        out_shape=[_sds((R, C), F32)] * 3, compiler_params=_params(("parallel",)))(w, g, m, v)


def _place():
    return lax.axis_index("x"), lax.axis_index("y"), lax.axis_index("c")


def _all_gather8(name, blk, space):
    m, n = blk.shape

    def body(x_ref, out_ref, send_sems, recv_sems, local_sem):
        x, y, c = _place()
        me, sibling = (x, y, c), (x, y, 1 - c)
        chips = [(1 - x, y), (x, 1 - y), (1 - x, 1 - y)]

        def rows(px, py, pc):
            return out_ref.at[pl.ds((4 * px + 2 * py + pc) * m, m), :]

        def copy(k, block, to, src=None):
            return pltpu.make_async_remote_copy(
                src_ref=rows(*block) if src is None else src, dst_ref=rows(*block),
                send_sem=send_sems.at[k], recv_sem=recv_sems.at[k],
                device_id=to, device_id_type=MESH)

        mine = pltpu.make_async_copy(x_ref, rows(*me), local_sem)
        mine.start()
        first = [copy(0, me, sibling, src=x_ref)]
        first += [copy(1 + j, me, (*chip, c), src=x_ref) for j, chip in enumerate(chips)]
        for cp in first:
            cp.start()
        passed = [copy(4 + j, (*chip, c), sibling) for j, chip in enumerate(chips)]
        for j, chip in enumerate(chips):
            copy(1 + j, (*chip, c), me).wait_recv()
            passed[j].start()
        copy(0, sibling, me).wait_recv()
        for j, chip in enumerate(chips):
            copy(4 + j, (*chip, 1 - c), me).wait_recv()
        for cp in first + passed:
            cp.wait_send()
        mine.wait()

    return pl.pallas_call(
        body, name=name, out_shape=_sds((8 * m, n), blk.dtype),
        in_specs=[pl.BlockSpec(memory_space=space)], out_specs=pl.BlockSpec(memory_space=space),
        scratch_shapes=[pltpu.SemaphoreType.DMA((7,)), pltpu.SemaphoreType.DMA((7,)),
                        pltpu.SemaphoreType.DMA],
        compiler_params=pltpu.CompilerParams(vmem_limit_bytes=VMEM_LIMIT))(blk)


def _swap_sibling(name, src):
    def body(src_ref, dst_ref, send_sem, recv_sem):
        x, y, c = _place()
        cp = pltpu.make_async_remote_copy(src_ref=src_ref, dst_ref=dst_ref, send_sem=send_sem,
                                          recv_sem=recv_sem, device_id=(x, y, 1 - c), device_id_type=MESH)
        cp.start()
        cp.wait()

    return pl.pallas_call(
        body, name=name, out_shape=_sds(src.shape, src.dtype),
        in_specs=[pl.BlockSpec(memory_space=pl.ANY)], out_specs=pl.BlockSpec(memory_space=pl.ANY),
        scratch_shapes=[pltpu.SemaphoreType.DMA, pltpu.SemaphoreType.DMA])(src)


def _swap_chips(name, src):
    def body(src_ref, dst_ref, send_sems, recv_sems, local_sem):
        x, y, c = _place()
        mine = 2 * x + y
        chips = [(1 - x, y), (x, 1 - y), (1 - x, 1 - y)]
        own = pltpu.make_async_copy(src_ref.at[mine], dst_ref.at[mine], local_sem)
        own.start()

        def copy(j):
            px, py = chips[j]
            return pltpu.make_async_remote_copy(
                src_ref=src_ref.at[2 * px + py], dst_ref=dst_ref.at[mine],
                send_sem=send_sems.at[j], recv_sem=recv_sems.at[j],
                device_id=(px, py, c), device_id_type=MESH)

        def landing(j):
            px, py = chips[j]
            return pltpu.make_async_remote_copy(
                src_ref=src_ref.at[mine], dst_ref=dst_ref.at[2 * px + py],
                send_sem=send_sems.at[j], recv_sem=recv_sems.at[j],
                device_id=(px, py, c), device_id_type=MESH)

        for j in range(3):
            copy(j).start()
        for j in range(3):
            landing(j).wait_recv()
        for j in range(3):
            copy(j).wait_send()
        own.wait()

    return pl.pallas_call(
        body, name=name, out_shape=_sds(src.shape, src.dtype),
        in_specs=[pl.BlockSpec(memory_space=pl.ANY)], out_specs=pl.BlockSpec(memory_space=pl.ANY),
        scratch_shapes=[pltpu.SemaphoreType.DMA((3,)), pltpu.SemaphoreType.DMA((3,)),
                        pltpu.SemaphoreType.DMA])(src)


def _add2(name, a, b, out_dtype):
    S, R, C = a.shape
    tr = 8
    while R % (tr * 2) == 0 and tr < 256:
        tr *= 2

    def body(a_ref, b_ref, o_ref):
        o_ref[...] = (a_ref[...].astype(F32) + b_ref[...].astype(F32)).astype(o_ref.dtype)

    spec = pl.BlockSpec((1, tr, C), lambda s, i: (s, i, 0))
    return pl.pallas_call(body, name=name, grid=(S, R // tr), in_specs=[spec, spec], out_specs=spec,
                          out_shape=_sds((S, R, C), out_dtype),
                          compiler_params=_params(("parallel", "parallel")))(a, b)


def _sum4(name, a):
    _, R, C = a.shape
    tr = 8
    while R % (tr * 2) == 0 and tr < 256:
        tr *= 2

    def body(a_ref, o_ref):
        acc = a_ref[0].astype(F32)
        for s in range(1, 4):
            acc = acc + a_ref[s].astype(F32)
        o_ref[...] = acc

    return pl.pallas_call(body, name=name, grid=(R // tr,),
                          in_specs=[pl.BlockSpec((4, tr, C), lambda i: (0, i, 0))],
                          out_specs=pl.BlockSpec((tr, C), lambda i: (i, 0)),
                          out_shape=_sds((R, C), F32), compiler_params=_params(("parallel",)))(a)


def _small_update(gathered, w, m, v):
    n = w.shape[1]
    tn = 2048
    c1 = 1.0 - ADAM_B1 ** ADAM_STEP
    c2 = 1.0 - ADAM_B2 ** ADAM_STEP

    def body(g_ref, w_ref, m_ref, v_ref, go_ref, d_ref, mo_ref, vo_ref):
        g = g_ref[0:1, :]
        for d in range(1, 8):
            g = g + g_ref[d:d + 1, :]
        go_ref[...] = g
        m = ADAM_B1 * m_ref[...] + (1.0 - ADAM_B1) * g
        v = ADAM_B2 * v_ref[...] + (1.0 - ADAM_B2) * (g * g)
        mo_ref[...] = m
        vo_ref[...] = v
        d_ref[...] = -ADAM_LR * ((m / c1) / (jnp.sqrt(v / c2) + ADAM_EPS) + ADAM_WD * w_ref[...])

    row = pl.BlockSpec((1, tn), lambda i: (0, i))
    return pl.pallas_call(
        body, name="small_update", grid=(n // tn,),
        in_specs=[pl.BlockSpec((8, tn), lambda i: (0, i)), row, row, row], out_specs=[row] * 4,
        out_shape=[_sds((1, n), F32)] * 4, compiler_params=_params(("parallel",)))(gathered, w, m, v)


def _swiglu(ps, es):
    g, u = ps
    return g * _sigmoid(g) * u, g, u


def _swiglu_bwd(ps, es):
    g, u = es[0].astype(F32), es[1].astype(F32)
    sg = _sigmoid(g)
    return ps[0] * u * (sg * (1.0 + g * (1.0 - sg))), ps[0] * (g * sg)


def _merge(ps, es):
    return _sigmoid(es[0]) * ps[0] + _sigmoid(es[1]) * ps[1], ps[0], ps[1]


def _merge_bwd(ps, es):
    a, b, ga, gm = es
    sa, sm = _sigmoid(ga), _sigmoid(gm)
    dm = ps[0]
    return dm * sa, dm * sm, dm * a * (sa * (1.0 - sa)), dm * b * (sm * (1.0 - sm))


W_IN_PIECES = (("q", 512), ("kv", 256), ("mqk", 1024), ("mv", 512), ("mo", 512), ("if", 8),
               ("ga", 1024), ("gm", 1024))


def _local_step(x, tgt, pos_col, mod, sp, W):
    sh_m, sc_m, gate_m, sh_f, sc_f, gate_f = mod
    w_a = jnp.concatenate([W["q"], W["kv"]], axis=1)
    w_m = jnp.concatenate([W["mqk"], W["mv"], W["mo"]], axis=1)
    w_g = jnp.concatenate([W["ga"], W["gm"]], axis=1)

    h = _pre_norm(x, sp["g_pre_mix"], sc_m, sh_m)
    pa, = _mm("proj_attn", [[(h, w_a)]], [], _first, [F32], tn=256)
    pm, = _mm("proj_mlstm", [[(h, w_m)]], [], _first, [F32], tn=512)
    pif, = _mm("proj_gates", [[(h, W["if"])]], [], _first, [F32], tn=128)
    pg, = _mm("proj_branch_gates", [[(h, w_g)]], [], _first, [F32], tn=512)
    inv = ROPE_THETA ** (-2.0 * jnp.arange(HEAD_DIM // 2, dtype=F32) / HEAD_DIM)
    cos, sin = _rope_tables(pos_col, jnp.tile(inv, 4).reshape(1, 128))
    ya = _attn_fwd(pa, cos, sin, sp["sinks"])
    qk = _conv_fwd(pm, sp["conv_w"], sp["conv_b"])
    bcol = jnp.pad(sp["b_if"], ((0, 0), (0, 120)))
    brow = jnp.broadcast_to(sp["b_if"].reshape(8, 1), (8, 128))
    grow = pif[:, :8].T
    hm, cs, nm = _mlstm_fwd(qk, pm, pif, bcol, grow, brow)
    ym = _mlstm_out(hm, pm, sp["norm_w"])
    merged, br_a, br_m = _mm("branches", [[(ya, W["ba"])], [(ym, W["bm"])]],
                             [(pg, 0), (pg, D_MODEL // 512)], _merge, [BF16, F32, F32], tn=512)
    mix, = _mm("mix_out", [[(merged, W["out"])]], [], _first, [F32], tn=512)
    x1, h2 = _res_norm(x, mix, gate_m, sp["g_post_mix"], sp["g_pre_ffn"], sc_f, sh_f)
    act, gt, up = _mm("ffn_in", [[(h2, W["fg"])], [(h2, W["fu"])]], [], _swiglu, [BF16] * 3, tn=256)
    ff, = _mm("ffn_down", [[(act, W["fd"])]], [], _first, [F32], tn=512)
    dy, dff, acc_l, loss = _final_loss(x1, ff, tgt, gate_f, sp["g_post_ffn"])

    G = {}
    dgt, dup = _mm("ffn_down_bwd", [[(dff, W["fd"])]], [gt, up], _swiglu_bwd, [BF16, BF16],
                   tn=256, nt=True)
    G["fd"] = _mm_tn("dw_ffn_down", act, dff, BF16, 1408, 512, 512)
    dh2, = _mm("ffn_in_bwd", [[(dgt, W["fg"]), (dup, W["fu"])]], [], _first, [F32], tn=512, nt=True)
    G["fg"] = _mm_tn("dw_ffn_gate", h2, dgt, BF16, 1024, 1408, 512)
    G["fu"] = _mm_tn("dw_ffn_up", h2, dup, BF16, 1024, 1408, 512)
    dx1, dmix, acc_r = _res_norm_bwd(x1, x, mix, dh2, dy, sc_f, gate_m, sp["g_pre_ffn"],
                                     sp["g_post_mix"])
    d_a, d_m, dga, dgm = _mm("mix_out_bwd", [[(dmix, W["out"])]],
                             [br_a, br_m, (pg, 0), (pg, D_MODEL // 512)], _merge_bwd,
                             [BF16] * 4, tn=512, nt=True)
    G["out"] = _mm_tn("dw_out", merged, dmix, BF16, 1024, 512, 512)
    dya, = _mm("branch_attn_bwd", [[(d_a, W["ba"])]], [], _first, [F32], tn=512, nt=True)
    dym, = _mm("branch_mlstm_bwd", [[(d_m, W["bm"])]], [], _first, [F32], tn=512, nt=True)
    G["ba"] = _mm_tn("dw_branch_attn", ya, d_a, BF16, 512, 512, 512)
    G["bm"] = _mm_tn("dw_branch_mlstm", ym, d_m, BF16, 512, 512, 512)
    dhm, do_m, acc_n = _mlstm_out_bwd(hm, pm, sp["norm_w"], dym)
    dqk, dv_m, dgc, dgr = _mlstm_bwd(qk, pm, pif, bcol, grow, brow, cs, nm, dhm)
    dif, acc_g = _gate_bwd(dgc, jnp.pad(dgr.T, ((0, 0), (0, 120))), pif, bcol)
    dpre, acc_c = _conv_bwd_pre(pm, sp["conv_w"], sp["conv_b"], dqk)
    du = _conv_bwd_in(dpre, sp["conv_w"])
    dq_a, dcur, dprv, dsink = _attn_bwd(pa, cos, sin, sp["sinks"], dya)
    dkv = _attn_kv_combine(dcur, dprv, cos, sin)
    dproj = {"q": dq_a, "kv": dkv, "mqk": du, "mv": dv_m, "mo": do_m, "if": dif, "ga": dga, "gm": dgm}
    dh, = _mm("proj_bwd", [[(dproj[k], W[k]) for k, _ in W_IN_PIECES]], [], _first, [F32],
              tn=512, nt=True)
    for k, _ in W_IN_PIECES:
        n = dproj[k].shape[1]
        G[k] = _mm_tn("dw_in_" + k, h, dproj[k], BF16, 1024, min(n, 512), 512)
    dx, acc_p = _pre_norm_bwd(x, dh, dx1, sp["g_pre_mix"], sc_m)

    small = {
        "mod": jnp.concatenate([acc_p[1], acc_p[0], acc_r[3], acc_r[1], acc_r[0], acc_l[0]]),
        "g_pre_mix": acc_p[2], "g_post_mix": acc_r[4], "b_if": acc_g[0, :8],
        "conv_w": acc_c[:CONV_WIDTH].reshape(-1), "conv_b": acc_c[CONV_WIDTH],
        "sinks": dsink[:, 0], "norm_w": acc_n[0], "g_pre_ffn": acc_r[2], "g_post_ffn": acc_l[1]}
    return loss, dx, G, small


BIG = (("w_in", 1024, 1218, True), ("ba", 512, 256, True), ("bm", 512, 256, True),
       ("out", 256, 1024, False), ("fg", 1024, 704, True), ("fu", 1024, 704, True),
       ("fd", 704, 1024, False))
PACK_ELEMS = sum(r * c for _, r, c, _ in BIG) // 2
PACK_ROWS = -(-PACK_ELEMS // (16 * 1024)) * 16


def _pack_half(shards, ci, dtype):
    parts = [lax.dynamic_slice_in_dim(shards[k], ci * (r // 2), r // 2, axis=0).reshape(-1)
             for k, r, _, _ in BIG]
    flat = jnp.concatenate(parts).astype(dtype)
    return jnp.pad(flat, (0, PACK_ROWS * 1024 - PACK_ELEMS)).reshape(PACK_ROWS, 1024)


def _unpack_full(blocks):
    out, off = {}, 0
    for k, r, c, by_col in BIG:
        n = (r // 2) * c
        t = blocks[:, off:off + n].reshape(4, r, c)
        out[k] = t.transpose(1, 0, 2).reshape(r, 4 * c) if by_col else t.reshape(4 * r, c)
        off += n
    return out


def _pack_full(full, dtype):
    parts = []
    for k, r, c, by_col in BIG:
        t = full[k]
        t = t.reshape(r, 4, c).transpose(1, 0, 2) if by_col else t.reshape(4, r, c)
        parts.append(t.reshape(8, (r // 2) * c))
    flat = jnp.concatenate(parts, axis=1).astype(dtype)
    return jnp.pad(flat, ((0, 0), (0, PACK_ROWS * 1024 - PACK_ELEMS))).reshape(8, PACK_ROWS, 1024)


def _unpack_shard(lo, hi):
    out, off = {}, 0
    lo, hi = lo.reshape(-1), hi.reshape(-1)
    for k, r, c, _ in BIG:
        n = (r // 2) * c
        out[k] = jnp.concatenate([lo[off:off + n].reshape(r // 2, c), hi[off:off + n].reshape(r // 2, c)])
        off += n
    return out


def _split_w_in(w_in):
    out, off = {}, 0
    for k, n in W_IN_PIECES:
        out[k] = w_in[:, off:off + n]
        off += n
    out["if"] = jnp.pad(out["if"], ((0, 0), (0, 120)))
    return out


SMALL = (("b_ada", 6144), ("g_pre_mix", 1024), ("g_post_mix", 1024), ("b_if", 128), ("conv_w", 4096),
         ("conv_b", 1024), ("sinks", 128), ("norm_w", 512), ("g_pre_ffn", 1024), ("g_post_ffn", 1024))
SMALL_LEN = 8 * 2048


def _pack_small(vals):
    parts = []
    for k, n in SMALL:
        v = vals[k].reshape(-1)
        parts.append(jnp.pad(v, (0, n - v.shape[0])))
    flat = jnp.concatenate(parts)
    return jnp.pad(flat, (0, SMALL_LEN - flat.shape[0]))


def _unpack_small(flat, shapes):
    out, off = {}, 0
    for k, n in SMALL:
        size = 1
        for d in shapes[k]:
            size *= d
        out[k] = flat[off:off + size].reshape(shapes[k])
        off += n
    return out


def kernel(x, c, positions, w_ada, b_ada, g_pre_mix, g_post_mix, w_in, b_if, conv_w, conv_b, attn_sinks, mlstm_norm_w, w_branch_attn, w_branch_mlstm, w_out, g_pre_ffn, g_post_ffn, w_ffn_gate, w_ffn_up, w_ffn_down, loss_target, m_w_ada, m_b_ada, m_g_pre_mix, m_g_post_mix, m_w_in, m_b_if, m_conv_w, m_conv_b, m_attn_sinks, m_mlstm_norm_w, m_w_branch_attn, m_w_branch_mlstm, m_w_out, m_g_pre_ffn, m_g_post_ffn, m_w_ffn_gate, m_w_ffn_up, m_w_ffn_down, v_w_ada, v_b_ada, v_g_pre_mix, v_g_post_mix, v_w_in, v_b_if, v_conv_w, v_conv_b, v_attn_sinks, v_mlstm_norm_w, v_w_branch_attn, v_w_branch_mlstm, v_w_out, v_g_pre_ffn, v_g_post_ffn, v_w_ffn_gate, v_w_ffn_up, v_w_ffn_down):
    xi, yi, ci = _place()
    chip = 2 * xi + yi
    dev = 2 * chip + ci
    T = x.shape[1]
    ada_cols = w_ada.shape[2]

    blk = jnp.concatenate([c.reshape(-1), conv_w.reshape(-1)]).reshape(8, 256)
    got = _all_gather8("gather_cond", blk, pltpu.VMEM).reshape(8, 2048)
    c_all = got[:, :D_MODEL].astype(BF16)
    conv_full = got[::2, D_MODEL:].reshape(4, CONV_WIDTH, -1).transpose(1, 0, 2).reshape(CONV_WIDTH, -1)

    b_sh = lax.dynamic_slice_in_dim(b_ada, chip * ada_cols, ada_cols, axis=1)
    mod_part, = _mm("ada_mod", [[(c_all, w_ada[0].astype(BF16))]], [b_sh],
                    lambda ps, es: (ps[0] + es[0],), [F32], tn=512, tm=8)
    mod_all = _all_gather8("gather_mod", mod_part, pltpu.VMEM).reshape(4, 2, 8, ada_cols)[:, 0]
    mod = lax.dynamic_index_in_dim(mod_all, dev, axis=1, keepdims=False).reshape(6, 1, D_MODEL)

    shards = {"w_in": w_in[0], "ba": w_branch_attn[0], "bm": w_branch_mlstm[0], "out": w_out[0],
              "fg": w_ffn_gate[0], "fu": w_ffn_up[0], "fd": w_ffn_down[0]}
    wall = _all_gather8("gather_weights", _pack_half(shards, ci, BF16), pl.ANY)
    W = _unpack_full(wall.reshape(8, -1))
    W.update(_split_w_in(W.pop("w_in")))

    sp = {"g_pre_mix": g_pre_mix, "g_post_mix": g_post_mix, "b_if": b_if, "conv_w": conv_full,
          "conv_b": conv_b, "sinks": attn_sinks, "norm_w": mlstm_norm_w, "g_pre_ffn": g_pre_ffn,
          "g_post_ffn": g_post_ffn}
    loss, dx, G, small = _local_step(x[0], loss_target[0], positions.reshape(T, 1),
                                     [mod[i] for i in range(6)], sp, W)

    G["w_in"] = jnp.concatenate([G.pop(k)[:, :n] for k, n in W_IN_PIECES], axis=1)
    gp = _pack_full(G, BF16).reshape(4, 2, PACK_ROWS, 1024)
    mine = lax.dynamic_index_in_dim(gp, ci, axis=1, keepdims=False)
    theirs = lax.dynamic_index_in_dim(gp, 1 - ci, axis=1, keepdims=False)
    pair = _add2("rs_pair_sum", mine, _swap_sibling("rs_pair", theirs), BF16)
    red = _sum4("rs_chip_sum", _swap_chips("rs_chips", pair))
    other = _swap_sibling("rs_share", red)
    gsh = _unpack_shard(jnp.where(ci == 0, red, other), jnp.where(ci == 0, other, red))

    small["b_ada"] = small.pop("mod")
    vec = _pack_small(small).reshape(8, 2048)
    g_all = _all_gather8("gather_small", vec, pltpu.VMEM).reshape(8, SMALL_LEN)
    dmod_sh = lax.dynamic_slice_in_dim(g_all[:, :6 * D_MODEL], chip * ada_cols, ada_cols, axis=1)
    g_w_ada = _mm_tn("dw_ada", c_all, dmod_sh.astype(BF16), F32, D_MODEL, 512, 8)

    smalls = {"b_ada": (b_ada, m_b_ada, v_b_ada), "g_pre_mix": (g_pre_mix, m_g_pre_mix, v_g_pre_mix),
              "g_post_mix": (g_post_mix, m_g_post_mix, v_g_post_mix), "b_if": (b_if, m_b_if, v_b_if),
              "conv_w": None, "conv_b": (conv_b, m_conv_b, v_conv_b),
              "sinks": (attn_sinks, m_attn_sinks, v_attn_sinks),
              "norm_w": (mlstm_norm_w, m_mlstm_norm_w, v_mlstm_norm_w),
              "g_pre_ffn": (g_pre_ffn, m_g_pre_ffn, v_g_pre_ffn),
              "g_post_ffn": (g_post_ffn, m_g_post_ffn, v_g_post_ffn)}
    shapes = {k: (t[0].shape if t is not None else (1, CONV_WIDTH, D_MODEL)) for k, t in smalls.items()}
    zeros = jnp.zeros((CONV_WIDTH * D_MODEL,), F32)
    packs = [_pack_small({k: (t[i] if t is not None else zeros) for k, t in smalls.items()}).reshape(1, -1)
             for i in range(3)]
    s_out = [_unpack_small(o[0], shapes) for o in _small_update(g_all, *packs)]
    g_conv = lax.dynamic_slice_in_dim(s_out[0]["conv_w"], chip * conv_w.shape[2], conv_w.shape[2], axis=2)

    res = {}
    for k, t in smalls.items():
        if t is not None:
            res[k] = tuple(o[k] for o in s_out)
    res["conv_w"] = (g_conv, *[o[None] for o in _adamw("adam_conv_w", conv_w[0], g_conv[0], m_conv_w[0], v_conv_w[0])])
    res["w_ada"] = (g_w_ada[None], *[o[None] for o in _adamw("adam_w_ada", w_ada[0], g_w_ada, m_w_ada[0], v_w_ada[0])])
    bigs = {"w_in": (w_in, m_w_in, v_w_in), "ba": (w_branch_attn, m_w_branch_attn, v_w_branch_attn),
            "bm": (w_branch_mlstm, m_w_branch_mlstm, v_w_branch_mlstm), "out": (w_out, m_w_out, v_w_out),
            "fg": (w_ffn_gate, m_w_ffn_gate, v_w_ffn_gate), "fu": (w_ffn_up, m_w_ffn_up, v_w_ffn_up),
            "fd": (w_ffn_down, m_w_ffn_down, v_w_ffn_down)}
    for k, (w, m, v) in bigs.items():
        res[k] = (gsh[k][None], *[o[None] for o in _adamw("adam_" + k, w[0], gsh[k], m[0], v[0])])

    order = ("w_ada", "b_ada", "g_pre_mix", "g_post_mix", "w_in", "b_if", "conv_w", "conv_b", "sinks",
             "norm_w", "ba", "bm", "out", "g_pre_ffn", "g_post_ffn", "fg", "fu", "fd")
    total = lax.psum(loss[0, 0], ("x", "y", "c"))
    return (total, dx[None], *[res[k][0] for k in order], *[res[k][1] for k in order],
            *[res[k][2] for k in order], *[res[k][3] for k in order])
```

```python
import functools

import jax
import jax.numpy as jnp
from jax import lax
from jax.experimental import pallas as pl
from jax.experimental.pallas import tpu as pltpu

F32, BF16 = jnp.float32, jnp.bfloat16
MESH = pl.DeviceIdType.MESH

D_MODEL = 1024
N_Q_HEADS, N_KV_HEADS, HEAD_DIM, WINDOW = 8, 2, 64, 128
ROPE_THETA = 10000.0
MLSTM_HEADS, MLSTM_HEAD_DIM, MLSTM_CHUNK, CONV_WIDTH = 4, 128, 64, 4
D_FF = 2816
NORM_EPS = 1e-6
ADAM_LR, ADAM_B1, ADAM_B2, ADAM_EPS, ADAM_WD, ADAM_STEP = 0.001, 0.9, 0.999, 1e-08, 0.01, 10

VMEM_LIMIT = 56 * 1024 * 1024
ROW_TILE = 256
MM_TM = 512
ATTN_BLK = WINDOW
STEP_ROWS = 2 * MLSTM_CHUNK
NEG_INF = float("-inf")


def _params(sem):
    return pltpu.CompilerParams(dimension_semantics=sem, vmem_limit_bytes=VMEM_LIMIT)


def _sds(shape, dtype):
    return jax.ShapeDtypeStruct(shape, dtype)


def _sigmoid(x):
    return 1.0 / (1.0 + jnp.exp(-x))


def _dot(a, b, ca, cb):
    return lax.dot_general(a, b, (((ca,), (cb,)), ((), ())), preferred_element_type=F32)


def _mm(name, prods, extras, epi, out_dtypes, cn, nt=False, tm=MM_TM):
    flat = [ab for p in prods for ab in p]
    counts = [len(p) for p in prods]
    M = flat[0][0].shape[0]
    N = flat[0][1].shape[0 if nt else 1]
    tm = min(tm, M)
    n_in = 2 * len(flat) + len(extras)

    def body(*refs):
        ins, outs = refs[:n_in], refs[n_in:]
        for j in range(N // cn):
            cols = slice(j * cn, (j + 1) * cn)
            k, ps = 0, []
            for cnt in counts:
                acc = None
                for _ in range(cnt):
                    b = ins[k + 1][cols, :] if nt else ins[k + 1][:, cols]
                    d = _dot(ins[k][...], b, 1, 1 if nt else 0)
                    acc = d if acc is None else acc + d
                    k += 2
                ps.append(acc)
            res = epi(ps, [r[:, cols] for r in ins[k:]])
            for o, r in zip(outs, res):
                o[:, cols] = r.astype(o.dtype)

    in_specs, args = [], []
    for a, b in flat:
        in_specs.append(pl.BlockSpec((tm, a.shape[1]), lambda i: (i, 0)))
        in_specs.append(pl.BlockSpec(b.shape, lambda i: (0, 0), pipeline_mode=pl.Buffered(1)))
        args += [a, b]
    for e in extras:
        e, off = e if isinstance(e, tuple) else (e, 0)
        rows = 1 if e.shape[0] == 1 else tm
        in_specs.append(pl.BlockSpec((rows, N), lambda i, off=off, rows=rows: (0 if rows == 1 else i, off)))
        args.append(e)
    return pl.pallas_call(
        body, name=name, grid=(M // tm,), in_specs=in_specs,
        out_specs=[pl.BlockSpec((tm, N), lambda i: (i, 0)) for _ in out_dtypes],
        out_shape=[_sds((M, N), dt) for dt in out_dtypes],
        compiler_params=_params(("parallel",)))(*args)


def _mm_tn(name, a, b, out_dtype, tk, tn, tt):
    T, Ka = a.shape
    N = b.shape[1]
    tt = min(tt, T)
    steps = T // tt

    def body(a_ref, b_ref, o_ref, acc):
        t = pl.program_id(2)

        @pl.when(t == 0)
        def _():
            acc[...] = jnp.zeros_like(acc)

        acc[...] += _dot(a_ref[...], b_ref[...], 0, 0)

        @pl.when(t == steps - 1)
        def _():
            o_ref[...] = acc[...].astype(o_ref.dtype)

    return pl.pallas_call(
        body, name=name, grid=(Ka // tk, N // tn, steps),
        in_specs=[pl.BlockSpec((tt, tk), lambda i, j, t: (t, i)),
                  pl.BlockSpec((tt, tn), lambda i, j, t: (t, j))],
        out_specs=pl.BlockSpec((tk, tn), lambda i, j, t: (i, j)),
        out_shape=_sds((Ka, N), out_dtype),
        scratch_shapes=[pltpu.VMEM((tk, tn), F32)],
        compiler_params=_params(("parallel", "parallel", "arbitrary")))(a, b)


def _first(ps, es):
    return (ps[0],)


def _rows(name, body, ins, out_shapes, T, tr=ROW_TILE):
    tr = min(tr, T)

    def spec(shape):
        if shape[0] == T:
            return pl.BlockSpec((tr,) + tuple(shape[1:]), lambda i: (i,) + (0,) * (len(shape) - 1))
        return pl.BlockSpec(tuple(shape), lambda i: (0,) * len(shape))

    return pl.pallas_call(
        body, name=name, grid=(T // tr,),
        in_specs=[spec(a.shape) for a in ins], out_specs=[spec(s.shape) for s in out_shapes],
        out_shape=out_shapes, compiler_params=_params(("arbitrary",)))(*ins)


def _rms(x):
    r = lax.rsqrt(jnp.mean(x * x, axis=-1, keepdims=True) + NORM_EPS)
    return x * r, r


def _rms_bwd(dxn, xn, r):
    return r * (dxn - xn * jnp.mean(dxn * xn, axis=-1, keepdims=True))


def _colsum(v):
    return jnp.sum(v, axis=0, keepdims=True)


def _pre_norm(x, g, sc, sh):
    T = x.shape[0]

    def body(x_ref, g_ref, sc_ref, sh_ref, h_ref):
        xn, _ = _rms(x_ref[...])
        h_ref[...] = (xn * g_ref[...] * (1.0 + sc_ref[...]) + sh_ref[...]).astype(BF16)

    return _rows("pre_norm", body, [x, g, sc, sh], [_sds((T, D_MODEL), BF16)], T)[0]


def _res_norm(x, mix, gate, gpost, g2, sc2, sh2):
    T = x.shape[0]

    def body(x_ref, mix_ref, gate_ref, gp_ref, g2_ref, sc_ref, sh_ref, x1_ref, h2_ref):
        mh, _ = _rms(mix_ref[...])
        x1 = x_ref[...] + gate_ref[...] * (mh * gp_ref[...])
        x1_ref[...] = x1
        xn, _ = _rms(x1)
        h2_ref[...] = (xn * g2_ref[...] * (1.0 + sc_ref[...]) + sh_ref[...]).astype(BF16)

    return _rows("res_norm", body, [x, mix, gate, gpost, g2, sc2, sh2],
                 [_sds((T, D_MODEL), F32), _sds((T, D_MODEL), BF16)], T)


def _final_loss(x1, ff, tgt, gate, gpost):
    T = x1.shape[0]

    def body(x1_ref, ff_ref, t_ref, gate_ref, gp_ref, dy_ref, dff_ref, acc_ref, loss_ref):
        @pl.when(pl.program_id(0) == 0)
        def _():
            acc_ref[...] = jnp.zeros_like(acc_ref)
            loss_ref[...] = jnp.zeros_like(loss_ref)

        fh, r = _rms(ff_ref[...])
        gate, gp = gate_ref[...], gp_ref[...]
        e = x1_ref[...] + gate * (fh * gp) - t_ref[...]
        loss_ref[...] += 0.5 * jnp.sum(jnp.mean(e * e, axis=-1, keepdims=True))
        dy = e * (1.0 / D_MODEL)
        dy_ref[...] = dy
        acc_ref[0:1, :] += _colsum(dy * fh * gp)
        acc_ref[1:2, :] += _colsum(dy * gate * fh)
        dff_ref[...] = _rms_bwd(dy * gate * gp, fh, r).astype(BF16)

    return _rows("final_loss", body, [x1, ff, tgt, gate, gpost],
                 [_sds((T, D_MODEL), F32), _sds((T, D_MODEL), BF16),
                  _sds((8, D_MODEL), F32), _sds((1, 128), F32)], T)


def _res_norm_bwd(x1, x, mix, dh2, dy, sc2, gate, g2, gpost):
    T = x.shape[0]

    def body(x1_ref, x_ref, mix_ref, dh_ref, dy_ref, sc_ref, gate_ref, g2_ref, gp_ref,
             dx1_ref, dmix_ref, acc_ref):
        @pl.when(pl.program_id(0) == 0)
        def _():
            acc_ref[...] = jnp.zeros_like(acc_ref)

        xn, r1 = _rms(x1_ref[...])
        dh, sc, g2 = dh_ref[...], sc_ref[...], g2_ref[...]
        acc_ref[0:1, :] += _colsum(dh * xn * g2)
        acc_ref[1:2, :] += _colsum(dh)
        acc_ref[2:3, :] += _colsum(dh * (1.0 + sc) * xn)
        dx1 = dy_ref[...] + _rms_bwd(dh * (1.0 + sc) * g2, xn, r1)
        dx1_ref[...] = dx1
        mh, rm = _rms(mix_ref[...])
        gate, gp = gate_ref[...], gp_ref[...]
        acc_ref[3:4, :] += _colsum(dx1 * mh * gp)
        acc_ref[4:5, :] += _colsum(dx1 * gate * mh)
        dmix_ref[...] = _rms_bwd(dx1 * gate * gp, mh, rm).astype(BF16)

    return _rows("res_norm_bwd", body, [x1, x, mix, dh2, dy, sc2, gate, g2, gpost],
                 [_sds((T, D_MODEL), F32), _sds((T, D_MODEL), BF16), _sds((8, D_MODEL), F32)], T)


def _pre_norm_bwd(x, dh, dx1, g, sc):
    T = x.shape[0]

    def body(x_ref, dh_ref, dx1_ref, g_ref, sc_ref, dx_ref, acc_ref):
        @pl.when(pl.program_id(0) == 0)
        def _():
            acc_ref[...] = jnp.zeros_like(acc_ref)

        xn, r = _rms(x_ref[...])
        dh, sc, g = dh_ref[...], sc_ref[...], g_ref[...]
        acc_ref[0:1, :] += _colsum(dh * xn * g)
        acc_ref[1:2, :] += _colsum(dh)
        acc_ref[2:3, :] += _colsum(dh * (1.0 + sc) * xn)
        dx_ref[...] = dx1_ref[...] + _rms_bwd(dh * (1.0 + sc) * g, xn, r)

    return _rows("pre_norm_bwd", body, [x, dh, dx1, g, sc],
                 [_sds((T, D_MODEL), F32), _sds((8, D_MODEL), F32)], T)


def _rope_tables(pos_col, inv_freq):
    T = pos_col.shape[0]

    def body(p_ref, f_ref, c_ref, s_ref):
        ang = p_ref[...].astype(F32) * f_ref[...]
        lane = lax.broadcasted_iota(jnp.int32, ang.shape, 1)
        c_ref[...] = jnp.cos(ang)
        s_ref[...] = jnp.where(lane % HEAD_DIM < HEAD_DIM // 2, -1.0, 1.0) * jnp.sin(ang)

    return _rows("rope_tables", body, [pos_col, inv_freq],
                 [_sds((T, 128), F32), _sds((T, 128), F32)], T, tr=512)


def _swap_halves(t):
    W = t.shape[1]
    lane = lax.broadcasted_iota(jnp.int32, t.shape, 1)
    half = HEAD_DIM // 2
    return jnp.where(lane % HEAD_DIM < half, pltpu.roll(t, W - half, 1), pltpu.roll(t, half, 1))


def _widen(c, W):
    return c if W == 128 else jnp.concatenate([c] * (W // 128), axis=1)


def _rope(t, c, s):
    W = t.shape[1]
    return t * _widen(c, W) + _swap_halves(t) * _widen(s, W)


def _unrope(dy, c, s):
    W = dy.shape[1]
    return dy * _widen(c, W) + _swap_halves(dy * _widen(s, W))


def _attn_mask(n):
    qi = lax.broadcasted_iota(jnp.int32, (ATTN_BLK, 2 * ATTN_BLK), 0)
    kj = lax.broadcasted_iota(jnp.int32, (ATTN_BLK, 2 * ATTN_BLK), 1)
    rel = kj - ATTN_BLK
    return (rel <= qi) & (qi - rel < WINDOW) & ((n > 0) | (kj >= ATTN_BLK))


def _attn_load(cur, prv, cc, sc, cp, sp):
    x, xp = cur[...], prv[...]
    q = _rope(x[:, :512], cc[...], sc[...]) * (HEAD_DIM ** -0.5)
    k = jnp.concatenate([_rope(xp[:, 512:640], cp[...], sp[...]),
                         _rope(x[:, 512:640], cc[...], sc[...])], axis=0)
    v = jnp.concatenate([xp[:, 640:768], x[:, 640:768]], axis=0)
    return q, k, v


def _head_operands(q, k, v, head):
    pair, a, kv = head // 2, head % 2, head // (N_Q_HEADS // N_KV_HEADS)
    lane_q = lax.broadcasted_iota(jnp.int32, (ATTN_BLK, 128), 1)
    lane_k = lax.broadcasted_iota(jnp.int32, (2 * ATTN_BLK, 128), 1)
    sel_q = (lane_q // HEAD_DIM) == a
    sel_k = (lane_k // HEAD_DIM) == a
    qh = jnp.where(sel_q, q[:, 128 * pair:128 * pair + 128], 0.0)
    ku = k if a == kv else pltpu.roll(k, HEAD_DIM, 1)
    vu = v if a == kv else pltpu.roll(v, HEAD_DIM, 1)
    return qh, jnp.where(sel_k, ku, 0.0), jnp.where(sel_k, vu, 0.0), sel_q, a != kv


def _attn_probs(qh, kh, mask, sink):
    s = _dot(qh.astype(BF16), kh.astype(BF16), 1, 1)
    s = jnp.where(mask, s, NEG_INF)
    m = jnp.maximum(jnp.max(s, axis=-1, keepdims=True), sink)
    p = jnp.exp(s - m)
    es = jnp.exp(sink - m)
    rl = 1.0 / (jnp.sum(p, axis=-1, keepdims=True) + es)
    return p, es, rl


def _attn_specs(nb):
    blk = lambda w: pl.BlockSpec((ATTN_BLK, w), lambda n: (n, 0))
    prv = lambda w: pl.BlockSpec((ATTN_BLK, w), lambda n: (jnp.maximum(n - 1, 0), 0))
    return [blk(768), prv(768), blk(128), blk(128), prv(128), prv(128),
            pl.BlockSpec(memory_space=pltpu.SMEM)]


def _attn_fwd(pa, cos, sin, sinks):
    T = pa.shape[0]
    nb = T // ATTN_BLK

    def body(cur, prv, cc, sc, cp, sp, snk, y_ref):
        n = pl.program_id(0)
        q, k, v = _attn_load(cur, prv, cc, sc, cp, sp)
        mask = _attn_mask(n)
        for pair in range(N_Q_HEADS // 2):
            o = None
            for a in range(2):
                head = 2 * pair + a
                qh, kh, vh, _, _ = _head_operands(q, k, v, head)
                p, _, rl = _attn_probs(qh, kh, mask, snk[0, head])
                oh = _dot(p.astype(BF16), vh.astype(BF16), 1, 0) * rl
                o = oh if o is None else o + oh
            y_ref[:, 128 * pair:128 * pair + 128] = o.astype(BF16)

    return pl.pallas_call(
        body, name="attn_fwd", grid=(nb,), in_specs=_attn_specs(nb),
        out_specs=pl.BlockSpec((ATTN_BLK, 512), lambda n: (n, 0)),
        out_shape=_sds((T, 512), BF16), compiler_params=_params(("parallel",)))(
            pa, pa, cos, sin, cos, sin, sinks)


def _attn_bwd(pa, cos, sin, sinks, dy):
    T = pa.shape[0]
    nb = T // ATTN_BLK

    def body(cur, prv, cc, sc, cp, sp, snk, dy_ref, dq_ref, dcur_ref, dprv_ref, dsink_ref):
        n = pl.program_id(0)

        @pl.when(n == 0)
        def _():
            dsink_ref[...] = jnp.zeros_like(dsink_ref)

        q, k, v = _attn_load(cur, prv, cc, sc, cp, sp)
        mask = _attn_mask(n)
        dk = jnp.zeros((2 * ATTN_BLK, 128), F32)
        dv = jnp.zeros((2 * ATTN_BLK, 128), F32)
        for pair in range(N_Q_HEADS // 2):
            dqp = None
            for a in range(2):
                head = 2 * pair + a
                qh, kh, vh, sel_q, rolled = _head_operands(q, k, v, head)
                p, es, rl = _attn_probs(qh, kh, mask, snk[0, head])
                pn = p * rl
                do = jnp.where(sel_q, dy_ref[:, 128 * pair:128 * pair + 128], 0.0).astype(BF16)
                dp = _dot(do, vh.astype(BF16), 1, 1)
                delta = jnp.sum(pn * dp, axis=-1, keepdims=True)
                ds = (pn * (dp - delta)).astype(BF16)
                dsink_ref[head:head + 1, :] += -jnp.sum(es * rl * delta)
                dqh = _dot(ds, kh.astype(BF16), 1, 0)
                dqp = dqh if dqp is None else dqp + dqh
                dkh = _dot(ds, qh.astype(BF16), 0, 0)
                dvh = _dot(pn.astype(BF16), do, 0, 0)
                if rolled:
                    dkh, dvh = pltpu.roll(dkh, HEAD_DIM, 1), pltpu.roll(dvh, HEAD_DIM, 1)
                dk, dv = dk + dkh, dv + dvh
            dq_ref[:, 128 * pair:128 * pair + 128] = _unrope(
                dqp * (HEAD_DIM ** -0.5), cc[...], sc[...]).astype(BF16)
        dcur_ref[:, 0:128] = dk[ATTN_BLK:]
        dcur_ref[:, 128:256] = dv[ATTN_BLK:]
        dprv_ref[:, 0:128] = dk[:ATTN_BLK]
        dprv_ref[:, 128:256] = dv[:ATTN_BLK]

    blk = lambda w: pl.BlockSpec((ATTN_BLK, w), lambda n: (n, 0))
    return pl.pallas_call(
        body, name="attn_bwd", grid=(nb,), in_specs=_attn_specs(nb) + [blk(512)],
        out_specs=[blk(512), blk(256), blk(256), pl.BlockSpec((8, 128), lambda n: (0, 0))],
        out_shape=[_sds((T, 512), BF16), _sds((T, 256), F32), _sds((T, 256), F32),
                   _sds((8, 128), F32)],
        compiler_params=_params(("arbitrary",)))(pa, pa, cos, sin, cos, sin, sinks, dy)


def _attn_kv_combine(dcur, dprv, cos, sin):
    T = dcur.shape[0]
    nb = T // ATTN_BLK

    def body(c_ref, p_ref, cc, sc, o_ref):
        n = pl.program_id(0)
        t = c_ref[...] + jnp.where(n < nb - 1, p_ref[...], 0.0)
        o_ref[:, 0:128] = _unrope(t[:, 0:128], cc[...], sc[...]).astype(BF16)
        o_ref[:, 128:256] = t[:, 128:256].astype(BF16)

    blk = lambda w: pl.BlockSpec((ATTN_BLK, w), lambda n: (n, 0))
    nxt = pl.BlockSpec((ATTN_BLK, 256), lambda n: (jnp.minimum(n + 1, nb - 1), 0))
    return pl.pallas_call(
        body, name="attn_kv_combine", grid=(nb,), in_specs=[blk(256), nxt, blk(128), blk(128)],
        out_specs=blk(256), out_shape=_sds((T, 256), BF16),
        compiler_params=_params(("parallel",)))(dcur, dprv, cos, sin)


CONV_COLS = 2 * MLSTM_HEADS * MLSTM_HEAD_DIM


def _conv_pre(cur_ref, halo_ref, w_ref, b_ref, i, tr):
    xx = jnp.concatenate([jnp.where(i > 0, halo_ref[...], 0.0), cur_ref[...]], axis=0)
    taps = [(pltpu.roll(xx, CONV_WIDTH - 1 - j, 0) if j < CONV_WIDTH - 1 else xx)[8:8 + tr]
            for j in range(CONV_WIDTH)]
    pre = b_ref[...]
    for j in range(CONV_WIDTH):
        pre = pre + taps[j] * w_ref[j:j + 1, :]
    return pre, taps


def _conv_specs(T, tr):
    return [pl.BlockSpec((tr, CONV_COLS), lambda i: (i, 0)),
            pl.BlockSpec((8, CONV_COLS), lambda i: (jnp.maximum(i * (tr // 8) - 1, 0), 0)),
            pl.BlockSpec((CONV_WIDTH, CONV_COLS), lambda i: (0, 0)),
            pl.BlockSpec((1, CONV_COLS), lambda i: (0, 0))]


def _conv_fwd(pm, w, b):
    T = pm.shape[0]
    tr = min(ROW_TILE, T)

    def body(cur_ref, halo_ref, w_ref, b_ref, o_ref):
        pre, _ = _conv_pre(cur_ref, halo_ref, w_ref, b_ref, pl.program_id(0), tr)
        o_ref[...] = pre * _sigmoid(pre)

    return pl.pallas_call(
        body, name="conv_fwd", grid=(T // tr,), in_specs=_conv_specs(T, tr),
        out_specs=pl.BlockSpec((tr, CONV_COLS), lambda i: (i, 0)),
        out_shape=_sds((T, CONV_COLS), F32), compiler_params=_params(("parallel",)))(pm, pm, w, b)


def _conv_bwd_pre(pm, w, b, dqk):
    T = pm.shape[0]
    tr = min(ROW_TILE, T)

    def body(cur_ref, halo_ref, w_ref, b_ref, d_ref, dpre_ref, acc_ref):
        i = pl.program_id(0)

        @pl.when(i == 0)
        def _():
            acc_ref[...] = jnp.zeros_like(acc_ref)

        pre, taps = _conv_pre(cur_ref, halo_ref, w_ref, b_ref, i, tr)
        sg = _sigmoid(pre)
        dpre = d_ref[...] * (sg * (1.0 + pre * (1.0 - sg)))
        dpre_ref[...] = dpre
        for j in range(CONV_WIDTH):
            acc_ref[j:j + 1, :] += _colsum(dpre * taps[j])
        acc_ref[CONV_WIDTH:CONV_WIDTH + 1, :] += _colsum(dpre)

    return pl.pallas_call(
        body, name="conv_bwd_pre", grid=(T // tr,),
        in_specs=_conv_specs(T, tr) + [pl.BlockSpec((tr, CONV_COLS), lambda i: (i, 0))],
        out_specs=[pl.BlockSpec((tr, CONV_COLS), lambda i: (i, 0)),
                   pl.BlockSpec((8, CONV_COLS), lambda i: (0, 0))],
        out_shape=[_sds((T, CONV_COLS), F32), _sds((8, CONV_COLS), F32)],
        compiler_params=_params(("arbitrary",)))(pm, pm, w, b, dqk)


def _conv_bwd_in(dpre, w):
    T = dpre.shape[0]
    tr = min(ROW_TILE, T)
    nt = T // tr

    def body(cur_ref, halo_ref, w_ref, o_ref):
        i = pl.program_id(0)
        yy = jnp.concatenate([cur_ref[...], jnp.where(i < nt - 1, halo_ref[...], 0.0)], axis=0)
        du = cur_ref[...] * w_ref[CONV_WIDTH - 1:CONV_WIDTH, :]
        for j in range(CONV_WIDTH - 1):
            k = CONV_WIDTH - 1 - j
            du = du + pltpu.roll(yy, tr + 8 - k, 0)[:tr] * w_ref[j:j + 1, :]
        o_ref[...] = du.astype(BF16)

    return pl.pallas_call(
        body, name="conv_bwd_in", grid=(nt,),
        in_specs=[pl.BlockSpec((tr, CONV_COLS), lambda i: (i, 0)),
                  pl.BlockSpec((8, CONV_COLS),
                               lambda i: (jnp.minimum((i + 1) * (tr // 8), T // 8 - 1), 0)),
                  pl.BlockSpec((CONV_WIDTH, CONV_COLS), lambda i: (0, 0))],
        out_specs=pl.BlockSpec((tr, CONV_COLS), lambda i: (i, 0)),
        out_shape=_sds((T, CONV_COLS), BF16), compiler_params=_params(("parallel",)))(dpre, dpre, w)


def _log_sigmoid(x):
    return jnp.minimum(x, 0.0) - jnp.log1p(jnp.exp(-jnp.abs(x)))


def _chunk_cumsum(x, axis):
    idx = lax.broadcasted_iota(jnp.int32, x.shape, axis) % MLSTM_CHUNK
    k = 1
    while k < MLSTM_CHUNK:
        x = x + jnp.where(idx >= k, pltpu.roll(x, k, axis), 0.0)
        k *= 2
    return x


def _chunk_rev_cumsum(x, axis):
    n = x.shape[axis]
    idx = lax.broadcasted_iota(jnp.int32, x.shape, axis) % MLSTM_CHUNK
    k = 1
    while k < MLSTM_CHUNK:
        x = x + jnp.where(idx < MLSTM_CHUNK - k, pltpu.roll(x, n - k, axis), 0.0)
        k *= 2
    return x


def _mlstm_gates(gc_ref, bc_ref, gr_ref, br_ref):
    gc = gc_ref[...] + bc_ref[...]
    gr = gr_ref[...] + br_ref[...]
    return gc, _chunk_cumsum(_log_sigmoid(gc), 0), gr, _chunk_cumsum(_log_sigmoid(gr), 1)


def _mlstm_head(hd, q_ref, k_ref, v_ref, gc, bc, gr, br, c_prev, n_prev, m_prev):
    L = MLSTM_CHUNK
    D = MLSTM_HEAD_DIM
    q = q_ref[:, D * hd:D * hd + D]
    ks = k_ref[:, D * hd:D * hd + D] * (D ** -0.5)
    v = v_ref[:, D * hd:D * hd + D]
    qb, kb, vb = q.astype(BF16), ks.astype(BF16), v.astype(BF16)
    b_col = bc[:, MLSTM_HEADS + hd:MLSTM_HEADS + hd + 1]
    i_col = gc[:, hd:hd + 1]
    b_row = br[MLSTM_HEADS + hd:MLSTM_HEADS + hd + 1, :]
    i_row = gr[hd:hd + 1, :]
    t = lax.broadcasted_iota(jnp.int32, (2 * L, 2 * L), 0)
    s = lax.broadcasted_iota(jnp.int32, (2 * L, 2 * L), 1)
    mask = (t // L == s // L) & (s <= t)
    d = jnp.where(mask, b_col - b_row + i_row, NEG_INF)
    row = lax.broadcasted_iota(jnp.int32, (2 * L, 1), 0)
    inter = b_col + jnp.where(row < L, m_prev[0], m_prev[1])
    m_t = jnp.maximum(inter, jnp.max(d, axis=-1, keepdims=True))
    w_intra = jnp.exp(d - m_t)
    w_inter = jnp.exp(inter - m_t)
    sc = _dot(qb, kb, 1, 1) * w_intra
    qc = jnp.concatenate([_dot(qb[:L], c_prev[0].astype(BF16), 1, 0),
                          _dot(qb[L:], c_prev[1].astype(BF16), 1, 0)], axis=0)
    qn = jnp.concatenate([jnp.sum(q[:L] * n_prev[0], axis=-1, keepdims=True),
                          jnp.sum(q[L:] * n_prev[1], axis=-1, keepdims=True)], axis=0)
    num = _dot(sc.astype(BF16), vb, 1, 0) + w_inter * qc
    den = jnp.sum(sc, axis=-1, keepdims=True) + w_inter * qn
    floor = jnp.exp(-m_t)
    return dict(q=q, ks=ks, v=v, qb=qb, kb=kb, vb=vb, b_col=b_col, i_col=i_col, w_intra=w_intra,
                w_inter=w_inter, sc=sc, qc=qc, qn=qn, num=num, den=den, floor=floor)


def _mlstm_update(f, ch, c, n, m):
    L = MLSTM_CHUNK
    rows = slice(L * ch, L * ch + L)
    b_col = f["b_col"][rows]
    g_last = b_col[L - 1:L]
    a_col = g_last - b_col + f["i_col"][rows]
    m_new = jnp.maximum(g_last + m, jnp.max(a_col, axis=0, keepdims=True))
    decay = jnp.exp(g_last + m - m_new)
    e_a = jnp.exp(a_col - m_new)
    kw = f["ks"][rows] * e_a
    c_new = decay * c + _dot(kw.astype(BF16), f["vb"][rows], 0, 0)
    n_new = decay * n + _colsum(kw)
    return c_new, n_new, m_new, decay, e_a, kw


def _mlstm_specs(T, order):
    blk = lambda w, col: pl.BlockSpec((STEP_ROWS, w), lambda s: (order(s), col))
    return [blk(512, 0), blk(512, 1), blk(512, 2), blk(128, 0),
            pl.BlockSpec((1, 128), lambda s: (0, 0)),
            pl.BlockSpec((8, STEP_ROWS), lambda s: (0, order(s))),
            pl.BlockSpec((8, 128), lambda s: (0, 0))]


def _mlstm_fwd(qk, pm, gcol, bcol, grow, brow):
    T = qk.shape[0]
    steps = T // STEP_ROWS
    H = MLSTM_HEADS

    def body(q_ref, k_ref, v_ref, gc_ref, bc_ref, gr_ref, br_ref, h_ref, cs_ref, nm_ref, c_st, nm_st):
        @pl.when(pl.program_id(0) == 0)
        def _():
            c_st[...] = jnp.zeros_like(c_st)
            nm_st[...] = jnp.zeros_like(nm_st)

        gc, bc, gr, br = _mlstm_gates(gc_ref, bc_ref, gr_ref, br_ref)
        for hd in range(H):
            c0, n0, m0 = c_st[hd], nm_st[hd:hd + 1, :], nm_st[H + hd:H + hd + 1, 0:1]
            D = MLSTM_HEAD_DIM
            pre = dict(ks=k_ref[:, D * hd:D * hd + D] * (D ** -0.5),
                       vb=v_ref[:, D * hd:D * hd + D].astype(BF16),
                       b_col=bc[:, H + hd:H + hd + 1], i_col=gc[:, hd:hd + 1])
            c1, n1, m1, _, _, _ = _mlstm_update(pre, 0, c0, n0, m0)
            c2, n2, m2, _, _, _ = _mlstm_update(pre, 1, c1, n1, m1)
            f = _mlstm_head(hd, q_ref, k_ref, v_ref, gc, bc, gr, br, (c0, c1), (n0, n1), (m0, m1))
            h_ref[:, D * hd:D * hd + D] = f["num"] / jnp.maximum(jnp.abs(f["den"]), f["floor"])
            cs_ref[0, hd], cs_ref[1, hd] = c0, c1
            nm_ref[0, hd:hd + 1, :], nm_ref[1, hd:hd + 1, :] = n0, n1
            nm_ref[0, H + hd:H + hd + 1, :] = jnp.broadcast_to(m0, (1, 128))
            nm_ref[1, H + hd:H + hd + 1, :] = jnp.broadcast_to(m1, (1, 128))
            c_st[hd] = c2
            nm_st[hd:hd + 1, :] = n2
            nm_st[H + hd:H + hd + 1, :] = jnp.broadcast_to(m2, (1, 128))

    return pl.pallas_call(
        body, name="mlstm_fwd", grid=(steps,), in_specs=_mlstm_specs(T, lambda s: s),
        out_specs=[pl.BlockSpec((STEP_ROWS, 512), lambda s: (s, 0)),
                   pl.BlockSpec((2, H, 128, 128), lambda s: (s, 0, 0, 0)),
                   pl.BlockSpec((2, 8, 128), lambda s: (s, 0, 0))],
        out_shape=[_sds((T, 512), F32), _sds((2 * steps, H, 128, 128), F32),
                   _sds((2 * steps, 8, 128), F32)],
        scratch_shapes=[pltpu.VMEM((H, 128, 128), F32), pltpu.VMEM((8, 128), F32)],
        compiler_params=_params(("arbitrary",)))(qk, qk, pm, gcol, bcol, grow, brow)


def _mlstm_bwd(qk, pm, gcol, bcol, grow, brow, cs, nm, dh):
    T = qk.shape[0]
    steps = T // STEP_ROWS
    H, L, D = MLSTM_HEADS, MLSTM_CHUNK, MLSTM_HEAD_DIM
    rev = lambda s: steps - 1 - s

    def body(q_ref, k_ref, v_ref, gc_ref, bc_ref, gr_ref, br_ref, cs_ref, nm_ref, dh_ref,
             dqk_ref, dv_ref, dgc_ref, dgr_ref, dc_st, dn_st):
        @pl.when(pl.program_id(0) == 0)
        def _():
            dc_st[...] = jnp.zeros_like(dc_st)
            dn_st[...] = jnp.zeros_like(dn_st)

        gc, bc, gr, br = _mlstm_gates(gc_ref, bc_ref, gr_ref, br_ref)
        lane = lax.broadcasted_iota(jnp.int32, (STEP_ROWS, 128), 1)
        sub = lax.broadcasted_iota(jnp.int32, (8, STEP_ROWS), 0)
        row = lax.broadcasted_iota(jnp.int32, (STEP_ROWS, 1), 0)
        dgc = jnp.zeros((STEP_ROWS, 128), F32)
        dgr = jnp.zeros((8, STEP_ROWS), F32)
        for hd in range(H):
            c_prev = (cs_ref[0, hd], cs_ref[1, hd])
            n_prev = (nm_ref[0, hd:hd + 1, :], nm_ref[1, hd:hd + 1, :])
            m_prev = (nm_ref[0, H + hd:H + hd + 1, 0:1], nm_ref[1, H + hd:H + hd + 1, 0:1])
            f = _mlstm_head(hd, q_ref, k_ref, v_ref, gc, bc, gr, br, c_prev, n_prev, m_prev)
            dh_h = dh_ref[:, D * hd:D * hd + D]
            big = jnp.abs(f["den"]) > f["floor"]
            rden = 1.0 / jnp.where(big, jnp.abs(f["den"]), f["floor"])
            dnum = dh_h * rden
            hdh = jnp.sum(f["num"] * dnum, axis=-1, keepdims=True)
            dden = jnp.where(big, -hdh * rden * jnp.sign(f["den"]), 0.0)
            dnum_b = dnum.astype(BF16)
            dsc = _dot(dnum_b, f["vb"], 1, 1) + dden
            g = dsc * f["sc"]
            dv = _dot(f["sc"].astype(BF16), dnum_b, 0, 0)
            dqk_ = (dsc * f["w_intra"]).astype(BF16)
            dq = _dot(dqk_, f["kb"], 1, 0)
            dks = _dot(dqk_, f["qb"], 0, 0)
            wdn = f["w_inter"] * dnum
            wdn_b = wdn.astype(BF16)
            u = jnp.sum(f["qc"] * wdn, axis=-1, keepdims=True) + f["w_inter"] * dden * f["qn"]
            dq_i, dks_s, dv_s, z_all, dg_rows = [], [None, None], [None, None], [None, None], [None, None]
            dcn, dnn = dc_st[hd], dn_st[hd:hd + 1, :]
            for ch in (1, 0):
                rows = slice(L * ch, L * ch + L)
                _, _, _, decay, e_a, kw = _mlstm_update(f, ch, c_prev[ch], n_prev[ch], m_prev[ch])
                dcv = _dot(f["vb"][rows], dcn.astype(BF16), 1, 1)
                dkw = dcv + dnn
                dks_s[ch] = e_a * dkw
                dv_s[ch] = _dot(kw.astype(BF16), dcn.astype(BF16), 1, 0)
                z = e_a * jnp.sum(f["ks"][rows] * dkw, axis=-1, keepdims=True)
                dg = jnp.sum(z) + jnp.sum(decay) * (jnp.sum(c_prev[ch] * dcn) + jnp.sum(n_prev[ch] * dnn))
                z_all[ch], dg_rows[ch] = z, dg
                wd = (f["w_inter"] * dden)[rows]
                dcn = decay * dcn + _dot(f["qb"][rows], wdn_b[rows], 0, 0)
                dnn = decay * dnn + _colsum(wd * f["q"][rows])
            dc_st[hd] = dcn
            dn_st[hd:hd + 1, :] = dnn
            dq_int = jnp.concatenate(
                [_dot(wdn_b[:L], c_prev[0].astype(BF16), 1, 1) + (f["w_inter"] * dden)[:L] * n_prev[0],
                 _dot(wdn_b[L:], c_prev[1].astype(BF16), 1, 1) + (f["w_inter"] * dden)[L:] * n_prev[1]],
                axis=0)
            dq = dq + dq_int
            dks = dks + jnp.concatenate(dks_s, axis=0)
            dv = dv + jnp.concatenate(dv_s, axis=0)
            z = jnp.concatenate(z_all, axis=0)
            dg_col = jnp.where(row == L - 1, dg_rows[0], 0.0) + jnp.where(row == 2 * L - 1, dg_rows[1], 0.0)
            db_col = jnp.sum(g, axis=-1, keepdims=True) + u - z + dg_col
            g_row = jnp.sum(g, axis=0, keepdims=True)
            dgc = dgc + jnp.where(lane == hd, z, 0.0) + jnp.where(lane == H + hd, db_col, 0.0)
            dgr = dgr + jnp.where(sub == hd, g_row, 0.0) - jnp.where(sub == H + hd, g_row, 0.0)
            dqk_ref[:, D * hd:D * hd + D] = dq
            dqk_ref[:, H * D + D * hd:H * D + D * hd + D] = dks * (D ** -0.5)
            dv_ref[:, D * hd:D * hd + D] = dv.astype(BF16)
        dgc_ref[...] = dgc
        dgr_ref[...] = dgr

    return pl.pallas_call(
        body, name="mlstm_bwd", grid=(steps,),
        in_specs=_mlstm_specs(T, rev) + [
            pl.BlockSpec((2, H, 128, 128), lambda s: (rev(s), 0, 0, 0)),
            pl.BlockSpec((2, 8, 128), lambda s: (rev(s), 0, 0)),
            pl.BlockSpec((STEP_ROWS, 512), lambda s: (rev(s), 0))],
        out_specs=[pl.BlockSpec((STEP_ROWS, 1024), lambda s: (rev(s), 0)),
                   pl.BlockSpec((STEP_ROWS, 512), lambda s: (rev(s), 0)),
                   pl.BlockSpec((STEP_ROWS, 128), lambda s: (rev(s), 0)),
                   pl.BlockSpec((8, STEP_ROWS), lambda s: (0, rev(s)))],
        out_shape=[_sds((T, 1024), F32), _sds((T, 512), BF16), _sds((T, 128), F32), _sds((8, T), F32)],
        scratch_shapes=[pltpu.VMEM((H, 128, 128), F32), pltpu.VMEM((8, 128), F32)],
        compiler_params=_params(("arbitrary",)))(qk, qk, pm, gcol, bcol, grow, brow, cs, nm, dh)


def _gate_bwd(dgc, dgr_t, gcol, bcol):
    T = dgc.shape[0]

    def body(a_ref, b_ref, g_ref, bias_ref, o_ref, acc_ref):
        @pl.when(pl.program_id(0) == 0)
        def _():
            acc_ref[...] = jnp.zeros_like(acc_ref)

        d = a_ref[...] + b_ref[...]
        lane = lax.broadcasted_iota(jnp.int32, d.shape, 1)
        is_f = (lane >= MLSTM_HEADS) & (lane < 2 * MLSTM_HEADS)
        dlogf = _chunk_rev_cumsum(jnp.where(is_f, d, 0.0), 0)
        out = jnp.where(is_f, dlogf * _sigmoid(-(g_ref[...] + bias_ref[...])), d)
        o_ref[...] = out.astype(BF16)
        acc_ref[0:1, :] += _colsum(out)

    return _rows("gate_bwd", body, [dgc, dgr_t, gcol, bcol],
                 [_sds((T, 128), BF16), _sds((8, 128), F32)], T)


def _head_norm(h, mu_axis=-1):
    mu = jnp.mean(h, axis=-1, keepdims=True)
    hc = h - mu
    r = lax.rsqrt(jnp.mean(hc * hc, axis=-1, keepdims=True) + NORM_EPS)
    return hc * r, r


def _mlstm_out(hm, pm, w):
    T = hm.shape[0]
    D = MLSTM_HEAD_DIM

    def body(h_ref, o_ref, w_ref, y_ref):
        for hd in range(MLSTM_HEADS):
            cols = slice(D * hd, D * hd + D)
            hn, _ = _head_norm(h_ref[:, cols])
            y_ref[:, cols] = (_sigmoid(o_ref[:, cols]) * hn * w_ref[:, cols]).astype(BF16)

    tr = min(ROW_TILE, T)
    return pl.pallas_call(
        body, name="mlstm_out", grid=(T // tr,),
        in_specs=[pl.BlockSpec((tr, 512), lambda i: (i, 0)), pl.BlockSpec((tr, 512), lambda i: (i, 3)),
                  pl.BlockSpec((1, 512), lambda i: (0, 0))],
        out_specs=pl.BlockSpec((tr, 512), lambda i: (i, 0)), out_shape=_sds((T, 512), BF16),
        compiler_params=_params(("parallel",)))(hm, pm, w)


def _mlstm_out_bwd(hm, pm, w, dy):
    T = hm.shape[0]
    D = MLSTM_HEAD_DIM
    tr = min(ROW_TILE, T)

    def body(h_ref, o_ref, w_ref, dy_ref, dh_ref, do_ref, acc_ref):
        @pl.when(pl.program_id(0) == 0)
        def _():
            acc_ref[...] = jnp.zeros_like(acc_ref)

        for hd in range(MLSTM_HEADS):
            cols = slice(D * hd, D * hd + D)
            hn, r = _head_norm(h_ref[:, cols])
            sg = _sigmoid(o_ref[:, cols])
            dy, w = dy_ref[:, cols], w_ref[:, cols]
            do_ref[:, cols] = (dy * hn * w * sg * (1.0 - sg)).astype(BF16)
            dyn = dy * sg
            acc_ref[0:1, cols] += _colsum(dyn * hn)
            dhn = dyn * w
            dh_ref[:, cols] = r * (dhn - jnp.mean(dhn, axis=-1, keepdims=True)
                                   - hn * jnp.mean(dhn * hn, axis=-1, keepdims=True))

    return pl.pallas_call(
        body, name="mlstm_out_bwd", grid=(T // tr,),
        in_specs=[pl.BlockSpec((tr, 512), lambda i: (i, 0)), pl.BlockSpec((tr, 512), lambda i: (i, 3)),
                  pl.BlockSpec((1, 512), lambda i: (0, 0)), pl.BlockSpec((tr, 512), lambda i: (i, 0))],
        out_specs=[pl.BlockSpec((tr, 512), lambda i: (i, 0)), pl.BlockSpec((tr, 512), lambda i: (i, 0)),
                   pl.BlockSpec((8, 512), lambda i: (0, 0))],
        out_shape=[_sds((T, 512), F32), _sds((T, 512), BF16), _sds((8, 512), F32)],
        compiler_params=_params(("arbitrary",)))(hm, pm, w, dy)


def _adamw(name, w, g, m, v, tr=64):
    R, C = w.shape
    tr = min(tr, R)
    c1 = 1.0 - ADAM_B1 ** ADAM_STEP
    c2 = 1.0 - ADAM_B2 ** ADAM_STEP

    def body(w_ref, g_ref, m_ref, v_ref, d_ref, mo_ref, vo_ref):
        g = g_ref[...]
        m = ADAM_B1 * m_ref[...] + (1.0 - ADAM_B1) * g
        v = ADAM_B2 * v_ref[...] + (1.0 - ADAM_B2) * (g * g)
        mo_ref[...] = m
        vo_ref[...] = v
        d_ref[...] = -ADAM_LR * ((m / c1) / (jnp.sqrt(v / c2) + ADAM_EPS) + ADAM_WD * w_ref[...])

    spec = pl.BlockSpec((tr, C), lambda i: (i, 0))
    return pl.pallas_call(
        body, name=name, grid=(R // tr,), in_specs=[spec] * 4, out_specs=[spec] * 3,
        out_shape=[_sds((R, C), F32)] * 3, compiler_params=_params(("parallel",)))(w, g, m, v)


def _place():
    return lax.axis_index("x"), lax.axis_index("y"), lax.axis_index("c")


def _all_gather8(name, blk, space):
    m, n = blk.shape

    def body(x_ref, out_ref, send_sems, recv_sems, local_sem):
        x, y, c = _place()
        me, sibling = (x, y, c), (x, y, 1 - c)
        chips = [(1 - x, y), (x, 1 - y), (1 - x, 1 - y)]

        def rows(px, py, pc):
            return out_ref.at[pl.ds((4 * px + 2 * py + pc) * m, m), :]

        def copy(k, block, to, src=None):
            return pltpu.make_async_remote_copy(
                src_ref=rows(*block) if src is None else src, dst_ref=rows(*block),
                send_sem=send_sems.at[k], recv_sem=recv_sems.at[k],
                device_id=to, device_id_type=MESH)

        mine = pltpu.make_async_copy(x_ref, rows(*me), local_sem)
        mine.start()
        first = [copy(0, me, sibling, src=x_ref)]
        first += [copy(1 + j, me, (*chip, c), src=x_ref) for j, chip in enumerate(chips)]
        for cp in first:
            cp.start()
        passed = [copy(4 + j, (*chip, c), sibling) for j, chip in enumerate(chips)]
        for j, chip in enumerate(chips):
            copy(1 + j, (*chip, c), me).wait_recv()
            passed[j].start()
        copy(0, sibling, me).wait_recv()
        for j, chip in enumerate(chips):
            copy(4 + j, (*chip, 1 - c), me).wait_recv()
        for cp in first + passed:
            cp.wait_send()
        mine.wait()

    return pl.pallas_call(
        body, name=name, out_shape=_sds((8 * m, n), blk.dtype),
        in_specs=[pl.BlockSpec(memory_space=space)], out_specs=pl.BlockSpec(memory_space=space),
        scratch_shapes=[pltpu.SemaphoreType.DMA((7,)), pltpu.SemaphoreType.DMA((7,)),
                        pltpu.SemaphoreType.DMA],
        compiler_params=pltpu.CompilerParams(vmem_limit_bytes=VMEM_LIMIT))(blk)


def _hbm_specs(n):
    return [pl.BlockSpec(memory_space=pl.ANY)] * n


def _gather_groups(name, blocks):
    nw = len(blocks)

    def body(*refs):
        srcs, outs = refs[:nw], refs[nw:2 * nw]
        send_sems, recv_sems, local_sems = refs[2 * nw:]
        x, y, c = _place()
        me, sibling = (x, y, c), (x, y, 1 - c)
        chips = [(1 - x, y), (x, 1 - y), (1 - x, 1 - y)]

        def slab(w, px, py, pc):
            return outs[w].at[2 * px + py, pl.ds(0, blocks[w].shape[0]), pc]

        def copy(w, k, block, to, src=None):
            return pltpu.make_async_remote_copy(
                src_ref=slab(w, *block) if src is None else src, dst_ref=slab(w, *block),
                send_sem=send_sems.at[w, k], recv_sem=recv_sems.at[w, k],
                device_id=to, device_id_type=MESH)

        mine = [pltpu.make_async_copy(srcs[w], slab(w, *me), local_sems.at[w]) for w in range(nw)]
        first = []
        for w in range(nw):
            mine[w].start()
            first.append(copy(w, 0, me, sibling, src=srcs[w]))
            first += [copy(w, 1 + j, me, (*chip, c), src=srcs[w]) for j, chip in enumerate(chips)]
        for cp in first:
            cp.start()
        passed = []
        for j, chip in enumerate(chips):
            for w in range(nw):
                copy(w, 1 + j, (*chip, c), me).wait_recv()
                passed.append(copy(w, 4 + j, (*chip, c), sibling))
                passed[-1].start()
        for w in range(nw):
            copy(w, 0, sibling, me).wait_recv()
            for j, chip in enumerate(chips):
                copy(w, 4 + j, (*chip, 1 - c), me).wait_recv()
        for cp in first + passed:
            cp.wait_send()
        for w in range(nw):
            mine[w].wait()

    return pl.pallas_call(
        body, name=name,
        out_shape=[_sds((4, b.shape[0], 2) + b.shape[1:], b.dtype) for b in blocks],
        in_specs=_hbm_specs(nw), out_specs=_hbm_specs(nw),
        scratch_shapes=[pltpu.SemaphoreType.DMA((nw, 7)), pltpu.SemaphoreType.DMA((nw, 7)),
                        pltpu.SemaphoreType.DMA((nw,))])(*blocks)


def _swap_sibling(name, srcs, halves=False):
    nw = len(srcs)

    def body(*refs):
        src_refs, dst_refs, send_sems, recv_sems = refs[:nw], refs[nw:2 * nw], refs[2 * nw], refs[2 * nw + 1]
        x, y, c = _place()
        cps = []
        for w in range(nw):
            s = src_refs[w]
            if halves:
                s = s.at[pl.ds(0, srcs[w].shape[0]), pl.ds(0, srcs[w].shape[1]), 1 - c]
            cps.append(pltpu.make_async_remote_copy(
                src_ref=s, dst_ref=dst_refs[w], send_sem=send_sems.at[w], recv_sem=recv_sems.at[w],
                device_id=(x, y, 1 - c), device_id_type=MESH))
        for cp in cps:
            cp.start()
        for cp in cps:
            cp.wait()

    shapes = [(s.shape[:2] + s.shape[3:]) if halves else s.shape for s in srcs]
    return pl.pallas_call(
        body, name=name, out_shape=[_sds(sh, s.dtype) for sh, s in zip(shapes, srcs)],
        in_specs=_hbm_specs(nw), out_specs=_hbm_specs(nw),
        scratch_shapes=[pltpu.SemaphoreType.DMA((nw,)), pltpu.SemaphoreType.DMA((nw,))])(*srcs)


def _swap_chips(name, srcs):
    nw = len(srcs)

    def body(*refs):
        src_refs, dst_refs = refs[:nw], refs[nw:2 * nw]
        send_sems, recv_sems, local_sems = refs[2 * nw:]
        x, y, c = _place()
        mine = 2 * x + y
        chips = [(1 - x, y), (x, 1 - y), (1 - x, 1 - y)]

        def copy(w, j):
            px, py = chips[j]
            return pltpu.make_async_remote_copy(
                src_ref=src_refs[w].at[2 * px + py], dst_ref=dst_refs[w].at[mine],
                send_sem=send_sems.at[w, j], recv_sem=recv_sems.at[w, j],
                device_id=(px, py, c), device_id_type=MESH)

        def landing(w, j):
            px, py = chips[j]
            return pltpu.make_async_remote_copy(
                src_ref=src_refs[w].at[mine], dst_ref=dst_refs[w].at[2 * px + py],
                send_sem=send_sems.at[w, j], recv_sem=recv_sems.at[w, j],
                device_id=(px, py, c), device_id_type=MESH)

        own = [pltpu.make_async_copy(src_refs[w].at[mine], dst_refs[w].at[mine], local_sems.at[w])
               for w in range(nw)]
        for w in range(nw):
            own[w].start()
            for j in range(3):
                copy(w, j).start()
        for w in range(nw):
            for j in range(3):
                landing(w, j).wait_recv()
        for w in range(nw):
            for j in range(3):
                copy(w, j).wait_send()
            own[w].wait()

    return pl.pallas_call(
        body, name=name, out_shape=[_sds(s.shape, s.dtype) for s in srcs],
        in_specs=_hbm_specs(nw), out_specs=_hbm_specs(nw),
        scratch_shapes=[pltpu.SemaphoreType.DMA((nw, 3)), pltpu.SemaphoreType.DMA((nw, 3)),
                        pltpu.SemaphoreType.DMA((nw,))])(*srcs)


def _pair_sum(name, full, got, core):
    _, g, _, m, n = full.shape

    def body(c_ref, a_ref, b_ref, o_ref):
        o_ref[...] = (a_ref[...].astype(F32) + b_ref[...].astype(F32)).astype(o_ref.dtype)

    slab = pl.BlockSpec((None, None, m, n), lambda s, w, c: (s, w, 0, 0))
    return pl.pallas_call(
        body, name=name,
        grid_spec=pltpu.PrefetchScalarGridSpec(
            num_scalar_prefetch=1, grid=(4, g),
            in_specs=[pl.BlockSpec((None, None, None, m, n), lambda s, w, c: (s, w, c[0], 0, 0)), slab],
            out_specs=slab),
        out_shape=_sds(got.shape, BF16),
        compiler_params=_params(("parallel", "parallel")))(core, full, got)


def _sum4(name, a):
    _, g, m, n = a.shape

    def body(a_ref, o_ref):
        acc = a_ref[0].astype(F32)
        for s in range(1, 4):
            acc = acc + a_ref[s].astype(F32)
        o_ref[...] = acc

    return pl.pallas_call(body, name=name, grid=(g,),
                          in_specs=[pl.BlockSpec((4, None, m, n), lambda w: (0, w, 0, 0))],
                          out_specs=pl.BlockSpec((None, m, n), lambda w: (w, 0, 0)),
                          out_shape=_sds((g, m, n), F32), compiler_params=_params(("parallel",)))(a)


def _small_update(gathered, w, m, v):
    n = w.shape[1]
    tn = 2048
    c1 = 1.0 - ADAM_B1 ** ADAM_STEP
    c2 = 1.0 - ADAM_B2 ** ADAM_STEP

    def body(g_ref, w_ref, m_ref, v_ref, go_ref, d_ref, mo_ref, vo_ref):
        g = g_ref[0:1, :]
        for d in range(1, 8):
            g = g + g_ref[d:d + 1, :]
        go_ref[...] = g
        m = ADAM_B1 * m_ref[...] + (1.0 - ADAM_B1) * g
        v = ADAM_B2 * v_ref[...] + (1.0 - ADAM_B2) * (g * g)
        mo_ref[...] = m
        vo_ref[...] = v
        d_ref[...] = -ADAM_LR * ((m / c1) / (jnp.sqrt(v / c2) + ADAM_EPS) + ADAM_WD * w_ref[...])

    row = pl.BlockSpec((1, tn), lambda i: (0, i))
    return pl.pallas_call(
        body, name="small_update", grid=(n // tn,),
        in_specs=[pl.BlockSpec((8, tn), lambda i: (0, i)), row, row, row], out_specs=[row] * 4,
        out_shape=[_sds((1, n), F32)] * 4, compiler_params=_params(("parallel",)))(gathered, w, m, v)


def _swiglu(ps, es):
    g, u = ps
    return g * _sigmoid(g) * u, g, u


def _swiglu_bwd(ps, es):
    g, u = es[0].astype(F32), es[1].astype(F32)
    sg = _sigmoid(g)
    return ps[0] * u * (sg * (1.0 + g * (1.0 - sg))), ps[0] * (g * sg)


def _merge(ps, es):
    return _sigmoid(es[0]) * ps[0] + _sigmoid(es[1]) * ps[1], ps[0], ps[1]


def _merge_bwd(ps, es):
    a, b, ga, gm = es
    sa, sm = _sigmoid(ga), _sigmoid(gm)
    dm = ps[0]
    return dm * sa, dm * sm, dm * a * (sa * (1.0 - sa)), dm * b * (sm * (1.0 - sm))


W_IN_PIECES = (("q", 512), ("kv", 256), ("mqk", 1024), ("mv", 512), ("mo", 512), ("if", 8),
               ("ga", 1024), ("gm", 1024))


def _local_step(x, tgt, pos_col, mod, sp, W):
    sh_m, sc_m, gate_m, sh_f, sc_f, gate_f = mod
    w_a = jnp.concatenate([W["q"], W["kv"]], axis=0)
    w_m = jnp.concatenate([W["mqk"], W["mv"], W["mo"]], axis=0)
    w_g = jnp.concatenate([W["ga"], W["gm"]], axis=0)

    h = _pre_norm(x, sp["g_pre_mix"], sc_m, sh_m)
    pa, = _mm("proj_attn", [[(h, w_a)]], [], _first, [F32], cn=256, nt=True)
    pm, = _mm("proj_mlstm", [[(h, w_m)]], [], _first, [F32], cn=512, nt=True)
    pif, = _mm("proj_gates", [[(h, W["if"])]], [], _first, [F32], cn=128, nt=True)
    pg, = _mm("proj_branch_gates", [[(h, w_g)]], [], _first, [F32], cn=512, nt=True)
    inv = ROPE_THETA ** (-2.0 * jnp.arange(HEAD_DIM // 2, dtype=F32) / HEAD_DIM)
    cos, sin = _rope_tables(pos_col, jnp.tile(inv, 4).reshape(1, 128))
    ya = _attn_fwd(pa, cos, sin, sp["sinks"])
    qk = _conv_fwd(pm, sp["conv_w"], sp["conv_b"])
    bcol = jnp.pad(sp["b_if"], ((0, 0), (0, 120)))
    brow = jnp.broadcast_to(sp["b_if"].reshape(8, 1), (8, 128))
    grow = pif[:, :8].T
    hm, cs, nm = _mlstm_fwd(qk, pm, pif, bcol, grow, brow)
    ym = _mlstm_out(hm, pm, sp["norm_w"])
    merged, br_a, br_m = _mm("branches", [[(ya, W["ba"])], [(ym, W["bm"])]],
                             [(pg, 0), (pg, 1)], _merge, [BF16, F32, F32], cn=512, nt=True)
    mix, = _mm("mix_out", [[(merged, W["out"])]], [], _first, [F32], cn=512)
    x1, h2 = _res_norm(x, mix, gate_m, sp["g_post_mix"], sp["g_pre_ffn"], sc_f, sh_f)
    act, gt, up = _mm("ffn_in", [[(h2, W["fg"])], [(h2, W["fu"])]], [], _swiglu, [BF16] * 3,
                      cn=256, nt=True)
    ff, = _mm("ffn_down", [[(act, W["fd"])]], [], _first, [F32], cn=512)
    dy, dff, acc_l, loss = _final_loss(x1, ff, tgt, gate_f, sp["g_post_ffn"])

    G = {}
    dgt, dup = _mm("ffn_down_bwd", [[(dff, W["fd"])]], [gt, up], _swiglu_bwd, [BF16, BF16],
                   cn=256, nt=True)
    G["fd"] = _mm_tn("dw_ffn_down", act, dff, BF16, 1408, 512, 512)
    dh2, = _mm("ffn_in_bwd", [[(dgt, W["fg"]), (dup, W["fu"])]], [], _first, [F32], cn=512)
    G["fg"] = _mm_tn("dw_ffn_gate", dgt, h2, BF16, 1408, 1024, 512)
    G["fu"] = _mm_tn("dw_ffn_up", dup, h2, BF16, 1408, 1024, 512)
    dx1, dmix, acc_r = _res_norm_bwd(x1, x, mix, dh2, dy, sc_f, gate_m, sp["g_pre_ffn"],
                                     sp["g_post_mix"])
    d_a, d_m, dga, dgm = _mm("mix_out_bwd", [[(dmix, W["out"])]],
                             [br_a, br_m, (pg, 0), (pg, 1)], _merge_bwd,
                             [BF16] * 4, cn=512, nt=True)
    G["out"] = _mm_tn("dw_out", merged, dmix, BF16, 1024, 512, 512)
    dya, = _mm("branch_attn_bwd", [[(d_a, W["ba"])]], [], _first, [F32], cn=512)
    dym, = _mm("branch_mlstm_bwd", [[(d_m, W["bm"])]], [], _first, [F32], cn=512)
    G["ba"] = _mm_tn("dw_branch_attn", d_a, ya, BF16, 1024, 512, 512)
    G["bm"] = _mm_tn("dw_branch_mlstm", d_m, ym, BF16, 1024, 512, 512)
    dhm, do_m, acc_n = _mlstm_out_bwd(hm, pm, sp["norm_w"], dym)
    dqk, dv_m, dgc, dgr = _mlstm_bwd(qk, pm, pif, bcol, grow, brow, cs, nm, dhm)
    dif, acc_g = _gate_bwd(dgc, jnp.pad(dgr.T, ((0, 0), (0, 120))), pif, bcol)
    dpre, acc_c = _conv_bwd_pre(pm, sp["conv_w"], sp["conv_b"], dqk)
    du = _conv_bwd_in(dpre, sp["conv_w"])
    dq_a, dcur, dprv, dsink = _attn_bwd(pa, cos, sin, sp["sinks"], dya)
    dkv = _attn_kv_combine(dcur, dprv, cos, sin)
    dproj = {"q": dq_a, "kv": dkv, "mqk": du, "mv": dv_m, "mo": do_m, "if": dif, "ga": dga, "gm": dgm}
    dh, = _mm("proj_bwd", [[(dproj[k], W[k]) for k, _ in W_IN_PIECES]], [], _first, [F32], cn=512)
    for k, _ in W_IN_PIECES:
        G[k] = _mm_tn("dw_in_" + k, dproj[k], h, BF16, dproj[k].shape[1], 1024, 512)
    dx, acc_p = _pre_norm_bwd(x, dh, dx1, sp["g_pre_mix"], sc_m)

    small = {
        "mod": jnp.concatenate([acc_p[1], acc_p[0], acc_r[3], acc_r[1], acc_r[0], acc_l[0]]),
        "g_pre_mix": acc_p[2], "g_post_mix": acc_r[4], "b_if": acc_g[0, :8],
        "conv_w": acc_c[:CONV_WIDTH].reshape(-1), "conv_b": acc_c[CONV_WIDTH],
        "sinks": dsink[:, 0], "norm_w": acc_n[0], "g_pre_ffn": acc_r[2], "g_post_ffn": acc_l[1]}
    return loss, dx, G, small


IN_WIDTH = sum(n for _, n in W_IN_PIECES)
IN_SHARD = IN_WIDTH // 4
IN_SHARD_PAD = -(-IN_SHARD // 32) * 32


def _split_w_in(w_in_t):
    out, off = {}, 0
    for k, n in W_IN_PIECES:
        out[k] = w_in_t[off:off + n]
        off += n
    out["if"] = jnp.pad(out["if"], ((0, 120), (0, 0)))
    return out


def _halves(a):
    return a.reshape(4, 2, a.shape[0] // 8, a.shape[1])


SMALL = (("b_ada", 6144), ("g_pre_mix", 1024), ("g_post_mix", 1024), ("b_if", 128), ("conv_w", 4096),
         ("conv_b", 1024), ("sinks", 128), ("norm_w", 512), ("g_pre_ffn", 1024), ("g_post_ffn", 1024))
SMALL_LEN = 8 * 2048


def _pack_small(vals):
    parts = []
    for k, n in SMALL:
        v = vals[k].reshape(-1)
        parts.append(jnp.pad(v, (0, n - v.shape[0])))
    flat = jnp.concatenate(parts)
    return jnp.pad(flat, (0, SMALL_LEN - flat.shape[0]))


def _unpack_small(flat, shapes):
    out, off = {}, 0
    for k, n in SMALL:
        size = 1
        for d in shapes[k]:
            size *= d
        out[k] = flat[off:off + size].reshape(shapes[k])
        off += n
    return out


def kernel(x, c, positions, w_ada, b_ada, g_pre_mix, g_post_mix, w_in, b_if, conv_w, conv_b, attn_sinks, mlstm_norm_w, w_branch_attn, w_branch_mlstm, w_out, g_pre_ffn, g_post_ffn, w_ffn_gate, w_ffn_up, w_ffn_down, loss_target, m_w_ada, m_b_ada, m_g_pre_mix, m_g_post_mix, m_w_in, m_b_if, m_conv_w, m_conv_b, m_attn_sinks, m_mlstm_norm_w, m_w_branch_attn, m_w_branch_mlstm, m_w_out, m_g_pre_ffn, m_g_post_ffn, m_w_ffn_gate, m_w_ffn_up, m_w_ffn_down, v_w_ada, v_b_ada, v_g_pre_mix, v_g_post_mix, v_w_in, v_b_if, v_conv_w, v_conv_b, v_attn_sinks, v_mlstm_norm_w, v_w_branch_attn, v_w_branch_mlstm, v_w_out, v_g_pre_ffn, v_g_post_ffn, v_w_ffn_gate, v_w_ffn_up, v_w_ffn_down):
    xi, yi, ci = _place()
    chip = 2 * xi + yi
    dev = 2 * chip + ci
    T = x.shape[1]
    ada_cols = w_ada.shape[2]

    blk = jnp.concatenate([c.reshape(-1), conv_w.reshape(-1)]).reshape(8, 256)
    got = _all_gather8("gather_cond", blk, pltpu.VMEM).reshape(8, 2048)
    c_all = got[:, :D_MODEL].astype(BF16)
    conv_full = got[::2, D_MODEL:].reshape(4, CONV_WIDTH, -1).transpose(1, 0, 2).reshape(CONV_WIDTH, -1)

    b_sh = lax.dynamic_slice_in_dim(b_ada, chip * ada_cols, ada_cols, axis=1)
    mod_part, = _mm("ada_mod", [[(c_all, w_ada[0].astype(BF16))]], [b_sh],
                    lambda ps, es: (ps[0] + es[0],), [F32], cn=512, tm=8)
    mod_all = _all_gather8("gather_mod", mod_part, pltpu.VMEM).reshape(4, 2, 8, ada_cols)[:, 0]
    mod = lax.dynamic_index_in_dim(mod_all, dev, axis=1, keepdims=False).reshape(6, 1, D_MODEL)

    def my_half(a):
        n = a.shape[0] // 2
        return lax.dynamic_slice_in_dim(a, ci * n, n, axis=0).astype(BF16)

    w_in_t = jnp.pad(w_in[0].T, ((0, IN_SHARD_PAD - IN_SHARD), (0, 0)))
    blocks = [jnp.stack([my_half(w_ffn_gate[0].T), my_half(w_ffn_up[0].T), my_half(w_ffn_down[0])]),
              my_half(w_in_t)[None], my_half(w_out[0])[None],
              jnp.stack([my_half(w_branch_attn[0].T), my_half(w_branch_mlstm[0].T)])]
    g_ffn, g_in, g_out, g_br = _gather_groups("gather_weights", blocks)
    W = {"fg": g_ffn[:, 0].reshape(D_FF, D_MODEL), "fu": g_ffn[:, 1].reshape(D_FF, D_MODEL),
         "fd": g_ffn[:, 2].reshape(D_FF, D_MODEL), "out": g_out.reshape(D_MODEL, D_MODEL),
         "ba": g_br[:, 0].reshape(D_MODEL, -1), "bm": g_br[:, 1].reshape(D_MODEL, -1)}
    W.update(_split_w_in(g_in.reshape(4, IN_SHARD_PAD, D_MODEL)[:, :IN_SHARD].reshape(IN_WIDTH, D_MODEL)))

    sp = {"g_pre_mix": g_pre_mix, "g_post_mix": g_post_mix, "b_if": b_if, "conv_w": conv_full,
          "conv_b": conv_b, "sinks": attn_sinks, "norm_w": mlstm_norm_w, "g_pre_ffn": g_pre_ffn,
          "g_post_ffn": g_post_ffn}
    loss, dx, G, small = _local_step(x[0], loss_target[0], positions.reshape(T, 1),
                                     [mod[i] for i in range(6)], sp, W)

    g_in_t = jnp.concatenate([G[k][:n] for k, n in W_IN_PIECES]).reshape(4, IN_SHARD, D_MODEL)
    g_in_t = jnp.pad(g_in_t, ((0, 0), (0, IN_SHARD_PAD - IN_SHARD), (0, 0)))
    groups = [jnp.stack([_halves(G["fg"]), _halves(G["fu"]), _halves(G["fd"])], axis=1),
              g_in_t.reshape(4, 1, 2, IN_SHARD_PAD // 2, D_MODEL), _halves(G["out"])[:, None],
              jnp.stack([_halves(G["ba"]), _halves(G["bm"])], axis=1)]
    core = ci.reshape(1).astype(jnp.int32)
    theirs = _swap_sibling("rs_pair", groups, halves=True)
    pairs = [_pair_sum("rs_pair_sum_%d" % i, a, b, core) for i, (a, b) in enumerate(zip(groups, theirs))]
    reds = [_sum4("rs_chip_sum_%d" % i, a) for i, a in enumerate(_swap_chips("rs_chips", pairs))]
    others = _swap_sibling("rs_share", reds)
    s_ffn, s_in, s_out, s_br = [
        jnp.concatenate([jnp.where(ci == 0, r, o), jnp.where(ci == 0, o, r)], axis=1)
        for r, o in zip(reds, others)]
    gsh = {"fg": s_ffn[0].T, "fu": s_ffn[1].T, "fd": s_ffn[2], "w_in": s_in[0, :IN_SHARD].T,
           "out": s_out[0], "ba": s_br[0].T, "bm": s_br[1].T}

    small["b_ada"] = small.pop("mod")
    vec = _pack_small(small).reshape(8, 2048)
    g_all = _all_gather8("gather_small", vec, pltpu.VMEM).reshape(8, SMALL_LEN)
    dmod_sh = lax.dynamic_slice_in_dim(g_all[:, :6 * D_MODEL], chip * ada_cols, ada_cols, axis=1)
    g_w_ada = _mm_tn("dw_ada", c_all, dmod_sh.astype(BF16), F32, D_MODEL, 512, 8)

    smalls = {"b_ada": (b_ada, m_b_ada, v_b_ada), "g_pre_mix": (g_pre_mix, m_g_pre_mix, v_g_pre_mix),
              "g_post_mix": (g_post_mix, m_g_post_mix, v_g_post_mix), "b_if": (b_if, m_b_if, v_b_if),
              "conv_w": None, "conv_b": (conv_b, m_conv_b, v_conv_b),
              "sinks": (attn_sinks, m_attn_sinks, v_attn_sinks),
              "norm_w": (mlstm_norm_w, m_mlstm_norm_w, v_mlstm_norm_w),
              "g_pre_ffn": (g_pre_ffn, m_g_pre_ffn, v_g_pre_ffn),
              "g_post_ffn": (g_post_ffn, m_g_post_ffn, v_g_post_ffn)}
    shapes = {k: (t[0].shape if t is not None else (1, CONV_WIDTH, D_MODEL)) for k, t in smalls.items()}
    zeros = jnp.zeros((CONV_WIDTH * D_MODEL,), F32)
    packs = [_pack_small({k: (t[i] if t is not None else zeros) for k, t in smalls.items()}).reshape(1, -1)
             for i in range(3)]
    s_out = [_unpack_small(o[0], shapes) for o in _small_update(g_all, *packs)]
    g_conv = lax.dynamic_slice_in_dim(s_out[0]["conv_w"], chip * conv_w.shape[2], conv_w.shape[2], axis=2)

    res = {}
    for k, t in smalls.items():
        if t is not None:
            res[k] = tuple(o[k] for o in s_out)
    res["conv_w"] = (g_conv, *[o[None] for o in _adamw("adam_conv_w", conv_w[0], g_conv[0], m_conv_w[0], v_conv_w[0])])
    res["w_ada"] = (g_w_ada[None], *[o[None] for o in _adamw("adam_w_ada", w_ada[0], g_w_ada, m_w_ada[0], v_w_ada[0])])
    bigs = {"w_in": (w_in, m_w_in, v_w_in), "ba": (w_branch_attn, m_w_branch_attn, v_w_branch_attn),
            "bm": (w_branch_mlstm, m_w_branch_mlstm, v_w_branch_mlstm), "out": (w_out, m_w_out, v_w_out),
            "fg": (w_ffn_gate, m_w_ffn_gate, v_w_ffn_gate), "fu": (w_ffn_up, m_w_ffn_up, v_w_ffn_up),
            "fd": (w_ffn_down, m_w_ffn_down, v_w_ffn_down)}
    for k, (w, m, v) in bigs.items():
        res[k] = (gsh[k][None], *[o[None] for o in _adamw("adam_" + k, w[0], gsh[k], m[0], v[0])])

    order = ("w_ada", "b_ada", "g_pre_mix", "g_post_mix", "w_in", "b_if", "conv_w", "conv_b", "sinks",
             "norm_w", "ba", "bm", "out", "g_pre_ffn", "g_post_ffn", "fg", "fu", "fd")
    total = lax.psum(loss[0, 0], ("x", "y", "c"))
    return (total, dx[None], *[res[k][0] for k in order], *[res[k][1] for k in order],
            *[res[k][2] for k in order], *[res[k][3] for k in order])
```

```python
import functools

import jax
import jax.numpy as jnp
from jax import lax
from jax.experimental import pallas as pl
from jax.experimental.pallas import tpu as pltpu

F32, BF16 = jnp.float32, jnp.bfloat16
MESH = pl.DeviceIdType.MESH

D_MODEL = 1024
N_Q_HEADS, N_KV_HEADS, HEAD_DIM, WINDOW = 8, 2, 64, 128
ROPE_THETA = 10000.0
MLSTM_HEADS, MLSTM_HEAD_DIM, MLSTM_CHUNK, CONV_WIDTH = 4, 128, 64, 4
D_FF = 2816
NORM_EPS = 1e-6
ADAM_LR, ADAM_B1, ADAM_B2, ADAM_EPS, ADAM_WD, ADAM_STEP = 0.001, 0.9, 0.999, 1e-08, 0.01, 10

VMEM_LIMIT = 56 * 1024 * 1024
ROW_TILE = 256
MM_TM = 512
ATTN_BLK = WINDOW
STEP_ROWS = 2 * MLSTM_CHUNK
NEG_INF = float("-inf")


def _params(sem):
    return pltpu.CompilerParams(dimension_semantics=sem, vmem_limit_bytes=VMEM_LIMIT)


def _sds(shape, dtype):
    return jax.ShapeDtypeStruct(shape, dtype)


def _sigmoid(x):
    return 1.0 / (1.0 + jnp.exp(-x))


def _dot(a, b, ca, cb):
    return lax.dot_general(a, b, (((ca,), (cb,)), ((), ())), preferred_element_type=F32)


def _bdot(a, b, ca, cb):
    return lax.dot_general(a, b, (((ca,), (cb,)), ((0,), (0,))), preferred_element_type=F32)


def _bdot_rows(a, b):
    return jnp.stack([_dot(a[h], b[h], 0, 0) for h in range(a.shape[0])])


def _mm(name, prods, extras, epi, out_dtypes, cn, nt=False, tm=MM_TM):
    flat = [ab for p in prods for ab in p]
    counts = [len(p) for p in prods]
    M = flat[0][0].shape[0]
    N = flat[0][1].shape[0 if nt else 1]
    tm = min(tm, M)
    n_in = 2 * len(flat) + len(extras)

    def body(*refs):
        ins, outs = refs[:n_in], refs[n_in:]
        for j in range(N // cn):
            cols = slice(j * cn, (j + 1) * cn)
            k, ps = 0, []
            for cnt in counts:
                acc = None
                for _ in range(cnt):
                    b = ins[k + 1][cols, :] if nt else ins[k + 1][:, cols]
                    d = _dot(ins[k][...], b, 1, 1 if nt else 0)
                    acc = d if acc is None else acc + d
                    k += 2
                ps.append(acc)
            res = epi(ps, [r[:, cols] for r in ins[k:]])
            for o, r in zip(outs, res):
                o[:, cols] = r.astype(o.dtype)

    in_specs, args = [], []
    for a, b in flat:
        in_specs.append(pl.BlockSpec((tm, a.shape[1]), lambda i: (i, 0)))
        in_specs.append(pl.BlockSpec(b.shape, lambda i: (0, 0), pipeline_mode=pl.Buffered(1)))
        args += [a, b]
    for e in extras:
        e, off = e if isinstance(e, tuple) else (e, 0)
        rows = 1 if e.shape[0] == 1 else tm
        in_specs.append(pl.BlockSpec((rows, N), lambda i, off=off, rows=rows: (0 if rows == 1 else i, off)))
        args.append(e)
    return pl.pallas_call(
        body, name=name, grid=(M // tm,), in_specs=in_specs,
        out_specs=[pl.BlockSpec((tm, N), lambda i: (i, 0)) for _ in out_dtypes],
        out_shape=[_sds((M, N), dt) for dt in out_dtypes],
        compiler_params=_params(("parallel",)))(*args)


def _mm_tn(name, a, b, out_dtype, tk, tn, tt):
    T, Ka = a.shape
    N = b.shape[1]
    tt = min(tt, T)
    steps = T // tt

    def body(a_ref, b_ref, o_ref, acc):
        t = pl.program_id(2)

        @pl.when(t == 0)
        def _():
            acc[...] = jnp.zeros_like(acc)

        acc[...] += _dot(a_ref[...], b_ref[...], 0, 0)

        @pl.when(t == steps - 1)
        def _():
            o_ref[...] = acc[...].astype(o_ref.dtype)

    return pl.pallas_call(
        body, name=name, grid=(Ka // tk, N // tn, steps),
        in_specs=[pl.BlockSpec((tt, tk), lambda i, j, t: (t, i)),
                  pl.BlockSpec((tt, tn), lambda i, j, t: (t, j))],
        out_specs=pl.BlockSpec((tk, tn), lambda i, j, t: (i, j)),
        out_shape=_sds((Ka, N), out_dtype),
        scratch_shapes=[pltpu.VMEM((tk, tn), F32)],
        compiler_params=_params(("parallel", "parallel", "arbitrary")))(a, b)


def _first(ps, es):
    return (ps[0],)


def _rows(name, body, ins, out_shapes, T, tr=ROW_TILE):
    tr = min(tr, T)

    def spec(shape):
        if shape[0] == T:
            return pl.BlockSpec((tr,) + tuple(shape[1:]), lambda i: (i,) + (0,) * (len(shape) - 1))
        return pl.BlockSpec(tuple(shape), lambda i: (0,) * len(shape))

    return pl.pallas_call(
        body, name=name, grid=(T // tr,),
        in_specs=[spec(a.shape) for a in ins], out_specs=[spec(s.shape) for s in out_shapes],
        out_shape=out_shapes, compiler_params=_params(("arbitrary",)))(*ins)


def _rms(x):
    r = lax.rsqrt(jnp.mean(x * x, axis=-1, keepdims=True) + NORM_EPS)
    return x * r, r


def _rms_bwd(dxn, xn, r):
    return r * (dxn - xn * jnp.mean(dxn * xn, axis=-1, keepdims=True))


def _colsum(v):
    return jnp.sum(v, axis=0, keepdims=True)


def _pre_norm(x, g, sc, sh):
    T = x.shape[0]

    def body(x_ref, g_ref, sc_ref, sh_ref, h_ref):
        xn, _ = _rms(x_ref[...])
        h_ref[...] = (xn * g_ref[...] * (1.0 + sc_ref[...]) + sh_ref[...]).astype(BF16)

    return _rows("pre_norm", body, [x, g, sc, sh], [_sds((T, D_MODEL), BF16)], T)[0]


def _res_norm(x, mix, gate, gpost, g2, sc2, sh2):
    T = x.shape[0]

    def body(x_ref, mix_ref, gate_ref, gp_ref, g2_ref, sc_ref, sh_ref, x1_ref, h2_ref):
        mh, _ = _rms(mix_ref[...])
        x1 = x_ref[...] + gate_ref[...] * (mh * gp_ref[...])
        x1_ref[...] = x1
        xn, _ = _rms(x1)
        h2_ref[...] = (xn * g2_ref[...] * (1.0 + sc_ref[...]) + sh_ref[...]).astype(BF16)

    return _rows("res_norm", body, [x, mix, gate, gpost, g2, sc2, sh2],
                 [_sds((T, D_MODEL), F32), _sds((T, D_MODEL), BF16)], T)


def _final_loss(x1, ff, tgt, gate, gpost):
    T = x1.shape[0]

    def body(x1_ref, ff_ref, t_ref, gate_ref, gp_ref, dy_ref, dff_ref, acc_ref, loss_ref):
        @pl.when(pl.program_id(0) == 0)
        def _():
            acc_ref[...] = jnp.zeros_like(acc_ref)
            loss_ref[...] = jnp.zeros_like(loss_ref)

        fh, r = _rms(ff_ref[...])
        gate, gp = gate_ref[...], gp_ref[...]
        e = x1_ref[...] + gate * (fh * gp) - t_ref[...]
        loss_ref[...] += 0.5 * jnp.sum(jnp.mean(e * e, axis=-1, keepdims=True))
        dy = e * (1.0 / D_MODEL)
        dy_ref[...] = dy
        acc_ref[0:1, :] += _colsum(dy * fh * gp)
        acc_ref[1:2, :] += _colsum(dy * gate * fh)
        dff_ref[...] = _rms_bwd(dy * gate * gp, fh, r).astype(BF16)

    return _rows("final_loss", body, [x1, ff, tgt, gate, gpost],
                 [_sds((T, D_MODEL), F32), _sds((T, D_MODEL), BF16),
                  _sds((8, D_MODEL), F32), _sds((1, 128), F32)], T)


def _res_norm_bwd(x1, mix, dh2, dy, sc2, gate, g2, gpost):
    T = x1.shape[0]

    def body(x1_ref, mix_ref, dh_ref, dy_ref, sc_ref, gate_ref, g2_ref, gp_ref,
             dx1_ref, dmix_ref, acc_ref):
        @pl.when(pl.program_id(0) == 0)
        def _():
            acc_ref[...] = jnp.zeros_like(acc_ref)

        xn, r1 = _rms(x1_ref[...])
        dh, sc, g2 = dh_ref[...], sc_ref[...], g2_ref[...]
        acc_ref[0:1, :] += _colsum(dh * xn * g2)
        acc_ref[1:2, :] += _colsum(dh)
        acc_ref[2:3, :] += _colsum(dh * (1.0 + sc) * xn)
        dx1 = dy_ref[...] + _rms_bwd(dh * (1.0 + sc) * g2, xn, r1)
        dx1_ref[...] = dx1
        mh, rm = _rms(mix_ref[...])
        gate, gp = gate_ref[...], gp_ref[...]
        acc_ref[3:4, :] += _colsum(dx1 * mh * gp)
        acc_ref[4:5, :] += _colsum(dx1 * gate * mh)
        dmix_ref[...] = _rms_bwd(dx1 * gate * gp, mh, rm).astype(BF16)

    return _rows("res_norm_bwd", body, [x1, mix, dh2, dy, sc2, gate, g2, gpost],
                 [_sds((T, D_MODEL), F32), _sds((T, D_MODEL), BF16), _sds((8, D_MODEL), F32)], T)


def _pre_norm_bwd(x, dh, dx1, g, sc):
    T = x.shape[0]

    def body(x_ref, dh_ref, dx1_ref, g_ref, sc_ref, dx_ref, acc_ref):
        @pl.when(pl.program_id(0) == 0)
        def _():
            acc_ref[...] = jnp.zeros_like(acc_ref)

        xn, r = _rms(x_ref[...])
        dh, sc, g = dh_ref[...], sc_ref[...], g_ref[...]
        acc_ref[0:1, :] += _colsum(dh * xn * g)
        acc_ref[1:2, :] += _colsum(dh)
        acc_ref[2:3, :] += _colsum(dh * (1.0 + sc) * xn)
        dx_ref[...] = dx1_ref[...] + _rms_bwd(dh * (1.0 + sc) * g, xn, r)

    return _rows("pre_norm_bwd", body, [x, dh, dx1, g, sc],
                 [_sds((T, D_MODEL), F32), _sds((8, D_MODEL), F32)], T)


def _rope_tables(pos_col, inv_freq):
    T = pos_col.shape[0]

    def body(p_ref, f_ref, c_ref, s_ref):
        ang = p_ref[...].astype(F32) * f_ref[...]
        lane = lax.broadcasted_iota(jnp.int32, ang.shape, 1)
        c_ref[...] = jnp.cos(ang)
        s_ref[...] = jnp.where(lane % HEAD_DIM < HEAD_DIM // 2, -1.0, 1.0) * jnp.sin(ang)

    return _rows("rope_tables", body, [pos_col, inv_freq],
                 [_sds((T, 128), F32), _sds((T, 128), F32)], T, tr=512)


def _swap_halves(t):
    W = t.shape[1]
    lane = lax.broadcasted_iota(jnp.int32, t.shape, 1)
    half = HEAD_DIM // 2
    return jnp.where(lane % HEAD_DIM < half, pltpu.roll(t, W - half, 1), pltpu.roll(t, half, 1))


def _widen(c, W):
    return c if W == 128 else jnp.concatenate([c] * (W // 128), axis=1)


def _rope(t, c, s):
    W = t.shape[1]
    return t * _widen(c, W) + _swap_halves(t) * _widen(s, W)


def _unrope(dy, c, s):
    W = dy.shape[1]
    return dy * _widen(c, W) + _swap_halves(dy * _widen(s, W))


def _attn_mask(n):
    qi = lax.broadcasted_iota(jnp.int32, (ATTN_BLK, 2 * ATTN_BLK), 0)
    kj = lax.broadcasted_iota(jnp.int32, (ATTN_BLK, 2 * ATTN_BLK), 1)
    rel = kj - ATTN_BLK
    return (rel <= qi) & (qi - rel < WINDOW) & ((n > 0) | (kj >= ATTN_BLK))


def _attn_load(cur, prv, cc, sc, cp, sp):
    x, xp = cur[...], prv[...]
    q = _rope(x[:, :512], cc[...], sc[...]) * (HEAD_DIM ** -0.5)
    k = jnp.concatenate([_rope(xp[:, 512:640], cp[...], sp[...]),
                         _rope(x[:, 512:640], cc[...], sc[...])], axis=0)
    v = jnp.concatenate([xp[:, 640:768], x[:, 640:768]], axis=0)
    return q, k, v


ROLLED = tuple(h for h in range(N_Q_HEADS) if h % 2 != h // (N_Q_HEADS // N_KV_HEADS))


def _pair_heads(t):
    half = lax.broadcasted_iota(jnp.int32, (ATTN_BLK, 128), 1) // HEAD_DIM
    return jnp.stack([jnp.where(half == h % 2, t[:, 128 * (h // 2):128 * (h // 2) + 128], 0.0)
                      for h in range(N_Q_HEADS)])


def _kv_heads(t):
    half = lax.broadcasted_iota(jnp.int32, t.shape, 1) // HEAD_DIM
    tr = pltpu.roll(t, HEAD_DIM, 1)
    return jnp.stack([jnp.where(half == h % 2, tr if h in ROLLED else t, 0.0)
                      for h in range(N_Q_HEADS)])


def _sink_column(snk):
    return jnp.stack([jnp.full((1, 1), snk[0, h], F32) for h in range(N_Q_HEADS)])


def _attn_probs(qh, kh, mask, sink):
    s = jnp.where(mask, _bdot(qh, kh, 2, 2), NEG_INF)
    m = jnp.maximum(jnp.max(s, axis=-1, keepdims=True), sink)
    p = jnp.exp(s - m)
    es = jnp.exp(sink - m)
    rl = 1.0 / (jnp.sum(p, axis=-1, keepdims=True) + es)
    return p, es, rl


def _attn_specs(nb):
    blk = lambda w: pl.BlockSpec((ATTN_BLK, w), lambda n: (n, 0))
    prv = lambda w: pl.BlockSpec((ATTN_BLK, w), lambda n: (jnp.maximum(n - 1, 0), 0))
    return [blk(768), prv(768), blk(128), blk(128), prv(128), prv(128),
            pl.BlockSpec(memory_space=pltpu.SMEM)]


def _attn_fwd(pa, cos, sin, sinks):
    T = pa.shape[0]
    nb = T // ATTN_BLK

    def body(cur, prv, cc, sc, cp, sp, snk, y_ref):
        n = pl.program_id(0)
        q, k, v = _attn_load(cur, prv, cc, sc, cp, sp)
        qh, kh, vh = _pair_heads(q).astype(BF16), _kv_heads(k).astype(BF16), _kv_heads(v).astype(BF16)
        p, _, rl = _attn_probs(qh, kh, _attn_mask(n), _sink_column(snk))
        o = _bdot(p.astype(BF16), vh, 2, 1) * rl
        for pair in range(N_Q_HEADS // 2):
            y_ref[:, 128 * pair:128 * pair + 128] = (o[2 * pair] + o[2 * pair + 1]).astype(BF16)

    return pl.pallas_call(
        body, name="attn_fwd", grid=(nb,), in_specs=_attn_specs(nb),
        out_specs=pl.BlockSpec((ATTN_BLK, 512), lambda n: (n, 0)),
        out_shape=_sds((T, 512), BF16), compiler_params=_params(("parallel",)))(
            pa, pa, cos, sin, cos, sin, sinks)


def _attn_bwd(pa, cos, sin, sinks, dy):
    T = pa.shape[0]
    nb = T // ATTN_BLK

    def body(cur, prv, cc, sc, cp, sp, snk, dy_ref, dq_ref, dcur_ref, dprv_ref, dsink_ref):
        n = pl.program_id(0)

        @pl.when(n == 0)
        def _():
            dsink_ref[...] = jnp.zeros_like(dsink_ref)

        q, k, v = _attn_load(cur, prv, cc, sc, cp, sp)
        qh, kh, vh = _pair_heads(q).astype(BF16), _kv_heads(k).astype(BF16), _kv_heads(v).astype(BF16)
        p, es, rl = _attn_probs(qh, kh, _attn_mask(n), _sink_column(snk))
        pn = p * rl
        do = _pair_heads(dy_ref[...]).astype(BF16)
        dp = _bdot(do, vh, 2, 2)
        delta = jnp.sum(pn * dp, axis=-1, keepdims=True)
        ds = (pn * (dp - delta)).astype(BF16)
        dsink = es * rl * delta
        dq = _bdot(ds, kh, 2, 1) * (HEAD_DIM ** -0.5)
        dkh = _bdot_rows(ds, qh)
        dvh = _bdot_rows(pn.astype(BF16), do)

        def fold(t):
            same = [t[h] for h in range(N_Q_HEADS) if h not in ROLLED]
            moved = [t[h] for h in ROLLED]
            return sum(same[1:], same[0]) + pltpu.roll(sum(moved[1:], moved[0]), HEAD_DIM, 1)

        dk, dv = fold(dkh), fold(dvh)
        for h in range(N_Q_HEADS):
            dsink_ref[h:h + 1, :] += -jnp.sum(dsink[h])
        for pair in range(N_Q_HEADS // 2):
            dq_ref[:, 128 * pair:128 * pair + 128] = _unrope(
                dq[2 * pair] + dq[2 * pair + 1], cc[...], sc[...]).astype(BF16)
        dcur_ref[:, 0:128] = dk[ATTN_BLK:]
        dcur_ref[:, 128:256] = dv[ATTN_BLK:]
        dprv_ref[:, 0:128] = dk[:ATTN_BLK]
        dprv_ref[:, 128:256] = dv[:ATTN_BLK]

    blk = lambda w: pl.BlockSpec((ATTN_BLK, w), lambda n: (n, 0))
    return pl.pallas_call(
        body, name="attn_bwd", grid=(nb,), in_specs=_attn_specs(nb) + [blk(512)],
        out_specs=[blk(512), blk(256), blk(256), pl.BlockSpec((8, 128), lambda n: (0, 0))],
        out_shape=[_sds((T, 512), BF16), _sds((T, 256), F32), _sds((T, 256), F32),
                   _sds((8, 128), F32)],
        compiler_params=_params(("arbitrary",)))(pa, pa, cos, sin, cos, sin, sinks, dy)


def _attn_kv_combine(dcur, dprv, cos, sin):
    T = dcur.shape[0]
    nb = T // ATTN_BLK

    def body(c_ref, p_ref, cc, sc, o_ref):
        n = pl.program_id(0)
        t = c_ref[...] + jnp.where(n < nb - 1, p_ref[...], 0.0)
        o_ref[:, 0:128] = _unrope(t[:, 0:128], cc[...], sc[...]).astype(BF16)
        o_ref[:, 128:256] = t[:, 128:256].astype(BF16)

    blk = lambda w: pl.BlockSpec((ATTN_BLK, w), lambda n: (n, 0))
    nxt = pl.BlockSpec((ATTN_BLK, 256), lambda n: (jnp.minimum(n + 1, nb - 1), 0))
    return pl.pallas_call(
        body, name="attn_kv_combine", grid=(nb,), in_specs=[blk(256), nxt, blk(128), blk(128)],
        out_specs=blk(256), out_shape=_sds((T, 256), BF16),
        compiler_params=_params(("parallel",)))(dcur, dprv, cos, sin)


CONV_COLS = 2 * MLSTM_HEADS * MLSTM_HEAD_DIM


def _conv_pre(cur_ref, halo_ref, w_ref, b_ref, i, tr):
    xx = jnp.concatenate([jnp.where(i > 0, halo_ref[...], 0.0), cur_ref[...]], axis=0)
    taps = [(pltpu.roll(xx, CONV_WIDTH - 1 - j, 0) if j < CONV_WIDTH - 1 else xx)[8:8 + tr]
            for j in range(CONV_WIDTH)]
    pre = b_ref[...]
    for j in range(CONV_WIDTH):
        pre = pre + taps[j] * w_ref[j:j + 1, :]
    return pre, taps


def _conv_specs(T, tr):
    return [pl.BlockSpec((tr, CONV_COLS), lambda i: (i, 0)),
            pl.BlockSpec((8, CONV_COLS), lambda i: (jnp.maximum(i * (tr // 8) - 1, 0), 0)),
            pl.BlockSpec((CONV_WIDTH, CONV_COLS), lambda i: (0, 0)),
            pl.BlockSpec((1, CONV_COLS), lambda i: (0, 0))]


def _conv_fwd(pm, w, b):
    T = pm.shape[0]
    tr = min(ROW_TILE, T)

    def body(cur_ref, halo_ref, w_ref, b_ref, o_ref):
        pre, _ = _conv_pre(cur_ref, halo_ref, w_ref, b_ref, pl.program_id(0), tr)
        o_ref[...] = pre * _sigmoid(pre)

    return pl.pallas_call(
        body, name="conv_fwd", grid=(T // tr,), in_specs=_conv_specs(T, tr),
        out_specs=pl.BlockSpec((tr, CONV_COLS), lambda i: (i, 0)),
        out_shape=_sds((T, CONV_COLS), F32), compiler_params=_params(("parallel",)))(pm, pm, w, b)


def _conv_bwd_pre(pm, w, b, dqk):
    T = pm.shape[0]
    tr = min(ROW_TILE, T)

    def body(cur_ref, halo_ref, w_ref, b_ref, d_ref, dpre_ref, acc_ref):
        i = pl.program_id(0)

        @pl.when(i == 0)
        def _():
            acc_ref[...] = jnp.zeros_like(acc_ref)

        pre, taps = _conv_pre(cur_ref, halo_ref, w_ref, b_ref, i, tr)
        sg = _sigmoid(pre)
        dpre = d_ref[...] * (sg * (1.0 + pre * (1.0 - sg)))
        dpre_ref[...] = dpre
        for j in range(CONV_WIDTH):
            acc_ref[j:j + 1, :] += _colsum(dpre * taps[j])
        acc_ref[CONV_WIDTH:CONV_WIDTH + 1, :] += _colsum(dpre)

    return pl.pallas_call(
        body, name="conv_bwd_pre", grid=(T // tr,),
        in_specs=_conv_specs(T, tr) + [pl.BlockSpec((tr, CONV_COLS), lambda i: (i, 0))],
        out_specs=[pl.BlockSpec((tr, CONV_COLS), lambda i: (i, 0)),
                   pl.BlockSpec((8, CONV_COLS), lambda i: (0, 0))],
        out_shape=[_sds((T, CONV_COLS), F32), _sds((8, CONV_COLS), F32)],
        compiler_params=_params(("arbitrary",)))(pm, pm, w, b, dqk)


def _conv_bwd_in(dpre, w):
    T = dpre.shape[0]
    tr = min(ROW_TILE, T)
    nt = T // tr

    def body(cur_ref, halo_ref, w_ref, o_ref):
        i = pl.program_id(0)
        yy = jnp.concatenate([cur_ref[...], jnp.where(i < nt - 1, halo_ref[...], 0.0)], axis=0)
        du = cur_ref[...] * w_ref[CONV_WIDTH - 1:CONV_WIDTH, :]
        for j in range(CONV_WIDTH - 1):
            k = CONV_WIDTH - 1 - j
            du = du + pltpu.roll(yy, tr + 8 - k, 0)[:tr] * w_ref[j:j + 1, :]
        o_ref[...] = du.astype(BF16)

    return pl.pallas_call(
        body, name="conv_bwd_in", grid=(nt,),
        in_specs=[pl.BlockSpec((tr, CONV_COLS), lambda i: (i, 0)),
                  pl.BlockSpec((8, CONV_COLS),
                               lambda i: (jnp.minimum((i + 1) * (tr // 8), T // 8 - 1), 0)),
                  pl.BlockSpec((CONV_WIDTH, CONV_COLS), lambda i: (0, 0))],
        out_specs=pl.BlockSpec((tr, CONV_COLS), lambda i: (i, 0)),
        out_shape=_sds((T, CONV_COLS), BF16), compiler_params=_params(("parallel",)))(dpre, dpre, w)


def _log_sigmoid(x):
    return jnp.minimum(x, 0.0) - jnp.log1p(jnp.exp(-jnp.abs(x)))


def _chunk_cumsum(x, axis):
    idx = lax.broadcasted_iota(jnp.int32, x.shape, axis) % MLSTM_CHUNK
    k = 1
    while k < MLSTM_CHUNK:
        x = x + jnp.where(idx >= k, pltpu.roll(x, k, axis), 0.0)
        k *= 2
    return x


def _chunk_rev_cumsum(x, axis):
    n = x.shape[axis]
    idx = lax.broadcasted_iota(jnp.int32, x.shape, axis) % MLSTM_CHUNK
    k = 1
    while k < MLSTM_CHUNK:
        x = x + jnp.where(idx < MLSTM_CHUNK - k, pltpu.roll(x, n - k, axis), 0.0)
        k *= 2
    return x


def _mlstm_gates(gc_ref, bc_ref, gr_ref, br_ref):
    gc = gc_ref[...] + bc_ref[...]
    gr = gr_ref[...] + br_ref[...]
    return gc, _chunk_cumsum(_log_sigmoid(gc), 0), gr, _chunk_cumsum(_log_sigmoid(gr), 1)


def _heads(ref, base=0):
    D = MLSTM_HEAD_DIM
    return jnp.stack([ref[:, base + D * h:base + D * h + D] for h in range(MLSTM_HEADS)])


def _mlstm_inputs(q_ref, k_ref, v_ref, gc, bc, gr, br):
    H = MLSTM_HEADS
    q, v = _heads(q_ref), _heads(v_ref)
    ks = _heads(k_ref) * (MLSTM_HEAD_DIM ** -0.5)
    return dict(
        q=q, ks=ks, qb=q.astype(BF16), kb=ks.astype(BF16), vb=v.astype(BF16),
        b_col=jnp.stack([bc[:, H + h:H + h + 1] for h in range(H)]),
        i_col=jnp.stack([gc[:, h:h + 1] for h in range(H)]),
        b_row=jnp.stack([br[H + h:H + h + 1, :] for h in range(H)]),
        i_row=jnp.stack([gr[h:h + 1, :] for h in range(H)]))


def _mlstm_head(f, c_prev, n_prev, m_prev):
    L = MLSTM_CHUNK
    q, qb = f["q"], f["qb"]
    t = lax.broadcasted_iota(jnp.int32, (1, 2 * L, 2 * L), 1)
    s = lax.broadcasted_iota(jnp.int32, (1, 2 * L, 2 * L), 2)
    mask = (t // L == s // L) & (s <= t)
    d = jnp.where(mask, f["b_col"] - f["b_row"] + f["i_row"], NEG_INF)
    row = lax.broadcasted_iota(jnp.int32, (1, 2 * L, 1), 1)
    inter = f["b_col"] + jnp.where(row < L, m_prev[0], m_prev[1])
    m_t = jnp.maximum(inter, jnp.max(d, axis=-1, keepdims=True))
    w_intra = jnp.exp(d - m_t)
    w_inter = jnp.exp(inter - m_t)
    sc = _bdot(qb, f["kb"], 2, 2) * w_intra
    qc = jnp.concatenate([_bdot(qb[:, :L], c_prev[0].astype(BF16), 2, 1),
                          _bdot(qb[:, L:], c_prev[1].astype(BF16), 2, 1)], axis=1)
    qn = jnp.concatenate([jnp.sum(q[:, :L] * n_prev[0], axis=-1, keepdims=True),
                          jnp.sum(q[:, L:] * n_prev[1], axis=-1, keepdims=True)], axis=1)
    num = _bdot(sc.astype(BF16), f["vb"], 2, 1) + w_inter * qc
    den = jnp.sum(sc, axis=-1, keepdims=True) + w_inter * qn
    return dict(f, w_intra=w_intra, w_inter=w_inter, sc=sc, qc=qc, qn=qn, num=num, den=den,
                floor=jnp.exp(-m_t))


def _mlstm_update(f, ch, c, n, m):
    L = MLSTM_CHUNK
    rows = slice(L * ch, L * ch + L)
    b_col = f["b_col"][:, rows]
    g_last = b_col[:, L - 1:L]
    a_col = g_last - b_col + f["i_col"][:, rows]
    m_new = jnp.maximum(g_last + m, jnp.max(a_col, axis=1, keepdims=True))
    decay = jnp.exp(g_last + m - m_new)
    e_a = jnp.exp(a_col - m_new)
    kw = f["ks"][:, rows] * e_a
    c_new = decay * c + _bdot_rows(kw.astype(BF16), f["vb"][:, rows])
    n_new = decay * n + jnp.sum(kw, axis=1, keepdims=True)
    return c_new, n_new, m_new, decay, e_a, kw


def _mlstm_specs(T, order):
    blk = lambda w, col: pl.BlockSpec((STEP_ROWS, w), lambda s: (order(s), col))
    return [blk(512, 0), blk(512, 1), blk(512, 2), blk(128, 0),
            pl.BlockSpec((1, 128), lambda s: (0, 0)),
            pl.BlockSpec((8, STEP_ROWS), lambda s: (0, order(s))),
            pl.BlockSpec((8, 128), lambda s: (0, 0))]


def _lanes(m):
    return jnp.broadcast_to(m, m.shape[:-1] + (128,))


def _mlstm_fwd(qk, pm, gcol, bcol, grow, brow):
    T = qk.shape[0]
    steps = T // STEP_ROWS
    H, D = MLSTM_HEADS, MLSTM_HEAD_DIM

    def body(q_ref, k_ref, v_ref, gc_ref, bc_ref, gr_ref, br_ref, h_ref, cs_ref, ns_ref, ms_ref,
             c_st, n_st, m_st):
        @pl.when(pl.program_id(0) == 0)
        def _():
            c_st[...] = jnp.zeros_like(c_st)
            n_st[...] = jnp.zeros_like(n_st)
            m_st[...] = jnp.zeros_like(m_st)

        f = _mlstm_inputs(q_ref, k_ref, v_ref, *_mlstm_gates(gc_ref, bc_ref, gr_ref, br_ref))
        c0, n0, m0 = c_st[...], n_st[...], m_st[:, :, 0:1]
        c1, n1, m1, _, _, _ = _mlstm_update(f, 0, c0, n0, m0)
        c2, n2, m2, _, _, _ = _mlstm_update(f, 1, c1, n1, m1)
        f = _mlstm_head(f, (c0, c1), (n0, n1), (m0, m1))
        h = f["num"] / jnp.maximum(jnp.abs(f["den"]), f["floor"])
        for hd in range(H):
            h_ref[:, D * hd:D * hd + D] = h[hd]
        cs_ref[0], cs_ref[1] = c0, c1
        ns_ref[0], ns_ref[1] = n0, n1
        ms_ref[0], ms_ref[1] = _lanes(m0), _lanes(m1)
        c_st[...], n_st[...], m_st[...] = c2, n2, _lanes(m2)

    vec = pl.BlockSpec((2, H, 1, 128), lambda s: (s, 0, 0, 0))
    return pl.pallas_call(
        body, name="mlstm_fwd", grid=(steps,), in_specs=_mlstm_specs(T, lambda s: s),
        out_specs=[pl.BlockSpec((STEP_ROWS, 512), lambda s: (s, 0)),
                   pl.BlockSpec((2, H, 128, 128), lambda s: (s, 0, 0, 0)), vec, vec],
        out_shape=[_sds((T, 512), F32), _sds((2 * steps, H, 128, 128), F32),
                   _sds((2 * steps, H, 1, 128), F32), _sds((2 * steps, H, 1, 128), F32)],
        scratch_shapes=[pltpu.VMEM((H, 128, 128), F32), pltpu.VMEM((H, 1, 128), F32),
                        pltpu.VMEM((H, 1, 128), F32)],
        compiler_params=_params(("arbitrary",)))(qk, qk, pm, gcol, bcol, grow, brow)


def _mlstm_bwd(qk, pm, gcol, bcol, grow, brow, cs, ns, ms, dh):
    T = qk.shape[0]
    steps = T // STEP_ROWS
    H, L, D = MLSTM_HEADS, MLSTM_CHUNK, MLSTM_HEAD_DIM
    rev = lambda s: steps - 1 - s

    def body(q_ref, k_ref, v_ref, gc_ref, bc_ref, gr_ref, br_ref, cs_ref, ns_ref, ms_ref, dh_ref,
             dqk_ref, dv_ref, dgc_ref, dgr_ref, dc_st, dn_st):
        @pl.when(pl.program_id(0) == 0)
        def _():
            dc_st[...] = jnp.zeros_like(dc_st)
            dn_st[...] = jnp.zeros_like(dn_st)

        f = _mlstm_inputs(q_ref, k_ref, v_ref, *_mlstm_gates(gc_ref, bc_ref, gr_ref, br_ref))
        c_prev = (cs_ref[0], cs_ref[1])
        n_prev = (ns_ref[0], ns_ref[1])
        m_prev = (ms_ref[0, :, :, 0:1], ms_ref[1, :, :, 0:1])
        f = _mlstm_head(f, c_prev, n_prev, m_prev)
        big = jnp.abs(f["den"]) > f["floor"]
        rden = 1.0 / jnp.where(big, jnp.abs(f["den"]), f["floor"])
        dnum = _heads(dh_ref) * rden
        hdh = jnp.sum(f["num"] * dnum, axis=-1, keepdims=True)
        dden = jnp.where(big, -hdh * rden * jnp.sign(f["den"]), 0.0)
        dnum_b = dnum.astype(BF16)
        dsc = _bdot(dnum_b, f["vb"], 2, 2) + dden
        g = dsc * f["sc"]
        dv = _bdot_rows(f["sc"].astype(BF16), dnum_b)
        dqk_ = (dsc * f["w_intra"]).astype(BF16)
        dq = _bdot(dqk_, f["kb"], 2, 1)
        dks = _bdot_rows(dqk_, f["qb"])
        wdn = f["w_inter"] * dnum
        wdn_b = wdn.astype(BF16)
        wdd = f["w_inter"] * dden
        u = jnp.sum(f["qc"] * wdn, axis=-1, keepdims=True) + wdd * f["qn"]
        dks_s, dv_s, z_s, dg_s = [None, None], [None, None], [None, None], [None, None]
        dcn, dnn = dc_st[...], dn_st[...]
        for ch in (1, 0):
            rows = slice(L * ch, L * ch + L)
            _, _, _, decay, e_a, kw = _mlstm_update(f, ch, c_prev[ch], n_prev[ch], m_prev[ch])
            dcn_b = dcn.astype(BF16)
            dkw = _bdot(f["vb"][:, rows], dcn_b, 2, 2) + dnn
            dks_s[ch] = e_a * dkw
            dv_s[ch] = _bdot(kw.astype(BF16), dcn_b, 2, 1)
            z_s[ch] = e_a * jnp.sum(f["ks"][:, rows] * dkw, axis=-1, keepdims=True)
            dg_s[ch] = jnp.sum(z_s[ch], axis=1, keepdims=True) + decay * (
                jnp.sum(c_prev[ch] * dcn, axis=(1, 2), keepdims=True)
                + jnp.sum(n_prev[ch] * dnn, axis=(1, 2), keepdims=True))
            dcn = decay * dcn + _bdot_rows(f["qb"][:, rows], wdn_b[:, rows])
            dnn = decay * dnn + jnp.sum(wdd[:, rows] * f["q"][:, rows], axis=1, keepdims=True)
        dc_st[...], dn_st[...] = dcn, dnn
        dq = dq + jnp.concatenate(
            [_bdot(wdn_b[:, :L], c_prev[0].astype(BF16), 2, 2) + wdd[:, :L] * n_prev[0],
             _bdot(wdn_b[:, L:], c_prev[1].astype(BF16), 2, 2) + wdd[:, L:] * n_prev[1]], axis=1)
        dks = (dks + jnp.concatenate(dks_s, axis=1)) * (D ** -0.5)
        dv = dv + jnp.concatenate(dv_s, axis=1)
        z = jnp.concatenate(z_s, axis=1)
        row = lax.broadcasted_iota(jnp.int32, (1, STEP_ROWS, 1), 1)
        dg_col = jnp.where(row == L - 1, dg_s[0], 0.0) + jnp.where(row == 2 * L - 1, dg_s[1], 0.0)
        db_col = jnp.sum(g, axis=-1, keepdims=True) + u - z + dg_col
        g_row = jnp.sum(g, axis=1, keepdims=True)
        lane = lax.broadcasted_iota(jnp.int32, (STEP_ROWS, 128), 1)
        sub = lax.broadcasted_iota(jnp.int32, (8, STEP_ROWS), 0)
        dgc = jnp.zeros((STEP_ROWS, 128), F32)
        dgr = jnp.zeros((8, STEP_ROWS), F32)
        for hd in range(H):
            dgc = dgc + jnp.where(lane == hd, z[hd], 0.0) + jnp.where(lane == H + hd, db_col[hd], 0.0)
            dgr = dgr + jnp.where(sub == hd, g_row[hd], 0.0) - jnp.where(sub == H + hd, g_row[hd], 0.0)
            dqk_ref[:, D * hd:D * hd + D] = dq[hd]
            dqk_ref[:, H * D + D * hd:H * D + D * hd + D] = dks[hd]
            dv_ref[:, D * hd:D * hd + D] = dv[hd].astype(BF16)
        dgc_ref[...] = dgc
        dgr_ref[...] = dgr

    return pl.pallas_call(
        body, name="mlstm_bwd", grid=(steps,),
        in_specs=_mlstm_specs(T, rev) + [
            pl.BlockSpec((2, H, 128, 128), lambda s: (rev(s), 0, 0, 0)),
            pl.BlockSpec((2, H, 1, 128), lambda s: (rev(s), 0, 0, 0)),
            pl.BlockSpec((2, H, 1, 128), lambda s: (rev(s), 0, 0, 0)),
            pl.BlockSpec((STEP_ROWS, 512), lambda s: (rev(s), 0))],
        out_specs=[pl.BlockSpec((STEP_ROWS, 1024), lambda s: (rev(s), 0)),
                   pl.BlockSpec((STEP_ROWS, 512), lambda s: (rev(s), 0)),
                   pl.BlockSpec((STEP_ROWS, 128), lambda s: (rev(s), 0)),
                   pl.BlockSpec((8, STEP_ROWS), lambda s: (0, rev(s)))],
        out_shape=[_sds((T, 1024), F32), _sds((T, 512), BF16), _sds((T, 128), F32), _sds((8, T), F32)],
        scratch_shapes=[pltpu.VMEM((H, 128, 128), F32), pltpu.VMEM((H, 1, 128), F32)],
        compiler_params=_params(("arbitrary",)))(qk, qk, pm, gcol, bcol, grow, brow, cs, ns, ms, dh)


def _gate_bwd(dgc, dgr_t, gcol, bcol):
    T = dgc.shape[0]

    def body(a_ref, b_ref, g_ref, bias_ref, o_ref, acc_ref):
        @pl.when(pl.program_id(0) == 0)
        def _():
            acc_ref[...] = jnp.zeros_like(acc_ref)

        d = a_ref[...] + b_ref[...]
        lane = lax.broadcasted_iota(jnp.int32, d.shape, 1)
        is_f = (lane >= MLSTM_HEADS) & (lane < 2 * MLSTM_HEADS)
        dlogf = _chunk_rev_cumsum(jnp.where(is_f, d, 0.0), 0)
        out = jnp.where(is_f, dlogf * _sigmoid(-(g_ref[...] + bias_ref[...])), d)
        o_ref[...] = out.astype(BF16)
        acc_ref[0:1, :] += _colsum(out)

    return _rows("gate_bwd", body, [dgc, dgr_t, gcol, bcol],
                 [_sds((T, 128), BF16), _sds((8, 128), F32)], T)


def _head_norm(h, mu_axis=-1):
    mu = jnp.mean(h, axis=-1, keepdims=True)
    hc = h - mu
    r = lax.rsqrt(jnp.mean(hc * hc, axis=-1, keepdims=True) + NORM_EPS)
    return hc * r, r


def _mlstm_out(hm, pm, w):
    T = hm.shape[0]
    D = MLSTM_HEAD_DIM

    def body(h_ref, o_ref, w_ref, y_ref):
        for hd in range(MLSTM_HEADS):
            cols = slice(D * hd, D * hd + D)
            hn, _ = _head_norm(h_ref[:, cols])
            y_ref[:, cols] = (_sigmoid(o_ref[:, cols]) * hn * w_ref[:, cols]).astype(BF16)

    tr = min(ROW_TILE, T)
    return pl.pallas_call(
        body, name="mlstm_out", grid=(T // tr,),
        in_specs=[pl.BlockSpec((tr, 512), lambda i: (i, 0)), pl.BlockSpec((tr, 512), lambda i: (i, 3)),
                  pl.BlockSpec((1, 512), lambda i: (0, 0))],
        out_specs=pl.BlockSpec((tr, 512), lambda i: (i, 0)), out_shape=_sds((T, 512), BF16),
        compiler_params=_params(("parallel",)))(hm, pm, w)


def _mlstm_out_bwd(hm, pm, w, dy):
    T = hm.shape[0]
    D = MLSTM_HEAD_DIM
    tr = min(ROW_TILE, T)

    def body(h_ref, o_ref, w_ref, dy_ref, dh_ref, do_ref, acc_ref):
        @pl.when(pl.program_id(0) == 0)
        def _():
            acc_ref[...] = jnp.zeros_like(acc_ref)

        for hd in range(MLSTM_HEADS):
            cols = slice(D * hd, D * hd + D)
            hn, r = _head_norm(h_ref[:, cols])
            sg = _sigmoid(o_ref[:, cols])
            dy, w = dy_ref[:, cols], w_ref[:, cols]
            do_ref[:, cols] = (dy * hn * w * sg * (1.0 - sg)).astype(BF16)
            dyn = dy * sg
            acc_ref[0:1, cols] += _colsum(dyn * hn)
            dhn = dyn * w
            dh_ref[:, cols] = r * (dhn - jnp.mean(dhn, axis=-1, keepdims=True)
                                   - hn * jnp.mean(dhn * hn, axis=-1, keepdims=True))

    return pl.pallas_call(
        body, name="mlstm_out_bwd", grid=(T // tr,),
        in_specs=[pl.BlockSpec((tr, 512), lambda i: (i, 0)), pl.BlockSpec((tr, 512), lambda i: (i, 3)),
                  pl.BlockSpec((1, 512), lambda i: (0, 0)), pl.BlockSpec((tr, 512), lambda i: (i, 0))],
        out_specs=[pl.BlockSpec((tr, 512), lambda i: (i, 0)), pl.BlockSpec((tr, 512), lambda i: (i, 0)),
                   pl.BlockSpec((8, 512), lambda i: (0, 0))],
        out_shape=[_sds((T, 512), F32), _sds((T, 512), BF16), _sds((8, 512), F32)],
        compiler_params=_params(("arbitrary",)))(hm, pm, w, dy)


def _adamw(name, w, g, m, v, tr=64):
    R, C = w.shape
    tr = min(tr, R)
    c1 = 1.0 - ADAM_B1 ** ADAM_STEP
    c2 = 1.0 - ADAM_B2 ** ADAM_STEP

    def body(w_ref, g_ref, m_ref, v_ref, d_ref, mo_ref, vo_ref):
        g = g_ref[...]
        m = ADAM_B1 * m_ref[...] + (1.0 - ADAM_B1) * g
        v = ADAM_B2 * v_ref[...] + (1.0 - ADAM_B2) * (g * g)
        mo_ref[...] = m
        vo_ref[...] = v
        d_ref[...] = -ADAM_LR * ((m / c1) / (jnp.sqrt(v / c2) + ADAM_EPS) + ADAM_WD * w_ref[...])

    spec = pl.BlockSpec((tr, C), lambda i: (i, 0))
    return pl.pallas_call(
        body, name=name, grid=(R // tr,), in_specs=[spec] * 4, out_specs=[spec] * 3,
        out_shape=[_sds((R, C), F32)] * 3, compiler_params=_params(("parallel",)))(w, g, m, v)


def _place():
    return lax.axis_index("x"), lax.axis_index("y"), lax.axis_index("c")


def _all_gather8(name, blk, space):
    m, n = blk.shape

    def body(x_ref, out_ref, send_sems, recv_sems, local_sem):
        x, y, c = _place()
        me, sibling = (x, y, c), (x, y, 1 - c)
        chips = [(1 - x, y), (x, 1 - y), (1 - x, 1 - y)]

        def rows(px, py, pc):
            return out_ref.at[pl.ds((4 * px + 2 * py + pc) * m, m), :]

        def copy(k, block, to, src=None):
            return pltpu.make_async_remote_copy(
                src_ref=rows(*block) if src is None else src, dst_ref=rows(*block),
                send_sem=send_sems.at[k], recv_sem=recv_sems.at[k],
                device_id=to, device_id_type=MESH)

        mine = pltpu.make_async_copy(x_ref, rows(*me), local_sem)
        mine.start()
        first = [copy(0, me, sibling, src=x_ref)]
        first += [copy(1 + j, me, (*chip, c), src=x_ref) for j, chip in enumerate(chips)]
        for cp in first:
            cp.start()
        passed = [copy(4 + j, (*chip, c), sibling) for j, chip in enumerate(chips)]
        for j, chip in enumerate(chips):
            copy(1 + j, (*chip, c), me).wait_recv()
            passed[j].start()
        copy(0, sibling, me).wait_recv()
        for j, chip in enumerate(chips):
            copy(4 + j, (*chip, 1 - c), me).wait_recv()
        for cp in first + passed:
            cp.wait_send()
        mine.wait()

    return pl.pallas_call(
        body, name=name, out_shape=_sds((8 * m, n), blk.dtype),
        in_specs=[pl.BlockSpec(memory_space=space)], out_specs=pl.BlockSpec(memory_space=space),
        scratch_shapes=[pltpu.SemaphoreType.DMA((7,)), pltpu.SemaphoreType.DMA((7,)),
                        pltpu.SemaphoreType.DMA],
        compiler_params=pltpu.CompilerParams(vmem_limit_bytes=VMEM_LIMIT))(blk)


def _hbm_specs(n):
    return [pl.BlockSpec(memory_space=pl.ANY)] * n


def _gather_groups(name, blocks):
    nw = len(blocks)

    def body(*refs):
        srcs, outs = refs[:nw], refs[nw:2 * nw]
        send_sems, recv_sems, local_sems = refs[2 * nw:]
        x, y, c = _place()
        me, sibling = (x, y, c), (x, y, 1 - c)
        chips = [(1 - x, y), (x, 1 - y), (1 - x, 1 - y)]

        def slab(w, px, py, pc):
            return outs[w].at[2 * px + py, pl.ds(0, blocks[w].shape[0]), pc]

        def copy(w, k, block, to, src=None):
            return pltpu.make_async_remote_copy(
                src_ref=slab(w, *block) if src is None else src, dst_ref=slab(w, *block),
                send_sem=send_sems.at[w, k], recv_sem=recv_sems.at[w, k],
                device_id=to, device_id_type=MESH)

        mine = [pltpu.make_async_copy(srcs[w], slab(w, *me), local_sems.at[w]) for w in range(nw)]
        first = []
        for w in range(nw):
            mine[w].start()
            first.append(copy(w, 0, me, sibling, src=srcs[w]))
            first += [copy(w, 1 + j, me, (*chip, c), src=srcs[w]) for j, chip in enumerate(chips)]
        for cp in first:
            cp.start()
        passed = []
        for j, chip in enumerate(chips):
            for w in range(nw):
                copy(w, 1 + j, (*chip, c), me).wait_recv()
                passed.append(copy(w, 4 + j, (*chip, c), sibling))
                passed[-1].start()
        for w in range(nw):
            copy(w, 0, sibling, me).wait_recv()
            for j, chip in enumerate(chips):
                copy(w, 4 + j, (*chip, 1 - c), me).wait_recv()
        for cp in first + passed:
            cp.wait_send()
        for w in range(nw):
            mine[w].wait()

    return pl.pallas_call(
        body, name=name,
        out_shape=[_sds((4, b.shape[0], 2) + b.shape[1:], b.dtype) for b in blocks],
        in_specs=_hbm_specs(nw), out_specs=_hbm_specs(nw),
        scratch_shapes=[pltpu.SemaphoreType.DMA((nw, 7)), pltpu.SemaphoreType.DMA((nw, 7)),
                        pltpu.SemaphoreType.DMA((nw,))])(*blocks)


def _swap_sibling(name, srcs, halves=False):
    nw = len(srcs)

    def body(*refs):
        src_refs, dst_refs, send_sems, recv_sems = refs[:nw], refs[nw:2 * nw], refs[2 * nw], refs[2 * nw + 1]
        x, y, c = _place()
        cps = []
        for w in range(nw):
            s = src_refs[w]
            if halves:
                s = s.at[pl.ds(0, srcs[w].shape[0]), pl.ds(0, srcs[w].shape[1]), 1 - c]
            cps.append(pltpu.make_async_remote_copy(
                src_ref=s, dst_ref=dst_refs[w], send_sem=send_sems.at[w], recv_sem=recv_sems.at[w],
                device_id=(x, y, 1 - c), device_id_type=MESH))
        for cp in cps:
            cp.start()
        for cp in cps:
            cp.wait()

    shapes = [(s.shape[:2] + s.shape[3:]) if halves else s.shape for s in srcs]
    return pl.pallas_call(
        body, name=name, out_shape=[_sds(sh, s.dtype) for sh, s in zip(shapes, srcs)],
        in_specs=_hbm_specs(nw), out_specs=_hbm_specs(nw),
        scratch_shapes=[pltpu.SemaphoreType.DMA((nw,)), pltpu.SemaphoreType.DMA((nw,))])(*srcs)


def _swap_chips(name, srcs):
    nw = len(srcs)

    def body(*refs):
        src_refs, dst_refs = refs[:nw], refs[nw:2 * nw]
        send_sems, recv_sems, local_sems = refs[2 * nw:]
        x, y, c = _place()
        mine = 2 * x + y
        chips = [(1 - x, y), (x, 1 - y), (1 - x, 1 - y)]

        def copy(w, j):
            px, py = chips[j]
            return pltpu.make_async_remote_copy(
                src_ref=src_refs[w].at[2 * px + py], dst_ref=dst_refs[w].at[mine],
                send_sem=send_sems.at[w, j], recv_sem=recv_sems.at[w, j],
                device_id=(px, py, c), device_id_type=MESH)

        def landing(w, j):
            px, py = chips[j]
            return pltpu.make_async_remote_copy(
                src_ref=src_refs[w].at[mine], dst_ref=dst_refs[w].at[2 * px + py],
                send_sem=send_sems.at[w, j], recv_sem=recv_sems.at[w, j],
                device_id=(px, py, c), device_id_type=MESH)

        own = [pltpu.make_async_copy(src_refs[w].at[mine], dst_refs[w].at[mine], local_sems.at[w])
               for w in range(nw)]
        for w in range(nw):
            own[w].start()
            for j in range(3):
                copy(w, j).start()
        for w in range(nw):
            for j in range(3):
                landing(w, j).wait_recv()
        for w in range(nw):
            for j in range(3):
                copy(w, j).wait_send()
            own[w].wait()

    return pl.pallas_call(
        body, name=name, out_shape=[_sds(s.shape, s.dtype) for s in srcs],
        in_specs=_hbm_specs(nw), out_specs=_hbm_specs(nw),
        scratch_shapes=[pltpu.SemaphoreType.DMA((nw, 3)), pltpu.SemaphoreType.DMA((nw, 3)),
                        pltpu.SemaphoreType.DMA((nw,))])(*srcs)


def _pair_sum(name, full, got, core):
    _, g, _, m, n = full.shape

    def body(c_ref, a_ref, b_ref, o_ref):
        o_ref[...] = (a_ref[...].astype(F32) + b_ref[...].astype(F32)).astype(o_ref.dtype)

    slab = pl.BlockSpec((None, None, m, n), lambda s, w, c: (s, w, 0, 0))
    return pl.pallas_call(
        body, name=name,
        grid_spec=pltpu.PrefetchScalarGridSpec(
            num_scalar_prefetch=1, grid=(4, g),
            in_specs=[pl.BlockSpec((None, None, None, m, n), lambda s, w, c: (s, w, c[0], 0, 0)), slab],
            out_specs=slab),
        out_shape=_sds(got.shape, BF16),
        compiler_params=_params(("parallel", "parallel")))(core, full, got)


def _sum4(name, a):
    _, g, m, n = a.shape

    def body(a_ref, o_ref):
        acc = a_ref[0].astype(F32)
        for s in range(1, 4):
            acc = acc + a_ref[s].astype(F32)
        o_ref[...] = acc

    return pl.pallas_call(body, name=name, grid=(g,),
                          in_specs=[pl.BlockSpec((4, None, m, n), lambda w: (0, w, 0, 0))],
                          out_specs=pl.BlockSpec((None, m, n), lambda w: (w, 0, 0)),
                          out_shape=_sds((g, m, n), F32), compiler_params=_params(("parallel",)))(a)


def _small_update(gathered, w, m, v):
    n = w.shape[1]
    tn = 2048
    c1 = 1.0 - ADAM_B1 ** ADAM_STEP
    c2 = 1.0 - ADAM_B2 ** ADAM_STEP

    def body(g_ref, w_ref, m_ref, v_ref, go_ref, d_ref, mo_ref, vo_ref):
        g = g_ref[0:1, :]
        for d in range(1, 8):
            g = g + g_ref[d:d + 1, :]
        go_ref[...] = g
        m = ADAM_B1 * m_ref[...] + (1.0 - ADAM_B1) * g
        v = ADAM_B2 * v_ref[...] + (1.0 - ADAM_B2) * (g * g)
        mo_ref[...] = m
        vo_ref[...] = v
        d_ref[...] = -ADAM_LR * ((m / c1) / (jnp.sqrt(v / c2) + ADAM_EPS) + ADAM_WD * w_ref[...])

    row = pl.BlockSpec((1, tn), lambda i: (0, i))
    return pl.pallas_call(
        body, name="small_update", grid=(n // tn,),
        in_specs=[pl.BlockSpec((8, tn), lambda i: (0, i)), row, row, row], out_specs=[row] * 4,
        out_shape=[_sds((1, n), F32)] * 4, compiler_params=_params(("parallel",)))(gathered, w, m, v)


def _swiglu(ps, es):
    g, u = ps
    return g * _sigmoid(g) * u, g, u


def _swiglu_bwd(ps, es):
    g, u = es[0].astype(F32), es[1].astype(F32)
    sg = _sigmoid(g)
    return ps[0] * u * (sg * (1.0 + g * (1.0 - sg))), ps[0] * (g * sg)


def _merge(ps, es):
    return _sigmoid(es[0]) * ps[0] + _sigmoid(es[1]) * ps[1], ps[0], ps[1]


def _merge_bwd(ps, es):
    a, b, ga, gm = es
    sa, sm = _sigmoid(ga), _sigmoid(gm)
    dm = ps[0]
    return dm * sa, dm * sm, dm * a * (sa * (1.0 - sa)), dm * b * (sm * (1.0 - sm))


W_IN_PIECES = (("q", 512), ("kv", 256), ("mqk", 1024), ("mv", 512), ("mo", 512), ("if", 8),
               ("ga", 1024), ("gm", 1024))


def _local_step(x, tgt, pos_col, mod, sp, W):
    sh_m, sc_m, gate_m, sh_f, sc_f, gate_f = mod
    w_a = jnp.concatenate([W["q"], W["kv"]], axis=0)
    w_m = jnp.concatenate([W["mqk"], W["mv"], W["mo"]], axis=0)
    w_g = jnp.concatenate([W["ga"], W["gm"]], axis=0)

    h = _pre_norm(x, sp["g_pre_mix"], sc_m, sh_m)
    pa, = _mm("proj_attn", [[(h, w_a)]], [], _first, [F32], cn=256, nt=True)
    pm, = _mm("proj_mlstm", [[(h, w_m)]], [], _first, [F32], cn=512, nt=True)
    pif, = _mm("proj_gates", [[(h, W["if"])]], [], _first, [F32], cn=128, nt=True)
    pg, = _mm("proj_branch_gates", [[(h, w_g)]], [], _first, [F32], cn=512, nt=True)
    inv = ROPE_THETA ** (-2.0 * jnp.arange(HEAD_DIM // 2, dtype=F32) / HEAD_DIM)
    cos, sin = _rope_tables(pos_col, jnp.tile(inv, 4).reshape(1, 128))
    ya = _attn_fwd(pa, cos, sin, sp["sinks"])
    qk = _conv_fwd(pm, sp["conv_w"], sp["conv_b"])
    bcol = jnp.pad(sp["b_if"], ((0, 0), (0, 120)))
    brow = jnp.broadcast_to(sp["b_if"].reshape(8, 1), (8, 128))
    grow = pif[:, :8].T
    hm, cs, ns, ms = _mlstm_fwd(qk, pm, pif, bcol, grow, brow)
    ym = _mlstm_out(hm, pm, sp["norm_w"])
    merged, br_a, br_m = _mm("branches", [[(ya, W["ba"])], [(ym, W["bm"])]],
                             [(pg, 0), (pg, 1)], _merge, [BF16, F32, F32], cn=512, nt=True)
    mix, = _mm("mix_out", [[(merged, W["out"])]], [], _first, [F32], cn=512)
    x1, h2 = _res_norm(x, mix, gate_m, sp["g_post_mix"], sp["g_pre_ffn"], sc_f, sh_f)
    act, gt, up = _mm("ffn_in", [[(h2, W["fg"])], [(h2, W["fu"])]], [], _swiglu, [BF16] * 3,
                      cn=256, nt=True)
    ff, = _mm("ffn_down", [[(act, W["fd"])]], [], _first, [F32], cn=512)
    dy, dff, acc_l, loss = _final_loss(x1, ff, tgt, gate_f, sp["g_post_ffn"])

    G = {}
    dgt, dup = _mm("ffn_down_bwd", [[(dff, W["fd"])]], [gt, up], _swiglu_bwd, [BF16, BF16],
                   cn=256, nt=True)
    G["fd"] = _mm_tn("dw_ffn_down", act, dff, BF16, 1408, 512, 512)
    dh2, = _mm("ffn_in_bwd", [[(dgt, W["fg"]), (dup, W["fu"])]], [], _first, [F32], cn=512)
    G["fg"] = _mm_tn("dw_ffn_gate", dgt, h2, BF16, 1408, 1024, 512)
    G["fu"] = _mm_tn("dw_ffn_up", dup, h2, BF16, 1408, 1024, 512)
    dx1, dmix, acc_r = _res_norm_bwd(x1, mix, dh2, dy, sc_f, gate_m, sp["g_pre_ffn"],
                                     sp["g_post_mix"])
    d_a, d_m, dga, dgm = _mm("mix_out_bwd", [[(dmix, W["out"])]],
                             [br_a, br_m, (pg, 0), (pg, 1)], _merge_bwd,
                             [BF16] * 4, cn=512, nt=True)
    G["out"] = _mm_tn("dw_out", merged, dmix, BF16, 1024, 512, 512)
    dya, = _mm("branch_attn_bwd", [[(d_a, W["ba"])]], [], _first, [F32], cn=512)
    dym, = _mm("branch_mlstm_bwd", [[(d_m, W["bm"])]], [], _first, [F32], cn=512)
    G["ba"] = _mm_tn("dw_branch_attn", d_a, ya, BF16, 1024, 512, 512)
    G["bm"] = _mm_tn("dw_branch_mlstm", d_m, ym, BF16, 1024, 512, 512)
    dhm, do_m, acc_n = _mlstm_out_bwd(hm, pm, sp["norm_w"], dym)
    dqk, dv_m, dgc, dgr = _mlstm_bwd(qk, pm, pif, bcol, grow, brow, cs, ns, ms, dhm)
    dif, acc_g = _gate_bwd(dgc, jnp.pad(dgr.T, ((0, 0), (0, 120))), pif, bcol)
    dpre, acc_c = _conv_bwd_pre(pm, sp["conv_w"], sp["conv_b"], dqk)
    du = _conv_bwd_in(dpre, sp["conv_w"])
    dq_a, dcur, dprv, dsink = _attn_bwd(pa, cos, sin, sp["sinks"], dya)
    dkv = _attn_kv_combine(dcur, dprv, cos, sin)
    dproj = {"q": dq_a, "kv": dkv, "mqk": du, "mv": dv_m, "mo": do_m, "if": dif, "ga": dga, "gm": dgm}
    dh, = _mm("proj_bwd", [[(dproj[k], W[k]) for k, _ in W_IN_PIECES]], [], _first, [F32], cn=512)
    for k, _ in W_IN_PIECES:
        G[k] = _mm_tn("dw_in_" + k, dproj[k], h, BF16, dproj[k].shape[1], 1024, 512)
    dx, acc_p = _pre_norm_bwd(x, dh, dx1, sp["g_pre_mix"], sc_m)

    small = {
        "mod": jnp.concatenate([acc_p[1], acc_p[0], acc_r[3], acc_r[1], acc_r[0], acc_l[0]]),
        "g_pre_mix": acc_p[2], "g_post_mix": acc_r[4], "b_if": acc_g[0, :8],
        "conv_w": acc_c[:CONV_WIDTH].reshape(-1), "conv_b": acc_c[CONV_WIDTH],
        "sinks": dsink[:, 0], "norm_w": acc_n[0], "g_pre_ffn": acc_r[2], "g_post_ffn": acc_l[1]}
    return loss, dx, G, small


IN_WIDTH = sum(n for _, n in W_IN_PIECES)
IN_SHARD = IN_WIDTH // 4
IN_SHARD_PAD = -(-IN_SHARD // 32) * 32


def _split_w_in(w_in_t):
    out, off = {}, 0
    for k, n in W_IN_PIECES:
        out[k] = w_in_t[off:off + n]
        off += n
    out["if"] = jnp.pad(out["if"], ((0, 120), (0, 0)))
    return out


def _halves(a):
    return a.reshape(4, 2, a.shape[0] // 8, a.shape[1])


SMALL = (("b_ada", 6144), ("g_pre_mix", 1024), ("g_post_mix", 1024), ("b_if", 128), ("conv_w", 4096),
         ("conv_b", 1024), ("sinks", 128), ("norm_w", 512), ("g_pre_ffn", 1024), ("g_post_ffn", 1024))
SMALL_LEN = 8 * 2048


def _pack_small(vals):
    parts = []
    for k, n in SMALL:
        v = vals[k].reshape(-1)
        parts.append(jnp.pad(v, (0, n - v.shape[0])))
    flat = jnp.concatenate(parts)
    return jnp.pad(flat, (0, SMALL_LEN - flat.shape[0]))


def _unpack_small(flat, shapes):
    out, off = {}, 0
    for k, n in SMALL:
        size = 1
        for d in shapes[k]:
            size *= d
        out[k] = flat[off:off + size].reshape(shapes[k])
        off += n
    return out


def kernel(x, c, positions, w_ada, b_ada, g_pre_mix, g_post_mix, w_in, b_if, conv_w, conv_b, attn_sinks, mlstm_norm_w, w_branch_attn, w_branch_mlstm, w_out, g_pre_ffn, g_post_ffn, w_ffn_gate, w_ffn_up, w_ffn_down, loss_target, m_w_ada, m_b_ada, m_g_pre_mix, m_g_post_mix, m_w_in, m_b_if, m_conv_w, m_conv_b, m_attn_sinks, m_mlstm_norm_w, m_w_branch_attn, m_w_branch_mlstm, m_w_out, m_g_pre_ffn, m_g_post_ffn, m_w_ffn_gate, m_w_ffn_up, m_w_ffn_down, v_w_ada, v_b_ada, v_g_pre_mix, v_g_post_mix, v_w_in, v_b_if, v_conv_w, v_conv_b, v_attn_sinks, v_mlstm_norm_w, v_w_branch_attn, v_w_branch_mlstm, v_w_out, v_g_pre_ffn, v_g_post_ffn, v_w_ffn_gate, v_w_ffn_up, v_w_ffn_down):
    xi, yi, ci = _place()
    chip = 2 * xi + yi
    dev = 2 * chip + ci
    T = x.shape[1]
    ada_cols = w_ada.shape[2]

    blk = jnp.concatenate([c.reshape(-1), conv_w.reshape(-1)]).reshape(8, 256)
    got = _all_gather8("gather_cond", blk, pltpu.VMEM).reshape(8, 2048)
    c_all = got[:, :D_MODEL].astype(BF16)
    conv_full = got[::2, D_MODEL:].reshape(4, CONV_WIDTH, -1).transpose(1, 0, 2).reshape(CONV_WIDTH, -1)

    b_sh = lax.dynamic_slice_in_dim(b_ada, chip * ada_cols, ada_cols, axis=1)
    mod_part, = _mm("ada_mod", [[(c_all, w_ada[0].astype(BF16))]], [b_sh],
                    lambda ps, es: (ps[0] + es[0],), [F32], cn=512, tm=8)
    mod_all = _all_gather8("gather_mod", mod_part, pltpu.VMEM).reshape(4, 2, 8, ada_cols)[:, 0]
    mod = lax.dynamic_index_in_dim(mod_all, dev, axis=1, keepdims=False).reshape(6, 1, D_MODEL)

    def my_half(a):
        n = a.shape[0] // 2
        return lax.dynamic_slice_in_dim(a, ci * n, n, axis=0).astype(BF16)

    w_in_t = jnp.pad(w_in[0].T, ((0, IN_SHARD_PAD - IN_SHARD), (0, 0)))
    blocks = [jnp.stack([my_half(w_ffn_gate[0].T), my_half(w_ffn_up[0].T), my_half(w_ffn_down[0])]),
              my_half(w_in_t)[None], my_half(w_out[0])[None],
              jnp.stack([my_half(w_branch_attn[0].T), my_half(w_branch_mlstm[0].T)])]
    g_ffn, g_in, g_out, g_br = _gather_groups("gather_weights", blocks)
    W = {"fg": g_ffn[:, 0].reshape(D_FF, D_MODEL), "fu": g_ffn[:, 1].reshape(D_FF, D_MODEL),
         "fd": g_ffn[:, 2].reshape(D_FF, D_MODEL), "out": g_out.reshape(D_MODEL, D_MODEL),
         "ba": g_br[:, 0].reshape(D_MODEL, -1), "bm": g_br[:, 1].reshape(D_MODEL, -1)}
    W.update(_split_w_in(g_in.reshape(4, IN_SHARD_PAD, D_MODEL)[:, :IN_SHARD].reshape(IN_WIDTH, D_MODEL)))

    sp = {"g_pre_mix": g_pre_mix, "g_post_mix": g_post_mix, "b_if": b_if, "conv_w": conv_full,
          "conv_b": conv_b, "sinks": attn_sinks, "norm_w": mlstm_norm_w, "g_pre_ffn": g_pre_ffn,
          "g_post_ffn": g_post_ffn}
    loss, dx, G, small = _local_step(x[0], loss_target[0], positions.reshape(T, 1),
                                     [mod[i] for i in range(6)], sp, W)

    g_in_t = jnp.concatenate([G[k][:n] for k, n in W_IN_PIECES]).reshape(4, IN_SHARD, D_MODEL)
    g_in_t = jnp.pad(g_in_t, ((0, 0), (0, IN_SHARD_PAD - IN_SHARD), (0, 0)))
    groups = [jnp.stack([_halves(G["fg"]), _halves(G["fu"]), _halves(G["fd"])], axis=1),
              g_in_t.reshape(4, 1, 2, IN_SHARD_PAD // 2, D_MODEL), _halves(G["out"])[:, None],
              jnp.stack([_halves(G["ba"]), _halves(G["bm"])], axis=1)]
    core = ci.reshape(1).astype(jnp.int32)
    theirs = _swap_sibling("rs_pair", groups, halves=True)
    pairs = [_pair_sum("rs_pair_sum_%d" % i, a, b, core) for i, (a, b) in enumerate(zip(groups, theirs))]
    reds = [_sum4("rs_chip_sum_%d" % i, a) for i, a in enumerate(_swap_chips("rs_chips", pairs))]
    others = _swap_sibling("rs_share", reds)
    s_ffn, s_in, s_out, s_br = [
        jnp.concatenate([jnp.where(ci == 0, r, o), jnp.where(ci == 0, o, r)], axis=1)
        for r, o in zip(reds, others)]
    gsh = {"fg": s_ffn[0].T, "fu": s_ffn[1].T, "fd": s_ffn[2], "w_in": s_in[0, :IN_SHARD].T,
           "out": s_out[0], "ba": s_br[0].T, "bm": s_br[1].T}

    small["b_ada"] = small.pop("mod")
    vec = _pack_small(small).reshape(8, 2048)
    g_all = _all_gather8("gather_small", vec, pltpu.VMEM).reshape(8, SMALL_LEN)
    dmod_sh = lax.dynamic_slice_in_dim(g_all[:, :6 * D_MODEL], chip * ada_cols, ada_cols, axis=1)
    g_w_ada = _mm_tn("dw_ada", c_all, dmod_sh.astype(BF16), F32, D_MODEL, 512, 8)

    smalls = {"b_ada": (b_ada, m_b_ada, v_b_ada), "g_pre_mix": (g_pre_mix, m_g_pre_mix, v_g_pre_mix),
              "g_post_mix": (g_post_mix, m_g_post_mix, v_g_post_mix), "b_if": (b_if, m_b_if, v_b_if),
              "conv_w": None, "conv_b": (conv_b, m_conv_b, v_conv_b),
              "sinks": (attn_sinks, m_attn_sinks, v_attn_sinks),
              "norm_w": (mlstm_norm_w, m_mlstm_norm_w, v_mlstm_norm_w),
              "g_pre_ffn": (g_pre_ffn, m_g_pre_ffn, v_g_pre_ffn),
              "g_post_ffn": (g_post_ffn, m_g_post_ffn, v_g_post_ffn)}
    shapes = {k: (t[0].shape if t is not None else (1, CONV_WIDTH, D_MODEL)) for k, t in smalls.items()}
    zeros = jnp.zeros((CONV_WIDTH * D_MODEL,), F32)
    packs = [_pack_small({k: (t[i] if t is not None else zeros) for k, t in smalls.items()}).reshape(1, -1)
             for i in range(3)]
    s_out = [_unpack_small(o[0], shapes) for o in _small_update(g_all, *packs)]
    g_conv = lax.dynamic_slice_in_dim(s_out[0]["conv_w"], chip * conv_w.shape[2], conv_w.shape[2], axis=2)

    res = {}
    for k, t in smalls.items():
        if t is not None:
            res[k] = tuple(o[k] for o in s_out)
    res["conv_w"] = (g_conv, *[o[None] for o in _adamw("adam_conv_w", conv_w[0], g_conv[0], m_conv_w[0], v_conv_w[0])])
    res["w_ada"] = (g_w_ada[None], *[o[None] for o in _adamw("adam_w_ada", w_ada[0], g_w_ada, m_w_ada[0], v_w_ada[0])])
    bigs = {"w_in": (w_in, m_w_in, v_w_in), "ba": (w_branch_attn, m_w_branch_attn, v_w_branch_attn),
            "bm": (w_branch_mlstm, m_w_branch_mlstm, v_w_branch_mlstm), "out": (w_out, m_w_out, v_w_out),
            "fg": (w_ffn_gate, m_w_ffn_gate, v_w_ffn_gate), "fu": (w_ffn_up, m_w_ffn_up, v_w_ffn_up),
            "fd": (w_ffn_down, m_w_ffn_down, v_w_ffn_down)}
    for k, (w, m, v) in bigs.items():
        res[k] = (gsh[k][None], *[o[None] for o in _adamw("adam_" + k, w[0], gsh[k], m[0], v[0])])

    order = ("w_ada", "b_ada", "g_pre_mix", "g_post_mix", "w_in", "b_if", "conv_w", "conv_b", "sinks",
             "norm_w", "ba", "bm", "out", "g_pre_ffn", "g_post_ffn", "fg", "fu", "fd")
    total = lax.psum(loss[0, 0], ("x", "y", "c"))
    return (total, dx[None], *[res[k][0] for k in order], *[res[k][1] for k in order],
            *[res[k][2] for k in order], *[res[k][3] for k in order])
```

```python
import functools

import jax
import jax.numpy as jnp
from jax import lax
from jax.experimental import pallas as pl
from jax.experimental.pallas import tpu as pltpu

F32, BF16 = jnp.float32, jnp.bfloat16
MESH = pl.DeviceIdType.MESH

D_MODEL = 1024
N_Q_HEADS, N_KV_HEADS, HEAD_DIM, WINDOW = 8, 2, 64, 128
ROPE_THETA = 10000.0
MLSTM_HEADS, MLSTM_HEAD_DIM, MLSTM_CHUNK, CONV_WIDTH = 4, 128, 64, 4
D_FF = 2816
NORM_EPS = 1e-6
ADAM_LR, ADAM_B1, ADAM_B2, ADAM_EPS, ADAM_WD, ADAM_STEP = 0.001, 0.9, 0.999, 1e-08, 0.01, 10

VMEM_LIMIT = 56 * 1024 * 1024
ROW_TILE = 256
MM_TM = 512
ATTN_BLK = WINDOW
STEP_ROWS = 2 * MLSTM_CHUNK
NEG_INF = float("-inf")


def _params(sem):
    return pltpu.CompilerParams(dimension_semantics=sem, vmem_limit_bytes=VMEM_LIMIT)


def _sds(shape, dtype):
    return jax.ShapeDtypeStruct(shape, dtype)


def _sigmoid(x):
    return 1.0 / (1.0 + jnp.exp(-x))


def _dot(a, b, ca, cb):
    return lax.dot_general(a, b, (((ca,), (cb,)), ((), ())), preferred_element_type=F32)


def _bdot(a, b, ca, cb):
    return lax.dot_general(a, b, (((ca,), (cb,)), ((0,), (0,))), preferred_element_type=F32)


def _bdot_rows(a, b):
    return jnp.stack([_dot(a[h], b[h], 0, 0) for h in range(a.shape[0])])


def _mm(name, prods, extras, epi, out_dtypes, cn, nt=False, tm=MM_TM):
    flat = [ab for p in prods for ab in p]
    counts = [len(p) for p in prods]
    M = flat[0][0].shape[0]
    N = flat[0][1].shape[0 if nt else 1]
    tm = min(tm, M)
    n_in = 2 * len(flat) + len(extras)

    def body(*refs):
        ins, outs = refs[:n_in], refs[n_in:]
        for j in range(N // cn):
            cols = slice(j * cn, (j + 1) * cn)
            k, ps = 0, []
            for cnt in counts:
                acc = None
                for _ in range(cnt):
                    b = ins[k + 1][cols, :] if nt else ins[k + 1][:, cols]
                    d = _dot(ins[k][...], b, 1, 1 if nt else 0)
                    acc = d if acc is None else acc + d
                    k += 2
                ps.append(acc)
            res = epi(ps, [r[:, cols] for r in ins[k:]])
            for o, r in zip(outs, res):
                o[:, cols] = r.astype(o.dtype)

    in_specs, args = [], []
    for a, b in flat:
        in_specs.append(pl.BlockSpec((tm, a.shape[1]), lambda i: (i, 0)))
        in_specs.append(pl.BlockSpec(b.shape, lambda i: (0, 0), pipeline_mode=pl.Buffered(1)))
        args += [a, b]
    for e in extras:
        e, off = e if isinstance(e, tuple) else (e, 0)
        rows = 1 if e.shape[0] == 1 else tm
        in_specs.append(pl.BlockSpec((rows, N), lambda i, off=off, rows=rows: (0 if rows == 1 else i, off)))
        args.append(e)
    return pl.pallas_call(
        body, name=name, grid=(M // tm,), in_specs=in_specs,
        out_specs=[pl.BlockSpec((tm, N), lambda i: (i, 0)) for _ in out_dtypes],
        out_shape=[_sds((M, N), dt) for dt in out_dtypes],
        compiler_params=_params(("parallel",)))(*args)


def _mm_tn(name, a, b, out_dtype, tk, tn, tt):
    T, Ka = a.shape
    N = b.shape[1]
    tt = min(tt, T)
    steps = T // tt

    def body(a_ref, b_ref, o_ref, acc):
        t = pl.program_id(2)

        @pl.when(t == 0)
        def _():
            acc[...] = jnp.zeros_like(acc)

        acc[...] += _dot(a_ref[...], b_ref[...], 0, 0)

        @pl.when(t == steps - 1)
        def _():
            o_ref[...] = acc[...].astype(o_ref.dtype)

    return pl.pallas_call(
        body, name=name, grid=(Ka // tk, N // tn, steps),
        in_specs=[pl.BlockSpec((tt, tk), lambda i, j, t: (t, i)),
                  pl.BlockSpec((tt, tn), lambda i, j, t: (t, j))],
        out_specs=pl.BlockSpec((tk, tn), lambda i, j, t: (i, j)),
        out_shape=_sds((Ka, N), out_dtype),
        scratch_shapes=[pltpu.VMEM((tk, tn), F32)],
        compiler_params=_params(("parallel", "parallel", "arbitrary")))(a, b)


def _first(ps, es):
    return (ps[0],)


def _rows(name, body, ins, out_shapes, T, tr=ROW_TILE):
    tr = min(tr, T)

    def spec(shape):
        if shape[0] == T:
            return pl.BlockSpec((tr,) + tuple(shape[1:]), lambda i: (i,) + (0,) * (len(shape) - 1))
        return pl.BlockSpec(tuple(shape), lambda i: (0,) * len(shape))

    return pl.pallas_call(
        body, name=name, grid=(T // tr,),
        in_specs=[spec(a.shape) for a in ins], out_specs=[spec(s.shape) for s in out_shapes],
        out_shape=out_shapes, compiler_params=_params(("arbitrary",)))(*ins)


def _rms(x):
    r = lax.rsqrt(jnp.mean(x * x, axis=-1, keepdims=True) + NORM_EPS)
    return x * r, r


def _rms_bwd(dxn, xn, r):
    return r * (dxn - xn * jnp.mean(dxn * xn, axis=-1, keepdims=True))


def _colsum(v):
    return jnp.sum(v, axis=0, keepdims=True)


def _pre_norm(x, g, sc, sh):
    T = x.shape[0]

    def body(x_ref, g_ref, sc_ref, sh_ref, h_ref):
        xn, _ = _rms(x_ref[...])
        h_ref[...] = (xn * g_ref[...] * (1.0 + sc_ref[...]) + sh_ref[...]).astype(BF16)

    return _rows("pre_norm", body, [x, g, sc, sh], [_sds((T, D_MODEL), BF16)], T)[0]


def _res_norm(x, mix, gate, gpost, g2, sc2, sh2):
    T = x.shape[0]

    def body(x_ref, mix_ref, gate_ref, gp_ref, g2_ref, sc_ref, sh_ref, x1_ref, h2_ref):
        mh, _ = _rms(mix_ref[...])
        x1 = x_ref[...] + gate_ref[...] * (mh * gp_ref[...])
        x1_ref[...] = x1
        xn, _ = _rms(x1)
        h2_ref[...] = (xn * g2_ref[...] * (1.0 + sc_ref[...]) + sh_ref[...]).astype(BF16)

    return _rows("res_norm", body, [x, mix, gate, gpost, g2, sc2, sh2],
                 [_sds((T, D_MODEL), F32), _sds((T, D_MODEL), BF16)], T)


def _final_loss(x1, ff, tgt, gate, gpost):
    T = x1.shape[0]

    def body(x1_ref, ff_ref, t_ref, gate_ref, gp_ref, dy_ref, dff_ref, acc_ref, loss_ref):
        @pl.when(pl.program_id(0) == 0)
        def _():
            acc_ref[...] = jnp.zeros_like(acc_ref)
            loss_ref[...] = jnp.zeros_like(loss_ref)

        fh, r = _rms(ff_ref[...])
        gate, gp = gate_ref[...], gp_ref[...]
        e = x1_ref[...] + gate * (fh * gp) - t_ref[...]
        loss_ref[...] += 0.5 * jnp.sum(jnp.mean(e * e, axis=-1, keepdims=True))
        dy = e * (1.0 / D_MODEL)
        dy_ref[...] = dy
        acc_ref[0:1, :] += _colsum(dy * fh * gp)
        acc_ref[1:2, :] += _colsum(dy * gate * fh)
        dff_ref[...] = _rms_bwd(dy * gate * gp, fh, r).astype(BF16)

    return _rows("final_loss", body, [x1, ff, tgt, gate, gpost],
                 [_sds((T, D_MODEL), F32), _sds((T, D_MODEL), BF16),
                  _sds((8, D_MODEL), F32), _sds((1, 128), F32)], T)


def _res_norm_bwd(x1, mix, dh2, dy, sc2, gate, g2, gpost):
    T = x1.shape[0]

    def body(x1_ref, mix_ref, dh_ref, dy_ref, sc_ref, gate_ref, g2_ref, gp_ref,
             dx1_ref, dmix_ref, acc_ref):
        @pl.when(pl.program_id(0) == 0)
        def _():
            acc_ref[...] = jnp.zeros_like(acc_ref)

        xn, r1 = _rms(x1_ref[...])
        dh, sc, g2 = dh_ref[...], sc_ref[...], g2_ref[...]
        acc_ref[0:1, :] += _colsum(dh * xn * g2)
        acc_ref[1:2, :] += _colsum(dh)
        acc_ref[2:3, :] += _colsum(dh * (1.0 + sc) * xn)
        dx1 = dy_ref[...] + _rms_bwd(dh * (1.0 + sc) * g2, xn, r1)
        dx1_ref[...] = dx1
        mh, rm = _rms(mix_ref[...])
        gate, gp = gate_ref[...], gp_ref[...]
        acc_ref[3:4, :] += _colsum(dx1 * mh * gp)
        acc_ref[4:5, :] += _colsum(dx1 * gate * mh)
        dmix_ref[...] = _rms_bwd(dx1 * gate * gp, mh, rm).astype(BF16)

    return _rows("res_norm_bwd", body, [x1, mix, dh2, dy, sc2, gate, g2, gpost],
                 [_sds((T, D_MODEL), F32), _sds((T, D_MODEL), BF16), _sds((8, D_MODEL), F32)], T)


def _pre_norm_bwd(x, dh, dx1, g, sc):
    T = x.shape[0]

    def body(x_ref, dh_ref, dx1_ref, g_ref, sc_ref, dx_ref, acc_ref):
        @pl.when(pl.program_id(0) == 0)
        def _():
            acc_ref[...] = jnp.zeros_like(acc_ref)

        xn, r = _rms(x_ref[...])
        dh, sc, g = dh_ref[...], sc_ref[...], g_ref[...]
        acc_ref[0:1, :] += _colsum(dh * xn * g)
        acc_ref[1:2, :] += _colsum(dh)
        acc_ref[2:3, :] += _colsum(dh * (1.0 + sc) * xn)
        dx_ref[...] = dx1_ref[...] + _rms_bwd(dh * (1.0 + sc) * g, xn, r)

    return _rows("pre_norm_bwd", body, [x, dh, dx1, g, sc],
                 [_sds((T, D_MODEL), F32), _sds((8, D_MODEL), F32)], T)


def _rope_tables(pos_col, inv_freq):
    T = pos_col.shape[0]

    def body(p_ref, f_ref, c_ref, s_ref):
        ang = p_ref[...].astype(F32) * f_ref[...]
        lane = lax.broadcasted_iota(jnp.int32, ang.shape, 1)
        c_ref[...] = jnp.cos(ang)
        s_ref[...] = jnp.where(lane % HEAD_DIM < HEAD_DIM // 2, -1.0, 1.0) * jnp.sin(ang)

    return _rows("rope_tables", body, [pos_col, inv_freq],
                 [_sds((T, 128), F32), _sds((T, 128), F32)], T, tr=512)


def _swap_halves(t):
    W = t.shape[1]
    lane = lax.broadcasted_iota(jnp.int32, t.shape, 1)
    half = HEAD_DIM // 2
    return jnp.where(lane % HEAD_DIM < half, pltpu.roll(t, W - half, 1), pltpu.roll(t, half, 1))


def _widen(c, W):
    return c if W == 128 else jnp.concatenate([c] * (W // 128), axis=1)


def _rope(t, c, s):
    W = t.shape[1]
    return t * _widen(c, W) + _swap_halves(t) * _widen(s, W)


def _unrope(dy, c, s):
    W = dy.shape[1]
    return dy * _widen(c, W) + _swap_halves(dy * _widen(s, W))


def _attn_mask(n):
    qi = lax.broadcasted_iota(jnp.int32, (ATTN_BLK, 2 * ATTN_BLK), 0)
    kj = lax.broadcasted_iota(jnp.int32, (ATTN_BLK, 2 * ATTN_BLK), 1)
    rel = kj - ATTN_BLK
    return (rel <= qi) & (qi - rel < WINDOW) & ((n > 0) | (kj >= ATTN_BLK))


def _attn_load(cur, prv, cc, sc, cp, sp):
    x, xp = cur[...], prv[...]
    q = _rope(x[:, :512], cc[...], sc[...]) * (HEAD_DIM ** -0.5)
    k = jnp.concatenate([_rope(xp[:, 512:640], cp[...], sp[...]),
                         _rope(x[:, 512:640], cc[...], sc[...])], axis=0)
    v = jnp.concatenate([xp[:, 640:768], x[:, 640:768]], axis=0)
    return q, k, v


ROLLED = tuple(h for h in range(N_Q_HEADS) if h % 2 != h // (N_Q_HEADS // N_KV_HEADS))


def _pair_heads(t):
    half = lax.broadcasted_iota(jnp.int32, (ATTN_BLK, 128), 1) // HEAD_DIM
    return jnp.stack([jnp.where(half == h % 2, t[:, 128 * (h // 2):128 * (h // 2) + 128], 0.0)
                      for h in range(N_Q_HEADS)])


def _kv_heads(t):
    half = lax.broadcasted_iota(jnp.int32, t.shape, 1) // HEAD_DIM
    tr = pltpu.roll(t, HEAD_DIM, 1)
    return jnp.stack([jnp.where(half == h % 2, tr if h in ROLLED else t, 0.0)
                      for h in range(N_Q_HEADS)])


def _sink_column(snk):
    return jnp.stack([jnp.full((1, 1), snk[0, h], F32) for h in range(N_Q_HEADS)])


def _attn_probs(qh, kh, mask, sink):
    s = jnp.where(mask, _bdot(qh, kh, 2, 2), NEG_INF)
    m = jnp.maximum(jnp.max(s, axis=-1, keepdims=True), sink)
    p = jnp.exp(s - m)
    es = jnp.exp(sink - m)
    rl = 1.0 / (jnp.sum(p, axis=-1, keepdims=True) + es)
    return p, es, rl


def _attn_specs(nb):
    blk = lambda w: pl.BlockSpec((ATTN_BLK, w), lambda n: (n, 0))
    prv = lambda w: pl.BlockSpec((ATTN_BLK, w), lambda n: (jnp.maximum(n - 1, 0), 0))
    return [blk(768), prv(768), blk(128), blk(128), prv(128), prv(128),
            pl.BlockSpec(memory_space=pltpu.SMEM)]


def _attn_fwd(pa, cos, sin, sinks):
    T = pa.shape[0]
    nb = T // ATTN_BLK

    def body(cur, prv, cc, sc, cp, sp, snk, y_ref):
        n = pl.program_id(0)
        q, k, v = _attn_load(cur, prv, cc, sc, cp, sp)
        qh, kh, vh = _pair_heads(q).astype(BF16), _kv_heads(k).astype(BF16), _kv_heads(v).astype(BF16)
        p, _, rl = _attn_probs(qh, kh, _attn_mask(n), _sink_column(snk))
        o = _bdot(p.astype(BF16), vh, 2, 1) * rl
        for pair in range(N_Q_HEADS // 2):
            y_ref[:, 128 * pair:128 * pair + 128] = (o[2 * pair] + o[2 * pair + 1]).astype(BF16)

    return pl.pallas_call(
        body, name="attn_fwd", grid=(nb,), in_specs=_attn_specs(nb),
        out_specs=pl.BlockSpec((ATTN_BLK, 512), lambda n: (n, 0)),
        out_shape=_sds((T, 512), BF16), compiler_params=_params(("parallel",)))(
            pa, pa, cos, sin, cos, sin, sinks)


def _attn_bwd(pa, cos, sin, sinks, dy):
    T = pa.shape[0]
    nb = T // ATTN_BLK

    def body(cur, prv, cc, sc, cp, sp, snk, dy_ref, dq_ref, dcur_ref, dprv_ref, dsink_ref):
        n = pl.program_id(0)

        @pl.when(n == 0)
        def _():
            dsink_ref[...] = jnp.zeros_like(dsink_ref)

        q, k, v = _attn_load(cur, prv, cc, sc, cp, sp)
        qh, kh, vh = _pair_heads(q).astype(BF16), _kv_heads(k).astype(BF16), _kv_heads(v).astype(BF16)
        p, es, rl = _attn_probs(qh, kh, _attn_mask(n), _sink_column(snk))
        pn = p * rl
        do = _pair_heads(dy_ref[...]).astype(BF16)
        dp = _bdot(do, vh, 2, 2)
        delta = jnp.sum(pn * dp, axis=-1, keepdims=True)
        ds = (pn * (dp - delta)).astype(BF16)
        dsink = es * rl * delta
        dq = _bdot(ds, kh, 2, 1) * (HEAD_DIM ** -0.5)
        dkh = _bdot_rows(ds, qh)
        dvh = _bdot_rows(pn.astype(BF16), do)

        def fold(t):
            same = [t[h] for h in range(N_Q_HEADS) if h not in ROLLED]
            moved = [t[h] for h in ROLLED]
            return sum(same[1:], same[0]) + pltpu.roll(sum(moved[1:], moved[0]), HEAD_DIM, 1)

        dk, dv = fold(dkh), fold(dvh)
        for h in range(N_Q_HEADS):
            dsink_ref[h:h + 1, :] += -jnp.sum(dsink[h])
        for pair in range(N_Q_HEADS // 2):
            dq_ref[:, 128 * pair:128 * pair + 128] = _unrope(
                dq[2 * pair] + dq[2 * pair + 1], cc[...], sc[...]).astype(BF16)
        dcur_ref[:, 0:128] = dk[ATTN_BLK:]
        dcur_ref[:, 128:256] = dv[ATTN_BLK:]
        dprv_ref[:, 0:128] = dk[:ATTN_BLK]
        dprv_ref[:, 128:256] = dv[:ATTN_BLK]

    blk = lambda w: pl.BlockSpec((ATTN_BLK, w), lambda n: (n, 0))
    return pl.pallas_call(
        body, name="attn_bwd", grid=(nb,), in_specs=_attn_specs(nb) + [blk(512)],
        out_specs=[blk(512), blk(256), blk(256), pl.BlockSpec((8, 128), lambda n: (0, 0))],
        out_shape=[_sds((T, 512), BF16), _sds((T, 256), F32), _sds((T, 256), F32),
                   _sds((8, 128), F32)],
        compiler_params=_params(("arbitrary",)))(pa, pa, cos, sin, cos, sin, sinks, dy)


def _attn_kv_combine(dcur, dprv, cos, sin):
    T = dcur.shape[0]
    nb = T // ATTN_BLK

    def body(c_ref, p_ref, cc, sc, o_ref):
        n = pl.program_id(0)
        t = c_ref[...] + jnp.where(n < nb - 1, p_ref[...], 0.0)
        o_ref[:, 0:128] = _unrope(t[:, 0:128], cc[...], sc[...]).astype(BF16)
        o_ref[:, 128:256] = t[:, 128:256].astype(BF16)

    blk = lambda w: pl.BlockSpec((ATTN_BLK, w), lambda n: (n, 0))
    nxt = pl.BlockSpec((ATTN_BLK, 256), lambda n: (jnp.minimum(n + 1, nb - 1), 0))
    return pl.pallas_call(
        body, name="attn_kv_combine", grid=(nb,), in_specs=[blk(256), nxt, blk(128), blk(128)],
        out_specs=blk(256), out_shape=_sds((T, 256), BF16),
        compiler_params=_params(("parallel",)))(dcur, dprv, cos, sin)


CONV_COLS = 2 * MLSTM_HEADS * MLSTM_HEAD_DIM


def _conv_pre(cur_ref, halo_ref, w_ref, b_ref, i, tr):
    xx = jnp.concatenate([jnp.where(i > 0, halo_ref[...], 0.0), cur_ref[...]], axis=0)
    taps = [(pltpu.roll(xx, CONV_WIDTH - 1 - j, 0) if j < CONV_WIDTH - 1 else xx)[8:8 + tr]
            for j in range(CONV_WIDTH)]
    pre = b_ref[...]
    for j in range(CONV_WIDTH):
        pre = pre + taps[j] * w_ref[j:j + 1, :]
    return pre, taps


def _conv_specs(T, tr):
    return [pl.BlockSpec((tr, CONV_COLS), lambda i: (i, 0)),
            pl.BlockSpec((8, CONV_COLS), lambda i: (jnp.maximum(i * (tr // 8) - 1, 0), 0)),
            pl.BlockSpec((CONV_WIDTH, CONV_COLS), lambda i: (0, 0)),
            pl.BlockSpec((1, CONV_COLS), lambda i: (0, 0))]


def _conv_fwd(pm, w, b):
    T = pm.shape[0]
    tr = min(ROW_TILE, T)

    def body(cur_ref, halo_ref, w_ref, b_ref, o_ref):
        pre, _ = _conv_pre(cur_ref, halo_ref, w_ref, b_ref, pl.program_id(0), tr)
        o_ref[...] = pre * _sigmoid(pre)

    return pl.pallas_call(
        body, name="conv_fwd", grid=(T // tr,), in_specs=_conv_specs(T, tr),
        out_specs=pl.BlockSpec((tr, CONV_COLS), lambda i: (i, 0)),
        out_shape=_sds((T, CONV_COLS), F32), compiler_params=_params(("parallel",)))(pm, pm, w, b)


def _conv_bwd_pre(pm, w, b, dqk):
    T = pm.shape[0]
    tr = min(ROW_TILE, T)

    def body(cur_ref, halo_ref, w_ref, b_ref, d_ref, dpre_ref, acc_ref):
        i = pl.program_id(0)

        @pl.when(i == 0)
        def _():
            acc_ref[...] = jnp.zeros_like(acc_ref)

        pre, taps = _conv_pre(cur_ref, halo_ref, w_ref, b_ref, i, tr)
        sg = _sigmoid(pre)
        dpre = d_ref[...] * (sg * (1.0 + pre * (1.0 - sg)))
        dpre_ref[...] = dpre
        for j in range(CONV_WIDTH):
            acc_ref[j:j + 1, :] += _colsum(dpre * taps[j])
        acc_ref[CONV_WIDTH:CONV_WIDTH + 1, :] += _colsum(dpre)

    return pl.pallas_call(
        body, name="conv_bwd_pre", grid=(T // tr,),
        in_specs=_conv_specs(T, tr) + [pl.BlockSpec((tr, CONV_COLS), lambda i: (i, 0))],
        out_specs=[pl.BlockSpec((tr, CONV_COLS), lambda i: (i, 0)),
                   pl.BlockSpec((8, CONV_COLS), lambda i: (0, 0))],
        out_shape=[_sds((T, CONV_COLS), F32), _sds((8, CONV_COLS), F32)],
        compiler_params=_params(("arbitrary",)))(pm, pm, w, b, dqk)


def _conv_bwd_in(dpre, w):
    T = dpre.shape[0]
    tr = min(ROW_TILE, T)
    nt = T // tr

    def body(cur_ref, halo_ref, w_ref, o_ref):
        i = pl.program_id(0)
        yy = jnp.concatenate([cur_ref[...], jnp.where(i < nt - 1, halo_ref[...], 0.0)], axis=0)
        du = cur_ref[...] * w_ref[CONV_WIDTH - 1:CONV_WIDTH, :]
        for j in range(CONV_WIDTH - 1):
            k = CONV_WIDTH - 1 - j
            du = du + pltpu.roll(yy, tr + 8 - k, 0)[:tr] * w_ref[j:j + 1, :]
        o_ref[...] = du.astype(BF16)

    return pl.pallas_call(
        body, name="conv_bwd_in", grid=(nt,),
        in_specs=[pl.BlockSpec((tr, CONV_COLS), lambda i: (i, 0)),
                  pl.BlockSpec((8, CONV_COLS),
                               lambda i: (jnp.minimum((i + 1) * (tr // 8), T // 8 - 1), 0)),
                  pl.BlockSpec((CONV_WIDTH, CONV_COLS), lambda i: (0, 0))],
        out_specs=pl.BlockSpec((tr, CONV_COLS), lambda i: (i, 0)),
        out_shape=_sds((T, CONV_COLS), BF16), compiler_params=_params(("parallel",)))(dpre, dpre, w)


def _log_sigmoid(x):
    return jnp.minimum(x, 0.0) - jnp.log1p(jnp.exp(-jnp.abs(x)))


def _chunk_cumsum(x, axis):
    idx = lax.broadcasted_iota(jnp.int32, x.shape, axis) % MLSTM_CHUNK
    k = 1
    while k < MLSTM_CHUNK:
        x = x + jnp.where(idx >= k, pltpu.roll(x, k, axis), 0.0)
        k *= 2
    return x


def _chunk_rev_cumsum(x, axis):
    n = x.shape[axis]
    idx = lax.broadcasted_iota(jnp.int32, x.shape, axis) % MLSTM_CHUNK
    k = 1
    while k < MLSTM_CHUNK:
        x = x + jnp.where(idx < MLSTM_CHUNK - k, pltpu.roll(x, n - k, axis), 0.0)
        k *= 2
    return x


def _mlstm_gates(gc_ref, bc_ref, gr_ref, br_ref):
    gc = gc_ref[...] + bc_ref[...]
    gr = gr_ref[...] + br_ref[...]
    return gc, _chunk_cumsum(_log_sigmoid(gc), 0), gr, _chunk_cumsum(_log_sigmoid(gr), 1)


def _heads(ref, base=0):
    D = MLSTM_HEAD_DIM
    return jnp.stack([ref[:, base + D * h:base + D * h + D] for h in range(MLSTM_HEADS)])


def _mlstm_inputs(q_ref, k_ref, v_ref, gc, bc, gr, br):
    H = MLSTM_HEADS
    q, v = _heads(q_ref), _heads(v_ref)
    ks = _heads(k_ref) * (MLSTM_HEAD_DIM ** -0.5)
    return dict(
        q=q, ks=ks, qb=q.astype(BF16), kb=ks.astype(BF16), vb=v.astype(BF16),
        b_col=jnp.stack([bc[:, H + h:H + h + 1] for h in range(H)]),
        i_col=jnp.stack([gc[:, h:h + 1] for h in range(H)]),
        b_row=jnp.stack([br[H + h:H + h + 1, :] for h in range(H)]),
        i_row=jnp.stack([gr[h:h + 1, :] for h in range(H)]))


def _mlstm_head(f, c_prev, n_prev, m_prev):
    L = MLSTM_CHUNK
    q, qb = f["q"], f["qb"]
    t = lax.broadcasted_iota(jnp.int32, (1, 2 * L, 2 * L), 1)
    s = lax.broadcasted_iota(jnp.int32, (1, 2 * L, 2 * L), 2)
    mask = (t // L == s // L) & (s <= t)
    d = jnp.where(mask, f["b_col"] - f["b_row"] + f["i_row"], NEG_INF)
    row = lax.broadcasted_iota(jnp.int32, (1, 2 * L, 1), 1)
    inter = f["b_col"] + jnp.where(row < L, m_prev[0], m_prev[1])
    m_t = jnp.maximum(inter, jnp.max(d, axis=-1, keepdims=True))
    w_intra = jnp.exp(d - m_t)
    w_inter = jnp.exp(inter - m_t)
    sc = _bdot(qb, f["kb"], 2, 2) * w_intra
    qc = jnp.concatenate([_bdot(qb[:, :L], c_prev[0].astype(BF16), 2, 1),
                          _bdot(qb[:, L:], c_prev[1].astype(BF16), 2, 1)], axis=1)
    qn = jnp.concatenate([jnp.sum(q[:, :L] * n_prev[0], axis=-1, keepdims=True),
                          jnp.sum(q[:, L:] * n_prev[1], axis=-1, keepdims=True)], axis=1)
    num = _bdot(sc.astype(BF16), f["vb"], 2, 1) + w_inter * qc
    den = jnp.sum(sc, axis=-1, keepdims=True) + w_inter * qn
    return dict(f, w_intra=w_intra, w_inter=w_inter, sc=sc, qc=qc, qn=qn, num=num, den=den,
                floor=jnp.exp(-m_t))


def _mlstm_update(f, ch, c, n, m):
    L = MLSTM_CHUNK
    rows = slice(L * ch, L * ch + L)
    b_col = f["b_col"][:, rows]
    g_last = b_col[:, L - 1:L]
    a_col = g_last - b_col + f["i_col"][:, rows]
    m_new = jnp.maximum(g_last + m, jnp.max(a_col, axis=1, keepdims=True))
    decay = jnp.exp(g_last + m - m_new)
    e_a = jnp.exp(a_col - m_new)
    kw = f["ks"][:, rows] * e_a
    c_new = decay * c + _bdot_rows(kw.astype(BF16), f["vb"][:, rows])
    n_new = decay * n + jnp.sum(kw, axis=1, keepdims=True)
    return c_new, n_new, m_new, decay, e_a, kw


def _mlstm_specs(T, order):
    blk = lambda w, col: pl.BlockSpec((STEP_ROWS, w), lambda s: (order(s), col))
    return [blk(512, 0), blk(512, 1), blk(512, 2), blk(128, 0),
            pl.BlockSpec((1, 128), lambda s: (0, 0)),
            pl.BlockSpec((8, STEP_ROWS), lambda s: (0, order(s))),
            pl.BlockSpec((8, 128), lambda s: (0, 0))]


def _lanes(m):
    return jnp.broadcast_to(m, m.shape[:-1] + (128,))


def _mlstm_fwd(qk, pm, gcol, bcol, grow, brow):
    T = qk.shape[0]
    steps = T // STEP_ROWS
    H, D = MLSTM_HEADS, MLSTM_HEAD_DIM

    def body(q_ref, k_ref, v_ref, gc_ref, bc_ref, gr_ref, br_ref, h_ref, cs_ref, ns_ref, ms_ref,
             c_st, n_st, m_st):
        @pl.when(pl.program_id(0) == 0)
        def _():
            c_st[...] = jnp.zeros_like(c_st)
            n_st[...] = jnp.zeros_like(n_st)
            m_st[...] = jnp.zeros_like(m_st)

        f = _mlstm_inputs(q_ref, k_ref, v_ref, *_mlstm_gates(gc_ref, bc_ref, gr_ref, br_ref))
        c0, n0, m0 = c_st[...], n_st[...], m_st[:, :, 0:1]
        c1, n1, m1, _, _, _ = _mlstm_update(f, 0, c0, n0, m0)
        c2, n2, m2, _, _, _ = _mlstm_update(f, 1, c1, n1, m1)
        f = _mlstm_head(f, (c0, c1), (n0, n1), (m0, m1))
        h = f["num"] / jnp.maximum(jnp.abs(f["den"]), f["floor"])
        for hd in range(H):
            h_ref[:, D * hd:D * hd + D] = h[hd]
        cs_ref[0], cs_ref[1] = c0, c1
        ns_ref[0], ns_ref[1] = n0, n1
        ms_ref[0], ms_ref[1] = _lanes(m0), _lanes(m1)
        c_st[...], n_st[...], m_st[...] = c2, n2, _lanes(m2)

    vec = pl.BlockSpec((2, H, 1, 128), lambda s: (s, 0, 0, 0))
    return pl.pallas_call(
        body, name="mlstm_fwd", grid=(steps,), in_specs=_mlstm_specs(T, lambda s: s),
        out_specs=[pl.BlockSpec((STEP_ROWS, 512), lambda s: (s, 0)),
                   pl.BlockSpec((2, H, 128, 128), lambda s: (s, 0, 0, 0)), vec, vec],
        out_shape=[_sds((T, 512), F32), _sds((2 * steps, H, 128, 128), F32),
                   _sds((2 * steps, H, 1, 128), F32), _sds((2 * steps, H, 1, 128), F32)],
        scratch_shapes=[pltpu.VMEM((H, 128, 128), F32), pltpu.VMEM((H, 1, 128), F32),
                        pltpu.VMEM((H, 1, 128), F32)],
        compiler_params=_params(("arbitrary",)))(qk, qk, pm, gcol, bcol, grow, brow)


def _mlstm_bwd(qk, pm, gcol, bcol, grow, brow, cs, ns, ms, dh):
    T = qk.shape[0]
    steps = T // STEP_ROWS
    H, L, D = MLSTM_HEADS, MLSTM_CHUNK, MLSTM_HEAD_DIM
    rev = lambda s: steps - 1 - s

    def body(q_ref, k_ref, v_ref, gc_ref, bc_ref, gr_ref, br_ref, cs_ref, ns_ref, ms_ref, dh_ref,
             dqk_ref, dv_ref, dgc_ref, dgr_ref, dc_st, dn_st):
        @pl.when(pl.program_id(0) == 0)
        def _():
            dc_st[...] = jnp.zeros_like(dc_st)
            dn_st[...] = jnp.zeros_like(dn_st)

        f = _mlstm_inputs(q_ref, k_ref, v_ref, *_mlstm_gates(gc_ref, bc_ref, gr_ref, br_ref))
        c_prev = (cs_ref[0], cs_ref[1])
        n_prev = (ns_ref[0], ns_ref[1])
        m_prev = (ms_ref[0, :, :, 0:1], ms_ref[1, :, :, 0:1])
        f = _mlstm_head(f, c_prev, n_prev, m_prev)
        big = jnp.abs(f["den"]) > f["floor"]
        rden = 1.0 / jnp.where(big, jnp.abs(f["den"]), f["floor"])
        dnum = _heads(dh_ref) * rden
        hdh = jnp.sum(f["num"] * dnum, axis=-1, keepdims=True)
        dden = jnp.where(big, -hdh * rden * jnp.sign(f["den"]), 0.0)
        dnum_b = dnum.astype(BF16)
        dsc = _bdot(dnum_b, f["vb"], 2, 2) + dden
        g = dsc * f["sc"]
        dv = _bdot_rows(f["sc"].astype(BF16), dnum_b)
        dqk_ = (dsc * f["w_intra"]).astype(BF16)
        dq = _bdot(dqk_, f["kb"], 2, 1)
        dks = _bdot_rows(dqk_, f["qb"])
        wdn = f["w_inter"] * dnum
        wdn_b = wdn.astype(BF16)
        wdd = f["w_inter"] * dden
        u = jnp.sum(f["qc"] * wdn, axis=-1, keepdims=True) + wdd * f["qn"]
        dks_s, dv_s, z_s, dg_s = [None, None], [None, None], [None, None], [None, None]
        dcn, dnn = dc_st[...], dn_st[...]
        for ch in (1, 0):
            rows = slice(L * ch, L * ch + L)
            _, _, _, decay, e_a, kw = _mlstm_update(f, ch, c_prev[ch], n_prev[ch], m_prev[ch])
            dcn_b = dcn.astype(BF16)
            dkw = _bdot(f["vb"][:, rows], dcn_b, 2, 2) + dnn
            dks_s[ch] = e_a * dkw
            dv_s[ch] = _bdot(kw.astype(BF16), dcn_b, 2, 1)
            z_s[ch] = e_a * jnp.sum(f["ks"][:, rows] * dkw, axis=-1, keepdims=True)
            dg_s[ch] = jnp.sum(z_s[ch], axis=1, keepdims=True) + decay * (
                jnp.sum(c_prev[ch] * dcn, axis=(1, 2), keepdims=True)
                + jnp.sum(n_prev[ch] * dnn, axis=(1, 2), keepdims=True))
            dcn = decay * dcn + _bdot_rows(f["qb"][:, rows], wdn_b[:, rows])
            dnn = decay * dnn + jnp.sum(wdd[:, rows] * f["q"][:, rows], axis=1, keepdims=True)
        dc_st[...], dn_st[...] = dcn, dnn
        dq = dq + jnp.concatenate(
            [_bdot(wdn_b[:, :L], c_prev[0].astype(BF16), 2, 2) + wdd[:, :L] * n_prev[0],
             _bdot(wdn_b[:, L:], c_prev[1].astype(BF16), 2, 2) + wdd[:, L:] * n_prev[1]], axis=1)
        dks = (dks + jnp.concatenate(dks_s, axis=1)) * (D ** -0.5)
        dv = dv + jnp.concatenate(dv_s, axis=1)
        z = jnp.concatenate(z_s, axis=1)
        row = lax.broadcasted_iota(jnp.int32, (1, STEP_ROWS, 1), 1)
        dg_col = jnp.where(row == L - 1, dg_s[0], 0.0) + jnp.where(row == 2 * L - 1, dg_s[1], 0.0)
        db_col = jnp.sum(g, axis=-1, keepdims=True) + u - z + dg_col
        g_row = jnp.sum(g, axis=1, keepdims=True)
        lane = lax.broadcasted_iota(jnp.int32, (STEP_ROWS, 128), 1)
        sub = lax.broadcasted_iota(jnp.int32, (8, STEP_ROWS), 0)
        dgc = jnp.zeros((STEP_ROWS, 128), F32)
        dgr = jnp.zeros((8, STEP_ROWS), F32)
        for hd in range(H):
            dgc = dgc + jnp.where(lane == hd, z[hd], 0.0) + jnp.where(lane == H + hd, db_col[hd], 0.0)
            dgr = dgr + jnp.where(sub == hd, g_row[hd], 0.0) - jnp.where(sub == H + hd, g_row[hd], 0.0)
            dqk_ref[:, D * hd:D * hd + D] = dq[hd]
            dqk_ref[:, H * D + D * hd:H * D + D * hd + D] = dks[hd]
            dv_ref[:, D * hd:D * hd + D] = dv[hd].astype(BF16)
        dgc_ref[...] = dgc
        dgr_ref[...] = dgr

    return pl.pallas_call(
        body, name="mlstm_bwd", grid=(steps,),
        in_specs=_mlstm_specs(T, rev) + [
            pl.BlockSpec((2, H, 128, 128), lambda s: (rev(s), 0, 0, 0)),
            pl.BlockSpec((2, H, 1, 128), lambda s: (rev(s), 0, 0, 0)),
            pl.BlockSpec((2, H, 1, 128), lambda s: (rev(s), 0, 0, 0)),
            pl.BlockSpec((STEP_ROWS, 512), lambda s: (rev(s), 0))],
        out_specs=[pl.BlockSpec((STEP_ROWS, 1024), lambda s: (rev(s), 0)),
                   pl.BlockSpec((STEP_ROWS, 512), lambda s: (rev(s), 0)),
                   pl.BlockSpec((STEP_ROWS, 128), lambda s: (rev(s), 0)),
                   pl.BlockSpec((8, STEP_ROWS), lambda s: (0, rev(s)))],
        out_shape=[_sds((T, 1024), F32), _sds((T, 512), BF16), _sds((T, 128), F32), _sds((8, T), F32)],
        scratch_shapes=[pltpu.VMEM((H, 128, 128), F32), pltpu.VMEM((H, 1, 128), F32)],
        compiler_params=_params(("arbitrary",)))(qk, qk, pm, gcol, bcol, grow, brow, cs, ns, ms, dh)


def _gate_bwd(dgc, dgr_t, gcol, bcol):
    T = dgc.shape[0]

    def body(a_ref, b_ref, g_ref, bias_ref, o_ref, acc_ref):
        @pl.when(pl.program_id(0) == 0)
        def _():
            acc_ref[...] = jnp.zeros_like(acc_ref)

        d = a_ref[...] + b_ref[...]
        lane = lax.broadcasted_iota(jnp.int32, d.shape, 1)
        is_f = (lane >= MLSTM_HEADS) & (lane < 2 * MLSTM_HEADS)
        dlogf = _chunk_rev_cumsum(jnp.where(is_f, d, 0.0), 0)
        out = jnp.where(is_f, dlogf * _sigmoid(-(g_ref[...] + bias_ref[...])), d)
        o_ref[...] = out.astype(BF16)
        acc_ref[0:1, :] += _colsum(out)

    return _rows("gate_bwd", body, [dgc, dgr_t, gcol, bcol],
                 [_sds((T, 128), BF16), _sds((8, 128), F32)], T)


def _head_norm(h, mu_axis=-1):
    mu = jnp.mean(h, axis=-1, keepdims=True)
    hc = h - mu
    r = lax.rsqrt(jnp.mean(hc * hc, axis=-1, keepdims=True) + NORM_EPS)
    return hc * r, r


def _mlstm_out(hm, pm, w):
    T = hm.shape[0]
    D = MLSTM_HEAD_DIM

    def body(h_ref, o_ref, w_ref, y_ref):
        for hd in range(MLSTM_HEADS):
            cols = slice(D * hd, D * hd + D)
            hn, _ = _head_norm(h_ref[:, cols])
            y_ref[:, cols] = (_sigmoid(o_ref[:, cols]) * hn * w_ref[:, cols]).astype(BF16)

    tr = min(ROW_TILE, T)
    return pl.pallas_call(
        body, name="mlstm_out", grid=(T // tr,),
        in_specs=[pl.BlockSpec((tr, 512), lambda i: (i, 0)), pl.BlockSpec((tr, 512), lambda i: (i, 3)),
                  pl.BlockSpec((1, 512), lambda i: (0, 0))],
        out_specs=pl.BlockSpec((tr, 512), lambda i: (i, 0)), out_shape=_sds((T, 512), BF16),
        compiler_params=_params(("parallel",)))(hm, pm, w)


def _mlstm_out_bwd(hm, pm, w, dy):
    T = hm.shape[0]
    D = MLSTM_HEAD_DIM
    tr = min(ROW_TILE, T)

    def body(h_ref, o_ref, w_ref, dy_ref, dh_ref, do_ref, acc_ref):
        @pl.when(pl.program_id(0) == 0)
        def _():
            acc_ref[...] = jnp.zeros_like(acc_ref)

        for hd in range(MLSTM_HEADS):
            cols = slice(D * hd, D * hd + D)
            hn, r = _head_norm(h_ref[:, cols])
            sg = _sigmoid(o_ref[:, cols])
            dy, w = dy_ref[:, cols], w_ref[:, cols]
            do_ref[:, cols] = (dy * hn * w * sg * (1.0 - sg)).astype(BF16)
            dyn = dy * sg
            acc_ref[0:1, cols] += _colsum(dyn * hn)
            dhn = dyn * w
            dh_ref[:, cols] = r * (dhn - jnp.mean(dhn, axis=-1, keepdims=True)
                                   - hn * jnp.mean(dhn * hn, axis=-1, keepdims=True))

    return pl.pallas_call(
        body, name="mlstm_out_bwd", grid=(T // tr,),
        in_specs=[pl.BlockSpec((tr, 512), lambda i: (i, 0)), pl.BlockSpec((tr, 512), lambda i: (i, 3)),
                  pl.BlockSpec((1, 512), lambda i: (0, 0)), pl.BlockSpec((tr, 512), lambda i: (i, 0))],
        out_specs=[pl.BlockSpec((tr, 512), lambda i: (i, 0)), pl.BlockSpec((tr, 512), lambda i: (i, 0)),
                   pl.BlockSpec((8, 512), lambda i: (0, 0))],
        out_shape=[_sds((T, 512), F32), _sds((T, 512), BF16), _sds((8, 512), F32)],
        compiler_params=_params(("arbitrary",)))(hm, pm, w, dy)


def _adamw(name, w, g, m, v, tr=64):
    R, C = w.shape
    tr = min(tr, R)
    c1 = 1.0 - ADAM_B1 ** ADAM_STEP
    c2 = 1.0 - ADAM_B2 ** ADAM_STEP

    def body(w_ref, g_ref, m_ref, v_ref, d_ref, mo_ref, vo_ref):
        g = g_ref[...]
        m = ADAM_B1 * m_ref[...] + (1.0 - ADAM_B1) * g
        v = ADAM_B2 * v_ref[...] + (1.0 - ADAM_B2) * (g * g)
        mo_ref[...] = m
        vo_ref[...] = v
        d_ref[...] = -ADAM_LR * ((m / c1) / (jnp.sqrt(v / c2) + ADAM_EPS) + ADAM_WD * w_ref[...])

    spec = pl.BlockSpec((tr, C), lambda i: (i, 0))
    return pl.pallas_call(
        body, name=name, grid=(R // tr,), in_specs=[spec] * 4, out_specs=[spec] * 3,
        out_shape=[_sds((R, C), F32)] * 3, compiler_params=_params(("parallel",)))(w, g, m, v)


def _place():
    return lax.axis_index("x"), lax.axis_index("y"), lax.axis_index("c")


def _all_gather8(name, blk, space):
    m, n = blk.shape

    def body(x_ref, out_ref, send_sems, recv_sems, local_sem):
        x, y, c = _place()
        me, sibling = (x, y, c), (x, y, 1 - c)
        chips = [(1 - x, y), (x, 1 - y), (1 - x, 1 - y)]

        def rows(px, py, pc):
            return out_ref.at[pl.ds((4 * px + 2 * py + pc) * m, m), :]

        def copy(k, block, to, src=None):
            return pltpu.make_async_remote_copy(
                src_ref=rows(*block) if src is None else src, dst_ref=rows(*block),
                send_sem=send_sems.at[k], recv_sem=recv_sems.at[k],
                device_id=to, device_id_type=MESH)

        mine = pltpu.make_async_copy(x_ref, rows(*me), local_sem)
        mine.start()
        first = [copy(0, me, sibling, src=x_ref)]
        first += [copy(1 + j, me, (*chip, c), src=x_ref) for j, chip in enumerate(chips)]
        for cp in first:
            cp.start()
        passed = [copy(4 + j, (*chip, c), sibling) for j, chip in enumerate(chips)]
        for j, chip in enumerate(chips):
            copy(1 + j, (*chip, c), me).wait_recv()
            passed[j].start()
        copy(0, sibling, me).wait_recv()
        for j, chip in enumerate(chips):
            copy(4 + j, (*chip, 1 - c), me).wait_recv()
        for cp in first + passed:
            cp.wait_send()
        mine.wait()

    return pl.pallas_call(
        body, name=name, out_shape=_sds((8 * m, n), blk.dtype),
        in_specs=[pl.BlockSpec(memory_space=space)], out_specs=pl.BlockSpec(memory_space=space),
        scratch_shapes=[pltpu.SemaphoreType.DMA((7,)), pltpu.SemaphoreType.DMA((7,)),
                        pltpu.SemaphoreType.DMA],
        compiler_params=pltpu.CompilerParams(vmem_limit_bytes=VMEM_LIMIT))(blk)


def _hbm_specs(n):
    return [pl.BlockSpec(memory_space=pl.ANY)] * n


def _gather_groups(name, blocks):
    nw = len(blocks)

    def body(*refs):
        srcs, outs = refs[:nw], refs[nw:2 * nw]
        send_sems, recv_sems, local_sems = refs[2 * nw:]
        x, y, c = _place()
        me, sibling = (x, y, c), (x, y, 1 - c)
        chips = [(1 - x, y), (x, 1 - y), (1 - x, 1 - y)]

        def slab(w, px, py, pc):
            return outs[w].at[2 * px + py, pl.ds(0, blocks[w].shape[0]), pc]

        def copy(w, k, block, to, src=None):
            return pltpu.make_async_remote_copy(
                src_ref=slab(w, *block) if src is None else src, dst_ref=slab(w, *block),
                send_sem=send_sems.at[w, k], recv_sem=recv_sems.at[w, k],
                device_id=to, device_id_type=MESH)

        mine = [pltpu.make_async_copy(srcs[w], slab(w, *me), local_sems.at[w]) for w in range(nw)]
        first = []
        for w in range(nw):
            mine[w].start()
            first.append(copy(w, 0, me, sibling, src=srcs[w]))
            first += [copy(w, 1 + j, me, (*chip, c), src=srcs[w]) for j, chip in enumerate(chips)]
        for cp in first:
            cp.start()
        passed = []
        for j, chip in enumerate(chips):
            for w in range(nw):
                copy(w, 1 + j, (*chip, c), me).wait_recv()
                passed.append(copy(w, 4 + j, (*chip, c), sibling))
                passed[-1].start()
        for w in range(nw):
            copy(w, 0, sibling, me).wait_recv()
            for j, chip in enumerate(chips):
                copy(w, 4 + j, (*chip, 1 - c), me).wait_recv()
        for cp in first + passed:
            cp.wait_send()
        for w in range(nw):
            mine[w].wait()

    return pl.pallas_call(
        body, name=name,
        out_shape=[_sds((4, b.shape[0], 2) + b.shape[1:], b.dtype) for b in blocks],
        in_specs=_hbm_specs(nw), out_specs=_hbm_specs(nw),
        scratch_shapes=[pltpu.SemaphoreType.DMA((nw, 7)), pltpu.SemaphoreType.DMA((nw, 7)),
                        pltpu.SemaphoreType.DMA((nw,))])(*blocks)


def _swap_sibling(name, srcs, halves=False):
    nw = len(srcs)

    def body(*refs):
        src_refs, dst_refs, send_sems, recv_sems = refs[:nw], refs[nw:2 * nw], refs[2 * nw], refs[2 * nw + 1]
        x, y, c = _place()
        cps = []
        for w in range(nw):
            s = src_refs[w]
            if halves:
                s = s.at[pl.ds(0, srcs[w].shape[0]), pl.ds(0, srcs[w].shape[1]), 1 - c]
            cps.append(pltpu.make_async_remote_copy(
                src_ref=s, dst_ref=dst_refs[w], send_sem=send_sems.at[w], recv_sem=recv_sems.at[w],
                device_id=(x, y, 1 - c), device_id_type=MESH))
        for cp in cps:
            cp.start()
        for cp in cps:
            cp.wait()

    shapes = [(s.shape[:2] + s.shape[3:]) if halves else s.shape for s in srcs]
    return pl.pallas_call(
        body, name=name, out_shape=[_sds(sh, s.dtype) for sh, s in zip(shapes, srcs)],
        in_specs=_hbm_specs(nw), out_specs=_hbm_specs(nw),
        scratch_shapes=[pltpu.SemaphoreType.DMA((nw,)), pltpu.SemaphoreType.DMA((nw,))])(*srcs)


def _swap_chips(name, srcs):
    nw = len(srcs)

    def body(*refs):
        src_refs, dst_refs = refs[:nw], refs[nw:2 * nw]
        send_sems, recv_sems, local_sems = refs[2 * nw:]
        x, y, c = _place()
        mine = 2 * x + y
        chips = [(1 - x, y), (x, 1 - y), (1 - x, 1 - y)]

        def copy(w, j):
            px, py = chips[j]
            return pltpu.make_async_remote_copy(
                src_ref=src_refs[w].at[2 * px + py], dst_ref=dst_refs[w].at[mine],
                send_sem=send_sems.at[w, j], recv_sem=recv_sems.at[w, j],
                device_id=(px, py, c), device_id_type=MESH)

        def landing(w, j):
            px, py = chips[j]
            return pltpu.make_async_remote_copy(
                src_ref=src_refs[w].at[mine], dst_ref=dst_refs[w].at[2 * px + py],
                send_sem=send_sems.at[w, j], recv_sem=recv_sems.at[w, j],
                device_id=(px, py, c), device_id_type=MESH)

        own = [pltpu.make_async_copy(src_refs[w].at[mine], dst_refs[w].at[mine], local_sems.at[w])
               for w in range(nw)]
        for w in range(nw):
            own[w].start()
            for j in range(3):
                copy(w, j).start()
        for w in range(nw):
            for j in range(3):
                landing(w, j).wait_recv()
        for w in range(nw):
            for j in range(3):
                copy(w, j).wait_send()
            own[w].wait()

    return pl.pallas_call(
        body, name=name, out_shape=[_sds(s.shape, s.dtype) for s in srcs],
        in_specs=_hbm_specs(nw), out_specs=_hbm_specs(nw),
        scratch_shapes=[pltpu.SemaphoreType.DMA((nw, 3)), pltpu.SemaphoreType.DMA((nw, 3)),
                        pltpu.SemaphoreType.DMA((nw,))])(*srcs)


def _split_start(name, src, land, copies, ncopies):
    def body(src_ref, land_ref, send_sems, recv_sems, src_thru, land_thru, token):
        for k, (s, d, dev) in enumerate(copies(src_ref, land_ref, *_place())):
            pltpu.make_async_remote_copy(src_ref=s, dst_ref=d, send_sem=send_sems.at[k],
                                         recv_sem=recv_sems.at[k], device_id=dev, device_id_type=MESH).start()
        token[...] = jnp.zeros_like(token)

    hbm, sem = pl.BlockSpec(memory_space=pltpu.HBM), pl.BlockSpec(memory_space=pltpu.SEMAPHORE)
    return pl.pallas_call(
        body, name=name,
        out_shape=(pltpu.SemaphoreType.DMA((ncopies,)), pltpu.SemaphoreType.DMA((ncopies,)),
                   pltpu.HBM(src.shape, src.dtype), pltpu.HBM(land.shape, land.dtype), _sds((8, 128), F32)),
        in_specs=(hbm, hbm), out_specs=(sem, sem, hbm, hbm, pl.BlockSpec(memory_space=pltpu.VMEM)),
        input_output_aliases={0: 2, 1: 3},
        compiler_params=pltpu.CompilerParams(has_side_effects=pltpu.SideEffectType.DATAFLOW_SIDE_EFFECTING))(
            pltpu.with_memory_space_constraint(src, pltpu.HBM), pltpu.with_memory_space_constraint(land, pltpu.HBM))


def _split_wait(name, started, after, waits):
    send_sems, recv_sems, src, land, _ = started

    def body(src_ref, land_ref, send_sems, recv_sems, after_ref, src_dead, got_ref):
        x, y, c = _place()
        for k, (s, d) in enumerate(waits(src_ref, land_ref, x, y, c)):
            cp = pltpu.make_async_remote_copy(src_ref=s, dst_ref=d, send_sem=send_sems.at[k],
                                              recv_sem=recv_sems.at[k], device_id=(x, y, 1 - c),
                                              device_id_type=MESH)
            cp.wait_send()
            cp.wait_recv()

    hbm, sem = pl.BlockSpec(memory_space=pltpu.HBM), pl.BlockSpec(memory_space=pltpu.SEMAPHORE)
    return pl.pallas_call(
        body, name=name, out_shape=(pltpu.HBM(src.shape, src.dtype), pltpu.HBM(land.shape, land.dtype)),
        in_specs=(hbm, hbm, sem, sem, pl.BlockSpec(memory_space=pl.ANY)), out_specs=(hbm, hbm),
        input_output_aliases={0: 0, 1: 1},
        compiler_params=pltpu.CompilerParams(has_side_effects=pltpu.SideEffectType.DATAFLOW_SIDE_EFFECTING))(
            src, land, send_sems, recv_sems, after)[1]


def _other_chips(x, y):
    return [(1 - x, y), (x, 1 - y), (1 - x, 1 - y)]


def _gather_sends(src_ref, land_ref, x, y, c):
    to = land_ref.at[2 * x + y, pl.ds(0, land_ref.shape[1]), c]
    return [(src_ref, to, (x, y, 1 - c))] + [(src_ref, to, (px, py, c)) for px, py in _other_chips(x, y)]


def _gather_lands(src_ref, land_ref, x, y, c):
    g = pl.ds(0, land_ref.shape[1])
    return [(src_ref, land_ref.at[2 * x + y, g, 1 - c])] + [
        (src_ref, land_ref.at[2 * px + py, g, c]) for px, py in _other_chips(x, y)]


def _scatter_sends(src_ref, land_ref, x, y, c):
    return [(src_ref.at[2 * px + py], land_ref.at[2 * x + y], (px, py, c)) for px, py in _other_chips(x, y)]


def _scatter_lands(src_ref, land_ref, x, y, c):
    return [(src_ref.at[2 * x + y], land_ref.at[2 * px + py]) for px, py in _other_chips(x, y)]


def _forward_sibling(name, land):
    def body(land_ref, out_ref, send_sems, recv_sems):
        x, y, c = _place()
        g = pl.ds(0, land_ref.shape[1])
        cps = [pltpu.make_async_remote_copy(
            src_ref=land_ref.at[2 * px + py, g, c], dst_ref=out_ref.at[2 * px + py, g, c],
            send_sem=send_sems.at[j], recv_sem=recv_sems.at[j], device_id=(x, y, 1 - c),
            device_id_type=MESH) for j, (px, py) in enumerate(_other_chips(x, y))]
        for cp in cps:
            cp.start()
        for j, (px, py) in enumerate(_other_chips(x, y)):
            slot = out_ref.at[2 * px + py, g, 1 - c]
            pltpu.make_async_remote_copy(src_ref=slot, dst_ref=slot, send_sem=send_sems.at[j],
                                         recv_sem=recv_sems.at[j], device_id=(x, y, 1 - c),
                                         device_id_type=MESH).wait_recv()
        for cp in cps:
            cp.wait_send()

    return pl.pallas_call(
        body, name=name, out_shape=_sds(land.shape, land.dtype),
        in_specs=_hbm_specs(1), out_specs=pl.BlockSpec(memory_space=pl.ANY), input_output_aliases={0: 0},
        scratch_shapes=[pltpu.SemaphoreType.DMA((3,)), pltpu.SemaphoreType.DMA((3,))])(land)


def _pair_sum(name, full, got, core):
    _, g, _, m, n = full.shape

    def body(c_ref, a_ref, b_ref, o_ref):
        o_ref[...] = (a_ref[...].astype(F32) + b_ref[...].astype(F32)).astype(o_ref.dtype)

    slab = pl.BlockSpec((None, None, m, n), lambda s, w, c: (s, w, 0, 0))
    return pl.pallas_call(
        body, name=name,
        grid_spec=pltpu.PrefetchScalarGridSpec(
            num_scalar_prefetch=1, grid=(4, g),
            in_specs=[pl.BlockSpec((None, None, None, m, n), lambda s, w, c: (s, w, c[0], 0, 0)), slab],
            out_specs=slab),
        out_shape=_sds(got.shape, BF16),
        compiler_params=_params(("parallel", "parallel")))(core, full, got)


def _sum4(name, a):
    _, g, m, n = a.shape

    def body(a_ref, o_ref):
        acc = a_ref[0].astype(F32)
        for s in range(1, 4):
            acc = acc + a_ref[s].astype(F32)
        o_ref[...] = acc

    return pl.pallas_call(body, name=name, grid=(g,),
                          in_specs=[pl.BlockSpec((4, None, m, n), lambda w: (0, w, 0, 0))],
                          out_specs=pl.BlockSpec((None, m, n), lambda w: (w, 0, 0)),
                          out_shape=_sds((g, m, n), F32), compiler_params=_params(("parallel",)))(a)


def _small_update(gathered, w, m, v):
    n = w.shape[1]
    tn = 2048
    c1 = 1.0 - ADAM_B1 ** ADAM_STEP
    c2 = 1.0 - ADAM_B2 ** ADAM_STEP

    def body(g_ref, w_ref, m_ref, v_ref, go_ref, d_ref, mo_ref, vo_ref):
        g = g_ref[0:1, :]
        for d in range(1, 8):
            g = g + g_ref[d:d + 1, :]
        go_ref[...] = g
        m = ADAM_B1 * m_ref[...] + (1.0 - ADAM_B1) * g
        v = ADAM_B2 * v_ref[...] + (1.0 - ADAM_B2) * (g * g)
        mo_ref[...] = m
        vo_ref[...] = v
        d_ref[...] = -ADAM_LR * ((m / c1) / (jnp.sqrt(v / c2) + ADAM_EPS) + ADAM_WD * w_ref[...])

    row = pl.BlockSpec((1, tn), lambda i: (0, i))
    return pl.pallas_call(
        body, name="small_update", grid=(n // tn,),
        in_specs=[pl.BlockSpec((8, tn), lambda i: (0, i)), row, row, row], out_specs=[row] * 4,
        out_shape=[_sds((1, n), F32)] * 4, compiler_params=_params(("parallel",)))(gathered, w, m, v)


def _swiglu(ps, es):
    g, u = ps
    return g * _sigmoid(g) * u, g, u


def _swiglu_bwd(ps, es):
    g, u = es[0].astype(F32), es[1].astype(F32)
    sg = _sigmoid(g)
    return ps[0] * u * (sg * (1.0 + g * (1.0 - sg))), ps[0] * (g * sg)


def _merge(ps, es):
    return _sigmoid(es[0]) * ps[0] + _sigmoid(es[1]) * ps[1], ps[0], ps[1]


def _merge_bwd(ps, es):
    a, b, ga, gm = es
    sa, sm = _sigmoid(ga), _sigmoid(gm)
    dm = ps[0]
    return dm * sa, dm * sm, dm * a * (sa * (1.0 - sa)), dm * b * (sm * (1.0 - sm))


W_IN_PIECES = (("q", 512), ("kv", 256), ("mqk", 1024), ("mv", 512), ("mo", 512), ("if", 8),
               ("ga", 1024), ("gm", 1024))


def _local_step(x, tgt, pos_col, mod, sp, W, ffn_weights, ffn_grads):
    sh_m, sc_m, gate_m, sh_f, sc_f, gate_f = mod
    w_a = jnp.concatenate([W["q"], W["kv"]], axis=0)
    w_m = jnp.concatenate([W["mqk"], W["mv"], W["mo"]], axis=0)
    w_g = jnp.concatenate([W["ga"], W["gm"]], axis=0)

    h = _pre_norm(x, sp["g_pre_mix"], sc_m, sh_m)
    pa, = _mm("proj_attn", [[(h, w_a)]], [], _first, [F32], cn=256, nt=True)
    pm, = _mm("proj_mlstm", [[(h, w_m)]], [], _first, [F32], cn=512, nt=True)
    pif, = _mm("proj_gates", [[(h, W["if"])]], [], _first, [F32], cn=128, nt=True)
    pg, = _mm("proj_branch_gates", [[(h, w_g)]], [], _first, [F32], cn=512, nt=True)
    inv = ROPE_THETA ** (-2.0 * jnp.arange(HEAD_DIM // 2, dtype=F32) / HEAD_DIM)
    cos, sin = _rope_tables(pos_col, jnp.tile(inv, 4).reshape(1, 128))
    ya = _attn_fwd(pa, cos, sin, sp["sinks"])
    qk = _conv_fwd(pm, sp["conv_w"], sp["conv_b"])
    bcol = jnp.pad(sp["b_if"], ((0, 0), (0, 120)))
    brow = jnp.broadcast_to(sp["b_if"].reshape(8, 1), (8, 128))
    grow = pif[:, :8].T
    hm, cs, ns, ms = _mlstm_fwd(qk, pm, pif, bcol, grow, brow)
    ym = _mlstm_out(hm, pm, sp["norm_w"])
    merged, br_a, br_m = _mm("branches", [[(ya, W["ba"])], [(ym, W["bm"])]],
                             [(pg, 0), (pg, 1)], _merge, [BF16, F32, F32], cn=512, nt=True)
    mix, = _mm("mix_out", [[(merged, W["out"])]], [], _first, [F32], cn=512)
    x1, h2 = _res_norm(x, mix, gate_m, sp["g_post_mix"], sp["g_pre_ffn"], sc_f, sh_f)
    w_fg, w_fu, w_fd = ffn_weights(mix)
    act, gt, up = _mm("ffn_in", [[(h2, w_fg)], [(h2, w_fu)]], [], _swiglu, [BF16] * 3,
                      cn=256, nt=True)
    ff, = _mm("ffn_down", [[(act, w_fd)]], [], _first, [F32], cn=512)
    dy, dff, acc_l, loss = _final_loss(x1, ff, tgt, gate_f, sp["g_post_ffn"])

    G = {}
    dgt, dup = _mm("ffn_down_bwd", [[(dff, w_fd)]], [gt, up], _swiglu_bwd, [BF16, BF16],
                   cn=256, nt=True)
    g_fd = _mm_tn("dw_ffn_down", act, dff, BF16, 1408, 512, 512)
    dh2, = _mm("ffn_in_bwd", [[(dgt, w_fg), (dup, w_fu)]], [], _first, [F32], cn=512)
    g_fg = _mm_tn("dw_ffn_gate", dgt, h2, BF16, 1408, 1024, 512)
    g_fu = _mm_tn("dw_ffn_up", dup, h2, BF16, 1408, 1024, 512)
    tie = ffn_grads(g_fg, g_fu, g_fd)
    dx1, dmix, acc_r = _res_norm_bwd(x1, mix, dh2, dy, sc_f + tie, gate_m, sp["g_pre_ffn"],
                                     sp["g_post_mix"])
    d_a, d_m, dga, dgm = _mm("mix_out_bwd", [[(dmix, W["out"])]],
                             [br_a, br_m, (pg, 0), (pg, 1)], _merge_bwd,
                             [BF16] * 4, cn=512, nt=True)
    G["out"] = _mm_tn("dw_out", merged, dmix, BF16, 1024, 512, 512)
    dya, = _mm("branch_attn_bwd", [[(d_a, W["ba"])]], [], _first, [F32], cn=512)
    dym, = _mm("branch_mlstm_bwd", [[(d_m, W["bm"])]], [], _first, [F32], cn=512)
    G["ba"] = _mm_tn("dw_branch_attn", d_a, ya, BF16, 1024, 512, 512)
    G["bm"] = _mm_tn("dw_branch_mlstm", d_m, ym, BF16, 1024, 512, 512)
    dhm, do_m, acc_n = _mlstm_out_bwd(hm, pm, sp["norm_w"], dym)
    dqk, dv_m, dgc, dgr = _mlstm_bwd(qk, pm, pif, bcol, grow, brow, cs, ns, ms, dhm)
    dif, acc_g = _gate_bwd(dgc, jnp.pad(dgr.T, ((0, 0), (0, 120))), pif, bcol)
    dpre, acc_c = _conv_bwd_pre(pm, sp["conv_w"], sp["conv_b"], dqk)
    du = _conv_bwd_in(dpre, sp["conv_w"])
    dq_a, dcur, dprv, dsink = _attn_bwd(pa, cos, sin, sp["sinks"], dya)
    dkv = _attn_kv_combine(dcur, dprv, cos, sin)
    dproj = {"q": dq_a, "kv": dkv, "mqk": du, "mv": dv_m, "mo": do_m, "if": dif, "ga": dga, "gm": dgm}
    dh, = _mm("proj_bwd", [[(dproj[k], W[k]) for k, _ in W_IN_PIECES]], [], _first, [F32], cn=512)
    for k, _ in W_IN_PIECES:
        G[k] = _mm_tn("dw_in_" + k, dproj[k], h, BF16, dproj[k].shape[1], 1024, 512)
    dx, acc_p = _pre_norm_bwd(x, dh, dx1, sp["g_pre_mix"], sc_m)

    small = {
        "mod": jnp.concatenate([acc_p[1], acc_p[0], acc_r[3], acc_r[1], acc_r[0], acc_l[0]]),
        "g_pre_mix": acc_p[2], "g_post_mix": acc_r[4], "b_if": acc_g[0, :8],
        "conv_w": acc_c[:CONV_WIDTH].reshape(-1), "conv_b": acc_c[CONV_WIDTH],
        "sinks": dsink[:, 0], "norm_w": acc_n[0], "g_pre_ffn": acc_r[2], "g_post_ffn": acc_l[1]}
    return loss, dx, G, small


IN_WIDTH = sum(n for _, n in W_IN_PIECES)
IN_SHARD = IN_WIDTH // 4
IN_SHARD_PAD = -(-IN_SHARD // 32) * 32


def _split_w_in(w_in_t):
    out, off = {}, 0
    for k, n in W_IN_PIECES:
        out[k] = w_in_t[off:off + n]
        off += n
    out["if"] = jnp.pad(out["if"], ((0, 120), (0, 0)))
    return out


def _halves(a):
    return a.reshape(4, 2, a.shape[0] // 8, a.shape[1])


SMALL = (("b_ada", 6144), ("g_pre_mix", 1024), ("g_post_mix", 1024), ("b_if", 128), ("conv_w", 4096),
         ("conv_b", 1024), ("sinks", 128), ("norm_w", 512), ("g_pre_ffn", 1024), ("g_post_ffn", 1024))
SMALL_LEN = 8 * 2048


def _pack_small(vals):
    parts = []
    for k, n in SMALL:
        v = vals[k].reshape(-1)
        parts.append(jnp.pad(v, (0, n - v.shape[0])))
    flat = jnp.concatenate(parts)
    return jnp.pad(flat, (0, SMALL_LEN - flat.shape[0]))


def _unpack_small(flat, shapes):
    out, off = {}, 0
    for k, n in SMALL:
        size = 1
        for d in shapes[k]:
            size *= d
        out[k] = flat[off:off + size].reshape(shapes[k])
        off += n
    return out


def kernel(x, c, positions, w_ada, b_ada, g_pre_mix, g_post_mix, w_in, b_if, conv_w, conv_b, attn_sinks, mlstm_norm_w, w_branch_attn, w_branch_mlstm, w_out, g_pre_ffn, g_post_ffn, w_ffn_gate, w_ffn_up, w_ffn_down, loss_target, m_w_ada, m_b_ada, m_g_pre_mix, m_g_post_mix, m_w_in, m_b_if, m_conv_w, m_conv_b, m_attn_sinks, m_mlstm_norm_w, m_w_branch_attn, m_w_branch_mlstm, m_w_out, m_g_pre_ffn, m_g_post_ffn, m_w_ffn_gate, m_w_ffn_up, m_w_ffn_down, v_w_ada, v_b_ada, v_g_pre_mix, v_g_post_mix, v_w_in, v_b_if, v_conv_w, v_conv_b, v_attn_sinks, v_mlstm_norm_w, v_w_branch_attn, v_w_branch_mlstm, v_w_out, v_g_pre_ffn, v_g_post_ffn, v_w_ffn_gate, v_w_ffn_up, v_w_ffn_down):
    xi, yi, ci = _place()
    chip = 2 * xi + yi
    dev = 2 * chip + ci
    T = x.shape[1]
    ada_cols = w_ada.shape[2]

    def my_half(a):
        n = a.shape[0] // 2
        return lax.dynamic_slice_in_dim(a, ci * n, n, axis=0).astype(BF16)

    ffn_blk = jnp.stack([my_half(w_ffn_gate[0].T), my_half(w_ffn_up[0].T), my_half(w_ffn_down[0])])
    ffn_land = lax.dynamic_update_slice(jnp.zeros((4, 3, 2) + ffn_blk.shape[1:], BF16),
                                        ffn_blk[None, :, None], (chip, 0, ci, 0, 0))
    ffn_started = _split_start("ffn_gather_start", ffn_blk, ffn_land, _gather_sends, 4)
    tie = ffn_started[4][0, 0]

    blk = jnp.concatenate([c.reshape(-1) + tie, conv_w.reshape(-1)]).reshape(8, 256)
    got = _all_gather8("gather_cond", blk, pltpu.VMEM).reshape(8, 2048)
    c_all = got[:, :D_MODEL].astype(BF16)
    conv_full = got[::2, D_MODEL:].reshape(4, CONV_WIDTH, -1).transpose(1, 0, 2).reshape(CONV_WIDTH, -1)

    b_sh = lax.dynamic_slice_in_dim(b_ada, chip * ada_cols, ada_cols, axis=1)
    mod_part, = _mm("ada_mod", [[(c_all, w_ada[0].astype(BF16))]], [b_sh],
                    lambda ps, es: (ps[0] + es[0],), [F32], cn=512, tm=8)
    mod_all = _all_gather8("gather_mod", mod_part, pltpu.VMEM).reshape(4, 2, 8, ada_cols)[:, 0]
    mod = lax.dynamic_index_in_dim(mod_all, dev, axis=1, keepdims=False).reshape(6, 1, D_MODEL)

    w_in_t = jnp.pad(w_in[0].T, ((0, IN_SHARD_PAD - IN_SHARD), (0, 0)))
    blocks = [my_half(w_in_t)[None], my_half(w_out[0])[None],
              jnp.stack([my_half(w_branch_attn[0].T), my_half(w_branch_mlstm[0].T)])]
    g_in, g_out, g_br = _gather_groups("gather_weights", blocks)
    W = {"out": g_out.reshape(D_MODEL, D_MODEL), "ba": g_br[:, 0].reshape(D_MODEL, -1),
         "bm": g_br[:, 1].reshape(D_MODEL, -1)}
    W.update(_split_w_in(g_in.reshape(4, IN_SHARD_PAD, D_MODEL)[:, :IN_SHARD].reshape(IN_WIDTH, D_MODEL)))

    core = ci.reshape(1).astype(jnp.int32)
    sent = {}

    def ffn_weights(after):
        land = _split_wait("ffn_gather_wait", ffn_started, after, _gather_lands)
        land = _forward_sibling("ffn_gather_forward", land)
        return [land[:, i].reshape(D_FF, D_MODEL) for i in range(3)]

    def ffn_grads(g_fg, g_fu, g_fd):
        grp = jnp.stack([_halves(g_fg), _halves(g_fu), _halves(g_fd)], axis=1)
        pair = _pair_sum("rs_pair_sum_ffn", grp, _swap_sibling("rs_pair_ffn", [grp], halves=True)[0], core)
        sent["ffn"] = _split_start("rs_ffn_start", pair, pair + jnp.zeros((), BF16), _scatter_sends, 3)
        return sent["ffn"][4][0, 0]

    sp = {"g_pre_mix": g_pre_mix, "g_post_mix": g_post_mix, "b_if": b_if, "conv_w": conv_full,
          "conv_b": conv_b, "sinks": attn_sinks, "norm_w": mlstm_norm_w, "g_pre_ffn": g_pre_ffn,
          "g_post_ffn": g_post_ffn}
    loss, dx, G, small = _local_step(x[0], loss_target[0], positions.reshape(T, 1),
                                     [mod[i] for i in range(6)], sp, W, ffn_weights, ffn_grads)

    g_in_t = jnp.concatenate([G[k][:n] for k, n in W_IN_PIECES]).reshape(4, IN_SHARD, D_MODEL)
    g_in_t = jnp.pad(g_in_t, ((0, 0), (0, IN_SHARD_PAD - IN_SHARD), (0, 0)))
    groups = [g_in_t.reshape(4, 1, 2, IN_SHARD_PAD // 2, D_MODEL), _halves(G["out"])[:, None],
              jnp.stack([_halves(G["ba"]), _halves(G["bm"])], axis=1)]
    theirs = _swap_sibling("rs_pair", groups, halves=True)
    pairs = [_pair_sum("rs_pair_sum_%d" % i, a, b, core) for i, (a, b) in enumerate(zip(groups, theirs))]
    landed = [_split_wait("rs_ffn_wait", sent["ffn"], dx, _scatter_lands), *_swap_chips("rs_chips", pairs)]
    reds = [_sum4("rs_chip_sum_%d" % i, a) for i, a in enumerate(landed)]
    others = _swap_sibling("rs_share", reds)
    s_ffn, s_in, s_out, s_br = [
        jnp.concatenate([jnp.where(ci == 0, r, o), jnp.where(ci == 0, o, r)], axis=1)
        for r, o in zip(reds, others)]
    gsh = {"fg": s_ffn[0].T, "fu": s_ffn[1].T, "fd": s_ffn[2], "w_in": s_in[0, :IN_SHARD].T,
           "out": s_out[0], "ba": s_br[0].T, "bm": s_br[1].T}

    small["b_ada"] = small.pop("mod")
    vec = _pack_small(small).reshape(8, 2048)
    g_all = _all_gather8("gather_small", vec, pltpu.VMEM).reshape(8, SMALL_LEN)
    dmod_sh = lax.dynamic_slice_in_dim(g_all[:, :6 * D_MODEL], chip * ada_cols, ada_cols, axis=1)
    g_w_ada = _mm_tn("dw_ada", c_all, dmod_sh.astype(BF16), F32, D_MODEL, 512, 8)

    smalls = {"b_ada": (b_ada, m_b_ada, v_b_ada), "g_pre_mix": (g_pre_mix, m_g_pre_mix, v_g_pre_mix),
              "g_post_mix": (g_post_mix, m_g_post_mix, v_g_post_mix), "b_if": (b_if, m_b_if, v_b_if),
              "conv_w": None, "conv_b": (conv_b, m_conv_b, v_conv_b),
              "sinks": (attn_sinks, m_attn_sinks, v_attn_sinks),
              "norm_w": (mlstm_norm_w, m_mlstm_norm_w, v_mlstm_norm_w),
              "g_pre_ffn": (g_pre_ffn, m_g_pre_ffn, v_g_pre_ffn),
              "g_post_ffn": (g_post_ffn, m_g_post_ffn, v_g_post_ffn)}
    shapes = {k: (t[0].shape if t is not None else (1, CONV_WIDTH, D_MODEL)) for k, t in smalls.items()}
    zeros = jnp.zeros((CONV_WIDTH * D_MODEL,), F32)
    packs = [_pack_small({k: (t[i] if t is not None else zeros) for k, t in smalls.items()}).reshape(1, -1)
             for i in range(3)]
    s_out = [_unpack_small(o[0], shapes) for o in _small_update(g_all, *packs)]
    g_conv = lax.dynamic_slice_in_dim(s_out[0]["conv_w"], chip * conv_w.shape[2], conv_w.shape[2], axis=2)

    res = {}
    for k, t in smalls.items():
        if t is not None:
            res[k] = tuple(o[k] for o in s_out)
    res["conv_w"] = (g_conv, *[o[None] for o in _adamw("adam_conv_w", conv_w[0], g_conv[0], m_conv_w[0], v_conv_w[0])])
    res["w_ada"] = (g_w_ada[None], *[o[None] for o in _adamw("adam_w_ada", w_ada[0], g_w_ada, m_w_ada[0], v_w_ada[0])])
    bigs = {"w_in": (w_in, m_w_in, v_w_in), "ba": (w_branch_attn, m_w_branch_attn, v_w_branch_attn),
            "bm": (w_branch_mlstm, m_w_branch_mlstm, v_w_branch_mlstm), "out": (w_out, m_w_out, v_w_out),
            "fg": (w_ffn_gate, m_w_ffn_gate, v_w_ffn_gate), "fu": (w_ffn_up, m_w_ffn_up, v_w_ffn_up),
            "fd": (w_ffn_down, m_w_ffn_down, v_w_ffn_down)}
    for k, (w, m, v) in bigs.items():
        res[k] = (gsh[k][None], *[o[None] for o in _adamw("adam_" + k, w[0], gsh[k], m[0], v[0])])

    order = ("w_ada", "b_ada", "g_pre_mix", "g_post_mix", "w_in", "b_if", "conv_w", "conv_b", "sinks",
             "norm_w", "ba", "bm", "out", "g_pre_ffn", "g_post_ffn", "fg", "fu", "fd")
    total = lax.psum(loss[0, 0], ("x", "y", "c"))
    return (total, dx[None], *[res[k][0] for k in order], *[res[k][1] for k in order],
            *[res[k][2] for k in order], *[res[k][3] for k in order])
```

```python
import functools

import jax
import jax.numpy as jnp
from jax import lax
from jax.experimental import pallas as pl
from jax.experimental.pallas import tpu as pltpu

F32, BF16 = jnp.float32, jnp.bfloat16
MESH = pl.DeviceIdType.MESH

D_MODEL = 1024
N_Q_HEADS, N_KV_HEADS, HEAD_DIM, WINDOW = 8, 2, 64, 128
ROPE_THETA = 10000.0
MLSTM_HEADS, MLSTM_HEAD_DIM, MLSTM_CHUNK, CONV_WIDTH = 4, 128, 64, 4
D_FF = 2816
NORM_EPS = 1e-6
ADAM_LR, ADAM_B1, ADAM_B2, ADAM_EPS, ADAM_WD, ADAM_STEP = 0.001, 0.9, 0.999, 1e-08, 0.01, 10

VMEM_LIMIT = 56 * 1024 * 1024
ROW_TILE = 256
MM_TM = 512
ATTN_BLK = WINDOW
STEP_ROWS = 2 * MLSTM_CHUNK
NEG_INF = float("-inf")


def _params(sem):
    return pltpu.CompilerParams(dimension_semantics=sem, vmem_limit_bytes=VMEM_LIMIT)


def _sds(shape, dtype):
    return jax.ShapeDtypeStruct(shape, dtype)


def _sigmoid(x):
    return 1.0 / (1.0 + jnp.exp(-x))


def _dot(a, b, ca, cb):
    return lax.dot_general(a, b, (((ca,), (cb,)), ((), ())), preferred_element_type=F32)


def _bdot(a, b, ca, cb):
    return lax.dot_general(a, b, (((ca,), (cb,)), ((0,), (0,))), preferred_element_type=F32)


def _bdot_rows(a, b):
    return jnp.stack([_dot(a[h], b[h], 0, 0) for h in range(a.shape[0])])


def _mm(name, prods, extras, epi, out_dtypes, cn, nt=False, tm=MM_TM):
    flat = [ab for p in prods for ab in p]
    counts = [len(p) for p in prods]
    M = flat[0][0].shape[0]
    N = flat[0][1].shape[0 if nt else 1]
    tm = min(tm, M)
    n_in = 2 * len(flat) + len(extras)

    def body(*refs):
        ins, outs = refs[:n_in], refs[n_in:]
        for j in range(N // cn):
            cols = slice(j * cn, (j + 1) * cn)
            k, ps = 0, []
            for cnt in counts:
                acc = None
                for _ in range(cnt):
                    b = ins[k + 1][cols, :] if nt else ins[k + 1][:, cols]
                    d = _dot(ins[k][...], b, 1, 1 if nt else 0)
                    acc = d if acc is None else acc + d
                    k += 2
                ps.append(acc)
            res = epi(ps, [r[:, cols] for r in ins[k:]])
            for o, r in zip(outs, res):
                o[:, cols] = r.astype(o.dtype)

    in_specs, args = [], []
    for a, b in flat:
        in_specs.append(pl.BlockSpec((tm, a.shape[1]), lambda i: (i, 0)))
        in_specs.append(pl.BlockSpec(b.shape, lambda i: (0, 0), pipeline_mode=pl.Buffered(1)))
        args += [a, b]
    for e in extras:
        e, off = e if isinstance(e, tuple) else (e, 0)
        rows = 1 if e.shape[0] == 1 else tm
        in_specs.append(pl.BlockSpec((rows, N), lambda i, off=off, rows=rows: (0 if rows == 1 else i, off)))
        args.append(e)
    return pl.pallas_call(
        body, name=name, grid=(M // tm,), in_specs=in_specs,
        out_specs=[pl.BlockSpec((tm, N), lambda i: (i, 0)) for _ in out_dtypes],
        out_shape=[_sds((M, N), dt) for dt in out_dtypes],
        compiler_params=_params(("parallel",)))(*args)


def _mm_tn(name, a, b, out_dtype, tk, tn, tt):
    T, Ka = a.shape
    N = b.shape[1]
    tt = min(tt, T)
    steps = T // tt

    def body(a_ref, b_ref, o_ref, acc):
        t = pl.program_id(2)

        @pl.when(t == 0)
        def _():
            acc[...] = jnp.zeros_like(acc)

        acc[...] += _dot(a_ref[...], b_ref[...], 0, 0)

        @pl.when(t == steps - 1)
        def _():
            o_ref[...] = acc[...].astype(o_ref.dtype)

    return pl.pallas_call(
        body, name=name, grid=(Ka // tk, N // tn, steps),
        in_specs=[pl.BlockSpec((tt, tk), lambda i, j, t: (t, i)),
                  pl.BlockSpec((tt, tn), lambda i, j, t: (t, j))],
        out_specs=pl.BlockSpec((tk, tn), lambda i, j, t: (i, j)),
        out_shape=_sds((Ka, N), out_dtype),
        scratch_shapes=[pltpu.VMEM((tk, tn), F32)],
        compiler_params=_params(("parallel", "parallel", "arbitrary")))(a, b)


def _first(ps, es):
    return (ps[0],)


def _rows(name, body, ins, out_shapes, T, tr=ROW_TILE):
    tr = min(tr, T)

    def spec(shape):
        if shape[0] == T:
            return pl.BlockSpec((tr,) + tuple(shape[1:]), lambda i: (i,) + (0,) * (len(shape) - 1))
        return pl.BlockSpec(tuple(shape), lambda i: (0,) * len(shape))

    return pl.pallas_call(
        body, name=name, grid=(T // tr,),
        in_specs=[spec(a.shape) for a in ins], out_specs=[spec(s.shape) for s in out_shapes],
        out_shape=out_shapes, compiler_params=_params(("arbitrary",)))(*ins)


def _rms(x):
    r = lax.rsqrt(jnp.mean(x * x, axis=-1, keepdims=True) + NORM_EPS)
    return x * r, r


def _rms_bwd(dxn, xn, r):
    return r * (dxn - xn * jnp.mean(dxn * xn, axis=-1, keepdims=True))


def _colsum(v):
    return jnp.sum(v, axis=0, keepdims=True)


def _pre_norm(x, g, sc, sh):
    T = x.shape[0]

    def body(x_ref, g_ref, sc_ref, sh_ref, h_ref):
        xn, _ = _rms(x_ref[...])
        h_ref[...] = (xn * g_ref[...] * (1.0 + sc_ref[...]) + sh_ref[...]).astype(BF16)

    return _rows("pre_norm", body, [x, g, sc, sh], [_sds((T, D_MODEL), BF16)], T)[0]


def _res_norm(x, mix, gate, gpost, g2, sc2, sh2):
    T = x.shape[0]

    def body(x_ref, mix_ref, gate_ref, gp_ref, g2_ref, sc_ref, sh_ref, x1_ref, h2_ref):
        mh, _ = _rms(mix_ref[...])
        x1 = x_ref[...] + gate_ref[...] * (mh * gp_ref[...])
        x1_ref[...] = x1
        xn, _ = _rms(x1)
        h2_ref[...] = (xn * g2_ref[...] * (1.0 + sc_ref[...]) + sh_ref[...]).astype(BF16)

    return _rows("res_norm", body, [x, mix, gate, gpost, g2, sc2, sh2],
                 [_sds((T, D_MODEL), F32), _sds((T, D_MODEL), BF16)], T)


def _final_loss(x1, ff, tgt, gate, gpost):
    T = x1.shape[0]

    def body(x1_ref, ff_ref, t_ref, gate_ref, gp_ref, dy_ref, dff_ref, acc_ref, loss_ref):
        @pl.when(pl.program_id(0) == 0)
        def _():
            acc_ref[...] = jnp.zeros_like(acc_ref)
            loss_ref[...] = jnp.zeros_like(loss_ref)

        fh, r = _rms(ff_ref[...])
        gate, gp = gate_ref[...], gp_ref[...]
        e = x1_ref[...] + gate * (fh * gp) - t_ref[...]
        loss_ref[...] += 0.5 * jnp.sum(jnp.mean(e * e, axis=-1, keepdims=True))
        dy = e * (1.0 / D_MODEL)
        dy_ref[...] = dy
        acc_ref[0:1, :] += _colsum(dy * fh * gp)
        acc_ref[1:2, :] += _colsum(dy * gate * fh)
        dff_ref[...] = _rms_bwd(dy * gate * gp, fh, r).astype(BF16)

    return _rows("final_loss", body, [x1, ff, tgt, gate, gpost],
                 [_sds((T, D_MODEL), F32), _sds((T, D_MODEL), BF16),
                  _sds((8, D_MODEL), F32), _sds((1, 128), F32)], T)


def _res_norm_bwd(x1, mix, dh2, dy, sc2, gate, g2, gpost):
    T = x1.shape[0]

    def body(x1_ref, mix_ref, dh_ref, dy_ref, sc_ref, gate_ref, g2_ref, gp_ref,
             dx1_ref, dmix_ref, acc_ref):
        @pl.when(pl.program_id(0) == 0)
        def _():
            acc_ref[...] = jnp.zeros_like(acc_ref)

        xn, r1 = _rms(x1_ref[...])
        dh, sc, g2 = dh_ref[...], sc_ref[...], g2_ref[...]
        acc_ref[0:1, :] += _colsum(dh * xn * g2)
        acc_ref[1:2, :] += _colsum(dh)
        acc_ref[2:3, :] += _colsum(dh * (1.0 + sc) * xn)
        dx1 = dy_ref[...] + _rms_bwd(dh * (1.0 + sc) * g2, xn, r1)
        dx1_ref[...] = dx1
        mh, rm = _rms(mix_ref[...])
        gate, gp = gate_ref[...], gp_ref[...]
        acc_ref[3:4, :] += _colsum(dx1 * mh * gp)
        acc_ref[4:5, :] += _colsum(dx1 * gate * mh)
        dmix_ref[...] = _rms_bwd(dx1 * gate * gp, mh, rm).astype(BF16)

    return _rows("res_norm_bwd", body, [x1, mix, dh2, dy, sc2, gate, g2, gpost],
                 [_sds((T, D_MODEL), F32), _sds((T, D_MODEL), BF16), _sds((8, D_MODEL), F32)], T)


def _pre_norm_bwd(x, dh, dx1, g, sc):
    T = x.shape[0]

    def body(x_ref, dh_ref, dx1_ref, g_ref, sc_ref, dx_ref, acc_ref):
        @pl.when(pl.program_id(0) == 0)
        def _():
            acc_ref[...] = jnp.zeros_like(acc_ref)

        xn, r = _rms(x_ref[...])
        dh, sc, g = dh_ref[...], sc_ref[...], g_ref[...]
        acc_ref[0:1, :] += _colsum(dh * xn * g)
        acc_ref[1:2, :] += _colsum(dh)
        acc_ref[2:3, :] += _colsum(dh * (1.0 + sc) * xn)
        dx_ref[...] = dx1_ref[...] + _rms_bwd(dh * (1.0 + sc) * g, xn, r)

    return _rows("pre_norm_bwd", body, [x, dh, dx1, g, sc],
                 [_sds((T, D_MODEL), F32), _sds((8, D_MODEL), F32)], T)


def _rope_tables(pos_col, inv_freq):
    T = pos_col.shape[0]

    def body(p_ref, f_ref, c_ref, s_ref):
        ang = p_ref[...].astype(F32) * f_ref[...]
        lane = lax.broadcasted_iota(jnp.int32, ang.shape, 1)
        c_ref[...] = jnp.cos(ang)
        s_ref[...] = jnp.where(lane % HEAD_DIM < HEAD_DIM // 2, -1.0, 1.0) * jnp.sin(ang)

    return _rows("rope_tables", body, [pos_col, inv_freq],
                 [_sds((T, 128), F32), _sds((T, 128), F32)], T, tr=512)


def _swap_halves(t):
    W = t.shape[1]
    lane = lax.broadcasted_iota(jnp.int32, t.shape, 1)
    half = HEAD_DIM // 2
    return jnp.where(lane % HEAD_DIM < half, pltpu.roll(t, W - half, 1), pltpu.roll(t, half, 1))


def _widen(c, W):
    return c if W == 128 else jnp.concatenate([c] * (W // 128), axis=1)


def _rope(t, c, s):
    W = t.shape[1]
    return t * _widen(c, W) + _swap_halves(t) * _widen(s, W)


def _unrope(dy, c, s):
    W = dy.shape[1]
    return dy * _widen(c, W) + _swap_halves(dy * _widen(s, W))


def _attn_mask(n):
    qi = lax.broadcasted_iota(jnp.int32, (ATTN_BLK, 2 * ATTN_BLK), 0)
    kj = lax.broadcasted_iota(jnp.int32, (ATTN_BLK, 2 * ATTN_BLK), 1)
    rel = kj - ATTN_BLK
    return (rel <= qi) & (qi - rel < WINDOW) & ((n > 0) | (kj >= ATTN_BLK))


def _attn_load(cur, prv, cc, sc, cp, sp):
    x, xp = cur[...], prv[...]
    q = _rope(x[:, :512], cc[...], sc[...]) * (HEAD_DIM ** -0.5)
    k = jnp.concatenate([_rope(xp[:, 512:640], cp[...], sp[...]),
                         _rope(x[:, 512:640], cc[...], sc[...])], axis=0)
    v = jnp.concatenate([xp[:, 640:768], x[:, 640:768]], axis=0)
    return q, k, v


ROLLED = tuple(h for h in range(N_Q_HEADS) if h % 2 != h // (N_Q_HEADS // N_KV_HEADS))


def _pair_heads(t):
    half = lax.broadcasted_iota(jnp.int32, (ATTN_BLK, 128), 1) // HEAD_DIM
    return jnp.stack([jnp.where(half == h % 2, t[:, 128 * (h // 2):128 * (h // 2) + 128], 0.0)
                      for h in range(N_Q_HEADS)])


def _kv_heads(t):
    half = lax.broadcasted_iota(jnp.int32, t.shape, 1) // HEAD_DIM
    tr = pltpu.roll(t, HEAD_DIM, 1)
    return jnp.stack([jnp.where(half == h % 2, tr if h in ROLLED else t, 0.0)
                      for h in range(N_Q_HEADS)])


def _sink_column(snk):
    return jnp.stack([jnp.full((1, 1), snk[0, h], F32) for h in range(N_Q_HEADS)])


def _attn_probs(qh, kh, mask, sink):
    s = jnp.where(mask, _bdot(qh, kh, 2, 2), NEG_INF)
    m = jnp.maximum(jnp.max(s, axis=-1, keepdims=True), sink)
    p = jnp.exp(s - m)
    es = jnp.exp(sink - m)
    rl = 1.0 / (jnp.sum(p, axis=-1, keepdims=True) + es)
    return p, es, rl


def _attn_specs(nb):
    blk = lambda w: pl.BlockSpec((ATTN_BLK, w), lambda n: (n, 0))
    prv = lambda w: pl.BlockSpec((ATTN_BLK, w), lambda n: (jnp.maximum(n - 1, 0), 0))
    return [blk(768), prv(768), blk(128), blk(128), prv(128), prv(128),
            pl.BlockSpec(memory_space=pltpu.SMEM)]


def _attn_fwd(pa, cos, sin, sinks):
    T = pa.shape[0]
    nb = T // ATTN_BLK

    def body(cur, prv, cc, sc, cp, sp, snk, y_ref):
        n = pl.program_id(0)
        q, k, v = _attn_load(cur, prv, cc, sc, cp, sp)
        qh, kh, vh = _pair_heads(q).astype(BF16), _kv_heads(k).astype(BF16), _kv_heads(v).astype(BF16)
        p, _, rl = _attn_probs(qh, kh, _attn_mask(n), _sink_column(snk))
        o = _bdot(p.astype(BF16), vh, 2, 1) * rl
        for pair in range(N_Q_HEADS // 2):
            y_ref[:, 128 * pair:128 * pair + 128] = (o[2 * pair] + o[2 * pair + 1]).astype(BF16)

    return pl.pallas_call(
        body, name="attn_fwd", grid=(nb,), in_specs=_attn_specs(nb),
        out_specs=pl.BlockSpec((ATTN_BLK, 512), lambda n: (n, 0)),
        out_shape=_sds((T, 512), BF16), compiler_params=_params(("parallel",)))(
            pa, pa, cos, sin, cos, sin, sinks)


def _attn_bwd(pa, cos, sin, sinks, dy):
    T = pa.shape[0]
    nb = T // ATTN_BLK

    def body(cur, prv, cc, sc, cp, sp, snk, dy_ref, dq_ref, dcur_ref, dprv_ref, dsink_ref):
        n = pl.program_id(0)

        @pl.when(n == 0)
        def _():
            dsink_ref[...] = jnp.zeros_like(dsink_ref)

        q, k, v = _attn_load(cur, prv, cc, sc, cp, sp)
        qh, kh, vh = _pair_heads(q).astype(BF16), _kv_heads(k).astype(BF16), _kv_heads(v).astype(BF16)
        p, es, rl = _attn_probs(qh, kh, _attn_mask(n), _sink_column(snk))
        pn = p * rl
        do = _pair_heads(dy_ref[...]).astype(BF16)
        dp = _bdot(do, vh, 2, 2)
        delta = jnp.sum(pn * dp, axis=-1, keepdims=True)
        ds = (pn * (dp - delta)).astype(BF16)
        dsink = es * rl * delta
        dq = _bdot(ds, kh, 2, 1) * (HEAD_DIM ** -0.5)
        dkh = _bdot_rows(ds, qh)
        dvh = _bdot_rows(pn.astype(BF16), do)

        def fold(t):
            same = [t[h] for h in range(N_Q_HEADS) if h not in ROLLED]
            moved = [t[h] for h in ROLLED]
            return sum(same[1:], same[0]) + pltpu.roll(sum(moved[1:], moved[0]), HEAD_DIM, 1)

        dk, dv = fold(dkh), fold(dvh)
        for h in range(N_Q_HEADS):
            dsink_ref[h:h + 1, :] += -jnp.sum(dsink[h])
        for pair in range(N_Q_HEADS // 2):
            dq_ref[:, 128 * pair:128 * pair + 128] = _unrope(
                dq[2 * pair] + dq[2 * pair + 1], cc[...], sc[...]).astype(BF16)
        dcur_ref[:, 0:128] = dk[ATTN_BLK:]
        dcur_ref[:, 128:256] = dv[ATTN_BLK:]
        dprv_ref[:, 0:128] = dk[:ATTN_BLK]
        dprv_ref[:, 128:256] = dv[:ATTN_BLK]

    blk = lambda w: pl.BlockSpec((ATTN_BLK, w), lambda n: (n, 0))
    return pl.pallas_call(
        body, name="attn_bwd", grid=(nb,), in_specs=_attn_specs(nb) + [blk(512)],
        out_specs=[blk(512), blk(256), blk(256), pl.BlockSpec((8, 128), lambda n: (0, 0))],
        out_shape=[_sds((T, 512), BF16), _sds((T, 256), F32), _sds((T, 256), F32),
                   _sds((8, 128), F32)],
        compiler_params=_params(("arbitrary",)))(pa, pa, cos, sin, cos, sin, sinks, dy)


def _attn_kv_combine(dcur, dprv, cos, sin):
    T = dcur.shape[0]
    nb = T // ATTN_BLK

    def body(c_ref, p_ref, cc, sc, o_ref):
        n = pl.program_id(0)
        t = c_ref[...] + jnp.where(n < nb - 1, p_ref[...], 0.0)
        o_ref[:, 0:128] = _unrope(t[:, 0:128], cc[...], sc[...]).astype(BF16)
        o_ref[:, 128:256] = t[:, 128:256].astype(BF16)

    blk = lambda w: pl.BlockSpec((ATTN_BLK, w), lambda n: (n, 0))
    nxt = pl.BlockSpec((ATTN_BLK, 256), lambda n: (jnp.minimum(n + 1, nb - 1), 0))
    return pl.pallas_call(
        body, name="attn_kv_combine", grid=(nb,), in_specs=[blk(256), nxt, blk(128), blk(128)],
        out_specs=blk(256), out_shape=_sds((T, 256), BF16),
        compiler_params=_params(("parallel",)))(dcur, dprv, cos, sin)


CONV_COLS = 2 * MLSTM_HEADS * MLSTM_HEAD_DIM


def _conv_pre(cur_ref, halo_ref, w_ref, b_ref, i, tr):
    xx = jnp.concatenate([jnp.where(i > 0, halo_ref[...], 0.0), cur_ref[...]], axis=0)
    taps = [(pltpu.roll(xx, CONV_WIDTH - 1 - j, 0) if j < CONV_WIDTH - 1 else xx)[8:8 + tr]
            for j in range(CONV_WIDTH)]
    pre = b_ref[...]
    for j in range(CONV_WIDTH):
        pre = pre + taps[j] * w_ref[j:j + 1, :]
    return pre, taps


def _conv_specs(T, tr):
    return [pl.BlockSpec((tr, CONV_COLS), lambda i: (i, 0)),
            pl.BlockSpec((8, CONV_COLS), lambda i: (jnp.maximum(i * (tr // 8) - 1, 0), 0)),
            pl.BlockSpec((CONV_WIDTH, CONV_COLS), lambda i: (0, 0)),
            pl.BlockSpec((1, CONV_COLS), lambda i: (0, 0))]


def _conv_fwd(pm, w, b):
    T = pm.shape[0]
    tr = min(ROW_TILE, T)

    def body(cur_ref, halo_ref, w_ref, b_ref, o_ref):
        pre, _ = _conv_pre(cur_ref, halo_ref, w_ref, b_ref, pl.program_id(0), tr)
        o_ref[...] = pre * _sigmoid(pre)

    return pl.pallas_call(
        body, name="conv_fwd", grid=(T // tr,), in_specs=_conv_specs(T, tr),
        out_specs=pl.BlockSpec((tr, CONV_COLS), lambda i: (i, 0)),
        out_shape=_sds((T, CONV_COLS), F32), compiler_params=_params(("parallel",)))(pm, pm, w, b)


def _conv_bwd_pre(pm, w, b, dqk):
    T = pm.shape[0]
    tr = min(ROW_TILE, T)

    def body(cur_ref, halo_ref, w_ref, b_ref, d_ref, dpre_ref, acc_ref):
        i = pl.program_id(0)

        @pl.when(i == 0)
        def _():
            acc_ref[...] = jnp.zeros_like(acc_ref)

        pre, taps = _conv_pre(cur_ref, halo_ref, w_ref, b_ref, i, tr)
        sg = _sigmoid(pre)
        dpre = d_ref[...] * (sg * (1.0 + pre * (1.0 - sg)))
        dpre_ref[...] = dpre
        for j in range(CONV_WIDTH):
            acc_ref[j:j + 1, :] += _colsum(dpre * taps[j])
        acc_ref[CONV_WIDTH:CONV_WIDTH + 1, :] += _colsum(dpre)

    return pl.pallas_call(
        body, name="conv_bwd_pre", grid=(T // tr,),
        in_specs=_conv_specs(T, tr) + [pl.BlockSpec((tr, CONV_COLS), lambda i: (i, 0))],
        out_specs=[pl.BlockSpec((tr, CONV_COLS), lambda i: (i, 0)),
                   pl.BlockSpec((8, CONV_COLS), lambda i: (0, 0))],
        out_shape=[_sds((T, CONV_COLS), F32), _sds((8, CONV_COLS), F32)],
        compiler_params=_params(("arbitrary",)))(pm, pm, w, b, dqk)


def _conv_bwd_in(dpre, w):
    T = dpre.shape[0]
    tr = min(ROW_TILE, T)
    nt = T // tr

    def body(cur_ref, halo_ref, w_ref, o_ref):
        i = pl.program_id(0)
        yy = jnp.concatenate([cur_ref[...], jnp.where(i < nt - 1, halo_ref[...], 0.0)], axis=0)
        du = cur_ref[...] * w_ref[CONV_WIDTH - 1:CONV_WIDTH, :]
        for j in range(CONV_WIDTH - 1):
            k = CONV_WIDTH - 1 - j
            du = du + pltpu.roll(yy, tr + 8 - k, 0)[:tr] * w_ref[j:j + 1, :]
        o_ref[...] = du.astype(BF16)

    return pl.pallas_call(
        body, name="conv_bwd_in", grid=(nt,),
        in_specs=[pl.BlockSpec((tr, CONV_COLS), lambda i: (i, 0)),
                  pl.BlockSpec((8, CONV_COLS),
                               lambda i: (jnp.minimum((i + 1) * (tr // 8), T // 8 - 1), 0)),
                  pl.BlockSpec((CONV_WIDTH, CONV_COLS), lambda i: (0, 0))],
        out_specs=pl.BlockSpec((tr, CONV_COLS), lambda i: (i, 0)),
        out_shape=_sds((T, CONV_COLS), BF16), compiler_params=_params(("parallel",)))(dpre, dpre, w)


def _log_sigmoid(x):
    return jnp.minimum(x, 0.0) - jnp.log1p(jnp.exp(-jnp.abs(x)))


def _chunk_cumsum(x, axis):
    idx = lax.broadcasted_iota(jnp.int32, x.shape, axis) % MLSTM_CHUNK
    k = 1
    while k < MLSTM_CHUNK:
        x = x + jnp.where(idx >= k, pltpu.roll(x, k, axis), 0.0)
        k *= 2
    return x


def _chunk_rev_cumsum(x, axis):
    n = x.shape[axis]
    idx = lax.broadcasted_iota(jnp.int32, x.shape, axis) % MLSTM_CHUNK
    k = 1
    while k < MLSTM_CHUNK:
        x = x + jnp.where(idx < MLSTM_CHUNK - k, pltpu.roll(x, n - k, axis), 0.0)
        k *= 2
    return x


def _mlstm_gates(gc_ref, bc_ref, gr_ref, br_ref):
    gc = gc_ref[...] + bc_ref[...]
    gr = gr_ref[...] + br_ref[...]
    return gc, _chunk_cumsum(_log_sigmoid(gc), 0), gr, _chunk_cumsum(_log_sigmoid(gr), 1)


def _heads(ref, base=0):
    D = MLSTM_HEAD_DIM
    return jnp.stack([ref[:, base + D * h:base + D * h + D] for h in range(MLSTM_HEADS)])


def _mlstm_inputs(q_ref, k_ref, v_ref, gc, bc, gr, br):
    H = MLSTM_HEADS
    q, v = _heads(q_ref), _heads(v_ref)
    ks = _heads(k_ref) * (MLSTM_HEAD_DIM ** -0.5)
    return dict(
        q=q, ks=ks, qb=q.astype(BF16), kb=ks.astype(BF16), vb=v.astype(BF16),
        b_col=jnp.stack([bc[:, H + h:H + h + 1] for h in range(H)]),
        i_col=jnp.stack([gc[:, h:h + 1] for h in range(H)]),
        b_row=jnp.stack([br[H + h:H + h + 1, :] for h in range(H)]),
        i_row=jnp.stack([gr[h:h + 1, :] for h in range(H)]))


def _mlstm_head(f, c_prev, n_prev, m_prev):
    L = MLSTM_CHUNK
    q, qb = f["q"], f["qb"]
    t = lax.broadcasted_iota(jnp.int32, (1, 2 * L, 2 * L), 1)
    s = lax.broadcasted_iota(jnp.int32, (1, 2 * L, 2 * L), 2)
    mask = (t // L == s // L) & (s <= t)
    d = jnp.where(mask, f["b_col"] - f["b_row"] + f["i_row"], NEG_INF)
    row = lax.broadcasted_iota(jnp.int32, (1, 2 * L, 1), 1)
    inter = f["b_col"] + jnp.where(row < L, m_prev[0], m_prev[1])
    m_t = jnp.maximum(inter, jnp.max(d, axis=-1, keepdims=True))
    w_intra = jnp.exp(d - m_t)
    w_inter = jnp.exp(inter - m_t)
    sc = _bdot(qb, f["kb"], 2, 2) * w_intra
    qc = jnp.concatenate([_bdot(qb[:, :L], c_prev[0].astype(BF16), 2, 1),
                          _bdot(qb[:, L:], c_prev[1].astype(BF16), 2, 1)], axis=1)
    qn = jnp.concatenate([jnp.sum(q[:, :L] * n_prev[0], axis=-1, keepdims=True),
                          jnp.sum(q[:, L:] * n_prev[1], axis=-1, keepdims=True)], axis=1)
    num = _bdot(sc.astype(BF16), f["vb"], 2, 1) + w_inter * qc
    den = jnp.sum(sc, axis=-1, keepdims=True) + w_inter * qn
    return dict(f, w_intra=w_intra, w_inter=w_inter, sc=sc, qc=qc, qn=qn, num=num, den=den,
                floor=jnp.exp(-m_t))


def _mlstm_update(f, ch, c, n, m):
    L = MLSTM_CHUNK
    rows = slice(L * ch, L * ch + L)
    b_col = f["b_col"][:, rows]
    g_last = b_col[:, L - 1:L]
    a_col = g_last - b_col + f["i_col"][:, rows]
    m_new = jnp.maximum(g_last + m, jnp.max(a_col, axis=1, keepdims=True))
    decay = jnp.exp(g_last + m - m_new)
    e_a = jnp.exp(a_col - m_new)
    kw = f["ks"][:, rows] * e_a
    c_new = decay * c + _bdot_rows(kw.astype(BF16), f["vb"][:, rows])
    n_new = decay * n + jnp.sum(kw, axis=1, keepdims=True)
    return c_new, n_new, m_new, decay, e_a, kw


def _mlstm_specs(T, order):
    blk = lambda w, col: pl.BlockSpec((STEP_ROWS, w), lambda s: (order(s), col))
    return [blk(512, 0), blk(512, 1), blk(512, 2), blk(128, 0),
            pl.BlockSpec((1, 128), lambda s: (0, 0)),
            pl.BlockSpec((8, STEP_ROWS), lambda s: (0, order(s))),
            pl.BlockSpec((8, 128), lambda s: (0, 0))]


def _lanes(m):
    return jnp.broadcast_to(m, m.shape[:-1] + (128,))


def _mlstm_fwd(qk, pm, gcol, bcol, grow, brow):
    T = qk.shape[0]
    steps = T // STEP_ROWS
    H, D = MLSTM_HEADS, MLSTM_HEAD_DIM

    def body(q_ref, k_ref, v_ref, gc_ref, bc_ref, gr_ref, br_ref, h_ref, cs_ref, ns_ref, ms_ref,
             c_st, n_st, m_st):
        @pl.when(pl.program_id(0) == 0)
        def _():
            c_st[...] = jnp.zeros_like(c_st)
            n_st[...] = jnp.zeros_like(n_st)
            m_st[...] = jnp.zeros_like(m_st)

        f = _mlstm_inputs(q_ref, k_ref, v_ref, *_mlstm_gates(gc_ref, bc_ref, gr_ref, br_ref))
        c0, n0, m0 = c_st[...], n_st[...], m_st[:, :, 0:1]
        c1, n1, m1, _, _, _ = _mlstm_update(f, 0, c0, n0, m0)
        c2, n2, m2, _, _, _ = _mlstm_update(f, 1, c1, n1, m1)
        f = _mlstm_head(f, (c0, c1), (n0, n1), (m0, m1))
        h = f["num"] / jnp.maximum(jnp.abs(f["den"]), f["floor"])
        for hd in range(H):
            h_ref[:, D * hd:D * hd + D] = h[hd]
        cs_ref[0], cs_ref[1] = c0, c1
        ns_ref[0], ns_ref[1] = n0, n1
        ms_ref[0], ms_ref[1] = _lanes(m0), _lanes(m1)
        c_st[...], n_st[...], m_st[...] = c2, n2, _lanes(m2)

    vec = pl.BlockSpec((2, H, 1, 128), lambda s: (s, 0, 0, 0))
    return pl.pallas_call(
        body, name="mlstm_fwd", grid=(steps,), in_specs=_mlstm_specs(T, lambda s: s),
        out_specs=[pl.BlockSpec((STEP_ROWS, 512), lambda s: (s, 0)),
                   pl.BlockSpec((2, H, 128, 128), lambda s: (s, 0, 0, 0)), vec, vec],
        out_shape=[_sds((T, 512), F32), _sds((2 * steps, H, 128, 128), F32),
                   _sds((2 * steps, H, 1, 128), F32), _sds((2 * steps, H, 1, 128), F32)],
        scratch_shapes=[pltpu.VMEM((H, 128, 128), F32), pltpu.VMEM((H, 1, 128), F32),
                        pltpu.VMEM((H, 1, 128), F32)],
        compiler_params=_params(("arbitrary",)))(qk, qk, pm, gcol, bcol, grow, brow)


def _mlstm_bwd(qk, pm, gcol, bcol, grow, brow, cs, ns, ms, dh):
    T = qk.shape[0]
    steps = T // STEP_ROWS
    H, L, D = MLSTM_HEADS, MLSTM_CHUNK, MLSTM_HEAD_DIM
    rev = lambda s: steps - 1 - s

    def body(q_ref, k_ref, v_ref, gc_ref, bc_ref, gr_ref, br_ref, cs_ref, ns_ref, ms_ref, dh_ref,
             dqk_ref, dv_ref, dgc_ref, dgr_ref, dc_st, dn_st):
        @pl.when(pl.program_id(0) == 0)
        def _():
            dc_st[...] = jnp.zeros_like(dc_st)
            dn_st[...] = jnp.zeros_like(dn_st)

        f = _mlstm_inputs(q_ref, k_ref, v_ref, *_mlstm_gates(gc_ref, bc_ref, gr_ref, br_ref))
        c_prev = (cs_ref[0], cs_ref[1])
        n_prev = (ns_ref[0], ns_ref[1])
        m_prev = (ms_ref[0, :, :, 0:1], ms_ref[1, :, :, 0:1])
        f = _mlstm_head(f, c_prev, n_prev, m_prev)
        big = jnp.abs(f["den"]) > f["floor"]
        rden = 1.0 / jnp.where(big, jnp.abs(f["den"]), f["floor"])
        dnum = _heads(dh_ref) * rden
        hdh = jnp.sum(f["num"] * dnum, axis=-1, keepdims=True)
        dden = jnp.where(big, -hdh * rden * jnp.sign(f["den"]), 0.0)
        dnum_b = dnum.astype(BF16)
        dsc = _bdot(dnum_b, f["vb"], 2, 2) + dden
        g = dsc * f["sc"]
        dv = _bdot_rows(f["sc"].astype(BF16), dnum_b)
        dqk_ = (dsc * f["w_intra"]).astype(BF16)
        dq = _bdot(dqk_, f["kb"], 2, 1)
        dks = _bdot_rows(dqk_, f["qb"])
        wdn = f["w_inter"] * dnum
        wdn_b = wdn.astype(BF16)
        wdd = f["w_inter"] * dden
        u = jnp.sum(f["qc"] * wdn, axis=-1, keepdims=True) + wdd * f["qn"]
        dks_s, dv_s, z_s, dg_s = [None, None], [None, None], [None, None], [None, None]
        dcn, dnn = dc_st[...], dn_st[...]
        for ch in (1, 0):
            rows = slice(L * ch, L * ch + L)
            _, _, _, decay, e_a, kw = _mlstm_update(f, ch, c_prev[ch], n_prev[ch], m_prev[ch])
            dcn_b = dcn.astype(BF16)
            dkw = _bdot(f["vb"][:, rows], dcn_b, 2, 2) + dnn
            dks_s[ch] = e_a * dkw
            dv_s[ch] = _bdot(kw.astype(BF16), dcn_b, 2, 1)
            z_s[ch] = e_a * jnp.sum(f["ks"][:, rows] * dkw, axis=-1, keepdims=True)
            dg_s[ch] = jnp.sum(z_s[ch], axis=1, keepdims=True) + decay * (
                jnp.sum(c_prev[ch] * dcn, axis=(1, 2), keepdims=True)
                + jnp.sum(n_prev[ch] * dnn, axis=(1, 2), keepdims=True))
            dcn = decay * dcn + _bdot_rows(f["qb"][:, rows], wdn_b[:, rows])
            dnn = decay * dnn + jnp.sum(wdd[:, rows] * f["q"][:, rows], axis=1, keepdims=True)
        dc_st[...], dn_st[...] = dcn, dnn
        dq = dq + jnp.concatenate(
            [_bdot(wdn_b[:, :L], c_prev[0].astype(BF16), 2, 2) + wdd[:, :L] * n_prev[0],
             _bdot(wdn_b[:, L:], c_prev[1].astype(BF16), 2, 2) + wdd[:, L:] * n_prev[1]], axis=1)
        dks = (dks + jnp.concatenate(dks_s, axis=1)) * (D ** -0.5)
        dv = dv + jnp.concatenate(dv_s, axis=1)
        z = jnp.concatenate(z_s, axis=1)
        row = lax.broadcasted_iota(jnp.int32, (1, STEP_ROWS, 1), 1)
        dg_col = jnp.where(row == L - 1, dg_s[0], 0.0) + jnp.where(row == 2 * L - 1, dg_s[1], 0.0)
        db_col = jnp.sum(g, axis=-1, keepdims=True) + u - z + dg_col
        g_row = jnp.sum(g, axis=1, keepdims=True)
        lane = lax.broadcasted_iota(jnp.int32, (STEP_ROWS, 128), 1)
        sub = lax.broadcasted_iota(jnp.int32, (8, STEP_ROWS), 0)
        dgc = jnp.zeros((STEP_ROWS, 128), F32)
        dgr = jnp.zeros((8, STEP_ROWS), F32)
        for hd in range(H):
            dgc = dgc + jnp.where(lane == hd, z[hd], 0.0) + jnp.where(lane == H + hd, db_col[hd], 0.0)
            dgr = dgr + jnp.where(sub == hd, g_row[hd], 0.0) - jnp.where(sub == H + hd, g_row[hd], 0.0)
            dqk_ref[:, D * hd:D * hd + D] = dq[hd]
            dqk_ref[:, H * D + D * hd:H * D + D * hd + D] = dks[hd]
            dv_ref[:, D * hd:D * hd + D] = dv[hd].astype(BF16)
        dgc_ref[...] = dgc
        dgr_ref[...] = dgr

    return pl.pallas_call(
        body, name="mlstm_bwd", grid=(steps,),
        in_specs=_mlstm_specs(T, rev) + [
            pl.BlockSpec((2, H, 128, 128), lambda s: (rev(s), 0, 0, 0)),
            pl.BlockSpec((2, H, 1, 128), lambda s: (rev(s), 0, 0, 0)),
            pl.BlockSpec((2, H, 1, 128), lambda s: (rev(s), 0, 0, 0)),
            pl.BlockSpec((STEP_ROWS, 512), lambda s: (rev(s), 0))],
        out_specs=[pl.BlockSpec((STEP_ROWS, 1024), lambda s: (rev(s), 0)),
                   pl.BlockSpec((STEP_ROWS, 512), lambda s: (rev(s), 0)),
                   pl.BlockSpec((STEP_ROWS, 128), lambda s: (rev(s), 0)),
                   pl.BlockSpec((8, STEP_ROWS), lambda s: (0, rev(s)))],
        out_shape=[_sds((T, 1024), F32), _sds((T, 512), BF16), _sds((T, 128), F32), _sds((8, T), F32)],
        scratch_shapes=[pltpu.VMEM((H, 128, 128), F32), pltpu.VMEM((H, 1, 128), F32)],
        compiler_params=_params(("arbitrary",)))(qk, qk, pm, gcol, bcol, grow, brow, cs, ns, ms, dh)


def _gate_bwd(dgc, dgr_t, gcol, bcol):
    T = dgc.shape[0]

    def body(a_ref, b_ref, g_ref, bias_ref, o_ref, acc_ref):
        @pl.when(pl.program_id(0) == 0)
        def _():
            acc_ref[...] = jnp.zeros_like(acc_ref)

        d = a_ref[...] + b_ref[...]
        lane = lax.broadcasted_iota(jnp.int32, d.shape, 1)
        is_f = (lane >= MLSTM_HEADS) & (lane < 2 * MLSTM_HEADS)
        dlogf = _chunk_rev_cumsum(jnp.where(is_f, d, 0.0), 0)
        out = jnp.where(is_f, dlogf * _sigmoid(-(g_ref[...] + bias_ref[...])), d)
        o_ref[...] = out.astype(BF16)
        acc_ref[0:1, :] += _colsum(out)

    return _rows("gate_bwd", body, [dgc, dgr_t, gcol, bcol],
                 [_sds((T, 128), BF16), _sds((8, 128), F32)], T)


def _head_norm(h, mu_axis=-1):
    mu = jnp.mean(h, axis=-1, keepdims=True)
    hc = h - mu
    r = lax.rsqrt(jnp.mean(hc * hc, axis=-1, keepdims=True) + NORM_EPS)
    return hc * r, r


def _mlstm_out(hm, pm, w):
    T = hm.shape[0]
    D = MLSTM_HEAD_DIM

    def body(h_ref, o_ref, w_ref, y_ref):
        for hd in range(MLSTM_HEADS):
            cols = slice(D * hd, D * hd + D)
            hn, _ = _head_norm(h_ref[:, cols])
            y_ref[:, cols] = (_sigmoid(o_ref[:, cols]) * hn * w_ref[:, cols]).astype(BF16)

    tr = min(ROW_TILE, T)
    return pl.pallas_call(
        body, name="mlstm_out", grid=(T // tr,),
        in_specs=[pl.BlockSpec((tr, 512), lambda i: (i, 0)), pl.BlockSpec((tr, 512), lambda i: (i, 3)),
                  pl.BlockSpec((1, 512), lambda i: (0, 0))],
        out_specs=pl.BlockSpec((tr, 512), lambda i: (i, 0)), out_shape=_sds((T, 512), BF16),
        compiler_params=_params(("parallel",)))(hm, pm, w)


def _mlstm_out_bwd(hm, pm, w, dy):
    T = hm.shape[0]
    D = MLSTM_HEAD_DIM
    tr = min(ROW_TILE, T)

    def body(h_ref, o_ref, w_ref, dy_ref, dh_ref, do_ref, acc_ref):
        @pl.when(pl.program_id(0) == 0)
        def _():
            acc_ref[...] = jnp.zeros_like(acc_ref)

        for hd in range(MLSTM_HEADS):
            cols = slice(D * hd, D * hd + D)
            hn, r = _head_norm(h_ref[:, cols])
            sg = _sigmoid(o_ref[:, cols])
            dy, w = dy_ref[:, cols], w_ref[:, cols]
            do_ref[:, cols] = (dy * hn * w * sg * (1.0 - sg)).astype(BF16)
            dyn = dy * sg
            acc_ref[0:1, cols] += _colsum(dyn * hn)
            dhn = dyn * w
            dh_ref[:, cols] = r * (dhn - jnp.mean(dhn, axis=-1, keepdims=True)
                                   - hn * jnp.mean(dhn * hn, axis=-1, keepdims=True))

    return pl.pallas_call(
        body, name="mlstm_out_bwd", grid=(T // tr,),
        in_specs=[pl.BlockSpec((tr, 512), lambda i: (i, 0)), pl.BlockSpec((tr, 512), lambda i: (i, 3)),
                  pl.BlockSpec((1, 512), lambda i: (0, 0)), pl.BlockSpec((tr, 512), lambda i: (i, 0))],
        out_specs=[pl.BlockSpec((tr, 512), lambda i: (i, 0)), pl.BlockSpec((tr, 512), lambda i: (i, 0)),
                   pl.BlockSpec((8, 512), lambda i: (0, 0))],
        out_shape=[_sds((T, 512), F32), _sds((T, 512), BF16), _sds((8, 512), F32)],
        compiler_params=_params(("arbitrary",)))(hm, pm, w, dy)


def _adamw(name, w, g, m, v, tr=64):
    R, C = w.shape
    tr = min(tr, R)
    c1 = 1.0 - ADAM_B1 ** ADAM_STEP
    c2 = 1.0 - ADAM_B2 ** ADAM_STEP

    def body(w_ref, g_ref, m_ref, v_ref, d_ref, mo_ref, vo_ref):
        g = g_ref[...]
        m = ADAM_B1 * m_ref[...] + (1.0 - ADAM_B1) * g
        v = ADAM_B2 * v_ref[...] + (1.0 - ADAM_B2) * (g * g)
        mo_ref[...] = m
        vo_ref[...] = v
        d_ref[...] = -ADAM_LR * ((m / c1) / (jnp.sqrt(v / c2) + ADAM_EPS) + ADAM_WD * w_ref[...])

    spec = pl.BlockSpec((tr, C), lambda i: (i, 0))
    return pl.pallas_call(
        body, name=name, grid=(R // tr,), in_specs=[spec] * 4, out_specs=[spec] * 3,
        out_shape=[_sds((R, C), F32)] * 3, compiler_params=_params(("parallel",)))(w, g, m, v)


def _place():
    return lax.axis_index("x"), lax.axis_index("y"), lax.axis_index("c")


def _all_gather8(name, blk, space):
    m, n = blk.shape

    def body(x_ref, out_ref, send_sems, recv_sems, local_sem):
        x, y, c = _place()
        me, sibling = (x, y, c), (x, y, 1 - c)
        chips = [(1 - x, y), (x, 1 - y), (1 - x, 1 - y)]

        def rows(px, py, pc):
            return out_ref.at[pl.ds((4 * px + 2 * py + pc) * m, m), :]

        def copy(k, block, to, src=None):
            return pltpu.make_async_remote_copy(
                src_ref=rows(*block) if src is None else src, dst_ref=rows(*block),
                send_sem=send_sems.at[k], recv_sem=recv_sems.at[k],
                device_id=to, device_id_type=MESH)

        mine = pltpu.make_async_copy(x_ref, rows(*me), local_sem)
        mine.start()
        first = [copy(0, me, sibling, src=x_ref)]
        first += [copy(1 + j, me, (*chip, c), src=x_ref) for j, chip in enumerate(chips)]
        for cp in first:
            cp.start()
        passed = [copy(4 + j, (*chip, c), sibling) for j, chip in enumerate(chips)]
        for j, chip in enumerate(chips):
            copy(1 + j, (*chip, c), me).wait_recv()
            passed[j].start()
        copy(0, sibling, me).wait_recv()
        for j, chip in enumerate(chips):
            copy(4 + j, (*chip, 1 - c), me).wait_recv()
        for cp in first + passed:
            cp.wait_send()
        mine.wait()

    return pl.pallas_call(
        body, name=name, out_shape=_sds((8 * m, n), blk.dtype),
        in_specs=[pl.BlockSpec(memory_space=space)], out_specs=pl.BlockSpec(memory_space=space),
        scratch_shapes=[pltpu.SemaphoreType.DMA((7,)), pltpu.SemaphoreType.DMA((7,)),
                        pltpu.SemaphoreType.DMA],
        compiler_params=pltpu.CompilerParams(vmem_limit_bytes=VMEM_LIMIT))(blk)


def _hbm_specs(n):
    return [pl.BlockSpec(memory_space=pl.ANY)] * n


def _gather_groups(name, blocks):
    nw = len(blocks)

    def body(*refs):
        srcs, outs = refs[:nw], refs[nw:2 * nw]
        send_sems, recv_sems, local_sems = refs[2 * nw:]
        x, y, c = _place()
        me, sibling = (x, y, c), (x, y, 1 - c)
        chips = [(1 - x, y), (x, 1 - y), (1 - x, 1 - y)]

        def slab(w, px, py, pc):
            return outs[w].at[2 * px + py, pl.ds(0, blocks[w].shape[0]), pc]

        def copy(w, k, block, to, src=None):
            return pltpu.make_async_remote_copy(
                src_ref=slab(w, *block) if src is None else src, dst_ref=slab(w, *block),
                send_sem=send_sems.at[w, k], recv_sem=recv_sems.at[w, k],
                device_id=to, device_id_type=MESH)

        mine = [pltpu.make_async_copy(srcs[w], slab(w, *me), local_sems.at[w]) for w in range(nw)]
        first = []
        for w in range(nw):
            mine[w].start()
            first.append(copy(w, 0, me, sibling, src=srcs[w]))
            first += [copy(w, 1 + j, me, (*chip, c), src=srcs[w]) for j, chip in enumerate(chips)]
        for cp in first:
            cp.start()
        passed = []
        for j, chip in enumerate(chips):
            for w in range(nw):
                copy(w, 1 + j, (*chip, c), me).wait_recv()
                passed.append(copy(w, 4 + j, (*chip, c), sibling))
                passed[-1].start()
        for w in range(nw):
            copy(w, 0, sibling, me).wait_recv()
            for j, chip in enumerate(chips):
                copy(w, 4 + j, (*chip, 1 - c), me).wait_recv()
        for cp in first + passed:
            cp.wait_send()
        for w in range(nw):
            mine[w].wait()

    return pl.pallas_call(
        body, name=name,
        out_shape=[_sds((4, b.shape[0], 2) + b.shape[1:], b.dtype) for b in blocks],
        in_specs=_hbm_specs(nw), out_specs=_hbm_specs(nw),
        scratch_shapes=[pltpu.SemaphoreType.DMA((nw, 7)), pltpu.SemaphoreType.DMA((nw, 7)),
                        pltpu.SemaphoreType.DMA((nw,))])(*blocks)


def _swap_sibling(name, srcs, halves=False):
    nw = len(srcs)

    def body(*refs):
        src_refs, dst_refs, send_sems, recv_sems = refs[:nw], refs[nw:2 * nw], refs[2 * nw], refs[2 * nw + 1]
        x, y, c = _place()
        cps = []
        for w in range(nw):
            s = src_refs[w]
            if halves:
                s = s.at[pl.ds(0, srcs[w].shape[0]), pl.ds(0, srcs[w].shape[1]), 1 - c]
            cps.append(pltpu.make_async_remote_copy(
                src_ref=s, dst_ref=dst_refs[w], send_sem=send_sems.at[w], recv_sem=recv_sems.at[w],
                device_id=(x, y, 1 - c), device_id_type=MESH))
        for cp in cps:
            cp.start()
        for cp in cps:
            cp.wait()

    shapes = [(s.shape[:2] + s.shape[3:]) if halves else s.shape for s in srcs]
    return pl.pallas_call(
        body, name=name, out_shape=[_sds(sh, s.dtype) for sh, s in zip(shapes, srcs)],
        in_specs=_hbm_specs(nw), out_specs=_hbm_specs(nw),
        scratch_shapes=[pltpu.SemaphoreType.DMA((nw,)), pltpu.SemaphoreType.DMA((nw,))])(*srcs)


def _split_start(name, srcs, lands, copies, per_array):
    nw = len(srcs)

    def body(*refs):
        send_sems, recv_sems, token = refs[2 * nw], refs[2 * nw + 1], refs[-1]
        for w in range(nw):
            for k, (s, d, dev) in enumerate(copies(refs[w], refs[nw + w], *_place())):
                pltpu.make_async_remote_copy(
                    src_ref=s, dst_ref=d, send_sem=send_sems.at[w * per_array + k],
                    recv_sem=recv_sems.at[w * per_array + k], device_id=dev, device_id_type=MESH).start()
        token[...] = jnp.zeros_like(token)

    hbm, sem = pl.BlockSpec(memory_space=pltpu.HBM), pl.BlockSpec(memory_space=pltpu.SEMAPHORE)
    arrays = list(srcs) + list(lands)
    out = pl.pallas_call(
        body, name=name,
        out_shape=(pltpu.SemaphoreType.DMA((nw * per_array,)), pltpu.SemaphoreType.DMA((nw * per_array,)),
                   *[pltpu.HBM(a.shape, a.dtype) for a in arrays], _sds((8, 128), F32)),
        in_specs=[hbm] * (2 * nw),
        out_specs=(sem, sem, *[hbm] * (2 * nw), pl.BlockSpec(memory_space=pltpu.VMEM)),
        input_output_aliases={i: 2 + i for i in range(2 * nw)},
        compiler_params=pltpu.CompilerParams(has_side_effects=pltpu.SideEffectType.DATAFLOW_SIDE_EFFECTING))(
            *[pltpu.with_memory_space_constraint(a, pltpu.HBM) for a in arrays])
    return out[0], out[1], out[2:2 + nw], out[2 + nw:2 + 2 * nw], out[-1]


def _split_wait(name, started, after, waits, per_array):
    send_sems, recv_sems, srcs, lands, _ = started
    nw = len(srcs)

    def body(*refs):
        send_sems, recv_sems = refs[2 * nw], refs[2 * nw + 1]
        x, y, c = _place()
        for w in range(nw):
            for k, (s, d) in enumerate(waits(refs[w], refs[nw + w], x, y, c)):
                cp = pltpu.make_async_remote_copy(
                    src_ref=s, dst_ref=d, send_sem=send_sems.at[w * per_array + k],
                    recv_sem=recv_sems.at[w * per_array + k], device_id=(x, y, 1 - c),
                    device_id_type=MESH)
                cp.wait_send()
                cp.wait_recv()

    hbm, sem = pl.BlockSpec(memory_space=pltpu.HBM), pl.BlockSpec(memory_space=pltpu.SEMAPHORE)
    arrays = list(srcs) + list(lands)
    out = pl.pallas_call(
        body, name=name, out_shape=tuple(pltpu.HBM(a.shape, a.dtype) for a in arrays),
        in_specs=[hbm] * (2 * nw) + [sem, sem, pl.BlockSpec(memory_space=pl.ANY)],
        out_specs=tuple([hbm] * (2 * nw)), input_output_aliases={i: i for i in range(2 * nw)},
        compiler_params=pltpu.CompilerParams(has_side_effects=pltpu.SideEffectType.DATAFLOW_SIDE_EFFECTING))(
            *arrays, send_sems, recv_sems, after)
    return list(out[nw:])


def _other_chips(x, y):
    return [(1 - x, y), (x, 1 - y), (1 - x, 1 - y)]


def _gather_sends(src_ref, land_ref, x, y, c):
    to = land_ref.at[2 * x + y, pl.ds(0, land_ref.shape[1]), c]
    return [(src_ref, to, (x, y, 1 - c))] + [(src_ref, to, (px, py, c)) for px, py in _other_chips(x, y)]


def _gather_lands(src_ref, land_ref, x, y, c):
    g = pl.ds(0, land_ref.shape[1])
    return [(src_ref, land_ref.at[2 * x + y, g, 1 - c])] + [
        (src_ref, land_ref.at[2 * px + py, g, c]) for px, py in _other_chips(x, y)]


def _scatter_sends(src_ref, land_ref, x, y, c):
    return [(src_ref.at[2 * px + py], land_ref.at[2 * x + y], (px, py, c)) for px, py in _other_chips(x, y)]


def _scatter_lands(src_ref, land_ref, x, y, c):
    return [(src_ref.at[2 * x + y], land_ref.at[2 * px + py]) for px, py in _other_chips(x, y)]


def _forward_sibling(name, land):
    def body(land_ref, out_ref, send_sems, recv_sems):
        x, y, c = _place()
        g = pl.ds(0, land_ref.shape[1])
        cps = [pltpu.make_async_remote_copy(
            src_ref=land_ref.at[2 * px + py, g, c], dst_ref=out_ref.at[2 * px + py, g, c],
            send_sem=send_sems.at[j], recv_sem=recv_sems.at[j], device_id=(x, y, 1 - c),
            device_id_type=MESH) for j, (px, py) in enumerate(_other_chips(x, y))]
        for cp in cps:
            cp.start()
        for j, (px, py) in enumerate(_other_chips(x, y)):
            slot = out_ref.at[2 * px + py, g, 1 - c]
            pltpu.make_async_remote_copy(src_ref=slot, dst_ref=slot, send_sem=send_sems.at[j],
                                         recv_sem=recv_sems.at[j], device_id=(x, y, 1 - c),
                                         device_id_type=MESH).wait_recv()
        for cp in cps:
            cp.wait_send()

    return pl.pallas_call(
        body, name=name, out_shape=_sds(land.shape, land.dtype),
        in_specs=_hbm_specs(1), out_specs=pl.BlockSpec(memory_space=pl.ANY), input_output_aliases={0: 0},
        scratch_shapes=[pltpu.SemaphoreType.DMA((3,)), pltpu.SemaphoreType.DMA((3,))])(land)


def _pair_sum(name, full, got, core):
    _, g, _, m, n = full.shape

    def body(c_ref, a_ref, b_ref, o_ref):
        o_ref[...] = (a_ref[...].astype(F32) + b_ref[...].astype(F32)).astype(o_ref.dtype)

    slab = pl.BlockSpec((None, None, m, n), lambda s, w, c: (s, w, 0, 0))
    return pl.pallas_call(
        body, name=name,
        grid_spec=pltpu.PrefetchScalarGridSpec(
            num_scalar_prefetch=1, grid=(4, g),
            in_specs=[pl.BlockSpec((None, None, None, m, n), lambda s, w, c: (s, w, c[0], 0, 0)), slab],
            out_specs=slab),
        out_shape=_sds(got.shape, BF16),
        compiler_params=_params(("parallel", "parallel")))(core, full, got)


def _sum4(name, a):
    _, g, m, n = a.shape

    def body(a_ref, o_ref):
        acc = a_ref[0].astype(F32)
        for s in range(1, 4):
            acc = acc + a_ref[s].astype(F32)
        o_ref[...] = acc

    return pl.pallas_call(body, name=name, grid=(g,),
                          in_specs=[pl.BlockSpec((4, None, m, n), lambda w: (0, w, 0, 0))],
                          out_specs=pl.BlockSpec((None, m, n), lambda w: (w, 0, 0)),
                          out_shape=_sds((g, m, n), F32), compiler_params=_params(("parallel",)))(a)


def _small_update(gathered, w, m, v):
    n = w.shape[1]
    tn = 2048
    c1 = 1.0 - ADAM_B1 ** ADAM_STEP
    c2 = 1.0 - ADAM_B2 ** ADAM_STEP

    def body(g_ref, w_ref, m_ref, v_ref, go_ref, d_ref, mo_ref, vo_ref):
        g = g_ref[0:1, :]
        for d in range(1, 8):
            g = g + g_ref[d:d + 1, :]
        go_ref[...] = g
        m = ADAM_B1 * m_ref[...] + (1.0 - ADAM_B1) * g
        v = ADAM_B2 * v_ref[...] + (1.0 - ADAM_B2) * (g * g)
        mo_ref[...] = m
        vo_ref[...] = v
        d_ref[...] = -ADAM_LR * ((m / c1) / (jnp.sqrt(v / c2) + ADAM_EPS) + ADAM_WD * w_ref[...])

    row = pl.BlockSpec((1, tn), lambda i: (0, i))
    return pl.pallas_call(
        body, name="small_update", grid=(n // tn,),
        in_specs=[pl.BlockSpec((8, tn), lambda i: (0, i)), row, row, row], out_specs=[row] * 4,
        out_shape=[_sds((1, n), F32)] * 4, compiler_params=_params(("parallel",)))(gathered, w, m, v)


def _swiglu(ps, es):
    g, u = ps
    return g * _sigmoid(g) * u, g, u


def _swiglu_bwd(ps, es):
    g, u = es[0].astype(F32), es[1].astype(F32)
    sg = _sigmoid(g)
    return ps[0] * u * (sg * (1.0 + g * (1.0 - sg))), ps[0] * (g * sg)


def _merge(ps, es):
    ga, gm = [e.astype(F32) for e in es]
    return _sigmoid(ga) * ps[0] + _sigmoid(gm) * ps[1], ps[0], ps[1]


def _merge_bwd(ps, es):
    a, b, ga, gm = [e.astype(F32) for e in es]
    sa, sm = _sigmoid(ga), _sigmoid(gm)
    dm = ps[0]
    return dm * sa, dm * sm, dm * a * (sa * (1.0 - sa)), dm * b * (sm * (1.0 - sm))


W_IN_PIECES = (("q", 512), ("kv", 256), ("mqk", 1024), ("mv", 512), ("mo", 512), ("if", 8),
               ("ga", 1024), ("gm", 1024))


def _local_step(x, tgt, pos_col, mod, sp, W, ffn_weights, ffn_grads, mixer_grads):
    sh_m, sc_m, gate_m, sh_f, sc_f, gate_f = mod
    w_a = jnp.concatenate([W["q"], W["kv"]], axis=0)
    w_m = jnp.concatenate([W["mqk"], W["mv"], W["mo"]], axis=0)
    w_g = jnp.concatenate([W["ga"], W["gm"]], axis=0)

    h = _pre_norm(x, sp["g_pre_mix"], sc_m, sh_m)
    pa, = _mm("proj_attn", [[(h, w_a)]], [], _first, [F32], cn=256, nt=True)
    pm, = _mm("proj_mlstm", [[(h, w_m)]], [], _first, [F32], cn=512, nt=True)
    pif, = _mm("proj_gates", [[(h, W["if"])]], [], _first, [F32], cn=128, nt=True)
    pg, = _mm("proj_branch_gates", [[(h, w_g)]], [], _first, [BF16], cn=512, nt=True)
    inv = ROPE_THETA ** (-2.0 * jnp.arange(HEAD_DIM // 2, dtype=F32) / HEAD_DIM)
    cos, sin = _rope_tables(pos_col, jnp.tile(inv, 4).reshape(1, 128))
    ya = _attn_fwd(pa, cos, sin, sp["sinks"])
    qk = _conv_fwd(pm, sp["conv_w"], sp["conv_b"])
    bcol = jnp.pad(sp["b_if"], ((0, 0), (0, 120)))
    brow = jnp.broadcast_to(sp["b_if"].reshape(8, 1), (8, 128))
    grow = pif[:, :8].T
    hm, cs, ns, ms = _mlstm_fwd(qk, pm, pif, bcol, grow, brow)
    ym = _mlstm_out(hm, pm, sp["norm_w"])
    merged, br_a, br_m = _mm("branches", [[(ya, W["ba"])], [(ym, W["bm"])]],
                             [(pg, 0), (pg, 1)], _merge, [BF16, BF16, BF16], cn=512, nt=True)
    mix, = _mm("mix_out", [[(merged, W["out"])]], [], _first, [F32], cn=512)
    x1, h2 = _res_norm(x, mix, gate_m, sp["g_post_mix"], sp["g_pre_ffn"], sc_f, sh_f)
    w_fg, w_fu, w_fd = ffn_weights(mix)
    act, gt, up = _mm("ffn_in", [[(h2, w_fg)], [(h2, w_fu)]], [], _swiglu, [BF16] * 3,
                      cn=256, nt=True)
    ff, = _mm("ffn_down", [[(act, w_fd)]], [], _first, [F32], cn=512)
    dy, dff, acc_l, loss = _final_loss(x1, ff, tgt, gate_f, sp["g_post_ffn"])

    G = {}
    dgt, dup = _mm("ffn_down_bwd", [[(dff, w_fd)]], [gt, up], _swiglu_bwd, [BF16, BF16],
                   cn=256, nt=True)
    g_fd = _mm_tn("dw_ffn_down", act, dff, BF16, 1408, 512, 512)
    dh2, = _mm("ffn_in_bwd", [[(dgt, w_fg), (dup, w_fu)]], [], _first, [F32], cn=512)
    g_fg = _mm_tn("dw_ffn_gate", dgt, h2, BF16, 1408, 1024, 512)
    g_fu = _mm_tn("dw_ffn_up", dup, h2, BF16, 1408, 1024, 512)
    tie = ffn_grads(g_fg, g_fu, g_fd)
    dx1, dmix, acc_r = _res_norm_bwd(x1, mix, dh2, dy, sc_f + tie, gate_m, sp["g_pre_ffn"],
                                     sp["g_post_mix"])
    d_a, d_m, dga, dgm = _mm("mix_out_bwd", [[(dmix, W["out"])]],
                             [br_a, br_m, (pg, 0), (pg, 1)], _merge_bwd,
                             [BF16] * 4, cn=512, nt=True)
    G["out"] = _mm_tn("dw_out", merged, dmix, BF16, 1024, 512, 512)
    dya, = _mm("branch_attn_bwd", [[(d_a, W["ba"])]], [], _first, [F32], cn=512)
    dym, = _mm("branch_mlstm_bwd", [[(d_m, W["bm"])]], [], _first, [F32], cn=512)
    G["ba"] = _mm_tn("dw_branch_attn", d_a, ya, BF16, 1024, 512, 512)
    G["bm"] = _mm_tn("dw_branch_mlstm", d_m, ym, BF16, 1024, 512, 512)
    dhm, do_m, acc_n = _mlstm_out_bwd(hm, pm, sp["norm_w"], dym)
    dqk, dv_m, dgc, dgr = _mlstm_bwd(qk, pm, pif, bcol, grow, brow, cs, ns, ms, dhm)
    dif, acc_g = _gate_bwd(dgc, jnp.pad(dgr.T, ((0, 0), (0, 120))), pif, bcol)
    dpre, acc_c = _conv_bwd_pre(pm, sp["conv_w"], sp["conv_b"], dqk)
    du = _conv_bwd_in(dpre, sp["conv_w"])
    dq_a, dcur, dprv, dsink = _attn_bwd(pa, cos, sin, sp["sinks"], dya)
    dkv = _attn_kv_combine(dcur, dprv, cos, sin)
    dproj = {"q": dq_a, "kv": dkv, "mqk": du, "mv": dv_m, "mo": do_m, "if": dif, "ga": dga, "gm": dgm}
    for k, _ in W_IN_PIECES:
        G[k] = _mm_tn("dw_in_" + k, dproj[k], h, BF16, dproj[k].shape[1], 1024, 512)
    w_tied = dict(W, **{"if": W["if"] + mixer_grads(G).astype(BF16)})
    dh, = _mm("proj_bwd", [[(dproj[k], w_tied[k]) for k, _ in W_IN_PIECES]], [], _first, [F32], cn=512)
    dx, acc_p = _pre_norm_bwd(x, dh, dx1, sp["g_pre_mix"], sc_m)

    small = {
        "mod": jnp.concatenate([acc_p[1], acc_p[0], acc_r[3], acc_r[1], acc_r[0], acc_l[0]]),
        "g_pre_mix": acc_p[2], "g_post_mix": acc_r[4], "b_if": acc_g[0, :8],
        "conv_w": acc_c[:CONV_WIDTH].reshape(-1), "conv_b": acc_c[CONV_WIDTH],
        "sinks": dsink[:, 0], "norm_w": acc_n[0], "g_pre_ffn": acc_r[2], "g_post_ffn": acc_l[1]}
    return loss, dx, small


IN_WIDTH = sum(n for _, n in W_IN_PIECES)
IN_SHARD = IN_WIDTH // 4
IN_SHARD_PAD = -(-IN_SHARD // 32) * 32


def _split_w_in(w_in_t):
    out, off = {}, 0
    for k, n in W_IN_PIECES:
        out[k] = w_in_t[off:off + n]
        off += n
    out["if"] = jnp.pad(out["if"], ((0, 120), (0, 0)))
    return out


def _halves(a):
    return a.reshape(4, 2, a.shape[0] // 8, a.shape[1])


SMALL = (("b_ada", 6144), ("g_pre_mix", 1024), ("g_post_mix", 1024), ("b_if", 128), ("conv_w", 4096),
         ("conv_b", 1024), ("sinks", 128), ("norm_w", 512), ("g_pre_ffn", 1024), ("g_post_ffn", 1024))
SMALL_LEN = 8 * 2048


def _pack_small(vals):
    parts = []
    for k, n in SMALL:
        v = vals[k].reshape(-1)
        parts.append(jnp.pad(v, (0, n - v.shape[0])))
    flat = jnp.concatenate(parts)
    return jnp.pad(flat, (0, SMALL_LEN - flat.shape[0]))


def _unpack_small(flat, shapes):
    out, off = {}, 0
    for k, n in SMALL:
        size = 1
        for d in shapes[k]:
            size *= d
        out[k] = flat[off:off + size].reshape(shapes[k])
        off += n
    return out


def kernel(x, c, positions, w_ada, b_ada, g_pre_mix, g_post_mix, w_in, b_if, conv_w, conv_b, attn_sinks, mlstm_norm_w, w_branch_attn, w_branch_mlstm, w_out, g_pre_ffn, g_post_ffn, w_ffn_gate, w_ffn_up, w_ffn_down, loss_target, m_w_ada, m_b_ada, m_g_pre_mix, m_g_post_mix, m_w_in, m_b_if, m_conv_w, m_conv_b, m_attn_sinks, m_mlstm_norm_w, m_w_branch_attn, m_w_branch_mlstm, m_w_out, m_g_pre_ffn, m_g_post_ffn, m_w_ffn_gate, m_w_ffn_up, m_w_ffn_down, v_w_ada, v_b_ada, v_g_pre_mix, v_g_post_mix, v_w_in, v_b_if, v_conv_w, v_conv_b, v_attn_sinks, v_mlstm_norm_w, v_w_branch_attn, v_w_branch_mlstm, v_w_out, v_g_pre_ffn, v_g_post_ffn, v_w_ffn_gate, v_w_ffn_up, v_w_ffn_down):
    xi, yi, ci = _place()
    chip = 2 * xi + yi
    dev = 2 * chip + ci
    T = x.shape[1]
    ada_cols = w_ada.shape[2]

    def my_half(a):
        n = a.shape[0] // 2
        return lax.dynamic_slice_in_dim(a, ci * n, n, axis=0).astype(BF16)

    blk = jnp.concatenate([c.reshape(-1), conv_w.reshape(-1)]).reshape(8, 256)
    got = _all_gather8("gather_cond", blk, pltpu.VMEM).reshape(8, 2048)
    c_all = got[:, :D_MODEL].astype(BF16)
    conv_full = got[::2, D_MODEL:].reshape(4, CONV_WIDTH, -1).transpose(1, 0, 2).reshape(CONV_WIDTH, -1)

    b_sh = lax.dynamic_slice_in_dim(b_ada, chip * ada_cols, ada_cols, axis=1)
    mod_part, = _mm("ada_mod", [[(c_all, w_ada[0].astype(BF16))]], [b_sh],
                    lambda ps, es: (ps[0] + es[0],), [F32], cn=512, tm=8)
    mod_all = _all_gather8("gather_mod", mod_part, pltpu.VMEM).reshape(4, 2, 8, ada_cols)[:, 0]
    mod = lax.dynamic_index_in_dim(mod_all, dev, axis=1, keepdims=False).reshape(6, 1, D_MODEL)

    w_in_t = jnp.pad(w_in[0].T, ((0, IN_SHARD_PAD - IN_SHARD), (0, 0)))
    blocks = [my_half(w_in_t)[None], my_half(w_out[0])[None],
              jnp.stack([my_half(w_branch_attn[0].T), my_half(w_branch_mlstm[0].T)])]
    g_in, g_out, g_br = _gather_groups("gather_weights", blocks)
    W = {"out": g_out.reshape(D_MODEL, D_MODEL), "ba": g_br[:, 0].reshape(D_MODEL, -1),
         "bm": g_br[:, 1].reshape(D_MODEL, -1)}
    W.update(_split_w_in(g_in.reshape(4, IN_SHARD_PAD, D_MODEL)[:, :IN_SHARD].reshape(IN_WIDTH, D_MODEL)))

    ffn_blk = jnp.stack([my_half(w_ffn_gate[0].T), my_half(w_ffn_up[0].T), my_half(w_ffn_down[0])])
    ffn_blk, W["out"] = lax.optimization_barrier((ffn_blk, W["out"]))
    ffn_land = lax.dynamic_update_slice(jnp.zeros((4, 3, 2) + ffn_blk.shape[1:], BF16),
                                        ffn_blk[None, :, None], (chip, 0, ci, 0, 0))
    ffn_started = _split_start("ffn_gather_start", [ffn_blk], [ffn_land], _gather_sends, 4)
    mod = mod + ffn_started[4][0, 0]

    core = ci.reshape(1).astype(jnp.int32)
    sent = {}

    def ffn_weights(after):
        land, = _split_wait("ffn_gather_wait", ffn_started, after, _gather_lands, 4)
        land = _forward_sibling("ffn_gather_forward", land)
        return [land[:, i].reshape(D_FF, D_MODEL) for i in range(3)]

    def scatter_start(name, groups):
        theirs = _swap_sibling(name + "_pair", groups, halves=True)
        pairs = [_pair_sum("%s_pair_sum_%d" % (name, i), a, b, core)
                 for i, (a, b) in enumerate(zip(groups, theirs))]
        sent[name] = _split_start(name + "_start", pairs, [p + jnp.zeros((), BF16) for p in pairs],
                                  _scatter_sends, 3)
        return sent[name][4][0, 0]

    def ffn_grads(g_fg, g_fu, g_fd):
        return scatter_start("rs_ffn", [jnp.stack([_halves(g_fg), _halves(g_fu), _halves(g_fd)], axis=1)])

    def mixer_grads(G):
        g_in_t = jnp.concatenate([G[k][:n] for k, n in W_IN_PIECES]).reshape(4, IN_SHARD, D_MODEL)
        g_in_t = jnp.pad(g_in_t, ((0, 0), (0, IN_SHARD_PAD - IN_SHARD), (0, 0)))
        return scatter_start("rs_mix", [g_in_t.reshape(4, 1, 2, IN_SHARD_PAD // 2, D_MODEL),
                                        _halves(G["out"])[:, None],
                                        jnp.stack([_halves(G["ba"]), _halves(G["bm"])], axis=1)])

    sp = {"g_pre_mix": g_pre_mix, "g_post_mix": g_post_mix, "b_if": b_if, "conv_w": conv_full,
          "conv_b": conv_b, "sinks": attn_sinks, "norm_w": mlstm_norm_w, "g_pre_ffn": g_pre_ffn,
          "g_post_ffn": g_post_ffn}
    loss, dx, small = _local_step(x[0], loss_target[0], positions.reshape(T, 1), [mod[i] for i in range(6)],
                                  sp, W, ffn_weights, ffn_grads, mixer_grads)

    landed = (_split_wait("rs_ffn_wait", sent["rs_ffn"], dx, _scatter_lands, 3)
              + _split_wait("rs_mix_wait", sent["rs_mix"], dx, _scatter_lands, 3))
    reds = [_sum4("rs_chip_sum_%d" % i, a) for i, a in enumerate(landed)]
    others = _swap_sibling("rs_share", reds)
    s_ffn, s_in, s_out, s_br = [
        jnp.concatenate([jnp.where(ci == 0, r, o), jnp.where(ci == 0, o, r)], axis=1)
        for r, o in zip(reds, others)]
    gsh = {"fg": s_ffn[0].T, "fu": s_ffn[1].T, "fd": s_ffn[2], "w_in": s_in[0, :IN_SHARD].T,
           "out": s_out[0], "ba": s_br[0].T, "bm": s_br[1].T}

    small["b_ada"] = small.pop("mod")
    vec = _pack_small(small).reshape(8, 2048)
    g_all = _all_gather8("gather_small", vec, pltpu.VMEM).reshape(8, SMALL_LEN)
    dmod_sh = lax.dynamic_slice_in_dim(g_all[:, :6 * D_MODEL], chip * ada_cols, ada_cols, axis=1)
    g_w_ada = _mm_tn("dw_ada", c_all, dmod_sh.astype(BF16), F32, D_MODEL, 512, 8)

    smalls = {"b_ada": (b_ada, m_b_ada, v_b_ada), "g_pre_mix": (g_pre_mix, m_g_pre_mix, v_g_pre_mix),
              "g_post_mix": (g_post_mix, m_g_post_mix, v_g_post_mix), "b_if": (b_if, m_b_if, v_b_if),
              "conv_w": None, "conv_b": (conv_b, m_conv_b, v_conv_b),
              "sinks": (attn_sinks, m_attn_sinks, v_attn_sinks),
              "norm_w": (mlstm_norm_w, m_mlstm_norm_w, v_mlstm_norm_w),
              "g_pre_ffn": (g_pre_ffn, m_g_pre_ffn, v_g_pre_ffn),
              "g_post_ffn": (g_post_ffn, m_g_post_ffn, v_g_post_ffn)}
    shapes = {k: (t[0].shape if t is not None else (1, CONV_WIDTH, D_MODEL)) for k, t in smalls.items()}
    zeros = jnp.zeros((CONV_WIDTH * D_MODEL,), F32)
    packs = [_pack_small({k: (t[i] if t is not None else zeros) for k, t in smalls.items()}).reshape(1, -1)
             for i in range(3)]
    s_out = [_unpack_small(o[0], shapes) for o in _small_update(g_all, *packs)]
    g_conv = lax.dynamic_slice_in_dim(s_out[0]["conv_w"], chip * conv_w.shape[2], conv_w.shape[2], axis=2)

    res = {}
    for k, t in smalls.items():
        if t is not None:
            res[k] = tuple(o[k] for o in s_out)
    res["conv_w"] = (g_conv, *[o[None] for o in _adamw("adam_conv_w", conv_w[0], g_conv[0], m_conv_w[0], v_conv_w[0])])
    res["w_ada"] = (g_w_ada[None], *[o[None] for o in _adamw("adam_w_ada", w_ada[0], g_w_ada, m_w_ada[0], v_w_ada[0])])
    bigs = {"w_in": (w_in, m_w_in, v_w_in), "ba": (w_branch_attn, m_w_branch_attn, v_w_branch_attn),
            "bm": (w_branch_mlstm, m_w_branch_mlstm, v_w_branch_mlstm), "out": (w_out, m_w_out, v_w_out),
            "fg": (w_ffn_gate, m_w_ffn_gate, v_w_ffn_gate), "fu": (w_ffn_up, m_w_ffn_up, v_w_ffn_up),
            "fd": (w_ffn_down, m_w_ffn_down, v_w_ffn_down)}
    for k, (w, m, v) in bigs.items():
        res[k] = (gsh[k][None], *[o[None] for o in _adamw("adam_" + k, w[0], gsh[k], m[0], v[0])])

    order = ("w_ada", "b_ada", "g_pre_mix", "g_post_mix", "w_in", "b_if", "conv_w", "conv_b", "sinks",
             "norm_w", "ba", "bm", "out", "g_pre_ffn", "g_post_ffn", "fg", "fu", "fd")
    total = lax.psum(loss[0, 0], ("x", "y", "c"))
    return (total, dx[None], *[res[k][0] for k in order], *[res[k][1] for k in order],
            *[res[k][2] for k in order], *[res[k][3] for k in order])
```

```python
import functools

import jax
import jax.numpy as jnp
from jax import lax
from jax.experimental import pallas as pl
from jax.experimental.pallas import tpu as pltpu

F32, BF16 = jnp.float32, jnp.bfloat16
MESH = pl.DeviceIdType.MESH

D_MODEL = 1024
N_Q_HEADS, N_KV_HEADS, HEAD_DIM, WINDOW = 8, 2, 64, 128
ROPE_THETA = 10000.0
MLSTM_HEADS, MLSTM_HEAD_DIM, MLSTM_CHUNK, CONV_WIDTH = 4, 128, 64, 4
D_FF = 2816
NORM_EPS = 1e-6
ADAM_LR, ADAM_B1, ADAM_B2, ADAM_EPS, ADAM_WD, ADAM_STEP = 0.001, 0.9, 0.999, 1e-08, 0.01, 10

VMEM_LIMIT = 56 * 1024 * 1024
ROW_TILE = 256
MM_TM = 512
MM_TT = 1024
ATTN_BLK = WINDOW
STEP_ROWS = 2 * MLSTM_CHUNK
NEG_INF = float("-inf")


def _params(sem):
    return pltpu.CompilerParams(dimension_semantics=sem, vmem_limit_bytes=VMEM_LIMIT)


def _sds(shape, dtype):
    return jax.ShapeDtypeStruct(shape, dtype)


def _sigmoid(x):
    return 1.0 / (1.0 + jnp.exp(-x))


def _dot(a, b, ca, cb):
    return lax.dot_general(a, b, (((ca,), (cb,)), ((), ())), preferred_element_type=F32)


def _bdot(a, b, ca, cb):
    return lax.dot_general(a, b, (((ca,), (cb,)), ((0,), (0,))), preferred_element_type=F32)


def _bdot_rows(a, b):
    return jnp.stack([_dot(a[h], b[h], 0, 0) for h in range(a.shape[0])])


def _mm(name, prods, extras, epi, out_dtypes, cn, nt=False, tm=MM_TM):
    flat = [ab for p in prods for ab in p]
    counts = [len(p) for p in prods]
    M = flat[0][0].shape[0]
    N = flat[0][1].shape[0 if nt else 1]
    tm = min(tm, M)
    n_in = 2 * len(flat) + len(extras)

    def body(*refs):
        ins, outs = refs[:n_in], refs[n_in:]
        for j in range(N // cn):
            cols = slice(j * cn, (j + 1) * cn)
            k, ps = 0, []
            for cnt in counts:
                acc = None
                for _ in range(cnt):
                    b = ins[k + 1][cols, :] if nt else ins[k + 1][:, cols]
                    d = _dot(ins[k][...], b, 1, 1 if nt else 0)
                    acc = d if acc is None else acc + d
                    k += 2
                ps.append(acc)
            res = epi(ps, [r[:, cols] for r in ins[k:]])
            for o, r in zip(outs, res):
                o[:, cols] = r.astype(o.dtype)

    in_specs, args = [], []
    for a, b in flat:
        in_specs.append(pl.BlockSpec((tm, a.shape[1]), lambda i: (i, 0)))
        in_specs.append(pl.BlockSpec(b.shape, lambda i: (0, 0), pipeline_mode=pl.Buffered(1)))
        args += [a, b]
    for e in extras:
        e, off = e if isinstance(e, tuple) else (e, 0)
        rows = 1 if e.shape[0] == 1 else tm
        in_specs.append(pl.BlockSpec((rows, N), lambda i, off=off, rows=rows: (0 if rows == 1 else i, off)))
        args.append(e)
    return pl.pallas_call(
        body, name=name, grid=(M // tm,), in_specs=in_specs,
        out_specs=[pl.BlockSpec((tm, N), lambda i: (i, 0)) for _ in out_dtypes],
        out_shape=[_sds((M, N), dt) for dt in out_dtypes],
        compiler_params=_params(("parallel",)))(*args)


def _mm_tn(name, a, b, out_dtype, tk, tn, tt=MM_TT):
    T, Ka = a.shape
    N = b.shape[1]
    tt = min(tt, T)
    steps = T // tt

    def body(a_ref, b_ref, o_ref, acc):
        t = pl.program_id(2)

        @pl.when(t == 0)
        def _():
            acc[...] = jnp.zeros_like(acc)

        acc[...] += _dot(a_ref[...], b_ref[...], 0, 0)

        @pl.when(t == steps - 1)
        def _():
            o_ref[...] = acc[...].astype(o_ref.dtype)

    return pl.pallas_call(
        body, name=name, grid=(Ka // tk, N // tn, steps),
        in_specs=[pl.BlockSpec((tt, tk), lambda i, j, t: (t, i)),
                  pl.BlockSpec((tt, tn), lambda i, j, t: (t, j))],
        out_specs=pl.BlockSpec((tk, tn), lambda i, j, t: (i, j)),
        out_shape=_sds((Ka, N), out_dtype),
        scratch_shapes=[pltpu.VMEM((tk, tn), F32)],
        compiler_params=_params(("parallel", "parallel", "arbitrary")))(a, b)


def _first(ps, es):
    return (ps[0],)


def _rows(name, body, ins, out_shapes, T, tr=ROW_TILE):
    tr = min(tr, T)

    def spec(shape):
        if shape[0] == T:
            return pl.BlockSpec((tr,) + tuple(shape[1:]), lambda i: (i,) + (0,) * (len(shape) - 1))
        return pl.BlockSpec(tuple(shape), lambda i: (0,) * len(shape))

    return pl.pallas_call(
        body, name=name, grid=(T // tr,),
        in_specs=[spec(a.shape) for a in ins], out_specs=[spec(s.shape) for s in out_shapes],
        out_shape=out_shapes, compiler_params=_params(("arbitrary",)))(*ins)


def _rms(x):
    r = lax.rsqrt(jnp.mean(x * x, axis=-1, keepdims=True) + NORM_EPS)
    return x * r, r


def _rms_bwd(dxn, xn, r):
    return r * (dxn - xn * jnp.mean(dxn * xn, axis=-1, keepdims=True))


def _colsum(v):
    return jnp.sum(v, axis=0, keepdims=True)


def _pre_norm(x, g, sc, sh):
    T = x.shape[0]

    def body(x_ref, g_ref, sc_ref, sh_ref, h_ref):
        xn, _ = _rms(x_ref[...])
        h_ref[...] = (xn * g_ref[...] * (1.0 + sc_ref[...]) + sh_ref[...]).astype(BF16)

    return _rows("pre_norm", body, [x, g, sc, sh], [_sds((T, D_MODEL), BF16)], T)[0]


def _res_norm(x, mix, gate, gpost, g2, sc2, sh2):
    T = x.shape[0]

    def body(x_ref, mix_ref, gate_ref, gp_ref, g2_ref, sc_ref, sh_ref, x1_ref, h2_ref):
        mh, _ = _rms(mix_ref[...])
        x1 = x_ref[...] + gate_ref[...] * (mh * gp_ref[...])
        x1_ref[...] = x1
        xn, _ = _rms(x1)
        h2_ref[...] = (xn * g2_ref[...] * (1.0 + sc_ref[...]) + sh_ref[...]).astype(BF16)

    return _rows("res_norm", body, [x, mix, gate, gpost, g2, sc2, sh2],
                 [_sds((T, D_MODEL), F32), _sds((T, D_MODEL), BF16)], T)


def _final_loss(x1, ff, tgt, gate, gpost):
    T = x1.shape[0]

    def body(x1_ref, ff_ref, t_ref, gate_ref, gp_ref, dy_ref, dff_ref, acc_ref, loss_ref):
        @pl.when(pl.program_id(0) == 0)
        def _():
            acc_ref[...] = jnp.zeros_like(acc_ref)
            loss_ref[...] = jnp.zeros_like(loss_ref)

        fh, r = _rms(ff_ref[...])
        gate, gp = gate_ref[...], gp_ref[...]
        e = x1_ref[...] + gate * (fh * gp) - t_ref[...]
        loss_ref[...] += 0.5 * jnp.sum(jnp.mean(e * e, axis=-1, keepdims=True))
        dy = e * (1.0 / D_MODEL)
        dy_ref[...] = dy
        acc_ref[0:1, :] += _colsum(dy * fh * gp)
        acc_ref[1:2, :] += _colsum(dy * gate * fh)
        dff_ref[...] = _rms_bwd(dy * gate * gp, fh, r).astype(BF16)

    return _rows("final_loss", body, [x1, ff, tgt, gate, gpost],
                 [_sds((T, D_MODEL), F32), _sds((T, D_MODEL), BF16),
                  _sds((8, D_MODEL), F32), _sds((1, 128), F32)], T)


def _res_norm_bwd(x1, mix, dh2, dy, sc2, gate, g2, gpost):
    T = x1.shape[0]

    def body(x1_ref, mix_ref, dh_ref, dy_ref, sc_ref, gate_ref, g2_ref, gp_ref,
             dx1_ref, dmix_ref, acc_ref):
        @pl.when(pl.program_id(0) == 0)
        def _():
            acc_ref[...] = jnp.zeros_like(acc_ref)

        xn, r1 = _rms(x1_ref[...])
        dh, sc, g2 = dh_ref[...], sc_ref[...], g2_ref[...]
        acc_ref[0:1, :] += _colsum(dh * xn * g2)
        acc_ref[1:2, :] += _colsum(dh)
        acc_ref[2:3, :] += _colsum(dh * (1.0 + sc) * xn)
        dx1 = dy_ref[...] + _rms_bwd(dh * (1.0 + sc) * g2, xn, r1)
        dx1_ref[...] = dx1
        mh, rm = _rms(mix_ref[...])
        gate, gp = gate_ref[...], gp_ref[...]
        acc_ref[3:4, :] += _colsum(dx1 * mh * gp)
        acc_ref[4:5, :] += _colsum(dx1 * gate * mh)
        dmix_ref[...] = _rms_bwd(dx1 * gate * gp, mh, rm).astype(BF16)

    return _rows("res_norm_bwd", body, [x1, mix, dh2, dy, sc2, gate, g2, gpost],
                 [_sds((T, D_MODEL), F32), _sds((T, D_MODEL), BF16), _sds((8, D_MODEL), F32)], T)


def _pre_norm_bwd(x, dh, dx1, g, sc):
    T = x.shape[0]

    def body(x_ref, dh_ref, dx1_ref, g_ref, sc_ref, dx_ref, acc_ref):
        @pl.when(pl.program_id(0) == 0)
        def _():
            acc_ref[...] = jnp.zeros_like(acc_ref)

        xn, r = _rms(x_ref[...])
        dh, sc, g = dh_ref[...], sc_ref[...], g_ref[...]
        acc_ref[0:1, :] += _colsum(dh * xn * g)
        acc_ref[1:2, :] += _colsum(dh)
        acc_ref[2:3, :] += _colsum(dh * (1.0 + sc) * xn)
        dx_ref[...] = dx1_ref[...] + _rms_bwd(dh * (1.0 + sc) * g, xn, r)

    return _rows("pre_norm_bwd", body, [x, dh, dx1, g, sc],
                 [_sds((T, D_MODEL), F32), _sds((8, D_MODEL), F32)], T)


def _rope_tables(pos_col, inv_freq):
    T = pos_col.shape[0]

    def body(p_ref, f_ref, c_ref, s_ref):
        ang = p_ref[...].astype(F32) * f_ref[...]
        lane = lax.broadcasted_iota(jnp.int32, ang.shape, 1)
        c_ref[...] = jnp.cos(ang)
        s_ref[...] = jnp.where(lane % HEAD_DIM < HEAD_DIM // 2, -1.0, 1.0) * jnp.sin(ang)

    return _rows("rope_tables", body, [pos_col, inv_freq],
                 [_sds((T, 128), F32), _sds((T, 128), F32)], T, tr=512)


def _swap_halves(t):
    W = t.shape[1]
    lane = lax.broadcasted_iota(jnp.int32, t.shape, 1)
    half = HEAD_DIM // 2
    return jnp.where(lane % HEAD_DIM < half, pltpu.roll(t, W - half, 1), pltpu.roll(t, half, 1))


def _widen(c, W):
    return c if W == 128 else jnp.concatenate([c] * (W // 128), axis=1)


def _rope(t, c, s):
    W = t.shape[1]
    return t * _widen(c, W) + _swap_halves(t) * _widen(s, W)


def _unrope(dy, c, s):
    W = dy.shape[1]
    return dy * _widen(c, W) + _swap_halves(dy * _widen(s, W))


def _attn_mask(n):
    qi = lax.broadcasted_iota(jnp.int32, (ATTN_BLK, 2 * ATTN_BLK), 0)
    kj = lax.broadcasted_iota(jnp.int32, (ATTN_BLK, 2 * ATTN_BLK), 1)
    rel = kj - ATTN_BLK
    return (rel <= qi) & (qi - rel < WINDOW) & ((n > 0) | (kj >= ATTN_BLK))


def _attn_load(cur, prv, cc, sc, cp, sp):
    x, xp = cur[...], prv[...]
    q = _rope(x[:, :512], cc[...], sc[...]) * (HEAD_DIM ** -0.5)
    k = jnp.concatenate([_rope(xp[:, 512:640], cp[...], sp[...]),
                         _rope(x[:, 512:640], cc[...], sc[...])], axis=0)
    v = jnp.concatenate([xp[:, 640:768], x[:, 640:768]], axis=0)
    return q, k, v


ROLLED = tuple(h for h in range(N_Q_HEADS) if h % 2 != h // (N_Q_HEADS // N_KV_HEADS))


def _pair_heads(t):
    half = lax.broadcasted_iota(jnp.int32, (ATTN_BLK, 128), 1) // HEAD_DIM
    return jnp.stack([jnp.where(half == h % 2, t[:, 128 * (h // 2):128 * (h // 2) + 128], 0.0)
                      for h in range(N_Q_HEADS)])


def _kv_heads(t):
    half = lax.broadcasted_iota(jnp.int32, t.shape, 1) // HEAD_DIM
    tr = pltpu.roll(t, HEAD_DIM, 1)
    return jnp.stack([jnp.where(half == h % 2, tr if h in ROLLED else t, 0.0)
                      for h in range(N_Q_HEADS)])


def _sink_column(snk):
    return jnp.stack([jnp.full((1, 1), snk[0, h], F32) for h in range(N_Q_HEADS)])


def _attn_probs(qh, kh, mask, sink):
    s = jnp.where(mask, _bdot(qh, kh, 2, 2), NEG_INF)
    m = jnp.maximum(jnp.max(s, axis=-1, keepdims=True), sink)
    p = jnp.exp(s - m)
    es = jnp.exp(sink - m)
    rl = 1.0 / (jnp.sum(p, axis=-1, keepdims=True) + es)
    return p, es, rl


def _attn_specs(nb):
    blk = lambda w: pl.BlockSpec((ATTN_BLK, w), lambda n: (n, 0))
    prv = lambda w: pl.BlockSpec((ATTN_BLK, w), lambda n: (jnp.maximum(n - 1, 0), 0))
    return [blk(768), prv(768), blk(128), blk(128), prv(128), prv(128),
            pl.BlockSpec(memory_space=pltpu.SMEM)]


def _attn_fwd(pa, cos, sin, sinks):
    T = pa.shape[0]
    nb = T // ATTN_BLK

    def body(cur, prv, cc, sc, cp, sp, snk, y_ref):
        n = pl.program_id(0)
        q, k, v = _attn_load(cur, prv, cc, sc, cp, sp)
        qh, kh, vh = _pair_heads(q).astype(BF16), _kv_heads(k).astype(BF16), _kv_heads(v).astype(BF16)
        p, _, rl = _attn_probs(qh, kh, _attn_mask(n), _sink_column(snk))
        o = _bdot(p.astype(BF16), vh, 2, 1) * rl
        for pair in range(N_Q_HEADS // 2):
            y_ref[:, 128 * pair:128 * pair + 128] = (o[2 * pair] + o[2 * pair + 1]).astype(BF16)

    return pl.pallas_call(
        body, name="attn_fwd", grid=(nb,), in_specs=_attn_specs(nb),
        out_specs=pl.BlockSpec((ATTN_BLK, 512), lambda n: (n, 0)),
        out_shape=_sds((T, 512), BF16), compiler_params=_params(("parallel",)))(
            pa, pa, cos, sin, cos, sin, sinks)


def _attn_bwd(pa, cos, sin, sinks, dy):
    T = pa.shape[0]
    nb = T // ATTN_BLK

    def body(cur, prv, cc, sc, cp, sp, snk, dy_ref, dq_ref, dcur_ref, dprv_ref, dsink_ref):
        n = pl.program_id(0)

        @pl.when(n == 0)
        def _():
            dsink_ref[...] = jnp.zeros_like(dsink_ref)

        q, k, v = _attn_load(cur, prv, cc, sc, cp, sp)
        qh, kh, vh = _pair_heads(q).astype(BF16), _kv_heads(k).astype(BF16), _kv_heads(v).astype(BF16)
        p, es, rl = _attn_probs(qh, kh, _attn_mask(n), _sink_column(snk))
        pn = p * rl
        do = _pair_heads(dy_ref[...]).astype(BF16)
        dp = _bdot(do, vh, 2, 2)
        delta = jnp.sum(pn * dp, axis=-1, keepdims=True)
        ds = (pn * (dp - delta)).astype(BF16)
        dsink = es * rl * delta
        dq = _bdot(ds, kh, 2, 1) * (HEAD_DIM ** -0.5)
        dkh = _bdot_rows(ds, qh)
        dvh = _bdot_rows(pn.astype(BF16), do)

        def fold(t):
            same = [t[h] for h in range(N_Q_HEADS) if h not in ROLLED]
            moved = [t[h] for h in ROLLED]
            return sum(same[1:], same[0]) + pltpu.roll(sum(moved[1:], moved[0]), HEAD_DIM, 1)

        dk, dv = fold(dkh), fold(dvh)
        for h in range(N_Q_HEADS):
            dsink_ref[h:h + 1, :] += -jnp.sum(dsink[h])
        for pair in range(N_Q_HEADS // 2):
            dq_ref[:, 128 * pair:128 * pair + 128] = _unrope(
                dq[2 * pair] + dq[2 * pair + 1], cc[...], sc[...]).astype(BF16)
        dcur_ref[:, 0:128] = dk[ATTN_BLK:]
        dcur_ref[:, 128:256] = dv[ATTN_BLK:]
        dprv_ref[:, 0:128] = dk[:ATTN_BLK]
        dprv_ref[:, 128:256] = dv[:ATTN_BLK]

    blk = lambda w: pl.BlockSpec((ATTN_BLK, w), lambda n: (n, 0))
    return pl.pallas_call(
        body, name="attn_bwd", grid=(nb,), in_specs=_attn_specs(nb) + [blk(512)],
        out_specs=[blk(512), blk(256), blk(256), pl.BlockSpec((8, 128), lambda n: (0, 0))],
        out_shape=[_sds((T, 512), BF16), _sds((T, 256), F32), _sds((T, 256), F32),
                   _sds((8, 128), F32)],
        compiler_params=_params(("arbitrary",)))(pa, pa, cos, sin, cos, sin, sinks, dy)


def _attn_kv_combine(dcur, dprv, cos, sin):
    T = dcur.shape[0]
    nb = T // ATTN_BLK

    def body(c_ref, p_ref, cc, sc, o_ref):
        n = pl.program_id(0)
        t = c_ref[...] + jnp.where(n < nb - 1, p_ref[...], 0.0)
        o_ref[:, 0:128] = _unrope(t[:, 0:128], cc[...], sc[...]).astype(BF16)
        o_ref[:, 128:256] = t[:, 128:256].astype(BF16)

    blk = lambda w: pl.BlockSpec((ATTN_BLK, w), lambda n: (n, 0))
    nxt = pl.BlockSpec((ATTN_BLK, 256), lambda n: (jnp.minimum(n + 1, nb - 1), 0))
    return pl.pallas_call(
        body, name="attn_kv_combine", grid=(nb,), in_specs=[blk(256), nxt, blk(128), blk(128)],
        out_specs=blk(256), out_shape=_sds((T, 256), BF16),
        compiler_params=_params(("parallel",)))(dcur, dprv, cos, sin)


CONV_COLS = 2 * MLSTM_HEADS * MLSTM_HEAD_DIM


def _conv_pre(cur_ref, halo_ref, w_ref, b_ref, i, tr):
    xx = jnp.concatenate([jnp.where(i > 0, halo_ref[...], 0.0), cur_ref[...]], axis=0)
    taps = [(pltpu.roll(xx, CONV_WIDTH - 1 - j, 0) if j < CONV_WIDTH - 1 else xx)[8:8 + tr]
            for j in range(CONV_WIDTH)]
    pre = b_ref[...]
    for j in range(CONV_WIDTH):
        pre = pre + taps[j] * w_ref[j:j + 1, :]
    return pre, taps


def _conv_specs(T, tr):
    return [pl.BlockSpec((tr, CONV_COLS), lambda i: (i, 0)),
            pl.BlockSpec((8, CONV_COLS), lambda i: (jnp.maximum(i * (tr // 8) - 1, 0), 0)),
            pl.BlockSpec((CONV_WIDTH, CONV_COLS), lambda i: (0, 0)),
            pl.BlockSpec((1, CONV_COLS), lambda i: (0, 0))]


def _conv_fwd(pm, w, b):
    T = pm.shape[0]
    tr = min(ROW_TILE, T)

    def body(cur_ref, halo_ref, w_ref, b_ref, o_ref):
        pre, _ = _conv_pre(cur_ref, halo_ref, w_ref, b_ref, pl.program_id(0), tr)
        o_ref[...] = pre * _sigmoid(pre)

    return pl.pallas_call(
        body, name="conv_fwd", grid=(T // tr,), in_specs=_conv_specs(T, tr),
        out_specs=pl.BlockSpec((tr, CONV_COLS), lambda i: (i, 0)),
        out_shape=_sds((T, CONV_COLS), F32), compiler_params=_params(("parallel",)))(pm, pm, w, b)


def _conv_bwd_pre(pm, w, b, dqk):
    T = pm.shape[0]
    tr = min(ROW_TILE, T)

    def body(cur_ref, halo_ref, w_ref, b_ref, d_ref, dpre_ref, acc_ref):
        i = pl.program_id(0)

        @pl.when(i == 0)
        def _():
            acc_ref[...] = jnp.zeros_like(acc_ref)

        pre, taps = _conv_pre(cur_ref, halo_ref, w_ref, b_ref, i, tr)
        sg = _sigmoid(pre)
        dpre = d_ref[...] * (sg * (1.0 + pre * (1.0 - sg)))
        dpre_ref[...] = dpre
        for j in range(CONV_WIDTH):
            acc_ref[j:j + 1, :] += _colsum(dpre * taps[j])
        acc_ref[CONV_WIDTH:CONV_WIDTH + 1, :] += _colsum(dpre)

    return pl.pallas_call(
        body, name="conv_bwd_pre", grid=(T // tr,),
        in_specs=_conv_specs(T, tr) + [pl.BlockSpec((tr, CONV_COLS), lambda i: (i, 0))],
        out_specs=[pl.BlockSpec((tr, CONV_COLS), lambda i: (i, 0)),
                   pl.BlockSpec((8, CONV_COLS), lambda i: (0, 0))],
        out_shape=[_sds((T, CONV_COLS), F32), _sds((8, CONV_COLS), F32)],
        compiler_params=_params(("arbitrary",)))(pm, pm, w, b, dqk)


def _conv_bwd_in(dpre, w):
    T = dpre.shape[0]
    tr = min(ROW_TILE, T)
    nt = T // tr

    def body(cur_ref, halo_ref, w_ref, o_ref):
        i = pl.program_id(0)
        yy = jnp.concatenate([cur_ref[...], jnp.where(i < nt - 1, halo_ref[...], 0.0)], axis=0)
        du = cur_ref[...] * w_ref[CONV_WIDTH - 1:CONV_WIDTH, :]
        for j in range(CONV_WIDTH - 1):
            k = CONV_WIDTH - 1 - j
            du = du + pltpu.roll(yy, tr + 8 - k, 0)[:tr] * w_ref[j:j + 1, :]
        o_ref[...] = du.astype(BF16)

    return pl.pallas_call(
        body, name="conv_bwd_in", grid=(nt,),
        in_specs=[pl.BlockSpec((tr, CONV_COLS), lambda i: (i, 0)),
                  pl.BlockSpec((8, CONV_COLS),
                               lambda i: (jnp.minimum((i + 1) * (tr // 8), T // 8 - 1), 0)),
                  pl.BlockSpec((CONV_WIDTH, CONV_COLS), lambda i: (0, 0))],
        out_specs=pl.BlockSpec((tr, CONV_COLS), lambda i: (i, 0)),
        out_shape=_sds((T, CONV_COLS), BF16), compiler_params=_params(("parallel",)))(dpre, dpre, w)


def _log_sigmoid(x):
    return jnp.minimum(x, 0.0) - jnp.log1p(jnp.exp(-jnp.abs(x)))


def _chunk_cumsum(x, axis):
    idx = lax.broadcasted_iota(jnp.int32, x.shape, axis) % MLSTM_CHUNK
    k = 1
    while k < MLSTM_CHUNK:
        x = x + jnp.where(idx >= k, pltpu.roll(x, k, axis), 0.0)
        k *= 2
    return x


def _chunk_rev_cumsum(x, axis):
    n = x.shape[axis]
    idx = lax.broadcasted_iota(jnp.int32, x.shape, axis) % MLSTM_CHUNK
    k = 1
    while k < MLSTM_CHUNK:
        x = x + jnp.where(idx < MLSTM_CHUNK - k, pltpu.roll(x, n - k, axis), 0.0)
        k *= 2
    return x


def _mlstm_gates(gc_ref, bc_ref, gr_ref, br_ref):
    gc = gc_ref[...] + bc_ref[...]
    gr = gr_ref[...] + br_ref[...]
    return gc, _chunk_cumsum(_log_sigmoid(gc), 0), gr, _chunk_cumsum(_log_sigmoid(gr), 1)


def _heads(ref, base=0):
    D = MLSTM_HEAD_DIM
    return jnp.stack([ref[:, base + D * h:base + D * h + D] for h in range(MLSTM_HEADS)])


def _mlstm_inputs(q_ref, k_ref, v_ref, gc, bc, gr, br):
    H = MLSTM_HEADS
    q, v = _heads(q_ref), _heads(v_ref)
    ks = _heads(k_ref) * (MLSTM_HEAD_DIM ** -0.5)
    return dict(
        q=q, ks=ks, qb=q.astype(BF16), kb=ks.astype(BF16), vb=v.astype(BF16),
        b_col=jnp.stack([bc[:, H + h:H + h + 1] for h in range(H)]),
        i_col=jnp.stack([gc[:, h:h + 1] for h in range(H)]),
        b_row=jnp.stack([br[H + h:H + h + 1, :] for h in range(H)]),
        i_row=jnp.stack([gr[h:h + 1, :] for h in range(H)]))


def _mlstm_head(f, c_prev, n_prev, m_prev):
    L = MLSTM_CHUNK
    q, qb = f["q"], f["qb"]
    t = lax.broadcasted_iota(jnp.int32, (1, 2 * L, 2 * L), 1)
    s = lax.broadcasted_iota(jnp.int32, (1, 2 * L, 2 * L), 2)
    mask = (t // L == s // L) & (s <= t)
    d = jnp.where(mask, f["b_col"] - f["b_row"] + f["i_row"], NEG_INF)
    row = lax.broadcasted_iota(jnp.int32, (1, 2 * L, 1), 1)
    inter = f["b_col"] + jnp.where(row < L, m_prev[0], m_prev[1])
    m_t = jnp.maximum(inter, jnp.max(d, axis=-1, keepdims=True))
    w_intra = jnp.exp(d - m_t)
    w_inter = jnp.exp(inter - m_t)
    sc = _bdot(qb, f["kb"], 2, 2) * w_intra
    qc = jnp.concatenate([_bdot(qb[:, :L], c_prev[0].astype(BF16), 2, 1),
                          _bdot(qb[:, L:], c_prev[1].astype(BF16), 2, 1)], axis=1)
    qn = jnp.concatenate([jnp.sum(q[:, :L] * n_prev[0], axis=-1, keepdims=True),
                          jnp.sum(q[:, L:] * n_prev[1], axis=-1, keepdims=True)], axis=1)
    num = _bdot(sc.astype(BF16), f["vb"], 2, 1) + w_inter * qc
    den = jnp.sum(sc, axis=-1, keepdims=True) + w_inter * qn
    return dict(f, w_intra=w_intra, w_inter=w_inter, sc=sc, qc=qc, qn=qn, num=num, den=den,
                floor=jnp.exp(-m_t))


def _mlstm_update(f, ch, c, n, m):
    L = MLSTM_CHUNK
    rows = slice(L * ch, L * ch + L)
    b_col = f["b_col"][:, rows]
    g_last = b_col[:, L - 1:L]
    a_col = g_last - b_col + f["i_col"][:, rows]
    m_new = jnp.maximum(g_last + m, jnp.max(a_col, axis=1, keepdims=True))
    decay = jnp.exp(g_last + m - m_new)
    e_a = jnp.exp(a_col - m_new)
    kw = f["ks"][:, rows] * e_a
    c_new = decay * c + _bdot_rows(kw.astype(BF16), f["vb"][:, rows])
    n_new = decay * n + jnp.sum(kw, axis=1, keepdims=True)
    return c_new, n_new, m_new, decay, e_a, kw


def _mlstm_specs(T, order):
    blk = lambda w, col: pl.BlockSpec((STEP_ROWS, w), lambda s: (order(s), col))
    return [blk(512, 0), blk(512, 1), blk(512, 2), blk(128, 0),
            pl.BlockSpec((1, 128), lambda s: (0, 0)),
            pl.BlockSpec((8, STEP_ROWS), lambda s: (0, order(s))),
            pl.BlockSpec((8, 128), lambda s: (0, 0))]


def _lanes(m):
    return jnp.broadcast_to(m, m.shape[:-1] + (128,))


def _mlstm_fwd(qk, pm, gcol, bcol, grow, brow):
    T = qk.shape[0]
    steps = T // STEP_ROWS
    H, D = MLSTM_HEADS, MLSTM_HEAD_DIM

    def body(q_ref, k_ref, v_ref, gc_ref, bc_ref, gr_ref, br_ref, h_ref, cs_ref, ns_ref, ms_ref,
             c_st, n_st, m_st):
        @pl.when(pl.program_id(0) == 0)
        def _():
            c_st[...] = jnp.zeros_like(c_st)
            n_st[...] = jnp.zeros_like(n_st)
            m_st[...] = jnp.zeros_like(m_st)

        f = _mlstm_inputs(q_ref, k_ref, v_ref, *_mlstm_gates(gc_ref, bc_ref, gr_ref, br_ref))
        c0, n0, m0 = c_st[...], n_st[...], m_st[:, :, 0:1]
        c1, n1, m1, _, _, _ = _mlstm_update(f, 0, c0, n0, m0)
        c2, n2, m2, _, _, _ = _mlstm_update(f, 1, c1, n1, m1)
        f = _mlstm_head(f, (c0, c1), (n0, n1), (m0, m1))
        h = f["num"] / jnp.maximum(jnp.abs(f["den"]), f["floor"])
        for hd in range(H):
            h_ref[:, D * hd:D * hd + D] = h[hd]
        cs_ref[0], cs_ref[1] = c0, c1
        ns_ref[0], ns_ref[1] = n0, n1
        ms_ref[0], ms_ref[1] = _lanes(m0), _lanes(m1)
        c_st[...], n_st[...], m_st[...] = c2, n2, _lanes(m2)

    vec = pl.BlockSpec((2, H, 1, 128), lambda s: (s, 0, 0, 0))
    return pl.pallas_call(
        body, name="mlstm_fwd", grid=(steps,), in_specs=_mlstm_specs(T, lambda s: s),
        out_specs=[pl.BlockSpec((STEP_ROWS, 512), lambda s: (s, 0)),
                   pl.BlockSpec((2, H, 128, 128), lambda s: (s, 0, 0, 0)), vec, vec],
        out_shape=[_sds((T, 512), F32), _sds((2 * steps, H, 128, 128), F32),
                   _sds((2 * steps, H, 1, 128), F32), _sds((2 * steps, H, 1, 128), F32)],
        scratch_shapes=[pltpu.VMEM((H, 128, 128), F32), pltpu.VMEM((H, 1, 128), F32),
                        pltpu.VMEM((H, 1, 128), F32)],
        compiler_params=_params(("arbitrary",)))(qk, qk, pm, gcol, bcol, grow, brow)


def _mlstm_bwd(qk, pm, gcol, bcol, grow, brow, cs, ns, ms, dh):
    T = qk.shape[0]
    steps = T // STEP_ROWS
    H, L, D = MLSTM_HEADS, MLSTM_CHUNK, MLSTM_HEAD_DIM
    rev = lambda s: steps - 1 - s

    def body(q_ref, k_ref, v_ref, gc_ref, bc_ref, gr_ref, br_ref, cs_ref, ns_ref, ms_ref, dh_ref,
             dqk_ref, dv_ref, dgc_ref, dgr_ref, dc_st, dn_st):
        @pl.when(pl.program_id(0) == 0)
        def _():
            dc_st[...] = jnp.zeros_like(dc_st)
            dn_st[...] = jnp.zeros_like(dn_st)

        f = _mlstm_inputs(q_ref, k_ref, v_ref, *_mlstm_gates(gc_ref, bc_ref, gr_ref, br_ref))
        c_prev = (cs_ref[0], cs_ref[1])
        n_prev = (ns_ref[0], ns_ref[1])
        m_prev = (ms_ref[0, :, :, 0:1], ms_ref[1, :, :, 0:1])
        f = _mlstm_head(f, c_prev, n_prev, m_prev)
        big = jnp.abs(f["den"]) > f["floor"]
        rden = 1.0 / jnp.where(big, jnp.abs(f["den"]), f["floor"])
        dnum = _heads(dh_ref) * rden
        hdh = jnp.sum(f["num"] * dnum, axis=-1, keepdims=True)
        dden = jnp.where(big, -hdh * rden * jnp.sign(f["den"]), 0.0)
        dnum_b = dnum.astype(BF16)
        dsc = _bdot(dnum_b, f["vb"], 2, 2) + dden
        g = dsc * f["sc"]
        dv = _bdot_rows(f["sc"].astype(BF16), dnum_b)
        dqk_ = (dsc * f["w_intra"]).astype(BF16)
        dq = _bdot(dqk_, f["kb"], 2, 1)
        dks = _bdot_rows(dqk_, f["qb"])
        wdn = f["w_inter"] * dnum
        wdn_b = wdn.astype(BF16)
        wdd = f["w_inter"] * dden
        u = jnp.sum(f["qc"] * wdn, axis=-1, keepdims=True) + wdd * f["qn"]
        dks_s, dv_s, z_s, dg_s = [None, None], [None, None], [None, None], [None, None]
        dcn, dnn = dc_st[...], dn_st[...]
        for ch in (1, 0):
            rows = slice(L * ch, L * ch + L)
            _, _, _, decay, e_a, kw = _mlstm_update(f, ch, c_prev[ch], n_prev[ch], m_prev[ch])
            dcn_b = dcn.astype(BF16)
            dkw = _bdot(f["vb"][:, rows], dcn_b, 2, 2) + dnn
            dks_s[ch] = e_a * dkw
            dv_s[ch] = _bdot(kw.astype(BF16), dcn_b, 2, 1)
            z_s[ch] = e_a * jnp.sum(f["ks"][:, rows] * dkw, axis=-1, keepdims=True)
            dg_s[ch] = jnp.sum(z_s[ch], axis=1, keepdims=True) + decay * (
                jnp.sum(c_prev[ch] * dcn, axis=(1, 2), keepdims=True)
                + jnp.sum(n_prev[ch] * dnn, axis=(1, 2), keepdims=True))
            dcn = decay * dcn + _bdot_rows(f["qb"][:, rows], wdn_b[:, rows])
            dnn = decay * dnn + jnp.sum(wdd[:, rows] * f["q"][:, rows], axis=1, keepdims=True)
        dc_st[...], dn_st[...] = dcn, dnn
        dq = dq + jnp.concatenate(
            [_bdot(wdn_b[:, :L], c_prev[0].astype(BF16), 2, 2) + wdd[:, :L] * n_prev[0],
             _bdot(wdn_b[:, L:], c_prev[1].astype(BF16), 2, 2) + wdd[:, L:] * n_prev[1]], axis=1)
        dks = (dks + jnp.concatenate(dks_s, axis=1)) * (D ** -0.5)
        dv = dv + jnp.concatenate(dv_s, axis=1)
        z = jnp.concatenate(z_s, axis=1)
        row = lax.broadcasted_iota(jnp.int32, (1, STEP_ROWS, 1), 1)
        dg_col = jnp.where(row == L - 1, dg_s[0], 0.0) + jnp.where(row == 2 * L - 1, dg_s[1], 0.0)
        db_col = jnp.sum(g, axis=-1, keepdims=True) + u - z + dg_col
        g_row = jnp.sum(g, axis=1, keepdims=True)
        lane = lax.broadcasted_iota(jnp.int32, (STEP_ROWS, 128), 1)
        sub = lax.broadcasted_iota(jnp.int32, (8, STEP_ROWS), 0)
        dgc = jnp.zeros((STEP_ROWS, 128), F32)
        dgr = jnp.zeros((8, STEP_ROWS), F32)
        for hd in range(H):
            dgc = dgc + jnp.where(lane == hd, z[hd], 0.0) + jnp.where(lane == H + hd, db_col[hd], 0.0)
            dgr = dgr + jnp.where(sub == hd, g_row[hd], 0.0) - jnp.where(sub == H + hd, g_row[hd], 0.0)
            dqk_ref[:, D * hd:D * hd + D] = dq[hd]
            dqk_ref[:, H * D + D * hd:H * D + D * hd + D] = dks[hd]
            dv_ref[:, D * hd:D * hd + D] = dv[hd].astype(BF16)
        dgc_ref[...] = dgc
        dgr_ref[...] = dgr

    return pl.pallas_call(
        body, name="mlstm_bwd", grid=(steps,),
        in_specs=_mlstm_specs(T, rev) + [
            pl.BlockSpec((2, H, 128, 128), lambda s: (rev(s), 0, 0, 0)),
            pl.BlockSpec((2, H, 1, 128), lambda s: (rev(s), 0, 0, 0)),
            pl.BlockSpec((2, H, 1, 128), lambda s: (rev(s), 0, 0, 0)),
            pl.BlockSpec((STEP_ROWS, 512), lambda s: (rev(s), 0))],
        out_specs=[pl.BlockSpec((STEP_ROWS, 1024), lambda s: (rev(s), 0)),
                   pl.BlockSpec((STEP_ROWS, 512), lambda s: (rev(s), 0)),
                   pl.BlockSpec((STEP_ROWS, 128), lambda s: (rev(s), 0)),
                   pl.BlockSpec((8, STEP_ROWS), lambda s: (0, rev(s)))],
        out_shape=[_sds((T, 1024), F32), _sds((T, 512), BF16), _sds((T, 128), F32), _sds((8, T), F32)],
        scratch_shapes=[pltpu.VMEM((H, 128, 128), F32), pltpu.VMEM((H, 1, 128), F32)],
        compiler_params=_params(("arbitrary",)))(qk, qk, pm, gcol, bcol, grow, brow, cs, ns, ms, dh)


def _gate_bwd(dgc, dgr_t, gcol, bcol):
    T = dgc.shape[0]

    def body(a_ref, b_ref, g_ref, bias_ref, o_ref, acc_ref):
        @pl.when(pl.program_id(0) == 0)
        def _():
            acc_ref[...] = jnp.zeros_like(acc_ref)

        d = a_ref[...] + b_ref[...]
        lane = lax.broadcasted_iota(jnp.int32, d.shape, 1)
        is_f = (lane >= MLSTM_HEADS) & (lane < 2 * MLSTM_HEADS)
        dlogf = _chunk_rev_cumsum(jnp.where(is_f, d, 0.0), 0)
        out = jnp.where(is_f, dlogf * _sigmoid(-(g_ref[...] + bias_ref[...])), d)
        o_ref[...] = out.astype(BF16)
        acc_ref[0:1, :] += _colsum(out)

    return _rows("gate_bwd", body, [dgc, dgr_t, gcol, bcol],
                 [_sds((T, 128), BF16), _sds((8, 128), F32)], T)


def _head_norm(h, mu_axis=-1):
    mu = jnp.mean(h, axis=-1, keepdims=True)
    hc = h - mu
    r = lax.rsqrt(jnp.mean(hc * hc, axis=-1, keepdims=True) + NORM_EPS)
    return hc * r, r


def _mlstm_out(hm, pm, w):
    T = hm.shape[0]
    D = MLSTM_HEAD_DIM

    def body(h_ref, o_ref, w_ref, y_ref):
        for hd in range(MLSTM_HEADS):
            cols = slice(D * hd, D * hd + D)
            hn, _ = _head_norm(h_ref[:, cols])
            y_ref[:, cols] = (_sigmoid(o_ref[:, cols]) * hn * w_ref[:, cols]).astype(BF16)

    tr = min(ROW_TILE, T)
    return pl.pallas_call(
        body, name="mlstm_out", grid=(T // tr,),
        in_specs=[pl.BlockSpec((tr, 512), lambda i: (i, 0)), pl.BlockSpec((tr, 512), lambda i: (i, 3)),
                  pl.BlockSpec((1, 512), lambda i: (0, 0))],
        out_specs=pl.BlockSpec((tr, 512), lambda i: (i, 0)), out_shape=_sds((T, 512), BF16),
        compiler_params=_params(("parallel",)))(hm, pm, w)


def _mlstm_out_bwd(hm, pm, w, dy):
    T = hm.shape[0]
    D = MLSTM_HEAD_DIM
    tr = min(ROW_TILE, T)

    def body(h_ref, o_ref, w_ref, dy_ref, dh_ref, do_ref, acc_ref):
        @pl.when(pl.program_id(0) == 0)
        def _():
            acc_ref[...] = jnp.zeros_like(acc_ref)

        for hd in range(MLSTM_HEADS):
            cols = slice(D * hd, D * hd + D)
            hn, r = _head_norm(h_ref[:, cols])
            sg = _sigmoid(o_ref[:, cols])
            dy, w = dy_ref[:, cols], w_ref[:, cols]
            do_ref[:, cols] = (dy * hn * w * sg * (1.0 - sg)).astype(BF16)
            dyn = dy * sg
            acc_ref[0:1, cols] += _colsum(dyn * hn)
            dhn = dyn * w
            dh_ref[:, cols] = r * (dhn - jnp.mean(dhn, axis=-1, keepdims=True)
                                   - hn * jnp.mean(dhn * hn, axis=-1, keepdims=True))

    return pl.pallas_call(
        body, name="mlstm_out_bwd", grid=(T // tr,),
        in_specs=[pl.BlockSpec((tr, 512), lambda i: (i, 0)), pl.BlockSpec((tr, 512), lambda i: (i, 3)),
                  pl.BlockSpec((1, 512), lambda i: (0, 0)), pl.BlockSpec((tr, 512), lambda i: (i, 0))],
        out_specs=[pl.BlockSpec((tr, 512), lambda i: (i, 0)), pl.BlockSpec((tr, 512), lambda i: (i, 0)),
                   pl.BlockSpec((8, 512), lambda i: (0, 0))],
        out_shape=[_sds((T, 512), F32), _sds((T, 512), BF16), _sds((8, 512), F32)],
        compiler_params=_params(("arbitrary",)))(hm, pm, w, dy)


def _adamw(name, w, g, m, v, tr=64):
    R, C = w.shape
    tr = min(tr, R)
    c1 = 1.0 - ADAM_B1 ** ADAM_STEP
    c2 = 1.0 - ADAM_B2 ** ADAM_STEP

    def body(w_ref, g_ref, m_ref, v_ref, d_ref, mo_ref, vo_ref):
        g = g_ref[...]
        m = ADAM_B1 * m_ref[...] + (1.0 - ADAM_B1) * g
        v = ADAM_B2 * v_ref[...] + (1.0 - ADAM_B2) * (g * g)
        mo_ref[...] = m
        vo_ref[...] = v
        d_ref[...] = -ADAM_LR * ((m / c1) / (jnp.sqrt(v / c2) + ADAM_EPS) + ADAM_WD * w_ref[...])

    spec = pl.BlockSpec((tr, C), lambda i: (i, 0))
    return pl.pallas_call(
        body, name=name, grid=(R // tr,), in_specs=[spec] * 4, out_specs=[spec] * 3,
        out_shape=[_sds((R, C), F32)] * 3, compiler_params=_params(("parallel",)))(w, g, m, v)


def _place():
    return lax.axis_index("x"), lax.axis_index("y"), lax.axis_index("c")


def _all_gather8(name, blk, space):
    m, n = blk.shape

    def body(x_ref, out_ref, send_sems, recv_sems, local_sem):
        x, y, c = _place()
        me, sibling = (x, y, c), (x, y, 1 - c)
        chips = [(1 - x, y), (x, 1 - y), (1 - x, 1 - y)]

        def rows(px, py, pc):
            return out_ref.at[pl.ds((4 * px + 2 * py + pc) * m, m), :]

        def copy(k, block, to, src=None):
            return pltpu.make_async_remote_copy(
                src_ref=rows(*block) if src is None else src, dst_ref=rows(*block),
                send_sem=send_sems.at[k], recv_sem=recv_sems.at[k],
                device_id=to, device_id_type=MESH)

        mine = pltpu.make_async_copy(x_ref, rows(*me), local_sem)
        mine.start()
        first = [copy(0, me, sibling, src=x_ref)]
        first += [copy(1 + j, me, (*chip, c), src=x_ref) for j, chip in enumerate(chips)]
        for cp in first:
            cp.start()
        passed = [copy(4 + j, (*chip, c), sibling) for j, chip in enumerate(chips)]
        for j, chip in enumerate(chips):
            copy(1 + j, (*chip, c), me).wait_recv()
            passed[j].start()
        copy(0, sibling, me).wait_recv()
        for j, chip in enumerate(chips):
            copy(4 + j, (*chip, 1 - c), me).wait_recv()
        for cp in first + passed:
            cp.wait_send()
        mine.wait()

    return pl.pallas_call(
        body, name=name, out_shape=_sds((8 * m, n), blk.dtype),
        in_specs=[pl.BlockSpec(memory_space=space)], out_specs=pl.BlockSpec(memory_space=space),
        scratch_shapes=[pltpu.SemaphoreType.DMA((7,)), pltpu.SemaphoreType.DMA((7,)),
                        pltpu.SemaphoreType.DMA],
        compiler_params=pltpu.CompilerParams(vmem_limit_bytes=VMEM_LIMIT))(blk)


def _hbm_specs(n):
    return [pl.BlockSpec(memory_space=pl.ANY)] * n


def _swap_sibling(name, srcs, halves=False):
    nw = len(srcs)

    def body(*refs):
        src_refs, dst_refs, send_sems, recv_sems = refs[:nw], refs[nw:2 * nw], refs[2 * nw], refs[2 * nw + 1]
        x, y, c = _place()
        cps = []
        for w in range(nw):
            s = src_refs[w]
            if halves:
                s = s.at[pl.ds(0, srcs[w].shape[0]), pl.ds(0, srcs[w].shape[1]), 1 - c]
            cps.append(pltpu.make_async_remote_copy(
                src_ref=s, dst_ref=dst_refs[w], send_sem=send_sems.at[w], recv_sem=recv_sems.at[w],
                device_id=(x, y, 1 - c), device_id_type=MESH))
        for cp in cps:
            cp.start()
        for cp in cps:
            cp.wait()

    shapes = [(s.shape[:2] + s.shape[3:]) if halves else s.shape for s in srcs]
    return pl.pallas_call(
        body, name=name, out_shape=[_sds(sh, s.dtype) for sh, s in zip(shapes, srcs)],
        in_specs=_hbm_specs(nw), out_specs=_hbm_specs(nw),
        scratch_shapes=[pltpu.SemaphoreType.DMA((nw,)), pltpu.SemaphoreType.DMA((nw,))])(*srcs)


def _split_start(name, srcs, lands, copies, per_array):
    nw = len(srcs)

    def body(*refs):
        send_sems, recv_sems, token = refs[2 * nw], refs[2 * nw + 1], refs[-1]
        for w in range(nw):
            for k, (s, d, dev) in enumerate(copies(refs[w], refs[nw + w], *_place())):
                pltpu.make_async_remote_copy(
                    src_ref=s, dst_ref=d, send_sem=send_sems.at[w * per_array + k],
                    recv_sem=recv_sems.at[w * per_array + k], device_id=dev, device_id_type=MESH).start()
        token[...] = jnp.zeros_like(token)

    hbm, sem = pl.BlockSpec(memory_space=pltpu.HBM), pl.BlockSpec(memory_space=pltpu.SEMAPHORE)
    arrays = list(srcs) + list(lands)
    out = pl.pallas_call(
        body, name=name,
        out_shape=(pltpu.SemaphoreType.DMA((nw * per_array,)), pltpu.SemaphoreType.DMA((nw * per_array,)),
                   *[pltpu.HBM(a.shape, a.dtype) for a in arrays], _sds((8, 128), F32)),
        in_specs=[hbm] * (2 * nw),
        out_specs=(sem, sem, *[hbm] * (2 * nw), pl.BlockSpec(memory_space=pltpu.VMEM)),
        input_output_aliases={i: 2 + i for i in range(2 * nw)},
        compiler_params=pltpu.CompilerParams(has_side_effects=pltpu.SideEffectType.DATAFLOW_SIDE_EFFECTING))(
            *[pltpu.with_memory_space_constraint(a, pltpu.HBM) for a in arrays])
    return out[0], out[1], out[2:2 + nw], out[2 + nw:2 + 2 * nw], out[-1]


def _split_wait(name, started, after, waits, per_array):
    send_sems, recv_sems, srcs, lands, _ = started
    nw = len(srcs)

    def body(*refs):
        send_sems, recv_sems = refs[2 * nw], refs[2 * nw + 1]
        x, y, c = _place()
        for w in range(nw):
            for k, (s, d) in enumerate(waits(refs[w], refs[nw + w], x, y, c)):
                cp = pltpu.make_async_remote_copy(
                    src_ref=s, dst_ref=d, send_sem=send_sems.at[w * per_array + k],
                    recv_sem=recv_sems.at[w * per_array + k], device_id=(x, y, 1 - c),
                    device_id_type=MESH)
                cp.wait_send()
                cp.wait_recv()

    hbm, sem = pl.BlockSpec(memory_space=pltpu.HBM), pl.BlockSpec(memory_space=pltpu.SEMAPHORE)
    arrays = list(srcs) + list(lands)
    out = pl.pallas_call(
        body, name=name, out_shape=tuple(pltpu.HBM(a.shape, a.dtype) for a in arrays),
        in_specs=[hbm] * (2 * nw) + [sem, sem, pl.BlockSpec(memory_space=pl.ANY)],
        out_specs=tuple([hbm] * (2 * nw)), input_output_aliases={i: i for i in range(2 * nw)},
        compiler_params=pltpu.CompilerParams(has_side_effects=pltpu.SideEffectType.DATAFLOW_SIDE_EFFECTING))(
            *arrays, send_sems, recv_sems, after)
    return list(out[nw:])


def _other_chips(x, y):
    return [(1 - x, y), (x, 1 - y), (1 - x, 1 - y)]


def _gather_sends(src_ref, land_ref, x, y, c):
    to = land_ref.at[2 * x + y, pl.ds(0, land_ref.shape[1]), c]
    return [(src_ref, to, (x, y, 1 - c))] + [(src_ref, to, (px, py, c)) for px, py in _other_chips(x, y)]


def _gather_lands(src_ref, land_ref, x, y, c):
    g = pl.ds(0, land_ref.shape[1])
    return [(src_ref, land_ref.at[2 * x + y, g, 1 - c])] + [
        (src_ref, land_ref.at[2 * px + py, g, c]) for px, py in _other_chips(x, y)]


def _scatter_sends(src_ref, land_ref, x, y, c):
    return [(src_ref.at[2 * px + py], land_ref.at[2 * x + y], (px, py, c)) for px, py in _other_chips(x, y)]


def _scatter_lands(src_ref, land_ref, x, y, c):
    return [(src_ref.at[2 * x + y], land_ref.at[2 * px + py]) for px, py in _other_chips(x, y)]


def _forward_sibling(name, lands):
    nw = len(lands)

    def body(*refs):
        land_refs, out_refs, send_sems, recv_sems = refs[:nw], refs[nw:2 * nw], refs[2 * nw], refs[2 * nw + 1]
        x, y, c = _place()
        cps = []
        for w in range(nw):
            g = pl.ds(0, lands[w].shape[1])
            cps += [pltpu.make_async_remote_copy(
                src_ref=land_refs[w].at[2 * px + py, g, c], dst_ref=out_refs[w].at[2 * px + py, g, c],
                send_sem=send_sems.at[w, j], recv_sem=recv_sems.at[w, j], device_id=(x, y, 1 - c),
                device_id_type=MESH) for j, (px, py) in enumerate(_other_chips(x, y))]
        for cp in cps:
            cp.start()
        for w in range(nw):
            g = pl.ds(0, lands[w].shape[1])
            for j, (px, py) in enumerate(_other_chips(x, y)):
                slot = out_refs[w].at[2 * px + py, g, 1 - c]
                pltpu.make_async_remote_copy(src_ref=slot, dst_ref=slot, send_sem=send_sems.at[w, j],
                                             recv_sem=recv_sems.at[w, j], device_id=(x, y, 1 - c),
                                             device_id_type=MESH).wait_recv()
        for cp in cps:
            cp.wait_send()

    return pl.pallas_call(
        body, name=name, out_shape=[_sds(a.shape, a.dtype) for a in lands],
        in_specs=_hbm_specs(nw), out_specs=_hbm_specs(nw), input_output_aliases={i: i for i in range(nw)},
        scratch_shapes=[pltpu.SemaphoreType.DMA((nw, 3)), pltpu.SemaphoreType.DMA((nw, 3))])(*lands)


def _pair_sum(name, full, got, core):
    _, g, _, m, n = full.shape

    def body(c_ref, a_ref, b_ref, o_ref):
        o_ref[...] = (a_ref[...].astype(F32) + b_ref[...].astype(F32)).astype(o_ref.dtype)

    slab = pl.BlockSpec((None, None, m, n), lambda s, w, c: (s, w, 0, 0))
    return pl.pallas_call(
        body, name=name,
        grid_spec=pltpu.PrefetchScalarGridSpec(
            num_scalar_prefetch=1, grid=(4, g),
            in_specs=[pl.BlockSpec((None, None, None, m, n), lambda s, w, c: (s, w, c[0], 0, 0)), slab],
            out_specs=slab),
        out_shape=_sds(got.shape, BF16),
        compiler_params=_params(("parallel", "parallel")))(core, full, got)


def _sum4(name, a):
    _, g, m, n = a.shape

    def body(a_ref, o_ref):
        acc = a_ref[0].astype(F32)
        for s in range(1, 4):
            acc = acc + a_ref[s].astype(F32)
        o_ref[...] = acc

    return pl.pallas_call(body, name=name, grid=(g,),
                          in_specs=[pl.BlockSpec((4, None, m, n), lambda w: (0, w, 0, 0))],
                          out_specs=pl.BlockSpec((None, m, n), lambda w: (w, 0, 0)),
                          out_shape=_sds((g, m, n), F32), compiler_params=_params(("parallel",)))(a)


def _small_update(gathered, w, m, v):
    n = w.shape[1]
    tn = 2048
    c1 = 1.0 - ADAM_B1 ** ADAM_STEP
    c2 = 1.0 - ADAM_B2 ** ADAM_STEP

    def body(g_ref, w_ref, m_ref, v_ref, go_ref, d_ref, mo_ref, vo_ref):
        g = g_ref[0:1, :]
        for d in range(1, 8):
            g = g + g_ref[d:d + 1, :]
        go_ref[...] = g
        m = ADAM_B1 * m_ref[...] + (1.0 - ADAM_B1) * g
        v = ADAM_B2 * v_ref[...] + (1.0 - ADAM_B2) * (g * g)
        mo_ref[...] = m
        vo_ref[...] = v
        d_ref[...] = -ADAM_LR * ((m / c1) / (jnp.sqrt(v / c2) + ADAM_EPS) + ADAM_WD * w_ref[...])

    row = pl.BlockSpec((1, tn), lambda i: (0, i))
    return pl.pallas_call(
        body, name="small_update", grid=(n // tn,),
        in_specs=[pl.BlockSpec((8, tn), lambda i: (0, i)), row, row, row], out_specs=[row] * 4,
        out_shape=[_sds((1, n), F32)] * 4, compiler_params=_params(("parallel",)))(gathered, w, m, v)


def _swiglu(ps, es):
    g, u = ps
    return g * _sigmoid(g) * u, g, u


def _swiglu_bwd(ps, es):
    g, u = es[0].astype(F32), es[1].astype(F32)
    sg = _sigmoid(g)
    return ps[0] * u * (sg * (1.0 + g * (1.0 - sg))), ps[0] * (g * sg)


def _merge(ps, es):
    ga, gm = [e.astype(F32) for e in es]
    return _sigmoid(ga) * ps[0] + _sigmoid(gm) * ps[1], ps[0], ps[1]


def _merge_bwd(ps, es):
    a, b, ga, gm = [e.astype(F32) for e in es]
    sa, sm = _sigmoid(ga), _sigmoid(gm)
    dm = ps[0]
    return dm * sa, dm * sm, dm * a * (sa * (1.0 - sa)), dm * b * (sm * (1.0 - sm))


W_IN_PIECES = (("q", 512), ("kv", 256), ("mqk", 1024), ("mv", 512), ("mo", 512), ("if", 8),
               ("ga", 1024), ("gm", 1024))


def _local_step(x, tgt, pos_col, mod, sp, in_weights, late_weights, ffn_grads, mixer_grads):
    sh_m, sc_m, gate_m, sh_f, sc_f, gate_f = mod
    h = _pre_norm(x, sp["g_pre_mix"], sc_m, sh_m)
    inv = ROPE_THETA ** (-2.0 * jnp.arange(HEAD_DIM // 2, dtype=F32) / HEAD_DIM)
    cos, sin = _rope_tables(pos_col, jnp.tile(inv, 4).reshape(1, 128))
    W = dict(in_weights(h))
    w_a = jnp.concatenate([W["q"], W["kv"]], axis=0)
    w_m = jnp.concatenate([W["mqk"], W["mv"], W["mo"]], axis=0)
    w_g = jnp.concatenate([W["ga"], W["gm"]], axis=0)
    pa, = _mm("proj_attn", [[(h, w_a)]], [], _first, [F32], cn=256, nt=True)
    pm, = _mm("proj_mlstm", [[(h, w_m)]], [], _first, [F32], cn=512, nt=True)
    pif, = _mm("proj_gates", [[(h, W["if"])]], [], _first, [F32], cn=128, nt=True)
    pg, = _mm("proj_branch_gates", [[(h, w_g)]], [], _first, [BF16], cn=512, nt=True)
    ya = _attn_fwd(pa, cos, sin, sp["sinks"])
    qk = _conv_fwd(pm, sp["conv_w"], sp["conv_b"])
    bcol = jnp.pad(sp["b_if"], ((0, 0), (0, 120)))
    brow = jnp.broadcast_to(sp["b_if"].reshape(8, 1), (8, 128))
    grow = pif[:, :8].T
    hm, cs, ns, ms = _mlstm_fwd(qk, pm, pif, bcol, grow, brow)
    ym = _mlstm_out(hm, pm, sp["norm_w"])
    W.update(late_weights(ym))
    w_fg, w_fu, w_fd = W["fg"], W["fu"], W["fd"]
    merged, br_a, br_m = _mm("branches", [[(ya, W["ba"])], [(ym, W["bm"])]],
                             [(pg, 0), (pg, 1)], _merge, [BF16, BF16, BF16], cn=512, nt=True)
    mix, = _mm("mix_out", [[(merged, W["out"])]], [], _first, [F32], cn=512)
    x1, h2 = _res_norm(x, mix, gate_m, sp["g_post_mix"], sp["g_pre_ffn"], sc_f, sh_f)
    act, gt, up = _mm("ffn_in", [[(h2, w_fg)], [(h2, w_fu)]], [], _swiglu, [BF16] * 3,
                      cn=256, nt=True)
    ff, = _mm("ffn_down", [[(act, w_fd)]], [], _first, [F32], cn=512)
    dy, dff, acc_l, loss = _final_loss(x1, ff, tgt, gate_f, sp["g_post_ffn"])

    G = {}
    dgt, dup = _mm("ffn_down_bwd", [[(dff, w_fd)]], [gt, up], _swiglu_bwd, [BF16, BF16],
                   cn=256, nt=True)
    g_fd = _mm_tn("dw_ffn_down", act, dff, BF16, 1408, 512)
    dh2, = _mm("ffn_in_bwd", [[(dgt, w_fg), (dup, w_fu)]], [], _first, [F32], cn=512)
    g_fg = _mm_tn("dw_ffn_gate", dgt, h2, BF16, 1408, 1024)
    g_fu = _mm_tn("dw_ffn_up", dup, h2, BF16, 1408, 1024)
    tie = ffn_grads(g_fg, g_fu, g_fd)
    dx1, dmix, acc_r = _res_norm_bwd(x1, mix, dh2, dy, sc_f + tie, gate_m, sp["g_pre_ffn"],
                                     sp["g_post_mix"])
    d_a, d_m, dga, dgm = _mm("mix_out_bwd", [[(dmix, W["out"])]],
                             [br_a, br_m, (pg, 0), (pg, 1)], _merge_bwd,
                             [BF16] * 4, cn=512, nt=True)
    G["out"] = _mm_tn("dw_out", merged, dmix, BF16, 1024, 512)
    dya, = _mm("branch_attn_bwd", [[(d_a, W["ba"])]], [], _first, [F32], cn=512)
    dym, = _mm("branch_mlstm_bwd", [[(d_m, W["bm"])]], [], _first, [F32], cn=512)
    G["ba"] = _mm_tn("dw_branch_attn", d_a, ya, BF16, 1024, 512)
    G["bm"] = _mm_tn("dw_branch_mlstm", d_m, ym, BF16, 1024, 512)
    dhm, do_m, acc_n = _mlstm_out_bwd(hm, pm, sp["norm_w"], dym)
    dqk, dv_m, dgc, dgr = _mlstm_bwd(qk, pm, pif, bcol, grow, brow, cs, ns, ms, dhm)
    dif, acc_g = _gate_bwd(dgc, jnp.pad(dgr.T, ((0, 0), (0, 120))), pif, bcol)
    dpre, acc_c = _conv_bwd_pre(pm, sp["conv_w"], sp["conv_b"], dqk)
    du = _conv_bwd_in(dpre, sp["conv_w"])
    dq_a, dcur, dprv, dsink = _attn_bwd(pa, cos, sin, sp["sinks"], dya)
    dkv = _attn_kv_combine(dcur, dprv, cos, sin)
    dproj = {"q": dq_a, "kv": dkv, "mqk": du, "mv": dv_m, "mo": do_m, "if": dif, "ga": dga, "gm": dgm}
    for k, _ in W_IN_PIECES:
        G[k] = _mm_tn("dw_in_" + k, dproj[k], h, BF16, dproj[k].shape[1], 1024)
    w_tied = dict(W, **{"if": W["if"] + mixer_grads(G).astype(BF16)})
    dh, = _mm("proj_bwd", [[(dproj[k], w_tied[k]) for k, _ in W_IN_PIECES]], [], _first, [F32], cn=512)
    dx, acc_p = _pre_norm_bwd(x, dh, dx1, sp["g_pre_mix"], sc_m)

    small = {
        "mod": jnp.concatenate([acc_p[1], acc_p[0], acc_r[3], acc_r[1], acc_r[0], acc_l[0]]),
        "g_pre_mix": acc_p[2], "g_post_mix": acc_r[4], "b_if": acc_g[0, :8],
        "conv_w": acc_c[:CONV_WIDTH].reshape(-1), "conv_b": acc_c[CONV_WIDTH],
        "sinks": dsink[:, 0], "norm_w": acc_n[0], "g_pre_ffn": acc_r[2], "g_post_ffn": acc_l[1]}
    return loss, dx, small


IN_WIDTH = sum(n for _, n in W_IN_PIECES)
IN_SHARD = IN_WIDTH // 4
IN_SHARD_PAD = -(-IN_SHARD // 32) * 32


def _split_w_in(w_in_t):
    out, off = {}, 0
    for k, n in W_IN_PIECES:
        out[k] = w_in_t[off:off + n]
        off += n
    out["if"] = jnp.pad(out["if"], ((0, 120), (0, 0)))
    return out


def _halves(a):
    return a.reshape(4, 2, a.shape[0] // 8, a.shape[1])


SMALL = (("b_ada", 6144), ("g_pre_mix", 1024), ("g_post_mix", 1024), ("b_if", 128), ("conv_w", 4096),
         ("conv_b", 1024), ("sinks", 128), ("norm_w", 512), ("g_pre_ffn", 1024), ("g_post_ffn", 1024))
SMALL_LEN = 8 * 2048


def _pack_small(vals):
    parts = []
    for k, n in SMALL:
        v = vals[k].reshape(-1)
        parts.append(jnp.pad(v, (0, n - v.shape[0])))
    flat = jnp.concatenate(parts)
    return jnp.pad(flat, (0, SMALL_LEN - flat.shape[0]))


def _unpack_small(flat, shapes):
    out, off = {}, 0
    for k, n in SMALL:
        size = 1
        for d in shapes[k]:
            size *= d
        out[k] = flat[off:off + size].reshape(shapes[k])
        off += n
    return out


def kernel(x, c, positions, w_ada, b_ada, g_pre_mix, g_post_mix, w_in, b_if, conv_w, conv_b, attn_sinks, mlstm_norm_w, w_branch_attn, w_branch_mlstm, w_out, g_pre_ffn, g_post_ffn, w_ffn_gate, w_ffn_up, w_ffn_down, loss_target, m_w_ada, m_b_ada, m_g_pre_mix, m_g_post_mix, m_w_in, m_b_if, m_conv_w, m_conv_b, m_attn_sinks, m_mlstm_norm_w, m_w_branch_attn, m_w_branch_mlstm, m_w_out, m_g_pre_ffn, m_g_post_ffn, m_w_ffn_gate, m_w_ffn_up, m_w_ffn_down, v_w_ada, v_b_ada, v_g_pre_mix, v_g_post_mix, v_w_in, v_b_if, v_conv_w, v_conv_b, v_attn_sinks, v_mlstm_norm_w, v_w_branch_attn, v_w_branch_mlstm, v_w_out, v_g_pre_ffn, v_g_post_ffn, v_w_ffn_gate, v_w_ffn_up, v_w_ffn_down):
    xi, yi, ci = _place()
    chip = 2 * xi + yi
    dev = 2 * chip + ci
    T = x.shape[1]
    ada_cols = w_ada.shape[2]

    def my_half(a):
        n = a.shape[0] // 2
        return lax.dynamic_slice_in_dim(a, ci * n, n, axis=0).astype(BF16)

    blk = jnp.concatenate([c.reshape(-1), conv_w.reshape(-1)]).reshape(8, 256)
    got = _all_gather8("gather_cond", blk, pltpu.VMEM).reshape(8, 2048)
    c_all = got[:, :D_MODEL].astype(BF16)
    conv_full = got[::2, D_MODEL:].reshape(4, CONV_WIDTH, -1).transpose(1, 0, 2).reshape(CONV_WIDTH, -1)

    b_sh = lax.dynamic_slice_in_dim(b_ada, chip * ada_cols, ada_cols, axis=1)
    mod_part, = _mm("ada_mod", [[(c_all, w_ada[0].astype(BF16))]], [b_sh],
                    lambda ps, es: (ps[0] + es[0],), [F32], cn=512, tm=8)
    mod_all = _all_gather8("gather_mod", mod_part, pltpu.VMEM).reshape(4, 2, 8, ada_cols)[:, 0]
    mod = lax.dynamic_index_in_dim(mod_all, dev, axis=1, keepdims=False).reshape(6, 1, D_MODEL)

    def gather_start(name, blks, after):
        blks, _ = lax.optimization_barrier((blks, after))
        lands = [lax.dynamic_update_slice(jnp.zeros((4, b.shape[0], 2) + b.shape[1:], BF16),
                                          b[None, :, None], (chip, 0, ci, 0, 0)) for b in blks]
        return _split_start(name + "_start", blks, lands, _gather_sends, 4)

    def gather_wait(name, started, after):
        return _forward_sibling(name + "_forward", _split_wait(name + "_wait", started, after, _gather_lands, 4))

    w_in_t = jnp.pad(w_in[0].T, ((0, IN_SHARD_PAD - IN_SHARD), (0, 0)))
    in_started = gather_start("in_gather", [my_half(w_in_t)[None]], mod)
    late_started = gather_start(
        "late_gather",
        [jnp.stack([my_half(w_ffn_gate[0].T), my_half(w_ffn_up[0].T), my_half(w_ffn_down[0])]),
         my_half(w_out[0])[None], jnp.stack([my_half(w_branch_attn[0].T), my_half(w_branch_mlstm[0].T)])],
        in_started[4])
    mod = mod + (in_started[4][0, 0] + late_started[4][0, 0])

    def in_weights(after):
        g_in, = gather_wait("in_gather", in_started, after)
        return _split_w_in(g_in.reshape(4, IN_SHARD_PAD, D_MODEL)[:, :IN_SHARD].reshape(IN_WIDTH, D_MODEL))

    def late_weights(after):
        g_ffn, g_out, g_br = gather_wait("late_gather", late_started, after)
        return {"fg": g_ffn[:, 0].reshape(D_FF, D_MODEL), "fu": g_ffn[:, 1].reshape(D_FF, D_MODEL),
                "fd": g_ffn[:, 2].reshape(D_FF, D_MODEL), "out": g_out.reshape(D_MODEL, D_MODEL),
                "ba": g_br[:, 0].reshape(D_MODEL, -1), "bm": g_br[:, 1].reshape(D_MODEL, -1)}

    core = ci.reshape(1).astype(jnp.int32)
    sent = {}

    def scatter_start(name, groups):
        theirs = _swap_sibling(name + "_pair", groups, halves=True)
        pairs = [_pair_sum("%s_pair_sum_%d" % (name, i), a, b, core)
                 for i, (a, b) in enumerate(zip(groups, theirs))]
        sent[name] = _split_start(name + "_start", pairs, [p + jnp.zeros((), BF16) for p in pairs],
                                  _scatter_sends, 3)
        return sent[name][4][0, 0]

    def ffn_grads(g_fg, g_fu, g_fd):
        return scatter_start("rs_ffn", [jnp.stack([_halves(g_fg), _halves(g_fu), _halves(g_fd)], axis=1)])

    def mixer_grads(G):
        g_in_t = jnp.concatenate([G[k][:n] for k, n in W_IN_PIECES]).reshape(4, IN_SHARD, D_MODEL)
        g_in_t = jnp.pad(g_in_t, ((0, 0), (0, IN_SHARD_PAD - IN_SHARD), (0, 0)))
        return scatter_start("rs_mix", [g_in_t.reshape(4, 1, 2, IN_SHARD_PAD // 2, D_MODEL),
                                        _halves(G["out"])[:, None],
                                        jnp.stack([_halves(G["ba"]), _halves(G["bm"])], axis=1)])

    sp = {"g_pre_mix": g_pre_mix, "g_post_mix": g_post_mix, "b_if": b_if, "conv_w": conv_full,
          "conv_b": conv_b, "sinks": attn_sinks, "norm_w": mlstm_norm_w, "g_pre_ffn": g_pre_ffn,
          "g_post_ffn": g_post_ffn}
    loss, dx, small = _local_step(x[0], loss_target[0], positions.reshape(T, 1), [mod[i] for i in range(6)],
                                  sp, in_weights, late_weights, ffn_grads, mixer_grads)

    landed = (_split_wait("rs_ffn_wait", sent["rs_ffn"], dx, _scatter_lands, 3)
              + _split_wait("rs_mix_wait", sent["rs_mix"], dx, _scatter_lands, 3))
    reds = [_sum4("rs_chip_sum_%d" % i, a) for i, a in enumerate(landed)]
    others = _swap_sibling("rs_share", reds)
    s_ffn, s_in, s_out, s_br = [
        jnp.concatenate([jnp.where(ci == 0, r, o), jnp.where(ci == 0, o, r)], axis=1)
        for r, o in zip(reds, others)]
    gsh = {"fg": s_ffn[0].T, "fu": s_ffn[1].T, "fd": s_ffn[2], "w_in": s_in[0, :IN_SHARD].T,
           "out": s_out[0], "ba": s_br[0].T, "bm": s_br[1].T}

    small["b_ada"] = small.pop("mod")
    vec = _pack_small(small).reshape(8, 2048)
    g_all = _all_gather8("gather_small", vec, pltpu.VMEM).reshape(8, SMALL_LEN)
    dmod_sh = lax.dynamic_slice_in_dim(g_all[:, :6 * D_MODEL], chip * ada_cols, ada_cols, axis=1)
    g_w_ada = _mm_tn("dw_ada", c_all, dmod_sh.astype(BF16), F32, D_MODEL, 512, 8)

    smalls = {"b_ada": (b_ada, m_b_ada, v_b_ada), "g_pre_mix": (g_pre_mix, m_g_pre_mix, v_g_pre_mix),
              "g_post_mix": (g_post_mix, m_g_post_mix, v_g_post_mix), "b_if": (b_if, m_b_if, v_b_if),
              "conv_w": None, "conv_b": (conv_b, m_conv_b, v_conv_b),
              "sinks": (attn_sinks, m_attn_sinks, v_attn_sinks),
              "norm_w": (mlstm_norm_w, m_mlstm_norm_w, v_mlstm_norm_w),
              "g_pre_ffn": (g_pre_ffn, m_g_pre_ffn, v_g_pre_ffn),
              "g_post_ffn": (g_post_ffn, m_g_post_ffn, v_g_post_ffn)}
    shapes = {k: (t[0].shape if t is not None else (1, CONV_WIDTH, D_MODEL)) for k, t in smalls.items()}
    zeros = jnp.zeros((CONV_WIDTH * D_MODEL,), F32)
    packs = [_pack_small({k: (t[i] if t is not None else zeros) for k, t in smalls.items()}).reshape(1, -1)
             for i in range(3)]
    s_out = [_unpack_small(o[0], shapes) for o in _small_update(g_all, *packs)]
    g_conv = lax.dynamic_slice_in_dim(s_out[0]["conv_w"], chip * conv_w.shape[2], conv_w.shape[2], axis=2)

    res = {}
    for k, t in smalls.items():
        if t is not None:
            res[k] = tuple(o[k] for o in s_out)
    res["conv_w"] = (g_conv, *[o[None] for o in _adamw("adam_conv_w", conv_w[0], g_conv[0], m_conv_w[0], v_conv_w[0])])
    res["w_ada"] = (g_w_ada[None], *[o[None] for o in _adamw("adam_w_ada", w_ada[0], g_w_ada, m_w_ada[0], v_w_ada[0])])
    bigs = {"w_in": (w_in, m_w_in, v_w_in), "ba": (w_branch_attn, m_w_branch_attn, v_w_branch_attn),
            "bm": (w_branch_mlstm, m_w_branch_mlstm, v_w_branch_mlstm), "out": (w_out, m_w_out, v_w_out),
            "fg": (w_ffn_gate, m_w_ffn_gate, v_w_ffn_gate), "fu": (w_ffn_up, m_w_ffn_up, v_w_ffn_up),
            "fd": (w_ffn_down, m_w_ffn_down, v_w_ffn_down)}
    for k, (w, m, v) in bigs.items():
        res[k] = (gsh[k][None], *[o[None] for o in _adamw("adam_" + k, w[0], gsh[k], m[0], v[0])])

    order = ("w_ada", "b_ada", "g_pre_mix", "g_post_mix", "w_in", "b_if", "conv_w", "conv_b", "sinks",
             "norm_w", "ba", "bm", "out", "g_pre_ffn", "g_post_ffn", "fg", "fu", "fd")
    total = lax.psum(loss[0, 0], ("x", "y", "c"))
    return (total, dx[None], *[res[k][0] for k in order], *[res[k][1] for k in order],
            *[res[k][2] for k in order], *[res[k][3] for k in order])
```

```python
import functools

import jax
import jax.numpy as jnp
from jax import lax
from jax.experimental import pallas as pl
from jax.experimental.pallas import tpu as pltpu

F32, BF16 = jnp.float32, jnp.bfloat16
MESH = pl.DeviceIdType.MESH

D_MODEL = 1024
N_Q_HEADS, N_KV_HEADS, HEAD_DIM, WINDOW = 8, 2, 64, 128
ROPE_THETA = 10000.0
MLSTM_HEADS, MLSTM_HEAD_DIM, MLSTM_CHUNK, CONV_WIDTH = 4, 128, 64, 4
D_FF = 2816
NORM_EPS = 1e-6
ADAM_LR, ADAM_B1, ADAM_B2, ADAM_EPS, ADAM_WD, ADAM_STEP = 0.001, 0.9, 0.999, 1e-08, 0.01, 10

VMEM_LIMIT = 56 * 1024 * 1024
ROW_TILE = 256
MM_TM = 512
MM_TT = 1024
ATTN_BLK = WINDOW
STEP_ROWS = 2 * MLSTM_CHUNK
NEG_INF = float("-inf")


def _params(sem):
    return pltpu.CompilerParams(dimension_semantics=sem, vmem_limit_bytes=VMEM_LIMIT)


def _sds(shape, dtype):
    return jax.ShapeDtypeStruct(shape, dtype)


def _sigmoid(x):
    return 1.0 / (1.0 + jnp.exp(-x))


def _dot(a, b, ca, cb):
    return lax.dot_general(a, b, (((ca,), (cb,)), ((), ())), preferred_element_type=F32)


def _bdot(a, b, ca, cb):
    return lax.dot_general(a, b, (((ca,), (cb,)), ((0,), (0,))), preferred_element_type=F32)


def _bdot_rows(a, b):
    return jnp.stack([_dot(a[h], b[h], 0, 0) for h in range(a.shape[0])])


def _mm(name, prods, extras, epi, out_dtypes, cn, nt=False, tm=MM_TM):
    flat = [ab for p in prods for ab in p]
    counts = [len(p) for p in prods]
    M = flat[0][0].shape[0]
    N = flat[0][1].shape[0 if nt else 1]
    tm = min(tm, M)
    n_in = 2 * len(flat) + len(extras)

    def body(*refs):
        ins, outs = refs[:n_in], refs[n_in:]
        for j in range(N // cn):
            cols = slice(j * cn, (j + 1) * cn)
            k, ps = 0, []
            for cnt in counts:
                acc = None
                for _ in range(cnt):
                    b = ins[k + 1][cols, :] if nt else ins[k + 1][:, cols]
                    d = _dot(ins[k][...], b, 1, 1 if nt else 0)
                    acc = d if acc is None else acc + d
                    k += 2
                ps.append(acc)
            res = epi(ps, [r[:, cols] for r in ins[k:]])
            for o, r in zip(outs, res):
                o[:, cols] = r.astype(o.dtype)

    in_specs, args = [], []
    for a, b in flat:
        in_specs.append(pl.BlockSpec((tm, a.shape[1]), lambda i: (i, 0)))
        in_specs.append(pl.BlockSpec(b.shape, lambda i: (0, 0), pipeline_mode=pl.Buffered(1)))
        args += [a, b]
    for e in extras:
        e, off = e if isinstance(e, tuple) else (e, 0)
        rows = 1 if e.shape[0] == 1 else tm
        in_specs.append(pl.BlockSpec((rows, N), lambda i, off=off, rows=rows: (0 if rows == 1 else i, off)))
        args.append(e)
    return pl.pallas_call(
        body, name=name, grid=(M // tm,), in_specs=in_specs,
        out_specs=[pl.BlockSpec((tm, N), lambda i: (i, 0)) for _ in out_dtypes],
        out_shape=[_sds((M, N), dt) for dt in out_dtypes],
        compiler_params=_params(("parallel",)))(*args)


def _mm_tn(name, a, b, out_dtype, tk, tn, tt=MM_TT):
    T, Ka = a.shape
    N = b.shape[1]
    tt = min(tt, T)
    steps = T // tt

    def body(a_ref, b_ref, o_ref, acc):
        t = pl.program_id(2)

        @pl.when(t == 0)
        def _():
            acc[...] = jnp.zeros_like(acc)

        acc[...] += _dot(a_ref[...], b_ref[...], 0, 0)

        @pl.when(t == steps - 1)
        def _():
            o_ref[...] = acc[...].astype(o_ref.dtype)

    return pl.pallas_call(
        body, name=name, grid=(Ka // tk, N // tn, steps),
        in_specs=[pl.BlockSpec((tt, tk), lambda i, j, t: (t, i)),
                  pl.BlockSpec((tt, tn), lambda i, j, t: (t, j))],
        out_specs=pl.BlockSpec((tk, tn), lambda i, j, t: (i, j)),
        out_shape=_sds((Ka, N), out_dtype),
        scratch_shapes=[pltpu.VMEM((tk, tn), F32)],
        compiler_params=_params(("parallel", "parallel", "arbitrary")))(a, b)


def _first(ps, es):
    return (ps[0],)


def _rows(name, body, ins, out_shapes, T, tr=ROW_TILE):
    tr = min(tr, T)

    def spec(shape):
        if shape[0] == T:
            return pl.BlockSpec((tr,) + tuple(shape[1:]), lambda i: (i,) + (0,) * (len(shape) - 1))
        return pl.BlockSpec(tuple(shape), lambda i: (0,) * len(shape))

    return pl.pallas_call(
        body, name=name, grid=(T // tr,),
        in_specs=[spec(a.shape) for a in ins], out_specs=[spec(s.shape) for s in out_shapes],
        out_shape=out_shapes, compiler_params=_params(("arbitrary",)))(*ins)


def _rms(x):
    r = lax.rsqrt(jnp.mean(x * x, axis=-1, keepdims=True) + NORM_EPS)
    return x * r, r


def _rms_bwd(dxn, xn, r):
    return r * (dxn - xn * jnp.mean(dxn * xn, axis=-1, keepdims=True))


def _colsum(v):
    return jnp.sum(v, axis=0, keepdims=True)


def _pre_norm(x, g, sc, sh):
    T = x.shape[0]

    def body(x_ref, g_ref, sc_ref, sh_ref, h_ref):
        xn, _ = _rms(x_ref[...])
        h_ref[...] = (xn * g_ref[...] * (1.0 + sc_ref[...]) + sh_ref[...]).astype(BF16)

    return _rows("pre_norm", body, [x, g, sc, sh], [_sds((T, D_MODEL), BF16)], T)[0]


def _res_norm(x, mix, gate, gpost, g2, sc2, sh2):
    T = x.shape[0]

    def body(x_ref, mix_ref, gate_ref, gp_ref, g2_ref, sc_ref, sh_ref, x1_ref, h2_ref):
        mh, _ = _rms(mix_ref[...])
        x1 = x_ref[...] + gate_ref[...] * (mh * gp_ref[...])
        x1_ref[...] = x1
        xn, _ = _rms(x1)
        h2_ref[...] = (xn * g2_ref[...] * (1.0 + sc_ref[...]) + sh_ref[...]).astype(BF16)

    return _rows("res_norm", body, [x, mix, gate, gpost, g2, sc2, sh2],
                 [_sds((T, D_MODEL), F32), _sds((T, D_MODEL), BF16)], T)


def _final_loss(x1, ff, tgt, gate, gpost):
    T = x1.shape[0]

    def body(x1_ref, ff_ref, t_ref, gate_ref, gp_ref, dy_ref, dff_ref, acc_ref, loss_ref):
        @pl.when(pl.program_id(0) == 0)
        def _():
            acc_ref[...] = jnp.zeros_like(acc_ref)
            loss_ref[...] = jnp.zeros_like(loss_ref)

        fh, r = _rms(ff_ref[...])
        gate, gp = gate_ref[...], gp_ref[...]
        e = x1_ref[...] + gate * (fh * gp) - t_ref[...]
        loss_ref[...] += 0.5 * jnp.sum(jnp.mean(e * e, axis=-1, keepdims=True))
        dy = e * (1.0 / D_MODEL)
        dy_ref[...] = dy
        acc_ref[0:1, :] += _colsum(dy * fh * gp)
        acc_ref[1:2, :] += _colsum(dy * gate * fh)
        dff_ref[...] = _rms_bwd(dy * gate * gp, fh, r).astype(BF16)

    return _rows("final_loss", body, [x1, ff, tgt, gate, gpost],
                 [_sds((T, D_MODEL), F32), _sds((T, D_MODEL), BF16),
                  _sds((8, D_MODEL), F32), _sds((1, 128), F32)], T)


def _res_norm_bwd(x1, mix, dh2, dy, sc2, gate, g2, gpost):
    T = x1.shape[0]

    def body(x1_ref, mix_ref, dh_ref, dy_ref, sc_ref, gate_ref, g2_ref, gp_ref,
             dx1_ref, dmix_ref, acc_ref):
        @pl.when(pl.program_id(0) == 0)
        def _():
            acc_ref[...] = jnp.zeros_like(acc_ref)

        xn, r1 = _rms(x1_ref[...])
        dh, sc, g2 = dh_ref[...], sc_ref[...], g2_ref[...]
        acc_ref[0:1, :] += _colsum(dh * xn * g2)
        acc_ref[1:2, :] += _colsum(dh)
        acc_ref[2:3, :] += _colsum(dh * (1.0 + sc) * xn)
        dx1 = dy_ref[...] + _rms_bwd(dh * (1.0 + sc) * g2, xn, r1)
        dx1_ref[...] = dx1
        mh, rm = _rms(mix_ref[...])
        gate, gp = gate_ref[...], gp_ref[...]
        acc_ref[3:4, :] += _colsum(dx1 * mh * gp)
        acc_ref[4:5, :] += _colsum(dx1 * gate * mh)
        dmix_ref[...] = _rms_bwd(dx1 * gate * gp, mh, rm).astype(BF16)

    return _rows("res_norm_bwd", body, [x1, mix, dh2, dy, sc2, gate, g2, gpost],
                 [_sds((T, D_MODEL), F32), _sds((T, D_MODEL), BF16), _sds((8, D_MODEL), F32)], T)


def _pre_norm_bwd(x, dh, dx1, g, sc):
    T = x.shape[0]

    def body(x_ref, dh_ref, dx1_ref, g_ref, sc_ref, dx_ref, acc_ref):
        @pl.when(pl.program_id(0) == 0)
        def _():
            acc_ref[...] = jnp.zeros_like(acc_ref)

        xn, r = _rms(x_ref[...])
        dh, sc, g = dh_ref[...], sc_ref[...], g_ref[...]
        acc_ref[0:1, :] += _colsum(dh * xn * g)
        acc_ref[1:2, :] += _colsum(dh)
        acc_ref[2:3, :] += _colsum(dh * (1.0 + sc) * xn)
        dx_ref[...] = dx1_ref[...] + _rms_bwd(dh * (1.0 + sc) * g, xn, r)

    return _rows("pre_norm_bwd", body, [x, dh, dx1, g, sc],
                 [_sds((T, D_MODEL), F32), _sds((8, D_MODEL), F32)], T)


def _rope_tables(pos_col, inv_freq):
    T = pos_col.shape[0]

    def body(p_ref, f_ref, c_ref, s_ref):
        ang = p_ref[...].astype(F32) * f_ref[...]
        lane = lax.broadcasted_iota(jnp.int32, ang.shape, 1)
        c_ref[...] = jnp.cos(ang)
        s_ref[...] = jnp.where(lane % HEAD_DIM < HEAD_DIM // 2, -1.0, 1.0) * jnp.sin(ang)

    return _rows("rope_tables", body, [pos_col, inv_freq],
                 [_sds((T, 128), F32), _sds((T, 128), F32)], T, tr=512)


def _swap_halves(t):
    W = t.shape[1]
    lane = lax.broadcasted_iota(jnp.int32, t.shape, 1)
    half = HEAD_DIM // 2
    return jnp.where(lane % HEAD_DIM < half, pltpu.roll(t, W - half, 1), pltpu.roll(t, half, 1))


def _widen(c, W):
    return c if W == 128 else jnp.concatenate([c] * (W // 128), axis=1)


def _rope(t, c, s):
    W = t.shape[1]
    return t * _widen(c, W) + _swap_halves(t) * _widen(s, W)


def _unrope(dy, c, s):
    W = dy.shape[1]
    return dy * _widen(c, W) + _swap_halves(dy * _widen(s, W))


def _attn_mask(n):
    qi = lax.broadcasted_iota(jnp.int32, (ATTN_BLK, 2 * ATTN_BLK), 0)
    kj = lax.broadcasted_iota(jnp.int32, (ATTN_BLK, 2 * ATTN_BLK), 1)
    rel = kj - ATTN_BLK
    return (rel <= qi) & (qi - rel < WINDOW) & ((n > 0) | (kj >= ATTN_BLK))


def _attn_load(cur, prv, cc, sc, cp, sp):
    x, xp = cur[...], prv[...]
    q = _rope(x[:, :512], cc[...], sc[...]) * (HEAD_DIM ** -0.5)
    k = jnp.concatenate([_rope(xp[:, 512:640], cp[...], sp[...]),
                         _rope(x[:, 512:640], cc[...], sc[...])], axis=0)
    v = jnp.concatenate([xp[:, 640:768], x[:, 640:768]], axis=0)
    return q, k, v


ROLLED = tuple(h for h in range(N_Q_HEADS) if h % 2 != h // (N_Q_HEADS // N_KV_HEADS))


def _pair_heads(t):
    half = lax.broadcasted_iota(jnp.int32, (ATTN_BLK, 128), 1) // HEAD_DIM
    return jnp.stack([jnp.where(half == h % 2, t[:, 128 * (h // 2):128 * (h // 2) + 128], 0.0)
                      for h in range(N_Q_HEADS)])


def _kv_heads(t):
    half = lax.broadcasted_iota(jnp.int32, t.shape, 1) // HEAD_DIM
    tr = pltpu.roll(t, HEAD_DIM, 1)
    return jnp.stack([jnp.where(half == h % 2, tr if h in ROLLED else t, 0.0)
                      for h in range(N_Q_HEADS)])


def _sink_column(snk):
    return jnp.stack([jnp.full((1, 1), snk[0, h], F32) for h in range(N_Q_HEADS)])


def _attn_probs(qh, kh, mask, sink):
    s = jnp.where(mask, _bdot(qh, kh, 2, 2), NEG_INF)
    m = jnp.maximum(jnp.max(s, axis=-1, keepdims=True), sink)
    p = jnp.exp(s - m)
    es = jnp.exp(sink - m)
    rl = 1.0 / (jnp.sum(p, axis=-1, keepdims=True) + es)
    return p, es, rl


def _attn_specs(nb):
    blk = lambda w: pl.BlockSpec((ATTN_BLK, w), lambda n: (n, 0))
    prv = lambda w: pl.BlockSpec((ATTN_BLK, w), lambda n: (jnp.maximum(n - 1, 0), 0))
    return [blk(768), prv(768), blk(128), blk(128), prv(128), prv(128),
            pl.BlockSpec(memory_space=pltpu.SMEM)]


def _attn_fwd(pa, cos, sin, sinks):
    T = pa.shape[0]
    nb = T // ATTN_BLK

    def body(cur, prv, cc, sc, cp, sp, snk, y_ref):
        n = pl.program_id(0)
        q, k, v = _attn_load(cur, prv, cc, sc, cp, sp)
        qh, kh, vh = _pair_heads(q).astype(BF16), _kv_heads(k).astype(BF16), _kv_heads(v).astype(BF16)
        p, _, rl = _attn_probs(qh, kh, _attn_mask(n), _sink_column(snk))
        o = _bdot(p.astype(BF16), vh, 2, 1) * rl
        for pair in range(N_Q_HEADS // 2):
            y_ref[:, 128 * pair:128 * pair + 128] = (o[2 * pair] + o[2 * pair + 1]).astype(BF16)

    return pl.pallas_call(
        body, name="attn_fwd", grid=(nb,), in_specs=_attn_specs(nb),
        out_specs=pl.BlockSpec((ATTN_BLK, 512), lambda n: (n, 0)),
        out_shape=_sds((T, 512), BF16), compiler_params=_params(("parallel",)))(
            pa, pa, cos, sin, cos, sin, sinks)


def _attn_bwd(pa, cos, sin, sinks, dy):
    T = pa.shape[0]
    nb = T // ATTN_BLK

    def body(cur, prv, cc, sc, cp, sp, snk, dy_ref, dq_ref, dcur_ref, dprv_ref, dsink_ref):
        n = pl.program_id(0)

        @pl.when(n == 0)
        def _():
            dsink_ref[...] = jnp.zeros_like(dsink_ref)

        q, k, v = _attn_load(cur, prv, cc, sc, cp, sp)
        qh, kh, vh = _pair_heads(q).astype(BF16), _kv_heads(k).astype(BF16), _kv_heads(v).astype(BF16)
        p, es, rl = _attn_probs(qh, kh, _attn_mask(n), _sink_column(snk))
        pn = p * rl
        do = _pair_heads(dy_ref[...]).astype(BF16)
        dp = _bdot(do, vh, 2, 2)
        delta = jnp.sum(pn * dp, axis=-1, keepdims=True)
        ds = (pn * (dp - delta)).astype(BF16)
        dsink = es * rl * delta
        dq = _bdot(ds, kh, 2, 1) * (HEAD_DIM ** -0.5)
        dkh = _bdot_rows(ds, qh)
        dvh = _bdot_rows(pn.astype(BF16), do)

        def fold(t):
            same = [t[h] for h in range(N_Q_HEADS) if h not in ROLLED]
            moved = [t[h] for h in ROLLED]
            return sum(same[1:], same[0]) + pltpu.roll(sum(moved[1:], moved[0]), HEAD_DIM, 1)

        dk, dv = fold(dkh), fold(dvh)
        for h in range(N_Q_HEADS):
            dsink_ref[h:h + 1, :] += -jnp.sum(dsink[h])
        for pair in range(N_Q_HEADS // 2):
            dq_ref[:, 128 * pair:128 * pair + 128] = _unrope(
                dq[2 * pair] + dq[2 * pair + 1], cc[...], sc[...]).astype(BF16)
        dcur_ref[:, 0:128] = dk[ATTN_BLK:]
        dcur_ref[:, 128:256] = dv[ATTN_BLK:]
        dprv_ref[:, 0:128] = dk[:ATTN_BLK]
        dprv_ref[:, 128:256] = dv[:ATTN_BLK]

    blk = lambda w: pl.BlockSpec((ATTN_BLK, w), lambda n: (n, 0))
    return pl.pallas_call(
        body, name="attn_bwd", grid=(nb,), in_specs=_attn_specs(nb) + [blk(512)],
        out_specs=[blk(512), blk(256), blk(256), pl.BlockSpec((8, 128), lambda n: (0, 0))],
        out_shape=[_sds((T, 512), BF16), _sds((T, 256), F32), _sds((T, 256), F32),
                   _sds((8, 128), F32)],
        compiler_params=_params(("arbitrary",)))(pa, pa, cos, sin, cos, sin, sinks, dy)


def _attn_kv_combine(dcur, dprv, cos, sin):
    T = dcur.shape[0]
    nb = T // ATTN_BLK

    def body(c_ref, p_ref, cc, sc, o_ref):
        n = pl.program_id(0)
        t = c_ref[...] + jnp.where(n < nb - 1, p_ref[...], 0.0)
        o_ref[:, 0:128] = _unrope(t[:, 0:128], cc[...], sc[...]).astype(BF16)
        o_ref[:, 128:256] = t[:, 128:256].astype(BF16)

    blk = lambda w: pl.BlockSpec((ATTN_BLK, w), lambda n: (n, 0))
    nxt = pl.BlockSpec((ATTN_BLK, 256), lambda n: (jnp.minimum(n + 1, nb - 1), 0))
    return pl.pallas_call(
        body, name="attn_kv_combine", grid=(nb,), in_specs=[blk(256), nxt, blk(128), blk(128)],
        out_specs=blk(256), out_shape=_sds((T, 256), BF16),
        compiler_params=_params(("parallel",)))(dcur, dprv, cos, sin)


CONV_COLS = 2 * MLSTM_HEADS * MLSTM_HEAD_DIM


def _conv_pre(cur_ref, halo_ref, w_ref, b_ref, i, tr):
    xx = jnp.concatenate([jnp.where(i > 0, halo_ref[...], 0.0), cur_ref[...]], axis=0)
    taps = [(pltpu.roll(xx, CONV_WIDTH - 1 - j, 0) if j < CONV_WIDTH - 1 else xx)[8:8 + tr]
            for j in range(CONV_WIDTH)]
    pre = b_ref[...]
    for j in range(CONV_WIDTH):
        pre = pre + taps[j] * w_ref[j:j + 1, :]
    return pre, taps


def _conv_specs(T, tr):
    return [pl.BlockSpec((tr, CONV_COLS), lambda i: (i, 0)),
            pl.BlockSpec((8, CONV_COLS), lambda i: (jnp.maximum(i * (tr // 8) - 1, 0), 0)),
            pl.BlockSpec((CONV_WIDTH, CONV_COLS), lambda i: (0, 0)),
            pl.BlockSpec((1, CONV_COLS), lambda i: (0, 0))]


def _conv_fwd(pm, w, b):
    T = pm.shape[0]
    tr = min(ROW_TILE, T)

    def body(cur_ref, halo_ref, w_ref, b_ref, o_ref):
        pre, _ = _conv_pre(cur_ref, halo_ref, w_ref, b_ref, pl.program_id(0), tr)
        o_ref[...] = pre * _sigmoid(pre)

    return pl.pallas_call(
        body, name="conv_fwd", grid=(T // tr,), in_specs=_conv_specs(T, tr),
        out_specs=pl.BlockSpec((tr, CONV_COLS), lambda i: (i, 0)),
        out_shape=_sds((T, CONV_COLS), F32), compiler_params=_params(("parallel",)))(pm, pm, w, b)


def _conv_bwd_pre(pm, w, b, dqk):
    T = pm.shape[0]
    tr = min(ROW_TILE, T)

    def body(cur_ref, halo_ref, w_ref, b_ref, d_ref, dpre_ref, acc_ref):
        i = pl.program_id(0)

        @pl.when(i == 0)
        def _():
            acc_ref[...] = jnp.zeros_like(acc_ref)

        pre, taps = _conv_pre(cur_ref, halo_ref, w_ref, b_ref, i, tr)
        sg = _sigmoid(pre)
        dpre = d_ref[...] * (sg * (1.0 + pre * (1.0 - sg)))
        dpre_ref[...] = dpre
        for j in range(CONV_WIDTH):
            acc_ref[j:j + 1, :] += _colsum(dpre * taps[j])
        acc_ref[CONV_WIDTH:CONV_WIDTH + 1, :] += _colsum(dpre)

    return pl.pallas_call(
        body, name="conv_bwd_pre", grid=(T // tr,),
        in_specs=_conv_specs(T, tr) + [pl.BlockSpec((tr, CONV_COLS), lambda i: (i, 0))],
        out_specs=[pl.BlockSpec((tr, CONV_COLS), lambda i: (i, 0)),
                   pl.BlockSpec((8, CONV_COLS), lambda i: (0, 0))],
        out_shape=[_sds((T, CONV_COLS), F32), _sds((8, CONV_COLS), F32)],
        compiler_params=_params(("arbitrary",)))(pm, pm, w, b, dqk)


def _conv_bwd_in(dpre, w):
    T = dpre.shape[0]
    tr = min(ROW_TILE, T)
    nt = T // tr

    def body(cur_ref, halo_ref, w_ref, o_ref):
        i = pl.program_id(0)
        yy = jnp.concatenate([cur_ref[...], jnp.where(i < nt - 1, halo_ref[...], 0.0)], axis=0)
        du = cur_ref[...] * w_ref[CONV_WIDTH - 1:CONV_WIDTH, :]
        for j in range(CONV_WIDTH - 1):
            k = CONV_WIDTH - 1 - j
            du = du + pltpu.roll(yy, tr + 8 - k, 0)[:tr] * w_ref[j:j + 1, :]
        o_ref[...] = du.astype(BF16)

    return pl.pallas_call(
        body, name="conv_bwd_in", grid=(nt,),
        in_specs=[pl.BlockSpec((tr, CONV_COLS), lambda i: (i, 0)),
                  pl.BlockSpec((8, CONV_COLS),
                               lambda i: (jnp.minimum((i + 1) * (tr // 8), T // 8 - 1), 0)),
                  pl.BlockSpec((CONV_WIDTH, CONV_COLS), lambda i: (0, 0))],
        out_specs=pl.BlockSpec((tr, CONV_COLS), lambda i: (i, 0)),
        out_shape=_sds((T, CONV_COLS), BF16), compiler_params=_params(("parallel",)))(dpre, dpre, w)


def _log_sigmoid(x):
    return jnp.minimum(x, 0.0) - jnp.log1p(jnp.exp(-jnp.abs(x)))


def _chunk_cumsum(x, axis):
    idx = lax.broadcasted_iota(jnp.int32, x.shape, axis) % MLSTM_CHUNK
    k = 1
    while k < MLSTM_CHUNK:
        x = x + jnp.where(idx >= k, pltpu.roll(x, k, axis), 0.0)
        k *= 2
    return x


def _chunk_rev_cumsum(x, axis):
    n = x.shape[axis]
    idx = lax.broadcasted_iota(jnp.int32, x.shape, axis) % MLSTM_CHUNK
    k = 1
    while k < MLSTM_CHUNK:
        x = x + jnp.where(idx < MLSTM_CHUNK - k, pltpu.roll(x, n - k, axis), 0.0)
        k *= 2
    return x


def _mlstm_gates(gc_ref, bc_ref, gr_ref, br_ref):
    gc = gc_ref[...] + bc_ref[...]
    gr = gr_ref[...] + br_ref[...]
    return gc, _chunk_cumsum(_log_sigmoid(gc), 0), gr, _chunk_cumsum(_log_sigmoid(gr), 1)


def _heads(ref, base=0):
    D = MLSTM_HEAD_DIM
    return jnp.stack([ref[:, base + D * h:base + D * h + D] for h in range(MLSTM_HEADS)])


def _mlstm_inputs(q_ref, k_ref, v_ref, gc, bc, gr, br):
    H = MLSTM_HEADS
    q, v = _heads(q_ref), _heads(v_ref)
    ks = _heads(k_ref) * (MLSTM_HEAD_DIM ** -0.5)
    return dict(
        q=q, ks=ks, qb=q.astype(BF16), kb=ks.astype(BF16), vb=v.astype(BF16),
        b_col=jnp.stack([bc[:, H + h:H + h + 1] for h in range(H)]),
        i_col=jnp.stack([gc[:, h:h + 1] for h in range(H)]),
        b_row=jnp.stack([br[H + h:H + h + 1, :] for h in range(H)]),
        i_row=jnp.stack([gr[h:h + 1, :] for h in range(H)]))


def _mlstm_head(f, c_prev, n_prev, m_prev):
    L = MLSTM_CHUNK
    q, qb = f["q"], f["qb"]
    t = lax.broadcasted_iota(jnp.int32, (1, 2 * L, 2 * L), 1)
    s = lax.broadcasted_iota(jnp.int32, (1, 2 * L, 2 * L), 2)
    mask = (t // L == s // L) & (s <= t)
    d = jnp.where(mask, f["b_col"] - f["b_row"] + f["i_row"], NEG_INF)
    row = lax.broadcasted_iota(jnp.int32, (1, 2 * L, 1), 1)
    inter = f["b_col"] + jnp.where(row < L, m_prev[0], m_prev[1])
    m_t = jnp.maximum(inter, jnp.max(d, axis=-1, keepdims=True))
    w_intra = jnp.exp(d - m_t)
    w_inter = jnp.exp(inter - m_t)
    sc = _bdot(qb, f["kb"], 2, 2) * w_intra
    qc = jnp.concatenate([_bdot(qb[:, :L], c_prev[0].astype(BF16), 2, 1),
                          _bdot(qb[:, L:], c_prev[1].astype(BF16), 2, 1)], axis=1)
    qn = jnp.concatenate([jnp.sum(q[:, :L] * n_prev[0], axis=-1, keepdims=True),
                          jnp.sum(q[:, L:] * n_prev[1], axis=-1, keepdims=True)], axis=1)
    num = _bdot(sc.astype(BF16), f["vb"], 2, 1) + w_inter * qc
    den = jnp.sum(sc, axis=-1, keepdims=True) + w_inter * qn
    return dict(f, w_intra=w_intra, w_inter=w_inter, sc=sc, qc=qc, qn=qn, num=num, den=den,
                floor=jnp.exp(-m_t))


def _mlstm_update(f, ch, c, n, m):
    L = MLSTM_CHUNK
    rows = slice(L * ch, L * ch + L)
    b_col = f["b_col"][:, rows]
    g_last = b_col[:, L - 1:L]
    a_col = g_last - b_col + f["i_col"][:, rows]
    m_new = jnp.maximum(g_last + m, jnp.max(a_col, axis=1, keepdims=True))
    decay = jnp.exp(g_last + m - m_new)
    e_a = jnp.exp(a_col - m_new)
    kw = f["ks"][:, rows] * e_a
    c_new = decay * c + _bdot_rows(kw.astype(BF16), f["vb"][:, rows])
    n_new = decay * n + jnp.sum(kw, axis=1, keepdims=True)
    return c_new, n_new, m_new, decay, e_a, kw


def _mlstm_specs(T, order):
    blk = lambda w, col: pl.BlockSpec((STEP_ROWS, w), lambda s: (order(s), col))
    return [blk(512, 0), blk(512, 1), blk(512, 2), blk(128, 0),
            pl.BlockSpec((1, 128), lambda s: (0, 0)),
            pl.BlockSpec((8, STEP_ROWS), lambda s: (0, order(s))),
            pl.BlockSpec((8, 128), lambda s: (0, 0))]


def _lanes(m):
    return jnp.broadcast_to(m, m.shape[:-1] + (128,))


def _mlstm_fwd(qk, pm, gcol, bcol, grow, brow):
    T = qk.shape[0]
    steps = T // STEP_ROWS
    H, D = MLSTM_HEADS, MLSTM_HEAD_DIM

    def body(q_ref, k_ref, v_ref, gc_ref, bc_ref, gr_ref, br_ref, h_ref, cs_ref, ns_ref, ms_ref,
             c_st, n_st, m_st):
        @pl.when(pl.program_id(0) == 0)
        def _():
            c_st[...] = jnp.zeros_like(c_st)
            n_st[...] = jnp.zeros_like(n_st)
            m_st[...] = jnp.zeros_like(m_st)

        f = _mlstm_inputs(q_ref, k_ref, v_ref, *_mlstm_gates(gc_ref, bc_ref, gr_ref, br_ref))
        c0, n0, m0 = c_st[...], n_st[...], m_st[:, :, 0:1]
        c1, n1, m1, _, _, _ = _mlstm_update(f, 0, c0, n0, m0)
        c2, n2, m2, _, _, _ = _mlstm_update(f, 1, c1, n1, m1)
        f = _mlstm_head(f, (c0, c1), (n0, n1), (m0, m1))
        h = f["num"] / jnp.maximum(jnp.abs(f["den"]), f["floor"])
        for hd in range(H):
            h_ref[:, D * hd:D * hd + D] = h[hd]
        cs_ref[0], cs_ref[1] = c0, c1
        ns_ref[0], ns_ref[1] = n0, n1
        ms_ref[0], ms_ref[1] = _lanes(m0), _lanes(m1)
        c_st[...], n_st[...], m_st[...] = c2, n2, _lanes(m2)

    vec = pl.BlockSpec((2, H, 1, 128), lambda s: (s, 0, 0, 0))
    return pl.pallas_call(
        body, name="mlstm_fwd", grid=(steps,), in_specs=_mlstm_specs(T, lambda s: s),
        out_specs=[pl.BlockSpec((STEP_ROWS, 512), lambda s: (s, 0)),
                   pl.BlockSpec((2, H, 128, 128), lambda s: (s, 0, 0, 0)), vec, vec],
        out_shape=[_sds((T, 512), F32), _sds((2 * steps, H, 128, 128), F32),
                   _sds((2 * steps, H, 1, 128), F32), _sds((2 * steps, H, 1, 128), F32)],
        scratch_shapes=[pltpu.VMEM((H, 128, 128), F32), pltpu.VMEM((H, 1, 128), F32),
                        pltpu.VMEM((H, 1, 128), F32)],
        compiler_params=_params(("arbitrary",)))(qk, qk, pm, gcol, bcol, grow, brow)


def _mlstm_bwd(qk, pm, gcol, bcol, grow, brow, cs, ns, ms, dh):
    T = qk.shape[0]
    steps = T // STEP_ROWS
    H, L, D = MLSTM_HEADS, MLSTM_CHUNK, MLSTM_HEAD_DIM
    rev = lambda s: steps - 1 - s

    def body(q_ref, k_ref, v_ref, gc_ref, bc_ref, gr_ref, br_ref, cs_ref, ns_ref, ms_ref, dh_ref,
             dqk_ref, dv_ref, dgc_ref, dgr_ref, dc_st, dn_st):
        @pl.when(pl.program_id(0) == 0)
        def _():
            dc_st[...] = jnp.zeros_like(dc_st)
            dn_st[...] = jnp.zeros_like(dn_st)

        f = _mlstm_inputs(q_ref, k_ref, v_ref, *_mlstm_gates(gc_ref, bc_ref, gr_ref, br_ref))
        c_prev = (cs_ref[0], cs_ref[1])
        n_prev = (ns_ref[0], ns_ref[1])
        m_prev = (ms_ref[0, :, :, 0:1], ms_ref[1, :, :, 0:1])
        f = _mlstm_head(f, c_prev, n_prev, m_prev)
        big = jnp.abs(f["den"]) > f["floor"]
        rden = 1.0 / jnp.where(big, jnp.abs(f["den"]), f["floor"])
        dnum = _heads(dh_ref) * rden
        hdh = jnp.sum(f["num"] * dnum, axis=-1, keepdims=True)
        dden = jnp.where(big, -hdh * rden * jnp.sign(f["den"]), 0.0)
        dnum_b = dnum.astype(BF16)
        dsc = _bdot(dnum_b, f["vb"], 2, 2) + dden
        g = dsc * f["sc"]
        dv = _bdot_rows(f["sc"].astype(BF16), dnum_b)
        dqk_ = (dsc * f["w_intra"]).astype(BF16)
        dq = _bdot(dqk_, f["kb"], 2, 1)
        dks = _bdot_rows(dqk_, f["qb"])
        wdn = f["w_inter"] * dnum
        wdn_b = wdn.astype(BF16)
        wdd = f["w_inter"] * dden
        u = jnp.sum(f["qc"] * wdn, axis=-1, keepdims=True) + wdd * f["qn"]
        dks_s, dv_s, z_s, dg_s = [None, None], [None, None], [None, None], [None, None]
        dcn, dnn = dc_st[...], dn_st[...]
        for ch in (1, 0):
            rows = slice(L * ch, L * ch + L)
            _, _, _, decay, e_a, kw = _mlstm_update(f, ch, c_prev[ch], n_prev[ch], m_prev[ch])
            dcn_b = dcn.astype(BF16)
            dkw = _bdot(f["vb"][:, rows], dcn_b, 2, 2) + dnn
            dks_s[ch] = e_a * dkw
            dv_s[ch] = _bdot(kw.astype(BF16), dcn_b, 2, 1)
            z_s[ch] = e_a * jnp.sum(f["ks"][:, rows] * dkw, axis=-1, keepdims=True)
            dg_s[ch] = jnp.sum(z_s[ch], axis=1, keepdims=True) + decay * (
                jnp.sum(c_prev[ch] * dcn, axis=(1, 2), keepdims=True)
                + jnp.sum(n_prev[ch] * dnn, axis=(1, 2), keepdims=True))
            dcn = decay * dcn + _bdot_rows(f["qb"][:, rows], wdn_b[:, rows])
            dnn = decay * dnn + jnp.sum(wdd[:, rows] * f["q"][:, rows], axis=1, keepdims=True)
        dc_st[...], dn_st[...] = dcn, dnn
        dq = dq + jnp.concatenate(
            [_bdot(wdn_b[:, :L], c_prev[0].astype(BF16), 2, 2) + wdd[:, :L] * n_prev[0],
             _bdot(wdn_b[:, L:], c_prev[1].astype(BF16), 2, 2) + wdd[:, L:] * n_prev[1]], axis=1)
        dks = (dks + jnp.concatenate(dks_s, axis=1)) * (D ** -0.5)
        dv = dv + jnp.concatenate(dv_s, axis=1)
        z = jnp.concatenate(z_s, axis=1)
        row = lax.broadcasted_iota(jnp.int32, (1, STEP_ROWS, 1), 1)
        dg_col = jnp.where(row == L - 1, dg_s[0], 0.0) + jnp.where(row == 2 * L - 1, dg_s[1], 0.0)
        db_col = jnp.sum(g, axis=-1, keepdims=True) + u - z + dg_col
        g_row = jnp.sum(g, axis=1, keepdims=True)
        lane = lax.broadcasted_iota(jnp.int32, (STEP_ROWS, 128), 1)
        sub = lax.broadcasted_iota(jnp.int32, (8, STEP_ROWS), 0)
        dgc = jnp.zeros((STEP_ROWS, 128), F32)
        dgr = jnp.zeros((8, STEP_ROWS), F32)
        for hd in range(H):
            dgc = dgc + jnp.where(lane == hd, z[hd], 0.0) + jnp.where(lane == H + hd, db_col[hd], 0.0)
            dgr = dgr + jnp.where(sub == hd, g_row[hd], 0.0) - jnp.where(sub == H + hd, g_row[hd], 0.0)
            dqk_ref[:, D * hd:D * hd + D] = dq[hd]
            dqk_ref[:, H * D + D * hd:H * D + D * hd + D] = dks[hd]
            dv_ref[:, D * hd:D * hd + D] = dv[hd].astype(BF16)
        dgc_ref[...] = dgc
        dgr_ref[...] = dgr

    return pl.pallas_call(
        body, name="mlstm_bwd", grid=(steps,),
        in_specs=_mlstm_specs(T, rev) + [
            pl.BlockSpec((2, H, 128, 128), lambda s: (rev(s), 0, 0, 0)),
            pl.BlockSpec((2, H, 1, 128), lambda s: (rev(s), 0, 0, 0)),
            pl.BlockSpec((2, H, 1, 128), lambda s: (rev(s), 0, 0, 0)),
            pl.BlockSpec((STEP_ROWS, 512), lambda s: (rev(s), 0))],
        out_specs=[pl.BlockSpec((STEP_ROWS, 1024), lambda s: (rev(s), 0)),
                   pl.BlockSpec((STEP_ROWS, 512), lambda s: (rev(s), 0)),
                   pl.BlockSpec((STEP_ROWS, 128), lambda s: (rev(s), 0)),
                   pl.BlockSpec((8, STEP_ROWS), lambda s: (0, rev(s)))],
        out_shape=[_sds((T, 1024), F32), _sds((T, 512), BF16), _sds((T, 128), F32), _sds((8, T), F32)],
        scratch_shapes=[pltpu.VMEM((H, 128, 128), F32), pltpu.VMEM((H, 1, 128), F32)],
        compiler_params=_params(("arbitrary",)))(qk, qk, pm, gcol, bcol, grow, brow, cs, ns, ms, dh)


def _gate_bwd(dgc, dgr_t, gcol, bcol):
    T = dgc.shape[0]

    def body(a_ref, b_ref, g_ref, bias_ref, o_ref, acc_ref):
        @pl.when(pl.program_id(0) == 0)
        def _():
            acc_ref[...] = jnp.zeros_like(acc_ref)

        d = a_ref[...] + b_ref[...]
        lane = lax.broadcasted_iota(jnp.int32, d.shape, 1)
        is_f = (lane >= MLSTM_HEADS) & (lane < 2 * MLSTM_HEADS)
        dlogf = _chunk_rev_cumsum(jnp.where(is_f, d, 0.0), 0)
        out = jnp.where(is_f, dlogf * _sigmoid(-(g_ref[...] + bias_ref[...])), d)
        o_ref[...] = out.astype(BF16)
        acc_ref[0:1, :] += _colsum(out)

    return _rows("gate_bwd", body, [dgc, dgr_t, gcol, bcol],
                 [_sds((T, 128), BF16), _sds((8, 128), F32)], T)


def _head_norm(h, mu_axis=-1):
    mu = jnp.mean(h, axis=-1, keepdims=True)
    hc = h - mu
    r = lax.rsqrt(jnp.mean(hc * hc, axis=-1, keepdims=True) + NORM_EPS)
    return hc * r, r


def _mlstm_out(hm, pm, w):
    T = hm.shape[0]
    D = MLSTM_HEAD_DIM

    def body(h_ref, o_ref, w_ref, y_ref):
        for hd in range(MLSTM_HEADS):
            cols = slice(D * hd, D * hd + D)
            hn, _ = _head_norm(h_ref[:, cols])
            y_ref[:, cols] = (_sigmoid(o_ref[:, cols]) * hn * w_ref[:, cols]).astype(BF16)

    tr = min(ROW_TILE, T)
    return pl.pallas_call(
        body, name="mlstm_out", grid=(T // tr,),
        in_specs=[pl.BlockSpec((tr, 512), lambda i: (i, 0)), pl.BlockSpec((tr, 512), lambda i: (i, 3)),
                  pl.BlockSpec((1, 512), lambda i: (0, 0))],
        out_specs=pl.BlockSpec((tr, 512), lambda i: (i, 0)), out_shape=_sds((T, 512), BF16),
        compiler_params=_params(("parallel",)))(hm, pm, w)


def _mlstm_out_bwd(hm, pm, w, dy):
    T = hm.shape[0]
    D = MLSTM_HEAD_DIM
    tr = min(ROW_TILE, T)

    def body(h_ref, o_ref, w_ref, dy_ref, dh_ref, do_ref, acc_ref):
        @pl.when(pl.program_id(0) == 0)
        def _():
            acc_ref[...] = jnp.zeros_like(acc_ref)

        for hd in range(MLSTM_HEADS):
            cols = slice(D * hd, D * hd + D)
            hn, r = _head_norm(h_ref[:, cols])
            sg = _sigmoid(o_ref[:, cols])
            dy, w = dy_ref[:, cols], w_ref[:, cols]
            do_ref[:, cols] = (dy * hn * w * sg * (1.0 - sg)).astype(BF16)
            dyn = dy * sg
            acc_ref[0:1, cols] += _colsum(dyn * hn)
            dhn = dyn * w
            dh_ref[:, cols] = r * (dhn - jnp.mean(dhn, axis=-1, keepdims=True)
                                   - hn * jnp.mean(dhn * hn, axis=-1, keepdims=True))

    return pl.pallas_call(
        body, name="mlstm_out_bwd", grid=(T // tr,),
        in_specs=[pl.BlockSpec((tr, 512), lambda i: (i, 0)), pl.BlockSpec((tr, 512), lambda i: (i, 3)),
                  pl.BlockSpec((1, 512), lambda i: (0, 0)), pl.BlockSpec((tr, 512), lambda i: (i, 0))],
        out_specs=[pl.BlockSpec((tr, 512), lambda i: (i, 0)), pl.BlockSpec((tr, 512), lambda i: (i, 0)),
                   pl.BlockSpec((8, 512), lambda i: (0, 0))],
        out_shape=[_sds((T, 512), F32), _sds((T, 512), BF16), _sds((8, 512), F32)],
        compiler_params=_params(("arbitrary",)))(hm, pm, w, dy)


def _adamw(name, w, g, m, v, tr=64):
    R, C = w.shape
    tr = min(tr, R)
    if R % tr:
        tr, block = R, (R, 128)
        spec, grid = pl.BlockSpec(block, lambda i: (0, i)), (C // 128,)
    else:
        spec, grid = pl.BlockSpec((tr, C), lambda i: (i, 0)), (R // tr,)
    c1 = 1.0 - ADAM_B1 ** ADAM_STEP
    c2 = 1.0 - ADAM_B2 ** ADAM_STEP

    def body(w_ref, g_ref, m_ref, v_ref, d_ref, mo_ref, vo_ref):
        g = g_ref[...]
        m = ADAM_B1 * m_ref[...] + (1.0 - ADAM_B1) * g
        v = ADAM_B2 * v_ref[...] + (1.0 - ADAM_B2) * (g * g)
        mo_ref[...] = m
        vo_ref[...] = v
        d_ref[...] = -ADAM_LR * ((m / c1) / (jnp.sqrt(v / c2) + ADAM_EPS) + ADAM_WD * w_ref[...])

    return pl.pallas_call(
        body, name=name, grid=grid, in_specs=[spec] * 4, out_specs=[spec] * 3,
        out_shape=[_sds((R, C), F32)] * 3, compiler_params=_params(("parallel",)))(w, g, m, v)


def _place():
    return lax.axis_index("x"), lax.axis_index("y"), lax.axis_index("c")


def _all_gather8(name, blk, space):
    m, n = blk.shape

    def body(x_ref, out_ref, send_sems, recv_sems, local_sem):
        x, y, c = _place()
        me, sibling = (x, y, c), (x, y, 1 - c)
        chips = [(1 - x, y), (x, 1 - y), (1 - x, 1 - y)]

        def rows(px, py, pc):
            return out_ref.at[pl.ds((4 * px + 2 * py + pc) * m, m), :]

        def copy(k, block, to, src=None):
            return pltpu.make_async_remote_copy(
                src_ref=rows(*block) if src is None else src, dst_ref=rows(*block),
                send_sem=send_sems.at[k], recv_sem=recv_sems.at[k],
                device_id=to, device_id_type=MESH)

        mine = pltpu.make_async_copy(x_ref, rows(*me), local_sem)
        mine.start()
        first = [copy(0, me, sibling, src=x_ref)]
        first += [copy(1 + j, me, (*chip, c), src=x_ref) for j, chip in enumerate(chips)]
        for cp in first:
            cp.start()
        passed = [copy(4 + j, (*chip, c), sibling) for j, chip in enumerate(chips)]
        for j, chip in enumerate(chips):
            copy(1 + j, (*chip, c), me).wait_recv()
            passed[j].start()
        copy(0, sibling, me).wait_recv()
        for j, chip in enumerate(chips):
            copy(4 + j, (*chip, 1 - c), me).wait_recv()
        for cp in first + passed:
            cp.wait_send()
        mine.wait()

    return pl.pallas_call(
        body, name=name, out_shape=_sds((8 * m, n), blk.dtype),
        in_specs=[pl.BlockSpec(memory_space=space)], out_specs=pl.BlockSpec(memory_space=space),
        scratch_shapes=[pltpu.SemaphoreType.DMA((7,)), pltpu.SemaphoreType.DMA((7,)),
                        pltpu.SemaphoreType.DMA],
        compiler_params=pltpu.CompilerParams(vmem_limit_bytes=VMEM_LIMIT))(blk)


def _hbm_specs(n):
    return [pl.BlockSpec(memory_space=pl.ANY)] * n


def _swap_halves_sibling(name, srcs):
    nw = len(srcs)

    def body(*refs):
        src_refs, dst_refs, send_sems, recv_sems = refs[:nw], refs[nw:2 * nw], refs[2 * nw], refs[2 * nw + 1]
        x, y, c = _place()
        cps = [pltpu.make_async_remote_copy(
            src_ref=src_refs[w].at[_whole(src_refs[w]), pl.ds(0, 4), 1 - c], dst_ref=dst_refs[w],
            send_sem=send_sems.at[w], recv_sem=recv_sems.at[w], device_id=(x, y, 1 - c),
            device_id_type=MESH) for w in range(nw)]
        for cp in cps:
            cp.start()
        for cp in cps:
            cp.wait()

    shapes = [s.shape[:2] + s.shape[3:] for s in srcs]
    return pl.pallas_call(
        body, name=name, out_shape=[_sds(sh, s.dtype) for sh, s in zip(shapes, srcs)],
        in_specs=_hbm_specs(nw), out_specs=_hbm_specs(nw),
        scratch_shapes=[pltpu.SemaphoreType.DMA((nw,)), pltpu.SemaphoreType.DMA((nw,))])(*srcs)


def _split_start(name, srcs, lands, copies, per_array):
    nw = len(srcs)

    def body(*refs):
        send_sems, recv_sems, token = refs[2 * nw], refs[2 * nw + 1], refs[-1]
        for w in range(nw):
            for k, (s, d, dev) in enumerate(copies(refs[w], refs[nw + w], *_place())):
                pltpu.make_async_remote_copy(
                    src_ref=s, dst_ref=d, send_sem=send_sems.at[w * per_array + k],
                    recv_sem=recv_sems.at[w * per_array + k], device_id=dev, device_id_type=MESH).start()
        token[...] = jnp.zeros_like(token)

    hbm, sem = pl.BlockSpec(memory_space=pltpu.HBM), pl.BlockSpec(memory_space=pltpu.SEMAPHORE)
    arrays = list(srcs) + list(lands)
    out = pl.pallas_call(
        body, name=name,
        out_shape=(pltpu.SemaphoreType.DMA((nw * per_array,)), pltpu.SemaphoreType.DMA((nw * per_array,)),
                   *[pltpu.HBM(a.shape, a.dtype) for a in arrays], _sds((8, 128), F32)),
        in_specs=[hbm] * (2 * nw),
        out_specs=(sem, sem, *[hbm] * (2 * nw), pl.BlockSpec(memory_space=pltpu.VMEM)),
        input_output_aliases={i: 2 + i for i in range(2 * nw)},
        compiler_params=pltpu.CompilerParams(has_side_effects=pltpu.SideEffectType.DATAFLOW_SIDE_EFFECTING))(
            *[pltpu.with_memory_space_constraint(a, pltpu.HBM) for a in arrays])
    return out[0], out[1], out[2:2 + nw], out[2 + nw:2 + 2 * nw], out[-1]


def _split_wait(name, started, after, waits, per_array):
    send_sems, recv_sems, srcs, lands, _ = started
    nw = len(srcs)

    def body(*refs):
        send_sems, recv_sems = refs[2 * nw], refs[2 * nw + 1]
        x, y, c = _place()
        for w in range(nw):
            for k, (s, d) in enumerate(waits(refs[w], refs[nw + w], x, y, c)):
                cp = pltpu.make_async_remote_copy(
                    src_ref=s, dst_ref=d, send_sem=send_sems.at[w * per_array + k],
                    recv_sem=recv_sems.at[w * per_array + k], device_id=(x, y, 1 - c),
                    device_id_type=MESH)
                cp.wait_send()
                cp.wait_recv()

    hbm, sem = pl.BlockSpec(memory_space=pltpu.HBM), pl.BlockSpec(memory_space=pltpu.SEMAPHORE)
    arrays = list(srcs) + list(lands)
    out = pl.pallas_call(
        body, name=name, out_shape=tuple(pltpu.HBM(a.shape, a.dtype) for a in arrays),
        in_specs=[hbm] * (2 * nw) + [sem, sem, pl.BlockSpec(memory_space=pl.ANY)],
        out_specs=tuple([hbm] * (2 * nw)), input_output_aliases={i: i for i in range(2 * nw)},
        compiler_params=pltpu.CompilerParams(has_side_effects=pltpu.SideEffectType.DATAFLOW_SIDE_EFFECTING))(
            *arrays, send_sems, recv_sems, after)
    return list(out[nw:])


def _other_chips(x, y):
    return [(1 - x, y), (x, 1 - y), (1 - x, 1 - y)]


def _whole(ref):
    return pl.ds(0, ref.shape[0])


def _gather_sends(src_ref, land_ref, x, y, c):
    to = land_ref.at[_whole(land_ref), 2 * x + y, c]
    return [(src_ref, to, (x, y, 1 - c))] + [(src_ref, to, (px, py, c)) for px, py in _other_chips(x, y)]


def _gather_lands(src_ref, land_ref, x, y, c):
    g = _whole(land_ref)
    return [(src_ref, land_ref.at[g, 2 * x + y, 1 - c])] + [
        (src_ref, land_ref.at[g, 2 * px + py, c]) for px, py in _other_chips(x, y)]


def _scatter_sends(src_ref, land_ref, x, y, c):
    g = _whole(src_ref)
    return [(src_ref.at[g, 2 * px + py], land_ref.at[g, 2 * x + y], (px, py, c)) for px, py in _other_chips(x, y)]


def _scatter_lands(src_ref, land_ref, x, y, c):
    g = _whole(src_ref)
    return [(src_ref.at[g, 2 * x + y], land_ref.at[g, 2 * px + py]) for px, py in _other_chips(x, y)]


def _forward_sibling(name, lands):
    nw = len(lands)

    def body(*refs):
        land_refs, out_refs, send_sems, recv_sems = refs[:nw], refs[nw:2 * nw], refs[2 * nw], refs[2 * nw + 1]
        x, y, c = _place()
        cps = []
        for w in range(nw):
            g = _whole(land_refs[w])
            cps += [pltpu.make_async_remote_copy(
                src_ref=land_refs[w].at[g, 2 * px + py, c], dst_ref=out_refs[w].at[g, 2 * px + py, c],
                send_sem=send_sems.at[w, j], recv_sem=recv_sems.at[w, j], device_id=(x, y, 1 - c),
                device_id_type=MESH) for j, (px, py) in enumerate(_other_chips(x, y))]
        for cp in cps:
            cp.start()
        for w in range(nw):
            g = _whole(land_refs[w])
            for j, (px, py) in enumerate(_other_chips(x, y)):
                slot = out_refs[w].at[g, 2 * px + py, 1 - c]
                pltpu.make_async_remote_copy(src_ref=slot, dst_ref=slot, send_sem=send_sems.at[w, j],
                                             recv_sem=recv_sems.at[w, j], device_id=(x, y, 1 - c),
                                             device_id_type=MESH).wait_recv()
        for cp in cps:
            cp.wait_send()

    return pl.pallas_call(
        body, name=name, out_shape=[_sds(a.shape, a.dtype) for a in lands],
        in_specs=_hbm_specs(nw), out_specs=_hbm_specs(nw), input_output_aliases={i: i for i in range(nw)},
        scratch_shapes=[pltpu.SemaphoreType.DMA((nw, 3)), pltpu.SemaphoreType.DMA((nw, 3))])(*lands)


def _share_halves(name, reds):
    nw = len(reds)

    def body(*refs):
        src_refs, out_refs = refs[:nw], refs[nw:2 * nw]
        send_sems, recv_sems, local_sems = refs[2 * nw:]
        x, y, c = _place()
        cps, own = [], []
        for w in range(nw):
            g = _whole(src_refs[w])
            own.append(pltpu.make_async_copy(src_refs[w], out_refs[w].at[g, c], local_sems.at[w]))
            cps.append(pltpu.make_async_remote_copy(
                src_ref=src_refs[w], dst_ref=out_refs[w].at[g, c], send_sem=send_sems.at[w],
                recv_sem=recv_sems.at[w], device_id=(x, y, 1 - c), device_id_type=MESH))
        for cp in own + cps:
            cp.start()
        for w in range(nw):
            slot = out_refs[w].at[_whole(src_refs[w]), 1 - c]
            pltpu.make_async_remote_copy(src_ref=slot, dst_ref=slot, send_sem=send_sems.at[w],
                                         recv_sem=recv_sems.at[w], device_id=(x, y, 1 - c),
                                         device_id_type=MESH).wait_recv()
        for w in range(nw):
            cps[w].wait_send()
            own[w].wait()

    return pl.pallas_call(
        body, name=name, out_shape=[_sds((r.shape[0], 2) + r.shape[1:], r.dtype) for r in reds],
        in_specs=_hbm_specs(nw), out_specs=_hbm_specs(nw),
        scratch_shapes=[pltpu.SemaphoreType.DMA((nw,)), pltpu.SemaphoreType.DMA((nw,)),
                        pltpu.SemaphoreType.DMA((nw,))])(*reds)


def _pair_sum(name, full, got, core):
    g, _, _, m, n = full.shape

    def body(c_ref, a_ref, b_ref, o_ref):
        o_ref[...] = (a_ref[...].astype(F32) + b_ref[...].astype(F32)).astype(o_ref.dtype)

    slab = pl.BlockSpec((None, None, m, n), lambda w, s, c: (w, s, 0, 0))
    return pl.pallas_call(
        body, name=name,
        grid_spec=pltpu.PrefetchScalarGridSpec(
            num_scalar_prefetch=1, grid=(g, 4),
            in_specs=[pl.BlockSpec((None, None, None, m, n), lambda w, s, c: (w, s, c[0], 0, 0)), slab],
            out_specs=slab),
        out_shape=_sds(got.shape, BF16),
        compiler_params=_params(("parallel", "parallel")))(core, full, got)


def _sum4(name, a):
    g, _, m, n = a.shape

    def body(a_ref, o_ref):
        acc = a_ref[0].astype(F32)
        for s in range(1, 4):
            acc = acc + a_ref[s].astype(F32)
        o_ref[...] = acc

    return pl.pallas_call(body, name=name, grid=(g,),
                          in_specs=[pl.BlockSpec((None, 4, m, n), lambda w: (w, 0, 0, 0))],
                          out_specs=pl.BlockSpec((None, m, n), lambda w: (w, 0, 0)),
                          out_shape=_sds((g, m, n), F32), compiler_params=_params(("parallel",)))(a)


def _small_update(gathered, w, m, v):
    n = w.shape[1]
    tn = 2048
    c1 = 1.0 - ADAM_B1 ** ADAM_STEP
    c2 = 1.0 - ADAM_B2 ** ADAM_STEP

    def body(g_ref, w_ref, m_ref, v_ref, go_ref, d_ref, mo_ref, vo_ref):
        g = g_ref[0:1, :]
        for d in range(1, 8):
            g = g + g_ref[d:d + 1, :]
        go_ref[...] = g
        m = ADAM_B1 * m_ref[...] + (1.0 - ADAM_B1) * g
        v = ADAM_B2 * v_ref[...] + (1.0 - ADAM_B2) * (g * g)
        mo_ref[...] = m
        vo_ref[...] = v
        d_ref[...] = -ADAM_LR * ((m / c1) / (jnp.sqrt(v / c2) + ADAM_EPS) + ADAM_WD * w_ref[...])

    row = pl.BlockSpec((1, tn), lambda i: (0, i))
    return pl.pallas_call(
        body, name="small_update", grid=(n // tn,),
        in_specs=[pl.BlockSpec((8, tn), lambda i: (0, i)), row, row, row], out_specs=[row] * 4,
        out_shape=[_sds((1, n), F32)] * 4, compiler_params=_params(("parallel",)))(gathered, w, m, v)


def _swiglu(ps, es):
    g, u = ps
    return g * _sigmoid(g) * u, g, u


def _swiglu_bwd(ps, es):
    g, u = es[0].astype(F32), es[1].astype(F32)
    sg = _sigmoid(g)
    return ps[0] * u * (sg * (1.0 + g * (1.0 - sg))), ps[0] * (g * sg)


def _merge(ps, es):
    ga, gm = [e.astype(F32) for e in es]
    return _sigmoid(ga) * ps[0] + _sigmoid(gm) * ps[1], ps[0], ps[1]


def _merge_bwd(ps, es):
    a, b, ga, gm = [e.astype(F32) for e in es]
    sa, sm = _sigmoid(ga), _sigmoid(gm)
    dm = ps[0]
    return dm * sa, dm * sm, dm * a * (sa * (1.0 - sa)), dm * b * (sm * (1.0 - sm))


W_IN_PIECES = (("q", 512), ("kv", 256), ("mqk", 1024), ("mv", 512), ("mo", 512), ("if", 8),
               ("ga", 1024), ("gm", 1024))


def _local_step(x, tgt, pos_col, mod, sp, in_weights, late_weights, ffn_grads, mixer_grads):
    sh_m, sc_m, gate_m, sh_f, sc_f, gate_f = mod
    h = _pre_norm(x, sp["g_pre_mix"], sc_m, sh_m)
    inv = ROPE_THETA ** (-2.0 * jnp.arange(HEAD_DIM // 2, dtype=F32) / HEAD_DIM)
    cos, sin = _rope_tables(pos_col, jnp.tile(inv, 4).reshape(1, 128))
    W = dict(in_weights(h))
    w_a = jnp.concatenate([W["q"], W["kv"]], axis=0)
    w_m = jnp.concatenate([W["mqk"], W["mv"], W["mo"]], axis=0)
    w_g = jnp.concatenate([W["ga"], W["gm"]], axis=0)
    pa, = _mm("proj_attn", [[(h, w_a)]], [], _first, [F32], cn=256, nt=True)
    pm, = _mm("proj_mlstm", [[(h, w_m)]], [], _first, [F32], cn=512, nt=True)
    pif, = _mm("proj_gates", [[(h, W["if"])]], [], _first, [F32], cn=128, nt=True)
    pg, = _mm("proj_branch_gates", [[(h, w_g)]], [], _first, [BF16], cn=512, nt=True)
    ya = _attn_fwd(pa, cos, sin, sp["sinks"])
    qk = _conv_fwd(pm, sp["conv_w"], sp["conv_b"])
    bcol = jnp.pad(sp["b_if"], ((0, 0), (0, 120)))
    brow = jnp.broadcast_to(sp["b_if"].reshape(8, 1), (8, 128))
    grow = pif[:, :8].T
    hm, cs, ns, ms = _mlstm_fwd(qk, pm, pif, bcol, grow, brow)
    ym = _mlstm_out(hm, pm, sp["norm_w"])
    W.update(late_weights(ym))
    w_fg, w_fu, w_fd = W["fg"], W["fu"], W["fd"]
    merged, br_a, br_m = _mm("branches", [[(ya, W["ba"])], [(ym, W["bm"])]],
                             [(pg, 0), (pg, 1)], _merge, [BF16, BF16, BF16], cn=512, nt=True)
    mix, = _mm("mix_out", [[(merged, W["out"])]], [], _first, [F32], cn=512)
    x1, h2 = _res_norm(x, mix, gate_m, sp["g_post_mix"], sp["g_pre_ffn"], sc_f, sh_f)
    act, gt, up = _mm("ffn_in", [[(h2, w_fg)], [(h2, w_fu)]], [], _swiglu, [BF16] * 3,
                      cn=256, nt=True)
    ff, = _mm("ffn_down", [[(act, w_fd)]], [], _first, [F32], cn=512)
    dy, dff, acc_l, loss = _final_loss(x1, ff, tgt, gate_f, sp["g_post_ffn"])

    G = {}
    dgt, dup = _mm("ffn_down_bwd", [[(dff, w_fd)]], [gt, up], _swiglu_bwd, [BF16, BF16],
                   cn=256, nt=True)
    g_fd = _mm_tn("dw_ffn_down", act, dff, BF16, 1408, 512)
    dh2, = _mm("ffn_in_bwd", [[(dgt, w_fg), (dup, w_fu)]], [], _first, [F32], cn=512)
    g_fg = _mm_tn("dw_ffn_gate", dgt, h2, BF16, 1408, 1024)
    g_fu = _mm_tn("dw_ffn_up", dup, h2, BF16, 1408, 1024)
    tie = ffn_grads(g_fg, g_fu, g_fd)
    dx1, dmix, acc_r = _res_norm_bwd(x1, mix, dh2, dy, sc_f + tie, gate_m, sp["g_pre_ffn"],
                                     sp["g_post_mix"])
    d_a, d_m, dga, dgm = _mm("mix_out_bwd", [[(dmix, W["out"])]],
                             [br_a, br_m, (pg, 0), (pg, 1)], _merge_bwd,
                             [BF16] * 4, cn=512, nt=True)
    G["out"] = _mm_tn("dw_out", merged, dmix, BF16, 1024, 512)
    dya, = _mm("branch_attn_bwd", [[(d_a, W["ba"])]], [], _first, [F32], cn=512)
    dym, = _mm("branch_mlstm_bwd", [[(d_m, W["bm"])]], [], _first, [F32], cn=512)
    G["ba"] = _mm_tn("dw_branch_attn", d_a, ya, BF16, 1024, 512)
    G["bm"] = _mm_tn("dw_branch_mlstm", d_m, ym, BF16, 1024, 512)
    dhm, do_m, acc_n = _mlstm_out_bwd(hm, pm, sp["norm_w"], dym)
    dqk, dv_m, dgc, dgr = _mlstm_bwd(qk, pm, pif, bcol, grow, brow, cs, ns, ms, dhm)
    dif, acc_g = _gate_bwd(dgc, jnp.pad(dgr.T, ((0, 0), (0, 120))), pif, bcol)
    dpre, acc_c = _conv_bwd_pre(pm, sp["conv_w"], sp["conv_b"], dqk)
    du = _conv_bwd_in(dpre, sp["conv_w"])
    dq_a, dcur, dprv, dsink = _attn_bwd(pa, cos, sin, sp["sinks"], dya)
    dkv = _attn_kv_combine(dcur, dprv, cos, sin)
    dproj = {"q": dq_a, "kv": dkv, "mqk": du, "mv": dv_m, "mo": do_m, "if": dif, "ga": dga, "gm": dgm}
    for k, _ in W_IN_PIECES:
        G[k] = _mm_tn("dw_in_" + k, dproj[k], h, BF16, dproj[k].shape[1], 1024)
    w_tied = dict(W, **{"if": W["if"] + mixer_grads(G).astype(BF16)})
    dh, = _mm("proj_bwd", [[(dproj[k], w_tied[k]) for k, _ in W_IN_PIECES]], [], _first, [F32], cn=512)
    dx, acc_p = _pre_norm_bwd(x, dh, dx1, sp["g_pre_mix"], sc_m)

    small = {
        "mod": jnp.concatenate([acc_p[1], acc_p[0], acc_r[3], acc_r[1], acc_r[0], acc_l[0]]),
        "g_pre_mix": acc_p[2], "g_post_mix": acc_r[4], "b_if": acc_g[0, :8],
        "conv_w": acc_c[:CONV_WIDTH].reshape(-1), "conv_b": acc_c[CONV_WIDTH],
        "sinks": dsink[:, 0], "norm_w": acc_n[0], "g_pre_ffn": acc_r[2], "g_post_ffn": acc_l[1]}
    return loss, dx, small


IN_WIDTH = sum(n for _, n in W_IN_PIECES)
IN_SHARD = IN_WIDTH // 4
IN_SHARD_PAD = -(-IN_SHARD // 32) * 32


def _split_w_in(w_in_t):
    out, off = {}, 0
    for k, n in W_IN_PIECES:
        out[k] = w_in_t[off:off + n]
        off += n
    out["if"] = jnp.pad(out["if"], ((0, 120), (0, 0)))
    return out


def _halves(a):
    return a.reshape(4, 2, a.shape[0] // 8, a.shape[1])


SMALL = (("b_ada", 6144), ("g_pre_mix", 1024), ("g_post_mix", 1024), ("b_if", 128), ("conv_w", 4096),
         ("conv_b", 1024), ("sinks", 128), ("norm_w", 512), ("g_pre_ffn", 1024), ("g_post_ffn", 1024))
SMALL_LEN = 8 * 2048


def _pack_small(vals):
    parts = []
    for k, n in SMALL:
        v = vals[k].reshape(-1)
        parts.append(jnp.pad(v, (0, n - v.shape[0])))
    flat = jnp.concatenate(parts)
    return jnp.pad(flat, (0, SMALL_LEN - flat.shape[0]))


def _unpack_small(flat, shapes):
    out, off = {}, 0
    for k, n in SMALL:
        size = 1
        for d in shapes[k]:
            size *= d
        out[k] = flat[off:off + size].reshape(shapes[k])
        off += n
    return out


def kernel(x, c, positions, w_ada, b_ada, g_pre_mix, g_post_mix, w_in, b_if, conv_w, conv_b, attn_sinks, mlstm_norm_w, w_branch_attn, w_branch_mlstm, w_out, g_pre_ffn, g_post_ffn, w_ffn_gate, w_ffn_up, w_ffn_down, loss_target, m_w_ada, m_b_ada, m_g_pre_mix, m_g_post_mix, m_w_in, m_b_if, m_conv_w, m_conv_b, m_attn_sinks, m_mlstm_norm_w, m_w_branch_attn, m_w_branch_mlstm, m_w_out, m_g_pre_ffn, m_g_post_ffn, m_w_ffn_gate, m_w_ffn_up, m_w_ffn_down, v_w_ada, v_b_ada, v_g_pre_mix, v_g_post_mix, v_w_in, v_b_if, v_conv_w, v_conv_b, v_attn_sinks, v_mlstm_norm_w, v_w_branch_attn, v_w_branch_mlstm, v_w_out, v_g_pre_ffn, v_g_post_ffn, v_w_ffn_gate, v_w_ffn_up, v_w_ffn_down):
    xi, yi, ci = _place()
    chip = 2 * xi + yi
    dev = 2 * chip + ci
    T = x.shape[1]
    ada_cols = w_ada.shape[2]

    def my_half(a):
        n = a.shape[0] // 2
        return lax.dynamic_slice_in_dim(a, ci * n, n, axis=0).astype(BF16)

    blk = jnp.concatenate([c.reshape(-1), conv_w.reshape(-1)]).reshape(8, 256)
    got = _all_gather8("gather_cond", blk, pltpu.VMEM).reshape(8, 2048)
    c_all = got[:, :D_MODEL].astype(BF16)
    conv_full = got[::2, D_MODEL:].reshape(4, CONV_WIDTH, -1).transpose(1, 0, 2).reshape(CONV_WIDTH, -1)

    b_sh = lax.dynamic_slice_in_dim(b_ada, chip * ada_cols, ada_cols, axis=1)
    mod_part, = _mm("ada_mod", [[(c_all, w_ada[0].astype(BF16))]], [b_sh],
                    lambda ps, es: (ps[0] + es[0],), [F32], cn=512, tm=8)
    mod_all = _all_gather8("gather_mod", mod_part, pltpu.VMEM).reshape(4, 2, 8, ada_cols)[:, 0]
    mod = lax.dynamic_index_in_dim(mod_all, dev, axis=1, keepdims=False).reshape(6, 1, D_MODEL)

    def gather_start(name, blks, after):
        blks, _ = lax.optimization_barrier((blks, after))
        lands = [lax.dynamic_update_slice(lax.empty((b.shape[0], 4, 2) + b.shape[1:], BF16),
                                          b[:, None, None], (0, chip, ci, 0, 0)) for b in blks]
        return _split_start(name + "_start", blks, lands, _gather_sends, 4)

    def gather_wait(name, started, after):
        return _forward_sibling(name + "_forward", _split_wait(name + "_wait", started, after, _gather_lands, 4))

    w_in_t = jnp.pad(w_in[0].T, ((0, IN_SHARD_PAD - IN_SHARD), (0, 0)))
    in_started = gather_start("in_gather", [my_half(w_in_t)[None]], mod)
    late_started = gather_start(
        "late_gather",
        [jnp.stack([my_half(w_ffn_gate[0].T), my_half(w_ffn_up[0].T), my_half(w_ffn_down[0])]),
         my_half(w_out[0])[None], jnp.stack([my_half(w_branch_attn[0].T), my_half(w_branch_mlstm[0].T)])],
        in_started[4])
    mod = mod + (in_started[4][0, 0] + late_started[4][0, 0])

    def in_weights(after):
        g_in, = gather_wait("in_gather", in_started, after)
        return _split_w_in(g_in.reshape(4, IN_SHARD_PAD, D_MODEL)[:, :IN_SHARD].reshape(IN_WIDTH, D_MODEL))

    def late_weights(after):
        g_ffn, g_out, g_br = gather_wait("late_gather", late_started, after)
        return {"fg": g_ffn[0].reshape(D_FF, D_MODEL), "fu": g_ffn[1].reshape(D_FF, D_MODEL),
                "fd": g_ffn[2].reshape(D_FF, D_MODEL), "out": g_out.reshape(D_MODEL, D_MODEL),
                "ba": g_br[0].reshape(D_MODEL, -1), "bm": g_br[1].reshape(D_MODEL, -1)}

    core = ci.reshape(1).astype(jnp.int32)
    sent = {}

    def scatter_start(name, groups):
        theirs = _swap_halves_sibling(name + "_pair", groups)
        pairs = [_pair_sum("%s_pair_sum_%d" % (name, i), a, b, core)
                 for i, (a, b) in enumerate(zip(groups, theirs))]
        sent[name] = _split_start(name + "_start", pairs, [p + jnp.zeros((), BF16) for p in pairs],
                                  _scatter_sends, 3)
        return sent[name][4][0, 0]

    def ffn_grads(g_fg, g_fu, g_fd):
        return scatter_start("rs_ffn", [jnp.stack([_halves(g_fg), _halves(g_fu), _halves(g_fd)])])

    def mixer_grads(G):
        g_in_t = jnp.concatenate([G[k][:n] for k, n in W_IN_PIECES]).reshape(4, IN_SHARD, D_MODEL)
        g_in_t = jnp.pad(g_in_t, ((0, 0), (0, IN_SHARD_PAD - IN_SHARD), (0, 0)))
        return scatter_start("rs_mix", [g_in_t.reshape(1, 4, 2, IN_SHARD_PAD // 2, D_MODEL),
                                        _halves(G["out"])[None],
                                        jnp.stack([_halves(G["ba"]), _halves(G["bm"])])])

    sp = {"g_pre_mix": g_pre_mix, "g_post_mix": g_post_mix, "b_if": b_if, "conv_w": conv_full,
          "conv_b": conv_b, "sinks": attn_sinks, "norm_w": mlstm_norm_w, "g_pre_ffn": g_pre_ffn,
          "g_post_ffn": g_post_ffn}
    loss, dx, small = _local_step(x[0], loss_target[0], positions.reshape(T, 1), [mod[i] for i in range(6)],
                                  sp, in_weights, late_weights, ffn_grads, mixer_grads)

    landed = (_split_wait("rs_ffn_wait", sent["rs_ffn"], dx, _scatter_lands, 3)
              + _split_wait("rs_mix_wait", sent["rs_mix"], dx, _scatter_lands, 3))
    reds = [_sum4("rs_chip_sum_%d" % i, a) for i, a in enumerate(landed)]
    s_ffn, s_in, s_out, s_br = [s.reshape(s.shape[0], -1, s.shape[-1]) for s in _share_halves("rs_share", reds)]
    gsh = {"fg": s_ffn[0], "fu": s_ffn[1], "fd": s_ffn[2], "w_in": s_in[0, :IN_SHARD],
           "out": s_out[0], "ba": s_br[0], "bm": s_br[1]}

    small["b_ada"] = small.pop("mod")
    vec = _pack_small(small).reshape(8, 2048)
    g_all = _all_gather8("gather_small", vec, pltpu.VMEM).reshape(8, SMALL_LEN)
    dmod_sh = lax.dynamic_slice_in_dim(g_all[:, :6 * D_MODEL], chip * ada_cols, ada_cols, axis=1)
    g_w_ada = _mm_tn("dw_ada", c_all, dmod_sh.astype(BF16), F32, D_MODEL, 512, 8)

    smalls = {"b_ada": (b_ada, m_b_ada, v_b_ada), "g_pre_mix": (g_pre_mix, m_g_pre_mix, v_g_pre_mix),
              "g_post_mix": (g_post_mix, m_g_post_mix, v_g_post_mix), "b_if": (b_if, m_b_if, v_b_if),
              "conv_w": None, "conv_b": (conv_b, m_conv_b, v_conv_b),
              "sinks": (attn_sinks, m_attn_sinks, v_attn_sinks),
              "norm_w": (mlstm_norm_w, m_mlstm_norm_w, v_mlstm_norm_w),
              "g_pre_ffn": (g_pre_ffn, m_g_pre_ffn, v_g_pre_ffn),
              "g_post_ffn": (g_post_ffn, m_g_post_ffn, v_g_post_ffn)}
    shapes = {k: (t[0].shape if t is not None else (1, CONV_WIDTH, D_MODEL)) for k, t in smalls.items()}
    zeros = jnp.zeros((CONV_WIDTH * D_MODEL,), F32)
    packs = [_pack_small({k: (t[i] if t is not None else zeros) for k, t in smalls.items()}).reshape(1, -1)
             for i in range(3)]
    s_out = [_unpack_small(o[0], shapes) for o in _small_update(g_all, *packs)]
    g_conv = lax.dynamic_slice_in_dim(s_out[0]["conv_w"], chip * conv_w.shape[2], conv_w.shape[2], axis=2)

    res = {}
    for k, t in smalls.items():
        if t is not None:
            res[k] = tuple(o[k] for o in s_out)
    res["conv_w"] = (g_conv, *[o[None] for o in _adamw("adam_conv_w", conv_w[0], g_conv[0], m_conv_w[0], v_conv_w[0])])
    res["w_ada"] = (g_w_ada[None], *[o[None] for o in _adamw("adam_w_ada", w_ada[0], g_w_ada, m_w_ada[0], v_w_ada[0])])
    bigs = {"w_in": (w_in, m_w_in, v_w_in), "ba": (w_branch_attn, m_w_branch_attn, v_w_branch_attn),
            "bm": (w_branch_mlstm, m_w_branch_mlstm, v_w_branch_mlstm), "out": (w_out, m_w_out, v_w_out),
            "fg": (w_ffn_gate, m_w_ffn_gate, v_w_ffn_gate), "fu": (w_ffn_up, m_w_ffn_up, v_w_ffn_up),
            "fd": (w_ffn_down, m_w_ffn_down, v_w_ffn_down)}
    for k, (w, m, v) in bigs.items():
        if k in ("w_in", "fg", "fu"):
            res[k] = tuple(o.T[None] for o in (gsh[k], *_adamw("adam_" + k, w[0].T, gsh[k], m[0].T, v[0].T)))
        else:
            g = gsh[k].T if k in ("ba", "bm") else gsh[k]
            res[k] = (g[None], *[o[None] for o in _adamw("adam_" + k, w[0], g, m[0], v[0])])

    order = ("w_ada", "b_ada", "g_pre_mix", "g_post_mix", "w_in", "b_if", "conv_w", "conv_b", "sinks",
             "norm_w", "ba", "bm", "out", "g_pre_ffn", "g_post_ffn", "fg", "fu", "fd")
    total = lax.psum(loss[0, 0], ("x", "y", "c"))
    return (total, dx[None], *[res[k][0] for k in order], *[res[k][1] for k in order],
            *[res[k][2] for k in order], *[res[k][3] for k in order])
```

```python
import functools

import jax
import jax.numpy as jnp
from jax import lax
from jax.experimental import pallas as pl
from jax.experimental.pallas import tpu as pltpu

F32, BF16 = jnp.float32, jnp.bfloat16
MESH = pl.DeviceIdType.MESH

D_MODEL = 1024
N_Q_HEADS, N_KV_HEADS, HEAD_DIM, WINDOW = 8, 2, 64, 128
ROPE_THETA = 10000.0
MLSTM_HEADS, MLSTM_HEAD_DIM, MLSTM_CHUNK, CONV_WIDTH = 4, 128, 64, 4
D_FF = 2816
NORM_EPS = 1e-6
ADAM_LR, ADAM_B1, ADAM_B2, ADAM_EPS, ADAM_WD, ADAM_STEP = 0.001, 0.9, 0.999, 1e-08, 0.01, 10

VMEM_LIMIT = 56 * 1024 * 1024
ROW_TILE = 256
MM_TM = 512
MM_TT = 1024
ATTN_BLK = WINDOW
STEP_ROWS = 2 * MLSTM_CHUNK
NEG_INF = float("-inf")


def _params(sem):
    return pltpu.CompilerParams(dimension_semantics=sem, vmem_limit_bytes=VMEM_LIMIT)


def _sds(shape, dtype):
    return jax.ShapeDtypeStruct(shape, dtype)


def _sigmoid(x):
    return 1.0 / (1.0 + jnp.exp(-x))


def _dot(a, b, ca, cb):
    return lax.dot_general(a, b, (((ca,), (cb,)), ((), ())), preferred_element_type=F32)


def _bdot(a, b, ca, cb):
    return lax.dot_general(a, b, (((ca,), (cb,)), ((0,), (0,))), preferred_element_type=F32)


def _bdot_rows(a, b):
    return jnp.stack([_dot(a[h], b[h], 0, 0) for h in range(a.shape[0])])


def _mm(name, prods, extras, epi, out_dtypes, cn, nt=False, tm=MM_TM):
    flat = [ab for p in prods for ab in p]
    counts = [len(p) for p in prods]
    M = flat[0][0].shape[0]
    N = flat[0][1].shape[0 if nt else 1]
    tm = min(tm, M)
    n_in = 2 * len(flat) + len(extras)

    def body(*refs):
        ins, outs = refs[:n_in], refs[n_in:]
        for j in range(N // cn):
            cols = slice(j * cn, (j + 1) * cn)
            k, ps = 0, []
            for cnt in counts:
                acc = None
                for _ in range(cnt):
                    b = ins[k + 1][cols, :] if nt else ins[k + 1][:, cols]
                    d = _dot(ins[k][...], b, 1, 1 if nt else 0)
                    acc = d if acc is None else acc + d
                    k += 2
                ps.append(acc)
            res = epi(ps, [r[:, cols] for r in ins[k:]])
            for o, r in zip(outs, res):
                o[:, cols] = r.astype(o.dtype)

    in_specs, args = [], []
    for a, b in flat:
        in_specs.append(pl.BlockSpec((tm, a.shape[1]), lambda i: (i, 0)))
        in_specs.append(pl.BlockSpec(b.shape, lambda i: (0, 0), pipeline_mode=pl.Buffered(1)))
        args += [a, b]
    for e in extras:
        e, off = e if isinstance(e, tuple) else (e, 0)
        rows = 1 if e.shape[0] == 1 else tm
        in_specs.append(pl.BlockSpec((rows, N), lambda i, off=off, rows=rows: (0 if rows == 1 else i, off)))
        args.append(e)
    return pl.pallas_call(
        body, name=name, grid=(M // tm,), in_specs=in_specs,
        out_specs=[pl.BlockSpec((tm, N), lambda i: (i, 0)) for _ in out_dtypes],
        out_shape=[_sds((M, N), dt) for dt in out_dtypes],
        compiler_params=_params(("parallel",)))(*args)


def _mm_tn(name, a, b, out_dtype, tk, tn, tt=MM_TT):
    T, Ka = a.shape
    N = b.shape[1]
    tt = min(tt, T)
    steps = T // tt

    def body(a_ref, b_ref, o_ref, acc):
        t = pl.program_id(2)

        @pl.when(t == 0)
        def _():
            acc[...] = jnp.zeros_like(acc)

        acc[...] += _dot(a_ref[...], b_ref[...], 0, 0)

        @pl.when(t == steps - 1)
        def _():
            o_ref[...] = acc[...].astype(o_ref.dtype)

    return pl.pallas_call(
        body, name=name, grid=(Ka // tk, N // tn, steps),
        in_specs=[pl.BlockSpec((tt, tk), lambda i, j, t: (t, i)),
                  pl.BlockSpec((tt, tn), lambda i, j, t: (t, j))],
        out_specs=pl.BlockSpec((tk, tn), lambda i, j, t: (i, j)),
        out_shape=_sds((Ka, N), out_dtype),
        scratch_shapes=[pltpu.VMEM((tk, tn), F32)],
        compiler_params=_params(("parallel", "parallel", "arbitrary")))(a, b)


def _first(ps, es):
    return (ps[0],)


def _rows(name, body, ins, out_shapes, T, tr=ROW_TILE):
    tr = min(tr, T)

    def spec(shape):
        if shape[0] == T:
            return pl.BlockSpec((tr,) + tuple(shape[1:]), lambda i: (i,) + (0,) * (len(shape) - 1))
        return pl.BlockSpec(tuple(shape), lambda i: (0,) * len(shape))

    return pl.pallas_call(
        body, name=name, grid=(T // tr,),
        in_specs=[spec(a.shape) for a in ins], out_specs=[spec(s.shape) for s in out_shapes],
        out_shape=out_shapes, compiler_params=_params(("arbitrary",)))(*ins)


def _rms(x):
    r = lax.rsqrt(jnp.mean(x * x, axis=-1, keepdims=True) + NORM_EPS)
    return x * r, r


def _rms_bwd(dxn, xn, r):
    return r * (dxn - xn * jnp.mean(dxn * xn, axis=-1, keepdims=True))


def _colsum(v):
    return jnp.sum(v, axis=0, keepdims=True)


def _pre_norm(x, g, sc, sh):
    T = x.shape[0]

    def body(x_ref, g_ref, sc_ref, sh_ref, h_ref):
        xn, _ = _rms(x_ref[...])
        h_ref[...] = (xn * g_ref[...] * (1.0 + sc_ref[...]) + sh_ref[...]).astype(BF16)

    return _rows("pre_norm", body, [x, g, sc, sh], [_sds((T, D_MODEL), BF16)], T)[0]


def _res_norm(x, mix, gate, gpost, g2, sc2, sh2):
    T = x.shape[0]

    def body(x_ref, mix_ref, gate_ref, gp_ref, g2_ref, sc_ref, sh_ref, x1_ref, h2_ref):
        mh, _ = _rms(mix_ref[...])
        x1 = x_ref[...] + gate_ref[...] * (mh * gp_ref[...])
        x1_ref[...] = x1
        xn, _ = _rms(x1)
        h2_ref[...] = (xn * g2_ref[...] * (1.0 + sc_ref[...]) + sh_ref[...]).astype(BF16)

    return _rows("res_norm", body, [x, mix, gate, gpost, g2, sc2, sh2],
                 [_sds((T, D_MODEL), F32), _sds((T, D_MODEL), BF16)], T)


def _final_loss(x1, ff, tgt, gate, gpost):
    T = x1.shape[0]

    def body(x1_ref, ff_ref, t_ref, gate_ref, gp_ref, dy_ref, dff_ref, acc_ref, loss_ref):
        @pl.when(pl.program_id(0) == 0)
        def _():
            acc_ref[...] = jnp.zeros_like(acc_ref)
            loss_ref[...] = jnp.zeros_like(loss_ref)

        fh, r = _rms(ff_ref[...])
        gate, gp = gate_ref[...], gp_ref[...]
        e = x1_ref[...] + gate * (fh * gp) - t_ref[...]
        loss_ref[...] += 0.5 * jnp.sum(jnp.mean(e * e, axis=-1, keepdims=True))
        dy = e * (1.0 / D_MODEL)
        dy_ref[...] = dy
        acc_ref[0:1, :] += _colsum(dy * fh * gp)
        acc_ref[1:2, :] += _colsum(dy * gate * fh)
        dff_ref[...] = _rms_bwd(dy * gate * gp, fh, r).astype(BF16)

    return _rows("final_loss", body, [x1, ff, tgt, gate, gpost],
                 [_sds((T, D_MODEL), F32), _sds((T, D_MODEL), BF16),
                  _sds((8, D_MODEL), F32), _sds((1, 128), F32)], T)


def _res_norm_bwd(x1, mix, dh2, dy, sc2, gate, g2, gpost):
    T = x1.shape[0]

    def body(x1_ref, mix_ref, dh_ref, dy_ref, sc_ref, gate_ref, g2_ref, gp_ref,
             dx1_ref, dmix_ref, acc_ref):
        @pl.when(pl.program_id(0) == 0)
        def _():
            acc_ref[...] = jnp.zeros_like(acc_ref)

        xn, r1 = _rms(x1_ref[...])
        dh, sc, g2 = dh_ref[...], sc_ref[...], g2_ref[...]
        acc_ref[0:1, :] += _colsum(dh * xn * g2)
        acc_ref[1:2, :] += _colsum(dh)
        acc_ref[2:3, :] += _colsum(dh * (1.0 + sc) * xn)
        dx1 = dy_ref[...] + _rms_bwd(dh * (1.0 + sc) * g2, xn, r1)
        dx1_ref[...] = dx1
        mh, rm = _rms(mix_ref[...])
        gate, gp = gate_ref[...], gp_ref[...]
        acc_ref[3:4, :] += _colsum(dx1 * mh * gp)
        acc_ref[4:5, :] += _colsum(dx1 * gate * mh)
        dmix_ref[...] = _rms_bwd(dx1 * gate * gp, mh, rm).astype(BF16)

    return _rows("res_norm_bwd", body, [x1, mix, dh2, dy, sc2, gate, g2, gpost],
                 [_sds((T, D_MODEL), F32), _sds((T, D_MODEL), BF16), _sds((8, D_MODEL), F32)], T)


def _pre_norm_bwd(x, dh, dx1, g, sc):
    T = x.shape[0]

    def body(x_ref, dh_ref, dx1_ref, g_ref, sc_ref, dx_ref, acc_ref):
        @pl.when(pl.program_id(0) == 0)
        def _():
            acc_ref[...] = jnp.zeros_like(acc_ref)

        xn, r = _rms(x_ref[...])
        dh, sc, g = dh_ref[...], sc_ref[...], g_ref[...]
        acc_ref[0:1, :] += _colsum(dh * xn * g)
        acc_ref[1:2, :] += _colsum(dh)
        acc_ref[2:3, :] += _colsum(dh * (1.0 + sc) * xn)
        dx_ref[...] = dx1_ref[...] + _rms_bwd(dh * (1.0 + sc) * g, xn, r)

    return _rows("pre_norm_bwd", body, [x, dh, dx1, g, sc],
                 [_sds((T, D_MODEL), F32), _sds((8, D_MODEL), F32)], T)


def _rope_tables(pos_col, inv_freq):
    T = pos_col.shape[0]

    def body(p_ref, f_ref, c_ref, s_ref):
        ang = p_ref[...].astype(F32) * f_ref[...]
        lane = lax.broadcasted_iota(jnp.int32, ang.shape, 1)
        c_ref[...] = jnp.cos(ang)
        s_ref[...] = jnp.where(lane % HEAD_DIM < HEAD_DIM // 2, -1.0, 1.0) * jnp.sin(ang)

    return _rows("rope_tables", body, [pos_col, inv_freq],
                 [_sds((T, 128), F32), _sds((T, 128), F32)], T, tr=512)


def _swap_halves(t):
    W = t.shape[1]
    lane = lax.broadcasted_iota(jnp.int32, t.shape, 1)
    half = HEAD_DIM // 2
    return jnp.where(lane % HEAD_DIM < half, pltpu.roll(t, W - half, 1), pltpu.roll(t, half, 1))


def _widen(c, W):
    return c if W == 128 else jnp.concatenate([c] * (W // 128), axis=1)


def _rope(t, c, s):
    W = t.shape[1]
    return t * _widen(c, W) + _swap_halves(t) * _widen(s, W)


def _unrope(dy, c, s):
    W = dy.shape[1]
    return dy * _widen(c, W) + _swap_halves(dy * _widen(s, W))


def _attn_mask(n):
    qi = lax.broadcasted_iota(jnp.int32, (ATTN_BLK, 2 * ATTN_BLK), 0)
    kj = lax.broadcasted_iota(jnp.int32, (ATTN_BLK, 2 * ATTN_BLK), 1)
    rel = kj - ATTN_BLK
    return (rel <= qi) & (qi - rel < WINDOW) & ((n > 0) | (kj >= ATTN_BLK))


def _attn_load(cur, prv, cc, sc, cp, sp):
    x, xp = cur[...], prv[...]
    q = _rope(x[:, :512], cc[...], sc[...]) * (HEAD_DIM ** -0.5)
    k = jnp.concatenate([_rope(xp[:, 512:640], cp[...], sp[...]),
                         _rope(x[:, 512:640], cc[...], sc[...])], axis=0)
    v = jnp.concatenate([xp[:, 640:768], x[:, 640:768]], axis=0)
    return q, k, v


ROLLED = tuple(h for h in range(N_Q_HEADS) if h % 2 != h // (N_Q_HEADS // N_KV_HEADS))


def _pair_heads(t):
    half = lax.broadcasted_iota(jnp.int32, (ATTN_BLK, 128), 1) // HEAD_DIM
    return jnp.stack([jnp.where(half == h % 2, t[:, 128 * (h // 2):128 * (h // 2) + 128], 0.0)
                      for h in range(N_Q_HEADS)])


def _kv_heads(t):
    half = lax.broadcasted_iota(jnp.int32, t.shape, 1) // HEAD_DIM
    tr = pltpu.roll(t, HEAD_DIM, 1)
    return jnp.stack([jnp.where(half == h % 2, tr if h in ROLLED else t, 0.0)
                      for h in range(N_Q_HEADS)])


def _sink_column(snk):
    return jnp.stack([jnp.full((1, 1), snk[0, h], F32) for h in range(N_Q_HEADS)])


def _attn_probs(qh, kh, mask, sink):
    s = jnp.where(mask, _bdot(qh, kh, 2, 2), NEG_INF)
    m = jnp.maximum(jnp.max(s, axis=-1, keepdims=True), sink)
    p = jnp.exp(s - m)
    es = jnp.exp(sink - m)
    rl = 1.0 / (jnp.sum(p, axis=-1, keepdims=True) + es)
    return p, es, rl


def _attn_specs(nb):
    blk = lambda w: pl.BlockSpec((ATTN_BLK, w), lambda n: (n, 0))
    prv = lambda w: pl.BlockSpec((ATTN_BLK, w), lambda n: (jnp.maximum(n - 1, 0), 0))
    return [blk(768), prv(768), blk(128), blk(128), prv(128), prv(128),
            pl.BlockSpec(memory_space=pltpu.SMEM)]


def _attn_fwd(pa, cos, sin, sinks):
    T = pa.shape[0]
    nb = T // ATTN_BLK

    def body(cur, prv, cc, sc, cp, sp, snk, y_ref):
        n = pl.program_id(0)
        q, k, v = _attn_load(cur, prv, cc, sc, cp, sp)
        qh, kh, vh = _pair_heads(q).astype(BF16), _kv_heads(k).astype(BF16), _kv_heads(v).astype(BF16)
        p, _, rl = _attn_probs(qh, kh, _attn_mask(n), _sink_column(snk))
        o = _bdot(p.astype(BF16), vh, 2, 1) * rl
        for pair in range(N_Q_HEADS // 2):
            y_ref[:, 128 * pair:128 * pair + 128] = (o[2 * pair] + o[2 * pair + 1]).astype(BF16)

    return pl.pallas_call(
        body, name="attn_fwd", grid=(nb,), in_specs=_attn_specs(nb),
        out_specs=pl.BlockSpec((ATTN_BLK, 512), lambda n: (n, 0)),
        out_shape=_sds((T, 512), BF16), compiler_params=_params(("parallel",)))(
            pa, pa, cos, sin, cos, sin, sinks)


def _attn_bwd(pa, cos, sin, sinks, dy):
    T = pa.shape[0]
    nb = T // ATTN_BLK

    def body(cur, prv, cc, sc, cp, sp, snk, dy_ref, dq_ref, dcur_ref, dprv_ref, dsink_ref):
        n = pl.program_id(0)

        @pl.when(n == 0)
        def _():
            dsink_ref[...] = jnp.zeros_like(dsink_ref)

        q, k, v = _attn_load(cur, prv, cc, sc, cp, sp)
        qh, kh, vh = _pair_heads(q).astype(BF16), _kv_heads(k).astype(BF16), _kv_heads(v).astype(BF16)
        p, es, rl = _attn_probs(qh, kh, _attn_mask(n), _sink_column(snk))
        pn = p * rl
        do = _pair_heads(dy_ref[...]).astype(BF16)
        dp = _bdot(do, vh, 2, 2)
        delta = jnp.sum(pn * dp, axis=-1, keepdims=True)
        ds = (pn * (dp - delta)).astype(BF16)
        dsink = es * rl * delta
        dq = _bdot(ds, kh, 2, 1) * (HEAD_DIM ** -0.5)
        dkh = _bdot_rows(ds, qh)
        dvh = _bdot_rows(pn.astype(BF16), do)

        def fold(t):
            same = [t[h] for h in range(N_Q_HEADS) if h not in ROLLED]
            moved = [t[h] for h in ROLLED]
            return sum(same[1:], same[0]) + pltpu.roll(sum(moved[1:], moved[0]), HEAD_DIM, 1)

        dk, dv = fold(dkh), fold(dvh)
        for h in range(N_Q_HEADS):
            dsink_ref[h:h + 1, :] += -jnp.sum(dsink[h])
        for pair in range(N_Q_HEADS // 2):
            dq_ref[:, 128 * pair:128 * pair + 128] = _unrope(
                dq[2 * pair] + dq[2 * pair + 1], cc[...], sc[...]).astype(BF16)
        dcur_ref[:, 0:128] = dk[ATTN_BLK:]
        dcur_ref[:, 128:256] = dv[ATTN_BLK:]
        dprv_ref[:, 0:128] = dk[:ATTN_BLK]
        dprv_ref[:, 128:256] = dv[:ATTN_BLK]

    blk = lambda w: pl.BlockSpec((ATTN_BLK, w), lambda n: (n, 0))
    return pl.pallas_call(
        body, name="attn_bwd", grid=(nb,), in_specs=_attn_specs(nb) + [blk(512)],
        out_specs=[blk(512), blk(256), blk(256), pl.BlockSpec((8, 128), lambda n: (0, 0))],
        out_shape=[_sds((T, 512), BF16), _sds((T, 256), F32), _sds((T, 256), F32),
                   _sds((8, 128), F32)],
        compiler_params=_params(("arbitrary",)))(pa, pa, cos, sin, cos, sin, sinks, dy)


def _attn_kv_combine(dcur, dprv, cos, sin):
    T = dcur.shape[0]
    nb = T // ATTN_BLK

    def body(c_ref, p_ref, cc, sc, o_ref):
        n = pl.program_id(0)
        t = c_ref[...] + jnp.where(n < nb - 1, p_ref[...], 0.0)
        o_ref[:, 0:128] = _unrope(t[:, 0:128], cc[...], sc[...]).astype(BF16)
        o_ref[:, 128:256] = t[:, 128:256].astype(BF16)

    blk = lambda w: pl.BlockSpec((ATTN_BLK, w), lambda n: (n, 0))
    nxt = pl.BlockSpec((ATTN_BLK, 256), lambda n: (jnp.minimum(n + 1, nb - 1), 0))
    return pl.pallas_call(
        body, name="attn_kv_combine", grid=(nb,), in_specs=[blk(256), nxt, blk(128), blk(128)],
        out_specs=blk(256), out_shape=_sds((T, 256), BF16),
        compiler_params=_params(("parallel",)))(dcur, dprv, cos, sin)


CONV_COLS = 2 * MLSTM_HEADS * MLSTM_HEAD_DIM


def _conv_pre(cur_ref, halo_ref, w_ref, b_ref, i, tr):
    xx = jnp.concatenate([jnp.where(i > 0, halo_ref[...], 0.0), cur_ref[...]], axis=0)
    taps = [(pltpu.roll(xx, CONV_WIDTH - 1 - j, 0) if j < CONV_WIDTH - 1 else xx)[8:8 + tr]
            for j in range(CONV_WIDTH)]
    pre = b_ref[...]
    for j in range(CONV_WIDTH):
        pre = pre + taps[j] * w_ref[j:j + 1, :]
    return pre, taps


def _conv_specs(T, tr):
    return [pl.BlockSpec((tr, CONV_COLS), lambda i: (i, 0)),
            pl.BlockSpec((8, CONV_COLS), lambda i: (jnp.maximum(i * (tr // 8) - 1, 0), 0)),
            pl.BlockSpec((CONV_WIDTH, CONV_COLS), lambda i: (0, 0)),
            pl.BlockSpec((1, CONV_COLS), lambda i: (0, 0))]


def _conv_fwd(pm, w, b):
    T = pm.shape[0]
    tr = min(ROW_TILE, T)

    def body(cur_ref, halo_ref, w_ref, b_ref, o_ref):
        pre, _ = _conv_pre(cur_ref, halo_ref, w_ref, b_ref, pl.program_id(0), tr)
        o_ref[...] = pre * _sigmoid(pre)

    return pl.pallas_call(
        body, name="conv_fwd", grid=(T // tr,), in_specs=_conv_specs(T, tr),
        out_specs=pl.BlockSpec((tr, CONV_COLS), lambda i: (i, 0)),
        out_shape=_sds((T, CONV_COLS), F32), compiler_params=_params(("parallel",)))(pm, pm, w, b)


def _conv_bwd_pre(pm, w, b, dqk):
    T = pm.shape[0]
    tr = min(ROW_TILE, T)

    def body(cur_ref, halo_ref, w_ref, b_ref, d_ref, dpre_ref, acc_ref):
        i = pl.program_id(0)

        @pl.when(i == 0)
        def _():
            acc_ref[...] = jnp.zeros_like(acc_ref)

        pre, taps = _conv_pre(cur_ref, halo_ref, w_ref, b_ref, i, tr)
        sg = _sigmoid(pre)
        dpre = d_ref[...] * (sg * (1.0 + pre * (1.0 - sg)))
        dpre_ref[...] = dpre
        for j in range(CONV_WIDTH):
            acc_ref[j:j + 1, :] += _colsum(dpre * taps[j])
        acc_ref[CONV_WIDTH:CONV_WIDTH + 1, :] += _colsum(dpre)

    return pl.pallas_call(
        body, name="conv_bwd_pre", grid=(T // tr,),
        in_specs=_conv_specs(T, tr) + [pl.BlockSpec((tr, CONV_COLS), lambda i: (i, 0))],
        out_specs=[pl.BlockSpec((tr, CONV_COLS), lambda i: (i, 0)),
                   pl.BlockSpec((8, CONV_COLS), lambda i: (0, 0))],
        out_shape=[_sds((T, CONV_COLS), F32), _sds((8, CONV_COLS), F32)],
        compiler_params=_params(("arbitrary",)))(pm, pm, w, b, dqk)


def _conv_bwd_in(dpre, w):
    T = dpre.shape[0]
    tr = min(ROW_TILE, T)
    nt = T // tr

    def body(cur_ref, halo_ref, w_ref, o_ref):
        i = pl.program_id(0)
        yy = jnp.concatenate([cur_ref[...], jnp.where(i < nt - 1, halo_ref[...], 0.0)], axis=0)
        du = cur_ref[...] * w_ref[CONV_WIDTH - 1:CONV_WIDTH, :]
        for j in range(CONV_WIDTH - 1):
            k = CONV_WIDTH - 1 - j
            du = du + pltpu.roll(yy, tr + 8 - k, 0)[:tr] * w_ref[j:j + 1, :]
        o_ref[...] = du.astype(BF16)

    return pl.pallas_call(
        body, name="conv_bwd_in", grid=(nt,),
        in_specs=[pl.BlockSpec((tr, CONV_COLS), lambda i: (i, 0)),
                  pl.BlockSpec((8, CONV_COLS),
                               lambda i: (jnp.minimum((i + 1) * (tr // 8), T // 8 - 1), 0)),
                  pl.BlockSpec((CONV_WIDTH, CONV_COLS), lambda i: (0, 0))],
        out_specs=pl.BlockSpec((tr, CONV_COLS), lambda i: (i, 0)),
        out_shape=_sds((T, CONV_COLS), BF16), compiler_params=_params(("parallel",)))(dpre, dpre, w)


def _log_sigmoid(x):
    return jnp.minimum(x, 0.0) - jnp.log1p(jnp.exp(-jnp.abs(x)))


def _chunk_cumsum(x, axis):
    idx = lax.broadcasted_iota(jnp.int32, x.shape, axis) % MLSTM_CHUNK
    k = 1
    while k < MLSTM_CHUNK:
        x = x + jnp.where(idx >= k, pltpu.roll(x, k, axis), 0.0)
        k *= 2
    return x


def _chunk_rev_cumsum(x, axis):
    n = x.shape[axis]
    idx = lax.broadcasted_iota(jnp.int32, x.shape, axis) % MLSTM_CHUNK
    k = 1
    while k < MLSTM_CHUNK:
        x = x + jnp.where(idx < MLSTM_CHUNK - k, pltpu.roll(x, n - k, axis), 0.0)
        k *= 2
    return x


def _mlstm_gates(gc_ref, bc_ref, gr_ref, br_ref):
    gc = gc_ref[...] + bc_ref[...]
    gr = gr_ref[...] + br_ref[...]
    return gc, _chunk_cumsum(_log_sigmoid(gc), 0), gr, _chunk_cumsum(_log_sigmoid(gr), 1)


def _heads(ref, base=0):
    D = MLSTM_HEAD_DIM
    return jnp.stack([ref[:, base + D * h:base + D * h + D] for h in range(MLSTM_HEADS)])


def _mlstm_inputs(q_ref, k_ref, v_ref, gc, bc, gr, br):
    H = MLSTM_HEADS
    q, v = _heads(q_ref), _heads(v_ref)
    ks = _heads(k_ref) * (MLSTM_HEAD_DIM ** -0.5)
    return dict(
        q=q, ks=ks, qb=q.astype(BF16), kb=ks.astype(BF16), vb=v.astype(BF16),
        b_col=jnp.stack([bc[:, H + h:H + h + 1] for h in range(H)]),
        i_col=jnp.stack([gc[:, h:h + 1] for h in range(H)]),
        b_row=jnp.stack([br[H + h:H + h + 1, :] for h in range(H)]),
        i_row=jnp.stack([gr[h:h + 1, :] for h in range(H)]))


def _mlstm_head(f, c_prev, n_prev, m_prev):
    L = MLSTM_CHUNK
    q, qb = f["q"], f["qb"]
    t = lax.broadcasted_iota(jnp.int32, (1, 2 * L, 2 * L), 1)
    s = lax.broadcasted_iota(jnp.int32, (1, 2 * L, 2 * L), 2)
    mask = (t // L == s // L) & (s <= t)
    d = jnp.where(mask, f["b_col"] - f["b_row"] + f["i_row"], NEG_INF)
    row = lax.broadcasted_iota(jnp.int32, (1, 2 * L, 1), 1)
    inter = f["b_col"] + jnp.where(row < L, m_prev[0], m_prev[1])
    m_t = jnp.maximum(inter, jnp.max(d, axis=-1, keepdims=True))
    w_intra = jnp.exp(d - m_t)
    w_inter = jnp.exp(inter - m_t)
    sc = _bdot(qb, f["kb"], 2, 2) * w_intra
    qc = jnp.concatenate([_bdot(qb[:, :L], c_prev[0].astype(BF16), 2, 1),
                          _bdot(qb[:, L:], c_prev[1].astype(BF16), 2, 1)], axis=1)
    qn = jnp.concatenate([jnp.sum(q[:, :L] * n_prev[0], axis=-1, keepdims=True),
                          jnp.sum(q[:, L:] * n_prev[1], axis=-1, keepdims=True)], axis=1)
    num = _bdot(sc.astype(BF16), f["vb"], 2, 1) + w_inter * qc
    den = jnp.sum(sc, axis=-1, keepdims=True) + w_inter * qn
    return dict(f, w_intra=w_intra, w_inter=w_inter, sc=sc, qc=qc, qn=qn, num=num, den=den,
                floor=jnp.exp(-m_t))


def _mlstm_update(f, ch, c, n, m):
    L = MLSTM_CHUNK
    rows = slice(L * ch, L * ch + L)
    b_col = f["b_col"][:, rows]
    g_last = b_col[:, L - 1:L]
    a_col = g_last - b_col + f["i_col"][:, rows]
    m_new = jnp.maximum(g_last + m, jnp.max(a_col, axis=1, keepdims=True))
    decay = jnp.exp(g_last + m - m_new)
    e_a = jnp.exp(a_col - m_new)
    kw = f["ks"][:, rows] * e_a
    c_new = decay * c + _bdot_rows(kw.astype(BF16), f["vb"][:, rows])
    n_new = decay * n + jnp.sum(kw, axis=1, keepdims=True)
    return c_new, n_new, m_new, decay, e_a, kw


def _mlstm_specs(T, order):
    blk = lambda w, col: pl.BlockSpec((STEP_ROWS, w), lambda s: (order(s), col))
    return [blk(512, 0), blk(512, 1), blk(512, 2), blk(128, 0),
            pl.BlockSpec((1, 128), lambda s: (0, 0)),
            pl.BlockSpec((8, STEP_ROWS), lambda s: (0, order(s))),
            pl.BlockSpec((8, 128), lambda s: (0, 0))]


def _lanes(m):
    return jnp.broadcast_to(m, m.shape[:-1] + (128,))


def _mlstm_fwd(qk, pm, gcol, bcol, grow, brow):
    T = qk.shape[0]
    steps = T // STEP_ROWS
    H, D = MLSTM_HEADS, MLSTM_HEAD_DIM

    def body(q_ref, k_ref, v_ref, gc_ref, bc_ref, gr_ref, br_ref, h_ref, cs_ref, ns_ref, ms_ref,
             c_st, n_st, m_st):
        @pl.when(pl.program_id(0) == 0)
        def _():
            c_st[...] = jnp.zeros_like(c_st)
            n_st[...] = jnp.zeros_like(n_st)
            m_st[...] = jnp.zeros_like(m_st)

        f = _mlstm_inputs(q_ref, k_ref, v_ref, *_mlstm_gates(gc_ref, bc_ref, gr_ref, br_ref))
        c0, n0, m0 = c_st[...], n_st[...], m_st[:, :, 0:1]
        c1, n1, m1, _, _, _ = _mlstm_update(f, 0, c0, n0, m0)
        c2, n2, m2, _, _, _ = _mlstm_update(f, 1, c1, n1, m1)
        f = _mlstm_head(f, (c0, c1), (n0, n1), (m0, m1))
        h = f["num"] / jnp.maximum(jnp.abs(f["den"]), f["floor"])
        for hd in range(H):
            h_ref[:, D * hd:D * hd + D] = h[hd]
        cs_ref[0], cs_ref[1] = c0, c1
        ns_ref[0], ns_ref[1] = n0, n1
        ms_ref[0], ms_ref[1] = _lanes(m0), _lanes(m1)
        c_st[...], n_st[...], m_st[...] = c2, n2, _lanes(m2)

    vec = pl.BlockSpec((2, H, 1, 128), lambda s: (s, 0, 0, 0))
    return pl.pallas_call(
        body, name="mlstm_fwd", grid=(steps,), in_specs=_mlstm_specs(T, lambda s: s),
        out_specs=[pl.BlockSpec((STEP_ROWS, 512), lambda s: (s, 0)),
                   pl.BlockSpec((2, H, 128, 128), lambda s: (s, 0, 0, 0)), vec, vec],
        out_shape=[_sds((T, 512), F32), _sds((2 * steps, H, 128, 128), F32),
                   _sds((2 * steps, H, 1, 128), F32), _sds((2 * steps, H, 1, 128), F32)],
        scratch_shapes=[pltpu.VMEM((H, 128, 128), F32), pltpu.VMEM((H, 1, 128), F32),
                        pltpu.VMEM((H, 1, 128), F32)],
        compiler_params=_params(("arbitrary",)))(qk, qk, pm, gcol, bcol, grow, brow)


def _mlstm_bwd(qk, pm, gcol, bcol, grow, brow, cs, ns, ms, dh):
    T = qk.shape[0]
    steps = T // STEP_ROWS
    H, L, D = MLSTM_HEADS, MLSTM_CHUNK, MLSTM_HEAD_DIM
    rev = lambda s: steps - 1 - s

    def body(q_ref, k_ref, v_ref, gc_ref, bc_ref, gr_ref, br_ref, cs_ref, ns_ref, ms_ref, dh_ref,
             dqk_ref, dv_ref, dgc_ref, dgr_ref, dc_st, dn_st):
        @pl.when(pl.program_id(0) == 0)
        def _():
            dc_st[...] = jnp.zeros_like(dc_st)
            dn_st[...] = jnp.zeros_like(dn_st)

        f = _mlstm_inputs(q_ref, k_ref, v_ref, *_mlstm_gates(gc_ref, bc_ref, gr_ref, br_ref))
        c_prev = (cs_ref[0], cs_ref[1])
        n_prev = (ns_ref[0], ns_ref[1])
        m_prev = (ms_ref[0, :, :, 0:1], ms_ref[1, :, :, 0:1])
        f = _mlstm_head(f, c_prev, n_prev, m_prev)
        big = jnp.abs(f["den"]) > f["floor"]
        rden = 1.0 / jnp.where(big, jnp.abs(f["den"]), f["floor"])
        dnum = _heads(dh_ref) * rden
        hdh = jnp.sum(f["num"] * dnum, axis=-1, keepdims=True)
        dden = jnp.where(big, -hdh * rden * jnp.sign(f["den"]), 0.0)
        dnum_b = dnum.astype(BF16)
        dsc = _bdot(dnum_b, f["vb"], 2, 2) + dden
        g = dsc * f["sc"]
        dv = _bdot_rows(f["sc"].astype(BF16), dnum_b)
        dqk_ = (dsc * f["w_intra"]).astype(BF16)
        dq = _bdot(dqk_, f["kb"], 2, 1)
        dks = _bdot_rows(dqk_, f["qb"])
        wdn = f["w_inter"] * dnum
        wdn_b = wdn.astype(BF16)
        wdd = f["w_inter"] * dden
        u = jnp.sum(f["qc"] * wdn, axis=-1, keepdims=True) + wdd * f["qn"]
        dks_s, dv_s, z_s, dg_s = [None, None], [None, None], [None, None], [None, None]
        dcn, dnn = dc_st[...], dn_st[...]
        for ch in (1, 0):
            rows = slice(L * ch, L * ch + L)
            _, _, _, decay, e_a, kw = _mlstm_update(f, ch, c_prev[ch], n_prev[ch], m_prev[ch])
            dcn_b = dcn.astype(BF16)
            dkw = _bdot(f["vb"][:, rows], dcn_b, 2, 2) + dnn
            dks_s[ch] = e_a * dkw
            dv_s[ch] = _bdot(kw.astype(BF16), dcn_b, 2, 1)
            z_s[ch] = e_a * jnp.sum(f["ks"][:, rows] * dkw, axis=-1, keepdims=True)
            dg_s[ch] = jnp.sum(z_s[ch], axis=1, keepdims=True) + decay * (
                jnp.sum(c_prev[ch] * dcn, axis=(1, 2), keepdims=True)
                + jnp.sum(n_prev[ch] * dnn, axis=(1, 2), keepdims=True))
            dcn = decay * dcn + _bdot_rows(f["qb"][:, rows], wdn_b[:, rows])
            dnn = decay * dnn + jnp.sum(wdd[:, rows] * f["q"][:, rows], axis=1, keepdims=True)
        dc_st[...], dn_st[...] = dcn, dnn
        dq = dq + jnp.concatenate(
            [_bdot(wdn_b[:, :L], c_prev[0].astype(BF16), 2, 2) + wdd[:, :L] * n_prev[0],
             _bdot(wdn_b[:, L:], c_prev[1].astype(BF16), 2, 2) + wdd[:, L:] * n_prev[1]], axis=1)
        dks = (dks + jnp.concatenate(dks_s, axis=1)) * (D ** -0.5)
        dv = dv + jnp.concatenate(dv_s, axis=1)
        z = jnp.concatenate(z_s, axis=1)
        row = lax.broadcasted_iota(jnp.int32, (1, STEP_ROWS, 1), 1)
        dg_col = jnp.where(row == L - 1, dg_s[0], 0.0) + jnp.where(row == 2 * L - 1, dg_s[1], 0.0)
        db_col = jnp.sum(g, axis=-1, keepdims=True) + u - z + dg_col
        g_row = jnp.sum(g, axis=1, keepdims=True)
        lane = lax.broadcasted_iota(jnp.int32, (STEP_ROWS, 128), 1)
        sub = lax.broadcasted_iota(jnp.int32, (8, STEP_ROWS), 0)
        dgc = jnp.zeros((STEP_ROWS, 128), F32)
        dgr = jnp.zeros((8, STEP_ROWS), F32)
        for hd in range(H):
            dgc = dgc + jnp.where(lane == hd, z[hd], 0.0) + jnp.where(lane == H + hd, db_col[hd], 0.0)
            dgr = dgr + jnp.where(sub == hd, g_row[hd], 0.0) - jnp.where(sub == H + hd, g_row[hd], 0.0)
            dqk_ref[:, D * hd:D * hd + D] = dq[hd]
            dqk_ref[:, H * D + D * hd:H * D + D * hd + D] = dks[hd]
            dv_ref[:, D * hd:D * hd + D] = dv[hd].astype(BF16)
        dgc_ref[...] = dgc
        dgr_ref[...] = dgr

    return pl.pallas_call(
        body, name="mlstm_bwd", grid=(steps,),
        in_specs=_mlstm_specs(T, rev) + [
            pl.BlockSpec((2, H, 128, 128), lambda s: (rev(s), 0, 0, 0)),
            pl.BlockSpec((2, H, 1, 128), lambda s: (rev(s), 0, 0, 0)),
            pl.BlockSpec((2, H, 1, 128), lambda s: (rev(s), 0, 0, 0)),
            pl.BlockSpec((STEP_ROWS, 512), lambda s: (rev(s), 0))],
        out_specs=[pl.BlockSpec((STEP_ROWS, 1024), lambda s: (rev(s), 0)),
                   pl.BlockSpec((STEP_ROWS, 512), lambda s: (rev(s), 0)),
                   pl.BlockSpec((STEP_ROWS, 128), lambda s: (rev(s), 0)),
                   pl.BlockSpec((8, STEP_ROWS), lambda s: (0, rev(s)))],
        out_shape=[_sds((T, 1024), F32), _sds((T, 512), BF16), _sds((T, 128), F32), _sds((8, T), F32)],
        scratch_shapes=[pltpu.VMEM((H, 128, 128), F32), pltpu.VMEM((H, 1, 128), F32)],
        compiler_params=_params(("arbitrary",)))(qk, qk, pm, gcol, bcol, grow, brow, cs, ns, ms, dh)


def _gate_bwd(dgc, dgr_t, gcol, bcol):
    T = dgc.shape[0]

    def body(a_ref, b_ref, g_ref, bias_ref, o_ref, acc_ref):
        @pl.when(pl.program_id(0) == 0)
        def _():
            acc_ref[...] = jnp.zeros_like(acc_ref)

        d = a_ref[...] + b_ref[...]
        lane = lax.broadcasted_iota(jnp.int32, d.shape, 1)
        is_f = (lane >= MLSTM_HEADS) & (lane < 2 * MLSTM_HEADS)
        dlogf = _chunk_rev_cumsum(jnp.where(is_f, d, 0.0), 0)
        out = jnp.where(is_f, dlogf * _sigmoid(-(g_ref[...] + bias_ref[...])), d)
        o_ref[...] = out.astype(BF16)
        acc_ref[0:1, :] += _colsum(out)

    return _rows("gate_bwd", body, [dgc, dgr_t, gcol, bcol],
                 [_sds((T, 128), BF16), _sds((8, 128), F32)], T)


def _head_norm(h, mu_axis=-1):
    mu = jnp.mean(h, axis=-1, keepdims=True)
    hc = h - mu
    r = lax.rsqrt(jnp.mean(hc * hc, axis=-1, keepdims=True) + NORM_EPS)
    return hc * r, r


def _mlstm_out(hm, pm, w):
    T = hm.shape[0]
    D = MLSTM_HEAD_DIM

    def body(h_ref, o_ref, w_ref, y_ref):
        for hd in range(MLSTM_HEADS):
            cols = slice(D * hd, D * hd + D)
            hn, _ = _head_norm(h_ref[:, cols])
            y_ref[:, cols] = (_sigmoid(o_ref[:, cols]) * hn * w_ref[:, cols]).astype(BF16)

    tr = min(ROW_TILE, T)
    return pl.pallas_call(
        body, name="mlstm_out", grid=(T // tr,),
        in_specs=[pl.BlockSpec((tr, 512), lambda i: (i, 0)), pl.BlockSpec((tr, 512), lambda i: (i, 3)),
                  pl.BlockSpec((1, 512), lambda i: (0, 0))],
        out_specs=pl.BlockSpec((tr, 512), lambda i: (i, 0)), out_shape=_sds((T, 512), BF16),
        compiler_params=_params(("parallel",)))(hm, pm, w)


def _mlstm_out_bwd(hm, pm, w, dy):
    T = hm.shape[0]
    D = MLSTM_HEAD_DIM
    tr = min(ROW_TILE, T)

    def body(h_ref, o_ref, w_ref, dy_ref, dh_ref, do_ref, acc_ref):
        @pl.when(pl.program_id(0) == 0)
        def _():
            acc_ref[...] = jnp.zeros_like(acc_ref)

        for hd in range(MLSTM_HEADS):
            cols = slice(D * hd, D * hd + D)
            hn, r = _head_norm(h_ref[:, cols])
            sg = _sigmoid(o_ref[:, cols])
            dy, w = dy_ref[:, cols], w_ref[:, cols]
            do_ref[:, cols] = (dy * hn * w * sg * (1.0 - sg)).astype(BF16)
            dyn = dy * sg
            acc_ref[0:1, cols] += _colsum(dyn * hn)
            dhn = dyn * w
            dh_ref[:, cols] = r * (dhn - jnp.mean(dhn, axis=-1, keepdims=True)
                                   - hn * jnp.mean(dhn * hn, axis=-1, keepdims=True))

    return pl.pallas_call(
        body, name="mlstm_out_bwd", grid=(T // tr,),
        in_specs=[pl.BlockSpec((tr, 512), lambda i: (i, 0)), pl.BlockSpec((tr, 512), lambda i: (i, 3)),
                  pl.BlockSpec((1, 512), lambda i: (0, 0)), pl.BlockSpec((tr, 512), lambda i: (i, 0))],
        out_specs=[pl.BlockSpec((tr, 512), lambda i: (i, 0)), pl.BlockSpec((tr, 512), lambda i: (i, 0)),
                   pl.BlockSpec((8, 512), lambda i: (0, 0))],
        out_shape=[_sds((T, 512), F32), _sds((T, 512), BF16), _sds((8, 512), F32)],
        compiler_params=_params(("arbitrary",)))(hm, pm, w, dy)


def _adamw(name, w, g, m, v, tr=64):
    R, C = w.shape
    tr = min(tr, R)
    if R % tr:
        tr, block = R, (R, 128)
        spec, grid = pl.BlockSpec(block, lambda i: (0, i)), (C // 128,)
    else:
        spec, grid = pl.BlockSpec((tr, C), lambda i: (i, 0)), (R // tr,)
    c1 = 1.0 - ADAM_B1 ** ADAM_STEP
    c2 = 1.0 - ADAM_B2 ** ADAM_STEP

    def body(w_ref, g_ref, m_ref, v_ref, d_ref, mo_ref, vo_ref):
        g = g_ref[...]
        m = ADAM_B1 * m_ref[...] + (1.0 - ADAM_B1) * g
        v = ADAM_B2 * v_ref[...] + (1.0 - ADAM_B2) * (g * g)
        mo_ref[...] = m
        vo_ref[...] = v
        d_ref[...] = -ADAM_LR * ((m / c1) / (jnp.sqrt(v / c2) + ADAM_EPS) + ADAM_WD * w_ref[...])

    return pl.pallas_call(
        body, name=name, grid=grid, in_specs=[spec] * 4, out_specs=[spec] * 3,
        out_shape=[_sds((R, C), F32)] * 3, compiler_params=_params(("parallel",)))(w, g, m, v)


def _place():
    return lax.axis_index("x"), lax.axis_index("y"), lax.axis_index("c")


def _all_gather8(name, blk, space):
    m, n = blk.shape

    def body(x_ref, out_ref, send_sems, recv_sems, local_sem):
        x, y, c = _place()
        me, sibling = (x, y, c), (x, y, 1 - c)
        chips = [(1 - x, y), (x, 1 - y), (1 - x, 1 - y)]

        def rows(px, py, pc):
            return out_ref.at[pl.ds((4 * px + 2 * py + pc) * m, m), :]

        def copy(k, block, to, src=None):
            return pltpu.make_async_remote_copy(
                src_ref=rows(*block) if src is None else src, dst_ref=rows(*block),
                send_sem=send_sems.at[k], recv_sem=recv_sems.at[k],
                device_id=to, device_id_type=MESH)

        mine = pltpu.make_async_copy(x_ref, rows(*me), local_sem)
        mine.start()
        first = [copy(0, me, sibling, src=x_ref)]
        first += [copy(1 + j, me, (*chip, c), src=x_ref) for j, chip in enumerate(chips)]
        for cp in first:
            cp.start()
        passed = [copy(4 + j, (*chip, c), sibling) for j, chip in enumerate(chips)]
        for j, chip in enumerate(chips):
            copy(1 + j, (*chip, c), me).wait_recv()
            passed[j].start()
        copy(0, sibling, me).wait_recv()
        for j, chip in enumerate(chips):
            copy(4 + j, (*chip, 1 - c), me).wait_recv()
        for cp in first + passed:
            cp.wait_send()
        mine.wait()

    return pl.pallas_call(
        body, name=name, out_shape=_sds((8 * m, n), blk.dtype),
        in_specs=[pl.BlockSpec(memory_space=space)], out_specs=pl.BlockSpec(memory_space=space),
        scratch_shapes=[pltpu.SemaphoreType.DMA((7,)), pltpu.SemaphoreType.DMA((7,)),
                        pltpu.SemaphoreType.DMA],
        compiler_params=pltpu.CompilerParams(vmem_limit_bytes=VMEM_LIMIT))(blk)


def _hbm_specs(n):
    return [pl.BlockSpec(memory_space=pl.ANY)] * n


def _swap_halves_sibling(name, srcs):
    nw = len(srcs)

    def body(*refs):
        src_refs, dst_refs, send_sems, recv_sems = refs[:nw], refs[nw:2 * nw], refs[2 * nw], refs[2 * nw + 1]
        x, y, c = _place()
        cps = [pltpu.make_async_remote_copy(
            src_ref=src_refs[w].at[_whole(src_refs[w]), pl.ds(0, 4), 1 - c], dst_ref=dst_refs[w],
            send_sem=send_sems.at[w], recv_sem=recv_sems.at[w], device_id=(x, y, 1 - c),
            device_id_type=MESH) for w in range(nw)]
        for cp in cps:
            cp.start()
        for cp in cps:
            cp.wait()

    shapes = [s.shape[:2] + s.shape[3:] for s in srcs]
    return pl.pallas_call(
        body, name=name, out_shape=[_sds(sh, s.dtype) for sh, s in zip(shapes, srcs)],
        in_specs=_hbm_specs(nw), out_specs=_hbm_specs(nw),
        scratch_shapes=[pltpu.SemaphoreType.DMA((nw,)), pltpu.SemaphoreType.DMA((nw,))])(*srcs)


def _split_start(name, srcs, lands, copies, per_array):
    nw = len(srcs)

    def body(*refs):
        send_sems, recv_sems, token = refs[2 * nw], refs[2 * nw + 1], refs[-1]
        for w in range(nw):
            for k, (s, d, dev) in enumerate(copies(refs[w], refs[nw + w], *_place())):
                pltpu.make_async_remote_copy(
                    src_ref=s, dst_ref=d, send_sem=send_sems.at[w * per_array + k],
                    recv_sem=recv_sems.at[w * per_array + k], device_id=dev, device_id_type=MESH).start()
        token[...] = jnp.zeros_like(token)

    hbm, sem = pl.BlockSpec(memory_space=pltpu.HBM), pl.BlockSpec(memory_space=pltpu.SEMAPHORE)
    arrays = list(srcs) + list(lands)
    out = pl.pallas_call(
        body, name=name,
        out_shape=(pltpu.SemaphoreType.DMA((nw * per_array,)), pltpu.SemaphoreType.DMA((nw * per_array,)),
                   *[pltpu.HBM(a.shape, a.dtype) for a in arrays], _sds((8, 128), F32)),
        in_specs=[hbm] * (2 * nw),
        out_specs=(sem, sem, *[hbm] * (2 * nw), pl.BlockSpec(memory_space=pltpu.VMEM)),
        input_output_aliases={i: 2 + i for i in range(2 * nw)},
        compiler_params=pltpu.CompilerParams(has_side_effects=pltpu.SideEffectType.DATAFLOW_SIDE_EFFECTING))(
            *[pltpu.with_memory_space_constraint(a, pltpu.HBM) for a in arrays])
    return out[0], out[1], out[2:2 + nw], out[2 + nw:2 + 2 * nw], out[-1]


def _split_wait(name, started, after, waits, per_array):
    send_sems, recv_sems, srcs, lands, _ = started
    nw = len(srcs)

    def body(*refs):
        send_sems, recv_sems = refs[2 * nw], refs[2 * nw + 1]
        x, y, c = _place()
        for w in range(nw):
            for k, (s, d) in enumerate(waits(refs[w], refs[nw + w], x, y, c)):
                cp = pltpu.make_async_remote_copy(
                    src_ref=s, dst_ref=d, send_sem=send_sems.at[w * per_array + k],
                    recv_sem=recv_sems.at[w * per_array + k], device_id=(x, y, 1 - c),
                    device_id_type=MESH)
                cp.wait_send()
                cp.wait_recv()

    hbm, sem = pl.BlockSpec(memory_space=pltpu.HBM), pl.BlockSpec(memory_space=pltpu.SEMAPHORE)
    arrays = list(srcs) + list(lands)
    out = pl.pallas_call(
        body, name=name, out_shape=tuple(pltpu.HBM(a.shape, a.dtype) for a in arrays),
        in_specs=[hbm] * (2 * nw) + [sem, sem, pl.BlockSpec(memory_space=pl.ANY)],
        out_specs=tuple([hbm] * (2 * nw)), input_output_aliases={i: i for i in range(2 * nw)},
        compiler_params=pltpu.CompilerParams(has_side_effects=pltpu.SideEffectType.DATAFLOW_SIDE_EFFECTING))(
            *arrays, send_sems, recv_sems, after)
    return list(out[nw:])


def _other_chips(x, y):
    return [(1 - x, y), (x, 1 - y), (1 - x, 1 - y)]


def _whole(ref):
    return pl.ds(0, ref.shape[0])


def _gather_sends(src_ref, land_ref, x, y, c):
    to = land_ref.at[_whole(land_ref), 2 * x + y, c]
    return [(src_ref, to, (x, y, 1 - c))] + [(src_ref, to, (px, py, c)) for px, py in _other_chips(x, y)]


def _gather_lands(src_ref, land_ref, x, y, c):
    g = _whole(land_ref)
    return [(src_ref, land_ref.at[g, 2 * x + y, 1 - c])] + [
        (src_ref, land_ref.at[g, 2 * px + py, c]) for px, py in _other_chips(x, y)]


def _scatter_sends(src_ref, land_ref, x, y, c):
    g = _whole(src_ref)
    return [(src_ref.at[g, 2 * px + py], land_ref.at[g, 2 * x + y], (px, py, c)) for px, py in _other_chips(x, y)]


def _scatter_lands(src_ref, land_ref, x, y, c):
    g = _whole(src_ref)
    return [(src_ref.at[g, 2 * x + y], land_ref.at[g, 2 * px + py]) for px, py in _other_chips(x, y)]


def _forward_sibling(name, lands):
    nw = len(lands)

    def body(*refs):
        land_refs, out_refs, send_sems, recv_sems = refs[:nw], refs[nw:2 * nw], refs[2 * nw], refs[2 * nw + 1]
        x, y, c = _place()
        cps = []
        for w in range(nw):
            g = _whole(land_refs[w])
            cps += [pltpu.make_async_remote_copy(
                src_ref=land_refs[w].at[g, 2 * px + py, c], dst_ref=out_refs[w].at[g, 2 * px + py, c],
                send_sem=send_sems.at[w, j], recv_sem=recv_sems.at[w, j], device_id=(x, y, 1 - c),
                device_id_type=MESH) for j, (px, py) in enumerate(_other_chips(x, y))]
        for cp in cps:
            cp.start()
        for w in range(nw):
            g = _whole(land_refs[w])
            for j, (px, py) in enumerate(_other_chips(x, y)):
                slot = out_refs[w].at[g, 2 * px + py, 1 - c]
                pltpu.make_async_remote_copy(src_ref=slot, dst_ref=slot, send_sem=send_sems.at[w, j],
                                             recv_sem=recv_sems.at[w, j], device_id=(x, y, 1 - c),
                                             device_id_type=MESH).wait_recv()
        for cp in cps:
            cp.wait_send()

    return pl.pallas_call(
        body, name=name, out_shape=[_sds(a.shape, a.dtype) for a in lands],
        in_specs=_hbm_specs(nw), out_specs=_hbm_specs(nw), input_output_aliases={i: i for i in range(nw)},
        scratch_shapes=[pltpu.SemaphoreType.DMA((nw, 3)), pltpu.SemaphoreType.DMA((nw, 3))])(*lands)


def _share_halves(name, halves):
    nw = len(halves)

    def body(*refs):
        in_refs, out_refs, send_sems, recv_sems = refs[:nw], refs[nw:2 * nw], refs[2 * nw], refs[2 * nw + 1]
        x, y, c = _place()
        cps = [pltpu.make_async_remote_copy(
            src_ref=in_refs[w].at[_whole(in_refs[w]), c], dst_ref=out_refs[w].at[_whole(in_refs[w]), c],
            send_sem=send_sems.at[w], recv_sem=recv_sems.at[w], device_id=(x, y, 1 - c),
            device_id_type=MESH) for w in range(nw)]
        for cp in cps:
            cp.start()
        for w in range(nw):
            slot = out_refs[w].at[_whole(in_refs[w]), 1 - c]
            pltpu.make_async_remote_copy(src_ref=slot, dst_ref=slot, send_sem=send_sems.at[w],
                                         recv_sem=recv_sems.at[w], device_id=(x, y, 1 - c),
                                         device_id_type=MESH).wait_recv()
        for cp in cps:
            cp.wait_send()

    return pl.pallas_call(
        body, name=name, out_shape=[_sds(a.shape, a.dtype) for a in halves],
        in_specs=_hbm_specs(nw), out_specs=_hbm_specs(nw), input_output_aliases={i: i for i in range(nw)},
        scratch_shapes=[pltpu.SemaphoreType.DMA((nw,)), pltpu.SemaphoreType.DMA((nw,))])(*halves)


def _pair_sum(name, full, got, core):
    g, _, _, m, n = full.shape

    def body(c_ref, a_ref, b_ref, o_ref):
        o_ref[...] = (a_ref[...].astype(F32) + b_ref[...].astype(F32)).astype(o_ref.dtype)

    slab = pl.BlockSpec((None, None, m, n), lambda w, s, c: (w, s, 0, 0))
    return pl.pallas_call(
        body, name=name,
        grid_spec=pltpu.PrefetchScalarGridSpec(
            num_scalar_prefetch=1, grid=(g, 4),
            in_specs=[pl.BlockSpec((None, None, None, m, n), lambda w, s, c: (w, s, c[0], 0, 0)), slab],
            out_specs=slab),
        out_shape=_sds(got.shape, BF16),
        compiler_params=_params(("parallel", "parallel")))(core, full, got)


def _sum4(name, a, core):
    g, _, m, n = a.shape

    def body(c_ref, a_ref, o_ref):
        acc = a_ref[0].astype(F32)
        for s in range(1, 4):
            acc = acc + a_ref[s].astype(F32)
        o_ref[...] = acc

    return pl.pallas_call(
        body, name=name,
        grid_spec=pltpu.PrefetchScalarGridSpec(
            num_scalar_prefetch=1, grid=(g,),
            in_specs=[pl.BlockSpec((None, 4, m, n), lambda w, c: (w, 0, 0, 0))],
            out_specs=pl.BlockSpec((None, None, m, n), lambda w, c: (w, c[0], 0, 0))),
        out_shape=_sds((g, 2, m, n), F32), compiler_params=_params(("parallel",)))(core, a)


def _small_update(gathered, w, m, v):
    n = w.shape[1]
    tn = 2048
    c1 = 1.0 - ADAM_B1 ** ADAM_STEP
    c2 = 1.0 - ADAM_B2 ** ADAM_STEP

    def body(g_ref, w_ref, m_ref, v_ref, go_ref, d_ref, mo_ref, vo_ref):
        g = g_ref[0:1, :]
        for d in range(1, 8):
            g = g + g_ref[d:d + 1, :]
        go_ref[...] = g
        m = ADAM_B1 * m_ref[...] + (1.0 - ADAM_B1) * g
        v = ADAM_B2 * v_ref[...] + (1.0 - ADAM_B2) * (g * g)
        mo_ref[...] = m
        vo_ref[...] = v
        d_ref[...] = -ADAM_LR * ((m / c1) / (jnp.sqrt(v / c2) + ADAM_EPS) + ADAM_WD * w_ref[...])

    row = pl.BlockSpec((1, tn), lambda i: (0, i))
    return pl.pallas_call(
        body, name="small_update", grid=(n // tn,),
        in_specs=[pl.BlockSpec((8, tn), lambda i: (0, i)), row, row, row], out_specs=[row] * 4,
        out_shape=[_sds((1, n), F32)] * 4, compiler_params=_params(("parallel",)))(gathered, w, m, v)


def _swiglu(ps, es):
    g, u = ps
    return g * _sigmoid(g) * u, g, u


def _swiglu_bwd(ps, es):
    g, u = es[0].astype(F32), es[1].astype(F32)
    sg = _sigmoid(g)
    return ps[0] * u * (sg * (1.0 + g * (1.0 - sg))), ps[0] * (g * sg)


def _merge(ps, es):
    ga, gm = [e.astype(F32) for e in es]
    return _sigmoid(ga) * ps[0] + _sigmoid(gm) * ps[1], ps[0], ps[1]


def _merge_bwd(ps, es):
    a, b, ga, gm = [e.astype(F32) for e in es]
    sa, sm = _sigmoid(ga), _sigmoid(gm)
    dm = ps[0]
    return dm * sa, dm * sm, dm * a * (sa * (1.0 - sa)), dm * b * (sm * (1.0 - sm))


W_IN_PIECES = (("q", 512), ("kv", 256), ("mqk", 1024), ("mv", 512), ("mo", 512), ("if", 8),
               ("ga", 1024), ("gm", 1024))


def _local_step(x, tgt, pos_col, mod, sp, in_weights, late_weights, ffn_grads, mixer_grads):
    sh_m, sc_m, gate_m, sh_f, sc_f, gate_f = mod
    h = _pre_norm(x, sp["g_pre_mix"], sc_m, sh_m)
    inv = ROPE_THETA ** (-2.0 * jnp.arange(HEAD_DIM // 2, dtype=F32) / HEAD_DIM)
    cos, sin = _rope_tables(pos_col, jnp.tile(inv, 4).reshape(1, 128))
    W = dict(in_weights(h))
    w_a = jnp.concatenate([W["q"], W["kv"]], axis=0)
    w_m = jnp.concatenate([W["mqk"], W["mv"], W["mo"]], axis=0)
    w_g = jnp.concatenate([W["ga"], W["gm"]], axis=0)
    pa, = _mm("proj_attn", [[(h, w_a)]], [], _first, [F32], cn=256, nt=True)
    pm, = _mm("proj_mlstm", [[(h, w_m)]], [], _first, [F32], cn=512, nt=True)
    pif, = _mm("proj_gates", [[(h, W["if"])]], [], _first, [F32], cn=128, nt=True)
    pg, = _mm("proj_branch_gates", [[(h, w_g)]], [], _first, [BF16], cn=512, nt=True)
    ya = _attn_fwd(pa, cos, sin, sp["sinks"])
    qk = _conv_fwd(pm, sp["conv_w"], sp["conv_b"])
    bcol = jnp.pad(sp["b_if"], ((0, 0), (0, 120)))
    brow = jnp.broadcast_to(sp["b_if"].reshape(8, 1), (8, 128))
    grow = pif[:, :8].T
    hm, cs, ns, ms = _mlstm_fwd(qk, pm, pif, bcol, grow, brow)
    ym = _mlstm_out(hm, pm, sp["norm_w"])
    W.update(late_weights(ym))
    w_fg, w_fu, w_fd = W["fg"], W["fu"], W["fd"]
    merged, br_a, br_m = _mm("branches", [[(ya, W["ba"])], [(ym, W["bm"])]],
                             [(pg, 0), (pg, 1)], _merge, [BF16, BF16, BF16], cn=512, nt=True)
    mix, = _mm("mix_out", [[(merged, W["out"])]], [], _first, [F32], cn=512)
    x1, h2 = _res_norm(x, mix, gate_m, sp["g_post_mix"], sp["g_pre_ffn"], sc_f, sh_f)
    act, gt, up = _mm("ffn_in", [[(h2, w_fg)], [(h2, w_fu)]], [], _swiglu, [BF16] * 3,
                      cn=256, nt=True)
    ff, = _mm("ffn_down", [[(act, w_fd)]], [], _first, [F32], cn=512)
    dy, dff, acc_l, loss = _final_loss(x1, ff, tgt, gate_f, sp["g_post_ffn"])

    G = {}
    dgt, dup = _mm("ffn_down_bwd", [[(dff, w_fd)]], [gt, up], _swiglu_bwd, [BF16, BF16],
                   cn=256, nt=True)
    g_fd = _mm_tn("dw_ffn_down", act, dff, BF16, 1408, 512)
    dh2, = _mm("ffn_in_bwd", [[(dgt, w_fg), (dup, w_fu)]], [], _first, [F32], cn=512)
    g_fg = _mm_tn("dw_ffn_gate", dgt, h2, BF16, 1408, 1024)
    g_fu = _mm_tn("dw_ffn_up", dup, h2, BF16, 1408, 1024)
    tie = ffn_grads(g_fg, g_fu, g_fd)
    dx1, dmix, acc_r = _res_norm_bwd(x1, mix, dh2, dy, sc_f + tie, gate_m, sp["g_pre_ffn"],
                                     sp["g_post_mix"])
    d_a, d_m, dga, dgm = _mm("mix_out_bwd", [[(dmix, W["out"])]],
                             [br_a, br_m, (pg, 0), (pg, 1)], _merge_bwd,
                             [BF16] * 4, cn=512, nt=True)
    G["out"] = _mm_tn("dw_out", merged, dmix, BF16, 1024, 512)
    dya, = _mm("branch_attn_bwd", [[(d_a, W["ba"])]], [], _first, [F32], cn=512)
    dym, = _mm("branch_mlstm_bwd", [[(d_m, W["bm"])]], [], _first, [F32], cn=512)
    G["ba"] = _mm_tn("dw_branch_attn", d_a, ya, BF16, 1024, 512)
    G["bm"] = _mm_tn("dw_branch_mlstm", d_m, ym, BF16, 1024, 512)
    dhm, do_m, acc_n = _mlstm_out_bwd(hm, pm, sp["norm_w"], dym)
    dqk, dv_m, dgc, dgr = _mlstm_bwd(qk, pm, pif, bcol, grow, brow, cs, ns, ms, dhm)
    dif, acc_g = _gate_bwd(dgc, jnp.pad(dgr.T, ((0, 0), (0, 120))), pif, bcol)
    dpre, acc_c = _conv_bwd_pre(pm, sp["conv_w"], sp["conv_b"], dqk)
    du = _conv_bwd_in(dpre, sp["conv_w"])
    dq_a, dcur, dprv, dsink = _attn_bwd(pa, cos, sin, sp["sinks"], dya)
    dkv = _attn_kv_combine(dcur, dprv, cos, sin)
    dproj = {"q": dq_a, "kv": dkv, "mqk": du, "mv": dv_m, "mo": do_m, "if": dif, "ga": dga, "gm": dgm}
    for k, _ in W_IN_PIECES:
        G[k] = _mm_tn("dw_in_" + k, dproj[k], h, BF16, dproj[k].shape[1], 1024)
    w_tied = dict(W, **{"if": W["if"] + mixer_grads(G).astype(BF16)})
    dh, = _mm("proj_bwd", [[(dproj[k], w_tied[k]) for k, _ in W_IN_PIECES]], [], _first, [F32], cn=512)
    dx, acc_p = _pre_norm_bwd(x, dh, dx1, sp["g_pre_mix"], sc_m)

    small = {
        "mod": jnp.concatenate([acc_p[1], acc_p[0], acc_r[3], acc_r[1], acc_r[0], acc_l[0]]),
        "g_pre_mix": acc_p[2], "g_post_mix": acc_r[4], "b_if": acc_g[0, :8],
        "conv_w": acc_c[:CONV_WIDTH].reshape(-1), "conv_b": acc_c[CONV_WIDTH],
        "sinks": dsink[:, 0], "norm_w": acc_n[0], "g_pre_ffn": acc_r[2], "g_post_ffn": acc_l[1]}
    return loss, dx, small


IN_WIDTH = sum(n for _, n in W_IN_PIECES)
IN_SHARD = IN_WIDTH // 4
IN_SHARD_PAD = -(-IN_SHARD // 32) * 32


def _split_w_in(w_in_t):
    out, off = {}, 0
    for k, n in W_IN_PIECES:
        out[k] = w_in_t[off:off + n]
        off += n
    out["if"] = jnp.pad(out["if"], ((0, 120), (0, 0)))
    return out


def _halves(a):
    return a.reshape(4, 2, a.shape[0] // 8, a.shape[1])


SMALL = (("b_ada", 6144), ("g_pre_mix", 1024), ("g_post_mix", 1024), ("b_if", 128), ("conv_w", 4096),
         ("conv_b", 1024), ("sinks", 128), ("norm_w", 512), ("g_pre_ffn", 1024), ("g_post_ffn", 1024))
SMALL_LEN = 8 * 2048


def _pack_small(vals):
    parts = []
    for k, n in SMALL:
        v = vals[k].reshape(-1)
        parts.append(jnp.pad(v, (0, n - v.shape[0])))
    flat = jnp.concatenate(parts)
    return jnp.pad(flat, (0, SMALL_LEN - flat.shape[0]))


def _unpack_small(flat, shapes):
    out, off = {}, 0
    for k, n in SMALL:
        size = 1
        for d in shapes[k]:
            size *= d
        out[k] = flat[off:off + size].reshape(shapes[k])
        off += n
    return out


def kernel(x, c, positions, w_ada, b_ada, g_pre_mix, g_post_mix, w_in, b_if, conv_w, conv_b, attn_sinks, mlstm_norm_w, w_branch_attn, w_branch_mlstm, w_out, g_pre_ffn, g_post_ffn, w_ffn_gate, w_ffn_up, w_ffn_down, loss_target, m_w_ada, m_b_ada, m_g_pre_mix, m_g_post_mix, m_w_in, m_b_if, m_conv_w, m_conv_b, m_attn_sinks, m_mlstm_norm_w, m_w_branch_attn, m_w_branch_mlstm, m_w_out, m_g_pre_ffn, m_g_post_ffn, m_w_ffn_gate, m_w_ffn_up, m_w_ffn_down, v_w_ada, v_b_ada, v_g_pre_mix, v_g_post_mix, v_w_in, v_b_if, v_conv_w, v_conv_b, v_attn_sinks, v_mlstm_norm_w, v_w_branch_attn, v_w_branch_mlstm, v_w_out, v_g_pre_ffn, v_g_post_ffn, v_w_ffn_gate, v_w_ffn_up, v_w_ffn_down):
    xi, yi, ci = _place()
    chip = 2 * xi + yi
    dev = 2 * chip + ci
    T = x.shape[1]
    ada_cols = w_ada.shape[2]

    def my_half(a):
        n = a.shape[0] // 2
        return lax.dynamic_slice_in_dim(a, ci * n, n, axis=0).astype(BF16)

    blk = jnp.concatenate([c.reshape(-1), conv_w.reshape(-1)]).reshape(8, 256)
    got = _all_gather8("gather_cond", blk, pltpu.VMEM).reshape(8, 2048)
    c_all = got[:, :D_MODEL].astype(BF16)
    conv_full = got[::2, D_MODEL:].reshape(4, CONV_WIDTH, -1).transpose(1, 0, 2).reshape(CONV_WIDTH, -1)

    b_sh = lax.dynamic_slice_in_dim(b_ada, chip * ada_cols, ada_cols, axis=1)
    mod_part, = _mm("ada_mod", [[(c_all, w_ada[0].astype(BF16))]], [b_sh],
                    lambda ps, es: (ps[0] + es[0],), [F32], cn=512, tm=8)
    mod_all = _all_gather8("gather_mod", mod_part, pltpu.VMEM).reshape(4, 2, 8, ada_cols)[:, 0]
    mod = lax.dynamic_index_in_dim(mod_all, dev, axis=1, keepdims=False).reshape(6, 1, D_MODEL)

    def gather_start(name, blks, after):
        blks, _ = lax.optimization_barrier((blks, after))
        lands = [lax.dynamic_update_slice(lax.empty((b.shape[0], 4, 2) + b.shape[1:], BF16),
                                          b[:, None, None], (0, chip, ci, 0, 0)) for b in blks]
        return _split_start(name + "_start", blks, lands, _gather_sends, 4)

    def gather_wait(name, started, after):
        return _forward_sibling(name + "_forward", _split_wait(name + "_wait", started, after, _gather_lands, 4))

    w_in_t = jnp.pad(w_in[0].T, ((0, IN_SHARD_PAD - IN_SHARD), (0, 0)))
    in_started = gather_start("in_gather", [my_half(w_in_t)[None]], mod)
    late_started = gather_start(
        "late_gather",
        [jnp.stack([my_half(w_ffn_gate[0].T), my_half(w_ffn_up[0].T), my_half(w_ffn_down[0])]),
         my_half(w_out[0])[None], jnp.stack([my_half(w_branch_attn[0].T), my_half(w_branch_mlstm[0].T)])],
        in_started[4])
    mod = mod + (in_started[4][0, 0] + late_started[4][0, 0])

    def in_weights(after):
        g_in, = gather_wait("in_gather", in_started, after)
        return _split_w_in(g_in.reshape(4, IN_SHARD_PAD, D_MODEL)[:, :IN_SHARD].reshape(IN_WIDTH, D_MODEL))

    def late_weights(after):
        g_ffn, g_out, g_br = gather_wait("late_gather", late_started, after)
        return {"fg": g_ffn[0].reshape(D_FF, D_MODEL), "fu": g_ffn[1].reshape(D_FF, D_MODEL),
                "fd": g_ffn[2].reshape(D_FF, D_MODEL), "out": g_out.reshape(D_MODEL, D_MODEL),
                "ba": g_br[0].reshape(D_MODEL, -1), "bm": g_br[1].reshape(D_MODEL, -1)}

    core = ci.reshape(1).astype(jnp.int32)
    sent = {}

    def scatter_start(name, groups):
        theirs = _swap_halves_sibling(name + "_pair", groups)
        pairs = [_pair_sum("%s_pair_sum_%d" % (name, i), a, b, core)
                 for i, (a, b) in enumerate(zip(groups, theirs))]
        sent[name] = _split_start(name + "_start", pairs, [p + jnp.zeros((), BF16) for p in pairs],
                                  _scatter_sends, 3)
        return sent[name][4][0, 0]

    def ffn_grads(g_fg, g_fu, g_fd):
        return scatter_start("rs_ffn", [jnp.stack([_halves(g_fg), _halves(g_fu), _halves(g_fd)])])

    def mixer_grads(G):
        g_in_t = jnp.concatenate([G[k][:n] for k, n in W_IN_PIECES]).reshape(4, IN_SHARD, D_MODEL)
        g_in_t = jnp.pad(g_in_t, ((0, 0), (0, IN_SHARD_PAD - IN_SHARD), (0, 0)))
        return scatter_start("rs_mix", [g_in_t.reshape(1, 4, 2, IN_SHARD_PAD // 2, D_MODEL),
                                        _halves(G["out"])[None],
                                        jnp.stack([_halves(G["ba"]), _halves(G["bm"])])])

    sp = {"g_pre_mix": g_pre_mix, "g_post_mix": g_post_mix, "b_if": b_if, "conv_w": conv_full,
          "conv_b": conv_b, "sinks": attn_sinks, "norm_w": mlstm_norm_w, "g_pre_ffn": g_pre_ffn,
          "g_post_ffn": g_post_ffn}
    loss, dx, small = _local_step(x[0], loss_target[0], positions.reshape(T, 1), [mod[i] for i in range(6)],
                                  sp, in_weights, late_weights, ffn_grads, mixer_grads)

    landed = (_split_wait("rs_ffn_wait", sent["rs_ffn"], dx, _scatter_lands, 3)
              + _split_wait("rs_mix_wait", sent["rs_mix"], dx, _scatter_lands, 3))
    reds = [_sum4("rs_chip_sum_%d" % i, a, core) for i, a in enumerate(landed)]
    s_ffn, s_in, s_out, s_br = [s.reshape(s.shape[0], -1, s.shape[-1]) for s in _share_halves("rs_share", reds)]
    gsh = {"fg": s_ffn[0], "fu": s_ffn[1], "fd": s_ffn[2], "w_in": s_in[0, :IN_SHARD],
           "out": s_out[0], "ba": s_br[0], "bm": s_br[1]}

    small["b_ada"] = small.pop("mod")
    vec = _pack_small(small).reshape(8, 2048)
    g_all = _all_gather8("gather_small", vec, pltpu.VMEM).reshape(8, SMALL_LEN)
    dmod_sh = lax.dynamic_slice_in_dim(g_all[:, :6 * D_MODEL], chip * ada_cols, ada_cols, axis=1)
    g_w_ada = _mm_tn("dw_ada", c_all, dmod_sh.astype(BF16), F32, D_MODEL, 512, 8)

    smalls = {"b_ada": (b_ada, m_b_ada, v_b_ada), "g_pre_mix": (g_pre_mix, m_g_pre_mix, v_g_pre_mix),
              "g_post_mix": (g_post_mix, m_g_post_mix, v_g_post_mix), "b_if": (b_if, m_b_if, v_b_if),
              "conv_w": None, "conv_b": (conv_b, m_conv_b, v_conv_b),
              "sinks": (attn_sinks, m_attn_sinks, v_attn_sinks),
              "norm_w": (mlstm_norm_w, m_mlstm_norm_w, v_mlstm_norm_w),
              "g_pre_ffn": (g_pre_ffn, m_g_pre_ffn, v_g_pre_ffn),
              "g_post_ffn": (g_post_ffn, m_g_post_ffn, v_g_post_ffn)}
    shapes = {k: (t[0].shape if t is not None else (1, CONV_WIDTH, D_MODEL)) for k, t in smalls.items()}
    zeros = jnp.zeros((CONV_WIDTH * D_MODEL,), F32)
    packs = [_pack_small({k: (t[i] if t is not None else zeros) for k, t in smalls.items()}).reshape(1, -1)
             for i in range(3)]
    s_out = [_unpack_small(o[0], shapes) for o in _small_update(g_all, *packs)]
    g_conv = lax.dynamic_slice_in_dim(s_out[0]["conv_w"], chip * conv_w.shape[2], conv_w.shape[2], axis=2)

    res = {}
    for k, t in smalls.items():
        if t is not None:
            res[k] = tuple(o[k] for o in s_out)
    res["conv_w"] = (g_conv, *[o[None] for o in _adamw("adam_conv_w", conv_w[0], g_conv[0], m_conv_w[0], v_conv_w[0])])
    res["w_ada"] = (g_w_ada[None], *[o[None] for o in _adamw("adam_w_ada", w_ada[0], g_w_ada, m_w_ada[0], v_w_ada[0])])
    bigs = {"w_in": (w_in, m_w_in, v_w_in), "ba": (w_branch_attn, m_w_branch_attn, v_w_branch_attn),
            "bm": (w_branch_mlstm, m_w_branch_mlstm, v_w_branch_mlstm), "out": (w_out, m_w_out, v_w_out),
            "fg": (w_ffn_gate, m_w_ffn_gate, v_w_ffn_gate), "fu": (w_ffn_up, m_w_ffn_up, v_w_ffn_up),
            "fd": (w_ffn_down, m_w_ffn_down, v_w_ffn_down)}
    for k, (w, m, v) in bigs.items():
        if k in ("w_in", "fg", "fu"):
            res[k] = tuple(o.T[None] for o in (gsh[k], *_adamw("adam_" + k, w[0].T, gsh[k], m[0].T, v[0].T)))
        else:
            g = gsh[k].T if k in ("ba", "bm") else gsh[k]
            res[k] = (g[None], *[o[None] for o in _adamw("adam_" + k, w[0], g, m[0], v[0])])

    order = ("w_ada", "b_ada", "g_pre_mix", "g_post_mix", "w_in", "b_if", "conv_w", "conv_b", "sinks",
             "norm_w", "ba", "bm", "out", "g_pre_ffn", "g_post_ffn", "fg", "fu", "fd")
    total = lax.psum(loss[0, 0], ("x", "y", "c"))
    return (total, dx[None], *[res[k][0] for k in order], *[res[k][1] for k in order],
            *[res[k][2] for k in order], *[res[k][3] for k in order])
```

```python
import functools

import jax
import jax.numpy as jnp
from jax import lax
from jax.experimental import pallas as pl
from jax.experimental.pallas import tpu as pltpu

F32, BF16 = jnp.float32, jnp.bfloat16
MESH = pl.DeviceIdType.MESH

D_MODEL = 1024
N_Q_HEADS, N_KV_HEADS, HEAD_DIM, WINDOW = 8, 2, 64, 128
ROPE_THETA = 10000.0
MLSTM_HEADS, MLSTM_HEAD_DIM, MLSTM_CHUNK, CONV_WIDTH = 4, 128, 64, 4
D_FF = 2816
NORM_EPS = 1e-6
ADAM_LR, ADAM_B1, ADAM_B2, ADAM_EPS, ADAM_WD, ADAM_STEP = 0.001, 0.9, 0.999, 1e-08, 0.01, 10

VMEM_LIMIT = 56 * 1024 * 1024
ROW_TILE = 256
MM_TM = 512
MM_TT = 1024
ATTN_BLK = WINDOW
STEP_ROWS = 2 * MLSTM_CHUNK
NEG_INF = float("-inf")


def _params(sem):
    return pltpu.CompilerParams(dimension_semantics=sem, vmem_limit_bytes=VMEM_LIMIT)


def _sds(shape, dtype):
    return jax.ShapeDtypeStruct(shape, dtype)


def _sigmoid(x):
    return 1.0 / (1.0 + jnp.exp(-x))


def _dot(a, b, ca, cb):
    return lax.dot_general(a, b, (((ca,), (cb,)), ((), ())), preferred_element_type=F32)


def _bdot(a, b, ca, cb):
    return lax.dot_general(a, b, (((ca,), (cb,)), ((0,), (0,))), preferred_element_type=F32)


def _bdot_rows(a, b):
    return jnp.stack([_dot(a[h], b[h], 0, 0) for h in range(a.shape[0])])


def _mm(name, prods, extras, epi, out_dtypes, cn, nt=False, tm=MM_TM):
    flat = [ab for p in prods for ab in p]
    counts = [len(p) for p in prods]
    M = flat[0][0].shape[0]
    N = flat[0][1].shape[0 if nt else 1]
    tm = min(tm, M)
    n_in = 2 * len(flat) + len(extras)

    def body(*refs):
        ins, outs = refs[:n_in], refs[n_in:]
        for j in range(N // cn):
            cols = slice(j * cn, (j + 1) * cn)
            k, ps = 0, []
            for cnt in counts:
                acc = None
                for _ in range(cnt):
                    b = ins[k + 1][cols, :] if nt else ins[k + 1][:, cols]
                    d = _dot(ins[k][...], b, 1, 1 if nt else 0)
                    acc = d if acc is None else acc + d
                    k += 2
                ps.append(acc)
            res = epi(ps, [r[:, cols] for r in ins[k:]])
            for o, r in zip(outs, res):
                o[:, cols] = r.astype(o.dtype)

    in_specs, args = [], []
    for a, b in flat:
        in_specs.append(pl.BlockSpec((tm, a.shape[1]), lambda i: (i, 0)))
        in_specs.append(pl.BlockSpec(b.shape, lambda i: (0, 0), pipeline_mode=pl.Buffered(1)))
        args += [a, b]
    for e in extras:
        e, off = e if isinstance(e, tuple) else (e, 0)
        rows = 1 if e.shape[0] == 1 else tm
        in_specs.append(pl.BlockSpec((rows, N), lambda i, off=off, rows=rows: (0 if rows == 1 else i, off)))
        args.append(e)
    return pl.pallas_call(
        body, name=name, grid=(M // tm,), in_specs=in_specs,
        out_specs=[pl.BlockSpec((tm, N), lambda i: (i, 0)) for _ in out_dtypes],
        out_shape=[_sds((M, N), dt) for dt in out_dtypes],
        compiler_params=_params(("parallel",)))(*args)


def _mm_rows(name, prods, extras, epi, outs, accs, cn, nt=False, tm=MM_TM):
    flat = [ab for p in prods for ab in p]
    counts = [len(p) for p in prods]
    M = flat[0][0].shape[0]
    N = flat[0][1].shape[0 if nt else 1]
    tm = min(tm, M)
    n_mm, n_in, n_out = 2 * len(flat), 2 * len(flat) + len(extras), len(outs)

    def body(*refs):
        ins, out_refs, acc_refs = refs[:n_in], refs[n_in:n_in + n_out], refs[n_in + n_out:]

        @pl.when(pl.program_id(0) == 0)
        def _():
            for a in acc_refs:
                a[...] = jnp.zeros_like(a)

        chunks = [[] for _ in counts]
        for j in range(N // cn):
            cols = slice(j * cn, (j + 1) * cn)
            k = 0
            for p, cnt in enumerate(counts):
                acc = None
                for _ in range(cnt):
                    b = ins[k + 1][cols, :] if nt else ins[k + 1][:, cols]
                    d = _dot(ins[k][...], b, 1, 1 if nt else 0)
                    acc = d if acc is None else acc + d
                    k += 2
                chunks[p].append(acc)
        ps = [c[0] if len(c) == 1 else jnp.concatenate(c, axis=1) for c in chunks]
        res, incs = epi(ps, [r[...] for r in ins[n_mm:]])
        for o, r in zip(out_refs, res):
            o[...] = r.astype(o.dtype)
        for a, inc in zip(acc_refs, incs):
            a[...] += inc

    in_specs, args = [], []
    for a, b in flat:
        in_specs.append(pl.BlockSpec((tm, a.shape[1]), lambda i: (i, 0)))
        in_specs.append(pl.BlockSpec(b.shape, lambda i: (0, 0), pipeline_mode=pl.Buffered(1)))
        args += [a, b]
    for e in extras:
        rows = 1 if e.shape[0] == 1 else tm
        in_specs.append(pl.BlockSpec((rows, e.shape[1]), lambda i, rows=rows: (0 if rows == 1 else i, 0)))
        args.append(e)
    return pl.pallas_call(
        body, name=name, grid=(M // tm,), in_specs=in_specs,
        out_specs=[pl.BlockSpec((tm, w), lambda i: (i, 0)) for w, _ in outs]
        + [pl.BlockSpec(s, lambda i: (0, 0)) for s in accs],
        out_shape=[_sds((M, w), dt) for w, dt in outs] + [_sds(s, F32) for s in accs],
        compiler_params=_params(("arbitrary",)))(*args)


def _mm_tn(name, a, b, out_dtype, tk, tn, tt=MM_TT):
    T, Ka = a.shape
    N = b.shape[1]
    tt = min(tt, T)
    steps = T // tt

    def body(a_ref, b_ref, o_ref, acc):
        t = pl.program_id(2)

        @pl.when(t == 0)
        def _():
            acc[...] = jnp.zeros_like(acc)

        acc[...] += _dot(a_ref[...], b_ref[...], 0, 0)

        @pl.when(t == steps - 1)
        def _():
            o_ref[...] = acc[...].astype(o_ref.dtype)

    return pl.pallas_call(
        body, name=name, grid=(Ka // tk, N // tn, steps),
        in_specs=[pl.BlockSpec((tt, tk), lambda i, j, t: (t, i)),
                  pl.BlockSpec((tt, tn), lambda i, j, t: (t, j))],
        out_specs=pl.BlockSpec((tk, tn), lambda i, j, t: (i, j)),
        out_shape=_sds((Ka, N), out_dtype),
        scratch_shapes=[pltpu.VMEM((tk, tn), F32)],
        compiler_params=_params(("parallel", "parallel", "arbitrary")))(a, b)


def _first(ps, es):
    return (ps[0],)


def _rows(name, body, ins, out_shapes, T, tr=ROW_TILE):
    tr = min(tr, T)

    def spec(shape):
        if shape[0] == T:
            return pl.BlockSpec((tr,) + tuple(shape[1:]), lambda i: (i,) + (0,) * (len(shape) - 1))
        return pl.BlockSpec(tuple(shape), lambda i: (0,) * len(shape))

    return pl.pallas_call(
        body, name=name, grid=(T // tr,),
        in_specs=[spec(a.shape) for a in ins], out_specs=[spec(s.shape) for s in out_shapes],
        out_shape=out_shapes, compiler_params=_params(("arbitrary",)))(*ins)


def _rms(x):
    r = lax.rsqrt(jnp.mean(x * x, axis=-1, keepdims=True) + NORM_EPS)
    return x * r, r


def _rms_bwd(dxn, xn, r):
    return r * (dxn - xn * jnp.mean(dxn * xn, axis=-1, keepdims=True))


def _colsum(v):
    return jnp.sum(v, axis=0, keepdims=True)


def _pre_norm(x, g, sc, sh):
    T = x.shape[0]

    def body(x_ref, g_ref, sc_ref, sh_ref, h_ref):
        xn, _ = _rms(x_ref[...])
        h_ref[...] = (xn * g_ref[...] * (1.0 + sc_ref[...]) + sh_ref[...]).astype(BF16)

    return _rows("pre_norm", body, [x, g, sc, sh], [_sds((T, D_MODEL), BF16)], T)[0]


def _acc_rows(rows):
    w = rows[0].shape[1]
    return jnp.concatenate(rows + [jnp.zeros((8 - len(rows), w), F32)], axis=0)


def _res_norm_rows(ps, es):
    mix = ps[0]
    x, gate, gp, g2, sc, sh = es
    mh, _ = _rms(mix)
    x1 = x + gate * (mh * gp)
    xn, _ = _rms(x1)
    return [mix, x1, xn * g2 * (1.0 + sc) + sh], []


def _final_loss_rows(ps, es):
    x1, tgt, gate, gp = es
    fh, r = _rms(ps[0])
    e = x1 + gate * (fh * gp) - tgt
    loss = 0.5 * jnp.sum(jnp.mean(e * e, axis=-1, keepdims=True))
    dy = e * (1.0 / D_MODEL)
    acc = _acc_rows([_colsum(dy * fh * gp), _colsum(dy * gate * fh)])
    return [dy, _rms_bwd(dy * gate * gp, fh, r)], [acc, jnp.full((1, 128), loss, F32)]


def _res_norm_bwd_rows(ps, es):
    dh = ps[0]
    x1, mix, dy, sc, gate, g2, gp = es
    xn, r1 = _rms(x1)
    rows = [_colsum(dh * xn * g2), _colsum(dh), _colsum(dh * (1.0 + sc) * xn)]
    dx1 = dy + _rms_bwd(dh * (1.0 + sc) * g2, xn, r1)
    mh, rm = _rms(mix)
    rows += [_colsum(dx1 * mh * gp), _colsum(dx1 * gate * mh)]
    return [dx1, _rms_bwd(dx1 * gate * gp, mh, rm)], [_acc_rows(rows)]


def _pre_norm_bwd_rows(ps, es):
    dh = ps[0]
    x, dx1, g, sc = es
    xn, r = _rms(x)
    rows = [_colsum(dh * xn * g), _colsum(dh), _colsum(dh * (1.0 + sc) * xn)]
    return [dx1 + _rms_bwd(dh * (1.0 + sc) * g, xn, r)], [_acc_rows(rows)]


def _rope_tables(pos_col, inv_freq):
    T = pos_col.shape[0]

    def body(p_ref, f_ref, c_ref, s_ref):
        ang = p_ref[...].astype(F32) * f_ref[...]
        lane = lax.broadcasted_iota(jnp.int32, ang.shape, 1)
        c_ref[...] = jnp.cos(ang)
        s_ref[...] = jnp.where(lane % HEAD_DIM < HEAD_DIM // 2, -1.0, 1.0) * jnp.sin(ang)

    return _rows("rope_tables", body, [pos_col, inv_freq],
                 [_sds((T, 128), F32), _sds((T, 128), F32)], T, tr=512)


def _swap_halves(t):
    W = t.shape[1]
    lane = lax.broadcasted_iota(jnp.int32, t.shape, 1)
    half = HEAD_DIM // 2
    return jnp.where(lane % HEAD_DIM < half, pltpu.roll(t, W - half, 1), pltpu.roll(t, half, 1))


def _widen(c, W):
    return c if W == 128 else jnp.concatenate([c] * (W // 128), axis=1)


def _rope(t, c, s):
    W = t.shape[1]
    return t * _widen(c, W) + _swap_halves(t) * _widen(s, W)


def _unrope(dy, c, s):
    W = dy.shape[1]
    return dy * _widen(c, W) + _swap_halves(dy * _widen(s, W))


def _attn_mask(n):
    qi = lax.broadcasted_iota(jnp.int32, (ATTN_BLK, 2 * ATTN_BLK), 0)
    kj = lax.broadcasted_iota(jnp.int32, (ATTN_BLK, 2 * ATTN_BLK), 1)
    rel = kj - ATTN_BLK
    return (rel <= qi) & (qi - rel < WINDOW) & ((n > 0) | (kj >= ATTN_BLK))


def _attn_load(cur, prv, cc, sc, cp, sp):
    x, xp = cur[...], prv[...]
    q = _rope(x[:, :512], cc[...], sc[...]) * (HEAD_DIM ** -0.5)
    k = jnp.concatenate([_rope(xp[:, 512:640], cp[...], sp[...]),
                         _rope(x[:, 512:640], cc[...], sc[...])], axis=0)
    v = jnp.concatenate([xp[:, 640:768], x[:, 640:768]], axis=0)
    return q, k, v


ROLLED = tuple(h for h in range(N_Q_HEADS) if h % 2 != h // (N_Q_HEADS // N_KV_HEADS))


def _pair_heads(t):
    half = lax.broadcasted_iota(jnp.int32, (ATTN_BLK, 128), 1) // HEAD_DIM
    return jnp.stack([jnp.where(half == h % 2, t[:, 128 * (h // 2):128 * (h // 2) + 128], 0.0)
                      for h in range(N_Q_HEADS)])


def _kv_heads(t):
    half = lax.broadcasted_iota(jnp.int32, t.shape, 1) // HEAD_DIM
    tr = pltpu.roll(t, HEAD_DIM, 1)
    return jnp.stack([jnp.where(half == h % 2, tr if h in ROLLED else t, 0.0)
                      for h in range(N_Q_HEADS)])


def _sink_column(snk):
    return jnp.stack([jnp.full((1, 1), snk[0, h], F32) for h in range(N_Q_HEADS)])


def _attn_probs(qh, kh, mask, sink):
    s = jnp.where(mask, _bdot(qh, kh, 2, 2), NEG_INF)
    m = jnp.maximum(jnp.max(s, axis=-1, keepdims=True), sink)
    p = jnp.exp(s - m)
    es = jnp.exp(sink - m)
    rl = 1.0 / (jnp.sum(p, axis=-1, keepdims=True) + es)
    return p, es, rl


def _attn_specs(nb):
    blk = lambda w: pl.BlockSpec((ATTN_BLK, w), lambda n: (n, 0))
    prv = lambda w: pl.BlockSpec((ATTN_BLK, w), lambda n: (jnp.maximum(n - 1, 0), 0))
    return [blk(768), prv(768), blk(128), blk(128), prv(128), prv(128),
            pl.BlockSpec(memory_space=pltpu.SMEM)]


def _attn_fwd(pa, cos, sin, sinks):
    T = pa.shape[0]
    nb = T // ATTN_BLK

    def body(cur, prv, cc, sc, cp, sp, snk, y_ref):
        n = pl.program_id(0)
        q, k, v = _attn_load(cur, prv, cc, sc, cp, sp)
        qh, kh, vh = _pair_heads(q).astype(BF16), _kv_heads(k).astype(BF16), _kv_heads(v).astype(BF16)
        p, _, rl = _attn_probs(qh, kh, _attn_mask(n), _sink_column(snk))
        o = _bdot(p.astype(BF16), vh, 2, 1) * rl
        for pair in range(N_Q_HEADS // 2):
            y_ref[:, 128 * pair:128 * pair + 128] = (o[2 * pair] + o[2 * pair + 1]).astype(BF16)

    return pl.pallas_call(
        body, name="attn_fwd", grid=(nb,), in_specs=_attn_specs(nb),
        out_specs=pl.BlockSpec((ATTN_BLK, 512), lambda n: (n, 0)),
        out_shape=_sds((T, 512), BF16), compiler_params=_params(("parallel",)))(
            pa, pa, cos, sin, cos, sin, sinks)


def _attn_bwd(pa, cos, sin, sinks, dy):
    T = pa.shape[0]
    nb = T // ATTN_BLK

    def body(cur, prv, cc, sc, cp, sp, snk, dy_ref, dq_ref, dcur_ref, dprv_ref, dsink_ref):
        n = pl.program_id(0)

        @pl.when(n == 0)
        def _():
            dsink_ref[...] = jnp.zeros_like(dsink_ref)

        q, k, v = _attn_load(cur, prv, cc, sc, cp, sp)
        qh, kh, vh = _pair_heads(q).astype(BF16), _kv_heads(k).astype(BF16), _kv_heads(v).astype(BF16)
        p, es, rl = _attn_probs(qh, kh, _attn_mask(n), _sink_column(snk))
        pn = p * rl
        do = _pair_heads(dy_ref[...]).astype(BF16)
        dp = _bdot(do, vh, 2, 2)
        delta = jnp.sum(pn * dp, axis=-1, keepdims=True)
        ds = (pn * (dp - delta)).astype(BF16)
        dsink = es * rl * delta
        dq = _bdot(ds, kh, 2, 1) * (HEAD_DIM ** -0.5)
        dkh = _bdot_rows(ds, qh)
        dvh = _bdot_rows(pn.astype(BF16), do)

        def fold(t):
            same = [t[h] for h in range(N_Q_HEADS) if h not in ROLLED]
            moved = [t[h] for h in ROLLED]
            return sum(same[1:], same[0]) + pltpu.roll(sum(moved[1:], moved[0]), HEAD_DIM, 1)

        dk, dv = fold(dkh), fold(dvh)
        for h in range(N_Q_HEADS):
            dsink_ref[h:h + 1, :] += -jnp.sum(dsink[h])
        for pair in range(N_Q_HEADS // 2):
            dq_ref[:, 128 * pair:128 * pair + 128] = _unrope(
                dq[2 * pair] + dq[2 * pair + 1], cc[...], sc[...]).astype(BF16)
        dcur_ref[:, 0:128] = dk[ATTN_BLK:]
        dcur_ref[:, 128:256] = dv[ATTN_BLK:]
        dprv_ref[:, 0:128] = dk[:ATTN_BLK]
        dprv_ref[:, 128:256] = dv[:ATTN_BLK]

    blk = lambda w: pl.BlockSpec((ATTN_BLK, w), lambda n: (n, 0))
    return pl.pallas_call(
        body, name="attn_bwd", grid=(nb,), in_specs=_attn_specs(nb) + [blk(512)],
        out_specs=[blk(512), blk(256), blk(256), pl.BlockSpec((8, 128), lambda n: (0, 0))],
        out_shape=[_sds((T, 512), BF16), _sds((T, 256), F32), _sds((T, 256), F32),
                   _sds((8, 128), F32)],
        compiler_params=_params(("arbitrary",)))(pa, pa, cos, sin, cos, sin, sinks, dy)


def _attn_kv_combine(dcur, dprv, cos, sin):
    T = dcur.shape[0]
    nb = T // ATTN_BLK

    def body(c_ref, p_ref, cc, sc, o_ref):
        n = pl.program_id(0)
        t = c_ref[...] + jnp.where(n < nb - 1, p_ref[...], 0.0)
        o_ref[:, 0:128] = _unrope(t[:, 0:128], cc[...], sc[...]).astype(BF16)
        o_ref[:, 128:256] = t[:, 128:256].astype(BF16)

    blk = lambda w: pl.BlockSpec((ATTN_BLK, w), lambda n: (n, 0))
    nxt = pl.BlockSpec((ATTN_BLK, 256), lambda n: (jnp.minimum(n + 1, nb - 1), 0))
    return pl.pallas_call(
        body, name="attn_kv_combine", grid=(nb,), in_specs=[blk(256), nxt, blk(128), blk(128)],
        out_specs=blk(256), out_shape=_sds((T, 256), BF16),
        compiler_params=_params(("parallel",)))(dcur, dprv, cos, sin)


CONV_COLS = 2 * MLSTM_HEADS * MLSTM_HEAD_DIM


def _conv_pre(cur_ref, halo_ref, w_ref, b_ref, i, tr):
    xx = jnp.concatenate([jnp.where(i > 0, halo_ref[...], 0.0), cur_ref[...]], axis=0)
    taps = [(pltpu.roll(xx, CONV_WIDTH - 1 - j, 0) if j < CONV_WIDTH - 1 else xx)[8:8 + tr]
            for j in range(CONV_WIDTH)]
    pre = b_ref[...]
    for j in range(CONV_WIDTH):
        pre = pre + taps[j] * w_ref[j:j + 1, :]
    return pre, taps


def _conv_specs(T, tr):
    return [pl.BlockSpec((tr, CONV_COLS), lambda i: (i, 0)),
            pl.BlockSpec((8, CONV_COLS), lambda i: (jnp.maximum(i * (tr // 8) - 1, 0), 0)),
            pl.BlockSpec((CONV_WIDTH, CONV_COLS), lambda i: (0, 0)),
            pl.BlockSpec((1, CONV_COLS), lambda i: (0, 0))]


def _conv_fwd(pm, w, b):
    T = pm.shape[0]
    tr = min(ROW_TILE, T)

    def body(cur_ref, halo_ref, w_ref, b_ref, o_ref):
        pre, _ = _conv_pre(cur_ref, halo_ref, w_ref, b_ref, pl.program_id(0), tr)
        o_ref[...] = pre * _sigmoid(pre)

    return pl.pallas_call(
        body, name="conv_fwd", grid=(T // tr,), in_specs=_conv_specs(T, tr),
        out_specs=pl.BlockSpec((tr, CONV_COLS), lambda i: (i, 0)),
        out_shape=_sds((T, CONV_COLS), F32), compiler_params=_params(("parallel",)))(pm, pm, w, b)


def _conv_bwd_pre(pm, w, b, dqk):
    T = pm.shape[0]
    tr = min(ROW_TILE, T)

    def body(cur_ref, halo_ref, w_ref, b_ref, d_ref, dpre_ref, acc_ref):
        i = pl.program_id(0)

        @pl.when(i == 0)
        def _():
            acc_ref[...] = jnp.zeros_like(acc_ref)

        pre, taps = _conv_pre(cur_ref, halo_ref, w_ref, b_ref, i, tr)
        sg = _sigmoid(pre)
        dpre = d_ref[...] * (sg * (1.0 + pre * (1.0 - sg)))
        dpre_ref[...] = dpre
        for j in range(CONV_WIDTH):
            acc_ref[j:j + 1, :] += _colsum(dpre * taps[j])
        acc_ref[CONV_WIDTH:CONV_WIDTH + 1, :] += _colsum(dpre)

    return pl.pallas_call(
        body, name="conv_bwd_pre", grid=(T // tr,),
        in_specs=_conv_specs(T, tr) + [pl.BlockSpec((tr, CONV_COLS), lambda i: (i, 0))],
        out_specs=[pl.BlockSpec((tr, CONV_COLS), lambda i: (i, 0)),
                   pl.BlockSpec((8, CONV_COLS), lambda i: (0, 0))],
        out_shape=[_sds((T, CONV_COLS), F32), _sds((8, CONV_COLS), F32)],
        compiler_params=_params(("arbitrary",)))(pm, pm, w, b, dqk)


def _conv_bwd_in(dpre, w):
    T = dpre.shape[0]
    tr = min(ROW_TILE, T)
    nt = T // tr

    def body(cur_ref, halo_ref, w_ref, o_ref):
        i = pl.program_id(0)
        yy = jnp.concatenate([cur_ref[...], jnp.where(i < nt - 1, halo_ref[...], 0.0)], axis=0)
        du = cur_ref[...] * w_ref[CONV_WIDTH - 1:CONV_WIDTH, :]
        for j in range(CONV_WIDTH - 1):
            k = CONV_WIDTH - 1 - j
            du = du + pltpu.roll(yy, tr + 8 - k, 0)[:tr] * w_ref[j:j + 1, :]
        o_ref[...] = du.astype(BF16)

    return pl.pallas_call(
        body, name="conv_bwd_in", grid=(nt,),
        in_specs=[pl.BlockSpec((tr, CONV_COLS), lambda i: (i, 0)),
                  pl.BlockSpec((8, CONV_COLS),
                               lambda i: (jnp.minimum((i + 1) * (tr // 8), T // 8 - 1), 0)),
                  pl.BlockSpec((CONV_WIDTH, CONV_COLS), lambda i: (0, 0))],
        out_specs=pl.BlockSpec((tr, CONV_COLS), lambda i: (i, 0)),
        out_shape=_sds((T, CONV_COLS), BF16), compiler_params=_params(("parallel",)))(dpre, dpre, w)


def _log_sigmoid(x):
    return jnp.minimum(x, 0.0) - jnp.log1p(jnp.exp(-jnp.abs(x)))


def _chunk_cumsum(x, axis):
    idx = lax.broadcasted_iota(jnp.int32, x.shape, axis) % MLSTM_CHUNK
    k = 1
    while k < MLSTM_CHUNK:
        x = x + jnp.where(idx >= k, pltpu.roll(x, k, axis), 0.0)
        k *= 2
    return x


def _chunk_rev_cumsum(x, axis):
    n = x.shape[axis]
    idx = lax.broadcasted_iota(jnp.int32, x.shape, axis) % MLSTM_CHUNK
    k = 1
    while k < MLSTM_CHUNK:
        x = x + jnp.where(idx < MLSTM_CHUNK - k, pltpu.roll(x, n - k, axis), 0.0)
        k *= 2
    return x


def _mlstm_gates(gc_ref, bc_ref, gr_ref, br_ref):
    gc = gc_ref[...] + bc_ref[...]
    gr = gr_ref[...] + br_ref[...]
    return gc, _chunk_cumsum(_log_sigmoid(gc), 0), gr, _chunk_cumsum(_log_sigmoid(gr), 1)


def _heads(ref, base=0):
    D = MLSTM_HEAD_DIM
    return jnp.stack([ref[:, base + D * h:base + D * h + D] for h in range(MLSTM_HEADS)])


def _mlstm_inputs(q_ref, k_ref, v_ref, gc, bc, gr, br):
    H = MLSTM_HEADS
    q, v = _heads(q_ref), _heads(v_ref)
    ks = _heads(k_ref) * (MLSTM_HEAD_DIM ** -0.5)
    return dict(
        q=q, ks=ks, qb=q.astype(BF16), kb=ks.astype(BF16), vb=v.astype(BF16),
        b_col=jnp.stack([bc[:, H + h:H + h + 1] for h in range(H)]),
        i_col=jnp.stack([gc[:, h:h + 1] for h in range(H)]),
        b_row=jnp.stack([br[H + h:H + h + 1, :] for h in range(H)]),
        i_row=jnp.stack([gr[h:h + 1, :] for h in range(H)]))


def _mlstm_head(f, c_prev, n_prev, m_prev):
    L = MLSTM_CHUNK
    q, qb = f["q"], f["qb"]
    t = lax.broadcasted_iota(jnp.int32, (1, 2 * L, 2 * L), 1)
    s = lax.broadcasted_iota(jnp.int32, (1, 2 * L, 2 * L), 2)
    mask = (t // L == s // L) & (s <= t)
    d = jnp.where(mask, f["b_col"] - f["b_row"] + f["i_row"], NEG_INF)
    row = lax.broadcasted_iota(jnp.int32, (1, 2 * L, 1), 1)
    inter = f["b_col"] + jnp.where(row < L, m_prev[0], m_prev[1])
    m_t = jnp.maximum(inter, jnp.max(d, axis=-1, keepdims=True))
    w_intra = jnp.exp(d - m_t)
    w_inter = jnp.exp(inter - m_t)
    sc = _bdot(qb, f["kb"], 2, 2) * w_intra
    qc = jnp.concatenate([_bdot(qb[:, :L], c_prev[0].astype(BF16), 2, 1),
                          _bdot(qb[:, L:], c_prev[1].astype(BF16), 2, 1)], axis=1)
    qn = jnp.concatenate([jnp.sum(q[:, :L] * n_prev[0], axis=-1, keepdims=True),
                          jnp.sum(q[:, L:] * n_prev[1], axis=-1, keepdims=True)], axis=1)
    num = _bdot(sc.astype(BF16), f["vb"], 2, 1) + w_inter * qc
    den = jnp.sum(sc, axis=-1, keepdims=True) + w_inter * qn
    return dict(f, w_intra=w_intra, w_inter=w_inter, sc=sc, qc=qc, qn=qn, num=num, den=den,
                floor=jnp.exp(-m_t))


def _mlstm_update(f, ch, c, n, m):
    L = MLSTM_CHUNK
    rows = slice(L * ch, L * ch + L)
    b_col = f["b_col"][:, rows]
    g_last = b_col[:, L - 1:L]
    a_col = g_last - b_col + f["i_col"][:, rows]
    m_new = jnp.maximum(g_last + m, jnp.max(a_col, axis=1, keepdims=True))
    decay = jnp.exp(g_last + m - m_new)
    e_a = jnp.exp(a_col - m_new)
    kw = f["ks"][:, rows] * e_a
    c_new = decay * c + _bdot_rows(kw.astype(BF16), f["vb"][:, rows])
    n_new = decay * n + jnp.sum(kw, axis=1, keepdims=True)
    return c_new, n_new, m_new, decay, e_a, kw


def _mlstm_specs(T, order):
    blk = lambda w, col: pl.BlockSpec((STEP_ROWS, w), lambda s: (order(s), col))
    return [blk(512, 0), blk(512, 1), blk(512, 2), blk(128, 0),
            pl.BlockSpec((1, 128), lambda s: (0, 0)),
            pl.BlockSpec((8, STEP_ROWS), lambda s: (0, order(s))),
            pl.BlockSpec((8, 128), lambda s: (0, 0))]


def _lanes(m):
    return jnp.broadcast_to(m, m.shape[:-1] + (128,))


def _mlstm_fwd(qk, pm, gcol, bcol, grow, brow):
    T = qk.shape[0]
    steps = T // STEP_ROWS
    H, D = MLSTM_HEADS, MLSTM_HEAD_DIM

    def body(q_ref, k_ref, v_ref, gc_ref, bc_ref, gr_ref, br_ref, h_ref, cs_ref, ns_ref, ms_ref,
             c_st, n_st, m_st):
        @pl.when(pl.program_id(0) == 0)
        def _():
            c_st[...] = jnp.zeros_like(c_st)
            n_st[...] = jnp.zeros_like(n_st)
            m_st[...] = jnp.zeros_like(m_st)

        f = _mlstm_inputs(q_ref, k_ref, v_ref, *_mlstm_gates(gc_ref, bc_ref, gr_ref, br_ref))
        c0, n0, m0 = c_st[...], n_st[...], m_st[:, :, 0:1]
        c1, n1, m1, _, _, _ = _mlstm_update(f, 0, c0, n0, m0)
        c2, n2, m2, _, _, _ = _mlstm_update(f, 1, c1, n1, m1)
        f = _mlstm_head(f, (c0, c1), (n0, n1), (m0, m1))
        h = f["num"] / jnp.maximum(jnp.abs(f["den"]), f["floor"])
        for hd in range(H):
            h_ref[:, D * hd:D * hd + D] = h[hd]
        cs_ref[0], cs_ref[1] = c0, c1
        ns_ref[0], ns_ref[1] = n0, n1
        ms_ref[0], ms_ref[1] = _lanes(m0), _lanes(m1)
        c_st[...], n_st[...], m_st[...] = c2, n2, _lanes(m2)

    vec = pl.BlockSpec((2, H, 1, 128), lambda s: (s, 0, 0, 0))
    return pl.pallas_call(
        body, name="mlstm_fwd", grid=(steps,), in_specs=_mlstm_specs(T, lambda s: s),
        out_specs=[pl.BlockSpec((STEP_ROWS, 512), lambda s: (s, 0)),
                   pl.BlockSpec((2, H, 128, 128), lambda s: (s, 0, 0, 0)), vec, vec],
        out_shape=[_sds((T, 512), F32), _sds((2 * steps, H, 128, 128), F32),
                   _sds((2 * steps, H, 1, 128), F32), _sds((2 * steps, H, 1, 128), F32)],
        scratch_shapes=[pltpu.VMEM((H, 128, 128), F32), pltpu.VMEM((H, 1, 128), F32),
                        pltpu.VMEM((H, 1, 128), F32)],
        compiler_params=_params(("arbitrary",)))(qk, qk, pm, gcol, bcol, grow, brow)


def _mlstm_bwd(qk, pm, gcol, bcol, grow, brow, cs, ns, ms, dh):
    T = qk.shape[0]
    steps = T // STEP_ROWS
    H, L, D = MLSTM_HEADS, MLSTM_CHUNK, MLSTM_HEAD_DIM
    rev = lambda s: steps - 1 - s

    def body(q_ref, k_ref, v_ref, gc_ref, bc_ref, gr_ref, br_ref, cs_ref, ns_ref, ms_ref, dh_ref,
             dqk_ref, dv_ref, dgc_ref, dgr_ref, dc_st, dn_st):
        @pl.when(pl.program_id(0) == 0)
        def _():
            dc_st[...] = jnp.zeros_like(dc_st)
            dn_st[...] = jnp.zeros_like(dn_st)

        f = _mlstm_inputs(q_ref, k_ref, v_ref, *_mlstm_gates(gc_ref, bc_ref, gr_ref, br_ref))
        c_prev = (cs_ref[0], cs_ref[1])
        n_prev = (ns_ref[0], ns_ref[1])
        m_prev = (ms_ref[0, :, :, 0:1], ms_ref[1, :, :, 0:1])
        f = _mlstm_head(f, c_prev, n_prev, m_prev)
        big = jnp.abs(f["den"]) > f["floor"]
        rden = 1.0 / jnp.where(big, jnp.abs(f["den"]), f["floor"])
        dnum = _heads(dh_ref) * rden
        hdh = jnp.sum(f["num"] * dnum, axis=-1, keepdims=True)
        dden = jnp.where(big, -hdh * rden * jnp.sign(f["den"]), 0.0)
        dnum_b = dnum.astype(BF16)
        dsc = _bdot(dnum_b, f["vb"], 2, 2) + dden
        g = dsc * f["sc"]
        dv = _bdot_rows(f["sc"].astype(BF16), dnum_b)
        dqk_ = (dsc * f["w_intra"]).astype(BF16)
        dq = _bdot(dqk_, f["kb"], 2, 1)
        dks = _bdot_rows(dqk_, f["qb"])
        wdn = f["w_inter"] * dnum
        wdn_b = wdn.astype(BF16)
        wdd = f["w_inter"] * dden
        u = jnp.sum(f["qc"] * wdn, axis=-1, keepdims=True) + wdd * f["qn"]
        dks_s, dv_s, z_s, dg_s = [None, None], [None, None], [None, None], [None, None]
        dcn, dnn = dc_st[...], dn_st[...]
        for ch in (1, 0):
            rows = slice(L * ch, L * ch + L)
            _, _, _, decay, e_a, kw = _mlstm_update(f, ch, c_prev[ch], n_prev[ch], m_prev[ch])
            dcn_b = dcn.astype(BF16)
            dkw = _bdot(f["vb"][:, rows], dcn_b, 2, 2) + dnn
            dks_s[ch] = e_a * dkw
            dv_s[ch] = _bdot(kw.astype(BF16), dcn_b, 2, 1)
            z_s[ch] = e_a * jnp.sum(f["ks"][:, rows] * dkw, axis=-1, keepdims=True)
            dg_s[ch] = jnp.sum(z_s[ch], axis=1, keepdims=True) + decay * (
                jnp.sum(c_prev[ch] * dcn, axis=(1, 2), keepdims=True)
                + jnp.sum(n_prev[ch] * dnn, axis=(1, 2), keepdims=True))
            dcn = decay * dcn + _bdot_rows(f["qb"][:, rows], wdn_b[:, rows])
            dnn = decay * dnn + jnp.sum(wdd[:, rows] * f["q"][:, rows], axis=1, keepdims=True)
        dc_st[...], dn_st[...] = dcn, dnn
        dq = dq + jnp.concatenate(
            [_bdot(wdn_b[:, :L], c_prev[0].astype(BF16), 2, 2) + wdd[:, :L] * n_prev[0],
             _bdot(wdn_b[:, L:], c_prev[1].astype(BF16), 2, 2) + wdd[:, L:] * n_prev[1]], axis=1)
        dks = (dks + jnp.concatenate(dks_s, axis=1)) * (D ** -0.5)
        dv = dv + jnp.concatenate(dv_s, axis=1)
        z = jnp.concatenate(z_s, axis=1)
        row = lax.broadcasted_iota(jnp.int32, (1, STEP_ROWS, 1), 1)
        dg_col = jnp.where(row == L - 1, dg_s[0], 0.0) + jnp.where(row == 2 * L - 1, dg_s[1], 0.0)
        db_col = jnp.sum(g, axis=-1, keepdims=True) + u - z + dg_col
        g_row = jnp.sum(g, axis=1, keepdims=True)
        lane = lax.broadcasted_iota(jnp.int32, (STEP_ROWS, 128), 1)
        sub = lax.broadcasted_iota(jnp.int32, (8, STEP_ROWS), 0)
        dgc = jnp.zeros((STEP_ROWS, 128), F32)
        dgr = jnp.zeros((8, STEP_ROWS), F32)
        for hd in range(H):
            dgc = dgc + jnp.where(lane == hd, z[hd], 0.0) + jnp.where(lane == H + hd, db_col[hd], 0.0)
            dgr = dgr + jnp.where(sub == hd, g_row[hd], 0.0) - jnp.where(sub == H + hd, g_row[hd], 0.0)
            dqk_ref[:, D * hd:D * hd + D] = dq[hd]
            dqk_ref[:, H * D + D * hd:H * D + D * hd + D] = dks[hd]
            dv_ref[:, D * hd:D * hd + D] = dv[hd].astype(BF16)
        dgc_ref[...] = dgc
        dgr_ref[...] = dgr

    return pl.pallas_call(
        body, name="mlstm_bwd", grid=(steps,),
        in_specs=_mlstm_specs(T, rev) + [
            pl.BlockSpec((2, H, 128, 128), lambda s: (rev(s), 0, 0, 0)),
            pl.BlockSpec((2, H, 1, 128), lambda s: (rev(s), 0, 0, 0)),
            pl.BlockSpec((2, H, 1, 128), lambda s: (rev(s), 0, 0, 0)),
            pl.BlockSpec((STEP_ROWS, 512), lambda s: (rev(s), 0))],
        out_specs=[pl.BlockSpec((STEP_ROWS, 1024), lambda s: (rev(s), 0)),
                   pl.BlockSpec((STEP_ROWS, 512), lambda s: (rev(s), 0)),
                   pl.BlockSpec((STEP_ROWS, 128), lambda s: (rev(s), 0)),
                   pl.BlockSpec((8, STEP_ROWS), lambda s: (0, rev(s)))],
        out_shape=[_sds((T, 1024), F32), _sds((T, 512), BF16), _sds((T, 128), F32), _sds((8, T), F32)],
        scratch_shapes=[pltpu.VMEM((H, 128, 128), F32), pltpu.VMEM((H, 1, 128), F32)],
        compiler_params=_params(("arbitrary",)))(qk, qk, pm, gcol, bcol, grow, brow, cs, ns, ms, dh)


def _gate_bwd(dgc, dgr_t, gcol, bcol):
    T = dgc.shape[0]

    def body(a_ref, b_ref, g_ref, bias_ref, o_ref, acc_ref):
        @pl.when(pl.program_id(0) == 0)
        def _():
            acc_ref[...] = jnp.zeros_like(acc_ref)

        d = a_ref[...] + b_ref[...]
        lane = lax.broadcasted_iota(jnp.int32, d.shape, 1)
        is_f = (lane >= MLSTM_HEADS) & (lane < 2 * MLSTM_HEADS)
        dlogf = _chunk_rev_cumsum(jnp.where(is_f, d, 0.0), 0)
        out = jnp.where(is_f, dlogf * _sigmoid(-(g_ref[...] + bias_ref[...])), d)
        o_ref[...] = out.astype(BF16)
        acc_ref[0:1, :] += _colsum(out)

    return _rows("gate_bwd", body, [dgc, dgr_t, gcol, bcol],
                 [_sds((T, 128), BF16), _sds((8, 128), F32)], T)


def _head_norm(h, mu_axis=-1):
    mu = jnp.mean(h, axis=-1, keepdims=True)
    hc = h - mu
    r = lax.rsqrt(jnp.mean(hc * hc, axis=-1, keepdims=True) + NORM_EPS)
    return hc * r, r


def _mlstm_out(hm, pm, w):
    T = hm.shape[0]
    D = MLSTM_HEAD_DIM

    def body(h_ref, o_ref, w_ref, y_ref):
        for hd in range(MLSTM_HEADS):
            cols = slice(D * hd, D * hd + D)
            hn, _ = _head_norm(h_ref[:, cols])
            y_ref[:, cols] = (_sigmoid(o_ref[:, cols]) * hn * w_ref[:, cols]).astype(BF16)

    tr = min(ROW_TILE, T)
    return pl.pallas_call(
        body, name="mlstm_out", grid=(T // tr,),
        in_specs=[pl.BlockSpec((tr, 512), lambda i: (i, 0)), pl.BlockSpec((tr, 512), lambda i: (i, 3)),
                  pl.BlockSpec((1, 512), lambda i: (0, 0))],
        out_specs=pl.BlockSpec((tr, 512), lambda i: (i, 0)), out_shape=_sds((T, 512), BF16),
        compiler_params=_params(("parallel",)))(hm, pm, w)


def _mlstm_out_bwd(hm, pm, w, dy):
    T = hm.shape[0]
    D = MLSTM_HEAD_DIM
    tr = min(ROW_TILE, T)

    def body(h_ref, o_ref, w_ref, dy_ref, dh_ref, do_ref, acc_ref):
        @pl.when(pl.program_id(0) == 0)
        def _():
            acc_ref[...] = jnp.zeros_like(acc_ref)

        for hd in range(MLSTM_HEADS):
            cols = slice(D * hd, D * hd + D)
            hn, r = _head_norm(h_ref[:, cols])
            sg = _sigmoid(o_ref[:, cols])
            dy, w = dy_ref[:, cols], w_ref[:, cols]
            do_ref[:, cols] = (dy * hn * w * sg * (1.0 - sg)).astype(BF16)
            dyn = dy * sg
            acc_ref[0:1, cols] += _colsum(dyn * hn)
            dhn = dyn * w
            dh_ref[:, cols] = r * (dhn - jnp.mean(dhn, axis=-1, keepdims=True)
                                   - hn * jnp.mean(dhn * hn, axis=-1, keepdims=True))

    return pl.pallas_call(
        body, name="mlstm_out_bwd", grid=(T // tr,),
        in_specs=[pl.BlockSpec((tr, 512), lambda i: (i, 0)), pl.BlockSpec((tr, 512), lambda i: (i, 3)),
                  pl.BlockSpec((1, 512), lambda i: (0, 0)), pl.BlockSpec((tr, 512), lambda i: (i, 0))],
        out_specs=[pl.BlockSpec((tr, 512), lambda i: (i, 0)), pl.BlockSpec((tr, 512), lambda i: (i, 0)),
                   pl.BlockSpec((8, 512), lambda i: (0, 0))],
        out_shape=[_sds((T, 512), F32), _sds((T, 512), BF16), _sds((8, 512), F32)],
        compiler_params=_params(("arbitrary",)))(hm, pm, w, dy)


def _adamw(name, w, g, m, v, tr=64):
    R, C = w.shape
    tr = min(tr, R)
    if R % tr:
        tr, block = R, (R, 128)
        spec, grid = pl.BlockSpec(block, lambda i: (0, i)), (C // 128,)
    else:
        spec, grid = pl.BlockSpec((tr, C), lambda i: (i, 0)), (R // tr,)
    c1 = 1.0 - ADAM_B1 ** ADAM_STEP
    c2 = 1.0 - ADAM_B2 ** ADAM_STEP

    def body(w_ref, g_ref, m_ref, v_ref, d_ref, mo_ref, vo_ref):
        g = g_ref[...]
        m = ADAM_B1 * m_ref[...] + (1.0 - ADAM_B1) * g
        v = ADAM_B2 * v_ref[...] + (1.0 - ADAM_B2) * (g * g)
        mo_ref[...] = m
        vo_ref[...] = v
        d_ref[...] = -ADAM_LR * ((m / c1) / (jnp.sqrt(v / c2) + ADAM_EPS) + ADAM_WD * w_ref[...])

    return pl.pallas_call(
        body, name=name, grid=grid, in_specs=[spec] * 4, out_specs=[spec] * 3,
        out_shape=[_sds((R, C), F32)] * 3, compiler_params=_params(("parallel",)))(w, g, m, v)


def _place():
    return lax.axis_index("x"), lax.axis_index("y"), lax.axis_index("c")


def _all_gather8(name, blk, space):
    m, n = blk.shape

    def body(x_ref, out_ref, send_sems, recv_sems, local_sem):
        x, y, c = _place()
        me, sibling = (x, y, c), (x, y, 1 - c)
        chips = [(1 - x, y), (x, 1 - y), (1 - x, 1 - y)]

        def rows(px, py, pc):
            return out_ref.at[pl.ds((4 * px + 2 * py + pc) * m, m), :]

        def copy(k, block, to, src=None):
            return pltpu.make_async_remote_copy(
                src_ref=rows(*block) if src is None else src, dst_ref=rows(*block),
                send_sem=send_sems.at[k], recv_sem=recv_sems.at[k],
                device_id=to, device_id_type=MESH)

        mine = pltpu.make_async_copy(x_ref, rows(*me), local_sem)
        mine.start()
        first = [copy(0, me, sibling, src=x_ref)]
        first += [copy(1 + j, me, (*chip, c), src=x_ref) for j, chip in enumerate(chips)]
        for cp in first:
            cp.start()
        passed = [copy(4 + j, (*chip, c), sibling) for j, chip in enumerate(chips)]
        for j, chip in enumerate(chips):
            copy(1 + j, (*chip, c), me).wait_recv()
            passed[j].start()
        copy(0, sibling, me).wait_recv()
        for j, chip in enumerate(chips):
            copy(4 + j, (*chip, 1 - c), me).wait_recv()
        for cp in first + passed:
            cp.wait_send()
        mine.wait()

    return pl.pallas_call(
        body, name=name, out_shape=_sds((8 * m, n), blk.dtype),
        in_specs=[pl.BlockSpec(memory_space=space)], out_specs=pl.BlockSpec(memory_space=space),
        scratch_shapes=[pltpu.SemaphoreType.DMA((7,)), pltpu.SemaphoreType.DMA((7,)),
                        pltpu.SemaphoreType.DMA],
        compiler_params=pltpu.CompilerParams(vmem_limit_bytes=VMEM_LIMIT))(blk)


def _hbm_specs(n):
    return [pl.BlockSpec(memory_space=pl.ANY)] * n


def _swap_halves_sibling(name, srcs):
    nw = len(srcs)

    def body(*refs):
        src_refs, dst_refs, send_sems, recv_sems = refs[:nw], refs[nw:2 * nw], refs[2 * nw], refs[2 * nw + 1]
        x, y, c = _place()
        cps = [pltpu.make_async_remote_copy(
            src_ref=src_refs[w].at[_whole(src_refs[w]), pl.ds(0, 4), 1 - c], dst_ref=dst_refs[w],
            send_sem=send_sems.at[w], recv_sem=recv_sems.at[w], device_id=(x, y, 1 - c),
            device_id_type=MESH) for w in range(nw)]
        for cp in cps:
            cp.start()
        for cp in cps:
            cp.wait()

    shapes = [s.shape[:2] + s.shape[3:] for s in srcs]
    return pl.pallas_call(
        body, name=name, out_shape=[_sds(sh, s.dtype) for sh, s in zip(shapes, srcs)],
        in_specs=_hbm_specs(nw), out_specs=_hbm_specs(nw),
        scratch_shapes=[pltpu.SemaphoreType.DMA((nw,)), pltpu.SemaphoreType.DMA((nw,))])(*srcs)


def _split_start(name, srcs, lands, copies, per_array):
    nw = len(srcs)

    def body(*refs):
        send_sems, recv_sems, token = refs[2 * nw], refs[2 * nw + 1], refs[-1]
        for w in range(nw):
            for k, (s, d, dev) in enumerate(copies(refs[w], refs[nw + w], *_place())):
                pltpu.make_async_remote_copy(
                    src_ref=s, dst_ref=d, send_sem=send_sems.at[w * per_array + k],
                    recv_sem=recv_sems.at[w * per_array + k], device_id=dev, device_id_type=MESH).start()
        token[...] = jnp.zeros_like(token)

    hbm, sem = pl.BlockSpec(memory_space=pltpu.HBM), pl.BlockSpec(memory_space=pltpu.SEMAPHORE)
    arrays = list(srcs) + list(lands)
    out = pl.pallas_call(
        body, name=name,
        out_shape=(pltpu.SemaphoreType.DMA((nw * per_array,)), pltpu.SemaphoreType.DMA((nw * per_array,)),
                   *[pltpu.HBM(a.shape, a.dtype) for a in arrays], _sds((8, 128), F32)),
        in_specs=[hbm] * (2 * nw),
        out_specs=(sem, sem, *[hbm] * (2 * nw), pl.BlockSpec(memory_space=pltpu.VMEM)),
        input_output_aliases={i: 2 + i for i in range(2 * nw)},
        compiler_params=pltpu.CompilerParams(has_side_effects=pltpu.SideEffectType.DATAFLOW_SIDE_EFFECTING))(
            *[pltpu.with_memory_space_constraint(a, pltpu.HBM) for a in arrays])
    return out[0], out[1], out[2:2 + nw], out[2 + nw:2 + 2 * nw], out[-1]


def _split_wait(name, started, after, waits, per_array):
    send_sems, recv_sems, srcs, lands, _ = started
    nw = len(srcs)

    def body(*refs):
        send_sems, recv_sems = refs[2 * nw], refs[2 * nw + 1]
        x, y, c = _place()
        for w in range(nw):
            for k, (s, d) in enumerate(waits(refs[w], refs[nw + w], x, y, c)):
                cp = pltpu.make_async_remote_copy(
                    src_ref=s, dst_ref=d, send_sem=send_sems.at[w * per_array + k],
                    recv_sem=recv_sems.at[w * per_array + k], device_id=(x, y, 1 - c),
                    device_id_type=MESH)
                cp.wait_send()
                cp.wait_recv()

    hbm, sem = pl.BlockSpec(memory_space=pltpu.HBM), pl.BlockSpec(memory_space=pltpu.SEMAPHORE)
    arrays = list(srcs) + list(lands)
    out = pl.pallas_call(
        body, name=name, out_shape=tuple(pltpu.HBM(a.shape, a.dtype) for a in arrays),
        in_specs=[hbm] * (2 * nw) + [sem, sem, pl.BlockSpec(memory_space=pl.ANY)],
        out_specs=tuple([hbm] * (2 * nw)), input_output_aliases={i: i for i in range(2 * nw)},
        compiler_params=pltpu.CompilerParams(has_side_effects=pltpu.SideEffectType.DATAFLOW_SIDE_EFFECTING))(
            *arrays, send_sems, recv_sems, after)
    return list(out[nw:])


def _other_chips(x, y):
    return [(1 - x, y), (x, 1 - y), (1 - x, 1 - y)]


def _whole(ref):
    return pl.ds(0, ref.shape[0])


def _gather_sends(src_ref, land_ref, x, y, c):
    to = land_ref.at[_whole(land_ref), 2 * x + y, c]
    return [(src_ref, to, (x, y, 1 - c))] + [(src_ref, to, (px, py, c)) for px, py in _other_chips(x, y)]


def _gather_lands(src_ref, land_ref, x, y, c):
    g = _whole(land_ref)
    return [(src_ref, land_ref.at[g, 2 * x + y, 1 - c])] + [
        (src_ref, land_ref.at[g, 2 * px + py, c]) for px, py in _other_chips(x, y)]


def _scatter_sends(src_ref, land_ref, x, y, c):
    g = _whole(src_ref)
    return [(src_ref.at[g, 2 * px + py], land_ref.at[g, 2 * x + y], (px, py, c)) for px, py in _other_chips(x, y)]


def _scatter_lands(src_ref, land_ref, x, y, c):
    g = _whole(src_ref)
    return [(src_ref.at[g, 2 * x + y], land_ref.at[g, 2 * px + py]) for px, py in _other_chips(x, y)]


def _forward_sibling(name, lands):
    nw = len(lands)

    def body(*refs):
        land_refs, out_refs, send_sems, recv_sems = refs[:nw], refs[nw:2 * nw], refs[2 * nw], refs[2 * nw + 1]
        x, y, c = _place()
        cps = []
        for w in range(nw):
            g = _whole(land_refs[w])
            cps += [pltpu.make_async_remote_copy(
                src_ref=land_refs[w].at[g, 2 * px + py, c], dst_ref=out_refs[w].at[g, 2 * px + py, c],
                send_sem=send_sems.at[w, j], recv_sem=recv_sems.at[w, j], device_id=(x, y, 1 - c),
                device_id_type=MESH) for j, (px, py) in enumerate(_other_chips(x, y))]
        for cp in cps:
            cp.start()
        for w in range(nw):
            g = _whole(land_refs[w])
            for j, (px, py) in enumerate(_other_chips(x, y)):
                slot = out_refs[w].at[g, 2 * px + py, 1 - c]
                pltpu.make_async_remote_copy(src_ref=slot, dst_ref=slot, send_sem=send_sems.at[w, j],
                                             recv_sem=recv_sems.at[w, j], device_id=(x, y, 1 - c),
                                             device_id_type=MESH).wait_recv()
        for cp in cps:
            cp.wait_send()

    return pl.pallas_call(
        body, name=name, out_shape=[_sds(a.shape, a.dtype) for a in lands],
        in_specs=_hbm_specs(nw), out_specs=_hbm_specs(nw), input_output_aliases={i: i for i in range(nw)},
        scratch_shapes=[pltpu.SemaphoreType.DMA((nw, 3)), pltpu.SemaphoreType.DMA((nw, 3))])(*lands)


def _share_halves(name, halves):
    nw = len(halves)

    def body(*refs):
        in_refs, out_refs, send_sems, recv_sems = refs[:nw], refs[nw:2 * nw], refs[2 * nw], refs[2 * nw + 1]
        x, y, c = _place()
        cps = [pltpu.make_async_remote_copy(
            src_ref=in_refs[w].at[_whole(in_refs[w]), c], dst_ref=out_refs[w].at[_whole(in_refs[w]), c],
            send_sem=send_sems.at[w], recv_sem=recv_sems.at[w], device_id=(x, y, 1 - c),
            device_id_type=MESH) for w in range(nw)]
        for cp in cps:
            cp.start()
        for w in range(nw):
            slot = out_refs[w].at[_whole(in_refs[w]), 1 - c]
            pltpu.make_async_remote_copy(src_ref=slot, dst_ref=slot, send_sem=send_sems.at[w],
                                         recv_sem=recv_sems.at[w], device_id=(x, y, 1 - c),
                                         device_id_type=MESH).wait_recv()
        for cp in cps:
            cp.wait_send()

    return pl.pallas_call(
        body, name=name, out_shape=[_sds(a.shape, a.dtype) for a in halves],
        in_specs=_hbm_specs(nw), out_specs=_hbm_specs(nw), input_output_aliases={i: i for i in range(nw)},
        scratch_shapes=[pltpu.SemaphoreType.DMA((nw,)), pltpu.SemaphoreType.DMA((nw,))])(*halves)


def _pair_sum(name, full, got, core):
    g, _, _, m, n = full.shape

    def body(c_ref, a_ref, b_ref, o_ref):
        o_ref[...] = (a_ref[...].astype(F32) + b_ref[...].astype(F32)).astype(o_ref.dtype)

    slab = pl.BlockSpec((None, None, m, n), lambda w, s, c: (w, s, 0, 0))
    return pl.pallas_call(
        body, name=name,
        grid_spec=pltpu.PrefetchScalarGridSpec(
            num_scalar_prefetch=1, grid=(g, 4),
            in_specs=[pl.BlockSpec((None, None, None, m, n), lambda w, s, c: (w, s, c[0], 0, 0)), slab],
            out_specs=slab),
        out_shape=_sds(got.shape, BF16),
        compiler_params=_params(("parallel", "parallel")))(core, full, got)


def _sum4(name, a, core):
    g, _, m, n = a.shape

    def body(c_ref, a_ref, o_ref):
        acc = a_ref[0].astype(F32)
        for s in range(1, 4):
            acc = acc + a_ref[s].astype(F32)
        o_ref[...] = acc

    return pl.pallas_call(
        body, name=name,
        grid_spec=pltpu.PrefetchScalarGridSpec(
            num_scalar_prefetch=1, grid=(g,),
            in_specs=[pl.BlockSpec((None, 4, m, n), lambda w, c: (w, 0, 0, 0))],
            out_specs=pl.BlockSpec((None, None, m, n), lambda w, c: (w, c[0], 0, 0))),
        out_shape=_sds((g, 2, m, n), F32), compiler_params=_params(("parallel",)))(core, a)


def _small_update(gathered, w, m, v):
    n = w.shape[1]
    tn = 2048
    c1 = 1.0 - ADAM_B1 ** ADAM_STEP
    c2 = 1.0 - ADAM_B2 ** ADAM_STEP

    def body(g_ref, w_ref, m_ref, v_ref, go_ref, d_ref, mo_ref, vo_ref):
        g = g_ref[0:1, :]
        for d in range(1, 8):
            g = g + g_ref[d:d + 1, :]
        go_ref[...] = g
        m = ADAM_B1 * m_ref[...] + (1.0 - ADAM_B1) * g
        v = ADAM_B2 * v_ref[...] + (1.0 - ADAM_B2) * (g * g)
        mo_ref[...] = m
        vo_ref[...] = v
        d_ref[...] = -ADAM_LR * ((m / c1) / (jnp.sqrt(v / c2) + ADAM_EPS) + ADAM_WD * w_ref[...])

    row = pl.BlockSpec((1, tn), lambda i: (0, i))
    return pl.pallas_call(
        body, name="small_update", grid=(n // tn,),
        in_specs=[pl.BlockSpec((8, tn), lambda i: (0, i)), row, row, row], out_specs=[row] * 4,
        out_shape=[_sds((1, n), F32)] * 4, compiler_params=_params(("parallel",)))(gathered, w, m, v)


def _swiglu(ps, es):
    g, u = ps
    return g * _sigmoid(g) * u, g, u


def _swiglu_bwd(ps, es):
    g, u = es[0].astype(F32), es[1].astype(F32)
    sg = _sigmoid(g)
    return ps[0] * u * (sg * (1.0 + g * (1.0 - sg))), ps[0] * (g * sg)


def _merge(ps, es):
    ga, gm = [e.astype(F32) for e in es]
    return _sigmoid(ga) * ps[0] + _sigmoid(gm) * ps[1], ps[0], ps[1]


def _merge_bwd(ps, es):
    a, b, ga, gm = [e.astype(F32) for e in es]
    sa, sm = _sigmoid(ga), _sigmoid(gm)
    dm = ps[0]
    return dm * sa, dm * sm, dm * a * (sa * (1.0 - sa)), dm * b * (sm * (1.0 - sm))


W_IN_PIECES = (("q", 512), ("kv", 256), ("mqk", 1024), ("mv", 512), ("mo", 512), ("if", 8),
               ("ga", 1024), ("gm", 1024))


def _local_step(x, tgt, pos_col, mod, sp, in_weights, late_weights, ffn_grads, mixer_grads):
    sh_m, sc_m, gate_m, sh_f, sc_f, gate_f = mod
    h = _pre_norm(x, sp["g_pre_mix"], sc_m, sh_m)
    inv = ROPE_THETA ** (-2.0 * jnp.arange(HEAD_DIM // 2, dtype=F32) / HEAD_DIM)
    cos, sin = _rope_tables(pos_col, jnp.tile(inv, 4).reshape(1, 128))
    W = dict(in_weights(h))
    w_a = jnp.concatenate([W["q"], W["kv"]], axis=0)
    w_m = jnp.concatenate([W["mqk"], W["mv"], W["mo"]], axis=0)
    w_g = jnp.concatenate([W["ga"], W["gm"]], axis=0)
    pa, = _mm("proj_attn", [[(h, w_a)]], [], _first, [F32], cn=256, nt=True)
    pm, = _mm("proj_mlstm", [[(h, w_m)]], [], _first, [F32], cn=512, nt=True)
    pif, = _mm("proj_gates", [[(h, W["if"])]], [], _first, [F32], cn=128, nt=True)
    pg, = _mm("proj_branch_gates", [[(h, w_g)]], [], _first, [BF16], cn=512, nt=True)
    ya = _attn_fwd(pa, cos, sin, sp["sinks"])
    qk = _conv_fwd(pm, sp["conv_w"], sp["conv_b"])
    bcol = jnp.pad(sp["b_if"], ((0, 0), (0, 120)))
    brow = jnp.broadcast_to(sp["b_if"].reshape(8, 1), (8, 128))
    grow = pif[:, :8].T
    hm, cs, ns, ms = _mlstm_fwd(qk, pm, pif, bcol, grow, brow)
    ym = _mlstm_out(hm, pm, sp["norm_w"])
    W.update(late_weights(ym))
    w_fg, w_fu, w_fd = W["fg"], W["fu"], W["fd"]
    merged, br_a, br_m = _mm("branches", [[(ya, W["ba"])], [(ym, W["bm"])]],
                             [(pg, 0), (pg, 1)], _merge, [BF16, BF16, BF16], cn=512, nt=True)
    wide, narrow = (D_MODEL, F32), (D_MODEL, BF16)
    mix, x1, h2 = _mm_rows("mix_out", [[(merged, W["out"])]],
                           [x, gate_m, sp["g_post_mix"], sp["g_pre_ffn"], sc_f, sh_f],
                           _res_norm_rows, [wide, wide, narrow], [], cn=512)
    act, gt, up = _mm("ffn_in", [[(h2, w_fg)], [(h2, w_fu)]], [], _swiglu, [BF16] * 3,
                      cn=256, nt=True)
    dy, dff, acc_l, loss = _mm_rows("ffn_down", [[(act, w_fd)]], [x1, tgt, gate_f, sp["g_post_ffn"]],
                                    _final_loss_rows, [wide, narrow], [(8, D_MODEL), (1, 128)], cn=512)

    G = {}
    dgt, dup = _mm("ffn_down_bwd", [[(dff, w_fd)]], [gt, up], _swiglu_bwd, [BF16, BF16],
                   cn=256, nt=True)
    g_fd = _mm_tn("dw_ffn_down", act, dff, BF16, 1408, 512)
    g_fg = _mm_tn("dw_ffn_gate", dgt, h2, BF16, 1408, 1024)
    g_fu = _mm_tn("dw_ffn_up", dup, h2, BF16, 1408, 1024)
    tie = ffn_grads(g_fg, g_fu, g_fd)
    dx1, dmix, acc_r = _mm_rows(
        "ffn_in_bwd", [[(dgt, w_fg), (dup, w_fu)]],
        [x1, mix, dy, sc_f + tie, gate_m, sp["g_pre_ffn"], sp["g_post_mix"]],
        _res_norm_bwd_rows, [wide, narrow], [(8, D_MODEL)], cn=512, tm=256)
    d_a, d_m, dga, dgm = _mm("mix_out_bwd", [[(dmix, W["out"])]],
                             [br_a, br_m, (pg, 0), (pg, 1)], _merge_bwd,
                             [BF16] * 4, cn=512, nt=True)
    G["out"] = _mm_tn("dw_out", merged, dmix, BF16, 1024, 512)
    dya, = _mm("branch_attn_bwd", [[(d_a, W["ba"])]], [], _first, [F32], cn=512)
    dym, = _mm("branch_mlstm_bwd", [[(d_m, W["bm"])]], [], _first, [F32], cn=512)
    G["ba"] = _mm_tn("dw_branch_attn", d_a, ya, BF16, 1024, 512)
    G["bm"] = _mm_tn("dw_branch_mlstm", d_m, ym, BF16, 1024, 512)
    dhm, do_m, acc_n = _mlstm_out_bwd(hm, pm, sp["norm_w"], dym)
    dqk, dv_m, dgc, dgr = _mlstm_bwd(qk, pm, pif, bcol, grow, brow, cs, ns, ms, dhm)
    dif, acc_g = _gate_bwd(dgc, jnp.pad(dgr.T, ((0, 0), (0, 120))), pif, bcol)
    dpre, acc_c = _conv_bwd_pre(pm, sp["conv_w"], sp["conv_b"], dqk)
    du = _conv_bwd_in(dpre, sp["conv_w"])
    dq_a, dcur, dprv, dsink = _attn_bwd(pa, cos, sin, sp["sinks"], dya)
    dkv = _attn_kv_combine(dcur, dprv, cos, sin)
    dproj = {"q": dq_a, "kv": dkv, "mqk": du, "mv": dv_m, "mo": do_m, "if": dif, "ga": dga, "gm": dgm}
    for k, _ in W_IN_PIECES:
        G[k] = _mm_tn("dw_in_" + k, dproj[k], h, BF16, dproj[k].shape[1], 1024)
    w_tied = dict(W, **{"if": W["if"] + mixer_grads(G).astype(BF16)})
    dx, acc_p = _mm_rows("proj_bwd", [[(dproj[k], w_tied[k]) for k, _ in W_IN_PIECES]],
                         [x, dx1, sp["g_pre_mix"], sc_m], _pre_norm_bwd_rows, [wide], [(8, D_MODEL)],
                         cn=512, tm=256)

    small = {
        "mod": jnp.concatenate([acc_p[1], acc_p[0], acc_r[3], acc_r[1], acc_r[0], acc_l[0]]),
        "g_pre_mix": acc_p[2], "g_post_mix": acc_r[4], "b_if": acc_g[0, :8],
        "conv_w": acc_c[:CONV_WIDTH].reshape(-1), "conv_b": acc_c[CONV_WIDTH],
        "sinks": dsink[:, 0], "norm_w": acc_n[0], "g_pre_ffn": acc_r[2], "g_post_ffn": acc_l[1]}
    return loss, dx, small


IN_WIDTH = sum(n for _, n in W_IN_PIECES)
IN_SHARD = IN_WIDTH // 4
IN_SHARD_PAD = -(-IN_SHARD // 32) * 32


def _split_w_in(w_in_t):
    out, off = {}, 0
    for k, n in W_IN_PIECES:
        out[k] = w_in_t[off:off + n]
        off += n
    out["if"] = jnp.pad(out["if"], ((0, 120), (0, 0)))
    return out


def _halves(a):
    return a.reshape(4, 2, a.shape[0] // 8, a.shape[1])


SMALL = (("b_ada", 6144), ("g_pre_mix", 1024), ("g_post_mix", 1024), ("b_if", 128), ("conv_w", 4096),
         ("conv_b", 1024), ("sinks", 128), ("norm_w", 512), ("g_pre_ffn", 1024), ("g_post_ffn", 1024))
SMALL_LEN = 8 * 2048


def _pack_small(vals):
    parts = []
    for k, n in SMALL:
        v = vals[k].reshape(-1)
        parts.append(jnp.pad(v, (0, n - v.shape[0])))
    flat = jnp.concatenate(parts)
    return jnp.pad(flat, (0, SMALL_LEN - flat.shape[0]))


def _unpack_small(flat, shapes):
    out, off = {}, 0
    for k, n in SMALL:
        size = 1
        for d in shapes[k]:
            size *= d
        out[k] = flat[off:off + size].reshape(shapes[k])
        off += n
    return out


def kernel(x, c, positions, w_ada, b_ada, g_pre_mix, g_post_mix, w_in, b_if, conv_w, conv_b, attn_sinks, mlstm_norm_w, w_branch_attn, w_branch_mlstm, w_out, g_pre_ffn, g_post_ffn, w_ffn_gate, w_ffn_up, w_ffn_down, loss_target, m_w_ada, m_b_ada, m_g_pre_mix, m_g_post_mix, m_w_in, m_b_if, m_conv_w, m_conv_b, m_attn_sinks, m_mlstm_norm_w, m_w_branch_attn, m_w_branch_mlstm, m_w_out, m_g_pre_ffn, m_g_post_ffn, m_w_ffn_gate, m_w_ffn_up, m_w_ffn_down, v_w_ada, v_b_ada, v_g_pre_mix, v_g_post_mix, v_w_in, v_b_if, v_conv_w, v_conv_b, v_attn_sinks, v_mlstm_norm_w, v_w_branch_attn, v_w_branch_mlstm, v_w_out, v_g_pre_ffn, v_g_post_ffn, v_w_ffn_gate, v_w_ffn_up, v_w_ffn_down):
    xi, yi, ci = _place()
    chip = 2 * xi + yi
    dev = 2 * chip + ci
    T = x.shape[1]
    ada_cols = w_ada.shape[2]

    def my_half(a):
        n = a.shape[0] // 2
        return lax.dynamic_slice_in_dim(a, ci * n, n, axis=0).astype(BF16)

    blk = jnp.concatenate([c.reshape(-1), conv_w.reshape(-1)]).reshape(8, 256)
    got = _all_gather8("gather_cond", blk, pltpu.VMEM).reshape(8, 2048)
    c_all = got[:, :D_MODEL].astype(BF16)
    conv_full = got[::2, D_MODEL:].reshape(4, CONV_WIDTH, -1).transpose(1, 0, 2).reshape(CONV_WIDTH, -1)

    b_sh = lax.dynamic_slice_in_dim(b_ada, chip * ada_cols, ada_cols, axis=1)
    mod_part, = _mm("ada_mod", [[(c_all, w_ada[0].astype(BF16))]], [b_sh],
                    lambda ps, es: (ps[0] + es[0],), [F32], cn=512, tm=8)
    mod_all = _all_gather8("gather_mod", mod_part, pltpu.VMEM).reshape(4, 2, 8, ada_cols)[:, 0]
    mod = lax.dynamic_index_in_dim(mod_all, dev, axis=1, keepdims=False).reshape(6, 1, D_MODEL)

    def gather_start(name, blks, after):
        blks, _ = lax.optimization_barrier((blks, after))
        lands = [lax.dynamic_update_slice(lax.empty((b.shape[0], 4, 2) + b.shape[1:], BF16),
                                          b[:, None, None], (0, chip, ci, 0, 0)) for b in blks]
        return _split_start(name + "_start", blks, lands, _gather_sends, 4)

    def gather_wait(name, started, after):
        return _forward_sibling(name + "_forward", _split_wait(name + "_wait", started, after, _gather_lands, 4))

    w_in_t = jnp.pad(w_in[0].T, ((0, IN_SHARD_PAD - IN_SHARD), (0, 0)))
    in_started = gather_start("in_gather", [my_half(w_in_t)[None]], mod)
    late_started = gather_start(
        "late_gather",
        [jnp.stack([my_half(w_ffn_gate[0].T), my_half(w_ffn_up[0].T), my_half(w_ffn_down[0])]),
         my_half(w_out[0])[None], jnp.stack([my_half(w_branch_attn[0].T), my_half(w_branch_mlstm[0].T)])],
        in_started[4])
    mod = mod + (in_started[4][0, 0] + late_started[4][0, 0])

    def in_weights(after):
        g_in, = gather_wait("in_gather", in_started, after)
        return _split_w_in(g_in.reshape(4, IN_SHARD_PAD, D_MODEL)[:, :IN_SHARD].reshape(IN_WIDTH, D_MODEL))

    def late_weights(after):
        g_ffn, g_out, g_br = gather_wait("late_gather", late_started, after)
        return {"fg": g_ffn[0].reshape(D_FF, D_MODEL), "fu": g_ffn[1].reshape(D_FF, D_MODEL),
                "fd": g_ffn[2].reshape(D_FF, D_MODEL), "out": g_out.reshape(D_MODEL, D_MODEL),
                "ba": g_br[0].reshape(D_MODEL, -1), "bm": g_br[1].reshape(D_MODEL, -1)}

    core = ci.reshape(1).astype(jnp.int32)
    sent = {}

    def scatter_start(name, groups):
        theirs = _swap_halves_sibling(name + "_pair", groups)
        pairs = [_pair_sum("%s_pair_sum_%d" % (name, i), a, b, core)
                 for i, (a, b) in enumerate(zip(groups, theirs))]
        sent[name] = _split_start(name + "_start", pairs, [p + jnp.zeros((), BF16) for p in pairs],
                                  _scatter_sends, 3)
        return sent[name][4][0, 0]

    def ffn_grads(g_fg, g_fu, g_fd):
        return scatter_start("rs_ffn", [jnp.stack([_halves(g_fg), _halves(g_fu), _halves(g_fd)])])

    def mixer_grads(G):
        g_in_t = jnp.concatenate([G[k][:n] for k, n in W_IN_PIECES]).reshape(4, IN_SHARD, D_MODEL)
        g_in_t = jnp.pad(g_in_t, ((0, 0), (0, IN_SHARD_PAD - IN_SHARD), (0, 0)))
        return scatter_start("rs_mix", [g_in_t.reshape(1, 4, 2, IN_SHARD_PAD // 2, D_MODEL),
                                        _halves(G["out"])[None],
                                        jnp.stack([_halves(G["ba"]), _halves(G["bm"])])])

    sp = {"g_pre_mix": g_pre_mix, "g_post_mix": g_post_mix, "b_if": b_if, "conv_w": conv_full,
          "conv_b": conv_b, "sinks": attn_sinks, "norm_w": mlstm_norm_w, "g_pre_ffn": g_pre_ffn,
          "g_post_ffn": g_post_ffn}
    loss, dx, small = _local_step(x[0], loss_target[0], positions.reshape(T, 1), [mod[i] for i in range(6)],
                                  sp, in_weights, late_weights, ffn_grads, mixer_grads)

    landed = (_split_wait("rs_ffn_wait", sent["rs_ffn"], dx, _scatter_lands, 3)
              + _split_wait("rs_mix_wait", sent["rs_mix"], dx, _scatter_lands, 3))
    reds = [_sum4("rs_chip_sum_%d" % i, a, core) for i, a in enumerate(landed)]
    s_ffn, s_in, s_out, s_br = [s.reshape(s.shape[0], -1, s.shape[-1]) for s in _share_halves("rs_share", reds)]
    gsh = {"fg": s_ffn[0], "fu": s_ffn[1], "fd": s_ffn[2], "w_in": s_in[0, :IN_SHARD],
           "out": s_out[0], "ba": s_br[0], "bm": s_br[1]}

    small["b_ada"] = small.pop("mod")
    vec = _pack_small(small).reshape(8, 2048)
    g_all = _all_gather8("gather_small", vec, pltpu.VMEM).reshape(8, SMALL_LEN)
    dmod_sh = lax.dynamic_slice_in_dim(g_all[:, :6 * D_MODEL], chip * ada_cols, ada_cols, axis=1)
    g_w_ada = _mm_tn("dw_ada", c_all, dmod_sh.astype(BF16), F32, D_MODEL, 512, 8)

    smalls = {"b_ada": (b_ada, m_b_ada, v_b_ada), "g_pre_mix": (g_pre_mix, m_g_pre_mix, v_g_pre_mix),
              "g_post_mix": (g_post_mix, m_g_post_mix, v_g_post_mix), "b_if": (b_if, m_b_if, v_b_if),
              "conv_w": None, "conv_b": (conv_b, m_conv_b, v_conv_b),
              "sinks": (attn_sinks, m_attn_sinks, v_attn_sinks),
              "norm_w": (mlstm_norm_w, m_mlstm_norm_w, v_mlstm_norm_w),
              "g_pre_ffn": (g_pre_ffn, m_g_pre_ffn, v_g_pre_ffn),
              "g_post_ffn": (g_post_ffn, m_g_post_ffn, v_g_post_ffn)}
    shapes = {k: (t[0].shape if t is not None else (1, CONV_WIDTH, D_MODEL)) for k, t in smalls.items()}
    zeros = jnp.zeros((CONV_WIDTH * D_MODEL,), F32)
    packs = [_pack_small({k: (t[i] if t is not None else zeros) for k, t in smalls.items()}).reshape(1, -1)
             for i in range(3)]
    s_out = [_unpack_small(o[0], shapes) for o in _small_update(g_all, *packs)]
    g_conv = lax.dynamic_slice_in_dim(s_out[0]["conv_w"], chip * conv_w.shape[2], conv_w.shape[2], axis=2)

    res = {}
    for k, t in smalls.items():
        if t is not None:
            res[k] = tuple(o[k] for o in s_out)
    res["conv_w"] = (g_conv, *[o[None] for o in _adamw("adam_conv_w", conv_w[0], g_conv[0], m_conv_w[0], v_conv_w[0])])
    res["w_ada"] = (g_w_ada[None], *[o[None] for o in _adamw("adam_w_ada", w_ada[0], g_w_ada, m_w_ada[0], v_w_ada[0])])
    bigs = {"w_in": (w_in, m_w_in, v_w_in), "ba": (w_branch_attn, m_w_branch_attn, v_w_branch_attn),
            "bm": (w_branch_mlstm, m_w_branch_mlstm, v_w_branch_mlstm), "out": (w_out, m_w_out, v_w_out),
            "fg": (w_ffn_gate, m_w_ffn_gate, v_w_ffn_gate), "fu": (w_ffn_up, m_w_ffn_up, v_w_ffn_up),
            "fd": (w_ffn_down, m_w_ffn_down, v_w_ffn_down)}
    for k, (w, m, v) in bigs.items():
        if k in ("w_in", "fg", "fu"):
            res[k] = tuple(o.T[None] for o in (gsh[k], *_adamw("adam_" + k, w[0].T, gsh[k], m[0].T, v[0].T)))
        else:
            g = gsh[k].T if k in ("ba", "bm") else gsh[k]
            res[k] = (g[None], *[o[None] for o in _adamw("adam_" + k, w[0], g, m[0], v[0])])

    order = ("w_ada", "b_ada", "g_pre_mix", "g_post_mix", "w_in", "b_if", "conv_w", "conv_b", "sinks",
             "norm_w", "ba", "bm", "out", "g_pre_ffn", "g_post_ffn", "fg", "fu", "fd")
    total = lax.psum(loss[0, 0], ("x", "y", "c"))
    return (total, dx[None], *[res[k][0] for k in order], *[res[k][1] for k in order],
            *[res[k][2] for k in order], *[res[k][3] for k in order])
```

```python
import functools

import jax
import jax.numpy as jnp
from jax import lax
from jax.experimental import pallas as pl
from jax.experimental.pallas import tpu as pltpu

F32, BF16 = jnp.float32, jnp.bfloat16
MESH = pl.DeviceIdType.MESH

D_MODEL = 1024
N_Q_HEADS, N_KV_HEADS, HEAD_DIM, WINDOW = 8, 2, 64, 128
ROPE_THETA = 10000.0
MLSTM_HEADS, MLSTM_HEAD_DIM, MLSTM_CHUNK, CONV_WIDTH = 4, 128, 64, 4
D_FF = 2816
NORM_EPS = 1e-6
ADAM_LR, ADAM_B1, ADAM_B2, ADAM_EPS, ADAM_WD, ADAM_STEP = 0.001, 0.9, 0.999, 1e-08, 0.01, 10

VMEM_LIMIT = 56 * 1024 * 1024
ROW_TILE = 256
MM_TM = 512
MM_TT = 1024
ATTN_BLK = WINDOW
STEP_ROWS = 2 * MLSTM_CHUNK
NEG_INF = float("-inf")


def _params(sem):
    return pltpu.CompilerParams(dimension_semantics=sem, vmem_limit_bytes=VMEM_LIMIT)


def _sds(shape, dtype):
    return jax.ShapeDtypeStruct(shape, dtype)


def _sigmoid(x):
    return 1.0 / (1.0 + jnp.exp(-x))


def _dot(a, b, ca, cb):
    return lax.dot_general(a, b, (((ca,), (cb,)), ((), ())), preferred_element_type=F32)


def _bdot(a, b, ca, cb):
    return lax.dot_general(a, b, (((ca,), (cb,)), ((0,), (0,))), preferred_element_type=F32)


def _bdot_rows(a, b):
    return jnp.stack([_dot(a[h], b[h], 0, 0) for h in range(a.shape[0])])


def _mm(name, prods, extras, epi, out_dtypes, cn, nt=False, tm=MM_TM):
    flat = [ab for p in prods for ab in p]
    counts = [len(p) for p in prods]
    M = flat[0][0].shape[0]
    N = flat[0][1].shape[0 if nt else 1]
    tm = min(tm, M)
    n_in = 2 * len(flat) + len(extras)

    def body(*refs):
        ins, outs = refs[:n_in], refs[n_in:]
        for j in range(N // cn):
            cols = slice(j * cn, (j + 1) * cn)
            k, ps = 0, []
            for cnt in counts:
                acc = None
                for _ in range(cnt):
                    b = ins[k + 1][cols, :] if nt else ins[k + 1][:, cols]
                    d = _dot(ins[k][...], b, 1, 1 if nt else 0)
                    acc = d if acc is None else acc + d
                    k += 2
                ps.append(acc)
            res = epi(ps, [r[:, cols] for r in ins[k:]])
            for o, r in zip(outs, res):
                o[:, cols] = r.astype(o.dtype)

    in_specs, args = [], []
    for a, b in flat:
        in_specs.append(pl.BlockSpec((tm, a.shape[1]), lambda i: (i, 0)))
        in_specs.append(pl.BlockSpec(b.shape, lambda i: (0, 0), pipeline_mode=pl.Buffered(1)))
        args += [a, b]
    for e in extras:
        e, off = e if isinstance(e, tuple) else (e, 0)
        rows = 1 if e.shape[0] == 1 else tm
        in_specs.append(pl.BlockSpec((rows, N), lambda i, off=off, rows=rows: (0 if rows == 1 else i, off)))
        args.append(e)
    return pl.pallas_call(
        body, name=name, grid=(M // tm,), in_specs=in_specs,
        out_specs=[pl.BlockSpec((tm, N), lambda i: (i, 0)) for _ in out_dtypes],
        out_shape=[_sds((M, N), dt) for dt in out_dtypes],
        compiler_params=_params(("parallel",)))(*args)


def _mm_rows(name, prods, extras, epi, outs, accs, cn, nt=False, tm=MM_TM):
    flat = [ab for p in prods for ab in p]
    counts = [len(p) for p in prods]
    M = flat[0][0].shape[0]
    N = flat[0][1].shape[0 if nt else 1]
    tm = min(tm, M)
    n_mm, n_in, n_out = 2 * len(flat), 2 * len(flat) + len(extras), len(outs)

    def body(*refs):
        ins, out_refs, acc_refs = refs[:n_in], refs[n_in:n_in + n_out], refs[n_in + n_out:]

        @pl.when(pl.program_id(0) == 0)
        def _():
            for a in acc_refs:
                a[...] = jnp.zeros_like(a)

        chunks = [[] for _ in counts]
        for j in range(N // cn):
            cols = slice(j * cn, (j + 1) * cn)
            k = 0
            for p, cnt in enumerate(counts):
                acc = None
                for _ in range(cnt):
                    b = ins[k + 1][cols, :] if nt else ins[k + 1][:, cols]
                    d = _dot(ins[k][...], b, 1, 1 if nt else 0)
                    acc = d if acc is None else acc + d
                    k += 2
                chunks[p].append(acc)
        ps = [c[0] if len(c) == 1 else jnp.concatenate(c, axis=1) for c in chunks]
        res, incs = epi(ps, [r[...] for r in ins[n_mm:]])
        for o, r in zip(out_refs, res):
            o[...] = r.astype(o.dtype)
        for a, inc in zip(acc_refs, incs):
            a[...] += inc

    in_specs, args = [], []
    for a, b in flat:
        in_specs.append(pl.BlockSpec((tm, a.shape[1]), lambda i: (i, 0)))
        in_specs.append(pl.BlockSpec(b.shape, lambda i: (0, 0), pipeline_mode=pl.Buffered(1)))
        args += [a, b]
    for e in extras:
        rows = 1 if e.shape[0] == 1 else tm
        in_specs.append(pl.BlockSpec((rows, e.shape[1]), lambda i, rows=rows: (0 if rows == 1 else i, 0)))
        args.append(e)
    return pl.pallas_call(
        body, name=name, grid=(M // tm,), in_specs=in_specs,
        out_specs=[pl.BlockSpec((tm, w), lambda i: (i, 0)) for w, _ in outs]
        + [pl.BlockSpec(s, lambda i: (0, 0)) for s in accs],
        out_shape=[_sds((M, w), dt) for w, dt in outs] + [_sds(s, F32) for s in accs],
        compiler_params=_params(("arbitrary",)))(*args)


def _mm_tn(name, a, b, out_dtype, tk, tn, tt=MM_TT):
    T, Ka = a.shape
    N = b.shape[1]
    tt = min(tt, T)
    steps = T // tt

    def body(a_ref, b_ref, o_ref, acc):
        t = pl.program_id(2)

        @pl.when(t == 0)
        def _():
            acc[...] = jnp.zeros_like(acc)

        acc[...] += _dot(a_ref[...], b_ref[...], 0, 0)

        @pl.when(t == steps - 1)
        def _():
            o_ref[...] = acc[...].astype(o_ref.dtype)

    return pl.pallas_call(
        body, name=name, grid=(Ka // tk, N // tn, steps),
        in_specs=[pl.BlockSpec((tt, tk), lambda i, j, t: (t, i)),
                  pl.BlockSpec((tt, tn), lambda i, j, t: (t, j))],
        out_specs=pl.BlockSpec((tk, tn), lambda i, j, t: (i, j)),
        out_shape=_sds((Ka, N), out_dtype),
        scratch_shapes=[pltpu.VMEM((tk, tn), F32)],
        compiler_params=_params(("parallel", "parallel", "arbitrary")))(a, b)


def _first(ps, es):
    return (ps[0],)


def _rows(name, body, ins, out_shapes, T, tr=ROW_TILE):
    tr = min(tr, T)

    def spec(shape):
        if shape[0] == T:
            return pl.BlockSpec((tr,) + tuple(shape[1:]), lambda i: (i,) + (0,) * (len(shape) - 1))
        return pl.BlockSpec(tuple(shape), lambda i: (0,) * len(shape))

    return pl.pallas_call(
        body, name=name, grid=(T // tr,),
        in_specs=[spec(a.shape) for a in ins], out_specs=[spec(s.shape) for s in out_shapes],
        out_shape=out_shapes, compiler_params=_params(("arbitrary",)))(*ins)


def _rms(x):
    r = lax.rsqrt(jnp.mean(x * x, axis=-1, keepdims=True) + NORM_EPS)
    return x * r, r


def _rms_bwd(dxn, xn, r):
    return r * (dxn - xn * jnp.mean(dxn * xn, axis=-1, keepdims=True))


def _colsum(v):
    return jnp.sum(v, axis=0, keepdims=True)


def _proj_in(x, g, sc, sh, groups):
    T = x.shape[0]
    tm = min(MM_TM, T)
    ng = len(groups)

    def body(x_ref, g_ref, sc_ref, sh_ref, *rest):
        w_refs, h_ref, out_refs = rest[:ng], rest[ng], rest[ng + 1:]
        xn, _ = _rms(x_ref[...])
        h = (xn * g_ref[...] * (1.0 + sc_ref[...]) + sh_ref[...]).astype(BF16)
        h_ref[...] = h
        for w_ref, o_ref, (w, _, cn) in zip(w_refs, out_refs, groups):
            for j in range(w.shape[0] // cn):
                cols = slice(j * cn, (j + 1) * cn)
                o_ref[:, cols] = _dot(h, w_ref[cols, :], 1, 1).astype(o_ref.dtype)

    row = pl.BlockSpec((1, D_MODEL), lambda i: (0, 0))
    tile = lambda w: pl.BlockSpec((tm, w), lambda i: (i, 0))
    return pl.pallas_call(
        body, name="proj_in", grid=(T // tm,),
        in_specs=[tile(D_MODEL), row, row, row] + [
            pl.BlockSpec(w.shape, lambda i: (0, 0), pipeline_mode=pl.Buffered(1)) for w, _, _ in groups],
        out_specs=[tile(D_MODEL)] + [tile(w.shape[0]) for w, _, _ in groups],
        out_shape=[_sds((T, D_MODEL), BF16)] + [_sds((T, w.shape[0]), dt) for w, dt, _ in groups],
        compiler_params=_params(("parallel",)))(x, g, sc, sh, *[w for w, _, _ in groups])


def _acc_rows(rows):
    w = rows[0].shape[1]
    return jnp.concatenate(rows + [jnp.zeros((8 - len(rows), w), F32)], axis=0)


def _res_norm_rows(ps, es):
    mix = ps[0]
    x, gate, gp, g2, sc, sh = es
    mh, _ = _rms(mix)
    x1 = x + gate * (mh * gp)
    xn, _ = _rms(x1)
    return [mix, x1, xn * g2 * (1.0 + sc) + sh], []


def _final_loss_rows(ps, es):
    x1, tgt, gate, gp = es
    fh, r = _rms(ps[0])
    e = x1 + gate * (fh * gp) - tgt
    loss = 0.5 * jnp.sum(jnp.mean(e * e, axis=-1, keepdims=True))
    dy = e * (1.0 / D_MODEL)
    acc = _acc_rows([_colsum(dy * fh * gp), _colsum(dy * gate * fh)])
    return [dy, _rms_bwd(dy * gate * gp, fh, r)], [acc, jnp.full((1, 128), loss, F32)]


def _res_norm_bwd_rows(ps, es):
    dh = ps[0]
    x1, mix, dy, sc, gate, g2, gp = es
    xn, r1 = _rms(x1)
    rows = [_colsum(dh * xn * g2), _colsum(dh), _colsum(dh * (1.0 + sc) * xn)]
    dx1 = dy + _rms_bwd(dh * (1.0 + sc) * g2, xn, r1)
    mh, rm = _rms(mix)
    rows += [_colsum(dx1 * mh * gp), _colsum(dx1 * gate * mh)]
    return [dx1, _rms_bwd(dx1 * gate * gp, mh, rm)], [_acc_rows(rows)]


def _pre_norm_bwd_rows(ps, es):
    dh = ps[0]
    x, dx1, g, sc = es
    xn, r = _rms(x)
    rows = [_colsum(dh * xn * g), _colsum(dh), _colsum(dh * (1.0 + sc) * xn)]
    return [dx1 + _rms_bwd(dh * (1.0 + sc) * g, xn, r)], [_acc_rows(rows)]


def _rope_tables(pos_col, inv_freq):
    T = pos_col.shape[0]

    def body(p_ref, f_ref, c_ref, s_ref):
        ang = p_ref[...].astype(F32) * f_ref[...]
        lane = lax.broadcasted_iota(jnp.int32, ang.shape, 1)
        c_ref[...] = jnp.cos(ang)
        s_ref[...] = jnp.where(lane % HEAD_DIM < HEAD_DIM // 2, -1.0, 1.0) * jnp.sin(ang)

    return _rows("rope_tables", body, [pos_col, inv_freq],
                 [_sds((T, 128), F32), _sds((T, 128), F32)], T, tr=512)


def _swap_halves(t):
    W = t.shape[1]
    lane = lax.broadcasted_iota(jnp.int32, t.shape, 1)
    half = HEAD_DIM // 2
    return jnp.where(lane % HEAD_DIM < half, pltpu.roll(t, W - half, 1), pltpu.roll(t, half, 1))


def _widen(c, W):
    return c if W == 128 else jnp.concatenate([c] * (W // 128), axis=1)


def _rope(t, c, s):
    W = t.shape[1]
    return t * _widen(c, W) + _swap_halves(t) * _widen(s, W)


def _unrope(dy, c, s):
    W = dy.shape[1]
    return dy * _widen(c, W) + _swap_halves(dy * _widen(s, W))


def _attn_mask(n):
    qi = lax.broadcasted_iota(jnp.int32, (ATTN_BLK, 2 * ATTN_BLK), 0)
    kj = lax.broadcasted_iota(jnp.int32, (ATTN_BLK, 2 * ATTN_BLK), 1)
    rel = kj - ATTN_BLK
    return (rel <= qi) & (qi - rel < WINDOW) & ((n > 0) | (kj >= ATTN_BLK))


def _attn_load(cur, prv, cc, sc, cp, sp):
    x, xp = cur[...].astype(F32), prv[...].astype(F32)
    q = _rope(x[:, :512], cc[...], sc[...]) * (HEAD_DIM ** -0.5)
    k = jnp.concatenate([_rope(xp[:, 512:640], cp[...], sp[...]),
                         _rope(x[:, 512:640], cc[...], sc[...])], axis=0)
    v = jnp.concatenate([xp[:, 640:768], x[:, 640:768]], axis=0)
    return q, k, v


ROLLED = tuple(h for h in range(N_Q_HEADS) if h % 2 != h // (N_Q_HEADS // N_KV_HEADS))


def _pair_heads(t):
    half = lax.broadcasted_iota(jnp.int32, (ATTN_BLK, 128), 1) // HEAD_DIM
    return jnp.stack([jnp.where(half == h % 2, t[:, 128 * (h // 2):128 * (h // 2) + 128], 0.0)
                      for h in range(N_Q_HEADS)])


def _kv_heads(t):
    half = lax.broadcasted_iota(jnp.int32, t.shape, 1) // HEAD_DIM
    tr = pltpu.roll(t, HEAD_DIM, 1)
    return jnp.stack([jnp.where(half == h % 2, tr if h in ROLLED else t, 0.0)
                      for h in range(N_Q_HEADS)])


def _sink_column(snk):
    return jnp.stack([jnp.full((1, 1), snk[0, h], F32) for h in range(N_Q_HEADS)])


def _attn_probs(qh, kh, mask, sink):
    s = jnp.where(mask, _bdot(qh, kh, 2, 2), NEG_INF)
    m = jnp.maximum(jnp.max(s, axis=-1, keepdims=True), sink)
    p = jnp.exp(s - m)
    es = jnp.exp(sink - m)
    rl = 1.0 / (jnp.sum(p, axis=-1, keepdims=True) + es)
    return p, es, rl


def _attn_specs(nb):
    blk = lambda w: pl.BlockSpec((ATTN_BLK, w), lambda n: (n, 0))
    prv = lambda w: pl.BlockSpec((ATTN_BLK, w), lambda n: (jnp.maximum(n - 1, 0), 0))
    return [blk(768), prv(768), blk(128), blk(128), prv(128), prv(128),
            pl.BlockSpec(memory_space=pltpu.SMEM)]


def _attn_fwd(pa, cos, sin, sinks):
    T = pa.shape[0]
    nb = T // ATTN_BLK

    def body(cur, prv, cc, sc, cp, sp, snk, y_ref):
        n = pl.program_id(0)
        q, k, v = _attn_load(cur, prv, cc, sc, cp, sp)
        qh, kh, vh = _pair_heads(q).astype(BF16), _kv_heads(k).astype(BF16), _kv_heads(v).astype(BF16)
        p, _, rl = _attn_probs(qh, kh, _attn_mask(n), _sink_column(snk))
        o = _bdot(p.astype(BF16), vh, 2, 1) * rl
        for pair in range(N_Q_HEADS // 2):
            y_ref[:, 128 * pair:128 * pair + 128] = (o[2 * pair] + o[2 * pair + 1]).astype(BF16)

    return pl.pallas_call(
        body, name="attn_fwd", grid=(nb,), in_specs=_attn_specs(nb),
        out_specs=pl.BlockSpec((ATTN_BLK, 512), lambda n: (n, 0)),
        out_shape=_sds((T, 512), BF16), compiler_params=_params(("parallel",)))(
            pa, pa, cos, sin, cos, sin, sinks)


def _attn_bwd(pa, cos, sin, sinks, dy):
    T = pa.shape[0]
    nb = T // ATTN_BLK

    def body(cur, prv, cc, sc, cp, sp, snk, dy_ref, dq_ref, dcur_ref, dprv_ref, dsink_ref):
        n = pl.program_id(0)

        @pl.when(n == 0)
        def _():
            dsink_ref[...] = jnp.zeros_like(dsink_ref)

        q, k, v = _attn_load(cur, prv, cc, sc, cp, sp)
        qh, kh, vh = _pair_heads(q).astype(BF16), _kv_heads(k).astype(BF16), _kv_heads(v).astype(BF16)
        p, es, rl = _attn_probs(qh, kh, _attn_mask(n), _sink_column(snk))
        pn = p * rl
        do = _pair_heads(dy_ref[...]).astype(BF16)
        dp = _bdot(do, vh, 2, 2)
        delta = jnp.sum(pn * dp, axis=-1, keepdims=True)
        ds = (pn * (dp - delta)).astype(BF16)
        dsink = es * rl * delta
        dq = _bdot(ds, kh, 2, 1) * (HEAD_DIM ** -0.5)
        dkh = _bdot_rows(ds, qh)
        dvh = _bdot_rows(pn.astype(BF16), do)

        def fold(t):
            same = [t[h] for h in range(N_Q_HEADS) if h not in ROLLED]
            moved = [t[h] for h in ROLLED]
            return sum(same[1:], same[0]) + pltpu.roll(sum(moved[1:], moved[0]), HEAD_DIM, 1)

        dk, dv = fold(dkh), fold(dvh)
        for h in range(N_Q_HEADS):
            dsink_ref[h:h + 1, :] += -jnp.sum(dsink[h])
        for pair in range(N_Q_HEADS // 2):
            dq_ref[:, 128 * pair:128 * pair + 128] = _unrope(
                dq[2 * pair] + dq[2 * pair + 1], cc[...], sc[...]).astype(BF16)
        dcur_ref[:, 0:128] = dk[ATTN_BLK:]
        dcur_ref[:, 128:256] = dv[ATTN_BLK:]
        dprv_ref[:, 0:128] = dk[:ATTN_BLK]
        dprv_ref[:, 128:256] = dv[:ATTN_BLK]

    blk = lambda w: pl.BlockSpec((ATTN_BLK, w), lambda n: (n, 0))
    return pl.pallas_call(
        body, name="attn_bwd", grid=(nb,), in_specs=_attn_specs(nb) + [blk(512)],
        out_specs=[blk(512), blk(256), blk(256), pl.BlockSpec((8, 128), lambda n: (0, 0))],
        out_shape=[_sds((T, 512), BF16), _sds((T, 256), F32), _sds((T, 256), F32),
                   _sds((8, 128), F32)],
        compiler_params=_params(("arbitrary",)))(pa, pa, cos, sin, cos, sin, sinks, dy)


def _attn_kv_combine(dcur, dprv, cos, sin):
    T = dcur.shape[0]
    nb = T // ATTN_BLK

    def body(c_ref, p_ref, cc, sc, o_ref):
        n = pl.program_id(0)
        t = c_ref[...] + jnp.where(n < nb - 1, p_ref[...], 0.0)
        o_ref[:, 0:128] = _unrope(t[:, 0:128], cc[...], sc[...]).astype(BF16)
        o_ref[:, 128:256] = t[:, 128:256].astype(BF16)

    blk = lambda w: pl.BlockSpec((ATTN_BLK, w), lambda n: (n, 0))
    nxt = pl.BlockSpec((ATTN_BLK, 256), lambda n: (jnp.minimum(n + 1, nb - 1), 0))
    return pl.pallas_call(
        body, name="attn_kv_combine", grid=(nb,), in_specs=[blk(256), nxt, blk(128), blk(128)],
        out_specs=blk(256), out_shape=_sds((T, 256), BF16),
        compiler_params=_params(("parallel",)))(dcur, dprv, cos, sin)


CONV_COLS = 2 * MLSTM_HEADS * MLSTM_HEAD_DIM


def _conv_pre(cur_ref, halo_ref, w_ref, b_ref, i, tr):
    xx = jnp.concatenate([jnp.where(i > 0, halo_ref[...], 0.0), cur_ref[...]], axis=0)
    taps = [(pltpu.roll(xx, CONV_WIDTH - 1 - j, 0) if j < CONV_WIDTH - 1 else xx)[8:8 + tr]
            for j in range(CONV_WIDTH)]
    pre = b_ref[...]
    for j in range(CONV_WIDTH):
        pre = pre + taps[j] * w_ref[j:j + 1, :]
    return pre, taps


def _conv_specs(T, tr):
    return [pl.BlockSpec((tr, CONV_COLS), lambda i: (i, 0)),
            pl.BlockSpec((8, CONV_COLS), lambda i: (jnp.maximum(i * (tr // 8) - 1, 0), 0)),
            pl.BlockSpec((CONV_WIDTH, CONV_COLS), lambda i: (0, 0)),
            pl.BlockSpec((1, CONV_COLS), lambda i: (0, 0))]


def _conv_fwd(pm, w, b):
    T = pm.shape[0]
    tr = min(ROW_TILE, T)

    def body(cur_ref, halo_ref, w_ref, b_ref, o_ref):
        pre, _ = _conv_pre(cur_ref, halo_ref, w_ref, b_ref, pl.program_id(0), tr)
        o_ref[...] = pre * _sigmoid(pre)

    return pl.pallas_call(
        body, name="conv_fwd", grid=(T // tr,), in_specs=_conv_specs(T, tr),
        out_specs=pl.BlockSpec((tr, CONV_COLS), lambda i: (i, 0)),
        out_shape=_sds((T, CONV_COLS), F32), compiler_params=_params(("parallel",)))(pm, pm, w, b)


def _conv_bwd_pre(pm, w, b, dqk):
    T = pm.shape[0]
    tr = min(ROW_TILE, T)

    def body(cur_ref, halo_ref, w_ref, b_ref, d_ref, dpre_ref, acc_ref):
        i = pl.program_id(0)

        @pl.when(i == 0)
        def _():
            acc_ref[...] = jnp.zeros_like(acc_ref)

        pre, taps = _conv_pre(cur_ref, halo_ref, w_ref, b_ref, i, tr)
        sg = _sigmoid(pre)
        dpre = d_ref[...] * (sg * (1.0 + pre * (1.0 - sg)))
        dpre_ref[...] = dpre
        for j in range(CONV_WIDTH):
            acc_ref[j:j + 1, :] += _colsum(dpre * taps[j])
        acc_ref[CONV_WIDTH:CONV_WIDTH + 1, :] += _colsum(dpre)

    return pl.pallas_call(
        body, name="conv_bwd_pre", grid=(T // tr,),
        in_specs=_conv_specs(T, tr) + [pl.BlockSpec((tr, CONV_COLS), lambda i: (i, 0))],
        out_specs=[pl.BlockSpec((tr, CONV_COLS), lambda i: (i, 0)),
                   pl.BlockSpec((8, CONV_COLS), lambda i: (0, 0))],
        out_shape=[_sds((T, CONV_COLS), F32), _sds((8, CONV_COLS), F32)],
        compiler_params=_params(("arbitrary",)))(pm, pm, w, b, dqk)


def _conv_bwd_in(dpre, w):
    T = dpre.shape[0]
    tr = min(ROW_TILE, T)
    nt = T // tr

    def body(cur_ref, halo_ref, w_ref, o_ref):
        i = pl.program_id(0)
        yy = jnp.concatenate([cur_ref[...], jnp.where(i < nt - 1, halo_ref[...], 0.0)], axis=0)
        du = cur_ref[...] * w_ref[CONV_WIDTH - 1:CONV_WIDTH, :]
        for j in range(CONV_WIDTH - 1):
            k = CONV_WIDTH - 1 - j
            du = du + pltpu.roll(yy, tr + 8 - k, 0)[:tr] * w_ref[j:j + 1, :]
        o_ref[...] = du.astype(BF16)

    return pl.pallas_call(
        body, name="conv_bwd_in", grid=(nt,),
        in_specs=[pl.BlockSpec((tr, CONV_COLS), lambda i: (i, 0)),
                  pl.BlockSpec((8, CONV_COLS),
                               lambda i: (jnp.minimum((i + 1) * (tr // 8), T // 8 - 1), 0)),
                  pl.BlockSpec((CONV_WIDTH, CONV_COLS), lambda i: (0, 0))],
        out_specs=pl.BlockSpec((tr, CONV_COLS), lambda i: (i, 0)),
        out_shape=_sds((T, CONV_COLS), BF16), compiler_params=_params(("parallel",)))(dpre, dpre, w)


def _log_sigmoid(x):
    return jnp.minimum(x, 0.0) - jnp.log1p(jnp.exp(-jnp.abs(x)))


def _chunk_cumsum(x, axis):
    idx = lax.broadcasted_iota(jnp.int32, x.shape, axis) % MLSTM_CHUNK
    k = 1
    while k < MLSTM_CHUNK:
        x = x + jnp.where(idx >= k, pltpu.roll(x, k, axis), 0.0)
        k *= 2
    return x


def _chunk_rev_cumsum(x, axis):
    n = x.shape[axis]
    idx = lax.broadcasted_iota(jnp.int32, x.shape, axis) % MLSTM_CHUNK
    k = 1
    while k < MLSTM_CHUNK:
        x = x + jnp.where(idx < MLSTM_CHUNK - k, pltpu.roll(x, n - k, axis), 0.0)
        k *= 2
    return x


def _mlstm_gates(gc_ref, bc_ref, gr_ref, br_ref):
    gc = gc_ref[...] + bc_ref[...]
    gr = gr_ref[...] + br_ref[...]
    return gc, _chunk_cumsum(_log_sigmoid(gc), 0), gr, _chunk_cumsum(_log_sigmoid(gr), 1)


def _heads(ref, base=0):
    D = MLSTM_HEAD_DIM
    return jnp.stack([ref[:, base + D * h:base + D * h + D] for h in range(MLSTM_HEADS)])


def _mlstm_inputs(q_ref, k_ref, v_ref, gc, bc, gr, br):
    H = MLSTM_HEADS
    q, v = _heads(q_ref), _heads(v_ref)
    ks = _heads(k_ref) * (MLSTM_HEAD_DIM ** -0.5)
    return dict(
        q=q, ks=ks, qb=q.astype(BF16), kb=ks.astype(BF16), vb=v.astype(BF16),
        b_col=jnp.stack([bc[:, H + h:H + h + 1] for h in range(H)]),
        i_col=jnp.stack([gc[:, h:h + 1] for h in range(H)]),
        b_row=jnp.stack([br[H + h:H + h + 1, :] for h in range(H)]),
        i_row=jnp.stack([gr[h:h + 1, :] for h in range(H)]))


def _mlstm_head(f, c_prev, n_prev, m_prev):
    L = MLSTM_CHUNK
    q, qb = f["q"], f["qb"]
    t = lax.broadcasted_iota(jnp.int32, (1, 2 * L, 2 * L), 1)
    s = lax.broadcasted_iota(jnp.int32, (1, 2 * L, 2 * L), 2)
    mask = (t // L == s // L) & (s <= t)
    d = jnp.where(mask, f["b_col"] - f["b_row"] + f["i_row"], NEG_INF)
    row = lax.broadcasted_iota(jnp.int32, (1, 2 * L, 1), 1)
    inter = f["b_col"] + jnp.where(row < L, m_prev[0], m_prev[1])
    m_t = jnp.maximum(inter, jnp.max(d, axis=-1, keepdims=True))
    w_intra = jnp.exp(d - m_t)
    w_inter = jnp.exp(inter - m_t)
    sc = _bdot(qb, f["kb"], 2, 2) * w_intra
    qc = jnp.concatenate([_bdot(qb[:, :L], c_prev[0].astype(BF16), 2, 1),
                          _bdot(qb[:, L:], c_prev[1].astype(BF16), 2, 1)], axis=1)
    qn = jnp.concatenate([jnp.sum(q[:, :L] * n_prev[0], axis=-1, keepdims=True),
                          jnp.sum(q[:, L:] * n_prev[1], axis=-1, keepdims=True)], axis=1)
    num = _bdot(sc.astype(BF16), f["vb"], 2, 1) + w_inter * qc
    den = jnp.sum(sc, axis=-1, keepdims=True) + w_inter * qn
    return dict(f, w_intra=w_intra, w_inter=w_inter, sc=sc, qc=qc, qn=qn, num=num, den=den,
                floor=jnp.exp(-m_t))


def _mlstm_update(f, ch, c, n, m):
    L = MLSTM_CHUNK
    rows = slice(L * ch, L * ch + L)
    b_col = f["b_col"][:, rows]
    g_last = b_col[:, L - 1:L]
    a_col = g_last - b_col + f["i_col"][:, rows]
    m_new = jnp.maximum(g_last + m, jnp.max(a_col, axis=1, keepdims=True))
    decay = jnp.exp(g_last + m - m_new)
    e_a = jnp.exp(a_col - m_new)
    kw = f["ks"][:, rows] * e_a
    c_new = decay * c + _bdot_rows(kw.astype(BF16), f["vb"][:, rows])
    n_new = decay * n + jnp.sum(kw, axis=1, keepdims=True)
    return c_new, n_new, m_new, decay, e_a, kw


def _mlstm_specs(T, order):
    blk = lambda w, col: pl.BlockSpec((STEP_ROWS, w), lambda s: (order(s), col))
    return [blk(512, 0), blk(512, 1), blk(512, 0), blk(128, 0),
            pl.BlockSpec((1, 128), lambda s: (0, 0)),
            pl.BlockSpec((8, STEP_ROWS), lambda s: (0, order(s))),
            pl.BlockSpec((8, 128), lambda s: (0, 0))]


def _lanes(m):
    return jnp.broadcast_to(m, m.shape[:-1] + (128,))


def _mlstm_fwd(qk, pm, gcol, bcol, grow, brow):
    T = qk.shape[0]
    steps = T // STEP_ROWS
    H, D = MLSTM_HEADS, MLSTM_HEAD_DIM

    def body(q_ref, k_ref, v_ref, gc_ref, bc_ref, gr_ref, br_ref, h_ref, cs_ref, ns_ref, ms_ref,
             c_st, n_st, m_st):
        @pl.when(pl.program_id(0) == 0)
        def _():
            c_st[...] = jnp.zeros_like(c_st)
            n_st[...] = jnp.zeros_like(n_st)
            m_st[...] = jnp.zeros_like(m_st)

        f = _mlstm_inputs(q_ref, k_ref, v_ref, *_mlstm_gates(gc_ref, bc_ref, gr_ref, br_ref))
        c0, n0, m0 = c_st[...], n_st[...], m_st[:, :, 0:1]
        c1, n1, m1, _, _, _ = _mlstm_update(f, 0, c0, n0, m0)
        c2, n2, m2, _, _, _ = _mlstm_update(f, 1, c1, n1, m1)
        f = _mlstm_head(f, (c0, c1), (n0, n1), (m0, m1))
        h = f["num"] / jnp.maximum(jnp.abs(f["den"]), f["floor"])
        for hd in range(H):
            h_ref[:, D * hd:D * hd + D] = h[hd]
        cs_ref[0], cs_ref[1] = c0, c1
        ns_ref[0], ns_ref[1] = n0, n1
        ms_ref[0], ms_ref[1] = _lanes(m0), _lanes(m1)
        c_st[...], n_st[...], m_st[...] = c2, n2, _lanes(m2)

    vec = pl.BlockSpec((2, H, 1, 128), lambda s: (s, 0, 0, 0))
    return pl.pallas_call(
        body, name="mlstm_fwd", grid=(steps,), in_specs=_mlstm_specs(T, lambda s: s),
        out_specs=[pl.BlockSpec((STEP_ROWS, 512), lambda s: (s, 0)),
                   pl.BlockSpec((2, H, 128, 128), lambda s: (s, 0, 0, 0)), vec, vec],
        out_shape=[_sds((T, 512), F32), _sds((2 * steps, H, 128, 128), F32),
                   _sds((2 * steps, H, 1, 128), F32), _sds((2 * steps, H, 1, 128), F32)],
        scratch_shapes=[pltpu.VMEM((H, 128, 128), F32), pltpu.VMEM((H, 1, 128), F32),
                        pltpu.VMEM((H, 1, 128), F32)],
        compiler_params=_params(("arbitrary",)))(qk, qk, pm, gcol, bcol, grow, brow)


def _mlstm_bwd(qk, pm, gcol, bcol, grow, brow, cs, ns, ms, dh):
    T = qk.shape[0]
    steps = T // STEP_ROWS
    H, L, D = MLSTM_HEADS, MLSTM_CHUNK, MLSTM_HEAD_DIM
    rev = lambda s: steps - 1 - s

    def body(q_ref, k_ref, v_ref, gc_ref, bc_ref, gr_ref, br_ref, cs_ref, ns_ref, ms_ref, dh_ref,
             dqk_ref, dv_ref, dgc_ref, dgr_ref, dc_st, dn_st):
        @pl.when(pl.program_id(0) == 0)
        def _():
            dc_st[...] = jnp.zeros_like(dc_st)
            dn_st[...] = jnp.zeros_like(dn_st)

        f = _mlstm_inputs(q_ref, k_ref, v_ref, *_mlstm_gates(gc_ref, bc_ref, gr_ref, br_ref))
        c_prev = (cs_ref[0], cs_ref[1])
        n_prev = (ns_ref[0], ns_ref[1])
        m_prev = (ms_ref[0, :, :, 0:1], ms_ref[1, :, :, 0:1])
        f = _mlstm_head(f, c_prev, n_prev, m_prev)
        big = jnp.abs(f["den"]) > f["floor"]
        rden = 1.0 / jnp.where(big, jnp.abs(f["den"]), f["floor"])
        dnum = _heads(dh_ref) * rden
        hdh = jnp.sum(f["num"] * dnum, axis=-1, keepdims=True)
        dden = jnp.where(big, -hdh * rden * jnp.sign(f["den"]), 0.0)
        dnum_b = dnum.astype(BF16)
        dsc = _bdot(dnum_b, f["vb"], 2, 2) + dden
        g = dsc * f["sc"]
        dv = _bdot_rows(f["sc"].astype(BF16), dnum_b)
        dqk_ = (dsc * f["w_intra"]).astype(BF16)
        dq = _bdot(dqk_, f["kb"], 2, 1)
        dks = _bdot_rows(dqk_, f["qb"])
        wdn = f["w_inter"] * dnum
        wdn_b = wdn.astype(BF16)
        wdd = f["w_inter"] * dden
        u = jnp.sum(f["qc"] * wdn, axis=-1, keepdims=True) + wdd * f["qn"]
        dks_s, dv_s, z_s, dg_s = [None, None], [None, None], [None, None], [None, None]
        dcn, dnn = dc_st[...], dn_st[...]
        for ch in (1, 0):
            rows = slice(L * ch, L * ch + L)
            _, _, _, decay, e_a, kw = _mlstm_update(f, ch, c_prev[ch], n_prev[ch], m_prev[ch])
            dcn_b = dcn.astype(BF16)
            dkw = _bdot(f["vb"][:, rows], dcn_b, 2, 2) + dnn
            dks_s[ch] = e_a * dkw
            dv_s[ch] = _bdot(kw.astype(BF16), dcn_b, 2, 1)
            z_s[ch] = e_a * jnp.sum(f["ks"][:, rows] * dkw, axis=-1, keepdims=True)
            dg_s[ch] = jnp.sum(z_s[ch], axis=1, keepdims=True) + decay * (
                jnp.sum(c_prev[ch] * dcn, axis=(1, 2), keepdims=True)
                + jnp.sum(n_prev[ch] * dnn, axis=(1, 2), keepdims=True))
            dcn = decay * dcn + _bdot_rows(f["qb"][:, rows], wdn_b[:, rows])
            dnn = decay * dnn + jnp.sum(wdd[:, rows] * f["q"][:, rows], axis=1, keepdims=True)
        dc_st[...], dn_st[...] = dcn, dnn
        dq = dq + jnp.concatenate(
            [_bdot(wdn_b[:, :L], c_prev[0].astype(BF16), 2, 2) + wdd[:, :L] * n_prev[0],
             _bdot(wdn_b[:, L:], c_prev[1].astype(BF16), 2, 2) + wdd[:, L:] * n_prev[1]], axis=1)
        dks = (dks + jnp.concatenate(dks_s, axis=1)) * (D ** -0.5)
        dv = dv + jnp.concatenate(dv_s, axis=1)
        z = jnp.concatenate(z_s, axis=1)
        row = lax.broadcasted_iota(jnp.int32, (1, STEP_ROWS, 1), 1)
        dg_col = jnp.where(row == L - 1, dg_s[0], 0.0) + jnp.where(row == 2 * L - 1, dg_s[1], 0.0)
        db_col = jnp.sum(g, axis=-1, keepdims=True) + u - z + dg_col
        g_row = jnp.sum(g, axis=1, keepdims=True)
        lane = lax.broadcasted_iota(jnp.int32, (STEP_ROWS, 128), 1)
        sub = lax.broadcasted_iota(jnp.int32, (8, STEP_ROWS), 0)
        dgc = jnp.zeros((STEP_ROWS, 128), F32)
        dgr = jnp.zeros((8, STEP_ROWS), F32)
        for hd in range(H):
            dgc = dgc + jnp.where(lane == hd, z[hd], 0.0) + jnp.where(lane == H + hd, db_col[hd], 0.0)
            dgr = dgr + jnp.where(sub == hd, g_row[hd], 0.0) - jnp.where(sub == H + hd, g_row[hd], 0.0)
            dqk_ref[:, D * hd:D * hd + D] = dq[hd]
            dqk_ref[:, H * D + D * hd:H * D + D * hd + D] = dks[hd]
            dv_ref[:, D * hd:D * hd + D] = dv[hd].astype(BF16)
        dgc_ref[...] = dgc
        dgr_ref[...] = dgr

    return pl.pallas_call(
        body, name="mlstm_bwd", grid=(steps,),
        in_specs=_mlstm_specs(T, rev) + [
            pl.BlockSpec((2, H, 128, 128), lambda s: (rev(s), 0, 0, 0)),
            pl.BlockSpec((2, H, 1, 128), lambda s: (rev(s), 0, 0, 0)),
            pl.BlockSpec((2, H, 1, 128), lambda s: (rev(s), 0, 0, 0)),
            pl.BlockSpec((STEP_ROWS, 512), lambda s: (rev(s), 0))],
        out_specs=[pl.BlockSpec((STEP_ROWS, 1024), lambda s: (rev(s), 0)),
                   pl.BlockSpec((STEP_ROWS, 512), lambda s: (rev(s), 0)),
                   pl.BlockSpec((STEP_ROWS, 128), lambda s: (rev(s), 0)),
                   pl.BlockSpec((8, STEP_ROWS), lambda s: (0, rev(s)))],
        out_shape=[_sds((T, 1024), F32), _sds((T, 512), BF16), _sds((T, 128), F32), _sds((8, T), F32)],
        scratch_shapes=[pltpu.VMEM((H, 128, 128), F32), pltpu.VMEM((H, 1, 128), F32)],
        compiler_params=_params(("arbitrary",)))(qk, qk, pm, gcol, bcol, grow, brow, cs, ns, ms, dh)


def _gate_bwd(dgc, dgr_t, gcol, bcol):
    T = dgc.shape[0]

    def body(a_ref, b_ref, g_ref, bias_ref, o_ref, acc_ref):
        @pl.when(pl.program_id(0) == 0)
        def _():
            acc_ref[...] = jnp.zeros_like(acc_ref)

        d = a_ref[...] + b_ref[...]
        lane = lax.broadcasted_iota(jnp.int32, d.shape, 1)
        is_f = (lane >= MLSTM_HEADS) & (lane < 2 * MLSTM_HEADS)
        dlogf = _chunk_rev_cumsum(jnp.where(is_f, d, 0.0), 0)
        out = jnp.where(is_f, dlogf * _sigmoid(-(g_ref[...] + bias_ref[...])), d)
        o_ref[...] = out.astype(BF16)
        acc_ref[0:1, :] += _colsum(out)

    return _rows("gate_bwd", body, [dgc, dgr_t, gcol, bcol],
                 [_sds((T, 128), BF16), _sds((8, 128), F32)], T)


def _head_norm(h, mu_axis=-1):
    mu = jnp.mean(h, axis=-1, keepdims=True)
    hc = h - mu
    r = lax.rsqrt(jnp.mean(hc * hc, axis=-1, keepdims=True) + NORM_EPS)
    return hc * r, r


def _mlstm_out(hm, pm, w):
    T = hm.shape[0]
    D = MLSTM_HEAD_DIM

    def body(h_ref, o_ref, w_ref, y_ref):
        for hd in range(MLSTM_HEADS):
            cols = slice(D * hd, D * hd + D)
            hn, _ = _head_norm(h_ref[:, cols])
            y_ref[:, cols] = (_sigmoid(o_ref[:, cols].astype(F32)) * hn * w_ref[:, cols]).astype(BF16)

    tr = min(ROW_TILE, T)
    return pl.pallas_call(
        body, name="mlstm_out", grid=(T // tr,),
        in_specs=[pl.BlockSpec((tr, 512), lambda i: (i, 0)), pl.BlockSpec((tr, 512), lambda i: (i, 1)),
                  pl.BlockSpec((1, 512), lambda i: (0, 0))],
        out_specs=pl.BlockSpec((tr, 512), lambda i: (i, 0)), out_shape=_sds((T, 512), BF16),
        compiler_params=_params(("parallel",)))(hm, pm, w)


def _mlstm_out_bwd(hm, pm, w, dy):
    T = hm.shape[0]
    D = MLSTM_HEAD_DIM
    tr = min(ROW_TILE, T)

    def body(h_ref, o_ref, w_ref, dy_ref, dh_ref, do_ref, acc_ref):
        @pl.when(pl.program_id(0) == 0)
        def _():
            acc_ref[...] = jnp.zeros_like(acc_ref)

        for hd in range(MLSTM_HEADS):
            cols = slice(D * hd, D * hd + D)
            hn, r = _head_norm(h_ref[:, cols])
            sg = _sigmoid(o_ref[:, cols].astype(F32))
            dy, w = dy_ref[:, cols], w_ref[:, cols]
            do_ref[:, cols] = (dy * hn * w * sg * (1.0 - sg)).astype(BF16)
            dyn = dy * sg
            acc_ref[0:1, cols] += _colsum(dyn * hn)
            dhn = dyn * w
            dh_ref[:, cols] = r * (dhn - jnp.mean(dhn, axis=-1, keepdims=True)
                                   - hn * jnp.mean(dhn * hn, axis=-1, keepdims=True))

    return pl.pallas_call(
        body, name="mlstm_out_bwd", grid=(T // tr,),
        in_specs=[pl.BlockSpec((tr, 512), lambda i: (i, 0)), pl.BlockSpec((tr, 512), lambda i: (i, 1)),
                  pl.BlockSpec((1, 512), lambda i: (0, 0)), pl.BlockSpec((tr, 512), lambda i: (i, 0))],
        out_specs=[pl.BlockSpec((tr, 512), lambda i: (i, 0)), pl.BlockSpec((tr, 512), lambda i: (i, 0)),
                   pl.BlockSpec((8, 512), lambda i: (0, 0))],
        out_shape=[_sds((T, 512), F32), _sds((T, 512), BF16), _sds((8, 512), F32)],
        compiler_params=_params(("arbitrary",)))(hm, pm, w, dy)


def _adamw(name, w, g, m, v, tr=64):
    R, C = w.shape
    tr = min(tr, R)
    if R % tr:
        tr, block = R, (R, 128)
        spec, grid = pl.BlockSpec(block, lambda i: (0, i)), (C // 128,)
    else:
        spec, grid = pl.BlockSpec((tr, C), lambda i: (i, 0)), (R // tr,)
    c1 = 1.0 - ADAM_B1 ** ADAM_STEP
    c2 = 1.0 - ADAM_B2 ** ADAM_STEP

    def body(w_ref, g_ref, m_ref, v_ref, d_ref, mo_ref, vo_ref):
        g = g_ref[...]
        m = ADAM_B1 * m_ref[...] + (1.0 - ADAM_B1) * g
        v = ADAM_B2 * v_ref[...] + (1.0 - ADAM_B2) * (g * g)
        mo_ref[...] = m
        vo_ref[...] = v
        d_ref[...] = -ADAM_LR * ((m / c1) / (jnp.sqrt(v / c2) + ADAM_EPS) + ADAM_WD * w_ref[...])

    return pl.pallas_call(
        body, name=name, grid=grid, in_specs=[spec] * 4, out_specs=[spec] * 3,
        out_shape=[_sds((R, C), F32)] * 3, compiler_params=_params(("parallel",)))(w, g, m, v)


def _place():
    return lax.axis_index("x"), lax.axis_index("y"), lax.axis_index("c")


def _all_gather8(name, blk, space):
    m, n = blk.shape

    def body(x_ref, out_ref, send_sems, recv_sems, local_sem):
        x, y, c = _place()
        me, sibling = (x, y, c), (x, y, 1 - c)
        chips = [(1 - x, y), (x, 1 - y), (1 - x, 1 - y)]

        def rows(px, py, pc):
            return out_ref.at[pl.ds((4 * px + 2 * py + pc) * m, m), :]

        def copy(k, block, to, src=None):
            return pltpu.make_async_remote_copy(
                src_ref=rows(*block) if src is None else src, dst_ref=rows(*block),
                send_sem=send_sems.at[k], recv_sem=recv_sems.at[k],
                device_id=to, device_id_type=MESH)

        mine = pltpu.make_async_copy(x_ref, rows(*me), local_sem)
        mine.start()
        first = [copy(0, me, sibling, src=x_ref)]
        first += [copy(1 + j, me, (*chip, c), src=x_ref) for j, chip in enumerate(chips)]
        for cp in first:
            cp.start()
        passed = [copy(4 + j, (*chip, c), sibling) for j, chip in enumerate(chips)]
        for j, chip in enumerate(chips):
            copy(1 + j, (*chip, c), me).wait_recv()
            passed[j].start()
        copy(0, sibling, me).wait_recv()
        for j, chip in enumerate(chips):
            copy(4 + j, (*chip, 1 - c), me).wait_recv()
        for cp in first + passed:
            cp.wait_send()
        mine.wait()

    return pl.pallas_call(
        body, name=name, out_shape=_sds((8 * m, n), blk.dtype),
        in_specs=[pl.BlockSpec(memory_space=space)], out_specs=pl.BlockSpec(memory_space=space),
        scratch_shapes=[pltpu.SemaphoreType.DMA((7,)), pltpu.SemaphoreType.DMA((7,)),
                        pltpu.SemaphoreType.DMA],
        compiler_params=pltpu.CompilerParams(vmem_limit_bytes=VMEM_LIMIT))(blk)


def _hbm_specs(n):
    return [pl.BlockSpec(memory_space=pl.ANY)] * n


def _swap_halves_sibling(name, srcs):
    nw = len(srcs)

    def body(*refs):
        src_refs, dst_refs, send_sems, recv_sems = refs[:nw], refs[nw:2 * nw], refs[2 * nw], refs[2 * nw + 1]
        x, y, c = _place()
        cps = [pltpu.make_async_remote_copy(
            src_ref=src_refs[w].at[_whole(src_refs[w]), pl.ds(0, 4), 1 - c], dst_ref=dst_refs[w],
            send_sem=send_sems.at[w], recv_sem=recv_sems.at[w], device_id=(x, y, 1 - c),
            device_id_type=MESH) for w in range(nw)]
        for cp in cps:
            cp.start()
        for cp in cps:
            cp.wait()

    shapes = [s.shape[:2] + s.shape[3:] for s in srcs]
    return pl.pallas_call(
        body, name=name, out_shape=[_sds(sh, s.dtype) for sh, s in zip(shapes, srcs)],
        in_specs=_hbm_specs(nw), out_specs=_hbm_specs(nw),
        scratch_shapes=[pltpu.SemaphoreType.DMA((nw,)), pltpu.SemaphoreType.DMA((nw,))])(*srcs)


def _split_start(name, srcs, lands, copies, per_array):
    nw = len(srcs)

    def body(*refs):
        send_sems, recv_sems, token = refs[2 * nw], refs[2 * nw + 1], refs[-1]
        for w in range(nw):
            for k, (s, d, dev) in enumerate(copies(refs[w], refs[nw + w], *_place())):
                pltpu.make_async_remote_copy(
                    src_ref=s, dst_ref=d, send_sem=send_sems.at[w * per_array + k],
                    recv_sem=recv_sems.at[w * per_array + k], device_id=dev, device_id_type=MESH).start()
        token[...] = jnp.zeros_like(token)

    hbm, sem = pl.BlockSpec(memory_space=pltpu.HBM), pl.BlockSpec(memory_space=pltpu.SEMAPHORE)
    arrays = list(srcs) + list(lands)
    out = pl.pallas_call(
        body, name=name,
        out_shape=(pltpu.SemaphoreType.DMA((nw * per_array,)), pltpu.SemaphoreType.DMA((nw * per_array,)),
                   *[pltpu.HBM(a.shape, a.dtype) for a in arrays], _sds((8, 128), F32)),
        in_specs=[hbm] * (2 * nw),
        out_specs=(sem, sem, *[hbm] * (2 * nw), pl.BlockSpec(memory_space=pltpu.VMEM)),
        input_output_aliases={i: 2 + i for i in range(2 * nw)},
        compiler_params=pltpu.CompilerParams(has_side_effects=pltpu.SideEffectType.DATAFLOW_SIDE_EFFECTING))(
            *[pltpu.with_memory_space_constraint(a, pltpu.HBM) for a in arrays])
    return out[0], out[1], out[2:2 + nw], out[2 + nw:2 + 2 * nw], out[-1]


def _split_wait(name, started, after, waits, per_array):
    send_sems, recv_sems, srcs, lands, _ = started
    nw = len(srcs)

    def body(*refs):
        send_sems, recv_sems = refs[2 * nw], refs[2 * nw + 1]
        x, y, c = _place()
        for w in range(nw):
            for k, (s, d) in enumerate(waits(refs[w], refs[nw + w], x, y, c)):
                cp = pltpu.make_async_remote_copy(
                    src_ref=s, dst_ref=d, send_sem=send_sems.at[w * per_array + k],
                    recv_sem=recv_sems.at[w * per_array + k], device_id=(x, y, 1 - c),
                    device_id_type=MESH)
                cp.wait_send()
                cp.wait_recv()

    hbm, sem = pl.BlockSpec(memory_space=pltpu.HBM), pl.BlockSpec(memory_space=pltpu.SEMAPHORE)
    arrays = list(srcs) + list(lands)
    out = pl.pallas_call(
        body, name=name, out_shape=tuple(pltpu.HBM(a.shape, a.dtype) for a in arrays),
        in_specs=[hbm] * (2 * nw) + [sem, sem, pl.BlockSpec(memory_space=pl.ANY)],
        out_specs=tuple([hbm] * (2 * nw)), input_output_aliases={i: i for i in range(2 * nw)},
        compiler_params=pltpu.CompilerParams(has_side_effects=pltpu.SideEffectType.DATAFLOW_SIDE_EFFECTING))(
            *arrays, send_sems, recv_sems, after)
    return list(out[nw:])


def _other_chips(x, y):
    return [(1 - x, y), (x, 1 - y), (1 - x, 1 - y)]


def _whole(ref):
    return pl.ds(0, ref.shape[0])


def _gather_sends(src_ref, land_ref, x, y, c):
    to = land_ref.at[_whole(land_ref), 2 * x + y, c]
    return [(src_ref, to, (x, y, 1 - c))] + [(src_ref, to, (px, py, c)) for px, py in _other_chips(x, y)]


def _gather_lands(src_ref, land_ref, x, y, c):
    g = _whole(land_ref)
    return [(src_ref, land_ref.at[g, 2 * x + y, 1 - c])] + [
        (src_ref, land_ref.at[g, 2 * px + py, c]) for px, py in _other_chips(x, y)]


def _scatter_sends(src_ref, land_ref, x, y, c):
    g = _whole(src_ref)
    return [(src_ref.at[g, 2 * px + py], land_ref.at[g, 2 * x + y], (px, py, c)) for px, py in _other_chips(x, y)]


def _scatter_lands(src_ref, land_ref, x, y, c):
    g = _whole(src_ref)
    return [(src_ref.at[g, 2 * x + y], land_ref.at[g, 2 * px + py]) for px, py in _other_chips(x, y)]


def _forward_sibling(name, lands):
    nw = len(lands)

    def body(*refs):
        land_refs, out_refs, send_sems, recv_sems = refs[:nw], refs[nw:2 * nw], refs[2 * nw], refs[2 * nw + 1]
        x, y, c = _place()
        cps = []
        for w in range(nw):
            g = _whole(land_refs[w])
            cps += [pltpu.make_async_remote_copy(
                src_ref=land_refs[w].at[g, 2 * px + py, c], dst_ref=out_refs[w].at[g, 2 * px + py, c],
                send_sem=send_sems.at[w, j], recv_sem=recv_sems.at[w, j], device_id=(x, y, 1 - c),
                device_id_type=MESH) for j, (px, py) in enumerate(_other_chips(x, y))]
        for cp in cps:
            cp.start()
        for w in range(nw):
            g = _whole(land_refs[w])
            for j, (px, py) in enumerate(_other_chips(x, y)):
                slot = out_refs[w].at[g, 2 * px + py, 1 - c]
                pltpu.make_async_remote_copy(src_ref=slot, dst_ref=slot, send_sem=send_sems.at[w, j],
                                             recv_sem=recv_sems.at[w, j], device_id=(x, y, 1 - c),
                                             device_id_type=MESH).wait_recv()
        for cp in cps:
            cp.wait_send()

    return pl.pallas_call(
        body, name=name, out_shape=[_sds(a.shape, a.dtype) for a in lands],
        in_specs=_hbm_specs(nw), out_specs=_hbm_specs(nw), input_output_aliases={i: i for i in range(nw)},
        scratch_shapes=[pltpu.SemaphoreType.DMA((nw, 3)), pltpu.SemaphoreType.DMA((nw, 3))])(*lands)


def _share_halves(name, halves):
    nw = len(halves)

    def body(*refs):
        in_refs, out_refs, send_sems, recv_sems = refs[:nw], refs[nw:2 * nw], refs[2 * nw], refs[2 * nw + 1]
        x, y, c = _place()
        cps = [pltpu.make_async_remote_copy(
            src_ref=in_refs[w].at[_whole(in_refs[w]), c], dst_ref=out_refs[w].at[_whole(in_refs[w]), c],
            send_sem=send_sems.at[w], recv_sem=recv_sems.at[w], device_id=(x, y, 1 - c),
            device_id_type=MESH) for w in range(nw)]
        for cp in cps:
            cp.start()
        for w in range(nw):
            slot = out_refs[w].at[_whole(in_refs[w]), 1 - c]
            pltpu.make_async_remote_copy(src_ref=slot, dst_ref=slot, send_sem=send_sems.at[w],
                                         recv_sem=recv_sems.at[w], device_id=(x, y, 1 - c),
                                         device_id_type=MESH).wait_recv()
        for cp in cps:
            cp.wait_send()

    return pl.pallas_call(
        body, name=name, out_shape=[_sds(a.shape, a.dtype) for a in halves],
        in_specs=_hbm_specs(nw), out_specs=_hbm_specs(nw), input_output_aliases={i: i for i in range(nw)},
        scratch_shapes=[pltpu.SemaphoreType.DMA((nw,)), pltpu.SemaphoreType.DMA((nw,))])(*halves)


def _pair_sum(name, full, got, core):
    g, _, _, m, n = full.shape

    def body(c_ref, a_ref, b_ref, o_ref):
        o_ref[...] = (a_ref[...].astype(F32) + b_ref[...].astype(F32)).astype(o_ref.dtype)

    slab = pl.BlockSpec((None, None, m, n), lambda w, s, c: (w, s, 0, 0))
    return pl.pallas_call(
        body, name=name,
        grid_spec=pltpu.PrefetchScalarGridSpec(
            num_scalar_prefetch=1, grid=(g, 4),
            in_specs=[pl.BlockSpec((None, None, None, m, n), lambda w, s, c: (w, s, c[0], 0, 0)), slab],
            out_specs=slab),
        out_shape=_sds(got.shape, BF16),
        compiler_params=_params(("parallel", "parallel")))(core, full, got)


def _sum4(name, a, core):
    g, _, m, n = a.shape

    def body(c_ref, a_ref, o_ref):
        acc = a_ref[0].astype(F32)
        for s in range(1, 4):
            acc = acc + a_ref[s].astype(F32)
        o_ref[...] = acc

    return pl.pallas_call(
        body, name=name,
        grid_spec=pltpu.PrefetchScalarGridSpec(
            num_scalar_prefetch=1, grid=(g,),
            in_specs=[pl.BlockSpec((None, 4, m, n), lambda w, c: (w, 0, 0, 0))],
            out_specs=pl.BlockSpec((None, None, m, n), lambda w, c: (w, c[0], 0, 0))),
        out_shape=_sds((g, 2, m, n), F32), compiler_params=_params(("parallel",)))(core, a)


def _small_update(gathered, w, m, v):
    n = w.shape[1]
    tn = 2048
    c1 = 1.0 - ADAM_B1 ** ADAM_STEP
    c2 = 1.0 - ADAM_B2 ** ADAM_STEP

    def body(g_ref, w_ref, m_ref, v_ref, go_ref, d_ref, mo_ref, vo_ref):
        g = g_ref[0:1, :]
        for d in range(1, 8):
            g = g + g_ref[d:d + 1, :]
        go_ref[...] = g
        m = ADAM_B1 * m_ref[...] + (1.0 - ADAM_B1) * g
        v = ADAM_B2 * v_ref[...] + (1.0 - ADAM_B2) * (g * g)
        mo_ref[...] = m
        vo_ref[...] = v
        d_ref[...] = -ADAM_LR * ((m / c1) / (jnp.sqrt(v / c2) + ADAM_EPS) + ADAM_WD * w_ref[...])

    row = pl.BlockSpec((1, tn), lambda i: (0, i))
    return pl.pallas_call(
        body, name="small_update", grid=(n // tn,),
        in_specs=[pl.BlockSpec((8, tn), lambda i: (0, i)), row, row, row], out_specs=[row] * 4,
        out_shape=[_sds((1, n), F32)] * 4, compiler_params=_params(("parallel",)))(gathered, w, m, v)


def _swiglu(ps, es):
    g, u = ps
    return g * _sigmoid(g) * u, g, u


def _swiglu_bwd(ps, es):
    g, u = es[0].astype(F32), es[1].astype(F32)
    sg = _sigmoid(g)
    return ps[0] * u * (sg * (1.0 + g * (1.0 - sg))), ps[0] * (g * sg)


def _merge(ps, es):
    ga, gm = [e.astype(F32) for e in es]
    return _sigmoid(ga) * ps[0] + _sigmoid(gm) * ps[1], ps[0], ps[1]


def _merge_bwd(ps, es):
    a, b, ga, gm = [e.astype(F32) for e in es]
    sa, sm = _sigmoid(ga), _sigmoid(gm)
    dm = ps[0]
    return dm * sa, dm * sm, dm * a * (sa * (1.0 - sa)), dm * b * (sm * (1.0 - sm))


W_IN_PIECES = (("q", 512), ("kv", 256), ("mqk", 1024), ("mv", 512), ("mo", 512), ("if", 8),
               ("ga", 1024), ("gm", 1024))


def _local_step(x, tgt, pos_col, mod, sp, in_weights, late_weights, ffn_grads, mixer_grads):
    sh_m, sc_m, gate_m, sh_f, sc_f, gate_f = mod
    inv = ROPE_THETA ** (-2.0 * jnp.arange(HEAD_DIM // 2, dtype=F32) / HEAD_DIM)
    cos, sin = _rope_tables(pos_col, jnp.tile(inv, 4).reshape(1, 128))
    W = dict(in_weights(cos))
    h, pa, pqk, pvo, pif, pg = _proj_in(x, sp["g_pre_mix"], sc_m, sh_m, [
        (jnp.concatenate([W["q"], W["kv"]]), BF16, 256), (W["mqk"], F32, 512),
        (jnp.concatenate([W["mv"], W["mo"]]), BF16, 512), (W["if"], F32, 128),
        (jnp.concatenate([W["ga"], W["gm"]]), BF16, 512)])
    ya = _attn_fwd(pa, cos, sin, sp["sinks"])
    qk = _conv_fwd(pqk, sp["conv_w"], sp["conv_b"])
    bcol = jnp.pad(sp["b_if"], ((0, 0), (0, 120)))
    brow = jnp.broadcast_to(sp["b_if"].reshape(8, 1), (8, 128))
    grow = pif[:, :8].T
    hm, cs, ns, ms = _mlstm_fwd(qk, pvo, pif, bcol, grow, brow)
    ym = _mlstm_out(hm, pvo, sp["norm_w"])
    W.update(late_weights(ym))
    w_fg, w_fu, w_fd = W["fg"], W["fu"], W["fd"]
    merged, br_a, br_m = _mm("branches", [[(ya, W["ba"])], [(ym, W["bm"])]],
                             [(pg, 0), (pg, 1)], _merge, [BF16, BF16, BF16], cn=512, nt=True)
    wide, narrow = (D_MODEL, F32), (D_MODEL, BF16)
    mix, x1, h2 = _mm_rows("mix_out", [[(merged, W["out"])]],
                           [x, gate_m, sp["g_post_mix"], sp["g_pre_ffn"], sc_f, sh_f],
                           _res_norm_rows, [wide, wide, narrow], [], cn=512)
    act, gt, up = _mm("ffn_in", [[(h2, w_fg)], [(h2, w_fu)]], [], _swiglu, [BF16] * 3,
                      cn=256, nt=True)
    dy, dff, acc_l, loss = _mm_rows("ffn_down", [[(act, w_fd)]], [x1, tgt, gate_f, sp["g_post_ffn"]],
                                    _final_loss_rows, [wide, narrow], [(8, D_MODEL), (1, 128)], cn=512)

    G = {}
    dgt, dup = _mm("ffn_down_bwd", [[(dff, w_fd)]], [gt, up], _swiglu_bwd, [BF16, BF16],
                   cn=256, nt=True)
    g_fd = _mm_tn("dw_ffn_down", act, dff, BF16, 1408, 512)
    g_fg = _mm_tn("dw_ffn_gate", dgt, h2, BF16, 1408, 1024)
    g_fu = _mm_tn("dw_ffn_up", dup, h2, BF16, 1408, 1024)
    tie = ffn_grads(g_fg, g_fu, g_fd)
    dx1, dmix, acc_r = _mm_rows(
        "ffn_in_bwd", [[(dgt, w_fg), (dup, w_fu)]],
        [x1, mix, dy, sc_f + tie, gate_m, sp["g_pre_ffn"], sp["g_post_mix"]],
        _res_norm_bwd_rows, [wide, narrow], [(8, D_MODEL)], cn=512, tm=256)
    d_a, d_m, dga, dgm = _mm("mix_out_bwd", [[(dmix, W["out"])]],
                             [br_a, br_m, (pg, 0), (pg, 1)], _merge_bwd,
                             [BF16] * 4, cn=512, nt=True)
    G["out"] = _mm_tn("dw_out", merged, dmix, BF16, 1024, 512)
    dya, = _mm("branch_attn_bwd", [[(d_a, W["ba"])]], [], _first, [F32], cn=512)
    dym, = _mm("branch_mlstm_bwd", [[(d_m, W["bm"])]], [], _first, [F32], cn=512)
    G["ba"] = _mm_tn("dw_branch_attn", d_a, ya, BF16, 1024, 512)
    G["bm"] = _mm_tn("dw_branch_mlstm", d_m, ym, BF16, 1024, 512)
    dhm, do_m, acc_n = _mlstm_out_bwd(hm, pvo, sp["norm_w"], dym)
    dqk, dv_m, dgc, dgr = _mlstm_bwd(qk, pvo, pif, bcol, grow, brow, cs, ns, ms, dhm)
    dif, acc_g = _gate_bwd(dgc, jnp.pad(dgr.T, ((0, 0), (0, 120))), pif, bcol)
    dpre, acc_c = _conv_bwd_pre(pqk, sp["conv_w"], sp["conv_b"], dqk)
    du = _conv_bwd_in(dpre, sp["conv_w"])
    dq_a, dcur, dprv, dsink = _attn_bwd(pa, cos, sin, sp["sinks"], dya)
    dkv = _attn_kv_combine(dcur, dprv, cos, sin)
    dproj = {"q": dq_a, "kv": dkv, "mqk": du, "mv": dv_m, "mo": do_m, "if": dif, "ga": dga, "gm": dgm}
    for k, _ in W_IN_PIECES:
        G[k] = _mm_tn("dw_in_" + k, dproj[k], h, BF16, dproj[k].shape[1], 1024)
    w_tied = dict(W, **{"if": W["if"] + mixer_grads(G).astype(BF16)})
    dx, acc_p = _mm_rows("proj_bwd", [[(dproj[k], w_tied[k]) for k, _ in W_IN_PIECES]],
                         [x, dx1, sp["g_pre_mix"], sc_m], _pre_norm_bwd_rows, [wide], [(8, D_MODEL)],
                         cn=512, tm=256)

    small = {
        "mod": jnp.concatenate([acc_p[1], acc_p[0], acc_r[3], acc_r[1], acc_r[0], acc_l[0]]),
        "g_pre_mix": acc_p[2], "g_post_mix": acc_r[4], "b_if": acc_g[0, :8],
        "conv_w": acc_c[:CONV_WIDTH].reshape(-1), "conv_b": acc_c[CONV_WIDTH],
        "sinks": dsink[:, 0], "norm_w": acc_n[0], "g_pre_ffn": acc_r[2], "g_post_ffn": acc_l[1]}
    return loss, dx, small


IN_WIDTH = sum(n for _, n in W_IN_PIECES)
IN_SHARD = IN_WIDTH // 4
IN_SHARD_PAD = -(-IN_SHARD // 32) * 32


def _split_w_in(w_in_t):
    out, off = {}, 0
    for k, n in W_IN_PIECES:
        out[k] = w_in_t[off:off + n]
        off += n
    out["if"] = jnp.pad(out["if"], ((0, 120), (0, 0)))
    return out


def _halves(a):
    return a.reshape(4, 2, a.shape[0] // 8, a.shape[1])


SMALL = (("b_ada", 6144), ("g_pre_mix", 1024), ("g_post_mix", 1024), ("b_if", 128), ("conv_w", 4096),
         ("conv_b", 1024), ("sinks", 128), ("norm_w", 512), ("g_pre_ffn", 1024), ("g_post_ffn", 1024))
SMALL_LEN = 8 * 2048


def _pack_small(vals):
    parts = []
    for k, n in SMALL:
        v = vals[k].reshape(-1)
        parts.append(jnp.pad(v, (0, n - v.shape[0])))
    flat = jnp.concatenate(parts)
    return jnp.pad(flat, (0, SMALL_LEN - flat.shape[0]))


def _unpack_small(flat, shapes):
    out, off = {}, 0
    for k, n in SMALL:
        size = 1
        for d in shapes[k]:
            size *= d
        out[k] = flat[off:off + size].reshape(shapes[k])
        off += n
    return out


def kernel(x, c, positions, w_ada, b_ada, g_pre_mix, g_post_mix, w_in, b_if, conv_w, conv_b, attn_sinks, mlstm_norm_w, w_branch_attn, w_branch_mlstm, w_out, g_pre_ffn, g_post_ffn, w_ffn_gate, w_ffn_up, w_ffn_down, loss_target, m_w_ada, m_b_ada, m_g_pre_mix, m_g_post_mix, m_w_in, m_b_if, m_conv_w, m_conv_b, m_attn_sinks, m_mlstm_norm_w, m_w_branch_attn, m_w_branch_mlstm, m_w_out, m_g_pre_ffn, m_g_post_ffn, m_w_ffn_gate, m_w_ffn_up, m_w_ffn_down, v_w_ada, v_b_ada, v_g_pre_mix, v_g_post_mix, v_w_in, v_b_if, v_conv_w, v_conv_b, v_attn_sinks, v_mlstm_norm_w, v_w_branch_attn, v_w_branch_mlstm, v_w_out, v_g_pre_ffn, v_g_post_ffn, v_w_ffn_gate, v_w_ffn_up, v_w_ffn_down):
    xi, yi, ci = _place()
    chip = 2 * xi + yi
    dev = 2 * chip + ci
    T = x.shape[1]
    ada_cols = w_ada.shape[2]

    def my_half(a):
        n = a.shape[0] // 2
        return lax.dynamic_slice_in_dim(a, ci * n, n, axis=0).astype(BF16)

    blk = jnp.concatenate([c.reshape(-1), conv_w.reshape(-1)]).reshape(8, 256)
    got = _all_gather8("gather_cond", blk, pltpu.VMEM).reshape(8, 2048)
    c_all = got[:, :D_MODEL].astype(BF16)
    conv_full = got[::2, D_MODEL:].reshape(4, CONV_WIDTH, -1).transpose(1, 0, 2).reshape(CONV_WIDTH, -1)

    b_sh = lax.dynamic_slice_in_dim(b_ada, chip * ada_cols, ada_cols, axis=1)
    mod_part, = _mm("ada_mod", [[(c_all, w_ada[0].astype(BF16))]], [b_sh],
                    lambda ps, es: (ps[0] + es[0],), [F32], cn=512, tm=8)
    mod_all = _all_gather8("gather_mod", mod_part, pltpu.VMEM).reshape(4, 2, 8, ada_cols)[:, 0]
    mod = lax.dynamic_index_in_dim(mod_all, dev, axis=1, keepdims=False).reshape(6, 1, D_MODEL)

    def gather_start(name, blks, after):
        blks, _ = lax.optimization_barrier((blks, after))
        lands = [lax.dynamic_update_slice(lax.empty((b.shape[0], 4, 2) + b.shape[1:], BF16),
                                          b[:, None, None], (0, chip, ci, 0, 0)) for b in blks]
        return _split_start(name + "_start", blks, lands, _gather_sends, 4)

    def gather_wait(name, started, after):
        return _forward_sibling(name + "_forward", _split_wait(name + "_wait", started, after, _gather_lands, 4))

    w_in_t = jnp.pad(w_in[0].T, ((0, IN_SHARD_PAD - IN_SHARD), (0, 0)))
    in_started = gather_start("in_gather", [my_half(w_in_t)[None]], mod)
    late_started = gather_start(
        "late_gather",
        [jnp.stack([my_half(w_ffn_gate[0].T), my_half(w_ffn_up[0].T), my_half(w_ffn_down[0])]),
         my_half(w_out[0])[None], jnp.stack([my_half(w_branch_attn[0].T), my_half(w_branch_mlstm[0].T)])],
        in_started[4])
    mod = mod + (in_started[4][0, 0] + late_started[4][0, 0])

    def in_weights(after):
        g_in, = gather_wait("in_gather", in_started, after)
        return _split_w_in(g_in.reshape(4, IN_SHARD_PAD, D_MODEL)[:, :IN_SHARD].reshape(IN_WIDTH, D_MODEL))

    def late_weights(after):
        g_ffn, g_out, g_br = gather_wait("late_gather", late_started, after)
        return {"fg": g_ffn[0].reshape(D_FF, D_MODEL), "fu": g_ffn[1].reshape(D_FF, D_MODEL),
                "fd": g_ffn[2].reshape(D_FF, D_MODEL), "out": g_out.reshape(D_MODEL, D_MODEL),
                "ba": g_br[0].reshape(D_MODEL, -1), "bm": g_br[1].reshape(D_MODEL, -1)}

    core = ci.reshape(1).astype(jnp.int32)
    sent = {}

    def scatter_start(name, groups):
        theirs = _swap_halves_sibling(name + "_pair", groups)
        pairs = [_pair_sum("%s_pair_sum_%d" % (name, i), a, b, core)
                 for i, (a, b) in enumerate(zip(groups, theirs))]
        sent[name] = _split_start(name + "_start", pairs, [p + jnp.zeros((), BF16) for p in pairs],
                                  _scatter_sends, 3)
        return sent[name][4][0, 0]

    def ffn_grads(g_fg, g_fu, g_fd):
        return scatter_start("rs_ffn", [jnp.stack([_halves(g_fg), _halves(g_fu), _halves(g_fd)])])

    def mixer_grads(G):
        g_in_t = jnp.concatenate([G[k][:n] for k, n in W_IN_PIECES]).reshape(4, IN_SHARD, D_MODEL)
        g_in_t = jnp.pad(g_in_t, ((0, 0), (0, IN_SHARD_PAD - IN_SHARD), (0, 0)))
        return scatter_start("rs_mix", [g_in_t.reshape(1, 4, 2, IN_SHARD_PAD // 2, D_MODEL),
                                        _halves(G["out"])[None],
                                        jnp.stack([_halves(G["ba"]), _halves(G["bm"])])])

    sp = {"g_pre_mix": g_pre_mix, "g_post_mix": g_post_mix, "b_if": b_if, "conv_w": conv_full,
          "conv_b": conv_b, "sinks": attn_sinks, "norm_w": mlstm_norm_w, "g_pre_ffn": g_pre_ffn,
          "g_post_ffn": g_post_ffn}
    loss, dx, small = _local_step(x[0], loss_target[0], positions.reshape(T, 1), [mod[i] for i in range(6)],
                                  sp, in_weights, late_weights, ffn_grads, mixer_grads)

    landed = (_split_wait("rs_ffn_wait", sent["rs_ffn"], dx, _scatter_lands, 3)
              + _split_wait("rs_mix_wait", sent["rs_mix"], dx, _scatter_lands, 3))
    reds = [_sum4("rs_chip_sum_%d" % i, a, core) for i, a in enumerate(landed)]
    s_ffn, s_in, s_out, s_br = [s.reshape(s.shape[0], -1, s.shape[-1]) for s in _share_halves("rs_share", reds)]
    gsh = {"fg": s_ffn[0], "fu": s_ffn[1], "fd": s_ffn[2], "w_in": s_in[0, :IN_SHARD],
           "out": s_out[0], "ba": s_br[0], "bm": s_br[1]}

    small["b_ada"] = small.pop("mod")
    vec = _pack_small(small).reshape(8, 2048)
    g_all = _all_gather8("gather_small", vec, pltpu.VMEM).reshape(8, SMALL_LEN)
    dmod_sh = lax.dynamic_slice_in_dim(g_all[:, :6 * D_MODEL], chip * ada_cols, ada_cols, axis=1)
    g_w_ada = _mm_tn("dw_ada", c_all, dmod_sh.astype(BF16), F32, D_MODEL, 512, 8)

    smalls = {"b_ada": (b_ada, m_b_ada, v_b_ada), "g_pre_mix": (g_pre_mix, m_g_pre_mix, v_g_pre_mix),
              "g_post_mix": (g_post_mix, m_g_post_mix, v_g_post_mix), "b_if": (b_if, m_b_if, v_b_if),
              "conv_w": None, "conv_b": (conv_b, m_conv_b, v_conv_b),
              "sinks": (attn_sinks, m_attn_sinks, v_attn_sinks),
              "norm_w": (mlstm_norm_w, m_mlstm_norm_w, v_mlstm_norm_w),
              "g_pre_ffn": (g_pre_ffn, m_g_pre_ffn, v_g_pre_ffn),
              "g_post_ffn": (g_post_ffn, m_g_post_ffn, v_g_post_ffn)}
    shapes = {k: (t[0].shape if t is not None else (1, CONV_WIDTH, D_MODEL)) for k, t in smalls.items()}
    zeros = jnp.zeros((CONV_WIDTH * D_MODEL,), F32)
    packs = [_pack_small({k: (t[i] if t is not None else zeros) for k, t in smalls.items()}).reshape(1, -1)
             for i in range(3)]
    s_out = [_unpack_small(o[0], shapes) for o in _small_update(g_all, *packs)]
    g_conv = lax.dynamic_slice_in_dim(s_out[0]["conv_w"], chip * conv_w.shape[2], conv_w.shape[2], axis=2)

    res = {}
    for k, t in smalls.items():
        if t is not None:
            res[k] = tuple(o[k] for o in s_out)
    res["conv_w"] = (g_conv, *[o[None] for o in _adamw("adam_conv_w", conv_w[0], g_conv[0], m_conv_w[0], v_conv_w[0])])
    res["w_ada"] = (g_w_ada[None], *[o[None] for o in _adamw("adam_w_ada", w_ada[0], g_w_ada, m_w_ada[0], v_w_ada[0])])
    bigs = {"w_in": (w_in, m_w_in, v_w_in), "ba": (w_branch_attn, m_w_branch_attn, v_w_branch_attn),
            "bm": (w_branch_mlstm, m_w_branch_mlstm, v_w_branch_mlstm), "out": (w_out, m_w_out, v_w_out),
            "fg": (w_ffn_gate, m_w_ffn_gate, v_w_ffn_gate), "fu": (w_ffn_up, m_w_ffn_up, v_w_ffn_up),
            "fd": (w_ffn_down, m_w_ffn_down, v_w_ffn_down)}
    for k, (w, m, v) in bigs.items():
        if k in ("w_in", "fg", "fu"):
            res[k] = tuple(o.T[None] for o in (gsh[k], *_adamw("adam_" + k, w[0].T, gsh[k], m[0].T, v[0].T)))
        else:
            g = gsh[k].T if k in ("ba", "bm") else gsh[k]
            res[k] = (g[None], *[o[None] for o in _adamw("adam_" + k, w[0], g, m[0], v[0])])

    order = ("w_ada", "b_ada", "g_pre_mix", "g_post_mix", "w_in", "b_if", "conv_w", "conv_b", "sinks",
             "norm_w", "ba", "bm", "out", "g_pre_ffn", "g_post_ffn", "fg", "fu", "fd")
    total = lax.psum(loss[0, 0], ("x", "y", "c"))
    return (total, dx[None], *[res[k][0] for k in order], *[res[k][1] for k in order],
            *[res[k][2] for k in order], *[res[k][3] for k in order])
```

```python
import functools

import jax
import jax.numpy as jnp
from jax import lax
from jax.experimental import pallas as pl
from jax.experimental.pallas import tpu as pltpu

F32, BF16 = jnp.float32, jnp.bfloat16
MESH = pl.DeviceIdType.MESH

D_MODEL = 1024
N_Q_HEADS, N_KV_HEADS, HEAD_DIM, WINDOW = 8, 2, 64, 128
ROPE_THETA = 10000.0
MLSTM_HEADS, MLSTM_HEAD_DIM, MLSTM_CHUNK, CONV_WIDTH = 4, 128, 64, 4
D_FF = 2816
NORM_EPS = 1e-6
ADAM_LR, ADAM_B1, ADAM_B2, ADAM_EPS, ADAM_WD, ADAM_STEP = 0.001, 0.9, 0.999, 1e-08, 0.01, 10

VMEM_LIMIT = 56 * 1024 * 1024
ROW_TILE = 256
MM_TM = 512
MM_TT = 1024
ATTN_BLK = WINDOW
STEP_ROWS = 2 * MLSTM_CHUNK
NEG_INF = float("-inf")


def _params(sem):
    return pltpu.CompilerParams(dimension_semantics=sem, vmem_limit_bytes=VMEM_LIMIT)


def _sds(shape, dtype):
    return jax.ShapeDtypeStruct(shape, dtype)


def _sigmoid(x):
    return 1.0 / (1.0 + jnp.exp(-x))


def _dot(a, b, ca, cb):
    return lax.dot_general(a, b, (((ca,), (cb,)), ((), ())), preferred_element_type=F32)


def _bdot(a, b, ca, cb):
    return lax.dot_general(a, b, (((ca,), (cb,)), ((0,), (0,))), preferred_element_type=F32)


def _bdot_rows(a, b):
    return jnp.stack([_dot(a[h], b[h], 0, 0) for h in range(a.shape[0])])


def _mm(name, prods, extras, epi, out_dtypes, cn, nt=False, tm=MM_TM):
    flat = [ab for p in prods for ab in p]
    counts = [len(p) for p in prods]
    M = flat[0][0].shape[0]
    N = flat[0][1].shape[0 if nt else 1]
    tm = min(tm, M)
    n_in = 2 * len(flat) + len(extras)

    def body(*refs):
        ins, outs = refs[:n_in], refs[n_in:]
        for j in range(N // cn):
            cols = slice(j * cn, (j + 1) * cn)
            k, ps = 0, []
            for cnt in counts:
                acc = None
                for _ in range(cnt):
                    b = ins[k + 1][cols, :] if nt else ins[k + 1][:, cols]
                    d = _dot(ins[k][...], b, 1, 1 if nt else 0)
                    acc = d if acc is None else acc + d
                    k += 2
                ps.append(acc)
            res = epi(ps, [r[:, cols] for r in ins[k:]])
            for o, r in zip(outs, res):
                o[:, cols] = r.astype(o.dtype)

    in_specs, args = [], []
    for a, b in flat:
        in_specs.append(pl.BlockSpec((tm, a.shape[1]), lambda i: (i, 0)))
        in_specs.append(pl.BlockSpec(b.shape, lambda i: (0, 0), pipeline_mode=pl.Buffered(1)))
        args += [a, b]
    for e in extras:
        e, off = e if isinstance(e, tuple) else (e, 0)
        rows = 1 if e.shape[0] == 1 else tm
        in_specs.append(pl.BlockSpec((rows, N), lambda i, off=off, rows=rows: (0 if rows == 1 else i, off)))
        args.append(e)
    return pl.pallas_call(
        body, name=name, grid=(M // tm,), in_specs=in_specs,
        out_specs=[pl.BlockSpec((tm, N), lambda i: (i, 0)) for _ in out_dtypes],
        out_shape=[_sds((M, N), dt) for dt in out_dtypes],
        compiler_params=_params(("parallel",)))(*args)


def _mm_rows(name, prods, extras, epi, outs, accs, cn, nt=False, tm=MM_TM):
    flat = [ab for p in prods for ab in p]
    counts = [len(p) for p in prods]
    M = flat[0][0].shape[0]
    N = flat[0][1].shape[0 if nt else 1]
    tm = min(tm, M)
    n_mm, n_in, n_out = 2 * len(flat), 2 * len(flat) + len(extras), len(outs)

    def body(*refs):
        ins, out_refs, acc_refs = refs[:n_in], refs[n_in:n_in + n_out], refs[n_in + n_out:]

        @pl.when(pl.program_id(0) == 0)
        def _():
            for a in acc_refs:
                a[...] = jnp.zeros_like(a)

        chunks = [[] for _ in counts]
        for j in range(N // cn):
            cols = slice(j * cn, (j + 1) * cn)
            k = 0
            for p, cnt in enumerate(counts):
                acc = None
                for _ in range(cnt):
                    b = ins[k + 1][cols, :] if nt else ins[k + 1][:, cols]
                    d = _dot(ins[k][...], b, 1, 1 if nt else 0)
                    acc = d if acc is None else acc + d
                    k += 2
                chunks[p].append(acc)
        ps = [c[0] if len(c) == 1 else jnp.concatenate(c, axis=1) for c in chunks]
        res, incs = epi(ps, [r[...] for r in ins[n_mm:]])
        for o, r in zip(out_refs, res):
            o[...] = r.astype(o.dtype)
        for a, inc in zip(acc_refs, incs):
            a[...] += inc

    in_specs, args = [], []
    for a, b in flat:
        in_specs.append(pl.BlockSpec((tm, a.shape[1]), lambda i: (i, 0)))
        in_specs.append(pl.BlockSpec(b.shape, lambda i: (0, 0), pipeline_mode=pl.Buffered(1)))
        args += [a, b]
    for e in extras:
        rows = 1 if e.shape[0] == 1 else tm
        in_specs.append(pl.BlockSpec((rows, e.shape[1]), lambda i, rows=rows: (0 if rows == 1 else i, 0)))
        args.append(e)
    return pl.pallas_call(
        body, name=name, grid=(M // tm,), in_specs=in_specs,
        out_specs=[pl.BlockSpec((tm, w), lambda i: (i, 0)) for w, _ in outs]
        + [pl.BlockSpec(s, lambda i: (0, 0)) for s in accs],
        out_shape=[_sds((M, w), dt) for w, dt in outs] + [_sds(s, F32) for s in accs],
        compiler_params=_params(("arbitrary",)))(*args)


def _mm_tn(name, a, b, out_dtype, tk, tn, tt=MM_TT):
    T, Ka = a.shape
    N = b.shape[1]
    tt = min(tt, T)
    steps = T // tt

    def body(a_ref, b_ref, o_ref, acc):
        t = pl.program_id(2)

        @pl.when(t == 0)
        def _():
            acc[...] = jnp.zeros_like(acc)

        acc[...] += _dot(a_ref[...], b_ref[...], 0, 0)

        @pl.when(t == steps - 1)
        def _():
            o_ref[...] = acc[...].astype(o_ref.dtype)

    return pl.pallas_call(
        body, name=name, grid=(Ka // tk, N // tn, steps),
        in_specs=[pl.BlockSpec((tt, tk), lambda i, j, t: (t, i)),
                  pl.BlockSpec((tt, tn), lambda i, j, t: (t, j))],
        out_specs=pl.BlockSpec((tk, tn), lambda i, j, t: (i, j)),
        out_shape=_sds((Ka, N), out_dtype),
        scratch_shapes=[pltpu.VMEM((tk, tn), F32)],
        compiler_params=_params(("parallel", "parallel", "arbitrary")))(a, b)


def _first(ps, es):
    return (ps[0],)


def _rows(name, body, ins, out_shapes, T, tr=ROW_TILE):
    tr = min(tr, T)

    def spec(shape):
        if shape[0] == T:
            return pl.BlockSpec((tr,) + tuple(shape[1:]), lambda i: (i,) + (0,) * (len(shape) - 1))
        return pl.BlockSpec(tuple(shape), lambda i: (0,) * len(shape))

    return pl.pallas_call(
        body, name=name, grid=(T // tr,),
        in_specs=[spec(a.shape) for a in ins], out_specs=[spec(s.shape) for s in out_shapes],
        out_shape=out_shapes, compiler_params=_params(("arbitrary",)))(*ins)


def _rms(x):
    r = lax.rsqrt(jnp.mean(x * x, axis=-1, keepdims=True) + NORM_EPS)
    return x * r, r


def _rms_bwd(dxn, xn, r):
    return r * (dxn - xn * jnp.mean(dxn * xn, axis=-1, keepdims=True))


def _colsum(v):
    return jnp.sum(v, axis=0, keepdims=True)


def _proj_in(x, g, sc, sh, groups):
    T = x.shape[0]
    tm = min(MM_TM, T)
    ng = len(groups)

    def body(x_ref, g_ref, sc_ref, sh_ref, *rest):
        w_refs, h_ref, out_refs = rest[:ng], rest[ng], rest[ng + 1:]
        xn, _ = _rms(x_ref[...])
        h = (xn * g_ref[...] * (1.0 + sc_ref[...]) + sh_ref[...]).astype(BF16)
        h_ref[...] = h
        for w_ref, o_ref, (w, _, cn) in zip(w_refs, out_refs, groups):
            for j in range(w.shape[0] // cn):
                cols = slice(j * cn, (j + 1) * cn)
                o_ref[:, cols] = _dot(h, w_ref[cols, :], 1, 1).astype(o_ref.dtype)

    row = pl.BlockSpec((1, D_MODEL), lambda i: (0, 0))
    tile = lambda w: pl.BlockSpec((tm, w), lambda i: (i, 0))
    return pl.pallas_call(
        body, name="proj_in", grid=(T // tm,),
        in_specs=[tile(D_MODEL), row, row, row] + [
            pl.BlockSpec(w.shape, lambda i: (0, 0), pipeline_mode=pl.Buffered(1)) for w, _, _ in groups],
        out_specs=[tile(D_MODEL)] + [tile(w.shape[0]) for w, _, _ in groups],
        out_shape=[_sds((T, D_MODEL), BF16)] + [_sds((T, w.shape[0]), dt) for w, dt, _ in groups],
        compiler_params=_params(("parallel",)))(x, g, sc, sh, *[w for w, _, _ in groups])


def _acc_rows(rows):
    w = rows[0].shape[1]
    return jnp.concatenate(rows + [jnp.zeros((8 - len(rows), w), F32)], axis=0)


def _res_norm_rows(ps, es):
    mix = ps[0]
    x, gate, gp, g2, sc, sh = es
    mh, _ = _rms(mix)
    x1 = x + gate * (mh * gp)
    xn, _ = _rms(x1)
    return [mix, x1, xn * g2 * (1.0 + sc) + sh], []


def _final_loss_rows(ps, es):
    x1, tgt, gate, gp = es
    fh, r = _rms(ps[0])
    e = x1 + gate * (fh * gp) - tgt
    loss = 0.5 * jnp.sum(jnp.mean(e * e, axis=-1, keepdims=True))
    dy = e * (1.0 / D_MODEL)
    acc = _acc_rows([_colsum(dy * fh * gp), _colsum(dy * gate * fh)])
    return [dy, _rms_bwd(dy * gate * gp, fh, r)], [acc, jnp.full((1, 128), loss, F32)]


def _res_norm_bwd_rows(ps, es):
    dh = ps[0]
    x1, mix, dy, sc, gate, g2, gp = es
    xn, r1 = _rms(x1)
    rows = [_colsum(dh * xn * g2), _colsum(dh), _colsum(dh * (1.0 + sc) * xn)]
    dx1 = dy + _rms_bwd(dh * (1.0 + sc) * g2, xn, r1)
    mh, rm = _rms(mix)
    rows += [_colsum(dx1 * mh * gp), _colsum(dx1 * gate * mh)]
    return [dx1, _rms_bwd(dx1 * gate * gp, mh, rm)], [_acc_rows(rows)]


def _pre_norm_bwd_rows(ps, es):
    dh = ps[0]
    x, dx1, g, sc = es
    xn, r = _rms(x)
    rows = [_colsum(dh * xn * g), _colsum(dh), _colsum(dh * (1.0 + sc) * xn)]
    return [dx1 + _rms_bwd(dh * (1.0 + sc) * g, xn, r)], [_acc_rows(rows)]


def _rope_tables(pos_col, inv_freq):
    T = pos_col.shape[0]

    def body(p_ref, f_ref, c_ref, s_ref):
        ang = p_ref[...].astype(F32) * f_ref[...]
        lane = lax.broadcasted_iota(jnp.int32, ang.shape, 1)
        c_ref[...] = jnp.cos(ang)
        s_ref[...] = jnp.where(lane % HEAD_DIM < HEAD_DIM // 2, -1.0, 1.0) * jnp.sin(ang)

    return _rows("rope_tables", body, [pos_col, inv_freq],
                 [_sds((T, 128), F32), _sds((T, 128), F32)], T, tr=512)


def _swap_halves(t):
    W = t.shape[1]
    lane = lax.broadcasted_iota(jnp.int32, t.shape, 1)
    half = HEAD_DIM // 2
    return jnp.where(lane % HEAD_DIM < half, pltpu.roll(t, W - half, 1), pltpu.roll(t, half, 1))


def _widen(c, W):
    return c if W == 128 else jnp.concatenate([c] * (W // 128), axis=1)


def _rope(t, c, s):
    W = t.shape[1]
    return t * _widen(c, W) + _swap_halves(t) * _widen(s, W)


def _unrope(dy, c, s):
    W = dy.shape[1]
    return dy * _widen(c, W) + _swap_halves(dy * _widen(s, W))


def _attn_mask(n):
    qi = lax.broadcasted_iota(jnp.int32, (ATTN_BLK, 2 * ATTN_BLK), 0)
    kj = lax.broadcasted_iota(jnp.int32, (ATTN_BLK, 2 * ATTN_BLK), 1)
    rel = kj - ATTN_BLK
    return (rel <= qi) & (qi - rel < WINDOW) & ((n > 0) | (kj >= ATTN_BLK))


def _attn_load(cur, prv, cc, sc, cp, sp):
    x, xp = cur[...], prv[...]
    q = _rope(x[:, :512], cc[...], sc[...]) * (HEAD_DIM ** -0.5)
    k = jnp.concatenate([_rope(xp[:, 512:640], cp[...], sp[...]),
                         _rope(x[:, 512:640], cc[...], sc[...])], axis=0)
    v = jnp.concatenate([xp[:, 640:768], x[:, 640:768]], axis=0)
    return q, k, v


ROLLED = tuple(h for h in range(N_Q_HEADS) if h % 2 != h // (N_Q_HEADS // N_KV_HEADS))


def _pair_heads(t):
    half = lax.broadcasted_iota(jnp.int32, (ATTN_BLK, 128), 1) // HEAD_DIM
    return jnp.stack([jnp.where(half == h % 2, t[:, 128 * (h // 2):128 * (h // 2) + 128], 0.0)
                      for h in range(N_Q_HEADS)])


def _kv_heads(t):
    half = lax.broadcasted_iota(jnp.int32, t.shape, 1) // HEAD_DIM
    tr = pltpu.roll(t, HEAD_DIM, 1)
    return jnp.stack([jnp.where(half == h % 2, tr if h in ROLLED else t, 0.0)
                      for h in range(N_Q_HEADS)])


def _sink_column(snk):
    return jnp.stack([jnp.full((1, 1), snk[0, h], F32) for h in range(N_Q_HEADS)])


def _attn_probs(qh, kh, mask, sink):
    s = jnp.where(mask, _bdot(qh, kh, 2, 2), NEG_INF)
    m = jnp.maximum(jnp.max(s, axis=-1, keepdims=True), sink)
    p = jnp.exp(s - m)
    es = jnp.exp(sink - m)
    rl = 1.0 / (jnp.sum(p, axis=-1, keepdims=True) + es)
    return p, es, rl


def _attn_specs(nb):
    blk = lambda w: pl.BlockSpec((ATTN_BLK, w), lambda n: (n, 0))
    prv = lambda w: pl.BlockSpec((ATTN_BLK, w), lambda n: (jnp.maximum(n - 1, 0), 0))
    return [blk(768), prv(768), blk(128), blk(128), prv(128), prv(128),
            pl.BlockSpec(memory_space=pltpu.SMEM)]


def _attn_fwd(pa, cos, sin, sinks):
    T = pa.shape[0]
    nb = T // ATTN_BLK

    def body(cur, prv, cc, sc, cp, sp, snk, y_ref):
        n = pl.program_id(0)
        q, k, v = _attn_load(cur, prv, cc, sc, cp, sp)
        qh, kh, vh = _pair_heads(q).astype(BF16), _kv_heads(k).astype(BF16), _kv_heads(v).astype(BF16)
        p, _, rl = _attn_probs(qh, kh, _attn_mask(n), _sink_column(snk))
        o = _bdot(p.astype(BF16), vh, 2, 1) * rl
        for pair in range(N_Q_HEADS // 2):
            y_ref[:, 128 * pair:128 * pair + 128] = (o[2 * pair] + o[2 * pair + 1]).astype(BF16)

    return pl.pallas_call(
        body, name="attn_fwd", grid=(nb,), in_specs=_attn_specs(nb),
        out_specs=pl.BlockSpec((ATTN_BLK, 512), lambda n: (n, 0)),
        out_shape=_sds((T, 512), BF16), compiler_params=_params(("parallel",)))(
            pa, pa, cos, sin, cos, sin, sinks)


def _attn_bwd(pa, cos, sin, sinks, dy):
    T = pa.shape[0]
    nb = T // ATTN_BLK

    def body(cur, prv, cc, sc, cp, sp, snk, dy_ref, dq_ref, dcur_ref, dprv_ref, dsink_ref):
        n = pl.program_id(0)

        @pl.when(n == 0)
        def _():
            dsink_ref[...] = jnp.zeros_like(dsink_ref)

        q, k, v = _attn_load(cur, prv, cc, sc, cp, sp)
        qh, kh, vh = _pair_heads(q).astype(BF16), _kv_heads(k).astype(BF16), _kv_heads(v).astype(BF16)
        p, es, rl = _attn_probs(qh, kh, _attn_mask(n), _sink_column(snk))
        pn = p * rl
        do = _pair_heads(dy_ref[...]).astype(BF16)
        dp = _bdot(do, vh, 2, 2)
        delta = jnp.sum(pn * dp, axis=-1, keepdims=True)
        ds = (pn * (dp - delta)).astype(BF16)
        dsink = es * rl * delta
        dq = _bdot(ds, kh, 2, 1) * (HEAD_DIM ** -0.5)
        dkh = _bdot_rows(ds, qh)
        dvh = _bdot_rows(pn.astype(BF16), do)

        def fold(t):
            same = [t[h] for h in range(N_Q_HEADS) if h not in ROLLED]
            moved = [t[h] for h in ROLLED]
            return sum(same[1:], same[0]) + pltpu.roll(sum(moved[1:], moved[0]), HEAD_DIM, 1)

        dk, dv = fold(dkh), fold(dvh)
        for h in range(N_Q_HEADS):
            dsink_ref[h:h + 1, :] += -jnp.sum(dsink[h])
        for pair in range(N_Q_HEADS // 2):
            dq_ref[:, 128 * pair:128 * pair + 128] = _unrope(
                dq[2 * pair] + dq[2 * pair + 1], cc[...], sc[...]).astype(BF16)
        dcur_ref[:, 0:128] = dk[ATTN_BLK:]
        dcur_ref[:, 128:256] = dv[ATTN_BLK:]
        dprv_ref[:, 0:128] = dk[:ATTN_BLK]
        dprv_ref[:, 128:256] = dv[:ATTN_BLK]

    blk = lambda w: pl.BlockSpec((ATTN_BLK, w), lambda n: (n, 0))
    return pl.pallas_call(
        body, name="attn_bwd", grid=(nb,), in_specs=_attn_specs(nb) + [blk(512)],
        out_specs=[blk(512), blk(256), blk(256), pl.BlockSpec((8, 128), lambda n: (0, 0))],
        out_shape=[_sds((T, 512), BF16), _sds((T, 256), F32), _sds((T, 256), F32),
                   _sds((8, 128), F32)],
        compiler_params=_params(("arbitrary",)))(pa, pa, cos, sin, cos, sin, sinks, dy)


def _attn_kv_combine(dcur, dprv, cos, sin):
    T = dcur.shape[0]
    nb = T // ATTN_BLK

    def body(c_ref, p_ref, cc, sc, o_ref):
        n = pl.program_id(0)
        t = c_ref[...] + jnp.where(n < nb - 1, p_ref[...], 0.0)
        o_ref[:, 0:128] = _unrope(t[:, 0:128], cc[...], sc[...]).astype(BF16)
        o_ref[:, 128:256] = t[:, 128:256].astype(BF16)

    blk = lambda w: pl.BlockSpec((ATTN_BLK, w), lambda n: (n, 0))
    nxt = pl.BlockSpec((ATTN_BLK, 256), lambda n: (jnp.minimum(n + 1, nb - 1), 0))
    return pl.pallas_call(
        body, name="attn_kv_combine", grid=(nb,), in_specs=[blk(256), nxt, blk(128), blk(128)],
        out_specs=blk(256), out_shape=_sds((T, 256), BF16),
        compiler_params=_params(("parallel",)))(dcur, dprv, cos, sin)


CONV_COLS = 2 * MLSTM_HEADS * MLSTM_HEAD_DIM


def _conv_pre(cur_ref, halo_ref, w_ref, b_ref, i, tr):
    xx = jnp.concatenate([jnp.where(i > 0, halo_ref[...], 0.0), cur_ref[...]], axis=0)
    taps = [(pltpu.roll(xx, CONV_WIDTH - 1 - j, 0) if j < CONV_WIDTH - 1 else xx)[8:8 + tr]
            for j in range(CONV_WIDTH)]
    pre = b_ref[...]
    for j in range(CONV_WIDTH):
        pre = pre + taps[j] * w_ref[j:j + 1, :]
    return pre, taps


def _conv_specs(T, tr):
    return [pl.BlockSpec((tr, CONV_COLS), lambda i: (i, 0)),
            pl.BlockSpec((8, CONV_COLS), lambda i: (jnp.maximum(i * (tr // 8) - 1, 0), 0)),
            pl.BlockSpec((CONV_WIDTH, CONV_COLS), lambda i: (0, 0)),
            pl.BlockSpec((1, CONV_COLS), lambda i: (0, 0))]


def _conv_fwd(pm, w, b):
    T = pm.shape[0]
    tr = min(ROW_TILE, T)

    def body(cur_ref, halo_ref, w_ref, b_ref, o_ref):
        pre, _ = _conv_pre(cur_ref, halo_ref, w_ref, b_ref, pl.program_id(0), tr)
        o_ref[...] = pre * _sigmoid(pre)

    return pl.pallas_call(
        body, name="conv_fwd", grid=(T // tr,), in_specs=_conv_specs(T, tr),
        out_specs=pl.BlockSpec((tr, CONV_COLS), lambda i: (i, 0)),
        out_shape=_sds((T, CONV_COLS), F32), compiler_params=_params(("parallel",)))(pm, pm, w, b)


def _conv_bwd_pre(pm, w, b, dqk):
    T = pm.shape[0]
    tr = min(ROW_TILE, T)

    def body(cur_ref, halo_ref, w_ref, b_ref, d_ref, dpre_ref, acc_ref):
        i = pl.program_id(0)

        @pl.when(i == 0)
        def _():
            acc_ref[...] = jnp.zeros_like(acc_ref)

        pre, taps = _conv_pre(cur_ref, halo_ref, w_ref, b_ref, i, tr)
        sg = _sigmoid(pre)
        dpre = d_ref[...] * (sg * (1.0 + pre * (1.0 - sg)))
        dpre_ref[...] = dpre
        for j in range(CONV_WIDTH):
            acc_ref[j:j + 1, :] += _colsum(dpre * taps[j])
        acc_ref[CONV_WIDTH:CONV_WIDTH + 1, :] += _colsum(dpre)

    return pl.pallas_call(
        body, name="conv_bwd_pre", grid=(T // tr,),
        in_specs=_conv_specs(T, tr) + [pl.BlockSpec((tr, CONV_COLS), lambda i: (i, 0))],
        out_specs=[pl.BlockSpec((tr, CONV_COLS), lambda i: (i, 0)),
                   pl.BlockSpec((8, CONV_COLS), lambda i: (0, 0))],
        out_shape=[_sds((T, CONV_COLS), F32), _sds((8, CONV_COLS), F32)],
        compiler_params=_params(("arbitrary",)))(pm, pm, w, b, dqk)


def _conv_bwd_in(dpre, w):
    T = dpre.shape[0]
    tr = min(ROW_TILE, T)
    nt = T // tr

    def body(cur_ref, halo_ref, w_ref, o_ref):
        i = pl.program_id(0)
        yy = jnp.concatenate([cur_ref[...], jnp.where(i < nt - 1, halo_ref[...], 0.0)], axis=0)
        du = cur_ref[...] * w_ref[CONV_WIDTH - 1:CONV_WIDTH, :]
        for j in range(CONV_WIDTH - 1):
            k = CONV_WIDTH - 1 - j
            du = du + pltpu.roll(yy, tr + 8 - k, 0)[:tr] * w_ref[j:j + 1, :]
        o_ref[...] = du.astype(BF16)

    return pl.pallas_call(
        body, name="conv_bwd_in", grid=(nt,),
        in_specs=[pl.BlockSpec((tr, CONV_COLS), lambda i: (i, 0)),
                  pl.BlockSpec((8, CONV_COLS),
                               lambda i: (jnp.minimum((i + 1) * (tr // 8), T // 8 - 1), 0)),
                  pl.BlockSpec((CONV_WIDTH, CONV_COLS), lambda i: (0, 0))],
        out_specs=pl.BlockSpec((tr, CONV_COLS), lambda i: (i, 0)),
        out_shape=_sds((T, CONV_COLS), BF16), compiler_params=_params(("parallel",)))(dpre, dpre, w)


def _log_sigmoid(x):
    return jnp.minimum(x, 0.0) - jnp.log1p(jnp.exp(-jnp.abs(x)))


def _chunk_cumsum(x, axis):
    idx = lax.broadcasted_iota(jnp.int32, x.shape, axis) % MLSTM_CHUNK
    k = 1
    while k < MLSTM_CHUNK:
        x = x + jnp.where(idx >= k, pltpu.roll(x, k, axis), 0.0)
        k *= 2
    return x


def _chunk_rev_cumsum(x, axis):
    n = x.shape[axis]
    idx = lax.broadcasted_iota(jnp.int32, x.shape, axis) % MLSTM_CHUNK
    k = 1
    while k < MLSTM_CHUNK:
        x = x + jnp.where(idx < MLSTM_CHUNK - k, pltpu.roll(x, n - k, axis), 0.0)
        k *= 2
    return x


def _mlstm_gates(gc_ref, bc_ref, gr_ref, br_ref):
    gc = gc_ref[...] + bc_ref[...]
    gr = gr_ref[...] + br_ref[...]
    return gc, _chunk_cumsum(_log_sigmoid(gc), 0), gr, _chunk_cumsum(_log_sigmoid(gr), 1)


def _heads(ref, base=0):
    D = MLSTM_HEAD_DIM
    return jnp.stack([ref[:, base + D * h:base + D * h + D] for h in range(MLSTM_HEADS)])


def _mlstm_inputs(q_ref, k_ref, v_ref, gc, bc, gr, br):
    H = MLSTM_HEADS
    q, v = _heads(q_ref), _heads(v_ref)
    ks = _heads(k_ref) * (MLSTM_HEAD_DIM ** -0.5)
    return dict(
        q=q, ks=ks, qb=q.astype(BF16), kb=ks.astype(BF16), vb=v.astype(BF16),
        b_col=jnp.stack([bc[:, H + h:H + h + 1] for h in range(H)]),
        i_col=jnp.stack([gc[:, h:h + 1] for h in range(H)]),
        b_row=jnp.stack([br[H + h:H + h + 1, :] for h in range(H)]),
        i_row=jnp.stack([gr[h:h + 1, :] for h in range(H)]))


def _mlstm_head(f, c_prev, n_prev, m_prev):
    L = MLSTM_CHUNK
    q, qb = f["q"], f["qb"]
    t = lax.broadcasted_iota(jnp.int32, (1, 2 * L, 2 * L), 1)
    s = lax.broadcasted_iota(jnp.int32, (1, 2 * L, 2 * L), 2)
    mask = (t // L == s // L) & (s <= t)
    d = jnp.where(mask, f["b_col"] - f["b_row"] + f["i_row"], NEG_INF)
    row = lax.broadcasted_iota(jnp.int32, (1, 2 * L, 1), 1)
    inter = f["b_col"] + jnp.where(row < L, m_prev[0], m_prev[1])
    m_t = jnp.maximum(inter, jnp.max(d, axis=-1, keepdims=True))
    w_intra = jnp.exp(d - m_t)
    w_inter = jnp.exp(inter - m_t)
    sc = _bdot(qb, f["kb"], 2, 2) * w_intra
    qc = jnp.concatenate([_bdot(qb[:, :L], c_prev[0].astype(BF16), 2, 1),
                          _bdot(qb[:, L:], c_prev[1].astype(BF16), 2, 1)], axis=1)
    qn = jnp.concatenate([jnp.sum(q[:, :L] * n_prev[0], axis=-1, keepdims=True),
                          jnp.sum(q[:, L:] * n_prev[1], axis=-1, keepdims=True)], axis=1)
    num = _bdot(sc.astype(BF16), f["vb"], 2, 1) + w_inter * qc
    den = jnp.sum(sc, axis=-1, keepdims=True) + w_inter * qn
    return dict(f, w_intra=w_intra, w_inter=w_inter, sc=sc, qc=qc, qn=qn, num=num, den=den,
                floor=jnp.exp(-m_t))


def _mlstm_update(f, ch, c, n, m):
    L = MLSTM_CHUNK
    rows = slice(L * ch, L * ch + L)
    b_col = f["b_col"][:, rows]
    g_last = b_col[:, L - 1:L]
    a_col = g_last - b_col + f["i_col"][:, rows]
    m_new = jnp.maximum(g_last + m, jnp.max(a_col, axis=1, keepdims=True))
    decay = jnp.exp(g_last + m - m_new)
    e_a = jnp.exp(a_col - m_new)
    kw = f["ks"][:, rows] * e_a
    c_new = decay * c + _bdot_rows(kw.astype(BF16), f["vb"][:, rows])
    n_new = decay * n + jnp.sum(kw, axis=1, keepdims=True)
    return c_new, n_new, m_new, decay, e_a, kw


def _mlstm_specs(T, order):
    blk = lambda w, col: pl.BlockSpec((STEP_ROWS, w), lambda s: (order(s), col))
    return [blk(512, 0), blk(512, 1), blk(512, 0), blk(128, 0),
            pl.BlockSpec((1, 128), lambda s: (0, 0)),
            pl.BlockSpec((8, STEP_ROWS), lambda s: (0, order(s))),
            pl.BlockSpec((8, 128), lambda s: (0, 0))]


def _lanes(m):
    return jnp.broadcast_to(m, m.shape[:-1] + (128,))


def _mlstm_fwd(qk, pm, gcol, bcol, grow, brow):
    T = qk.shape[0]
    steps = T // STEP_ROWS
    H, D = MLSTM_HEADS, MLSTM_HEAD_DIM

    def body(q_ref, k_ref, v_ref, gc_ref, bc_ref, gr_ref, br_ref, h_ref, cs_ref, ns_ref, ms_ref,
             c_st, n_st, m_st):
        @pl.when(pl.program_id(0) == 0)
        def _():
            c_st[...] = jnp.zeros_like(c_st)
            n_st[...] = jnp.zeros_like(n_st)
            m_st[...] = jnp.zeros_like(m_st)

        f = _mlstm_inputs(q_ref, k_ref, v_ref, *_mlstm_gates(gc_ref, bc_ref, gr_ref, br_ref))
        c0, n0, m0 = c_st[...], n_st[...], m_st[:, :, 0:1]
        c1, n1, m1, _, _, _ = _mlstm_update(f, 0, c0, n0, m0)
        c2, n2, m2, _, _, _ = _mlstm_update(f, 1, c1, n1, m1)
        f = _mlstm_head(f, (c0, c1), (n0, n1), (m0, m1))
        h = f["num"] / jnp.maximum(jnp.abs(f["den"]), f["floor"])
        for hd in range(H):
            h_ref[:, D * hd:D * hd + D] = h[hd]
        cs_ref[0], cs_ref[1] = c0, c1
        ns_ref[0], ns_ref[1] = n0, n1
        ms_ref[0], ms_ref[1] = _lanes(m0), _lanes(m1)
        c_st[...], n_st[...], m_st[...] = c2, n2, _lanes(m2)

    vec = pl.BlockSpec((2, H, 1, 128), lambda s: (s, 0, 0, 0))
    return pl.pallas_call(
        body, name="mlstm_fwd", grid=(steps,), in_specs=_mlstm_specs(T, lambda s: s),
        out_specs=[pl.BlockSpec((STEP_ROWS, 512), lambda s: (s, 0)),
                   pl.BlockSpec((2, H, 128, 128), lambda s: (s, 0, 0, 0)), vec, vec],
        out_shape=[_sds((T, 512), F32), _sds((2 * steps, H, 128, 128), F32),
                   _sds((2 * steps, H, 1, 128), F32), _sds((2 * steps, H, 1, 128), F32)],
        scratch_shapes=[pltpu.VMEM((H, 128, 128), F32), pltpu.VMEM((H, 1, 128), F32),
                        pltpu.VMEM((H, 1, 128), F32)],
        compiler_params=_params(("arbitrary",)))(qk, qk, pm, gcol, bcol, grow, brow)


def _mlstm_bwd(qk, pm, gcol, bcol, grow, brow, cs, ns, ms, dh):
    T = qk.shape[0]
    steps = T // STEP_ROWS
    H, L, D = MLSTM_HEADS, MLSTM_CHUNK, MLSTM_HEAD_DIM
    rev = lambda s: steps - 1 - s

    def body(q_ref, k_ref, v_ref, gc_ref, bc_ref, gr_ref, br_ref, cs_ref, ns_ref, ms_ref, dh_ref,
             dqk_ref, dv_ref, dgc_ref, dgr_ref, dc_st, dn_st):
        @pl.when(pl.program_id(0) == 0)
        def _():
            dc_st[...] = jnp.zeros_like(dc_st)
            dn_st[...] = jnp.zeros_like(dn_st)

        f = _mlstm_inputs(q_ref, k_ref, v_ref, *_mlstm_gates(gc_ref, bc_ref, gr_ref, br_ref))
        c_prev = (cs_ref[0], cs_ref[1])
        n_prev = (ns_ref[0], ns_ref[1])
        m_prev = (ms_ref[0, :, :, 0:1], ms_ref[1, :, :, 0:1])
        f = _mlstm_head(f, c_prev, n_prev, m_prev)
        big = jnp.abs(f["den"]) > f["floor"]
        rden = 1.0 / jnp.where(big, jnp.abs(f["den"]), f["floor"])
        dnum = _heads(dh_ref) * rden
        hdh = jnp.sum(f["num"] * dnum, axis=-1, keepdims=True)
        dden = jnp.where(big, -hdh * rden * jnp.sign(f["den"]), 0.0)
        dnum_b = dnum.astype(BF16)
        dsc = _bdot(dnum_b, f["vb"], 2, 2) + dden
        g = dsc * f["sc"]
        dv = _bdot_rows(f["sc"].astype(BF16), dnum_b)
        dqk_ = (dsc * f["w_intra"]).astype(BF16)
        dq = _bdot(dqk_, f["kb"], 2, 1)
        dks = _bdot_rows(dqk_, f["qb"])
        wdn = f["w_inter"] * dnum
        wdn_b = wdn.astype(BF16)
        wdd = f["w_inter"] * dden
        u = jnp.sum(f["qc"] * wdn, axis=-1, keepdims=True) + wdd * f["qn"]
        dks_s, dv_s, z_s, dg_s = [None, None], [None, None], [None, None], [None, None]
        dcn, dnn = dc_st[...], dn_st[...]
        for ch in (1, 0):
            rows = slice(L * ch, L * ch + L)
            _, _, _, decay, e_a, kw = _mlstm_update(f, ch, c_prev[ch], n_prev[ch], m_prev[ch])
            dcn_b = dcn.astype(BF16)
            dkw = _bdot(f["vb"][:, rows], dcn_b, 2, 2) + dnn
            dks_s[ch] = e_a * dkw
            dv_s[ch] = _bdot(kw.astype(BF16), dcn_b, 2, 1)
            z_s[ch] = e_a * jnp.sum(f["ks"][:, rows] * dkw, axis=-1, keepdims=True)
            dg_s[ch] = jnp.sum(z_s[ch], axis=1, keepdims=True) + decay * (
                jnp.sum(c_prev[ch] * dcn, axis=(1, 2), keepdims=True)
                + jnp.sum(n_prev[ch] * dnn, axis=(1, 2), keepdims=True))
            dcn = decay * dcn + _bdot_rows(f["qb"][:, rows], wdn_b[:, rows])
            dnn = decay * dnn + jnp.sum(wdd[:, rows] * f["q"][:, rows], axis=1, keepdims=True)
        dc_st[...], dn_st[...] = dcn, dnn
        dq = dq + jnp.concatenate(
            [_bdot(wdn_b[:, :L], c_prev[0].astype(BF16), 2, 2) + wdd[:, :L] * n_prev[0],
             _bdot(wdn_b[:, L:], c_prev[1].astype(BF16), 2, 2) + wdd[:, L:] * n_prev[1]], axis=1)
        dks = (dks + jnp.concatenate(dks_s, axis=1)) * (D ** -0.5)
        dv = dv + jnp.concatenate(dv_s, axis=1)
        z = jnp.concatenate(z_s, axis=1)
        row = lax.broadcasted_iota(jnp.int32, (1, STEP_ROWS, 1), 1)
        dg_col = jnp.where(row == L - 1, dg_s[0], 0.0) + jnp.where(row == 2 * L - 1, dg_s[1], 0.0)
        db_col = jnp.sum(g, axis=-1, keepdims=True) + u - z + dg_col
        g_row = jnp.sum(g, axis=1, keepdims=True)
        lane = lax.broadcasted_iota(jnp.int32, (STEP_ROWS, 128), 1)
        sub = lax.broadcasted_iota(jnp.int32, (8, STEP_ROWS), 0)
        dgc = jnp.zeros((STEP_ROWS, 128), F32)
        dgr = jnp.zeros((8, STEP_ROWS), F32)
        for hd in range(H):
            dgc = dgc + jnp.where(lane == hd, z[hd], 0.0) + jnp.where(lane == H + hd, db_col[hd], 0.0)
            dgr = dgr + jnp.where(sub == hd, g_row[hd], 0.0) - jnp.where(sub == H + hd, g_row[hd], 0.0)
            dqk_ref[:, D * hd:D * hd + D] = dq[hd]
            dqk_ref[:, H * D + D * hd:H * D + D * hd + D] = dks[hd]
            dv_ref[:, D * hd:D * hd + D] = dv[hd].astype(BF16)
        dgc_ref[...] = dgc
        dgr_ref[...] = dgr

    return pl.pallas_call(
        body, name="mlstm_bwd", grid=(steps,),
        in_specs=_mlstm_specs(T, rev) + [
            pl.BlockSpec((2, H, 128, 128), lambda s: (rev(s), 0, 0, 0)),
            pl.BlockSpec((2, H, 1, 128), lambda s: (rev(s), 0, 0, 0)),
            pl.BlockSpec((2, H, 1, 128), lambda s: (rev(s), 0, 0, 0)),
            pl.BlockSpec((STEP_ROWS, 512), lambda s: (rev(s), 0))],
        out_specs=[pl.BlockSpec((STEP_ROWS, 1024), lambda s: (rev(s), 0)),
                   pl.BlockSpec((STEP_ROWS, 512), lambda s: (rev(s), 0)),
                   pl.BlockSpec((STEP_ROWS, 128), lambda s: (rev(s), 0)),
                   pl.BlockSpec((8, STEP_ROWS), lambda s: (0, rev(s)))],
        out_shape=[_sds((T, 1024), F32), _sds((T, 512), BF16), _sds((T, 128), F32), _sds((8, T), F32)],
        scratch_shapes=[pltpu.VMEM((H, 128, 128), F32), pltpu.VMEM((H, 1, 128), F32)],
        compiler_params=_params(("arbitrary",)))(qk, qk, pm, gcol, bcol, grow, brow, cs, ns, ms, dh)


def _gate_bwd(dgc, dgr_t, gcol, bcol):
    T = dgc.shape[0]

    def body(a_ref, b_ref, g_ref, bias_ref, o_ref, acc_ref):
        @pl.when(pl.program_id(0) == 0)
        def _():
            acc_ref[...] = jnp.zeros_like(acc_ref)

        d = a_ref[...] + b_ref[...]
        lane = lax.broadcasted_iota(jnp.int32, d.shape, 1)
        is_f = (lane >= MLSTM_HEADS) & (lane < 2 * MLSTM_HEADS)
        dlogf = _chunk_rev_cumsum(jnp.where(is_f, d, 0.0), 0)
        out = jnp.where(is_f, dlogf * _sigmoid(-(g_ref[...] + bias_ref[...])), d)
        o_ref[...] = out.astype(BF16)
        acc_ref[0:1, :] += _colsum(out)

    return _rows("gate_bwd", body, [dgc, dgr_t, gcol, bcol],
                 [_sds((T, 128), BF16), _sds((8, 128), F32)], T)


def _head_norm(h, mu_axis=-1):
    mu = jnp.mean(h, axis=-1, keepdims=True)
    hc = h - mu
    r = lax.rsqrt(jnp.mean(hc * hc, axis=-1, keepdims=True) + NORM_EPS)
    return hc * r, r


def _mlstm_out(hm, pm, w):
    T = hm.shape[0]
    D = MLSTM_HEAD_DIM

    def body(h_ref, o_ref, w_ref, y_ref):
        for hd in range(MLSTM_HEADS):
            cols = slice(D * hd, D * hd + D)
            hn, _ = _head_norm(h_ref[:, cols])
            y_ref[:, cols] = (_sigmoid(o_ref[:, cols].astype(F32)) * hn * w_ref[:, cols]).astype(BF16)

    tr = min(ROW_TILE, T)
    return pl.pallas_call(
        body, name="mlstm_out", grid=(T // tr,),
        in_specs=[pl.BlockSpec((tr, 512), lambda i: (i, 0)), pl.BlockSpec((tr, 512), lambda i: (i, 1)),
                  pl.BlockSpec((1, 512), lambda i: (0, 0))],
        out_specs=pl.BlockSpec((tr, 512), lambda i: (i, 0)), out_shape=_sds((T, 512), BF16),
        compiler_params=_params(("parallel",)))(hm, pm, w)


def _mlstm_out_bwd(hm, pm, w, dy):
    T = hm.shape[0]
    D = MLSTM_HEAD_DIM
    tr = min(ROW_TILE, T)

    def body(h_ref, o_ref, w_ref, dy_ref, dh_ref, do_ref, acc_ref):
        @pl.when(pl.program_id(0) == 0)
        def _():
            acc_ref[...] = jnp.zeros_like(acc_ref)

        for hd in range(MLSTM_HEADS):
            cols = slice(D * hd, D * hd + D)
            hn, r = _head_norm(h_ref[:, cols])
            sg = _sigmoid(o_ref[:, cols].astype(F32))
            dy, w = dy_ref[:, cols], w_ref[:, cols]
            do_ref[:, cols] = (dy * hn * w * sg * (1.0 - sg)).astype(BF16)
            dyn = dy * sg
            acc_ref[0:1, cols] += _colsum(dyn * hn)
            dhn = dyn * w
            dh_ref[:, cols] = r * (dhn - jnp.mean(dhn, axis=-1, keepdims=True)
                                   - hn * jnp.mean(dhn * hn, axis=-1, keepdims=True))

    return pl.pallas_call(
        body, name="mlstm_out_bwd", grid=(T // tr,),
        in_specs=[pl.BlockSpec((tr, 512), lambda i: (i, 0)), pl.BlockSpec((tr, 512), lambda i: (i, 1)),
                  pl.BlockSpec((1, 512), lambda i: (0, 0)), pl.BlockSpec((tr, 512), lambda i: (i, 0))],
        out_specs=[pl.BlockSpec((tr, 512), lambda i: (i, 0)), pl.BlockSpec((tr, 512), lambda i: (i, 0)),
                   pl.BlockSpec((8, 512), lambda i: (0, 0))],
        out_shape=[_sds((T, 512), F32), _sds((T, 512), BF16), _sds((8, 512), F32)],
        compiler_params=_params(("arbitrary",)))(hm, pm, w, dy)


ADAM_TILE_ELEMS = 256 * 1024


def _adamw(name, w, g, m, v):
    R, C = w.shape
    fits = [t for t in range(8, R + 1, 8) if R % t == 0 and t * C <= ADAM_TILE_ELEMS]
    tr = fits[-1] if fits else R
    spec, grid = pl.BlockSpec((tr, C), lambda i: (i, 0)), (R // tr,)
    c1 = 1.0 - ADAM_B1 ** ADAM_STEP
    c2 = 1.0 - ADAM_B2 ** ADAM_STEP

    def body(w_ref, g_ref, m_ref, v_ref, d_ref, mo_ref, vo_ref):
        g = g_ref[...]
        m = ADAM_B1 * m_ref[...] + (1.0 - ADAM_B1) * g
        v = ADAM_B2 * v_ref[...] + (1.0 - ADAM_B2) * (g * g)
        mo_ref[...] = m
        vo_ref[...] = v
        d_ref[...] = -ADAM_LR * ((m / c1) / (jnp.sqrt(v / c2) + ADAM_EPS) + ADAM_WD * w_ref[...])

    return pl.pallas_call(
        body, name=name, grid=grid, in_specs=[spec] * 4, out_specs=[spec] * 3,
        out_shape=[_sds((R, C), F32)] * 3, compiler_params=_params(("parallel",)))(w, g, m, v)


def _place():
    return lax.axis_index("x"), lax.axis_index("y"), lax.axis_index("c")


def _all_gather8(name, blk, space):
    m, n = blk.shape

    def body(x_ref, out_ref, send_sems, recv_sems, local_sem):
        x, y, c = _place()
        me, sibling = (x, y, c), (x, y, 1 - c)
        chips = [(1 - x, y), (x, 1 - y), (1 - x, 1 - y)]

        def rows(px, py, pc):
            return out_ref.at[pl.ds((4 * px + 2 * py + pc) * m, m), :]

        def copy(k, block, to, src=None):
            return pltpu.make_async_remote_copy(
                src_ref=rows(*block) if src is None else src, dst_ref=rows(*block),
                send_sem=send_sems.at[k], recv_sem=recv_sems.at[k],
                device_id=to, device_id_type=MESH)

        mine = pltpu.make_async_copy(x_ref, rows(*me), local_sem)
        mine.start()
        first = [copy(0, me, sibling, src=x_ref)]
        first += [copy(1 + j, me, (*chip, c), src=x_ref) for j, chip in enumerate(chips)]
        for cp in first:
            cp.start()
        passed = [copy(4 + j, (*chip, c), sibling) for j, chip in enumerate(chips)]
        for j, chip in enumerate(chips):
            copy(1 + j, (*chip, c), me).wait_recv()
            passed[j].start()
        copy(0, sibling, me).wait_recv()
        for j, chip in enumerate(chips):
            copy(4 + j, (*chip, 1 - c), me).wait_recv()
        for cp in first + passed:
            cp.wait_send()
        mine.wait()

    return pl.pallas_call(
        body, name=name, out_shape=_sds((8 * m, n), blk.dtype),
        in_specs=[pl.BlockSpec(memory_space=space)], out_specs=pl.BlockSpec(memory_space=space),
        scratch_shapes=[pltpu.SemaphoreType.DMA((7,)), pltpu.SemaphoreType.DMA((7,)),
                        pltpu.SemaphoreType.DMA],
        compiler_params=pltpu.CompilerParams(vmem_limit_bytes=VMEM_LIMIT))(blk)


def _hbm_specs(n):
    return [pl.BlockSpec(memory_space=pl.ANY)] * n


def _swap_halves_sibling(name, srcs):
    nw = len(srcs)

    def body(*refs):
        src_refs, dst_refs, send_sems, recv_sems = refs[:nw], refs[nw:2 * nw], refs[2 * nw], refs[2 * nw + 1]
        x, y, c = _place()
        cps = [pltpu.make_async_remote_copy(
            src_ref=src_refs[w].at[pl.ds(0, 4), 1 - c], dst_ref=dst_refs[w],
            send_sem=send_sems.at[w], recv_sem=recv_sems.at[w], device_id=(x, y, 1 - c),
            device_id_type=MESH) for w in range(nw)]
        for cp in cps:
            cp.start()
        for cp in cps:
            cp.wait()

    return pl.pallas_call(
        body, name=name, out_shape=[_sds(s.shape[:1] + s.shape[2:], s.dtype) for s in srcs],
        in_specs=_hbm_specs(nw), out_specs=_hbm_specs(nw),
        scratch_shapes=[pltpu.SemaphoreType.DMA((nw,)), pltpu.SemaphoreType.DMA((nw,))])(*srcs)


def _split_start(name, srcs, lands, copies, per_array):
    nw = len(srcs)

    def body(*refs):
        send_sems, recv_sems, token = refs[2 * nw], refs[2 * nw + 1], refs[-1]
        for w in range(nw):
            for k, (s, d, dev) in enumerate(copies(refs[w], refs[nw + w], *_place())):
                pltpu.make_async_remote_copy(
                    src_ref=s, dst_ref=d, send_sem=send_sems.at[w * per_array + k],
                    recv_sem=recv_sems.at[w * per_array + k], device_id=dev, device_id_type=MESH).start()
        token[...] = jnp.zeros_like(token)

    hbm, sem = pl.BlockSpec(memory_space=pltpu.HBM), pl.BlockSpec(memory_space=pltpu.SEMAPHORE)
    arrays = list(srcs) + list(lands)
    out = pl.pallas_call(
        body, name=name,
        out_shape=(pltpu.SemaphoreType.DMA((nw * per_array,)), pltpu.SemaphoreType.DMA((nw * per_array,)),
                   *[pltpu.HBM(a.shape, a.dtype) for a in arrays], _sds((8, 128), F32)),
        in_specs=[hbm] * (2 * nw),
        out_specs=(sem, sem, *[hbm] * (2 * nw), pl.BlockSpec(memory_space=pltpu.VMEM)),
        input_output_aliases={i: 2 + i for i in range(2 * nw)},
        compiler_params=pltpu.CompilerParams(has_side_effects=pltpu.SideEffectType.DATAFLOW_SIDE_EFFECTING))(
            *[pltpu.with_memory_space_constraint(a, pltpu.HBM) for a in arrays])
    return out[0], out[1], out[2:2 + nw], out[2 + nw:2 + 2 * nw], out[-1]


def _split_wait(name, started, after, waits, per_array):
    send_sems, recv_sems, srcs, lands, _ = started
    nw = len(srcs)

    def body(*refs):
        send_sems, recv_sems = refs[2 * nw], refs[2 * nw + 1]
        x, y, c = _place()
        for w in range(nw):
            for k, (s, d) in enumerate(waits(refs[w], refs[nw + w], x, y, c)):
                cp = pltpu.make_async_remote_copy(
                    src_ref=s, dst_ref=d, send_sem=send_sems.at[w * per_array + k],
                    recv_sem=recv_sems.at[w * per_array + k], device_id=(x, y, 1 - c),
                    device_id_type=MESH)
                cp.wait_send()
                cp.wait_recv()

    hbm, sem = pl.BlockSpec(memory_space=pltpu.HBM), pl.BlockSpec(memory_space=pltpu.SEMAPHORE)
    arrays = list(srcs) + list(lands)
    out = pl.pallas_call(
        body, name=name, out_shape=tuple(pltpu.HBM(a.shape, a.dtype) for a in arrays),
        in_specs=[hbm] * (2 * nw) + [sem, sem, pl.BlockSpec(memory_space=pl.ANY)],
        out_specs=tuple([hbm] * (2 * nw)), input_output_aliases={i: i for i in range(2 * nw)},
        compiler_params=pltpu.CompilerParams(has_side_effects=pltpu.SideEffectType.DATAFLOW_SIDE_EFFECTING))(
            *arrays, send_sems, recv_sems, after)
    return list(out[nw:])


def _other_chips(x, y):
    return [(1 - x, y), (x, 1 - y), (1 - x, 1 - y)]


def _gather_sends(src_ref, land_ref, x, y, c):
    to = land_ref.at[2 * x + y, c]
    return [(src_ref, to, (x, y, 1 - c))] + [(src_ref, to, (px, py, c)) for px, py in _other_chips(x, y)]


def _gather_lands(src_ref, land_ref, x, y, c):
    return [(src_ref, land_ref.at[2 * x + y, 1 - c])] + [
        (src_ref, land_ref.at[2 * px + py, c]) for px, py in _other_chips(x, y)]


def _scatter_sends(src_ref, land_ref, x, y, c):
    return [(src_ref.at[2 * px + py], land_ref.at[2 * x + y], (px, py, c)) for px, py in _other_chips(x, y)]


def _scatter_lands(src_ref, land_ref, x, y, c):
    return [(src_ref.at[2 * x + y], land_ref.at[2 * px + py]) for px, py in _other_chips(x, y)]


def _forward_sibling(name, lands):
    nw = len(lands)

    def body(*refs):
        land_refs, out_refs, send_sems, recv_sems = refs[:nw], refs[nw:2 * nw], refs[2 * nw], refs[2 * nw + 1]
        x, y, c = _place()
        cps = []
        for w in range(nw):
            cps += [pltpu.make_async_remote_copy(
                src_ref=land_refs[w].at[2 * px + py, c], dst_ref=out_refs[w].at[2 * px + py, c],
                send_sem=send_sems.at[w, j], recv_sem=recv_sems.at[w, j], device_id=(x, y, 1 - c),
                device_id_type=MESH) for j, (px, py) in enumerate(_other_chips(x, y))]
        for cp in cps:
            cp.start()
        for w in range(nw):
            for j, (px, py) in enumerate(_other_chips(x, y)):
                slot = out_refs[w].at[2 * px + py, 1 - c]
                pltpu.make_async_remote_copy(src_ref=slot, dst_ref=slot, send_sem=send_sems.at[w, j],
                                             recv_sem=recv_sems.at[w, j], device_id=(x, y, 1 - c),
                                             device_id_type=MESH).wait_recv()
        for cp in cps:
            cp.wait_send()

    return pl.pallas_call(
        body, name=name, out_shape=[_sds(a.shape, a.dtype) for a in lands],
        in_specs=_hbm_specs(nw), out_specs=_hbm_specs(nw), input_output_aliases={i: i for i in range(nw)},
        scratch_shapes=[pltpu.SemaphoreType.DMA((nw, 3)), pltpu.SemaphoreType.DMA((nw, 3))])(*lands)


def _share_halves(name, halves):
    nw = len(halves)

    def body(*refs):
        in_refs, out_refs, send_sems, recv_sems = refs[:nw], refs[nw:2 * nw], refs[2 * nw], refs[2 * nw + 1]
        x, y, c = _place()
        cps = [pltpu.make_async_remote_copy(
            src_ref=in_refs[w].at[c], dst_ref=out_refs[w].at[c], send_sem=send_sems.at[w],
            recv_sem=recv_sems.at[w], device_id=(x, y, 1 - c), device_id_type=MESH) for w in range(nw)]
        for cp in cps:
            cp.start()
        for w in range(nw):
            slot = out_refs[w].at[1 - c]
            pltpu.make_async_remote_copy(src_ref=slot, dst_ref=slot, send_sem=send_sems.at[w],
                                         recv_sem=recv_sems.at[w], device_id=(x, y, 1 - c),
                                         device_id_type=MESH).wait_recv()
        for cp in cps:
            cp.wait_send()

    return pl.pallas_call(
        body, name=name, out_shape=[_sds(a.shape, a.dtype) for a in halves],
        in_specs=_hbm_specs(nw), out_specs=_hbm_specs(nw), input_output_aliases={i: i for i in range(nw)},
        scratch_shapes=[pltpu.SemaphoreType.DMA((nw,)), pltpu.SemaphoreType.DMA((nw,))])(*halves)


def _pair_sum(name, fulls, gots, core):
    nw = len(fulls)

    def body(c_ref, *refs):
        for a_ref, b_ref, o_ref in zip(refs[:nw], refs[nw:2 * nw], refs[2 * nw:]):
            o_ref[...] = (a_ref[...].astype(F32) + b_ref[...].astype(F32)).astype(o_ref.dtype)

    slab = lambda a: pl.BlockSpec((None,) + a.shape[1:], lambda s, c: (s, 0, 0))
    return pl.pallas_call(
        body, name=name,
        grid_spec=pltpu.PrefetchScalarGridSpec(
            num_scalar_prefetch=1, grid=(4,),
            in_specs=[pl.BlockSpec((None, None) + a.shape[2:], lambda s, c: (s, c[0], 0, 0)) for a in fulls]
            + [slab(b) for b in gots],
            out_specs=[slab(b) for b in gots]),
        out_shape=[_sds(b.shape, BF16) for b in gots],
        compiler_params=_params(("parallel",)))(core, *fulls, *gots)


def _sum4(name, arrs, core):
    nw = len(arrs)

    def body(c_ref, *refs):
        for a_ref, o_ref in zip(refs[:nw], refs[nw:]):
            acc = a_ref[0].astype(F32)
            for s in range(1, 4):
                acc = acc + a_ref[s].astype(F32)
            o_ref[...] = acc

    return pl.pallas_call(
        body, name=name,
        grid_spec=pltpu.PrefetchScalarGridSpec(
            num_scalar_prefetch=1, grid=(1,),
            in_specs=[pl.BlockSpec(a.shape, lambda i, c: (0, 0, 0)) for a in arrs],
            out_specs=[pl.BlockSpec((None,) + a.shape[1:], lambda i, c: (c[0], 0, 0)) for a in arrs]),
        out_shape=[_sds((2,) + a.shape[1:], F32) for a in arrs],
        compiler_params=_params(("arbitrary",)))(core, *arrs)


def _small_update(gathered, w, m, v):
    n = w.shape[1]
    tn = 2048
    c1 = 1.0 - ADAM_B1 ** ADAM_STEP
    c2 = 1.0 - ADAM_B2 ** ADAM_STEP

    def body(g_ref, w_ref, m_ref, v_ref, go_ref, d_ref, mo_ref, vo_ref):
        g = g_ref[0:1, :]
        for d in range(1, 8):
            g = g + g_ref[d:d + 1, :]
        go_ref[...] = g
        m = ADAM_B1 * m_ref[...] + (1.0 - ADAM_B1) * g
        v = ADAM_B2 * v_ref[...] + (1.0 - ADAM_B2) * (g * g)
        mo_ref[...] = m
        vo_ref[...] = v
        d_ref[...] = -ADAM_LR * ((m / c1) / (jnp.sqrt(v / c2) + ADAM_EPS) + ADAM_WD * w_ref[...])

    row = pl.BlockSpec((1, tn), lambda i: (0, i))
    return pl.pallas_call(
        body, name="small_update", grid=(n // tn,),
        in_specs=[pl.BlockSpec((8, tn), lambda i: (0, i)), row, row, row], out_specs=[row] * 4,
        out_shape=[_sds((1, n), F32)] * 4, compiler_params=_params(("parallel",)))(gathered, w, m, v)


def _swiglu(ps, es):
    g, u = ps
    return g * _sigmoid(g) * u, g, u


def _swiglu_bwd(ps, es):
    g, u = es[0].astype(F32), es[1].astype(F32)
    sg = _sigmoid(g)
    return ps[0] * u * (sg * (1.0 + g * (1.0 - sg))), ps[0] * (g * sg)


def _merge(ps, es):
    ga, gm = [e.astype(F32) for e in es]
    return _sigmoid(ga) * ps[0] + _sigmoid(gm) * ps[1], ps[0], ps[1]


def _merge_bwd(ps, es):
    a, b, ga, gm = [e.astype(F32) for e in es]
    sa, sm = _sigmoid(ga), _sigmoid(gm)
    dm = ps[0]
    return dm * sa, dm * sm, dm * a * (sa * (1.0 - sa)), dm * b * (sm * (1.0 - sm))


W_IN_PIECES = (("q", 512), ("kv", 256), ("mqk", 1024), ("mv", 512), ("mo", 512), ("if", 8),
               ("ga", 1024), ("gm", 1024))


def _local_step(x, tgt, pos_col, mod, sp, in_weights, late_weights, ffn_grads, mixer_grads):
    sh_m, sc_m, gate_m, sh_f, sc_f, gate_f = mod
    inv = ROPE_THETA ** (-2.0 * jnp.arange(HEAD_DIM // 2, dtype=F32) / HEAD_DIM)
    cos, sin = _rope_tables(pos_col, jnp.tile(inv, 4).reshape(1, 128))
    W = dict(in_weights(cos))
    h, pa, pqk, pvo, pif, pg = _proj_in(x, sp["g_pre_mix"], sc_m, sh_m, [
        (jnp.concatenate([W["q"], W["kv"]]), F32, 256), (W["mqk"], F32, 512),
        (jnp.concatenate([W["mv"], W["mo"]]), BF16, 512), (W["if"], F32, 128),
        (jnp.concatenate([W["ga"], W["gm"]]), BF16, 512)])
    ya = _attn_fwd(pa, cos, sin, sp["sinks"])
    qk = _conv_fwd(pqk, sp["conv_w"], sp["conv_b"])
    bcol = jnp.pad(sp["b_if"], ((0, 0), (0, 120)))
    brow = jnp.broadcast_to(sp["b_if"].reshape(8, 1), (8, 128))
    grow = pif[:, :8].T
    hm, cs, ns, ms = _mlstm_fwd(qk, pvo, pif, bcol, grow, brow)
    ym = _mlstm_out(hm, pvo, sp["norm_w"])
    W.update(late_weights(ym))
    w_fg, w_fu, w_fd = W["fg"], W["fu"], W["fd"]
    merged, br_a, br_m = _mm("branches", [[(ya, W["ba"])], [(ym, W["bm"])]],
                             [(pg, 0), (pg, 1)], _merge, [BF16, BF16, BF16], cn=512, nt=True)
    wide, narrow = (D_MODEL, F32), (D_MODEL, BF16)
    mix, x1, h2 = _mm_rows("mix_out", [[(merged, W["out"])]],
                           [x, gate_m, sp["g_post_mix"], sp["g_pre_ffn"], sc_f, sh_f],
                           _res_norm_rows, [wide, wide, narrow], [], cn=512)
    act, gt, up = _mm("ffn_in", [[(h2, w_fg)], [(h2, w_fu)]], [], _swiglu, [BF16] * 3,
                      cn=256, nt=True)
    dy, dff, acc_l, loss = _mm_rows("ffn_down", [[(act, w_fd)]], [x1, tgt, gate_f, sp["g_post_ffn"]],
                                    _final_loss_rows, [wide, narrow], [(8, D_MODEL), (1, 128)], cn=512)

    G = {}
    dgt, dup = _mm("ffn_down_bwd", [[(dff, w_fd)]], [gt, up], _swiglu_bwd, [BF16, BF16],
                   cn=256, nt=True)
    g_fd = _mm_tn("dw_ffn_down", act, dff, BF16, 1408, 512)
    g_fg = _mm_tn("dw_ffn_gate", dgt, h2, BF16, 1408, 1024)
    g_fu = _mm_tn("dw_ffn_up", dup, h2, BF16, 1408, 1024)
    tie = ffn_grads(g_fg, g_fu, g_fd)
    dx1, dmix, acc_r = _mm_rows(
        "ffn_in_bwd", [[(dgt, w_fg), (dup, w_fu)]],
        [x1, mix, dy, sc_f + tie, gate_m, sp["g_pre_ffn"], sp["g_post_mix"]],
        _res_norm_bwd_rows, [wide, narrow], [(8, D_MODEL)], cn=512, tm=256)
    d_a, d_m, dga, dgm = _mm("mix_out_bwd", [[(dmix, W["out"])]],
                             [br_a, br_m, (pg, 0), (pg, 1)], _merge_bwd,
                             [BF16] * 4, cn=512, nt=True)
    G["out"] = _mm_tn("dw_out", merged, dmix, BF16, 1024, 512)
    dya, = _mm("branch_attn_bwd", [[(d_a, W["ba"])]], [], _first, [F32], cn=512)
    dym, = _mm("branch_mlstm_bwd", [[(d_m, W["bm"])]], [], _first, [F32], cn=512)
    G["ba"] = _mm_tn("dw_branch_attn", d_a, ya, BF16, 1024, 512)
    G["bm"] = _mm_tn("dw_branch_mlstm", d_m, ym, BF16, 1024, 512)
    dhm, do_m, acc_n = _mlstm_out_bwd(hm, pvo, sp["norm_w"], dym)
    dqk, dv_m, dgc, dgr = _mlstm_bwd(qk, pvo, pif, bcol, grow, brow, cs, ns, ms, dhm)
    dif, acc_g = _gate_bwd(dgc, jnp.pad(dgr.T, ((0, 0), (0, 120))), pif, bcol)
    dpre, acc_c = _conv_bwd_pre(pqk, sp["conv_w"], sp["conv_b"], dqk)
    du = _conv_bwd_in(dpre, sp["conv_w"])
    dq_a, dcur, dprv, dsink = _attn_bwd(pa, cos, sin, sp["sinks"], dya)
    dkv = _attn_kv_combine(dcur, dprv, cos, sin)
    dproj = {"q": dq_a, "kv": dkv, "mqk": du, "mv": dv_m, "mo": do_m, "if": dif, "ga": dga, "gm": dgm}
    for k, _ in W_IN_PIECES:
        G[k] = _mm_tn("dw_in_" + k, dproj[k], h, BF16, dproj[k].shape[1], 1024)
    w_tied = dict(W, **{"if": W["if"] + mixer_grads(G).astype(BF16)})
    dx, acc_p = _mm_rows("proj_bwd", [[(dproj[k], w_tied[k]) for k, _ in W_IN_PIECES]],
                         [x, dx1, sp["g_pre_mix"], sc_m], _pre_norm_bwd_rows, [wide], [(8, D_MODEL)],
                         cn=512, tm=256)

    small = {
        "mod": jnp.concatenate([acc_p[1], acc_p[0], acc_r[3], acc_r[1], acc_r[0], acc_l[0]]),
        "g_pre_mix": acc_p[2], "g_post_mix": acc_r[4], "b_if": acc_g[0, :8],
        "conv_w": acc_c[:CONV_WIDTH].reshape(-1), "conv_b": acc_c[CONV_WIDTH],
        "sinks": dsink[:, 0], "norm_w": acc_n[0], "g_pre_ffn": acc_r[2], "g_post_ffn": acc_l[1]}
    return loss, dx, small


IN_WIDTH = sum(n for _, n in W_IN_PIECES)
IN_SHARD = IN_WIDTH // 4
IN_SHARD_PAD = -(-IN_SHARD // 32) * 32


def _split_w_in(w_in_t):
    out, off = {}, 0
    for k, n in W_IN_PIECES:
        out[k] = w_in_t[off:off + n]
        off += n
    out["if"] = jnp.pad(out["if"], ((0, 120), (0, 0)))
    return out


def _halves(a):
    return a.reshape(4, 2, a.shape[0] // 8, a.shape[1])


SMALL = (("b_ada", 6144), ("g_pre_mix", 1024), ("g_post_mix", 1024), ("b_if", 128), ("conv_w", 4096),
         ("conv_b", 1024), ("sinks", 128), ("norm_w", 512), ("g_pre_ffn", 1024), ("g_post_ffn", 1024))
SMALL_LEN = 8 * 2048


def _pack_small(vals):
    parts = []
    for k, n in SMALL:
        v = vals[k].reshape(-1)
        parts.append(jnp.pad(v, (0, n - v.shape[0])))
    flat = jnp.concatenate(parts)
    return jnp.pad(flat, (0, SMALL_LEN - flat.shape[0]))


def _unpack_small(flat, shapes):
    out, off = {}, 0
    for k, n in SMALL:
        size = 1
        for d in shapes[k]:
            size *= d
        out[k] = flat[off:off + size].reshape(shapes[k])
        off += n
    return out


def kernel(x, c, positions, w_ada, b_ada, g_pre_mix, g_post_mix, w_in, b_if, conv_w, conv_b, attn_sinks, mlstm_norm_w, w_branch_attn, w_branch_mlstm, w_out, g_pre_ffn, g_post_ffn, w_ffn_gate, w_ffn_up, w_ffn_down, loss_target, m_w_ada, m_b_ada, m_g_pre_mix, m_g_post_mix, m_w_in, m_b_if, m_conv_w, m_conv_b, m_attn_sinks, m_mlstm_norm_w, m_w_branch_attn, m_w_branch_mlstm, m_w_out, m_g_pre_ffn, m_g_post_ffn, m_w_ffn_gate, m_w_ffn_up, m_w_ffn_down, v_w_ada, v_b_ada, v_g_pre_mix, v_g_post_mix, v_w_in, v_b_if, v_conv_w, v_conv_b, v_attn_sinks, v_mlstm_norm_w, v_w_branch_attn, v_w_branch_mlstm, v_w_out, v_g_pre_ffn, v_g_post_ffn, v_w_ffn_gate, v_w_ffn_up, v_w_ffn_down):
    xi, yi, ci = _place()
    chip = 2 * xi + yi
    dev = 2 * chip + ci
    T = x.shape[1]
    ada_cols = w_ada.shape[2]

    def my_half(a):
        n = a.shape[0] // 2
        return lax.dynamic_slice_in_dim(a, ci * n, n, axis=0).astype(BF16)

    blk = jnp.concatenate([c.reshape(-1), conv_w.reshape(-1)]).reshape(8, 256)
    got = _all_gather8("gather_cond", blk, pltpu.VMEM).reshape(8, 2048)
    c_all = got[:, :D_MODEL].astype(BF16)
    conv_full = got[::2, D_MODEL:].reshape(4, CONV_WIDTH, -1).transpose(1, 0, 2).reshape(CONV_WIDTH, -1)

    b_sh = lax.dynamic_slice_in_dim(b_ada, chip * ada_cols, ada_cols, axis=1)
    mod_part, = _mm("ada_mod", [[(c_all, w_ada[0].astype(BF16))]], [b_sh],
                    lambda ps, es: (ps[0] + es[0],), [F32], cn=512, tm=8)
    mod_all = _all_gather8("gather_mod", mod_part, pltpu.VMEM).reshape(4, 2, 8, ada_cols)[:, 0]
    mod = lax.dynamic_index_in_dim(mod_all, dev, axis=1, keepdims=False).reshape(6, 1, D_MODEL)

    def gather_start(name, blks, after):
        blks, _ = lax.optimization_barrier((blks, after))
        lands = [lax.dynamic_update_slice(lax.empty((4, 2) + b.shape, BF16), b[None, None], (chip, ci, 0, 0))
                 for b in blks]
        return _split_start(name + "_start", blks, lands, _gather_sends, 4)

    def gather_wait(name, started, after):
        return _forward_sibling(name + "_forward", _split_wait(name + "_wait", started, after, _gather_lands, 4))

    in_flat = lambda a: jnp.pad(a[0].T.reshape(-1, 128), ((0, (IN_SHARD_PAD - IN_SHARD) * 8), (0, 0)))
    in_started = gather_start("in_gather", [my_half(in_flat(w_in))], mod)
    late_keys = ("fg", "fu", "fd", "out", "ba", "bm")
    late_started = gather_start(
        "late_gather",
        [my_half(w_ffn_gate[0].T), my_half(w_ffn_up[0].T), my_half(w_ffn_down[0]), my_half(w_out[0]),
         my_half(w_branch_attn[0].T), my_half(w_branch_mlstm[0].T)], in_started[4])
    mod = mod + (in_started[4][0, 0] + late_started[4][0, 0])

    def in_weights(after):
        g_in, = gather_wait("in_gather", in_started, after)
        return _split_w_in(g_in.reshape(4, IN_SHARD_PAD, D_MODEL)[:, :IN_SHARD].reshape(IN_WIDTH, D_MODEL))

    def late_weights(after):
        lands = gather_wait("late_gather", late_started, after)
        return {k: a.reshape(-1, a.shape[-1]) for k, a in zip(late_keys, lands)}

    core = ci.reshape(1).astype(jnp.int32)
    sent = {}

    def scatter_start(name, groups):
        pairs = _pair_sum(name + "_pair_sum", groups, _swap_halves_sibling(name + "_pair", groups), core)
        sent[name] = _split_start(name + "_start", pairs, [p + jnp.zeros((), BF16) for p in pairs],
                                  _scatter_sends, 3)
        return sent[name][4][0, 0]

    def ffn_grads(g_fg, g_fu, g_fd):
        return scatter_start("rs_ffn", [_halves(g_fg), _halves(g_fu), _halves(g_fd)])

    def mixer_grads(G):
        g_in_t = jnp.concatenate([G[k][:n] for k, n in W_IN_PIECES]).reshape(4, IN_SHARD, D_MODEL)
        g_in_t = jnp.pad(g_in_t, ((0, 0), (0, IN_SHARD_PAD - IN_SHARD), (0, 0)))
        return scatter_start("rs_mix", [g_in_t.reshape(4, 2, IN_SHARD_PAD * 4, 128), _halves(G["out"]),
                                        _halves(G["ba"]), _halves(G["bm"])])

    sp = {"g_pre_mix": g_pre_mix, "g_post_mix": g_post_mix, "b_if": b_if, "conv_w": conv_full,
          "conv_b": conv_b, "sinks": attn_sinks, "norm_w": mlstm_norm_w, "g_pre_ffn": g_pre_ffn,
          "g_post_ffn": g_post_ffn}
    loss, dx, small = _local_step(x[0], loss_target[0], positions.reshape(T, 1), [mod[i] for i in range(6)],
                                  sp, in_weights, late_weights, ffn_grads, mixer_grads)

    reds = (_sum4("rs_ffn_chip_sum", _split_wait("rs_ffn_wait", sent["rs_ffn"], dx, _scatter_lands, 3), core)
            + _sum4("rs_mix_chip_sum", _split_wait("rs_mix_wait", sent["rs_mix"], dx, _scatter_lands, 3), core))
    gsh = {k: s.reshape(-1, s.shape[-1])
           for k, s in zip(("fg", "fu", "fd", "w_in", "out", "ba", "bm"), _share_halves("rs_share", reds))}
    gsh["w_in"] = gsh["w_in"][:IN_SHARD * 8]

    small["b_ada"] = small.pop("mod")
    vec = _pack_small(small).reshape(8, 2048)
    g_all = _all_gather8("gather_small", vec, pltpu.VMEM).reshape(8, SMALL_LEN)
    dmod_sh = lax.dynamic_slice_in_dim(g_all[:, :6 * D_MODEL], chip * ada_cols, ada_cols, axis=1)
    g_w_ada = _mm_tn("dw_ada", c_all, dmod_sh.astype(BF16), F32, D_MODEL, 512, 8)

    smalls = {"b_ada": (b_ada, m_b_ada, v_b_ada), "g_pre_mix": (g_pre_mix, m_g_pre_mix, v_g_pre_mix),
              "g_post_mix": (g_post_mix, m_g_post_mix, v_g_post_mix), "b_if": (b_if, m_b_if, v_b_if),
              "conv_w": None, "conv_b": (conv_b, m_conv_b, v_conv_b),
              "sinks": (attn_sinks, m_attn_sinks, v_attn_sinks),
              "norm_w": (mlstm_norm_w, m_mlstm_norm_w, v_mlstm_norm_w),
              "g_pre_ffn": (g_pre_ffn, m_g_pre_ffn, v_g_pre_ffn),
              "g_post_ffn": (g_post_ffn, m_g_post_ffn, v_g_post_ffn)}
    shapes = {k: (t[0].shape if t is not None else (1, CONV_WIDTH, D_MODEL)) for k, t in smalls.items()}
    zeros = jnp.zeros((CONV_WIDTH * D_MODEL,), F32)
    packs = [_pack_small({k: (t[i] if t is not None else zeros) for k, t in smalls.items()}).reshape(1, -1)
             for i in range(3)]
    s_out = [_unpack_small(o[0], shapes) for o in _small_update(g_all, *packs)]
    g_conv = lax.dynamic_slice_in_dim(s_out[0]["conv_w"], chip * conv_w.shape[2], conv_w.shape[2], axis=2)

    res = {}
    for k, t in smalls.items():
        if t is not None:
            res[k] = tuple(o[k] for o in s_out)
    res["conv_w"] = (g_conv, *[o[None] for o in _adamw("adam_conv_w", conv_w[0], g_conv[0], m_conv_w[0], v_conv_w[0])])
    res["w_ada"] = (g_w_ada[None], *[o[None] for o in _adamw("adam_w_ada", w_ada[0], g_w_ada, m_w_ada[0], v_w_ada[0])])
    bigs = {"w_in": (w_in, m_w_in, v_w_in), "ba": (w_branch_attn, m_w_branch_attn, v_w_branch_attn),
            "bm": (w_branch_mlstm, m_w_branch_mlstm, v_w_branch_mlstm), "out": (w_out, m_w_out, v_w_out),
            "fg": (w_ffn_gate, m_w_ffn_gate, v_w_ffn_gate), "fu": (w_ffn_up, m_w_ffn_up, v_w_ffn_up),
            "fd": (w_ffn_down, m_w_ffn_down, v_w_ffn_down)}
    for k, (w, m, v) in bigs.items():
        if k in ("w_in", "fg", "fu"):
            form = (lambda a: a[0].T.reshape(-1, 128)) if k == "w_in" else (lambda a: a[0].T)
            outs = (gsh[k], *_adamw("adam_" + k, form(w), gsh[k], form(m), form(v)))
            res[k] = tuple(o.reshape(w.shape[2], w.shape[1]).T[None] for o in outs)
        else:
            g = gsh[k].T if k in ("ba", "bm") else gsh[k]
            res[k] = (g[None], *[o[None] for o in _adamw("adam_" + k, w[0], g, m[0], v[0])])

    order = ("w_ada", "b_ada", "g_pre_mix", "g_post_mix", "w_in", "b_if", "conv_w", "conv_b", "sinks",
             "norm_w", "ba", "bm", "out", "g_pre_ffn", "g_post_ffn", "fg", "fu", "fd")
    total = lax.psum(loss[0, 0], ("x", "y", "c"))
    return (total, dx[None], *[res[k][0] for k in order], *[res[k][1] for k in order],
            *[res[k][2] for k in order], *[res[k][3] for k in order])
```

```python
import functools

import jax
import jax.numpy as jnp
from jax import lax
from jax.experimental import pallas as pl
from jax.experimental.pallas import tpu as pltpu

F32, BF16 = jnp.float32, jnp.bfloat16
MESH = pl.DeviceIdType.MESH

D_MODEL = 1024
N_Q_HEADS, N_KV_HEADS, HEAD_DIM, WINDOW = 8, 2, 64, 128
ROPE_THETA = 10000.0
MLSTM_HEADS, MLSTM_HEAD_DIM, MLSTM_CHUNK, CONV_WIDTH = 4, 128, 64, 4
D_FF = 2816
NORM_EPS = 1e-6
ADAM_LR, ADAM_B1, ADAM_B2, ADAM_EPS, ADAM_WD, ADAM_STEP = 0.001, 0.9, 0.999, 1e-08, 0.01, 10

VMEM_LIMIT = 56 * 1024 * 1024
ROW_TILE = 256
MM_TM = 512
MM_TT = 1024
ATTN_BLK = WINDOW
STEP_ROWS = 2 * MLSTM_CHUNK
NEG_INF = float("-inf")


def _params(sem):
    return pltpu.CompilerParams(dimension_semantics=sem, vmem_limit_bytes=VMEM_LIMIT)


def _sds(shape, dtype):
    return jax.ShapeDtypeStruct(shape, dtype)


def _sigmoid(x):
    return 1.0 / (1.0 + jnp.exp(-x))


def _dot(a, b, ca, cb):
    return lax.dot_general(a, b, (((ca,), (cb,)), ((), ())), preferred_element_type=F32)


def _bdot(a, b, ca, cb):
    return lax.dot_general(a, b, (((ca,), (cb,)), ((0,), (0,))), preferred_element_type=F32)


def _bdot_rows(a, b):
    return jnp.stack([_dot(a[h], b[h], 0, 0) for h in range(a.shape[0])])


def _mm(name, prods, extras, epi, out_dtypes, cn, nt=False, tm=MM_TM):
    flat = [ab for p in prods for ab in p]
    counts = [len(p) for p in prods]
    M = flat[0][0].shape[0]
    N = flat[0][1].shape[0 if nt else 1]
    tm = min(tm, M)
    n_in = 2 * len(flat) + len(extras)

    def body(*refs):
        ins, outs = refs[:n_in], refs[n_in:]
        for j in range(N // cn):
            cols = slice(j * cn, (j + 1) * cn)
            k, ps = 0, []
            for cnt in counts:
                acc = None
                for _ in range(cnt):
                    b = ins[k + 1][cols, :] if nt else ins[k + 1][:, cols]
                    d = _dot(ins[k][...], b, 1, 1 if nt else 0)
                    acc = d if acc is None else acc + d
                    k += 2
                ps.append(acc)
            res = epi(ps, [r[:, cols] for r in ins[k:]])
            for o, r in zip(outs, res):
                o[:, cols] = r.astype(o.dtype)

    in_specs, args = [], []
    for a, b in flat:
        in_specs.append(pl.BlockSpec((tm, a.shape[1]), lambda i: (i, 0)))
        in_specs.append(pl.BlockSpec(b.shape, lambda i: (0, 0), pipeline_mode=pl.Buffered(1)))
        args += [a, b]
    for e in extras:
        e, off = e if isinstance(e, tuple) else (e, 0)
        rows = 1 if e.shape[0] == 1 else tm
        in_specs.append(pl.BlockSpec((rows, N), lambda i, off=off, rows=rows: (0 if rows == 1 else i, off)))
        args.append(e)
    return pl.pallas_call(
        body, name=name, grid=(M // tm,), in_specs=in_specs,
        out_specs=[pl.BlockSpec((tm, N), lambda i: (i, 0)) for _ in out_dtypes],
        out_shape=[_sds((M, N), dt) for dt in out_dtypes],
        compiler_params=_params(("parallel",)))(*args)


def _mm_rows(name, prods, extras, epi, outs, accs, cn, nt=False, tm=MM_TM):
    flat = [ab for p in prods for ab in p]
    counts = [len(p) for p in prods]
    M = flat[0][0].shape[0]
    N = flat[0][1].shape[0 if nt else 1]
    tm = min(tm, M)
    n_mm, n_in, n_out = 2 * len(flat), 2 * len(flat) + len(extras), len(outs)

    def body(*refs):
        ins, out_refs, acc_refs = refs[:n_in], refs[n_in:n_in + n_out], refs[n_in + n_out:]

        @pl.when(pl.program_id(0) == 0)
        def _():
            for a in acc_refs:
                a[...] = jnp.zeros_like(a)

        chunks = [[] for _ in counts]
        for j in range(N // cn):
            cols = slice(j * cn, (j + 1) * cn)
            k = 0
            for p, cnt in enumerate(counts):
                acc = None
                for _ in range(cnt):
                    b = ins[k + 1][cols, :] if nt else ins[k + 1][:, cols]
                    d = _dot(ins[k][...], b, 1, 1 if nt else 0)
                    acc = d if acc is None else acc + d
                    k += 2
                chunks[p].append(acc)
        ps = [c[0] if len(c) == 1 else jnp.concatenate(c, axis=1) for c in chunks]
        res, incs = epi(ps, [r[...] for r in ins[n_mm:]])
        for o, r in zip(out_refs, res):
            o[...] = r.astype(o.dtype)
        for a, inc in zip(acc_refs, incs):
            a[...] += inc

    in_specs, args = [], []
    for a, b in flat:
        in_specs.append(pl.BlockSpec((tm, a.shape[1]), lambda i: (i, 0)))
        in_specs.append(pl.BlockSpec(b.shape, lambda i: (0, 0), pipeline_mode=pl.Buffered(1)))
        args += [a, b]
    for e in extras:
        rows = 1 if e.shape[0] == 1 else tm
        in_specs.append(pl.BlockSpec((rows, e.shape[1]), lambda i, rows=rows: (0 if rows == 1 else i, 0)))
        args.append(e)
    return pl.pallas_call(
        body, name=name, grid=(M // tm,), in_specs=in_specs,
        out_specs=[pl.BlockSpec((tm, w), lambda i: (i, 0)) for w, _ in outs]
        + [pl.BlockSpec(s, lambda i: (0, 0)) for s in accs],
        out_shape=[_sds((M, w), dt) for w, dt in outs] + [_sds(s, F32) for s in accs],
        compiler_params=_params(("arbitrary",)))(*args)


def _mm_tn(name, a, b, out_dtype, tk, tn, tt=MM_TT):
    T, Ka = a.shape
    N = b.shape[1]
    tt = min(tt, T)
    steps = T // tt

    def body(a_ref, b_ref, o_ref, acc):
        t = pl.program_id(2)

        @pl.when(t == 0)
        def _():
            acc[...] = jnp.zeros_like(acc)

        acc[...] += _dot(a_ref[...], b_ref[...], 0, 0)

        @pl.when(t == steps - 1)
        def _():
            o_ref[...] = acc[...].astype(o_ref.dtype)

    return pl.pallas_call(
        body, name=name, grid=(Ka // tk, N // tn, steps),
        in_specs=[pl.BlockSpec((tt, tk), lambda i, j, t: (t, i)),
                  pl.BlockSpec((tt, tn), lambda i, j, t: (t, j))],
        out_specs=pl.BlockSpec((tk, tn), lambda i, j, t: (i, j)),
        out_shape=_sds((Ka, N), out_dtype),
        scratch_shapes=[pltpu.VMEM((tk, tn), F32)],
        compiler_params=_params(("parallel", "parallel", "arbitrary")))(a, b)


def _first(ps, es):
    return (ps[0],)


def _rows(name, body, ins, out_shapes, T, tr=ROW_TILE):
    tr = min(tr, T)

    def spec(shape):
        if shape[0] == T:
            return pl.BlockSpec((tr,) + tuple(shape[1:]), lambda i: (i,) + (0,) * (len(shape) - 1))
        return pl.BlockSpec(tuple(shape), lambda i: (0,) * len(shape))

    return pl.pallas_call(
        body, name=name, grid=(T // tr,),
        in_specs=[spec(a.shape) for a in ins], out_specs=[spec(s.shape) for s in out_shapes],
        out_shape=out_shapes, compiler_params=_params(("arbitrary",)))(*ins)


def _rms(x):
    r = lax.rsqrt(jnp.mean(x * x, axis=-1, keepdims=True) + NORM_EPS)
    return x * r, r


def _rms_bwd(dxn, xn, r):
    return r * (dxn - xn * jnp.mean(dxn * xn, axis=-1, keepdims=True))


def _colsum(v):
    return jnp.sum(v, axis=0, keepdims=True)


def _proj_in(x, g, sc, sh, groups):
    T = x.shape[0]
    tm = min(MM_TM, T)
    ng = len(groups)

    def body(x_ref, g_ref, sc_ref, sh_ref, *rest):
        w_refs, h_ref, out_refs = rest[:ng], rest[ng], rest[ng + 1:]
        xn, _ = _rms(x_ref[...])
        h = (xn * g_ref[...] * (1.0 + sc_ref[...]) + sh_ref[...]).astype(BF16)
        h_ref[...] = h
        for w_ref, o_ref, (w, _, cn) in zip(w_refs, out_refs, groups):
            for j in range(w.shape[0] // cn):
                cols = slice(j * cn, (j + 1) * cn)
                o_ref[:, cols] = _dot(h, w_ref[cols, :], 1, 1).astype(o_ref.dtype)

    row = pl.BlockSpec((1, D_MODEL), lambda i: (0, 0))
    tile = lambda w: pl.BlockSpec((tm, w), lambda i: (i, 0))
    return pl.pallas_call(
        body, name="proj_in", grid=(T // tm,),
        in_specs=[tile(D_MODEL), row, row, row] + [
            pl.BlockSpec(w.shape, lambda i: (0, 0), pipeline_mode=pl.Buffered(1)) for w, _, _ in groups],
        out_specs=[tile(D_MODEL)] + [tile(w.shape[0]) for w, _, _ in groups],
        out_shape=[_sds((T, D_MODEL), BF16)] + [_sds((T, w.shape[0]), dt) for w, dt, _ in groups],
        compiler_params=_params(("parallel",)))(x, g, sc, sh, *[w for w, _, _ in groups])


def _acc_rows(rows):
    w = rows[0].shape[1]
    return jnp.concatenate(rows + [jnp.zeros((8 - len(rows), w), F32)], axis=0)


def _res_norm_rows(ps, es):
    mix = ps[0]
    x, gate, gp, g2, sc, sh = es
    mh, _ = _rms(mix)
    x1 = x + gate * (mh * gp)
    xn, _ = _rms(x1)
    return [mix, x1, xn * g2 * (1.0 + sc) + sh], []


def _final_loss_rows(ps, es):
    x1, tgt, gate, gp = es
    fh, r = _rms(ps[0])
    e = x1 + gate * (fh * gp) - tgt
    loss = 0.5 * jnp.sum(jnp.mean(e * e, axis=-1, keepdims=True))
    dy = e * (1.0 / D_MODEL)
    acc = _acc_rows([_colsum(dy * fh * gp), _colsum(dy * gate * fh)])
    return [dy, _rms_bwd(dy * gate * gp, fh, r)], [acc, jnp.full((1, 128), loss, F32)]


def _res_norm_bwd_rows(ps, es):
    dh = ps[0]
    x1, mix, dy, sc, gate, g2, gp = es
    xn, r1 = _rms(x1)
    rows = [_colsum(dh * xn * g2), _colsum(dh), _colsum(dh * (1.0 + sc) * xn)]
    dx1 = dy + _rms_bwd(dh * (1.0 + sc) * g2, xn, r1)
    mh, rm = _rms(mix)
    rows += [_colsum(dx1 * mh * gp), _colsum(dx1 * gate * mh)]
    return [dx1, _rms_bwd(dx1 * gate * gp, mh, rm)], [_acc_rows(rows)]


def _pre_norm_bwd_rows(ps, es):
    dh = ps[0]
    x, dx1, g, sc = es
    xn, r = _rms(x)
    rows = [_colsum(dh * xn * g), _colsum(dh), _colsum(dh * (1.0 + sc) * xn)]
    return [dx1 + _rms_bwd(dh * (1.0 + sc) * g, xn, r)], [_acc_rows(rows)]


def _rope_tables(pos_col, inv_freq):
    T = pos_col.shape[0]

    def body(p_ref, f_ref, c_ref, s_ref):
        ang = p_ref[...].astype(F32) * f_ref[...]
        lane = lax.broadcasted_iota(jnp.int32, ang.shape, 1)
        c_ref[...] = jnp.cos(ang)
        s_ref[...] = jnp.where(lane % HEAD_DIM < HEAD_DIM // 2, -1.0, 1.0) * jnp.sin(ang)

    return _rows("rope_tables", body, [pos_col, inv_freq],
                 [_sds((T, 128), F32), _sds((T, 128), F32)], T, tr=512)


def _swap_halves(t):
    W = t.shape[1]
    lane = lax.broadcasted_iota(jnp.int32, t.shape, 1)
    half = HEAD_DIM // 2
    return jnp.where(lane % HEAD_DIM < half, pltpu.roll(t, W - half, 1), pltpu.roll(t, half, 1))


def _widen(c, W):
    return c if W == 128 else jnp.concatenate([c] * (W // 128), axis=1)


def _rope(t, c, s):
    W = t.shape[1]
    return t * _widen(c, W) + _swap_halves(t) * _widen(s, W)


def _unrope(dy, c, s):
    W = dy.shape[1]
    return dy * _widen(c, W) + _swap_halves(dy * _widen(s, W))


def _attn_mask(n):
    qi = lax.broadcasted_iota(jnp.int32, (ATTN_BLK, 2 * ATTN_BLK), 0)
    kj = lax.broadcasted_iota(jnp.int32, (ATTN_BLK, 2 * ATTN_BLK), 1)
    rel = kj - ATTN_BLK
    return (rel <= qi) & (qi - rel < WINDOW) & ((n > 0) | (kj >= ATTN_BLK))


def _attn_load(cur, prv, cc, sc, cp, sp):
    x, xp = cur[...], prv[...]
    q = _rope(x[:, :512], cc[...], sc[...]) * (HEAD_DIM ** -0.5)
    k = jnp.concatenate([_rope(xp[:, 512:640], cp[...], sp[...]),
                         _rope(x[:, 512:640], cc[...], sc[...])], axis=0)
    v = jnp.concatenate([xp[:, 640:768], x[:, 640:768]], axis=0)
    return q, k, v


ROLLED = tuple(h for h in range(N_Q_HEADS) if h % 2 != h // (N_Q_HEADS // N_KV_HEADS))


def _pair_heads(t):
    half = lax.broadcasted_iota(jnp.int32, (ATTN_BLK, 128), 1) // HEAD_DIM
    return jnp.stack([jnp.where(half == h % 2, t[:, 128 * (h // 2):128 * (h // 2) + 128], 0.0)
                      for h in range(N_Q_HEADS)])


def _kv_heads(t):
    half = lax.broadcasted_iota(jnp.int32, t.shape, 1) // HEAD_DIM
    tr = pltpu.roll(t, HEAD_DIM, 1)
    return jnp.stack([jnp.where(half == h % 2, tr if h in ROLLED else t, 0.0)
                      for h in range(N_Q_HEADS)])


def _sink_column(snk):
    return jnp.stack([jnp.full((1, 1), snk[0, h], F32) for h in range(N_Q_HEADS)])


def _attn_probs(qh, kh, mask, sink):
    s = jnp.where(mask, _bdot(qh, kh, 2, 2), NEG_INF)
    m = jnp.maximum(jnp.max(s, axis=-1, keepdims=True), sink)
    p = jnp.exp(s - m)
    es = jnp.exp(sink - m)
    rl = 1.0 / (jnp.sum(p, axis=-1, keepdims=True) + es)
    return p, es, rl


def _attn_specs(nb):
    blk = lambda w: pl.BlockSpec((ATTN_BLK, w), lambda n: (n, 0))
    prv = lambda w: pl.BlockSpec((ATTN_BLK, w), lambda n: (jnp.maximum(n - 1, 0), 0))
    return [blk(768), prv(768), blk(128), blk(128), prv(128), prv(128),
            pl.BlockSpec(memory_space=pltpu.SMEM)]


def _attn_fwd(pa, cos, sin, sinks):
    T = pa.shape[0]
    nb = T // ATTN_BLK

    def body(cur, prv, cc, sc, cp, sp, snk, y_ref):
        n = pl.program_id(0)
        q, k, v = _attn_load(cur, prv, cc, sc, cp, sp)
        qh, kh, vh = _pair_heads(q).astype(BF16), _kv_heads(k).astype(BF16), _kv_heads(v).astype(BF16)
        p, _, rl = _attn_probs(qh, kh, _attn_mask(n), _sink_column(snk))
        o = _bdot(p.astype(BF16), vh, 2, 1) * rl
        for pair in range(N_Q_HEADS // 2):
            y_ref[:, 128 * pair:128 * pair + 128] = (o[2 * pair] + o[2 * pair + 1]).astype(BF16)

    return pl.pallas_call(
        body, name="attn_fwd", grid=(nb,), in_specs=_attn_specs(nb),
        out_specs=pl.BlockSpec((ATTN_BLK, 512), lambda n: (n, 0)),
        out_shape=_sds((T, 512), BF16), compiler_params=_params(("parallel",)))(
            pa, pa, cos, sin, cos, sin, sinks)


def _attn_bwd(pa, cos, sin, sinks, dy):
    T = pa.shape[0]
    nb = T // ATTN_BLK

    def body(cur, prv, cc, sc, cp, sp, snk, dy_ref, dq_ref, dcur_ref, dprv_ref, dsink_ref):
        n = pl.program_id(0)

        @pl.when(n == 0)
        def _():
            dsink_ref[...] = jnp.zeros_like(dsink_ref)

        q, k, v = _attn_load(cur, prv, cc, sc, cp, sp)
        qh, kh, vh = _pair_heads(q).astype(BF16), _kv_heads(k).astype(BF16), _kv_heads(v).astype(BF16)
        p, es, rl = _attn_probs(qh, kh, _attn_mask(n), _sink_column(snk))
        pn = p * rl
        do = _pair_heads(dy_ref[...]).astype(BF16)
        dp = _bdot(do, vh, 2, 2)
        delta = jnp.sum(pn * dp, axis=-1, keepdims=True)
        ds = (pn * (dp - delta)).astype(BF16)
        dsink = es * rl * delta
        dq = _bdot(ds, kh, 2, 1) * (HEAD_DIM ** -0.5)
        dkh = _bdot_rows(ds, qh)
        dvh = _bdot_rows(pn.astype(BF16), do)

        def fold(t):
            same = [t[h] for h in range(N_Q_HEADS) if h not in ROLLED]
            moved = [t[h] for h in ROLLED]
            return sum(same[1:], same[0]) + pltpu.roll(sum(moved[1:], moved[0]), HEAD_DIM, 1)

        dk, dv = fold(dkh), fold(dvh)
        for h in range(N_Q_HEADS):
            dsink_ref[h:h + 1, :] += -jnp.sum(dsink[h])
        for pair in range(N_Q_HEADS // 2):
            dq_ref[:, 128 * pair:128 * pair + 128] = _unrope(
                dq[2 * pair] + dq[2 * pair + 1], cc[...], sc[...]).astype(BF16)
        dcur_ref[:, 0:128] = dk[ATTN_BLK:]
        dcur_ref[:, 128:256] = dv[ATTN_BLK:]
        dprv_ref[:, 0:128] = dk[:ATTN_BLK]
        dprv_ref[:, 128:256] = dv[:ATTN_BLK]

    blk = lambda w: pl.BlockSpec((ATTN_BLK, w), lambda n: (n, 0))
    return pl.pallas_call(
        body, name="attn_bwd", grid=(nb,), in_specs=_attn_specs(nb) + [blk(512)],
        out_specs=[blk(512), blk(256), blk(256), pl.BlockSpec((8, 128), lambda n: (0, 0))],
        out_shape=[_sds((T, 512), BF16), _sds((T, 256), F32), _sds((T, 256), F32),
                   _sds((8, 128), F32)],
        compiler_params=_params(("arbitrary",)))(pa, pa, cos, sin, cos, sin, sinks, dy)


def _attn_kv_combine(dcur, dprv, cos, sin):
    T = dcur.shape[0]
    nb = T // ATTN_BLK

    def body(c_ref, p_ref, cc, sc, o_ref):
        n = pl.program_id(0)
        t = c_ref[...] + jnp.where(n < nb - 1, p_ref[...], 0.0)
        o_ref[:, 0:128] = _unrope(t[:, 0:128], cc[...], sc[...]).astype(BF16)
        o_ref[:, 128:256] = t[:, 128:256].astype(BF16)

    blk = lambda w: pl.BlockSpec((ATTN_BLK, w), lambda n: (n, 0))
    nxt = pl.BlockSpec((ATTN_BLK, 256), lambda n: (jnp.minimum(n + 1, nb - 1), 0))
    return pl.pallas_call(
        body, name="attn_kv_combine", grid=(nb,), in_specs=[blk(256), nxt, blk(128), blk(128)],
        out_specs=blk(256), out_shape=_sds((T, 256), BF16),
        compiler_params=_params(("parallel",)))(dcur, dprv, cos, sin)


CONV_COLS = 2 * MLSTM_HEADS * MLSTM_HEAD_DIM


def _conv_pre(cur_ref, halo_ref, w_ref, b_ref, i, tr):
    xx = jnp.concatenate([jnp.where(i > 0, halo_ref[...], 0.0), cur_ref[...]], axis=0)
    taps = [(pltpu.roll(xx, CONV_WIDTH - 1 - j, 0) if j < CONV_WIDTH - 1 else xx)[8:8 + tr]
            for j in range(CONV_WIDTH)]
    pre = b_ref[...]
    for j in range(CONV_WIDTH):
        pre = pre + taps[j] * w_ref[j:j + 1, :]
    return pre, taps


def _conv_specs(T, tr):
    return [pl.BlockSpec((tr, CONV_COLS), lambda i: (i, 0)),
            pl.BlockSpec((8, CONV_COLS), lambda i: (jnp.maximum(i * (tr // 8) - 1, 0), 0)),
            pl.BlockSpec((CONV_WIDTH, CONV_COLS), lambda i: (0, 0)),
            pl.BlockSpec((1, CONV_COLS), lambda i: (0, 0))]


def _conv_fwd(pm, w, b):
    T = pm.shape[0]
    tr = min(ROW_TILE, T)

    def body(cur_ref, halo_ref, w_ref, b_ref, o_ref):
        pre, _ = _conv_pre(cur_ref, halo_ref, w_ref, b_ref, pl.program_id(0), tr)
        o_ref[...] = pre * _sigmoid(pre)

    return pl.pallas_call(
        body, name="conv_fwd", grid=(T // tr,), in_specs=_conv_specs(T, tr),
        out_specs=pl.BlockSpec((tr, CONV_COLS), lambda i: (i, 0)),
        out_shape=_sds((T, CONV_COLS), F32), compiler_params=_params(("parallel",)))(pm, pm, w, b)


def _conv_bwd_pre(pm, w, b, dqk):
    T = pm.shape[0]
    tr = min(ROW_TILE, T)

    def body(cur_ref, halo_ref, w_ref, b_ref, d_ref, dpre_ref, acc_ref):
        i = pl.program_id(0)

        @pl.when(i == 0)
        def _():
            acc_ref[...] = jnp.zeros_like(acc_ref)

        pre, taps = _conv_pre(cur_ref, halo_ref, w_ref, b_ref, i, tr)
        sg = _sigmoid(pre)
        dpre = d_ref[...] * (sg * (1.0 + pre * (1.0 - sg)))
        dpre_ref[...] = dpre
        for j in range(CONV_WIDTH):
            acc_ref[j:j + 1, :] += _colsum(dpre * taps[j])
        acc_ref[CONV_WIDTH:CONV_WIDTH + 1, :] += _colsum(dpre)

    return pl.pallas_call(
        body, name="conv_bwd_pre", grid=(T // tr,),
        in_specs=_conv_specs(T, tr) + [pl.BlockSpec((tr, CONV_COLS), lambda i: (i, 0))],
        out_specs=[pl.BlockSpec((tr, CONV_COLS), lambda i: (i, 0)),
                   pl.BlockSpec((8, CONV_COLS), lambda i: (0, 0))],
        out_shape=[_sds((T, CONV_COLS), F32), _sds((8, CONV_COLS), F32)],
        compiler_params=_params(("arbitrary",)))(pm, pm, w, b, dqk)


def _conv_bwd_in(dpre, w):
    T = dpre.shape[0]
    tr = min(ROW_TILE, T)
    nt = T // tr

    def body(cur_ref, halo_ref, w_ref, o_ref):
        i = pl.program_id(0)
        yy = jnp.concatenate([cur_ref[...], jnp.where(i < nt - 1, halo_ref[...], 0.0)], axis=0)
        du = cur_ref[...] * w_ref[CONV_WIDTH - 1:CONV_WIDTH, :]
        for j in range(CONV_WIDTH - 1):
            k = CONV_WIDTH - 1 - j
            du = du + pltpu.roll(yy, tr + 8 - k, 0)[:tr] * w_ref[j:j + 1, :]
        o_ref[...] = du.astype(BF16)

    return pl.pallas_call(
        body, name="conv_bwd_in", grid=(nt,),
        in_specs=[pl.BlockSpec((tr, CONV_COLS), lambda i: (i, 0)),
                  pl.BlockSpec((8, CONV_COLS),
                               lambda i: (jnp.minimum((i + 1) * (tr // 8), T // 8 - 1), 0)),
                  pl.BlockSpec((CONV_WIDTH, CONV_COLS), lambda i: (0, 0))],
        out_specs=pl.BlockSpec((tr, CONV_COLS), lambda i: (i, 0)),
        out_shape=_sds((T, CONV_COLS), BF16), compiler_params=_params(("parallel",)))(dpre, dpre, w)


def _log_sigmoid(x):
    return jnp.minimum(x, 0.0) - jnp.log1p(jnp.exp(-jnp.abs(x)))


def _chunk_cumsum(x, axis):
    idx = lax.broadcasted_iota(jnp.int32, x.shape, axis) % MLSTM_CHUNK
    k = 1
    while k < MLSTM_CHUNK:
        x = x + jnp.where(idx >= k, pltpu.roll(x, k, axis), 0.0)
        k *= 2
    return x


def _chunk_rev_cumsum(x, axis):
    n = x.shape[axis]
    idx = lax.broadcasted_iota(jnp.int32, x.shape, axis) % MLSTM_CHUNK
    k = 1
    while k < MLSTM_CHUNK:
        x = x + jnp.where(idx < MLSTM_CHUNK - k, pltpu.roll(x, n - k, axis), 0.0)
        k *= 2
    return x


def _mlstm_gates(gc_ref, bc_ref, gr_ref, br_ref):
    gc = gc_ref[...] + bc_ref[...]
    gr = gr_ref[...] + br_ref[...]
    return gc, _chunk_cumsum(_log_sigmoid(gc), 0), gr, _chunk_cumsum(_log_sigmoid(gr), 1)


def _heads(ref, base=0):
    D = MLSTM_HEAD_DIM
    return jnp.stack([ref[:, base + D * h:base + D * h + D] for h in range(MLSTM_HEADS)])


def _mlstm_inputs(q_ref, k_ref, v_ref, gc, bc, gr, br):
    H = MLSTM_HEADS
    q, v = _heads(q_ref), _heads(v_ref)
    ks = _heads(k_ref) * (MLSTM_HEAD_DIM ** -0.5)
    return dict(
        q=q, ks=ks, qb=q.astype(BF16), kb=ks.astype(BF16), vb=v.astype(BF16),
        b_col=jnp.stack([bc[:, H + h:H + h + 1] for h in range(H)]),
        i_col=jnp.stack([gc[:, h:h + 1] for h in range(H)]),
        b_row=jnp.stack([br[H + h:H + h + 1, :] for h in range(H)]),
        i_row=jnp.stack([gr[h:h + 1, :] for h in range(H)]))


def _mlstm_head(f, c_prev, n_prev, m_prev):
    L = MLSTM_CHUNK
    q, qb = f["q"], f["qb"]
    t = lax.broadcasted_iota(jnp.int32, (1, 2 * L, 2 * L), 1)
    s = lax.broadcasted_iota(jnp.int32, (1, 2 * L, 2 * L), 2)
    mask = (t // L == s // L) & (s <= t)
    d = jnp.where(mask, f["b_col"] - f["b_row"] + f["i_row"], NEG_INF)
    row = lax.broadcasted_iota(jnp.int32, (1, 2 * L, 1), 1)
    inter = f["b_col"] + jnp.where(row < L, m_prev[0], m_prev[1])
    m_t = jnp.maximum(inter, jnp.max(d, axis=-1, keepdims=True))
    w_intra = jnp.exp(d - m_t)
    w_inter = jnp.exp(inter - m_t)
    sc = _bdot(qb, f["kb"], 2, 2) * w_intra
    qc = jnp.concatenate([_bdot(qb[:, :L], c_prev[0].astype(BF16), 2, 1),
                          _bdot(qb[:, L:], c_prev[1].astype(BF16), 2, 1)], axis=1)
    qn = jnp.concatenate([jnp.sum(q[:, :L] * n_prev[0], axis=-1, keepdims=True),
                          jnp.sum(q[:, L:] * n_prev[1], axis=-1, keepdims=True)], axis=1)
    num = _bdot(sc.astype(BF16), f["vb"], 2, 1) + w_inter * qc
    den = jnp.sum(sc, axis=-1, keepdims=True) + w_inter * qn
    return dict(f, w_intra=w_intra, w_inter=w_inter, sc=sc, qc=qc, qn=qn, num=num, den=den,
                floor=jnp.exp(-m_t))


def _mlstm_update(f, ch, c, n, m):
    L = MLSTM_CHUNK
    rows = slice(L * ch, L * ch + L)
    b_col = f["b_col"][:, rows]
    g_last = b_col[:, L - 1:L]
    a_col = g_last - b_col + f["i_col"][:, rows]
    m_new = jnp.maximum(g_last + m, jnp.max(a_col, axis=1, keepdims=True))
    decay = jnp.exp(g_last + m - m_new)
    e_a = jnp.exp(a_col - m_new)
    kw = f["ks"][:, rows] * e_a
    c_new = decay * c + _bdot_rows(kw.astype(BF16), f["vb"][:, rows])
    n_new = decay * n + jnp.sum(kw, axis=1, keepdims=True)
    return c_new, n_new, m_new, decay, e_a, kw


def _mlstm_specs(T, order):
    blk = lambda w, col: pl.BlockSpec((STEP_ROWS, w), lambda s: (order(s), col))
    return [blk(512, 0), blk(512, 1), blk(512, 0), blk(128, 0),
            pl.BlockSpec((1, 128), lambda s: (0, 0)),
            pl.BlockSpec((8, STEP_ROWS), lambda s: (0, order(s))),
            pl.BlockSpec((8, 128), lambda s: (0, 0))]


def _lanes(m):
    return jnp.broadcast_to(m, m.shape[:-1] + (128,))


def _mlstm_fwd(qk, pm, gcol, bcol, grow, brow):
    T = qk.shape[0]
    steps = T // STEP_ROWS
    H, D = MLSTM_HEADS, MLSTM_HEAD_DIM

    def body(q_ref, k_ref, v_ref, gc_ref, bc_ref, gr_ref, br_ref, h_ref, cs_ref, ns_ref, ms_ref,
             c_st, n_st, m_st):
        @pl.when(pl.program_id(0) == 0)
        def _():
            c_st[...] = jnp.zeros_like(c_st)
            n_st[...] = jnp.zeros_like(n_st)
            m_st[...] = jnp.zeros_like(m_st)

        f = _mlstm_inputs(q_ref, k_ref, v_ref, *_mlstm_gates(gc_ref, bc_ref, gr_ref, br_ref))
        c0, n0, m0 = c_st[...], n_st[...], m_st[:, :, 0:1]
        c1, n1, m1, _, _, _ = _mlstm_update(f, 0, c0, n0, m0)
        c2, n2, m2, _, _, _ = _mlstm_update(f, 1, c1, n1, m1)
        f = _mlstm_head(f, (c0, c1), (n0, n1), (m0, m1))
        h = f["num"] / jnp.maximum(jnp.abs(f["den"]), f["floor"])
        for hd in range(H):
            h_ref[:, D * hd:D * hd + D] = h[hd]
        cs_ref[0], cs_ref[1] = c0, c1
        ns_ref[0], ns_ref[1] = n0, n1
        ms_ref[0], ms_ref[1] = _lanes(m0), _lanes(m1)
        c_st[...], n_st[...], m_st[...] = c2, n2, _lanes(m2)

    vec = pl.BlockSpec((2, H, 1, 128), lambda s: (s, 0, 0, 0))
    return pl.pallas_call(
        body, name="mlstm_fwd", grid=(steps,), in_specs=_mlstm_specs(T, lambda s: s),
        out_specs=[pl.BlockSpec((STEP_ROWS, 512), lambda s: (s, 0)),
                   pl.BlockSpec((2, H, 128, 128), lambda s: (s, 0, 0, 0)), vec, vec],
        out_shape=[_sds((T, 512), F32), _sds((2 * steps, H, 128, 128), F32),
                   _sds((2 * steps, H, 1, 128), F32), _sds((2 * steps, H, 1, 128), F32)],
        scratch_shapes=[pltpu.VMEM((H, 128, 128), F32), pltpu.VMEM((H, 1, 128), F32),
                        pltpu.VMEM((H, 1, 128), F32)],
        compiler_params=_params(("arbitrary",)))(qk, qk, pm, gcol, bcol, grow, brow)


def _mlstm_bwd(qk, pm, gcol, bcol, grow, brow, cs, ns, ms, dh):
    T = qk.shape[0]
    steps = T // STEP_ROWS
    H, L, D = MLSTM_HEADS, MLSTM_CHUNK, MLSTM_HEAD_DIM
    rev = lambda s: steps - 1 - s

    def body(q_ref, k_ref, v_ref, gc_ref, bc_ref, gr_ref, br_ref, cs_ref, ns_ref, ms_ref, dh_ref,
             dqk_ref, dv_ref, dgc_ref, dgr_ref, dc_st, dn_st):
        @pl.when(pl.program_id(0) == 0)
        def _():
            dc_st[...] = jnp.zeros_like(dc_st)
            dn_st[...] = jnp.zeros_like(dn_st)

        f = _mlstm_inputs(q_ref, k_ref, v_ref, *_mlstm_gates(gc_ref, bc_ref, gr_ref, br_ref))
        c_prev = (cs_ref[0], cs_ref[1])
        n_prev = (ns_ref[0], ns_ref[1])
        m_prev = (ms_ref[0, :, :, 0:1], ms_ref[1, :, :, 0:1])
        f = _mlstm_head(f, c_prev, n_prev, m_prev)
        big = jnp.abs(f["den"]) > f["floor"]
        rden = 1.0 / jnp.where(big, jnp.abs(f["den"]), f["floor"])
        dnum = _heads(dh_ref) * rden
        hdh = jnp.sum(f["num"] * dnum, axis=-1, keepdims=True)
        dden = jnp.where(big, -hdh * rden * jnp.sign(f["den"]), 0.0)
        dnum_b = dnum.astype(BF16)
        dsc = _bdot(dnum_b, f["vb"], 2, 2) + dden
        g = dsc * f["sc"]
        dv = _bdot_rows(f["sc"].astype(BF16), dnum_b)
        dqk_ = (dsc * f["w_intra"]).astype(BF16)
        dq = _bdot(dqk_, f["kb"], 2, 1)
        dks = _bdot_rows(dqk_, f["qb"])
        wdn = f["w_inter"] * dnum
        wdn_b = wdn.astype(BF16)
        wdd = f["w_inter"] * dden
        u = jnp.sum(f["qc"] * wdn, axis=-1, keepdims=True) + wdd * f["qn"]
        dks_s, dv_s, z_s, dg_s = [None, None], [None, None], [None, None], [None, None]
        dcn, dnn = dc_st[...], dn_st[...]
        for ch in (1, 0):
            rows = slice(L * ch, L * ch + L)
            _, _, _, decay, e_a, kw = _mlstm_update(f, ch, c_prev[ch], n_prev[ch], m_prev[ch])
            dcn_b = dcn.astype(BF16)
            dkw = _bdot(f["vb"][:, rows], dcn_b, 2, 2) + dnn
            dks_s[ch] = e_a * dkw
            dv_s[ch] = _bdot(kw.astype(BF16), dcn_b, 2, 1)
            z_s[ch] = e_a * jnp.sum(f["ks"][:, rows] * dkw, axis=-1, keepdims=True)
            dg_s[ch] = jnp.sum(z_s[ch], axis=1, keepdims=True) + decay * (
                jnp.sum(c_prev[ch] * dcn, axis=(1, 2), keepdims=True)
                + jnp.sum(n_prev[ch] * dnn, axis=(1, 2), keepdims=True))
            dcn = decay * dcn + _bdot_rows(f["qb"][:, rows], wdn_b[:, rows])
            dnn = decay * dnn + jnp.sum(wdd[:, rows] * f["q"][:, rows], axis=1, keepdims=True)
        dc_st[...], dn_st[...] = dcn, dnn
        dq = dq + jnp.concatenate(
            [_bdot(wdn_b[:, :L], c_prev[0].astype(BF16), 2, 2) + wdd[:, :L] * n_prev[0],
             _bdot(wdn_b[:, L:], c_prev[1].astype(BF16), 2, 2) + wdd[:, L:] * n_prev[1]], axis=1)
        dks = (dks + jnp.concatenate(dks_s, axis=1)) * (D ** -0.5)
        dv = dv + jnp.concatenate(dv_s, axis=1)
        z = jnp.concatenate(z_s, axis=1)
        row = lax.broadcasted_iota(jnp.int32, (1, STEP_ROWS, 1), 1)
        dg_col = jnp.where(row == L - 1, dg_s[0], 0.0) + jnp.where(row == 2 * L - 1, dg_s[1], 0.0)
        db_col = jnp.sum(g, axis=-1, keepdims=True) + u - z + dg_col
        g_row = jnp.sum(g, axis=1, keepdims=True)
        lane = lax.broadcasted_iota(jnp.int32, (STEP_ROWS, 128), 1)
        sub = lax.broadcasted_iota(jnp.int32, (8, STEP_ROWS), 0)
        dgc = jnp.zeros((STEP_ROWS, 128), F32)
        dgr = jnp.zeros((8, STEP_ROWS), F32)
        for hd in range(H):
            dgc = dgc + jnp.where(lane == hd, z[hd], 0.0) + jnp.where(lane == H + hd, db_col[hd], 0.0)
            dgr = dgr + jnp.where(sub == hd, g_row[hd], 0.0) - jnp.where(sub == H + hd, g_row[hd], 0.0)
            dqk_ref[:, D * hd:D * hd + D] = dq[hd]
            dqk_ref[:, H * D + D * hd:H * D + D * hd + D] = dks[hd]
            dv_ref[:, D * hd:D * hd + D] = dv[hd].astype(BF16)
        dgc_ref[...] = dgc
        dgr_ref[...] = dgr

    return pl.pallas_call(
        body, name="mlstm_bwd", grid=(steps,),
        in_specs=_mlstm_specs(T, rev) + [
            pl.BlockSpec((2, H, 128, 128), lambda s: (rev(s), 0, 0, 0)),
            pl.BlockSpec((2, H, 1, 128), lambda s: (rev(s), 0, 0, 0)),
            pl.BlockSpec((2, H, 1, 128), lambda s: (rev(s), 0, 0, 0)),
            pl.BlockSpec((STEP_ROWS, 512), lambda s: (rev(s), 0))],
        out_specs=[pl.BlockSpec((STEP_ROWS, 1024), lambda s: (rev(s), 0)),
                   pl.BlockSpec((STEP_ROWS, 512), lambda s: (rev(s), 0)),
                   pl.BlockSpec((STEP_ROWS, 128), lambda s: (rev(s), 0)),
                   pl.BlockSpec((8, STEP_ROWS), lambda s: (0, rev(s)))],
        out_shape=[_sds((T, 1024), F32), _sds((T, 512), BF16), _sds((T, 128), F32), _sds((8, T), F32)],
        scratch_shapes=[pltpu.VMEM((H, 128, 128), F32), pltpu.VMEM((H, 1, 128), F32)],
        compiler_params=_params(("arbitrary",)))(qk, qk, pm, gcol, bcol, grow, brow, cs, ns, ms, dh)


def _gate_bwd(dgc, dgr_t, gcol, bcol):
    T = dgc.shape[0]

    def body(a_ref, b_ref, g_ref, bias_ref, o_ref, acc_ref):
        @pl.when(pl.program_id(0) == 0)
        def _():
            acc_ref[...] = jnp.zeros_like(acc_ref)

        d = a_ref[...] + b_ref[...]
        lane = lax.broadcasted_iota(jnp.int32, d.shape, 1)
        is_f = (lane >= MLSTM_HEADS) & (lane < 2 * MLSTM_HEADS)
        dlogf = _chunk_rev_cumsum(jnp.where(is_f, d, 0.0), 0)
        out = jnp.where(is_f, dlogf * _sigmoid(-(g_ref[...] + bias_ref[...])), d)
        o_ref[...] = out.astype(BF16)
        acc_ref[0:1, :] += _colsum(out)

    return _rows("gate_bwd", body, [dgc, dgr_t, gcol, bcol],
                 [_sds((T, 128), BF16), _sds((8, 128), F32)], T)


def _head_norm(h, mu_axis=-1):
    mu = jnp.mean(h, axis=-1, keepdims=True)
    hc = h - mu
    r = lax.rsqrt(jnp.mean(hc * hc, axis=-1, keepdims=True) + NORM_EPS)
    return hc * r, r


def _mlstm_out(hm, pm, w):
    T = hm.shape[0]
    D = MLSTM_HEAD_DIM

    def body(h_ref, o_ref, w_ref, y_ref):
        for hd in range(MLSTM_HEADS):
            cols = slice(D * hd, D * hd + D)
            hn, _ = _head_norm(h_ref[:, cols])
            y_ref[:, cols] = (_sigmoid(o_ref[:, cols].astype(F32)) * hn * w_ref[:, cols]).astype(BF16)

    tr = min(ROW_TILE, T)
    return pl.pallas_call(
        body, name="mlstm_out", grid=(T // tr,),
        in_specs=[pl.BlockSpec((tr, 512), lambda i: (i, 0)), pl.BlockSpec((tr, 512), lambda i: (i, 1)),
                  pl.BlockSpec((1, 512), lambda i: (0, 0))],
        out_specs=pl.BlockSpec((tr, 512), lambda i: (i, 0)), out_shape=_sds((T, 512), BF16),
        compiler_params=_params(("parallel",)))(hm, pm, w)


def _mlstm_out_bwd(hm, pm, w, dy):
    T = hm.shape[0]
    D = MLSTM_HEAD_DIM
    tr = min(ROW_TILE, T)

    def body(h_ref, o_ref, w_ref, dy_ref, dh_ref, do_ref, acc_ref):
        @pl.when(pl.program_id(0) == 0)
        def _():
            acc_ref[...] = jnp.zeros_like(acc_ref)

        for hd in range(MLSTM_HEADS):
            cols = slice(D * hd, D * hd + D)
            hn, r = _head_norm(h_ref[:, cols])
            sg = _sigmoid(o_ref[:, cols].astype(F32))
            dy, w = dy_ref[:, cols], w_ref[:, cols]
            do_ref[:, cols] = (dy * hn * w * sg * (1.0 - sg)).astype(BF16)
            dyn = dy * sg
            acc_ref[0:1, cols] += _colsum(dyn * hn)
            dhn = dyn * w
            dh_ref[:, cols] = r * (dhn - jnp.mean(dhn, axis=-1, keepdims=True)
                                   - hn * jnp.mean(dhn * hn, axis=-1, keepdims=True))

    return pl.pallas_call(
        body, name="mlstm_out_bwd", grid=(T // tr,),
        in_specs=[pl.BlockSpec((tr, 512), lambda i: (i, 0)), pl.BlockSpec((tr, 512), lambda i: (i, 1)),
                  pl.BlockSpec((1, 512), lambda i: (0, 0)), pl.BlockSpec((tr, 512), lambda i: (i, 0))],
        out_specs=[pl.BlockSpec((tr, 512), lambda i: (i, 0)), pl.BlockSpec((tr, 512), lambda i: (i, 0)),
                   pl.BlockSpec((8, 512), lambda i: (0, 0))],
        out_shape=[_sds((T, 512), F32), _sds((T, 512), BF16), _sds((8, 512), F32)],
        compiler_params=_params(("arbitrary",)))(hm, pm, w, dy)


ADAM_TILE_ELEMS = 256 * 1024


def _adamw(name, w, g, m, v):
    R, C = w.shape
    fits = [t for t in range(8, R + 1, 8) if R % t == 0 and t * C <= ADAM_TILE_ELEMS]
    tr = fits[-1] if fits else R
    spec, grid = pl.BlockSpec((tr, C), lambda i: (i, 0)), (R // tr,)
    c1 = 1.0 - ADAM_B1 ** ADAM_STEP
    c2 = 1.0 - ADAM_B2 ** ADAM_STEP

    def body(w_ref, g_ref, m_ref, v_ref, d_ref, mo_ref, vo_ref):
        g = g_ref[...]
        m = ADAM_B1 * m_ref[...] + (1.0 - ADAM_B1) * g
        v = ADAM_B2 * v_ref[...] + (1.0 - ADAM_B2) * (g * g)
        mo_ref[...] = m
        vo_ref[...] = v
        d_ref[...] = -ADAM_LR * ((m / c1) / (jnp.sqrt(v / c2) + ADAM_EPS) + ADAM_WD * w_ref[...])

    return pl.pallas_call(
        body, name=name, grid=grid, in_specs=[spec] * 4, out_specs=[spec] * 3,
        out_shape=[_sds((R, C), F32)] * 3, compiler_params=_params(("parallel",)))(w, g, m, v)


def _place():
    return lax.axis_index("x"), lax.axis_index("y"), lax.axis_index("c")


def _all_gather8(name, blk, space):
    m, n = blk.shape

    def body(x_ref, out_ref, send_sems, recv_sems, local_sem):
        x, y, c = _place()
        me, sibling = (x, y, c), (x, y, 1 - c)
        chips = [(1 - x, y), (x, 1 - y), (1 - x, 1 - y)]

        def rows(px, py, pc):
            return out_ref.at[pl.ds((4 * px + 2 * py + pc) * m, m), :]

        def copy(k, block, to, src=None):
            return pltpu.make_async_remote_copy(
                src_ref=rows(*block) if src is None else src, dst_ref=rows(*block),
                send_sem=send_sems.at[k], recv_sem=recv_sems.at[k],
                device_id=to, device_id_type=MESH)

        mine = pltpu.make_async_copy(x_ref, rows(*me), local_sem)
        mine.start()
        first = [copy(0, me, sibling, src=x_ref)]
        first += [copy(1 + j, me, (*chip, c), src=x_ref) for j, chip in enumerate(chips)]
        for cp in first:
            cp.start()
        passed = [copy(4 + j, (*chip, c), sibling) for j, chip in enumerate(chips)]
        for j, chip in enumerate(chips):
            copy(1 + j, (*chip, c), me).wait_recv()
            passed[j].start()
        copy(0, sibling, me).wait_recv()
        for j, chip in enumerate(chips):
            copy(4 + j, (*chip, 1 - c), me).wait_recv()
        for cp in first + passed:
            cp.wait_send()
        mine.wait()

    return pl.pallas_call(
        body, name=name, out_shape=_sds((8 * m, n), blk.dtype),
        in_specs=[pl.BlockSpec(memory_space=space)], out_specs=pl.BlockSpec(memory_space=space),
        scratch_shapes=[pltpu.SemaphoreType.DMA((7,)), pltpu.SemaphoreType.DMA((7,)),
                        pltpu.SemaphoreType.DMA],
        compiler_params=pltpu.CompilerParams(vmem_limit_bytes=VMEM_LIMIT))(blk)


def _hbm_specs(n):
    return [pl.BlockSpec(memory_space=pl.ANY)] * n


def _swap_halves_sibling(name, srcs):
    nw = len(srcs)

    def body(*refs):
        src_refs, dst_refs, send_sems, recv_sems = refs[:nw], refs[nw:2 * nw], refs[2 * nw], refs[2 * nw + 1]
        x, y, c = _place()
        cps = [pltpu.make_async_remote_copy(
            src_ref=src_refs[w].at[pl.ds(0, 4), 1 - c], dst_ref=dst_refs[w],
            send_sem=send_sems.at[w], recv_sem=recv_sems.at[w], device_id=(x, y, 1 - c),
            device_id_type=MESH) for w in range(nw)]
        for cp in cps:
            cp.start()
        for cp in cps:
            cp.wait()

    return pl.pallas_call(
        body, name=name, out_shape=[_sds(s.shape[:1] + s.shape[2:], s.dtype) for s in srcs],
        in_specs=_hbm_specs(nw), out_specs=_hbm_specs(nw),
        scratch_shapes=[pltpu.SemaphoreType.DMA((nw,)), pltpu.SemaphoreType.DMA((nw,))])(*srcs)


def _split_start(name, srcs, lands, copies, per_array, after):
    nw = len(srcs)

    def body(*refs):
        send_sems, recv_sems, token = refs[2 * nw + 1], refs[2 * nw + 2], refs[-1]
        for w in range(nw):
            for k, (s, d, dev) in enumerate(copies(refs[w], refs[nw + w], *_place())):
                pltpu.make_async_remote_copy(
                    src_ref=s, dst_ref=d, send_sem=send_sems.at[w * per_array + k],
                    recv_sem=recv_sems.at[w * per_array + k], device_id=dev, device_id_type=MESH).start()
        token[...] = jnp.zeros_like(token)

    hbm, sem = pl.BlockSpec(memory_space=pltpu.HBM), pl.BlockSpec(memory_space=pltpu.SEMAPHORE)
    arrays = list(srcs) + list(lands)
    out = pl.pallas_call(
        body, name=name,
        out_shape=(pltpu.SemaphoreType.DMA((nw * per_array,)), pltpu.SemaphoreType.DMA((nw * per_array,)),
                   *[pltpu.HBM(a.shape, a.dtype) for a in arrays], _sds((8, 128), F32)),
        in_specs=[hbm] * (2 * nw) + [pl.BlockSpec(memory_space=pl.ANY)],
        out_specs=(sem, sem, *[hbm] * (2 * nw), pl.BlockSpec(memory_space=pltpu.VMEM)),
        input_output_aliases={i: 2 + i for i in range(2 * nw)},
        compiler_params=pltpu.CompilerParams(has_side_effects=pltpu.SideEffectType.DATAFLOW_SIDE_EFFECTING))(
            *[pltpu.with_memory_space_constraint(a, pltpu.HBM) for a in arrays], after)
    return out[0], out[1], out[2:2 + nw], out[2 + nw:2 + 2 * nw], out[-1]


def _split_wait(name, started, after, waits, per_array):
    send_sems, recv_sems, srcs, lands, _ = started
    nw = len(srcs)

    def body(*refs):
        send_sems, recv_sems = refs[2 * nw], refs[2 * nw + 1]
        x, y, c = _place()
        for w in range(nw):
            for k, (s, d) in enumerate(waits(refs[w], refs[nw + w], x, y, c)):
                cp = pltpu.make_async_remote_copy(
                    src_ref=s, dst_ref=d, send_sem=send_sems.at[w * per_array + k],
                    recv_sem=recv_sems.at[w * per_array + k], device_id=(x, y, 1 - c),
                    device_id_type=MESH)
                cp.wait_send()
                cp.wait_recv()

    hbm, sem = pl.BlockSpec(memory_space=pltpu.HBM), pl.BlockSpec(memory_space=pltpu.SEMAPHORE)
    arrays = list(srcs) + list(lands)
    out = pl.pallas_call(
        body, name=name, out_shape=tuple(pltpu.HBM(a.shape, a.dtype) for a in arrays),
        in_specs=[hbm] * (2 * nw) + [sem, sem, pl.BlockSpec(memory_space=pl.ANY)],
        out_specs=tuple([hbm] * (2 * nw)), input_output_aliases={i: i for i in range(2 * nw)},
        compiler_params=pltpu.CompilerParams(has_side_effects=pltpu.SideEffectType.DATAFLOW_SIDE_EFFECTING))(
            *arrays, send_sems, recv_sems, after)
    return list(out[nw:])


def _other_chips(x, y):
    return [(1 - x, y), (x, 1 - y), (1 - x, 1 - y)]


def _gather_sends(src_ref, land_ref, x, y, c):
    to = land_ref.at[2 * x + y, c]
    return [(src_ref, to, (x, y, 1 - c))] + [(src_ref, to, (px, py, c)) for px, py in _other_chips(x, y)]


def _gather_lands(src_ref, land_ref, x, y, c):
    return [(src_ref, land_ref.at[2 * x + y, 1 - c])] + [
        (src_ref, land_ref.at[2 * px + py, c]) for px, py in _other_chips(x, y)]


def _scatter_sends(src_ref, land_ref, x, y, c):
    return [(src_ref.at[2 * px + py], land_ref.at[2 * x + y], (px, py, c)) for px, py in _other_chips(x, y)]


def _scatter_lands(src_ref, land_ref, x, y, c):
    return [(src_ref.at[2 * x + y], land_ref.at[2 * px + py]) for px, py in _other_chips(x, y)]


def _forward_sibling(name, lands):
    nw = len(lands)

    def body(*refs):
        land_refs, out_refs, send_sems, recv_sems = refs[:nw], refs[nw:2 * nw], refs[2 * nw], refs[2 * nw + 1]
        x, y, c = _place()
        cps = []
        for w in range(nw):
            cps += [pltpu.make_async_remote_copy(
                src_ref=land_refs[w].at[2 * px + py, c], dst_ref=out_refs[w].at[2 * px + py, c],
                send_sem=send_sems.at[w, j], recv_sem=recv_sems.at[w, j], device_id=(x, y, 1 - c),
                device_id_type=MESH) for j, (px, py) in enumerate(_other_chips(x, y))]
        for cp in cps:
            cp.start()
        for w in range(nw):
            for j, (px, py) in enumerate(_other_chips(x, y)):
                slot = out_refs[w].at[2 * px + py, 1 - c]
                pltpu.make_async_remote_copy(src_ref=slot, dst_ref=slot, send_sem=send_sems.at[w, j],
                                             recv_sem=recv_sems.at[w, j], device_id=(x, y, 1 - c),
                                             device_id_type=MESH).wait_recv()
        for cp in cps:
            cp.wait_send()

    return pl.pallas_call(
        body, name=name, out_shape=[_sds(a.shape, a.dtype) for a in lands],
        in_specs=_hbm_specs(nw), out_specs=_hbm_specs(nw), input_output_aliases={i: i for i in range(nw)},
        scratch_shapes=[pltpu.SemaphoreType.DMA((nw, 3)), pltpu.SemaphoreType.DMA((nw, 3))])(*lands)


def _share_halves(name, halves):
    nw = len(halves)

    def body(*refs):
        in_refs, out_refs, send_sems, recv_sems = refs[:nw], refs[nw:2 * nw], refs[2 * nw], refs[2 * nw + 1]
        x, y, c = _place()
        cps = [pltpu.make_async_remote_copy(
            src_ref=in_refs[w].at[c], dst_ref=out_refs[w].at[c], send_sem=send_sems.at[w],
            recv_sem=recv_sems.at[w], device_id=(x, y, 1 - c), device_id_type=MESH) for w in range(nw)]
        for cp in cps:
            cp.start()
        for w in range(nw):
            slot = out_refs[w].at[1 - c]
            pltpu.make_async_remote_copy(src_ref=slot, dst_ref=slot, send_sem=send_sems.at[w],
                                         recv_sem=recv_sems.at[w], device_id=(x, y, 1 - c),
                                         device_id_type=MESH).wait_recv()
        for cp in cps:
            cp.wait_send()

    return pl.pallas_call(
        body, name=name, out_shape=[_sds(a.shape, a.dtype) for a in halves],
        in_specs=_hbm_specs(nw), out_specs=_hbm_specs(nw), input_output_aliases={i: i for i in range(nw)},
        scratch_shapes=[pltpu.SemaphoreType.DMA((nw,)), pltpu.SemaphoreType.DMA((nw,))])(*halves)


def _place_blocks(name, blks, place):
    nw = len(blks)

    def body(p_ref, *refs):
        for b_ref, o_ref in zip(refs[:nw], refs[nw:]):
            o_ref[...] = b_ref[...]

    return pl.pallas_call(
        body, name=name,
        grid_spec=pltpu.PrefetchScalarGridSpec(
            num_scalar_prefetch=1, grid=(1,),
            in_specs=[pl.BlockSpec(b.shape, lambda i, p: (0, 0)) for b in blks],
            out_specs=[pl.BlockSpec((None, None) + b.shape, lambda i, p: (p[0], p[1], 0, 0)) for b in blks]),
        out_shape=[_sds((4, 2) + b.shape, b.dtype) for b in blks],
        compiler_params=_params(("arbitrary",)))(place, *blks)


def _pair_sum(name, fulls, gots, place):
    nw = len(fulls)

    def body(p_ref, *refs):
        s = pl.program_id(0)
        for a_ref, b_ref, o_ref, l_ref in zip(refs[:nw], refs[nw:2 * nw], refs[2 * nw:3 * nw], refs[3 * nw:]):
            o_ref[...] = (a_ref[...].astype(F32) + b_ref[...].astype(F32)).astype(o_ref.dtype)

            @pl.when(s == p_ref[0])
            def _():
                l_ref[...] = o_ref[...]

    slab = lambda a: pl.BlockSpec((None,) + a.shape[1:], lambda s, p: (s, 0, 0))
    mine = lambda a: pl.BlockSpec((None,) + a.shape[1:], lambda s, p: (p[0], 0, 0))
    out = pl.pallas_call(
        body, name=name,
        grid_spec=pltpu.PrefetchScalarGridSpec(
            num_scalar_prefetch=1, grid=(4,),
            in_specs=[pl.BlockSpec((None, None) + a.shape[2:], lambda s, p: (s, p[1], 0, 0)) for a in fulls]
            + [slab(b) for b in gots],
            out_specs=[slab(b) for b in gots] + [mine(b) for b in gots]),
        out_shape=[_sds(b.shape, BF16) for b in gots] * 2,
        compiler_params=_params(("arbitrary",)))(place, *fulls, *gots)
    return out[:nw], out[nw:]


def _sum4(name, arrs, place):
    nw = len(arrs)

    def body(p_ref, *refs):
        for a_ref, o_ref in zip(refs[:nw], refs[nw:]):
            acc = a_ref[0].astype(F32)
            for s in range(1, 4):
                acc = acc + a_ref[s].astype(F32)
            o_ref[...] = acc

    return pl.pallas_call(
        body, name=name,
        grid_spec=pltpu.PrefetchScalarGridSpec(
            num_scalar_prefetch=1, grid=(1,),
            in_specs=[pl.BlockSpec(a.shape, lambda i, p: (0, 0, 0)) for a in arrs],
            out_specs=[pl.BlockSpec((None,) + a.shape[1:], lambda i, p: (p[1], 0, 0)) for a in arrs]),
        out_shape=[_sds((2,) + a.shape[1:], F32) for a in arrs],
        compiler_params=_params(("arbitrary",)))(place, *arrs)


def _small_update(gathered, w, m, v):
    n = w.shape[1]
    tn = 2048
    c1 = 1.0 - ADAM_B1 ** ADAM_STEP
    c2 = 1.0 - ADAM_B2 ** ADAM_STEP

    def body(g_ref, w_ref, m_ref, v_ref, go_ref, d_ref, mo_ref, vo_ref):
        g = g_ref[0:1, :]
        for d in range(1, 8):
            g = g + g_ref[d:d + 1, :]
        go_ref[...] = g
        m = ADAM_B1 * m_ref[...] + (1.0 - ADAM_B1) * g
        v = ADAM_B2 * v_ref[...] + (1.0 - ADAM_B2) * (g * g)
        mo_ref[...] = m
        vo_ref[...] = v
        d_ref[...] = -ADAM_LR * ((m / c1) / (jnp.sqrt(v / c2) + ADAM_EPS) + ADAM_WD * w_ref[...])

    row = pl.BlockSpec((1, tn), lambda i: (0, i))
    return pl.pallas_call(
        body, name="small_update", grid=(n // tn,),
        in_specs=[pl.BlockSpec((8, tn), lambda i: (0, i)), row, row, row], out_specs=[row] * 4,
        out_shape=[_sds((1, n), F32)] * 4, compiler_params=_params(("parallel",)))(gathered, w, m, v)


def _swiglu(ps, es):
    g, u = ps
    return g * _sigmoid(g) * u, g, u


def _swiglu_bwd(ps, es):
    g, u = es[0].astype(F32), es[1].astype(F32)
    sg = _sigmoid(g)
    return ps[0] * u * (sg * (1.0 + g * (1.0 - sg))), ps[0] * (g * sg)


def _merge(ps, es):
    ga, gm = [e.astype(F32) for e in es]
    return _sigmoid(ga) * ps[0] + _sigmoid(gm) * ps[1], ps[0], ps[1]


def _merge_bwd(ps, es):
    a, b, ga, gm = [e.astype(F32) for e in es]
    sa, sm = _sigmoid(ga), _sigmoid(gm)
    dm = ps[0]
    return dm * sa, dm * sm, dm * a * (sa * (1.0 - sa)), dm * b * (sm * (1.0 - sm))


W_IN_PIECES = (("q", 512), ("kv", 256), ("mqk", 1024), ("mv", 512), ("mo", 512), ("if", 8),
               ("ga", 1024), ("gm", 1024))


def _local_step(x, tgt, pos_col, mod, sp, in_weights, late_weights, ffn_grads, mixer_grads):
    sh_m, sc_m, gate_m, sh_f, sc_f, gate_f = mod
    inv = ROPE_THETA ** (-2.0 * jnp.arange(HEAD_DIM // 2, dtype=F32) / HEAD_DIM)
    cos, sin = _rope_tables(pos_col, jnp.tile(inv, 4).reshape(1, 128))
    W = dict(in_weights(cos))
    h, pa, pqk, pvo, pif, pg = _proj_in(x, sp["g_pre_mix"], sc_m, sh_m, [
        (jnp.concatenate([W["q"], W["kv"]]), F32, 256), (W["mqk"], F32, 512),
        (jnp.concatenate([W["mv"], W["mo"]]), BF16, 512), (W["if"], F32, 128),
        (jnp.concatenate([W["ga"], W["gm"]]), BF16, 512)])
    ya = _attn_fwd(pa, cos, sin, sp["sinks"])
    qk = _conv_fwd(pqk, sp["conv_w"], sp["conv_b"])
    bcol = jnp.pad(sp["b_if"], ((0, 0), (0, 120)))
    brow = jnp.broadcast_to(sp["b_if"].reshape(8, 1), (8, 128))
    grow = pif[:, :8].T
    hm, cs, ns, ms = _mlstm_fwd(qk, pvo, pif, bcol, grow, brow)
    ym = _mlstm_out(hm, pvo, sp["norm_w"])
    W.update(late_weights(ym))
    w_fg, w_fu, w_fd = W["fg"], W["fu"], W["fd"]
    merged, br_a, br_m = _mm("branches", [[(ya, W["ba"])], [(ym, W["bm"])]],
                             [(pg, 0), (pg, 1)], _merge, [BF16, BF16, BF16], cn=512, nt=True)
    wide, narrow = (D_MODEL, F32), (D_MODEL, BF16)
    mix, x1, h2 = _mm_rows("mix_out", [[(merged, W["out"])]],
                           [x, gate_m, sp["g_post_mix"], sp["g_pre_ffn"], sc_f, sh_f],
                           _res_norm_rows, [wide, wide, narrow], [], cn=512)
    act, gt, up = _mm("ffn_in", [[(h2, w_fg)], [(h2, w_fu)]], [], _swiglu, [BF16] * 3,
                      cn=256, nt=True)
    dy, dff, acc_l, loss = _mm_rows("ffn_down", [[(act, w_fd)]], [x1, tgt, gate_f, sp["g_post_ffn"]],
                                    _final_loss_rows, [wide, narrow], [(8, D_MODEL), (1, 128)], cn=512)

    G = {}
    dgt, dup = _mm("ffn_down_bwd", [[(dff, w_fd)]], [gt, up], _swiglu_bwd, [BF16, BF16],
                   cn=256, nt=True)
    g_fd = _mm_tn("dw_ffn_down", act, dff, BF16, 1408, 512)
    g_fg = _mm_tn("dw_ffn_gate", dgt, h2, BF16, 1408, 1024)
    g_fu = _mm_tn("dw_ffn_up", dup, h2, BF16, 1408, 1024)
    tie = ffn_grads(g_fg, g_fu, g_fd)
    dx1, dmix, acc_r = _mm_rows(
        "ffn_in_bwd", [[(dgt, w_fg), (dup, w_fu)]],
        [x1, mix, dy, sc_f + tie, gate_m, sp["g_pre_ffn"], sp["g_post_mix"]],
        _res_norm_bwd_rows, [wide, narrow], [(8, D_MODEL)], cn=512, tm=256)
    d_a, d_m, dga, dgm = _mm("mix_out_bwd", [[(dmix, W["out"])]],
                             [br_a, br_m, (pg, 0), (pg, 1)], _merge_bwd,
                             [BF16] * 4, cn=512, nt=True)
    G["out"] = _mm_tn("dw_out", merged, dmix, BF16, 1024, 512)
    dya, = _mm("branch_attn_bwd", [[(d_a, W["ba"])]], [], _first, [F32], cn=512)
    dym, = _mm("branch_mlstm_bwd", [[(d_m, W["bm"])]], [], _first, [F32], cn=512)
    G["ba"] = _mm_tn("dw_branch_attn", d_a, ya, BF16, 1024, 512)
    G["bm"] = _mm_tn("dw_branch_mlstm", d_m, ym, BF16, 1024, 512)
    dhm, do_m, acc_n = _mlstm_out_bwd(hm, pvo, sp["norm_w"], dym)
    dqk, dv_m, dgc, dgr = _mlstm_bwd(qk, pvo, pif, bcol, grow, brow, cs, ns, ms, dhm)
    dif, acc_g = _gate_bwd(dgc, jnp.pad(dgr.T, ((0, 0), (0, 120))), pif, bcol)
    dpre, acc_c = _conv_bwd_pre(pqk, sp["conv_w"], sp["conv_b"], dqk)
    du = _conv_bwd_in(dpre, sp["conv_w"])
    dq_a, dcur, dprv, dsink = _attn_bwd(pa, cos, sin, sp["sinks"], dya)
    dkv = _attn_kv_combine(dcur, dprv, cos, sin)
    dproj = {"q": dq_a, "kv": dkv, "mqk": du, "mv": dv_m, "mo": do_m, "if": dif, "ga": dga, "gm": dgm}
    for k, _ in W_IN_PIECES:
        G[k] = _mm_tn("dw_in_" + k, dproj[k], h, BF16, dproj[k].shape[1], 1024)
    w_tied = dict(W, **{"if": W["if"] + mixer_grads(G).astype(BF16)})
    dx, acc_p = _mm_rows("proj_bwd", [[(dproj[k], w_tied[k]) for k, _ in W_IN_PIECES]],
                         [x, dx1, sp["g_pre_mix"], sc_m], _pre_norm_bwd_rows, [wide], [(8, D_MODEL)],
                         cn=512, tm=256)

    small = {
        "mod": jnp.concatenate([acc_p[1], acc_p[0], acc_r[3], acc_r[1], acc_r[0], acc_l[0]]),
        "g_pre_mix": acc_p[2], "g_post_mix": acc_r[4], "b_if": acc_g[0, :8],
        "conv_w": acc_c[:CONV_WIDTH].reshape(-1), "conv_b": acc_c[CONV_WIDTH],
        "sinks": dsink[:, 0], "norm_w": acc_n[0], "g_pre_ffn": acc_r[2], "g_post_ffn": acc_l[1]}
    return loss, dx, small


IN_WIDTH = sum(n for _, n in W_IN_PIECES)
IN_SHARD = IN_WIDTH // 4
IN_SHARD_PAD = -(-IN_SHARD // 32) * 32


def _split_w_in(w_in_t):
    out, off = {}, 0
    for k, n in W_IN_PIECES:
        out[k] = w_in_t[off:off + n]
        off += n
    out["if"] = jnp.pad(out["if"], ((0, 120), (0, 0)))
    return out


def _halves(a):
    return a.reshape(4, 2, a.shape[0] // 8, a.shape[1])


SMALL = (("b_ada", 6144), ("g_pre_mix", 1024), ("g_post_mix", 1024), ("b_if", 128), ("conv_w", 4096),
         ("conv_b", 1024), ("sinks", 128), ("norm_w", 512), ("g_pre_ffn", 1024), ("g_post_ffn", 1024))
SMALL_LEN = 8 * 2048


def _pack_small(vals):
    parts = []
    for k, n in SMALL:
        v = vals[k].reshape(-1)
        parts.append(jnp.pad(v, (0, n - v.shape[0])))
    flat = jnp.concatenate(parts)
    return jnp.pad(flat, (0, SMALL_LEN - flat.shape[0]))


def _unpack_small(flat, shapes):
    out, off = {}, 0
    for k, n in SMALL:
        size = 1
        for d in shapes[k]:
            size *= d
        out[k] = flat[off:off + size].reshape(shapes[k])
        off += n
    return out


def kernel(x, c, positions, w_ada, b_ada, g_pre_mix, g_post_mix, w_in, b_if, conv_w, conv_b, attn_sinks, mlstm_norm_w, w_branch_attn, w_branch_mlstm, w_out, g_pre_ffn, g_post_ffn, w_ffn_gate, w_ffn_up, w_ffn_down, loss_target, m_w_ada, m_b_ada, m_g_pre_mix, m_g_post_mix, m_w_in, m_b_if, m_conv_w, m_conv_b, m_attn_sinks, m_mlstm_norm_w, m_w_branch_attn, m_w_branch_mlstm, m_w_out, m_g_pre_ffn, m_g_post_ffn, m_w_ffn_gate, m_w_ffn_up, m_w_ffn_down, v_w_ada, v_b_ada, v_g_pre_mix, v_g_post_mix, v_w_in, v_b_if, v_conv_w, v_conv_b, v_attn_sinks, v_mlstm_norm_w, v_w_branch_attn, v_w_branch_mlstm, v_w_out, v_g_pre_ffn, v_g_post_ffn, v_w_ffn_gate, v_w_ffn_up, v_w_ffn_down):
    xi, yi, ci = _place()
    chip = 2 * xi + yi
    dev = 2 * chip + ci
    T = x.shape[1]
    ada_cols = w_ada.shape[2]

    place = jnp.stack([chip, ci]).astype(jnp.int32)

    def my_half(a):
        n = a.shape[0] // 2
        return lax.dynamic_slice_in_dim(a, ci * n, n, axis=0).astype(BF16)

    blk = jnp.concatenate([c.reshape(-1), conv_w.reshape(-1)]).reshape(8, 256)
    got = _all_gather8("gather_cond", blk, pltpu.VMEM).reshape(8, 2048)
    c_all = got[:, :D_MODEL].astype(BF16)
    conv_full = got[::2, D_MODEL:].reshape(4, CONV_WIDTH, -1).transpose(1, 0, 2).reshape(CONV_WIDTH, -1)

    b_sh = lax.dynamic_slice_in_dim(b_ada, chip * ada_cols, ada_cols, axis=1)
    mod_part, = _mm("ada_mod", [[(c_all, w_ada[0].astype(BF16))]], [b_sh],
                    lambda ps, es: (ps[0] + es[0],), [F32], cn=512, tm=8)
    mod_all = _all_gather8("gather_mod", mod_part, pltpu.VMEM).reshape(4, 2, 8, ada_cols)[:, 0]
    mod = lax.dynamic_index_in_dim(mod_all, dev, axis=1, keepdims=False).reshape(6, 1, D_MODEL)

    def gather_start(name, blks, after):
        return _split_start(name + "_start", blks, _place_blocks(name + "_place", blks, place),
                            _gather_sends, 4, after)

    def gather_wait(name, started, after):
        return _forward_sibling(name + "_forward", _split_wait(name + "_wait", started, after, _gather_lands, 4))

    in_flat = lambda a: jnp.pad(a[0].T.reshape(-1, 128), ((0, (IN_SHARD_PAD - IN_SHARD) * 8), (0, 0)))
    in_started = gather_start("in_gather", [my_half(in_flat(w_in))], mod)
    late_keys = ("fg", "fu", "fd", "out", "ba", "bm")
    late_started = gather_start(
        "late_gather",
        [my_half(w_ffn_gate[0].T), my_half(w_ffn_up[0].T), my_half(w_ffn_down[0]), my_half(w_out[0]),
         my_half(w_branch_attn[0].T), my_half(w_branch_mlstm[0].T)], in_started[4])
    mod = mod + (in_started[4][0, 0] + late_started[4][0, 0])

    def in_weights(after):
        g_in, = gather_wait("in_gather", in_started, after)
        return _split_w_in(g_in.reshape(4, IN_SHARD_PAD, D_MODEL)[:, :IN_SHARD].reshape(IN_WIDTH, D_MODEL))

    def late_weights(after):
        lands = gather_wait("late_gather", late_started, after)
        return {k: a.reshape(-1, a.shape[-1]) for k, a in zip(late_keys, lands)}

    sent = {}

    def scatter_start(name, groups):
        pairs, lands = _pair_sum(name + "_pair_sum", groups, _swap_halves_sibling(name + "_pair", groups), place)
        sent[name] = _split_start(name + "_start", pairs, lands, _scatter_sends, 3, pairs[0])
        return sent[name][4][0, 0]

    def ffn_grads(g_fg, g_fu, g_fd):
        return scatter_start("rs_ffn", [_halves(g_fg), _halves(g_fu), _halves(g_fd)])

    def mixer_grads(G):
        g_in_t = jnp.concatenate([G[k][:n] for k, n in W_IN_PIECES]).reshape(4, IN_SHARD, D_MODEL)
        g_in_t = jnp.pad(g_in_t, ((0, 0), (0, IN_SHARD_PAD - IN_SHARD), (0, 0)))
        return scatter_start("rs_mix", [g_in_t.reshape(4, 2, IN_SHARD_PAD * 4, 128), _halves(G["out"]),
                                        _halves(G["ba"]), _halves(G["bm"])])

    sp = {"g_pre_mix": g_pre_mix, "g_post_mix": g_post_mix, "b_if": b_if, "conv_w": conv_full,
          "conv_b": conv_b, "sinks": attn_sinks, "norm_w": mlstm_norm_w, "g_pre_ffn": g_pre_ffn,
          "g_post_ffn": g_post_ffn}
    loss, dx, small = _local_step(x[0], loss_target[0], positions.reshape(T, 1), [mod[i] for i in range(6)],
                                  sp, in_weights, late_weights, ffn_grads, mixer_grads)

    reds = (_sum4("rs_ffn_chip_sum", _split_wait("rs_ffn_wait", sent["rs_ffn"], dx, _scatter_lands, 3), place)
            + _sum4("rs_mix_chip_sum", _split_wait("rs_mix_wait", sent["rs_mix"], dx, _scatter_lands, 3), place))
    gsh = {k: s.reshape(-1, s.shape[-1])
           for k, s in zip(("fg", "fu", "fd", "w_in", "out", "ba", "bm"), _share_halves("rs_share", reds))}
    gsh["w_in"] = gsh["w_in"][:IN_SHARD * 8]

    small["b_ada"] = small.pop("mod")
    vec = _pack_small(small).reshape(8, 2048)
    g_all = _all_gather8("gather_small", vec, pltpu.VMEM).reshape(8, SMALL_LEN)
    dmod_sh = lax.dynamic_slice_in_dim(g_all[:, :6 * D_MODEL], chip * ada_cols, ada_cols, axis=1)
    g_w_ada = _mm_tn("dw_ada", c_all, dmod_sh.astype(BF16), F32, D_MODEL, 512, 8)

    smalls = {"b_ada": (b_ada, m_b_ada, v_b_ada), "g_pre_mix": (g_pre_mix, m_g_pre_mix, v_g_pre_mix),
              "g_post_mix": (g_post_mix, m_g_post_mix, v_g_post_mix), "b_if": (b_if, m_b_if, v_b_if),
              "conv_w": None, "conv_b": (conv_b, m_conv_b, v_conv_b),
              "sinks": (attn_sinks, m_attn_sinks, v_attn_sinks),
              "norm_w": (mlstm_norm_w, m_mlstm_norm_w, v_mlstm_norm_w),
              "g_pre_ffn": (g_pre_ffn, m_g_pre_ffn, v_g_pre_ffn),
              "g_post_ffn": (g_post_ffn, m_g_post_ffn, v_g_post_ffn)}
    shapes = {k: (t[0].shape if t is not None else (1, CONV_WIDTH, D_MODEL)) for k, t in smalls.items()}
    zeros = jnp.zeros((CONV_WIDTH * D_MODEL,), F32)
    packs = [_pack_small({k: (t[i] if t is not None else zeros) for k, t in smalls.items()}).reshape(1, -1)
             for i in range(3)]
    s_out = [_unpack_small(o[0], shapes) for o in _small_update(g_all, *packs)]
    g_conv = lax.dynamic_slice_in_dim(s_out[0]["conv_w"], chip * conv_w.shape[2], conv_w.shape[2], axis=2)

    res = {}
    for k, t in smalls.items():
        if t is not None:
            res[k] = tuple(o[k] for o in s_out)
    res["conv_w"] = (g_conv, *[o[None] for o in _adamw("adam_conv_w", conv_w[0], g_conv[0], m_conv_w[0], v_conv_w[0])])
    res["w_ada"] = (g_w_ada[None], *[o[None] for o in _adamw("adam_w_ada", w_ada[0], g_w_ada, m_w_ada[0], v_w_ada[0])])
    bigs = {"w_in": (w_in, m_w_in, v_w_in), "ba": (w_branch_attn, m_w_branch_attn, v_w_branch_attn),
            "bm": (w_branch_mlstm, m_w_branch_mlstm, v_w_branch_mlstm), "out": (w_out, m_w_out, v_w_out),
            "fg": (w_ffn_gate, m_w_ffn_gate, v_w_ffn_gate), "fu": (w_ffn_up, m_w_ffn_up, v_w_ffn_up),
            "fd": (w_ffn_down, m_w_ffn_down, v_w_ffn_down)}
    for k, (w, m, v) in bigs.items():
        if k in ("w_in", "fg", "fu"):
            form = (lambda a: a[0].T.reshape(-1, 128)) if k == "w_in" else (lambda a: a[0].T)
            outs = (gsh[k], *_adamw("adam_" + k, form(w), gsh[k], form(m), form(v)))
            res[k] = tuple(o.reshape(w.shape[2], w.shape[1]).T[None] for o in outs)
        else:
            g = gsh[k].T if k in ("ba", "bm") else gsh[k]
            res[k] = (g[None], *[o[None] for o in _adamw("adam_" + k, w[0], g, m[0], v[0])])

    order = ("w_ada", "b_ada", "g_pre_mix", "g_post_mix", "w_in", "b_if", "conv_w", "conv_b", "sinks",
             "norm_w", "ba", "bm", "out", "g_pre_ffn", "g_post_ffn", "fg", "fu", "fd")
    total = lax.psum(loss[0, 0], ("x", "y", "c"))
    return (total, dx[None], *[res[k][0] for k in order], *[res[k][1] for k in order],
            *[res[k][2] for k in order], *[res[k][3] for k in order])
```

```python
import functools

import jax
import jax.numpy as jnp
from jax import lax
from jax.experimental import pallas as pl
from jax.experimental.pallas import tpu as pltpu

F32, BF16 = jnp.float32, jnp.bfloat16
MESH = pl.DeviceIdType.MESH

D_MODEL = 1024
N_Q_HEADS, N_KV_HEADS, HEAD_DIM, WINDOW = 8, 2, 64, 128
ROPE_THETA = 10000.0
MLSTM_HEADS, MLSTM_HEAD_DIM, MLSTM_CHUNK, CONV_WIDTH = 4, 128, 64, 4
D_FF = 2816
NORM_EPS = 1e-6
ADAM_LR, ADAM_B1, ADAM_B2, ADAM_EPS, ADAM_WD, ADAM_STEP = 0.001, 0.9, 0.999, 1e-08, 0.01, 10

VMEM_LIMIT = 56 * 1024 * 1024
ROW_TILE = 256
MM_TM = 512
MM_TT = 1024
ATTN_BLK = WINDOW
STEP_ROWS = 2 * MLSTM_CHUNK
NEG_INF = float("-inf")


def _params(sem):
    return pltpu.CompilerParams(dimension_semantics=sem, vmem_limit_bytes=VMEM_LIMIT)


def _sds(shape, dtype):
    return jax.ShapeDtypeStruct(shape, dtype)


def _sigmoid(x):
    return 1.0 / (1.0 + jnp.exp(-x))


def _dot(a, b, ca, cb):
    return lax.dot_general(a, b, (((ca,), (cb,)), ((), ())), preferred_element_type=F32)


def _bdot(a, b, ca, cb):
    return lax.dot_general(a, b, (((ca,), (cb,)), ((0,), (0,))), preferred_element_type=F32)


def _bdot_rows(a, b):
    return jnp.stack([_dot(a[h], b[h], 0, 0) for h in range(a.shape[0])])


def _mm(name, prods, extras, epi, out_dtypes, cn, nt=False, tm=MM_TM):
    flat = [ab for p in prods for ab in p]
    counts = [len(p) for p in prods]
    M = flat[0][0].shape[0]
    N = flat[0][1].shape[0 if nt else 1]
    tm = min(tm, M)
    n_in = 2 * len(flat) + len(extras)

    def body(*refs):
        ins, outs = refs[:n_in], refs[n_in:]
        for j in range(N // cn):
            cols = slice(j * cn, (j + 1) * cn)
            k, ps = 0, []
            for cnt in counts:
                acc = None
                for _ in range(cnt):
                    b = ins[k + 1][cols, :] if nt else ins[k + 1][:, cols]
                    d = _dot(ins[k][...], b, 1, 1 if nt else 0)
                    acc = d if acc is None else acc + d
                    k += 2
                ps.append(acc)
            res = epi(ps, [r[:, cols] for r in ins[k:]])
            for o, r in zip(outs, res):
                o[:, cols] = r.astype(o.dtype)

    in_specs, args = [], []
    for a, b in flat:
        in_specs.append(pl.BlockSpec((tm, a.shape[1]), lambda i: (i, 0)))
        in_specs.append(pl.BlockSpec(b.shape, lambda i: (0, 0), pipeline_mode=pl.Buffered(1)))
        args += [a, b]
    for e in extras:
        e, off = e if isinstance(e, tuple) else (e, 0)
        rows = 1 if e.shape[0] == 1 else tm
        in_specs.append(pl.BlockSpec((rows, N), lambda i, off=off, rows=rows: (0 if rows == 1 else i, off)))
        args.append(e)
    return pl.pallas_call(
        body, name=name, grid=(M // tm,), in_specs=in_specs,
        out_specs=[pl.BlockSpec((tm, N), lambda i: (i, 0)) for _ in out_dtypes],
        out_shape=[_sds((M, N), dt) for dt in out_dtypes],
        compiler_params=_params(("parallel",)))(*args)


def _mm_rows(name, prods, extras, epi, outs, accs, cn, nt=False, tm=MM_TM):
    flat = [ab for p in prods for ab in p]
    counts = [len(p) for p in prods]
    M = flat[0][0].shape[0]
    N = flat[0][1].shape[0 if nt else 1]
    tm = min(tm, M)
    n_mm, n_in, n_out = 2 * len(flat), 2 * len(flat) + len(extras), len(outs)

    def body(*refs):
        ins, out_refs, acc_refs = refs[:n_in], refs[n_in:n_in + n_out], refs[n_in + n_out:]

        @pl.when(pl.program_id(0) == 0)
        def _():
            for a in acc_refs:
                a[...] = jnp.zeros_like(a)

        chunks = [[] for _ in counts]
        for j in range(N // cn):
            cols = slice(j * cn, (j + 1) * cn)
            k = 0
            for p, cnt in enumerate(counts):
                acc = None
                for _ in range(cnt):
                    b = ins[k + 1][cols, :] if nt else ins[k + 1][:, cols]
                    d = _dot(ins[k][...], b, 1, 1 if nt else 0)
                    acc = d if acc is None else acc + d
                    k += 2
                chunks[p].append(acc)
        ps = [c[0] if len(c) == 1 else jnp.concatenate(c, axis=1) for c in chunks]
        res, incs = epi(ps, [r[...] for r in ins[n_mm:]])
        for o, r in zip(out_refs, res):
            o[...] = r.astype(o.dtype)
        for a, inc in zip(acc_refs, incs):
            a[...] += inc

    in_specs, args = [], []
    for a, b in flat:
        in_specs.append(pl.BlockSpec((tm, a.shape[1]), lambda i: (i, 0)))
        in_specs.append(pl.BlockSpec(b.shape, lambda i: (0, 0), pipeline_mode=pl.Buffered(1)))
        args += [a, b]
    for e in extras:
        rows = 1 if e.shape[0] == 1 else tm
        in_specs.append(pl.BlockSpec((rows, e.shape[1]), lambda i, rows=rows: (0 if rows == 1 else i, 0)))
        args.append(e)
    return pl.pallas_call(
        body, name=name, grid=(M // tm,), in_specs=in_specs,
        out_specs=[pl.BlockSpec((tm, w), lambda i: (i, 0)) for w, _ in outs]
        + [pl.BlockSpec(s, lambda i: (0, 0)) for s in accs],
        out_shape=[_sds((M, w), dt) for w, dt in outs] + [_sds(s, F32) for s in accs],
        compiler_params=_params(("arbitrary",)))(*args)


def _mm_tn(name, a, b, out_dtype, tk, tn, tt=MM_TT):
    T, Ka = a.shape
    N = b.shape[1]
    tt = min(tt, T)
    steps = T // tt

    def body(a_ref, b_ref, o_ref, acc):
        t = pl.program_id(2)

        @pl.when(t == 0)
        def _():
            acc[...] = jnp.zeros_like(acc)

        acc[...] += _dot(a_ref[...], b_ref[...], 0, 0)

        @pl.when(t == steps - 1)
        def _():
            o_ref[...] = acc[...].astype(o_ref.dtype)

    return pl.pallas_call(
        body, name=name, grid=(Ka // tk, N // tn, steps),
        in_specs=[pl.BlockSpec((tt, tk), lambda i, j, t: (t, i)),
                  pl.BlockSpec((tt, tn), lambda i, j, t: (t, j))],
        out_specs=pl.BlockSpec((tk, tn), lambda i, j, t: (i, j)),
        out_shape=_sds((Ka, N), out_dtype),
        scratch_shapes=[pltpu.VMEM((tk, tn), F32)],
        compiler_params=_params(("parallel", "parallel", "arbitrary")))(a, b)


def _first(ps, es):
    return (ps[0],)


def _rows(name, body, ins, out_shapes, T, tr=ROW_TILE):
    tr = min(tr, T)

    def spec(shape):
        if shape[0] == T:
            return pl.BlockSpec((tr,) + tuple(shape[1:]), lambda i: (i,) + (0,) * (len(shape) - 1))
        return pl.BlockSpec(tuple(shape), lambda i: (0,) * len(shape))

    return pl.pallas_call(
        body, name=name, grid=(T // tr,),
        in_specs=[spec(a.shape) for a in ins], out_specs=[spec(s.shape) for s in out_shapes],
        out_shape=out_shapes, compiler_params=_params(("arbitrary",)))(*ins)


def _rms(x):
    r = lax.rsqrt(jnp.mean(x * x, axis=-1, keepdims=True) + NORM_EPS)
    return x * r, r


def _rms_bwd(dxn, xn, r):
    return r * (dxn - xn * jnp.mean(dxn * xn, axis=-1, keepdims=True))


def _colsum(v):
    return jnp.sum(v, axis=0, keepdims=True)


def _proj_in(x, g, sc, sh, groups):
    T = x.shape[0]
    tm = min(MM_TM, T)
    ng = len(groups)

    def body(x_ref, g_ref, sc_ref, sh_ref, *rest):
        w_refs, h_ref, out_refs = rest[:ng], rest[ng], rest[ng + 1:]
        xn, _ = _rms(x_ref[...])
        h = (xn * g_ref[...] * (1.0 + sc_ref[...]) + sh_ref[...]).astype(BF16)
        h_ref[...] = h
        for w_ref, o_ref, (w, _, cn) in zip(w_refs, out_refs, groups):
            for j in range(w.shape[0] // cn):
                cols = slice(j * cn, (j + 1) * cn)
                o_ref[:, cols] = _dot(h, w_ref[cols, :], 1, 1).astype(o_ref.dtype)

    row = pl.BlockSpec((1, D_MODEL), lambda i: (0, 0))
    tile = lambda w: pl.BlockSpec((tm, w), lambda i: (i, 0))
    return pl.pallas_call(
        body, name="proj_in", grid=(T // tm,),
        in_specs=[tile(D_MODEL), row, row, row] + [
            pl.BlockSpec(w.shape, lambda i: (0, 0), pipeline_mode=pl.Buffered(1)) for w, _, _ in groups],
        out_specs=[tile(D_MODEL)] + [tile(w.shape[0]) for w, _, _ in groups],
        out_shape=[_sds((T, D_MODEL), BF16)] + [_sds((T, w.shape[0]), dt) for w, dt, _ in groups],
        compiler_params=_params(("parallel",)))(x, g, sc, sh, *[w for w, _, _ in groups])


def _acc_rows(rows):
    w = rows[0].shape[1]
    return jnp.concatenate(rows + [jnp.zeros((8 - len(rows), w), F32)], axis=0)


def _res_norm_rows(ps, es):
    mix = ps[0]
    x, gate, gp, g2, sc, sh = es
    mh, _ = _rms(mix)
    x1 = x + gate * (mh * gp)
    xn, _ = _rms(x1)
    return [mix, x1, xn * g2 * (1.0 + sc) + sh], []


def _final_loss_rows(ps, es):
    x1, tgt, gate, gp = es
    fh, r = _rms(ps[0])
    e = x1 + gate * (fh * gp) - tgt
    loss = 0.5 * jnp.sum(jnp.mean(e * e, axis=-1, keepdims=True))
    dy = e * (1.0 / D_MODEL)
    acc = _acc_rows([_colsum(dy * fh * gp), _colsum(dy * gate * fh)])
    return [dy, _rms_bwd(dy * gate * gp, fh, r)], [acc, jnp.full((1, 128), loss, F32)]


def _res_norm_bwd_rows(ps, es):
    dh = ps[0]
    x1, mix, dy, sc, gate, g2, gp = es
    xn, r1 = _rms(x1)
    rows = [_colsum(dh * xn * g2), _colsum(dh), _colsum(dh * (1.0 + sc) * xn)]
    dx1 = dy + _rms_bwd(dh * (1.0 + sc) * g2, xn, r1)
    mh, rm = _rms(mix)
    rows += [_colsum(dx1 * mh * gp), _colsum(dx1 * gate * mh)]
    return [dx1, _rms_bwd(dx1 * gate * gp, mh, rm)], [_acc_rows(rows)]


def _pre_norm_bwd_rows(ps, es):
    dh = ps[0]
    x, dx1, g, sc = es
    xn, r = _rms(x)
    rows = [_colsum(dh * xn * g), _colsum(dh), _colsum(dh * (1.0 + sc) * xn)]
    return [dx1 + _rms_bwd(dh * (1.0 + sc) * g, xn, r)], [_acc_rows(rows)]


def _rope_tables(pos_col, inv_freq):
    T = pos_col.shape[0]

    def body(p_ref, f_ref, c_ref, s_ref):
        ang = p_ref[...].astype(F32) * f_ref[...]
        lane = lax.broadcasted_iota(jnp.int32, ang.shape, 1)
        c_ref[...] = jnp.cos(ang)
        s_ref[...] = jnp.where(lane % HEAD_DIM < HEAD_DIM // 2, -1.0, 1.0) * jnp.sin(ang)

    return _rows("rope_tables", body, [pos_col, inv_freq],
                 [_sds((T, 128), F32), _sds((T, 128), F32)], T, tr=512)


def _swap_halves(t):
    W = t.shape[1]
    lane = lax.broadcasted_iota(jnp.int32, t.shape, 1)
    half = HEAD_DIM // 2
    return jnp.where(lane % HEAD_DIM < half, pltpu.roll(t, W - half, 1), pltpu.roll(t, half, 1))


def _widen(c, W):
    return c if W == 128 else jnp.concatenate([c] * (W // 128), axis=1)


def _rope(t, c, s):
    W = t.shape[1]
    return t * _widen(c, W) + _swap_halves(t) * _widen(s, W)


def _unrope(dy, c, s):
    W = dy.shape[1]
    return dy * _widen(c, W) + _swap_halves(dy * _widen(s, W))


def _attn_mask(n):
    qi = lax.broadcasted_iota(jnp.int32, (ATTN_BLK, 2 * ATTN_BLK), 0)
    kj = lax.broadcasted_iota(jnp.int32, (ATTN_BLK, 2 * ATTN_BLK), 1)
    rel = kj - ATTN_BLK
    return (rel <= qi) & (qi - rel < WINDOW) & ((n > 0) | (kj >= ATTN_BLK))


def _attn_load(cur, prv, cc, sc, cp, sp):
    x, xp = cur[...], prv[...]
    q = _rope(x[:, :512], cc[...], sc[...]) * (HEAD_DIM ** -0.5)
    k = jnp.concatenate([_rope(xp[:, 512:640], cp[...], sp[...]),
                         _rope(x[:, 512:640], cc[...], sc[...])], axis=0)
    v = jnp.concatenate([xp[:, 640:768], x[:, 640:768]], axis=0)
    return q, k, v


ROLLED = tuple(h for h in range(N_Q_HEADS) if h % 2 != h // (N_Q_HEADS // N_KV_HEADS))


def _pair_heads(t):
    half = lax.broadcasted_iota(jnp.int32, (ATTN_BLK, 128), 1) // HEAD_DIM
    return jnp.stack([jnp.where(half == h % 2, t[:, 128 * (h // 2):128 * (h // 2) + 128], 0.0)
                      for h in range(N_Q_HEADS)])


def _kv_heads(t):
    half = lax.broadcasted_iota(jnp.int32, t.shape, 1) // HEAD_DIM
    tr = pltpu.roll(t, HEAD_DIM, 1)
    return jnp.stack([jnp.where(half == h % 2, tr if h in ROLLED else t, 0.0)
                      for h in range(N_Q_HEADS)])


def _sink_column(snk):
    return jnp.stack([jnp.full((1, 1), snk[0, h], F32) for h in range(N_Q_HEADS)])


def _attn_probs(qh, kh, mask, sink):
    s = jnp.where(mask, _bdot(qh, kh, 2, 2), NEG_INF)
    m = jnp.maximum(jnp.max(s, axis=-1, keepdims=True), sink)
    p = jnp.exp(s - m)
    es = jnp.exp(sink - m)
    rl = 1.0 / (jnp.sum(p, axis=-1, keepdims=True) + es)
    return p, es, rl


def _attn_specs(nb):
    blk = lambda w: pl.BlockSpec((ATTN_BLK, w), lambda n: (n, 0))
    prv = lambda w: pl.BlockSpec((ATTN_BLK, w), lambda n: (jnp.maximum(n - 1, 0), 0))
    return [blk(768), prv(768), blk(128), blk(128), prv(128), prv(128),
            pl.BlockSpec(memory_space=pltpu.SMEM)]


def _attn_fwd(pa, cos, sin, sinks):
    T = pa.shape[0]
    nb = T // ATTN_BLK

    def body(cur, prv, cc, sc, cp, sp, snk, y_ref):
        n = pl.program_id(0)
        q, k, v = _attn_load(cur, prv, cc, sc, cp, sp)
        qh, kh, vh = _pair_heads(q).astype(BF16), _kv_heads(k).astype(BF16), _kv_heads(v).astype(BF16)
        p, _, rl = _attn_probs(qh, kh, _attn_mask(n), _sink_column(snk))
        o = _bdot(p.astype(BF16), vh, 2, 1) * rl
        for pair in range(N_Q_HEADS // 2):
            y_ref[:, 128 * pair:128 * pair + 128] = (o[2 * pair] + o[2 * pair + 1]).astype(BF16)

    return pl.pallas_call(
        body, name="attn_fwd", grid=(nb,), in_specs=_attn_specs(nb),
        out_specs=pl.BlockSpec((ATTN_BLK, 512), lambda n: (n, 0)),
        out_shape=_sds((T, 512), BF16), compiler_params=_params(("parallel",)))(
            pa, pa, cos, sin, cos, sin, sinks)


def _attn_bwd(pa, cos, sin, sinks, dy):
    T = pa.shape[0]
    nb = T // ATTN_BLK

    def body(cur, prv, cc, sc, cp, sp, snk, dy_ref, dq_ref, dcur_ref, dprv_ref, dsink_ref):
        n = pl.program_id(0)

        @pl.when(n == 0)
        def _():
            dsink_ref[...] = jnp.zeros_like(dsink_ref)

        q, k, v = _attn_load(cur, prv, cc, sc, cp, sp)
        qh, kh, vh = _pair_heads(q).astype(BF16), _kv_heads(k).astype(BF16), _kv_heads(v).astype(BF16)
        p, es, rl = _attn_probs(qh, kh, _attn_mask(n), _sink_column(snk))
        pn = p * rl
        do = _pair_heads(dy_ref[...]).astype(BF16)
        dp = _bdot(do, vh, 2, 2)
        delta = jnp.sum(pn * dp, axis=-1, keepdims=True)
        ds = (pn * (dp - delta)).astype(BF16)
        dsink = es * rl * delta
        dq = _bdot(ds, kh, 2, 1) * (HEAD_DIM ** -0.5)
        dkh = _bdot_rows(ds, qh)
        dvh = _bdot_rows(pn.astype(BF16), do)

        def fold(t):
            same = [t[h] for h in range(N_Q_HEADS) if h not in ROLLED]
            moved = [t[h] for h in ROLLED]
            return sum(same[1:], same[0]) + pltpu.roll(sum(moved[1:], moved[0]), HEAD_DIM, 1)

        dk, dv = fold(dkh), fold(dvh)
        for h in range(N_Q_HEADS):
            dsink_ref[h:h + 1, :] += -jnp.sum(dsink[h])
        for pair in range(N_Q_HEADS // 2):
            dq_ref[:, 128 * pair:128 * pair + 128] = _unrope(
                dq[2 * pair] + dq[2 * pair + 1], cc[...], sc[...]).astype(BF16)
        dcur_ref[:, 0:128] = dk[ATTN_BLK:]
        dcur_ref[:, 128:256] = dv[ATTN_BLK:]
        dprv_ref[:, 0:128] = dk[:ATTN_BLK]
        dprv_ref[:, 128:256] = dv[:ATTN_BLK]

    blk = lambda w: pl.BlockSpec((ATTN_BLK, w), lambda n: (n, 0))
    return pl.pallas_call(
        body, name="attn_bwd", grid=(nb,), in_specs=_attn_specs(nb) + [blk(512)],
        out_specs=[blk(512), blk(256), blk(256), pl.BlockSpec((8, 128), lambda n: (0, 0))],
        out_shape=[_sds((T, 512), BF16), _sds((T, 256), F32), _sds((T, 256), F32),
                   _sds((8, 128), F32)],
        compiler_params=_params(("arbitrary",)))(pa, pa, cos, sin, cos, sin, sinks, dy)


def _attn_kv_combine(dcur, dprv, cos, sin):
    T = dcur.shape[0]
    nb = T // ATTN_BLK

    def body(c_ref, p_ref, cc, sc, o_ref):
        n = pl.program_id(0)
        t = c_ref[...] + jnp.where(n < nb - 1, p_ref[...], 0.0)
        o_ref[:, 0:128] = _unrope(t[:, 0:128], cc[...], sc[...]).astype(BF16)
        o_ref[:, 128:256] = t[:, 128:256].astype(BF16)

    blk = lambda w: pl.BlockSpec((ATTN_BLK, w), lambda n: (n, 0))
    nxt = pl.BlockSpec((ATTN_BLK, 256), lambda n: (jnp.minimum(n + 1, nb - 1), 0))
    return pl.pallas_call(
        body, name="attn_kv_combine", grid=(nb,), in_specs=[blk(256), nxt, blk(128), blk(128)],
        out_specs=blk(256), out_shape=_sds((T, 256), BF16),
        compiler_params=_params(("parallel",)))(dcur, dprv, cos, sin)


CONV_COLS = 2 * MLSTM_HEADS * MLSTM_HEAD_DIM


def _conv_pre(cur_ref, halo_ref, w_ref, b_ref, i, tr):
    xx = jnp.concatenate([jnp.where(i > 0, halo_ref[...], 0.0), cur_ref[...]], axis=0)
    taps = [(pltpu.roll(xx, CONV_WIDTH - 1 - j, 0) if j < CONV_WIDTH - 1 else xx)[8:8 + tr]
            for j in range(CONV_WIDTH)]
    pre = b_ref[...]
    for j in range(CONV_WIDTH):
        pre = pre + taps[j] * w_ref[j:j + 1, :]
    return pre, taps


def _conv_specs(T, tr):
    return [pl.BlockSpec((tr, CONV_COLS), lambda i: (i, 0)),
            pl.BlockSpec((8, CONV_COLS), lambda i: (jnp.maximum(i * (tr // 8) - 1, 0), 0)),
            pl.BlockSpec((CONV_WIDTH, CONV_COLS), lambda i: (0, 0)),
            pl.BlockSpec((1, CONV_COLS), lambda i: (0, 0))]


def _conv_fwd(pm, w, b):
    T = pm.shape[0]
    tr = min(ROW_TILE, T)

    def body(cur_ref, halo_ref, w_ref, b_ref, o_ref):
        pre, _ = _conv_pre(cur_ref, halo_ref, w_ref, b_ref, pl.program_id(0), tr)
        o_ref[...] = pre * _sigmoid(pre)

    return pl.pallas_call(
        body, name="conv_fwd", grid=(T // tr,), in_specs=_conv_specs(T, tr),
        out_specs=pl.BlockSpec((tr, CONV_COLS), lambda i: (i, 0)),
        out_shape=_sds((T, CONV_COLS), F32), compiler_params=_params(("parallel",)))(pm, pm, w, b)


def _conv_bwd(pqk, w, b, dqk):
    T = pqk.shape[0]
    tr = min(ROW_TILE, T)
    nt = T // tr

    def body(cur_ref, prev_ref, next_ref, w_ref, b_ref, d_ref, dnext_ref, du_ref, acc_ref):
        i = pl.program_id(0)

        @pl.when(i == 0)
        def _():
            acc_ref[...] = jnp.zeros_like(acc_ref)

        last = i == nt - 1
        xx = jnp.concatenate([jnp.where(i > 0, prev_ref[...], 0.0), cur_ref[...],
                              jnp.where(last, 0.0, next_ref[...])], axis=0)
        taps = [(pltpu.roll(xx, CONV_WIDTH - 1 - j, 0) if j < CONV_WIDTH - 1 else xx)[8:16 + tr]
                for j in range(CONV_WIDTH)]
        pre = b_ref[...]
        for j in range(CONV_WIDTH):
            pre = pre + taps[j] * w_ref[j:j + 1, :]
        sg = _sigmoid(pre)
        dd = jnp.concatenate([d_ref[...], jnp.where(last, 0.0, dnext_ref[...])], axis=0)
        dpre = dd * (sg * (1.0 + pre * (1.0 - sg)))
        for j in range(CONV_WIDTH):
            acc_ref[j:j + 1, :] += _colsum(dpre[:tr] * taps[j][:tr])
        acc_ref[CONV_WIDTH:CONV_WIDTH + 1, :] += _colsum(dpre[:tr])
        du = dpre[:tr] * w_ref[CONV_WIDTH - 1:CONV_WIDTH, :]
        for j in range(CONV_WIDTH - 1):
            k = CONV_WIDTH - 1 - j
            du = du + pltpu.roll(dpre, tr + 8 - k, 0)[:tr] * w_ref[j:j + 1, :]
        du_ref[...] = du.astype(BF16)

    tile = pl.BlockSpec((tr, CONV_COLS), lambda i: (i, 0))
    after = pl.BlockSpec((8, CONV_COLS), lambda i: (jnp.minimum((i + 1) * (tr // 8), T // 8 - 1), 0))
    before = pl.BlockSpec((8, CONV_COLS), lambda i: (jnp.maximum(i * (tr // 8) - 1, 0), 0))
    return pl.pallas_call(
        body, name="conv_bwd", grid=(nt,),
        in_specs=[tile, before, after, pl.BlockSpec((CONV_WIDTH, CONV_COLS), lambda i: (0, 0)),
                  pl.BlockSpec((1, CONV_COLS), lambda i: (0, 0)), tile, after],
        out_specs=[tile, pl.BlockSpec((8, CONV_COLS), lambda i: (0, 0))],
        out_shape=[_sds((T, CONV_COLS), BF16), _sds((8, CONV_COLS), F32)],
        compiler_params=_params(("arbitrary",)))(pqk, pqk, pqk, w, b, dqk, dqk)


def _log_sigmoid(x):
    return jnp.minimum(x, 0.0) - jnp.log1p(jnp.exp(-jnp.abs(x)))


def _chunk_cumsum(x, axis):
    idx = lax.broadcasted_iota(jnp.int32, x.shape, axis) % MLSTM_CHUNK
    k = 1
    while k < MLSTM_CHUNK:
        x = x + jnp.where(idx >= k, pltpu.roll(x, k, axis), 0.0)
        k *= 2
    return x


def _chunk_rev_cumsum(x, axis):
    n = x.shape[axis]
    idx = lax.broadcasted_iota(jnp.int32, x.shape, axis) % MLSTM_CHUNK
    k = 1
    while k < MLSTM_CHUNK:
        x = x + jnp.where(idx < MLSTM_CHUNK - k, pltpu.roll(x, n - k, axis), 0.0)
        k *= 2
    return x


def _mlstm_gates(gc_ref, bc_ref, gr_ref, br_ref):
    gc = gc_ref[...] + bc_ref[...]
    gr = gr_ref[...] + br_ref[...]
    return gc, _chunk_cumsum(_log_sigmoid(gc), 0), gr, _chunk_cumsum(_log_sigmoid(gr), 1)


def _heads(ref, base=0):
    D = MLSTM_HEAD_DIM
    return jnp.stack([ref[:, base + D * h:base + D * h + D] for h in range(MLSTM_HEADS)])


def _mlstm_inputs(q_ref, k_ref, v_ref, gc, bc, gr, br):
    H = MLSTM_HEADS
    q, v = _heads(q_ref), _heads(v_ref)
    ks = _heads(k_ref) * (MLSTM_HEAD_DIM ** -0.5)
    return dict(
        q=q, ks=ks, qb=q.astype(BF16), kb=ks.astype(BF16), vb=v.astype(BF16),
        b_col=jnp.stack([bc[:, H + h:H + h + 1] for h in range(H)]),
        i_col=jnp.stack([gc[:, h:h + 1] for h in range(H)]),
        b_row=jnp.stack([br[H + h:H + h + 1, :] for h in range(H)]),
        i_row=jnp.stack([gr[h:h + 1, :] for h in range(H)]))


def _mlstm_head(f, c_prev, n_prev, m_prev):
    L = MLSTM_CHUNK
    q, qb = f["q"], f["qb"]
    t = lax.broadcasted_iota(jnp.int32, (1, 2 * L, 2 * L), 1)
    s = lax.broadcasted_iota(jnp.int32, (1, 2 * L, 2 * L), 2)
    mask = (t // L == s // L) & (s <= t)
    d = jnp.where(mask, f["b_col"] - f["b_row"] + f["i_row"], NEG_INF)
    row = lax.broadcasted_iota(jnp.int32, (1, 2 * L, 1), 1)
    inter = f["b_col"] + jnp.where(row < L, m_prev[0], m_prev[1])
    m_t = jnp.maximum(inter, jnp.max(d, axis=-1, keepdims=True))
    w_intra = jnp.exp(d - m_t)
    w_inter = jnp.exp(inter - m_t)
    sc = _bdot(qb, f["kb"], 2, 2) * w_intra
    qc = jnp.concatenate([_bdot(qb[:, :L], c_prev[0].astype(BF16), 2, 1),
                          _bdot(qb[:, L:], c_prev[1].astype(BF16), 2, 1)], axis=1)
    qn = jnp.concatenate([jnp.sum(q[:, :L] * n_prev[0], axis=-1, keepdims=True),
                          jnp.sum(q[:, L:] * n_prev[1], axis=-1, keepdims=True)], axis=1)
    num = _bdot(sc.astype(BF16), f["vb"], 2, 1) + w_inter * qc
    den = jnp.sum(sc, axis=-1, keepdims=True) + w_inter * qn
    return dict(f, w_intra=w_intra, w_inter=w_inter, sc=sc, qc=qc, qn=qn, num=num, den=den,
                floor=jnp.exp(-m_t))


def _mlstm_update(f, ch, c, n, m):
    L = MLSTM_CHUNK
    rows = slice(L * ch, L * ch + L)
    b_col = f["b_col"][:, rows]
    g_last = b_col[:, L - 1:L]
    a_col = g_last - b_col + f["i_col"][:, rows]
    m_new = jnp.maximum(g_last + m, jnp.max(a_col, axis=1, keepdims=True))
    decay = jnp.exp(g_last + m - m_new)
    e_a = jnp.exp(a_col - m_new)
    kw = f["ks"][:, rows] * e_a
    c_new = decay * c + _bdot_rows(kw.astype(BF16), f["vb"][:, rows])
    n_new = decay * n + jnp.sum(kw, axis=1, keepdims=True)
    return c_new, n_new, m_new, decay, e_a, kw


def _mlstm_specs(T, order):
    blk = lambda w, col: pl.BlockSpec((STEP_ROWS, w), lambda s: (order(s), col))
    return [blk(512, 0), blk(512, 1), blk(512, 0), blk(128, 0),
            pl.BlockSpec((1, 128), lambda s: (0, 0)),
            pl.BlockSpec((8, STEP_ROWS), lambda s: (0, order(s))),
            pl.BlockSpec((8, 128), lambda s: (0, 0))]


def _lanes(m):
    return jnp.broadcast_to(m, m.shape[:-1] + (128,))


def _mlstm_fwd(qk, pm, gcol, bcol, grow, brow):
    T = qk.shape[0]
    steps = T // STEP_ROWS
    H, D = MLSTM_HEADS, MLSTM_HEAD_DIM

    def body(q_ref, k_ref, v_ref, gc_ref, bc_ref, gr_ref, br_ref, h_ref, cs_ref, ns_ref, ms_ref,
             c_st, n_st, m_st):
        @pl.when(pl.program_id(0) == 0)
        def _():
            c_st[...] = jnp.zeros_like(c_st)
            n_st[...] = jnp.zeros_like(n_st)
            m_st[...] = jnp.zeros_like(m_st)

        f = _mlstm_inputs(q_ref, k_ref, v_ref, *_mlstm_gates(gc_ref, bc_ref, gr_ref, br_ref))
        c0, n0, m0 = c_st[...], n_st[...], m_st[:, :, 0:1]
        c1, n1, m1, _, _, _ = _mlstm_update(f, 0, c0, n0, m0)
        c2, n2, m2, _, _, _ = _mlstm_update(f, 1, c1, n1, m1)
        f = _mlstm_head(f, (c0, c1), (n0, n1), (m0, m1))
        h = f["num"] / jnp.maximum(jnp.abs(f["den"]), f["floor"])
        for hd in range(H):
            h_ref[:, D * hd:D * hd + D] = h[hd]
        cs_ref[0], cs_ref[1] = c0, c1
        ns_ref[0], ns_ref[1] = n0, n1
        ms_ref[0], ms_ref[1] = _lanes(m0), _lanes(m1)
        c_st[...], n_st[...], m_st[...] = c2, n2, _lanes(m2)

    vec = pl.BlockSpec((2, H, 1, 128), lambda s: (s, 0, 0, 0))
    return pl.pallas_call(
        body, name="mlstm_fwd", grid=(steps,), in_specs=_mlstm_specs(T, lambda s: s),
        out_specs=[pl.BlockSpec((STEP_ROWS, 512), lambda s: (s, 0)),
                   pl.BlockSpec((2, H, 128, 128), lambda s: (s, 0, 0, 0)), vec, vec],
        out_shape=[_sds((T, 512), F32), _sds((2 * steps, H, 128, 128), F32),
                   _sds((2 * steps, H, 1, 128), F32), _sds((2 * steps, H, 1, 128), F32)],
        scratch_shapes=[pltpu.VMEM((H, 128, 128), F32), pltpu.VMEM((H, 1, 128), F32),
                        pltpu.VMEM((H, 1, 128), F32)],
        compiler_params=_params(("arbitrary",)))(qk, qk, pm, gcol, bcol, grow, brow)


def _mlstm_bwd(qk, pm, gcol, bcol, grow, brow, cs, ns, ms, dh):
    T = qk.shape[0]
    steps = T // STEP_ROWS
    H, L, D = MLSTM_HEADS, MLSTM_CHUNK, MLSTM_HEAD_DIM
    rev = lambda s: steps - 1 - s

    def body(q_ref, k_ref, v_ref, gc_ref, bc_ref, gr_ref, br_ref, cs_ref, ns_ref, ms_ref, dh_ref,
             dqk_ref, dv_ref, dgc_ref, dgr_ref, dc_st, dn_st):
        @pl.when(pl.program_id(0) == 0)
        def _():
            dc_st[...] = jnp.zeros_like(dc_st)
            dn_st[...] = jnp.zeros_like(dn_st)

        f = _mlstm_inputs(q_ref, k_ref, v_ref, *_mlstm_gates(gc_ref, bc_ref, gr_ref, br_ref))
        c_prev = (cs_ref[0], cs_ref[1])
        n_prev = (ns_ref[0], ns_ref[1])
        m_prev = (ms_ref[0, :, :, 0:1], ms_ref[1, :, :, 0:1])
        f = _mlstm_head(f, c_prev, n_prev, m_prev)
        big = jnp.abs(f["den"]) > f["floor"]
        rden = 1.0 / jnp.where(big, jnp.abs(f["den"]), f["floor"])
        dnum = _heads(dh_ref) * rden
        hdh = jnp.sum(f["num"] * dnum, axis=-1, keepdims=True)
        dden = jnp.where(big, -hdh * rden * jnp.sign(f["den"]), 0.0)
        dnum_b = dnum.astype(BF16)
        dsc = _bdot(dnum_b, f["vb"], 2, 2) + dden
        g = dsc * f["sc"]
        dv = _bdot_rows(f["sc"].astype(BF16), dnum_b)
        dqk_ = (dsc * f["w_intra"]).astype(BF16)
        dq = _bdot(dqk_, f["kb"], 2, 1)
        dks = _bdot_rows(dqk_, f["qb"])
        wdn = f["w_inter"] * dnum
        wdn_b = wdn.astype(BF16)
        wdd = f["w_inter"] * dden
        u = jnp.sum(f["qc"] * wdn, axis=-1, keepdims=True) + wdd * f["qn"]
        dks_s, dv_s, z_s, dg_s = [None, None], [None, None], [None, None], [None, None]
        dcn, dnn = dc_st[...], dn_st[...]
        for ch in (1, 0):
            rows = slice(L * ch, L * ch + L)
            _, _, _, decay, e_a, kw = _mlstm_update(f, ch, c_prev[ch], n_prev[ch], m_prev[ch])
            dcn_b = dcn.astype(BF16)
            dkw = _bdot(f["vb"][:, rows], dcn_b, 2, 2) + dnn
            dks_s[ch] = e_a * dkw
            dv_s[ch] = _bdot(kw.astype(BF16), dcn_b, 2, 1)
            z_s[ch] = e_a * jnp.sum(f["ks"][:, rows] * dkw, axis=-1, keepdims=True)
            dg_s[ch] = jnp.sum(z_s[ch], axis=1, keepdims=True) + decay * (
                jnp.sum(c_prev[ch] * dcn, axis=(1, 2), keepdims=True)
                + jnp.sum(n_prev[ch] * dnn, axis=(1, 2), keepdims=True))
            dcn = decay * dcn + _bdot_rows(f["qb"][:, rows], wdn_b[:, rows])
            dnn = decay * dnn + jnp.sum(wdd[:, rows] * f["q"][:, rows], axis=1, keepdims=True)
        dc_st[...], dn_st[...] = dcn, dnn
        dq = dq + jnp.concatenate(
            [_bdot(wdn_b[:, :L], c_prev[0].astype(BF16), 2, 2) + wdd[:, :L] * n_prev[0],
             _bdot(wdn_b[:, L:], c_prev[1].astype(BF16), 2, 2) + wdd[:, L:] * n_prev[1]], axis=1)
        dks = (dks + jnp.concatenate(dks_s, axis=1)) * (D ** -0.5)
        dv = dv + jnp.concatenate(dv_s, axis=1)
        z = jnp.concatenate(z_s, axis=1)
        row = lax.broadcasted_iota(jnp.int32, (1, STEP_ROWS, 1), 1)
        dg_col = jnp.where(row == L - 1, dg_s[0], 0.0) + jnp.where(row == 2 * L - 1, dg_s[1], 0.0)
        db_col = jnp.sum(g, axis=-1, keepdims=True) + u - z + dg_col
        g_row = jnp.sum(g, axis=1, keepdims=True)
        lane = lax.broadcasted_iota(jnp.int32, (STEP_ROWS, 128), 1)
        sub = lax.broadcasted_iota(jnp.int32, (8, STEP_ROWS), 0)
        dgc = jnp.zeros((STEP_ROWS, 128), F32)
        dgr = jnp.zeros((8, STEP_ROWS), F32)
        for hd in range(H):
            dgc = dgc + jnp.where(lane == hd, z[hd], 0.0) + jnp.where(lane == H + hd, db_col[hd], 0.0)
            dgr = dgr + jnp.where(sub == hd, g_row[hd], 0.0) - jnp.where(sub == H + hd, g_row[hd], 0.0)
            dqk_ref[:, D * hd:D * hd + D] = dq[hd]
            dqk_ref[:, H * D + D * hd:H * D + D * hd + D] = dks[hd]
            dv_ref[:, D * hd:D * hd + D] = dv[hd].astype(BF16)
        dgc_ref[...] = dgc
        dgr_ref[...] = dgr

    return pl.pallas_call(
        body, name="mlstm_bwd", grid=(steps,),
        in_specs=_mlstm_specs(T, rev) + [
            pl.BlockSpec((2, H, 128, 128), lambda s: (rev(s), 0, 0, 0)),
            pl.BlockSpec((2, H, 1, 128), lambda s: (rev(s), 0, 0, 0)),
            pl.BlockSpec((2, H, 1, 128), lambda s: (rev(s), 0, 0, 0)),
            pl.BlockSpec((STEP_ROWS, 512), lambda s: (rev(s), 0))],
        out_specs=[pl.BlockSpec((STEP_ROWS, 1024), lambda s: (rev(s), 0)),
                   pl.BlockSpec((STEP_ROWS, 512), lambda s: (rev(s), 0)),
                   pl.BlockSpec((STEP_ROWS, 128), lambda s: (rev(s), 0)),
                   pl.BlockSpec((8, STEP_ROWS), lambda s: (0, rev(s)))],
        out_shape=[_sds((T, 1024), F32), _sds((T, 512), BF16), _sds((T, 128), F32), _sds((8, T), F32)],
        scratch_shapes=[pltpu.VMEM((H, 128, 128), F32), pltpu.VMEM((H, 1, 128), F32)],
        compiler_params=_params(("arbitrary",)))(qk, qk, pm, gcol, bcol, grow, brow, cs, ns, ms, dh)


def _gate_bwd(dgc, dgr_t, gcol, bcol):
    T = dgc.shape[0]

    def body(a_ref, b_ref, g_ref, bias_ref, o_ref, acc_ref):
        @pl.when(pl.program_id(0) == 0)
        def _():
            acc_ref[...] = jnp.zeros_like(acc_ref)

        d = a_ref[...] + b_ref[...]
        lane = lax.broadcasted_iota(jnp.int32, d.shape, 1)
        is_f = (lane >= MLSTM_HEADS) & (lane < 2 * MLSTM_HEADS)
        dlogf = _chunk_rev_cumsum(jnp.where(is_f, d, 0.0), 0)
        out = jnp.where(is_f, dlogf * _sigmoid(-(g_ref[...] + bias_ref[...])), d)
        o_ref[...] = out.astype(BF16)
        acc_ref[0:1, :] += _colsum(out)

    return _rows("gate_bwd", body, [dgc, dgr_t, gcol, bcol],
                 [_sds((T, 128), BF16), _sds((8, 128), F32)], T)


def _head_norm(h, mu_axis=-1):
    mu = jnp.mean(h, axis=-1, keepdims=True)
    hc = h - mu
    r = lax.rsqrt(jnp.mean(hc * hc, axis=-1, keepdims=True) + NORM_EPS)
    return hc * r, r


def _mlstm_out(hm, pm, w):
    T = hm.shape[0]
    D = MLSTM_HEAD_DIM

    def body(h_ref, o_ref, w_ref, y_ref):
        for hd in range(MLSTM_HEADS):
            cols = slice(D * hd, D * hd + D)
            hn, _ = _head_norm(h_ref[:, cols])
            y_ref[:, cols] = (_sigmoid(o_ref[:, cols].astype(F32)) * hn * w_ref[:, cols]).astype(BF16)

    tr = min(ROW_TILE, T)
    return pl.pallas_call(
        body, name="mlstm_out", grid=(T // tr,),
        in_specs=[pl.BlockSpec((tr, 512), lambda i: (i, 0)), pl.BlockSpec((tr, 512), lambda i: (i, 1)),
                  pl.BlockSpec((1, 512), lambda i: (0, 0))],
        out_specs=pl.BlockSpec((tr, 512), lambda i: (i, 0)), out_shape=_sds((T, 512), BF16),
        compiler_params=_params(("parallel",)))(hm, pm, w)


def _mlstm_out_bwd_rows(ps, es):
    hm, vo, w_all = es
    D, width = MLSTM_HEAD_DIM, MLSTM_HEADS * MLSTM_HEAD_DIM
    dhs, dos, dws = [], [], []
    for hd in range(MLSTM_HEADS):
        cols = slice(D * hd, D * hd + D)
        hn, r = _head_norm(hm[:, cols])
        sg = _sigmoid(vo[:, width + D * hd:width + D * hd + D].astype(F32))
        dy, w = ps[0][:, cols], w_all[:, cols]
        dos.append(dy * hn * w * sg * (1.0 - sg))
        dyn = dy * sg
        dws.append(_colsum(dyn * hn))
        dhn = dyn * w
        dhs.append(r * (dhn - jnp.mean(dhn, axis=-1, keepdims=True)
                        - hn * jnp.mean(dhn * hn, axis=-1, keepdims=True)))
    cat = lambda parts: jnp.concatenate(parts, axis=1)
    return [cat(dhs), cat(dos)], [_acc_rows([cat(dws)])]


ADAM_TILE_ELEMS = 256 * 1024


def _adamw(name, w, g, m, v):
    R, C = w.shape
    fits = [t for t in range(8, R + 1, 8) if R % t == 0 and t * C <= ADAM_TILE_ELEMS]
    if fits or R * C <= ADAM_TILE_ELEMS:
        tr = fits[-1] if fits else R
        spec, grid = pl.BlockSpec((tr, C), lambda i: (i, 0)), (R // tr,)
    else:
        spec, grid = pl.BlockSpec((R, 128), lambda i: (0, i)), (C // 128,)
    c1 = 1.0 - ADAM_B1 ** ADAM_STEP
    c2 = 1.0 - ADAM_B2 ** ADAM_STEP

    def body(w_ref, g_ref, m_ref, v_ref, d_ref, mo_ref, vo_ref):
        g = g_ref[...]
        m = ADAM_B1 * m_ref[...] + (1.0 - ADAM_B1) * g
        v = ADAM_B2 * v_ref[...] + (1.0 - ADAM_B2) * (g * g)
        mo_ref[...] = m
        vo_ref[...] = v
        d_ref[...] = -ADAM_LR * ((m / c1) / (jnp.sqrt(v / c2) + ADAM_EPS) + ADAM_WD * w_ref[...])

    return pl.pallas_call(
        body, name=name, grid=grid, in_specs=[spec] * 4, out_specs=[spec] * 3,
        out_shape=[_sds((R, C), F32)] * 3, compiler_params=_params(("parallel",)))(w, g, m, v)


def _place():
    return lax.axis_index("x"), lax.axis_index("y"), lax.axis_index("c")


def _all_gather8(name, blk, space):
    m, n = blk.shape

    def body(x_ref, out_ref, send_sems, recv_sems, local_sem):
        x, y, c = _place()
        me, sibling = (x, y, c), (x, y, 1 - c)
        chips = [(1 - x, y), (x, 1 - y), (1 - x, 1 - y)]

        def rows(px, py, pc):
            return out_ref.at[pl.ds((4 * px + 2 * py + pc) * m, m), :]

        def copy(k, block, to, src=None):
            return pltpu.make_async_remote_copy(
                src_ref=rows(*block) if src is None else src, dst_ref=rows(*block),
                send_sem=send_sems.at[k], recv_sem=recv_sems.at[k],
                device_id=to, device_id_type=MESH)

        mine = pltpu.make_async_copy(x_ref, rows(*me), local_sem)
        mine.start()
        first = [copy(0, me, sibling, src=x_ref)]
        first += [copy(1 + j, me, (*chip, c), src=x_ref) for j, chip in enumerate(chips)]
        for cp in first:
            cp.start()
        passed = [copy(4 + j, (*chip, c), sibling) for j, chip in enumerate(chips)]
        for j, chip in enumerate(chips):
            copy(1 + j, (*chip, c), me).wait_recv()
            passed[j].start()
        copy(0, sibling, me).wait_recv()
        for j, chip in enumerate(chips):
            copy(4 + j, (*chip, 1 - c), me).wait_recv()
        for cp in first + passed:
            cp.wait_send()
        mine.wait()

    return pl.pallas_call(
        body, name=name, out_shape=_sds((8 * m, n), blk.dtype),
        in_specs=[pl.BlockSpec(memory_space=space)], out_specs=pl.BlockSpec(memory_space=space),
        scratch_shapes=[pltpu.SemaphoreType.DMA((7,)), pltpu.SemaphoreType.DMA((7,)),
                        pltpu.SemaphoreType.DMA],
        compiler_params=pltpu.CompilerParams(vmem_limit_bytes=VMEM_LIMIT))(blk)


def _hbm_specs(n):
    return [pl.BlockSpec(memory_space=pl.ANY)] * n


def _swap_halves_sibling(name, srcs):
    nw = len(srcs)

    def body(*refs):
        src_refs, dst_refs, send_sems, recv_sems = refs[:nw], refs[nw:2 * nw], refs[2 * nw], refs[2 * nw + 1]
        x, y, c = _place()
        cps = [pltpu.make_async_remote_copy(
            src_ref=src_refs[w].at[pl.ds(0, 4), 1 - c], dst_ref=dst_refs[w],
            send_sem=send_sems.at[w], recv_sem=recv_sems.at[w], device_id=(x, y, 1 - c),
            device_id_type=MESH) for w in range(nw)]
        for cp in cps:
            cp.start()
        for cp in cps:
            cp.wait()

    return pl.pallas_call(
        body, name=name, out_shape=[_sds(s.shape[:1] + s.shape[2:], s.dtype) for s in srcs],
        in_specs=_hbm_specs(nw), out_specs=_hbm_specs(nw),
        scratch_shapes=[pltpu.SemaphoreType.DMA((nw,)), pltpu.SemaphoreType.DMA((nw,))])(*srcs)


def _split_start(name, srcs, lands, copies, per_array, after):
    nw = len(srcs)

    def body(*refs):
        send_sems, recv_sems, token = refs[2 * nw + 1], refs[2 * nw + 2], refs[-1]
        for w in range(nw):
            for k, (s, d, dev) in enumerate(copies(refs[w], refs[nw + w], *_place())):
                pltpu.make_async_remote_copy(
                    src_ref=s, dst_ref=d, send_sem=send_sems.at[w * per_array + k],
                    recv_sem=recv_sems.at[w * per_array + k], device_id=dev, device_id_type=MESH).start()
        token[...] = jnp.zeros_like(token)

    hbm, sem = pl.BlockSpec(memory_space=pltpu.HBM), pl.BlockSpec(memory_space=pltpu.SEMAPHORE)
    arrays = list(srcs) + list(lands)
    out = pl.pallas_call(
        body, name=name,
        out_shape=(pltpu.SemaphoreType.DMA((nw * per_array,)), pltpu.SemaphoreType.DMA((nw * per_array,)),
                   *[pltpu.HBM(a.shape, a.dtype) for a in arrays], _sds((8, 128), F32)),
        in_specs=[hbm] * (2 * nw) + [pl.BlockSpec(memory_space=pl.ANY)],
        out_specs=(sem, sem, *[hbm] * (2 * nw), pl.BlockSpec(memory_space=pltpu.VMEM)),
        input_output_aliases={i: 2 + i for i in range(2 * nw)},
        compiler_params=pltpu.CompilerParams(has_side_effects=pltpu.SideEffectType.DATAFLOW_SIDE_EFFECTING))(
            *[pltpu.with_memory_space_constraint(a, pltpu.HBM) for a in arrays], after)
    return out[0], out[1], out[2:2 + nw], out[2 + nw:2 + 2 * nw], out[-1]


def _split_wait(name, started, after, waits, per_array):
    send_sems, recv_sems, srcs, lands, _ = started
    nw = len(srcs)

    def body(*refs):
        send_sems, recv_sems = refs[2 * nw], refs[2 * nw + 1]
        x, y, c = _place()
        for w in range(nw):
            for k, (s, d) in enumerate(waits(refs[w], refs[nw + w], x, y, c)):
                cp = pltpu.make_async_remote_copy(
                    src_ref=s, dst_ref=d, send_sem=send_sems.at[w * per_array + k],
                    recv_sem=recv_sems.at[w * per_array + k], device_id=(x, y, 1 - c),
                    device_id_type=MESH)
                cp.wait_send()
                cp.wait_recv()

    hbm, sem = pl.BlockSpec(memory_space=pltpu.HBM), pl.BlockSpec(memory_space=pltpu.SEMAPHORE)
    arrays = list(srcs) + list(lands)
    out = pl.pallas_call(
        body, name=name, out_shape=tuple(pltpu.HBM(a.shape, a.dtype) for a in arrays),
        in_specs=[hbm] * (2 * nw) + [sem, sem, pl.BlockSpec(memory_space=pl.ANY)],
        out_specs=tuple([hbm] * (2 * nw)), input_output_aliases={i: i for i in range(2 * nw)},
        compiler_params=pltpu.CompilerParams(has_side_effects=pltpu.SideEffectType.DATAFLOW_SIDE_EFFECTING))(
            *arrays, send_sems, recv_sems, after)
    return list(out[nw:])


def _other_chips(x, y):
    return [(1 - x, y), (x, 1 - y), (1 - x, 1 - y)]


def _gather_sends(src_ref, land_ref, x, y, c):
    to = land_ref.at[2 * x + y, c]
    return [(src_ref, to, (x, y, 1 - c))] + [(src_ref, to, (px, py, c)) for px, py in _other_chips(x, y)]


def _gather_lands(src_ref, land_ref, x, y, c):
    return [(src_ref, land_ref.at[2 * x + y, 1 - c])] + [
        (src_ref, land_ref.at[2 * px + py, c]) for px, py in _other_chips(x, y)]


def _scatter_sends(src_ref, land_ref, x, y, c):
    return [(src_ref.at[2 * px + py], land_ref.at[2 * x + y], (px, py, c)) for px, py in _other_chips(x, y)]


def _scatter_lands(src_ref, land_ref, x, y, c):
    return [(src_ref.at[2 * x + y], land_ref.at[2 * px + py]) for px, py in _other_chips(x, y)]


def _forward_sibling(name, lands):
    nw = len(lands)

    def body(*refs):
        land_refs, out_refs, send_sems, recv_sems = refs[:nw], refs[nw:2 * nw], refs[2 * nw], refs[2 * nw + 1]
        x, y, c = _place()
        cps = []
        for w in range(nw):
            cps += [pltpu.make_async_remote_copy(
                src_ref=land_refs[w].at[2 * px + py, c], dst_ref=out_refs[w].at[2 * px + py, c],
                send_sem=send_sems.at[w, j], recv_sem=recv_sems.at[w, j], device_id=(x, y, 1 - c),
                device_id_type=MESH) for j, (px, py) in enumerate(_other_chips(x, y))]
        for cp in cps:
            cp.start()
        for w in range(nw):
            for j, (px, py) in enumerate(_other_chips(x, y)):
                slot = out_refs[w].at[2 * px + py, 1 - c]
                pltpu.make_async_remote_copy(src_ref=slot, dst_ref=slot, send_sem=send_sems.at[w, j],
                                             recv_sem=recv_sems.at[w, j], device_id=(x, y, 1 - c),
                                             device_id_type=MESH).wait_recv()
        for cp in cps:
            cp.wait_send()

    return pl.pallas_call(
        body, name=name, out_shape=[_sds(a.shape, a.dtype) for a in lands],
        in_specs=_hbm_specs(nw), out_specs=_hbm_specs(nw), input_output_aliases={i: i for i in range(nw)},
        scratch_shapes=[pltpu.SemaphoreType.DMA((nw, 3)), pltpu.SemaphoreType.DMA((nw, 3))])(*lands)


def _share_halves(name, halves):
    nw = len(halves)

    def body(*refs):
        in_refs, out_refs, send_sems, recv_sems = refs[:nw], refs[nw:2 * nw], refs[2 * nw], refs[2 * nw + 1]
        x, y, c = _place()
        cps = [pltpu.make_async_remote_copy(
            src_ref=in_refs[w].at[c], dst_ref=out_refs[w].at[c], send_sem=send_sems.at[w],
            recv_sem=recv_sems.at[w], device_id=(x, y, 1 - c), device_id_type=MESH) for w in range(nw)]
        for cp in cps:
            cp.start()
        for w in range(nw):
            slot = out_refs[w].at[1 - c]
            pltpu.make_async_remote_copy(src_ref=slot, dst_ref=slot, send_sem=send_sems.at[w],
                                         recv_sem=recv_sems.at[w], device_id=(x, y, 1 - c),
                                         device_id_type=MESH).wait_recv()
        for cp in cps:
            cp.wait_send()

    return pl.pallas_call(
        body, name=name, out_shape=[_sds(a.shape, a.dtype) for a in halves],
        in_specs=_hbm_specs(nw), out_specs=_hbm_specs(nw), input_output_aliases={i: i for i in range(nw)},
        scratch_shapes=[pltpu.SemaphoreType.DMA((nw,)), pltpu.SemaphoreType.DMA((nw,))])(*halves)


def _place_blocks(name, blks, place):
    nw = len(blks)

    def body(p_ref, *refs):
        for b_ref, o_ref in zip(refs[:nw], refs[nw:]):
            o_ref[...] = b_ref[...]

    return pl.pallas_call(
        body, name=name,
        grid_spec=pltpu.PrefetchScalarGridSpec(
            num_scalar_prefetch=1, grid=(1,),
            in_specs=[pl.BlockSpec(b.shape, lambda i, p: (0, 0)) for b in blks],
            out_specs=[pl.BlockSpec((None, None) + b.shape, lambda i, p: (p[0], p[1], 0, 0)) for b in blks]),
        out_shape=[_sds((4, 2) + b.shape, b.dtype) for b in blks],
        compiler_params=_params(("arbitrary",)))(place, *blks)


def _pair_sum(name, fulls, gots, place):
    nw = len(fulls)

    def body(p_ref, *refs):
        s = pl.program_id(0)
        for a_ref, b_ref, o_ref, l_ref in zip(refs[:nw], refs[nw:2 * nw], refs[2 * nw:3 * nw], refs[3 * nw:]):
            o_ref[...] = (a_ref[...].astype(F32) + b_ref[...].astype(F32)).astype(o_ref.dtype)

            @pl.when(s == p_ref[0])
            def _():
                l_ref[...] = o_ref[...]

    slab = lambda a: pl.BlockSpec((None,) + a.shape[1:], lambda s, p: (s, 0, 0))
    mine = lambda a: pl.BlockSpec((None,) + a.shape[1:], lambda s, p: (p[0], 0, 0))
    out = pl.pallas_call(
        body, name=name,
        grid_spec=pltpu.PrefetchScalarGridSpec(
            num_scalar_prefetch=1, grid=(4,),
            in_specs=[pl.BlockSpec((None, None) + a.shape[2:], lambda s, p: (s, p[1], 0, 0)) for a in fulls]
            + [slab(b) for b in gots],
            out_specs=[slab(b) for b in gots] + [mine(b) for b in gots]),
        out_shape=[_sds(b.shape, BF16) for b in gots] * 2,
        compiler_params=_params(("arbitrary",)))(place, *fulls, *gots)
    return out[:nw], out[nw:]


def _sum4(name, arrs, place):
    nw = len(arrs)

    def body(p_ref, *refs):
        for a_ref, o_ref in zip(refs[:nw], refs[nw:]):
            acc = a_ref[0].astype(F32)
            for s in range(1, 4):
                acc = acc + a_ref[s].astype(F32)
            o_ref[...] = acc

    return pl.pallas_call(
        body, name=name,
        grid_spec=pltpu.PrefetchScalarGridSpec(
            num_scalar_prefetch=1, grid=(1,),
            in_specs=[pl.BlockSpec(a.shape, lambda i, p: (0, 0, 0)) for a in arrs],
            out_specs=[pl.BlockSpec((None,) + a.shape[1:], lambda i, p: (p[1], 0, 0)) for a in arrs]),
        out_shape=[_sds((2,) + a.shape[1:], F32) for a in arrs],
        compiler_params=_params(("arbitrary",)))(place, *arrs)


def _small_update(gathered, w, m, v):
    n = w.shape[1]
    tn = 2048
    c1 = 1.0 - ADAM_B1 ** ADAM_STEP
    c2 = 1.0 - ADAM_B2 ** ADAM_STEP

    def body(g_ref, w_ref, m_ref, v_ref, go_ref, d_ref, mo_ref, vo_ref):
        g = g_ref[0:1, :]
        for d in range(1, 8):
            g = g + g_ref[d:d + 1, :]
        go_ref[...] = g
        m = ADAM_B1 * m_ref[...] + (1.0 - ADAM_B1) * g
        v = ADAM_B2 * v_ref[...] + (1.0 - ADAM_B2) * (g * g)
        mo_ref[...] = m
        vo_ref[...] = v
        d_ref[...] = -ADAM_LR * ((m / c1) / (jnp.sqrt(v / c2) + ADAM_EPS) + ADAM_WD * w_ref[...])

    row = pl.BlockSpec((1, tn), lambda i: (0, i))
    return pl.pallas_call(
        body, name="small_update", grid=(n // tn,),
        in_specs=[pl.BlockSpec((8, tn), lambda i: (0, i)), row, row, row], out_specs=[row] * 4,
        out_shape=[_sds((1, n), F32)] * 4, compiler_params=_params(("parallel",)))(gathered, w, m, v)


def _swiglu(ps, es):
    g, u = ps
    return g * _sigmoid(g) * u, g, u


def _swiglu_bwd(ps, es):
    g, u = es[0].astype(F32), es[1].astype(F32)
    sg = _sigmoid(g)
    return ps[0] * u * (sg * (1.0 + g * (1.0 - sg))), ps[0] * (g * sg)


def _merge(ps, es):
    ga, gm = [e.astype(F32) for e in es]
    return _sigmoid(ga) * ps[0] + _sigmoid(gm) * ps[1], ps[0], ps[1]


def _merge_bwd(ps, es):
    a, b, ga, gm = [e.astype(F32) for e in es]
    sa, sm = _sigmoid(ga), _sigmoid(gm)
    dm = ps[0]
    return dm * sa, dm * sm, dm * a * (sa * (1.0 - sa)), dm * b * (sm * (1.0 - sm))


W_IN_PIECES = (("q", 512), ("kv", 256), ("mqk", 1024), ("mv", 512), ("mo", 512), ("if", 8),
               ("ga", 1024), ("gm", 1024))


def _local_step(x, tgt, pos_col, mod, sp, in_weights, late_weights, ffn_grads, mixer_grads):
    sh_m, sc_m, gate_m, sh_f, sc_f, gate_f = mod
    inv = ROPE_THETA ** (-2.0 * jnp.arange(HEAD_DIM // 2, dtype=F32) / HEAD_DIM)
    cos, sin = _rope_tables(pos_col, jnp.tile(inv, 4).reshape(1, 128))
    W = dict(in_weights(cos))
    h, pa, pqk, pvo, pif, pg = _proj_in(x, sp["g_pre_mix"], sc_m, sh_m, [
        (jnp.concatenate([W["q"], W["kv"]]), F32, 256), (W["mqk"], F32, 512),
        (jnp.concatenate([W["mv"], W["mo"]]), BF16, 512), (W["if"], F32, 128),
        (jnp.concatenate([W["ga"], W["gm"]]), BF16, 512)])
    ya = _attn_fwd(pa, cos, sin, sp["sinks"])
    qk = _conv_fwd(pqk, sp["conv_w"], sp["conv_b"])
    bcol = jnp.pad(sp["b_if"], ((0, 0), (0, 120)))
    brow = jnp.broadcast_to(sp["b_if"].reshape(8, 1), (8, 128))
    grow = pif[:, :8].T
    hm, cs, ns, ms = _mlstm_fwd(qk, pvo, pif, bcol, grow, brow)
    ym = _mlstm_out(hm, pvo, sp["norm_w"])
    W.update(late_weights(ym))
    w_fg, w_fu, w_fd = W["fg"], W["fu"], W["fd"]
    merged, br_a, br_m = _mm("branches", [[(ya, W["ba"])], [(ym, W["bm"])]],
                             [(pg, 0), (pg, 1)], _merge, [BF16, BF16, BF16], cn=512, nt=True)
    wide, narrow = (D_MODEL, F32), (D_MODEL, BF16)
    mix, x1, h2 = _mm_rows("mix_out", [[(merged, W["out"])]],
                           [x, gate_m, sp["g_post_mix"], sp["g_pre_ffn"], sc_f, sh_f],
                           _res_norm_rows, [wide, wide, narrow], [], cn=512)
    act, gt, up = _mm("ffn_in", [[(h2, w_fg)], [(h2, w_fu)]], [], _swiglu, [BF16] * 3,
                      cn=256, nt=True)
    dy, dff, acc_l, loss = _mm_rows("ffn_down", [[(act, w_fd)]], [x1, tgt, gate_f, sp["g_post_ffn"]],
                                    _final_loss_rows, [wide, narrow], [(8, D_MODEL), (1, 128)], cn=512)

    G = {}
    dgt, dup = _mm("ffn_down_bwd", [[(dff, w_fd)]], [gt, up], _swiglu_bwd, [BF16, BF16],
                   cn=256, nt=True)
    g_fd = _mm_tn("dw_ffn_down", act, dff, BF16, 1408, 512)
    g_fg = _mm_tn("dw_ffn_gate", dgt, h2, BF16, 1408, 1024)
    g_fu = _mm_tn("dw_ffn_up", dup, h2, BF16, 1408, 1024)
    tie = ffn_grads(g_fg, g_fu, g_fd)
    dx1, dmix, acc_r = _mm_rows(
        "ffn_in_bwd", [[(dgt, w_fg), (dup, w_fu)]],
        [x1, mix, dy, sc_f + tie, gate_m, sp["g_pre_ffn"], sp["g_post_mix"]],
        _res_norm_bwd_rows, [wide, narrow], [(8, D_MODEL)], cn=512, tm=256)
    d_a, d_m, dga, dgm = _mm("mix_out_bwd", [[(dmix, W["out"])]],
                             [br_a, br_m, (pg, 0), (pg, 1)], _merge_bwd,
                             [BF16] * 4, cn=512, nt=True)
    G["out"] = _mm_tn("dw_out", merged, dmix, BF16, 1024, 512)
    dya, = _mm("branch_attn_bwd", [[(d_a, W["ba"])]], [], _first, [F32], cn=512)
    heads = MLSTM_HEADS * MLSTM_HEAD_DIM
    dhm, do_m, acc_n = _mm_rows("branch_mlstm_bwd", [[(d_m, W["bm"])]], [hm, pvo, sp["norm_w"]],
                                _mlstm_out_bwd_rows, [(heads, F32), (heads, BF16)], [(8, heads)], cn=512)
    G["ba"] = _mm_tn("dw_branch_attn", d_a, ya, BF16, 1024, 512)
    G["bm"] = _mm_tn("dw_branch_mlstm", d_m, ym, BF16, 1024, 512)
    dqk, dv_m, dgc, dgr = _mlstm_bwd(qk, pvo, pif, bcol, grow, brow, cs, ns, ms, dhm)
    dif, acc_g = _gate_bwd(dgc, jnp.pad(dgr.T, ((0, 0), (0, 120))), pif, bcol)
    du, acc_c = _conv_bwd(pqk, sp["conv_w"], sp["conv_b"], dqk)
    dq_a, dcur, dprv, dsink = _attn_bwd(pa, cos, sin, sp["sinks"], dya)
    dkv = _attn_kv_combine(dcur, dprv, cos, sin)
    dproj = {"q": dq_a, "kv": dkv, "mqk": du, "mv": dv_m, "mo": do_m, "if": dif, "ga": dga, "gm": dgm}
    for k, _ in W_IN_PIECES:
        G[k] = _mm_tn("dw_in_" + k, dproj[k], h, BF16, dproj[k].shape[1], 1024)
    w_tied = dict(W, **{"if": W["if"] + mixer_grads(G).astype(BF16)})
    dx, acc_p = _mm_rows("proj_bwd", [[(dproj[k], w_tied[k]) for k, _ in W_IN_PIECES]],
                         [x, dx1, sp["g_pre_mix"], sc_m], _pre_norm_bwd_rows, [wide], [(8, D_MODEL)],
                         cn=512, tm=256)

    small = {
        "mod": jnp.concatenate([acc_p[1], acc_p[0], acc_r[3], acc_r[1], acc_r[0], acc_l[0]]),
        "g_pre_mix": acc_p[2], "g_post_mix": acc_r[4], "b_if": acc_g[0, :8],
        "conv_w": acc_c[:CONV_WIDTH].reshape(-1), "conv_b": acc_c[CONV_WIDTH],
        "sinks": dsink[:, 0], "norm_w": acc_n[0], "g_pre_ffn": acc_r[2], "g_post_ffn": acc_l[1]}
    return loss, dx, small


IN_WIDTH = sum(n for _, n in W_IN_PIECES)
IN_SHARD = IN_WIDTH // 4
IN_SHARD_PAD = -(-IN_SHARD // 32) * 32


def _split_w_in(w_in_t):
    out, off = {}, 0
    for k, n in W_IN_PIECES:
        out[k] = w_in_t[off:off + n]
        off += n
    out["if"] = jnp.pad(out["if"], ((0, 120), (0, 0)))
    return out


def _halves(a):
    return a.reshape(4, 2, a.shape[0] // 8, a.shape[1])


SMALL = (("b_ada", 6144), ("g_pre_mix", 1024), ("g_post_mix", 1024), ("b_if", 128), ("conv_w", 4096),
         ("conv_b", 1024), ("sinks", 128), ("norm_w", 512), ("g_pre_ffn", 1024), ("g_post_ffn", 1024))
SMALL_LEN = 8 * 2048


def _pack_small(vals):
    parts = []
    for k, n in SMALL:
        v = vals[k].reshape(-1)
        parts.append(jnp.pad(v, (0, n - v.shape[0])))
    flat = jnp.concatenate(parts)
    return jnp.pad(flat, (0, SMALL_LEN - flat.shape[0]))


def _unpack_small(flat, shapes):
    out, off = {}, 0
    for k, n in SMALL:
        size = 1
        for d in shapes[k]:
            size *= d
        out[k] = flat[off:off + size].reshape(shapes[k])
        off += n
    return out


def kernel(x, c, positions, w_ada, b_ada, g_pre_mix, g_post_mix, w_in, b_if, conv_w, conv_b, attn_sinks, mlstm_norm_w, w_branch_attn, w_branch_mlstm, w_out, g_pre_ffn, g_post_ffn, w_ffn_gate, w_ffn_up, w_ffn_down, loss_target, m_w_ada, m_b_ada, m_g_pre_mix, m_g_post_mix, m_w_in, m_b_if, m_conv_w, m_conv_b, m_attn_sinks, m_mlstm_norm_w, m_w_branch_attn, m_w_branch_mlstm, m_w_out, m_g_pre_ffn, m_g_post_ffn, m_w_ffn_gate, m_w_ffn_up, m_w_ffn_down, v_w_ada, v_b_ada, v_g_pre_mix, v_g_post_mix, v_w_in, v_b_if, v_conv_w, v_conv_b, v_attn_sinks, v_mlstm_norm_w, v_w_branch_attn, v_w_branch_mlstm, v_w_out, v_g_pre_ffn, v_g_post_ffn, v_w_ffn_gate, v_w_ffn_up, v_w_ffn_down):
    xi, yi, ci = _place()
    chip = 2 * xi + yi
    dev = 2 * chip + ci
    T = x.shape[1]
    ada_cols = w_ada.shape[2]

    place = jnp.stack([chip, ci]).astype(jnp.int32)

    def my_half(a):
        n = a.shape[0] // 2
        return lax.dynamic_slice_in_dim(a, ci * n, n, axis=0).astype(BF16)

    blk = jnp.concatenate([c.reshape(-1), conv_w.reshape(-1)]).reshape(8, 256)
    got = _all_gather8("gather_cond", blk, pltpu.VMEM).reshape(8, 2048)
    c_all = got[:, :D_MODEL].astype(BF16)
    conv_full = got[::2, D_MODEL:].reshape(4, CONV_WIDTH, -1).transpose(1, 0, 2).reshape(CONV_WIDTH, -1)

    b_sh = lax.dynamic_slice_in_dim(b_ada, chip * ada_cols, ada_cols, axis=1)
    mod_part, = _mm("ada_mod", [[(c_all, w_ada[0].astype(BF16))]], [b_sh],
                    lambda ps, es: (ps[0] + es[0],), [F32], cn=512, tm=8)
    mod_all = _all_gather8("gather_mod", mod_part, pltpu.VMEM).reshape(4, 2, 8, ada_cols)[:, 0]
    mod = lax.dynamic_index_in_dim(mod_all, dev, axis=1, keepdims=False).reshape(6, 1, D_MODEL)

    def gather_start(name, blks, after):
        return _split_start(name + "_start", blks, _place_blocks(name + "_place", blks, place),
                            _gather_sends, 4, after)

    def gather_wait(name, started, after):
        return _forward_sibling(name + "_forward", _split_wait(name + "_wait", started, after, _gather_lands, 4))

    w_in_t = jnp.pad(w_in[0].T, ((0, IN_SHARD_PAD - IN_SHARD), (0, 0)))
    in_started = gather_start("in_gather", [my_half(w_in_t)], mod)
    late_keys = ("fg", "fu", "fd", "out", "ba", "bm")
    late_started = gather_start(
        "late_gather",
        [my_half(w_ffn_gate[0].T), my_half(w_ffn_up[0].T), my_half(w_ffn_down[0]), my_half(w_out[0]),
         my_half(w_branch_attn[0].T), my_half(w_branch_mlstm[0].T)], in_started[4])
    mod = mod + (in_started[4][0, 0] + late_started[4][0, 0])

    def in_weights(after):
        g_in, = gather_wait("in_gather", in_started, after)
        return _split_w_in(g_in.reshape(4, IN_SHARD_PAD, D_MODEL)[:, :IN_SHARD].reshape(IN_WIDTH, D_MODEL))

    def late_weights(after):
        lands = gather_wait("late_gather", late_started, after)
        return {k: a.reshape(-1, a.shape[-1]) for k, a in zip(late_keys, lands)}

    sent = {}

    def scatter_start(name, groups):
        pairs, lands = _pair_sum(name + "_pair_sum", groups, _swap_halves_sibling(name + "_pair", groups), place)
        sent[name] = _split_start(name + "_start", pairs, lands, _scatter_sends, 3, pairs[0])
        return sent[name][4][0, 0]

    def ffn_grads(g_fg, g_fu, g_fd):
        return scatter_start("rs_ffn", [_halves(g_fg), _halves(g_fu), _halves(g_fd)])

    def mixer_grads(G):
        g_in_t = jnp.concatenate([G[k][:n] for k, n in W_IN_PIECES]).reshape(4, IN_SHARD, D_MODEL)
        g_in_t = jnp.pad(g_in_t, ((0, 0), (0, IN_SHARD_PAD - IN_SHARD), (0, 0)))
        return scatter_start("rs_mix", [g_in_t.reshape(4, 2, IN_SHARD_PAD // 2, D_MODEL), _halves(G["out"]),
                                        _halves(G["ba"]), _halves(G["bm"])])

    sp = {"g_pre_mix": g_pre_mix, "g_post_mix": g_post_mix, "b_if": b_if, "conv_w": conv_full,
          "conv_b": conv_b, "sinks": attn_sinks, "norm_w": mlstm_norm_w, "g_pre_ffn": g_pre_ffn,
          "g_post_ffn": g_post_ffn}
    loss, dx, small = _local_step(x[0], loss_target[0], positions.reshape(T, 1), [mod[i] for i in range(6)],
                                  sp, in_weights, late_weights, ffn_grads, mixer_grads)

    reds = (_sum4("rs_ffn_chip_sum", _split_wait("rs_ffn_wait", sent["rs_ffn"], dx, _scatter_lands, 3), place)
            + _sum4("rs_mix_chip_sum", _split_wait("rs_mix_wait", sent["rs_mix"], dx, _scatter_lands, 3), place))
    gsh = {k: s.reshape(-1, s.shape[-1])
           for k, s in zip(("fg", "fu", "fd", "w_in", "out", "ba", "bm"), _share_halves("rs_share", reds))}
    gsh["w_in"] = gsh["w_in"][:IN_SHARD]

    small["b_ada"] = small.pop("mod")
    vec = _pack_small(small).reshape(8, 2048)
    g_all = _all_gather8("gather_small", vec, pltpu.VMEM).reshape(8, SMALL_LEN)
    dmod_sh = lax.dynamic_slice_in_dim(g_all[:, :6 * D_MODEL], chip * ada_cols, ada_cols, axis=1)
    g_w_ada = _mm_tn("dw_ada", c_all, dmod_sh.astype(BF16), F32, D_MODEL, 512, 8)

    smalls = {"b_ada": (b_ada, m_b_ada, v_b_ada), "g_pre_mix": (g_pre_mix, m_g_pre_mix, v_g_pre_mix),
              "g_post_mix": (g_post_mix, m_g_post_mix, v_g_post_mix), "b_if": (b_if, m_b_if, v_b_if),
              "conv_w": None, "conv_b": (conv_b, m_conv_b, v_conv_b),
              "sinks": (attn_sinks, m_attn_sinks, v_attn_sinks),
              "norm_w": (mlstm_norm_w, m_mlstm_norm_w, v_mlstm_norm_w),
              "g_pre_ffn": (g_pre_ffn, m_g_pre_ffn, v_g_pre_ffn),
              "g_post_ffn": (g_post_ffn, m_g_post_ffn, v_g_post_ffn)}
    shapes = {k: (t[0].shape if t is not None else (1, CONV_WIDTH, D_MODEL)) for k, t in smalls.items()}
    zeros = jnp.zeros((CONV_WIDTH * D_MODEL,), F32)
    packs = [_pack_small({k: (t[i] if t is not None else zeros) for k, t in smalls.items()}).reshape(1, -1)
             for i in range(3)]
    s_out = [_unpack_small(o[0], shapes) for o in _small_update(g_all, *packs)]
    g_conv = lax.dynamic_slice_in_dim(s_out[0]["conv_w"], chip * conv_w.shape[2], conv_w.shape[2], axis=2)

    res = {}
    for k, t in smalls.items():
        if t is not None:
            res[k] = tuple(o[k] for o in s_out)
    res["conv_w"] = (g_conv, *[o[None] for o in _adamw("adam_conv_w", conv_w[0], g_conv[0], m_conv_w[0], v_conv_w[0])])
    res["w_ada"] = (g_w_ada[None], *[o[None] for o in _adamw("adam_w_ada", w_ada[0], g_w_ada, m_w_ada[0], v_w_ada[0])])
    bigs = {"w_in": (w_in, m_w_in, v_w_in), "ba": (w_branch_attn, m_w_branch_attn, v_w_branch_attn),
            "bm": (w_branch_mlstm, m_w_branch_mlstm, v_w_branch_mlstm), "out": (w_out, m_w_out, v_w_out),
            "fg": (w_ffn_gate, m_w_ffn_gate, v_w_ffn_gate), "fu": (w_ffn_up, m_w_ffn_up, v_w_ffn_up),
            "fd": (w_ffn_down, m_w_ffn_down, v_w_ffn_down)}
    for k, (w, m, v) in bigs.items():
        if k in ("w_in", "fg", "fu"):
            res[k] = tuple(o.T[None] for o in (gsh[k], *_adamw("adam_" + k, w[0].T, gsh[k], m[0].T, v[0].T)))
        else:
            g = gsh[k].T if k in ("ba", "bm") else gsh[k]
            res[k] = (g[None], *[o[None] for o in _adamw("adam_" + k, w[0], g, m[0], v[0])])

    order = ("w_ada", "b_ada", "g_pre_mix", "g_post_mix", "w_in", "b_if", "conv_w", "conv_b", "sinks",
             "norm_w", "ba", "bm", "out", "g_pre_ffn", "g_post_ffn", "fg", "fu", "fd")
    total = lax.psum(loss[0, 0], ("x", "y", "c"))
    return (total, dx[None], *[res[k][0] for k in order], *[res[k][1] for k in order],
            *[res[k][2] for k in order], *[res[k][3] for k in order])
```

```python
import functools

import jax
import jax.numpy as jnp
from jax import lax
from jax.experimental import pallas as pl
from jax.experimental.pallas import tpu as pltpu

F32, BF16 = jnp.float32, jnp.bfloat16
MESH = pl.DeviceIdType.MESH

D_MODEL = 1024
N_Q_HEADS, N_KV_HEADS, HEAD_DIM, WINDOW = 8, 2, 64, 128
ROPE_THETA = 10000.0
MLSTM_HEADS, MLSTM_HEAD_DIM, MLSTM_CHUNK, CONV_WIDTH = 4, 128, 64, 4
D_FF = 2816
NORM_EPS = 1e-6
ADAM_LR, ADAM_B1, ADAM_B2, ADAM_EPS, ADAM_WD, ADAM_STEP = 0.001, 0.9, 0.999, 1e-08, 0.01, 10

VMEM_LIMIT = 56 * 1024 * 1024
ROW_TILE = 256
MM_TM = 512
MM_TT = 1024
ATTN_BLK = WINDOW
STEP_ROWS = 2 * MLSTM_CHUNK
NEG_INF = float("-inf")


def _params(sem):
    return pltpu.CompilerParams(dimension_semantics=sem, vmem_limit_bytes=VMEM_LIMIT)


def _sds(shape, dtype):
    return jax.ShapeDtypeStruct(shape, dtype)


def _sigmoid(x):
    return 1.0 / (1.0 + jnp.exp(-x))


def _dot(a, b, ca, cb):
    return lax.dot_general(a, b, (((ca,), (cb,)), ((), ())), preferred_element_type=F32)


def _bdot(a, b, ca, cb):
    return lax.dot_general(a, b, (((ca,), (cb,)), ((0,), (0,))), preferred_element_type=F32)


def _bdot_rows(a, b):
    return jnp.stack([_dot(a[h], b[h], 0, 0) for h in range(a.shape[0])])


def _mm(name, prods, extras, epi, out_dtypes, cn, nt=False, tm=MM_TM):
    flat = [ab for p in prods for ab in p]
    counts = [len(p) for p in prods]
    M = flat[0][0].shape[0]
    N = flat[0][1].shape[0 if nt else 1]
    tm = min(tm, M)
    n_in = 2 * len(flat) + len(extras)

    def body(*refs):
        ins, outs = refs[:n_in], refs[n_in:]
        for j in range(N // cn):
            cols = slice(j * cn, (j + 1) * cn)
            k, ps = 0, []
            for cnt in counts:
                acc = None
                for _ in range(cnt):
                    b = ins[k + 1][cols, :] if nt else ins[k + 1][:, cols]
                    d = _dot(ins[k][...], b, 1, 1 if nt else 0)
                    acc = d if acc is None else acc + d
                    k += 2
                ps.append(acc)
            res = epi(ps, [r[:, cols] for r in ins[k:]])
            for o, r in zip(outs, res):
                o[:, cols] = r.astype(o.dtype)

    in_specs, args = [], []
    for a, b in flat:
        in_specs.append(pl.BlockSpec((tm, a.shape[1]), lambda i: (i, 0)))
        in_specs.append(pl.BlockSpec(b.shape, lambda i: (0, 0), pipeline_mode=pl.Buffered(1)))
        args += [a, b]
    for e in extras:
        e, off = e if isinstance(e, tuple) else (e, 0)
        rows = 1 if e.shape[0] == 1 else tm
        in_specs.append(pl.BlockSpec((rows, N), lambda i, off=off, rows=rows: (0 if rows == 1 else i, off)))
        args.append(e)
    return pl.pallas_call(
        body, name=name, grid=(M // tm,), in_specs=in_specs,
        out_specs=[pl.BlockSpec((tm, N), lambda i: (i, 0)) for _ in out_dtypes],
        out_shape=[_sds((M, N), dt) for dt in out_dtypes],
        compiler_params=_params(("parallel",)))(*args)


def _mm_rows(name, prods, extras, epi, outs, accs, cn, nt=False, tm=MM_TM):
    flat = [ab for p in prods for ab in p]
    counts = [len(p) for p in prods]
    M = flat[0][0].shape[0]
    N = flat[0][1].shape[0 if nt else 1]
    tm = min(tm, M)
    n_mm, n_in, n_out = 2 * len(flat), 2 * len(flat) + len(extras), len(outs)

    def body(*refs):
        ins, out_refs, acc_refs = refs[:n_in], refs[n_in:n_in + n_out], refs[n_in + n_out:]

        @pl.when(pl.program_id(0) == 0)
        def _():
            for a in acc_refs:
                a[...] = jnp.zeros_like(a)

        chunks = [[] for _ in counts]
        for j in range(N // cn):
            cols = slice(j * cn, (j + 1) * cn)
            k = 0
            for p, cnt in enumerate(counts):
                acc = None
                for _ in range(cnt):
                    b = ins[k + 1][cols, :] if nt else ins[k + 1][:, cols]
                    d = _dot(ins[k][...], b, 1, 1 if nt else 0)
                    acc = d if acc is None else acc + d
                    k += 2
                chunks[p].append(acc)
        ps = [c[0] if len(c) == 1 else jnp.concatenate(c, axis=1) for c in chunks]
        res, incs = epi(ps, [r[...] for r in ins[n_mm:]])
        for o, r in zip(out_refs, res):
            o[...] = r.astype(o.dtype)
        for a, inc in zip(acc_refs, incs):
            a[...] += inc

    in_specs, args = [], []
    for a, b in flat:
        in_specs.append(pl.BlockSpec((tm, a.shape[1]), lambda i: (i, 0)))
        in_specs.append(pl.BlockSpec(b.shape, lambda i: (0, 0), pipeline_mode=pl.Buffered(1)))
        args += [a, b]
    for e in extras:
        rows = 1 if e.shape[0] == 1 else tm
        in_specs.append(pl.BlockSpec((rows, e.shape[1]), lambda i, rows=rows: (0 if rows == 1 else i, 0)))
        args.append(e)
    return pl.pallas_call(
        body, name=name, grid=(M // tm,), in_specs=in_specs,
        out_specs=[pl.BlockSpec((tm, w), lambda i: (i, 0)) for w, _ in outs]
        + [pl.BlockSpec(s, lambda i: (0, 0)) for s in accs],
        out_shape=[_sds((M, w), dt) for w, dt in outs] + [_sds(s, F32) for s in accs],
        compiler_params=_params(("arbitrary",)))(*args)


def _mm_tn(name, a, b, out_dtype, tk, tn, tt=MM_TT):
    T, Ka = a.shape
    N = b.shape[1]
    tt = min(tt, T)
    steps = T // tt

    def body(a_ref, b_ref, o_ref, acc):
        t = pl.program_id(2)

        @pl.when(t == 0)
        def _():
            acc[...] = jnp.zeros_like(acc)

        acc[...] += _dot(a_ref[...], b_ref[...], 0, 0)

        @pl.when(t == steps - 1)
        def _():
            o_ref[...] = acc[...].astype(o_ref.dtype)

    return pl.pallas_call(
        body, name=name, grid=(Ka // tk, N // tn, steps),
        in_specs=[pl.BlockSpec((tt, tk), lambda i, j, t: (t, i)),
                  pl.BlockSpec((tt, tn), lambda i, j, t: (t, j))],
        out_specs=pl.BlockSpec((tk, tn), lambda i, j, t: (i, j)),
        out_shape=_sds((Ka, N), out_dtype),
        scratch_shapes=[pltpu.VMEM((tk, tn), F32)],
        compiler_params=_params(("parallel", "parallel", "arbitrary")))(a, b)


def _first(ps, es):
    return (ps[0],)


def _rows(name, body, ins, out_shapes, T, tr=ROW_TILE):
    tr = min(tr, T)

    def spec(shape):
        if shape[0] == T:
            return pl.BlockSpec((tr,) + tuple(shape[1:]), lambda i: (i,) + (0,) * (len(shape) - 1))
        return pl.BlockSpec(tuple(shape), lambda i: (0,) * len(shape))

    return pl.pallas_call(
        body, name=name, grid=(T // tr,),
        in_specs=[spec(a.shape) for a in ins], out_specs=[spec(s.shape) for s in out_shapes],
        out_shape=out_shapes, compiler_params=_params(("arbitrary",)))(*ins)


def _rms(x):
    r = lax.rsqrt(jnp.mean(x * x, axis=-1, keepdims=True) + NORM_EPS)
    return x * r, r


def _rms_bwd(dxn, xn, r):
    return r * (dxn - xn * jnp.mean(dxn * xn, axis=-1, keepdims=True))


def _colsum(v):
    return jnp.sum(v, axis=0, keepdims=True)


def _proj_in(x, g, sc, sh, groups):
    T = x.shape[0]
    tm = min(MM_TM, T)
    ng = len(groups)

    def body(x_ref, g_ref, sc_ref, sh_ref, *rest):
        w_refs, h_ref, out_refs = rest[:ng], rest[ng], rest[ng + 1:]
        xn, _ = _rms(x_ref[...])
        h = (xn * g_ref[...] * (1.0 + sc_ref[...]) + sh_ref[...]).astype(BF16)
        h_ref[...] = h
        for w_ref, o_ref, (w, _, cn) in zip(w_refs, out_refs, groups):
            for j in range(w.shape[0] // cn):
                cols = slice(j * cn, (j + 1) * cn)
                o_ref[:, cols] = _dot(h, w_ref[cols, :], 1, 1).astype(o_ref.dtype)

    row = pl.BlockSpec((1, D_MODEL), lambda i: (0, 0))
    tile = lambda w: pl.BlockSpec((tm, w), lambda i: (i, 0))
    return pl.pallas_call(
        body, name="proj_in", grid=(T // tm,),
        in_specs=[tile(D_MODEL), row, row, row] + [
            pl.BlockSpec(w.shape, lambda i: (0, 0), pipeline_mode=pl.Buffered(1)) for w, _, _ in groups],
        out_specs=[tile(D_MODEL)] + [tile(w.shape[0]) for w, _, _ in groups],
        out_shape=[_sds((T, D_MODEL), BF16)] + [_sds((T, w.shape[0]), dt) for w, dt, _ in groups],
        compiler_params=_params(("parallel",)))(x, g, sc, sh, *[w for w, _, _ in groups])


def _acc_rows(rows):
    w = rows[0].shape[1]
    return jnp.concatenate(rows + [jnp.zeros((8 - len(rows), w), F32)], axis=0)


def _res_norm_rows(ps, es):
    mix = ps[0]
    x, gate, gp, g2, sc, sh = es
    mh, _ = _rms(mix)
    x1 = x + gate * (mh * gp)
    xn, _ = _rms(x1)
    return [mix, x1, xn * g2 * (1.0 + sc) + sh], []


def _final_loss_rows(ps, es):
    x1, tgt, gate, gp = es
    fh, r = _rms(ps[0])
    e = x1 + gate * (fh * gp) - tgt
    loss = 0.5 * jnp.sum(jnp.mean(e * e, axis=-1, keepdims=True))
    dy = e * (1.0 / D_MODEL)
    acc = _acc_rows([_colsum(dy * fh * gp), _colsum(dy * gate * fh)])
    return [dy, _rms_bwd(dy * gate * gp, fh, r)], [acc, jnp.full((1, 128), loss, F32)]


def _res_norm_bwd_rows(ps, es):
    dh = ps[0]
    x1, mix, dy, sc, gate, g2, gp = es
    xn, r1 = _rms(x1)
    rows = [_colsum(dh * xn * g2), _colsum(dh), _colsum(dh * (1.0 + sc) * xn)]
    dx1 = dy + _rms_bwd(dh * (1.0 + sc) * g2, xn, r1)
    mh, rm = _rms(mix)
    rows += [_colsum(dx1 * mh * gp), _colsum(dx1 * gate * mh)]
    return [dx1, _rms_bwd(dx1 * gate * gp, mh, rm)], [_acc_rows(rows)]


def _pre_norm_bwd_rows(ps, es):
    dh = ps[0]
    x, dx1, g, sc = es
    xn, r = _rms(x)
    rows = [_colsum(dh * xn * g), _colsum(dh), _colsum(dh * (1.0 + sc) * xn)]
    return [dx1 + _rms_bwd(dh * (1.0 + sc) * g, xn, r)], [_acc_rows(rows)]


def _rope_tables(pos_col, inv_freq):
    T = pos_col.shape[0]

    def body(p_ref, f_ref, c_ref, s_ref):
        ang = p_ref[...].astype(F32) * f_ref[...]
        lane = lax.broadcasted_iota(jnp.int32, ang.shape, 1)
        c_ref[...] = jnp.cos(ang)
        s_ref[...] = jnp.where(lane % HEAD_DIM < HEAD_DIM // 2, -1.0, 1.0) * jnp.sin(ang)

    return _rows("rope_tables", body, [pos_col, inv_freq],
                 [_sds((T, 128), F32), _sds((T, 128), F32)], T, tr=512)


def _swap_halves(t):
    W = t.shape[1]
    lane = lax.broadcasted_iota(jnp.int32, t.shape, 1)
    half = HEAD_DIM // 2
    return jnp.where(lane % HEAD_DIM < half, pltpu.roll(t, W - half, 1), pltpu.roll(t, half, 1))


def _widen(c, W):
    return c if W == 128 else jnp.concatenate([c] * (W // 128), axis=1)


def _rope(t, c, s):
    W = t.shape[1]
    return t * _widen(c, W) + _swap_halves(t) * _widen(s, W)


def _unrope(dy, c, s):
    W = dy.shape[1]
    return dy * _widen(c, W) + _swap_halves(dy * _widen(s, W))


def _attn_mask(n):
    qi = lax.broadcasted_iota(jnp.int32, (ATTN_BLK, 2 * ATTN_BLK), 0)
    kj = lax.broadcasted_iota(jnp.int32, (ATTN_BLK, 2 * ATTN_BLK), 1)
    rel = kj - ATTN_BLK
    return (rel <= qi) & (qi - rel < WINDOW) & ((n > 0) | (kj >= ATTN_BLK))


def _attn_load(cur, prv, cc, sc, cp, sp):
    x, xp = cur[...], prv[...]
    q = _rope(x[:, :512], cc[...], sc[...]) * (HEAD_DIM ** -0.5)
    k = jnp.concatenate([_rope(xp[:, 512:640], cp[...], sp[...]),
                         _rope(x[:, 512:640], cc[...], sc[...])], axis=0)
    v = jnp.concatenate([xp[:, 640:768], x[:, 640:768]], axis=0)
    return q, k, v


ROLLED = tuple(h for h in range(N_Q_HEADS) if h % 2 != h // (N_Q_HEADS // N_KV_HEADS))


def _pair_heads(t):
    half = lax.broadcasted_iota(jnp.int32, (ATTN_BLK, 128), 1) // HEAD_DIM
    return jnp.stack([jnp.where(half == h % 2, t[:, 128 * (h // 2):128 * (h // 2) + 128], 0.0)
                      for h in range(N_Q_HEADS)])


def _kv_heads(t):
    half = lax.broadcasted_iota(jnp.int32, t.shape, 1) // HEAD_DIM
    tr = pltpu.roll(t, HEAD_DIM, 1)
    return jnp.stack([jnp.where(half == h % 2, tr if h in ROLLED else t, 0.0)
                      for h in range(N_Q_HEADS)])


def _sink_column(snk):
    return jnp.stack([jnp.full((1, 1), snk[0, h], F32) for h in range(N_Q_HEADS)])


def _attn_probs(qh, kh, mask, sink):
    s = jnp.where(mask, _bdot(qh, kh, 2, 2), NEG_INF)
    m = jnp.maximum(jnp.max(s, axis=-1, keepdims=True), sink)
    p = jnp.exp(s - m)
    es = jnp.exp(sink - m)
    rl = 1.0 / (jnp.sum(p, axis=-1, keepdims=True) + es)
    return p, es, rl


def _attn_specs(order):
    blk = lambda w: pl.BlockSpec((ATTN_BLK, w), lambda s: (order(s), 0))
    prv = lambda w: pl.BlockSpec((ATTN_BLK, w), lambda s: (jnp.maximum(order(s) - 1, 0), 0))
    return [blk(768), prv(768), blk(128), blk(128), prv(128), prv(128),
            pl.BlockSpec(memory_space=pltpu.SMEM)]


def _attn_fwd(pa, cos, sin, sinks):
    T = pa.shape[0]
    nb = T // ATTN_BLK

    def body(cur, prv, cc, sc, cp, sp, snk, y_ref):
        n = pl.program_id(0)
        q, k, v = _attn_load(cur, prv, cc, sc, cp, sp)
        qh, kh, vh = _pair_heads(q).astype(BF16), _kv_heads(k).astype(BF16), _kv_heads(v).astype(BF16)
        p, _, rl = _attn_probs(qh, kh, _attn_mask(n), _sink_column(snk))
        o = _bdot(p.astype(BF16), vh, 2, 1) * rl
        for pair in range(N_Q_HEADS // 2):
            y_ref[:, 128 * pair:128 * pair + 128] = (o[2 * pair] + o[2 * pair + 1]).astype(BF16)

    return pl.pallas_call(
        body, name="attn_fwd", grid=(nb,), in_specs=_attn_specs(lambda s: s),
        out_specs=pl.BlockSpec((ATTN_BLK, 512), lambda n: (n, 0)),
        out_shape=_sds((T, 512), BF16), compiler_params=_params(("parallel",)))(
            pa, pa, cos, sin, cos, sin, sinks)


def _attn_bwd(pa, cos, sin, sinks, dy):
    T = pa.shape[0]
    nb = T // ATTN_BLK
    rev = lambda s: nb - 1 - s

    def body(cur, prv, cc, sc, cp, sp, snk, dy_ref, dq_ref, dkv_ref, dsink_ref, carry):
        n = rev(pl.program_id(0))

        @pl.when(pl.program_id(0) == 0)
        def _():
            dsink_ref[...] = jnp.zeros_like(dsink_ref)
            carry[...] = jnp.zeros_like(carry)

        q, k, v = _attn_load(cur, prv, cc, sc, cp, sp)
        qh, kh, vh = _pair_heads(q).astype(BF16), _kv_heads(k).astype(BF16), _kv_heads(v).astype(BF16)
        p, es, rl = _attn_probs(qh, kh, _attn_mask(n), _sink_column(snk))
        pn = p * rl
        do = _pair_heads(dy_ref[...]).astype(BF16)
        dp = _bdot(do, vh, 2, 2)
        delta = jnp.sum(pn * dp, axis=-1, keepdims=True)
        ds = (pn * (dp - delta)).astype(BF16)
        dsink = es * rl * delta
        dq = _bdot(ds, kh, 2, 1) * (HEAD_DIM ** -0.5)
        dkh = _bdot_rows(ds, qh)
        dvh = _bdot_rows(pn.astype(BF16), do)

        def fold(t):
            same = [t[h] for h in range(N_Q_HEADS) if h not in ROLLED]
            moved = [t[h] for h in ROLLED]
            return sum(same[1:], same[0]) + pltpu.roll(sum(moved[1:], moved[0]), HEAD_DIM, 1)

        dk, dv = fold(dkh), fold(dvh)
        for h in range(N_Q_HEADS):
            dsink_ref[h:h + 1, :] += -jnp.sum(dsink[h])
        for pair in range(N_Q_HEADS // 2):
            dq_ref[:, 128 * pair:128 * pair + 128] = _unrope(
                dq[2 * pair] + dq[2 * pair + 1], cc[...], sc[...]).astype(BF16)
        dkv_ref[:, 0:128] = _unrope(dk[ATTN_BLK:] + carry[:, 0:128], cc[...], sc[...]).astype(BF16)
        dkv_ref[:, 128:256] = (dv[ATTN_BLK:] + carry[:, 128:256]).astype(BF16)
        carry[:, 0:128] = dk[:ATTN_BLK]
        carry[:, 128:256] = dv[:ATTN_BLK]

    blk = lambda w: pl.BlockSpec((ATTN_BLK, w), lambda s: (rev(s), 0))
    return pl.pallas_call(
        body, name="attn_bwd", grid=(nb,), in_specs=_attn_specs(rev) + [blk(512)],
        out_specs=[blk(512), blk(256), pl.BlockSpec((8, 128), lambda s: (0, 0))],
        out_shape=[_sds((T, 512), BF16), _sds((T, 256), BF16), _sds((8, 128), F32)],
        scratch_shapes=[pltpu.VMEM((ATTN_BLK, 256), F32)],
        compiler_params=_params(("arbitrary",)))(pa, pa, cos, sin, cos, sin, sinks, dy)


CONV_COLS = 2 * MLSTM_HEADS * MLSTM_HEAD_DIM


def _conv_pre(cur_ref, halo_ref, w_ref, b_ref, i, tr):
    xx = jnp.concatenate([jnp.where(i > 0, halo_ref[...], 0.0), cur_ref[...]], axis=0)
    taps = [(pltpu.roll(xx, CONV_WIDTH - 1 - j, 0) if j < CONV_WIDTH - 1 else xx)[8:8 + tr]
            for j in range(CONV_WIDTH)]
    pre = b_ref[...]
    for j in range(CONV_WIDTH):
        pre = pre + taps[j] * w_ref[j:j + 1, :]
    return pre, taps


def _conv_specs(T, tr):
    return [pl.BlockSpec((tr, CONV_COLS), lambda i: (i, 0)),
            pl.BlockSpec((8, CONV_COLS), lambda i: (jnp.maximum(i * (tr // 8) - 1, 0), 0)),
            pl.BlockSpec((CONV_WIDTH, CONV_COLS), lambda i: (0, 0)),
            pl.BlockSpec((1, CONV_COLS), lambda i: (0, 0))]


def _conv_fwd(pm, w, b):
    T = pm.shape[0]
    tr = min(ROW_TILE, T)

    def body(cur_ref, halo_ref, w_ref, b_ref, o_ref):
        pre, _ = _conv_pre(cur_ref, halo_ref, w_ref, b_ref, pl.program_id(0), tr)
        o_ref[...] = pre * _sigmoid(pre)

    return pl.pallas_call(
        body, name="conv_fwd", grid=(T // tr,), in_specs=_conv_specs(T, tr),
        out_specs=pl.BlockSpec((tr, CONV_COLS), lambda i: (i, 0)),
        out_shape=_sds((T, CONV_COLS), F32), compiler_params=_params(("parallel",)))(pm, pm, w, b)


def _conv_bwd(pqk, w, b, dqk):
    T = pqk.shape[0]
    tr = min(ROW_TILE, T)
    nt = T // tr

    def body(cur_ref, prev_ref, next_ref, w_ref, b_ref, d_ref, dnext_ref, du_ref, acc_ref):
        i = pl.program_id(0)

        @pl.when(i == 0)
        def _():
            acc_ref[...] = jnp.zeros_like(acc_ref)

        last = i == nt - 1
        xx = jnp.concatenate([jnp.where(i > 0, prev_ref[...], 0.0), cur_ref[...],
                              jnp.where(last, 0.0, next_ref[...])], axis=0)
        taps = [(pltpu.roll(xx, CONV_WIDTH - 1 - j, 0) if j < CONV_WIDTH - 1 else xx)[8:16 + tr]
                for j in range(CONV_WIDTH)]
        pre = b_ref[...]
        for j in range(CONV_WIDTH):
            pre = pre + taps[j] * w_ref[j:j + 1, :]
        sg = _sigmoid(pre)
        dd = jnp.concatenate([d_ref[...], jnp.where(last, 0.0, dnext_ref[...])], axis=0)
        dpre = dd * (sg * (1.0 + pre * (1.0 - sg)))
        for j in range(CONV_WIDTH):
            acc_ref[j:j + 1, :] += _colsum(dpre[:tr] * taps[j][:tr])
        acc_ref[CONV_WIDTH:CONV_WIDTH + 1, :] += _colsum(dpre[:tr])
        du = dpre[:tr] * w_ref[CONV_WIDTH - 1:CONV_WIDTH, :]
        for j in range(CONV_WIDTH - 1):
            k = CONV_WIDTH - 1 - j
            du = du + pltpu.roll(dpre, tr + 8 - k, 0)[:tr] * w_ref[j:j + 1, :]
        du_ref[...] = du.astype(BF16)

    tile = pl.BlockSpec((tr, CONV_COLS), lambda i: (i, 0))
    after = pl.BlockSpec((8, CONV_COLS), lambda i: (jnp.minimum((i + 1) * (tr // 8), T // 8 - 1), 0))
    before = pl.BlockSpec((8, CONV_COLS), lambda i: (jnp.maximum(i * (tr // 8) - 1, 0), 0))
    return pl.pallas_call(
        body, name="conv_bwd", grid=(nt,),
        in_specs=[tile, before, after, pl.BlockSpec((CONV_WIDTH, CONV_COLS), lambda i: (0, 0)),
                  pl.BlockSpec((1, CONV_COLS), lambda i: (0, 0)), tile, after],
        out_specs=[tile, pl.BlockSpec((8, CONV_COLS), lambda i: (0, 0))],
        out_shape=[_sds((T, CONV_COLS), BF16), _sds((8, CONV_COLS), F32)],
        compiler_params=_params(("arbitrary",)))(pqk, pqk, pqk, w, b, dqk, dqk)


def _log_sigmoid(x):
    return jnp.minimum(x, 0.0) - jnp.log1p(jnp.exp(-jnp.abs(x)))


def _chunk_cumsum(x, axis):
    idx = lax.broadcasted_iota(jnp.int32, x.shape, axis) % MLSTM_CHUNK
    k = 1
    while k < MLSTM_CHUNK:
        x = x + jnp.where(idx >= k, pltpu.roll(x, k, axis), 0.0)
        k *= 2
    return x


def _chunk_rev_cumsum(x, axis):
    n = x.shape[axis]
    idx = lax.broadcasted_iota(jnp.int32, x.shape, axis) % MLSTM_CHUNK
    k = 1
    while k < MLSTM_CHUNK:
        x = x + jnp.where(idx < MLSTM_CHUNK - k, pltpu.roll(x, n - k, axis), 0.0)
        k *= 2
    return x


def _mlstm_gates(gc_ref, bc_ref, gr_ref, br_ref):
    gc = gc_ref[...] + bc_ref[...]
    gr = gr_ref[...] + br_ref[...]
    return gc, _chunk_cumsum(_log_sigmoid(gc), 0), gr, _chunk_cumsum(_log_sigmoid(gr), 1)


def _heads(ref, base=0):
    D = MLSTM_HEAD_DIM
    return jnp.stack([ref[:, base + D * h:base + D * h + D] for h in range(MLSTM_HEADS)])


def _mlstm_inputs(q_ref, k_ref, v_ref, gc, bc, gr, br):
    H = MLSTM_HEADS
    q, v = _heads(q_ref), _heads(v_ref)
    ks = _heads(k_ref) * (MLSTM_HEAD_DIM ** -0.5)
    return dict(
        q=q, ks=ks, qb=q.astype(BF16), kb=ks.astype(BF16), vb=v.astype(BF16),
        b_col=jnp.stack([bc[:, H + h:H + h + 1] for h in range(H)]),
        i_col=jnp.stack([gc[:, h:h + 1] for h in range(H)]),
        b_row=jnp.stack([br[H + h:H + h + 1, :] for h in range(H)]),
        i_row=jnp.stack([gr[h:h + 1, :] for h in range(H)]))


def _mlstm_head(f, c_prev, n_prev, m_prev):
    L = MLSTM_CHUNK
    q, qb = f["q"], f["qb"]
    t = lax.broadcasted_iota(jnp.int32, (1, 2 * L, 2 * L), 1)
    s = lax.broadcasted_iota(jnp.int32, (1, 2 * L, 2 * L), 2)
    mask = (t // L == s // L) & (s <= t)
    d = jnp.where(mask, f["b_col"] - f["b_row"] + f["i_row"], NEG_INF)
    row = lax.broadcasted_iota(jnp.int32, (1, 2 * L, 1), 1)
    inter = f["b_col"] + jnp.where(row < L, m_prev[0], m_prev[1])
    m_t = jnp.maximum(inter, jnp.max(d, axis=-1, keepdims=True))
    w_intra = jnp.exp(d - m_t)
    w_inter = jnp.exp(inter - m_t)
    sc = _bdot(qb, f["kb"], 2, 2) * w_intra
    qc = jnp.concatenate([_bdot(qb[:, :L], c_prev[0].astype(BF16), 2, 1),
                          _bdot(qb[:, L:], c_prev[1].astype(BF16), 2, 1)], axis=1)
    qn = jnp.concatenate([jnp.sum(q[:, :L] * n_prev[0], axis=-1, keepdims=True),
                          jnp.sum(q[:, L:] * n_prev[1], axis=-1, keepdims=True)], axis=1)
    num = _bdot(sc.astype(BF16), f["vb"], 2, 1) + w_inter * qc
    den = jnp.sum(sc, axis=-1, keepdims=True) + w_inter * qn
    return dict(f, w_intra=w_intra, w_inter=w_inter, sc=sc, qc=qc, qn=qn, num=num, den=den,
                floor=jnp.exp(-m_t))


def _mlstm_update(f, ch, c, n, m):
    L = MLSTM_CHUNK
    rows = slice(L * ch, L * ch + L)
    b_col = f["b_col"][:, rows]
    g_last = b_col[:, L - 1:L]
    a_col = g_last - b_col + f["i_col"][:, rows]
    m_new = jnp.maximum(g_last + m, jnp.max(a_col, axis=1, keepdims=True))
    decay = jnp.exp(g_last + m - m_new)
    e_a = jnp.exp(a_col - m_new)
    kw = f["ks"][:, rows] * e_a
    c_new = decay * c + _bdot_rows(kw.astype(BF16), f["vb"][:, rows])
    n_new = decay * n + jnp.sum(kw, axis=1, keepdims=True)
    return c_new, n_new, m_new, decay, e_a, kw


def _mlstm_specs(T, order):
    blk = lambda w, col: pl.BlockSpec((STEP_ROWS, w), lambda s: (order(s), col))
    return [blk(512, 0), blk(512, 1), blk(512, 0), blk(128, 0),
            pl.BlockSpec((1, 128), lambda s: (0, 0)),
            pl.BlockSpec((8, STEP_ROWS), lambda s: (0, order(s))),
            pl.BlockSpec((8, 128), lambda s: (0, 0))]


def _lanes(m):
    return jnp.broadcast_to(m, m.shape[:-1] + (128,))


def _mlstm_fwd(qk, pm, gcol, bcol, grow, brow):
    T = qk.shape[0]
    steps = T // STEP_ROWS
    H, D = MLSTM_HEADS, MLSTM_HEAD_DIM

    def body(q_ref, k_ref, v_ref, gc_ref, bc_ref, gr_ref, br_ref, h_ref, cs_ref, ns_ref, ms_ref,
             c_st, n_st, m_st):
        @pl.when(pl.program_id(0) == 0)
        def _():
            c_st[...] = jnp.zeros_like(c_st)
            n_st[...] = jnp.zeros_like(n_st)
            m_st[...] = jnp.zeros_like(m_st)

        f = _mlstm_inputs(q_ref, k_ref, v_ref, *_mlstm_gates(gc_ref, bc_ref, gr_ref, br_ref))
        c0, n0, m0 = c_st[...], n_st[...], m_st[:, :, 0:1]
        c1, n1, m1, _, _, _ = _mlstm_update(f, 0, c0, n0, m0)
        c2, n2, m2, _, _, _ = _mlstm_update(f, 1, c1, n1, m1)
        f = _mlstm_head(f, (c0, c1), (n0, n1), (m0, m1))
        h = f["num"] / jnp.maximum(jnp.abs(f["den"]), f["floor"])
        for hd in range(H):
            h_ref[:, D * hd:D * hd + D] = h[hd]
        cs_ref[0], cs_ref[1] = c0, c1
        ns_ref[0], ns_ref[1] = n0, n1
        ms_ref[0], ms_ref[1] = _lanes(m0), _lanes(m1)
        c_st[...], n_st[...], m_st[...] = c2, n2, _lanes(m2)

    vec = pl.BlockSpec((2, H, 1, 128), lambda s: (s, 0, 0, 0))
    return pl.pallas_call(
        body, name="mlstm_fwd", grid=(steps,), in_specs=_mlstm_specs(T, lambda s: s),
        out_specs=[pl.BlockSpec((STEP_ROWS, 512), lambda s: (s, 0)),
                   pl.BlockSpec((2, H, 128, 128), lambda s: (s, 0, 0, 0)), vec, vec],
        out_shape=[_sds((T, 512), F32), _sds((2 * steps, H, 128, 128), F32),
                   _sds((2 * steps, H, 1, 128), F32), _sds((2 * steps, H, 1, 128), F32)],
        scratch_shapes=[pltpu.VMEM((H, 128, 128), F32), pltpu.VMEM((H, 1, 128), F32),
                        pltpu.VMEM((H, 1, 128), F32)],
        compiler_params=_params(("arbitrary",)))(qk, qk, pm, gcol, bcol, grow, brow)


def _mlstm_bwd(qk, pm, gcol, bcol, grow, brow, cs, ns, ms, dh):
    T = qk.shape[0]
    steps = T // STEP_ROWS
    H, L, D = MLSTM_HEADS, MLSTM_CHUNK, MLSTM_HEAD_DIM
    rev = lambda s: steps - 1 - s

    def body(q_ref, k_ref, v_ref, gc_ref, bc_ref, gr_ref, br_ref, cs_ref, ns_ref, ms_ref, dh_ref,
             dqk_ref, dv_ref, dgc_ref, dgr_ref, dc_st, dn_st):
        @pl.when(pl.program_id(0) == 0)
        def _():
            dc_st[...] = jnp.zeros_like(dc_st)
            dn_st[...] = jnp.zeros_like(dn_st)

        f = _mlstm_inputs(q_ref, k_ref, v_ref, *_mlstm_gates(gc_ref, bc_ref, gr_ref, br_ref))
        c_prev = (cs_ref[0], cs_ref[1])
        n_prev = (ns_ref[0], ns_ref[1])
        m_prev = (ms_ref[0, :, :, 0:1], ms_ref[1, :, :, 0:1])
        f = _mlstm_head(f, c_prev, n_prev, m_prev)
        big = jnp.abs(f["den"]) > f["floor"]
        rden = 1.0 / jnp.where(big, jnp.abs(f["den"]), f["floor"])
        dnum = _heads(dh_ref) * rden
        hdh = jnp.sum(f["num"] * dnum, axis=-1, keepdims=True)
        dden = jnp.where(big, -hdh * rden * jnp.sign(f["den"]), 0.0)
        dnum_b = dnum.astype(BF16)
        dsc = _bdot(dnum_b, f["vb"], 2, 2) + dden
        g = dsc * f["sc"]
        dv = _bdot_rows(f["sc"].astype(BF16), dnum_b)
        dqk_ = (dsc * f["w_intra"]).astype(BF16)
        dq = _bdot(dqk_, f["kb"], 2, 1)
        dks = _bdot_rows(dqk_, f["qb"])
        wdn = f["w_inter"] * dnum
        wdn_b = wdn.astype(BF16)
        wdd = f["w_inter"] * dden
        u = jnp.sum(f["qc"] * wdn, axis=-1, keepdims=True) + wdd * f["qn"]
        dks_s, dv_s, z_s, dg_s = [None, None], [None, None], [None, None], [None, None]
        dcn, dnn = dc_st[...], dn_st[...]
        for ch in (1, 0):
            rows = slice(L * ch, L * ch + L)
            _, _, _, decay, e_a, kw = _mlstm_update(f, ch, c_prev[ch], n_prev[ch], m_prev[ch])
            dcn_b = dcn.astype(BF16)
            dkw = _bdot(f["vb"][:, rows], dcn_b, 2, 2) + dnn
            dks_s[ch] = e_a * dkw
            dv_s[ch] = _bdot(kw.astype(BF16), dcn_b, 2, 1)
            z_s[ch] = e_a * jnp.sum(f["ks"][:, rows] * dkw, axis=-1, keepdims=True)
            dg_s[ch] = jnp.sum(z_s[ch], axis=1, keepdims=True) + decay * (
                jnp.sum(c_prev[ch] * dcn, axis=(1, 2), keepdims=True)
                + jnp.sum(n_prev[ch] * dnn, axis=(1, 2), keepdims=True))
            dcn = decay * dcn + _bdot_rows(f["qb"][:, rows], wdn_b[:, rows])
            dnn = decay * dnn + jnp.sum(wdd[:, rows] * f["q"][:, rows], axis=1, keepdims=True)
        dc_st[...], dn_st[...] = dcn, dnn
        dq = dq + jnp.concatenate(
            [_bdot(wdn_b[:, :L], c_prev[0].astype(BF16), 2, 2) + wdd[:, :L] * n_prev[0],
             _bdot(wdn_b[:, L:], c_prev[1].astype(BF16), 2, 2) + wdd[:, L:] * n_prev[1]], axis=1)
        dks = (dks + jnp.concatenate(dks_s, axis=1)) * (D ** -0.5)
        dv = dv + jnp.concatenate(dv_s, axis=1)
        z = jnp.concatenate(z_s, axis=1)
        row = lax.broadcasted_iota(jnp.int32, (1, STEP_ROWS, 1), 1)
        dg_col = jnp.where(row == L - 1, dg_s[0], 0.0) + jnp.where(row == 2 * L - 1, dg_s[1], 0.0)
        db_col = jnp.sum(g, axis=-1, keepdims=True) + u - z + dg_col
        g_row = jnp.sum(g, axis=1, keepdims=True)
        lane = lax.broadcasted_iota(jnp.int32, (STEP_ROWS, 128), 1)
        sub = lax.broadcasted_iota(jnp.int32, (8, STEP_ROWS), 0)
        dgc = jnp.zeros((STEP_ROWS, 128), F32)
        dgr = jnp.zeros((8, STEP_ROWS), F32)
        for hd in range(H):
            dgc = dgc + jnp.where(lane == hd, z[hd], 0.0) + jnp.where(lane == H + hd, db_col[hd], 0.0)
            dgr = dgr + jnp.where(sub == hd, g_row[hd], 0.0) - jnp.where(sub == H + hd, g_row[hd], 0.0)
            dqk_ref[:, D * hd:D * hd + D] = dq[hd]
            dqk_ref[:, H * D + D * hd:H * D + D * hd + D] = dks[hd]
            dv_ref[:, D * hd:D * hd + D] = dv[hd].astype(BF16)
        dgc_ref[...] = dgc
        dgr_ref[...] = dgr

    return pl.pallas_call(
        body, name="mlstm_bwd", grid=(steps,),
        in_specs=_mlstm_specs(T, rev) + [
            pl.BlockSpec((2, H, 128, 128), lambda s: (rev(s), 0, 0, 0)),
            pl.BlockSpec((2, H, 1, 128), lambda s: (rev(s), 0, 0, 0)),
            pl.BlockSpec((2, H, 1, 128), lambda s: (rev(s), 0, 0, 0)),
            pl.BlockSpec((STEP_ROWS, 512), lambda s: (rev(s), 0))],
        out_specs=[pl.BlockSpec((STEP_ROWS, 1024), lambda s: (rev(s), 0)),
                   pl.BlockSpec((STEP_ROWS, 512), lambda s: (rev(s), 0)),
                   pl.BlockSpec((STEP_ROWS, 128), lambda s: (rev(s), 0)),
                   pl.BlockSpec((8, STEP_ROWS), lambda s: (0, rev(s)))],
        out_shape=[_sds((T, 1024), F32), _sds((T, 512), BF16), _sds((T, 128), F32), _sds((8, T), F32)],
        scratch_shapes=[pltpu.VMEM((H, 128, 128), F32), pltpu.VMEM((H, 1, 128), F32)],
        compiler_params=_params(("arbitrary",)))(qk, qk, pm, gcol, bcol, grow, brow, cs, ns, ms, dh)


def _gate_bwd(dgc, dgr_t, gcol, bcol):
    T = dgc.shape[0]

    def body(a_ref, b_ref, g_ref, bias_ref, o_ref, acc_ref):
        @pl.when(pl.program_id(0) == 0)
        def _():
            acc_ref[...] = jnp.zeros_like(acc_ref)

        d = a_ref[...] + b_ref[...]
        lane = lax.broadcasted_iota(jnp.int32, d.shape, 1)
        is_f = (lane >= MLSTM_HEADS) & (lane < 2 * MLSTM_HEADS)
        dlogf = _chunk_rev_cumsum(jnp.where(is_f, d, 0.0), 0)
        out = jnp.where(is_f, dlogf * _sigmoid(-(g_ref[...] + bias_ref[...])), d)
        o_ref[...] = out.astype(BF16)
        acc_ref[0:1, :] += _colsum(out)

    return _rows("gate_bwd", body, [dgc, dgr_t, gcol, bcol],
                 [_sds((T, 128), BF16), _sds((8, 128), F32)], T)


def _head_norm(h, mu_axis=-1):
    mu = jnp.mean(h, axis=-1, keepdims=True)
    hc = h - mu
    r = lax.rsqrt(jnp.mean(hc * hc, axis=-1, keepdims=True) + NORM_EPS)
    return hc * r, r


def _mlstm_out(hm, pm, w):
    T = hm.shape[0]
    D = MLSTM_HEAD_DIM

    def body(h_ref, o_ref, w_ref, y_ref):
        for hd in range(MLSTM_HEADS):
            cols = slice(D * hd, D * hd + D)
            hn, _ = _head_norm(h_ref[:, cols])
            y_ref[:, cols] = (_sigmoid(o_ref[:, cols].astype(F32)) * hn * w_ref[:, cols]).astype(BF16)

    tr = min(ROW_TILE, T)
    return pl.pallas_call(
        body, name="mlstm_out", grid=(T // tr,),
        in_specs=[pl.BlockSpec((tr, 512), lambda i: (i, 0)), pl.BlockSpec((tr, 512), lambda i: (i, 1)),
                  pl.BlockSpec((1, 512), lambda i: (0, 0))],
        out_specs=pl.BlockSpec((tr, 512), lambda i: (i, 0)), out_shape=_sds((T, 512), BF16),
        compiler_params=_params(("parallel",)))(hm, pm, w)


def _mlstm_out_bwd_rows(ps, es):
    hm, vo, w_all = es
    D, width = MLSTM_HEAD_DIM, MLSTM_HEADS * MLSTM_HEAD_DIM
    dhs, dos, dws = [], [], []
    for hd in range(MLSTM_HEADS):
        cols = slice(D * hd, D * hd + D)
        hn, r = _head_norm(hm[:, cols])
        sg = _sigmoid(vo[:, width + D * hd:width + D * hd + D].astype(F32))
        dy, w = ps[0][:, cols], w_all[:, cols]
        dos.append(dy * hn * w * sg * (1.0 - sg))
        dyn = dy * sg
        dws.append(_colsum(dyn * hn))
        dhn = dyn * w
        dhs.append(r * (dhn - jnp.mean(dhn, axis=-1, keepdims=True)
                        - hn * jnp.mean(dhn * hn, axis=-1, keepdims=True)))
    cat = lambda parts: jnp.concatenate(parts, axis=1)
    return [cat(dhs), cat(dos)], [_acc_rows([cat(dws)])]


ADAM_TILE_ELEMS = 256 * 1024


def _adamw(name, w, g, m, v):
    R, C = w.shape
    fits = [t for t in range(8, R + 1, 8) if R % t == 0 and t * C <= ADAM_TILE_ELEMS]
    if fits or R * C <= ADAM_TILE_ELEMS:
        tr = fits[-1] if fits else R
        spec, grid = pl.BlockSpec((tr, C), lambda i: (i, 0)), (R // tr,)
    else:
        spec, grid = pl.BlockSpec((R, 128), lambda i: (0, i)), (C // 128,)
    c1 = 1.0 - ADAM_B1 ** ADAM_STEP
    c2 = 1.0 - ADAM_B2 ** ADAM_STEP

    def body(w_ref, g_ref, m_ref, v_ref, d_ref, mo_ref, vo_ref):
        g = g_ref[...]
        m = ADAM_B1 * m_ref[...] + (1.0 - ADAM_B1) * g
        v = ADAM_B2 * v_ref[...] + (1.0 - ADAM_B2) * (g * g)
        mo_ref[...] = m
        vo_ref[...] = v
        d_ref[...] = -ADAM_LR * ((m / c1) / (jnp.sqrt(v / c2) + ADAM_EPS) + ADAM_WD * w_ref[...])

    return pl.pallas_call(
        body, name=name, grid=grid, in_specs=[spec] * 4, out_specs=[spec] * 3,
        out_shape=[_sds((R, C), F32)] * 3, compiler_params=_params(("parallel",)))(w, g, m, v)


def _place():
    return lax.axis_index("x"), lax.axis_index("y"), lax.axis_index("c")


def _all_gather8(name, blk, space):
    m, n = blk.shape

    def body(x_ref, out_ref, send_sems, recv_sems, local_sem):
        x, y, c = _place()
        me, sibling = (x, y, c), (x, y, 1 - c)
        chips = [(1 - x, y), (x, 1 - y), (1 - x, 1 - y)]

        def rows(px, py, pc):
            return out_ref.at[pl.ds((4 * px + 2 * py + pc) * m, m), :]

        def copy(k, block, to, src=None):
            return pltpu.make_async_remote_copy(
                src_ref=rows(*block) if src is None else src, dst_ref=rows(*block),
                send_sem=send_sems.at[k], recv_sem=recv_sems.at[k],
                device_id=to, device_id_type=MESH)

        mine = pltpu.make_async_copy(x_ref, rows(*me), local_sem)
        mine.start()
        first = [copy(0, me, sibling, src=x_ref)]
        first += [copy(1 + j, me, (*chip, c), src=x_ref) for j, chip in enumerate(chips)]
        for cp in first:
            cp.start()
        passed = [copy(4 + j, (*chip, c), sibling) for j, chip in enumerate(chips)]
        for j, chip in enumerate(chips):
            copy(1 + j, (*chip, c), me).wait_recv()
            passed[j].start()
        copy(0, sibling, me).wait_recv()
        for j, chip in enumerate(chips):
            copy(4 + j, (*chip, 1 - c), me).wait_recv()
        for cp in first + passed:
            cp.wait_send()
        mine.wait()

    return pl.pallas_call(
        body, name=name, out_shape=_sds((8 * m, n), blk.dtype),
        in_specs=[pl.BlockSpec(memory_space=space)], out_specs=pl.BlockSpec(memory_space=space),
        scratch_shapes=[pltpu.SemaphoreType.DMA((7,)), pltpu.SemaphoreType.DMA((7,)),
                        pltpu.SemaphoreType.DMA],
        compiler_params=pltpu.CompilerParams(vmem_limit_bytes=VMEM_LIMIT))(blk)


def _hbm_specs(n):
    return [pl.BlockSpec(memory_space=pl.ANY)] * n


def _swap_halves_sibling(name, srcs):
    nw = len(srcs)

    def body(*refs):
        src_refs, dst_refs, send_sems, recv_sems = refs[:nw], refs[nw:2 * nw], refs[2 * nw], refs[2 * nw + 1]
        x, y, c = _place()
        cps = [pltpu.make_async_remote_copy(
            src_ref=src_refs[w].at[pl.ds(0, 4), 1 - c], dst_ref=dst_refs[w],
            send_sem=send_sems.at[w], recv_sem=recv_sems.at[w], device_id=(x, y, 1 - c),
            device_id_type=MESH) for w in range(nw)]
        for cp in cps:
            cp.start()
        for cp in cps:
            cp.wait()

    return pl.pallas_call(
        body, name=name, out_shape=[_sds(s.shape[:1] + s.shape[2:], s.dtype) for s in srcs],
        in_specs=_hbm_specs(nw), out_specs=_hbm_specs(nw),
        scratch_shapes=[pltpu.SemaphoreType.DMA((nw,)), pltpu.SemaphoreType.DMA((nw,))])(*srcs)


def _split_start(name, srcs, lands, copies, per_array, after):
    nw = len(srcs)

    def body(*refs):
        send_sems, recv_sems, token = refs[2 * nw + 1], refs[2 * nw + 2], refs[-1]
        for w in range(nw):
            for k, (s, d, dev) in enumerate(copies(refs[w], refs[nw + w], *_place())):
                pltpu.make_async_remote_copy(
                    src_ref=s, dst_ref=d, send_sem=send_sems.at[w * per_array + k],
                    recv_sem=recv_sems.at[w * per_array + k], device_id=dev, device_id_type=MESH).start()
        token[...] = jnp.zeros_like(token)

    hbm, sem = pl.BlockSpec(memory_space=pltpu.HBM), pl.BlockSpec(memory_space=pltpu.SEMAPHORE)
    arrays = list(srcs) + list(lands)
    out = pl.pallas_call(
        body, name=name,
        out_shape=(pltpu.SemaphoreType.DMA((nw * per_array,)), pltpu.SemaphoreType.DMA((nw * per_array,)),
                   *[pltpu.HBM(a.shape, a.dtype) for a in arrays], _sds((8, 128), F32)),
        in_specs=[hbm] * (2 * nw) + [pl.BlockSpec(memory_space=pl.ANY)],
        out_specs=(sem, sem, *[hbm] * (2 * nw), pl.BlockSpec(memory_space=pltpu.VMEM)),
        input_output_aliases={i: 2 + i for i in range(2 * nw)},
        compiler_params=pltpu.CompilerParams(has_side_effects=pltpu.SideEffectType.DATAFLOW_SIDE_EFFECTING))(
            *[pltpu.with_memory_space_constraint(a, pltpu.HBM) for a in arrays], after)
    return out[0], out[1], out[2:2 + nw], out[2 + nw:2 + 2 * nw], out[-1]


def _split_wait(name, started, after, waits, per_array):
    send_sems, recv_sems, srcs, lands, _ = started
    nw = len(srcs)

    def body(*refs):
        send_sems, recv_sems = refs[2 * nw], refs[2 * nw + 1]
        x, y, c = _place()
        for w in range(nw):
            for k, (s, d) in enumerate(waits(refs[w], refs[nw + w], x, y, c)):
                cp = pltpu.make_async_remote_copy(
                    src_ref=s, dst_ref=d, send_sem=send_sems.at[w * per_array + k],
                    recv_sem=recv_sems.at[w * per_array + k], device_id=(x, y, 1 - c),
                    device_id_type=MESH)
                cp.wait_send()
                cp.wait_recv()

    hbm, sem = pl.BlockSpec(memory_space=pltpu.HBM), pl.BlockSpec(memory_space=pltpu.SEMAPHORE)
    arrays = list(srcs) + list(lands)
    out = pl.pallas_call(
        body, name=name, out_shape=tuple(pltpu.HBM(a.shape, a.dtype) for a in arrays),
        in_specs=[hbm] * (2 * nw) + [sem, sem, pl.BlockSpec(memory_space=pl.ANY)],
        out_specs=tuple([hbm] * (2 * nw)), input_output_aliases={i: i for i in range(2 * nw)},
        compiler_params=pltpu.CompilerParams(has_side_effects=pltpu.SideEffectType.DATAFLOW_SIDE_EFFECTING))(
            *arrays, send_sems, recv_sems, after)
    return list(out[nw:])


def _other_chips(x, y):
    return [(1 - x, y), (x, 1 - y), (1 - x, 1 - y)]


def _gather_sends(src_ref, land_ref, x, y, c):
    to = land_ref.at[2 * x + y, c]
    return [(src_ref, to, (x, y, 1 - c))] + [(src_ref, to, (px, py, c)) for px, py in _other_chips(x, y)]


def _gather_lands(src_ref, land_ref, x, y, c):
    return [(src_ref, land_ref.at[2 * x + y, 1 - c])] + [
        (src_ref, land_ref.at[2 * px + py, c]) for px, py in _other_chips(x, y)]


def _gather_sends_all(src_ref, land_ref, x, y, c):
    to = land_ref.at[2 * x + y, c]
    return [(src_ref, to, (x, y, 1 - c))] + [
        (src_ref, to, (px, py, pc)) for px, py in _other_chips(x, y) for pc in (c, 1 - c)]


def _gather_lands_all(src_ref, land_ref, x, y, c):
    return [(src_ref, land_ref.at[2 * x + y, 1 - c])] + [
        (src_ref, land_ref.at[2 * px + py, pc]) for px, py in _other_chips(x, y) for pc in (c, 1 - c)]


def _scatter_sends(src_ref, land_ref, x, y, c):
    return [(src_ref.at[2 * px + py], land_ref.at[2 * x + y], (px, py, c)) for px, py in _other_chips(x, y)]


def _scatter_lands(src_ref, land_ref, x, y, c):
    return [(src_ref.at[2 * x + y], land_ref.at[2 * px + py]) for px, py in _other_chips(x, y)]


def _forward_sibling(name, lands):
    nw = len(lands)

    def body(*refs):
        land_refs, out_refs, send_sems, recv_sems = refs[:nw], refs[nw:2 * nw], refs[2 * nw], refs[2 * nw + 1]
        x, y, c = _place()
        cps = []
        for w in range(nw):
            cps += [pltpu.make_async_remote_copy(
                src_ref=land_refs[w].at[2 * px + py, c], dst_ref=out_refs[w].at[2 * px + py, c],
                send_sem=send_sems.at[w, j], recv_sem=recv_sems.at[w, j], device_id=(x, y, 1 - c),
                device_id_type=MESH) for j, (px, py) in enumerate(_other_chips(x, y))]
        for cp in cps:
            cp.start()
        for w in range(nw):
            for j, (px, py) in enumerate(_other_chips(x, y)):
                slot = out_refs[w].at[2 * px + py, 1 - c]
                pltpu.make_async_remote_copy(src_ref=slot, dst_ref=slot, send_sem=send_sems.at[w, j],
                                             recv_sem=recv_sems.at[w, j], device_id=(x, y, 1 - c),
                                             device_id_type=MESH).wait_recv()
        for cp in cps:
            cp.wait_send()

    return pl.pallas_call(
        body, name=name, out_shape=[_sds(a.shape, a.dtype) for a in lands],
        in_specs=_hbm_specs(nw), out_specs=_hbm_specs(nw), input_output_aliases={i: i for i in range(nw)},
        scratch_shapes=[pltpu.SemaphoreType.DMA((nw, 3)), pltpu.SemaphoreType.DMA((nw, 3))])(*lands)


def _share_halves(name, halves):
    nw = len(halves)

    def body(*refs):
        in_refs, out_refs, send_sems, recv_sems = refs[:nw], refs[nw:2 * nw], refs[2 * nw], refs[2 * nw + 1]
        x, y, c = _place()
        cps = [pltpu.make_async_remote_copy(
            src_ref=in_refs[w].at[c], dst_ref=out_refs[w].at[c], send_sem=send_sems.at[w],
            recv_sem=recv_sems.at[w], device_id=(x, y, 1 - c), device_id_type=MESH) for w in range(nw)]
        for cp in cps:
            cp.start()
        for w in range(nw):
            slot = out_refs[w].at[1 - c]
            pltpu.make_async_remote_copy(src_ref=slot, dst_ref=slot, send_sem=send_sems.at[w],
                                         recv_sem=recv_sems.at[w], device_id=(x, y, 1 - c),
                                         device_id_type=MESH).wait_recv()
        for cp in cps:
            cp.wait_send()

    return pl.pallas_call(
        body, name=name, out_shape=[_sds(a.shape, a.dtype) for a in halves],
        in_specs=_hbm_specs(nw), out_specs=_hbm_specs(nw), input_output_aliases={i: i for i in range(nw)},
        scratch_shapes=[pltpu.SemaphoreType.DMA((nw,)), pltpu.SemaphoreType.DMA((nw,))])(*halves)


def _place_blocks(name, blks, place):
    nw = len(blks)

    def body(p_ref, *refs):
        for b_ref, o_ref in zip(refs[:nw], refs[nw:]):
            o_ref[...] = b_ref[...]

    return pl.pallas_call(
        body, name=name,
        grid_spec=pltpu.PrefetchScalarGridSpec(
            num_scalar_prefetch=1, grid=(1,),
            in_specs=[pl.BlockSpec(b.shape, lambda i, p: (0, 0)) for b in blks],
            out_specs=[pl.BlockSpec((None, None) + b.shape, lambda i, p: (p[0], p[1], 0, 0)) for b in blks]),
        out_shape=[_sds((4, 2) + b.shape, b.dtype) for b in blks],
        compiler_params=_params(("arbitrary",)))(place, *blks)


def _pair_sum(name, fulls, gots, place):
    nw = len(fulls)

    def body(p_ref, *refs):
        s = pl.program_id(0)
        for a_ref, b_ref, o_ref, l_ref in zip(refs[:nw], refs[nw:2 * nw], refs[2 * nw:3 * nw], refs[3 * nw:]):
            o_ref[...] = (a_ref[...].astype(F32) + b_ref[...].astype(F32)).astype(o_ref.dtype)

            @pl.when(s == p_ref[0])
            def _():
                l_ref[...] = o_ref[...]

    slab = lambda a: pl.BlockSpec((None,) + a.shape[1:], lambda s, p: (s, 0, 0))
    mine = lambda a: pl.BlockSpec((None,) + a.shape[1:], lambda s, p: (p[0], 0, 0))
    out = pl.pallas_call(
        body, name=name,
        grid_spec=pltpu.PrefetchScalarGridSpec(
            num_scalar_prefetch=1, grid=(4,),
            in_specs=[pl.BlockSpec((None, None) + a.shape[2:], lambda s, p: (s, p[1], 0, 0)) for a in fulls]
            + [slab(b) for b in gots],
            out_specs=[slab(b) for b in gots] + [mine(b) for b in gots]),
        out_shape=[_sds(b.shape, BF16) for b in gots] * 2,
        compiler_params=_params(("arbitrary",)))(place, *fulls, *gots)
    return out[:nw], out[nw:]


def _sum4(name, arrs, place):
    nw = len(arrs)

    def body(p_ref, *refs):
        for a_ref, o_ref in zip(refs[:nw], refs[nw:]):
            acc = a_ref[0].astype(F32)
            for s in range(1, 4):
                acc = acc + a_ref[s].astype(F32)
            o_ref[...] = acc

    return pl.pallas_call(
        body, name=name,
        grid_spec=pltpu.PrefetchScalarGridSpec(
            num_scalar_prefetch=1, grid=(1,),
            in_specs=[pl.BlockSpec(a.shape, lambda i, p: (0, 0, 0)) for a in arrs],
            out_specs=[pl.BlockSpec((None,) + a.shape[1:], lambda i, p: (p[1], 0, 0)) for a in arrs]),
        out_shape=[_sds((2,) + a.shape[1:], F32) for a in arrs],
        compiler_params=_params(("arbitrary",)))(place, *arrs)


def _small_update(gathered, w, m, v):
    n = w.shape[1]
    tn = 2048
    c1 = 1.0 - ADAM_B1 ** ADAM_STEP
    c2 = 1.0 - ADAM_B2 ** ADAM_STEP

    def body(g_ref, w_ref, m_ref, v_ref, go_ref, d_ref, mo_ref, vo_ref):
        g = g_ref[0:1, :]
        for d in range(1, 8):
            g = g + g_ref[d:d + 1, :]
        go_ref[...] = g
        m = ADAM_B1 * m_ref[...] + (1.0 - ADAM_B1) * g
        v = ADAM_B2 * v_ref[...] + (1.0 - ADAM_B2) * (g * g)
        mo_ref[...] = m
        vo_ref[...] = v
        d_ref[...] = -ADAM_LR * ((m / c1) / (jnp.sqrt(v / c2) + ADAM_EPS) + ADAM_WD * w_ref[...])

    row = pl.BlockSpec((1, tn), lambda i: (0, i))
    return pl.pallas_call(
        body, name="small_update", grid=(n // tn,),
        in_specs=[pl.BlockSpec((8, tn), lambda i: (0, i)), row, row, row], out_specs=[row] * 4,
        out_shape=[_sds((1, n), F32)] * 4, compiler_params=_params(("parallel",)))(gathered, w, m, v)


def _swiglu(ps, es):
    g, u = ps
    return g * _sigmoid(g) * u, g, u


def _swiglu_bwd(ps, es):
    g, u = es[0].astype(F32), es[1].astype(F32)
    sg = _sigmoid(g)
    return ps[0] * u * (sg * (1.0 + g * (1.0 - sg))), ps[0] * (g * sg)


def _merge(ps, es):
    ga, gm = [e.astype(F32) for e in es]
    return _sigmoid(ga) * ps[0] + _sigmoid(gm) * ps[1], ps[0], ps[1]


def _merge_bwd(ps, es):
    a, b, ga, gm = [e.astype(F32) for e in es]
    sa, sm = _sigmoid(ga), _sigmoid(gm)
    dm = ps[0]
    return dm * sa, dm * sm, dm * a * (sa * (1.0 - sa)), dm * b * (sm * (1.0 - sm))


W_IN_PIECES = (("q", 512), ("kv", 256), ("mqk", 1024), ("mv", 512), ("mo", 512), ("if", 8),
               ("ga", 1024), ("gm", 1024))


def _local_step(x, tgt, pos_col, mod, sp, in_weights, late_weights, ffn_grads, mixer_grads):
    sh_m, sc_m, gate_m, sh_f, sc_f, gate_f = mod
    inv = ROPE_THETA ** (-2.0 * jnp.arange(HEAD_DIM // 2, dtype=F32) / HEAD_DIM)
    cos, sin = _rope_tables(pos_col, jnp.tile(inv, 4).reshape(1, 128))
    W = dict(in_weights(cos))
    h, pa, pqk, pvo, pif, pg = _proj_in(x, sp["g_pre_mix"], sc_m, sh_m, [
        (jnp.concatenate([W["q"], W["kv"]]), F32, 256), (W["mqk"], F32, 512),
        (jnp.concatenate([W["mv"], W["mo"]]), BF16, 512), (W["if"], F32, 128),
        (jnp.concatenate([W["ga"], W["gm"]]), BF16, 512)])
    ya = _attn_fwd(pa, cos, sin, sp["sinks"])
    qk = _conv_fwd(pqk, sp["conv_w"], sp["conv_b"])
    bcol = jnp.pad(sp["b_if"], ((0, 0), (0, 120)))
    brow = jnp.broadcast_to(sp["b_if"].reshape(8, 1), (8, 128))
    grow = pif[:, :8].T
    hm, cs, ns, ms = _mlstm_fwd(qk, pvo, pif, bcol, grow, brow)
    ym = _mlstm_out(hm, pvo, sp["norm_w"])
    W.update(late_weights(ym))
    w_fg, w_fu, w_fd = W["fg"], W["fu"], W["fd"]
    merged, br_a, br_m = _mm("branches", [[(ya, W["ba"])], [(ym, W["bm"])]],
                             [(pg, 0), (pg, 1)], _merge, [BF16, BF16, BF16], cn=512, nt=True)
    wide, narrow = (D_MODEL, F32), (D_MODEL, BF16)
    mix, x1, h2 = _mm_rows("mix_out", [[(merged, W["out"])]],
                           [x, gate_m, sp["g_post_mix"], sp["g_pre_ffn"], sc_f, sh_f],
                           _res_norm_rows, [wide, wide, narrow], [], cn=512)
    act, gt, up = _mm("ffn_in", [[(h2, w_fg)], [(h2, w_fu)]], [], _swiglu, [BF16] * 3,
                      cn=256, nt=True)
    dy, dff, acc_l, loss = _mm_rows("ffn_down", [[(act, w_fd)]], [x1, tgt, gate_f, sp["g_post_ffn"]],
                                    _final_loss_rows, [wide, narrow], [(8, D_MODEL), (1, 128)], cn=512)

    G = {}
    dgt, dup = _mm("ffn_down_bwd", [[(dff, w_fd)]], [gt, up], _swiglu_bwd, [BF16, BF16],
                   cn=256, nt=True)
    g_fd = _mm_tn("dw_ffn_down", act, dff, BF16, 1408, 512)
    g_fg = _mm_tn("dw_ffn_gate", dgt, h2, BF16, 1408, 1024)
    g_fu = _mm_tn("dw_ffn_up", dup, h2, BF16, 1408, 1024)
    tie = ffn_grads(g_fg, g_fu, g_fd)
    dx1, dmix, acc_r = _mm_rows(
        "ffn_in_bwd", [[(dgt, w_fg), (dup, w_fu)]],
        [x1, mix, dy, sc_f + tie, gate_m, sp["g_pre_ffn"], sp["g_post_mix"]],
        _res_norm_bwd_rows, [wide, narrow], [(8, D_MODEL)], cn=512, tm=256)
    d_a, d_m, dga, dgm = _mm("mix_out_bwd", [[(dmix, W["out"])]],
                             [br_a, br_m, (pg, 0), (pg, 1)], _merge_bwd,
                             [BF16] * 4, cn=512, nt=True)
    G["out"] = _mm_tn("dw_out", merged, dmix, BF16, 1024, 512)
    dya, = _mm("branch_attn_bwd", [[(d_a, W["ba"])]], [], _first, [F32], cn=512)
    heads = MLSTM_HEADS * MLSTM_HEAD_DIM
    dhm, do_m, acc_n = _mm_rows("branch_mlstm_bwd", [[(d_m, W["bm"])]], [hm, pvo, sp["norm_w"]],
                                _mlstm_out_bwd_rows, [(heads, F32), (heads, BF16)], [(8, heads)], cn=512)
    G["ba"] = _mm_tn("dw_branch_attn", d_a, ya, BF16, 1024, 512)
    G["bm"] = _mm_tn("dw_branch_mlstm", d_m, ym, BF16, 1024, 512)
    dqk, dv_m, dgc, dgr = _mlstm_bwd(qk, pvo, pif, bcol, grow, brow, cs, ns, ms, dhm)
    dif, acc_g = _gate_bwd(dgc, jnp.pad(dgr.T, ((0, 0), (0, 120))), pif, bcol)
    du, acc_c = _conv_bwd(pqk, sp["conv_w"], sp["conv_b"], dqk)
    dq_a, dkv, dsink = _attn_bwd(pa, cos, sin, sp["sinks"], dya)
    dproj = {"q": dq_a, "kv": dkv, "mqk": du, "mv": dv_m, "mo": do_m, "if": dif, "ga": dga, "gm": dgm}
    for k, _ in W_IN_PIECES:
        G[k] = _mm_tn("dw_in_" + k, dproj[k], h, BF16, dproj[k].shape[1], 1024)
    w_tied = dict(W, **{"if": W["if"] + mixer_grads(G).astype(BF16)})
    dx, acc_p = _mm_rows("proj_bwd", [[(dproj[k], w_tied[k]) for k, _ in W_IN_PIECES]],
                         [x, dx1, sp["g_pre_mix"], sc_m], _pre_norm_bwd_rows, [wide], [(8, D_MODEL)],
                         cn=512, tm=256)

    small = {
        "mod": jnp.concatenate([acc_p[1], acc_p[0], acc_r[3], acc_r[1], acc_r[0], acc_l[0]]),
        "g_pre_mix": acc_p[2], "g_post_mix": acc_r[4], "b_if": acc_g[0, :8],
        "conv_w": acc_c[:CONV_WIDTH].reshape(-1), "conv_b": acc_c[CONV_WIDTH],
        "sinks": dsink[:, 0], "norm_w": acc_n[0], "g_pre_ffn": acc_r[2], "g_post_ffn": acc_l[1]}
    return loss, dx, small


IN_WIDTH = sum(n for _, n in W_IN_PIECES)
IN_SHARD = IN_WIDTH // 4
IN_SHARD_PAD = -(-IN_SHARD // 32) * 32


def _split_w_in(w_in_t):
    out, off = {}, 0
    for k, n in W_IN_PIECES:
        out[k] = w_in_t[off:off + n]
        off += n
    out["if"] = jnp.pad(out["if"], ((0, 120), (0, 0)))
    return out


def _halves(a):
    return a.reshape(4, 2, a.shape[0] // 8, a.shape[1])


SMALL = (("b_ada", 6144), ("g_pre_mix", 1024), ("g_post_mix", 1024), ("b_if", 128), ("conv_w", 4096),
         ("conv_b", 1024), ("sinks", 128), ("norm_w", 512), ("g_pre_ffn", 1024), ("g_post_ffn", 1024))
SMALL_LEN = 8 * 2048


def _pack_small(vals):
    parts = []
    for k, n in SMALL:
        v = vals[k].reshape(-1)
        parts.append(jnp.pad(v, (0, n - v.shape[0])))
    flat = jnp.concatenate(parts)
    return jnp.pad(flat, (0, SMALL_LEN - flat.shape[0]))


def _unpack_small(flat, shapes):
    out, off = {}, 0
    for k, n in SMALL:
        size = 1
        for d in shapes[k]:
            size *= d
        out[k] = flat[off:off + size].reshape(shapes[k])
        off += n
    return out


def kernel(x, c, positions, w_ada, b_ada, g_pre_mix, g_post_mix, w_in, b_if, conv_w, conv_b, attn_sinks, mlstm_norm_w, w_branch_attn, w_branch_mlstm, w_out, g_pre_ffn, g_post_ffn, w_ffn_gate, w_ffn_up, w_ffn_down, loss_target, m_w_ada, m_b_ada, m_g_pre_mix, m_g_post_mix, m_w_in, m_b_if, m_conv_w, m_conv_b, m_attn_sinks, m_mlstm_norm_w, m_w_branch_attn, m_w_branch_mlstm, m_w_out, m_g_pre_ffn, m_g_post_ffn, m_w_ffn_gate, m_w_ffn_up, m_w_ffn_down, v_w_ada, v_b_ada, v_g_pre_mix, v_g_post_mix, v_w_in, v_b_if, v_conv_w, v_conv_b, v_attn_sinks, v_mlstm_norm_w, v_w_branch_attn, v_w_branch_mlstm, v_w_out, v_g_pre_ffn, v_g_post_ffn, v_w_ffn_gate, v_w_ffn_up, v_w_ffn_down):
    xi, yi, ci = _place()
    chip = 2 * xi + yi
    dev = 2 * chip + ci
    T = x.shape[1]
    ada_cols = w_ada.shape[2]

    place = jnp.stack([chip, ci]).astype(jnp.int32)

    def my_half(a):
        n = a.shape[0] // 2
        return lax.dynamic_slice_in_dim(a, ci * n, n, axis=0).astype(BF16)

    blk = jnp.concatenate([c.reshape(-1), conv_w.reshape(-1)]).reshape(8, 256)
    got = _all_gather8("gather_cond", blk, pltpu.VMEM).reshape(8, 2048)
    c_all = got[:, :D_MODEL].astype(BF16)
    conv_full = got[::2, D_MODEL:].reshape(4, CONV_WIDTH, -1).transpose(1, 0, 2).reshape(CONV_WIDTH, -1)

    b_sh = lax.dynamic_slice_in_dim(b_ada, chip * ada_cols, ada_cols, axis=1)
    mod_part, = _mm("ada_mod", [[(c_all, w_ada[0].astype(BF16))]], [b_sh],
                    lambda ps, es: (ps[0] + es[0],), [F32], cn=512, tm=8)
    mod_all = _all_gather8("gather_mod", mod_part, pltpu.VMEM).reshape(4, 2, 8, ada_cols)[:, 0]
    mod = lax.dynamic_index_in_dim(mod_all, dev, axis=1, keepdims=False).reshape(6, 1, D_MODEL)

    def gather_start(name, blks, after, sends, copies):
        return _split_start(name + "_start", blks, _place_blocks(name + "_place", blks, place),
                            sends, copies, after)

    w_in_t = jnp.pad(w_in[0].T, ((0, IN_SHARD_PAD - IN_SHARD), (0, 0)))
    in_started = gather_start("in_gather", [my_half(w_in_t)], mod, _gather_sends, 4)
    late_keys = ("fg", "fu", "fd", "out", "ba", "bm")
    late_started = gather_start(
        "late_gather",
        [my_half(w_ffn_gate[0].T), my_half(w_ffn_up[0].T), my_half(w_ffn_down[0]), my_half(w_out[0]),
         my_half(w_branch_attn[0].T), my_half(w_branch_mlstm[0].T)], in_started[4], _gather_sends_all, 7)
    mod = mod + (in_started[4][0, 0] + late_started[4][0, 0])

    def in_weights(after):
        g_in, = _forward_sibling("in_gather_forward",
                                 _split_wait("in_gather_wait", in_started, after, _gather_lands, 4))
        return _split_w_in(g_in.reshape(4, IN_SHARD_PAD, D_MODEL)[:, :IN_SHARD].reshape(IN_WIDTH, D_MODEL))

    def late_weights(after):
        lands = _split_wait("late_gather_wait", late_started, after, _gather_lands_all, 7)
        return {k: a.reshape(-1, a.shape[-1]) for k, a in zip(late_keys, lands)}

    sent = {}

    def scatter_start(name, groups):
        pairs, lands = _pair_sum(name + "_pair_sum", groups, _swap_halves_sibling(name + "_pair", groups), place)
        sent[name] = _split_start(name + "_start", pairs, lands, _scatter_sends, 3, pairs[0])
        return sent[name][4][0, 0]

    def ffn_grads(g_fg, g_fu, g_fd):
        return scatter_start("rs_ffn", [_halves(g_fg), _halves(g_fu), _halves(g_fd)])

    def mixer_grads(G):
        g_in_t = jnp.concatenate([G[k][:n] for k, n in W_IN_PIECES]).reshape(4, IN_SHARD, D_MODEL)
        g_in_t = jnp.pad(g_in_t, ((0, 0), (0, IN_SHARD_PAD - IN_SHARD), (0, 0)))
        return scatter_start("rs_mix", [g_in_t.reshape(4, 2, IN_SHARD_PAD // 2, D_MODEL), _halves(G["out"]),
                                        _halves(G["ba"]), _halves(G["bm"])])

    sp = {"g_pre_mix": g_pre_mix, "g_post_mix": g_post_mix, "b_if": b_if, "conv_w": conv_full,
          "conv_b": conv_b, "sinks": attn_sinks, "norm_w": mlstm_norm_w, "g_pre_ffn": g_pre_ffn,
          "g_post_ffn": g_post_ffn}
    loss, dx, small = _local_step(x[0], loss_target[0], positions.reshape(T, 1), [mod[i] for i in range(6)],
                                  sp, in_weights, late_weights, ffn_grads, mixer_grads)

    reds = (_sum4("rs_ffn_chip_sum", _split_wait("rs_ffn_wait", sent["rs_ffn"], dx, _scatter_lands, 3), place)
            + _sum4("rs_mix_chip_sum", _split_wait("rs_mix_wait", sent["rs_mix"], dx, _scatter_lands, 3), place))
    gsh = {k: s.reshape(-1, s.shape[-1])
           for k, s in zip(("fg", "fu", "fd", "w_in", "out", "ba", "bm"), _share_halves("rs_share", reds))}
    gsh["w_in"] = gsh["w_in"][:IN_SHARD]

    small["b_ada"] = small.pop("mod")
    vec = _pack_small(small).reshape(8, 2048)
    g_all = _all_gather8("gather_small", vec, pltpu.VMEM).reshape(8, SMALL_LEN)
    dmod_sh = lax.dynamic_slice_in_dim(g_all[:, :6 * D_MODEL], chip * ada_cols, ada_cols, axis=1)
    g_w_ada = _mm_tn("dw_ada", c_all, dmod_sh.astype(BF16), F32, D_MODEL, 512, 8)

    smalls = {"b_ada": (b_ada, m_b_ada, v_b_ada), "g_pre_mix": (g_pre_mix, m_g_pre_mix, v_g_pre_mix),
              "g_post_mix": (g_post_mix, m_g_post_mix, v_g_post_mix), "b_if": (b_if, m_b_if, v_b_if),
              "conv_w": None, "conv_b": (conv_b, m_conv_b, v_conv_b),
              "sinks": (attn_sinks, m_attn_sinks, v_attn_sinks),
              "norm_w": (mlstm_norm_w, m_mlstm_norm_w, v_mlstm_norm_w),
              "g_pre_ffn": (g_pre_ffn, m_g_pre_ffn, v_g_pre_ffn),
              "g_post_ffn": (g_post_ffn, m_g_post_ffn, v_g_post_ffn)}
    shapes = {k: (t[0].shape if t is not None else (1, CONV_WIDTH, D_MODEL)) for k, t in smalls.items()}
    zeros = jnp.zeros((CONV_WIDTH * D_MODEL,), F32)
    packs = [_pack_small({k: (t[i] if t is not None else zeros) for k, t in smalls.items()}).reshape(1, -1)
             for i in range(3)]
    s_out = [_unpack_small(o[0], shapes) for o in _small_update(g_all, *packs)]
    g_conv = lax.dynamic_slice_in_dim(s_out[0]["conv_w"], chip * conv_w.shape[2], conv_w.shape[2], axis=2)

    res = {}
    for k, t in smalls.items():
        if t is not None:
            res[k] = tuple(o[k] for o in s_out)
    res["conv_w"] = (g_conv, *[o[None] for o in _adamw("adam_conv_w", conv_w[0], g_conv[0], m_conv_w[0], v_conv_w[0])])
    res["w_ada"] = (g_w_ada[None], *[o[None] for o in _adamw("adam_w_ada", w_ada[0], g_w_ada, m_w_ada[0], v_w_ada[0])])
    bigs = {"w_in": (w_in, m_w_in, v_w_in), "ba": (w_branch_attn, m_w_branch_attn, v_w_branch_attn),
            "bm": (w_branch_mlstm, m_w_branch_mlstm, v_w_branch_mlstm), "out": (w_out, m_w_out, v_w_out),
            "fg": (w_ffn_gate, m_w_ffn_gate, v_w_ffn_gate), "fu": (w_ffn_up, m_w_ffn_up, v_w_ffn_up),
            "fd": (w_ffn_down, m_w_ffn_down, v_w_ffn_down)}
    for k, (w, m, v) in bigs.items():
        if k in ("w_in", "fg", "fu"):
            res[k] = tuple(o.T[None] for o in (gsh[k], *_adamw("adam_" + k, w[0].T, gsh[k], m[0].T, v[0].T)))
        else:
            g = gsh[k].T if k in ("ba", "bm") else gsh[k]
            res[k] = (g[None], *[o[None] for o in _adamw("adam_" + k, w[0], g, m[0], v[0])])

    order = ("w_ada", "b_ada", "g_pre_mix", "g_post_mix", "w_in", "b_if", "conv_w", "conv_b", "sinks",
             "norm_w", "ba", "bm", "out", "g_pre_ffn", "g_post_ffn", "fg", "fu", "fd")
    total = lax.psum(loss[0, 0], ("x", "y", "c"))
    return (total, dx[None], *[res[k][0] for k in order], *[res[k][1] for k in order],
            *[res[k][2] for k in order], *[res[k][3] for k in order])
```

```python
import functools

import jax
import jax.numpy as jnp
from jax import lax
from jax.experimental import pallas as pl
from jax.experimental.pallas import tpu as pltpu

F32, BF16 = jnp.float32, jnp.bfloat16
MESH = pl.DeviceIdType.MESH

D_MODEL = 1024
N_Q_HEADS, N_KV_HEADS, HEAD_DIM, WINDOW = 8, 2, 64, 128
ROPE_THETA = 10000.0
MLSTM_HEADS, MLSTM_HEAD_DIM, MLSTM_CHUNK, CONV_WIDTH = 4, 128, 64, 4
D_FF = 2816
NORM_EPS = 1e-6
ADAM_LR, ADAM_B1, ADAM_B2, ADAM_EPS, ADAM_WD, ADAM_STEP = 0.001, 0.9, 0.999, 1e-08, 0.01, 10

VMEM_LIMIT = 56 * 1024 * 1024
ROW_TILE = 256
MM_TM = 512
MM_TT = 1024
ATTN_BLK = WINDOW
STEP_ROWS = 2 * MLSTM_CHUNK
NEG_INF = float("-inf")


def _params(sem):
    return pltpu.CompilerParams(dimension_semantics=sem, vmem_limit_bytes=VMEM_LIMIT)


def _sds(shape, dtype):
    return jax.ShapeDtypeStruct(shape, dtype)


def _sigmoid(x):
    return 1.0 / (1.0 + jnp.exp(-x))


def _dot(a, b, ca, cb):
    return lax.dot_general(a, b, (((ca,), (cb,)), ((), ())), preferred_element_type=F32)


def _bdot(a, b, ca, cb):
    return lax.dot_general(a, b, (((ca,), (cb,)), ((0,), (0,))), preferred_element_type=F32)


def _bdot_rows(a, b):
    return jnp.stack([_dot(a[h], b[h], 0, 0) for h in range(a.shape[0])])


def _mm(name, prods, extras, epi, out_dtypes, cn, nt=False, tm=MM_TM):
    flat = [ab for p in prods for ab in p]
    counts = [len(p) for p in prods]
    M = flat[0][0].shape[0]
    N = flat[0][1].shape[0 if nt else 1]
    tm = min(tm, M)
    n_in = 2 * len(flat) + len(extras)

    def body(*refs):
        ins, outs = refs[:n_in], refs[n_in:]
        for j in range(N // cn):
            cols = slice(j * cn, (j + 1) * cn)
            k, ps = 0, []
            for cnt in counts:
                acc = None
                for _ in range(cnt):
                    b = ins[k + 1][cols, :] if nt else ins[k + 1][:, cols]
                    d = _dot(ins[k][...], b, 1, 1 if nt else 0)
                    acc = d if acc is None else acc + d
                    k += 2
                ps.append(acc)
            res = epi(ps, [r[:, cols] for r in ins[k:]])
            for o, r in zip(outs, res):
                o[:, cols] = r.astype(o.dtype)

    in_specs, args = [], []
    for a, b in flat:
        in_specs.append(pl.BlockSpec((tm, a.shape[1]), lambda i: (i, 0)))
        in_specs.append(pl.BlockSpec(b.shape, lambda i: (0, 0), pipeline_mode=pl.Buffered(1)))
        args += [a, b]
    for e in extras:
        e, off = e if isinstance(e, tuple) else (e, 0)
        rows = 1 if e.shape[0] == 1 else tm
        in_specs.append(pl.BlockSpec((rows, N), lambda i, off=off, rows=rows: (0 if rows == 1 else i, off)))
        args.append(e)
    return pl.pallas_call(
        body, name=name, grid=(M // tm,), in_specs=in_specs,
        out_specs=[pl.BlockSpec((tm, N), lambda i: (i, 0)) for _ in out_dtypes],
        out_shape=[_sds((M, N), dt) for dt in out_dtypes],
        compiler_params=_params(("parallel",)))(*args)


def _mm_rows(name, prods, extras, epi, outs, accs, cn, nt=False, tm=MM_TM):
    flat = [ab for p in prods for ab in p]
    counts = [len(p) for p in prods]
    M = flat[0][0].shape[0]
    N = flat[0][1].shape[0 if nt else 1]
    tm = min(tm, M)
    n_mm, n_in, n_out = 2 * len(flat), 2 * len(flat) + len(extras), len(outs)

    def body(*refs):
        ins, out_refs, acc_refs = refs[:n_in], refs[n_in:n_in + n_out], refs[n_in + n_out:]

        @pl.when(pl.program_id(0) == 0)
        def _():
            for a in acc_refs:
                a[...] = jnp.zeros_like(a)

        chunks = [[] for _ in counts]
        for j in range(N // cn):
            cols = slice(j * cn, (j + 1) * cn)
            k = 0
            for p, cnt in enumerate(counts):
                acc = None
                for _ in range(cnt):
                    b = ins[k + 1][cols, :] if nt else ins[k + 1][:, cols]
                    d = _dot(ins[k][...], b, 1, 1 if nt else 0)
                    acc = d if acc is None else acc + d
                    k += 2
                chunks[p].append(acc)
        ps = [c[0] if len(c) == 1 else jnp.concatenate(c, axis=1) for c in chunks]
        res, incs = epi(ps, [r[...] for r in ins[n_mm:]])
        for o, r in zip(out_refs, res):
            o[...] = r.astype(o.dtype)
        for a, inc in zip(acc_refs, incs):
            a[...] += inc

    in_specs, args = [], []
    for a, b in flat:
        in_specs.append(pl.BlockSpec((tm, a.shape[1]), lambda i: (i, 0)))
        in_specs.append(pl.BlockSpec(b.shape, lambda i: (0, 0), pipeline_mode=pl.Buffered(1)))
        args += [a, b]
    for e in extras:
        rows = 1 if e.shape[0] == 1 else tm
        in_specs.append(pl.BlockSpec((rows, e.shape[1]), lambda i, rows=rows: (0 if rows == 1 else i, 0)))
        args.append(e)
    return pl.pallas_call(
        body, name=name, grid=(M // tm,), in_specs=in_specs,
        out_specs=[pl.BlockSpec((tm, w), lambda i: (i, 0)) for w, _ in outs]
        + [pl.BlockSpec(s, lambda i: (0, 0)) for s in accs],
        out_shape=[_sds((M, w), dt) for w, dt in outs] + [_sds(s, F32) for s in accs],
        compiler_params=_params(("arbitrary",)))(*args)


FFN_CHUNK = 256
FFN_TM = 256


def _ffn_fwd(h2, w_fg, w_fu, w_fd, row_extras, row_epi):
    T = h2.shape[0]
    tm = min(FFN_TM, T)
    n_ex = len(row_extras)

    def body(h_ref, fg_ref, fu_ref, fd_ref, *rest):
        ex, (act_ref, g_ref, u_ref, dy_ref, dff_ref, acc_ref, loss_ref) = rest[:n_ex], rest[n_ex:]

        @pl.when(pl.program_id(0) == 0)
        def _():
            acc_ref[...] = jnp.zeros_like(acc_ref)
            loss_ref[...] = jnp.zeros_like(loss_ref)

        h = h_ref[...]
        ff = None
        for j in range(D_FF // FFN_CHUNK):
            cols = slice(j * FFN_CHUNK, (j + 1) * FFN_CHUNK)
            act, g, u = _swiglu([_dot(h, fg_ref[cols, :], 1, 1), _dot(h, fu_ref[cols, :], 1, 1)], None)
            act = act.astype(BF16)
            act_ref[:, cols], g_ref[:, cols], u_ref[:, cols] = act, g.astype(BF16), u.astype(BF16)
            d = _dot(act, fd_ref[cols, :], 1, 0)
            ff = d if ff is None else ff + d
        (dy, dff), (acc, loss) = row_epi([ff], [r[...] for r in ex])
        dy_ref[...] = dy
        dff_ref[...] = dff.astype(BF16)
        acc_ref[...] += acc
        loss_ref[...] += loss

    tile = lambda w: pl.BlockSpec((tm, w), lambda i: (i, 0))
    whole = lambda a: pl.BlockSpec(a.shape, lambda i: (0, 0), pipeline_mode=pl.Buffered(1))
    ex_spec = lambda e: pl.BlockSpec(e.shape, lambda i: (0, 0)) if e.shape[0] == 1 else tile(e.shape[1])
    return pl.pallas_call(
        body, name="ffn_fwd", grid=(T // tm,),
        in_specs=[tile(D_MODEL), whole(w_fg), whole(w_fu), whole(w_fd)] + [ex_spec(e) for e in row_extras],
        out_specs=[tile(D_FF)] * 3 + [tile(D_MODEL)] * 2
        + [pl.BlockSpec((8, D_MODEL), lambda i: (0, 0)), pl.BlockSpec((1, 128), lambda i: (0, 0))],
        out_shape=[_sds((T, D_FF), BF16)] * 3 + [_sds((T, D_MODEL), F32), _sds((T, D_MODEL), BF16),
                                                 _sds((8, D_MODEL), F32), _sds((1, 128), F32)],
        compiler_params=_params(("arbitrary",)))(h2, w_fg, w_fu, w_fd, *row_extras)


def _ffn_bwd(dff, gt, up, w_fd, w_fg, w_fu, row_extras, row_epi):
    T = dff.shape[0]
    tm = min(FFN_TM, T)
    n_ex = len(row_extras)

    def body(d_ref, g_ref, u_ref, fd_ref, fg_ref, fu_ref, *rest):
        ex, (dg_ref, du_ref, dx1_ref, dmix_ref, acc_ref) = rest[:n_ex], rest[n_ex:]

        @pl.when(pl.program_id(0) == 0)
        def _():
            acc_ref[...] = jnp.zeros_like(acc_ref)

        dff_t = d_ref[...]
        dh = None
        for j in range(D_FF // FFN_CHUNK):
            cols = slice(j * FFN_CHUNK, (j + 1) * FFN_CHUNK)
            dg, du = _swiglu_bwd([_dot(dff_t, fd_ref[cols, :], 1, 1)], [g_ref[:, cols], u_ref[:, cols]])
            dg, du = dg.astype(BF16), du.astype(BF16)
            dg_ref[:, cols], du_ref[:, cols] = dg, du
            d = _dot(dg, fg_ref[cols, :], 1, 0) + _dot(du, fu_ref[cols, :], 1, 0)
            dh = d if dh is None else dh + d
        (dx1, dmix), (acc,) = row_epi([dh], [r[...] for r in ex])
        dx1_ref[...] = dx1
        dmix_ref[...] = dmix.astype(BF16)
        acc_ref[...] += acc

    tile = lambda w: pl.BlockSpec((tm, w), lambda i: (i, 0))
    whole = lambda a: pl.BlockSpec(a.shape, lambda i: (0, 0), pipeline_mode=pl.Buffered(1))
    ex_spec = lambda e: pl.BlockSpec(e.shape, lambda i: (0, 0)) if e.shape[0] == 1 else tile(e.shape[1])
    return pl.pallas_call(
        body, name="ffn_bwd", grid=(T // tm,),
        in_specs=[tile(D_MODEL), tile(D_FF), tile(D_FF), whole(w_fd), whole(w_fg), whole(w_fu)]
        + [ex_spec(e) for e in row_extras],
        out_specs=[tile(D_FF)] * 2 + [tile(D_MODEL)] * 2 + [pl.BlockSpec((8, D_MODEL), lambda i: (0, 0))],
        out_shape=[_sds((T, D_FF), BF16)] * 2 + [_sds((T, D_MODEL), F32), _sds((T, D_MODEL), BF16),
                                                 _sds((8, D_MODEL), F32)],
        compiler_params=_params(("arbitrary",)))(dff, gt, up, w_fd, w_fg, w_fu, *row_extras)


def _mm_tn(name, a, b, out_dtype, tk, tn, tt=MM_TT):
    T, Ka = a.shape
    N = b.shape[1]
    tt = min(tt, T)
    steps = T // tt

    def body(a_ref, b_ref, o_ref, acc):
        t = pl.program_id(2)

        @pl.when(t == 0)
        def _():
            acc[...] = jnp.zeros_like(acc)

        acc[...] += _dot(a_ref[...], b_ref[...], 0, 0)

        @pl.when(t == steps - 1)
        def _():
            o_ref[...] = acc[...].astype(o_ref.dtype)

    return pl.pallas_call(
        body, name=name, grid=(Ka // tk, N // tn, steps),
        in_specs=[pl.BlockSpec((tt, tk), lambda i, j, t: (t, i)),
                  pl.BlockSpec((tt, tn), lambda i, j, t: (t, j))],
        out_specs=pl.BlockSpec((tk, tn), lambda i, j, t: (i, j)),
        out_shape=_sds((Ka, N), out_dtype),
        scratch_shapes=[pltpu.VMEM((tk, tn), F32)],
        compiler_params=_params(("parallel", "parallel", "arbitrary")))(a, b)


def _first(ps, es):
    return (ps[0],)


def _rows(name, body, ins, out_shapes, T, tr=ROW_TILE):
    tr = min(tr, T)

    def spec(shape):
        if shape[0] == T:
            return pl.BlockSpec((tr,) + tuple(shape[1:]), lambda i: (i,) + (0,) * (len(shape) - 1))
        return pl.BlockSpec(tuple(shape), lambda i: (0,) * len(shape))

    return pl.pallas_call(
        body, name=name, grid=(T // tr,),
        in_specs=[spec(a.shape) for a in ins], out_specs=[spec(s.shape) for s in out_shapes],
        out_shape=out_shapes, compiler_params=_params(("arbitrary",)))(*ins)


def _rms(x):
    r = lax.rsqrt(jnp.mean(x * x, axis=-1, keepdims=True) + NORM_EPS)
    return x * r, r


def _rms_bwd(dxn, xn, r):
    return r * (dxn - xn * jnp.mean(dxn * xn, axis=-1, keepdims=True))


def _colsum(v):
    return jnp.sum(v, axis=0, keepdims=True)


def _proj_in(x, g, sc, sh, groups):
    T = x.shape[0]
    tm = min(MM_TM, T)
    ng = len(groups)

    def body(x_ref, g_ref, sc_ref, sh_ref, *rest):
        w_refs, h_ref, out_refs = rest[:ng], rest[ng], rest[ng + 1:]
        xn, _ = _rms(x_ref[...])
        h = (xn * g_ref[...] * (1.0 + sc_ref[...]) + sh_ref[...]).astype(BF16)
        h_ref[...] = h
        for w_ref, o_ref, (w, _, cn) in zip(w_refs, out_refs, groups):
            for j in range(w.shape[0] // cn):
                cols = slice(j * cn, (j + 1) * cn)
                o_ref[:, cols] = _dot(h, w_ref[cols, :], 1, 1).astype(o_ref.dtype)

    row = pl.BlockSpec((1, D_MODEL), lambda i: (0, 0))
    tile = lambda w: pl.BlockSpec((tm, w), lambda i: (i, 0))
    return pl.pallas_call(
        body, name="proj_in", grid=(T // tm,),
        in_specs=[tile(D_MODEL), row, row, row] + [
            pl.BlockSpec(w.shape, lambda i: (0, 0), pipeline_mode=pl.Buffered(1)) for w, _, _ in groups],
        out_specs=[tile(D_MODEL)] + [tile(w.shape[0]) for w, _, _ in groups],
        out_shape=[_sds((T, D_MODEL), BF16)] + [_sds((T, w.shape[0]), dt) for w, dt, _ in groups],
        compiler_params=_params(("parallel",)))(x, g, sc, sh, *[w for w, _, _ in groups])


def _acc_rows(rows):
    w = rows[0].shape[1]
    return jnp.concatenate(rows + [jnp.zeros((8 - len(rows), w), F32)], axis=0)


def _res_norm_rows(ps, es):
    mix = ps[0]
    x, gate, gp, g2, sc, sh = es
    mh, _ = _rms(mix)
    x1 = x + gate * (mh * gp)
    xn, _ = _rms(x1)
    return [mix, x1, xn * g2 * (1.0 + sc) + sh], []


def _final_loss_rows(ps, es):
    x1, tgt, gate, gp = es
    fh, r = _rms(ps[0])
    e = x1 + gate * (fh * gp) - tgt
    loss = 0.5 * jnp.sum(jnp.mean(e * e, axis=-1, keepdims=True))
    dy = e * (1.0 / D_MODEL)
    acc = _acc_rows([_colsum(dy * fh * gp), _colsum(dy * gate * fh)])
    return [dy, _rms_bwd(dy * gate * gp, fh, r)], [acc, jnp.full((1, 128), loss, F32)]


def _res_norm_bwd_rows(ps, es):
    dh = ps[0]
    x1, mix, dy, sc, gate, g2, gp = es
    xn, r1 = _rms(x1)
    rows = [_colsum(dh * xn * g2), _colsum(dh), _colsum(dh * (1.0 + sc) * xn)]
    dx1 = dy + _rms_bwd(dh * (1.0 + sc) * g2, xn, r1)
    mh, rm = _rms(mix)
    rows += [_colsum(dx1 * mh * gp), _colsum(dx1 * gate * mh)]
    return [dx1, _rms_bwd(dx1 * gate * gp, mh, rm)], [_acc_rows(rows)]


def _pre_norm_bwd_rows(ps, es):
    dh = ps[0]
    x, dx1, g, sc = es
    xn, r = _rms(x)
    rows = [_colsum(dh * xn * g), _colsum(dh), _colsum(dh * (1.0 + sc) * xn)]
    return [dx1 + _rms_bwd(dh * (1.0 + sc) * g, xn, r)], [_acc_rows(rows)]


def _rope_tables(pos_col, inv_freq):
    T = pos_col.shape[0]

    def body(p_ref, f_ref, c_ref, s_ref):
        ang = p_ref[...].astype(F32) * f_ref[...]
        lane = lax.broadcasted_iota(jnp.int32, ang.shape, 1)
        c_ref[...] = jnp.cos(ang)
        s_ref[...] = jnp.where(lane % HEAD_DIM < HEAD_DIM // 2, -1.0, 1.0) * jnp.sin(ang)

    return _rows("rope_tables", body, [pos_col, inv_freq],
                 [_sds((T, 128), F32), _sds((T, 128), F32)], T, tr=512)


def _swap_halves(t):
    W = t.shape[1]
    lane = lax.broadcasted_iota(jnp.int32, t.shape, 1)
    half = HEAD_DIM // 2
    return jnp.where(lane % HEAD_DIM < half, pltpu.roll(t, W - half, 1), pltpu.roll(t, half, 1))


def _widen(c, W):
    return c if W == 128 else jnp.concatenate([c] * (W // 128), axis=1)


def _rope(t, c, s):
    W = t.shape[1]
    return t * _widen(c, W) + _swap_halves(t) * _widen(s, W)


def _unrope(dy, c, s):
    W = dy.shape[1]
    return dy * _widen(c, W) + _swap_halves(dy * _widen(s, W))


def _attn_mask(n):
    qi = lax.broadcasted_iota(jnp.int32, (ATTN_BLK, 2 * ATTN_BLK), 0)
    kj = lax.broadcasted_iota(jnp.int32, (ATTN_BLK, 2 * ATTN_BLK), 1)
    rel = kj - ATTN_BLK
    return (rel <= qi) & (qi - rel < WINDOW) & ((n > 0) | (kj >= ATTN_BLK))


def _attn_load(cur, prv, cc, sc, cp, sp):
    x, xp = cur[...], prv[...]
    q = _rope(x[:, :512], cc[...], sc[...]) * (HEAD_DIM ** -0.5)
    k = jnp.concatenate([_rope(xp[:, 512:640], cp[...], sp[...]),
                         _rope(x[:, 512:640], cc[...], sc[...])], axis=0)
    v = jnp.concatenate([xp[:, 640:768], x[:, 640:768]], axis=0)
    return q, k, v


ROLLED = tuple(h for h in range(N_Q_HEADS) if h % 2 != h // (N_Q_HEADS // N_KV_HEADS))


def _pair_heads(t):
    half = lax.broadcasted_iota(jnp.int32, (ATTN_BLK, 128), 1) // HEAD_DIM
    return jnp.stack([jnp.where(half == h % 2, t[:, 128 * (h // 2):128 * (h // 2) + 128], 0.0)
                      for h in range(N_Q_HEADS)])


def _kv_heads(t):
    half = lax.broadcasted_iota(jnp.int32, t.shape, 1) // HEAD_DIM
    tr = pltpu.roll(t, HEAD_DIM, 1)
    return jnp.stack([jnp.where(half == h % 2, tr if h in ROLLED else t, 0.0)
                      for h in range(N_Q_HEADS)])


def _sink_column(snk):
    return jnp.stack([jnp.full((1, 1), snk[0, h], F32) for h in range(N_Q_HEADS)])


def _attn_probs(qh, kh, mask, sink):
    s = jnp.where(mask, _bdot(qh, kh, 2, 2), NEG_INF)
    m = jnp.maximum(jnp.max(s, axis=-1, keepdims=True), sink)
    p = jnp.exp(s - m)
    es = jnp.exp(sink - m)
    rl = 1.0 / (jnp.sum(p, axis=-1, keepdims=True) + es)
    return p, es, rl


def _attn_specs(order):
    blk = lambda w: pl.BlockSpec((ATTN_BLK, w), lambda s: (order(s), 0))
    prv = lambda w: pl.BlockSpec((ATTN_BLK, w), lambda s: (jnp.maximum(order(s) - 1, 0), 0))
    return [blk(768), prv(768), blk(128), blk(128), prv(128), prv(128),
            pl.BlockSpec(memory_space=pltpu.SMEM)]


def _attn_fwd(pa, cos, sin, sinks):
    T = pa.shape[0]
    nb = T // ATTN_BLK

    def body(cur, prv, cc, sc, cp, sp, snk, y_ref):
        n = pl.program_id(0)
        q, k, v = _attn_load(cur, prv, cc, sc, cp, sp)
        qh, kh, vh = _pair_heads(q).astype(BF16), _kv_heads(k).astype(BF16), _kv_heads(v).astype(BF16)
        p, _, rl = _attn_probs(qh, kh, _attn_mask(n), _sink_column(snk))
        o = _bdot(p.astype(BF16), vh, 2, 1) * rl
        for pair in range(N_Q_HEADS // 2):
            y_ref[:, 128 * pair:128 * pair + 128] = (o[2 * pair] + o[2 * pair + 1]).astype(BF16)

    return pl.pallas_call(
        body, name="attn_fwd", grid=(nb,), in_specs=_attn_specs(lambda s: s),
        out_specs=pl.BlockSpec((ATTN_BLK, 512), lambda n: (n, 0)),
        out_shape=_sds((T, 512), BF16), compiler_params=_params(("parallel",)))(
            pa, pa, cos, sin, cos, sin, sinks)


def _attn_bwd(pa, cos, sin, sinks, dy):
    T = pa.shape[0]
    nb = T // ATTN_BLK
    rev = lambda s: nb - 1 - s

    def body(cur, prv, cc, sc, cp, sp, snk, dy_ref, dq_ref, dkv_ref, dsink_ref, carry):
        n = rev(pl.program_id(0))

        @pl.when(pl.program_id(0) == 0)
        def _():
            dsink_ref[...] = jnp.zeros_like(dsink_ref)
            carry[...] = jnp.zeros_like(carry)

        q, k, v = _attn_load(cur, prv, cc, sc, cp, sp)
        qh, kh, vh = _pair_heads(q).astype(BF16), _kv_heads(k).astype(BF16), _kv_heads(v).astype(BF16)
        p, es, rl = _attn_probs(qh, kh, _attn_mask(n), _sink_column(snk))
        pn = p * rl
        do = _pair_heads(dy_ref[...]).astype(BF16)
        dp = _bdot(do, vh, 2, 2)
        delta = jnp.sum(pn * dp, axis=-1, keepdims=True)
        ds = (pn * (dp - delta)).astype(BF16)
        dsink = es * rl * delta
        dq = _bdot(ds, kh, 2, 1) * (HEAD_DIM ** -0.5)
        dkh = _bdot_rows(ds, qh)
        dvh = _bdot_rows(pn.astype(BF16), do)

        def fold(t):
            same = [t[h] for h in range(N_Q_HEADS) if h not in ROLLED]
            moved = [t[h] for h in ROLLED]
            return sum(same[1:], same[0]) + pltpu.roll(sum(moved[1:], moved[0]), HEAD_DIM, 1)

        dk, dv = fold(dkh), fold(dvh)
        for h in range(N_Q_HEADS):
            dsink_ref[h:h + 1, :] += -jnp.sum(dsink[h])
        for pair in range(N_Q_HEADS // 2):
            dq_ref[:, 128 * pair:128 * pair + 128] = _unrope(
                dq[2 * pair] + dq[2 * pair + 1], cc[...], sc[...]).astype(BF16)
        dkv_ref[:, 0:128] = _unrope(dk[ATTN_BLK:] + carry[:, 0:128], cc[...], sc[...]).astype(BF16)
        dkv_ref[:, 128:256] = (dv[ATTN_BLK:] + carry[:, 128:256]).astype(BF16)
        carry[:, 0:128] = dk[:ATTN_BLK]
        carry[:, 128:256] = dv[:ATTN_BLK]

    blk = lambda w: pl.BlockSpec((ATTN_BLK, w), lambda s: (rev(s), 0))
    return pl.pallas_call(
        body, name="attn_bwd", grid=(nb,), in_specs=_attn_specs(rev) + [blk(512)],
        out_specs=[blk(512), blk(256), pl.BlockSpec((8, 128), lambda s: (0, 0))],
        out_shape=[_sds((T, 512), BF16), _sds((T, 256), BF16), _sds((8, 128), F32)],
        scratch_shapes=[pltpu.VMEM((ATTN_BLK, 256), F32)],
        compiler_params=_params(("arbitrary",)))(pa, pa, cos, sin, cos, sin, sinks, dy)


CONV_COLS = 2 * MLSTM_HEADS * MLSTM_HEAD_DIM


def _conv_pre(cur_ref, halo_ref, w_ref, b_ref, i, tr):
    xx = jnp.concatenate([jnp.where(i > 0, halo_ref[...], 0.0), cur_ref[...]], axis=0)
    taps = [(pltpu.roll(xx, CONV_WIDTH - 1 - j, 0) if j < CONV_WIDTH - 1 else xx)[8:8 + tr]
            for j in range(CONV_WIDTH)]
    pre = b_ref[...]
    for j in range(CONV_WIDTH):
        pre = pre + taps[j] * w_ref[j:j + 1, :]
    return pre, taps


def _conv_specs(T, tr):
    return [pl.BlockSpec((tr, CONV_COLS), lambda i: (i, 0)),
            pl.BlockSpec((8, CONV_COLS), lambda i: (jnp.maximum(i * (tr // 8) - 1, 0), 0)),
            pl.BlockSpec((CONV_WIDTH, CONV_COLS), lambda i: (0, 0)),
            pl.BlockSpec((1, CONV_COLS), lambda i: (0, 0))]


def _conv_fwd(pm, w, b):
    T = pm.shape[0]
    tr = min(ROW_TILE, T)

    def body(cur_ref, halo_ref, w_ref, b_ref, o_ref):
        pre, _ = _conv_pre(cur_ref, halo_ref, w_ref, b_ref, pl.program_id(0), tr)
        o_ref[...] = pre * _sigmoid(pre)

    return pl.pallas_call(
        body, name="conv_fwd", grid=(T // tr,), in_specs=_conv_specs(T, tr),
        out_specs=pl.BlockSpec((tr, CONV_COLS), lambda i: (i, 0)),
        out_shape=_sds((T, CONV_COLS), F32), compiler_params=_params(("parallel",)))(pm, pm, w, b)


def _conv_bwd(pqk, w, b, dqk):
    T = pqk.shape[0]
    tr = min(ROW_TILE, T)
    nt = T // tr

    def body(cur_ref, prev_ref, next_ref, w_ref, b_ref, d_ref, dnext_ref, du_ref, acc_ref):
        i = pl.program_id(0)

        @pl.when(i == 0)
        def _():
            acc_ref[...] = jnp.zeros_like(acc_ref)

        last = i == nt - 1
        xx = jnp.concatenate([jnp.where(i > 0, prev_ref[...], 0.0), cur_ref[...],
                              jnp.where(last, 0.0, next_ref[...])], axis=0)
        taps = [(pltpu.roll(xx, CONV_WIDTH - 1 - j, 0) if j < CONV_WIDTH - 1 else xx)[8:16 + tr]
                for j in range(CONV_WIDTH)]
        pre = b_ref[...]
        for j in range(CONV_WIDTH):
            pre = pre + taps[j] * w_ref[j:j + 1, :]
        sg = _sigmoid(pre)
        dd = jnp.concatenate([d_ref[...], jnp.where(last, 0.0, dnext_ref[...])], axis=0)
        dpre = dd * (sg * (1.0 + pre * (1.0 - sg)))
        for j in range(CONV_WIDTH):
            acc_ref[j:j + 1, :] += _colsum(dpre[:tr] * taps[j][:tr])
        acc_ref[CONV_WIDTH:CONV_WIDTH + 1, :] += _colsum(dpre[:tr])
        du = dpre[:tr] * w_ref[CONV_WIDTH - 1:CONV_WIDTH, :]
        for j in range(CONV_WIDTH - 1):
            k = CONV_WIDTH - 1 - j
            du = du + pltpu.roll(dpre, tr + 8 - k, 0)[:tr] * w_ref[j:j + 1, :]
        du_ref[...] = du.astype(BF16)

    tile = pl.BlockSpec((tr, CONV_COLS), lambda i: (i, 0))
    after = pl.BlockSpec((8, CONV_COLS), lambda i: (jnp.minimum((i + 1) * (tr // 8), T // 8 - 1), 0))
    before = pl.BlockSpec((8, CONV_COLS), lambda i: (jnp.maximum(i * (tr // 8) - 1, 0), 0))
    return pl.pallas_call(
        body, name="conv_bwd", grid=(nt,),
        in_specs=[tile, before, after, pl.BlockSpec((CONV_WIDTH, CONV_COLS), lambda i: (0, 0)),
                  pl.BlockSpec((1, CONV_COLS), lambda i: (0, 0)), tile, after],
        out_specs=[tile, pl.BlockSpec((8, CONV_COLS), lambda i: (0, 0))],
        out_shape=[_sds((T, CONV_COLS), BF16), _sds((8, CONV_COLS), F32)],
        compiler_params=_params(("arbitrary",)))(pqk, pqk, pqk, w, b, dqk, dqk)


def _log_sigmoid(x):
    return jnp.minimum(x, 0.0) - jnp.log1p(jnp.exp(-jnp.abs(x)))


def _chunk_cumsum(x, axis):
    idx = lax.broadcasted_iota(jnp.int32, x.shape, axis) % MLSTM_CHUNK
    k = 1
    while k < MLSTM_CHUNK:
        x = x + jnp.where(idx >= k, pltpu.roll(x, k, axis), 0.0)
        k *= 2
    return x


def _chunk_rev_cumsum(x, axis):
    n = x.shape[axis]
    idx = lax.broadcasted_iota(jnp.int32, x.shape, axis) % MLSTM_CHUNK
    k = 1
    while k < MLSTM_CHUNK:
        x = x + jnp.where(idx < MLSTM_CHUNK - k, pltpu.roll(x, n - k, axis), 0.0)
        k *= 2
    return x


def _mlstm_gates(gc_ref, bc_ref, gr_ref, br_ref):
    gc = gc_ref[...] + bc_ref[...]
    gr = gr_ref[...] + br_ref[...]
    return gc, _chunk_cumsum(_log_sigmoid(gc), 0), gr, _chunk_cumsum(_log_sigmoid(gr), 1)


def _heads(ref, base=0):
    D = MLSTM_HEAD_DIM
    return jnp.stack([ref[:, base + D * h:base + D * h + D] for h in range(MLSTM_HEADS)])


def _mlstm_inputs(q_ref, k_ref, v_ref, gc, bc, gr, br):
    H = MLSTM_HEADS
    q, v = _heads(q_ref), _heads(v_ref)
    ks = _heads(k_ref) * (MLSTM_HEAD_DIM ** -0.5)
    return dict(
        q=q, ks=ks, qb=q.astype(BF16), kb=ks.astype(BF16), vb=v.astype(BF16),
        b_col=jnp.stack([bc[:, H + h:H + h + 1] for h in range(H)]),
        i_col=jnp.stack([gc[:, h:h + 1] for h in range(H)]),
        b_row=jnp.stack([br[H + h:H + h + 1, :] for h in range(H)]),
        i_row=jnp.stack([gr[h:h + 1, :] for h in range(H)]))


def _mlstm_head(f, c_prev, n_prev, m_prev):
    L = MLSTM_CHUNK
    q, qb = f["q"], f["qb"]
    t = lax.broadcasted_iota(jnp.int32, (1, 2 * L, 2 * L), 1)
    s = lax.broadcasted_iota(jnp.int32, (1, 2 * L, 2 * L), 2)
    mask = (t // L == s // L) & (s <= t)
    d = jnp.where(mask, f["b_col"] - f["b_row"] + f["i_row"], NEG_INF)
    row = lax.broadcasted_iota(jnp.int32, (1, 2 * L, 1), 1)
    inter = f["b_col"] + jnp.where(row < L, m_prev[0], m_prev[1])
    m_t = jnp.maximum(inter, jnp.max(d, axis=-1, keepdims=True))
    w_intra = jnp.exp(d - m_t)
    w_inter = jnp.exp(inter - m_t)
    sc = _bdot(qb, f["kb"], 2, 2) * w_intra
    qc = jnp.concatenate([_bdot(qb[:, :L], c_prev[0].astype(BF16), 2, 1),
                          _bdot(qb[:, L:], c_prev[1].astype(BF16), 2, 1)], axis=1)
    qn = jnp.concatenate([jnp.sum(q[:, :L] * n_prev[0], axis=-1, keepdims=True),
                          jnp.sum(q[:, L:] * n_prev[1], axis=-1, keepdims=True)], axis=1)
    num = _bdot(sc.astype(BF16), f["vb"], 2, 1) + w_inter * qc
    den = jnp.sum(sc, axis=-1, keepdims=True) + w_inter * qn
    return dict(f, w_intra=w_intra, w_inter=w_inter, sc=sc, qc=qc, qn=qn, num=num, den=den,
                floor=jnp.exp(-m_t))


def _mlstm_update(f, ch, c, n, m):
    L = MLSTM_CHUNK
    rows = slice(L * ch, L * ch + L)
    b_col = f["b_col"][:, rows]
    g_last = b_col[:, L - 1:L]
    a_col = g_last - b_col + f["i_col"][:, rows]
    m_new = jnp.maximum(g_last + m, jnp.max(a_col, axis=1, keepdims=True))
    decay = jnp.exp(g_last + m - m_new)
    e_a = jnp.exp(a_col - m_new)
    kw = f["ks"][:, rows] * e_a
    c_new = decay * c + _bdot_rows(kw.astype(BF16), f["vb"][:, rows])
    n_new = decay * n + jnp.sum(kw, axis=1, keepdims=True)
    return c_new, n_new, m_new, decay, e_a, kw


def _mlstm_specs(T, order):
    blk = lambda w, col: pl.BlockSpec((STEP_ROWS, w), lambda s: (order(s), col))
    return [blk(512, 0), blk(512, 1), blk(512, 0), blk(128, 0),
            pl.BlockSpec((1, 128), lambda s: (0, 0)),
            pl.BlockSpec((8, STEP_ROWS), lambda s: (0, order(s))),
            pl.BlockSpec((8, 128), lambda s: (0, 0))]


def _lanes(m):
    return jnp.broadcast_to(m, m.shape[:-1] + (128,))


def _mlstm_fwd(qk, pm, gcol, bcol, grow, brow):
    T = qk.shape[0]
    steps = T // STEP_ROWS
    H, D = MLSTM_HEADS, MLSTM_HEAD_DIM

    def body(q_ref, k_ref, v_ref, gc_ref, bc_ref, gr_ref, br_ref, h_ref, cs_ref, ns_ref, ms_ref,
             c_st, n_st, m_st):
        @pl.when(pl.program_id(0) == 0)
        def _():
            c_st[...] = jnp.zeros_like(c_st)
            n_st[...] = jnp.zeros_like(n_st)
            m_st[...] = jnp.zeros_like(m_st)

        f = _mlstm_inputs(q_ref, k_ref, v_ref, *_mlstm_gates(gc_ref, bc_ref, gr_ref, br_ref))
        c0, n0, m0 = c_st[...], n_st[...], m_st[:, :, 0:1]
        c1, n1, m1, _, _, _ = _mlstm_update(f, 0, c0, n0, m0)
        c2, n2, m2, _, _, _ = _mlstm_update(f, 1, c1, n1, m1)
        f = _mlstm_head(f, (c0, c1), (n0, n1), (m0, m1))
        h = f["num"] / jnp.maximum(jnp.abs(f["den"]), f["floor"])
        for hd in range(H):
            h_ref[:, D * hd:D * hd + D] = h[hd]
        cs_ref[0], cs_ref[1] = c0, c1
        ns_ref[0], ns_ref[1] = n0, n1
        ms_ref[0], ms_ref[1] = _lanes(m0), _lanes(m1)
        c_st[...], n_st[...], m_st[...] = c2, n2, _lanes(m2)

    vec = pl.BlockSpec((2, H, 1, 128), lambda s: (s, 0, 0, 0))
    return pl.pallas_call(
        body, name="mlstm_fwd", grid=(steps,), in_specs=_mlstm_specs(T, lambda s: s),
        out_specs=[pl.BlockSpec((STEP_ROWS, 512), lambda s: (s, 0)),
                   pl.BlockSpec((2, H, 128, 128), lambda s: (s, 0, 0, 0)), vec, vec],
        out_shape=[_sds((T, 512), F32), _sds((2 * steps, H, 128, 128), F32),
                   _sds((2 * steps, H, 1, 128), F32), _sds((2 * steps, H, 1, 128), F32)],
        scratch_shapes=[pltpu.VMEM((H, 128, 128), F32), pltpu.VMEM((H, 1, 128), F32),
                        pltpu.VMEM((H, 1, 128), F32)],
        compiler_params=_params(("arbitrary",)))(qk, qk, pm, gcol, bcol, grow, brow)


def _mlstm_bwd(qk, pm, gcol, bcol, grow, brow, cs, ns, ms, dh):
    T = qk.shape[0]
    steps = T // STEP_ROWS
    H, L, D = MLSTM_HEADS, MLSTM_CHUNK, MLSTM_HEAD_DIM
    rev = lambda s: steps - 1 - s

    def body(q_ref, k_ref, v_ref, gc_ref, bc_ref, gr_ref, br_ref, cs_ref, ns_ref, ms_ref, dh_ref,
             dqk_ref, dv_ref, dgc_ref, dgr_ref, dc_st, dn_st):
        @pl.when(pl.program_id(0) == 0)
        def _():
            dc_st[...] = jnp.zeros_like(dc_st)
            dn_st[...] = jnp.zeros_like(dn_st)

        f = _mlstm_inputs(q_ref, k_ref, v_ref, *_mlstm_gates(gc_ref, bc_ref, gr_ref, br_ref))
        c_prev = (cs_ref[0], cs_ref[1])
        n_prev = (ns_ref[0], ns_ref[1])
        m_prev = (ms_ref[0, :, :, 0:1], ms_ref[1, :, :, 0:1])
        f = _mlstm_head(f, c_prev, n_prev, m_prev)
        big = jnp.abs(f["den"]) > f["floor"]
        rden = 1.0 / jnp.where(big, jnp.abs(f["den"]), f["floor"])
        dnum = _heads(dh_ref) * rden
        hdh = jnp.sum(f["num"] * dnum, axis=-1, keepdims=True)
        dden = jnp.where(big, -hdh * rden * jnp.sign(f["den"]), 0.0)
        dnum_b = dnum.astype(BF16)
        dsc = _bdot(dnum_b, f["vb"], 2, 2) + dden
        g = dsc * f["sc"]
        dv = _bdot_rows(f["sc"].astype(BF16), dnum_b)
        dqk_ = (dsc * f["w_intra"]).astype(BF16)
        dq = _bdot(dqk_, f["kb"], 2, 1)
        dks = _bdot_rows(dqk_, f["qb"])
        wdn = f["w_inter"] * dnum
        wdn_b = wdn.astype(BF16)
        wdd = f["w_inter"] * dden
        u = jnp.sum(f["qc"] * wdn, axis=-1, keepdims=True) + wdd * f["qn"]
        dks_s, dv_s, z_s, dg_s = [None, None], [None, None], [None, None], [None, None]
        dcn, dnn = dc_st[...], dn_st[...]
        for ch in (1, 0):
            rows = slice(L * ch, L * ch + L)
            _, _, _, decay, e_a, kw = _mlstm_update(f, ch, c_prev[ch], n_prev[ch], m_prev[ch])
            dcn_b = dcn.astype(BF16)
            dkw = _bdot(f["vb"][:, rows], dcn_b, 2, 2) + dnn
            dks_s[ch] = e_a * dkw
            dv_s[ch] = _bdot(kw.astype(BF16), dcn_b, 2, 1)
            z_s[ch] = e_a * jnp.sum(f["ks"][:, rows] * dkw, axis=-1, keepdims=True)
            dg_s[ch] = jnp.sum(z_s[ch], axis=1, keepdims=True) + decay * (
                jnp.sum(c_prev[ch] * dcn, axis=(1, 2), keepdims=True)
                + jnp.sum(n_prev[ch] * dnn, axis=(1, 2), keepdims=True))
            dcn = decay * dcn + _bdot_rows(f["qb"][:, rows], wdn_b[:, rows])
            dnn = decay * dnn + jnp.sum(wdd[:, rows] * f["q"][:, rows], axis=1, keepdims=True)
        dc_st[...], dn_st[...] = dcn, dnn
        dq = dq + jnp.concatenate(
            [_bdot(wdn_b[:, :L], c_prev[0].astype(BF16), 2, 2) + wdd[:, :L] * n_prev[0],
             _bdot(wdn_b[:, L:], c_prev[1].astype(BF16), 2, 2) + wdd[:, L:] * n_prev[1]], axis=1)
        dks = (dks + jnp.concatenate(dks_s, axis=1)) * (D ** -0.5)
        dv = dv + jnp.concatenate(dv_s, axis=1)
        z = jnp.concatenate(z_s, axis=1)
        row = lax.broadcasted_iota(jnp.int32, (1, STEP_ROWS, 1), 1)
        dg_col = jnp.where(row == L - 1, dg_s[0], 0.0) + jnp.where(row == 2 * L - 1, dg_s[1], 0.0)
        db_col = jnp.sum(g, axis=-1, keepdims=True) + u - z + dg_col
        g_row = jnp.sum(g, axis=1, keepdims=True)
        lane = lax.broadcasted_iota(jnp.int32, (STEP_ROWS, 128), 1)
        sub = lax.broadcasted_iota(jnp.int32, (8, STEP_ROWS), 0)
        dgc = jnp.zeros((STEP_ROWS, 128), F32)
        dgr = jnp.zeros((8, STEP_ROWS), F32)
        for hd in range(H):
            dgc = dgc + jnp.where(lane == hd, z[hd], 0.0) + jnp.where(lane == H + hd, db_col[hd], 0.0)
            dgr = dgr + jnp.where(sub == hd, g_row[hd], 0.0) - jnp.where(sub == H + hd, g_row[hd], 0.0)
            dqk_ref[:, D * hd:D * hd + D] = dq[hd]
            dqk_ref[:, H * D + D * hd:H * D + D * hd + D] = dks[hd]
            dv_ref[:, D * hd:D * hd + D] = dv[hd].astype(BF16)
        dgc_ref[...] = dgc
        dgr_ref[...] = dgr

    return pl.pallas_call(
        body, name="mlstm_bwd", grid=(steps,),
        in_specs=_mlstm_specs(T, rev) + [
            pl.BlockSpec((2, H, 128, 128), lambda s: (rev(s), 0, 0, 0)),
            pl.BlockSpec((2, H, 1, 128), lambda s: (rev(s), 0, 0, 0)),
            pl.BlockSpec((2, H, 1, 128), lambda s: (rev(s), 0, 0, 0)),
            pl.BlockSpec((STEP_ROWS, 512), lambda s: (rev(s), 0))],
        out_specs=[pl.BlockSpec((STEP_ROWS, 1024), lambda s: (rev(s), 0)),
                   pl.BlockSpec((STEP_ROWS, 512), lambda s: (rev(s), 0)),
                   pl.BlockSpec((STEP_ROWS, 128), lambda s: (rev(s), 0)),
                   pl.BlockSpec((8, STEP_ROWS), lambda s: (0, rev(s)))],
        out_shape=[_sds((T, 1024), F32), _sds((T, 512), BF16), _sds((T, 128), F32), _sds((8, T), F32)],
        scratch_shapes=[pltpu.VMEM((H, 128, 128), F32), pltpu.VMEM((H, 1, 128), F32)],
        compiler_params=_params(("arbitrary",)))(qk, qk, pm, gcol, bcol, grow, brow, cs, ns, ms, dh)


def _rows_to_lanes(x):
    eye = (lax.broadcasted_iota(jnp.int32, (8, 128), 0)
           == lax.broadcasted_iota(jnp.int32, (8, 128), 1)).astype(BF16)
    out, rest = None, x
    for _ in range(3):
        piece = rest.astype(BF16)
        rest = rest - piece.astype(F32)
        t = _dot(piece, eye, 0, 0)
        out = t if out is None else out + t
    return out


def _gate_bwd(dgc, dgr, gcol, bcol):
    T = dgc.shape[0]
    tr = min(ROW_TILE, T)

    def body(a_ref, b_ref, g_ref, bias_ref, o_ref, acc_ref):
        i = pl.program_id(0)

        @pl.when(i == 0)
        def _():
            acc_ref[...] = jnp.zeros_like(acc_ref)

        d = a_ref[...] + _rows_to_lanes(b_ref[:, pl.ds(pl.multiple_of(i * tr, 128), tr)])
        lane = lax.broadcasted_iota(jnp.int32, d.shape, 1)
        is_f = (lane >= MLSTM_HEADS) & (lane < 2 * MLSTM_HEADS)
        dlogf = _chunk_rev_cumsum(jnp.where(is_f, d, 0.0), 0)
        out = jnp.where(is_f, dlogf * _sigmoid(-(g_ref[...] + bias_ref[...])), d)
        o_ref[...] = out.astype(BF16)
        acc_ref[0:1, :] += _colsum(out)

    return _rows("gate_bwd", body, [dgc, dgr, gcol, bcol],
                 [_sds((T, 128), BF16), _sds((8, 128), F32)], T, tr=tr)


def _head_norm(h, mu_axis=-1):
    mu = jnp.mean(h, axis=-1, keepdims=True)
    hc = h - mu
    r = lax.rsqrt(jnp.mean(hc * hc, axis=-1, keepdims=True) + NORM_EPS)
    return hc * r, r


def _mlstm_out(hm, pm, w):
    T = hm.shape[0]
    D = MLSTM_HEAD_DIM

    def body(h_ref, o_ref, w_ref, y_ref):
        for hd in range(MLSTM_HEADS):
            cols = slice(D * hd, D * hd + D)
            hn, _ = _head_norm(h_ref[:, cols])
            y_ref[:, cols] = (_sigmoid(o_ref[:, cols].astype(F32)) * hn * w_ref[:, cols]).astype(BF16)

    tr = min(ROW_TILE, T)
    return pl.pallas_call(
        body, name="mlstm_out", grid=(T // tr,),
        in_specs=[pl.BlockSpec((tr, 512), lambda i: (i, 0)), pl.BlockSpec((tr, 512), lambda i: (i, 1)),
                  pl.BlockSpec((1, 512), lambda i: (0, 0))],
        out_specs=pl.BlockSpec((tr, 512), lambda i: (i, 0)), out_shape=_sds((T, 512), BF16),
        compiler_params=_params(("parallel",)))(hm, pm, w)


def _mlstm_out_bwd_rows(ps, es):
    hm, vo, w_all = es
    D, width = MLSTM_HEAD_DIM, MLSTM_HEADS * MLSTM_HEAD_DIM
    dhs, dos, dws = [], [], []
    for hd in range(MLSTM_HEADS):
        cols = slice(D * hd, D * hd + D)
        hn, r = _head_norm(hm[:, cols])
        sg = _sigmoid(vo[:, width + D * hd:width + D * hd + D].astype(F32))
        dy, w = ps[0][:, cols], w_all[:, cols]
        dos.append(dy * hn * w * sg * (1.0 - sg))
        dyn = dy * sg
        dws.append(_colsum(dyn * hn))
        dhn = dyn * w
        dhs.append(r * (dhn - jnp.mean(dhn, axis=-1, keepdims=True)
                        - hn * jnp.mean(dhn * hn, axis=-1, keepdims=True)))
    cat = lambda parts: jnp.concatenate(parts, axis=1)
    return [cat(dhs), cat(dos)], [_acc_rows([cat(dws)])]


ADAM_TILE_ELEMS = 256 * 1024


def _adamw(name, w, g, m, v):
    R, C = w.shape
    fits = [t for t in range(8, R + 1, 8) if R % t == 0 and t * C <= ADAM_TILE_ELEMS]
    if fits or R * C <= ADAM_TILE_ELEMS:
        tr = fits[-1] if fits else R
        spec, grid = pl.BlockSpec((tr, C), lambda i: (i, 0)), (R // tr,)
    else:
        spec, grid = pl.BlockSpec((R, 128), lambda i: (0, i)), (C // 128,)
    c1 = 1.0 - ADAM_B1 ** ADAM_STEP
    c2 = 1.0 - ADAM_B2 ** ADAM_STEP

    def body(w_ref, g_ref, m_ref, v_ref, d_ref, mo_ref, vo_ref):
        g = g_ref[...]
        m = ADAM_B1 * m_ref[...] + (1.0 - ADAM_B1) * g
        v = ADAM_B2 * v_ref[...] + (1.0 - ADAM_B2) * (g * g)
        mo_ref[...] = m
        vo_ref[...] = v
        d_ref[...] = -ADAM_LR * ((m / c1) / (jnp.sqrt(v / c2) + ADAM_EPS) + ADAM_WD * w_ref[...])

    return pl.pallas_call(
        body, name=name, grid=grid, in_specs=[spec] * 4, out_specs=[spec] * 3,
        out_shape=[_sds((R, C), F32)] * 3, compiler_params=_params(("parallel",)))(w, g, m, v)


def _place():
    return lax.axis_index("x"), lax.axis_index("y"), lax.axis_index("c")


def _all_gather8(name, blk, space):
    m, n = blk.shape

    def body(x_ref, out_ref, send_sems, recv_sems, local_sem):
        x, y, c = _place()
        me, sibling = (x, y, c), (x, y, 1 - c)
        chips = [(1 - x, y), (x, 1 - y), (1 - x, 1 - y)]

        def rows(px, py, pc):
            return out_ref.at[pl.ds((4 * px + 2 * py + pc) * m, m), :]

        def copy(k, block, to, src=None):
            return pltpu.make_async_remote_copy(
                src_ref=rows(*block) if src is None else src, dst_ref=rows(*block),
                send_sem=send_sems.at[k], recv_sem=recv_sems.at[k],
                device_id=to, device_id_type=MESH)

        mine = pltpu.make_async_copy(x_ref, rows(*me), local_sem)
        mine.start()
        first = [copy(0, me, sibling, src=x_ref)]
        first += [copy(1 + j, me, (*chip, c), src=x_ref) for j, chip in enumerate(chips)]
        for cp in first:
            cp.start()
        passed = [copy(4 + j, (*chip, c), sibling) for j, chip in enumerate(chips)]
        for j, chip in enumerate(chips):
            copy(1 + j, (*chip, c), me).wait_recv()
            passed[j].start()
        copy(0, sibling, me).wait_recv()
        for j, chip in enumerate(chips):
            copy(4 + j, (*chip, 1 - c), me).wait_recv()
        for cp in first + passed:
            cp.wait_send()
        mine.wait()

    return pl.pallas_call(
        body, name=name, out_shape=_sds((8 * m, n), blk.dtype),
        in_specs=[pl.BlockSpec(memory_space=space)], out_specs=pl.BlockSpec(memory_space=space),
        scratch_shapes=[pltpu.SemaphoreType.DMA((7,)), pltpu.SemaphoreType.DMA((7,)),
                        pltpu.SemaphoreType.DMA],
        compiler_params=pltpu.CompilerParams(vmem_limit_bytes=VMEM_LIMIT))(blk)


def _hbm_specs(n):
    return [pl.BlockSpec(memory_space=pl.ANY)] * n


def _swap_halves_sibling(name, srcs):
    nw = len(srcs)

    def body(*refs):
        src_refs, dst_refs, send_sems, recv_sems = refs[:nw], refs[nw:2 * nw], refs[2 * nw], refs[2 * nw + 1]
        x, y, c = _place()
        cps = [pltpu.make_async_remote_copy(
            src_ref=src_refs[w].at[pl.ds(0, 4), 1 - c], dst_ref=dst_refs[w],
            send_sem=send_sems.at[w], recv_sem=recv_sems.at[w], device_id=(x, y, 1 - c),
            device_id_type=MESH) for w in range(nw)]
        for cp in cps:
            cp.start()
        for cp in cps:
            cp.wait()

    return pl.pallas_call(
        body, name=name, out_shape=[_sds(s.shape[:1] + s.shape[2:], s.dtype) for s in srcs],
        in_specs=_hbm_specs(nw), out_specs=_hbm_specs(nw),
        scratch_shapes=[pltpu.SemaphoreType.DMA((nw,)), pltpu.SemaphoreType.DMA((nw,))])(*srcs)


def _split_start(name, srcs, lands, copies, per_array, after):
    nw = len(srcs)

    def body(*refs):
        send_sems, recv_sems, token = refs[2 * nw + 1], refs[2 * nw + 2], refs[-1]
        for w in range(nw):
            for k, (s, d, dev) in enumerate(copies(refs[w], refs[nw + w], *_place())):
                pltpu.make_async_remote_copy(
                    src_ref=s, dst_ref=d, send_sem=send_sems.at[w * per_array + k],
                    recv_sem=recv_sems.at[w * per_array + k], device_id=dev, device_id_type=MESH).start()
        token[...] = jnp.zeros_like(token)

    hbm, sem = pl.BlockSpec(memory_space=pltpu.HBM), pl.BlockSpec(memory_space=pltpu.SEMAPHORE)
    arrays = list(srcs) + list(lands)
    out = pl.pallas_call(
        body, name=name,
        out_shape=(pltpu.SemaphoreType.DMA((nw * per_array,)), pltpu.SemaphoreType.DMA((nw * per_array,)),
                   *[pltpu.HBM(a.shape, a.dtype) for a in arrays], _sds((8, 128), F32)),
        in_specs=[hbm] * (2 * nw) + [pl.BlockSpec(memory_space=pl.ANY)],
        out_specs=(sem, sem, *[hbm] * (2 * nw), pl.BlockSpec(memory_space=pltpu.VMEM)),
        input_output_aliases={i: 2 + i for i in range(2 * nw)},
        compiler_params=pltpu.CompilerParams(has_side_effects=pltpu.SideEffectType.DATAFLOW_SIDE_EFFECTING))(
            *[pltpu.with_memory_space_constraint(a, pltpu.HBM) for a in arrays], after)
    return out[0], out[1], out[2:2 + nw], out[2 + nw:2 + 2 * nw], out[-1]


def _split_wait(name, started, after, waits, per_array):
    send_sems, recv_sems, srcs, lands, _ = started
    nw = len(srcs)

    def body(*refs):
        send_sems, recv_sems = refs[2 * nw], refs[2 * nw + 1]
        x, y, c = _place()
        for w in range(nw):
            for k, (s, d) in enumerate(waits(refs[w], refs[nw + w], x, y, c)):
                cp = pltpu.make_async_remote_copy(
                    src_ref=s, dst_ref=d, send_sem=send_sems.at[w * per_array + k],
                    recv_sem=recv_sems.at[w * per_array + k], device_id=(x, y, 1 - c),
                    device_id_type=MESH)
                cp.wait_send()
                cp.wait_recv()

    hbm, sem = pl.BlockSpec(memory_space=pltpu.HBM), pl.BlockSpec(memory_space=pltpu.SEMAPHORE)
    arrays = list(srcs) + list(lands)
    out = pl.pallas_call(
        body, name=name, out_shape=tuple(pltpu.HBM(a.shape, a.dtype) for a in arrays),
        in_specs=[hbm] * (2 * nw) + [sem, sem, pl.BlockSpec(memory_space=pl.ANY)],
        out_specs=tuple([hbm] * (2 * nw)), input_output_aliases={i: i for i in range(2 * nw)},
        compiler_params=pltpu.CompilerParams(has_side_effects=pltpu.SideEffectType.DATAFLOW_SIDE_EFFECTING))(
            *arrays, send_sems, recv_sems, after)
    return list(out[nw:])


def _other_chips(x, y):
    return [(1 - x, y), (x, 1 - y), (1 - x, 1 - y)]


def _gather_sends(src_ref, land_ref, x, y, c):
    to = land_ref.at[2 * x + y, c]
    return [(src_ref, to, (x, y, 1 - c))] + [(src_ref, to, (px, py, c)) for px, py in _other_chips(x, y)]


def _gather_lands(src_ref, land_ref, x, y, c):
    return [(src_ref, land_ref.at[2 * x + y, 1 - c])] + [
        (src_ref, land_ref.at[2 * px + py, c]) for px, py in _other_chips(x, y)]


def _gather_sends_all(src_ref, land_ref, x, y, c):
    to = land_ref.at[2 * x + y, c]
    return [(src_ref, to, (x, y, 1 - c))] + [
        (src_ref, to, (px, py, pc)) for px, py in _other_chips(x, y) for pc in (c, 1 - c)]


def _gather_lands_all(src_ref, land_ref, x, y, c):
    return [(src_ref, land_ref.at[2 * x + y, 1 - c])] + [
        (src_ref, land_ref.at[2 * px + py, pc]) for px, py in _other_chips(x, y) for pc in (c, 1 - c)]


def _scatter_sends(src_ref, land_ref, x, y, c):
    return [(src_ref.at[2 * px + py], land_ref.at[2 * x + y], (px, py, c)) for px, py in _other_chips(x, y)]


def _scatter_lands(src_ref, land_ref, x, y, c):
    return [(src_ref.at[2 * x + y], land_ref.at[2 * px + py]) for px, py in _other_chips(x, y)]


def _forward_sibling(name, lands):
    nw = len(lands)

    def body(*refs):
        land_refs, out_refs, send_sems, recv_sems = refs[:nw], refs[nw:2 * nw], refs[2 * nw], refs[2 * nw + 1]
        x, y, c = _place()
        cps = []
        for w in range(nw):
            cps += [pltpu.make_async_remote_copy(
                src_ref=land_refs[w].at[2 * px + py, c], dst_ref=out_refs[w].at[2 * px + py, c],
                send_sem=send_sems.at[w, j], recv_sem=recv_sems.at[w, j], device_id=(x, y, 1 - c),
                device_id_type=MESH) for j, (px, py) in enumerate(_other_chips(x, y))]
        for cp in cps:
            cp.start()
        for w in range(nw):
            for j, (px, py) in enumerate(_other_chips(x, y)):
                slot = out_refs[w].at[2 * px + py, 1 - c]
                pltpu.make_async_remote_copy(src_ref=slot, dst_ref=slot, send_sem=send_sems.at[w, j],
                                             recv_sem=recv_sems.at[w, j], device_id=(x, y, 1 - c),
                                             device_id_type=MESH).wait_recv()
        for cp in cps:
            cp.wait_send()

    return pl.pallas_call(
        body, name=name, out_shape=[_sds(a.shape, a.dtype) for a in lands],
        in_specs=_hbm_specs(nw), out_specs=_hbm_specs(nw), input_output_aliases={i: i for i in range(nw)},
        scratch_shapes=[pltpu.SemaphoreType.DMA((nw, 3)), pltpu.SemaphoreType.DMA((nw, 3))])(*lands)


def _share_halves(name, halves):
    nw = len(halves)

    def body(*refs):
        in_refs, out_refs, send_sems, recv_sems = refs[:nw], refs[nw:2 * nw], refs[2 * nw], refs[2 * nw + 1]
        x, y, c = _place()
        cps = [pltpu.make_async_remote_copy(
            src_ref=in_refs[w].at[c], dst_ref=out_refs[w].at[c], send_sem=send_sems.at[w],
            recv_sem=recv_sems.at[w], device_id=(x, y, 1 - c), device_id_type=MESH) for w in range(nw)]
        for cp in cps:
            cp.start()
        for w in range(nw):
            slot = out_refs[w].at[1 - c]
            pltpu.make_async_remote_copy(src_ref=slot, dst_ref=slot, send_sem=send_sems.at[w],
                                         recv_sem=recv_sems.at[w], device_id=(x, y, 1 - c),
                                         device_id_type=MESH).wait_recv()
        for cp in cps:
            cp.wait_send()

    return pl.pallas_call(
        body, name=name, out_shape=[_sds(a.shape, a.dtype) for a in halves],
        in_specs=_hbm_specs(nw), out_specs=_hbm_specs(nw), input_output_aliases={i: i for i in range(nw)},
        scratch_shapes=[pltpu.SemaphoreType.DMA((nw,)), pltpu.SemaphoreType.DMA((nw,))])(*halves)


def _place_blocks(name, blks, place):
    nw = len(blks)

    def body(p_ref, *refs):
        for b_ref, o_ref in zip(refs[:nw], refs[nw:]):
            o_ref[...] = b_ref[...]

    return pl.pallas_call(
        body, name=name,
        grid_spec=pltpu.PrefetchScalarGridSpec(
            num_scalar_prefetch=1, grid=(1,),
            in_specs=[pl.BlockSpec(b.shape, lambda i, p: (0, 0)) for b in blks],
            out_specs=[pl.BlockSpec((None, None) + b.shape, lambda i, p: (p[0], p[1], 0, 0)) for b in blks]),
        out_shape=[_sds((4, 2) + b.shape, b.dtype) for b in blks],
        compiler_params=_params(("arbitrary",)))(place, *blks)


def _pair_sum(name, fulls, gots, place):
    nw = len(fulls)

    def body(p_ref, *refs):
        s = pl.program_id(0)
        for a_ref, b_ref, o_ref, l_ref in zip(refs[:nw], refs[nw:2 * nw], refs[2 * nw:3 * nw], refs[3 * nw:]):
            o_ref[...] = (a_ref[...].astype(F32) + b_ref[...].astype(F32)).astype(o_ref.dtype)

            @pl.when(s == p_ref[0])
            def _():
                l_ref[...] = o_ref[...]

    slab = lambda a: pl.BlockSpec((None,) + a.shape[1:], lambda s, p: (s, 0, 0))
    mine = lambda a: pl.BlockSpec((None,) + a.shape[1:], lambda s, p: (p[0], 0, 0))
    out = pl.pallas_call(
        body, name=name,
        grid_spec=pltpu.PrefetchScalarGridSpec(
            num_scalar_prefetch=1, grid=(4,),
            in_specs=[pl.BlockSpec((None, None) + a.shape[2:], lambda s, p: (s, p[1], 0, 0)) for a in fulls]
            + [slab(b) for b in gots],
            out_specs=[slab(b) for b in gots] + [mine(b) for b in gots]),
        out_shape=[_sds(b.shape, BF16) for b in gots] * 2,
        compiler_params=_params(("arbitrary",)))(place, *fulls, *gots)
    return out[:nw], out[nw:]


def _sum4(name, arrs, place):
    nw = len(arrs)

    def body(p_ref, *refs):
        for a_ref, o_ref in zip(refs[:nw], refs[nw:]):
            acc = a_ref[0].astype(F32)
            for s in range(1, 4):
                acc = acc + a_ref[s].astype(F32)
            o_ref[...] = acc

    return pl.pallas_call(
        body, name=name,
        grid_spec=pltpu.PrefetchScalarGridSpec(
            num_scalar_prefetch=1, grid=(1,),
            in_specs=[pl.BlockSpec(a.shape, lambda i, p: (0, 0, 0)) for a in arrs],
            out_specs=[pl.BlockSpec((None,) + a.shape[1:], lambda i, p: (p[1], 0, 0)) for a in arrs]),
        out_shape=[_sds((2,) + a.shape[1:], F32) for a in arrs],
        compiler_params=_params(("arbitrary",)))(place, *arrs)


def _small_update(gathered, w, m, v):
    n = w.shape[1]
    tn = 2048
    c1 = 1.0 - ADAM_B1 ** ADAM_STEP
    c2 = 1.0 - ADAM_B2 ** ADAM_STEP

    def body(g_ref, w_ref, m_ref, v_ref, go_ref, d_ref, mo_ref, vo_ref):
        g = g_ref[0:1, :]
        for d in range(1, 8):
            g = g + g_ref[d:d + 1, :]
        go_ref[...] = g
        m = ADAM_B1 * m_ref[...] + (1.0 - ADAM_B1) * g
        v = ADAM_B2 * v_ref[...] + (1.0 - ADAM_B2) * (g * g)
        mo_ref[...] = m
        vo_ref[...] = v
        d_ref[...] = -ADAM_LR * ((m / c1) / (jnp.sqrt(v / c2) + ADAM_EPS) + ADAM_WD * w_ref[...])

    row = pl.BlockSpec((1, tn), lambda i: (0, i))
    return pl.pallas_call(
        body, name="small_update", grid=(n // tn,),
        in_specs=[pl.BlockSpec((8, tn), lambda i: (0, i)), row, row, row], out_specs=[row] * 4,
        out_shape=[_sds((1, n), F32)] * 4, compiler_params=_params(("parallel",)))(gathered, w, m, v)


def _swiglu(ps, es):
    g, u = ps
    return g * _sigmoid(g) * u, g, u


def _swiglu_bwd(ps, es):
    g, u = es[0].astype(F32), es[1].astype(F32)
    sg = _sigmoid(g)
    return ps[0] * u * (sg * (1.0 + g * (1.0 - sg))), ps[0] * (g * sg)


def _merge(ps, es):
    ga, gm = [e.astype(F32) for e in es]
    return _sigmoid(ga) * ps[0] + _sigmoid(gm) * ps[1], ps[0], ps[1]


def _merge_bwd(ps, es):
    a, b, ga, gm = [e.astype(F32) for e in es]
    sa, sm = _sigmoid(ga), _sigmoid(gm)
    dm = ps[0]
    return dm * sa, dm * sm, dm * a * (sa * (1.0 - sa)), dm * b * (sm * (1.0 - sm))


W_IN_PIECES = (("q", 512), ("kv", 256), ("mqk", 1024), ("mv", 512), ("mo", 512), ("if", 8),
               ("ga", 1024), ("gm", 1024))


def _local_step(x, tgt, pos_col, mod, sp, in_weights, late_weights, ffn_grads, mixer_grads):
    sh_m, sc_m, gate_m, sh_f, sc_f, gate_f = mod
    inv = ROPE_THETA ** (-2.0 * jnp.arange(HEAD_DIM // 2, dtype=F32) / HEAD_DIM)
    cos, sin = _rope_tables(pos_col, jnp.tile(inv, 4).reshape(1, 128))
    W = dict(in_weights(cos))
    h, pa, pqk, pvo, pif, pg = _proj_in(x, sp["g_pre_mix"], sc_m, sh_m, [
        (W["q+kv"], F32, 256), (W["mqk"], F32, 512), (W["mv+mo"], BF16, 512), (W["if"], F32, 128),
        (W["ga+gm"], BF16, 512)])
    ya = _attn_fwd(pa, cos, sin, sp["sinks"])
    qk = _conv_fwd(pqk, sp["conv_w"], sp["conv_b"])
    bcol = jnp.pad(sp["b_if"], ((0, 0), (0, 120)))
    brow = jnp.broadcast_to(sp["b_if"].reshape(8, 1), (8, 128))
    grow = pif[:, :8].T
    hm, cs, ns, ms = _mlstm_fwd(qk, pvo, pif, bcol, grow, brow)
    ym = _mlstm_out(hm, pvo, sp["norm_w"])
    W.update(late_weights(ym))
    w_fg, w_fu, w_fd = W["fg"], W["fu"], W["fd"]
    merged, br_a, br_m = _mm("branches", [[(ya, W["ba"])], [(ym, W["bm"])]],
                             [(pg, 0), (pg, 1)], _merge, [BF16, BF16, BF16], cn=512, nt=True)
    wide, narrow = (D_MODEL, F32), (D_MODEL, BF16)
    mix, x1, h2 = _mm_rows("mix_out", [[(merged, W["out"])]],
                           [x, gate_m, sp["g_post_mix"], sp["g_pre_ffn"], sc_f, sh_f],
                           _res_norm_rows, [wide, wide, narrow], [], cn=512)
    act, gt, up, dy, dff, acc_l, loss = _ffn_fwd(h2, w_fg, w_fu, w_fd, [x1, tgt, gate_f, sp["g_post_ffn"]],
                                                 _final_loss_rows)

    G = {}
    dgt, dup, dx1, dmix, acc_r = _ffn_bwd(
        dff, gt, up, w_fd, w_fg, w_fu, [x1, mix, dy, sc_f, gate_m, sp["g_pre_ffn"], sp["g_post_mix"]],
        _res_norm_bwd_rows)
    g_fd = _mm_tn("dw_ffn_down", act, dff, BF16, 1408, 512)
    g_fg = _mm_tn("dw_ffn_gate", dgt, h2, BF16, 1408, 1024)
    g_fu = _mm_tn("dw_ffn_up", dup, h2, BF16, 1408, 1024)
    w_out_tied = W["out"] + ffn_grads(g_fg, g_fu, g_fd).astype(BF16)
    d_a, d_m, dga, dgm = _mm("mix_out_bwd", [[(dmix, w_out_tied)]],
                             [br_a, br_m, (pg, 0), (pg, 1)], _merge_bwd,
                             [BF16] * 4, cn=512, nt=True)
    G["out"] = _mm_tn("dw_out", merged, dmix, BF16, 1024, 512)
    dya, = _mm("branch_attn_bwd", [[(d_a, W["ba"])]], [], _first, [F32], cn=512)
    heads = MLSTM_HEADS * MLSTM_HEAD_DIM
    dhm, do_m, acc_n = _mm_rows("branch_mlstm_bwd", [[(d_m, W["bm"])]], [hm, pvo, sp["norm_w"]],
                                _mlstm_out_bwd_rows, [(heads, F32), (heads, BF16)], [(8, heads)], cn=512)
    G["ba"] = _mm_tn("dw_branch_attn", d_a, ya, BF16, 1024, 512)
    G["bm"] = _mm_tn("dw_branch_mlstm", d_m, ym, BF16, 1024, 512)
    dqk, dv_m, dgc, dgr = _mlstm_bwd(qk, pvo, pif, bcol, grow, brow, cs, ns, ms, dhm)
    dif, acc_g = _gate_bwd(dgc, dgr, pif, bcol)
    du, acc_c = _conv_bwd(pqk, sp["conv_w"], sp["conv_b"], dqk)
    dq_a, dkv, dsink = _attn_bwd(pa, cos, sin, sp["sinks"], dya)
    dproj = {"q": dq_a, "kv": dkv, "mqk": du, "mv": dv_m, "mo": do_m, "if": dif, "ga": dga, "gm": dgm}
    for k, _ in W_IN_PIECES:
        G[k] = _mm_tn("dw_in_" + k, dproj[k], h, BF16, dproj[k].shape[1], 1024)
    w_tied = dict(W, **{"if": W["if"] + mixer_grads(G).astype(BF16)})
    dx, acc_p = _mm_rows("proj_bwd", [[(dproj[k], w_tied[k]) for k, _ in W_IN_PIECES]],
                         [x, dx1, sp["g_pre_mix"], sc_m], _pre_norm_bwd_rows, [wide], [(8, D_MODEL)], cn=512)

    small = {
        "mod": jnp.concatenate([acc_p[1], acc_p[0], acc_r[3], acc_r[1], acc_r[0], acc_l[0]]),
        "g_pre_mix": acc_p[2], "g_post_mix": acc_r[4], "b_if": acc_g[0, :8],
        "conv_w": acc_c[:CONV_WIDTH].reshape(-1), "conv_b": acc_c[CONV_WIDTH],
        "sinks": dsink[:, 0], "norm_w": acc_n[0], "g_pre_ffn": acc_r[2], "g_post_ffn": acc_l[1]}
    return loss, dx, small


IN_WIDTH = sum(n for _, n in W_IN_PIECES)
IN_SHARD = IN_WIDTH // 4
IN_SHARD_PAD = -(-IN_SHARD // 32) * 32


def _split_w_in(w_in_t):
    out, off, start = {}, 0, {}
    for k, n in W_IN_PIECES:
        out[k], start[k] = w_in_t[off:off + n], off
        off += n
    out["if"] = jnp.pad(out["if"], ((0, 120), (0, 0)))
    for name, first, last in (("q+kv", "q", "kv"), ("mv+mo", "mv", "mo"), ("ga+gm", "ga", "gm")):
        out[name] = w_in_t[start[first]:start[last] + out[last].shape[0]]
    return out


def _halves(a):
    return a.reshape(4, 2, a.shape[0] // 8, a.shape[1])


SMALL = (("b_ada", 6144), ("g_pre_mix", 1024), ("g_post_mix", 1024), ("b_if", 128), ("conv_w", 4096),
         ("conv_b", 1024), ("sinks", 128), ("norm_w", 512), ("g_pre_ffn", 1024), ("g_post_ffn", 1024))
SMALL_LEN = 8 * 2048


def _pack_small(vals):
    parts = []
    for k, n in SMALL:
        v = vals[k].reshape(-1)
        parts.append(jnp.pad(v, (0, n - v.shape[0])))
    flat = jnp.concatenate(parts)
    return jnp.pad(flat, (0, SMALL_LEN - flat.shape[0]))


def _unpack_small(flat, shapes):
    out, off = {}, 0
    for k, n in SMALL:
        size = 1
        for d in shapes[k]:
            size *= d
        out[k] = flat[off:off + size].reshape(shapes[k])
        off += n
    return out


def kernel(x, c, positions, w_ada, b_ada, g_pre_mix, g_post_mix, w_in, b_if, conv_w, conv_b, attn_sinks, mlstm_norm_w, w_branch_attn, w_branch_mlstm, w_out, g_pre_ffn, g_post_ffn, w_ffn_gate, w_ffn_up, w_ffn_down, loss_target, m_w_ada, m_b_ada, m_g_pre_mix, m_g_post_mix, m_w_in, m_b_if, m_conv_w, m_conv_b, m_attn_sinks, m_mlstm_norm_w, m_w_branch_attn, m_w_branch_mlstm, m_w_out, m_g_pre_ffn, m_g_post_ffn, m_w_ffn_gate, m_w_ffn_up, m_w_ffn_down, v_w_ada, v_b_ada, v_g_pre_mix, v_g_post_mix, v_w_in, v_b_if, v_conv_w, v_conv_b, v_attn_sinks, v_mlstm_norm_w, v_w_branch_attn, v_w_branch_mlstm, v_w_out, v_g_pre_ffn, v_g_post_ffn, v_w_ffn_gate, v_w_ffn_up, v_w_ffn_down):
    xi, yi, ci = _place()
    chip = 2 * xi + yi
    dev = 2 * chip + ci
    T = x.shape[1]
    ada_cols = w_ada.shape[2]

    place = jnp.stack([chip, ci]).astype(jnp.int32)

    def my_half(a):
        n = a.shape[0] // 2
        return lax.dynamic_slice_in_dim(a, ci * n, n, axis=0).astype(BF16)

    blk = jnp.concatenate([c.reshape(-1), conv_w.reshape(-1)]).reshape(8, 256)
    got = _all_gather8("gather_cond", blk, pltpu.VMEM).reshape(8, 2048)
    c_all = got[:, :D_MODEL].astype(BF16)
    conv_full = got[::2, D_MODEL:].reshape(4, CONV_WIDTH, -1).transpose(1, 0, 2).reshape(CONV_WIDTH, -1)

    b_sh = lax.dynamic_slice_in_dim(b_ada, chip * ada_cols, ada_cols, axis=1)
    mod_part, = _mm("ada_mod", [[(c_all, w_ada[0].astype(BF16))]], [b_sh],
                    lambda ps, es: (ps[0] + es[0],), [F32], cn=512, tm=8)
    mod_all = _all_gather8("gather_mod", mod_part, pltpu.VMEM).reshape(4, 2, 8, ada_cols)[:, 0]
    mod = lax.dynamic_index_in_dim(mod_all, dev, axis=1, keepdims=False).reshape(6, 1, D_MODEL)

    def gather_start(name, blks, after, sends, copies):
        return _split_start(name + "_start", blks, _place_blocks(name + "_place", blks, place),
                            sends, copies, after)

    w_in_t = jnp.pad(w_in[0].T, ((0, IN_SHARD_PAD - IN_SHARD), (0, 0)))
    in_started = gather_start("in_gather", [my_half(w_in_t)], mod, _gather_sends, 4)
    late_keys = ("fg", "fu", "fd", "out", "ba", "bm")
    late_started = gather_start(
        "late_gather",
        [my_half(w_ffn_gate[0].T), my_half(w_ffn_up[0].T), my_half(w_ffn_down[0]), my_half(w_out[0]),
         my_half(w_branch_attn[0].T), my_half(w_branch_mlstm[0].T)], in_started[4], _gather_sends_all, 7)
    mod = mod + (in_started[4][0, 0] + late_started[4][0, 0])

    def in_weights(after):
        g_in, = _forward_sibling("in_gather_forward",
                                 _split_wait("in_gather_wait", in_started, after, _gather_lands, 4))
        return _split_w_in(g_in.reshape(4, IN_SHARD_PAD, D_MODEL)[:, :IN_SHARD].reshape(IN_WIDTH, D_MODEL))

    def late_weights(after):
        lands = _split_wait("late_gather_wait", late_started, after, _gather_lands_all, 7)
        return {k: a.reshape(-1, a.shape[-1]) for k, a in zip(late_keys, lands)}

    sent = {}

    def scatter_start(name, groups):
        pairs, lands = _pair_sum(name + "_pair_sum", groups, _swap_halves_sibling(name + "_pair", groups), place)
        sent[name] = _split_start(name + "_start", pairs, lands, _scatter_sends, 3, pairs[0])
        return sent[name][4][0, 0]

    def ffn_grads(g_fg, g_fu, g_fd):
        return scatter_start("rs_ffn", [_halves(g_fg), _halves(g_fu), _halves(g_fd)])

    def mixer_grads(G):
        g_in_t = jnp.concatenate([G[k][:n] for k, n in W_IN_PIECES]).reshape(4, IN_SHARD, D_MODEL)
        g_in_t = jnp.pad(g_in_t, ((0, 0), (0, IN_SHARD_PAD - IN_SHARD), (0, 0)))
        return scatter_start("rs_mix", [g_in_t.reshape(4, 2, IN_SHARD_PAD // 2, D_MODEL), _halves(G["out"]),
                                        _halves(G["ba"]), _halves(G["bm"])])

    sp = {"g_pre_mix": g_pre_mix, "g_post_mix": g_post_mix, "b_if": b_if, "conv_w": conv_full,
          "conv_b": conv_b, "sinks": attn_sinks, "norm_w": mlstm_norm_w, "g_pre_ffn": g_pre_ffn,
          "g_post_ffn": g_post_ffn}
    loss, dx, small = _local_step(x[0], loss_target[0], positions.reshape(T, 1), [mod[i] for i in range(6)],
                                  sp, in_weights, late_weights, ffn_grads, mixer_grads)

    reds = (_sum4("rs_ffn_chip_sum", _split_wait("rs_ffn_wait", sent["rs_ffn"], dx, _scatter_lands, 3), place)
            + _sum4("rs_mix_chip_sum", _split_wait("rs_mix_wait", sent["rs_mix"], dx, _scatter_lands, 3), place))
    gsh = {k: s.reshape(-1, s.shape[-1])
           for k, s in zip(("fg", "fu", "fd", "w_in", "out", "ba", "bm"), _share_halves("rs_share", reds))}
    gsh["w_in"] = gsh["w_in"][:IN_SHARD]

    small["b_ada"] = small.pop("mod")
    vec = _pack_small(small).reshape(8, 2048)
    g_all = _all_gather8("gather_small", vec, pltpu.VMEM).reshape(8, SMALL_LEN)
    dmod_sh = lax.dynamic_slice_in_dim(g_all[:, :6 * D_MODEL], chip * ada_cols, ada_cols, axis=1)
    g_w_ada = _mm_tn("dw_ada", c_all, dmod_sh.astype(BF16), F32, D_MODEL, 512, 8)

    smalls = {"b_ada": (b_ada, m_b_ada, v_b_ada), "g_pre_mix": (g_pre_mix, m_g_pre_mix, v_g_pre_mix),
              "g_post_mix": (g_post_mix, m_g_post_mix, v_g_post_mix), "b_if": (b_if, m_b_if, v_b_if),
              "conv_w": None, "conv_b": (conv_b, m_conv_b, v_conv_b),
              "sinks": (attn_sinks, m_attn_sinks, v_attn_sinks),
              "norm_w": (mlstm_norm_w, m_mlstm_norm_w, v_mlstm_norm_w),
              "g_pre_ffn": (g_pre_ffn, m_g_pre_ffn, v_g_pre_ffn),
              "g_post_ffn": (g_post_ffn, m_g_post_ffn, v_g_post_ffn)}
    shapes = {k: (t[0].shape if t is not None else (1, CONV_WIDTH, D_MODEL)) for k, t in smalls.items()}
    zeros = jnp.zeros((CONV_WIDTH * D_MODEL,), F32)
    packs = [_pack_small({k: (t[i] if t is not None else zeros) for k, t in smalls.items()}).reshape(1, -1)
             for i in range(3)]
    s_out = [_unpack_small(o[0], shapes) for o in _small_update(g_all, *packs)]
    g_conv = lax.dynamic_slice_in_dim(s_out[0]["conv_w"], chip * conv_w.shape[2], conv_w.shape[2], axis=2)

    res = {}
    for k, t in smalls.items():
        if t is not None:
            res[k] = tuple(o[k] for o in s_out)
    res["conv_w"] = (g_conv, *[o[None] for o in _adamw("adam_conv_w", conv_w[0], g_conv[0], m_conv_w[0], v_conv_w[0])])
    res["w_ada"] = (g_w_ada[None], *[o[None] for o in _adamw("adam_w_ada", w_ada[0], g_w_ada, m_w_ada[0], v_w_ada[0])])
    bigs = {"w_in": (w_in, m_w_in, v_w_in), "ba": (w_branch_attn, m_w_branch_attn, v_w_branch_attn),
            "bm": (w_branch_mlstm, m_w_branch_mlstm, v_w_branch_mlstm), "out": (w_out, m_w_out, v_w_out),
            "fg": (w_ffn_gate, m_w_ffn_gate, v_w_ffn_gate), "fu": (w_ffn_up, m_w_ffn_up, v_w_ffn_up),
            "fd": (w_ffn_down, m_w_ffn_down, v_w_ffn_down)}
    for k, (w, m, v) in bigs.items():
        if k in ("w_in", "fg", "fu"):
            res[k] = tuple(o.T[None] for o in (gsh[k], *_adamw("adam_" + k, w[0].T, gsh[k], m[0].T, v[0].T)))
        else:
            g = gsh[k].T if k in ("ba", "bm") else gsh[k]
            res[k] = (g[None], *[o[None] for o in _adamw("adam_" + k, w[0], g, m[0], v[0])])

    order = ("w_ada", "b_ada", "g_pre_mix", "g_post_mix", "w_in", "b_if", "conv_w", "conv_b", "sinks",
             "norm_w", "ba", "bm", "out", "g_pre_ffn", "g_post_ffn", "fg", "fu", "fd")
    total = lax.psum(loss[0, 0], ("x", "y", "c"))
    return (total, dx[None], *[res[k][0] for k in order], *[res[k][1] for k in order],
            *[res[k][2] for k in order], *[res[k][3] for k in order])
```

```python
import functools

import jax
import jax.numpy as jnp
from jax import lax
from jax.experimental import pallas as pl
from jax.experimental.pallas import tpu as pltpu

F32, BF16 = jnp.float32, jnp.bfloat16
MESH = pl.DeviceIdType.MESH

D_MODEL = 1024
N_Q_HEADS, N_KV_HEADS, HEAD_DIM, WINDOW = 8, 2, 64, 128
ROPE_THETA = 10000.0
MLSTM_HEADS, MLSTM_HEAD_DIM, MLSTM_CHUNK, CONV_WIDTH = 4, 128, 64, 4
D_FF = 2816
NORM_EPS = 1e-6
ADAM_LR, ADAM_B1, ADAM_B2, ADAM_EPS, ADAM_WD, ADAM_STEP = 0.001, 0.9, 0.999, 1e-08, 0.01, 10

VMEM_LIMIT = 56 * 1024 * 1024
ROW_TILE = 256
MM_TM = 512
MM_TT = 1024
ATTN_BLK = WINDOW
STEP_ROWS = 2 * MLSTM_CHUNK
NEG_INF = float("-inf")


def _params(sem):
    return pltpu.CompilerParams(dimension_semantics=sem, vmem_limit_bytes=VMEM_LIMIT)


def _sds(shape, dtype):
    return jax.ShapeDtypeStruct(shape, dtype)


def _sigmoid(x):
    return 1.0 / (1.0 + jnp.exp(-x))


def _dot(a, b, ca, cb):
    return lax.dot_general(a, b, (((ca,), (cb,)), ((), ())), preferred_element_type=F32)


def _bdot(a, b, ca, cb):
    return lax.dot_general(a, b, (((ca,), (cb,)), ((0,), (0,))), preferred_element_type=F32)


def _bdot_rows(a, b):
    return jnp.stack([_dot(a[h], b[h], 0, 0) for h in range(a.shape[0])])


def _mm(name, prods, extras, epi, out_dtypes, cn, nt=False, tm=MM_TM):
    flat = [ab for p in prods for ab in p]
    counts = [len(p) for p in prods]
    M = flat[0][0].shape[0]
    N = flat[0][1].shape[0 if nt else 1]
    tm = min(tm, M)
    n_in = 2 * len(flat) + len(extras)

    def body(*refs):
        ins, outs = refs[:n_in], refs[n_in:]
        for j in range(N // cn):
            cols = slice(j * cn, (j + 1) * cn)
            k, ps = 0, []
            for cnt in counts:
                acc = None
                for _ in range(cnt):
                    b = ins[k + 1][cols, :] if nt else ins[k + 1][:, cols]
                    d = _dot(ins[k][...], b, 1, 1 if nt else 0)
                    acc = d if acc is None else acc + d
                    k += 2
                ps.append(acc)
            res = epi(ps, [r[:, cols] for r in ins[k:]])
            for o, r in zip(outs, res):
                o[:, cols] = r.astype(o.dtype)

    in_specs, args = [], []
    for a, b in flat:
        in_specs.append(pl.BlockSpec((tm, a.shape[1]), lambda i: (i, 0)))
        in_specs.append(pl.BlockSpec(b.shape, lambda i: (0, 0), pipeline_mode=pl.Buffered(1)))
        args += [a, b]
    for e in extras:
        e, off = e if isinstance(e, tuple) else (e, 0)
        rows = 1 if e.shape[0] == 1 else tm
        in_specs.append(pl.BlockSpec((rows, N), lambda i, off=off, rows=rows: (0 if rows == 1 else i, off)))
        args.append(e)
    return pl.pallas_call(
        body, name=name, grid=(M // tm,), in_specs=in_specs,
        out_specs=[pl.BlockSpec((tm, N), lambda i: (i, 0)) for _ in out_dtypes],
        out_shape=[_sds((M, N), dt) for dt in out_dtypes],
        compiler_params=_params(("parallel",)))(*args)


def _mm_rows(name, prods, extras, epi, outs, accs, cn, nt=False, tm=MM_TM):
    flat = [ab for p in prods for ab in p]
    counts = [len(p) for p in prods]
    M = flat[0][0].shape[0]
    N = flat[0][1].shape[0 if nt else 1]
    tm = min(tm, M)
    n_mm, n_in, n_out = 2 * len(flat), 2 * len(flat) + len(extras), len(outs)

    def body(*refs):
        ins, out_refs, acc_refs = refs[:n_in], refs[n_in:n_in + n_out], refs[n_in + n_out:]

        @pl.when(pl.program_id(0) == 0)
        def _():
            for a in acc_refs:
                a[...] = jnp.zeros_like(a)

        chunks = [[] for _ in counts]
        for j in range(N // cn):
            cols = slice(j * cn, (j + 1) * cn)
            k = 0
            for p, cnt in enumerate(counts):
                acc = None
                for _ in range(cnt):
                    b = ins[k + 1][cols, :] if nt else ins[k + 1][:, cols]
                    d = _dot(ins[k][...], b, 1, 1 if nt else 0)
                    acc = d if acc is None else acc + d
                    k += 2
                chunks[p].append(acc)
        ps = [c[0] if len(c) == 1 else jnp.concatenate(c, axis=1) for c in chunks]
        res, incs = epi(ps, [r[...] for r in ins[n_mm:]])
        for o, r in zip(out_refs, res):
            o[...] = r.astype(o.dtype)
        for a, inc in zip(acc_refs, incs):
            a[...] += inc

    in_specs, args = [], []
    for a, b in flat:
        in_specs.append(pl.BlockSpec((tm, a.shape[1]), lambda i: (i, 0)))
        in_specs.append(pl.BlockSpec(b.shape, lambda i: (0, 0), pipeline_mode=pl.Buffered(1)))
        args += [a, b]
    for e in extras:
        rows = 1 if e.shape[0] == 1 else tm
        in_specs.append(pl.BlockSpec((rows, e.shape[1]), lambda i, rows=rows: (0 if rows == 1 else i, 0)))
        args.append(e)
    return pl.pallas_call(
        body, name=name, grid=(M // tm,), in_specs=in_specs,
        out_specs=[pl.BlockSpec((tm, w), lambda i: (i, 0)) for w, _ in outs]
        + [pl.BlockSpec(s, lambda i: (0, 0)) for s in accs],
        out_shape=[_sds((M, w), dt) for w, dt in outs] + [_sds(s, F32) for s in accs],
        compiler_params=_params(("arbitrary",)))(*args)


def _mm_tn(name, a, b, out_dtype, tk, tn, tt=MM_TT):
    T, Ka = a.shape
    N = b.shape[1]
    tt = min(tt, T)
    steps = T // tt

    def body(a_ref, b_ref, o_ref, acc):
        t = pl.program_id(2)

        @pl.when(t == 0)
        def _():
            acc[...] = jnp.zeros_like(acc)

        acc[...] += _dot(a_ref[...], b_ref[...], 0, 0)

        @pl.when(t == steps - 1)
        def _():
            o_ref[...] = acc[...].astype(o_ref.dtype)

    return pl.pallas_call(
        body, name=name, grid=(Ka // tk, N // tn, steps),
        in_specs=[pl.BlockSpec((tt, tk), lambda i, j, t: (t, i)),
                  pl.BlockSpec((tt, tn), lambda i, j, t: (t, j))],
        out_specs=pl.BlockSpec((tk, tn), lambda i, j, t: (i, j)),
        out_shape=_sds((Ka, N), out_dtype),
        scratch_shapes=[pltpu.VMEM((tk, tn), F32)],
        compiler_params=_params(("parallel", "parallel", "arbitrary")))(a, b)


def _first(ps, es):
    return (ps[0],)


def _rows(name, body, ins, out_shapes, T, tr=ROW_TILE):
    tr = min(tr, T)

    def spec(shape):
        if shape[0] == T:
            return pl.BlockSpec((tr,) + tuple(shape[1:]), lambda i: (i,) + (0,) * (len(shape) - 1))
        return pl.BlockSpec(tuple(shape), lambda i: (0,) * len(shape))

    return pl.pallas_call(
        body, name=name, grid=(T // tr,),
        in_specs=[spec(a.shape) for a in ins], out_specs=[spec(s.shape) for s in out_shapes],
        out_shape=out_shapes, compiler_params=_params(("arbitrary",)))(*ins)


def _rms(x):
    r = lax.rsqrt(jnp.mean(x * x, axis=-1, keepdims=True) + NORM_EPS)
    return x * r, r


def _rms_bwd(dxn, xn, r):
    return r * (dxn - xn * jnp.mean(dxn * xn, axis=-1, keepdims=True))


def _colsum(v):
    return jnp.sum(v, axis=0, keepdims=True)


def _proj_in(x, g, sc, sh, groups):
    T = x.shape[0]
    tm = min(MM_TM, T)
    ng = len(groups)

    def body(x_ref, g_ref, sc_ref, sh_ref, *rest):
        w_refs, h_ref, out_refs = rest[:ng], rest[ng], rest[ng + 1:]
        xn, _ = _rms(x_ref[...])
        h = (xn * g_ref[...] * (1.0 + sc_ref[...]) + sh_ref[...]).astype(BF16)
        h_ref[...] = h
        for w_ref, o_ref, (w, _, cn) in zip(w_refs, out_refs, groups):
            for j in range(w.shape[0] // cn):
                cols = slice(j * cn, (j + 1) * cn)
                o_ref[:, cols] = _dot(h, w_ref[cols, :], 1, 1).astype(o_ref.dtype)

    row = pl.BlockSpec((1, D_MODEL), lambda i: (0, 0))
    tile = lambda w: pl.BlockSpec((tm, w), lambda i: (i, 0))
    return pl.pallas_call(
        body, name="proj_in", grid=(T // tm,),
        in_specs=[tile(D_MODEL), row, row, row] + [
            pl.BlockSpec(w.shape, lambda i: (0, 0), pipeline_mode=pl.Buffered(1)) for w, _, _ in groups],
        out_specs=[tile(D_MODEL)] + [tile(w.shape[0]) for w, _, _ in groups],
        out_shape=[_sds((T, D_MODEL), BF16)] + [_sds((T, w.shape[0]), dt) for w, dt, _ in groups],
        compiler_params=_params(("parallel",)))(x, g, sc, sh, *[w for w, _, _ in groups])


def _acc_rows(rows):
    w = rows[0].shape[1]
    return jnp.concatenate(rows + [jnp.zeros((8 - len(rows), w), F32)], axis=0)


def _res_norm_rows(ps, es):
    mix = ps[0]
    x, gate, gp, g2, sc, sh = es
    mh, _ = _rms(mix)
    x1 = x + gate * (mh * gp)
    xn, _ = _rms(x1)
    return [mix, x1, xn * g2 * (1.0 + sc) + sh], []


def _final_loss_rows(ps, es):
    x1, tgt, gate, gp = es
    fh, r = _rms(ps[0])
    e = x1 + gate * (fh * gp) - tgt
    loss = 0.5 * jnp.sum(jnp.mean(e * e, axis=-1, keepdims=True))
    dy = e * (1.0 / D_MODEL)
    acc = _acc_rows([_colsum(dy * fh * gp), _colsum(dy * gate * fh)])
    return [dy, _rms_bwd(dy * gate * gp, fh, r)], [acc, jnp.full((1, 128), loss, F32)]


def _res_norm_bwd_rows(ps, es):
    dh = ps[0]
    x1, mix, dy, sc, gate, g2, gp = es
    xn, r1 = _rms(x1)
    rows = [_colsum(dh * xn * g2), _colsum(dh), _colsum(dh * (1.0 + sc) * xn)]
    dx1 = dy + _rms_bwd(dh * (1.0 + sc) * g2, xn, r1)
    mh, rm = _rms(mix)
    rows += [_colsum(dx1 * mh * gp), _colsum(dx1 * gate * mh)]
    return [dx1, _rms_bwd(dx1 * gate * gp, mh, rm)], [_acc_rows(rows)]


def _pre_norm_bwd_rows(ps, es):
    dh = ps[0]
    x, dx1, g, sc = es
    xn, r = _rms(x)
    rows = [_colsum(dh * xn * g), _colsum(dh), _colsum(dh * (1.0 + sc) * xn)]
    return [dx1 + _rms_bwd(dh * (1.0 + sc) * g, xn, r)], [_acc_rows(rows)]


def _rope_tables(pos_col, inv_freq):
    T = pos_col.shape[0]

    def body(p_ref, f_ref, c_ref, s_ref):
        ang = p_ref[...].astype(F32) * f_ref[...]
        lane = lax.broadcasted_iota(jnp.int32, ang.shape, 1)
        c_ref[...] = jnp.cos(ang)
        s_ref[...] = jnp.where(lane % HEAD_DIM < HEAD_DIM // 2, -1.0, 1.0) * jnp.sin(ang)

    return _rows("rope_tables", body, [pos_col, inv_freq],
                 [_sds((T, 128), F32), _sds((T, 128), F32)], T, tr=512)


def _swap_halves(t):
    W = t.shape[1]
    lane = lax.broadcasted_iota(jnp.int32, t.shape, 1)
    half = HEAD_DIM // 2
    return jnp.where(lane % HEAD_DIM < half, pltpu.roll(t, W - half, 1), pltpu.roll(t, half, 1))


def _widen(c, W):
    return c if W == 128 else jnp.concatenate([c] * (W // 128), axis=1)


def _rope(t, c, s):
    W = t.shape[1]
    return t * _widen(c, W) + _swap_halves(t) * _widen(s, W)


def _unrope(dy, c, s):
    W = dy.shape[1]
    return dy * _widen(c, W) + _swap_halves(dy * _widen(s, W))


def _attn_mask(n):
    qi = lax.broadcasted_iota(jnp.int32, (ATTN_BLK, 2 * ATTN_BLK), 0)
    kj = lax.broadcasted_iota(jnp.int32, (ATTN_BLK, 2 * ATTN_BLK), 1)
    rel = kj - ATTN_BLK
    return (rel <= qi) & (qi - rel < WINDOW) & ((n > 0) | (kj >= ATTN_BLK))


def _attn_load(cur, prv, cc, sc, cp, sp):
    x, xp = cur[...], prv[...]
    q = _rope(x[:, :512], cc[...], sc[...]) * (HEAD_DIM ** -0.5)
    k = jnp.concatenate([_rope(xp[:, 512:640], cp[...], sp[...]),
                         _rope(x[:, 512:640], cc[...], sc[...])], axis=0)
    v = jnp.concatenate([xp[:, 640:768], x[:, 640:768]], axis=0)
    return q, k, v


ROLLED = tuple(h for h in range(N_Q_HEADS) if h % 2 != h // (N_Q_HEADS // N_KV_HEADS))


def _pair_heads(t):
    half = lax.broadcasted_iota(jnp.int32, (ATTN_BLK, 128), 1) // HEAD_DIM
    return jnp.stack([jnp.where(half == h % 2, t[:, 128 * (h // 2):128 * (h // 2) + 128], 0.0)
                      for h in range(N_Q_HEADS)])


def _kv_heads(t):
    half = lax.broadcasted_iota(jnp.int32, t.shape, 1) // HEAD_DIM
    tr = pltpu.roll(t, HEAD_DIM, 1)
    return jnp.stack([jnp.where(half == h % 2, tr if h in ROLLED else t, 0.0)
                      for h in range(N_Q_HEADS)])


def _sink_column(snk):
    return jnp.stack([jnp.full((1, 1), snk[0, h], F32) for h in range(N_Q_HEADS)])


def _attn_probs(qh, kh, mask, sink):
    s = jnp.where(mask, _bdot(qh, kh, 2, 2), NEG_INF)
    m = jnp.maximum(jnp.max(s, axis=-1, keepdims=True), sink)
    p = jnp.exp(s - m)
    es = jnp.exp(sink - m)
    rl = 1.0 / (jnp.sum(p, axis=-1, keepdims=True) + es)
    return p, es, rl


def _attn_specs(order):
    blk = lambda w: pl.BlockSpec((ATTN_BLK, w), lambda s: (order(s), 0))
    prv = lambda w: pl.BlockSpec((ATTN_BLK, w), lambda s: (jnp.maximum(order(s) - 1, 0), 0))
    return [blk(768), prv(768), blk(128), blk(128), prv(128), prv(128),
            pl.BlockSpec(memory_space=pltpu.SMEM)]


def _attn_fwd(pa, cos, sin, sinks):
    T = pa.shape[0]
    nb = T // ATTN_BLK

    def body(cur, prv, cc, sc, cp, sp, snk, y_ref):
        n = pl.program_id(0)
        q, k, v = _attn_load(cur, prv, cc, sc, cp, sp)
        qh, kh, vh = _pair_heads(q).astype(BF16), _kv_heads(k).astype(BF16), _kv_heads(v).astype(BF16)
        p, _, rl = _attn_probs(qh, kh, _attn_mask(n), _sink_column(snk))
        o = _bdot(p.astype(BF16), vh, 2, 1) * rl
        for pair in range(N_Q_HEADS // 2):
            y_ref[:, 128 * pair:128 * pair + 128] = (o[2 * pair] + o[2 * pair + 1]).astype(BF16)

    return pl.pallas_call(
        body, name="attn_fwd", grid=(nb,), in_specs=_attn_specs(lambda s: s),
        out_specs=pl.BlockSpec((ATTN_BLK, 512), lambda n: (n, 0)),
        out_shape=_sds((T, 512), BF16), compiler_params=_params(("parallel",)))(
            pa, pa, cos, sin, cos, sin, sinks)


def _attn_bwd(pa, cos, sin, sinks, dy):
    T = pa.shape[0]
    nb = T // ATTN_BLK
    rev = lambda s: nb - 1 - s

    def body(cur, prv, cc, sc, cp, sp, snk, dy_ref, dq_ref, dkv_ref, dsink_ref, carry):
        n = rev(pl.program_id(0))

        @pl.when(pl.program_id(0) == 0)
        def _():
            dsink_ref[...] = jnp.zeros_like(dsink_ref)
            carry[...] = jnp.zeros_like(carry)

        q, k, v = _attn_load(cur, prv, cc, sc, cp, sp)
        qh, kh, vh = _pair_heads(q).astype(BF16), _kv_heads(k).astype(BF16), _kv_heads(v).astype(BF16)
        p, es, rl = _attn_probs(qh, kh, _attn_mask(n), _sink_column(snk))
        pn = p * rl
        do = _pair_heads(dy_ref[...]).astype(BF16)
        dp = _bdot(do, vh, 2, 2)
        delta = jnp.sum(pn * dp, axis=-1, keepdims=True)
        ds = (pn * (dp - delta)).astype(BF16)
        dsink = es * rl * delta
        dq = _bdot(ds, kh, 2, 1) * (HEAD_DIM ** -0.5)
        dkh = _bdot_rows(ds, qh)
        dvh = _bdot_rows(pn.astype(BF16), do)

        def fold(t):
            same = [t[h] for h in range(N_Q_HEADS) if h not in ROLLED]
            moved = [t[h] for h in ROLLED]
            return sum(same[1:], same[0]) + pltpu.roll(sum(moved[1:], moved[0]), HEAD_DIM, 1)

        dk, dv = fold(dkh), fold(dvh)
        for h in range(N_Q_HEADS):
            dsink_ref[h:h + 1, :] += -jnp.sum(dsink[h])
        for pair in range(N_Q_HEADS // 2):
            dq_ref[:, 128 * pair:128 * pair + 128] = _unrope(
                dq[2 * pair] + dq[2 * pair + 1], cc[...], sc[...]).astype(BF16)
        dkv_ref[:, 0:128] = _unrope(dk[ATTN_BLK:] + carry[:, 0:128], cc[...], sc[...]).astype(BF16)
        dkv_ref[:, 128:256] = (dv[ATTN_BLK:] + carry[:, 128:256]).astype(BF16)
        carry[:, 0:128] = dk[:ATTN_BLK]
        carry[:, 128:256] = dv[:ATTN_BLK]

    blk = lambda w: pl.BlockSpec((ATTN_BLK, w), lambda s: (rev(s), 0))
    return pl.pallas_call(
        body, name="attn_bwd", grid=(nb,), in_specs=_attn_specs(rev) + [blk(512)],
        out_specs=[blk(512), blk(256), pl.BlockSpec((8, 128), lambda s: (0, 0))],
        out_shape=[_sds((T, 512), BF16), _sds((T, 256), BF16), _sds((8, 128), F32)],
        scratch_shapes=[pltpu.VMEM((ATTN_BLK, 256), F32)],
        compiler_params=_params(("arbitrary",)))(pa, pa, cos, sin, cos, sin, sinks, dy)


CONV_COLS = 2 * MLSTM_HEADS * MLSTM_HEAD_DIM


def _conv_pre(cur_ref, halo_ref, w_ref, b_ref, i, tr):
    xx = jnp.concatenate([jnp.where(i > 0, halo_ref[...], 0.0), cur_ref[...]], axis=0)
    taps = [(pltpu.roll(xx, CONV_WIDTH - 1 - j, 0) if j < CONV_WIDTH - 1 else xx)[8:8 + tr]
            for j in range(CONV_WIDTH)]
    pre = b_ref[...]
    for j in range(CONV_WIDTH):
        pre = pre + taps[j] * w_ref[j:j + 1, :]
    return pre, taps


def _conv_specs(T, tr):
    return [pl.BlockSpec((tr, CONV_COLS), lambda i: (i, 0)),
            pl.BlockSpec((8, CONV_COLS), lambda i: (jnp.maximum(i * (tr // 8) - 1, 0), 0)),
            pl.BlockSpec((CONV_WIDTH, CONV_COLS), lambda i: (0, 0)),
            pl.BlockSpec((1, CONV_COLS), lambda i: (0, 0))]


def _conv_fwd(pm, w, b):
    T = pm.shape[0]
    tr = min(ROW_TILE, T)

    def body(cur_ref, halo_ref, w_ref, b_ref, o_ref):
        pre, _ = _conv_pre(cur_ref, halo_ref, w_ref, b_ref, pl.program_id(0), tr)
        o_ref[...] = pre * _sigmoid(pre)

    return pl.pallas_call(
        body, name="conv_fwd", grid=(T // tr,), in_specs=_conv_specs(T, tr),
        out_specs=pl.BlockSpec((tr, CONV_COLS), lambda i: (i, 0)),
        out_shape=_sds((T, CONV_COLS), F32), compiler_params=_params(("parallel",)))(pm, pm, w, b)


def _conv_bwd(pqk, w, b, dqk):
    T = pqk.shape[0]
    tr = min(ROW_TILE, T)
    nt = T // tr

    def body(cur_ref, prev_ref, next_ref, w_ref, b_ref, d_ref, dnext_ref, du_ref, acc_ref):
        i = pl.program_id(0)

        @pl.when(i == 0)
        def _():
            acc_ref[...] = jnp.zeros_like(acc_ref)

        last = i == nt - 1
        xx = jnp.concatenate([jnp.where(i > 0, prev_ref[...], 0.0), cur_ref[...],
                              jnp.where(last, 0.0, next_ref[...])], axis=0)
        taps = [(pltpu.roll(xx, CONV_WIDTH - 1 - j, 0) if j < CONV_WIDTH - 1 else xx)[8:16 + tr]
                for j in range(CONV_WIDTH)]
        pre = b_ref[...]
        for j in range(CONV_WIDTH):
            pre = pre + taps[j] * w_ref[j:j + 1, :]
        sg = _sigmoid(pre)
        dd = jnp.concatenate([d_ref[...], jnp.where(last, 0.0, dnext_ref[...])], axis=0)
        dpre = dd * (sg * (1.0 + pre * (1.0 - sg)))
        for j in range(CONV_WIDTH):
            acc_ref[j:j + 1, :] += _colsum(dpre[:tr] * taps[j][:tr])
        acc_ref[CONV_WIDTH:CONV_WIDTH + 1, :] += _colsum(dpre[:tr])
        du = dpre[:tr] * w_ref[CONV_WIDTH - 1:CONV_WIDTH, :]
        for j in range(CONV_WIDTH - 1):
            k = CONV_WIDTH - 1 - j
            du = du + pltpu.roll(dpre, tr + 8 - k, 0)[:tr] * w_ref[j:j + 1, :]
        du_ref[...] = du.astype(BF16)

    tile = pl.BlockSpec((tr, CONV_COLS), lambda i: (i, 0))
    after = pl.BlockSpec((8, CONV_COLS), lambda i: (jnp.minimum((i + 1) * (tr // 8), T // 8 - 1), 0))
    before = pl.BlockSpec((8, CONV_COLS), lambda i: (jnp.maximum(i * (tr // 8) - 1, 0), 0))
    return pl.pallas_call(
        body, name="conv_bwd", grid=(nt,),
        in_specs=[tile, before, after, pl.BlockSpec((CONV_WIDTH, CONV_COLS), lambda i: (0, 0)),
                  pl.BlockSpec((1, CONV_COLS), lambda i: (0, 0)), tile, after],
        out_specs=[tile, pl.BlockSpec((8, CONV_COLS), lambda i: (0, 0))],
        out_shape=[_sds((T, CONV_COLS), BF16), _sds((8, CONV_COLS), F32)],
        compiler_params=_params(("arbitrary",)))(pqk, pqk, pqk, w, b, dqk, dqk)


def _log_sigmoid(x):
    return jnp.minimum(x, 0.0) - jnp.log1p(jnp.exp(-jnp.abs(x)))


def _chunk_cumsum(x, axis):
    idx = lax.broadcasted_iota(jnp.int32, x.shape, axis) % MLSTM_CHUNK
    k = 1
    while k < MLSTM_CHUNK:
        x = x + jnp.where(idx >= k, pltpu.roll(x, k, axis), 0.0)
        k *= 2
    return x


def _chunk_rev_cumsum(x, axis):
    n = x.shape[axis]
    idx = lax.broadcasted_iota(jnp.int32, x.shape, axis) % MLSTM_CHUNK
    k = 1
    while k < MLSTM_CHUNK:
        x = x + jnp.where(idx < MLSTM_CHUNK - k, pltpu.roll(x, n - k, axis), 0.0)
        k *= 2
    return x


def _mlstm_gates(gc_ref, bc_ref, gr_ref, br_ref):
    gc = gc_ref[...] + bc_ref[...]
    gr = gr_ref[...] + br_ref[...]
    return gc, _chunk_cumsum(_log_sigmoid(gc), 0), gr, _chunk_cumsum(_log_sigmoid(gr), 1)


def _heads(ref, base=0):
    D = MLSTM_HEAD_DIM
    return jnp.stack([ref[:, base + D * h:base + D * h + D] for h in range(MLSTM_HEADS)])


def _mlstm_inputs(q_ref, k_ref, v_ref, gc, bc, gr, br):
    H = MLSTM_HEADS
    q, v = _heads(q_ref), _heads(v_ref)
    ks = _heads(k_ref) * (MLSTM_HEAD_DIM ** -0.5)
    return dict(
        q=q, ks=ks, qb=q.astype(BF16), kb=ks.astype(BF16), vb=v.astype(BF16),
        b_col=jnp.stack([bc[:, H + h:H + h + 1] for h in range(H)]),
        i_col=jnp.stack([gc[:, h:h + 1] for h in range(H)]),
        b_row=jnp.stack([br[H + h:H + h + 1, :] for h in range(H)]),
        i_row=jnp.stack([gr[h:h + 1, :] for h in range(H)]))


def _mlstm_head(f, c_prev, n_prev, m_prev):
    L = MLSTM_CHUNK
    q, qb = f["q"], f["qb"]
    t = lax.broadcasted_iota(jnp.int32, (1, 2 * L, 2 * L), 1)
    s = lax.broadcasted_iota(jnp.int32, (1, 2 * L, 2 * L), 2)
    mask = (t // L == s // L) & (s <= t)
    d = jnp.where(mask, f["b_col"] - f["b_row"] + f["i_row"], NEG_INF)
    row = lax.broadcasted_iota(jnp.int32, (1, 2 * L, 1), 1)
    inter = f["b_col"] + jnp.where(row < L, m_prev[0], m_prev[1])
    m_t = jnp.maximum(inter, jnp.max(d, axis=-1, keepdims=True))
    w_intra = jnp.exp(d - m_t)
    w_inter = jnp.exp(inter - m_t)
    sc = _bdot(qb, f["kb"], 2, 2) * w_intra
    qc = jnp.concatenate([_bdot(qb[:, :L], c_prev[0].astype(BF16), 2, 1),
                          _bdot(qb[:, L:], c_prev[1].astype(BF16), 2, 1)], axis=1)
    qn = jnp.concatenate([jnp.sum(q[:, :L] * n_prev[0], axis=-1, keepdims=True),
                          jnp.sum(q[:, L:] * n_prev[1], axis=-1, keepdims=True)], axis=1)
    num = _bdot(sc.astype(BF16), f["vb"], 2, 1) + w_inter * qc
    den = jnp.sum(sc, axis=-1, keepdims=True) + w_inter * qn
    return dict(f, w_intra=w_intra, w_inter=w_inter, sc=sc, qc=qc, qn=qn, num=num, den=den,
                floor=jnp.exp(-m_t))


def _mlstm_update(f, ch, c, n, m):
    L = MLSTM_CHUNK
    rows = slice(L * ch, L * ch + L)
    b_col = f["b_col"][:, rows]
    g_last = b_col[:, L - 1:L]
    a_col = g_last - b_col + f["i_col"][:, rows]
    m_new = jnp.maximum(g_last + m, jnp.max(a_col, axis=1, keepdims=True))
    decay = jnp.exp(g_last + m - m_new)
    e_a = jnp.exp(a_col - m_new)
    kw = f["ks"][:, rows] * e_a
    c_new = decay * c + _bdot_rows(kw.astype(BF16), f["vb"][:, rows])
    n_new = decay * n + jnp.sum(kw, axis=1, keepdims=True)
    return c_new, n_new, m_new, decay, e_a, kw


def _mlstm_specs(T, order):
    blk = lambda w, col: pl.BlockSpec((STEP_ROWS, w), lambda s: (order(s), col))
    return [blk(512, 0), blk(512, 1), blk(512, 0), blk(128, 0),
            pl.BlockSpec((1, 128), lambda s: (0, 0)),
            pl.BlockSpec((8, STEP_ROWS), lambda s: (0, order(s))),
            pl.BlockSpec((8, 128), lambda s: (0, 0))]


def _lanes(m):
    return jnp.broadcast_to(m, m.shape[:-1] + (128,))


def _mlstm_fwd(qk, pm, gcol, bcol, grow, brow):
    T = qk.shape[0]
    steps = T // STEP_ROWS
    H, D = MLSTM_HEADS, MLSTM_HEAD_DIM

    def body(q_ref, k_ref, v_ref, gc_ref, bc_ref, gr_ref, br_ref, h_ref, cs_ref, ns_ref, ms_ref,
             c_st, n_st, m_st):
        @pl.when(pl.program_id(0) == 0)
        def _():
            c_st[...] = jnp.zeros_like(c_st)
            n_st[...] = jnp.zeros_like(n_st)
            m_st[...] = jnp.zeros_like(m_st)

        f = _mlstm_inputs(q_ref, k_ref, v_ref, *_mlstm_gates(gc_ref, bc_ref, gr_ref, br_ref))
        c0, n0, m0 = c_st[...], n_st[...], m_st[:, :, 0:1]
        c1, n1, m1, _, _, _ = _mlstm_update(f, 0, c0, n0, m0)
        c2, n2, m2, _, _, _ = _mlstm_update(f, 1, c1, n1, m1)
        f = _mlstm_head(f, (c0, c1), (n0, n1), (m0, m1))
        h = f["num"] / jnp.maximum(jnp.abs(f["den"]), f["floor"])
        for hd in range(H):
            h_ref[:, D * hd:D * hd + D] = h[hd]
        cs_ref[0], cs_ref[1] = c0, c1
        ns_ref[0], ns_ref[1] = n0, n1
        ms_ref[0], ms_ref[1] = _lanes(m0), _lanes(m1)
        c_st[...], n_st[...], m_st[...] = c2, n2, _lanes(m2)

    vec = pl.BlockSpec((2, H, 1, 128), lambda s: (s, 0, 0, 0))
    return pl.pallas_call(
        body, name="mlstm_fwd", grid=(steps,), in_specs=_mlstm_specs(T, lambda s: s),
        out_specs=[pl.BlockSpec((STEP_ROWS, 512), lambda s: (s, 0)),
                   pl.BlockSpec((2, H, 128, 128), lambda s: (s, 0, 0, 0)), vec, vec],
        out_shape=[_sds((T, 512), F32), _sds((2 * steps, H, 128, 128), F32),
                   _sds((2 * steps, H, 1, 128), F32), _sds((2 * steps, H, 1, 128), F32)],
        scratch_shapes=[pltpu.VMEM((H, 128, 128), F32), pltpu.VMEM((H, 1, 128), F32),
                        pltpu.VMEM((H, 1, 128), F32)],
        compiler_params=_params(("arbitrary",)))(qk, qk, pm, gcol, bcol, grow, brow)


def _mlstm_bwd(qk, pm, gcol, bcol, grow, brow, cs, ns, ms, dh):
    T = qk.shape[0]
    steps = T // STEP_ROWS
    H, L, D = MLSTM_HEADS, MLSTM_CHUNK, MLSTM_HEAD_DIM
    rev = lambda s: steps - 1 - s

    def body(q_ref, k_ref, v_ref, gc_ref, bc_ref, gr_ref, br_ref, cs_ref, ns_ref, ms_ref, dh_ref,
             dqk_ref, dv_ref, dgc_ref, dgr_ref, dc_st, dn_st):
        @pl.when(pl.program_id(0) == 0)
        def _():
            dc_st[...] = jnp.zeros_like(dc_st)
            dn_st[...] = jnp.zeros_like(dn_st)

        f = _mlstm_inputs(q_ref, k_ref, v_ref, *_mlstm_gates(gc_ref, bc_ref, gr_ref, br_ref))
        c_prev = (cs_ref[0], cs_ref[1])
        n_prev = (ns_ref[0], ns_ref[1])
        m_prev = (ms_ref[0, :, :, 0:1], ms_ref[1, :, :, 0:1])
        f = _mlstm_head(f, c_prev, n_prev, m_prev)
        big = jnp.abs(f["den"]) > f["floor"]
        rden = 1.0 / jnp.where(big, jnp.abs(f["den"]), f["floor"])
        dnum = _heads(dh_ref) * rden
        hdh = jnp.sum(f["num"] * dnum, axis=-1, keepdims=True)
        dden = jnp.where(big, -hdh * rden * jnp.sign(f["den"]), 0.0)
        dnum_b = dnum.astype(BF16)
        dsc = _bdot(dnum_b, f["vb"], 2, 2) + dden
        g = dsc * f["sc"]
        dv = _bdot_rows(f["sc"].astype(BF16), dnum_b)
        dqk_ = (dsc * f["w_intra"]).astype(BF16)
        dq = _bdot(dqk_, f["kb"], 2, 1)
        dks = _bdot_rows(dqk_, f["qb"])
        wdn = f["w_inter"] * dnum
        wdn_b = wdn.astype(BF16)
        wdd = f["w_inter"] * dden
        u = jnp.sum(f["qc"] * wdn, axis=-1, keepdims=True) + wdd * f["qn"]
        dks_s, dv_s, z_s, dg_s = [None, None], [None, None], [None, None], [None, None]
        dcn, dnn = dc_st[...], dn_st[...]
        for ch in (1, 0):
            rows = slice(L * ch, L * ch + L)
            _, _, _, decay, e_a, kw = _mlstm_update(f, ch, c_prev[ch], n_prev[ch], m_prev[ch])
            dcn_b = dcn.astype(BF16)
            dkw = _bdot(f["vb"][:, rows], dcn_b, 2, 2) + dnn
            dks_s[ch] = e_a * dkw
            dv_s[ch] = _bdot(kw.astype(BF16), dcn_b, 2, 1)
            z_s[ch] = e_a * jnp.sum(f["ks"][:, rows] * dkw, axis=-1, keepdims=True)
            dg_s[ch] = jnp.sum(z_s[ch], axis=1, keepdims=True) + decay * (
                jnp.sum(c_prev[ch] * dcn, axis=(1, 2), keepdims=True)
                + jnp.sum(n_prev[ch] * dnn, axis=(1, 2), keepdims=True))
            dcn = decay * dcn + _bdot_rows(f["qb"][:, rows], wdn_b[:, rows])
            dnn = decay * dnn + jnp.sum(wdd[:, rows] * f["q"][:, rows], axis=1, keepdims=True)
        dc_st[...], dn_st[...] = dcn, dnn
        dq = dq + jnp.concatenate(
            [_bdot(wdn_b[:, :L], c_prev[0].astype(BF16), 2, 2) + wdd[:, :L] * n_prev[0],
             _bdot(wdn_b[:, L:], c_prev[1].astype(BF16), 2, 2) + wdd[:, L:] * n_prev[1]], axis=1)
        dks = (dks + jnp.concatenate(dks_s, axis=1)) * (D ** -0.5)
        dv = dv + jnp.concatenate(dv_s, axis=1)
        z = jnp.concatenate(z_s, axis=1)
        row = lax.broadcasted_iota(jnp.int32, (1, STEP_ROWS, 1), 1)
        dg_col = jnp.where(row == L - 1, dg_s[0], 0.0) + jnp.where(row == 2 * L - 1, dg_s[1], 0.0)
        db_col = jnp.sum(g, axis=-1, keepdims=True) + u - z + dg_col
        g_row = jnp.sum(g, axis=1, keepdims=True)
        lane = lax.broadcasted_iota(jnp.int32, (STEP_ROWS, 128), 1)
        sub = lax.broadcasted_iota(jnp.int32, (8, STEP_ROWS), 0)
        dgc = jnp.zeros((STEP_ROWS, 128), F32)
        dgr = jnp.zeros((8, STEP_ROWS), F32)
        for hd in range(H):
            dgc = dgc + jnp.where(lane == hd, z[hd], 0.0) + jnp.where(lane == H + hd, db_col[hd], 0.0)
            dgr = dgr + jnp.where(sub == hd, g_row[hd], 0.0) - jnp.where(sub == H + hd, g_row[hd], 0.0)
            dqk_ref[:, D * hd:D * hd + D] = dq[hd]
            dqk_ref[:, H * D + D * hd:H * D + D * hd + D] = dks[hd]
            dv_ref[:, D * hd:D * hd + D] = dv[hd].astype(BF16)
        dgc_ref[...] = dgc
        dgr_ref[...] = dgr

    return pl.pallas_call(
        body, name="mlstm_bwd", grid=(steps,),
        in_specs=_mlstm_specs(T, rev) + [
            pl.BlockSpec((2, H, 128, 128), lambda s: (rev(s), 0, 0, 0)),
            pl.BlockSpec((2, H, 1, 128), lambda s: (rev(s), 0, 0, 0)),
            pl.BlockSpec((2, H, 1, 128), lambda s: (rev(s), 0, 0, 0)),
            pl.BlockSpec((STEP_ROWS, 512), lambda s: (rev(s), 0))],
        out_specs=[pl.BlockSpec((STEP_ROWS, 1024), lambda s: (rev(s), 0)),
                   pl.BlockSpec((STEP_ROWS, 512), lambda s: (rev(s), 0)),
                   pl.BlockSpec((STEP_ROWS, 128), lambda s: (rev(s), 0)),
                   pl.BlockSpec((8, STEP_ROWS), lambda s: (0, rev(s)))],
        out_shape=[_sds((T, 1024), F32), _sds((T, 512), BF16), _sds((T, 128), F32), _sds((8, T), F32)],
        scratch_shapes=[pltpu.VMEM((H, 128, 128), F32), pltpu.VMEM((H, 1, 128), F32)],
        compiler_params=_params(("arbitrary",)))(qk, qk, pm, gcol, bcol, grow, brow, cs, ns, ms, dh)


def _rows_to_lanes(x):
    eye = (lax.broadcasted_iota(jnp.int32, (8, 128), 0)
           == lax.broadcasted_iota(jnp.int32, (8, 128), 1)).astype(BF16)
    out, rest = None, x
    for _ in range(3):
        piece = rest.astype(BF16)
        rest = rest - piece.astype(F32)
        t = _dot(piece, eye, 0, 0)
        out = t if out is None else out + t
    return out


def _gate_bwd(dgc, dgr, gcol, bcol):
    T = dgc.shape[0]
    tr = min(ROW_TILE, T)

    def body(a_ref, b_ref, g_ref, bias_ref, o_ref, acc_ref):
        i = pl.program_id(0)

        @pl.when(i == 0)
        def _():
            acc_ref[...] = jnp.zeros_like(acc_ref)

        d = a_ref[...] + _rows_to_lanes(b_ref[:, pl.ds(pl.multiple_of(i * tr, 128), tr)])
        lane = lax.broadcasted_iota(jnp.int32, d.shape, 1)
        is_f = (lane >= MLSTM_HEADS) & (lane < 2 * MLSTM_HEADS)
        dlogf = _chunk_rev_cumsum(jnp.where(is_f, d, 0.0), 0)
        out = jnp.where(is_f, dlogf * _sigmoid(-(g_ref[...] + bias_ref[...])), d)
        o_ref[...] = out.astype(BF16)
        acc_ref[0:1, :] += _colsum(out)

    return _rows("gate_bwd", body, [dgc, dgr, gcol, bcol],
                 [_sds((T, 128), BF16), _sds((8, 128), F32)], T, tr=tr)


def _head_norm(h, mu_axis=-1):
    mu = jnp.mean(h, axis=-1, keepdims=True)
    hc = h - mu
    r = lax.rsqrt(jnp.mean(hc * hc, axis=-1, keepdims=True) + NORM_EPS)
    return hc * r, r


def _mlstm_out(hm, pm, w):
    T = hm.shape[0]
    D = MLSTM_HEAD_DIM

    def body(h_ref, o_ref, w_ref, y_ref):
        for hd in range(MLSTM_HEADS):
            cols = slice(D * hd, D * hd + D)
            hn, _ = _head_norm(h_ref[:, cols])
            y_ref[:, cols] = (_sigmoid(o_ref[:, cols].astype(F32)) * hn * w_ref[:, cols]).astype(BF16)

    tr = min(ROW_TILE, T)
    return pl.pallas_call(
        body, name="mlstm_out", grid=(T // tr,),
        in_specs=[pl.BlockSpec((tr, 512), lambda i: (i, 0)), pl.BlockSpec((tr, 512), lambda i: (i, 1)),
                  pl.BlockSpec((1, 512), lambda i: (0, 0))],
        out_specs=pl.BlockSpec((tr, 512), lambda i: (i, 0)), out_shape=_sds((T, 512), BF16),
        compiler_params=_params(("parallel",)))(hm, pm, w)


def _mlstm_out_bwd_rows(ps, es):
    hm, vo, w_all = es
    D, width = MLSTM_HEAD_DIM, MLSTM_HEADS * MLSTM_HEAD_DIM
    dhs, dos, dws = [], [], []
    for hd in range(MLSTM_HEADS):
        cols = slice(D * hd, D * hd + D)
        hn, r = _head_norm(hm[:, cols])
        sg = _sigmoid(vo[:, width + D * hd:width + D * hd + D].astype(F32))
        dy, w = ps[0][:, cols], w_all[:, cols]
        dos.append(dy * hn * w * sg * (1.0 - sg))
        dyn = dy * sg
        dws.append(_colsum(dyn * hn))
        dhn = dyn * w
        dhs.append(r * (dhn - jnp.mean(dhn, axis=-1, keepdims=True)
                        - hn * jnp.mean(dhn * hn, axis=-1, keepdims=True)))
    cat = lambda parts: jnp.concatenate(parts, axis=1)
    return [cat(dhs), cat(dos)], [_acc_rows([cat(dws)])]


ADAM_TILE_ELEMS = 256 * 1024


def _adamw(name, w, g, m, v):
    R, C = w.shape
    fits = [t for t in range(8, R + 1, 8) if R % t == 0 and t * C <= ADAM_TILE_ELEMS]
    if fits or R * C <= ADAM_TILE_ELEMS:
        tr = fits[-1] if fits else R
        spec, grid = pl.BlockSpec((tr, C), lambda i: (i, 0)), (R // tr,)
    else:
        spec, grid = pl.BlockSpec((R, 128), lambda i: (0, i)), (C // 128,)
    c1 = 1.0 - ADAM_B1 ** ADAM_STEP
    c2 = 1.0 - ADAM_B2 ** ADAM_STEP

    def body(w_ref, g_ref, m_ref, v_ref, d_ref, mo_ref, vo_ref):
        g = g_ref[...]
        m = ADAM_B1 * m_ref[...] + (1.0 - ADAM_B1) * g
        v = ADAM_B2 * v_ref[...] + (1.0 - ADAM_B2) * (g * g)
        mo_ref[...] = m
        vo_ref[...] = v
        d_ref[...] = -ADAM_LR * ((m / c1) / (jnp.sqrt(v / c2) + ADAM_EPS) + ADAM_WD * w_ref[...])

    return pl.pallas_call(
        body, name=name, grid=grid, in_specs=[spec] * 4, out_specs=[spec] * 3,
        out_shape=[_sds((R, C), F32)] * 3, compiler_params=_params(("parallel",)))(w, g, m, v)


def _place():
    return lax.axis_index("x"), lax.axis_index("y"), lax.axis_index("c")


def _all_gather8(name, blk, space):
    m, n = blk.shape

    def body(x_ref, out_ref, send_sems, recv_sems, local_sem):
        x, y, c = _place()
        me, sibling = (x, y, c), (x, y, 1 - c)
        chips = [(1 - x, y), (x, 1 - y), (1 - x, 1 - y)]

        def rows(px, py, pc):
            return out_ref.at[pl.ds((4 * px + 2 * py + pc) * m, m), :]

        def copy(k, block, to, src=None):
            return pltpu.make_async_remote_copy(
                src_ref=rows(*block) if src is None else src, dst_ref=rows(*block),
                send_sem=send_sems.at[k], recv_sem=recv_sems.at[k],
                device_id=to, device_id_type=MESH)

        mine = pltpu.make_async_copy(x_ref, rows(*me), local_sem)
        mine.start()
        first = [copy(0, me, sibling, src=x_ref)]
        first += [copy(1 + j, me, (*chip, c), src=x_ref) for j, chip in enumerate(chips)]
        for cp in first:
            cp.start()
        passed = [copy(4 + j, (*chip, c), sibling) for j, chip in enumerate(chips)]
        for j, chip in enumerate(chips):
            copy(1 + j, (*chip, c), me).wait_recv()
            passed[j].start()
        copy(0, sibling, me).wait_recv()
        for j, chip in enumerate(chips):
            copy(4 + j, (*chip, 1 - c), me).wait_recv()
        for cp in first + passed:
            cp.wait_send()
        mine.wait()

    return pl.pallas_call(
        body, name=name, out_shape=_sds((8 * m, n), blk.dtype),
        in_specs=[pl.BlockSpec(memory_space=space)], out_specs=pl.BlockSpec(memory_space=space),
        scratch_shapes=[pltpu.SemaphoreType.DMA((7,)), pltpu.SemaphoreType.DMA((7,)),
                        pltpu.SemaphoreType.DMA],
        compiler_params=pltpu.CompilerParams(vmem_limit_bytes=VMEM_LIMIT))(blk)


def _hbm_specs(n):
    return [pl.BlockSpec(memory_space=pl.ANY)] * n


def _swap_halves_sibling(name, srcs):
    nw = len(srcs)

    def body(*refs):
        src_refs, dst_refs, send_sems, recv_sems = refs[:nw], refs[nw:2 * nw], refs[2 * nw], refs[2 * nw + 1]
        x, y, c = _place()
        cps = [pltpu.make_async_remote_copy(
            src_ref=src_refs[w].at[pl.ds(0, 4), 1 - c], dst_ref=dst_refs[w],
            send_sem=send_sems.at[w], recv_sem=recv_sems.at[w], device_id=(x, y, 1 - c),
            device_id_type=MESH) for w in range(nw)]
        for cp in cps:
            cp.start()
        for cp in cps:
            cp.wait()

    return pl.pallas_call(
        body, name=name, out_shape=[_sds(s.shape[:1] + s.shape[2:], s.dtype) for s in srcs],
        in_specs=_hbm_specs(nw), out_specs=_hbm_specs(nw),
        scratch_shapes=[pltpu.SemaphoreType.DMA((nw,)), pltpu.SemaphoreType.DMA((nw,))])(*srcs)


def _split_start(name, srcs, lands, copies, per_array, after):
    nw = len(srcs)

    def body(*refs):
        send_sems, recv_sems, token = refs[2 * nw + 1], refs[2 * nw + 2], refs[-1]
        for w in range(nw):
            for k, (s, d, dev) in enumerate(copies(refs[w], refs[nw + w], *_place())):
                pltpu.make_async_remote_copy(
                    src_ref=s, dst_ref=d, send_sem=send_sems.at[w * per_array + k],
                    recv_sem=recv_sems.at[w * per_array + k], device_id=dev, device_id_type=MESH).start()
        token[...] = jnp.zeros_like(token)

    hbm, sem = pl.BlockSpec(memory_space=pltpu.HBM), pl.BlockSpec(memory_space=pltpu.SEMAPHORE)
    arrays = list(srcs) + list(lands)
    out = pl.pallas_call(
        body, name=name,
        out_shape=(pltpu.SemaphoreType.DMA((nw * per_array,)), pltpu.SemaphoreType.DMA((nw * per_array,)),
                   *[pltpu.HBM(a.shape, a.dtype) for a in arrays], _sds((8, 128), F32)),
        in_specs=[hbm] * (2 * nw) + [pl.BlockSpec(memory_space=pl.ANY)],
        out_specs=(sem, sem, *[hbm] * (2 * nw), pl.BlockSpec(memory_space=pltpu.VMEM)),
        input_output_aliases={i: 2 + i for i in range(2 * nw)},
        compiler_params=pltpu.CompilerParams(has_side_effects=pltpu.SideEffectType.DATAFLOW_SIDE_EFFECTING))(
            *[pltpu.with_memory_space_constraint(a, pltpu.HBM) for a in arrays], after)
    return out[0], out[1], out[2:2 + nw], out[2 + nw:2 + 2 * nw], out[-1]


def _split_wait(name, started, after, waits, per_array):
    send_sems, recv_sems, srcs, lands, _ = started
    nw = len(srcs)

    def body(*refs):
        send_sems, recv_sems = refs[2 * nw], refs[2 * nw + 1]
        x, y, c = _place()
        for w in range(nw):
            for k, (s, d) in enumerate(waits(refs[w], refs[nw + w], x, y, c)):
                cp = pltpu.make_async_remote_copy(
                    src_ref=s, dst_ref=d, send_sem=send_sems.at[w * per_array + k],
                    recv_sem=recv_sems.at[w * per_array + k], device_id=(x, y, 1 - c),
                    device_id_type=MESH)
                cp.wait_send()
                cp.wait_recv()

    hbm, sem = pl.BlockSpec(memory_space=pltpu.HBM), pl.BlockSpec(memory_space=pltpu.SEMAPHORE)
    arrays = list(srcs) + list(lands)
    out = pl.pallas_call(
        body, name=name, out_shape=tuple(pltpu.HBM(a.shape, a.dtype) for a in arrays),
        in_specs=[hbm] * (2 * nw) + [sem, sem, pl.BlockSpec(memory_space=pl.ANY)],
        out_specs=tuple([hbm] * (2 * nw)), input_output_aliases={i: i for i in range(2 * nw)},
        compiler_params=pltpu.CompilerParams(has_side_effects=pltpu.SideEffectType.DATAFLOW_SIDE_EFFECTING))(
            *arrays, send_sems, recv_sems, after)
    return list(out[nw:])


def _other_chips(x, y):
    return [(1 - x, y), (x, 1 - y), (1 - x, 1 - y)]


def _gather_sends(src_ref, land_ref, x, y, c):
    to = land_ref.at[2 * x + y, c]
    return [(src_ref, to, (x, y, 1 - c))] + [(src_ref, to, (px, py, c)) for px, py in _other_chips(x, y)]


def _gather_lands(src_ref, land_ref, x, y, c):
    return [(src_ref, land_ref.at[2 * x + y, 1 - c])] + [
        (src_ref, land_ref.at[2 * px + py, c]) for px, py in _other_chips(x, y)]


def _gather_sends_all(src_ref, land_ref, x, y, c):
    to = land_ref.at[2 * x + y, c]
    return [(src_ref, to, (x, y, 1 - c))] + [
        (src_ref, to, (px, py, pc)) for px, py in _other_chips(x, y) for pc in (c, 1 - c)]


def _gather_lands_all(src_ref, land_ref, x, y, c):
    return [(src_ref, land_ref.at[2 * x + y, 1 - c])] + [
        (src_ref, land_ref.at[2 * px + py, pc]) for px, py in _other_chips(x, y) for pc in (c, 1 - c)]


def _scatter_sends(src_ref, land_ref, x, y, c):
    return [(src_ref.at[2 * px + py], land_ref.at[2 * x + y], (px, py, c)) for px, py in _other_chips(x, y)]


def _scatter_lands(src_ref, land_ref, x, y, c):
    return [(src_ref.at[2 * x + y], land_ref.at[2 * px + py]) for px, py in _other_chips(x, y)]


def _forward_sibling(name, lands):
    nw = len(lands)

    def body(*refs):
        land_refs, out_refs, send_sems, recv_sems = refs[:nw], refs[nw:2 * nw], refs[2 * nw], refs[2 * nw + 1]
        x, y, c = _place()
        cps = []
        for w in range(nw):
            cps += [pltpu.make_async_remote_copy(
                src_ref=land_refs[w].at[2 * px + py, c], dst_ref=out_refs[w].at[2 * px + py, c],
                send_sem=send_sems.at[w, j], recv_sem=recv_sems.at[w, j], device_id=(x, y, 1 - c),
                device_id_type=MESH) for j, (px, py) in enumerate(_other_chips(x, y))]
        for cp in cps:
            cp.start()
        for w in range(nw):
            for j, (px, py) in enumerate(_other_chips(x, y)):
                slot = out_refs[w].at[2 * px + py, 1 - c]
                pltpu.make_async_remote_copy(src_ref=slot, dst_ref=slot, send_sem=send_sems.at[w, j],
                                             recv_sem=recv_sems.at[w, j], device_id=(x, y, 1 - c),
                                             device_id_type=MESH).wait_recv()
        for cp in cps:
            cp.wait_send()

    return pl.pallas_call(
        body, name=name, out_shape=[_sds(a.shape, a.dtype) for a in lands],
        in_specs=_hbm_specs(nw), out_specs=_hbm_specs(nw), input_output_aliases={i: i for i in range(nw)},
        scratch_shapes=[pltpu.SemaphoreType.DMA((nw, 3)), pltpu.SemaphoreType.DMA((nw, 3))])(*lands)


def _share_halves(name, halves):
    nw = len(halves)

    def body(*refs):
        in_refs, out_refs, send_sems, recv_sems = refs[:nw], refs[nw:2 * nw], refs[2 * nw], refs[2 * nw + 1]
        x, y, c = _place()
        cps = [pltpu.make_async_remote_copy(
            src_ref=in_refs[w].at[c], dst_ref=out_refs[w].at[c], send_sem=send_sems.at[w],
            recv_sem=recv_sems.at[w], device_id=(x, y, 1 - c), device_id_type=MESH) for w in range(nw)]
        for cp in cps:
            cp.start()
        for w in range(nw):
            slot = out_refs[w].at[1 - c]
            pltpu.make_async_remote_copy(src_ref=slot, dst_ref=slot, send_sem=send_sems.at[w],
                                         recv_sem=recv_sems.at[w], device_id=(x, y, 1 - c),
                                         device_id_type=MESH).wait_recv()
        for cp in cps:
            cp.wait_send()

    return pl.pallas_call(
        body, name=name, out_shape=[_sds(a.shape, a.dtype) for a in halves],
        in_specs=_hbm_specs(nw), out_specs=_hbm_specs(nw), input_output_aliases={i: i for i in range(nw)},
        scratch_shapes=[pltpu.SemaphoreType.DMA((nw,)), pltpu.SemaphoreType.DMA((nw,))])(*halves)


def _place_blocks(name, blks, place):
    nw = len(blks)

    def body(p_ref, *refs):
        for b_ref, o_ref in zip(refs[:nw], refs[nw:]):
            o_ref[...] = b_ref[...]

    return pl.pallas_call(
        body, name=name,
        grid_spec=pltpu.PrefetchScalarGridSpec(
            num_scalar_prefetch=1, grid=(1,),
            in_specs=[pl.BlockSpec(b.shape, lambda i, p: (0, 0)) for b in blks],
            out_specs=[pl.BlockSpec((None, None) + b.shape, lambda i, p: (p[0], p[1], 0, 0)) for b in blks]),
        out_shape=[_sds((4, 2) + b.shape, b.dtype) for b in blks],
        compiler_params=_params(("arbitrary",)))(place, *blks)


def _pair_sum(name, fulls, gots, place):
    nw = len(fulls)

    def body(p_ref, *refs):
        s = pl.program_id(0)
        for a_ref, b_ref, o_ref, l_ref in zip(refs[:nw], refs[nw:2 * nw], refs[2 * nw:3 * nw], refs[3 * nw:]):
            o_ref[...] = (a_ref[...].astype(F32) + b_ref[...].astype(F32)).astype(o_ref.dtype)

            @pl.when(s == p_ref[0])
            def _():
                l_ref[...] = o_ref[...]

    slab = lambda a: pl.BlockSpec((None,) + a.shape[1:], lambda s, p: (s, 0, 0))
    mine = lambda a: pl.BlockSpec((None,) + a.shape[1:], lambda s, p: (p[0], 0, 0))
    out = pl.pallas_call(
        body, name=name,
        grid_spec=pltpu.PrefetchScalarGridSpec(
            num_scalar_prefetch=1, grid=(4,),
            in_specs=[pl.BlockSpec((None, None) + a.shape[2:], lambda s, p: (s, p[1], 0, 0)) for a in fulls]
            + [slab(b) for b in gots],
            out_specs=[slab(b) for b in gots] + [mine(b) for b in gots]),
        out_shape=[_sds(b.shape, BF16) for b in gots] * 2,
        compiler_params=_params(("arbitrary",)))(place, *fulls, *gots)
    return out[:nw], out[nw:]


def _sum4(name, arrs, place):
    nw = len(arrs)

    def body(p_ref, *refs):
        for a_ref, o_ref in zip(refs[:nw], refs[nw:]):
            acc = a_ref[0].astype(F32)
            for s in range(1, 4):
                acc = acc + a_ref[s].astype(F32)
            o_ref[...] = acc

    return pl.pallas_call(
        body, name=name,
        grid_spec=pltpu.PrefetchScalarGridSpec(
            num_scalar_prefetch=1, grid=(1,),
            in_specs=[pl.BlockSpec(a.shape, lambda i, p: (0, 0, 0)) for a in arrs],
            out_specs=[pl.BlockSpec((None,) + a.shape[1:], lambda i, p: (p[1], 0, 0)) for a in arrs]),
        out_shape=[_sds((2,) + a.shape[1:], F32) for a in arrs],
        compiler_params=_params(("arbitrary",)))(place, *arrs)


def _small_update(gathered, w, m, v):
    n = w.shape[1]
    tn = 2048
    c1 = 1.0 - ADAM_B1 ** ADAM_STEP
    c2 = 1.0 - ADAM_B2 ** ADAM_STEP

    def body(g_ref, w_ref, m_ref, v_ref, go_ref, d_ref, mo_ref, vo_ref):
        g = g_ref[0:1, :]
        for d in range(1, 8):
            g = g + g_ref[d:d + 1, :]
        go_ref[...] = g
        m = ADAM_B1 * m_ref[...] + (1.0 - ADAM_B1) * g
        v = ADAM_B2 * v_ref[...] + (1.0 - ADAM_B2) * (g * g)
        mo_ref[...] = m
        vo_ref[...] = v
        d_ref[...] = -ADAM_LR * ((m / c1) / (jnp.sqrt(v / c2) + ADAM_EPS) + ADAM_WD * w_ref[...])

    row = pl.BlockSpec((1, tn), lambda i: (0, i))
    return pl.pallas_call(
        body, name="small_update", grid=(n // tn,),
        in_specs=[pl.BlockSpec((8, tn), lambda i: (0, i)), row, row, row], out_specs=[row] * 4,
        out_shape=[_sds((1, n), F32)] * 4, compiler_params=_params(("parallel",)))(gathered, w, m, v)


def _swiglu(ps, es):
    g, u = ps
    return g * _sigmoid(g) * u, g, u


def _swiglu_bwd(ps, es):
    g, u = es[0].astype(F32), es[1].astype(F32)
    sg = _sigmoid(g)
    return ps[0] * u * (sg * (1.0 + g * (1.0 - sg))), ps[0] * (g * sg)


def _merge(ps, es):
    ga, gm = [e.astype(F32) for e in es]
    return _sigmoid(ga) * ps[0] + _sigmoid(gm) * ps[1], ps[0], ps[1]


def _merge_bwd(ps, es):
    a, b, ga, gm = [e.astype(F32) for e in es]
    sa, sm = _sigmoid(ga), _sigmoid(gm)
    dm = ps[0]
    return dm * sa, dm * sm, dm * a * (sa * (1.0 - sa)), dm * b * (sm * (1.0 - sm))


W_IN_PIECES = (("q", 512), ("kv", 256), ("mqk", 1024), ("mv", 512), ("mo", 512), ("if", 8),
               ("ga", 1024), ("gm", 1024))


def _local_step(x, tgt, pos_col, mod, sp, in_weights, late_weights, ffn_grads, mixer_grads):
    sh_m, sc_m, gate_m, sh_f, sc_f, gate_f = mod
    inv = ROPE_THETA ** (-2.0 * jnp.arange(HEAD_DIM // 2, dtype=F32) / HEAD_DIM)
    cos, sin = _rope_tables(pos_col, jnp.tile(inv, 4).reshape(1, 128))
    W = dict(in_weights(cos))
    h, pa, pqk, pvo, pif, pg = _proj_in(x, sp["g_pre_mix"], sc_m, sh_m, [
        (W["q+kv"], F32, 256), (W["mqk"], F32, 512), (W["mv+mo"], BF16, 512), (W["if"], F32, 128),
        (W["ga+gm"], BF16, 512)])
    ya = _attn_fwd(pa, cos, sin, sp["sinks"])
    qk = _conv_fwd(pqk, sp["conv_w"], sp["conv_b"])
    bcol = jnp.pad(sp["b_if"], ((0, 0), (0, 120)))
    brow = jnp.broadcast_to(sp["b_if"].reshape(8, 1), (8, 128))
    grow = pif[:, :8].T
    hm, cs, ns, ms = _mlstm_fwd(qk, pvo, pif, bcol, grow, brow)
    ym = _mlstm_out(hm, pvo, sp["norm_w"])
    W.update(late_weights(ym))
    w_fg, w_fu, w_fd = W["fg"], W["fu"], W["fd"]
    merged, br_a, br_m = _mm("branches", [[(ya, W["ba"])], [(ym, W["bm"])]],
                             [(pg, 0), (pg, 1)], _merge, [BF16, BF16, BF16], cn=512, nt=True)
    wide, narrow = (D_MODEL, F32), (D_MODEL, BF16)
    mix, x1, h2 = _mm_rows("mix_out", [[(merged, W["out"])]],
                           [x, gate_m, sp["g_post_mix"], sp["g_pre_ffn"], sc_f, sh_f],
                           _res_norm_rows, [wide, wide, narrow], [], cn=512)
    act, gt, up = _mm("ffn_in", [[(h2, w_fg)], [(h2, w_fu)]], [], _swiglu, [BF16] * 3,
                      cn=256, nt=True)
    dy, dff, acc_l, loss = _mm_rows("ffn_down", [[(act, w_fd)]], [x1, tgt, gate_f, sp["g_post_ffn"]],
                                    _final_loss_rows, [wide, narrow], [(8, D_MODEL), (1, 128)], cn=512)

    G = {}
    dgt, dup = _mm("ffn_down_bwd", [[(dff, w_fd)]], [gt, up], _swiglu_bwd, [BF16, BF16],
                   cn=256, nt=True)
    g_fd = _mm_tn("dw_ffn_down", act, dff, BF16, 1408, 512)
    g_fg = _mm_tn("dw_ffn_gate", dgt, h2, BF16, 1408, 1024)
    g_fu = _mm_tn("dw_ffn_up", dup, h2, BF16, 1408, 1024)
    tie = ffn_grads(g_fg, g_fu, g_fd)
    dx1, dmix, acc_r = _mm_rows(
        "ffn_in_bwd", [[(dgt, w_fg), (dup, w_fu)]],
        [x1, mix, dy, sc_f + tie, gate_m, sp["g_pre_ffn"], sp["g_post_mix"]],
        _res_norm_bwd_rows, [wide, narrow], [(8, D_MODEL)], cn=512, tm=256)
    d_a, d_m, dga, dgm = _mm("mix_out_bwd", [[(dmix, W["out"])]],
                             [br_a, br_m, (pg, 0), (pg, 1)], _merge_bwd,
                             [BF16] * 4, cn=512, nt=True)
    G["out"] = _mm_tn("dw_out", merged, dmix, BF16, 1024, 512)
    dya, = _mm("branch_attn_bwd", [[(d_a, W["ba"])]], [], _first, [F32], cn=512)
    heads = MLSTM_HEADS * MLSTM_HEAD_DIM
    dhm, do_m, acc_n = _mm_rows("branch_mlstm_bwd", [[(d_m, W["bm"])]], [hm, pvo, sp["norm_w"]],
                                _mlstm_out_bwd_rows, [(heads, F32), (heads, BF16)], [(8, heads)], cn=512)
    G["ba"] = _mm_tn("dw_branch_attn", d_a, ya, BF16, 1024, 512)
    G["bm"] = _mm_tn("dw_branch_mlstm", d_m, ym, BF16, 1024, 512)
    dqk, dv_m, dgc, dgr = _mlstm_bwd(qk, pvo, pif, bcol, grow, brow, cs, ns, ms, dhm)
    dif, acc_g = _gate_bwd(dgc, dgr, pif, bcol)
    du, acc_c = _conv_bwd(pqk, sp["conv_w"], sp["conv_b"], dqk)
    dq_a, dkv, dsink = _attn_bwd(pa, cos, sin, sp["sinks"], dya)
    dproj = {"q": dq_a, "kv": dkv, "mqk": du, "mv": dv_m, "mo": do_m, "if": dif, "ga": dga, "gm": dgm}
    for k, _ in W_IN_PIECES:
        G[k] = _mm_tn("dw_in_" + k, dproj[k], h, BF16, dproj[k].shape[1], 1024)
    w_tied = dict(W, **{"if": W["if"] + mixer_grads(G).astype(BF16)})
    dx, acc_p = _mm_rows("proj_bwd", [[(dproj[k], w_tied[k]) for k, _ in W_IN_PIECES]],
                         [x, dx1, sp["g_pre_mix"], sc_m], _pre_norm_bwd_rows, [wide], [(8, D_MODEL)], cn=512)

    small = {
        "mod": jnp.concatenate([acc_p[1], acc_p[0], acc_r[3], acc_r[1], acc_r[0], acc_l[0]]),
        "g_pre_mix": acc_p[2], "g_post_mix": acc_r[4], "b_if": acc_g[0, :8],
        "conv_w": acc_c[:CONV_WIDTH].reshape(-1), "conv_b": acc_c[CONV_WIDTH],
        "sinks": dsink[:, 0], "norm_w": acc_n[0], "g_pre_ffn": acc_r[2], "g_post_ffn": acc_l[1]}
    return loss, dx, small


IN_WIDTH = sum(n for _, n in W_IN_PIECES)
IN_SHARD = IN_WIDTH // 4
IN_SHARD_PAD = -(-IN_SHARD // 32) * 32


def _split_w_in(w_in_t):
    out, off, start = {}, 0, {}
    for k, n in W_IN_PIECES:
        out[k], start[k] = w_in_t[off:off + n], off
        off += n
    out["if"] = jnp.pad(out["if"], ((0, 120), (0, 0)))
    for name, first, last in (("q+kv", "q", "kv"), ("mv+mo", "mv", "mo"), ("ga+gm", "ga", "gm")):
        out[name] = w_in_t[start[first]:start[last] + out[last].shape[0]]
    return out


def _halves(a):
    return a.reshape(4, 2, a.shape[0] // 8, a.shape[1])


SMALL = (("b_ada", 6144), ("g_pre_mix", 1024), ("g_post_mix", 1024), ("b_if", 128), ("conv_w", 4096),
         ("conv_b", 1024), ("sinks", 128), ("norm_w", 512), ("g_pre_ffn", 1024), ("g_post_ffn", 1024))
SMALL_LEN = 8 * 2048


def _pack_small(vals):
    parts = []
    for k, n in SMALL:
        v = vals[k].reshape(-1)
        parts.append(jnp.pad(v, (0, n - v.shape[0])))
    flat = jnp.concatenate(parts)
    return jnp.pad(flat, (0, SMALL_LEN - flat.shape[0]))


def _unpack_small(flat, shapes):
    out, off = {}, 0
    for k, n in SMALL:
        size = 1
        for d in shapes[k]:
            size *= d
        out[k] = flat[off:off + size].reshape(shapes[k])
        off += n
    return out


def kernel(x, c, positions, w_ada, b_ada, g_pre_mix, g_post_mix, w_in, b_if, conv_w, conv_b, attn_sinks, mlstm_norm_w, w_branch_attn, w_branch_mlstm, w_out, g_pre_ffn, g_post_ffn, w_ffn_gate, w_ffn_up, w_ffn_down, loss_target, m_w_ada, m_b_ada, m_g_pre_mix, m_g_post_mix, m_w_in, m_b_if, m_conv_w, m_conv_b, m_attn_sinks, m_mlstm_norm_w, m_w_branch_attn, m_w_branch_mlstm, m_w_out, m_g_pre_ffn, m_g_post_ffn, m_w_ffn_gate, m_w_ffn_up, m_w_ffn_down, v_w_ada, v_b_ada, v_g_pre_mix, v_g_post_mix, v_w_in, v_b_if, v_conv_w, v_conv_b, v_attn_sinks, v_mlstm_norm_w, v_w_branch_attn, v_w_branch_mlstm, v_w_out, v_g_pre_ffn, v_g_post_ffn, v_w_ffn_gate, v_w_ffn_up, v_w_ffn_down):
    xi, yi, ci = _place()
    chip = 2 * xi + yi
    dev = 2 * chip + ci
    T = x.shape[1]
    ada_cols = w_ada.shape[2]

    place = jnp.stack([chip, ci]).astype(jnp.int32)

    def my_half(a):
        n = a.shape[0] // 2
        return lax.dynamic_slice_in_dim(a, ci * n, n, axis=0).astype(BF16)

    blk = jnp.concatenate([c.reshape(-1), conv_w.reshape(-1)]).reshape(8, 256)
    got = _all_gather8("gather_cond", blk, pltpu.VMEM).reshape(8, 2048)
    c_all = got[:, :D_MODEL].astype(BF16)
    conv_full = got[::2, D_MODEL:].reshape(4, CONV_WIDTH, -1).transpose(1, 0, 2).reshape(CONV_WIDTH, -1)

    b_sh = lax.dynamic_slice_in_dim(b_ada, chip * ada_cols, ada_cols, axis=1)
    mod_part, = _mm("ada_mod", [[(c_all, w_ada[0].astype(BF16))]], [b_sh],
                    lambda ps, es: (ps[0] + es[0],), [F32], cn=512, tm=8)
    mod_all = _all_gather8("gather_mod", mod_part, pltpu.VMEM).reshape(4, 2, 8, ada_cols)[:, 0]
    mod = lax.dynamic_index_in_dim(mod_all, dev, axis=1, keepdims=False).reshape(6, 1, D_MODEL)

    def gather_start(name, blks, after, sends, copies):
        return _split_start(name + "_start", blks, _place_blocks(name + "_place", blks, place),
                            sends, copies, after)

    w_in_t = jnp.pad(w_in[0].T, ((0, IN_SHARD_PAD - IN_SHARD), (0, 0)))
    in_started = gather_start("in_gather", [my_half(w_in_t)], mod, _gather_sends, 4)
    late_keys = ("fg", "fu", "fd", "out", "ba", "bm")
    late_started = gather_start(
        "late_gather",
        [my_half(w_ffn_gate[0].T), my_half(w_ffn_up[0].T), my_half(w_ffn_down[0]), my_half(w_out[0]),
         my_half(w_branch_attn[0].T), my_half(w_branch_mlstm[0].T)], in_started[4], _gather_sends_all, 7)
    mod = mod + (in_started[4][0, 0] + late_started[4][0, 0])

    def in_weights(after):
        g_in, = _forward_sibling("in_gather_forward",
                                 _split_wait("in_gather_wait", in_started, after, _gather_lands, 4))
        return _split_w_in(g_in.reshape(4, IN_SHARD_PAD, D_MODEL)[:, :IN_SHARD].reshape(IN_WIDTH, D_MODEL))

    def late_weights(after):
        lands = _split_wait("late_gather_wait", late_started, after, _gather_lands_all, 7)
        return {k: a.reshape(-1, a.shape[-1]) for k, a in zip(late_keys, lands)}

    sent = {}

    def scatter_start(name, groups):
        pairs, lands = _pair_sum(name + "_pair_sum", groups, _swap_halves_sibling(name + "_pair", groups), place)
        sent[name] = _split_start(name + "_start", pairs, lands, _scatter_sends, 3, pairs[0])
        return sent[name][4][0, 0]

    def ffn_grads(g_fg, g_fu, g_fd):
        return scatter_start("rs_ffn", [_halves(g_fg), _halves(g_fu), _halves(g_fd)])

    def mixer_grads(G):
        g_in_t = jnp.concatenate([G[k][:n] for k, n in W_IN_PIECES]).reshape(4, IN_SHARD, D_MODEL)
        g_in_t = jnp.pad(g_in_t, ((0, 0), (0, IN_SHARD_PAD - IN_SHARD), (0, 0)))
        return scatter_start("rs_mix", [g_in_t.reshape(4, 2, IN_SHARD_PAD // 2, D_MODEL), _halves(G["out"]),
                                        _halves(G["ba"]), _halves(G["bm"])])

    sp = {"g_pre_mix": g_pre_mix, "g_post_mix": g_post_mix, "b_if": b_if, "conv_w": conv_full,
          "conv_b": conv_b, "sinks": attn_sinks, "norm_w": mlstm_norm_w, "g_pre_ffn": g_pre_ffn,
          "g_post_ffn": g_post_ffn}
    loss, dx, small = _local_step(x[0], loss_target[0], positions.reshape(T, 1), [mod[i] for i in range(6)],
                                  sp, in_weights, late_weights, ffn_grads, mixer_grads)

    reds = (_sum4("rs_ffn_chip_sum", _split_wait("rs_ffn_wait", sent["rs_ffn"], dx, _scatter_lands, 3), place)
            + _sum4("rs_mix_chip_sum", _split_wait("rs_mix_wait", sent["rs_mix"], dx, _scatter_lands, 3), place))
    gsh = {k: s.reshape(-1, s.shape[-1])
           for k, s in zip(("fg", "fu", "fd", "w_in", "out", "ba", "bm"), _share_halves("rs_share", reds))}
    gsh["w_in"] = gsh["w_in"][:IN_SHARD]

    small["b_ada"] = small.pop("mod")
    vec = _pack_small(small).reshape(8, 2048)
    g_all = _all_gather8("gather_small", vec, pltpu.VMEM).reshape(8, SMALL_LEN)
    dmod_sh = lax.dynamic_slice_in_dim(g_all[:, :6 * D_MODEL], chip * ada_cols, ada_cols, axis=1)
    g_w_ada = _mm_tn("dw_ada", c_all, dmod_sh.astype(BF16), F32, D_MODEL, 512, 8)

    smalls = {"b_ada": (b_ada, m_b_ada, v_b_ada), "g_pre_mix": (g_pre_mix, m_g_pre_mix, v_g_pre_mix),
              "g_post_mix": (g_post_mix, m_g_post_mix, v_g_post_mix), "b_if": (b_if, m_b_if, v_b_if),
              "conv_w": None, "conv_b": (conv_b, m_conv_b, v_conv_b),
              "sinks": (attn_sinks, m_attn_sinks, v_attn_sinks),
              "norm_w": (mlstm_norm_w, m_mlstm_norm_w, v_mlstm_norm_w),
              "g_pre_ffn": (g_pre_ffn, m_g_pre_ffn, v_g_pre_ffn),
              "g_post_ffn": (g_post_ffn, m_g_post_ffn, v_g_post_ffn)}
    shapes = {k: (t[0].shape if t is not None else (1, CONV_WIDTH, D_MODEL)) for k, t in smalls.items()}
    zeros = jnp.zeros((CONV_WIDTH * D_MODEL,), F32)
    packs = [_pack_small({k: (t[i] if t is not None else zeros) for k, t in smalls.items()}).reshape(1, -1)
             for i in range(3)]
    s_out = [_unpack_small(o[0], shapes) for o in _small_update(g_all, *packs)]
    g_conv = lax.dynamic_slice_in_dim(s_out[0]["conv_w"], chip * conv_w.shape[2], conv_w.shape[2], axis=2)

    res = {}
    for k, t in smalls.items():
        if t is not None:
            res[k] = tuple(o[k] for o in s_out)
    res["conv_w"] = (g_conv, *[o[None] for o in _adamw("adam_conv_w", conv_w[0], g_conv[0], m_conv_w[0], v_conv_w[0])])
    res["w_ada"] = (g_w_ada[None], *[o[None] for o in _adamw("adam_w_ada", w_ada[0], g_w_ada, m_w_ada[0], v_w_ada[0])])
    bigs = {"w_in": (w_in, m_w_in, v_w_in), "ba": (w_branch_attn, m_w_branch_attn, v_w_branch_attn),
            "bm": (w_branch_mlstm, m_w_branch_mlstm, v_w_branch_mlstm), "out": (w_out, m_w_out, v_w_out),
            "fg": (w_ffn_gate, m_w_ffn_gate, v_w_ffn_gate), "fu": (w_ffn_up, m_w_ffn_up, v_w_ffn_up),
            "fd": (w_ffn_down, m_w_ffn_down, v_w_ffn_down)}
    for k, (w, m, v) in bigs.items():
        if k in ("w_in", "fg", "fu"):
            res[k] = tuple(o.T[None] for o in (gsh[k], *_adamw("adam_" + k, w[0].T, gsh[k], m[0].T, v[0].T)))
        else:
            g = gsh[k].T if k in ("ba", "bm") else gsh[k]
            res[k] = (g[None], *[o[None] for o in _adamw("adam_" + k, w[0], g, m[0], v[0])])

    order = ("w_ada", "b_ada", "g_pre_mix", "g_post_mix", "w_in", "b_if", "conv_w", "conv_b", "sinks",
             "norm_w", "ba", "bm", "out", "g_pre_ffn", "g_post_ffn", "fg", "fu", "fd")
    total = lax.psum(loss[0, 0], ("x", "y", "c"))
    return (total, dx[None], *[res[k][0] for k in order], *[res[k][1] for k in order],
            *[res[k][2] for k in order], *[res[k][3] for k in order])
```

```python
import functools

import jax
import jax.numpy as jnp
from jax import lax
from jax.experimental import pallas as pl
from jax.experimental.pallas import tpu as pltpu

F32, BF16 = jnp.float32, jnp.bfloat16
MESH = pl.DeviceIdType.MESH

D_MODEL = 1024
N_Q_HEADS, N_KV_HEADS, HEAD_DIM, WINDOW = 8, 2, 64, 128
ROPE_THETA = 10000.0
MLSTM_HEADS, MLSTM_HEAD_DIM, MLSTM_CHUNK, CONV_WIDTH = 4, 128, 64, 4
D_FF = 2816
NORM_EPS = 1e-6
ADAM_LR, ADAM_B1, ADAM_B2, ADAM_EPS, ADAM_WD, ADAM_STEP = 0.001, 0.9, 0.999, 1e-08, 0.01, 10

VMEM_LIMIT = 56 * 1024 * 1024
ROW_TILE = 256
MM_TM = 512
MM_TT = 1024
ATTN_BLK = WINDOW
STEP_ROWS = 2 * MLSTM_CHUNK
NEG_INF = float("-inf")


def _params(sem):
    return pltpu.CompilerParams(dimension_semantics=sem, vmem_limit_bytes=VMEM_LIMIT)


def _sds(shape, dtype):
    return jax.ShapeDtypeStruct(shape, dtype)


def _sigmoid(x):
    return 1.0 / (1.0 + jnp.exp(-x))


def _dot(a, b, ca, cb):
    return lax.dot_general(a, b, (((ca,), (cb,)), ((), ())), preferred_element_type=F32)


def _bdot(a, b, ca, cb):
    return lax.dot_general(a, b, (((ca,), (cb,)), ((0,), (0,))), preferred_element_type=F32)


def _bdot_rows(a, b):
    return jnp.stack([_dot(a[h], b[h], 0, 0) for h in range(a.shape[0])])


def _mm(name, prods, extras, epi, out_dtypes, cn, nt=False, tm=MM_TM):
    flat = [ab for p in prods for ab in p]
    counts = [len(p) for p in prods]
    M = flat[0][0].shape[0]
    N = flat[0][1].shape[0 if nt else 1]
    tm = min(tm, M)
    n_in = 2 * len(flat) + len(extras)

    def body(*refs):
        ins, outs = refs[:n_in], refs[n_in:]
        for j in range(N // cn):
            cols = slice(j * cn, (j + 1) * cn)
            k, ps = 0, []
            for cnt in counts:
                acc = None
                for _ in range(cnt):
                    b = ins[k + 1][cols, :] if nt else ins[k + 1][:, cols]
                    d = _dot(ins[k][...], b, 1, 1 if nt else 0)
                    acc = d if acc is None else acc + d
                    k += 2
                ps.append(acc)
            res = epi(ps, [r[:, cols] for r in ins[k:]])
            for o, r in zip(outs, res):
                o[:, cols] = r.astype(o.dtype)

    in_specs, args = [], []
    for a, b in flat:
        in_specs.append(pl.BlockSpec((tm, a.shape[1]), lambda i: (i, 0)))
        in_specs.append(pl.BlockSpec(b.shape, lambda i: (0, 0), pipeline_mode=pl.Buffered(1)))
        args += [a, b]
    for e in extras:
        e, off = e if isinstance(e, tuple) else (e, 0)
        rows = 1 if e.shape[0] == 1 else tm
        in_specs.append(pl.BlockSpec((rows, N), lambda i, off=off, rows=rows: (0 if rows == 1 else i, off)))
        args.append(e)
    return pl.pallas_call(
        body, name=name, grid=(M // tm,), in_specs=in_specs,
        out_specs=[pl.BlockSpec((tm, N), lambda i: (i, 0)) for _ in out_dtypes],
        out_shape=[_sds((M, N), dt) for dt in out_dtypes],
        compiler_params=_params(("parallel",)))(*args)


def _mm_rows(name, prods, extras, epi, outs, accs, cn, nt=False, tm=MM_TM):
    flat = [ab for p in prods for ab in p]
    counts = [len(p) for p in prods]
    M = flat[0][0].shape[0]
    N = flat[0][1].shape[0 if nt else 1]
    tm = min(tm, M)
    n_mm, n_in, n_out = 2 * len(flat), 2 * len(flat) + len(extras), len(outs)

    def body(*refs):
        ins, out_refs, acc_refs = refs[:n_in], refs[n_in:n_in + n_out], refs[n_in + n_out:]

        @pl.when(pl.program_id(0) == 0)
        def _():
            for a in acc_refs:
                a[...] = jnp.zeros_like(a)

        chunks = [[] for _ in counts]
        for j in range(N // cn):
            cols = slice(j * cn, (j + 1) * cn)
            k = 0
            for p, cnt in enumerate(counts):
                acc = None
                for _ in range(cnt):
                    b = ins[k + 1][cols, :] if nt else ins[k + 1][:, cols]
                    d = _dot(ins[k][...], b, 1, 1 if nt else 0)
                    acc = d if acc is None else acc + d
                    k += 2
                chunks[p].append(acc)
        ps = [c[0] if len(c) == 1 else jnp.concatenate(c, axis=1) for c in chunks]
        res, incs = epi(ps, [r[...] for r in ins[n_mm:]])
        for o, r in zip(out_refs, res):
            o[...] = r.astype(o.dtype)
        for a, inc in zip(acc_refs, incs):
            a[...] += inc

    in_specs, args = [], []
    for a, b in flat:
        in_specs.append(pl.BlockSpec((tm, a.shape[1]), lambda i: (i, 0)))
        in_specs.append(pl.BlockSpec(b.shape, lambda i: (0, 0), pipeline_mode=pl.Buffered(1)))
        args += [a, b]
    for e in extras:
        rows = 1 if e.shape[0] == 1 else tm
        in_specs.append(pl.BlockSpec((rows, e.shape[1]), lambda i, rows=rows: (0 if rows == 1 else i, 0)))
        args.append(e)
    return pl.pallas_call(
        body, name=name, grid=(M // tm,), in_specs=in_specs,
        out_specs=[pl.BlockSpec((tm, w), lambda i: (i, 0)) for w, _ in outs]
        + [pl.BlockSpec(s, lambda i: (0, 0)) for s in accs],
        out_shape=[_sds((M, w), dt) for w, dt in outs] + [_sds(s, F32) for s in accs],
        compiler_params=_params(("arbitrary",)))(*args)


def _mm_tn(name, a, b, out_dtype, tk, tn, tt=MM_TT):
    T, Ka = a.shape
    N = b.shape[1]
    tt = min(tt, T)
    steps = T // tt

    def body(a_ref, b_ref, o_ref, acc):
        t = pl.program_id(2)

        @pl.when(t == 0)
        def _():
            acc[...] = jnp.zeros_like(acc)

        acc[...] += _dot(a_ref[...], b_ref[...], 0, 0)

        @pl.when(t == steps - 1)
        def _():
            o_ref[...] = acc[...].astype(o_ref.dtype)

    return pl.pallas_call(
        body, name=name, grid=(Ka // tk, N // tn, steps),
        in_specs=[pl.BlockSpec((tt, tk), lambda i, j, t: (t, i)),
                  pl.BlockSpec((tt, tn), lambda i, j, t: (t, j))],
        out_specs=pl.BlockSpec((tk, tn), lambda i, j, t: (i, j)),
        out_shape=_sds((Ka, N), out_dtype),
        scratch_shapes=[pltpu.VMEM((tk, tn), F32)],
        compiler_params=_params(("parallel", "parallel", "arbitrary")))(a, b)


def _mm_tn_group(name, pieces, b, out_dtype, tt=MM_TT):
    T, N = b.shape
    tt = min(tt, T)
    steps, n = T // tt, len(pieces)

    def body(*refs):
        a_refs, b_ref, out_refs, accs = refs[:n], refs[n], refs[n + 1:2 * n + 1], refs[2 * n + 1:]
        t = pl.program_id(0)

        @pl.when(t == 0)
        def _():
            for acc in accs:
                acc[...] = jnp.zeros_like(acc)

        for a_ref, acc in zip(a_refs, accs):
            acc[...] += _dot(a_ref[...], b_ref[...], 0, 0)

        @pl.when(t == steps - 1)
        def _():
            for o_ref, acc in zip(out_refs, accs):
                o_ref[...] = acc[...].astype(o_ref.dtype)

    return pl.pallas_call(
        body, name=name, grid=(steps,),
        in_specs=[pl.BlockSpec((tt, a.shape[1]), lambda t: (t, 0)) for a in pieces]
        + [pl.BlockSpec((tt, N), lambda t: (t, 0))],
        out_specs=[pl.BlockSpec((a.shape[1], N), lambda t: (0, 0)) for a in pieces],
        out_shape=[_sds((a.shape[1], N), out_dtype) for a in pieces],
        scratch_shapes=[pltpu.VMEM((a.shape[1], N), F32) for a in pieces],
        compiler_params=_params(("arbitrary",)))(*pieces, b)


def _first(ps, es):
    return (ps[0],)


def _rows(name, body, ins, out_shapes, T, tr=ROW_TILE):
    tr = min(tr, T)

    def spec(shape):
        if shape[0] == T:
            return pl.BlockSpec((tr,) + tuple(shape[1:]), lambda i: (i,) + (0,) * (len(shape) - 1))
        return pl.BlockSpec(tuple(shape), lambda i: (0,) * len(shape))

    return pl.pallas_call(
        body, name=name, grid=(T // tr,),
        in_specs=[spec(a.shape) for a in ins], out_specs=[spec(s.shape) for s in out_shapes],
        out_shape=out_shapes, compiler_params=_params(("arbitrary",)))(*ins)


def _rms(x):
    r = lax.rsqrt(jnp.mean(x * x, axis=-1, keepdims=True) + NORM_EPS)
    return x * r, r


def _rms_bwd(dxn, xn, r):
    return r * (dxn - xn * jnp.mean(dxn * xn, axis=-1, keepdims=True))


def _colsum(v):
    return jnp.sum(v, axis=0, keepdims=True)


def _proj_in(x, g, sc, sh, groups):
    T = x.shape[0]
    tm = min(MM_TM, T)
    ng = len(groups)

    def body(x_ref, g_ref, sc_ref, sh_ref, *rest):
        w_refs, h_ref, out_refs = rest[:ng], rest[ng], rest[ng + 1:]
        xn, _ = _rms(x_ref[...])
        h = (xn * g_ref[...] * (1.0 + sc_ref[...]) + sh_ref[...]).astype(BF16)
        h_ref[...] = h
        for w_ref, o_ref, (w, _, cn) in zip(w_refs, out_refs, groups):
            for j in range(w.shape[0] // cn):
                cols = slice(j * cn, (j + 1) * cn)
                o_ref[:, cols] = _dot(h, w_ref[cols, :], 1, 1).astype(o_ref.dtype)

    row = pl.BlockSpec((1, D_MODEL), lambda i: (0, 0))
    tile = lambda w: pl.BlockSpec((tm, w), lambda i: (i, 0))
    return pl.pallas_call(
        body, name="proj_in", grid=(T // tm,),
        in_specs=[tile(D_MODEL), row, row, row] + [
            pl.BlockSpec(w.shape, lambda i: (0, 0), pipeline_mode=pl.Buffered(1)) for w, _, _ in groups],
        out_specs=[tile(D_MODEL)] + [tile(w.shape[0]) for w, _, _ in groups],
        out_shape=[_sds((T, D_MODEL), BF16)] + [_sds((T, w.shape[0]), dt) for w, dt, _ in groups],
        compiler_params=_params(("parallel",)))(x, g, sc, sh, *[w for w, _, _ in groups])


def _acc_rows(rows):
    w = rows[0].shape[1]
    return jnp.concatenate(rows + [jnp.zeros((8 - len(rows), w), F32)], axis=0)


def _res_norm_rows(ps, es):
    mix = ps[0]
    x, gate, gp, g2, sc, sh = es
    mh, _ = _rms(mix)
    x1 = x + gate * (mh * gp)
    xn, _ = _rms(x1)
    return [mix, x1, xn * g2 * (1.0 + sc) + sh], []


def _final_loss_rows(ps, es):
    x1, tgt, gate, gp = es
    fh, r = _rms(ps[0])
    e = x1 + gate * (fh * gp) - tgt
    loss = 0.5 * jnp.sum(jnp.mean(e * e, axis=-1, keepdims=True))
    dy = e * (1.0 / D_MODEL)
    acc = _acc_rows([_colsum(dy * fh * gp), _colsum(dy * gate * fh)])
    return [dy, _rms_bwd(dy * gate * gp, fh, r)], [acc, jnp.full((1, 128), loss, F32)]


def _res_norm_bwd_rows(ps, es):
    dh = ps[0]
    x1, mix, dy, sc, gate, g2, gp = es
    xn, r1 = _rms(x1)
    rows = [_colsum(dh * xn * g2), _colsum(dh), _colsum(dh * (1.0 + sc) * xn)]
    dx1 = dy + _rms_bwd(dh * (1.0 + sc) * g2, xn, r1)
    mh, rm = _rms(mix)
    rows += [_colsum(dx1 * mh * gp), _colsum(dx1 * gate * mh)]
    return [dx1, _rms_bwd(dx1 * gate * gp, mh, rm)], [_acc_rows(rows)]


def _pre_norm_bwd_rows(ps, es):
    dh = ps[0]
    x, dx1, g, sc = es
    xn, r = _rms(x)
    rows = [_colsum(dh * xn * g), _colsum(dh), _colsum(dh * (1.0 + sc) * xn)]
    return [dx1 + _rms_bwd(dh * (1.0 + sc) * g, xn, r)], [_acc_rows(rows)]


def _rope_tables(pos_col, inv_freq):
    T = pos_col.shape[0]

    def body(p_ref, f_ref, c_ref, s_ref):
        ang = p_ref[...].astype(F32) * f_ref[...]
        lane = lax.broadcasted_iota(jnp.int32, ang.shape, 1)
        c_ref[...] = jnp.cos(ang)
        s_ref[...] = jnp.where(lane % HEAD_DIM < HEAD_DIM // 2, -1.0, 1.0) * jnp.sin(ang)

    return _rows("rope_tables", body, [pos_col, inv_freq],
                 [_sds((T, 128), F32), _sds((T, 128), F32)], T, tr=512)


def _swap_halves(t):
    W = t.shape[1]
    lane = lax.broadcasted_iota(jnp.int32, t.shape, 1)
    half = HEAD_DIM // 2
    return jnp.where(lane % HEAD_DIM < half, pltpu.roll(t, W - half, 1), pltpu.roll(t, half, 1))


def _widen(c, W):
    return c if W == 128 else jnp.concatenate([c] * (W // 128), axis=1)


def _rope(t, c, s):
    W = t.shape[1]
    return t * _widen(c, W) + _swap_halves(t) * _widen(s, W)


def _unrope(dy, c, s):
    W = dy.shape[1]
    return dy * _widen(c, W) + _swap_halves(dy * _widen(s, W))


def _attn_mask(n):
    qi = lax.broadcasted_iota(jnp.int32, (ATTN_BLK, 2 * ATTN_BLK), 0)
    kj = lax.broadcasted_iota(jnp.int32, (ATTN_BLK, 2 * ATTN_BLK), 1)
    rel = kj - ATTN_BLK
    return (rel <= qi) & (qi - rel < WINDOW) & ((n > 0) | (kj >= ATTN_BLK))


def _attn_load(cur, prv, cc, sc, cp, sp):
    x, xp = cur[...], prv[...]
    q = _rope(x[:, :512], cc[...], sc[...]) * (HEAD_DIM ** -0.5)
    k = jnp.concatenate([_rope(xp[:, 512:640], cp[...], sp[...]),
                         _rope(x[:, 512:640], cc[...], sc[...])], axis=0)
    v = jnp.concatenate([xp[:, 640:768], x[:, 640:768]], axis=0)
    return q, k, v


ROLLED = tuple(h for h in range(N_Q_HEADS) if h % 2 != h // (N_Q_HEADS // N_KV_HEADS))


def _pair_heads(t):
    half = lax.broadcasted_iota(jnp.int32, (ATTN_BLK, 128), 1) // HEAD_DIM
    return jnp.stack([jnp.where(half == h % 2, t[:, 128 * (h // 2):128 * (h // 2) + 128], 0.0)
                      for h in range(N_Q_HEADS)])


def _kv_heads(t):
    half = lax.broadcasted_iota(jnp.int32, t.shape, 1) // HEAD_DIM
    tr = pltpu.roll(t, HEAD_DIM, 1)
    return jnp.stack([jnp.where(half == h % 2, tr if h in ROLLED else t, 0.0)
                      for h in range(N_Q_HEADS)])


def _sink_column(snk):
    return jnp.stack([jnp.full((1, 1), snk[0, h], F32) for h in range(N_Q_HEADS)])


def _attn_probs(qh, kh, mask, sink):
    s = jnp.where(mask, _bdot(qh, kh, 2, 2), NEG_INF)
    m = jnp.maximum(jnp.max(s, axis=-1, keepdims=True), sink)
    p = jnp.exp(s - m)
    es = jnp.exp(sink - m)
    rl = 1.0 / (jnp.sum(p, axis=-1, keepdims=True) + es)
    return p, es, rl


def _attn_specs(order):
    blk = lambda w: pl.BlockSpec((ATTN_BLK, w), lambda s: (order(s), 0))
    prv = lambda w: pl.BlockSpec((ATTN_BLK, w), lambda s: (jnp.maximum(order(s) - 1, 0), 0))
    return [blk(768), prv(768), blk(128), blk(128), prv(128), prv(128),
            pl.BlockSpec(memory_space=pltpu.SMEM)]


def _attn_fwd(pa, cos, sin, sinks):
    T = pa.shape[0]
    nb = T // ATTN_BLK

    def body(cur, prv, cc, sc, cp, sp, snk, y_ref):
        n = pl.program_id(0)
        q, k, v = _attn_load(cur, prv, cc, sc, cp, sp)
        qh, kh, vh = _pair_heads(q).astype(BF16), _kv_heads(k).astype(BF16), _kv_heads(v).astype(BF16)
        p, _, rl = _attn_probs(qh, kh, _attn_mask(n), _sink_column(snk))
        o = _bdot(p.astype(BF16), vh, 2, 1) * rl
        for pair in range(N_Q_HEADS // 2):
            y_ref[:, 128 * pair:128 * pair + 128] = (o[2 * pair] + o[2 * pair + 1]).astype(BF16)

    return pl.pallas_call(
        body, name="attn_fwd", grid=(nb,), in_specs=_attn_specs(lambda s: s),
        out_specs=pl.BlockSpec((ATTN_BLK, 512), lambda n: (n, 0)),
        out_shape=_sds((T, 512), BF16), compiler_params=_params(("parallel",)))(
            pa, pa, cos, sin, cos, sin, sinks)


def _attn_bwd(pa, cos, sin, sinks, dy):
    T = pa.shape[0]
    nb = T // ATTN_BLK
    rev = lambda s: nb - 1 - s

    def body(cur, prv, cc, sc, cp, sp, snk, dy_ref, dq_ref, dkv_ref, dsink_ref, carry):
        n = rev(pl.program_id(0))

        @pl.when(pl.program_id(0) == 0)
        def _():
            dsink_ref[...] = jnp.zeros_like(dsink_ref)
            carry[...] = jnp.zeros_like(carry)

        q, k, v = _attn_load(cur, prv, cc, sc, cp, sp)
        qh, kh, vh = _pair_heads(q).astype(BF16), _kv_heads(k).astype(BF16), _kv_heads(v).astype(BF16)
        p, es, rl = _attn_probs(qh, kh, _attn_mask(n), _sink_column(snk))
        pn = p * rl
        do = _pair_heads(dy_ref[...]).astype(BF16)
        dp = _bdot(do, vh, 2, 2)
        delta = jnp.sum(pn * dp, axis=-1, keepdims=True)
        ds = (pn * (dp - delta)).astype(BF16)
        dsink = es * rl * delta
        dq = _bdot(ds, kh, 2, 1) * (HEAD_DIM ** -0.5)
        dkh = _bdot_rows(ds, qh)
        dvh = _bdot_rows(pn.astype(BF16), do)

        def fold(t):
            same = [t[h] for h in range(N_Q_HEADS) if h not in ROLLED]
            moved = [t[h] for h in ROLLED]
            return sum(same[1:], same[0]) + pltpu.roll(sum(moved[1:], moved[0]), HEAD_DIM, 1)

        dk, dv = fold(dkh), fold(dvh)
        for h in range(N_Q_HEADS):
            dsink_ref[h:h + 1, :] += -jnp.sum(dsink[h])
        for pair in range(N_Q_HEADS // 2):
            dq_ref[:, 128 * pair:128 * pair + 128] = _unrope(
                dq[2 * pair] + dq[2 * pair + 1], cc[...], sc[...]).astype(BF16)
        dkv_ref[:, 0:128] = _unrope(dk[ATTN_BLK:] + carry[:, 0:128], cc[...], sc[...]).astype(BF16)
        dkv_ref[:, 128:256] = (dv[ATTN_BLK:] + carry[:, 128:256]).astype(BF16)
        carry[:, 0:128] = dk[:ATTN_BLK]
        carry[:, 128:256] = dv[:ATTN_BLK]

    blk = lambda w: pl.BlockSpec((ATTN_BLK, w), lambda s: (rev(s), 0))
    return pl.pallas_call(
        body, name="attn_bwd", grid=(nb,), in_specs=_attn_specs(rev) + [blk(512)],
        out_specs=[blk(512), blk(256), pl.BlockSpec((8, 128), lambda s: (0, 0))],
        out_shape=[_sds((T, 512), BF16), _sds((T, 256), BF16), _sds((8, 128), F32)],
        scratch_shapes=[pltpu.VMEM((ATTN_BLK, 256), F32)],
        compiler_params=_params(("arbitrary",)))(pa, pa, cos, sin, cos, sin, sinks, dy)


CONV_COLS = 2 * MLSTM_HEADS * MLSTM_HEAD_DIM


def _conv_pre(cur_ref, halo_ref, w_ref, b_ref, i, tr):
    xx = jnp.concatenate([jnp.where(i > 0, halo_ref[...], 0.0), cur_ref[...]], axis=0)
    taps = [(pltpu.roll(xx, CONV_WIDTH - 1 - j, 0) if j < CONV_WIDTH - 1 else xx)[8:8 + tr]
            for j in range(CONV_WIDTH)]
    pre = b_ref[...]
    for j in range(CONV_WIDTH):
        pre = pre + taps[j] * w_ref[j:j + 1, :]
    return pre, taps


def _conv_specs(T, tr):
    return [pl.BlockSpec((tr, CONV_COLS), lambda i: (i, 0)),
            pl.BlockSpec((8, CONV_COLS), lambda i: (jnp.maximum(i * (tr // 8) - 1, 0), 0)),
            pl.BlockSpec((CONV_WIDTH, CONV_COLS), lambda i: (0, 0)),
            pl.BlockSpec((1, CONV_COLS), lambda i: (0, 0))]


def _conv_fwd(pm, w, b):
    T = pm.shape[0]
    tr = min(ROW_TILE, T)

    def body(cur_ref, halo_ref, w_ref, b_ref, o_ref):
        pre, _ = _conv_pre(cur_ref, halo_ref, w_ref, b_ref, pl.program_id(0), tr)
        o_ref[...] = pre * _sigmoid(pre)

    return pl.pallas_call(
        body, name="conv_fwd", grid=(T // tr,), in_specs=_conv_specs(T, tr),
        out_specs=pl.BlockSpec((tr, CONV_COLS), lambda i: (i, 0)),
        out_shape=_sds((T, CONV_COLS), F32), compiler_params=_params(("parallel",)))(pm, pm, w, b)


def _conv_bwd(pqk, w, b, dqk):
    T = pqk.shape[0]
    tr = min(ROW_TILE, T)
    nt = T // tr

    def body(cur_ref, prev_ref, next_ref, w_ref, b_ref, d_ref, dnext_ref, du_ref, acc_ref):
        i = pl.program_id(0)

        @pl.when(i == 0)
        def _():
            acc_ref[...] = jnp.zeros_like(acc_ref)

        last = i == nt - 1
        xx = jnp.concatenate([jnp.where(i > 0, prev_ref[...], 0.0), cur_ref[...],
                              jnp.where(last, 0.0, next_ref[...])], axis=0)
        taps = [(pltpu.roll(xx, CONV_WIDTH - 1 - j, 0) if j < CONV_WIDTH - 1 else xx)[8:16 + tr]
                for j in range(CONV_WIDTH)]
        pre = b_ref[...]
        for j in range(CONV_WIDTH):
            pre = pre + taps[j] * w_ref[j:j + 1, :]
        sg = _sigmoid(pre)
        dd = jnp.concatenate([d_ref[...], jnp.where(last, 0.0, dnext_ref[...])], axis=0)
        dpre = dd * (sg * (1.0 + pre * (1.0 - sg)))
        for j in range(CONV_WIDTH):
            acc_ref[j:j + 1, :] += _colsum(dpre[:tr] * taps[j][:tr])
        acc_ref[CONV_WIDTH:CONV_WIDTH + 1, :] += _colsum(dpre[:tr])
        du = dpre[:tr] * w_ref[CONV_WIDTH - 1:CONV_WIDTH, :]
        for j in range(CONV_WIDTH - 1):
            k = CONV_WIDTH - 1 - j
            du = du + pltpu.roll(dpre, tr + 8 - k, 0)[:tr] * w_ref[j:j + 1, :]
        du_ref[...] = du.astype(BF16)

    tile = pl.BlockSpec((tr, CONV_COLS), lambda i: (i, 0))
    after = pl.BlockSpec((8, CONV_COLS), lambda i: (jnp.minimum((i + 1) * (tr // 8), T // 8 - 1), 0))
    before = pl.BlockSpec((8, CONV_COLS), lambda i: (jnp.maximum(i * (tr // 8) - 1, 0), 0))
    return pl.pallas_call(
        body, name="conv_bwd", grid=(nt,),
        in_specs=[tile, before, after, pl.BlockSpec((CONV_WIDTH, CONV_COLS), lambda i: (0, 0)),
                  pl.BlockSpec((1, CONV_COLS), lambda i: (0, 0)), tile, after],
        out_specs=[tile, pl.BlockSpec((8, CONV_COLS), lambda i: (0, 0))],
        out_shape=[_sds((T, CONV_COLS), BF16), _sds((8, CONV_COLS), F32)],
        compiler_params=_params(("arbitrary",)))(pqk, pqk, pqk, w, b, dqk, dqk)


def _log_sigmoid(x):
    return jnp.minimum(x, 0.0) - jnp.log1p(jnp.exp(-jnp.abs(x)))


def _chunk_cumsum(x, axis):
    idx = lax.broadcasted_iota(jnp.int32, x.shape, axis) % MLSTM_CHUNK
    k = 1
    while k < MLSTM_CHUNK:
        x = x + jnp.where(idx >= k, pltpu.roll(x, k, axis), 0.0)
        k *= 2
    return x


def _chunk_rev_cumsum(x, axis):
    n = x.shape[axis]
    idx = lax.broadcasted_iota(jnp.int32, x.shape, axis) % MLSTM_CHUNK
    k = 1
    while k < MLSTM_CHUNK:
        x = x + jnp.where(idx < MLSTM_CHUNK - k, pltpu.roll(x, n - k, axis), 0.0)
        k *= 2
    return x


def _mlstm_gates(gc_ref, bc_ref, gr_ref, br_ref):
    gc = gc_ref[...] + bc_ref[...]
    gr = gr_ref[...] + br_ref[...]
    return gc, _chunk_cumsum(_log_sigmoid(gc), 0), gr, _chunk_cumsum(_log_sigmoid(gr), 1)


def _heads(ref, base=0):
    D = MLSTM_HEAD_DIM
    return jnp.stack([ref[:, base + D * h:base + D * h + D] for h in range(MLSTM_HEADS)])


def _mlstm_inputs(q_ref, k_ref, v_ref, gc, bc, gr, br):
    H = MLSTM_HEADS
    q, v = _heads(q_ref), _heads(v_ref)
    ks = _heads(k_ref) * (MLSTM_HEAD_DIM ** -0.5)
    return dict(
        q=q, ks=ks, qb=q.astype(BF16), kb=ks.astype(BF16), vb=v.astype(BF16),
        b_col=jnp.stack([bc[:, H + h:H + h + 1] for h in range(H)]),
        i_col=jnp.stack([gc[:, h:h + 1] for h in range(H)]),
        b_row=jnp.stack([br[H + h:H + h + 1, :] for h in range(H)]),
        i_row=jnp.stack([gr[h:h + 1, :] for h in range(H)]))


def _mlstm_head(f, c_prev, n_prev, m_prev):
    L = MLSTM_CHUNK
    q, qb = f["q"], f["qb"]
    t = lax.broadcasted_iota(jnp.int32, (1, 2 * L, 2 * L), 1)
    s = lax.broadcasted_iota(jnp.int32, (1, 2 * L, 2 * L), 2)
    mask = (t // L == s // L) & (s <= t)
    d = jnp.where(mask, f["b_col"] - f["b_row"] + f["i_row"], NEG_INF)
    row = lax.broadcasted_iota(jnp.int32, (1, 2 * L, 1), 1)
    inter = f["b_col"] + jnp.where(row < L, m_prev[0], m_prev[1])
    m_t = jnp.maximum(inter, jnp.max(d, axis=-1, keepdims=True))
    w_intra = jnp.exp(d - m_t)
    w_inter = jnp.exp(inter - m_t)
    sc = _bdot(qb, f["kb"], 2, 2) * w_intra
    qc = jnp.concatenate([_bdot(qb[:, :L], c_prev[0].astype(BF16), 2, 1),
                          _bdot(qb[:, L:], c_prev[1].astype(BF16), 2, 1)], axis=1)
    qn = jnp.concatenate([jnp.sum(q[:, :L] * n_prev[0], axis=-1, keepdims=True),
                          jnp.sum(q[:, L:] * n_prev[1], axis=-1, keepdims=True)], axis=1)
    num = _bdot(sc.astype(BF16), f["vb"], 2, 1) + w_inter * qc
    den = jnp.sum(sc, axis=-1, keepdims=True) + w_inter * qn
    return dict(f, w_intra=w_intra, w_inter=w_inter, sc=sc, qc=qc, qn=qn, num=num, den=den,
                floor=jnp.exp(-m_t))


def _mlstm_update(f, ch, c, n, m):
    L = MLSTM_CHUNK
    rows = slice(L * ch, L * ch + L)
    b_col = f["b_col"][:, rows]
    g_last = b_col[:, L - 1:L]
    a_col = g_last - b_col + f["i_col"][:, rows]
    m_new = jnp.maximum(g_last + m, jnp.max(a_col, axis=1, keepdims=True))
    decay = jnp.exp(g_last + m - m_new)
    e_a = jnp.exp(a_col - m_new)
    kw = f["ks"][:, rows] * e_a
    c_new = decay * c + _bdot_rows(kw.astype(BF16), f["vb"][:, rows])
    n_new = decay * n + jnp.sum(kw, axis=1, keepdims=True)
    return c_new, n_new, m_new, decay, e_a, kw


def _mlstm_specs(T, order):
    blk = lambda w, col: pl.BlockSpec((STEP_ROWS, w), lambda s: (order(s), col))
    return [blk(512, 0), blk(512, 1), blk(512, 0), blk(128, 0),
            pl.BlockSpec((1, 128), lambda s: (0, 0)),
            pl.BlockSpec((8, STEP_ROWS), lambda s: (0, order(s))),
            pl.BlockSpec((8, 128), lambda s: (0, 0))]


def _lanes(m):
    return jnp.broadcast_to(m, m.shape[:-1] + (128,))


def _mlstm_fwd(qk, pm, gcol, bcol, grow, brow):
    T = qk.shape[0]
    steps = T // STEP_ROWS
    H, D = MLSTM_HEADS, MLSTM_HEAD_DIM

    def body(q_ref, k_ref, v_ref, gc_ref, bc_ref, gr_ref, br_ref, h_ref, cs_ref, ns_ref, ms_ref,
             c_st, n_st, m_st):
        @pl.when(pl.program_id(0) == 0)
        def _():
            c_st[...] = jnp.zeros_like(c_st)
            n_st[...] = jnp.zeros_like(n_st)
            m_st[...] = jnp.zeros_like(m_st)

        f = _mlstm_inputs(q_ref, k_ref, v_ref, *_mlstm_gates(gc_ref, bc_ref, gr_ref, br_ref))
        c0, n0, m0 = c_st[...], n_st[...], m_st[:, :, 0:1]
        c1, n1, m1, _, _, _ = _mlstm_update(f, 0, c0, n0, m0)
        c2, n2, m2, _, _, _ = _mlstm_update(f, 1, c1, n1, m1)
        f = _mlstm_head(f, (c0, c1), (n0, n1), (m0, m1))
        h = f["num"] / jnp.maximum(jnp.abs(f["den"]), f["floor"])
        for hd in range(H):
            h_ref[:, D * hd:D * hd + D] = h[hd]
        cs_ref[0], cs_ref[1] = c0, c1
        ns_ref[0], ns_ref[1] = n0, n1
        ms_ref[0], ms_ref[1] = _lanes(m0), _lanes(m1)
        c_st[...], n_st[...], m_st[...] = c2, n2, _lanes(m2)

    vec = pl.BlockSpec((2, H, 1, 128), lambda s: (s, 0, 0, 0))
    return pl.pallas_call(
        body, name="mlstm_fwd", grid=(steps,), in_specs=_mlstm_specs(T, lambda s: s),
        out_specs=[pl.BlockSpec((STEP_ROWS, 512), lambda s: (s, 0)),
                   pl.BlockSpec((2, H, 128, 128), lambda s: (s, 0, 0, 0)), vec, vec],
        out_shape=[_sds((T, 512), F32), _sds((2 * steps, H, 128, 128), F32),
                   _sds((2 * steps, H, 1, 128), F32), _sds((2 * steps, H, 1, 128), F32)],
        scratch_shapes=[pltpu.VMEM((H, 128, 128), F32), pltpu.VMEM((H, 1, 128), F32),
                        pltpu.VMEM((H, 1, 128), F32)],
        compiler_params=_params(("arbitrary",)))(qk, qk, pm, gcol, bcol, grow, brow)


def _mlstm_bwd(qk, pm, gcol, bcol, grow, brow, cs, ns, ms, dh):
    T = qk.shape[0]
    steps = T // STEP_ROWS
    H, L, D = MLSTM_HEADS, MLSTM_CHUNK, MLSTM_HEAD_DIM
    rev = lambda s: steps - 1 - s

    def body(q_ref, k_ref, v_ref, gc_ref, bc_ref, gr_ref, br_ref, cs_ref, ns_ref, ms_ref, dh_ref,
             dqk_ref, dv_ref, dgc_ref, dgr_ref, dc_st, dn_st):
        @pl.when(pl.program_id(0) == 0)
        def _():
            dc_st[...] = jnp.zeros_like(dc_st)
            dn_st[...] = jnp.zeros_like(dn_st)

        f = _mlstm_inputs(q_ref, k_ref, v_ref, *_mlstm_gates(gc_ref, bc_ref, gr_ref, br_ref))
        c_prev = (cs_ref[0], cs_ref[1])
        n_prev = (ns_ref[0], ns_ref[1])
        m_prev = (ms_ref[0, :, :, 0:1], ms_ref[1, :, :, 0:1])
        f = _mlstm_head(f, c_prev, n_prev, m_prev)
        big = jnp.abs(f["den"]) > f["floor"]
        rden = 1.0 / jnp.where(big, jnp.abs(f["den"]), f["floor"])
        dnum = _heads(dh_ref) * rden
        hdh = jnp.sum(f["num"] * dnum, axis=-1, keepdims=True)
        dden = jnp.where(big, -hdh * rden * jnp.sign(f["den"]), 0.0)
        dnum_b = dnum.astype(BF16)
        dsc = _bdot(dnum_b, f["vb"], 2, 2) + dden
        g = dsc * f["sc"]
        dv = _bdot_rows(f["sc"].astype(BF16), dnum_b)
        dqk_ = (dsc * f["w_intra"]).astype(BF16)
        dq = _bdot(dqk_, f["kb"], 2, 1)
        dks = _bdot_rows(dqk_, f["qb"])
        wdn = f["w_inter"] * dnum
        wdn_b = wdn.astype(BF16)
        wdd = f["w_inter"] * dden
        u = jnp.sum(f["qc"] * wdn, axis=-1, keepdims=True) + wdd * f["qn"]
        dks_s, dv_s, z_s, dg_s = [None, None], [None, None], [None, None], [None, None]
        dcn, dnn = dc_st[...], dn_st[...]
        for ch in (1, 0):
            rows = slice(L * ch, L * ch + L)
            _, _, _, decay, e_a, kw = _mlstm_update(f, ch, c_prev[ch], n_prev[ch], m_prev[ch])
            dcn_b = dcn.astype(BF16)
            dkw = _bdot(f["vb"][:, rows], dcn_b, 2, 2) + dnn
            dks_s[ch] = e_a * dkw
            dv_s[ch] = _bdot(kw.astype(BF16), dcn_b, 2, 1)
            z_s[ch] = e_a * jnp.sum(f["ks"][:, rows] * dkw, axis=-1, keepdims=True)
            dg_s[ch] = jnp.sum(z_s[ch], axis=1, keepdims=True) + decay * (
                jnp.sum(c_prev[ch] * dcn, axis=(1, 2), keepdims=True)
                + jnp.sum(n_prev[ch] * dnn, axis=(1, 2), keepdims=True))
            dcn = decay * dcn + _bdot_rows(f["qb"][:, rows], wdn_b[:, rows])
            dnn = decay * dnn + jnp.sum(wdd[:, rows] * f["q"][:, rows], axis=1, keepdims=True)
        dc_st[...], dn_st[...] = dcn, dnn
        dq = dq + jnp.concatenate(
            [_bdot(wdn_b[:, :L], c_prev[0].astype(BF16), 2, 2) + wdd[:, :L] * n_prev[0],
             _bdot(wdn_b[:, L:], c_prev[1].astype(BF16), 2, 2) + wdd[:, L:] * n_prev[1]], axis=1)
        dks = (dks + jnp.concatenate(dks_s, axis=1)) * (D ** -0.5)
        dv = dv + jnp.concatenate(dv_s, axis=1)
        z = jnp.concatenate(z_s, axis=1)
        row = lax.broadcasted_iota(jnp.int32, (1, STEP_ROWS, 1), 1)
        dg_col = jnp.where(row == L - 1, dg_s[0], 0.0) + jnp.where(row == 2 * L - 1, dg_s[1], 0.0)
        db_col = jnp.sum(g, axis=-1, keepdims=True) + u - z + dg_col
        g_row = jnp.sum(g, axis=1, keepdims=True)
        lane = lax.broadcasted_iota(jnp.int32, (STEP_ROWS, 128), 1)
        sub = lax.broadcasted_iota(jnp.int32, (8, STEP_ROWS), 0)
        dgc = jnp.zeros((STEP_ROWS, 128), F32)
        dgr = jnp.zeros((8, STEP_ROWS), F32)
        for hd in range(H):
            dgc = dgc + jnp.where(lane == hd, z[hd], 0.0) + jnp.where(lane == H + hd, db_col[hd], 0.0)
            dgr = dgr + jnp.where(sub == hd, g_row[hd], 0.0) - jnp.where(sub == H + hd, g_row[hd], 0.0)
            dqk_ref[:, D * hd:D * hd + D] = dq[hd]
            dqk_ref[:, H * D + D * hd:H * D + D * hd + D] = dks[hd]
            dv_ref[:, D * hd:D * hd + D] = dv[hd].astype(BF16)
        dgc_ref[...] = dgc
        dgr_ref[...] = dgr

    return pl.pallas_call(
        body, name="mlstm_bwd", grid=(steps,),
        in_specs=_mlstm_specs(T, rev) + [
            pl.BlockSpec((2, H, 128, 128), lambda s: (rev(s), 0, 0, 0)),
            pl.BlockSpec((2, H, 1, 128), lambda s: (rev(s), 0, 0, 0)),
            pl.BlockSpec((2, H, 1, 128), lambda s: (rev(s), 0, 0, 0)),
            pl.BlockSpec((STEP_ROWS, 512), lambda s: (rev(s), 0))],
        out_specs=[pl.BlockSpec((STEP_ROWS, 1024), lambda s: (rev(s), 0)),
                   pl.BlockSpec((STEP_ROWS, 512), lambda s: (rev(s), 0)),
                   pl.BlockSpec((STEP_ROWS, 128), lambda s: (rev(s), 0)),
                   pl.BlockSpec((8, STEP_ROWS), lambda s: (0, rev(s)))],
        out_shape=[_sds((T, 1024), F32), _sds((T, 512), BF16), _sds((T, 128), F32), _sds((8, T), F32)],
        scratch_shapes=[pltpu.VMEM((H, 128, 128), F32), pltpu.VMEM((H, 1, 128), F32)],
        compiler_params=_params(("arbitrary",)))(qk, qk, pm, gcol, bcol, grow, brow, cs, ns, ms, dh)


def _rows_to_lanes(x):
    eye = (lax.broadcasted_iota(jnp.int32, (8, 128), 0)
           == lax.broadcasted_iota(jnp.int32, (8, 128), 1)).astype(BF16)
    out, rest = None, x
    for _ in range(3):
        piece = rest.astype(BF16)
        rest = rest - piece.astype(F32)
        t = _dot(piece, eye, 0, 0)
        out = t if out is None else out + t
    return out


def _gate_bwd(dgc, dgr, gcol, bcol):
    T = dgc.shape[0]
    tr = min(ROW_TILE, T)

    def body(a_ref, b_ref, g_ref, bias_ref, o_ref, acc_ref):
        i = pl.program_id(0)

        @pl.when(i == 0)
        def _():
            acc_ref[...] = jnp.zeros_like(acc_ref)

        d = a_ref[...] + _rows_to_lanes(b_ref[:, pl.ds(pl.multiple_of(i * tr, 128), tr)])
        lane = lax.broadcasted_iota(jnp.int32, d.shape, 1)
        is_f = (lane >= MLSTM_HEADS) & (lane < 2 * MLSTM_HEADS)
        dlogf = _chunk_rev_cumsum(jnp.where(is_f, d, 0.0), 0)
        out = jnp.where(is_f, dlogf * _sigmoid(-(g_ref[...] + bias_ref[...])), d)
        o_ref[...] = out.astype(BF16)
        acc_ref[0:1, :] += _colsum(out)

    return _rows("gate_bwd", body, [dgc, dgr, gcol, bcol],
                 [_sds((T, 128), BF16), _sds((8, 128), F32)], T, tr=tr)


def _head_norm(h, mu_axis=-1):
    mu = jnp.mean(h, axis=-1, keepdims=True)
    hc = h - mu
    r = lax.rsqrt(jnp.mean(hc * hc, axis=-1, keepdims=True) + NORM_EPS)
    return hc * r, r


def _mlstm_out(hm, pm, w):
    T = hm.shape[0]
    D = MLSTM_HEAD_DIM

    def body(h_ref, o_ref, w_ref, y_ref):
        for hd in range(MLSTM_HEADS):
            cols = slice(D * hd, D * hd + D)
            hn, _ = _head_norm(h_ref[:, cols])
            y_ref[:, cols] = (_sigmoid(o_ref[:, cols].astype(F32)) * hn * w_ref[:, cols]).astype(BF16)

    tr = min(ROW_TILE, T)
    return pl.pallas_call(
        body, name="mlstm_out", grid=(T // tr,),
        in_specs=[pl.BlockSpec((tr, 512), lambda i: (i, 0)), pl.BlockSpec((tr, 512), lambda i: (i, 1)),
                  pl.BlockSpec((1, 512), lambda i: (0, 0))],
        out_specs=pl.BlockSpec((tr, 512), lambda i: (i, 0)), out_shape=_sds((T, 512), BF16),
        compiler_params=_params(("parallel",)))(hm, pm, w)


def _mlstm_out_bwd_rows(ps, es):
    hm, vo, w_all = es
    D, width = MLSTM_HEAD_DIM, MLSTM_HEADS * MLSTM_HEAD_DIM
    dhs, dos, dws = [], [], []
    for hd in range(MLSTM_HEADS):
        cols = slice(D * hd, D * hd + D)
        hn, r = _head_norm(hm[:, cols])
        sg = _sigmoid(vo[:, width + D * hd:width + D * hd + D].astype(F32))
        dy, w = ps[0][:, cols], w_all[:, cols]
        dos.append(dy * hn * w * sg * (1.0 - sg))
        dyn = dy * sg
        dws.append(_colsum(dyn * hn))
        dhn = dyn * w
        dhs.append(r * (dhn - jnp.mean(dhn, axis=-1, keepdims=True)
                        - hn * jnp.mean(dhn * hn, axis=-1, keepdims=True)))
    cat = lambda parts: jnp.concatenate(parts, axis=1)
    return [cat(dhs), cat(dos)], [_acc_rows([cat(dws)])]


ADAM_TILE_ELEMS = 256 * 1024


def _adamw(name, w, g, m, v):
    R, C = w.shape
    fits = [t for t in range(8, R + 1, 8) if R % t == 0 and t * C <= ADAM_TILE_ELEMS]
    if fits or R * C <= ADAM_TILE_ELEMS:
        tr = fits[-1] if fits else R
        spec, grid = pl.BlockSpec((tr, C), lambda i: (i, 0)), (R // tr,)
    else:
        spec, grid = pl.BlockSpec((R, 128), lambda i: (0, i)), (C // 128,)
    c1 = 1.0 - ADAM_B1 ** ADAM_STEP
    c2 = 1.0 - ADAM_B2 ** ADAM_STEP

    def body(w_ref, g_ref, m_ref, v_ref, d_ref, mo_ref, vo_ref):
        g = g_ref[...]
        m = ADAM_B1 * m_ref[...] + (1.0 - ADAM_B1) * g
        v = ADAM_B2 * v_ref[...] + (1.0 - ADAM_B2) * (g * g)
        mo_ref[...] = m
        vo_ref[...] = v
        d_ref[...] = -ADAM_LR * ((m / c1) / (jnp.sqrt(v / c2) + ADAM_EPS) + ADAM_WD * w_ref[...])

    return pl.pallas_call(
        body, name=name, grid=grid, in_specs=[spec] * 4, out_specs=[spec] * 3,
        out_shape=[_sds((R, C), F32)] * 3, compiler_params=_params(("parallel",)))(w, g, m, v)


def _place():
    return lax.axis_index("x"), lax.axis_index("y"), lax.axis_index("c")


def _all_gather8(name, blk, space):
    m, n = blk.shape

    def body(x_ref, out_ref, send_sems, recv_sems, local_sem):
        x, y, c = _place()
        me, sibling = (x, y, c), (x, y, 1 - c)
        chips = [(1 - x, y), (x, 1 - y), (1 - x, 1 - y)]

        def rows(px, py, pc):
            return out_ref.at[pl.ds((4 * px + 2 * py + pc) * m, m), :]

        def copy(k, block, to, src=None):
            return pltpu.make_async_remote_copy(
                src_ref=rows(*block) if src is None else src, dst_ref=rows(*block),
                send_sem=send_sems.at[k], recv_sem=recv_sems.at[k],
                device_id=to, device_id_type=MESH)

        mine = pltpu.make_async_copy(x_ref, rows(*me), local_sem)
        mine.start()
        first = [copy(0, me, sibling, src=x_ref)]
        first += [copy(1 + j, me, (*chip, c), src=x_ref) for j, chip in enumerate(chips)]
        for cp in first:
            cp.start()
        passed = [copy(4 + j, (*chip, c), sibling) for j, chip in enumerate(chips)]
        for j, chip in enumerate(chips):
            copy(1 + j, (*chip, c), me).wait_recv()
            passed[j].start()
        copy(0, sibling, me).wait_recv()
        for j, chip in enumerate(chips):
            copy(4 + j, (*chip, 1 - c), me).wait_recv()
        for cp in first + passed:
            cp.wait_send()
        mine.wait()

    return pl.pallas_call(
        body, name=name, out_shape=_sds((8 * m, n), blk.dtype),
        in_specs=[pl.BlockSpec(memory_space=space)], out_specs=pl.BlockSpec(memory_space=space),
        scratch_shapes=[pltpu.SemaphoreType.DMA((7,)), pltpu.SemaphoreType.DMA((7,)),
                        pltpu.SemaphoreType.DMA],
        compiler_params=pltpu.CompilerParams(vmem_limit_bytes=VMEM_LIMIT))(blk)


def _hbm_specs(n):
    return [pl.BlockSpec(memory_space=pl.ANY)] * n


def _swap_halves_sibling(name, srcs):
    nw = len(srcs)

    def body(*refs):
        src_refs, dst_refs, send_sems, recv_sems = refs[:nw], refs[nw:2 * nw], refs[2 * nw], refs[2 * nw + 1]
        x, y, c = _place()
        cps = [pltpu.make_async_remote_copy(
            src_ref=src_refs[w].at[pl.ds(0, 4), 1 - c], dst_ref=dst_refs[w],
            send_sem=send_sems.at[w], recv_sem=recv_sems.at[w], device_id=(x, y, 1 - c),
            device_id_type=MESH) for w in range(nw)]
        for cp in cps:
            cp.start()
        for cp in cps:
            cp.wait()

    return pl.pallas_call(
        body, name=name, out_shape=[_sds(s.shape[:1] + s.shape[2:], s.dtype) for s in srcs],
        in_specs=_hbm_specs(nw), out_specs=_hbm_specs(nw),
        scratch_shapes=[pltpu.SemaphoreType.DMA((nw,)), pltpu.SemaphoreType.DMA((nw,))])(*srcs)


def _split_start(name, srcs, lands, copies, per_array, after):
    nw = len(srcs)

    def body(*refs):
        send_sems, recv_sems, token = refs[2 * nw + 1], refs[2 * nw + 2], refs[-1]
        for w in range(nw):
            for k, (s, d, dev) in enumerate(copies(refs[w], refs[nw + w], *_place())):
                pltpu.make_async_remote_copy(
                    src_ref=s, dst_ref=d, send_sem=send_sems.at[w * per_array + k],
                    recv_sem=recv_sems.at[w * per_array + k], device_id=dev, device_id_type=MESH).start()
        token[...] = jnp.zeros_like(token)

    hbm, sem = pl.BlockSpec(memory_space=pltpu.HBM), pl.BlockSpec(memory_space=pltpu.SEMAPHORE)
    arrays = list(srcs) + list(lands)
    out = pl.pallas_call(
        body, name=name,
        out_shape=(pltpu.SemaphoreType.DMA((nw * per_array,)), pltpu.SemaphoreType.DMA((nw * per_array,)),
                   *[pltpu.HBM(a.shape, a.dtype) for a in arrays], _sds((8, 128), F32)),
        in_specs=[hbm] * (2 * nw) + [pl.BlockSpec(memory_space=pl.ANY)],
        out_specs=(sem, sem, *[hbm] * (2 * nw), pl.BlockSpec(memory_space=pltpu.VMEM)),
        input_output_aliases={i: 2 + i for i in range(2 * nw)},
        compiler_params=pltpu.CompilerParams(has_side_effects=pltpu.SideEffectType.DATAFLOW_SIDE_EFFECTING))(
            *[pltpu.with_memory_space_constraint(a, pltpu.HBM) for a in arrays], after)
    return out[0], out[1], out[2:2 + nw], out[2 + nw:2 + 2 * nw], out[-1]


def _split_wait(name, started, after, waits, per_array):
    send_sems, recv_sems, srcs, lands, _ = started
    nw = len(srcs)

    def body(*refs):
        send_sems, recv_sems = refs[2 * nw], refs[2 * nw + 1]
        x, y, c = _place()
        for w in range(nw):
            for k, (s, d) in enumerate(waits(refs[w], refs[nw + w], x, y, c)):
                cp = pltpu.make_async_remote_copy(
                    src_ref=s, dst_ref=d, send_sem=send_sems.at[w * per_array + k],
                    recv_sem=recv_sems.at[w * per_array + k], device_id=(x, y, 1 - c),
                    device_id_type=MESH)
                cp.wait_send()
                cp.wait_recv()

    hbm, sem = pl.BlockSpec(memory_space=pltpu.HBM), pl.BlockSpec(memory_space=pltpu.SEMAPHORE)
    arrays = list(srcs) + list(lands)
    out = pl.pallas_call(
        body, name=name, out_shape=tuple(pltpu.HBM(a.shape, a.dtype) for a in arrays),
        in_specs=[hbm] * (2 * nw) + [sem, sem, pl.BlockSpec(memory_space=pl.ANY)],
        out_specs=tuple([hbm] * (2 * nw)), input_output_aliases={i: i for i in range(2 * nw)},
        compiler_params=pltpu.CompilerParams(has_side_effects=pltpu.SideEffectType.DATAFLOW_SIDE_EFFECTING))(
            *arrays, send_sems, recv_sems, after)
    return list(out[nw:])


def _other_chips(x, y):
    return [(1 - x, y), (x, 1 - y), (1 - x, 1 - y)]


def _gather_sends(src_ref, land_ref, x, y, c):
    to = land_ref.at[2 * x + y, c]
    return [(src_ref, to, (x, y, 1 - c))] + [(src_ref, to, (px, py, c)) for px, py in _other_chips(x, y)]


def _gather_lands(src_ref, land_ref, x, y, c):
    return [(src_ref, land_ref.at[2 * x + y, 1 - c])] + [
        (src_ref, land_ref.at[2 * px + py, c]) for px, py in _other_chips(x, y)]


def _gather_sends_all(src_ref, land_ref, x, y, c):
    to = land_ref.at[2 * x + y, c]
    return [(src_ref, to, (x, y, 1 - c))] + [
        (src_ref, to, (px, py, pc)) for px, py in _other_chips(x, y) for pc in (c, 1 - c)]


def _gather_lands_all(src_ref, land_ref, x, y, c):
    return [(src_ref, land_ref.at[2 * x + y, 1 - c])] + [
        (src_ref, land_ref.at[2 * px + py, pc]) for px, py in _other_chips(x, y) for pc in (c, 1 - c)]


def _scatter_sends(src_ref, land_ref, x, y, c):
    return [(src_ref.at[2 * px + py], land_ref.at[2 * x + y], (px, py, c)) for px, py in _other_chips(x, y)]


def _scatter_lands(src_ref, land_ref, x, y, c):
    return [(src_ref.at[2 * x + y], land_ref.at[2 * px + py]) for px, py in _other_chips(x, y)]


def _forward_sibling(name, lands):
    nw = len(lands)

    def body(*refs):
        land_refs, out_refs, send_sems, recv_sems = refs[:nw], refs[nw:2 * nw], refs[2 * nw], refs[2 * nw + 1]
        x, y, c = _place()
        cps = []
        for w in range(nw):
            cps += [pltpu.make_async_remote_copy(
                src_ref=land_refs[w].at[2 * px + py, c], dst_ref=out_refs[w].at[2 * px + py, c],
                send_sem=send_sems.at[w, j], recv_sem=recv_sems.at[w, j], device_id=(x, y, 1 - c),
                device_id_type=MESH) for j, (px, py) in enumerate(_other_chips(x, y))]
        for cp in cps:
            cp.start()
        for w in range(nw):
            for j, (px, py) in enumerate(_other_chips(x, y)):
                slot = out_refs[w].at[2 * px + py, 1 - c]
                pltpu.make_async_remote_copy(src_ref=slot, dst_ref=slot, send_sem=send_sems.at[w, j],
                                             recv_sem=recv_sems.at[w, j], device_id=(x, y, 1 - c),
                                             device_id_type=MESH).wait_recv()
        for cp in cps:
            cp.wait_send()

    return pl.pallas_call(
        body, name=name, out_shape=[_sds(a.shape, a.dtype) for a in lands],
        in_specs=_hbm_specs(nw), out_specs=_hbm_specs(nw), input_output_aliases={i: i for i in range(nw)},
        scratch_shapes=[pltpu.SemaphoreType.DMA((nw, 3)), pltpu.SemaphoreType.DMA((nw, 3))])(*lands)


def _share_halves(name, halves):
    nw = len(halves)

    def body(*refs):
        in_refs, out_refs, send_sems, recv_sems = refs[:nw], refs[nw:2 * nw], refs[2 * nw], refs[2 * nw + 1]
        x, y, c = _place()
        cps = [pltpu.make_async_remote_copy(
            src_ref=in_refs[w].at[c], dst_ref=out_refs[w].at[c], send_sem=send_sems.at[w],
            recv_sem=recv_sems.at[w], device_id=(x, y, 1 - c), device_id_type=MESH) for w in range(nw)]
        for cp in cps:
            cp.start()
        for w in range(nw):
            slot = out_refs[w].at[1 - c]
            pltpu.make_async_remote_copy(src_ref=slot, dst_ref=slot, send_sem=send_sems.at[w],
                                         recv_sem=recv_sems.at[w], device_id=(x, y, 1 - c),
                                         device_id_type=MESH).wait_recv()
        for cp in cps:
            cp.wait_send()

    return pl.pallas_call(
        body, name=name, out_shape=[_sds(a.shape, a.dtype) for a in halves],
        in_specs=_hbm_specs(nw), out_specs=_hbm_specs(nw), input_output_aliases={i: i for i in range(nw)},
        scratch_shapes=[pltpu.SemaphoreType.DMA((nw,)), pltpu.SemaphoreType.DMA((nw,))])(*halves)


def _place_blocks(name, blks, place):
    nw = len(blks)

    def body(p_ref, *refs):
        for b_ref, o_ref in zip(refs[:nw], refs[nw:]):
            o_ref[...] = b_ref[...]

    return pl.pallas_call(
        body, name=name,
        grid_spec=pltpu.PrefetchScalarGridSpec(
            num_scalar_prefetch=1, grid=(1,),
            in_specs=[pl.BlockSpec(b.shape, lambda i, p: (0, 0)) for b in blks],
            out_specs=[pl.BlockSpec((None, None) + b.shape, lambda i, p: (p[0], p[1], 0, 0)) for b in blks]),
        out_shape=[_sds((4, 2) + b.shape, b.dtype) for b in blks],
        compiler_params=_params(("arbitrary",)))(place, *blks)


def _pair_sum(name, fulls, gots, place):
    nw = len(fulls)

    def body(p_ref, *refs):
        s = pl.program_id(0)
        for a_ref, b_ref, o_ref, l_ref in zip(refs[:nw], refs[nw:2 * nw], refs[2 * nw:3 * nw], refs[3 * nw:]):
            o_ref[...] = (a_ref[...].astype(F32) + b_ref[...].astype(F32)).astype(o_ref.dtype)

            @pl.when(s == p_ref[0])
            def _():
                l_ref[...] = o_ref[...]

    slab = lambda a: pl.BlockSpec((None,) + a.shape[1:], lambda s, p: (s, 0, 0))
    mine = lambda a: pl.BlockSpec((None,) + a.shape[1:], lambda s, p: (p[0], 0, 0))
    out = pl.pallas_call(
        body, name=name,
        grid_spec=pltpu.PrefetchScalarGridSpec(
            num_scalar_prefetch=1, grid=(4,),
            in_specs=[pl.BlockSpec((None, None) + a.shape[2:], lambda s, p: (s, p[1], 0, 0)) for a in fulls]
            + [slab(b) for b in gots],
            out_specs=[slab(b) for b in gots] + [mine(b) for b in gots]),
        out_shape=[_sds(b.shape, BF16) for b in gots] * 2,
        compiler_params=_params(("arbitrary",)))(place, *fulls, *gots)
    return out[:nw], out[nw:]


def _sum4(name, arrs, place):
    nw = len(arrs)

    def body(p_ref, *refs):
        for a_ref, o_ref in zip(refs[:nw], refs[nw:]):
            acc = a_ref[0].astype(F32)
            for s in range(1, 4):
                acc = acc + a_ref[s].astype(F32)
            o_ref[...] = acc

    return pl.pallas_call(
        body, name=name,
        grid_spec=pltpu.PrefetchScalarGridSpec(
            num_scalar_prefetch=1, grid=(1,),
            in_specs=[pl.BlockSpec(a.shape, lambda i, p: (0, 0, 0)) for a in arrs],
            out_specs=[pl.BlockSpec((None,) + a.shape[1:], lambda i, p: (p[1], 0, 0)) for a in arrs]),
        out_shape=[_sds((2,) + a.shape[1:], F32) for a in arrs],
        compiler_params=_params(("arbitrary",)))(place, *arrs)


def _small_update(gathered, w, m, v):
    n = w.shape[1]
    tn = 2048
    c1 = 1.0 - ADAM_B1 ** ADAM_STEP
    c2 = 1.0 - ADAM_B2 ** ADAM_STEP

    def body(g_ref, w_ref, m_ref, v_ref, go_ref, d_ref, mo_ref, vo_ref):
        g = g_ref[0:1, :]
        for d in range(1, 8):
            g = g + g_ref[d:d + 1, :]
        go_ref[...] = g
        m = ADAM_B1 * m_ref[...] + (1.0 - ADAM_B1) * g
        v = ADAM_B2 * v_ref[...] + (1.0 - ADAM_B2) * (g * g)
        mo_ref[...] = m
        vo_ref[...] = v
        d_ref[...] = -ADAM_LR * ((m / c1) / (jnp.sqrt(v / c2) + ADAM_EPS) + ADAM_WD * w_ref[...])

    row = pl.BlockSpec((1, tn), lambda i: (0, i))
    return pl.pallas_call(
        body, name="small_update", grid=(n // tn,),
        in_specs=[pl.BlockSpec((8, tn), lambda i: (0, i)), row, row, row], out_specs=[row] * 4,
        out_shape=[_sds((1, n), F32)] * 4, compiler_params=_params(("parallel",)))(gathered, w, m, v)


def _swiglu(ps, es):
    g, u = ps
    return g * _sigmoid(g) * u, g, u


def _swiglu_bwd(ps, es):
    g, u = es[0].astype(F32), es[1].astype(F32)
    sg = _sigmoid(g)
    return ps[0] * u * (sg * (1.0 + g * (1.0 - sg))), ps[0] * (g * sg)


def _merge(ps, es):
    ga, gm = [e.astype(F32) for e in es]
    return _sigmoid(ga) * ps[0] + _sigmoid(gm) * ps[1], ps[0], ps[1]


def _merge_bwd(ps, es):
    a, b, ga, gm = [e.astype(F32) for e in es]
    sa, sm = _sigmoid(ga), _sigmoid(gm)
    dm = ps[0]
    return dm * sa, dm * sm, dm * a * (sa * (1.0 - sa)), dm * b * (sm * (1.0 - sm))


W_IN_PIECES = (("q", 512), ("kv", 256), ("mqk", 1024), ("mv", 512), ("mo", 512), ("if", 8),
               ("ga", 1024), ("gm", 1024))


def _local_step(x, tgt, pos_col, mod, sp, in_weights, late_weights, ffn_grads, mixer_grads):
    sh_m, sc_m, gate_m, sh_f, sc_f, gate_f = mod
    inv = ROPE_THETA ** (-2.0 * jnp.arange(HEAD_DIM // 2, dtype=F32) / HEAD_DIM)
    cos, sin = _rope_tables(pos_col, jnp.tile(inv, 4).reshape(1, 128))
    W = dict(in_weights(cos))
    h, pa, pqk, pvo, pif, pg = _proj_in(x, sp["g_pre_mix"], sc_m, sh_m, [
        (W["q+kv"], F32, 256), (W["mqk"], F32, 512), (W["mv+mo"], BF16, 512), (W["if"], F32, 128),
        (W["ga+gm"], BF16, 512)])
    ya = _attn_fwd(pa, cos, sin, sp["sinks"])
    qk = _conv_fwd(pqk, sp["conv_w"], sp["conv_b"])
    bcol = jnp.pad(sp["b_if"], ((0, 0), (0, 120)))
    brow = jnp.broadcast_to(sp["b_if"].reshape(8, 1), (8, 128))
    grow = pif[:, :8].T
    hm, cs, ns, ms = _mlstm_fwd(qk, pvo, pif, bcol, grow, brow)
    ym = _mlstm_out(hm, pvo, sp["norm_w"])
    W.update(late_weights(ym))
    w_fg, w_fu, w_fd = W["fg"], W["fu"], W["fd"]
    merged, br_a, br_m = _mm("branches", [[(ya, W["ba"])], [(ym, W["bm"])]],
                             [(pg, 0), (pg, 1)], _merge, [BF16, BF16, BF16], cn=512, nt=True)
    wide, narrow = (D_MODEL, F32), (D_MODEL, BF16)
    mix, x1, h2 = _mm_rows("mix_out", [[(merged, W["out"])]],
                           [x, gate_m, sp["g_post_mix"], sp["g_pre_ffn"], sc_f, sh_f],
                           _res_norm_rows, [wide, wide, narrow], [], cn=512)
    act, gt, up = _mm("ffn_in", [[(h2, w_fg)], [(h2, w_fu)]], [], _swiglu, [BF16] * 3,
                      cn=256, nt=True)
    dy, dff, acc_l, loss = _mm_rows("ffn_down", [[(act, w_fd)]], [x1, tgt, gate_f, sp["g_post_ffn"]],
                                    _final_loss_rows, [wide, narrow], [(8, D_MODEL), (1, 128)], cn=512)

    G = {}
    dgt, dup = _mm("ffn_down_bwd", [[(dff, w_fd)]], [gt, up], _swiglu_bwd, [BF16, BF16],
                   cn=256, nt=True)
    g_fd = _mm_tn("dw_ffn_down", act, dff, BF16, 1408, 512)
    g_fg = _mm_tn("dw_ffn_gate", dgt, h2, BF16, 1408, 1024)
    g_fu = _mm_tn("dw_ffn_up", dup, h2, BF16, 1408, 1024)
    tie = ffn_grads(g_fg, g_fu, g_fd)
    dx1, dmix, acc_r = _mm_rows(
        "ffn_in_bwd", [[(dgt, w_fg), (dup, w_fu)]],
        [x1, mix, dy, sc_f + tie, gate_m, sp["g_pre_ffn"], sp["g_post_mix"]],
        _res_norm_bwd_rows, [wide, narrow], [(8, D_MODEL)], cn=512, tm=256)
    d_a, d_m, dga, dgm = _mm("mix_out_bwd", [[(dmix, W["out"])]],
                             [br_a, br_m, (pg, 0), (pg, 1)], _merge_bwd,
                             [BF16] * 4, cn=512, nt=True)
    G["out"] = _mm_tn("dw_out", merged, dmix, BF16, 1024, 512)
    dya, = _mm("branch_attn_bwd", [[(d_a, W["ba"])]], [], _first, [F32], cn=512)
    heads = MLSTM_HEADS * MLSTM_HEAD_DIM
    dhm, do_m, acc_n = _mm_rows("branch_mlstm_bwd", [[(d_m, W["bm"])]], [hm, pvo, sp["norm_w"]],
                                _mlstm_out_bwd_rows, [(heads, F32), (heads, BF16)], [(8, heads)], cn=512)
    G["ba"] = _mm_tn("dw_branch_attn", d_a, ya, BF16, 1024, 512)
    G["bm"] = _mm_tn("dw_branch_mlstm", d_m, ym, BF16, 1024, 512)
    dqk, dv_m, dgc, dgr = _mlstm_bwd(qk, pvo, pif, bcol, grow, brow, cs, ns, ms, dhm)
    dif, acc_g = _gate_bwd(dgc, dgr, pif, bcol)
    du, acc_c = _conv_bwd(pqk, sp["conv_w"], sp["conv_b"], dqk)
    dq_a, dkv, dsink = _attn_bwd(pa, cos, sin, sp["sinks"], dya)
    dproj = {"q": dq_a, "kv": dkv, "mqk": du, "mv": dv_m, "mo": do_m, "if": dif, "ga": dga, "gm": dgm}
    names = [k for k, _ in W_IN_PIECES]
    for part in (names[:4], names[4:]):
        G.update(zip(part, _mm_tn_group("dw_in_from_" + part[0], [dproj[k] for k in part], h, BF16)))
    w_tied = dict(W, **{"if": W["if"] + mixer_grads(G).astype(BF16)})
    dx, acc_p = _mm_rows("proj_bwd", [[(dproj[k], w_tied[k]) for k, _ in W_IN_PIECES]],
                         [x, dx1, sp["g_pre_mix"], sc_m], _pre_norm_bwd_rows, [wide], [(8, D_MODEL)], cn=512)

    small = {
        "mod": jnp.concatenate([acc_p[1], acc_p[0], acc_r[3], acc_r[1], acc_r[0], acc_l[0]]),
        "g_pre_mix": acc_p[2], "g_post_mix": acc_r[4], "b_if": acc_g[0, :8],
        "conv_w": acc_c[:CONV_WIDTH].reshape(-1), "conv_b": acc_c[CONV_WIDTH],
        "sinks": dsink[:, 0], "norm_w": acc_n[0], "g_pre_ffn": acc_r[2], "g_post_ffn": acc_l[1]}
    return loss, dx, small


IN_WIDTH = sum(n for _, n in W_IN_PIECES)
IN_SHARD = IN_WIDTH // 4
IN_SHARD_PAD = -(-IN_SHARD // 32) * 32


def _split_w_in(w_in_t):
    out, off, start = {}, 0, {}
    for k, n in W_IN_PIECES:
        out[k], start[k] = w_in_t[off:off + n], off
        off += n
    out["if"] = jnp.pad(out["if"], ((0, 120), (0, 0)))
    for name, first, last in (("q+kv", "q", "kv"), ("mv+mo", "mv", "mo"), ("ga+gm", "ga", "gm")):
        out[name] = w_in_t[start[first]:start[last] + out[last].shape[0]]
    return out


def _halves(a):
    return a.reshape(4, 2, a.shape[0] // 8, a.shape[1])


SMALL = (("b_ada", 6144), ("g_pre_mix", 1024), ("g_post_mix", 1024), ("b_if", 128), ("conv_w", 4096),
         ("conv_b", 1024), ("sinks", 128), ("norm_w", 512), ("g_pre_ffn", 1024), ("g_post_ffn", 1024))
SMALL_LEN = 8 * 2048


def _pack_small(vals):
    parts = []
    for k, n in SMALL:
        v = vals[k].reshape(-1)
        parts.append(jnp.pad(v, (0, n - v.shape[0])))
    flat = jnp.concatenate(parts)
    return jnp.pad(flat, (0, SMALL_LEN - flat.shape[0]))


def _unpack_small(flat, shapes):
    out, off = {}, 0
    for k, n in SMALL:
        size = 1
        for d in shapes[k]:
            size *= d
        out[k] = flat[off:off + size].reshape(shapes[k])
        off += n
    return out


def kernel(x, c, positions, w_ada, b_ada, g_pre_mix, g_post_mix, w_in, b_if, conv_w, conv_b, attn_sinks, mlstm_norm_w, w_branch_attn, w_branch_mlstm, w_out, g_pre_ffn, g_post_ffn, w_ffn_gate, w_ffn_up, w_ffn_down, loss_target, m_w_ada, m_b_ada, m_g_pre_mix, m_g_post_mix, m_w_in, m_b_if, m_conv_w, m_conv_b, m_attn_sinks, m_mlstm_norm_w, m_w_branch_attn, m_w_branch_mlstm, m_w_out, m_g_pre_ffn, m_g_post_ffn, m_w_ffn_gate, m_w_ffn_up, m_w_ffn_down, v_w_ada, v_b_ada, v_g_pre_mix, v_g_post_mix, v_w_in, v_b_if, v_conv_w, v_conv_b, v_attn_sinks, v_mlstm_norm_w, v_w_branch_attn, v_w_branch_mlstm, v_w_out, v_g_pre_ffn, v_g_post_ffn, v_w_ffn_gate, v_w_ffn_up, v_w_ffn_down):
    xi, yi, ci = _place()
    chip = 2 * xi + yi
    dev = 2 * chip + ci
    T = x.shape[1]
    ada_cols = w_ada.shape[2]

    place = jnp.stack([chip, ci]).astype(jnp.int32)

    def my_half(a):
        n = a.shape[0] // 2
        return lax.dynamic_slice_in_dim(a, ci * n, n, axis=0).astype(BF16)

    blk = jnp.concatenate([c.reshape(-1), conv_w.reshape(-1)]).reshape(8, 256)
    got = _all_gather8("gather_cond", blk, pltpu.VMEM).reshape(8, 2048)
    c_all = got[:, :D_MODEL].astype(BF16)
    conv_full = got[::2, D_MODEL:].reshape(4, CONV_WIDTH, -1).transpose(1, 0, 2).reshape(CONV_WIDTH, -1)

    b_sh = lax.dynamic_slice_in_dim(b_ada, chip * ada_cols, ada_cols, axis=1)
    mod_part, = _mm("ada_mod", [[(c_all, w_ada[0].astype(BF16))]], [b_sh],
                    lambda ps, es: (ps[0] + es[0],), [F32], cn=512, tm=8)
    mod_all = _all_gather8("gather_mod", mod_part, pltpu.VMEM).reshape(4, 2, 8, ada_cols)[:, 0]
    mod = lax.dynamic_index_in_dim(mod_all, dev, axis=1, keepdims=False).reshape(6, 1, D_MODEL)

    def gather_start(name, blks, after, sends, copies):
        return _split_start(name + "_start", blks, _place_blocks(name + "_place", blks, place),
                            sends, copies, after)

    w_in_t = jnp.pad(w_in[0].T, ((0, IN_SHARD_PAD - IN_SHARD), (0, 0)))
    in_started = gather_start("in_gather", [my_half(w_in_t)], mod, _gather_sends, 4)
    late_keys = ("fg", "fu", "fd", "out", "ba", "bm")
    late_started = gather_start(
        "late_gather",
        [my_half(w_ffn_gate[0].T), my_half(w_ffn_up[0].T), my_half(w_ffn_down[0]), my_half(w_out[0]),
         my_half(w_branch_attn[0].T), my_half(w_branch_mlstm[0].T)], in_started[4], _gather_sends_all, 7)
    mod = mod + (in_started[4][0, 0] + late_started[4][0, 0])

    def in_weights(after):
        g_in, = _forward_sibling("in_gather_forward",
                                 _split_wait("in_gather_wait", in_started, after, _gather_lands, 4))
        return _split_w_in(g_in.reshape(4, IN_SHARD_PAD, D_MODEL)[:, :IN_SHARD].reshape(IN_WIDTH, D_MODEL))

    def late_weights(after):
        lands = _split_wait("late_gather_wait", late_started, after, _gather_lands_all, 7)
        return {k: a.reshape(-1, a.shape[-1]) for k, a in zip(late_keys, lands)}

    sent = {}

    def scatter_start(name, groups):
        pairs, lands = _pair_sum(name + "_pair_sum", groups, _swap_halves_sibling(name + "_pair", groups), place)
        sent[name] = _split_start(name + "_start", pairs, lands, _scatter_sends, 3, pairs[0])
        return sent[name][4][0, 0]

    def ffn_grads(g_fg, g_fu, g_fd):
        return scatter_start("rs_ffn", [_halves(g_fg), _halves(g_fu), _halves(g_fd)])

    def mixer_grads(G):
        g_in_t = jnp.concatenate([G[k][:n] for k, n in W_IN_PIECES]).reshape(4, IN_SHARD, D_MODEL)
        g_in_t = jnp.pad(g_in_t, ((0, 0), (0, IN_SHARD_PAD - IN_SHARD), (0, 0)))
        return scatter_start("rs_mix", [g_in_t.reshape(4, 2, IN_SHARD_PAD // 2, D_MODEL), _halves(G["out"]),
                                        _halves(G["ba"]), _halves(G["bm"])])

    sp = {"g_pre_mix": g_pre_mix, "g_post_mix": g_post_mix, "b_if": b_if, "conv_w": conv_full,
          "conv_b": conv_b, "sinks": attn_sinks, "norm_w": mlstm_norm_w, "g_pre_ffn": g_pre_ffn,
          "g_post_ffn": g_post_ffn}
    loss, dx, small = _local_step(x[0], loss_target[0], positions.reshape(T, 1), [mod[i] for i in range(6)],
                                  sp, in_weights, late_weights, ffn_grads, mixer_grads)

    reds = (_sum4("rs_ffn_chip_sum", _split_wait("rs_ffn_wait", sent["rs_ffn"], dx, _scatter_lands, 3), place)
            + _sum4("rs_mix_chip_sum", _split_wait("rs_mix_wait", sent["rs_mix"], dx, _scatter_lands, 3), place))
    gsh = {k: s.reshape(-1, s.shape[-1])
           for k, s in zip(("fg", "fu", "fd", "w_in", "out", "ba", "bm"), _share_halves("rs_share", reds))}
    gsh["w_in"] = gsh["w_in"][:IN_SHARD]

    small["b_ada"] = small.pop("mod")
    vec = _pack_small(small).reshape(8, 2048)
    g_all = _all_gather8("gather_small", vec, pltpu.VMEM).reshape(8, SMALL_LEN)
    dmod_sh = lax.dynamic_slice_in_dim(g_all[:, :6 * D_MODEL], chip * ada_cols, ada_cols, axis=1)
    g_w_ada = _mm_tn("dw_ada", c_all, dmod_sh.astype(BF16), F32, D_MODEL, 512, 8)

    smalls = {"b_ada": (b_ada, m_b_ada, v_b_ada), "g_pre_mix": (g_pre_mix, m_g_pre_mix, v_g_pre_mix),
              "g_post_mix": (g_post_mix, m_g_post_mix, v_g_post_mix), "b_if": (b_if, m_b_if, v_b_if),
              "conv_w": None, "conv_b": (conv_b, m_conv_b, v_conv_b),
              "sinks": (attn_sinks, m_attn_sinks, v_attn_sinks),
              "norm_w": (mlstm_norm_w, m_mlstm_norm_w, v_mlstm_norm_w),
              "g_pre_ffn": (g_pre_ffn, m_g_pre_ffn, v_g_pre_ffn),
              "g_post_ffn": (g_post_ffn, m_g_post_ffn, v_g_post_ffn)}
    shapes = {k: (t[0].shape if t is not None else (1, CONV_WIDTH, D_MODEL)) for k, t in smalls.items()}
    zeros = jnp.zeros((CONV_WIDTH * D_MODEL,), F32)
    packs = [_pack_small({k: (t[i] if t is not None else zeros) for k, t in smalls.items()}).reshape(1, -1)
             for i in range(3)]
    s_out = [_unpack_small(o[0], shapes) for o in _small_update(g_all, *packs)]
    g_conv = lax.dynamic_slice_in_dim(s_out[0]["conv_w"], chip * conv_w.shape[2], conv_w.shape[2], axis=2)

    res = {}
    for k, t in smalls.items():
        if t is not None:
            res[k] = tuple(o[k] for o in s_out)
    res["conv_w"] = (g_conv, *[o[None] for o in _adamw("adam_conv_w", conv_w[0], g_conv[0], m_conv_w[0], v_conv_w[0])])
    res["w_ada"] = (g_w_ada[None], *[o[None] for o in _adamw("adam_w_ada", w_ada[0], g_w_ada, m_w_ada[0], v_w_ada[0])])
    bigs = {"w_in": (w_in, m_w_in, v_w_in), "ba": (w_branch_attn, m_w_branch_attn, v_w_branch_attn),
            "bm": (w_branch_mlstm, m_w_branch_mlstm, v_w_branch_mlstm), "out": (w_out, m_w_out, v_w_out),
            "fg": (w_ffn_gate, m_w_ffn_gate, v_w_ffn_gate), "fu": (w_ffn_up, m_w_ffn_up, v_w_ffn_up),
            "fd": (w_ffn_down, m_w_ffn_down, v_w_ffn_down)}
    for k, (w, m, v) in bigs.items():
        if k in ("w_in", "fg", "fu"):
            res[k] = tuple(o.T[None] for o in (gsh[k], *_adamw("adam_" + k, w[0].T, gsh[k], m[0].T, v[0].T)))
        else:
            g = gsh[k].T if k in ("ba", "bm") else gsh[k]
            res[k] = (g[None], *[o[None] for o in _adamw("adam_" + k, w[0], g, m[0], v[0])])

    order = ("w_ada", "b_ada", "g_pre_mix", "g_post_mix", "w_in", "b_if", "conv_w", "conv_b", "sinks",
             "norm_w", "ba", "bm", "out", "g_pre_ffn", "g_post_ffn", "fg", "fu", "fd")
    total = lax.psum(loss[0, 0], ("x", "y", "c"))
    return (total, dx[None], *[res[k][0] for k in order], *[res[k][1] for k in order],
            *[res[k][2] for k in order], *[res[k][3] for k in order])
```

```python
import functools

import jax
import jax.numpy as jnp
from jax import lax
from jax.experimental import pallas as pl
from jax.experimental.pallas import tpu as pltpu

F32, BF16 = jnp.float32, jnp.bfloat16
MESH = pl.DeviceIdType.MESH

D_MODEL = 1024
N_Q_HEADS, N_KV_HEADS, HEAD_DIM, WINDOW = 8, 2, 64, 128
ROPE_THETA = 10000.0
MLSTM_HEADS, MLSTM_HEAD_DIM, MLSTM_CHUNK, CONV_WIDTH = 4, 128, 64, 4
D_FF = 2816
NORM_EPS = 1e-6
ADAM_LR, ADAM_B1, ADAM_B2, ADAM_EPS, ADAM_WD, ADAM_STEP = 0.001, 0.9, 0.999, 1e-08, 0.01, 10

VMEM_LIMIT = 56 * 1024 * 1024
ROW_TILE = 256
MM_TM = 512
MM_TT = 1024
ATTN_BLK = WINDOW
STEP_ROWS = 2 * MLSTM_CHUNK
NEG_INF = float("-inf")


def _params(sem):
    return pltpu.CompilerParams(dimension_semantics=sem, vmem_limit_bytes=VMEM_LIMIT)


def _sds(shape, dtype):
    return jax.ShapeDtypeStruct(shape, dtype)


def _sigmoid(x):
    return 1.0 / (1.0 + jnp.exp(-x))


def _dot(a, b, ca, cb):
    return lax.dot_general(a, b, (((ca,), (cb,)), ((), ())), preferred_element_type=F32)


def _bdot(a, b, ca, cb):
    return lax.dot_general(a, b, (((ca,), (cb,)), ((0,), (0,))), preferred_element_type=F32)


def _bdot_rows(a, b):
    return jnp.stack([_dot(a[h], b[h], 0, 0) for h in range(a.shape[0])])


def _mm(name, prods, extras, epi, out_dtypes, cn, nt=False, tm=MM_TM):
    flat = [ab for p in prods for ab in p]
    counts = [len(p) for p in prods]
    M = flat[0][0].shape[0]
    N = flat[0][1].shape[0 if nt else 1]
    tm = min(tm, M)
    n_in = 2 * len(flat) + len(extras)

    def body(*refs):
        ins, outs = refs[:n_in], refs[n_in:]
        for j in range(N // cn):
            cols = slice(j * cn, (j + 1) * cn)
            k, ps = 0, []
            for cnt in counts:
                acc = None
                for _ in range(cnt):
                    b = ins[k + 1][cols, :] if nt else ins[k + 1][:, cols]
                    d = _dot(ins[k][...], b, 1, 1 if nt else 0)
                    acc = d if acc is None else acc + d
                    k += 2
                ps.append(acc)
            res = epi(ps, [r[:, cols] for r in ins[k:]])
            for o, r in zip(outs, res):
                o[:, cols] = r.astype(o.dtype)

    in_specs, args = [], []
    for a, b in flat:
        in_specs.append(pl.BlockSpec((tm, a.shape[1]), lambda i: (i, 0)))
        in_specs.append(pl.BlockSpec(b.shape, lambda i: (0, 0), pipeline_mode=pl.Buffered(1)))
        args += [a, b]
    for e in extras:
        e, off = e if isinstance(e, tuple) else (e, 0)
        rows = 1 if e.shape[0] == 1 else tm
        in_specs.append(pl.BlockSpec((rows, N), lambda i, off=off, rows=rows: (0 if rows == 1 else i, off)))
        args.append(e)
    return pl.pallas_call(
        body, name=name, grid=(M // tm,), in_specs=in_specs,
        out_specs=[pl.BlockSpec((tm, N), lambda i: (i, 0)) for _ in out_dtypes],
        out_shape=[_sds((M, N), dt) for dt in out_dtypes],
        compiler_params=_params(("parallel",)))(*args)


def _mm_rows(name, prods, extras, epi, outs, accs, cn, nt=False, tm=MM_TM):
    flat = [ab for p in prods for ab in p]
    counts = [len(p) for p in prods]
    M = flat[0][0].shape[0]
    N = flat[0][1].shape[0 if nt else 1]
    tm = min(tm, M)
    n_mm, n_in, n_out = 2 * len(flat), 2 * len(flat) + len(extras), len(outs)

    def body(*refs):
        ins, out_refs, acc_refs = refs[:n_in], refs[n_in:n_in + n_out], refs[n_in + n_out:]

        @pl.when(pl.program_id(0) == 0)
        def _():
            for a in acc_refs:
                a[...] = jnp.zeros_like(a)

        chunks = [[] for _ in counts]
        for j in range(N // cn):
            cols = slice(j * cn, (j + 1) * cn)
            k = 0
            for p, cnt in enumerate(counts):
                acc = None
                for _ in range(cnt):
                    b = ins[k + 1][cols, :] if nt else ins[k + 1][:, cols]
                    d = _dot(ins[k][...], b, 1, 1 if nt else 0)
                    acc = d if acc is None else acc + d
                    k += 2
                chunks[p].append(acc)
        ps = [c[0] if len(c) == 1 else jnp.concatenate(c, axis=1) for c in chunks]
        res, incs = epi(ps, [r[...] for r in ins[n_mm:]])
        for o, r in zip(out_refs, res):
            o[...] = r.astype(o.dtype)
        for a, inc in zip(acc_refs, incs):
            a[...] += inc

    in_specs, args = [], []
    for a, b in flat:
        in_specs.append(pl.BlockSpec((tm, a.shape[1]), lambda i: (i, 0)))
        in_specs.append(pl.BlockSpec(b.shape, lambda i: (0, 0), pipeline_mode=pl.Buffered(1)))
        args += [a, b]
    for e in extras:
        rows = 1 if e.shape[0] == 1 else tm
        in_specs.append(pl.BlockSpec((rows, e.shape[1]), lambda i, rows=rows: (0 if rows == 1 else i, 0)))
        args.append(e)
    return pl.pallas_call(
        body, name=name, grid=(M // tm,), in_specs=in_specs,
        out_specs=[pl.BlockSpec((tm, w), lambda i: (i, 0)) for w, _ in outs]
        + [pl.BlockSpec(s, lambda i: (0, 0)) for s in accs],
        out_shape=[_sds((M, w), dt) for w, dt in outs] + [_sds(s, F32) for s in accs],
        compiler_params=_params(("arbitrary",)))(*args)


def _mm_tn_group(name, pieces, b, out_dtype, tt=MM_TT):
    T, N = b.shape
    tt = min(tt, T)
    steps, n = T // tt, len(pieces)

    def body(*refs):
        a_refs, b_ref, out_refs, accs = refs[:n], refs[n], refs[n + 1:2 * n + 1], refs[2 * n + 1:]
        t = pl.program_id(0)

        @pl.when(t == 0)
        def _():
            for acc in accs:
                acc[...] = jnp.zeros_like(acc)

        for a_ref, acc in zip(a_refs, accs):
            acc[...] += _dot(a_ref[...], b_ref[...], 0, 0)

        @pl.when(t == steps - 1)
        def _():
            for o_ref, acc in zip(out_refs, accs):
                o_ref[...] = acc[...].astype(o_ref.dtype)

    return pl.pallas_call(
        body, name=name, grid=(steps,),
        in_specs=[pl.BlockSpec((tt, a.shape[1]), lambda t: (t, 0)) for a in pieces]
        + [pl.BlockSpec((tt, N), lambda t: (t, 0))],
        out_specs=[pl.BlockSpec((a.shape[1], N), lambda t: (0, 0)) for a in pieces],
        out_shape=[_sds((a.shape[1], N), out_dtype) for a in pieces],
        scratch_shapes=[pltpu.VMEM((a.shape[1], N), F32) for a in pieces],
        compiler_params=_params(("arbitrary",)))(*pieces, b)


def _first(ps, es):
    return (ps[0],)


def _rows(name, body, ins, out_shapes, T, tr=ROW_TILE):
    tr = min(tr, T)

    def spec(shape):
        if shape[0] == T:
            return pl.BlockSpec((tr,) + tuple(shape[1:]), lambda i: (i,) + (0,) * (len(shape) - 1))
        return pl.BlockSpec(tuple(shape), lambda i: (0,) * len(shape))

    return pl.pallas_call(
        body, name=name, grid=(T // tr,),
        in_specs=[spec(a.shape) for a in ins], out_specs=[spec(s.shape) for s in out_shapes],
        out_shape=out_shapes, compiler_params=_params(("arbitrary",)))(*ins)


def _rms(x):
    r = lax.rsqrt(jnp.mean(x * x, axis=-1, keepdims=True) + NORM_EPS)
    return x * r, r


def _rms_bwd(dxn, xn, r):
    return r * (dxn - xn * jnp.mean(dxn * xn, axis=-1, keepdims=True))


def _colsum(v):
    return jnp.sum(v, axis=0, keepdims=True)


def _proj_in(x, g, sc, sh, groups):
    T = x.shape[0]
    tm = min(MM_TM, T)
    ng = len(groups)

    def body(x_ref, g_ref, sc_ref, sh_ref, *rest):
        w_refs, h_ref, out_refs = rest[:ng], rest[ng], rest[ng + 1:]
        xn, _ = _rms(x_ref[...])
        h = (xn * g_ref[...] * (1.0 + sc_ref[...]) + sh_ref[...]).astype(BF16)
        h_ref[...] = h
        for w_ref, o_ref, (w, _, cn) in zip(w_refs, out_refs, groups):
            for j in range(w.shape[0] // cn):
                cols = slice(j * cn, (j + 1) * cn)
                o_ref[:, cols] = _dot(h, w_ref[cols, :], 1, 1).astype(o_ref.dtype)

    row = pl.BlockSpec((1, D_MODEL), lambda i: (0, 0))
    tile = lambda w: pl.BlockSpec((tm, w), lambda i: (i, 0))
    return pl.pallas_call(
        body, name="proj_in", grid=(T // tm,),
        in_specs=[tile(D_MODEL), row, row, row] + [
            pl.BlockSpec(w.shape, lambda i: (0, 0), pipeline_mode=pl.Buffered(1)) for w, _, _ in groups],
        out_specs=[tile(D_MODEL)] + [tile(w.shape[0]) for w, _, _ in groups],
        out_shape=[_sds((T, D_MODEL), BF16)] + [_sds((T, w.shape[0]), dt) for w, dt, _ in groups],
        compiler_params=_params(("parallel",)))(x, g, sc, sh, *[w for w, _, _ in groups])


def _acc_rows(rows):
    w = rows[0].shape[1]
    return jnp.concatenate(rows + [jnp.zeros((8 - len(rows), w), F32)], axis=0)


def _res_norm_rows(ps, es):
    mix = ps[0]
    x, gate, gp, g2, sc, sh = es
    mh, _ = _rms(mix)
    x1 = x + gate * (mh * gp)
    xn, _ = _rms(x1)
    return [mix, x1, xn * g2 * (1.0 + sc) + sh], []


def _final_loss_rows(ps, es):
    x1, tgt, gate, gp = es
    fh, r = _rms(ps[0])
    e = x1 + gate * (fh * gp) - tgt
    loss = 0.5 * jnp.sum(jnp.mean(e * e, axis=-1, keepdims=True))
    dy = e * (1.0 / D_MODEL)
    acc = _acc_rows([_colsum(dy * fh * gp), _colsum(dy * gate * fh)])
    return [dy, _rms_bwd(dy * gate * gp, fh, r)], [acc, jnp.full((1, 128), loss, F32)]


def _res_norm_bwd_rows(ps, es):
    dh = ps[0]
    x1, mix, dy, sc, gate, g2, gp = es
    xn, r1 = _rms(x1)
    rows = [_colsum(dh * xn * g2), _colsum(dh), _colsum(dh * (1.0 + sc) * xn)]
    dx1 = dy + _rms_bwd(dh * (1.0 + sc) * g2, xn, r1)
    mh, rm = _rms(mix)
    rows += [_colsum(dx1 * mh * gp), _colsum(dx1 * gate * mh)]
    return [dx1, _rms_bwd(dx1 * gate * gp, mh, rm)], [_acc_rows(rows)]


def _pre_norm_bwd_rows(ps, es):
    dh = ps[0]
    x, dx1, g, sc = es
    xn, r = _rms(x)
    rows = [_colsum(dh * xn * g), _colsum(dh), _colsum(dh * (1.0 + sc) * xn)]
    return [dx1 + _rms_bwd(dh * (1.0 + sc) * g, xn, r)], [_acc_rows(rows)]


def _rope_tables(pos_col, inv_freq):
    T = pos_col.shape[0]

    def body(p_ref, f_ref, c_ref, s_ref):
        ang = p_ref[...].astype(F32) * f_ref[...]
        lane = lax.broadcasted_iota(jnp.int32, ang.shape, 1)
        c_ref[...] = jnp.cos(ang)
        s_ref[...] = jnp.where(lane % HEAD_DIM < HEAD_DIM // 2, -1.0, 1.0) * jnp.sin(ang)

    return _rows("rope_tables", body, [pos_col, inv_freq],
                 [_sds((T, 128), F32), _sds((T, 128), F32)], T, tr=512)


def _swap_halves(t):
    W = t.shape[1]
    lane = lax.broadcasted_iota(jnp.int32, t.shape, 1)
    half = HEAD_DIM // 2
    return jnp.where(lane % HEAD_DIM < half, pltpu.roll(t, W - half, 1), pltpu.roll(t, half, 1))


def _widen(c, W):
    return c if W == 128 else jnp.concatenate([c] * (W // 128), axis=1)


def _rope(t, c, s):
    W = t.shape[1]
    return t * _widen(c, W) + _swap_halves(t) * _widen(s, W)


def _unrope(dy, c, s):
    W = dy.shape[1]
    return dy * _widen(c, W) + _swap_halves(dy * _widen(s, W))


def _attn_mask(n):
    qi = lax.broadcasted_iota(jnp.int32, (ATTN_BLK, 2 * ATTN_BLK), 0)
    kj = lax.broadcasted_iota(jnp.int32, (ATTN_BLK, 2 * ATTN_BLK), 1)
    rel = kj - ATTN_BLK
    return (rel <= qi) & (qi - rel < WINDOW) & ((n > 0) | (kj >= ATTN_BLK))


def _attn_load(cur, prv, cc, sc, cp, sp):
    x, xp = cur[...], prv[...]
    q = _rope(x[:, :512], cc[...], sc[...]) * (HEAD_DIM ** -0.5)
    k = jnp.concatenate([_rope(xp[:, 512:640], cp[...], sp[...]),
                         _rope(x[:, 512:640], cc[...], sc[...])], axis=0)
    v = jnp.concatenate([xp[:, 640:768], x[:, 640:768]], axis=0)
    return q, k, v


ROLLED = tuple(h for h in range(N_Q_HEADS) if h % 2 != h // (N_Q_HEADS // N_KV_HEADS))


def _pair_heads(t):
    half = lax.broadcasted_iota(jnp.int32, (ATTN_BLK, 128), 1) // HEAD_DIM
    return jnp.stack([jnp.where(half == h % 2, t[:, 128 * (h // 2):128 * (h // 2) + 128], 0.0)
                      for h in range(N_Q_HEADS)])


def _kv_heads(t):
    half = lax.broadcasted_iota(jnp.int32, t.shape, 1) // HEAD_DIM
    tr = pltpu.roll(t, HEAD_DIM, 1)
    return jnp.stack([jnp.where(half == h % 2, tr if h in ROLLED else t, 0.0)
                      for h in range(N_Q_HEADS)])


def _sink_column(snk):
    return jnp.stack([jnp.full((1, 1), snk[0, h], F32) for h in range(N_Q_HEADS)])


def _attn_probs(qh, kh, mask, sink):
    s = jnp.where(mask, _bdot(qh, kh, 2, 2), NEG_INF)
    m = jnp.maximum(jnp.max(s, axis=-1, keepdims=True), sink)
    p = jnp.exp(s - m)
    es = jnp.exp(sink - m)
    rl = 1.0 / (jnp.sum(p, axis=-1, keepdims=True) + es)
    return p, es, rl


def _attn_specs(order):
    blk = lambda w: pl.BlockSpec((ATTN_BLK, w), lambda s: (order(s), 0))
    prv = lambda w: pl.BlockSpec((ATTN_BLK, w), lambda s: (jnp.maximum(order(s) - 1, 0), 0))
    return [blk(768), prv(768), blk(128), blk(128), prv(128), prv(128),
            pl.BlockSpec(memory_space=pltpu.SMEM)]


def _attn_fwd(pa, cos, sin, sinks):
    T = pa.shape[0]
    nb = T // ATTN_BLK

    def body(cur, prv, cc, sc, cp, sp, snk, y_ref):
        n = pl.program_id(0)
        q, k, v = _attn_load(cur, prv, cc, sc, cp, sp)
        qh, kh, vh = _pair_heads(q).astype(BF16), _kv_heads(k).astype(BF16), _kv_heads(v).astype(BF16)
        p, _, rl = _attn_probs(qh, kh, _attn_mask(n), _sink_column(snk))
        o = _bdot(p.astype(BF16), vh, 2, 1) * rl
        for pair in range(N_Q_HEADS // 2):
            y_ref[:, 128 * pair:128 * pair + 128] = (o[2 * pair] + o[2 * pair + 1]).astype(BF16)

    return pl.pallas_call(
        body, name="attn_fwd", grid=(nb,), in_specs=_attn_specs(lambda s: s),
        out_specs=pl.BlockSpec((ATTN_BLK, 512), lambda n: (n, 0)),
        out_shape=_sds((T, 512), BF16), compiler_params=_params(("parallel",)))(
            pa, pa, cos, sin, cos, sin, sinks)


def _attn_bwd(pa, cos, sin, sinks, dy):
    T = pa.shape[0]
    nb = T // ATTN_BLK
    rev = lambda s: nb - 1 - s

    def body(cur, prv, cc, sc, cp, sp, snk, dy_ref, dq_ref, dkv_ref, dsink_ref, carry):
        n = rev(pl.program_id(0))

        @pl.when(pl.program_id(0) == 0)
        def _():
            dsink_ref[...] = jnp.zeros_like(dsink_ref)
            carry[...] = jnp.zeros_like(carry)

        q, k, v = _attn_load(cur, prv, cc, sc, cp, sp)
        qh, kh, vh = _pair_heads(q).astype(BF16), _kv_heads(k).astype(BF16), _kv_heads(v).astype(BF16)
        p, es, rl = _attn_probs(qh, kh, _attn_mask(n), _sink_column(snk))
        pn = p * rl
        do = _pair_heads(dy_ref[...]).astype(BF16)
        dp = _bdot(do, vh, 2, 2)
        delta = jnp.sum(pn * dp, axis=-1, keepdims=True)
        ds = (pn * (dp - delta)).astype(BF16)
        dsink = es * rl * delta
        dq = _bdot(ds, kh, 2, 1) * (HEAD_DIM ** -0.5)
        dkh = _bdot_rows(ds, qh)
        dvh = _bdot_rows(pn.astype(BF16), do)

        def fold(t):
            same = [t[h] for h in range(N_Q_HEADS) if h not in ROLLED]
            moved = [t[h] for h in ROLLED]
            return sum(same[1:], same[0]) + pltpu.roll(sum(moved[1:], moved[0]), HEAD_DIM, 1)

        dk, dv = fold(dkh), fold(dvh)
        for h in range(N_Q_HEADS):
            dsink_ref[h:h + 1, :] += -jnp.sum(dsink[h])
        for pair in range(N_Q_HEADS // 2):
            dq_ref[:, 128 * pair:128 * pair + 128] = _unrope(
                dq[2 * pair] + dq[2 * pair + 1], cc[...], sc[...]).astype(BF16)
        dkv_ref[:, 0:128] = _unrope(dk[ATTN_BLK:] + carry[:, 0:128], cc[...], sc[...]).astype(BF16)
        dkv_ref[:, 128:256] = (dv[ATTN_BLK:] + carry[:, 128:256]).astype(BF16)
        carry[:, 0:128] = dk[:ATTN_BLK]
        carry[:, 128:256] = dv[:ATTN_BLK]

    blk = lambda w: pl.BlockSpec((ATTN_BLK, w), lambda s: (rev(s), 0))
    return pl.pallas_call(
        body, name="attn_bwd", grid=(nb,), in_specs=_attn_specs(rev) + [blk(512)],
        out_specs=[blk(512), blk(256), pl.BlockSpec((8, 128), lambda s: (0, 0))],
        out_shape=[_sds((T, 512), BF16), _sds((T, 256), BF16), _sds((8, 128), F32)],
        scratch_shapes=[pltpu.VMEM((ATTN_BLK, 256), F32)],
        compiler_params=_params(("arbitrary",)))(pa, pa, cos, sin, cos, sin, sinks, dy)


CONV_COLS = 2 * MLSTM_HEADS * MLSTM_HEAD_DIM


def _conv_pre(cur_ref, halo_ref, w_ref, b_ref, i, tr):
    xx = jnp.concatenate([jnp.where(i > 0, halo_ref[...], 0.0), cur_ref[...]], axis=0)
    taps = [(pltpu.roll(xx, CONV_WIDTH - 1 - j, 0) if j < CONV_WIDTH - 1 else xx)[8:8 + tr]
            for j in range(CONV_WIDTH)]
    pre = b_ref[...]
    for j in range(CONV_WIDTH):
        pre = pre + taps[j] * w_ref[j:j + 1, :]
    return pre, taps


def _conv_specs(T, tr):
    return [pl.BlockSpec((tr, CONV_COLS), lambda i: (i, 0)),
            pl.BlockSpec((8, CONV_COLS), lambda i: (jnp.maximum(i * (tr // 8) - 1, 0), 0)),
            pl.BlockSpec((CONV_WIDTH, CONV_COLS), lambda i: (0, 0)),
            pl.BlockSpec((1, CONV_COLS), lambda i: (0, 0))]


def _conv_fwd(pm, w, b):
    T = pm.shape[0]
    tr = min(ROW_TILE, T)

    def body(cur_ref, halo_ref, w_ref, b_ref, o_ref):
        pre, _ = _conv_pre(cur_ref, halo_ref, w_ref, b_ref, pl.program_id(0), tr)
        o_ref[...] = pre * _sigmoid(pre)

    return pl.pallas_call(
        body, name="conv_fwd", grid=(T // tr,), in_specs=_conv_specs(T, tr),
        out_specs=pl.BlockSpec((tr, CONV_COLS), lambda i: (i, 0)),
        out_shape=_sds((T, CONV_COLS), F32), compiler_params=_params(("parallel",)))(pm, pm, w, b)


def _conv_bwd(pqk, w, b, dqk):
    T = pqk.shape[0]
    tr = min(ROW_TILE, T)
    nt = T // tr

    def body(cur_ref, prev_ref, next_ref, w_ref, b_ref, d_ref, dnext_ref, du_ref, acc_ref):
        i = pl.program_id(0)

        @pl.when(i == 0)
        def _():
            acc_ref[...] = jnp.zeros_like(acc_ref)

        last = i == nt - 1
        xx = jnp.concatenate([jnp.where(i > 0, prev_ref[...], 0.0), cur_ref[...],
                              jnp.where(last, 0.0, next_ref[...])], axis=0)
        taps = [(pltpu.roll(xx, CONV_WIDTH - 1 - j, 0) if j < CONV_WIDTH - 1 else xx)[8:16 + tr]
                for j in range(CONV_WIDTH)]
        pre = b_ref[...]
        for j in range(CONV_WIDTH):
            pre = pre + taps[j] * w_ref[j:j + 1, :]
        sg = _sigmoid(pre)
        dd = jnp.concatenate([d_ref[...], jnp.where(last, 0.0, dnext_ref[...])], axis=0)
        dpre = dd * (sg * (1.0 + pre * (1.0 - sg)))
        for j in range(CONV_WIDTH):
            acc_ref[j:j + 1, :] += _colsum(dpre[:tr] * taps[j][:tr])
        acc_ref[CONV_WIDTH:CONV_WIDTH + 1, :] += _colsum(dpre[:tr])
        du = dpre[:tr] * w_ref[CONV_WIDTH - 1:CONV_WIDTH, :]
        for j in range(CONV_WIDTH - 1):
            k = CONV_WIDTH - 1 - j
            du = du + pltpu.roll(dpre, tr + 8 - k, 0)[:tr] * w_ref[j:j + 1, :]
        du_ref[...] = du.astype(BF16)

    tile = pl.BlockSpec((tr, CONV_COLS), lambda i: (i, 0))
    after = pl.BlockSpec((8, CONV_COLS), lambda i: (jnp.minimum((i + 1) * (tr // 8), T // 8 - 1), 0))
    before = pl.BlockSpec((8, CONV_COLS), lambda i: (jnp.maximum(i * (tr // 8) - 1, 0), 0))
    return pl.pallas_call(
        body, name="conv_bwd", grid=(nt,),
        in_specs=[tile, before, after, pl.BlockSpec((CONV_WIDTH, CONV_COLS), lambda i: (0, 0)),
                  pl.BlockSpec((1, CONV_COLS), lambda i: (0, 0)), tile, after],
        out_specs=[tile, pl.BlockSpec((8, CONV_COLS), lambda i: (0, 0))],
        out_shape=[_sds((T, CONV_COLS), BF16), _sds((8, CONV_COLS), F32)],
        compiler_params=_params(("arbitrary",)))(pqk, pqk, pqk, w, b, dqk, dqk)


def _log_sigmoid(x):
    return jnp.minimum(x, 0.0) - jnp.log1p(jnp.exp(-jnp.abs(x)))


def _chunk_cumsum(x, axis):
    idx = lax.broadcasted_iota(jnp.int32, x.shape, axis) % MLSTM_CHUNK
    k = 1
    while k < MLSTM_CHUNK:
        x = x + jnp.where(idx >= k, pltpu.roll(x, k, axis), 0.0)
        k *= 2
    return x


def _chunk_rev_cumsum(x, axis):
    n = x.shape[axis]
    idx = lax.broadcasted_iota(jnp.int32, x.shape, axis) % MLSTM_CHUNK
    k = 1
    while k < MLSTM_CHUNK:
        x = x + jnp.where(idx < MLSTM_CHUNK - k, pltpu.roll(x, n - k, axis), 0.0)
        k *= 2
    return x


def _mlstm_gates(gc_ref, bc_ref, gr_ref, br_ref):
    gc = gc_ref[...] + bc_ref[...]
    gr = gr_ref[...] + br_ref[...]
    return gc, _chunk_cumsum(_log_sigmoid(gc), 0), gr, _chunk_cumsum(_log_sigmoid(gr), 1)


def _heads(ref, base=0):
    D = MLSTM_HEAD_DIM
    return jnp.stack([ref[:, base + D * h:base + D * h + D] for h in range(MLSTM_HEADS)])


def _mlstm_inputs(q_ref, k_ref, v_ref, gc, bc, gr, br):
    H = MLSTM_HEADS
    q, v = _heads(q_ref), _heads(v_ref)
    ks = _heads(k_ref) * (MLSTM_HEAD_DIM ** -0.5)
    return dict(
        q=q, ks=ks, qb=q.astype(BF16), kb=ks.astype(BF16), vb=v.astype(BF16),
        b_col=jnp.stack([bc[:, H + h:H + h + 1] for h in range(H)]),
        i_col=jnp.stack([gc[:, h:h + 1] for h in range(H)]),
        b_row=jnp.stack([br[H + h:H + h + 1, :] for h in range(H)]),
        i_row=jnp.stack([gr[h:h + 1, :] for h in range(H)]))


def _mlstm_head(f, c_prev, n_prev, m_prev):
    L = MLSTM_CHUNK
    q, qb = f["q"], f["qb"]
    t = lax.broadcasted_iota(jnp.int32, (1, 2 * L, 2 * L), 1)
    s = lax.broadcasted_iota(jnp.int32, (1, 2 * L, 2 * L), 2)
    mask = (t // L == s // L) & (s <= t)
    d = jnp.where(mask, f["b_col"] - f["b_row"] + f["i_row"], NEG_INF)
    row = lax.broadcasted_iota(jnp.int32, (1, 2 * L, 1), 1)
    inter = f["b_col"] + jnp.where(row < L, m_prev[0], m_prev[1])
    m_t = jnp.maximum(inter, jnp.max(d, axis=-1, keepdims=True))
    w_intra = jnp.exp(d - m_t)
    w_inter = jnp.exp(inter - m_t)
    sc = _bdot(qb, f["kb"], 2, 2) * w_intra
    qc = jnp.concatenate([_bdot(qb[:, :L], c_prev[0].astype(BF16), 2, 1),
                          _bdot(qb[:, L:], c_prev[1].astype(BF16), 2, 1)], axis=1)
    qn = jnp.concatenate([jnp.sum(q[:, :L] * n_prev[0], axis=-1, keepdims=True),
                          jnp.sum(q[:, L:] * n_prev[1], axis=-1, keepdims=True)], axis=1)
    num = _bdot(sc.astype(BF16), f["vb"], 2, 1) + w_inter * qc
    den = jnp.sum(sc, axis=-1, keepdims=True) + w_inter * qn
    return dict(f, w_intra=w_intra, w_inter=w_inter, sc=sc, qc=qc, qn=qn, num=num, den=den,
                floor=jnp.exp(-m_t))


def _mlstm_update(f, ch, c, n, m):
    L = MLSTM_CHUNK
    rows = slice(L * ch, L * ch + L)
    b_col = f["b_col"][:, rows]
    g_last = b_col[:, L - 1:L]
    a_col = g_last - b_col + f["i_col"][:, rows]
    m_new = jnp.maximum(g_last + m, jnp.max(a_col, axis=1, keepdims=True))
    decay = jnp.exp(g_last + m - m_new)
    e_a = jnp.exp(a_col - m_new)
    kw = f["ks"][:, rows] * e_a
    c_new = decay * c + _bdot_rows(kw.astype(BF16), f["vb"][:, rows])
    n_new = decay * n + jnp.sum(kw, axis=1, keepdims=True)
    return c_new, n_new, m_new, decay, e_a, kw


def _mlstm_specs(T, order):
    blk = lambda w, col: pl.BlockSpec((STEP_ROWS, w), lambda s: (order(s), col))
    return [blk(512, 0), blk(512, 1), blk(512, 0), blk(128, 0),
            pl.BlockSpec((1, 128), lambda s: (0, 0)),
            pl.BlockSpec((8, STEP_ROWS), lambda s: (0, order(s))),
            pl.BlockSpec((8, 128), lambda s: (0, 0))]


def _lanes(m):
    return jnp.broadcast_to(m, m.shape[:-1] + (128,))


def _mlstm_fwd(qk, pm, gcol, bcol, grow, brow):
    T = qk.shape[0]
    steps = T // STEP_ROWS
    H, D = MLSTM_HEADS, MLSTM_HEAD_DIM

    def body(q_ref, k_ref, v_ref, gc_ref, bc_ref, gr_ref, br_ref, h_ref, cs_ref, ns_ref, ms_ref,
             c_st, n_st, m_st):
        @pl.when(pl.program_id(0) == 0)
        def _():
            c_st[...] = jnp.zeros_like(c_st)
            n_st[...] = jnp.zeros_like(n_st)
            m_st[...] = jnp.zeros_like(m_st)

        f = _mlstm_inputs(q_ref, k_ref, v_ref, *_mlstm_gates(gc_ref, bc_ref, gr_ref, br_ref))
        c0, n0, m0 = c_st[...], n_st[...], m_st[:, :, 0:1]
        c1, n1, m1, _, _, _ = _mlstm_update(f, 0, c0, n0, m0)
        c2, n2, m2, _, _, _ = _mlstm_update(f, 1, c1, n1, m1)
        f = _mlstm_head(f, (c0, c1), (n0, n1), (m0, m1))
        h = f["num"] / jnp.maximum(jnp.abs(f["den"]), f["floor"])
        for hd in range(H):
            h_ref[:, D * hd:D * hd + D] = h[hd]
        cs_ref[0], cs_ref[1] = c0, c1
        ns_ref[0], ns_ref[1] = n0, n1
        ms_ref[0], ms_ref[1] = _lanes(m0), _lanes(m1)
        c_st[...], n_st[...], m_st[...] = c2, n2, _lanes(m2)

    vec = pl.BlockSpec((2, H, 1, 128), lambda s: (s, 0, 0, 0))
    return pl.pallas_call(
        body, name="mlstm_fwd", grid=(steps,), in_specs=_mlstm_specs(T, lambda s: s),
        out_specs=[pl.BlockSpec((STEP_ROWS, 512), lambda s: (s, 0)),
                   pl.BlockSpec((2, H, 128, 128), lambda s: (s, 0, 0, 0)), vec, vec],
        out_shape=[_sds((T, 512), F32), _sds((2 * steps, H, 128, 128), F32),
                   _sds((2 * steps, H, 1, 128), F32), _sds((2 * steps, H, 1, 128), F32)],
        scratch_shapes=[pltpu.VMEM((H, 128, 128), F32), pltpu.VMEM((H, 1, 128), F32),
                        pltpu.VMEM((H, 1, 128), F32)],
        compiler_params=_params(("arbitrary",)))(qk, qk, pm, gcol, bcol, grow, brow)


def _mlstm_bwd(qk, pm, gcol, bcol, grow, brow, cs, ns, ms, dh):
    T = qk.shape[0]
    steps = T // STEP_ROWS
    H, L, D = MLSTM_HEADS, MLSTM_CHUNK, MLSTM_HEAD_DIM
    rev = lambda s: steps - 1 - s

    def body(q_ref, k_ref, v_ref, gc_ref, bc_ref, gr_ref, br_ref, cs_ref, ns_ref, ms_ref, dh_ref,
             dqk_ref, dv_ref, dgc_ref, dgr_ref, dc_st, dn_st):
        @pl.when(pl.program_id(0) == 0)
        def _():
            dc_st[...] = jnp.zeros_like(dc_st)
            dn_st[...] = jnp.zeros_like(dn_st)

        f = _mlstm_inputs(q_ref, k_ref, v_ref, *_mlstm_gates(gc_ref, bc_ref, gr_ref, br_ref))
        c_prev = (cs_ref[0], cs_ref[1])
        n_prev = (ns_ref[0], ns_ref[1])
        m_prev = (ms_ref[0, :, :, 0:1], ms_ref[1, :, :, 0:1])
        f = _mlstm_head(f, c_prev, n_prev, m_prev)
        big = jnp.abs(f["den"]) > f["floor"]
        rden = 1.0 / jnp.where(big, jnp.abs(f["den"]), f["floor"])
        dnum = _heads(dh_ref) * rden
        hdh = jnp.sum(f["num"] * dnum, axis=-1, keepdims=True)
        dden = jnp.where(big, -hdh * rden * jnp.sign(f["den"]), 0.0)
        dnum_b = dnum.astype(BF16)
        dsc = _bdot(dnum_b, f["vb"], 2, 2) + dden
        g = dsc * f["sc"]
        dv = _bdot_rows(f["sc"].astype(BF16), dnum_b)
        dqk_ = (dsc * f["w_intra"]).astype(BF16)
        dq = _bdot(dqk_, f["kb"], 2, 1)
        dks = _bdot_rows(dqk_, f["qb"])
        wdn = f["w_inter"] * dnum
        wdn_b = wdn.astype(BF16)
        wdd = f["w_inter"] * dden
        u = jnp.sum(f["qc"] * wdn, axis=-1, keepdims=True) + wdd * f["qn"]
        dks_s, dv_s, z_s, dg_s = [None, None], [None, None], [None, None], [None, None]
        dcn, dnn = dc_st[...], dn_st[...]
        for ch in (1, 0):
            rows = slice(L * ch, L * ch + L)
            _, _, _, decay, e_a, kw = _mlstm_update(f, ch, c_prev[ch], n_prev[ch], m_prev[ch])
            dcn_b = dcn.astype(BF16)
            dkw = _bdot(f["vb"][:, rows], dcn_b, 2, 2) + dnn
            dks_s[ch] = e_a * dkw
            dv_s[ch] = _bdot(kw.astype(BF16), dcn_b, 2, 1)
            z_s[ch] = e_a * jnp.sum(f["ks"][:, rows] * dkw, axis=-1, keepdims=True)
            dg_s[ch] = jnp.sum(z_s[ch], axis=1, keepdims=True) + decay * (
                jnp.sum(c_prev[ch] * dcn, axis=(1, 2), keepdims=True)
                + jnp.sum(n_prev[ch] * dnn, axis=(1, 2), keepdims=True))
            dcn = decay * dcn + _bdot_rows(f["qb"][:, rows], wdn_b[:, rows])
            dnn = decay * dnn + jnp.sum(wdd[:, rows] * f["q"][:, rows], axis=1, keepdims=True)
        dc_st[...], dn_st[...] = dcn, dnn
        dq = dq + jnp.concatenate(
            [_bdot(wdn_b[:, :L], c_prev[0].astype(BF16), 2, 2) + wdd[:, :L] * n_prev[0],
             _bdot(wdn_b[:, L:], c_prev[1].astype(BF16), 2, 2) + wdd[:, L:] * n_prev[1]], axis=1)
        dks = (dks + jnp.concatenate(dks_s, axis=1)) * (D ** -0.5)
        dv = dv + jnp.concatenate(dv_s, axis=1)
        z = jnp.concatenate(z_s, axis=1)
        row = lax.broadcasted_iota(jnp.int32, (1, STEP_ROWS, 1), 1)
        dg_col = jnp.where(row == L - 1, dg_s[0], 0.0) + jnp.where(row == 2 * L - 1, dg_s[1], 0.0)
        db_col = jnp.sum(g, axis=-1, keepdims=True) + u - z + dg_col
        g_row = jnp.sum(g, axis=1, keepdims=True)
        lane = lax.broadcasted_iota(jnp.int32, (STEP_ROWS, 128), 1)
        sub = lax.broadcasted_iota(jnp.int32, (8, STEP_ROWS), 0)
        dgc = jnp.zeros((STEP_ROWS, 128), F32)
        dgr = jnp.zeros((8, STEP_ROWS), F32)
        for hd in range(H):
            dgc = dgc + jnp.where(lane == hd, z[hd], 0.0) + jnp.where(lane == H + hd, db_col[hd], 0.0)
            dgr = dgr + jnp.where(sub == hd, g_row[hd], 0.0) - jnp.where(sub == H + hd, g_row[hd], 0.0)
            dqk_ref[:, D * hd:D * hd + D] = dq[hd]
            dqk_ref[:, H * D + D * hd:H * D + D * hd + D] = dks[hd]
            dv_ref[:, D * hd:D * hd + D] = dv[hd].astype(BF16)
        dgc_ref[...] = dgc
        dgr_ref[...] = dgr

    return pl.pallas_call(
        body, name="mlstm_bwd", grid=(steps,),
        in_specs=_mlstm_specs(T, rev) + [
            pl.BlockSpec((2, H, 128, 128), lambda s: (rev(s), 0, 0, 0)),
            pl.BlockSpec((2, H, 1, 128), lambda s: (rev(s), 0, 0, 0)),
            pl.BlockSpec((2, H, 1, 128), lambda s: (rev(s), 0, 0, 0)),
            pl.BlockSpec((STEP_ROWS, 512), lambda s: (rev(s), 0))],
        out_specs=[pl.BlockSpec((STEP_ROWS, 1024), lambda s: (rev(s), 0)),
                   pl.BlockSpec((STEP_ROWS, 512), lambda s: (rev(s), 0)),
                   pl.BlockSpec((STEP_ROWS, 128), lambda s: (rev(s), 0)),
                   pl.BlockSpec((8, STEP_ROWS), lambda s: (0, rev(s)))],
        out_shape=[_sds((T, 1024), F32), _sds((T, 512), BF16), _sds((T, 128), F32), _sds((8, T), F32)],
        scratch_shapes=[pltpu.VMEM((H, 128, 128), F32), pltpu.VMEM((H, 1, 128), F32)],
        compiler_params=_params(("arbitrary",)))(qk, qk, pm, gcol, bcol, grow, brow, cs, ns, ms, dh)


def _rows_to_lanes(x):
    eye = (lax.broadcasted_iota(jnp.int32, (8, 128), 0)
           == lax.broadcasted_iota(jnp.int32, (8, 128), 1)).astype(BF16)
    out, rest = None, x
    for _ in range(3):
        piece = rest.astype(BF16)
        rest = rest - piece.astype(F32)
        t = _dot(piece, eye, 0, 0)
        out = t if out is None else out + t
    return out


def _gate_bwd(dgc, dgr, gcol, bcol):
    T = dgc.shape[0]
    tr = min(ROW_TILE, T)

    def body(a_ref, b_ref, g_ref, bias_ref, o_ref, acc_ref):
        i = pl.program_id(0)

        @pl.when(i == 0)
        def _():
            acc_ref[...] = jnp.zeros_like(acc_ref)

        d = a_ref[...] + _rows_to_lanes(b_ref[:, pl.ds(pl.multiple_of(i * tr, 128), tr)])
        lane = lax.broadcasted_iota(jnp.int32, d.shape, 1)
        is_f = (lane >= MLSTM_HEADS) & (lane < 2 * MLSTM_HEADS)
        dlogf = _chunk_rev_cumsum(jnp.where(is_f, d, 0.0), 0)
        out = jnp.where(is_f, dlogf * _sigmoid(-(g_ref[...] + bias_ref[...])), d)
        o_ref[...] = out.astype(BF16)
        acc_ref[0:1, :] += _colsum(out)

    return _rows("gate_bwd", body, [dgc, dgr, gcol, bcol],
                 [_sds((T, 128), BF16), _sds((8, 128), F32)], T, tr=tr)


def _head_norm(h, mu_axis=-1):
    mu = jnp.mean(h, axis=-1, keepdims=True)
    hc = h - mu
    r = lax.rsqrt(jnp.mean(hc * hc, axis=-1, keepdims=True) + NORM_EPS)
    return hc * r, r


def _mlstm_out(hm, pm, w):
    T = hm.shape[0]
    D = MLSTM_HEAD_DIM

    def body(h_ref, o_ref, w_ref, y_ref):
        for hd in range(MLSTM_HEADS):
            cols = slice(D * hd, D * hd + D)
            hn, _ = _head_norm(h_ref[:, cols])
            y_ref[:, cols] = (_sigmoid(o_ref[:, cols].astype(F32)) * hn * w_ref[:, cols]).astype(BF16)

    tr = min(ROW_TILE, T)
    return pl.pallas_call(
        body, name="mlstm_out", grid=(T // tr,),
        in_specs=[pl.BlockSpec((tr, 512), lambda i: (i, 0)), pl.BlockSpec((tr, 512), lambda i: (i, 1)),
                  pl.BlockSpec((1, 512), lambda i: (0, 0))],
        out_specs=pl.BlockSpec((tr, 512), lambda i: (i, 0)), out_shape=_sds((T, 512), BF16),
        compiler_params=_params(("parallel",)))(hm, pm, w)


def _mlstm_out_bwd_rows(ps, es):
    hm, vo, w_all = es
    D, width = MLSTM_HEAD_DIM, MLSTM_HEADS * MLSTM_HEAD_DIM
    dhs, dos, dws = [], [], []
    for hd in range(MLSTM_HEADS):
        cols = slice(D * hd, D * hd + D)
        hn, r = _head_norm(hm[:, cols])
        sg = _sigmoid(vo[:, width + D * hd:width + D * hd + D].astype(F32))
        dy, w = ps[0][:, cols], w_all[:, cols]
        dos.append(dy * hn * w * sg * (1.0 - sg))
        dyn = dy * sg
        dws.append(_colsum(dyn * hn))
        dhn = dyn * w
        dhs.append(r * (dhn - jnp.mean(dhn, axis=-1, keepdims=True)
                        - hn * jnp.mean(dhn * hn, axis=-1, keepdims=True)))
    cat = lambda parts: jnp.concatenate(parts, axis=1)
    return [cat(dhs), cat(dos)], [_acc_rows([cat(dws)])]


ADAM_TILE_ELEMS = 256 * 1024


def _adamw(name, w, g, m, v):
    R, C = w.shape
    fits = [t for t in range(8, R + 1, 8) if R % t == 0 and t * C <= ADAM_TILE_ELEMS]
    if fits or R * C <= ADAM_TILE_ELEMS:
        tr = fits[-1] if fits else R
        spec, grid = pl.BlockSpec((tr, C), lambda i: (i, 0)), (R // tr,)
    else:
        spec, grid = pl.BlockSpec((R, 128), lambda i: (0, i)), (C // 128,)
    c1 = 1.0 - ADAM_B1 ** ADAM_STEP
    c2 = 1.0 - ADAM_B2 ** ADAM_STEP

    def body(w_ref, g_ref, m_ref, v_ref, d_ref, mo_ref, vo_ref):
        g = g_ref[...]
        m = ADAM_B1 * m_ref[...] + (1.0 - ADAM_B1) * g
        v = ADAM_B2 * v_ref[...] + (1.0 - ADAM_B2) * (g * g)
        mo_ref[...] = m
        vo_ref[...] = v
        d_ref[...] = -ADAM_LR * ((m / c1) / (jnp.sqrt(v / c2) + ADAM_EPS) + ADAM_WD * w_ref[...])

    return pl.pallas_call(
        body, name=name, grid=grid, in_specs=[spec] * 4, out_specs=[spec] * 3,
        out_shape=[_sds((R, C), F32)] * 3, compiler_params=_params(("parallel",)))(w, g, m, v)


def _place():
    return lax.axis_index("x"), lax.axis_index("y"), lax.axis_index("c")


def _all_gather8(name, blk, space):
    m, n = blk.shape

    def body(x_ref, out_ref, send_sems, recv_sems, local_sem):
        x, y, c = _place()
        me, sibling = (x, y, c), (x, y, 1 - c)
        chips = [(1 - x, y), (x, 1 - y), (1 - x, 1 - y)]

        def rows(px, py, pc):
            return out_ref.at[pl.ds((4 * px + 2 * py + pc) * m, m), :]

        def copy(k, block, to, src=None):
            return pltpu.make_async_remote_copy(
                src_ref=rows(*block) if src is None else src, dst_ref=rows(*block),
                send_sem=send_sems.at[k], recv_sem=recv_sems.at[k],
                device_id=to, device_id_type=MESH)

        mine = pltpu.make_async_copy(x_ref, rows(*me), local_sem)
        mine.start()
        first = [copy(0, me, sibling, src=x_ref)]
        first += [copy(1 + j, me, (*chip, c), src=x_ref) for j, chip in enumerate(chips)]
        for cp in first:
            cp.start()
        passed = [copy(4 + j, (*chip, c), sibling) for j, chip in enumerate(chips)]
        for j, chip in enumerate(chips):
            copy(1 + j, (*chip, c), me).wait_recv()
            passed[j].start()
        copy(0, sibling, me).wait_recv()
        for j, chip in enumerate(chips):
            copy(4 + j, (*chip, 1 - c), me).wait_recv()
        for cp in first + passed:
            cp.wait_send()
        mine.wait()

    return pl.pallas_call(
        body, name=name, out_shape=_sds((8 * m, n), blk.dtype),
        in_specs=[pl.BlockSpec(memory_space=space)], out_specs=pl.BlockSpec(memory_space=space),
        scratch_shapes=[pltpu.SemaphoreType.DMA((7,)), pltpu.SemaphoreType.DMA((7,)),
                        pltpu.SemaphoreType.DMA],
        compiler_params=pltpu.CompilerParams(vmem_limit_bytes=VMEM_LIMIT))(blk)


def _hbm_specs(n):
    return [pl.BlockSpec(memory_space=pl.ANY)] * n


def _swap_halves_sibling(name, srcs):
    nw = len(srcs)

    def body(*refs):
        src_refs, dst_refs, send_sems, recv_sems = refs[:nw], refs[nw:2 * nw], refs[2 * nw], refs[2 * nw + 1]
        x, y, c = _place()
        cps = [pltpu.make_async_remote_copy(
            src_ref=src_refs[w].at[pl.ds(0, 4), 1 - c], dst_ref=dst_refs[w],
            send_sem=send_sems.at[w], recv_sem=recv_sems.at[w], device_id=(x, y, 1 - c),
            device_id_type=MESH) for w in range(nw)]
        for cp in cps:
            cp.start()
        for cp in cps:
            cp.wait()

    return pl.pallas_call(
        body, name=name, out_shape=[_sds(s.shape[:1] + s.shape[2:], s.dtype) for s in srcs],
        in_specs=_hbm_specs(nw), out_specs=_hbm_specs(nw),
        scratch_shapes=[pltpu.SemaphoreType.DMA((nw,)), pltpu.SemaphoreType.DMA((nw,))])(*srcs)


def _split_start(name, srcs, lands, copies, per_array, after):
    nw = len(srcs)

    def body(*refs):
        send_sems, recv_sems, token = refs[2 * nw + 1], refs[2 * nw + 2], refs[-1]
        for w in range(nw):
            for k, (s, d, dev) in enumerate(copies(refs[w], refs[nw + w], *_place())):
                pltpu.make_async_remote_copy(
                    src_ref=s, dst_ref=d, send_sem=send_sems.at[w * per_array + k],
                    recv_sem=recv_sems.at[w * per_array + k], device_id=dev, device_id_type=MESH).start()
        token[...] = jnp.zeros_like(token)

    hbm, sem = pl.BlockSpec(memory_space=pltpu.HBM), pl.BlockSpec(memory_space=pltpu.SEMAPHORE)
    arrays = list(srcs) + list(lands)
    out = pl.pallas_call(
        body, name=name,
        out_shape=(pltpu.SemaphoreType.DMA((nw * per_array,)), pltpu.SemaphoreType.DMA((nw * per_array,)),
                   *[pltpu.HBM(a.shape, a.dtype) for a in arrays], _sds((8, 128), F32)),
        in_specs=[hbm] * (2 * nw) + [pl.BlockSpec(memory_space=pl.ANY)],
        out_specs=(sem, sem, *[hbm] * (2 * nw), pl.BlockSpec(memory_space=pltpu.VMEM)),
        input_output_aliases={i: 2 + i for i in range(2 * nw)},
        compiler_params=pltpu.CompilerParams(has_side_effects=pltpu.SideEffectType.DATAFLOW_SIDE_EFFECTING))(
            *[pltpu.with_memory_space_constraint(a, pltpu.HBM) for a in arrays], after)
    return out[0], out[1], out[2:2 + nw], out[2 + nw:2 + 2 * nw], out[-1]


def _split_wait(name, started, after, waits, per_array):
    send_sems, recv_sems, srcs, lands, _ = started
    nw = len(srcs)

    def body(*refs):
        send_sems, recv_sems = refs[2 * nw], refs[2 * nw + 1]
        x, y, c = _place()
        for w in range(nw):
            for k, (s, d) in enumerate(waits(refs[w], refs[nw + w], x, y, c)):
                cp = pltpu.make_async_remote_copy(
                    src_ref=s, dst_ref=d, send_sem=send_sems.at[w * per_array + k],
                    recv_sem=recv_sems.at[w * per_array + k], device_id=(x, y, 1 - c),
                    device_id_type=MESH)
                cp.wait_send()
                cp.wait_recv()

    hbm, sem = pl.BlockSpec(memory_space=pltpu.HBM), pl.BlockSpec(memory_space=pltpu.SEMAPHORE)
    arrays = list(srcs) + list(lands)
    out = pl.pallas_call(
        body, name=name, out_shape=tuple(pltpu.HBM(a.shape, a.dtype) for a in arrays),
        in_specs=[hbm] * (2 * nw) + [sem, sem, pl.BlockSpec(memory_space=pl.ANY)],
        out_specs=tuple([hbm] * (2 * nw)), input_output_aliases={i: i for i in range(2 * nw)},
        compiler_params=pltpu.CompilerParams(has_side_effects=pltpu.SideEffectType.DATAFLOW_SIDE_EFFECTING))(
            *arrays, send_sems, recv_sems, after)
    return list(out[nw:])


def _other_chips(x, y):
    return [(1 - x, y), (x, 1 - y), (1 - x, 1 - y)]


def _gather_sends(src_ref, land_ref, x, y, c):
    to = land_ref.at[2 * x + y, c]
    return [(src_ref, to, (x, y, 1 - c))] + [(src_ref, to, (px, py, c)) for px, py in _other_chips(x, y)]


def _gather_lands(src_ref, land_ref, x, y, c):
    return [(src_ref, land_ref.at[2 * x + y, 1 - c])] + [
        (src_ref, land_ref.at[2 * px + py, c]) for px, py in _other_chips(x, y)]


def _gather_sends_all(src_ref, land_ref, x, y, c):
    to = land_ref.at[2 * x + y, c]
    return [(src_ref, to, (x, y, 1 - c))] + [
        (src_ref, to, (px, py, pc)) for px, py in _other_chips(x, y) for pc in (c, 1 - c)]


def _gather_lands_all(src_ref, land_ref, x, y, c):
    return [(src_ref, land_ref.at[2 * x + y, 1 - c])] + [
        (src_ref, land_ref.at[2 * px + py, pc]) for px, py in _other_chips(x, y) for pc in (c, 1 - c)]


def _scatter_sends(src_ref, land_ref, x, y, c):
    return [(src_ref.at[2 * px + py], land_ref.at[2 * x + y], (px, py, c)) for px, py in _other_chips(x, y)]


def _scatter_lands(src_ref, land_ref, x, y, c):
    return [(src_ref.at[2 * x + y], land_ref.at[2 * px + py]) for px, py in _other_chips(x, y)]


def _forward_sibling(name, lands):
    nw = len(lands)

    def body(*refs):
        land_refs, out_refs, send_sems, recv_sems = refs[:nw], refs[nw:2 * nw], refs[2 * nw], refs[2 * nw + 1]
        x, y, c = _place()
        cps = []
        for w in range(nw):
            cps += [pltpu.make_async_remote_copy(
                src_ref=land_refs[w].at[2 * px + py, c], dst_ref=out_refs[w].at[2 * px + py, c],
                send_sem=send_sems.at[w, j], recv_sem=recv_sems.at[w, j], device_id=(x, y, 1 - c),
                device_id_type=MESH) for j, (px, py) in enumerate(_other_chips(x, y))]
        for cp in cps:
            cp.start()
        for w in range(nw):
            for j, (px, py) in enumerate(_other_chips(x, y)):
                slot = out_refs[w].at[2 * px + py, 1 - c]
                pltpu.make_async_remote_copy(src_ref=slot, dst_ref=slot, send_sem=send_sems.at[w, j],
                                             recv_sem=recv_sems.at[w, j], device_id=(x, y, 1 - c),
                                             device_id_type=MESH).wait_recv()
        for cp in cps:
            cp.wait_send()

    return pl.pallas_call(
        body, name=name, out_shape=[_sds(a.shape, a.dtype) for a in lands],
        in_specs=_hbm_specs(nw), out_specs=_hbm_specs(nw), input_output_aliases={i: i for i in range(nw)},
        scratch_shapes=[pltpu.SemaphoreType.DMA((nw, 3)), pltpu.SemaphoreType.DMA((nw, 3))])(*lands)


def _share_halves(name, halves):
    nw = len(halves)

    def body(*refs):
        in_refs, out_refs, send_sems, recv_sems = refs[:nw], refs[nw:2 * nw], refs[2 * nw], refs[2 * nw + 1]
        x, y, c = _place()
        cps = [pltpu.make_async_remote_copy(
            src_ref=in_refs[w].at[c], dst_ref=out_refs[w].at[c], send_sem=send_sems.at[w],
            recv_sem=recv_sems.at[w], device_id=(x, y, 1 - c), device_id_type=MESH) for w in range(nw)]
        for cp in cps:
            cp.start()
        for w in range(nw):
            slot = out_refs[w].at[1 - c]
            pltpu.make_async_remote_copy(src_ref=slot, dst_ref=slot, send_sem=send_sems.at[w],
                                         recv_sem=recv_sems.at[w], device_id=(x, y, 1 - c),
                                         device_id_type=MESH).wait_recv()
        for cp in cps:
            cp.wait_send()

    return pl.pallas_call(
        body, name=name, out_shape=[_sds(a.shape, a.dtype) for a in halves],
        in_specs=_hbm_specs(nw), out_specs=_hbm_specs(nw), input_output_aliases={i: i for i in range(nw)},
        scratch_shapes=[pltpu.SemaphoreType.DMA((nw,)), pltpu.SemaphoreType.DMA((nw,))])(*halves)


def _place_blocks(name, blks, place):
    nw = len(blks)

    def body(p_ref, *refs):
        for b_ref, o_ref in zip(refs[:nw], refs[nw:]):
            o_ref[...] = b_ref[...]

    return pl.pallas_call(
        body, name=name,
        grid_spec=pltpu.PrefetchScalarGridSpec(
            num_scalar_prefetch=1, grid=(1,),
            in_specs=[pl.BlockSpec(b.shape, lambda i, p: (0, 0)) for b in blks],
            out_specs=[pl.BlockSpec((None, None) + b.shape, lambda i, p: (p[0], p[1], 0, 0)) for b in blks]),
        out_shape=[_sds((4, 2) + b.shape, b.dtype) for b in blks],
        compiler_params=_params(("arbitrary",)))(place, *blks)


def _pair_sum(name, fulls, gots, place):
    nw = len(fulls)

    def body(p_ref, *refs):
        s = pl.program_id(0)
        for a_ref, b_ref, o_ref, l_ref in zip(refs[:nw], refs[nw:2 * nw], refs[2 * nw:3 * nw], refs[3 * nw:]):
            o_ref[...] = (a_ref[...].astype(F32) + b_ref[...].astype(F32)).astype(o_ref.dtype)

            @pl.when(s == p_ref[0])
            def _():
                l_ref[...] = o_ref[...]

    slab = lambda a: pl.BlockSpec((None,) + a.shape[1:], lambda s, p: (s, 0, 0))
    mine = lambda a: pl.BlockSpec((None,) + a.shape[1:], lambda s, p: (p[0], 0, 0))
    out = pl.pallas_call(
        body, name=name,
        grid_spec=pltpu.PrefetchScalarGridSpec(
            num_scalar_prefetch=1, grid=(4,),
            in_specs=[pl.BlockSpec((None, None) + a.shape[2:], lambda s, p: (s, p[1], 0, 0)) for a in fulls]
            + [slab(b) for b in gots],
            out_specs=[slab(b) for b in gots] + [mine(b) for b in gots]),
        out_shape=[_sds(b.shape, BF16) for b in gots] * 2,
        compiler_params=_params(("arbitrary",)))(place, *fulls, *gots)
    return out[:nw], out[nw:]


def _sum4(name, arrs, place):
    nw = len(arrs)

    def body(p_ref, *refs):
        for a_ref, o_ref in zip(refs[:nw], refs[nw:]):
            acc = a_ref[0].astype(F32)
            for s in range(1, 4):
                acc = acc + a_ref[s].astype(F32)
            o_ref[...] = acc

    return pl.pallas_call(
        body, name=name,
        grid_spec=pltpu.PrefetchScalarGridSpec(
            num_scalar_prefetch=1, grid=(1,),
            in_specs=[pl.BlockSpec(a.shape, lambda i, p: (0, 0, 0)) for a in arrs],
            out_specs=[pl.BlockSpec((None,) + a.shape[1:], lambda i, p: (p[1], 0, 0)) for a in arrs]),
        out_shape=[_sds((2,) + a.shape[1:], F32) for a in arrs],
        compiler_params=_params(("arbitrary",)))(place, *arrs)


def _small_update(gathered, w, m, v):
    n = w.shape[1]
    tn = 2048
    c1 = 1.0 - ADAM_B1 ** ADAM_STEP
    c2 = 1.0 - ADAM_B2 ** ADAM_STEP

    def body(g_ref, w_ref, m_ref, v_ref, go_ref, d_ref, mo_ref, vo_ref):
        g = g_ref[0:1, :]
        for d in range(1, 8):
            g = g + g_ref[d:d + 1, :]
        go_ref[...] = g
        m = ADAM_B1 * m_ref[...] + (1.0 - ADAM_B1) * g
        v = ADAM_B2 * v_ref[...] + (1.0 - ADAM_B2) * (g * g)
        mo_ref[...] = m
        vo_ref[...] = v
        d_ref[...] = -ADAM_LR * ((m / c1) / (jnp.sqrt(v / c2) + ADAM_EPS) + ADAM_WD * w_ref[...])

    row = pl.BlockSpec((1, tn), lambda i: (0, i))
    return pl.pallas_call(
        body, name="small_update", grid=(n // tn,),
        in_specs=[pl.BlockSpec((8, tn), lambda i: (0, i)), row, row, row], out_specs=[row] * 4,
        out_shape=[_sds((1, n), F32)] * 4, compiler_params=_params(("parallel",)))(gathered, w, m, v)


def _swiglu(ps, es):
    g, u = ps
    return g * _sigmoid(g) * u, g, u


def _swiglu_bwd(ps, es):
    g, u = es[0].astype(F32), es[1].astype(F32)
    sg = _sigmoid(g)
    return ps[0] * u * (sg * (1.0 + g * (1.0 - sg))), ps[0] * (g * sg)


def _merge(ps, es):
    ga, gm = [e.astype(F32) for e in es]
    return _sigmoid(ga) * ps[0] + _sigmoid(gm) * ps[1], ps[0], ps[1]


def _merge_bwd(ps, es):
    a, b, ga, gm = [e.astype(F32) for e in es]
    sa, sm = _sigmoid(ga), _sigmoid(gm)
    dm = ps[0]
    return dm * sa, dm * sm, dm * a * (sa * (1.0 - sa)), dm * b * (sm * (1.0 - sm))


W_IN_PIECES = (("q", 512), ("kv", 256), ("mqk", 1024), ("mv", 512), ("mo", 512), ("if", 8),
               ("ga", 1024), ("gm", 1024))


def _local_step(x, tgt, pos_col, mod, sp, in_weights, late_weights, ffn_grads, mixer_grads):
    sh_m, sc_m, gate_m, sh_f, sc_f, gate_f = mod
    inv = ROPE_THETA ** (-2.0 * jnp.arange(HEAD_DIM // 2, dtype=F32) / HEAD_DIM)
    cos, sin = _rope_tables(pos_col, jnp.tile(inv, 4).reshape(1, 128))
    W = dict(in_weights(cos))
    h, pa, pqk, pvo, pif, pg = _proj_in(x, sp["g_pre_mix"], sc_m, sh_m, [
        (W["q+kv"], F32, 256), (W["mqk"], F32, 512), (W["mv+mo"], BF16, 512), (W["if"], F32, 128),
        (W["ga+gm"], BF16, 512)])
    ya = _attn_fwd(pa, cos, sin, sp["sinks"])
    qk = _conv_fwd(pqk, sp["conv_w"], sp["conv_b"])
    bcol = jnp.pad(sp["b_if"], ((0, 0), (0, 120)))
    brow = jnp.broadcast_to(sp["b_if"].reshape(8, 1), (8, 128))
    grow = pif[:, :8].T
    hm, cs, ns, ms = _mlstm_fwd(qk, pvo, pif, bcol, grow, brow)
    ym = _mlstm_out(hm, pvo, sp["norm_w"])
    W.update(late_weights(ym))
    w_fg, w_fu, w_fd = W["fg"], W["fu"], W["fd"]
    merged, br_a, br_m = _mm("branches", [[(ya, W["ba"])], [(ym, W["bm"])]],
                             [(pg, 0), (pg, 1)], _merge, [BF16, BF16, BF16], cn=512, nt=True)
    wide, narrow = (D_MODEL, F32), (D_MODEL, BF16)
    mix, x1, h2 = _mm_rows("mix_out", [[(merged, W["out"])]],
                           [x, gate_m, sp["g_post_mix"], sp["g_pre_ffn"], sc_f, sh_f],
                           _res_norm_rows, [wide, wide, narrow], [], cn=512)
    act, gt, up = _mm("ffn_in", [[(h2, w_fg)], [(h2, w_fu)]], [], _swiglu, [BF16] * 3,
                      cn=256, nt=True)
    dy, dff, acc_l, loss = _mm_rows("ffn_down", [[(act, w_fd)]], [x1, tgt, gate_f, sp["g_post_ffn"]],
                                    _final_loss_rows, [wide, narrow], [(8, D_MODEL), (1, 128)], cn=512)

    G = {}
    dgt, dup = _mm("ffn_down_bwd", [[(dff, w_fd)]], [gt, up], _swiglu_bwd, [BF16, BF16],
                   cn=256, nt=True)
    g_fd, = _mm_tn_group("dw_ffn_down", [act], dff, BF16)
    g_fg, = _mm_tn_group("dw_ffn_gate", [dgt], h2, BF16)
    g_fu, = _mm_tn_group("dw_ffn_up", [dup], h2, BF16)
    tie = ffn_grads(g_fg, g_fu, g_fd)
    dx1, dmix, acc_r = _mm_rows(
        "ffn_in_bwd", [[(dgt, w_fg), (dup, w_fu)]],
        [x1, mix, dy, sc_f + tie, gate_m, sp["g_pre_ffn"], sp["g_post_mix"]],
        _res_norm_bwd_rows, [wide, narrow], [(8, D_MODEL)], cn=512, tm=256)
    d_a, d_m, dga, dgm = _mm("mix_out_bwd", [[(dmix, W["out"])]],
                             [br_a, br_m, (pg, 0), (pg, 1)], _merge_bwd,
                             [BF16] * 4, cn=512, nt=True)
    G["out"], = _mm_tn_group("dw_out", [merged], dmix, BF16)
    dya, = _mm("branch_attn_bwd", [[(d_a, W["ba"])]], [], _first, [F32], cn=512)
    heads = MLSTM_HEADS * MLSTM_HEAD_DIM
    dhm, do_m, acc_n = _mm_rows("branch_mlstm_bwd", [[(d_m, W["bm"])]], [hm, pvo, sp["norm_w"]],
                                _mlstm_out_bwd_rows, [(heads, F32), (heads, BF16)], [(8, heads)], cn=512)
    G["ba"], = _mm_tn_group("dw_branch_attn", [d_a], ya, BF16)
    G["bm"], = _mm_tn_group("dw_branch_mlstm", [d_m], ym, BF16)
    dqk, dv_m, dgc, dgr = _mlstm_bwd(qk, pvo, pif, bcol, grow, brow, cs, ns, ms, dhm)
    dif, acc_g = _gate_bwd(dgc, dgr, pif, bcol)
    du, acc_c = _conv_bwd(pqk, sp["conv_w"], sp["conv_b"], dqk)
    dq_a, dkv, dsink = _attn_bwd(pa, cos, sin, sp["sinks"], dya)
    dproj = {"q": dq_a, "kv": dkv, "mqk": du, "mv": dv_m, "mo": do_m, "if": dif, "ga": dga, "gm": dgm}
    names = [k for k, _ in W_IN_PIECES]
    for part in (names[:4], names[4:]):
        G.update(zip(part, _mm_tn_group("dw_in_from_" + part[0], [dproj[k] for k in part], h, BF16)))
    w_tied = dict(W, **{"if": W["if"] + mixer_grads(G).astype(BF16)})
    dx, acc_p = _mm_rows("proj_bwd", [[(dproj[k], w_tied[k]) for k, _ in W_IN_PIECES]],
                         [x, dx1, sp["g_pre_mix"], sc_m], _pre_norm_bwd_rows, [wide], [(8, D_MODEL)], cn=512)

    small = {
        "mod": jnp.concatenate([acc_p[1], acc_p[0], acc_r[3], acc_r[1], acc_r[0], acc_l[0]]),
        "g_pre_mix": acc_p[2], "g_post_mix": acc_r[4], "b_if": acc_g[0, :8],
        "conv_w": acc_c[:CONV_WIDTH].reshape(-1), "conv_b": acc_c[CONV_WIDTH],
        "sinks": dsink[:, 0], "norm_w": acc_n[0], "g_pre_ffn": acc_r[2], "g_post_ffn": acc_l[1]}
    return loss, dx, small


IN_WIDTH = sum(n for _, n in W_IN_PIECES)
IN_SHARD = IN_WIDTH // 4
IN_SHARD_PAD = -(-IN_SHARD // 32) * 32


def _split_w_in(w_in_t):
    out, off, start = {}, 0, {}
    for k, n in W_IN_PIECES:
        out[k], start[k] = w_in_t[off:off + n], off
        off += n
    out["if"] = jnp.pad(out["if"], ((0, 120), (0, 0)))
    for name, first, last in (("q+kv", "q", "kv"), ("mv+mo", "mv", "mo"), ("ga+gm", "ga", "gm")):
        out[name] = w_in_t[start[first]:start[last] + out[last].shape[0]]
    return out


def _halves(a):
    return a.reshape(4, 2, a.shape[0] // 8, a.shape[1])


SMALL = (("b_ada", 6144), ("g_pre_mix", 1024), ("g_post_mix", 1024), ("b_if", 128), ("conv_w", 4096),
         ("conv_b", 1024), ("sinks", 128), ("norm_w", 512), ("g_pre_ffn", 1024), ("g_post_ffn", 1024))
SMALL_LEN = 8 * 2048


def _pack_small(vals):
    parts = []
    for k, n in SMALL:
        v = vals[k].reshape(-1)
        parts.append(jnp.pad(v, (0, n - v.shape[0])))
    flat = jnp.concatenate(parts)
    return jnp.pad(flat, (0, SMALL_LEN - flat.shape[0]))


def _unpack_small(flat, shapes):
    out, off = {}, 0
    for k, n in SMALL:
        size = 1
        for d in shapes[k]:
            size *= d
        out[k] = flat[off:off + size].reshape(shapes[k])
        off += n
    return out


def kernel(x, c, positions, w_ada, b_ada, g_pre_mix, g_post_mix, w_in, b_if, conv_w, conv_b, attn_sinks, mlstm_norm_w, w_branch_attn, w_branch_mlstm, w_out, g_pre_ffn, g_post_ffn, w_ffn_gate, w_ffn_up, w_ffn_down, loss_target, m_w_ada, m_b_ada, m_g_pre_mix, m_g_post_mix, m_w_in, m_b_if, m_conv_w, m_conv_b, m_attn_sinks, m_mlstm_norm_w, m_w_branch_attn, m_w_branch_mlstm, m_w_out, m_g_pre_ffn, m_g_post_ffn, m_w_ffn_gate, m_w_ffn_up, m_w_ffn_down, v_w_ada, v_b_ada, v_g_pre_mix, v_g_post_mix, v_w_in, v_b_if, v_conv_w, v_conv_b, v_attn_sinks, v_mlstm_norm_w, v_w_branch_attn, v_w_branch_mlstm, v_w_out, v_g_pre_ffn, v_g_post_ffn, v_w_ffn_gate, v_w_ffn_up, v_w_ffn_down):
    xi, yi, ci = _place()
    chip = 2 * xi + yi
    dev = 2 * chip + ci
    T = x.shape[1]
    ada_cols = w_ada.shape[2]

    place = jnp.stack([chip, ci]).astype(jnp.int32)

    def my_half(a):
        n = a.shape[0] // 2
        return lax.dynamic_slice_in_dim(a, ci * n, n, axis=0).astype(BF16)

    blk = jnp.concatenate([c.reshape(-1), conv_w.reshape(-1)]).reshape(8, 256)
    got = _all_gather8("gather_cond", blk, pltpu.VMEM).reshape(8, 2048)
    c_all = got[:, :D_MODEL].astype(BF16)
    conv_full = got[::2, D_MODEL:].reshape(4, CONV_WIDTH, -1).transpose(1, 0, 2).reshape(CONV_WIDTH, -1)

    b_sh = lax.dynamic_slice_in_dim(b_ada, chip * ada_cols, ada_cols, axis=1)
    mod_part, = _mm("ada_mod", [[(c_all, w_ada[0].astype(BF16))]], [b_sh],
                    lambda ps, es: (ps[0] + es[0],), [F32], cn=512, tm=8)
    mod_all = _all_gather8("gather_mod", mod_part, pltpu.VMEM).reshape(4, 2, 8, ada_cols)[:, 0]
    mod = lax.dynamic_index_in_dim(mod_all, dev, axis=1, keepdims=False).reshape(6, 1, D_MODEL)

    def gather_start(name, blks, after, sends, copies):
        return _split_start(name + "_start", blks, _place_blocks(name + "_place", blks, place),
                            sends, copies, after)

    w_in_t = jnp.pad(w_in[0].T, ((0, IN_SHARD_PAD - IN_SHARD), (0, 0)))
    in_started = gather_start("in_gather", [my_half(w_in_t)], mod, _gather_sends, 4)
    late_keys = ("fg", "fu", "fd", "out", "ba", "bm")
    late_started = gather_start(
        "late_gather",
        [my_half(w_ffn_gate[0].T), my_half(w_ffn_up[0].T), my_half(w_ffn_down[0]), my_half(w_out[0]),
         my_half(w_branch_attn[0].T), my_half(w_branch_mlstm[0].T)], in_started[4], _gather_sends_all, 7)
    mod = mod + (in_started[4][0, 0] + late_started[4][0, 0])

    def in_weights(after):
        g_in, = _forward_sibling("in_gather_forward",
                                 _split_wait("in_gather_wait", in_started, after, _gather_lands, 4))
        return _split_w_in(g_in.reshape(4, IN_SHARD_PAD, D_MODEL)[:, :IN_SHARD].reshape(IN_WIDTH, D_MODEL))

    def late_weights(after):
        lands = _split_wait("late_gather_wait", late_started, after, _gather_lands_all, 7)
        return {k: a.reshape(-1, a.shape[-1]) for k, a in zip(late_keys, lands)}

    sent = {}

    def scatter_start(name, groups):
        pairs, lands = _pair_sum(name + "_pair_sum", groups, _swap_halves_sibling(name + "_pair", groups), place)
        sent[name] = _split_start(name + "_start", pairs, lands, _scatter_sends, 3, pairs[0])
        return sent[name][4][0, 0]

    def ffn_grads(g_fg, g_fu, g_fd):
        return scatter_start("rs_ffn", [_halves(g_fg), _halves(g_fu), _halves(g_fd)])

    def mixer_grads(G):
        g_in_t = jnp.concatenate([G[k][:n] for k, n in W_IN_PIECES]).reshape(4, IN_SHARD, D_MODEL)
        g_in_t = jnp.pad(g_in_t, ((0, 0), (0, IN_SHARD_PAD - IN_SHARD), (0, 0)))
        return scatter_start("rs_mix", [g_in_t.reshape(4, 2, IN_SHARD_PAD // 2, D_MODEL), _halves(G["out"]),
                                        _halves(G["ba"]), _halves(G["bm"])])

    sp = {"g_pre_mix": g_pre_mix, "g_post_mix": g_post_mix, "b_if": b_if, "conv_w": conv_full,
          "conv_b": conv_b, "sinks": attn_sinks, "norm_w": mlstm_norm_w, "g_pre_ffn": g_pre_ffn,
          "g_post_ffn": g_post_ffn}
    loss, dx, small = _local_step(x[0], loss_target[0], positions.reshape(T, 1), [mod[i] for i in range(6)],
                                  sp, in_weights, late_weights, ffn_grads, mixer_grads)

    reds = (_sum4("rs_ffn_chip_sum", _split_wait("rs_ffn_wait", sent["rs_ffn"], dx, _scatter_lands, 3), place)
            + _sum4("rs_mix_chip_sum", _split_wait("rs_mix_wait", sent["rs_mix"], dx, _scatter_lands, 3), place))
    gsh = {k: s.reshape(-1, s.shape[-1])
           for k, s in zip(("fg", "fu", "fd", "w_in", "out", "ba", "bm"), _share_halves("rs_share", reds))}
    gsh["w_in"] = gsh["w_in"][:IN_SHARD]

    small["b_ada"] = small.pop("mod")
    vec = _pack_small(small).reshape(8, 2048)
    g_all = _all_gather8("gather_small", vec, pltpu.VMEM).reshape(8, SMALL_LEN)
    dmod_sh = lax.dynamic_slice_in_dim(g_all[:, :6 * D_MODEL], chip * ada_cols, ada_cols, axis=1)
    g_w_ada, = _mm_tn_group("dw_ada", [c_all], dmod_sh.astype(BF16), F32)

    smalls = {"b_ada": (b_ada, m_b_ada, v_b_ada), "g_pre_mix": (g_pre_mix, m_g_pre_mix, v_g_pre_mix),
              "g_post_mix": (g_post_mix, m_g_post_mix, v_g_post_mix), "b_if": (b_if, m_b_if, v_b_if),
              "conv_w": None, "conv_b": (conv_b, m_conv_b, v_conv_b),
              "sinks": (attn_sinks, m_attn_sinks, v_attn_sinks),
              "norm_w": (mlstm_norm_w, m_mlstm_norm_w, v_mlstm_norm_w),
              "g_pre_ffn": (g_pre_ffn, m_g_pre_ffn, v_g_pre_ffn),
              "g_post_ffn": (g_post_ffn, m_g_post_ffn, v_g_post_ffn)}
    shapes = {k: (t[0].shape if t is not None else (1, CONV_WIDTH, D_MODEL)) for k, t in smalls.items()}
    zeros = jnp.zeros((CONV_WIDTH * D_MODEL,), F32)
    packs = [_pack_small({k: (t[i] if t is not None else zeros) for k, t in smalls.items()}).reshape(1, -1)
             for i in range(3)]
    s_out = [_unpack_small(o[0], shapes) for o in _small_update(g_all, *packs)]
    g_conv = lax.dynamic_slice_in_dim(s_out[0]["conv_w"], chip * conv_w.shape[2], conv_w.shape[2], axis=2)

    res = {}
    for k, t in smalls.items():
        if t is not None:
            res[k] = tuple(o[k] for o in s_out)
    res["conv_w"] = (g_conv, *[o[None] for o in _adamw("adam_conv_w", conv_w[0], g_conv[0], m_conv_w[0], v_conv_w[0])])
    res["w_ada"] = (g_w_ada[None], *[o[None] for o in _adamw("adam_w_ada", w_ada[0], g_w_ada, m_w_ada[0], v_w_ada[0])])
    bigs = {"w_in": (w_in, m_w_in, v_w_in), "ba": (w_branch_attn, m_w_branch_attn, v_w_branch_attn),
            "bm": (w_branch_mlstm, m_w_branch_mlstm, v_w_branch_mlstm), "out": (w_out, m_w_out, v_w_out),
            "fg": (w_ffn_gate, m_w_ffn_gate, v_w_ffn_gate), "fu": (w_ffn_up, m_w_ffn_up, v_w_ffn_up),
            "fd": (w_ffn_down, m_w_ffn_down, v_w_ffn_down)}
    for k, (w, m, v) in bigs.items():
        if k in ("w_in", "fg", "fu"):
            res[k] = tuple(o.T[None] for o in (gsh[k], *_adamw("adam_" + k, w[0].T, gsh[k], m[0].T, v[0].T)))
        else:
            g = gsh[k].T if k in ("ba", "bm") else gsh[k]
            res[k] = (g[None], *[o[None] for o in _adamw("adam_" + k, w[0], g, m[0], v[0])])

    order = ("w_ada", "b_ada", "g_pre_mix", "g_post_mix", "w_in", "b_if", "conv_w", "conv_b", "sinks",
             "norm_w", "ba", "bm", "out", "g_pre_ffn", "g_post_ffn", "fg", "fu", "fd")
    total = lax.psum(loss[0, 0], ("x", "y", "c"))
    return (total, dx[None], *[res[k][0] for k in order], *[res[k][1] for k in order],
            *[res[k][2] for k in order], *[res[k][3] for k in order])
```

```python
import functools

import jax
import jax.numpy as jnp
from jax import lax
from jax.experimental import pallas as pl
from jax.experimental.pallas import tpu as pltpu

F32, BF16 = jnp.float32, jnp.bfloat16
MESH = pl.DeviceIdType.MESH

D_MODEL = 1024
N_Q_HEADS, N_KV_HEADS, HEAD_DIM, WINDOW = 8, 2, 64, 128
ROPE_THETA = 10000.0
MLSTM_HEADS, MLSTM_HEAD_DIM, MLSTM_CHUNK, CONV_WIDTH = 4, 128, 64, 4
D_FF = 2816
NORM_EPS = 1e-6
ADAM_LR, ADAM_B1, ADAM_B2, ADAM_EPS, ADAM_WD, ADAM_STEP = 0.001, 0.9, 0.999, 1e-08, 0.01, 10

VMEM_LIMIT = 56 * 1024 * 1024
ROW_TILE = 256
MM_TM = 512
MM_TT = 1024
ATTN_BLK = WINDOW
STEP_ROWS = 2 * MLSTM_CHUNK
NEG_INF = float("-inf")


def _params(sem):
    return pltpu.CompilerParams(dimension_semantics=sem, vmem_limit_bytes=VMEM_LIMIT)


def _sds(shape, dtype):
    return jax.ShapeDtypeStruct(shape, dtype)


def _sigmoid(x):
    return 1.0 / (1.0 + jnp.exp(-x))


def _dot(a, b, ca, cb):
    return lax.dot_general(a, b, (((ca,), (cb,)), ((), ())), preferred_element_type=F32)


def _bdot(a, b, ca, cb):
    return lax.dot_general(a, b, (((ca,), (cb,)), ((0,), (0,))), preferred_element_type=F32)


def _bdot_rows(a, b):
    return jnp.stack([_dot(a[h], b[h], 0, 0) for h in range(a.shape[0])])


def _mm(name, prods, extras, epi, out_dtypes, cn, nt=False, tm=MM_TM):
    flat = [ab for p in prods for ab in p]
    counts = [len(p) for p in prods]
    M = flat[0][0].shape[0]
    N = flat[0][1].shape[0 if nt else 1]
    tm = min(tm, M)
    n_in = 2 * len(flat) + len(extras)

    def body(*refs):
        ins, outs = refs[:n_in], refs[n_in:]
        for j in range(N // cn):
            cols = slice(j * cn, (j + 1) * cn)
            k, ps = 0, []
            for cnt in counts:
                acc = None
                for _ in range(cnt):
                    b = ins[k + 1][cols, :] if nt else ins[k + 1][:, cols]
                    d = _dot(ins[k][...], b, 1, 1 if nt else 0)
                    acc = d if acc is None else acc + d
                    k += 2
                ps.append(acc)
            res = epi(ps, [r[:, cols] for r in ins[k:]])
            for o, r in zip(outs, res):
                o[:, cols] = r.astype(o.dtype)

    in_specs, args = [], []
    for a, b in flat:
        in_specs.append(pl.BlockSpec((tm, a.shape[1]), lambda i: (i, 0)))
        in_specs.append(pl.BlockSpec(b.shape, lambda i: (0, 0), pipeline_mode=pl.Buffered(1)))
        args += [a, b]
    for e in extras:
        e, off = e if isinstance(e, tuple) else (e, 0)
        rows = 1 if e.shape[0] == 1 else tm
        in_specs.append(pl.BlockSpec((rows, N), lambda i, off=off, rows=rows: (0 if rows == 1 else i, off)))
        args.append(e)
    return pl.pallas_call(
        body, name=name, grid=(M // tm,), in_specs=in_specs,
        out_specs=[pl.BlockSpec((tm, N), lambda i: (i, 0)) for _ in out_dtypes],
        out_shape=[_sds((M, N), dt) for dt in out_dtypes],
        compiler_params=_params(("parallel",)))(*args)


def _mm_rows(name, prods, extras, epi, outs, accs, cn, nt=False, tm=MM_TM):
    flat = [ab for p in prods for ab in p]
    counts = [len(p) for p in prods]
    M = flat[0][0].shape[0]
    N = flat[0][1].shape[0 if nt else 1]
    tm = min(tm, M)
    n_mm, n_in, n_out = 2 * len(flat), 2 * len(flat) + len(extras), len(outs)

    def body(*refs):
        ins, out_refs, acc_refs = refs[:n_in], refs[n_in:n_in + n_out], refs[n_in + n_out:]

        @pl.when(pl.program_id(0) == 0)
        def _():
            for a in acc_refs:
                a[...] = jnp.zeros_like(a)

        chunks = [[] for _ in counts]
        for j in range(N // cn):
            cols = slice(j * cn, (j + 1) * cn)
            k = 0
            for p, cnt in enumerate(counts):
                acc = None
                for _ in range(cnt):
                    b = ins[k + 1][cols, :] if nt else ins[k + 1][:, cols]
                    d = _dot(ins[k][...], b, 1, 1 if nt else 0)
                    acc = d if acc is None else acc + d
                    k += 2
                chunks[p].append(acc)
        ps = [c[0] if len(c) == 1 else jnp.concatenate(c, axis=1) for c in chunks]
        res, incs = epi(ps, [r[...] for r in ins[n_mm:]])
        for o, r in zip(out_refs, res):
            o[...] = r.astype(o.dtype)
        for a, inc in zip(acc_refs, incs):
            a[...] += inc

    in_specs, args = [], []
    for a, b in flat:
        in_specs.append(pl.BlockSpec((tm, a.shape[1]), lambda i: (i, 0)))
        in_specs.append(pl.BlockSpec(b.shape, lambda i: (0, 0), pipeline_mode=pl.Buffered(1)))
        args += [a, b]
    for e in extras:
        rows = 1 if e.shape[0] == 1 else tm
        in_specs.append(pl.BlockSpec((rows, e.shape[1]), lambda i, rows=rows: (0 if rows == 1 else i, 0)))
        args.append(e)
    return pl.pallas_call(
        body, name=name, grid=(M // tm,), in_specs=in_specs,
        out_specs=[pl.BlockSpec((tm, w), lambda i: (i, 0)) for w, _ in outs]
        + [pl.BlockSpec(s, lambda i: (0, 0)) for s in accs],
        out_shape=[_sds((M, w), dt) for w, dt in outs] + [_sds(s, F32) for s in accs],
        compiler_params=_params(("arbitrary",)))(*args)


def _mm_tn_group(name, pieces, b, out_dtype, tt=MM_TT):
    T, N = b.shape
    tt = min(tt, T)
    steps, n = T // tt, len(pieces)

    def body(*refs):
        a_refs, b_ref, out_refs, accs = refs[:n], refs[n], refs[n + 1:2 * n + 1], refs[2 * n + 1:]
        t = pl.program_id(0)

        @pl.when(t == 0)
        def _():
            for acc in accs:
                acc[...] = jnp.zeros_like(acc)

        for a_ref, acc in zip(a_refs, accs):
            acc[...] += _dot(a_ref[...], b_ref[...], 0, 0)

        @pl.when(t == steps - 1)
        def _():
            for o_ref, acc in zip(out_refs, accs):
                o_ref[...] = acc[...].astype(o_ref.dtype)

    return pl.pallas_call(
        body, name=name, grid=(steps,),
        in_specs=[pl.BlockSpec((tt, a.shape[1]), lambda t: (t, 0)) for a in pieces]
        + [pl.BlockSpec((tt, N), lambda t: (t, 0))],
        out_specs=[pl.BlockSpec((a.shape[1], N), lambda t: (0, 0)) for a in pieces],
        out_shape=[_sds((a.shape[1], N), out_dtype) for a in pieces],
        scratch_shapes=[pltpu.VMEM((a.shape[1], N), F32) for a in pieces],
        compiler_params=_params(("arbitrary",)))(*pieces, b)


def _first(ps, es):
    return (ps[0],)


def _rows(name, body, ins, out_shapes, T, tr=ROW_TILE):
    tr = min(tr, T)

    def spec(shape):
        if shape[0] == T:
            return pl.BlockSpec((tr,) + tuple(shape[1:]), lambda i: (i,) + (0,) * (len(shape) - 1))
        return pl.BlockSpec(tuple(shape), lambda i: (0,) * len(shape))

    return pl.pallas_call(
        body, name=name, grid=(T // tr,),
        in_specs=[spec(a.shape) for a in ins], out_specs=[spec(s.shape) for s in out_shapes],
        out_shape=out_shapes, compiler_params=_params(("arbitrary",)))(*ins)


def _rms(x):
    r = lax.rsqrt(jnp.mean(x * x, axis=-1, keepdims=True) + NORM_EPS)
    return x * r, r


def _rms_bwd(dxn, xn, r):
    return r * (dxn - xn * jnp.mean(dxn * xn, axis=-1, keepdims=True))


def _colsum(v):
    return jnp.sum(v, axis=0, keepdims=True)


def _proj_in(x, g, sc, sh, groups):
    T = x.shape[0]
    tm = min(MM_TM, T)
    ng = len(groups)

    def body(x_ref, g_ref, sc_ref, sh_ref, *rest):
        w_refs, h_ref, out_refs = rest[:ng], rest[ng], rest[ng + 1:]
        xn, _ = _rms(x_ref[...])
        h = (xn * g_ref[...] * (1.0 + sc_ref[...]) + sh_ref[...]).astype(BF16)
        h_ref[...] = h
        for w_ref, o_ref, (w, _, cn) in zip(w_refs, out_refs, groups):
            for j in range(w.shape[0] // cn):
                cols = slice(j * cn, (j + 1) * cn)
                o_ref[:, cols] = _dot(h, w_ref[cols, :], 1, 1).astype(o_ref.dtype)

    row = pl.BlockSpec((1, D_MODEL), lambda i: (0, 0))
    tile = lambda w: pl.BlockSpec((tm, w), lambda i: (i, 0))
    return pl.pallas_call(
        body, name="proj_in", grid=(T // tm,),
        in_specs=[tile(D_MODEL), row, row, row] + [
            pl.BlockSpec(w.shape, lambda i: (0, 0), pipeline_mode=pl.Buffered(1)) for w, _, _ in groups],
        out_specs=[tile(D_MODEL)] + [tile(w.shape[0]) for w, _, _ in groups],
        out_shape=[_sds((T, D_MODEL), BF16)] + [_sds((T, w.shape[0]), dt) for w, dt, _ in groups],
        compiler_params=_params(("parallel",)))(x, g, sc, sh, *[w for w, _, _ in groups])


def _acc_rows(rows):
    w = rows[0].shape[1]
    return jnp.concatenate(rows + [jnp.zeros((8 - len(rows), w), F32)], axis=0)


def _res_norm_rows(ps, es):
    mix = ps[0]
    x, gate, gp, g2, sc, sh = es
    mh, _ = _rms(mix)
    x1 = x + gate * (mh * gp)
    xn, _ = _rms(x1)
    return [mix, x1, xn * g2 * (1.0 + sc) + sh], []


def _final_loss_rows(ps, es):
    x1, tgt, gate, gp = es
    fh, r = _rms(ps[0])
    e = x1 + gate * (fh * gp) - tgt
    loss = 0.5 * jnp.sum(jnp.mean(e * e, axis=-1, keepdims=True))
    dy = e * (1.0 / D_MODEL)
    acc = _acc_rows([_colsum(dy * fh * gp), _colsum(dy * gate * fh)])
    return [dy, _rms_bwd(dy * gate * gp, fh, r)], [acc, jnp.full((1, 128), loss, F32)]


def _res_norm_bwd_rows(ps, es):
    dh = ps[0]
    x1, mix, dy, sc, gate, g2, gp = es
    xn, r1 = _rms(x1)
    rows = [_colsum(dh * xn * g2), _colsum(dh), _colsum(dh * (1.0 + sc) * xn)]
    dx1 = dy + _rms_bwd(dh * (1.0 + sc) * g2, xn, r1)
    mh, rm = _rms(mix)
    rows += [_colsum(dx1 * mh * gp), _colsum(dx1 * gate * mh)]
    return [dx1, _rms_bwd(dx1 * gate * gp, mh, rm)], [_acc_rows(rows)]


def _pre_norm_bwd_rows(ps, es):
    dh = ps[0]
    x, dx1, g, sc = es
    xn, r = _rms(x)
    rows = [_colsum(dh * xn * g), _colsum(dh), _colsum(dh * (1.0 + sc) * xn)]
    return [dx1 + _rms_bwd(dh * (1.0 + sc) * g, xn, r)], [_acc_rows(rows)]


def _rope_tables(pos_col, inv_freq):
    T = pos_col.shape[0]

    def body(p_ref, f_ref, c_ref, s_ref):
        ang = p_ref[...].astype(F32) * f_ref[...]
        lane = lax.broadcasted_iota(jnp.int32, ang.shape, 1)
        c_ref[...] = jnp.cos(ang)
        s_ref[...] = jnp.where(lane % HEAD_DIM < HEAD_DIM // 2, -1.0, 1.0) * jnp.sin(ang)

    return _rows("rope_tables", body, [pos_col, inv_freq],
                 [_sds((T, 128), F32), _sds((T, 128), F32)], T, tr=512)


def _swap_halves(t):
    W = t.shape[1]
    lane = lax.broadcasted_iota(jnp.int32, t.shape, 1)
    half = HEAD_DIM // 2
    return jnp.where(lane % HEAD_DIM < half, pltpu.roll(t, W - half, 1), pltpu.roll(t, half, 1))


def _widen(c, W):
    return c if W == 128 else jnp.concatenate([c] * (W // 128), axis=1)


def _rope(t, c, s):
    W = t.shape[1]
    return t * _widen(c, W) + _swap_halves(t) * _widen(s, W)


def _unrope(dy, c, s):
    W = dy.shape[1]
    return dy * _widen(c, W) + _swap_halves(dy * _widen(s, W))


def _attn_mask(n):
    qi = lax.broadcasted_iota(jnp.int32, (ATTN_BLK, 2 * ATTN_BLK), 0)
    kj = lax.broadcasted_iota(jnp.int32, (ATTN_BLK, 2 * ATTN_BLK), 1)
    rel = kj - ATTN_BLK
    return (rel <= qi) & (qi - rel < WINDOW) & ((n > 0) | (kj >= ATTN_BLK))


def _attn_load(cur, prv, cc, sc, cp, sp):
    x, xp = cur[...], prv[...]
    q = _rope(x[:, :512], cc[...], sc[...]) * (HEAD_DIM ** -0.5)
    k = jnp.concatenate([_rope(xp[:, 512:640], cp[...], sp[...]),
                         _rope(x[:, 512:640], cc[...], sc[...])], axis=0)
    v = jnp.concatenate([xp[:, 640:768], x[:, 640:768]], axis=0)
    return q, k, v


ROLLED = tuple(h for h in range(N_Q_HEADS) if h % 2 != h // (N_Q_HEADS // N_KV_HEADS))


def _pair_heads(t):
    half = lax.broadcasted_iota(jnp.int32, (ATTN_BLK, 128), 1) // HEAD_DIM
    return jnp.stack([jnp.where(half == h % 2, t[:, 128 * (h // 2):128 * (h // 2) + 128], 0.0)
                      for h in range(N_Q_HEADS)])


def _kv_heads(t):
    half = lax.broadcasted_iota(jnp.int32, t.shape, 1) // HEAD_DIM
    tr = pltpu.roll(t, HEAD_DIM, 1)
    return jnp.stack([jnp.where(half == h % 2, tr if h in ROLLED else t, 0.0)
                      for h in range(N_Q_HEADS)])


def _sink_column(snk):
    return jnp.stack([jnp.full((1, 1), snk[0, h], F32) for h in range(N_Q_HEADS)])


def _attn_probs(qh, kh, mask, sink):
    s = jnp.where(mask, _bdot(qh, kh, 2, 2), NEG_INF)
    m = jnp.maximum(jnp.max(s, axis=-1, keepdims=True), sink)
    p = jnp.exp(s - m)
    es = jnp.exp(sink - m)
    rl = 1.0 / (jnp.sum(p, axis=-1, keepdims=True) + es)
    return p, es, rl


def _attn_specs(order):
    blk = lambda w: pl.BlockSpec((ATTN_BLK, w), lambda s: (order(s), 0))
    prv = lambda w: pl.BlockSpec((ATTN_BLK, w), lambda s: (jnp.maximum(order(s) - 1, 0), 0))
    return [blk(768), prv(768), blk(128), blk(128), prv(128), prv(128),
            pl.BlockSpec(memory_space=pltpu.SMEM)]


def _attn_fwd(pa, cos, sin, sinks):
    T = pa.shape[0]
    nb = T // ATTN_BLK

    def body(cur, prv, cc, sc, cp, sp, snk, y_ref):
        n = pl.program_id(0)
        q, k, v = _attn_load(cur, prv, cc, sc, cp, sp)
        qh, kh, vh = _pair_heads(q).astype(BF16), _kv_heads(k).astype(BF16), _kv_heads(v).astype(BF16)
        p, _, rl = _attn_probs(qh, kh, _attn_mask(n), _sink_column(snk))
        o = _bdot(p.astype(BF16), vh, 2, 1) * rl
        for pair in range(N_Q_HEADS // 2):
            y_ref[:, 128 * pair:128 * pair + 128] = (o[2 * pair] + o[2 * pair + 1]).astype(BF16)

    return pl.pallas_call(
        body, name="attn_fwd", grid=(nb,), in_specs=_attn_specs(lambda s: s),
        out_specs=pl.BlockSpec((ATTN_BLK, 512), lambda n: (n, 0)),
        out_shape=_sds((T, 512), BF16), compiler_params=_params(("parallel",)))(
            pa, pa, cos, sin, cos, sin, sinks)


def _attn_bwd(pa, cos, sin, sinks, dy):
    T = pa.shape[0]
    nb = T // ATTN_BLK
    rev = lambda s: nb - 1 - s

    def body(cur, prv, cc, sc, cp, sp, snk, dy_ref, dq_ref, dkv_ref, dsink_ref, carry):
        n = rev(pl.program_id(0))

        @pl.when(pl.program_id(0) == 0)
        def _():
            dsink_ref[...] = jnp.zeros_like(dsink_ref)
            carry[...] = jnp.zeros_like(carry)

        q, k, v = _attn_load(cur, prv, cc, sc, cp, sp)
        qh, kh, vh = _pair_heads(q).astype(BF16), _kv_heads(k).astype(BF16), _kv_heads(v).astype(BF16)
        p, es, rl = _attn_probs(qh, kh, _attn_mask(n), _sink_column(snk))
        pn = p * rl
        do = _pair_heads(dy_ref[...]).astype(BF16)
        dp = _bdot(do, vh, 2, 2)
        delta = jnp.sum(pn * dp, axis=-1, keepdims=True)
        ds = (pn * (dp - delta)).astype(BF16)
        dsink = es * rl * delta
        dq = _bdot(ds, kh, 2, 1) * (HEAD_DIM ** -0.5)
        dkh = _bdot_rows(ds, qh)
        dvh = _bdot_rows(pn.astype(BF16), do)

        def fold(t):
            same = [t[h] for h in range(N_Q_HEADS) if h not in ROLLED]
            moved = [t[h] for h in ROLLED]
            return sum(same[1:], same[0]) + pltpu.roll(sum(moved[1:], moved[0]), HEAD_DIM, 1)

        dk, dv = fold(dkh), fold(dvh)
        for h in range(N_Q_HEADS):
            dsink_ref[h:h + 1, :] += -jnp.sum(dsink[h])
        for pair in range(N_Q_HEADS // 2):
            dq_ref[:, 128 * pair:128 * pair + 128] = _unrope(
                dq[2 * pair] + dq[2 * pair + 1], cc[...], sc[...]).astype(BF16)
        dkv_ref[:, 0:128] = _unrope(dk[ATTN_BLK:] + carry[:, 0:128], cc[...], sc[...]).astype(BF16)
        dkv_ref[:, 128:256] = (dv[ATTN_BLK:] + carry[:, 128:256]).astype(BF16)
        carry[:, 0:128] = dk[:ATTN_BLK]
        carry[:, 128:256] = dv[:ATTN_BLK]

    blk = lambda w: pl.BlockSpec((ATTN_BLK, w), lambda s: (rev(s), 0))
    return pl.pallas_call(
        body, name="attn_bwd", grid=(nb,), in_specs=_attn_specs(rev) + [blk(512)],
        out_specs=[blk(512), blk(256), pl.BlockSpec((8, 128), lambda s: (0, 0))],
        out_shape=[_sds((T, 512), BF16), _sds((T, 256), BF16), _sds((8, 128), F32)],
        scratch_shapes=[pltpu.VMEM((ATTN_BLK, 256), F32)],
        compiler_params=_params(("arbitrary",)))(pa, pa, cos, sin, cos, sin, sinks, dy)


CONV_COLS = 2 * MLSTM_HEADS * MLSTM_HEAD_DIM


def _conv_pre(cur_ref, halo_ref, w_ref, b_ref, i, tr):
    xx = jnp.concatenate([jnp.where(i > 0, halo_ref[...], 0.0), cur_ref[...]], axis=0)
    taps = [(pltpu.roll(xx, CONV_WIDTH - 1 - j, 0) if j < CONV_WIDTH - 1 else xx)[8:8 + tr]
            for j in range(CONV_WIDTH)]
    pre = b_ref[...]
    for j in range(CONV_WIDTH):
        pre = pre + taps[j] * w_ref[j:j + 1, :]
    return pre, taps


def _conv_specs(T, tr):
    return [pl.BlockSpec((tr, CONV_COLS), lambda i: (i, 0)),
            pl.BlockSpec((8, CONV_COLS), lambda i: (jnp.maximum(i * (tr // 8) - 1, 0), 0)),
            pl.BlockSpec((CONV_WIDTH, CONV_COLS), lambda i: (0, 0)),
            pl.BlockSpec((1, CONV_COLS), lambda i: (0, 0))]


def _conv_fwd(pm, w, b):
    T = pm.shape[0]
    tr = min(ROW_TILE, T)

    def body(cur_ref, halo_ref, w_ref, b_ref, o_ref):
        pre, _ = _conv_pre(cur_ref, halo_ref, w_ref, b_ref, pl.program_id(0), tr)
        o_ref[...] = pre * _sigmoid(pre)

    return pl.pallas_call(
        body, name="conv_fwd", grid=(T // tr,), in_specs=_conv_specs(T, tr),
        out_specs=pl.BlockSpec((tr, CONV_COLS), lambda i: (i, 0)),
        out_shape=_sds((T, CONV_COLS), F32), compiler_params=_params(("parallel",)))(pm, pm, w, b)


def _conv_bwd(pqk, w, b, dqk):
    T = pqk.shape[0]
    tr = min(ROW_TILE, T)
    nt = T // tr

    def body(cur_ref, prev_ref, next_ref, w_ref, b_ref, d_ref, dnext_ref, du_ref, acc_ref):
        i = pl.program_id(0)

        @pl.when(i == 0)
        def _():
            acc_ref[...] = jnp.zeros_like(acc_ref)

        last = i == nt - 1
        xx = jnp.concatenate([jnp.where(i > 0, prev_ref[...], 0.0), cur_ref[...],
                              jnp.where(last, 0.0, next_ref[...])], axis=0)
        taps = [(pltpu.roll(xx, CONV_WIDTH - 1 - j, 0) if j < CONV_WIDTH - 1 else xx)[8:16 + tr]
                for j in range(CONV_WIDTH)]
        pre = b_ref[...]
        for j in range(CONV_WIDTH):
            pre = pre + taps[j] * w_ref[j:j + 1, :]
        sg = _sigmoid(pre)
        dd = jnp.concatenate([d_ref[...], jnp.where(last, 0.0, dnext_ref[...])], axis=0)
        dpre = dd * (sg * (1.0 + pre * (1.0 - sg)))
        for j in range(CONV_WIDTH):
            acc_ref[j:j + 1, :] += _colsum(dpre[:tr] * taps[j][:tr])
        acc_ref[CONV_WIDTH:CONV_WIDTH + 1, :] += _colsum(dpre[:tr])
        du = dpre[:tr] * w_ref[CONV_WIDTH - 1:CONV_WIDTH, :]
        for j in range(CONV_WIDTH - 1):
            k = CONV_WIDTH - 1 - j
            du = du + pltpu.roll(dpre, tr + 8 - k, 0)[:tr] * w_ref[j:j + 1, :]
        du_ref[...] = du.astype(BF16)

    tile = pl.BlockSpec((tr, CONV_COLS), lambda i: (i, 0))
    after = pl.BlockSpec((8, CONV_COLS), lambda i: (jnp.minimum((i + 1) * (tr // 8), T // 8 - 1), 0))
    before = pl.BlockSpec((8, CONV_COLS), lambda i: (jnp.maximum(i * (tr // 8) - 1, 0), 0))
    return pl.pallas_call(
        body, name="conv_bwd", grid=(nt,),
        in_specs=[tile, before, after, pl.BlockSpec((CONV_WIDTH, CONV_COLS), lambda i: (0, 0)),
                  pl.BlockSpec((1, CONV_COLS), lambda i: (0, 0)), tile, after],
        out_specs=[tile, pl.BlockSpec((8, CONV_COLS), lambda i: (0, 0))],
        out_shape=[_sds((T, CONV_COLS), BF16), _sds((8, CONV_COLS), F32)],
        compiler_params=_params(("arbitrary",)))(pqk, pqk, pqk, w, b, dqk, dqk)


def _log_sigmoid(x):
    return jnp.minimum(x, 0.0) - jnp.log1p(jnp.exp(-jnp.abs(x)))


def _chunk_cumsum(x, axis):
    idx = lax.broadcasted_iota(jnp.int32, x.shape, axis) % MLSTM_CHUNK
    k = 1
    while k < MLSTM_CHUNK:
        x = x + jnp.where(idx >= k, pltpu.roll(x, k, axis), 0.0)
        k *= 2
    return x


def _chunk_rev_cumsum(x, axis):
    n = x.shape[axis]
    idx = lax.broadcasted_iota(jnp.int32, x.shape, axis) % MLSTM_CHUNK
    k = 1
    while k < MLSTM_CHUNK:
        x = x + jnp.where(idx < MLSTM_CHUNK - k, pltpu.roll(x, n - k, axis), 0.0)
        k *= 2
    return x


def _mlstm_gates(gc_ref, bc_ref, gr_ref, br_ref):
    gc = gc_ref[...] + bc_ref[...]
    gr = gr_ref[...] + br_ref[...]
    return gc, _chunk_cumsum(_log_sigmoid(gc), 0), gr, _chunk_cumsum(_log_sigmoid(gr), 1)


def _heads(ref, base=0):
    D = MLSTM_HEAD_DIM
    return jnp.stack([ref[:, base + D * h:base + D * h + D] for h in range(MLSTM_HEADS)])


def _mlstm_inputs(q_ref, k_ref, v_ref, gc, bc, gr, br):
    H = MLSTM_HEADS
    q, v = _heads(q_ref), _heads(v_ref)
    ks = _heads(k_ref) * (MLSTM_HEAD_DIM ** -0.5)
    return dict(
        q=q, ks=ks, qb=q.astype(BF16), kb=ks.astype(BF16), vb=v.astype(BF16),
        b_col=jnp.stack([bc[:, H + h:H + h + 1] for h in range(H)]),
        i_col=jnp.stack([gc[:, h:h + 1] for h in range(H)]),
        b_row=jnp.stack([br[H + h:H + h + 1, :] for h in range(H)]),
        i_row=jnp.stack([gr[h:h + 1, :] for h in range(H)]))


def _mlstm_head(f, c_prev, n_prev, m_prev):
    L = MLSTM_CHUNK
    q, qb = f["q"], f["qb"]
    t = lax.broadcasted_iota(jnp.int32, (1, 2 * L, 2 * L), 1)
    s = lax.broadcasted_iota(jnp.int32, (1, 2 * L, 2 * L), 2)
    mask = (t // L == s // L) & (s <= t)
    d = jnp.where(mask, f["b_col"] - f["b_row"] + f["i_row"], NEG_INF)
    row = lax.broadcasted_iota(jnp.int32, (1, 2 * L, 1), 1)
    inter = f["b_col"] + jnp.where(row < L, m_prev[0], m_prev[1])
    m_t = jnp.maximum(inter, jnp.max(d, axis=-1, keepdims=True))
    w_intra = jnp.exp(d - m_t)
    w_inter = jnp.exp(inter - m_t)
    sc = _bdot(qb, f["kb"], 2, 2) * w_intra
    qc = jnp.concatenate([_bdot(qb[:, :L], c_prev[0].astype(BF16), 2, 1),
                          _bdot(qb[:, L:], c_prev[1].astype(BF16), 2, 1)], axis=1)
    qn = jnp.concatenate([jnp.sum(q[:, :L] * n_prev[0], axis=-1, keepdims=True),
                          jnp.sum(q[:, L:] * n_prev[1], axis=-1, keepdims=True)], axis=1)
    num = _bdot(sc.astype(BF16), f["vb"], 2, 1) + w_inter * qc
    den = jnp.sum(sc, axis=-1, keepdims=True) + w_inter * qn
    return dict(f, w_intra=w_intra, w_inter=w_inter, sc=sc, qc=qc, qn=qn, num=num, den=den,
                floor=jnp.exp(-m_t))


def _mlstm_update(f, ch, c, n, m):
    L = MLSTM_CHUNK
    rows = slice(L * ch, L * ch + L)
    b_col = f["b_col"][:, rows]
    g_last = b_col[:, L - 1:L]
    a_col = g_last - b_col + f["i_col"][:, rows]
    m_new = jnp.maximum(g_last + m, jnp.max(a_col, axis=1, keepdims=True))
    decay = jnp.exp(g_last + m - m_new)
    e_a = jnp.exp(a_col - m_new)
    kw = f["ks"][:, rows] * e_a
    c_new = decay * c + _bdot_rows(kw.astype(BF16), f["vb"][:, rows])
    n_new = decay * n + jnp.sum(kw, axis=1, keepdims=True)
    return c_new, n_new, m_new, decay, e_a, kw


def _mlstm_specs(T, order):
    blk = lambda w, col: pl.BlockSpec((STEP_ROWS, w), lambda s: (order(s), col))
    return [blk(512, 0), blk(512, 1), blk(512, 0), blk(128, 0),
            pl.BlockSpec((1, 128), lambda s: (0, 0)),
            pl.BlockSpec((8, STEP_ROWS), lambda s: (0, order(s))),
            pl.BlockSpec((8, 128), lambda s: (0, 0))]


def _lanes(m):
    return jnp.broadcast_to(m, m.shape[:-1] + (128,))


def _mlstm_fwd(qk, pm, gcol, bcol, grow, brow):
    T = qk.shape[0]
    steps = T // STEP_ROWS
    H, D = MLSTM_HEADS, MLSTM_HEAD_DIM

    def body(q_ref, k_ref, v_ref, gc_ref, bc_ref, gr_ref, br_ref, h_ref, cs_ref, ns_ref, ms_ref,
             c_st, n_st, m_st):
        @pl.when(pl.program_id(0) == 0)
        def _():
            c_st[...] = jnp.zeros_like(c_st)
            n_st[...] = jnp.zeros_like(n_st)
            m_st[...] = jnp.zeros_like(m_st)

        f = _mlstm_inputs(q_ref, k_ref, v_ref, *_mlstm_gates(gc_ref, bc_ref, gr_ref, br_ref))
        c0, n0, m0 = c_st[...], n_st[...], m_st[:, :, 0:1]
        c1, n1, m1, _, _, _ = _mlstm_update(f, 0, c0, n0, m0)
        c2, n2, m2, _, _, _ = _mlstm_update(f, 1, c1, n1, m1)
        f = _mlstm_head(f, (c0, c1), (n0, n1), (m0, m1))
        h = f["num"] / jnp.maximum(jnp.abs(f["den"]), f["floor"])
        for hd in range(H):
            h_ref[:, D * hd:D * hd + D] = h[hd]
        cs_ref[0], cs_ref[1] = c0, c1
        ns_ref[0], ns_ref[1] = n0, n1
        ms_ref[0], ms_ref[1] = _lanes(m0), _lanes(m1)
        c_st[...], n_st[...], m_st[...] = c2, n2, _lanes(m2)

    vec = pl.BlockSpec((2, H, 1, 128), lambda s: (s, 0, 0, 0))
    return pl.pallas_call(
        body, name="mlstm_fwd", grid=(steps,), in_specs=_mlstm_specs(T, lambda s: s),
        out_specs=[pl.BlockSpec((STEP_ROWS, 512), lambda s: (s, 0)),
                   pl.BlockSpec((2, H, 128, 128), lambda s: (s, 0, 0, 0)), vec, vec],
        out_shape=[_sds((T, 512), F32), _sds((2 * steps, H, 128, 128), F32),
                   _sds((2 * steps, H, 1, 128), F32), _sds((2 * steps, H, 1, 128), F32)],
        scratch_shapes=[pltpu.VMEM((H, 128, 128), F32), pltpu.VMEM((H, 1, 128), F32),
                        pltpu.VMEM((H, 1, 128), F32)],
        compiler_params=_params(("arbitrary",)))(qk, qk, pm, gcol, bcol, grow, brow)


def _mlstm_bwd(qk, pm, gcol, bcol, grow, brow, cs, ns, ms, dh):
    T = qk.shape[0]
    steps = T // STEP_ROWS
    H, L, D = MLSTM_HEADS, MLSTM_CHUNK, MLSTM_HEAD_DIM
    rev = lambda s: steps - 1 - s

    def body(q_ref, k_ref, v_ref, gc_ref, bc_ref, gr_ref, br_ref, cs_ref, ns_ref, ms_ref, dh_ref,
             dqk_ref, dv_ref, dgc_ref, dgr_ref, dc_st, dn_st):
        @pl.when(pl.program_id(0) == 0)
        def _():
            dc_st[...] = jnp.zeros_like(dc_st)
            dn_st[...] = jnp.zeros_like(dn_st)

        f = _mlstm_inputs(q_ref, k_ref, v_ref, *_mlstm_gates(gc_ref, bc_ref, gr_ref, br_ref))
        c_prev = (cs_ref[0], cs_ref[1])
        n_prev = (ns_ref[0], ns_ref[1])
        m_prev = (ms_ref[0, :, :, 0:1], ms_ref[1, :, :, 0:1])
        f = _mlstm_head(f, c_prev, n_prev, m_prev)
        big = jnp.abs(f["den"]) > f["floor"]
        rden = 1.0 / jnp.where(big, jnp.abs(f["den"]), f["floor"])
        dnum = _heads(dh_ref) * rden
        hdh = jnp.sum(f["num"] * dnum, axis=-1, keepdims=True)
        dden = jnp.where(big, -hdh * rden * jnp.sign(f["den"]), 0.0)
        dnum_b = dnum.astype(BF16)
        dsc = _bdot(dnum_b, f["vb"], 2, 2) + dden
        g = dsc * f["sc"]
        dv = _bdot_rows(f["sc"].astype(BF16), dnum_b)
        dqk_ = (dsc * f["w_intra"]).astype(BF16)
        dq = _bdot(dqk_, f["kb"], 2, 1)
        dks = _bdot_rows(dqk_, f["qb"])
        wdn = f["w_inter"] * dnum
        wdn_b = wdn.astype(BF16)
        wdd = f["w_inter"] * dden
        u = jnp.sum(f["qc"] * wdn, axis=-1, keepdims=True) + wdd * f["qn"]
        dks_s, dv_s, z_s, dg_s = [None, None], [None, None], [None, None], [None, None]
        dcn, dnn = dc_st[...], dn_st[...]
        for ch in (1, 0):
            rows = slice(L * ch, L * ch + L)
            _, _, _, decay, e_a, kw = _mlstm_update(f, ch, c_prev[ch], n_prev[ch], m_prev[ch])
            dcn_b = dcn.astype(BF16)
            dkw = _bdot(f["vb"][:, rows], dcn_b, 2, 2) + dnn
            dks_s[ch] = e_a * dkw
            dv_s[ch] = _bdot(kw.astype(BF16), dcn_b, 2, 1)
            z_s[ch] = e_a * jnp.sum(f["ks"][:, rows] * dkw, axis=-1, keepdims=True)
            dg_s[ch] = jnp.sum(z_s[ch], axis=1, keepdims=True) + decay * (
                jnp.sum(c_prev[ch] * dcn, axis=(1, 2), keepdims=True)
                + jnp.sum(n_prev[ch] * dnn, axis=(1, 2), keepdims=True))
            dcn = decay * dcn + _bdot_rows(f["qb"][:, rows], wdn_b[:, rows])
            dnn = decay * dnn + jnp.sum(wdd[:, rows] * f["q"][:, rows], axis=1, keepdims=True)
        dc_st[...], dn_st[...] = dcn, dnn
        dq = dq + jnp.concatenate(
            [_bdot(wdn_b[:, :L], c_prev[0].astype(BF16), 2, 2) + wdd[:, :L] * n_prev[0],
             _bdot(wdn_b[:, L:], c_prev[1].astype(BF16), 2, 2) + wdd[:, L:] * n_prev[1]], axis=1)
        dks = (dks + jnp.concatenate(dks_s, axis=1)) * (D ** -0.5)
        dv = dv + jnp.concatenate(dv_s, axis=1)
        z = jnp.concatenate(z_s, axis=1)
        row = lax.broadcasted_iota(jnp.int32, (1, STEP_ROWS, 1), 1)
        dg_col = jnp.where(row == L - 1, dg_s[0], 0.0) + jnp.where(row == 2 * L - 1, dg_s[1], 0.0)
        db_col = jnp.sum(g, axis=-1, keepdims=True) + u - z + dg_col
        g_row = jnp.sum(g, axis=1, keepdims=True)
        lane = lax.broadcasted_iota(jnp.int32, (STEP_ROWS, 128), 1)
        sub = lax.broadcasted_iota(jnp.int32, (8, STEP_ROWS), 0)
        dgc = jnp.zeros((STEP_ROWS, 128), F32)
        dgr = jnp.zeros((8, STEP_ROWS), F32)
        for hd in range(H):
            dgc = dgc + jnp.where(lane == hd, z[hd], 0.0) + jnp.where(lane == H + hd, db_col[hd], 0.0)
            dgr = dgr + jnp.where(sub == hd, g_row[hd], 0.0) - jnp.where(sub == H + hd, g_row[hd], 0.0)
            dqk_ref[:, D * hd:D * hd + D] = dq[hd]
            dqk_ref[:, H * D + D * hd:H * D + D * hd + D] = dks[hd]
            dv_ref[:, D * hd:D * hd + D] = dv[hd].astype(BF16)
        dgc_ref[...] = dgc
        dgr_ref[...] = dgr

    return pl.pallas_call(
        body, name="mlstm_bwd", grid=(steps,),
        in_specs=_mlstm_specs(T, rev) + [
            pl.BlockSpec((2, H, 128, 128), lambda s: (rev(s), 0, 0, 0)),
            pl.BlockSpec((2, H, 1, 128), lambda s: (rev(s), 0, 0, 0)),
            pl.BlockSpec((2, H, 1, 128), lambda s: (rev(s), 0, 0, 0)),
            pl.BlockSpec((STEP_ROWS, 512), lambda s: (rev(s), 0))],
        out_specs=[pl.BlockSpec((STEP_ROWS, 1024), lambda s: (rev(s), 0)),
                   pl.BlockSpec((STEP_ROWS, 512), lambda s: (rev(s), 0)),
                   pl.BlockSpec((STEP_ROWS, 128), lambda s: (rev(s), 0)),
                   pl.BlockSpec((8, STEP_ROWS), lambda s: (0, rev(s)))],
        out_shape=[_sds((T, 1024), F32), _sds((T, 512), BF16), _sds((T, 128), F32), _sds((8, T), F32)],
        scratch_shapes=[pltpu.VMEM((H, 128, 128), F32), pltpu.VMEM((H, 1, 128), F32)],
        compiler_params=_params(("arbitrary",)))(qk, qk, pm, gcol, bcol, grow, brow, cs, ns, ms, dh)


def _rows_to_lanes(x):
    eye = (lax.broadcasted_iota(jnp.int32, (8, 128), 0)
           == lax.broadcasted_iota(jnp.int32, (8, 128), 1)).astype(BF16)
    out, rest = None, x
    for _ in range(3):
        piece = rest.astype(BF16)
        rest = rest - piece.astype(F32)
        t = _dot(piece, eye, 0, 0)
        out = t if out is None else out + t
    return out


def _gate_bwd(dgc, dgr, gcol, bcol):
    T = dgc.shape[0]
    tr = min(ROW_TILE, T)

    def body(a_ref, b_ref, g_ref, bias_ref, o_ref, acc_ref):
        i = pl.program_id(0)

        @pl.when(i == 0)
        def _():
            acc_ref[...] = jnp.zeros_like(acc_ref)

        d = a_ref[...] + _rows_to_lanes(b_ref[:, pl.ds(pl.multiple_of(i * tr, 128), tr)])
        lane = lax.broadcasted_iota(jnp.int32, d.shape, 1)
        is_f = (lane >= MLSTM_HEADS) & (lane < 2 * MLSTM_HEADS)
        dlogf = _chunk_rev_cumsum(jnp.where(is_f, d, 0.0), 0)
        out = jnp.where(is_f, dlogf * _sigmoid(-(g_ref[...] + bias_ref[...])), d)
        o_ref[...] = out.astype(BF16)
        acc_ref[0:1, :] += _colsum(out)

    return _rows("gate_bwd", body, [dgc, dgr, gcol, bcol],
                 [_sds((T, 128), BF16), _sds((8, 128), F32)], T, tr=tr)


def _head_norm(h, mu_axis=-1):
    mu = jnp.mean(h, axis=-1, keepdims=True)
    hc = h - mu
    r = lax.rsqrt(jnp.mean(hc * hc, axis=-1, keepdims=True) + NORM_EPS)
    return hc * r, r


def _mlstm_out(hm, pm, w):
    T = hm.shape[0]
    D = MLSTM_HEAD_DIM

    def body(h_ref, o_ref, w_ref, y_ref):
        for hd in range(MLSTM_HEADS):
            cols = slice(D * hd, D * hd + D)
            hn, _ = _head_norm(h_ref[:, cols])
            y_ref[:, cols] = (_sigmoid(o_ref[:, cols].astype(F32)) * hn * w_ref[:, cols]).astype(BF16)

    tr = min(ROW_TILE, T)
    return pl.pallas_call(
        body, name="mlstm_out", grid=(T // tr,),
        in_specs=[pl.BlockSpec((tr, 512), lambda i: (i, 0)), pl.BlockSpec((tr, 512), lambda i: (i, 1)),
                  pl.BlockSpec((1, 512), lambda i: (0, 0))],
        out_specs=pl.BlockSpec((tr, 512), lambda i: (i, 0)), out_shape=_sds((T, 512), BF16),
        compiler_params=_params(("parallel",)))(hm, pm, w)


def _mlstm_out_bwd_rows(ps, es):
    hm, vo, w_all = es
    D, width = MLSTM_HEAD_DIM, MLSTM_HEADS * MLSTM_HEAD_DIM
    dhs, dos, dws = [], [], []
    for hd in range(MLSTM_HEADS):
        cols = slice(D * hd, D * hd + D)
        hn, r = _head_norm(hm[:, cols])
        sg = _sigmoid(vo[:, width + D * hd:width + D * hd + D].astype(F32))
        dy, w = ps[0][:, cols], w_all[:, cols]
        dos.append(dy * hn * w * sg * (1.0 - sg))
        dyn = dy * sg
        dws.append(_colsum(dyn * hn))
        dhn = dyn * w
        dhs.append(r * (dhn - jnp.mean(dhn, axis=-1, keepdims=True)
                        - hn * jnp.mean(dhn * hn, axis=-1, keepdims=True)))
    cat = lambda parts: jnp.concatenate(parts, axis=1)
    return [cat(dhs), cat(dos)], [_acc_rows([cat(dws)])]


ADAM_TILE_ELEMS = 256 * 1024


def _adamw(name, w, g, m, v):
    R, C = w.shape
    fits = [t for t in range(8, R + 1, 8) if R % t == 0 and t * C <= ADAM_TILE_ELEMS]
    if fits or R * C <= ADAM_TILE_ELEMS:
        tr = fits[-1] if fits else R
        spec, grid = pl.BlockSpec((tr, C), lambda i: (i, 0)), (R // tr,)
    else:
        spec, grid = pl.BlockSpec((R, 128), lambda i: (0, i)), (C // 128,)
    c1 = 1.0 - ADAM_B1 ** ADAM_STEP
    c2 = 1.0 - ADAM_B2 ** ADAM_STEP

    def body(w_ref, g_ref, m_ref, v_ref, d_ref, mo_ref, vo_ref):
        g = g_ref[...]
        m = ADAM_B1 * m_ref[...] + (1.0 - ADAM_B1) * g
        v = ADAM_B2 * v_ref[...] + (1.0 - ADAM_B2) * (g * g)
        mo_ref[...] = m
        vo_ref[...] = v
        d_ref[...] = -ADAM_LR * ((m / c1) / (jnp.sqrt(v / c2) + ADAM_EPS) + ADAM_WD * w_ref[...])

    return pl.pallas_call(
        body, name=name, grid=grid, in_specs=[spec] * 4, out_specs=[spec] * 3,
        out_shape=[_sds((R, C), F32)] * 3, compiler_params=_params(("parallel",)))(w, g, m, v)


def _place():
    return lax.axis_index("x"), lax.axis_index("y"), lax.axis_index("c")


def _all_gather8(name, blk, space):
    m, n = blk.shape

    def body(x_ref, out_ref, send_sems, recv_sems, local_sem):
        x, y, c = _place()
        me, sibling = (x, y, c), (x, y, 1 - c)
        chips = [(1 - x, y), (x, 1 - y), (1 - x, 1 - y)]

        def rows(px, py, pc):
            return out_ref.at[pl.ds((4 * px + 2 * py + pc) * m, m), :]

        def copy(k, block, to, src=None):
            return pltpu.make_async_remote_copy(
                src_ref=rows(*block) if src is None else src, dst_ref=rows(*block),
                send_sem=send_sems.at[k], recv_sem=recv_sems.at[k],
                device_id=to, device_id_type=MESH)

        mine = pltpu.make_async_copy(x_ref, rows(*me), local_sem)
        mine.start()
        first = [copy(0, me, sibling, src=x_ref)]
        first += [copy(1 + j, me, (*chip, c), src=x_ref) for j, chip in enumerate(chips)]
        for cp in first:
            cp.start()
        passed = [copy(4 + j, (*chip, c), sibling) for j, chip in enumerate(chips)]
        for j, chip in enumerate(chips):
            copy(1 + j, (*chip, c), me).wait_recv()
            passed[j].start()
        copy(0, sibling, me).wait_recv()
        for j, chip in enumerate(chips):
            copy(4 + j, (*chip, 1 - c), me).wait_recv()
        for cp in first + passed:
            cp.wait_send()
        mine.wait()

    return pl.pallas_call(
        body, name=name, out_shape=_sds((8 * m, n), blk.dtype),
        in_specs=[pl.BlockSpec(memory_space=space)], out_specs=pl.BlockSpec(memory_space=space),
        scratch_shapes=[pltpu.SemaphoreType.DMA((7,)), pltpu.SemaphoreType.DMA((7,)),
                        pltpu.SemaphoreType.DMA],
        compiler_params=pltpu.CompilerParams(vmem_limit_bytes=VMEM_LIMIT))(blk)


def _hbm_specs(n):
    return [pl.BlockSpec(memory_space=pl.ANY)] * n


def _swap_halves_sibling(name, srcs):
    nw = len(srcs)

    def body(*refs):
        src_refs, dst_refs, send_sems, recv_sems = refs[:nw], refs[nw:2 * nw], refs[2 * nw], refs[2 * nw + 1]
        x, y, c = _place()
        cps = [pltpu.make_async_remote_copy(
            src_ref=src_refs[w].at[pl.ds(0, 4), 1 - c], dst_ref=dst_refs[w],
            send_sem=send_sems.at[w], recv_sem=recv_sems.at[w], device_id=(x, y, 1 - c),
            device_id_type=MESH) for w in range(nw)]
        for cp in cps:
            cp.start()
        for cp in cps:
            cp.wait()

    return pl.pallas_call(
        body, name=name, out_shape=[_sds(s.shape[:1] + s.shape[2:], s.dtype) for s in srcs],
        in_specs=_hbm_specs(nw), out_specs=_hbm_specs(nw),
        scratch_shapes=[pltpu.SemaphoreType.DMA((nw,)), pltpu.SemaphoreType.DMA((nw,))])(*srcs)


def _split_start(name, srcs, lands, copies, per_array, after):
    nw = len(srcs)

    def body(*refs):
        send_sems, recv_sems, token = refs[2 * nw + 1], refs[2 * nw + 2], refs[-1]
        for w in range(nw):
            for k, (s, d, dev) in enumerate(copies(refs[w], refs[nw + w], *_place())):
                pltpu.make_async_remote_copy(
                    src_ref=s, dst_ref=d, send_sem=send_sems.at[w * per_array + k],
                    recv_sem=recv_sems.at[w * per_array + k], device_id=dev, device_id_type=MESH).start()
        token[...] = jnp.zeros_like(token)

    hbm, sem = pl.BlockSpec(memory_space=pltpu.HBM), pl.BlockSpec(memory_space=pltpu.SEMAPHORE)
    arrays = list(srcs) + list(lands)
    out = pl.pallas_call(
        body, name=name,
        out_shape=(pltpu.SemaphoreType.DMA((nw * per_array,)), pltpu.SemaphoreType.DMA((nw * per_array,)),
                   *[pltpu.HBM(a.shape, a.dtype) for a in arrays], _sds((8, 128), F32)),
        in_specs=[hbm] * (2 * nw) + [pl.BlockSpec(memory_space=pl.ANY)],
        out_specs=(sem, sem, *[hbm] * (2 * nw), pl.BlockSpec(memory_space=pltpu.VMEM)),
        input_output_aliases={i: 2 + i for i in range(2 * nw)},
        compiler_params=pltpu.CompilerParams(has_side_effects=pltpu.SideEffectType.DATAFLOW_SIDE_EFFECTING))(
            *[pltpu.with_memory_space_constraint(a, pltpu.HBM) for a in arrays], after)
    return out[0], out[1], out[2:2 + nw], out[2 + nw:2 + 2 * nw], out[-1]


def _split_wait(name, started, after, waits, per_array):
    send_sems, recv_sems, srcs, lands, _ = started
    nw = len(srcs)

    def body(*refs):
        send_sems, recv_sems = refs[2 * nw], refs[2 * nw + 1]
        x, y, c = _place()
        for w in range(nw):
            for k, (s, d) in enumerate(waits(refs[w], refs[nw + w], x, y, c)):
                cp = pltpu.make_async_remote_copy(
                    src_ref=s, dst_ref=d, send_sem=send_sems.at[w * per_array + k],
                    recv_sem=recv_sems.at[w * per_array + k], device_id=(x, y, 1 - c),
                    device_id_type=MESH)
                cp.wait_send()
                cp.wait_recv()

    hbm, sem = pl.BlockSpec(memory_space=pltpu.HBM), pl.BlockSpec(memory_space=pltpu.SEMAPHORE)
    arrays = list(srcs) + list(lands)
    out = pl.pallas_call(
        body, name=name, out_shape=tuple(pltpu.HBM(a.shape, a.dtype) for a in arrays),
        in_specs=[hbm] * (2 * nw) + [sem, sem, pl.BlockSpec(memory_space=pl.ANY)],
        out_specs=tuple([hbm] * (2 * nw)), input_output_aliases={i: i for i in range(2 * nw)},
        compiler_params=pltpu.CompilerParams(has_side_effects=pltpu.SideEffectType.DATAFLOW_SIDE_EFFECTING))(
            *arrays, send_sems, recv_sems, after)
    return list(out[nw:])


def _other_chips(x, y):
    return [(1 - x, y), (x, 1 - y), (1 - x, 1 - y)]


def _gather_sends(src_ref, land_ref, x, y, c):
    to = land_ref.at[2 * x + y, c]
    return [(src_ref, to, (x, y, 1 - c))] + [(src_ref, to, (px, py, c)) for px, py in _other_chips(x, y)]


def _gather_lands(src_ref, land_ref, x, y, c):
    return [(src_ref, land_ref.at[2 * x + y, 1 - c])] + [
        (src_ref, land_ref.at[2 * px + py, c]) for px, py in _other_chips(x, y)]


def _gather_sends_all(src_ref, land_ref, x, y, c):
    to = land_ref.at[2 * x + y, c]
    return [(src_ref, to, (x, y, 1 - c))] + [
        (src_ref, to, (px, py, pc)) for px, py in _other_chips(x, y) for pc in (c, 1 - c)]


def _gather_lands_all(src_ref, land_ref, x, y, c):
    return [(src_ref, land_ref.at[2 * x + y, 1 - c])] + [
        (src_ref, land_ref.at[2 * px + py, pc]) for px, py in _other_chips(x, y) for pc in (c, 1 - c)]


def _scatter_sends(src_ref, land_ref, x, y, c):
    return [(src_ref.at[2 * px + py], land_ref.at[2 * x + y], (px, py, c)) for px, py in _other_chips(x, y)]


def _scatter_lands(src_ref, land_ref, x, y, c):
    return [(src_ref.at[2 * x + y], land_ref.at[2 * px + py]) for px, py in _other_chips(x, y)]


def _forward_sibling(name, lands):
    nw = len(lands)

    def body(*refs):
        land_refs, out_refs, send_sems, recv_sems = refs[:nw], refs[nw:2 * nw], refs[2 * nw], refs[2 * nw + 1]
        x, y, c = _place()
        cps = []
        for w in range(nw):
            cps += [pltpu.make_async_remote_copy(
                src_ref=land_refs[w].at[2 * px + py, c], dst_ref=out_refs[w].at[2 * px + py, c],
                send_sem=send_sems.at[w, j], recv_sem=recv_sems.at[w, j], device_id=(x, y, 1 - c),
                device_id_type=MESH) for j, (px, py) in enumerate(_other_chips(x, y))]
        for cp in cps:
            cp.start()
        for w in range(nw):
            for j, (px, py) in enumerate(_other_chips(x, y)):
                slot = out_refs[w].at[2 * px + py, 1 - c]
                pltpu.make_async_remote_copy(src_ref=slot, dst_ref=slot, send_sem=send_sems.at[w, j],
                                             recv_sem=recv_sems.at[w, j], device_id=(x, y, 1 - c),
                                             device_id_type=MESH).wait_recv()
        for cp in cps:
            cp.wait_send()

    return pl.pallas_call(
        body, name=name, out_shape=[_sds(a.shape, a.dtype) for a in lands],
        in_specs=_hbm_specs(nw), out_specs=_hbm_specs(nw), input_output_aliases={i: i for i in range(nw)},
        scratch_shapes=[pltpu.SemaphoreType.DMA((nw, 3)), pltpu.SemaphoreType.DMA((nw, 3))])(*lands)


def _share_halves(name, halves):
    nw = len(halves)

    def body(*refs):
        in_refs, out_refs, send_sems, recv_sems = refs[:nw], refs[nw:2 * nw], refs[2 * nw], refs[2 * nw + 1]
        x, y, c = _place()
        cps = [pltpu.make_async_remote_copy(
            src_ref=in_refs[w].at[c], dst_ref=out_refs[w].at[c], send_sem=send_sems.at[w],
            recv_sem=recv_sems.at[w], device_id=(x, y, 1 - c), device_id_type=MESH) for w in range(nw)]
        for cp in cps:
            cp.start()
        for w in range(nw):
            slot = out_refs[w].at[1 - c]
            pltpu.make_async_remote_copy(src_ref=slot, dst_ref=slot, send_sem=send_sems.at[w],
                                         recv_sem=recv_sems.at[w], device_id=(x, y, 1 - c),
                                         device_id_type=MESH).wait_recv()
        for cp in cps:
            cp.wait_send()

    return pl.pallas_call(
        body, name=name, out_shape=[_sds(a.shape, a.dtype) for a in halves],
        in_specs=_hbm_specs(nw), out_specs=_hbm_specs(nw), input_output_aliases={i: i for i in range(nw)},
        scratch_shapes=[pltpu.SemaphoreType.DMA((nw,)), pltpu.SemaphoreType.DMA((nw,))])(*halves)


def _place_blocks(name, blks, place):
    nw = len(blks)

    def body(p_ref, *refs):
        for b_ref, o_ref in zip(refs[:nw], refs[nw:]):
            o_ref[...] = b_ref[...]

    return pl.pallas_call(
        body, name=name,
        grid_spec=pltpu.PrefetchScalarGridSpec(
            num_scalar_prefetch=1, grid=(1,),
            in_specs=[pl.BlockSpec(b.shape, lambda i, p: (0, 0)) for b in blks],
            out_specs=[pl.BlockSpec((None, None) + b.shape, lambda i, p: (p[0], p[1], 0, 0)) for b in blks]),
        out_shape=[_sds((4, 2) + b.shape, b.dtype) for b in blks],
        compiler_params=_params(("arbitrary",)))(place, *blks)


def _pair_sum(name, fulls, gots, place):
    nw = len(fulls)

    def body(p_ref, *refs):
        s = pl.program_id(0)
        for a_ref, b_ref, o_ref, l_ref in zip(refs[:nw], refs[nw:2 * nw], refs[2 * nw:3 * nw], refs[3 * nw:]):
            o_ref[...] = (a_ref[...].astype(F32) + b_ref[...].astype(F32)).astype(o_ref.dtype)

            @pl.when(s == p_ref[0])
            def _():
                l_ref[...] = o_ref[...]

    slab = lambda a: pl.BlockSpec((None,) + a.shape[1:], lambda s, p: (s, 0, 0))
    mine = lambda a: pl.BlockSpec((None,) + a.shape[1:], lambda s, p: (p[0], 0, 0))
    out = pl.pallas_call(
        body, name=name,
        grid_spec=pltpu.PrefetchScalarGridSpec(
            num_scalar_prefetch=1, grid=(4,),
            in_specs=[pl.BlockSpec((None, None) + a.shape[2:], lambda s, p: (s, p[1], 0, 0)) for a in fulls]
            + [slab(b) for b in gots],
            out_specs=[slab(b) for b in gots] + [mine(b) for b in gots]),
        out_shape=[_sds(b.shape, BF16) for b in gots] * 2,
        compiler_params=_params(("arbitrary",)))(place, *fulls, *gots)
    return out[:nw], out[nw:]


def _sum4(name, arrs, place):
    nw = len(arrs)

    def body(p_ref, *refs):
        for a_ref, o_ref in zip(refs[:nw], refs[nw:]):
            acc = a_ref[0].astype(F32)
            for s in range(1, 4):
                acc = acc + a_ref[s].astype(F32)
            o_ref[...] = acc

    return pl.pallas_call(
        body, name=name,
        grid_spec=pltpu.PrefetchScalarGridSpec(
            num_scalar_prefetch=1, grid=(1,),
            in_specs=[pl.BlockSpec(a.shape, lambda i, p: (0, 0, 0)) for a in arrs],
            out_specs=[pl.BlockSpec((None,) + a.shape[1:], lambda i, p: (p[1], 0, 0)) for a in arrs]),
        out_shape=[_sds((2,) + a.shape[1:], F32) for a in arrs],
        compiler_params=_params(("arbitrary",)))(place, *arrs)


def _small_update(gathered, w, m, v):
    n = w.shape[1]
    tn = 2048
    c1 = 1.0 - ADAM_B1 ** ADAM_STEP
    c2 = 1.0 - ADAM_B2 ** ADAM_STEP

    def body(g_ref, w_ref, m_ref, v_ref, go_ref, d_ref, mo_ref, vo_ref):
        g = g_ref[0:1, :]
        for d in range(1, 8):
            g = g + g_ref[d:d + 1, :]
        go_ref[...] = g
        m = ADAM_B1 * m_ref[...] + (1.0 - ADAM_B1) * g
        v = ADAM_B2 * v_ref[...] + (1.0 - ADAM_B2) * (g * g)
        mo_ref[...] = m
        vo_ref[...] = v
        d_ref[...] = -ADAM_LR * ((m / c1) / (jnp.sqrt(v / c2) + ADAM_EPS) + ADAM_WD * w_ref[...])

    row = pl.BlockSpec((1, tn), lambda i: (0, i))
    return pl.pallas_call(
        body, name="small_update", grid=(n // tn,),
        in_specs=[pl.BlockSpec((8, tn), lambda i: (0, i)), row, row, row], out_specs=[row] * 4,
        out_shape=[_sds((1, n), F32)] * 4, compiler_params=_params(("parallel",)))(gathered, w, m, v)


def _swiglu(ps, es):
    g, u = ps
    return g * _sigmoid(g) * u, g, u


def _swiglu_bwd(ps, es):
    g, u = es[0].astype(F32), es[1].astype(F32)
    sg = _sigmoid(g)
    return ps[0] * u * (sg * (1.0 + g * (1.0 - sg))), ps[0] * (g * sg)


def _merge(ps, es):
    ga, gm = [e.astype(F32) for e in es]
    return (_sigmoid(ga) * ps[0] + _sigmoid(gm) * ps[1],)


def _merge_bwd(ps, es):
    dm, a, b = ps
    ga, gm = [e.astype(F32) for e in es]
    sa, sm = _sigmoid(ga), _sigmoid(gm)
    return dm * sa, dm * sm, dm * a * (sa * (1.0 - sa)), dm * b * (sm * (1.0 - sm))


W_IN_PIECES = (("q", 512), ("kv", 256), ("mqk", 1024), ("mv", 512), ("mo", 512), ("if", 8),
               ("ga", 1024), ("gm", 1024))


def _local_step(x, tgt, pos_col, mod, sp, in_weights, late_weights, ffn_grads, mixer_grads):
    sh_m, sc_m, gate_m, sh_f, sc_f, gate_f = mod
    inv = ROPE_THETA ** (-2.0 * jnp.arange(HEAD_DIM // 2, dtype=F32) / HEAD_DIM)
    cos, sin = _rope_tables(pos_col, jnp.tile(inv, 4).reshape(1, 128))
    W = dict(in_weights(cos))
    h, pa, pqk, pvo, pif, pg = _proj_in(x, sp["g_pre_mix"], sc_m, sh_m, [
        (W["q+kv"], F32, 256), (W["mqk"], F32, 512), (W["mv+mo"], BF16, 512), (W["if"], F32, 128),
        (W["ga+gm"], BF16, 512)])
    ya = _attn_fwd(pa, cos, sin, sp["sinks"])
    qk = _conv_fwd(pqk, sp["conv_w"], sp["conv_b"])
    bcol = jnp.pad(sp["b_if"], ((0, 0), (0, 120)))
    brow = jnp.broadcast_to(sp["b_if"].reshape(8, 1), (8, 128))
    grow = pif[:, :8].T
    hm, cs, ns, ms = _mlstm_fwd(qk, pvo, pif, bcol, grow, brow)
    ym = _mlstm_out(hm, pvo, sp["norm_w"])
    W.update(late_weights(ym))
    w_fg, w_fu, w_fd = W["fg"], W["fu"], W["fd"]
    merged, = _mm("branches", [[(ya, W["ba"])], [(ym, W["bm"])]], [(pg, 0), (pg, 1)], _merge, [BF16],
                  cn=512, nt=True)
    wide, narrow = (D_MODEL, F32), (D_MODEL, BF16)
    mix, x1, h2 = _mm_rows("mix_out", [[(merged, W["out"])]],
                           [x, gate_m, sp["g_post_mix"], sp["g_pre_ffn"], sc_f, sh_f],
                           _res_norm_rows, [wide, wide, narrow], [], cn=512)
    act, gt, up = _mm("ffn_in", [[(h2, w_fg)], [(h2, w_fu)]], [], _swiglu, [BF16] * 3,
                      cn=256, nt=True)
    dy, dff, acc_l, loss = _mm_rows("ffn_down", [[(act, w_fd)]], [x1, tgt, gate_f, sp["g_post_ffn"]],
                                    _final_loss_rows, [wide, narrow], [(8, D_MODEL), (1, 128)], cn=512)

    G = {}
    dgt, dup = _mm("ffn_down_bwd", [[(dff, w_fd)]], [gt, up], _swiglu_bwd, [BF16, BF16],
                   cn=256, nt=True)
    g_fd, = _mm_tn_group("dw_ffn_down", [act], dff, BF16)
    g_fg, = _mm_tn_group("dw_ffn_gate", [dgt], h2, BF16)
    g_fu, = _mm_tn_group("dw_ffn_up", [dup], h2, BF16)
    tie = ffn_grads(g_fg, g_fu, g_fd)
    dx1, dmix, acc_r = _mm_rows(
        "ffn_in_bwd", [[(dgt, w_fg), (dup, w_fu)]],
        [x1, mix, dy, sc_f + tie, gate_m, sp["g_pre_ffn"], sp["g_post_mix"]],
        _res_norm_bwd_rows, [wide, narrow], [(8, D_MODEL)], cn=512, tm=256)
    d_a, d_m, dga, dgm = _mm("mix_out_bwd", [[(dmix, W["out"])], [(ya, W["ba"])], [(ym, W["bm"])]],
                             [(pg, 0), (pg, 1)], _merge_bwd, [BF16] * 4, cn=512, nt=True)
    G["out"], = _mm_tn_group("dw_out", [merged], dmix, BF16)
    dya, = _mm("branch_attn_bwd", [[(d_a, W["ba"])]], [], _first, [F32], cn=512)
    heads = MLSTM_HEADS * MLSTM_HEAD_DIM
    dhm, do_m, acc_n = _mm_rows("branch_mlstm_bwd", [[(d_m, W["bm"])]], [hm, pvo, sp["norm_w"]],
                                _mlstm_out_bwd_rows, [(heads, F32), (heads, BF16)], [(8, heads)], cn=512)
    G["ba"], = _mm_tn_group("dw_branch_attn", [d_a], ya, BF16)
    G["bm"], = _mm_tn_group("dw_branch_mlstm", [d_m], ym, BF16)
    dqk, dv_m, dgc, dgr = _mlstm_bwd(qk, pvo, pif, bcol, grow, brow, cs, ns, ms, dhm)
    dif, acc_g = _gate_bwd(dgc, dgr, pif, bcol)
    du, acc_c = _conv_bwd(pqk, sp["conv_w"], sp["conv_b"], dqk)
    dq_a, dkv, dsink = _attn_bwd(pa, cos, sin, sp["sinks"], dya)
    dproj = {"q": dq_a, "kv": dkv, "mqk": du, "mv": dv_m, "mo": do_m, "if": dif, "ga": dga, "gm": dgm}
    names = [k for k, _ in W_IN_PIECES]
    for part in (names[:4], names[4:]):
        G.update(zip(part, _mm_tn_group("dw_in_from_" + part[0], [dproj[k] for k in part], h, BF16)))
    w_tied = dict(W, **{"if": W["if"] + mixer_grads(G).astype(BF16)})
    dx, acc_p = _mm_rows("proj_bwd", [[(dproj[k], w_tied[k]) for k, _ in W_IN_PIECES]],
                         [x, dx1, sp["g_pre_mix"], sc_m], _pre_norm_bwd_rows, [wide], [(8, D_MODEL)], cn=512)

    small = {
        "mod": jnp.concatenate([acc_p[1], acc_p[0], acc_r[3], acc_r[1], acc_r[0], acc_l[0]]),
        "g_pre_mix": acc_p[2], "g_post_mix": acc_r[4], "b_if": acc_g[0, :8],
        "conv_w": acc_c[:CONV_WIDTH].reshape(-1), "conv_b": acc_c[CONV_WIDTH],
        "sinks": dsink[:, 0], "norm_w": acc_n[0], "g_pre_ffn": acc_r[2], "g_post_ffn": acc_l[1]}
    return loss, dx, small


IN_WIDTH = sum(n for _, n in W_IN_PIECES)
IN_SHARD = IN_WIDTH // 4
IN_SHARD_PAD = -(-IN_SHARD // 32) * 32


def _split_w_in(w_in_t):
    out, off, start = {}, 0, {}
    for k, n in W_IN_PIECES:
        out[k], start[k] = w_in_t[off:off + n], off
        off += n
    out["if"] = jnp.pad(out["if"], ((0, 120), (0, 0)))
    for name, first, last in (("q+kv", "q", "kv"), ("mv+mo", "mv", "mo"), ("ga+gm", "ga", "gm")):
        out[name] = w_in_t[start[first]:start[last] + out[last].shape[0]]
    return out


def _halves(a):
    return a.reshape(4, 2, a.shape[0] // 8, a.shape[1])


SMALL = (("b_ada", 6144), ("g_pre_mix", 1024), ("g_post_mix", 1024), ("b_if", 128), ("conv_w", 4096),
         ("conv_b", 1024), ("sinks", 128), ("norm_w", 512), ("g_pre_ffn", 1024), ("g_post_ffn", 1024))
SMALL_LEN = 8 * 2048


def _pack_small(vals):
    parts = []
    for k, n in SMALL:
        v = vals[k].reshape(-1)
        parts.append(jnp.pad(v, (0, n - v.shape[0])))
    flat = jnp.concatenate(parts)
    return jnp.pad(flat, (0, SMALL_LEN - flat.shape[0]))


def _unpack_small(flat, shapes):
    out, off = {}, 0
    for k, n in SMALL:
        size = 1
        for d in shapes[k]:
            size *= d
        out[k] = flat[off:off + size].reshape(shapes[k])
        off += n
    return out


def kernel(x, c, positions, w_ada, b_ada, g_pre_mix, g_post_mix, w_in, b_if, conv_w, conv_b, attn_sinks, mlstm_norm_w, w_branch_attn, w_branch_mlstm, w_out, g_pre_ffn, g_post_ffn, w_ffn_gate, w_ffn_up, w_ffn_down, loss_target, m_w_ada, m_b_ada, m_g_pre_mix, m_g_post_mix, m_w_in, m_b_if, m_conv_w, m_conv_b, m_attn_sinks, m_mlstm_norm_w, m_w_branch_attn, m_w_branch_mlstm, m_w_out, m_g_pre_ffn, m_g_post_ffn, m_w_ffn_gate, m_w_ffn_up, m_w_ffn_down, v_w_ada, v_b_ada, v_g_pre_mix, v_g_post_mix, v_w_in, v_b_if, v_conv_w, v_conv_b, v_attn_sinks, v_mlstm_norm_w, v_w_branch_attn, v_w_branch_mlstm, v_w_out, v_g_pre_ffn, v_g_post_ffn, v_w_ffn_gate, v_w_ffn_up, v_w_ffn_down):
    xi, yi, ci = _place()
    chip = 2 * xi + yi
    dev = 2 * chip + ci
    T = x.shape[1]
    ada_cols = w_ada.shape[2]

    place = jnp.stack([chip, ci]).astype(jnp.int32)

    def my_half(a):
        n = a.shape[0] // 2
        return lax.dynamic_slice_in_dim(a, ci * n, n, axis=0).astype(BF16)

    blk = jnp.concatenate([c.reshape(-1), conv_w.reshape(-1)]).reshape(8, 256)
    got = _all_gather8("gather_cond", blk, pltpu.VMEM).reshape(8, 2048)
    c_all = got[:, :D_MODEL].astype(BF16)
    conv_full = got[::2, D_MODEL:].reshape(4, CONV_WIDTH, -1).transpose(1, 0, 2).reshape(CONV_WIDTH, -1)

    b_sh = lax.dynamic_slice_in_dim(b_ada, chip * ada_cols, ada_cols, axis=1)
    mod_part, = _mm("ada_mod", [[(c_all, w_ada[0].astype(BF16))]], [b_sh],
                    lambda ps, es: (ps[0] + es[0],), [F32], cn=512, tm=8)
    mod_all = _all_gather8("gather_mod", mod_part, pltpu.VMEM).reshape(4, 2, 8, ada_cols)[:, 0]
    mod = lax.dynamic_index_in_dim(mod_all, dev, axis=1, keepdims=False).reshape(6, 1, D_MODEL)

    def gather_start(name, blks, after, sends, copies):
        return _split_start(name + "_start", blks, _place_blocks(name + "_place", blks, place),
                            sends, copies, after)

    w_in_t = jnp.pad(w_in[0].T, ((0, IN_SHARD_PAD - IN_SHARD), (0, 0)))
    in_started = gather_start("in_gather", [my_half(w_in_t)], mod, _gather_sends, 4)
    late_keys = ("fg", "fu", "fd", "out", "ba", "bm")
    late_started = gather_start(
        "late_gather",
        [my_half(w_ffn_gate[0].T), my_half(w_ffn_up[0].T), my_half(w_ffn_down[0]), my_half(w_out[0]),
         my_half(w_branch_attn[0].T), my_half(w_branch_mlstm[0].T)], in_started[4], _gather_sends_all, 7)
    mod = mod + (in_started[4][0, 0] + late_started[4][0, 0])

    def in_weights(after):
        g_in, = _forward_sibling("in_gather_forward",
                                 _split_wait("in_gather_wait", in_started, after, _gather_lands, 4))
        return _split_w_in(g_in.reshape(4, IN_SHARD_PAD, D_MODEL)[:, :IN_SHARD].reshape(IN_WIDTH, D_MODEL))

    def late_weights(after):
        lands = _split_wait("late_gather_wait", late_started, after, _gather_lands_all, 7)
        return {k: a.reshape(-1, a.shape[-1]) for k, a in zip(late_keys, lands)}

    sent = {}

    def scatter_start(name, groups):
        pairs, lands = _pair_sum(name + "_pair_sum", groups, _swap_halves_sibling(name + "_pair", groups), place)
        sent[name] = _split_start(name + "_start", pairs, lands, _scatter_sends, 3, pairs[0])
        return sent[name][4][0, 0]

    def ffn_grads(g_fg, g_fu, g_fd):
        return scatter_start("rs_ffn", [_halves(g_fg), _halves(g_fu), _halves(g_fd)])

    def mixer_grads(G):
        g_in_t = jnp.concatenate([G[k][:n] for k, n in W_IN_PIECES]).reshape(4, IN_SHARD, D_MODEL)
        g_in_t = jnp.pad(g_in_t, ((0, 0), (0, IN_SHARD_PAD - IN_SHARD), (0, 0)))
        return scatter_start("rs_mix", [g_in_t.reshape(4, 2, IN_SHARD_PAD // 2, D_MODEL), _halves(G["out"]),
                                        _halves(G["ba"]), _halves(G["bm"])])

    sp = {"g_pre_mix": g_pre_mix, "g_post_mix": g_post_mix, "b_if": b_if, "conv_w": conv_full,
          "conv_b": conv_b, "sinks": attn_sinks, "norm_w": mlstm_norm_w, "g_pre_ffn": g_pre_ffn,
          "g_post_ffn": g_post_ffn}
    loss, dx, small = _local_step(x[0], loss_target[0], positions.reshape(T, 1), [mod[i] for i in range(6)],
                                  sp, in_weights, late_weights, ffn_grads, mixer_grads)

    reds = (_sum4("rs_ffn_chip_sum", _split_wait("rs_ffn_wait", sent["rs_ffn"], dx, _scatter_lands, 3), place)
            + _sum4("rs_mix_chip_sum", _split_wait("rs_mix_wait", sent["rs_mix"], dx, _scatter_lands, 3), place))
    gsh = {k: s.reshape(-1, s.shape[-1])
           for k, s in zip(("fg", "fu", "fd", "w_in", "out", "ba", "bm"), _share_halves("rs_share", reds))}
    gsh["w_in"] = gsh["w_in"][:IN_SHARD]

    small["b_ada"] = small.pop("mod")
    vec = _pack_small(small).reshape(8, 2048)
    g_all = _all_gather8("gather_small", vec, pltpu.VMEM).reshape(8, SMALL_LEN)
    dmod_sh = lax.dynamic_slice_in_dim(g_all[:, :6 * D_MODEL], chip * ada_cols, ada_cols, axis=1)
    g_w_ada, = _mm_tn_group("dw_ada", [c_all], dmod_sh.astype(BF16), F32)

    smalls = {"b_ada": (b_ada, m_b_ada, v_b_ada), "g_pre_mix": (g_pre_mix, m_g_pre_mix, v_g_pre_mix),
              "g_post_mix": (g_post_mix, m_g_post_mix, v_g_post_mix), "b_if": (b_if, m_b_if, v_b_if),
              "conv_w": None, "conv_b": (conv_b, m_conv_b, v_conv_b),
              "sinks": (attn_sinks, m_attn_sinks, v_attn_sinks),
              "norm_w": (mlstm_norm_w, m_mlstm_norm_w, v_mlstm_norm_w),
              "g_pre_ffn": (g_pre_ffn, m_g_pre_ffn, v_g_pre_ffn),
              "g_post_ffn": (g_post_ffn, m_g_post_ffn, v_g_post_ffn)}
    shapes = {k: (t[0].shape if t is not None else (1, CONV_WIDTH, D_MODEL)) for k, t in smalls.items()}
    zeros = jnp.zeros((CONV_WIDTH * D_MODEL,), F32)
    packs = [_pack_small({k: (t[i] if t is not None else zeros) for k, t in smalls.items()}).reshape(1, -1)
             for i in range(3)]
    s_out = [_unpack_small(o[0], shapes) for o in _small_update(g_all, *packs)]
    g_conv = lax.dynamic_slice_in_dim(s_out[0]["conv_w"], chip * conv_w.shape[2], conv_w.shape[2], axis=2)

    res = {}
    for k, t in smalls.items():
        if t is not None:
            res[k] = tuple(o[k] for o in s_out)
    res["conv_w"] = (g_conv, *[o[None] for o in _adamw("adam_conv_w", conv_w[0], g_conv[0], m_conv_w[0], v_conv_w[0])])
    res["w_ada"] = (g_w_ada[None], *[o[None] for o in _adamw("adam_w_ada", w_ada[0], g_w_ada, m_w_ada[0], v_w_ada[0])])
    bigs = {"w_in": (w_in, m_w_in, v_w_in), "ba": (w_branch_attn, m_w_branch_attn, v_w_branch_attn),
            "bm": (w_branch_mlstm, m_w_branch_mlstm, v_w_branch_mlstm), "out": (w_out, m_w_out, v_w_out),
            "fg": (w_ffn_gate, m_w_ffn_gate, v_w_ffn_gate), "fu": (w_ffn_up, m_w_ffn_up, v_w_ffn_up),
            "fd": (w_ffn_down, m_w_ffn_down, v_w_ffn_down)}
    for k, (w, m, v) in bigs.items():
        if k in ("w_in", "fg", "fu"):
            res[k] = tuple(o.T[None] for o in (gsh[k], *_adamw("adam_" + k, w[0].T, gsh[k], m[0].T, v[0].T)))
        else:
            g = gsh[k].T if k in ("ba", "bm") else gsh[k]
            res[k] = (g[None], *[o[None] for o in _adamw("adam_" + k, w[0], g, m[0], v[0])])

    order = ("w_ada", "b_ada", "g_pre_mix", "g_post_mix", "w_in", "b_if", "conv_w", "conv_b", "sinks",
             "norm_w", "ba", "bm", "out", "g_pre_ffn", "g_post_ffn", "fg", "fu", "fd")
    total = lax.psum(loss[0, 0], ("x", "y", "c"))
    return (total, dx[None], *[res[k][0] for k in order], *[res[k][1] for k in order],
            *[res[k][2] for k in order], *[res[k][3] for k in order])
```

```python
import functools

import jax
import jax.numpy as jnp
from jax import lax
from jax.experimental import pallas as pl
from jax.experimental.pallas import tpu as pltpu

F32, BF16 = jnp.float32, jnp.bfloat16
MESH = pl.DeviceIdType.MESH

D_MODEL = 1024
N_Q_HEADS, N_KV_HEADS, HEAD_DIM, WINDOW = 8, 2, 64, 128
ROPE_THETA = 10000.0
MLSTM_HEADS, MLSTM_HEAD_DIM, MLSTM_CHUNK, CONV_WIDTH = 4, 128, 64, 4
D_FF = 2816
NORM_EPS = 1e-6
ADAM_LR, ADAM_B1, ADAM_B2, ADAM_EPS, ADAM_WD, ADAM_STEP = 0.001, 0.9, 0.999, 1e-08, 0.01, 10

VMEM_LIMIT = 56 * 1024 * 1024
ROW_TILE = 256
MM_TM = 512
MM_TT = 1024
ATTN_BLK = WINDOW
STEP_ROWS = 2 * MLSTM_CHUNK
NEG_INF = float("-inf")


def _params(sem):
    return pltpu.CompilerParams(dimension_semantics=sem, vmem_limit_bytes=VMEM_LIMIT)


def _sds(shape, dtype):
    return jax.ShapeDtypeStruct(shape, dtype)


def _sigmoid(x):
    return 1.0 / (1.0 + jnp.exp(-x))


def _dot(a, b, ca, cb):
    return lax.dot_general(a, b, (((ca,), (cb,)), ((), ())), preferred_element_type=F32)


def _bdot(a, b, ca, cb):
    return lax.dot_general(a, b, (((ca,), (cb,)), ((0,), (0,))), preferred_element_type=F32)


def _bdot_rows(a, b):
    return jnp.stack([_dot(a[h], b[h], 0, 0) for h in range(a.shape[0])])


def _mm(name, prods, extras, epi, out_dtypes, cn, nt=False, tm=MM_TM):
    flat = [ab for p in prods for ab in p]
    counts = [len(p) for p in prods]
    M = flat[0][0].shape[0]
    N = flat[0][1].shape[0 if nt else 1]
    tm = min(tm, M)
    n_in = 2 * len(flat) + len(extras)

    def body(*refs):
        ins, outs = refs[:n_in], refs[n_in:]
        for j in range(N // cn):
            cols = slice(j * cn, (j + 1) * cn)
            k, ps = 0, []
            for cnt in counts:
                acc = None
                for _ in range(cnt):
                    b = ins[k + 1][cols, :] if nt else ins[k + 1][:, cols]
                    d = _dot(ins[k][...], b, 1, 1 if nt else 0)
                    acc = d if acc is None else acc + d
                    k += 2
                ps.append(acc)
            res = epi(ps, [r[:, cols] for r in ins[k:]])
            for o, r in zip(outs, res):
                o[:, cols] = r.astype(o.dtype)

    in_specs, args = [], []
    for a, b in flat:
        in_specs.append(pl.BlockSpec((tm, a.shape[1]), lambda i: (i, 0)))
        in_specs.append(pl.BlockSpec(b.shape, lambda i: (0, 0), pipeline_mode=pl.Buffered(1)))
        args += [a, b]
    for e in extras:
        e, off = e if isinstance(e, tuple) else (e, 0)
        rows = 1 if e.shape[0] == 1 else tm
        in_specs.append(pl.BlockSpec((rows, N), lambda i, off=off, rows=rows: (0 if rows == 1 else i, off)))
        args.append(e)
    return pl.pallas_call(
        body, name=name, grid=(M // tm,), in_specs=in_specs,
        out_specs=[pl.BlockSpec((tm, N), lambda i: (i, 0)) for _ in out_dtypes],
        out_shape=[_sds((M, N), dt) for dt in out_dtypes],
        compiler_params=_params(("parallel",)))(*args)


def _mm_rows(name, prods, extras, epi, outs, accs, cn, nt=False, tm=MM_TM):
    flat = [ab for p in prods for ab in p]
    counts = [len(p) for p in prods]
    M = flat[0][0].shape[0]
    N = flat[0][1].shape[0 if nt else 1]
    tm = min(tm, M)
    n_mm, n_in, n_out = 2 * len(flat), 2 * len(flat) + len(extras), len(outs)

    def body(*refs):
        ins, out_refs, acc_refs = refs[:n_in], refs[n_in:n_in + n_out], refs[n_in + n_out:]

        @pl.when(pl.program_id(0) == 0)
        def _():
            for a in acc_refs:
                a[...] = jnp.zeros_like(a)

        chunks = [[] for _ in counts]
        for j in range(N // cn):
            cols = slice(j * cn, (j + 1) * cn)
            k = 0
            for p, cnt in enumerate(counts):
                acc = None
                for _ in range(cnt):
                    b = ins[k + 1][cols, :] if nt else ins[k + 1][:, cols]
                    d = _dot(ins[k][...], b, 1, 1 if nt else 0)
                    acc = d if acc is None else acc + d
                    k += 2
                chunks[p].append(acc)
        ps = [c[0] if len(c) == 1 else jnp.concatenate(c, axis=1) for c in chunks]
        res, incs = epi(ps, [r[...] for r in ins[n_mm:]])
        for o, r in zip(out_refs, res):
            o[...] = r.astype(o.dtype)
        for a, inc in zip(acc_refs, incs):
            a[...] += inc

    in_specs, args = [], []
    for a, b in flat:
        in_specs.append(pl.BlockSpec((tm, a.shape[1]), lambda i: (i, 0)))
        in_specs.append(pl.BlockSpec(b.shape, lambda i: (0, 0), pipeline_mode=pl.Buffered(1)))
        args += [a, b]
    for e in extras:
        rows = 1 if e.shape[0] == 1 else tm
        in_specs.append(pl.BlockSpec((rows, e.shape[1]), lambda i, rows=rows: (0 if rows == 1 else i, 0)))
        args.append(e)
    return pl.pallas_call(
        body, name=name, grid=(M // tm,), in_specs=in_specs,
        out_specs=[pl.BlockSpec((tm, w), lambda i: (i, 0)) for w, _ in outs]
        + [pl.BlockSpec(s, lambda i: (0, 0)) for s in accs],
        out_shape=[_sds((M, w), dt) for w, dt in outs] + [_sds(s, F32) for s in accs],
        compiler_params=_params(("arbitrary",)))(*args)


def _mm_tn_group(name, pieces, b, out_dtype, tt=MM_TT):
    T, N = b.shape
    tt = min(tt, T)
    steps, n = T // tt, len(pieces)

    def body(*refs):
        a_refs, b_ref, out_refs, accs = refs[:n], refs[n], refs[n + 1:2 * n + 1], refs[2 * n + 1:]
        t = pl.program_id(0)

        @pl.when(t == 0)
        def _():
            for acc in accs:
                acc[...] = jnp.zeros_like(acc)

        for a_ref, acc in zip(a_refs, accs):
            acc[...] += _dot(a_ref[...], b_ref[...], 0, 0)

        @pl.when(t == steps - 1)
        def _():
            for o_ref, acc in zip(out_refs, accs):
                o_ref[...] = acc[...].astype(o_ref.dtype)

    return pl.pallas_call(
        body, name=name, grid=(steps,),
        in_specs=[pl.BlockSpec((tt, a.shape[1]), lambda t: (t, 0)) for a in pieces]
        + [pl.BlockSpec((tt, N), lambda t: (t, 0))],
        out_specs=[pl.BlockSpec((a.shape[1], N), lambda t: (0, 0)) for a in pieces],
        out_shape=[_sds((a.shape[1], N), out_dtype) for a in pieces],
        scratch_shapes=[pltpu.VMEM((a.shape[1], N), F32) for a in pieces],
        compiler_params=_params(("arbitrary",)))(*pieces, b)


def _first(ps, es):
    return (ps[0],)


def _rows(name, body, ins, out_shapes, T, tr=ROW_TILE):
    tr = min(tr, T)

    def spec(shape):
        if shape[0] == T:
            return pl.BlockSpec((tr,) + tuple(shape[1:]), lambda i: (i,) + (0,) * (len(shape) - 1))
        return pl.BlockSpec(tuple(shape), lambda i: (0,) * len(shape))

    return pl.pallas_call(
        body, name=name, grid=(T // tr,),
        in_specs=[spec(a.shape) for a in ins], out_specs=[spec(s.shape) for s in out_shapes],
        out_shape=out_shapes, compiler_params=_params(("arbitrary",)))(*ins)


def _rms(x):
    r = lax.rsqrt(jnp.mean(x * x, axis=-1, keepdims=True) + NORM_EPS)
    return x * r, r


def _rms_bwd(dxn, xn, r):
    return r * (dxn - xn * jnp.mean(dxn * xn, axis=-1, keepdims=True))


def _colsum(v):
    return jnp.sum(v, axis=0, keepdims=True)


def _proj_in(x, g, sc, sh, groups):
    T = x.shape[0]
    tm = min(MM_TM, T)
    ng = len(groups)

    def body(x_ref, g_ref, sc_ref, sh_ref, *rest):
        w_refs, h_ref, out_refs = rest[:ng], rest[ng], rest[ng + 1:]
        xn, _ = _rms(x_ref[...])
        h = (xn * g_ref[...] * (1.0 + sc_ref[...]) + sh_ref[...]).astype(BF16)
        h_ref[...] = h
        for w_ref, o_ref, (w, _, cn) in zip(w_refs, out_refs, groups):
            for j in range(w.shape[0] // cn):
                cols = slice(j * cn, (j + 1) * cn)
                o_ref[:, cols] = _dot(h, w_ref[cols, :], 1, 1).astype(o_ref.dtype)

    row = pl.BlockSpec((1, D_MODEL), lambda i: (0, 0))
    tile = lambda w: pl.BlockSpec((tm, w), lambda i: (i, 0))
    return pl.pallas_call(
        body, name="proj_in", grid=(T // tm,),
        in_specs=[tile(D_MODEL), row, row, row] + [
            pl.BlockSpec(w.shape, lambda i: (0, 0), pipeline_mode=pl.Buffered(1)) for w, _, _ in groups],
        out_specs=[tile(D_MODEL)] + [tile(w.shape[0]) for w, _, _ in groups],
        out_shape=[_sds((T, D_MODEL), BF16)] + [_sds((T, w.shape[0]), dt) for w, dt, _ in groups],
        compiler_params=_params(("parallel",)))(x, g, sc, sh, *[w for w, _, _ in groups])


def _acc_rows(rows):
    w = rows[0].shape[1]
    return jnp.concatenate(rows + [jnp.zeros((8 - len(rows), w), F32)], axis=0)


def _res_norm_rows(ps, es):
    mix = ps[0]
    x, gate, gp, g2, sc, sh = es
    mh, _ = _rms(mix)
    x1 = x + gate * (mh * gp)
    xn, _ = _rms(x1)
    return [mix, x1, xn * g2 * (1.0 + sc) + sh], []


def _final_loss_rows(ps, es):
    x1, tgt, gate, gp = es
    fh, r = _rms(ps[0])
    e = x1 + gate * (fh * gp) - tgt
    loss = 0.5 * jnp.sum(jnp.mean(e * e, axis=-1, keepdims=True))
    dy = e * (1.0 / D_MODEL)
    acc = _acc_rows([_colsum(dy * fh * gp), _colsum(dy * gate * fh)])
    return [dy, _rms_bwd(dy * gate * gp, fh, r)], [acc, jnp.full((1, 128), loss, F32)]


def _res_norm_bwd_rows(ps, es):
    dh = ps[0]
    x1, mix, dy, sc, gate, g2, gp = es
    xn, r1 = _rms(x1)
    rows = [_colsum(dh * xn * g2), _colsum(dh), _colsum(dh * (1.0 + sc) * xn)]
    dx1 = dy + _rms_bwd(dh * (1.0 + sc) * g2, xn, r1)
    mh, rm = _rms(mix)
    rows += [_colsum(dx1 * mh * gp), _colsum(dx1 * gate * mh)]
    return [dx1, _rms_bwd(dx1 * gate * gp, mh, rm)], [_acc_rows(rows)]


def _pre_norm_bwd_rows(ps, es):
    dh = ps[0]
    x, dx1, g, sc = es
    xn, r = _rms(x)
    rows = [_colsum(dh * xn * g), _colsum(dh), _colsum(dh * (1.0 + sc) * xn)]
    return [dx1 + _rms_bwd(dh * (1.0 + sc) * g, xn, r)], [_acc_rows(rows)]


def _rope_tables(pos_col, inv_freq):
    T = pos_col.shape[0]

    def body(p_ref, f_ref, c_ref, s_ref):
        ang = p_ref[...].astype(F32) * f_ref[...]
        lane = lax.broadcasted_iota(jnp.int32, ang.shape, 1)
        c_ref[...] = jnp.cos(ang)
        s_ref[...] = jnp.where(lane % HEAD_DIM < HEAD_DIM // 2, -1.0, 1.0) * jnp.sin(ang)

    return _rows("rope_tables", body, [pos_col, inv_freq],
                 [_sds((T, 128), F32), _sds((T, 128), F32)], T, tr=512)


def _swap_halves(t):
    W = t.shape[1]
    lane = lax.broadcasted_iota(jnp.int32, t.shape, 1)
    half = HEAD_DIM // 2
    return jnp.where(lane % HEAD_DIM < half, pltpu.roll(t, W - half, 1), pltpu.roll(t, half, 1))


def _widen(c, W):
    return c if W == 128 else jnp.concatenate([c] * (W // 128), axis=1)


def _rope(t, c, s):
    W = t.shape[1]
    return t * _widen(c, W) + _swap_halves(t) * _widen(s, W)


def _unrope(dy, c, s):
    W = dy.shape[1]
    return dy * _widen(c, W) + _swap_halves(dy * _widen(s, W))


def _attn_mask(n):
    qi = lax.broadcasted_iota(jnp.int32, (ATTN_BLK, 2 * ATTN_BLK), 0)
    kj = lax.broadcasted_iota(jnp.int32, (ATTN_BLK, 2 * ATTN_BLK), 1)
    rel = kj - ATTN_BLK
    return (rel <= qi) & (qi - rel < WINDOW) & ((n > 0) | (kj >= ATTN_BLK))


def _attn_load(cur, prv, cc, sc, cp, sp):
    x, xp = cur[...], prv[...]
    q = _rope(x[:, :512], cc[...], sc[...]) * (HEAD_DIM ** -0.5)
    k = jnp.concatenate([_rope(xp[:, 512:640], cp[...], sp[...]),
                         _rope(x[:, 512:640], cc[...], sc[...])], axis=0)
    v = jnp.concatenate([xp[:, 640:768], x[:, 640:768]], axis=0)
    return q, k, v


ROLLED = tuple(h for h in range(N_Q_HEADS) if h % 2 != h // (N_Q_HEADS // N_KV_HEADS))


def _pair_heads(t):
    half = lax.broadcasted_iota(jnp.int32, (ATTN_BLK, 128), 1) // HEAD_DIM
    return jnp.stack([jnp.where(half == h % 2, t[:, 128 * (h // 2):128 * (h // 2) + 128], 0.0)
                      for h in range(N_Q_HEADS)])


def _kv_heads(t):
    half = lax.broadcasted_iota(jnp.int32, t.shape, 1) // HEAD_DIM
    tr = pltpu.roll(t, HEAD_DIM, 1)
    return jnp.stack([jnp.where(half == h % 2, tr if h in ROLLED else t, 0.0)
                      for h in range(N_Q_HEADS)])


def _sink_column(snk):
    return jnp.stack([jnp.full((1, 1), snk[0, h], F32) for h in range(N_Q_HEADS)])


def _attn_probs(qh, kh, mask, sink):
    s = jnp.where(mask, _bdot(qh, kh, 2, 2), NEG_INF)
    m = jnp.maximum(jnp.max(s, axis=-1, keepdims=True), sink)
    p = jnp.exp(s - m)
    es = jnp.exp(sink - m)
    rl = 1.0 / (jnp.sum(p, axis=-1, keepdims=True) + es)
    return p, es, rl


def _attn_specs(order):
    blk = lambda w: pl.BlockSpec((ATTN_BLK, w), lambda s: (order(s), 0))
    prv = lambda w: pl.BlockSpec((ATTN_BLK, w), lambda s: (jnp.maximum(order(s) - 1, 0), 0))
    return [blk(768), prv(768), blk(128), blk(128), prv(128), prv(128),
            pl.BlockSpec(memory_space=pltpu.SMEM)]


def _attn_fwd(pa, cos, sin, sinks):
    T = pa.shape[0]
    nb = T // ATTN_BLK

    def body(cur, prv, cc, sc, cp, sp, snk, y_ref):
        n = pl.program_id(0)
        q, k, v = _attn_load(cur, prv, cc, sc, cp, sp)
        qh, kh, vh = _pair_heads(q).astype(BF16), _kv_heads(k).astype(BF16), _kv_heads(v).astype(BF16)
        p, _, rl = _attn_probs(qh, kh, _attn_mask(n), _sink_column(snk))
        o = _bdot(p.astype(BF16), vh, 2, 1) * rl
        for pair in range(N_Q_HEADS // 2):
            y_ref[:, 128 * pair:128 * pair + 128] = (o[2 * pair] + o[2 * pair + 1]).astype(BF16)

    return pl.pallas_call(
        body, name="attn_fwd", grid=(nb,), in_specs=_attn_specs(lambda s: s),
        out_specs=pl.BlockSpec((ATTN_BLK, 512), lambda n: (n, 0)),
        out_shape=_sds((T, 512), BF16), compiler_params=_params(("parallel",)))(
            pa, pa, cos, sin, cos, sin, sinks)


def _attn_bwd(pa, cos, sin, sinks, dy):
    T = pa.shape[0]
    nb = T // ATTN_BLK
    rev = lambda s: nb - 1 - s

    def body(cur, prv, cc, sc, cp, sp, snk, dy_ref, dq_ref, dkv_ref, dsink_ref, carry):
        n = rev(pl.program_id(0))

        @pl.when(pl.program_id(0) == 0)
        def _():
            dsink_ref[...] = jnp.zeros_like(dsink_ref)
            carry[...] = jnp.zeros_like(carry)

        q, k, v = _attn_load(cur, prv, cc, sc, cp, sp)
        qh, kh, vh = _pair_heads(q).astype(BF16), _kv_heads(k).astype(BF16), _kv_heads(v).astype(BF16)
        p, es, rl = _attn_probs(qh, kh, _attn_mask(n), _sink_column(snk))
        pn = p * rl
        do = _pair_heads(dy_ref[...]).astype(BF16)
        dp = _bdot(do, vh, 2, 2)
        delta = jnp.sum(pn * dp, axis=-1, keepdims=True)
        ds = (pn * (dp - delta)).astype(BF16)
        dsink = es * rl * delta
        dq = _bdot(ds, kh, 2, 1) * (HEAD_DIM ** -0.5)
        dkh = _bdot_rows(ds, qh)
        dvh = _bdot_rows(pn.astype(BF16), do)

        def fold(t):
            same = [t[h] for h in range(N_Q_HEADS) if h not in ROLLED]
            moved = [t[h] for h in ROLLED]
            return sum(same[1:], same[0]) + pltpu.roll(sum(moved[1:], moved[0]), HEAD_DIM, 1)

        dk, dv = fold(dkh), fold(dvh)
        for h in range(N_Q_HEADS):
            dsink_ref[h:h + 1, :] += -jnp.sum(dsink[h])
        for pair in range(N_Q_HEADS // 2):
            dq_ref[:, 128 * pair:128 * pair + 128] = _unrope(
                dq[2 * pair] + dq[2 * pair + 1], cc[...], sc[...]).astype(BF16)
        dkv_ref[:, 0:128] = _unrope(dk[ATTN_BLK:] + carry[:, 0:128], cc[...], sc[...]).astype(BF16)
        dkv_ref[:, 128:256] = (dv[ATTN_BLK:] + carry[:, 128:256]).astype(BF16)
        carry[:, 0:128] = dk[:ATTN_BLK]
        carry[:, 128:256] = dv[:ATTN_BLK]

    blk = lambda w: pl.BlockSpec((ATTN_BLK, w), lambda s: (rev(s), 0))
    return pl.pallas_call(
        body, name="attn_bwd", grid=(nb,), in_specs=_attn_specs(rev) + [blk(512)],
        out_specs=[blk(512), blk(256), pl.BlockSpec((8, 128), lambda s: (0, 0))],
        out_shape=[_sds((T, 512), BF16), _sds((T, 256), BF16), _sds((8, 128), F32)],
        scratch_shapes=[pltpu.VMEM((ATTN_BLK, 256), F32)],
        compiler_params=_params(("arbitrary",)))(pa, pa, cos, sin, cos, sin, sinks, dy)


CONV_COLS = 2 * MLSTM_HEADS * MLSTM_HEAD_DIM


def _conv_pre(cur_ref, halo_ref, w_ref, b_ref, i, tr):
    xx = jnp.concatenate([jnp.where(i > 0, halo_ref[...], 0.0), cur_ref[...]], axis=0)
    taps = [(pltpu.roll(xx, CONV_WIDTH - 1 - j, 0) if j < CONV_WIDTH - 1 else xx)[8:8 + tr]
            for j in range(CONV_WIDTH)]
    pre = b_ref[...]
    for j in range(CONV_WIDTH):
        pre = pre + taps[j] * w_ref[j:j + 1, :]
    return pre, taps


def _conv_specs(T, tr):
    return [pl.BlockSpec((tr, CONV_COLS), lambda i: (i, 0)),
            pl.BlockSpec((8, CONV_COLS), lambda i: (jnp.maximum(i * (tr // 8) - 1, 0), 0)),
            pl.BlockSpec((CONV_WIDTH, CONV_COLS), lambda i: (0, 0)),
            pl.BlockSpec((1, CONV_COLS), lambda i: (0, 0))]


def _conv_fwd(pm, w, b):
    T = pm.shape[0]
    tr = min(ROW_TILE, T)

    def body(cur_ref, halo_ref, w_ref, b_ref, o_ref):
        pre, _ = _conv_pre(cur_ref, halo_ref, w_ref, b_ref, pl.program_id(0), tr)
        o_ref[...] = pre * _sigmoid(pre)

    return pl.pallas_call(
        body, name="conv_fwd", grid=(T // tr,), in_specs=_conv_specs(T, tr),
        out_specs=pl.BlockSpec((tr, CONV_COLS), lambda i: (i, 0)),
        out_shape=_sds((T, CONV_COLS), F32), compiler_params=_params(("parallel",)))(pm, pm, w, b)


def _conv_bwd(pqk, w, b, dqk):
    T = pqk.shape[0]
    tr = min(ROW_TILE, T)
    nt = T // tr

    def body(cur_ref, prev_ref, next_ref, w_ref, b_ref, d_ref, dnext_ref, du_ref, acc_ref):
        i = pl.program_id(0)

        @pl.when(i == 0)
        def _():
            acc_ref[...] = jnp.zeros_like(acc_ref)

        last = i == nt - 1
        xx = jnp.concatenate([jnp.where(i > 0, prev_ref[...], 0.0), cur_ref[...],
                              jnp.where(last, 0.0, next_ref[...])], axis=0)
        taps = [(pltpu.roll(xx, CONV_WIDTH - 1 - j, 0) if j < CONV_WIDTH - 1 else xx)[8:16 + tr]
                for j in range(CONV_WIDTH)]
        pre = b_ref[...]
        for j in range(CONV_WIDTH):
            pre = pre + taps[j] * w_ref[j:j + 1, :]
        sg = _sigmoid(pre)
        dd = jnp.concatenate([d_ref[...], jnp.where(last, 0.0, dnext_ref[...])], axis=0)
        dpre = dd * (sg * (1.0 + pre * (1.0 - sg)))
        for j in range(CONV_WIDTH):
            acc_ref[j:j + 1, :] += _colsum(dpre[:tr] * taps[j][:tr])
        acc_ref[CONV_WIDTH:CONV_WIDTH + 1, :] += _colsum(dpre[:tr])
        du = dpre[:tr] * w_ref[CONV_WIDTH - 1:CONV_WIDTH, :]
        for j in range(CONV_WIDTH - 1):
            k = CONV_WIDTH - 1 - j
            du = du + pltpu.roll(dpre, tr + 8 - k, 0)[:tr] * w_ref[j:j + 1, :]
        du_ref[...] = du.astype(BF16)

    tile = pl.BlockSpec((tr, CONV_COLS), lambda i: (i, 0))
    after = pl.BlockSpec((8, CONV_COLS), lambda i: (jnp.minimum((i + 1) * (tr // 8), T // 8 - 1), 0))
    before = pl.BlockSpec((8, CONV_COLS), lambda i: (jnp.maximum(i * (tr // 8) - 1, 0), 0))
    return pl.pallas_call(
        body, name="conv_bwd", grid=(nt,),
        in_specs=[tile, before, after, pl.BlockSpec((CONV_WIDTH, CONV_COLS), lambda i: (0, 0)),
                  pl.BlockSpec((1, CONV_COLS), lambda i: (0, 0)), tile, after],
        out_specs=[tile, pl.BlockSpec((8, CONV_COLS), lambda i: (0, 0))],
        out_shape=[_sds((T, CONV_COLS), BF16), _sds((8, CONV_COLS), F32)],
        compiler_params=_params(("arbitrary",)))(pqk, pqk, pqk, w, b, dqk, dqk)


def _log_sigmoid(x):
    return jnp.minimum(x, 0.0) - jnp.log1p(jnp.exp(-jnp.abs(x)))


def _chunk_cumsum(x, axis):
    idx = lax.broadcasted_iota(jnp.int32, x.shape, axis) % MLSTM_CHUNK
    k = 1
    while k < MLSTM_CHUNK:
        x = x + jnp.where(idx >= k, pltpu.roll(x, k, axis), 0.0)
        k *= 2
    return x


def _chunk_rev_cumsum(x, axis):
    n = x.shape[axis]
    idx = lax.broadcasted_iota(jnp.int32, x.shape, axis) % MLSTM_CHUNK
    k = 1
    while k < MLSTM_CHUNK:
        x = x + jnp.where(idx < MLSTM_CHUNK - k, pltpu.roll(x, n - k, axis), 0.0)
        k *= 2
    return x


def _mlstm_gates(gc_ref, bc_ref, gr_ref, br_ref):
    gc = gc_ref[...] + bc_ref[...]
    gr = gr_ref[...] + br_ref[...]
    return gc, _chunk_cumsum(_log_sigmoid(gc), 0), gr, _chunk_cumsum(_log_sigmoid(gr), 1)


def _heads(ref, base=0):
    D = MLSTM_HEAD_DIM
    return jnp.stack([ref[:, base + D * h:base + D * h + D] for h in range(MLSTM_HEADS)])


def _mlstm_inputs(q_ref, k_ref, v_ref, gc, bc, gr, br):
    H = MLSTM_HEADS
    q, v = _heads(q_ref), _heads(v_ref)
    ks = _heads(k_ref) * (MLSTM_HEAD_DIM ** -0.5)
    return dict(
        q=q, ks=ks, qb=q.astype(BF16), kb=ks.astype(BF16), vb=v.astype(BF16),
        b_col=jnp.stack([bc[:, H + h:H + h + 1] for h in range(H)]),
        i_col=jnp.stack([gc[:, h:h + 1] for h in range(H)]),
        b_row=jnp.stack([br[H + h:H + h + 1, :] for h in range(H)]),
        i_row=jnp.stack([gr[h:h + 1, :] for h in range(H)]))


def _mlstm_head(f, c_prev, n_prev, m_prev):
    L = MLSTM_CHUNK
    q, qb = f["q"], f["qb"]
    t = lax.broadcasted_iota(jnp.int32, (1, 2 * L, 2 * L), 1)
    s = lax.broadcasted_iota(jnp.int32, (1, 2 * L, 2 * L), 2)
    mask = (t // L == s // L) & (s <= t)
    d = jnp.where(mask, f["b_col"] - f["b_row"] + f["i_row"], NEG_INF)
    row = lax.broadcasted_iota(jnp.int32, (1, 2 * L, 1), 1)
    inter = f["b_col"] + jnp.where(row < L, m_prev[0], m_prev[1])
    m_t = jnp.maximum(inter, jnp.max(d, axis=-1, keepdims=True))
    w_intra = jnp.exp(d - m_t)
    w_inter = jnp.exp(inter - m_t)
    sc = _bdot(qb, f["kb"], 2, 2) * w_intra
    qc = jnp.concatenate([_bdot(qb[:, :L], c_prev[0].astype(BF16), 2, 1),
                          _bdot(qb[:, L:], c_prev[1].astype(BF16), 2, 1)], axis=1)
    qn = jnp.concatenate([jnp.sum(q[:, :L] * n_prev[0], axis=-1, keepdims=True),
                          jnp.sum(q[:, L:] * n_prev[1], axis=-1, keepdims=True)], axis=1)
    num = _bdot(sc.astype(BF16), f["vb"], 2, 1) + w_inter * qc
    den = jnp.sum(sc, axis=-1, keepdims=True) + w_inter * qn
    return dict(f, w_intra=w_intra, w_inter=w_inter, sc=sc, qc=qc, qn=qn, num=num, den=den,
                floor=jnp.exp(-m_t))


def _mlstm_update(f, ch, c, n, m):
    L = MLSTM_CHUNK
    rows = slice(L * ch, L * ch + L)
    b_col = f["b_col"][:, rows]
    g_last = b_col[:, L - 1:L]
    a_col = g_last - b_col + f["i_col"][:, rows]
    m_new = jnp.maximum(g_last + m, jnp.max(a_col, axis=1, keepdims=True))
    decay = jnp.exp(g_last + m - m_new)
    e_a = jnp.exp(a_col - m_new)
    kw = f["ks"][:, rows] * e_a
    c_new = decay * c + _bdot_rows(kw.astype(BF16), f["vb"][:, rows])
    n_new = decay * n + jnp.sum(kw, axis=1, keepdims=True)
    return c_new, n_new, m_new, decay, e_a, kw


def _mlstm_specs(T, order):
    blk = lambda w, col: pl.BlockSpec((STEP_ROWS, w), lambda s: (order(s), col))
    return [blk(512, 0), blk(512, 1), blk(512, 0), blk(128, 0),
            pl.BlockSpec((1, 128), lambda s: (0, 0)),
            pl.BlockSpec((8, STEP_ROWS), lambda s: (0, order(s))),
            pl.BlockSpec((8, 128), lambda s: (0, 0))]


def _lanes(m):
    return jnp.broadcast_to(m, m.shape[:-1] + (128,))


def _mlstm_fwd(qk, pm, gcol, bcol, grow, brow):
    T = qk.shape[0]
    steps = T // STEP_ROWS
    H, D = MLSTM_HEADS, MLSTM_HEAD_DIM

    def body(q_ref, k_ref, v_ref, gc_ref, bc_ref, gr_ref, br_ref, h_ref, cs_ref, ns_ref, ms_ref,
             c_st, n_st, m_st):
        @pl.when(pl.program_id(0) == 0)
        def _():
            c_st[...] = jnp.zeros_like(c_st)
            n_st[...] = jnp.zeros_like(n_st)
            m_st[...] = jnp.zeros_like(m_st)

        f = _mlstm_inputs(q_ref, k_ref, v_ref, *_mlstm_gates(gc_ref, bc_ref, gr_ref, br_ref))
        c0, n0, m0 = c_st[...], n_st[...], m_st[:, :, 0:1]
        c1, n1, m1, _, _, _ = _mlstm_update(f, 0, c0, n0, m0)
        c2, n2, m2, _, _, _ = _mlstm_update(f, 1, c1, n1, m1)
        f = _mlstm_head(f, (c0, c1), (n0, n1), (m0, m1))
        h = f["num"] / jnp.maximum(jnp.abs(f["den"]), f["floor"])
        for hd in range(H):
            h_ref[:, D * hd:D * hd + D] = h[hd]
        cs_ref[0], cs_ref[1] = c0, c1
        ns_ref[0], ns_ref[1] = n0, n1
        ms_ref[0], ms_ref[1] = _lanes(m0), _lanes(m1)
        c_st[...], n_st[...], m_st[...] = c2, n2, _lanes(m2)

    vec = pl.BlockSpec((2, H, 1, 128), lambda s: (s, 0, 0, 0))
    return pl.pallas_call(
        body, name="mlstm_fwd", grid=(steps,), in_specs=_mlstm_specs(T, lambda s: s),
        out_specs=[pl.BlockSpec((STEP_ROWS, 512), lambda s: (s, 0)),
                   pl.BlockSpec((2, H, 128, 128), lambda s: (s, 0, 0, 0)), vec, vec],
        out_shape=[_sds((T, 512), F32), _sds((2 * steps, H, 128, 128), F32),
                   _sds((2 * steps, H, 1, 128), F32), _sds((2 * steps, H, 1, 128), F32)],
        scratch_shapes=[pltpu.VMEM((H, 128, 128), F32), pltpu.VMEM((H, 1, 128), F32),
                        pltpu.VMEM((H, 1, 128), F32)],
        compiler_params=_params(("arbitrary",)))(qk, qk, pm, gcol, bcol, grow, brow)


def _mlstm_bwd(qk, pm, gcol, bcol, grow, brow, cs, ns, ms, dh):
    T = qk.shape[0]
    steps = T // STEP_ROWS
    H, L, D = MLSTM_HEADS, MLSTM_CHUNK, MLSTM_HEAD_DIM
    rev = lambda s: steps - 1 - s

    def body(q_ref, k_ref, v_ref, gc_ref, bc_ref, gr_ref, br_ref, cs_ref, ns_ref, ms_ref, dh_ref,
             dqk_ref, dv_ref, dgc_ref, dgr_ref, dc_st, dn_st):
        @pl.when(pl.program_id(0) == 0)
        def _():
            dc_st[...] = jnp.zeros_like(dc_st)
            dn_st[...] = jnp.zeros_like(dn_st)

        f = _mlstm_inputs(q_ref, k_ref, v_ref, *_mlstm_gates(gc_ref, bc_ref, gr_ref, br_ref))
        c_prev = (cs_ref[0], cs_ref[1])
        n_prev = (ns_ref[0], ns_ref[1])
        m_prev = (ms_ref[0, :, :, 0:1], ms_ref[1, :, :, 0:1])
        f = _mlstm_head(f, c_prev, n_prev, m_prev)
        big = jnp.abs(f["den"]) > f["floor"]
        rden = 1.0 / jnp.where(big, jnp.abs(f["den"]), f["floor"])
        dnum = _heads(dh_ref) * rden
        hdh = jnp.sum(f["num"] * dnum, axis=-1, keepdims=True)
        dden = jnp.where(big, -hdh * rden * jnp.sign(f["den"]), 0.0)
        dnum_b = dnum.astype(BF16)
        dsc = _bdot(dnum_b, f["vb"], 2, 2) + dden
        g = dsc * f["sc"]
        dv = _bdot_rows(f["sc"].astype(BF16), dnum_b)
        dqk_ = (dsc * f["w_intra"]).astype(BF16)
        dq = _bdot(dqk_, f["kb"], 2, 1)
        dks = _bdot_rows(dqk_, f["qb"])
        wdn = f["w_inter"] * dnum
        wdn_b = wdn.astype(BF16)
        wdd = f["w_inter"] * dden
        u = jnp.sum(f["qc"] * wdn, axis=-1, keepdims=True) + wdd * f["qn"]
        dks_s, dv_s, z_s, dg_s = [None, None], [None, None], [None, None], [None, None]
        dcn, dnn = dc_st[...], dn_st[...]
        for ch in (1, 0):
            rows = slice(L * ch, L * ch + L)
            _, _, _, decay, e_a, kw = _mlstm_update(f, ch, c_prev[ch], n_prev[ch], m_prev[ch])
            dcn_b = dcn.astype(BF16)
            dkw = _bdot(f["vb"][:, rows], dcn_b, 2, 2) + dnn
            dks_s[ch] = e_a * dkw
            dv_s[ch] = _bdot(kw.astype(BF16), dcn_b, 2, 1)
            z_s[ch] = e_a * jnp.sum(f["ks"][:, rows] * dkw, axis=-1, keepdims=True)
            dg_s[ch] = jnp.sum(z_s[ch], axis=1, keepdims=True) + decay * (
                jnp.sum(c_prev[ch] * dcn, axis=(1, 2), keepdims=True)
                + jnp.sum(n_prev[ch] * dnn, axis=(1, 2), keepdims=True))
            dcn = decay * dcn + _bdot_rows(f["qb"][:, rows], wdn_b[:, rows])
            dnn = decay * dnn + jnp.sum(wdd[:, rows] * f["q"][:, rows], axis=1, keepdims=True)
        dc_st[...], dn_st[...] = dcn, dnn
        dq = dq + jnp.concatenate(
            [_bdot(wdn_b[:, :L], c_prev[0].astype(BF16), 2, 2) + wdd[:, :L] * n_prev[0],
             _bdot(wdn_b[:, L:], c_prev[1].astype(BF16), 2, 2) + wdd[:, L:] * n_prev[1]], axis=1)
        dks = (dks + jnp.concatenate(dks_s, axis=1)) * (D ** -0.5)
        dv = dv + jnp.concatenate(dv_s, axis=1)
        z = jnp.concatenate(z_s, axis=1)
        row = lax.broadcasted_iota(jnp.int32, (1, STEP_ROWS, 1), 1)
        dg_col = jnp.where(row == L - 1, dg_s[0], 0.0) + jnp.where(row == 2 * L - 1, dg_s[1], 0.0)
        db_col = jnp.sum(g, axis=-1, keepdims=True) + u - z + dg_col
        g_row = jnp.sum(g, axis=1, keepdims=True)
        lane = lax.broadcasted_iota(jnp.int32, (STEP_ROWS, 128), 1)
        sub = lax.broadcasted_iota(jnp.int32, (8, STEP_ROWS), 0)
        dgc = jnp.zeros((STEP_ROWS, 128), F32)
        dgr = jnp.zeros((8, STEP_ROWS), F32)
        for hd in range(H):
            dgc = dgc + jnp.where(lane == hd, z[hd], 0.0) + jnp.where(lane == H + hd, db_col[hd], 0.0)
            dgr = dgr + jnp.where(sub == hd, g_row[hd], 0.0) - jnp.where(sub == H + hd, g_row[hd], 0.0)
            dqk_ref[:, D * hd:D * hd + D] = dq[hd]
            dqk_ref[:, H * D + D * hd:H * D + D * hd + D] = dks[hd]
            dv_ref[:, D * hd:D * hd + D] = dv[hd].astype(BF16)
        dgc_ref[...] = dgc
        dgr_ref[...] = dgr

    return pl.pallas_call(
        body, name="mlstm_bwd", grid=(steps,),
        in_specs=_mlstm_specs(T, rev) + [
            pl.BlockSpec((2, H, 128, 128), lambda s: (rev(s), 0, 0, 0)),
            pl.BlockSpec((2, H, 1, 128), lambda s: (rev(s), 0, 0, 0)),
            pl.BlockSpec((2, H, 1, 128), lambda s: (rev(s), 0, 0, 0)),
            pl.BlockSpec((STEP_ROWS, 512), lambda s: (rev(s), 0))],
        out_specs=[pl.BlockSpec((STEP_ROWS, 1024), lambda s: (rev(s), 0)),
                   pl.BlockSpec((STEP_ROWS, 512), lambda s: (rev(s), 0)),
                   pl.BlockSpec((STEP_ROWS, 128), lambda s: (rev(s), 0)),
                   pl.BlockSpec((8, STEP_ROWS), lambda s: (0, rev(s)))],
        out_shape=[_sds((T, 1024), F32), _sds((T, 512), BF16), _sds((T, 128), F32), _sds((8, T), F32)],
        scratch_shapes=[pltpu.VMEM((H, 128, 128), F32), pltpu.VMEM((H, 1, 128), F32)],
        compiler_params=_params(("arbitrary",)))(qk, qk, pm, gcol, bcol, grow, brow, cs, ns, ms, dh)


def _rows_to_lanes(x):
    eye = (lax.broadcasted_iota(jnp.int32, (8, 128), 0)
           == lax.broadcasted_iota(jnp.int32, (8, 128), 1)).astype(BF16)
    out, rest = None, x
    for _ in range(3):
        piece = rest.astype(BF16)
        rest = rest - piece.astype(F32)
        t = _dot(piece, eye, 0, 0)
        out = t if out is None else out + t
    return out


def _gate_bwd(dgc, dgr, gcol, bcol):
    T = dgc.shape[0]
    tr = min(ROW_TILE, T)

    def body(a_ref, b_ref, g_ref, bias_ref, o_ref, acc_ref):
        i = pl.program_id(0)

        @pl.when(i == 0)
        def _():
            acc_ref[...] = jnp.zeros_like(acc_ref)

        d = a_ref[...] + _rows_to_lanes(b_ref[:, pl.ds(pl.multiple_of(i * tr, 128), tr)])
        lane = lax.broadcasted_iota(jnp.int32, d.shape, 1)
        is_f = (lane >= MLSTM_HEADS) & (lane < 2 * MLSTM_HEADS)
        dlogf = _chunk_rev_cumsum(jnp.where(is_f, d, 0.0), 0)
        out = jnp.where(is_f, dlogf * _sigmoid(-(g_ref[...] + bias_ref[...])), d)
        o_ref[...] = out.astype(BF16)
        acc_ref[0:1, :] += _colsum(out)

    return _rows("gate_bwd", body, [dgc, dgr, gcol, bcol],
                 [_sds((T, 128), BF16), _sds((8, 128), F32)], T, tr=tr)


def _head_norm(h, mu_axis=-1):
    mu = jnp.mean(h, axis=-1, keepdims=True)
    hc = h - mu
    r = lax.rsqrt(jnp.mean(hc * hc, axis=-1, keepdims=True) + NORM_EPS)
    return hc * r, r


def _mlstm_out(hm, pm, w):
    T = hm.shape[0]
    D = MLSTM_HEAD_DIM

    def body(h_ref, o_ref, w_ref, y_ref):
        for hd in range(MLSTM_HEADS):
            cols = slice(D * hd, D * hd + D)
            hn, _ = _head_norm(h_ref[:, cols])
            y_ref[:, cols] = (_sigmoid(o_ref[:, cols].astype(F32)) * hn * w_ref[:, cols]).astype(BF16)

    tr = min(ROW_TILE, T)
    return pl.pallas_call(
        body, name="mlstm_out", grid=(T // tr,),
        in_specs=[pl.BlockSpec((tr, 512), lambda i: (i, 0)), pl.BlockSpec((tr, 512), lambda i: (i, 1)),
                  pl.BlockSpec((1, 512), lambda i: (0, 0))],
        out_specs=pl.BlockSpec((tr, 512), lambda i: (i, 0)), out_shape=_sds((T, 512), BF16),
        compiler_params=_params(("parallel",)))(hm, pm, w)


def _mlstm_out_bwd_rows(ps, es):
    hm, vo, w_all = es
    D, width = MLSTM_HEAD_DIM, MLSTM_HEADS * MLSTM_HEAD_DIM
    dhs, dos, dws = [], [], []
    for hd in range(MLSTM_HEADS):
        cols = slice(D * hd, D * hd + D)
        hn, r = _head_norm(hm[:, cols])
        sg = _sigmoid(vo[:, width + D * hd:width + D * hd + D].astype(F32))
        dy, w = ps[0][:, cols], w_all[:, cols]
        dos.append(dy * hn * w * sg * (1.0 - sg))
        dyn = dy * sg
        dws.append(_colsum(dyn * hn))
        dhn = dyn * w
        dhs.append(r * (dhn - jnp.mean(dhn, axis=-1, keepdims=True)
                        - hn * jnp.mean(dhn * hn, axis=-1, keepdims=True)))
    cat = lambda parts: jnp.concatenate(parts, axis=1)
    return [cat(dhs), cat(dos)], [_acc_rows([cat(dws)])]


ADAM_TILE_ELEMS = 256 * 1024


def _adamw(name, w, g, m, v):
    R, C = w.shape
    fits = [t for t in range(8, R + 1, 8) if R % t == 0 and t * C <= ADAM_TILE_ELEMS]
    if fits or R * C <= ADAM_TILE_ELEMS:
        tr = fits[-1] if fits else R
        spec, grid = pl.BlockSpec((tr, C), lambda i: (i, 0)), (R // tr,)
    else:
        spec, grid = pl.BlockSpec((R, 128), lambda i: (0, i)), (C // 128,)
    c1 = 1.0 - ADAM_B1 ** ADAM_STEP
    c2 = 1.0 - ADAM_B2 ** ADAM_STEP

    def body(w_ref, g_ref, m_ref, v_ref, d_ref, mo_ref, vo_ref):
        g = g_ref[...]
        m = ADAM_B1 * m_ref[...] + (1.0 - ADAM_B1) * g
        v = ADAM_B2 * v_ref[...] + (1.0 - ADAM_B2) * (g * g)
        mo_ref[...] = m
        vo_ref[...] = v
        d_ref[...] = -ADAM_LR * ((m / c1) / (jnp.sqrt(v / c2) + ADAM_EPS) + ADAM_WD * w_ref[...])

    return pl.pallas_call(
        body, name=name, grid=grid, in_specs=[spec] * 4, out_specs=[spec] * 3,
        out_shape=[_sds((R, C), F32)] * 3, compiler_params=_params(("parallel",)))(w, g, m, v)


def _place():
    return lax.axis_index("x"), lax.axis_index("y"), lax.axis_index("c")


def _all_gather8(name, blk, space):
    m, n = blk.shape

    def body(x_ref, out_ref, send_sems, recv_sems, local_sem):
        x, y, c = _place()
        me, sibling = (x, y, c), (x, y, 1 - c)
        chips = [(1 - x, y), (x, 1 - y), (1 - x, 1 - y)]

        def rows(px, py, pc):
            return out_ref.at[pl.ds((4 * px + 2 * py + pc) * m, m), :]

        def copy(k, block, to, src=None):
            return pltpu.make_async_remote_copy(
                src_ref=rows(*block) if src is None else src, dst_ref=rows(*block),
                send_sem=send_sems.at[k], recv_sem=recv_sems.at[k],
                device_id=to, device_id_type=MESH)

        mine = pltpu.make_async_copy(x_ref, rows(*me), local_sem)
        mine.start()
        first = [copy(0, me, sibling, src=x_ref)]
        first += [copy(1 + j, me, (*chip, c), src=x_ref) for j, chip in enumerate(chips)]
        for cp in first:
            cp.start()
        passed = [copy(4 + j, (*chip, c), sibling) for j, chip in enumerate(chips)]
        for j, chip in enumerate(chips):
            copy(1 + j, (*chip, c), me).wait_recv()
            passed[j].start()
        copy(0, sibling, me).wait_recv()
        for j, chip in enumerate(chips):
            copy(4 + j, (*chip, 1 - c), me).wait_recv()
        for cp in first + passed:
            cp.wait_send()
        mine.wait()

    return pl.pallas_call(
        body, name=name, out_shape=_sds((8 * m, n), blk.dtype),
        in_specs=[pl.BlockSpec(memory_space=space)], out_specs=pl.BlockSpec(memory_space=space),
        scratch_shapes=[pltpu.SemaphoreType.DMA((7,)), pltpu.SemaphoreType.DMA((7,)),
                        pltpu.SemaphoreType.DMA],
        compiler_params=pltpu.CompilerParams(vmem_limit_bytes=VMEM_LIMIT))(blk)


def _hbm_specs(n):
    return [pl.BlockSpec(memory_space=pl.ANY)] * n


def _swap_halves_sibling(name, srcs):
    nw = len(srcs)

    def body(*refs):
        src_refs, dst_refs, send_sems, recv_sems = refs[:nw], refs[nw:2 * nw], refs[2 * nw], refs[2 * nw + 1]
        x, y, c = _place()
        cps = [pltpu.make_async_remote_copy(
            src_ref=src_refs[w].at[pl.ds(0, 4), 1 - c], dst_ref=dst_refs[w],
            send_sem=send_sems.at[w], recv_sem=recv_sems.at[w], device_id=(x, y, 1 - c),
            device_id_type=MESH) for w in range(nw)]
        for cp in cps:
            cp.start()
        for cp in cps:
            cp.wait()

    return pl.pallas_call(
        body, name=name, out_shape=[_sds(s.shape[:1] + s.shape[2:], s.dtype) for s in srcs],
        in_specs=_hbm_specs(nw), out_specs=_hbm_specs(nw),
        scratch_shapes=[pltpu.SemaphoreType.DMA((nw,)), pltpu.SemaphoreType.DMA((nw,))])(*srcs)


def _split_start(name, srcs, lands, copies, per_array, after):
    nw = len(srcs)

    def body(*refs):
        send_sems, recv_sems, token = refs[2 * nw + 1], refs[2 * nw + 2], refs[-1]
        for w in range(nw):
            for k, (s, d, dev) in enumerate(copies(refs[w], refs[nw + w], *_place())):
                pltpu.make_async_remote_copy(
                    src_ref=s, dst_ref=d, send_sem=send_sems.at[w * per_array + k],
                    recv_sem=recv_sems.at[w * per_array + k], device_id=dev, device_id_type=MESH).start()
        token[...] = jnp.zeros_like(token)

    hbm, sem = pl.BlockSpec(memory_space=pltpu.HBM), pl.BlockSpec(memory_space=pltpu.SEMAPHORE)
    arrays = list(srcs) + list(lands)
    out = pl.pallas_call(
        body, name=name,
        out_shape=(pltpu.SemaphoreType.DMA((nw * per_array,)), pltpu.SemaphoreType.DMA((nw * per_array,)),
                   *[pltpu.HBM(a.shape, a.dtype) for a in arrays], _sds((8, 128), F32)),
        in_specs=[hbm] * (2 * nw) + [pl.BlockSpec(memory_space=pl.ANY)],
        out_specs=(sem, sem, *[hbm] * (2 * nw), pl.BlockSpec(memory_space=pltpu.VMEM)),
        input_output_aliases={i: 2 + i for i in range(2 * nw)},
        compiler_params=pltpu.CompilerParams(has_side_effects=pltpu.SideEffectType.DATAFLOW_SIDE_EFFECTING))(
            *[pltpu.with_memory_space_constraint(a, pltpu.HBM) for a in arrays], after)
    return out[0], out[1], out[2:2 + nw], out[2 + nw:2 + 2 * nw], out[-1]


def _split_wait(name, started, after, waits, per_array):
    send_sems, recv_sems, srcs, lands, _ = started
    nw = len(srcs)

    def body(*refs):
        send_sems, recv_sems = refs[2 * nw], refs[2 * nw + 1]
        x, y, c = _place()
        for w in range(nw):
            for k, (s, d) in enumerate(waits(refs[w], refs[nw + w], x, y, c)):
                cp = pltpu.make_async_remote_copy(
                    src_ref=s, dst_ref=d, send_sem=send_sems.at[w * per_array + k],
                    recv_sem=recv_sems.at[w * per_array + k], device_id=(x, y, 1 - c),
                    device_id_type=MESH)
                cp.wait_send()
                cp.wait_recv()

    hbm, sem = pl.BlockSpec(memory_space=pltpu.HBM), pl.BlockSpec(memory_space=pltpu.SEMAPHORE)
    arrays = list(srcs) + list(lands)
    out = pl.pallas_call(
        body, name=name, out_shape=tuple(pltpu.HBM(a.shape, a.dtype) for a in arrays),
        in_specs=[hbm] * (2 * nw) + [sem, sem, pl.BlockSpec(memory_space=pl.ANY)],
        out_specs=tuple([hbm] * (2 * nw)), input_output_aliases={i: i for i in range(2 * nw)},
        compiler_params=pltpu.CompilerParams(has_side_effects=pltpu.SideEffectType.DATAFLOW_SIDE_EFFECTING))(
            *arrays, send_sems, recv_sems, after)
    return list(out[nw:])


def _other_chips(x, y):
    return [(1 - x, y), (x, 1 - y), (1 - x, 1 - y)]


def _gather_sends(src_ref, land_ref, x, y, c):
    to = land_ref.at[2 * x + y, c]
    return [(src_ref, to, (x, y, 1 - c))] + [(src_ref, to, (px, py, c)) for px, py in _other_chips(x, y)]


def _gather_lands(src_ref, land_ref, x, y, c):
    return [(src_ref, land_ref.at[2 * x + y, 1 - c])] + [
        (src_ref, land_ref.at[2 * px + py, c]) for px, py in _other_chips(x, y)]


def _gather_sends_all(src_ref, land_ref, x, y, c):
    to = land_ref.at[2 * x + y, c]
    return [(src_ref, to, (x, y, 1 - c))] + [
        (src_ref, to, (px, py, pc)) for px, py in _other_chips(x, y) for pc in (c, 1 - c)]


def _gather_lands_all(src_ref, land_ref, x, y, c):
    return [(src_ref, land_ref.at[2 * x + y, 1 - c])] + [
        (src_ref, land_ref.at[2 * px + py, pc]) for px, py in _other_chips(x, y) for pc in (c, 1 - c)]


def _scatter_sends(src_ref, land_ref, x, y, c):
    return [(src_ref.at[2 * px + py], land_ref.at[2 * x + y], (px, py, c)) for px, py in _other_chips(x, y)]


def _scatter_lands(src_ref, land_ref, x, y, c):
    return [(src_ref.at[2 * x + y], land_ref.at[2 * px + py]) for px, py in _other_chips(x, y)]


def _forward_sibling(name, lands):
    nw = len(lands)

    def body(*refs):
        land_refs, out_refs, send_sems, recv_sems = refs[:nw], refs[nw:2 * nw], refs[2 * nw], refs[2 * nw + 1]
        x, y, c = _place()
        cps = []
        for w in range(nw):
            cps += [pltpu.make_async_remote_copy(
                src_ref=land_refs[w].at[2 * px + py, c], dst_ref=out_refs[w].at[2 * px + py, c],
                send_sem=send_sems.at[w, j], recv_sem=recv_sems.at[w, j], device_id=(x, y, 1 - c),
                device_id_type=MESH) for j, (px, py) in enumerate(_other_chips(x, y))]
        for cp in cps:
            cp.start()
        for w in range(nw):
            for j, (px, py) in enumerate(_other_chips(x, y)):
                slot = out_refs[w].at[2 * px + py, 1 - c]
                pltpu.make_async_remote_copy(src_ref=slot, dst_ref=slot, send_sem=send_sems.at[w, j],
                                             recv_sem=recv_sems.at[w, j], device_id=(x, y, 1 - c),
                                             device_id_type=MESH).wait_recv()
        for cp in cps:
            cp.wait_send()

    return pl.pallas_call(
        body, name=name, out_shape=[_sds(a.shape, a.dtype) for a in lands],
        in_specs=_hbm_specs(nw), out_specs=_hbm_specs(nw), input_output_aliases={i: i for i in range(nw)},
        scratch_shapes=[pltpu.SemaphoreType.DMA((nw, 3)), pltpu.SemaphoreType.DMA((nw, 3))])(*lands)


def _share_halves(name, halves):
    nw = len(halves)

    def body(*refs):
        in_refs, out_refs, send_sems, recv_sems = refs[:nw], refs[nw:2 * nw], refs[2 * nw], refs[2 * nw + 1]
        x, y, c = _place()
        cps = [pltpu.make_async_remote_copy(
            src_ref=in_refs[w].at[c], dst_ref=out_refs[w].at[c], send_sem=send_sems.at[w],
            recv_sem=recv_sems.at[w], device_id=(x, y, 1 - c), device_id_type=MESH) for w in range(nw)]
        for cp in cps:
            cp.start()
        for w in range(nw):
            slot = out_refs[w].at[1 - c]
            pltpu.make_async_remote_copy(src_ref=slot, dst_ref=slot, send_sem=send_sems.at[w],
                                         recv_sem=recv_sems.at[w], device_id=(x, y, 1 - c),
                                         device_id_type=MESH).wait_recv()
        for cp in cps:
            cp.wait_send()

    return pl.pallas_call(
        body, name=name, out_shape=[_sds(a.shape, a.dtype) for a in halves],
        in_specs=_hbm_specs(nw), out_specs=_hbm_specs(nw), input_output_aliases={i: i for i in range(nw)},
        scratch_shapes=[pltpu.SemaphoreType.DMA((nw,)), pltpu.SemaphoreType.DMA((nw,))])(*halves)


def _place_blocks(name, blks, place):
    nw = len(blks)

    def body(p_ref, *refs):
        for b_ref, o_ref in zip(refs[:nw], refs[nw:]):
            o_ref[...] = b_ref[...]

    return pl.pallas_call(
        body, name=name,
        grid_spec=pltpu.PrefetchScalarGridSpec(
            num_scalar_prefetch=1, grid=(1,),
            in_specs=[pl.BlockSpec(b.shape, lambda i, p: (0, 0)) for b in blks],
            out_specs=[pl.BlockSpec((None, None) + b.shape, lambda i, p: (p[0], p[1], 0, 0)) for b in blks]),
        out_shape=[_sds((4, 2) + b.shape, b.dtype) for b in blks],
        compiler_params=_params(("arbitrary",)))(place, *blks)


def _pair_sum(name, fulls, gots, place):
    nw = len(fulls)

    def body(p_ref, *refs):
        s = pl.program_id(0)
        for a_ref, b_ref, o_ref, l_ref in zip(refs[:nw], refs[nw:2 * nw], refs[2 * nw:3 * nw], refs[3 * nw:]):
            o_ref[...] = (a_ref[...].astype(F32) + b_ref[...].astype(F32)).astype(o_ref.dtype)

            @pl.when(s == p_ref[0])
            def _():
                l_ref[...] = o_ref[...]

    slab = lambda a: pl.BlockSpec((None,) + a.shape[1:], lambda s, p: (s, 0, 0))
    mine = lambda a: pl.BlockSpec((None,) + a.shape[1:], lambda s, p: (p[0], 0, 0))
    out = pl.pallas_call(
        body, name=name,
        grid_spec=pltpu.PrefetchScalarGridSpec(
            num_scalar_prefetch=1, grid=(4,),
            in_specs=[pl.BlockSpec((None, None) + a.shape[2:], lambda s, p: (s, p[1], 0, 0)) for a in fulls]
            + [slab(b) for b in gots],
            out_specs=[slab(b) for b in gots] + [mine(b) for b in gots]),
        out_shape=[_sds(b.shape, BF16) for b in gots] * 2,
        compiler_params=_params(("arbitrary",)))(place, *fulls, *gots)
    return out[:nw], out[nw:]


def _sum4(name, arrs, place):
    nw = len(arrs)

    def body(p_ref, *refs):
        for a_ref, o_ref in zip(refs[:nw], refs[nw:]):
            acc = a_ref[0].astype(F32)
            for s in range(1, 4):
                acc = acc + a_ref[s].astype(F32)
            o_ref[...] = acc

    return pl.pallas_call(
        body, name=name,
        grid_spec=pltpu.PrefetchScalarGridSpec(
            num_scalar_prefetch=1, grid=(1,),
            in_specs=[pl.BlockSpec(a.shape, lambda i, p: (0, 0, 0)) for a in arrs],
            out_specs=[pl.BlockSpec((None,) + a.shape[1:], lambda i, p: (p[1], 0, 0)) for a in arrs]),
        out_shape=[_sds((2,) + a.shape[1:], F32) for a in arrs],
        compiler_params=_params(("arbitrary",)))(place, *arrs)


def _small_update(gathered, params, slots):
    c1 = 1.0 - ADAM_B1 ** ADAM_STEP
    c2 = 1.0 - ADAM_B2 ** ADAM_STEP
    n = len(params)

    def body(g_ref, *refs):
        ins, sum_ref, outs = refs[:3 * n], refs[3 * n], refs[3 * n + 1:]
        g_all = g_ref[0:1, :]
        for d in range(1, 8):
            g_all = g_all + g_ref[d:d + 1, :]
        sum_ref[...] = g_all
        for k, (off, width) in enumerate(slots):
            w_ref, m_ref, v_ref = ins[3 * k:3 * k + 3]
            go_ref, d_ref, mo_ref, vo_ref = outs[4 * k:4 * k + 4]
            g = g_all[:, off:off + width]
            m = ADAM_B1 * m_ref[...] + (1.0 - ADAM_B1) * g
            v = ADAM_B2 * v_ref[...] + (1.0 - ADAM_B2) * (g * g)
            go_ref[...], mo_ref[...], vo_ref[...] = g, m, v
            d_ref[...] = -ADAM_LR * ((m / c1) / (jnp.sqrt(v / c2) + ADAM_EPS) + ADAM_WD * w_ref[...])

    flat = [a for p in params for a in p]
    out = pl.pallas_call(
        body, name="small_update",
        out_shape=[_sds((1, gathered.shape[1]), F32)] + [_sds(p[0].shape, F32) for p in params for _ in range(4)],
        compiler_params=pltpu.CompilerParams(vmem_limit_bytes=VMEM_LIMIT))(gathered, *flat)
    return out[0], [tuple(out[1 + 4 * k:5 + 4 * k]) for k in range(n)]


def _swiglu(ps, es):
    g, u = ps
    return g * _sigmoid(g) * u, g, u


def _swiglu_bwd(ps, es):
    g, u = es[0].astype(F32), es[1].astype(F32)
    sg = _sigmoid(g)
    return ps[0] * u * (sg * (1.0 + g * (1.0 - sg))), ps[0] * (g * sg)


def _merge(ps, es):
    ga, gm = [e.astype(F32) for e in es]
    return (_sigmoid(ga) * ps[0] + _sigmoid(gm) * ps[1],)


def _merge_bwd(ps, es):
    dm, a, b = ps
    ga, gm = [e.astype(F32) for e in es]
    sa, sm = _sigmoid(ga), _sigmoid(gm)
    return dm * sa, dm * sm, dm * a * (sa * (1.0 - sa)), dm * b * (sm * (1.0 - sm))


W_IN_PIECES = (("q", 512), ("kv", 256), ("mqk", 1024), ("mv", 512), ("mo", 512), ("if", 8),
               ("ga", 1024), ("gm", 1024))


def _local_step(x, tgt, pos_col, mod, sp, in_weights, late_weights, ffn_grads, mixer_grads):
    sh_m, sc_m, gate_m, sh_f, sc_f, gate_f = mod
    inv = ROPE_THETA ** (-2.0 * jnp.arange(HEAD_DIM // 2, dtype=F32) / HEAD_DIM)
    cos, sin = _rope_tables(pos_col, jnp.tile(inv, 4).reshape(1, 128))
    W = dict(in_weights(cos))
    h, pa, pqk, pvo, pif, pg = _proj_in(x, sp["g_pre_mix"], sc_m, sh_m, [
        (W["q+kv"], F32, 256), (W["mqk"], F32, 512), (W["mv+mo"], BF16, 512), (W["if"], F32, 128),
        (W["ga+gm"], BF16, 512)])
    ya = _attn_fwd(pa, cos, sin, sp["sinks"])
    qk = _conv_fwd(pqk, sp["conv_w"], sp["conv_b"])
    bcol = jnp.pad(sp["b_if"], ((0, 0), (0, 120)))
    brow = jnp.broadcast_to(sp["b_if"].reshape(8, 1), (8, 128))
    grow = pif[:, :8].T
    hm, cs, ns, ms = _mlstm_fwd(qk, pvo, pif, bcol, grow, brow)
    ym = _mlstm_out(hm, pvo, sp["norm_w"])
    W.update(late_weights(ym))
    w_fg, w_fu, w_fd = W["fg"], W["fu"], W["fd"]
    merged, = _mm("branches", [[(ya, W["ba"])], [(ym, W["bm"])]], [(pg, 0), (pg, 1)], _merge, [BF16],
                  cn=512, nt=True)
    wide, narrow = (D_MODEL, F32), (D_MODEL, BF16)
    mix, x1, h2 = _mm_rows("mix_out", [[(merged, W["out"])]],
                           [x, gate_m, sp["g_post_mix"], sp["g_pre_ffn"], sc_f, sh_f],
                           _res_norm_rows, [wide, wide, narrow], [], cn=512)
    act, gt, up = _mm("ffn_in", [[(h2, w_fg)], [(h2, w_fu)]], [], _swiglu, [BF16] * 3,
                      cn=256, nt=True)
    dy, dff, acc_l, loss = _mm_rows("ffn_down", [[(act, w_fd)]], [x1, tgt, gate_f, sp["g_post_ffn"]],
                                    _final_loss_rows, [wide, narrow], [(8, D_MODEL), (1, 128)], cn=512)

    G = {}
    dgt, dup = _mm("ffn_down_bwd", [[(dff, w_fd)]], [gt, up], _swiglu_bwd, [BF16, BF16],
                   cn=256, nt=True)
    g_fd, = _mm_tn_group("dw_ffn_down", [act], dff, BF16)
    g_fg, = _mm_tn_group("dw_ffn_gate", [dgt], h2, BF16)
    g_fu, = _mm_tn_group("dw_ffn_up", [dup], h2, BF16)
    tie = ffn_grads(g_fg, g_fu, g_fd)
    dx1, dmix, acc_r = _mm_rows(
        "ffn_in_bwd", [[(dgt, w_fg), (dup, w_fu)]],
        [x1, mix, dy, sc_f + tie, gate_m, sp["g_pre_ffn"], sp["g_post_mix"]],
        _res_norm_bwd_rows, [wide, narrow], [(8, D_MODEL)], cn=512, tm=256)
    d_a, d_m, dga, dgm = _mm("mix_out_bwd", [[(dmix, W["out"])], [(ya, W["ba"])], [(ym, W["bm"])]],
                             [(pg, 0), (pg, 1)], _merge_bwd, [BF16] * 4, cn=512, nt=True)
    G["out"], = _mm_tn_group("dw_out", [merged], dmix, BF16)
    dya, = _mm("branch_attn_bwd", [[(d_a, W["ba"])]], [], _first, [F32], cn=512)
    heads = MLSTM_HEADS * MLSTM_HEAD_DIM
    dhm, do_m, acc_n = _mm_rows("branch_mlstm_bwd", [[(d_m, W["bm"])]], [hm, pvo, sp["norm_w"]],
                                _mlstm_out_bwd_rows, [(heads, F32), (heads, BF16)], [(8, heads)], cn=512)
    G["ba"], = _mm_tn_group("dw_branch_attn", [d_a], ya, BF16)
    G["bm"], = _mm_tn_group("dw_branch_mlstm", [d_m], ym, BF16)
    dqk, dv_m, dgc, dgr = _mlstm_bwd(qk, pvo, pif, bcol, grow, brow, cs, ns, ms, dhm)
    dif, acc_g = _gate_bwd(dgc, dgr, pif, bcol)
    du, acc_c = _conv_bwd(pqk, sp["conv_w"], sp["conv_b"], dqk)
    dq_a, dkv, dsink = _attn_bwd(pa, cos, sin, sp["sinks"], dya)
    dproj = {"q": dq_a, "kv": dkv, "mqk": du, "mv": dv_m, "mo": do_m, "if": dif, "ga": dga, "gm": dgm}
    names = [k for k, _ in W_IN_PIECES]
    for part in (names[:4], names[4:]):
        G.update(zip(part, _mm_tn_group("dw_in_from_" + part[0], [dproj[k] for k in part], h, BF16)))
    w_tied = dict(W, **{"if": W["if"] + mixer_grads(G).astype(BF16)})
    dx, acc_p = _mm_rows("proj_bwd", [[(dproj[k], w_tied[k]) for k, _ in W_IN_PIECES]],
                         [x, dx1, sp["g_pre_mix"], sc_m], _pre_norm_bwd_rows, [wide], [(8, D_MODEL)], cn=512)

    small = {
        "mod": jnp.concatenate([acc_p[1], acc_p[0], acc_r[3], acc_r[1], acc_r[0], acc_l[0]]),
        "g_pre_mix": acc_p[2], "g_post_mix": acc_r[4], "b_if": acc_g[0, :8],
        "conv_w": acc_c[:CONV_WIDTH].reshape(-1), "conv_b": acc_c[CONV_WIDTH],
        "sinks": dsink[:, 0], "norm_w": acc_n[0], "g_pre_ffn": acc_r[2], "g_post_ffn": acc_l[1]}
    return loss, dx, small


IN_WIDTH = sum(n for _, n in W_IN_PIECES)
IN_SHARD = IN_WIDTH // 4
IN_SHARD_PAD = -(-IN_SHARD // 32) * 32


def _split_w_in(w_in_t):
    out, off, start = {}, 0, {}
    for k, n in W_IN_PIECES:
        out[k], start[k] = w_in_t[off:off + n], off
        off += n
    out["if"] = jnp.pad(out["if"], ((0, 120), (0, 0)))
    for name, first, last in (("q+kv", "q", "kv"), ("mv+mo", "mv", "mo"), ("ga+gm", "ga", "gm")):
        out[name] = w_in_t[start[first]:start[last] + out[last].shape[0]]
    return out


def _halves(a):
    return a.reshape(4, 2, a.shape[0] // 8, a.shape[1])


SMALL = (("b_ada", 6144), ("g_pre_mix", 1024), ("g_post_mix", 1024), ("b_if", 128), ("conv_w", 4096),
         ("conv_b", 1024), ("sinks", 128), ("norm_w", 512), ("g_pre_ffn", 1024), ("g_post_ffn", 1024))
SMALL_LEN = 8 * 2048


def _pack_small(vals):
    parts = []
    for k, n in SMALL:
        v = vals[k].reshape(-1)
        parts.append(jnp.pad(v, (0, n - v.shape[0])))
    flat = jnp.concatenate(parts)
    return jnp.pad(flat, (0, SMALL_LEN - flat.shape[0]))


def kernel(x, c, positions, w_ada, b_ada, g_pre_mix, g_post_mix, w_in, b_if, conv_w, conv_b, attn_sinks, mlstm_norm_w, w_branch_attn, w_branch_mlstm, w_out, g_pre_ffn, g_post_ffn, w_ffn_gate, w_ffn_up, w_ffn_down, loss_target, m_w_ada, m_b_ada, m_g_pre_mix, m_g_post_mix, m_w_in, m_b_if, m_conv_w, m_conv_b, m_attn_sinks, m_mlstm_norm_w, m_w_branch_attn, m_w_branch_mlstm, m_w_out, m_g_pre_ffn, m_g_post_ffn, m_w_ffn_gate, m_w_ffn_up, m_w_ffn_down, v_w_ada, v_b_ada, v_g_pre_mix, v_g_post_mix, v_w_in, v_b_if, v_conv_w, v_conv_b, v_attn_sinks, v_mlstm_norm_w, v_w_branch_attn, v_w_branch_mlstm, v_w_out, v_g_pre_ffn, v_g_post_ffn, v_w_ffn_gate, v_w_ffn_up, v_w_ffn_down):
    xi, yi, ci = _place()
    chip = 2 * xi + yi
    dev = 2 * chip + ci
    T = x.shape[1]
    ada_cols = w_ada.shape[2]

    place = jnp.stack([chip, ci]).astype(jnp.int32)

    def my_half(a):
        n = a.shape[0] // 2
        return lax.dynamic_slice_in_dim(a, ci * n, n, axis=0).astype(BF16)

    blk = jnp.concatenate([c.reshape(-1), conv_w.reshape(-1)]).reshape(8, 256)
    got = _all_gather8("gather_cond", blk, pltpu.VMEM).reshape(8, 2048)
    c_all = got[:, :D_MODEL].astype(BF16)
    conv_full = got[::2, D_MODEL:].reshape(4, CONV_WIDTH, -1).transpose(1, 0, 2).reshape(CONV_WIDTH, -1)

    b_sh = lax.dynamic_slice_in_dim(b_ada, chip * ada_cols, ada_cols, axis=1)
    mod_part, = _mm("ada_mod", [[(c_all, w_ada[0].astype(BF16))]], [b_sh],
                    lambda ps, es: (ps[0] + es[0],), [F32], cn=512, tm=8)
    mod_all = _all_gather8("gather_mod", mod_part, pltpu.VMEM).reshape(4, 2, 8, ada_cols)[:, 0]
    mod = lax.dynamic_index_in_dim(mod_all, dev, axis=1, keepdims=False).reshape(6, 1, D_MODEL)

    def gather_start(name, blks, after, sends, copies):
        return _split_start(name + "_start", blks, _place_blocks(name + "_place", blks, place),
                            sends, copies, after)

    w_in_t = jnp.pad(w_in[0].T, ((0, IN_SHARD_PAD - IN_SHARD), (0, 0)))
    in_started = gather_start("in_gather", [my_half(w_in_t)], mod, _gather_sends, 4)
    late_keys = ("fg", "fu", "fd", "out", "ba", "bm")
    late_started = gather_start(
        "late_gather",
        [my_half(w_ffn_gate[0].T), my_half(w_ffn_up[0].T), my_half(w_ffn_down[0]), my_half(w_out[0]),
         my_half(w_branch_attn[0].T), my_half(w_branch_mlstm[0].T)], in_started[4], _gather_sends_all, 7)
    mod = mod + (in_started[4][0, 0] + late_started[4][0, 0])

    def in_weights(after):
        g_in, = _forward_sibling("in_gather_forward",
                                 _split_wait("in_gather_wait", in_started, after, _gather_lands, 4))
        return _split_w_in(g_in.reshape(4, IN_SHARD_PAD, D_MODEL)[:, :IN_SHARD].reshape(IN_WIDTH, D_MODEL))

    def late_weights(after):
        lands = _split_wait("late_gather_wait", late_started, after, _gather_lands_all, 7)
        return {k: a.reshape(-1, a.shape[-1]) for k, a in zip(late_keys, lands)}

    sent = {}

    def scatter_start(name, groups):
        pairs, lands = _pair_sum(name + "_pair_sum", groups, _swap_halves_sibling(name + "_pair", groups), place)
        sent[name] = _split_start(name + "_start", pairs, lands, _scatter_sends, 3, pairs[0])
        return sent[name][4][0, 0]

    def ffn_grads(g_fg, g_fu, g_fd):
        return scatter_start("rs_ffn", [_halves(g_fg), _halves(g_fu), _halves(g_fd)])

    def mixer_grads(G):
        g_in_t = jnp.concatenate([G[k][:n] for k, n in W_IN_PIECES]).reshape(4, IN_SHARD, D_MODEL)
        g_in_t = jnp.pad(g_in_t, ((0, 0), (0, IN_SHARD_PAD - IN_SHARD), (0, 0)))
        return scatter_start("rs_mix", [g_in_t.reshape(4, 2, IN_SHARD_PAD // 2, D_MODEL), _halves(G["out"]),
                                        _halves(G["ba"]), _halves(G["bm"])])

    sp = {"g_pre_mix": g_pre_mix, "g_post_mix": g_post_mix, "b_if": b_if, "conv_w": conv_full,
          "conv_b": conv_b, "sinks": attn_sinks, "norm_w": mlstm_norm_w, "g_pre_ffn": g_pre_ffn,
          "g_post_ffn": g_post_ffn}
    loss, dx, small = _local_step(x[0], loss_target[0], positions.reshape(T, 1), [mod[i] for i in range(6)],
                                  sp, in_weights, late_weights, ffn_grads, mixer_grads)

    reds = (_sum4("rs_ffn_chip_sum", _split_wait("rs_ffn_wait", sent["rs_ffn"], dx, _scatter_lands, 3), place)
            + _sum4("rs_mix_chip_sum", _split_wait("rs_mix_wait", sent["rs_mix"], dx, _scatter_lands, 3), place))
    gsh = {k: s.reshape(-1, s.shape[-1])
           for k, s in zip(("fg", "fu", "fd", "w_in", "out", "ba", "bm"), _share_halves("rs_share", reds))}
    gsh["w_in"] = gsh["w_in"][:IN_SHARD]

    small["b_ada"] = small.pop("mod")
    vec = _pack_small(small).reshape(8, 2048)
    g_all = _all_gather8("gather_small", vec, pltpu.VMEM).reshape(8, SMALL_LEN)
    dmod_sh = lax.dynamic_slice_in_dim(g_all[:, :6 * D_MODEL], chip * ada_cols, ada_cols, axis=1)
    g_w_ada, = _mm_tn_group("dw_ada", [c_all], dmod_sh.astype(BF16), F32)

    smalls = {"b_ada": (b_ada, m_b_ada, v_b_ada), "g_pre_mix": (g_pre_mix, m_g_pre_mix, v_g_pre_mix),
              "g_post_mix": (g_post_mix, m_g_post_mix, v_g_post_mix), "b_if": (b_if, m_b_if, v_b_if),
              "conv_b": (conv_b, m_conv_b, v_conv_b), "sinks": (attn_sinks, m_attn_sinks, v_attn_sinks),
              "norm_w": (mlstm_norm_w, m_mlstm_norm_w, v_mlstm_norm_w),
              "g_pre_ffn": (g_pre_ffn, m_g_pre_ffn, v_g_pre_ffn),
              "g_post_ffn": (g_post_ffn, m_g_post_ffn, v_g_post_ffn)}
    offsets, off = {}, 0
    for k, width in SMALL:
        offsets[k], off = off, off + width
    g_sum, updates = _small_update(g_all, list(smalls.values()),
                                   [(offsets[k], t[0].shape[1]) for k, t in smalls.items()])
    g_conv = g_sum[:, offsets["conv_w"]:offsets["conv_w"] + CONV_WIDTH * D_MODEL].reshape(1, CONV_WIDTH, D_MODEL)
    g_conv = lax.dynamic_slice_in_dim(g_conv, chip * conv_w.shape[2], conv_w.shape[2], axis=2)

    res = dict(zip(smalls, updates))
    res["conv_w"] = (g_conv, *[o[None] for o in _adamw("adam_conv_w", conv_w[0], g_conv[0], m_conv_w[0], v_conv_w[0])])
    res["w_ada"] = (g_w_ada[None], *[o[None] for o in _adamw("adam_w_ada", w_ada[0], g_w_ada, m_w_ada[0], v_w_ada[0])])
    bigs = {"w_in": (w_in, m_w_in, v_w_in), "ba": (w_branch_attn, m_w_branch_attn, v_w_branch_attn),
            "bm": (w_branch_mlstm, m_w_branch_mlstm, v_w_branch_mlstm), "out": (w_out, m_w_out, v_w_out),
            "fg": (w_ffn_gate, m_w_ffn_gate, v_w_ffn_gate), "fu": (w_ffn_up, m_w_ffn_up, v_w_ffn_up),
            "fd": (w_ffn_down, m_w_ffn_down, v_w_ffn_down)}
    for k, (w, m, v) in bigs.items():
        if k in ("w_in", "fg", "fu"):
            res[k] = tuple(o.T[None] for o in (gsh[k], *_adamw("adam_" + k, w[0].T, gsh[k], m[0].T, v[0].T)))
        else:
            g = gsh[k].T if k in ("ba", "bm") else gsh[k]
            res[k] = (g[None], *[o[None] for o in _adamw("adam_" + k, w[0], g, m[0], v[0])])

    order = ("w_ada", "b_ada", "g_pre_mix", "g_post_mix", "w_in", "b_if", "conv_w", "conv_b", "sinks",
             "norm_w", "ba", "bm", "out", "g_pre_ffn", "g_post_ffn", "fg", "fu", "fd")
    total = lax.psum(loss[0, 0], ("x", "y", "c"))
    return (total, dx[None], *[res[k][0] for k in order], *[res[k][1] for k in order],
            *[res[k][2] for k in order], *[res[k][3] for k in order])
```

```python
import functools

import jax
import jax.numpy as jnp
from jax import lax
from jax.experimental import pallas as pl
from jax.experimental.pallas import tpu as pltpu

F32, BF16 = jnp.float32, jnp.bfloat16
MESH = pl.DeviceIdType.MESH

D_MODEL = 1024
N_Q_HEADS, N_KV_HEADS, HEAD_DIM, WINDOW = 8, 2, 64, 128
ROPE_THETA = 10000.0
MLSTM_HEADS, MLSTM_HEAD_DIM, MLSTM_CHUNK, CONV_WIDTH = 4, 128, 64, 4
D_FF = 2816
NORM_EPS = 1e-6
ADAM_LR, ADAM_B1, ADAM_B2, ADAM_EPS, ADAM_WD, ADAM_STEP = 0.001, 0.9, 0.999, 1e-08, 0.01, 10

VMEM_LIMIT = 56 * 1024 * 1024
ROW_TILE = 256
MM_TM = 512
MM_TT = 1024
ATTN_BLK = WINDOW
STEP_ROWS = 2 * MLSTM_CHUNK
NEG_INF = float("-inf")


def _params(sem):
    return pltpu.CompilerParams(dimension_semantics=sem, vmem_limit_bytes=VMEM_LIMIT)


def _sds(shape, dtype):
    return jax.ShapeDtypeStruct(shape, dtype)


def _sigmoid(x):
    return 1.0 / (1.0 + jnp.exp(-x))


def _dot(a, b, ca, cb):
    return lax.dot_general(a, b, (((ca,), (cb,)), ((), ())), preferred_element_type=F32)


def _bdot(a, b, ca, cb):
    return lax.dot_general(a, b, (((ca,), (cb,)), ((0,), (0,))), preferred_element_type=F32)


def _bdot_rows(a, b):
    return jnp.stack([_dot(a[h], b[h], 0, 0) for h in range(a.shape[0])])


def _mm(name, prods, extras, epi, out_dtypes, cn, nt=False, tm=MM_TM):
    flat = [ab for p in prods for ab in p]
    counts = [len(p) for p in prods]
    M = flat[0][0].shape[0]
    N = flat[0][1].shape[0 if nt else 1]
    tm = min(tm, M)
    n_in = 2 * len(flat) + len(extras)

    def body(*refs):
        ins, outs = refs[:n_in], refs[n_in:]
        for j in range(N // cn):
            cols = slice(j * cn, (j + 1) * cn)
            k, ps = 0, []
            for cnt in counts:
                acc = None
                for _ in range(cnt):
                    b = ins[k + 1][cols, :] if nt else ins[k + 1][:, cols]
                    d = _dot(ins[k][...], b, 1, 1 if nt else 0)
                    acc = d if acc is None else acc + d
                    k += 2
                ps.append(acc)
            res = epi(ps, [r[:, cols] for r in ins[k:]])
            for o, r in zip(outs, res):
                o[:, cols] = r.astype(o.dtype)

    in_specs, args = [], []
    for a, b in flat:
        in_specs.append(pl.BlockSpec((tm, a.shape[1]), lambda i: (i, 0)))
        in_specs.append(pl.BlockSpec(b.shape, lambda i: (0, 0), pipeline_mode=pl.Buffered(1)))
        args += [a, b]
    for e in extras:
        e, off = e if isinstance(e, tuple) else (e, 0)
        rows = 1 if e.shape[0] == 1 else tm
        in_specs.append(pl.BlockSpec((rows, N), lambda i, off=off, rows=rows: (0 if rows == 1 else i, off)))
        args.append(e)
    return pl.pallas_call(
        body, name=name, grid=(M // tm,), in_specs=in_specs,
        out_specs=[pl.BlockSpec((tm, N), lambda i: (i, 0)) for _ in out_dtypes],
        out_shape=[_sds((M, N), dt) for dt in out_dtypes],
        compiler_params=_params(("parallel",)))(*args)


def _mm_rows(name, prods, extras, epi, outs, accs, cn, nt=False, tm=MM_TM):
    flat = [ab for p in prods for ab in p]
    counts = [len(p) for p in prods]
    M = flat[0][0].shape[0]
    N = flat[0][1].shape[0 if nt else 1]
    tm = min(tm, M)
    n_mm, n_in, n_out = 2 * len(flat), 2 * len(flat) + len(extras), len(outs)

    def body(*refs):
        ins, out_refs, acc_refs = refs[:n_in], refs[n_in:n_in + n_out], refs[n_in + n_out:]

        @pl.when(pl.program_id(0) == 0)
        def _():
            for a in acc_refs:
                a[...] = jnp.zeros_like(a)

        chunks = [[] for _ in counts]
        for j in range(N // cn):
            cols = slice(j * cn, (j + 1) * cn)
            k = 0
            for p, cnt in enumerate(counts):
                acc = None
                for _ in range(cnt):
                    b = ins[k + 1][cols, :] if nt else ins[k + 1][:, cols]
                    d = _dot(ins[k][...], b, 1, 1 if nt else 0)
                    acc = d if acc is None else acc + d
                    k += 2
                chunks[p].append(acc)
        ps = [c[0] if len(c) == 1 else jnp.concatenate(c, axis=1) for c in chunks]
        res, incs = epi(ps, [r[...] for r in ins[n_mm:]])
        for o, r in zip(out_refs, res):
            o[...] = r.astype(o.dtype)
        for a, inc in zip(acc_refs, incs):
            a[...] += inc

    in_specs, args = [], []
    for a, b in flat:
        in_specs.append(pl.BlockSpec((tm, a.shape[1]), lambda i: (i, 0)))
        in_specs.append(pl.BlockSpec(b.shape, lambda i: (0, 0), pipeline_mode=pl.Buffered(1)))
        args += [a, b]
    for e in extras:
        rows = 1 if e.shape[0] == 1 else tm
        in_specs.append(pl.BlockSpec((rows, e.shape[1]), lambda i, rows=rows: (0 if rows == 1 else i, 0)))
        args.append(e)
    return pl.pallas_call(
        body, name=name, grid=(M // tm,), in_specs=in_specs,
        out_specs=[pl.BlockSpec((tm, w), lambda i: (i, 0)) for w, _ in outs]
        + [pl.BlockSpec(s, lambda i: (0, 0)) for s in accs],
        out_shape=[_sds((M, w), dt) for w, dt in outs] + [_sds(s, F32) for s in accs],
        compiler_params=_params(("arbitrary",)))(*args)


def _mm_tn_group(name, pieces, b, out_dtype, tt=MM_TT):
    T, N = b.shape
    tt = min(tt, T)
    steps, n = T // tt, len(pieces)

    def body(*refs):
        a_refs, b_ref, out_refs, accs = refs[:n], refs[n], refs[n + 1:2 * n + 1], refs[2 * n + 1:]
        t = pl.program_id(0)

        @pl.when(t == 0)
        def _():
            for acc in accs:
                acc[...] = jnp.zeros_like(acc)

        for a_ref, acc in zip(a_refs, accs):
            acc[...] += _dot(a_ref[...], b_ref[...], 0, 0)

        @pl.when(t == steps - 1)
        def _():
            for o_ref, acc in zip(out_refs, accs):
                o_ref[...] = acc[...].astype(o_ref.dtype)

    return pl.pallas_call(
        body, name=name, grid=(steps,),
        in_specs=[pl.BlockSpec((tt, a.shape[1]), lambda t: (t, 0)) for a in pieces]
        + [pl.BlockSpec((tt, N), lambda t: (t, 0))],
        out_specs=[pl.BlockSpec((a.shape[1], N), lambda t: (0, 0)) for a in pieces],
        out_shape=[_sds((a.shape[1], N), out_dtype) for a in pieces],
        scratch_shapes=[pltpu.VMEM((a.shape[1], N), F32) for a in pieces],
        compiler_params=_params(("arbitrary",)))(*pieces, b)


def _first(ps, es):
    return (ps[0],)


def _rows(name, body, ins, out_shapes, T, tr=ROW_TILE):
    tr = min(tr, T)

    def spec(shape):
        if shape[0] == T:
            return pl.BlockSpec((tr,) + tuple(shape[1:]), lambda i: (i,) + (0,) * (len(shape) - 1))
        return pl.BlockSpec(tuple(shape), lambda i: (0,) * len(shape))

    return pl.pallas_call(
        body, name=name, grid=(T // tr,),
        in_specs=[spec(a.shape) for a in ins], out_specs=[spec(s.shape) for s in out_shapes],
        out_shape=out_shapes, compiler_params=_params(("arbitrary",)))(*ins)


def _rms(x):
    r = lax.rsqrt(jnp.mean(x * x, axis=-1, keepdims=True) + NORM_EPS)
    return x * r, r


def _rms_bwd(dxn, xn, r):
    return r * (dxn - xn * jnp.mean(dxn * xn, axis=-1, keepdims=True))


def _colsum(v):
    return jnp.sum(v, axis=0, keepdims=True)


def _proj_in(x, g, sc, sh, groups):
    T = x.shape[0]
    tm = min(MM_TM, T)
    ng = len(groups)

    def body(x_ref, g_ref, sc_ref, sh_ref, *rest):
        w_refs, h_ref, out_refs = rest[:ng], rest[ng], rest[ng + 1:]
        xn, _ = _rms(x_ref[...])
        h = (xn * g_ref[...] * (1.0 + sc_ref[...]) + sh_ref[...]).astype(BF16)
        h_ref[...] = h
        for w_ref, o_ref, (w, _, cn) in zip(w_refs, out_refs, groups):
            for j in range(w.shape[0] // cn):
                cols = slice(j * cn, (j + 1) * cn)
                o_ref[:, cols] = _dot(h, w_ref[cols, :], 1, 1).astype(o_ref.dtype)

    row = pl.BlockSpec((1, D_MODEL), lambda i: (0, 0))
    tile = lambda w: pl.BlockSpec((tm, w), lambda i: (i, 0))
    return pl.pallas_call(
        body, name="proj_in", grid=(T // tm,),
        in_specs=[tile(D_MODEL), row, row, row] + [
            pl.BlockSpec(w.shape, lambda i: (0, 0), pipeline_mode=pl.Buffered(1)) for w, _, _ in groups],
        out_specs=[tile(D_MODEL)] + [tile(w.shape[0]) for w, _, _ in groups],
        out_shape=[_sds((T, D_MODEL), BF16)] + [_sds((T, w.shape[0]), dt) for w, dt, _ in groups],
        compiler_params=_params(("parallel",)))(x, g, sc, sh, *[w for w, _, _ in groups])


def _acc_rows(rows):
    w = rows[0].shape[1]
    return jnp.concatenate(rows + [jnp.zeros((8 - len(rows), w), F32)], axis=0)


def _res_norm_rows(ps, es):
    mix = ps[0]
    x, gate, gp, g2, sc, sh = es
    mh, _ = _rms(mix)
    x1 = x + gate * (mh * gp)
    xn, _ = _rms(x1)
    return [mix, x1, xn * g2 * (1.0 + sc) + sh], []


def _final_loss_rows(ps, es):
    x1, tgt, gate, gp = es
    fh, r = _rms(ps[0])
    e = x1 + gate * (fh * gp) - tgt
    loss = 0.5 * jnp.sum(jnp.mean(e * e, axis=-1, keepdims=True))
    dy = e * (1.0 / D_MODEL)
    acc = _acc_rows([_colsum(dy * fh * gp), _colsum(dy * gate * fh)])
    return [dy, _rms_bwd(dy * gate * gp, fh, r)], [acc, jnp.full((1, 128), loss, F32)]


def _res_norm_bwd_rows(ps, es):
    dh = ps[0]
    x1, mix, dy, sc, gate, g2, gp = es
    xn, r1 = _rms(x1)
    rows = [_colsum(dh * xn * g2), _colsum(dh), _colsum(dh * (1.0 + sc) * xn)]
    dx1 = dy + _rms_bwd(dh * (1.0 + sc) * g2, xn, r1)
    mh, rm = _rms(mix)
    rows += [_colsum(dx1 * mh * gp), _colsum(dx1 * gate * mh)]
    return [dx1, _rms_bwd(dx1 * gate * gp, mh, rm)], [_acc_rows(rows)]


def _pre_norm_bwd_rows(ps, es):
    dh = ps[0]
    x, dx1, g, sc = es
    xn, r = _rms(x)
    rows = [_colsum(dh * xn * g), _colsum(dh), _colsum(dh * (1.0 + sc) * xn)]
    return [dx1 + _rms_bwd(dh * (1.0 + sc) * g, xn, r)], [_acc_rows(rows)]


def _rope_tables(pos_col, inv_freq):
    T = pos_col.shape[0]

    def body(p_ref, f_ref, c_ref, s_ref):
        ang = p_ref[...].astype(F32) * f_ref[...]
        lane = lax.broadcasted_iota(jnp.int32, ang.shape, 1)
        c_ref[...] = jnp.cos(ang)
        s_ref[...] = jnp.where(lane % HEAD_DIM < HEAD_DIM // 2, -1.0, 1.0) * jnp.sin(ang)

    return _rows("rope_tables", body, [pos_col, inv_freq],
                 [_sds((T, 128), F32), _sds((T, 128), F32)], T, tr=512)


def _swap_halves(t):
    W = t.shape[1]
    lane = lax.broadcasted_iota(jnp.int32, t.shape, 1)
    half = HEAD_DIM // 2
    return jnp.where(lane % HEAD_DIM < half, pltpu.roll(t, W - half, 1), pltpu.roll(t, half, 1))


def _widen(c, W):
    return c if W == 128 else jnp.concatenate([c] * (W // 128), axis=1)


def _rope(t, c, s):
    W = t.shape[1]
    return t * _widen(c, W) + _swap_halves(t) * _widen(s, W)


def _unrope(dy, c, s):
    W = dy.shape[1]
    return dy * _widen(c, W) + _swap_halves(dy * _widen(s, W))


def _attn_mask(n):
    qi = lax.broadcasted_iota(jnp.int32, (ATTN_BLK, 2 * ATTN_BLK), 0)
    kj = lax.broadcasted_iota(jnp.int32, (ATTN_BLK, 2 * ATTN_BLK), 1)
    rel = kj - ATTN_BLK
    return (rel <= qi) & (qi - rel < WINDOW) & ((n > 0) | (kj >= ATTN_BLK))


def _attn_load(cur, prv, cc, sc, cp, sp):
    x, xp = cur[...], prv[...]
    q = _rope(x[:, :512], cc[...], sc[...]) * (HEAD_DIM ** -0.5)
    k = jnp.concatenate([_rope(xp[:, 512:640], cp[...], sp[...]),
                         _rope(x[:, 512:640], cc[...], sc[...])], axis=0)
    v = jnp.concatenate([xp[:, 640:768], x[:, 640:768]], axis=0)
    return q, k, v


ROLLED = tuple(h for h in range(N_Q_HEADS) if h % 2 != h // (N_Q_HEADS // N_KV_HEADS))


def _pair_heads(t):
    half = lax.broadcasted_iota(jnp.int32, (ATTN_BLK, 128), 1) // HEAD_DIM
    return jnp.stack([jnp.where(half == h % 2, t[:, 128 * (h // 2):128 * (h // 2) + 128], 0.0)
                      for h in range(N_Q_HEADS)])


def _kv_heads(t):
    half = lax.broadcasted_iota(jnp.int32, t.shape, 1) // HEAD_DIM
    tr = pltpu.roll(t, HEAD_DIM, 1)
    return jnp.stack([jnp.where(half == h % 2, tr if h in ROLLED else t, 0.0)
                      for h in range(N_Q_HEADS)])


def _sink_column(snk):
    return jnp.stack([jnp.full((1, 1), snk[0, h], F32) for h in range(N_Q_HEADS)])


def _attn_probs(qh, kh, mask, sink):
    s = jnp.where(mask, _bdot(qh, kh, 2, 2), NEG_INF)
    m = jnp.maximum(jnp.max(s, axis=-1, keepdims=True), sink)
    p = jnp.exp(s - m)
    es = jnp.exp(sink - m)
    rl = 1.0 / (jnp.sum(p, axis=-1, keepdims=True) + es)
    return p, es, rl


def _attn_specs(order):
    blk = lambda w: pl.BlockSpec((ATTN_BLK, w), lambda s: (order(s), 0))
    prv = lambda w: pl.BlockSpec((ATTN_BLK, w), lambda s: (jnp.maximum(order(s) - 1, 0), 0))
    return [blk(768), prv(768), blk(128), blk(128), prv(128), prv(128),
            pl.BlockSpec(memory_space=pltpu.SMEM)]


def _attn_fwd(pa, cos, sin, sinks):
    T = pa.shape[0]
    nb = T // ATTN_BLK

    def body(cur, prv, cc, sc, cp, sp, snk, y_ref):
        n = pl.program_id(0)
        q, k, v = _attn_load(cur, prv, cc, sc, cp, sp)
        qh, kh, vh = _pair_heads(q).astype(BF16), _kv_heads(k).astype(BF16), _kv_heads(v).astype(BF16)
        p, _, rl = _attn_probs(qh, kh, _attn_mask(n), _sink_column(snk))
        o = _bdot(p.astype(BF16), vh, 2, 1) * rl
        for pair in range(N_Q_HEADS // 2):
            y_ref[:, 128 * pair:128 * pair + 128] = (o[2 * pair] + o[2 * pair + 1]).astype(BF16)

    return pl.pallas_call(
        body, name="attn_fwd", grid=(nb,), in_specs=_attn_specs(lambda s: s),
        out_specs=pl.BlockSpec((ATTN_BLK, 512), lambda n: (n, 0)),
        out_shape=_sds((T, 512), BF16), compiler_params=_params(("parallel",)))(
            pa, pa, cos, sin, cos, sin, sinks)


def _attn_bwd(pa, cos, sin, sinks, dy):
    T = pa.shape[0]
    nb = T // ATTN_BLK
    rev = lambda s: nb - 1 - s

    def body(cur, prv, cc, sc, cp, sp, snk, dy_ref, dq_ref, dkv_ref, dsink_ref, carry):
        n = rev(pl.program_id(0))

        @pl.when(pl.program_id(0) == 0)
        def _():
            dsink_ref[...] = jnp.zeros_like(dsink_ref)
            carry[...] = jnp.zeros_like(carry)

        q, k, v = _attn_load(cur, prv, cc, sc, cp, sp)
        qh, kh, vh = _pair_heads(q).astype(BF16), _kv_heads(k).astype(BF16), _kv_heads(v).astype(BF16)
        p, es, rl = _attn_probs(qh, kh, _attn_mask(n), _sink_column(snk))
        pn = p * rl
        do = _pair_heads(dy_ref[...]).astype(BF16)
        dp = _bdot(do, vh, 2, 2)
        delta = jnp.sum(pn * dp, axis=-1, keepdims=True)
        ds = (pn * (dp - delta)).astype(BF16)
        dsink = es * rl * delta
        dq = _bdot(ds, kh, 2, 1) * (HEAD_DIM ** -0.5)
        dkh = _bdot_rows(ds, qh)
        dvh = _bdot_rows(pn.astype(BF16), do)

        def fold(t):
            same = [t[h] for h in range(N_Q_HEADS) if h not in ROLLED]
            moved = [t[h] for h in ROLLED]
            return sum(same[1:], same[0]) + pltpu.roll(sum(moved[1:], moved[0]), HEAD_DIM, 1)

        dk, dv = fold(dkh), fold(dvh)
        for h in range(N_Q_HEADS):
            dsink_ref[h:h + 1, :] += -jnp.sum(dsink[h])
        for pair in range(N_Q_HEADS // 2):
            dq_ref[:, 128 * pair:128 * pair + 128] = _unrope(
                dq[2 * pair] + dq[2 * pair + 1], cc[...], sc[...]).astype(BF16)
        dkv_ref[:, 0:128] = _unrope(dk[ATTN_BLK:] + carry[:, 0:128], cc[...], sc[...]).astype(BF16)
        dkv_ref[:, 128:256] = (dv[ATTN_BLK:] + carry[:, 128:256]).astype(BF16)
        carry[:, 0:128] = dk[:ATTN_BLK]
        carry[:, 128:256] = dv[:ATTN_BLK]

    blk = lambda w: pl.BlockSpec((ATTN_BLK, w), lambda s: (rev(s), 0))
    return pl.pallas_call(
        body, name="attn_bwd", grid=(nb,), in_specs=_attn_specs(rev) + [blk(512)],
        out_specs=[blk(512), blk(256), pl.BlockSpec((8, 128), lambda s: (0, 0))],
        out_shape=[_sds((T, 512), BF16), _sds((T, 256), BF16), _sds((8, 128), F32)],
        scratch_shapes=[pltpu.VMEM((ATTN_BLK, 256), F32)],
        compiler_params=_params(("arbitrary",)))(pa, pa, cos, sin, cos, sin, sinks, dy)


CONV_COLS = 2 * MLSTM_HEADS * MLSTM_HEAD_DIM


def _conv_pre(cur_ref, halo_ref, w_ref, b_ref, i, tr):
    xx = jnp.concatenate([jnp.where(i > 0, halo_ref[...], 0.0), cur_ref[...]], axis=0)
    taps = [(pltpu.roll(xx, CONV_WIDTH - 1 - j, 0) if j < CONV_WIDTH - 1 else xx)[8:8 + tr]
            for j in range(CONV_WIDTH)]
    pre = b_ref[...]
    for j in range(CONV_WIDTH):
        pre = pre + taps[j] * w_ref[j:j + 1, :]
    return pre, taps


def _conv_specs(T, tr):
    return [pl.BlockSpec((tr, CONV_COLS), lambda i: (i, 0)),
            pl.BlockSpec((8, CONV_COLS), lambda i: (jnp.maximum(i * (tr // 8) - 1, 0), 0)),
            pl.BlockSpec((CONV_WIDTH, CONV_COLS), lambda i: (0, 0)),
            pl.BlockSpec((1, CONV_COLS), lambda i: (0, 0))]


def _conv_fwd(pm, w, b):
    T = pm.shape[0]
    tr = min(ROW_TILE, T)

    def body(cur_ref, halo_ref, w_ref, b_ref, o_ref):
        pre, _ = _conv_pre(cur_ref, halo_ref, w_ref, b_ref, pl.program_id(0), tr)
        o_ref[...] = pre * _sigmoid(pre)

    return pl.pallas_call(
        body, name="conv_fwd", grid=(T // tr,), in_specs=_conv_specs(T, tr),
        out_specs=pl.BlockSpec((tr, CONV_COLS), lambda i: (i, 0)),
        out_shape=_sds((T, CONV_COLS), F32), compiler_params=_params(("parallel",)))(pm, pm, w, b)


def _conv_bwd(pqk, w, b, dqk):
    T = pqk.shape[0]
    tr = min(ROW_TILE, T)
    nt = T // tr

    def body(cur_ref, prev_ref, next_ref, w_ref, b_ref, d_ref, dnext_ref, du_ref, acc_ref):
        i = pl.program_id(0)

        @pl.when(i == 0)
        def _():
            acc_ref[...] = jnp.zeros_like(acc_ref)

        last = i == nt - 1
        xx = jnp.concatenate([jnp.where(i > 0, prev_ref[...], 0.0), cur_ref[...],
                              jnp.where(last, 0.0, next_ref[...])], axis=0)
        taps = [(pltpu.roll(xx, CONV_WIDTH - 1 - j, 0) if j < CONV_WIDTH - 1 else xx)[8:16 + tr]
                for j in range(CONV_WIDTH)]
        pre = b_ref[...]
        for j in range(CONV_WIDTH):
            pre = pre + taps[j] * w_ref[j:j + 1, :]
        sg = _sigmoid(pre)
        dd = jnp.concatenate([d_ref[...], jnp.where(last, 0.0, dnext_ref[...])], axis=0)
        dpre = dd * (sg * (1.0 + pre * (1.0 - sg)))
        for j in range(CONV_WIDTH):
            acc_ref[j:j + 1, :] += _colsum(dpre[:tr] * taps[j][:tr])
        acc_ref[CONV_WIDTH:CONV_WIDTH + 1, :] += _colsum(dpre[:tr])
        du = dpre[:tr] * w_ref[CONV_WIDTH - 1:CONV_WIDTH, :]
        for j in range(CONV_WIDTH - 1):
            k = CONV_WIDTH - 1 - j
            du = du + pltpu.roll(dpre, tr + 8 - k, 0)[:tr] * w_ref[j:j + 1, :]
        du_ref[...] = du.astype(BF16)

    tile = pl.BlockSpec((tr, CONV_COLS), lambda i: (i, 0))
    after = pl.BlockSpec((8, CONV_COLS), lambda i: (jnp.minimum((i + 1) * (tr // 8), T // 8 - 1), 0))
    before = pl.BlockSpec((8, CONV_COLS), lambda i: (jnp.maximum(i * (tr // 8) - 1, 0), 0))
    return pl.pallas_call(
        body, name="conv_bwd", grid=(nt,),
        in_specs=[tile, before, after, pl.BlockSpec((CONV_WIDTH, CONV_COLS), lambda i: (0, 0)),
                  pl.BlockSpec((1, CONV_COLS), lambda i: (0, 0)), tile, after],
        out_specs=[tile, pl.BlockSpec((8, CONV_COLS), lambda i: (0, 0))],
        out_shape=[_sds((T, CONV_COLS), BF16), _sds((8, CONV_COLS), F32)],
        compiler_params=_params(("arbitrary",)))(pqk, pqk, pqk, w, b, dqk, dqk)


def _log_sigmoid(x):
    return jnp.minimum(x, 0.0) - jnp.log1p(jnp.exp(-jnp.abs(x)))


def _chunk_cumsum(x, axis):
    idx = lax.broadcasted_iota(jnp.int32, x.shape, axis) % MLSTM_CHUNK
    k = 1
    while k < MLSTM_CHUNK:
        x = x + jnp.where(idx >= k, pltpu.roll(x, k, axis), 0.0)
        k *= 2
    return x


def _chunk_rev_cumsum(x, axis):
    n = x.shape[axis]
    idx = lax.broadcasted_iota(jnp.int32, x.shape, axis) % MLSTM_CHUNK
    k = 1
    while k < MLSTM_CHUNK:
        x = x + jnp.where(idx < MLSTM_CHUNK - k, pltpu.roll(x, n - k, axis), 0.0)
        k *= 2
    return x


def _mlstm_gates(gc_ref, bc_ref, gr_ref, br_ref):
    gc = gc_ref[...] + bc_ref[...]
    gr = gr_ref[...] + br_ref[...]
    return gc, _chunk_cumsum(_log_sigmoid(gc), 0), gr, _chunk_cumsum(_log_sigmoid(gr), 1)


def _heads(ref, base=0):
    D = MLSTM_HEAD_DIM
    return jnp.stack([ref[:, base + D * h:base + D * h + D] for h in range(MLSTM_HEADS)])


def _mlstm_inputs(q_ref, k_ref, v_ref, gc, bc, gr, br):
    H = MLSTM_HEADS
    q, v = _heads(q_ref), _heads(v_ref)
    ks = _heads(k_ref) * (MLSTM_HEAD_DIM ** -0.5)
    return dict(
        q=q, ks=ks, qb=q.astype(BF16), kb=ks.astype(BF16), vb=v.astype(BF16),
        b_col=jnp.stack([bc[:, H + h:H + h + 1] for h in range(H)]),
        i_col=jnp.stack([gc[:, h:h + 1] for h in range(H)]),
        b_row=jnp.stack([br[H + h:H + h + 1, :] for h in range(H)]),
        i_row=jnp.stack([gr[h:h + 1, :] for h in range(H)]))


def _mlstm_head(f, c_prev, n_prev, m_prev):
    L = MLSTM_CHUNK
    q, qb = f["q"], f["qb"]
    t = lax.broadcasted_iota(jnp.int32, (1, 2 * L, 2 * L), 1)
    s = lax.broadcasted_iota(jnp.int32, (1, 2 * L, 2 * L), 2)
    mask = (t // L == s // L) & (s <= t)
    d = jnp.where(mask, f["b_col"] - f["b_row"] + f["i_row"], NEG_INF)
    row = lax.broadcasted_iota(jnp.int32, (1, 2 * L, 1), 1)
    inter = f["b_col"] + jnp.where(row < L, m_prev[0], m_prev[1])
    m_t = jnp.maximum(inter, jnp.max(d, axis=-1, keepdims=True))
    w_intra = jnp.exp(d - m_t)
    w_inter = jnp.exp(inter - m_t)
    sc = _bdot(qb, f["kb"], 2, 2) * w_intra
    qc = jnp.concatenate([_bdot(qb[:, :L], c_prev[0].astype(BF16), 2, 1),
                          _bdot(qb[:, L:], c_prev[1].astype(BF16), 2, 1)], axis=1)
    qn = jnp.concatenate([jnp.sum(q[:, :L] * n_prev[0], axis=-1, keepdims=True),
                          jnp.sum(q[:, L:] * n_prev[1], axis=-1, keepdims=True)], axis=1)
    num = _bdot(sc.astype(BF16), f["vb"], 2, 1) + w_inter * qc
    den = jnp.sum(sc, axis=-1, keepdims=True) + w_inter * qn
    return dict(f, w_intra=w_intra, w_inter=w_inter, sc=sc, qc=qc, qn=qn, num=num, den=den,
                floor=jnp.exp(-m_t))


def _mlstm_update(f, ch, c, n, m):
    L = MLSTM_CHUNK
    rows = slice(L * ch, L * ch + L)
    b_col = f["b_col"][:, rows]
    g_last = b_col[:, L - 1:L]
    a_col = g_last - b_col + f["i_col"][:, rows]
    m_new = jnp.maximum(g_last + m, jnp.max(a_col, axis=1, keepdims=True))
    decay = jnp.exp(g_last + m - m_new)
    e_a = jnp.exp(a_col - m_new)
    kw = f["ks"][:, rows] * e_a
    c_new = decay * c + _bdot_rows(kw.astype(BF16), f["vb"][:, rows])
    n_new = decay * n + jnp.sum(kw, axis=1, keepdims=True)
    return c_new, n_new, m_new, decay, e_a, kw


def _mlstm_specs(T, order):
    blk = lambda w, col: pl.BlockSpec((STEP_ROWS, w), lambda s: (order(s), col))
    return [blk(512, 0), blk(512, 1), blk(512, 0), blk(128, 0),
            pl.BlockSpec((1, 128), lambda s: (0, 0)),
            pl.BlockSpec((8, STEP_ROWS), lambda s: (0, order(s))),
            pl.BlockSpec((8, 128), lambda s: (0, 0))]


def _lanes(m):
    return jnp.broadcast_to(m, m.shape[:-1] + (128,))


def _mlstm_fwd(qk, pm, gcol, bcol, grow, brow):
    T = qk.shape[0]
    steps = T // STEP_ROWS
    H, D = MLSTM_HEADS, MLSTM_HEAD_DIM

    def body(q_ref, k_ref, v_ref, gc_ref, bc_ref, gr_ref, br_ref, h_ref, cs_ref, ns_ref, ms_ref,
             c_st, n_st, m_st):
        @pl.when(pl.program_id(0) == 0)
        def _():
            c_st[...] = jnp.zeros_like(c_st)
            n_st[...] = jnp.zeros_like(n_st)
            m_st[...] = jnp.zeros_like(m_st)

        f = _mlstm_inputs(q_ref, k_ref, v_ref, *_mlstm_gates(gc_ref, bc_ref, gr_ref, br_ref))
        c0, n0, m0 = c_st[...], n_st[...], m_st[:, :, 0:1]
        c1, n1, m1, _, _, _ = _mlstm_update(f, 0, c0, n0, m0)
        c2, n2, m2, _, _, _ = _mlstm_update(f, 1, c1, n1, m1)
        f = _mlstm_head(f, (c0, c1), (n0, n1), (m0, m1))
        h = f["num"] / jnp.maximum(jnp.abs(f["den"]), f["floor"])
        for hd in range(H):
            h_ref[:, D * hd:D * hd + D] = h[hd]
        cs_ref[0], cs_ref[1] = c0, c1
        ns_ref[0], ns_ref[1] = n0, n1
        ms_ref[0], ms_ref[1] = _lanes(m0), _lanes(m1)
        c_st[...], n_st[...], m_st[...] = c2, n2, _lanes(m2)

    vec = pl.BlockSpec((2, H, 1, 128), lambda s: (s, 0, 0, 0))
    return pl.pallas_call(
        body, name="mlstm_fwd", grid=(steps,), in_specs=_mlstm_specs(T, lambda s: s),
        out_specs=[pl.BlockSpec((STEP_ROWS, 512), lambda s: (s, 0)),
                   pl.BlockSpec((2, H, 128, 128), lambda s: (s, 0, 0, 0)), vec, vec],
        out_shape=[_sds((T, 512), F32), _sds((2 * steps, H, 128, 128), F32),
                   _sds((2 * steps, H, 1, 128), F32), _sds((2 * steps, H, 1, 128), F32)],
        scratch_shapes=[pltpu.VMEM((H, 128, 128), F32), pltpu.VMEM((H, 1, 128), F32),
                        pltpu.VMEM((H, 1, 128), F32)],
        compiler_params=_params(("arbitrary",)))(qk, qk, pm, gcol, bcol, grow, brow)


def _mlstm_bwd(qk, pm, gcol, bcol, grow, brow, cs, ns, ms, dh):
    T = qk.shape[0]
    steps = T // STEP_ROWS
    H, L, D = MLSTM_HEADS, MLSTM_CHUNK, MLSTM_HEAD_DIM
    rev = lambda s: steps - 1 - s

    def body(q_ref, k_ref, v_ref, gc_ref, bc_ref, gr_ref, br_ref, cs_ref, ns_ref, ms_ref, dh_ref,
             dqk_ref, dv_ref, dgc_ref, dgr_ref, dc_st, dn_st):
        @pl.when(pl.program_id(0) == 0)
        def _():
            dc_st[...] = jnp.zeros_like(dc_st)
            dn_st[...] = jnp.zeros_like(dn_st)

        f = _mlstm_inputs(q_ref, k_ref, v_ref, *_mlstm_gates(gc_ref, bc_ref, gr_ref, br_ref))
        c_prev = (cs_ref[0], cs_ref[1])
        n_prev = (ns_ref[0], ns_ref[1])
        m_prev = (ms_ref[0, :, :, 0:1], ms_ref[1, :, :, 0:1])
        f = _mlstm_head(f, c_prev, n_prev, m_prev)
        big = jnp.abs(f["den"]) > f["floor"]
        rden = 1.0 / jnp.where(big, jnp.abs(f["den"]), f["floor"])
        dnum = _heads(dh_ref) * rden
        hdh = jnp.sum(f["num"] * dnum, axis=-1, keepdims=True)
        dden = jnp.where(big, -hdh * rden * jnp.sign(f["den"]), 0.0)
        dnum_b = dnum.astype(BF16)
        dsc = _bdot(dnum_b, f["vb"], 2, 2) + dden
        g = dsc * f["sc"]
        dv = _bdot_rows(f["sc"].astype(BF16), dnum_b)
        dqk_ = (dsc * f["w_intra"]).astype(BF16)
        dq = _bdot(dqk_, f["kb"], 2, 1)
        dks = _bdot_rows(dqk_, f["qb"])
        wdn = f["w_inter"] * dnum
        wdn_b = wdn.astype(BF16)
        wdd = f["w_inter"] * dden
        u = jnp.sum(f["qc"] * wdn, axis=-1, keepdims=True) + wdd * f["qn"]
        dks_s, dv_s, z_s, dg_s = [None, None], [None, None], [None, None], [None, None]
        dcn, dnn = dc_st[...], dn_st[...]
        for ch in (1, 0):
            rows = slice(L * ch, L * ch + L)
            _, _, _, decay, e_a, kw = _mlstm_update(f, ch, c_prev[ch], n_prev[ch], m_prev[ch])
            dcn_b = dcn.astype(BF16)
            dkw = _bdot(f["vb"][:, rows], dcn_b, 2, 2) + dnn
            dks_s[ch] = e_a * dkw
            dv_s[ch] = _bdot(kw.astype(BF16), dcn_b, 2, 1)
            z_s[ch] = e_a * jnp.sum(f["ks"][:, rows] * dkw, axis=-1, keepdims=True)
            dg_s[ch] = jnp.sum(z_s[ch], axis=1, keepdims=True) + decay * (
                jnp.sum(c_prev[ch] * dcn, axis=(1, 2), keepdims=True)
                + jnp.sum(n_prev[ch] * dnn, axis=(1, 2), keepdims=True))
            dcn = decay * dcn + _bdot_rows(f["qb"][:, rows], wdn_b[:, rows])
            dnn = decay * dnn + jnp.sum(wdd[:, rows] * f["q"][:, rows], axis=1, keepdims=True)
        dc_st[...], dn_st[...] = dcn, dnn
        dq = dq + jnp.concatenate(
            [_bdot(wdn_b[:, :L], c_prev[0].astype(BF16), 2, 2) + wdd[:, :L] * n_prev[0],
             _bdot(wdn_b[:, L:], c_prev[1].astype(BF16), 2, 2) + wdd[:, L:] * n_prev[1]], axis=1)
        dks = (dks + jnp.concatenate(dks_s, axis=1)) * (D ** -0.5)
        dv = dv + jnp.concatenate(dv_s, axis=1)
        z = jnp.concatenate(z_s, axis=1)
        row = lax.broadcasted_iota(jnp.int32, (1, STEP_ROWS, 1), 1)
        dg_col = jnp.where(row == L - 1, dg_s[0], 0.0) + jnp.where(row == 2 * L - 1, dg_s[1], 0.0)
        db_col = jnp.sum(g, axis=-1, keepdims=True) + u - z + dg_col
        g_row = jnp.sum(g, axis=1, keepdims=True)
        lane = lax.broadcasted_iota(jnp.int32, (STEP_ROWS, 128), 1)
        sub = lax.broadcasted_iota(jnp.int32, (8, STEP_ROWS), 0)
        dgc = jnp.zeros((STEP_ROWS, 128), F32)
        dgr = jnp.zeros((8, STEP_ROWS), F32)
        for hd in range(H):
            dgc = dgc + jnp.where(lane == hd, z[hd], 0.0) + jnp.where(lane == H + hd, db_col[hd], 0.0)
            dgr = dgr + jnp.where(sub == hd, g_row[hd], 0.0) - jnp.where(sub == H + hd, g_row[hd], 0.0)
            dqk_ref[:, D * hd:D * hd + D] = dq[hd]
            dqk_ref[:, H * D + D * hd:H * D + D * hd + D] = dks[hd]
            dv_ref[:, D * hd:D * hd + D] = dv[hd].astype(BF16)
        dgc_ref[...] = dgc
        dgr_ref[...] = dgr

    return pl.pallas_call(
        body, name="mlstm_bwd", grid=(steps,),
        in_specs=_mlstm_specs(T, rev) + [
            pl.BlockSpec((2, H, 128, 128), lambda s: (rev(s), 0, 0, 0)),
            pl.BlockSpec((2, H, 1, 128), lambda s: (rev(s), 0, 0, 0)),
            pl.BlockSpec((2, H, 1, 128), lambda s: (rev(s), 0, 0, 0)),
            pl.BlockSpec((STEP_ROWS, 512), lambda s: (rev(s), 0))],
        out_specs=[pl.BlockSpec((STEP_ROWS, 1024), lambda s: (rev(s), 0)),
                   pl.BlockSpec((STEP_ROWS, 512), lambda s: (rev(s), 0)),
                   pl.BlockSpec((STEP_ROWS, 128), lambda s: (rev(s), 0)),
                   pl.BlockSpec((8, STEP_ROWS), lambda s: (0, rev(s)))],
        out_shape=[_sds((T, 1024), F32), _sds((T, 512), BF16), _sds((T, 128), F32), _sds((8, T), F32)],
        scratch_shapes=[pltpu.VMEM((H, 128, 128), F32), pltpu.VMEM((H, 1, 128), F32)],
        compiler_params=_params(("arbitrary",)))(qk, qk, pm, gcol, bcol, grow, brow, cs, ns, ms, dh)


def _rows_to_lanes(x):
    eye = (lax.broadcasted_iota(jnp.int32, (8, 128), 0)
           == lax.broadcasted_iota(jnp.int32, (8, 128), 1)).astype(BF16)
    out, rest = None, x
    for _ in range(3):
        piece = rest.astype(BF16)
        rest = rest - piece.astype(F32)
        t = _dot(piece, eye, 0, 0)
        out = t if out is None else out + t
    return out


def _gate_bwd(dgc, dgr, gcol, bcol):
    T = dgc.shape[0]
    tr = min(ROW_TILE, T)

    def body(a_ref, b_ref, g_ref, bias_ref, o_ref, acc_ref):
        i = pl.program_id(0)

        @pl.when(i == 0)
        def _():
            acc_ref[...] = jnp.zeros_like(acc_ref)

        d = a_ref[...] + _rows_to_lanes(b_ref[:, pl.ds(pl.multiple_of(i * tr, 128), tr)])
        lane = lax.broadcasted_iota(jnp.int32, d.shape, 1)
        is_f = (lane >= MLSTM_HEADS) & (lane < 2 * MLSTM_HEADS)
        dlogf = _chunk_rev_cumsum(jnp.where(is_f, d, 0.0), 0)
        out = jnp.where(is_f, dlogf * _sigmoid(-(g_ref[...] + bias_ref[...])), d)
        o_ref[...] = out.astype(BF16)
        acc_ref[0:1, :] += _colsum(out)

    return _rows("gate_bwd", body, [dgc, dgr, gcol, bcol],
                 [_sds((T, 128), BF16), _sds((8, 128), F32)], T, tr=tr)


def _head_norm(h, mu_axis=-1):
    mu = jnp.mean(h, axis=-1, keepdims=True)
    hc = h - mu
    r = lax.rsqrt(jnp.mean(hc * hc, axis=-1, keepdims=True) + NORM_EPS)
    return hc * r, r


def _mlstm_out(hm, pm, w):
    T = hm.shape[0]
    D = MLSTM_HEAD_DIM

    def body(h_ref, o_ref, w_ref, y_ref):
        for hd in range(MLSTM_HEADS):
            cols = slice(D * hd, D * hd + D)
            hn, _ = _head_norm(h_ref[:, cols])
            y_ref[:, cols] = (_sigmoid(o_ref[:, cols].astype(F32)) * hn * w_ref[:, cols]).astype(BF16)

    tr = min(ROW_TILE, T)
    return pl.pallas_call(
        body, name="mlstm_out", grid=(T // tr,),
        in_specs=[pl.BlockSpec((tr, 512), lambda i: (i, 0)), pl.BlockSpec((tr, 512), lambda i: (i, 1)),
                  pl.BlockSpec((1, 512), lambda i: (0, 0))],
        out_specs=pl.BlockSpec((tr, 512), lambda i: (i, 0)), out_shape=_sds((T, 512), BF16),
        compiler_params=_params(("parallel",)))(hm, pm, w)


def _mlstm_out_bwd_rows(ps, es):
    hm, vo, w_all = es
    D, width = MLSTM_HEAD_DIM, MLSTM_HEADS * MLSTM_HEAD_DIM
    dhs, dos, dws = [], [], []
    for hd in range(MLSTM_HEADS):
        cols = slice(D * hd, D * hd + D)
        hn, r = _head_norm(hm[:, cols])
        sg = _sigmoid(vo[:, width + D * hd:width + D * hd + D].astype(F32))
        dy, w = ps[0][:, cols], w_all[:, cols]
        dos.append(dy * hn * w * sg * (1.0 - sg))
        dyn = dy * sg
        dws.append(_colsum(dyn * hn))
        dhn = dyn * w
        dhs.append(r * (dhn - jnp.mean(dhn, axis=-1, keepdims=True)
                        - hn * jnp.mean(dhn * hn, axis=-1, keepdims=True)))
    cat = lambda parts: jnp.concatenate(parts, axis=1)
    return [cat(dhs), cat(dos)], [_acc_rows([cat(dws)])]


ADAM_TILE_ELEMS = 256 * 1024


def _adamw(name, w, g, m, v):
    R, C = w.shape
    fits = [t for t in range(8, R + 1, 8) if R % t == 0 and t * C <= ADAM_TILE_ELEMS]
    if fits or R * C <= ADAM_TILE_ELEMS:
        tr = fits[-1] if fits else R
        spec, grid = pl.BlockSpec((tr, C), lambda i: (i, 0)), (R // tr,)
    else:
        spec, grid = pl.BlockSpec((R, 128), lambda i: (0, i)), (C // 128,)
    c1 = 1.0 - ADAM_B1 ** ADAM_STEP
    c2 = 1.0 - ADAM_B2 ** ADAM_STEP

    def body(w_ref, g_ref, m_ref, v_ref, d_ref, mo_ref, vo_ref):
        g = g_ref[...]
        m = ADAM_B1 * m_ref[...] + (1.0 - ADAM_B1) * g
        v = ADAM_B2 * v_ref[...] + (1.0 - ADAM_B2) * (g * g)
        mo_ref[...] = m
        vo_ref[...] = v
        d_ref[...] = -ADAM_LR * ((m / c1) / (jnp.sqrt(v / c2) + ADAM_EPS) + ADAM_WD * w_ref[...])

    return pl.pallas_call(
        body, name=name, grid=grid, in_specs=[spec] * 4, out_specs=[spec] * 3,
        out_shape=[_sds((R, C), F32)] * 3, compiler_params=_params(("parallel",)))(w, g, m, v)


def _place():
    return lax.axis_index("x"), lax.axis_index("y"), lax.axis_index("c")


def _all_gather8(name, blk, space):
    m, n = blk.shape

    def body(x_ref, out_ref, send_sems, recv_sems, local_sem):
        x, y, c = _place()
        me, sibling = (x, y, c), (x, y, 1 - c)
        chips = [(1 - x, y), (x, 1 - y), (1 - x, 1 - y)]

        def rows(px, py, pc):
            return out_ref.at[pl.ds((4 * px + 2 * py + pc) * m, m), :]

        def copy(k, block, to, src=None):
            return pltpu.make_async_remote_copy(
                src_ref=rows(*block) if src is None else src, dst_ref=rows(*block),
                send_sem=send_sems.at[k], recv_sem=recv_sems.at[k],
                device_id=to, device_id_type=MESH)

        mine = pltpu.make_async_copy(x_ref, rows(*me), local_sem)
        mine.start()
        first = [copy(0, me, sibling, src=x_ref)]
        first += [copy(1 + j, me, (*chip, c), src=x_ref) for j, chip in enumerate(chips)]
        for cp in first:
            cp.start()
        passed = [copy(4 + j, (*chip, c), sibling) for j, chip in enumerate(chips)]
        for j, chip in enumerate(chips):
            copy(1 + j, (*chip, c), me).wait_recv()
            passed[j].start()
        copy(0, sibling, me).wait_recv()
        for j, chip in enumerate(chips):
            copy(4 + j, (*chip, 1 - c), me).wait_recv()
        for cp in first + passed:
            cp.wait_send()
        mine.wait()

    return pl.pallas_call(
        body, name=name, out_shape=_sds((8 * m, n), blk.dtype),
        in_specs=[pl.BlockSpec(memory_space=space)], out_specs=pl.BlockSpec(memory_space=space),
        scratch_shapes=[pltpu.SemaphoreType.DMA((7,)), pltpu.SemaphoreType.DMA((7,)),
                        pltpu.SemaphoreType.DMA],
        compiler_params=pltpu.CompilerParams(vmem_limit_bytes=VMEM_LIMIT))(blk)


def _hbm_specs(n):
    return [pl.BlockSpec(memory_space=pl.ANY)] * n


def _swap_halves_sibling(name, srcs):
    nw = len(srcs)

    def body(*refs):
        src_refs, dst_refs, send_sems, recv_sems = refs[:nw], refs[nw:2 * nw], refs[2 * nw], refs[2 * nw + 1]
        x, y, c = _place()
        cps = [pltpu.make_async_remote_copy(
            src_ref=src_refs[w].at[pl.ds(0, 4), 1 - c], dst_ref=dst_refs[w],
            send_sem=send_sems.at[w], recv_sem=recv_sems.at[w], device_id=(x, y, 1 - c),
            device_id_type=MESH) for w in range(nw)]
        for cp in cps:
            cp.start()
        for cp in cps:
            cp.wait()

    return pl.pallas_call(
        body, name=name, out_shape=[_sds(s.shape[:1] + s.shape[2:], s.dtype) for s in srcs],
        in_specs=_hbm_specs(nw), out_specs=_hbm_specs(nw),
        scratch_shapes=[pltpu.SemaphoreType.DMA((nw,)), pltpu.SemaphoreType.DMA((nw,))])(*srcs)


def _split_start(name, srcs, lands, copies, per_array, after):
    nw = len(srcs)

    def body(*refs):
        send_sems, recv_sems, token = refs[2 * nw + 1], refs[2 * nw + 2], refs[-1]
        for w in range(nw):
            for k, (s, d, dev) in enumerate(copies(refs[w], refs[nw + w], *_place())):
                pltpu.make_async_remote_copy(
                    src_ref=s, dst_ref=d, send_sem=send_sems.at[w * per_array + k],
                    recv_sem=recv_sems.at[w * per_array + k], device_id=dev, device_id_type=MESH).start()
        token[...] = jnp.zeros_like(token)

    hbm, sem = pl.BlockSpec(memory_space=pltpu.HBM), pl.BlockSpec(memory_space=pltpu.SEMAPHORE)
    arrays = list(srcs) + list(lands)
    out = pl.pallas_call(
        body, name=name,
        out_shape=(pltpu.SemaphoreType.DMA((nw * per_array,)), pltpu.SemaphoreType.DMA((nw * per_array,)),
                   *[pltpu.HBM(a.shape, a.dtype) for a in arrays], _sds((8, 128), F32)),
        in_specs=[hbm] * (2 * nw) + [pl.BlockSpec(memory_space=pl.ANY)],
        out_specs=(sem, sem, *[hbm] * (2 * nw), pl.BlockSpec(memory_space=pltpu.VMEM)),
        input_output_aliases={i: 2 + i for i in range(2 * nw)},
        compiler_params=pltpu.CompilerParams(has_side_effects=pltpu.SideEffectType.DATAFLOW_SIDE_EFFECTING))(
            *[pltpu.with_memory_space_constraint(a, pltpu.HBM) for a in arrays], after)
    return out[0], out[1], out[2:2 + nw], out[2 + nw:2 + 2 * nw], out[-1]


def _split_wait(name, started, after, waits, per_array):
    send_sems, recv_sems, srcs, lands, _ = started
    nw = len(srcs)

    def body(*refs):
        send_sems, recv_sems = refs[2 * nw], refs[2 * nw + 1]
        x, y, c = _place()
        for w in range(nw):
            for k, (s, d) in enumerate(waits(refs[w], refs[nw + w], x, y, c)):
                cp = pltpu.make_async_remote_copy(
                    src_ref=s, dst_ref=d, send_sem=send_sems.at[w * per_array + k],
                    recv_sem=recv_sems.at[w * per_array + k], device_id=(x, y, 1 - c),
                    device_id_type=MESH)
                cp.wait_send()
                cp.wait_recv()

    hbm, sem = pl.BlockSpec(memory_space=pltpu.HBM), pl.BlockSpec(memory_space=pltpu.SEMAPHORE)
    arrays = list(srcs) + list(lands)
    out = pl.pallas_call(
        body, name=name, out_shape=tuple(pltpu.HBM(a.shape, a.dtype) for a in arrays),
        in_specs=[hbm] * (2 * nw) + [sem, sem, pl.BlockSpec(memory_space=pl.ANY)],
        out_specs=tuple([hbm] * (2 * nw)), input_output_aliases={i: i for i in range(2 * nw)},
        compiler_params=pltpu.CompilerParams(has_side_effects=pltpu.SideEffectType.DATAFLOW_SIDE_EFFECTING))(
            *arrays, send_sems, recv_sems, after)
    return list(out[nw:])


def _other_chips(x, y):
    return [(1 - x, y), (x, 1 - y), (1 - x, 1 - y)]


def _gather_sends(src_ref, land_ref, x, y, c):
    to = land_ref.at[2 * x + y, c]
    return [(src_ref, to, (x, y, 1 - c))] + [(src_ref, to, (px, py, c)) for px, py in _other_chips(x, y)]


def _gather_lands(src_ref, land_ref, x, y, c):
    return [(src_ref, land_ref.at[2 * x + y, 1 - c])] + [
        (src_ref, land_ref.at[2 * px + py, c]) for px, py in _other_chips(x, y)]


def _gather_sends_all(src_ref, land_ref, x, y, c):
    to = land_ref.at[2 * x + y, c]
    return [(src_ref, to, (x, y, 1 - c))] + [
        (src_ref, to, (px, py, pc)) for px, py in _other_chips(x, y) for pc in (c, 1 - c)]


def _gather_lands_all(src_ref, land_ref, x, y, c):
    return [(src_ref, land_ref.at[2 * x + y, 1 - c])] + [
        (src_ref, land_ref.at[2 * px + py, pc]) for px, py in _other_chips(x, y) for pc in (c, 1 - c)]


def _scatter_sends(src_ref, land_ref, x, y, c):
    return [(src_ref.at[2 * px + py], land_ref.at[2 * x + y], (px, py, c)) for px, py in _other_chips(x, y)]


def _scatter_lands(src_ref, land_ref, x, y, c):
    return [(src_ref.at[2 * x + y], land_ref.at[2 * px + py]) for px, py in _other_chips(x, y)]


def _forward_sibling(name, lands):
    nw = len(lands)

    def body(*refs):
        land_refs, out_refs, send_sems, recv_sems = refs[:nw], refs[nw:2 * nw], refs[2 * nw], refs[2 * nw + 1]
        x, y, c = _place()
        cps = []
        for w in range(nw):
            cps += [pltpu.make_async_remote_copy(
                src_ref=land_refs[w].at[2 * px + py, c], dst_ref=out_refs[w].at[2 * px + py, c],
                send_sem=send_sems.at[w, j], recv_sem=recv_sems.at[w, j], device_id=(x, y, 1 - c),
                device_id_type=MESH) for j, (px, py) in enumerate(_other_chips(x, y))]
        for cp in cps:
            cp.start()
        for w in range(nw):
            for j, (px, py) in enumerate(_other_chips(x, y)):
                slot = out_refs[w].at[2 * px + py, 1 - c]
                pltpu.make_async_remote_copy(src_ref=slot, dst_ref=slot, send_sem=send_sems.at[w, j],
                                             recv_sem=recv_sems.at[w, j], device_id=(x, y, 1 - c),
                                             device_id_type=MESH).wait_recv()
        for cp in cps:
            cp.wait_send()

    return pl.pallas_call(
        body, name=name, out_shape=[_sds(a.shape, a.dtype) for a in lands],
        in_specs=_hbm_specs(nw), out_specs=_hbm_specs(nw), input_output_aliases={i: i for i in range(nw)},
        scratch_shapes=[pltpu.SemaphoreType.DMA((nw, 3)), pltpu.SemaphoreType.DMA((nw, 3))])(*lands)


def _share_halves(name, halves):
    nw = len(halves)

    def body(*refs):
        in_refs, out_refs, send_sems, recv_sems = refs[:nw], refs[nw:2 * nw], refs[2 * nw], refs[2 * nw + 1]
        x, y, c = _place()
        cps = [pltpu.make_async_remote_copy(
            src_ref=in_refs[w].at[c], dst_ref=out_refs[w].at[c], send_sem=send_sems.at[w],
            recv_sem=recv_sems.at[w], device_id=(x, y, 1 - c), device_id_type=MESH) for w in range(nw)]
        for cp in cps:
            cp.start()
        for w in range(nw):
            slot = out_refs[w].at[1 - c]
            pltpu.make_async_remote_copy(src_ref=slot, dst_ref=slot, send_sem=send_sems.at[w],
                                         recv_sem=recv_sems.at[w], device_id=(x, y, 1 - c),
                                         device_id_type=MESH).wait_recv()
        for cp in cps:
            cp.wait_send()

    return pl.pallas_call(
        body, name=name, out_shape=[_sds(a.shape, a.dtype) for a in halves],
        in_specs=_hbm_specs(nw), out_specs=_hbm_specs(nw), input_output_aliases={i: i for i in range(nw)},
        scratch_shapes=[pltpu.SemaphoreType.DMA((nw,)), pltpu.SemaphoreType.DMA((nw,))])(*halves)


def _place_blocks(name, blks, place):
    nw = len(blks)

    def body(p_ref, *refs):
        for b_ref, o_ref in zip(refs[:nw], refs[nw:]):
            o_ref[...] = b_ref[...]

    return pl.pallas_call(
        body, name=name,
        grid_spec=pltpu.PrefetchScalarGridSpec(
            num_scalar_prefetch=1, grid=(1,),
            in_specs=[pl.BlockSpec(b.shape, lambda i, p: (0, 0)) for b in blks],
            out_specs=[pl.BlockSpec((None, None) + b.shape, lambda i, p: (p[0], p[1], 0, 0)) for b in blks]),
        out_shape=[_sds((4, 2) + b.shape, b.dtype) for b in blks],
        compiler_params=_params(("arbitrary",)))(place, *blks)


def _pair_sum(name, fulls, gots, place):
    nw = len(fulls)

    def body(p_ref, *refs):
        s = pl.program_id(0)
        for a_ref, b_ref, o_ref, l_ref in zip(refs[:nw], refs[nw:2 * nw], refs[2 * nw:3 * nw], refs[3 * nw:]):
            o_ref[...] = (a_ref[...].astype(F32) + b_ref[...].astype(F32)).astype(o_ref.dtype)

            @pl.when(s == p_ref[0])
            def _():
                l_ref[...] = o_ref[...]

    slab = lambda a: pl.BlockSpec((None,) + a.shape[1:], lambda s, p: (s, 0, 0))
    mine = lambda a: pl.BlockSpec((None,) + a.shape[1:], lambda s, p: (p[0], 0, 0))
    out = pl.pallas_call(
        body, name=name,
        grid_spec=pltpu.PrefetchScalarGridSpec(
            num_scalar_prefetch=1, grid=(4,),
            in_specs=[pl.BlockSpec((None, None) + a.shape[2:], lambda s, p: (s, p[1], 0, 0)) for a in fulls]
            + [slab(b) for b in gots],
            out_specs=[slab(b) for b in gots] + [mine(b) for b in gots]),
        out_shape=[_sds(b.shape, BF16) for b in gots] * 2,
        compiler_params=_params(("arbitrary",)))(place, *fulls, *gots)
    return out[:nw], out[nw:]


def _sum4(name, arrs, place):
    nw = len(arrs)

    def body(p_ref, *refs):
        for a_ref, o_ref in zip(refs[:nw], refs[nw:]):
            acc = a_ref[0].astype(F32)
            for s in range(1, 4):
                acc = acc + a_ref[s].astype(F32)
            o_ref[...] = acc

    return pl.pallas_call(
        body, name=name,
        grid_spec=pltpu.PrefetchScalarGridSpec(
            num_scalar_prefetch=1, grid=(1,),
            in_specs=[pl.BlockSpec(a.shape, lambda i, p: (0, 0, 0)) for a in arrs],
            out_specs=[pl.BlockSpec((None,) + a.shape[1:], lambda i, p: (p[1], 0, 0)) for a in arrs]),
        out_shape=[_sds((2,) + a.shape[1:], F32) for a in arrs],
        compiler_params=_params(("arbitrary",)))(place, *arrs)


def _small_update(gathered, params, slots):
    c1 = 1.0 - ADAM_B1 ** ADAM_STEP
    c2 = 1.0 - ADAM_B2 ** ADAM_STEP
    n = len(params)

    def body(g_ref, *refs):
        ins, sum_ref, outs = refs[:3 * n], refs[3 * n], refs[3 * n + 1:]
        g_all = g_ref[0:1, :]
        for d in range(1, 8):
            g_all = g_all + g_ref[d:d + 1, :]
        sum_ref[...] = g_all
        for k, (off, width) in enumerate(slots):
            w_ref, m_ref, v_ref = ins[3 * k:3 * k + 3]
            go_ref, d_ref, mo_ref, vo_ref = outs[4 * k:4 * k + 4]
            g = g_all[:, off:off + width]
            m = ADAM_B1 * m_ref[...] + (1.0 - ADAM_B1) * g
            v = ADAM_B2 * v_ref[...] + (1.0 - ADAM_B2) * (g * g)
            go_ref[...], mo_ref[...], vo_ref[...] = g, m, v
            d_ref[...] = -ADAM_LR * ((m / c1) / (jnp.sqrt(v / c2) + ADAM_EPS) + ADAM_WD * w_ref[...])

    flat = [a for p in params for a in p]
    out = pl.pallas_call(
        body, name="small_update",
        out_shape=[_sds((1, gathered.shape[1]), F32)] + [_sds(p[0].shape, F32) for p in params for _ in range(4)],
        compiler_params=pltpu.CompilerParams(vmem_limit_bytes=VMEM_LIMIT))(gathered, *flat)
    return out[0], [tuple(out[1 + 4 * k:5 + 4 * k]) for k in range(n)]


def _swiglu(ps, es):
    g, u = ps
    return g * _sigmoid(g) * u, g, u


def _swiglu_bwd(ps, es):
    g, u = es[0].astype(F32), es[1].astype(F32)
    sg = _sigmoid(g)
    return ps[0] * u * (sg * (1.0 + g * (1.0 - sg))), ps[0] * (g * sg)


def _merge(ps, es):
    ga, gm = [e.astype(F32) for e in es]
    return (_sigmoid(ga) * ps[0] + _sigmoid(gm) * ps[1],)


def _merge_bwd(ps, es):
    dm, a, b = ps
    ga, gm = [e.astype(F32) for e in es]
    sa, sm = _sigmoid(ga), _sigmoid(gm)
    return dm * sa, dm * sm, dm * a * (sa * (1.0 - sa)), dm * b * (sm * (1.0 - sm))


W_IN_PIECES = (("q", 512), ("kv", 256), ("mqk", 1024), ("mv", 512), ("mo", 512), ("if", 8),
               ("ga", 1024), ("gm", 1024))


def _local_step(x, tgt, pos_col, mod, sp, in_weights, late_weights, ffn_grads, mixer_grads):
    sh_m, sc_m, gate_m, sh_f, sc_f, gate_f = mod
    inv = ROPE_THETA ** (-2.0 * jnp.arange(HEAD_DIM // 2, dtype=F32) / HEAD_DIM)
    cos, sin = _rope_tables(pos_col, jnp.tile(inv, 4).reshape(1, 128))
    W = dict(in_weights(cos))
    h, pa, pqk, pvo, pif, pg = _proj_in(x, sp["g_pre_mix"], sc_m, sh_m, [
        (W["q+kv"], F32, 256), (W["mqk"], F32, 512), (W["mv+mo"], BF16, 512), (W["if"], F32, 128),
        (W["ga+gm"], BF16, 512)])
    ya = _attn_fwd(pa, cos, sin, sp["sinks"])
    qk = _conv_fwd(pqk, sp["conv_w"], sp["conv_b"])
    bcol = jnp.pad(sp["b_if"], ((0, 0), (0, 120)))
    brow = jnp.broadcast_to(sp["b_if"].reshape(8, 1), (8, 128))
    grow = pif[:, :8].T
    hm, cs, ns, ms = _mlstm_fwd(qk, pvo, pif, bcol, grow, brow)
    ym = _mlstm_out(hm, pvo, sp["norm_w"])
    W.update(late_weights(ym))
    w_fg, w_fu, w_fd = W["fg"], W["fu"], W["fd"]
    merged, = _mm("branches", [[(ya, W["ba"])], [(ym, W["bm"])]], [(pg, 0), (pg, 1)], _merge, [BF16],
                  cn=512, nt=True)
    wide, narrow = (D_MODEL, F32), (D_MODEL, BF16)
    mix, x1, h2 = _mm_rows("mix_out", [[(merged, W["out"])]],
                           [x, gate_m, sp["g_post_mix"], sp["g_pre_ffn"], sc_f, sh_f],
                           _res_norm_rows, [wide, wide, narrow], [], cn=512)
    act, gt, up = _mm("ffn_in", [[(h2, w_fg)], [(h2, w_fu)]], [], _swiglu, [BF16] * 3,
                      cn=256, nt=True)
    dy, dff, acc_l, loss = _mm_rows("ffn_down", [[(act, w_fd)]], [x1, tgt, gate_f, sp["g_post_ffn"]],
                                    _final_loss_rows, [wide, narrow], [(8, D_MODEL), (1, 128)], cn=512)

    G = {}
    dgt, dup = _mm("ffn_down_bwd", [[(dff, w_fd)]], [gt, up], _swiglu_bwd, [BF16, BF16],
                   cn=256, nt=True)
    g_fd, = _mm_tn_group("dw_ffn_down", [act], dff, BF16)
    g_fg, = _mm_tn_group("dw_ffn_gate", [dgt], h2, BF16)
    g_fu, = _mm_tn_group("dw_ffn_up", [dup], h2, BF16)
    tie = ffn_grads(g_fg, g_fu, g_fd)
    dx1, dmix, acc_r = _mm_rows(
        "ffn_in_bwd", [[(dgt, w_fg), (dup, w_fu)]],
        [x1, mix, dy, sc_f + tie, gate_m, sp["g_pre_ffn"], sp["g_post_mix"]],
        _res_norm_bwd_rows, [wide, narrow], [(8, D_MODEL)], cn=512, tm=256)
    d_a, d_m, dga, dgm = _mm("mix_out_bwd", [[(dmix, W["out"])], [(ya, W["ba"])], [(ym, W["bm"])]],
                             [(pg, 0), (pg, 1)], _merge_bwd, [BF16] * 4, cn=512, nt=True)
    G["out"], = _mm_tn_group("dw_out", [merged], dmix, BF16)
    dya, = _mm("branch_attn_bwd", [[(d_a, W["ba"])]], [], _first, [F32], cn=512)
    heads = MLSTM_HEADS * MLSTM_HEAD_DIM
    dhm, do_m, acc_n = _mm_rows("branch_mlstm_bwd", [[(d_m, W["bm"])]], [hm, pvo, sp["norm_w"]],
                                _mlstm_out_bwd_rows, [(heads, F32), (heads, BF16)], [(8, heads)], cn=512)
    G["ba"], = _mm_tn_group("dw_branch_attn", [d_a], ya, BF16)
    G["bm"], = _mm_tn_group("dw_branch_mlstm", [d_m], ym, BF16)
    dqk, dv_m, dgc, dgr = _mlstm_bwd(qk, pvo, pif, bcol, grow, brow, cs, ns, ms, dhm)
    dif, acc_g = _gate_bwd(dgc, dgr, pif, bcol)
    du, acc_c = _conv_bwd(pqk, sp["conv_w"], sp["conv_b"], dqk)
    dq_a, dkv, dsink = _attn_bwd(pa, cos, sin, sp["sinks"], dya)
    dproj = {"q": dq_a, "kv": dkv, "mqk": du, "mv": dv_m, "mo": do_m, "if": dif, "ga": dga, "gm": dgm}
    names = [k for k, _ in W_IN_PIECES]
    for part in (names[:4], names[4:]):
        G.update(zip(part, _mm_tn_group("dw_in_from_" + part[0], [dproj[k] for k in part], h, BF16)))
    w_tied = dict(W, **{"if": W["if"] + mixer_grads(G).astype(BF16)})
    dx, acc_p = _mm_rows("proj_bwd", [[(dproj[k], w_tied[k]) for k, _ in W_IN_PIECES]],
                         [x, dx1, sp["g_pre_mix"], sc_m], _pre_norm_bwd_rows, [wide], [(8, D_MODEL)], cn=512)

    small = {
        "mod": jnp.concatenate([acc_p[1], acc_p[0], acc_r[3], acc_r[1], acc_r[0], acc_l[0]]),
        "g_pre_mix": acc_p[2], "g_post_mix": acc_r[4], "b_if": acc_g[0, :8],
        "conv_w": acc_c[:CONV_WIDTH].reshape(-1), "conv_b": acc_c[CONV_WIDTH],
        "sinks": dsink[:, 0], "norm_w": acc_n[0], "g_pre_ffn": acc_r[2], "g_post_ffn": acc_l[1]}
    return loss, dx, small


IN_WIDTH = sum(n for _, n in W_IN_PIECES)
IN_SHARD = IN_WIDTH // 4
IN_SHARD_PAD = -(-IN_SHARD // 32) * 32


def _chip_segments(lo, hi):
    return [(s, max(lo, IN_SHARD * s) - IN_SHARD * s, min(hi, IN_SHARD * (s + 1)) - IN_SHARD * s)
            for s in range(4) if max(lo, IN_SHARD * s) < min(hi, IN_SHARD * (s + 1))]


def _split_w_in(shards):
    rows = lambda lo, hi: jnp.concatenate([shards[s, a:b] for s, a, b in _chip_segments(lo, hi)])
    out, off, start = {}, 0, {}
    for k, n in W_IN_PIECES:
        out[k], start[k] = rows(off, off + n), off
        off += n
    out["if"] = jnp.pad(out["if"], ((0, 120), (0, 0)))
    for name, first, last in (("q+kv", "q", "kv"), ("mv+mo", "mv", "mo"), ("ga+gm", "ga", "gm")):
        out[name] = rows(start[first], start[last] + dict(W_IN_PIECES)[last])
    return out


def _join_w_in(pieces):
    rows = jnp.concatenate([pieces[k][:n] for k, n in W_IN_PIECES]).reshape(4, IN_SHARD, -1)
    return jnp.pad(rows, ((0, 0), (0, IN_SHARD_PAD - IN_SHARD), (0, 0)))


def _halves(a):
    return a.reshape(4, 2, a.shape[0] // 8, a.shape[1])


SMALL = (("b_ada", 6144), ("g_pre_mix", 1024), ("g_post_mix", 1024), ("b_if", 128), ("conv_w", 4096),
         ("conv_b", 1024), ("sinks", 128), ("norm_w", 512), ("g_pre_ffn", 1024), ("g_post_ffn", 1024))
SMALL_LEN = 8 * 2048


def _pack_small(vals):
    parts = []
    for k, n in SMALL:
        v = vals[k].reshape(-1)
        parts.append(jnp.pad(v, (0, n - v.shape[0])))
    flat = jnp.concatenate(parts)
    return jnp.pad(flat, (0, SMALL_LEN - flat.shape[0]))


def kernel(x, c, positions, w_ada, b_ada, g_pre_mix, g_post_mix, w_in, b_if, conv_w, conv_b, attn_sinks, mlstm_norm_w, w_branch_attn, w_branch_mlstm, w_out, g_pre_ffn, g_post_ffn, w_ffn_gate, w_ffn_up, w_ffn_down, loss_target, m_w_ada, m_b_ada, m_g_pre_mix, m_g_post_mix, m_w_in, m_b_if, m_conv_w, m_conv_b, m_attn_sinks, m_mlstm_norm_w, m_w_branch_attn, m_w_branch_mlstm, m_w_out, m_g_pre_ffn, m_g_post_ffn, m_w_ffn_gate, m_w_ffn_up, m_w_ffn_down, v_w_ada, v_b_ada, v_g_pre_mix, v_g_post_mix, v_w_in, v_b_if, v_conv_w, v_conv_b, v_attn_sinks, v_mlstm_norm_w, v_w_branch_attn, v_w_branch_mlstm, v_w_out, v_g_pre_ffn, v_g_post_ffn, v_w_ffn_gate, v_w_ffn_up, v_w_ffn_down):
    xi, yi, ci = _place()
    chip = 2 * xi + yi
    dev = 2 * chip + ci
    T = x.shape[1]
    ada_cols = w_ada.shape[2]

    place = jnp.stack([chip, ci]).astype(jnp.int32)

    def my_half(a):
        n = a.shape[0] // 2
        return lax.dynamic_slice_in_dim(a, ci * n, n, axis=0).astype(BF16)

    blk = jnp.concatenate([c.reshape(-1), conv_w.reshape(-1)]).reshape(8, 256)
    got = _all_gather8("gather_cond", blk, pltpu.VMEM).reshape(8, 2048)
    c_all = got[:, :D_MODEL].astype(BF16)
    conv_full = got[::2, D_MODEL:].reshape(4, CONV_WIDTH, -1).transpose(1, 0, 2).reshape(CONV_WIDTH, -1)

    b_sh = lax.dynamic_slice_in_dim(b_ada, chip * ada_cols, ada_cols, axis=1)
    mod_part, = _mm("ada_mod", [[(c_all, w_ada[0].astype(BF16))]], [b_sh],
                    lambda ps, es: (ps[0] + es[0],), [F32], cn=512, tm=8)
    mod_all = _all_gather8("gather_mod", mod_part, pltpu.VMEM).reshape(4, 2, 8, ada_cols)[:, 0]
    mod = lax.dynamic_index_in_dim(mod_all, dev, axis=1, keepdims=False).reshape(6, 1, D_MODEL)

    def gather_start(name, blks, after, sends, copies):
        return _split_start(name + "_start", blks, _place_blocks(name + "_place", blks, place),
                            sends, copies, after)

    w_in_t = jnp.pad(w_in[0].T, ((0, IN_SHARD_PAD - IN_SHARD), (0, 0)))
    in_started = gather_start("in_gather", [my_half(w_in_t)], mod, _gather_sends, 4)
    late_keys = ("fg", "fu", "fd", "out", "ba", "bm")
    late_started = gather_start(
        "late_gather",
        [my_half(w_ffn_gate[0].T), my_half(w_ffn_up[0].T), my_half(w_ffn_down[0]), my_half(w_out[0]),
         my_half(w_branch_attn[0].T), my_half(w_branch_mlstm[0].T)], in_started[4], _gather_sends_all, 7)
    mod = mod + (in_started[4][0, 0] + late_started[4][0, 0])

    def in_weights(after):
        g_in, = _forward_sibling("in_gather_forward",
                                 _split_wait("in_gather_wait", in_started, after, _gather_lands, 4))
        return _split_w_in(g_in.reshape(4, IN_SHARD_PAD, D_MODEL))

    def late_weights(after):
        lands = _split_wait("late_gather_wait", late_started, after, _gather_lands_all, 7)
        return {k: a.reshape(-1, a.shape[-1]) for k, a in zip(late_keys, lands)}

    sent = {}

    def scatter_start(name, groups):
        pairs, lands = _pair_sum(name + "_pair_sum", groups, _swap_halves_sibling(name + "_pair", groups), place)
        sent[name] = _split_start(name + "_start", pairs, lands, _scatter_sends, 3, pairs[0])
        return sent[name][4][0, 0]

    def ffn_grads(g_fg, g_fu, g_fd):
        return scatter_start("rs_ffn", [_halves(g_fg), _halves(g_fu), _halves(g_fd)])

    def mixer_grads(G):
        return scatter_start("rs_mix", [_join_w_in(G).reshape(4, 2, IN_SHARD_PAD // 2, D_MODEL), _halves(G["out"]),
                                        _halves(G["ba"]), _halves(G["bm"])])

    sp = {"g_pre_mix": g_pre_mix, "g_post_mix": g_post_mix, "b_if": b_if, "conv_w": conv_full,
          "conv_b": conv_b, "sinks": attn_sinks, "norm_w": mlstm_norm_w, "g_pre_ffn": g_pre_ffn,
          "g_post_ffn": g_post_ffn}
    loss, dx, small = _local_step(x[0], loss_target[0], positions.reshape(T, 1), [mod[i] for i in range(6)],
                                  sp, in_weights, late_weights, ffn_grads, mixer_grads)

    reds = (_sum4("rs_ffn_chip_sum", _split_wait("rs_ffn_wait", sent["rs_ffn"], dx, _scatter_lands, 3), place)
            + _sum4("rs_mix_chip_sum", _split_wait("rs_mix_wait", sent["rs_mix"], dx, _scatter_lands, 3), place))
    gsh = {k: s.reshape(-1, s.shape[-1])
           for k, s in zip(("fg", "fu", "fd", "w_in", "out", "ba", "bm"), _share_halves("rs_share", reds))}
    gsh["w_in"] = gsh["w_in"][:IN_SHARD]

    small["b_ada"] = small.pop("mod")
    vec = _pack_small(small).reshape(8, 2048)
    g_all = _all_gather8("gather_small", vec, pltpu.VMEM).reshape(8, SMALL_LEN)
    dmod_sh = lax.dynamic_slice_in_dim(g_all[:, :6 * D_MODEL], chip * ada_cols, ada_cols, axis=1)
    g_w_ada, = _mm_tn_group("dw_ada", [c_all], dmod_sh.astype(BF16), F32)

    smalls = {"b_ada": (b_ada, m_b_ada, v_b_ada), "g_pre_mix": (g_pre_mix, m_g_pre_mix, v_g_pre_mix),
              "g_post_mix": (g_post_mix, m_g_post_mix, v_g_post_mix), "b_if": (b_if, m_b_if, v_b_if),
              "conv_b": (conv_b, m_conv_b, v_conv_b), "sinks": (attn_sinks, m_attn_sinks, v_attn_sinks),
              "norm_w": (mlstm_norm_w, m_mlstm_norm_w, v_mlstm_norm_w),
              "g_pre_ffn": (g_pre_ffn, m_g_pre_ffn, v_g_pre_ffn),
              "g_post_ffn": (g_post_ffn, m_g_post_ffn, v_g_post_ffn)}
    offsets, off = {}, 0
    for k, width in SMALL:
        offsets[k], off = off, off + width
    g_sum, updates = _small_update(g_all, list(smalls.values()),
                                   [(offsets[k], t[0].shape[1]) for k, t in smalls.items()])
    g_conv = g_sum[:, offsets["conv_w"]:offsets["conv_w"] + CONV_WIDTH * D_MODEL].reshape(1, CONV_WIDTH, D_MODEL)
    g_conv = lax.dynamic_slice_in_dim(g_conv, chip * conv_w.shape[2], conv_w.shape[2], axis=2)

    res = dict(zip(smalls, updates))
    res["conv_w"] = (g_conv, *[o[None] for o in _adamw("adam_conv_w", conv_w[0], g_conv[0], m_conv_w[0], v_conv_w[0])])
    res["w_ada"] = (g_w_ada[None], *[o[None] for o in _adamw("adam_w_ada", w_ada[0], g_w_ada, m_w_ada[0], v_w_ada[0])])
    bigs = {"w_in": (w_in, m_w_in, v_w_in), "ba": (w_branch_attn, m_w_branch_attn, v_w_branch_attn),
            "bm": (w_branch_mlstm, m_w_branch_mlstm, v_w_branch_mlstm), "out": (w_out, m_w_out, v_w_out),
            "fg": (w_ffn_gate, m_w_ffn_gate, v_w_ffn_gate), "fu": (w_ffn_up, m_w_ffn_up, v_w_ffn_up),
            "fd": (w_ffn_down, m_w_ffn_down, v_w_ffn_down)}
    for k, (w, m, v) in bigs.items():
        if k in ("w_in", "fg", "fu"):
            res[k] = tuple(o.T[None] for o in (gsh[k], *_adamw("adam_" + k, w[0].T, gsh[k], m[0].T, v[0].T)))
        else:
            g = gsh[k].T if k in ("ba", "bm") else gsh[k]
            res[k] = (g[None], *[o[None] for o in _adamw("adam_" + k, w[0], g, m[0], v[0])])

    order = ("w_ada", "b_ada", "g_pre_mix", "g_post_mix", "w_in", "b_if", "conv_w", "conv_b", "sinks",
             "norm_w", "ba", "bm", "out", "g_pre_ffn", "g_post_ffn", "fg", "fu", "fd")
    total = lax.psum(loss[0, 0], ("x", "y", "c"))
    return (total, dx[None], *[res[k][0] for k in order], *[res[k][1] for k in order],
            *[res[k][2] for k in order], *[res[k][3] for k in order])
```

```python
import functools

import jax
import jax.numpy as jnp
from jax import lax
from jax.experimental import pallas as pl
from jax.experimental.pallas import tpu as pltpu

F32, BF16 = jnp.float32, jnp.bfloat16
MESH = pl.DeviceIdType.MESH

D_MODEL = 1024
N_Q_HEADS, N_KV_HEADS, HEAD_DIM, WINDOW = 8, 2, 64, 128
ROPE_THETA = 10000.0
MLSTM_HEADS, MLSTM_HEAD_DIM, MLSTM_CHUNK, CONV_WIDTH = 4, 128, 64, 4
D_FF = 2816
NORM_EPS = 1e-6
ADAM_LR, ADAM_B1, ADAM_B2, ADAM_EPS, ADAM_WD, ADAM_STEP = 0.001, 0.9, 0.999, 1e-08, 0.01, 10

VMEM_LIMIT = 56 * 1024 * 1024
ROW_TILE = 256
MM_TM = 512
MM_TT = 1024
ATTN_BLK = WINDOW
STEP_ROWS = 2 * MLSTM_CHUNK
NEG_INF = float("-inf")


def _params(sem):
    return pltpu.CompilerParams(dimension_semantics=sem, vmem_limit_bytes=VMEM_LIMIT)


def _sds(shape, dtype):
    return jax.ShapeDtypeStruct(shape, dtype)


def _sigmoid(x):
    return 1.0 / (1.0 + jnp.exp(-x))


def _dot(a, b, ca, cb):
    return lax.dot_general(a, b, (((ca,), (cb,)), ((), ())), preferred_element_type=F32)


def _bdot(a, b, ca, cb):
    return lax.dot_general(a, b, (((ca,), (cb,)), ((0,), (0,))), preferred_element_type=F32)


def _bdot_rows(a, b):
    return jnp.stack([_dot(a[h], b[h], 0, 0) for h in range(a.shape[0])])


def _mm(name, prods, extras, epi, out_dtypes, cn, nt=False, tm=MM_TM):
    flat = [ab for p in prods for ab in p]
    counts = [len(p) for p in prods]
    M = flat[0][0].shape[0]
    N = flat[0][1].shape[0 if nt else 1]
    tm = min(tm, M)
    n_in = 2 * len(flat) + len(extras)

    def body(*refs):
        ins, outs = refs[:n_in], refs[n_in:]
        for j in range(N // cn):
            cols = slice(j * cn, (j + 1) * cn)
            k, ps = 0, []
            for cnt in counts:
                acc = None
                for _ in range(cnt):
                    b = ins[k + 1][cols, :] if nt else ins[k + 1][:, cols]
                    d = _dot(ins[k][...], b, 1, 1 if nt else 0)
                    acc = d if acc is None else acc + d
                    k += 2
                ps.append(acc)
            res = epi(ps, [r[:, cols] for r in ins[k:]])
            for o, r in zip(outs, res):
                o[:, cols] = r.astype(o.dtype)

    in_specs, args = [], []
    for a, b in flat:
        in_specs.append(pl.BlockSpec((tm, a.shape[1]), lambda i: (i, 0)))
        in_specs.append(pl.BlockSpec(b.shape, lambda i: (0, 0), pipeline_mode=pl.Buffered(1)))
        args += [a, b]
    for e in extras:
        e, off = e if isinstance(e, tuple) else (e, 0)
        rows = 1 if e.shape[0] == 1 else tm
        in_specs.append(pl.BlockSpec((rows, N), lambda i, off=off, rows=rows: (0 if rows == 1 else i, off)))
        args.append(e)
    return pl.pallas_call(
        body, name=name, grid=(M // tm,), in_specs=in_specs,
        out_specs=[pl.BlockSpec((tm, N), lambda i: (i, 0)) for _ in out_dtypes],
        out_shape=[_sds((M, N), dt) for dt in out_dtypes],
        compiler_params=_params(("parallel",)))(*args)


def _mm_rows(name, prods, extras, epi, outs, accs, cn, nt=False, tm=MM_TM):
    flat = [ab for p in prods for ab in p]
    counts = [len(p) for p in prods]
    M = flat[0][0].shape[0]
    N = flat[0][1].shape[0 if nt else 1]
    tm = min(tm, M)
    n_mm, n_in, n_out = 2 * len(flat), 2 * len(flat) + len(extras), len(outs)

    def body(*refs):
        ins, out_refs, acc_refs = refs[:n_in], refs[n_in:n_in + n_out], refs[n_in + n_out:]

        @pl.when(pl.program_id(0) == 0)
        def _():
            for a in acc_refs:
                a[...] = jnp.zeros_like(a)

        chunks = [[] for _ in counts]
        for j in range(N // cn):
            cols = slice(j * cn, (j + 1) * cn)
            k = 0
            for p, cnt in enumerate(counts):
                acc = None
                for _ in range(cnt):
                    b = ins[k + 1][cols, :] if nt else ins[k + 1][:, cols]
                    d = _dot(ins[k][...], b, 1, 1 if nt else 0)
                    acc = d if acc is None else acc + d
                    k += 2
                chunks[p].append(acc)
        ps = [c[0] if len(c) == 1 else jnp.concatenate(c, axis=1) for c in chunks]
        res, incs = epi(ps, [r[...] for r in ins[n_mm:]])
        for o, r in zip(out_refs, res):
            o[...] = r.astype(o.dtype)
        for a, inc in zip(acc_refs, incs):
            a[...] += inc

    in_specs, args = [], []
    for a, b in flat:
        in_specs.append(pl.BlockSpec((tm, a.shape[1]), lambda i: (i, 0)))
        in_specs.append(pl.BlockSpec(b.shape, lambda i: (0, 0), pipeline_mode=pl.Buffered(1)))
        args += [a, b]
    for e in extras:
        rows = 1 if e.shape[0] == 1 else tm
        in_specs.append(pl.BlockSpec((rows, e.shape[1]), lambda i, rows=rows: (0 if rows == 1 else i, 0)))
        args.append(e)
    return pl.pallas_call(
        body, name=name, grid=(M // tm,), in_specs=in_specs,
        out_specs=[pl.BlockSpec((tm, w), lambda i: (i, 0)) for w, _ in outs]
        + [pl.BlockSpec(s, lambda i: (0, 0)) for s in accs],
        out_shape=[_sds((M, w), dt) for w, dt in outs] + [_sds(s, F32) for s in accs],
        compiler_params=_params(("arbitrary",)))(*args)


def _mm_tn_group(name, pieces, b, out_dtype, tt=MM_TT):
    T, N = b.shape
    tt = min(tt, T)
    steps, n = T // tt, len(pieces)

    def body(*refs):
        a_refs, b_ref, out_refs, accs = refs[:n], refs[n], refs[n + 1:2 * n + 1], refs[2 * n + 1:]
        t = pl.program_id(0)

        @pl.when(t == 0)
        def _():
            for acc in accs:
                acc[...] = jnp.zeros_like(acc)

        for a_ref, acc in zip(a_refs, accs):
            acc[...] += _dot(a_ref[...], b_ref[...], 0, 0)

        @pl.when(t == steps - 1)
        def _():
            for o_ref, acc in zip(out_refs, accs):
                o_ref[...] = acc[...].astype(o_ref.dtype)

    return pl.pallas_call(
        body, name=name, grid=(steps,),
        in_specs=[pl.BlockSpec((tt, a.shape[1]), lambda t: (t, 0)) for a in pieces]
        + [pl.BlockSpec((tt, N), lambda t: (t, 0))],
        out_specs=[pl.BlockSpec((a.shape[1], N), lambda t: (0, 0)) for a in pieces],
        out_shape=[_sds((a.shape[1], N), out_dtype) for a in pieces],
        scratch_shapes=[pltpu.VMEM((a.shape[1], N), F32) for a in pieces],
        compiler_params=_params(("arbitrary",)))(*pieces, b)


def _first(ps, es):
    return (ps[0],)


def _rows(name, body, ins, out_shapes, T, tr=ROW_TILE):
    tr = min(tr, T)

    def spec(shape):
        if shape[0] == T:
            return pl.BlockSpec((tr,) + tuple(shape[1:]), lambda i: (i,) + (0,) * (len(shape) - 1))
        return pl.BlockSpec(tuple(shape), lambda i: (0,) * len(shape))

    return pl.pallas_call(
        body, name=name, grid=(T // tr,),
        in_specs=[spec(a.shape) for a in ins], out_specs=[spec(s.shape) for s in out_shapes],
        out_shape=out_shapes, compiler_params=_params(("arbitrary",)))(*ins)


def _rms(x):
    r = lax.rsqrt(jnp.mean(x * x, axis=-1, keepdims=True) + NORM_EPS)
    return x * r, r


def _rms_bwd(dxn, xn, r):
    return r * (dxn - xn * jnp.mean(dxn * xn, axis=-1, keepdims=True))


def _colsum(v):
    return jnp.sum(v, axis=0, keepdims=True)


def _proj_in(x, g, sc, sh, groups):
    T = x.shape[0]
    tm = min(MM_TM, T)
    ng = len(groups)

    def body(x_ref, g_ref, sc_ref, sh_ref, *rest):
        w_refs, h_ref, out_refs = rest[:ng], rest[ng], rest[ng + 1:]
        xn, _ = _rms(x_ref[...])
        h = (xn * g_ref[...] * (1.0 + sc_ref[...]) + sh_ref[...]).astype(BF16)
        h_ref[...] = h
        for w_ref, o_ref, (w, _, cn) in zip(w_refs, out_refs, groups):
            for j in range(w.shape[0] // cn):
                cols = slice(j * cn, (j + 1) * cn)
                o_ref[:, cols] = _dot(h, w_ref[cols, :], 1, 1).astype(o_ref.dtype)

    row = pl.BlockSpec((1, D_MODEL), lambda i: (0, 0))
    tile = lambda w: pl.BlockSpec((tm, w), lambda i: (i, 0))
    return pl.pallas_call(
        body, name="proj_in", grid=(T // tm,),
        in_specs=[tile(D_MODEL), row, row, row] + [
            pl.BlockSpec(w.shape, lambda i: (0, 0), pipeline_mode=pl.Buffered(1)) for w, _, _ in groups],
        out_specs=[tile(D_MODEL)] + [tile(w.shape[0]) for w, _, _ in groups],
        out_shape=[_sds((T, D_MODEL), BF16)] + [_sds((T, w.shape[0]), dt) for w, dt, _ in groups],
        compiler_params=_params(("parallel",)))(x, g, sc, sh, *[w for w, _, _ in groups])


def _acc_rows(rows):
    w = rows[0].shape[1]
    return jnp.concatenate(rows + [jnp.zeros((8 - len(rows), w), F32)], axis=0)


def _res_norm_rows(ps, es):
    mix = ps[0]
    x, gate, gp, g2, sc, sh = es
    mh, _ = _rms(mix)
    x1 = x + gate * (mh * gp)
    xn, _ = _rms(x1)
    return [mix, x1, xn * g2 * (1.0 + sc) + sh], []


def _final_loss_rows(ps, es):
    x1, tgt, gate, gp = es
    fh, r = _rms(ps[0])
    e = x1 + gate * (fh * gp) - tgt
    loss = 0.5 * jnp.sum(jnp.mean(e * e, axis=-1, keepdims=True))
    dy = e * (1.0 / D_MODEL)
    acc = _acc_rows([_colsum(dy * fh * gp), _colsum(dy * gate * fh)])
    return [dy, _rms_bwd(dy * gate * gp, fh, r)], [acc, jnp.full((1, 128), loss, F32)]


def _res_norm_bwd_rows(ps, es):
    dh = ps[0]
    x1, mix, dy, sc, gate, g2, gp = es
    xn, r1 = _rms(x1)
    rows = [_colsum(dh * xn * g2), _colsum(dh), _colsum(dh * (1.0 + sc) * xn)]
    dx1 = dy + _rms_bwd(dh * (1.0 + sc) * g2, xn, r1)
    mh, rm = _rms(mix)
    rows += [_colsum(dx1 * mh * gp), _colsum(dx1 * gate * mh)]
    return [dx1, _rms_bwd(dx1 * gate * gp, mh, rm)], [_acc_rows(rows)]


def _pre_norm_bwd_rows(ps, es):
    dh = ps[0]
    x, dx1, g, sc = es
    xn, r = _rms(x)
    rows = [_colsum(dh * xn * g), _colsum(dh), _colsum(dh * (1.0 + sc) * xn)]
    return [dx1 + _rms_bwd(dh * (1.0 + sc) * g, xn, r)], [_acc_rows(rows)]


def _rope_tables(pos_col, inv_freq):
    T = pos_col.shape[0]

    def body(p_ref, f_ref, c_ref, s_ref):
        ang = p_ref[...].astype(F32) * f_ref[...]
        lane = lax.broadcasted_iota(jnp.int32, ang.shape, 1)
        c_ref[...] = jnp.cos(ang)
        s_ref[...] = jnp.where(lane % HEAD_DIM < HEAD_DIM // 2, -1.0, 1.0) * jnp.sin(ang)

    return _rows("rope_tables", body, [pos_col, inv_freq],
                 [_sds((T, 128), F32), _sds((T, 128), F32)], T, tr=512)


def _swap_halves(t):
    W = t.shape[1]
    lane = lax.broadcasted_iota(jnp.int32, t.shape, 1)
    half = HEAD_DIM // 2
    return jnp.where(lane % HEAD_DIM < half, pltpu.roll(t, W - half, 1), pltpu.roll(t, half, 1))


def _widen(c, W):
    return c if W == 128 else jnp.concatenate([c] * (W // 128), axis=1)


def _rope(t, c, s):
    W = t.shape[1]
    return t * _widen(c, W) + _swap_halves(t) * _widen(s, W)


def _unrope(dy, c, s):
    W = dy.shape[1]
    return dy * _widen(c, W) + _swap_halves(dy * _widen(s, W))


def _attn_mask(n):
    qi = lax.broadcasted_iota(jnp.int32, (ATTN_BLK, 2 * ATTN_BLK), 0)
    kj = lax.broadcasted_iota(jnp.int32, (ATTN_BLK, 2 * ATTN_BLK), 1)
    rel = kj - ATTN_BLK
    return (rel <= qi) & (qi - rel < WINDOW) & ((n > 0) | (kj >= ATTN_BLK))


def _attn_load(cur, prv, cc, sc, cp, sp):
    x, xp = cur[...], prv[...]
    q = _rope(x[:, :512], cc[...], sc[...]) * (HEAD_DIM ** -0.5)
    k = jnp.concatenate([_rope(xp[:, 512:640], cp[...], sp[...]),
                         _rope(x[:, 512:640], cc[...], sc[...])], axis=0)
    v = jnp.concatenate([xp[:, 640:768], x[:, 640:768]], axis=0)
    return q, k, v


ROLLED = tuple(h for h in range(N_Q_HEADS) if h % 2 != h // (N_Q_HEADS // N_KV_HEADS))


def _pair_heads(t):
    half = lax.broadcasted_iota(jnp.int32, (ATTN_BLK, 128), 1) // HEAD_DIM
    return jnp.stack([jnp.where(half == h % 2, t[:, 128 * (h // 2):128 * (h // 2) + 128], 0.0)
                      for h in range(N_Q_HEADS)])


def _kv_heads(t):
    half = lax.broadcasted_iota(jnp.int32, t.shape, 1) // HEAD_DIM
    tr = pltpu.roll(t, HEAD_DIM, 1)
    return jnp.stack([jnp.where(half == h % 2, tr if h in ROLLED else t, 0.0)
                      for h in range(N_Q_HEADS)])


def _sink_column(snk):
    return jnp.stack([jnp.full((1, 1), snk[0, h], F32) for h in range(N_Q_HEADS)])


def _attn_probs(qh, kh, mask, sink):
    s = jnp.where(mask, _bdot(qh, kh, 2, 2), NEG_INF)
    m = jnp.maximum(jnp.max(s, axis=-1, keepdims=True), sink)
    p = jnp.exp(s - m)
    es = jnp.exp(sink - m)
    rl = 1.0 / (jnp.sum(p, axis=-1, keepdims=True) + es)
    return p, es, rl


def _attn_specs(order):
    blk = lambda w: pl.BlockSpec((ATTN_BLK, w), lambda s: (order(s), 0))
    prv = lambda w: pl.BlockSpec((ATTN_BLK, w), lambda s: (jnp.maximum(order(s) - 1, 0), 0))
    return [blk(768), prv(768), blk(128), blk(128), prv(128), prv(128),
            pl.BlockSpec(memory_space=pltpu.SMEM)]


def _attn_fwd(pa, cos, sin, sinks):
    T = pa.shape[0]
    nb = T // ATTN_BLK

    def body(cur, prv, cc, sc, cp, sp, snk, y_ref):
        n = pl.program_id(0)
        q, k, v = _attn_load(cur, prv, cc, sc, cp, sp)
        mask = _attn_mask(n)
        half_q = lax.broadcasted_iota(jnp.int32, (ATTN_BLK, 128), 1) // HEAD_DIM
        half_k = lax.broadcasted_iota(jnp.int32, k.shape, 1) // HEAD_DIM
        moved = (pltpu.roll(k, HEAD_DIM, 1), pltpu.roll(v, HEAD_DIM, 1))
        for pair in range(N_Q_HEADS // 2):
            o = None
            for a in range(2):
                h = 2 * pair + a
                ku, vu = moved if h in ROLLED else (k, v)
                qh = jnp.where(half_q == a, q[:, 128 * pair:128 * pair + 128], 0.0).astype(BF16)
                kh = jnp.where(half_k == a, ku, 0.0).astype(BF16)
                vh = jnp.where(half_k == a, vu, 0.0).astype(BF16)
                s = jnp.where(mask, _dot(qh, kh, 1, 1), NEG_INF)
                m = jnp.maximum(jnp.max(s, axis=-1, keepdims=True), snk[0, h])
                p = jnp.exp(s - m)
                rl = 1.0 / (jnp.sum(p, axis=-1, keepdims=True) + jnp.exp(snk[0, h] - m))
                oh = _dot(p.astype(BF16), vh, 1, 0) * rl
                o = oh if o is None else o + oh
            y_ref[:, 128 * pair:128 * pair + 128] = o.astype(BF16)

    return pl.pallas_call(
        body, name="attn_fwd", grid=(nb,), in_specs=_attn_specs(lambda s: s),
        out_specs=pl.BlockSpec((ATTN_BLK, 512), lambda n: (n, 0)),
        out_shape=_sds((T, 512), BF16), compiler_params=_params(("parallel",)))(
            pa, pa, cos, sin, cos, sin, sinks)


def _attn_bwd(pa, cos, sin, sinks, dy):
    T = pa.shape[0]
    nb = T // ATTN_BLK
    rev = lambda s: nb - 1 - s

    def body(cur, prv, cc, sc, cp, sp, snk, dy_ref, dq_ref, dkv_ref, dsink_ref, carry):
        n = rev(pl.program_id(0))

        @pl.when(pl.program_id(0) == 0)
        def _():
            dsink_ref[...] = jnp.zeros_like(dsink_ref)
            carry[...] = jnp.zeros_like(carry)

        q, k, v = _attn_load(cur, prv, cc, sc, cp, sp)
        qh, kh, vh = _pair_heads(q).astype(BF16), _kv_heads(k).astype(BF16), _kv_heads(v).astype(BF16)
        p, es, rl = _attn_probs(qh, kh, _attn_mask(n), _sink_column(snk))
        pn = p * rl
        do = _pair_heads(dy_ref[...]).astype(BF16)
        dp = _bdot(do, vh, 2, 2)
        delta = jnp.sum(pn * dp, axis=-1, keepdims=True)
        ds = (pn * (dp - delta)).astype(BF16)
        dsink = es * rl * delta
        dq = _bdot(ds, kh, 2, 1) * (HEAD_DIM ** -0.5)
        dkh = _bdot_rows(ds, qh)
        dvh = _bdot_rows(pn.astype(BF16), do)

        def fold(t):
            same = [t[h] for h in range(N_Q_HEADS) if h not in ROLLED]
            moved = [t[h] for h in ROLLED]
            return sum(same[1:], same[0]) + pltpu.roll(sum(moved[1:], moved[0]), HEAD_DIM, 1)

        dk, dv = fold(dkh), fold(dvh)
        for h in range(N_Q_HEADS):
            dsink_ref[h:h + 1, :] += -jnp.sum(dsink[h])
        for pair in range(N_Q_HEADS // 2):
            dq_ref[:, 128 * pair:128 * pair + 128] = _unrope(
                dq[2 * pair] + dq[2 * pair + 1], cc[...], sc[...]).astype(BF16)
        dkv_ref[:, 0:128] = _unrope(dk[ATTN_BLK:] + carry[:, 0:128], cc[...], sc[...]).astype(BF16)
        dkv_ref[:, 128:256] = (dv[ATTN_BLK:] + carry[:, 128:256]).astype(BF16)
        carry[:, 0:128] = dk[:ATTN_BLK]
        carry[:, 128:256] = dv[:ATTN_BLK]

    blk = lambda w: pl.BlockSpec((ATTN_BLK, w), lambda s: (rev(s), 0))
    return pl.pallas_call(
        body, name="attn_bwd", grid=(nb,), in_specs=_attn_specs(rev) + [blk(512)],
        out_specs=[blk(512), blk(256), pl.BlockSpec((8, 128), lambda s: (0, 0))],
        out_shape=[_sds((T, 512), BF16), _sds((T, 256), BF16), _sds((8, 128), F32)],
        scratch_shapes=[pltpu.VMEM((ATTN_BLK, 256), F32)],
        compiler_params=_params(("arbitrary",)))(pa, pa, cos, sin, cos, sin, sinks, dy)


CONV_COLS = 2 * MLSTM_HEADS * MLSTM_HEAD_DIM


def _conv_pre(cur_ref, halo_ref, w_ref, b_ref, i, tr):
    xx = jnp.concatenate([jnp.where(i > 0, halo_ref[...], 0.0), cur_ref[...]], axis=0)
    taps = [(pltpu.roll(xx, CONV_WIDTH - 1 - j, 0) if j < CONV_WIDTH - 1 else xx)[8:8 + tr]
            for j in range(CONV_WIDTH)]
    pre = b_ref[...]
    for j in range(CONV_WIDTH):
        pre = pre + taps[j] * w_ref[j:j + 1, :]
    return pre, taps


def _conv_specs(T, tr):
    return [pl.BlockSpec((tr, CONV_COLS), lambda i: (i, 0)),
            pl.BlockSpec((8, CONV_COLS), lambda i: (jnp.maximum(i * (tr // 8) - 1, 0), 0)),
            pl.BlockSpec((CONV_WIDTH, CONV_COLS), lambda i: (0, 0)),
            pl.BlockSpec((1, CONV_COLS), lambda i: (0, 0))]


def _conv_fwd(pm, w, b):
    T = pm.shape[0]
    tr = min(ROW_TILE, T)

    def body(cur_ref, halo_ref, w_ref, b_ref, o_ref):
        pre, _ = _conv_pre(cur_ref, halo_ref, w_ref, b_ref, pl.program_id(0), tr)
        o_ref[...] = pre * _sigmoid(pre)

    return pl.pallas_call(
        body, name="conv_fwd", grid=(T // tr,), in_specs=_conv_specs(T, tr),
        out_specs=pl.BlockSpec((tr, CONV_COLS), lambda i: (i, 0)),
        out_shape=_sds((T, CONV_COLS), F32), compiler_params=_params(("parallel",)))(pm, pm, w, b)


def _conv_bwd(pqk, w, b, dqk):
    T = pqk.shape[0]
    tr = min(ROW_TILE, T)
    nt = T // tr

    def body(cur_ref, prev_ref, next_ref, w_ref, b_ref, d_ref, dnext_ref, du_ref, acc_ref):
        i = pl.program_id(0)

        @pl.when(i == 0)
        def _():
            acc_ref[...] = jnp.zeros_like(acc_ref)

        last = i == nt - 1
        xx = jnp.concatenate([jnp.where(i > 0, prev_ref[...], 0.0), cur_ref[...],
                              jnp.where(last, 0.0, next_ref[...])], axis=0)
        taps = [(pltpu.roll(xx, CONV_WIDTH - 1 - j, 0) if j < CONV_WIDTH - 1 else xx)[8:16 + tr]
                for j in range(CONV_WIDTH)]
        pre = b_ref[...]
        for j in range(CONV_WIDTH):
            pre = pre + taps[j] * w_ref[j:j + 1, :]
        sg = _sigmoid(pre)
        dd = jnp.concatenate([d_ref[...], jnp.where(last, 0.0, dnext_ref[...])], axis=0)
        dpre = dd * (sg * (1.0 + pre * (1.0 - sg)))
        for j in range(CONV_WIDTH):
            acc_ref[j:j + 1, :] += _colsum(dpre[:tr] * taps[j][:tr])
        acc_ref[CONV_WIDTH:CONV_WIDTH + 1, :] += _colsum(dpre[:tr])
        du = dpre[:tr] * w_ref[CONV_WIDTH - 1:CONV_WIDTH, :]
        for j in range(CONV_WIDTH - 1):
            k = CONV_WIDTH - 1 - j
            du = du + pltpu.roll(dpre, tr + 8 - k, 0)[:tr] * w_ref[j:j + 1, :]
        du_ref[...] = du.astype(BF16)

    tile = pl.BlockSpec((tr, CONV_COLS), lambda i: (i, 0))
    after = pl.BlockSpec((8, CONV_COLS), lambda i: (jnp.minimum((i + 1) * (tr // 8), T // 8 - 1), 0))
    before = pl.BlockSpec((8, CONV_COLS), lambda i: (jnp.maximum(i * (tr // 8) - 1, 0), 0))
    return pl.pallas_call(
        body, name="conv_bwd", grid=(nt,),
        in_specs=[tile, before, after, pl.BlockSpec((CONV_WIDTH, CONV_COLS), lambda i: (0, 0)),
                  pl.BlockSpec((1, CONV_COLS), lambda i: (0, 0)), tile, after],
        out_specs=[tile, pl.BlockSpec((8, CONV_COLS), lambda i: (0, 0))],
        out_shape=[_sds((T, CONV_COLS), BF16), _sds((8, CONV_COLS), F32)],
        compiler_params=_params(("arbitrary",)))(pqk, pqk, pqk, w, b, dqk, dqk)


def _log_sigmoid(x):
    return jnp.minimum(x, 0.0) - jnp.log1p(jnp.exp(-jnp.abs(x)))


def _chunk_cumsum(x, axis):
    idx = lax.broadcasted_iota(jnp.int32, x.shape, axis) % MLSTM_CHUNK
    k = 1
    while k < MLSTM_CHUNK:
        x = x + jnp.where(idx >= k, pltpu.roll(x, k, axis), 0.0)
        k *= 2
    return x


def _chunk_rev_cumsum(x, axis):
    n = x.shape[axis]
    idx = lax.broadcasted_iota(jnp.int32, x.shape, axis) % MLSTM_CHUNK
    k = 1
    while k < MLSTM_CHUNK:
        x = x + jnp.where(idx < MLSTM_CHUNK - k, pltpu.roll(x, n - k, axis), 0.0)
        k *= 2
    return x


def _mlstm_gates(gc_ref, bc_ref, gr_ref, br_ref):
    gc = gc_ref[...] + bc_ref[...]
    gr = gr_ref[...] + br_ref[...]
    return gc, _chunk_cumsum(_log_sigmoid(gc), 0), gr, _chunk_cumsum(_log_sigmoid(gr), 1)


def _heads(ref, base=0):
    D = MLSTM_HEAD_DIM
    return jnp.stack([ref[:, base + D * h:base + D * h + D] for h in range(MLSTM_HEADS)])


def _mlstm_inputs(q_ref, k_ref, v_ref, gc, bc, gr, br):
    H = MLSTM_HEADS
    q, v = _heads(q_ref), _heads(v_ref)
    ks = _heads(k_ref) * (MLSTM_HEAD_DIM ** -0.5)
    return dict(
        q=q, ks=ks, qb=q.astype(BF16), kb=ks.astype(BF16), vb=v.astype(BF16),
        b_col=jnp.stack([bc[:, H + h:H + h + 1] for h in range(H)]),
        i_col=jnp.stack([gc[:, h:h + 1] for h in range(H)]),
        b_row=jnp.stack([br[H + h:H + h + 1, :] for h in range(H)]),
        i_row=jnp.stack([gr[h:h + 1, :] for h in range(H)]))


def _mlstm_head(f, c_prev, n_prev, m_prev):
    L = MLSTM_CHUNK
    q, qb = f["q"], f["qb"]
    t = lax.broadcasted_iota(jnp.int32, (1, 2 * L, 2 * L), 1)
    s = lax.broadcasted_iota(jnp.int32, (1, 2 * L, 2 * L), 2)
    mask = (t // L == s // L) & (s <= t)
    d = jnp.where(mask, f["b_col"] - f["b_row"] + f["i_row"], NEG_INF)
    row = lax.broadcasted_iota(jnp.int32, (1, 2 * L, 1), 1)
    inter = f["b_col"] + jnp.where(row < L, m_prev[0], m_prev[1])
    m_t = jnp.maximum(inter, jnp.max(d, axis=-1, keepdims=True))
    w_intra = jnp.exp(d - m_t)
    w_inter = jnp.exp(inter - m_t)
    sc = _bdot(qb, f["kb"], 2, 2) * w_intra
    qc = jnp.concatenate([_bdot(qb[:, :L], c_prev[0].astype(BF16), 2, 1),
                          _bdot(qb[:, L:], c_prev[1].astype(BF16), 2, 1)], axis=1)
    qn = jnp.concatenate([jnp.sum(q[:, :L] * n_prev[0], axis=-1, keepdims=True),
                          jnp.sum(q[:, L:] * n_prev[1], axis=-1, keepdims=True)], axis=1)
    num = _bdot(sc.astype(BF16), f["vb"], 2, 1) + w_inter * qc
    den = jnp.sum(sc, axis=-1, keepdims=True) + w_inter * qn
    return dict(f, w_intra=w_intra, w_inter=w_inter, sc=sc, qc=qc, qn=qn, num=num, den=den,
                floor=jnp.exp(-m_t))


def _mlstm_update(f, ch, c, n, m):
    L = MLSTM_CHUNK
    rows = slice(L * ch, L * ch + L)
    b_col = f["b_col"][:, rows]
    g_last = b_col[:, L - 1:L]
    a_col = g_last - b_col + f["i_col"][:, rows]
    m_new = jnp.maximum(g_last + m, jnp.max(a_col, axis=1, keepdims=True))
    decay = jnp.exp(g_last + m - m_new)
    e_a = jnp.exp(a_col - m_new)
    kw = f["ks"][:, rows] * e_a
    c_new = decay * c + _bdot_rows(kw.astype(BF16), f["vb"][:, rows])
    n_new = decay * n + jnp.sum(kw, axis=1, keepdims=True)
    return c_new, n_new, m_new, decay, e_a, kw


def _mlstm_specs(T, order):
    blk = lambda w, col: pl.BlockSpec((STEP_ROWS, w), lambda s: (order(s), col))
    return [blk(512, 0), blk(512, 1), blk(512, 0), blk(128, 0),
            pl.BlockSpec((1, 128), lambda s: (0, 0)),
            pl.BlockSpec((8, STEP_ROWS), lambda s: (0, order(s))),
            pl.BlockSpec((8, 128), lambda s: (0, 0))]


def _lanes(m):
    return jnp.broadcast_to(m, m.shape[:-1] + (128,))


def _mlstm_fwd(qk, pm, gcol, bcol, grow, brow):
    T = qk.shape[0]
    steps = T // STEP_ROWS
    H, D = MLSTM_HEADS, MLSTM_HEAD_DIM

    def body(q_ref, k_ref, v_ref, gc_ref, bc_ref, gr_ref, br_ref, h_ref, cs_ref, ns_ref, ms_ref,
             c_st, n_st, m_st):
        @pl.when(pl.program_id(0) == 0)
        def _():
            c_st[...] = jnp.zeros_like(c_st)
            n_st[...] = jnp.zeros_like(n_st)
            m_st[...] = jnp.zeros_like(m_st)

        f = _mlstm_inputs(q_ref, k_ref, v_ref, *_mlstm_gates(gc_ref, bc_ref, gr_ref, br_ref))
        c0, n0, m0 = c_st[...], n_st[...], m_st[:, :, 0:1]
        c1, n1, m1, _, _, _ = _mlstm_update(f, 0, c0, n0, m0)
        c2, n2, m2, _, _, _ = _mlstm_update(f, 1, c1, n1, m1)
        f = _mlstm_head(f, (c0, c1), (n0, n1), (m0, m1))
        h = f["num"] / jnp.maximum(jnp.abs(f["den"]), f["floor"])
        for hd in range(H):
            h_ref[:, D * hd:D * hd + D] = h[hd]
        cs_ref[0], cs_ref[1] = c0, c1
        ns_ref[0], ns_ref[1] = n0, n1
        ms_ref[0], ms_ref[1] = _lanes(m0), _lanes(m1)
        c_st[...], n_st[...], m_st[...] = c2, n2, _lanes(m2)

    vec = pl.BlockSpec((2, H, 1, 128), lambda s: (s, 0, 0, 0))
    return pl.pallas_call(
        body, name="mlstm_fwd", grid=(steps,), in_specs=_mlstm_specs(T, lambda s: s),
        out_specs=[pl.BlockSpec((STEP_ROWS, 512), lambda s: (s, 0)),
                   pl.BlockSpec((2, H, 128, 128), lambda s: (s, 0, 0, 0)), vec, vec],
        out_shape=[_sds((T, 512), F32), _sds((2 * steps, H, 128, 128), F32),
                   _sds((2 * steps, H, 1, 128), F32), _sds((2 * steps, H, 1, 128), F32)],
        scratch_shapes=[pltpu.VMEM((H, 128, 128), F32), pltpu.VMEM((H, 1, 128), F32),
                        pltpu.VMEM((H, 1, 128), F32)],
        compiler_params=_params(("arbitrary",)))(qk, qk, pm, gcol, bcol, grow, brow)


def _mlstm_bwd(qk, pm, gcol, bcol, grow, brow, cs, ns, ms, dh):
    T = qk.shape[0]
    steps = T // STEP_ROWS
    H, L, D = MLSTM_HEADS, MLSTM_CHUNK, MLSTM_HEAD_DIM
    rev = lambda s: steps - 1 - s

    def body(q_ref, k_ref, v_ref, gc_ref, bc_ref, gr_ref, br_ref, cs_ref, ns_ref, ms_ref, dh_ref,
             dqk_ref, dv_ref, dgc_ref, dgr_ref, dc_st, dn_st):
        @pl.when(pl.program_id(0) == 0)
        def _():
            dc_st[...] = jnp.zeros_like(dc_st)
            dn_st[...] = jnp.zeros_like(dn_st)

        f = _mlstm_inputs(q_ref, k_ref, v_ref, *_mlstm_gates(gc_ref, bc_ref, gr_ref, br_ref))
        c_prev = (cs_ref[0], cs_ref[1])
        n_prev = (ns_ref[0], ns_ref[1])
        m_prev = (ms_ref[0, :, :, 0:1], ms_ref[1, :, :, 0:1])
        f = _mlstm_head(f, c_prev, n_prev, m_prev)
        big = jnp.abs(f["den"]) > f["floor"]
        rden = 1.0 / jnp.where(big, jnp.abs(f["den"]), f["floor"])
        dnum = _heads(dh_ref) * rden
        hdh = jnp.sum(f["num"] * dnum, axis=-1, keepdims=True)
        dden = jnp.where(big, -hdh * rden * jnp.sign(f["den"]), 0.0)
        dnum_b = dnum.astype(BF16)
        dsc = _bdot(dnum_b, f["vb"], 2, 2) + dden
        g = dsc * f["sc"]
        dv = _bdot_rows(f["sc"].astype(BF16), dnum_b)
        dqk_ = (dsc * f["w_intra"]).astype(BF16)
        dq = _bdot(dqk_, f["kb"], 2, 1)
        dks = _bdot_rows(dqk_, f["qb"])
        wdn = f["w_inter"] * dnum
        wdn_b = wdn.astype(BF16)
        wdd = f["w_inter"] * dden
        u = jnp.sum(f["qc"] * wdn, axis=-1, keepdims=True) + wdd * f["qn"]
        dks_s, dv_s, z_s, dg_s = [None, None], [None, None], [None, None], [None, None]
        dcn, dnn = dc_st[...], dn_st[...]
        for ch in (1, 0):
            rows = slice(L * ch, L * ch + L)
            _, _, _, decay, e_a, kw = _mlstm_update(f, ch, c_prev[ch], n_prev[ch], m_prev[ch])
            dcn_b = dcn.astype(BF16)
            dkw = _bdot(f["vb"][:, rows], dcn_b, 2, 2) + dnn
            dks_s[ch] = e_a * dkw
            dv_s[ch] = _bdot(kw.astype(BF16), dcn_b, 2, 1)
            z_s[ch] = e_a * jnp.sum(f["ks"][:, rows] * dkw, axis=-1, keepdims=True)
            dg_s[ch] = jnp.sum(z_s[ch], axis=1, keepdims=True) + decay * (
                jnp.sum(c_prev[ch] * dcn, axis=(1, 2), keepdims=True)
                + jnp.sum(n_prev[ch] * dnn, axis=(1, 2), keepdims=True))
            dcn = decay * dcn + _bdot_rows(f["qb"][:, rows], wdn_b[:, rows])
            dnn = decay * dnn + jnp.sum(wdd[:, rows] * f["q"][:, rows], axis=1, keepdims=True)
        dc_st[...], dn_st[...] = dcn, dnn
        dq = dq + jnp.concatenate(
            [_bdot(wdn_b[:, :L], c_prev[0].astype(BF16), 2, 2) + wdd[:, :L] * n_prev[0],
             _bdot(wdn_b[:, L:], c_prev[1].astype(BF16), 2, 2) + wdd[:, L:] * n_prev[1]], axis=1)
        dks = (dks + jnp.concatenate(dks_s, axis=1)) * (D ** -0.5)
        dv = dv + jnp.concatenate(dv_s, axis=1)
        z = jnp.concatenate(z_s, axis=1)
        row = lax.broadcasted_iota(jnp.int32, (1, STEP_ROWS, 1), 1)
        dg_col = jnp.where(row == L - 1, dg_s[0], 0.0) + jnp.where(row == 2 * L - 1, dg_s[1], 0.0)
        db_col = jnp.sum(g, axis=-1, keepdims=True) + u - z + dg_col
        g_row = jnp.sum(g, axis=1, keepdims=True)
        lane = lax.broadcasted_iota(jnp.int32, (STEP_ROWS, 128), 1)
        sub = lax.broadcasted_iota(jnp.int32, (8, STEP_ROWS), 0)
        dgc = jnp.zeros((STEP_ROWS, 128), F32)
        dgr = jnp.zeros((8, STEP_ROWS), F32)
        for hd in range(H):
            dgc = dgc + jnp.where(lane == hd, z[hd], 0.0) + jnp.where(lane == H + hd, db_col[hd], 0.0)
            dgr = dgr + jnp.where(sub == hd, g_row[hd], 0.0) - jnp.where(sub == H + hd, g_row[hd], 0.0)
            dqk_ref[:, D * hd:D * hd + D] = dq[hd]
            dqk_ref[:, H * D + D * hd:H * D + D * hd + D] = dks[hd]
            dv_ref[:, D * hd:D * hd + D] = dv[hd].astype(BF16)
        dgc_ref[...] = dgc
        dgr_ref[...] = dgr

    return pl.pallas_call(
        body, name="mlstm_bwd", grid=(steps,),
        in_specs=_mlstm_specs(T, rev) + [
            pl.BlockSpec((2, H, 128, 128), lambda s: (rev(s), 0, 0, 0)),
            pl.BlockSpec((2, H, 1, 128), lambda s: (rev(s), 0, 0, 0)),
            pl.BlockSpec((2, H, 1, 128), lambda s: (rev(s), 0, 0, 0)),
            pl.BlockSpec((STEP_ROWS, 512), lambda s: (rev(s), 0))],
        out_specs=[pl.BlockSpec((STEP_ROWS, 1024), lambda s: (rev(s), 0)),
                   pl.BlockSpec((STEP_ROWS, 512), lambda s: (rev(s), 0)),
                   pl.BlockSpec((STEP_ROWS, 128), lambda s: (rev(s), 0)),
                   pl.BlockSpec((8, STEP_ROWS), lambda s: (0, rev(s)))],
        out_shape=[_sds((T, 1024), F32), _sds((T, 512), BF16), _sds((T, 128), F32), _sds((8, T), F32)],
        scratch_shapes=[pltpu.VMEM((H, 128, 128), F32), pltpu.VMEM((H, 1, 128), F32)],
        compiler_params=_params(("arbitrary",)))(qk, qk, pm, gcol, bcol, grow, brow, cs, ns, ms, dh)


def _rows_to_lanes(x):
    eye = (lax.broadcasted_iota(jnp.int32, (8, 128), 0)
           == lax.broadcasted_iota(jnp.int32, (8, 128), 1)).astype(BF16)
    out, rest = None, x
    for _ in range(3):
        piece = rest.astype(BF16)
        rest = rest - piece.astype(F32)
        t = _dot(piece, eye, 0, 0)
        out = t if out is None else out + t
    return out


def _gate_bwd(dgc, dgr, gcol, bcol):
    T = dgc.shape[0]
    tr = min(ROW_TILE, T)

    def body(a_ref, b_ref, g_ref, bias_ref, o_ref, acc_ref):
        i = pl.program_id(0)

        @pl.when(i == 0)
        def _():
            acc_ref[...] = jnp.zeros_like(acc_ref)

        d = a_ref[...] + _rows_to_lanes(b_ref[:, pl.ds(pl.multiple_of(i * tr, 128), tr)])
        lane = lax.broadcasted_iota(jnp.int32, d.shape, 1)
        is_f = (lane >= MLSTM_HEADS) & (lane < 2 * MLSTM_HEADS)
        dlogf = _chunk_rev_cumsum(jnp.where(is_f, d, 0.0), 0)
        out = jnp.where(is_f, dlogf * _sigmoid(-(g_ref[...] + bias_ref[...])), d)
        o_ref[...] = out.astype(BF16)
        acc_ref[0:1, :] += _colsum(out)

    return _rows("gate_bwd", body, [dgc, dgr, gcol, bcol],
                 [_sds((T, 128), BF16), _sds((8, 128), F32)], T, tr=tr)


def _head_norm(h, mu_axis=-1):
    mu = jnp.mean(h, axis=-1, keepdims=True)
    hc = h - mu
    r = lax.rsqrt(jnp.mean(hc * hc, axis=-1, keepdims=True) + NORM_EPS)
    return hc * r, r


def _mlstm_out(hm, pm, w):
    T = hm.shape[0]
    D = MLSTM_HEAD_DIM

    def body(h_ref, o_ref, w_ref, y_ref):
        for hd in range(MLSTM_HEADS):
            cols = slice(D * hd, D * hd + D)
            hn, _ = _head_norm(h_ref[:, cols])
            y_ref[:, cols] = (_sigmoid(o_ref[:, cols].astype(F32)) * hn * w_ref[:, cols]).astype(BF16)

    tr = min(ROW_TILE, T)
    return pl.pallas_call(
        body, name="mlstm_out", grid=(T // tr,),
        in_specs=[pl.BlockSpec((tr, 512), lambda i: (i, 0)), pl.BlockSpec((tr, 512), lambda i: (i, 1)),
                  pl.BlockSpec((1, 512), lambda i: (0, 0))],
        out_specs=pl.BlockSpec((tr, 512), lambda i: (i, 0)), out_shape=_sds((T, 512), BF16),
        compiler_params=_params(("parallel",)))(hm, pm, w)


def _mlstm_out_bwd_rows(ps, es):
    hm, vo, w_all = es
    D, width = MLSTM_HEAD_DIM, MLSTM_HEADS * MLSTM_HEAD_DIM
    dhs, dos, dws = [], [], []
    for hd in range(MLSTM_HEADS):
        cols = slice(D * hd, D * hd + D)
        hn, r = _head_norm(hm[:, cols])
        sg = _sigmoid(vo[:, width + D * hd:width + D * hd + D].astype(F32))
        dy, w = ps[0][:, cols], w_all[:, cols]
        dos.append(dy * hn * w * sg * (1.0 - sg))
        dyn = dy * sg
        dws.append(_colsum(dyn * hn))
        dhn = dyn * w
        dhs.append(r * (dhn - jnp.mean(dhn, axis=-1, keepdims=True)
                        - hn * jnp.mean(dhn * hn, axis=-1, keepdims=True)))
    cat = lambda parts: jnp.concatenate(parts, axis=1)
    return [cat(dhs), cat(dos)], [_acc_rows([cat(dws)])]


ADAM_TILE_ELEMS = 256 * 1024


def _adamw(name, w, g, m, v):
    R, C = w.shape
    fits = [t for t in range(8, R + 1, 8) if R % t == 0 and t * C <= ADAM_TILE_ELEMS]
    if fits or R * C <= ADAM_TILE_ELEMS:
        tr = fits[-1] if fits else R
        spec, grid = pl.BlockSpec((tr, C), lambda i: (i, 0)), (R // tr,)
    else:
        spec, grid = pl.BlockSpec((R, 128), lambda i: (0, i)), (C // 128,)
    c1 = 1.0 - ADAM_B1 ** ADAM_STEP
    c2 = 1.0 - ADAM_B2 ** ADAM_STEP

    def body(w_ref, g_ref, m_ref, v_ref, d_ref, mo_ref, vo_ref):
        g = g_ref[...]
        m = ADAM_B1 * m_ref[...] + (1.0 - ADAM_B1) * g
        v = ADAM_B2 * v_ref[...] + (1.0 - ADAM_B2) * (g * g)
        mo_ref[...] = m
        vo_ref[...] = v
        d_ref[...] = -ADAM_LR * ((m / c1) / (jnp.sqrt(v / c2) + ADAM_EPS) + ADAM_WD * w_ref[...])

    return pl.pallas_call(
        body, name=name, grid=grid, in_specs=[spec] * 4, out_specs=[spec] * 3,
        out_shape=[_sds((R, C), F32)] * 3, compiler_params=_params(("parallel",)))(w, g, m, v)


def _place():
    return lax.axis_index("x"), lax.axis_index("y"), lax.axis_index("c")


def _all_gather8(name, blk, space):
    m, n = blk.shape

    def body(x_ref, out_ref, send_sems, recv_sems, local_sem):
        x, y, c = _place()
        me, sibling = (x, y, c), (x, y, 1 - c)
        chips = [(1 - x, y), (x, 1 - y), (1 - x, 1 - y)]

        def rows(px, py, pc):
            return out_ref.at[pl.ds((4 * px + 2 * py + pc) * m, m), :]

        def copy(k, block, to, src=None):
            return pltpu.make_async_remote_copy(
                src_ref=rows(*block) if src is None else src, dst_ref=rows(*block),
                send_sem=send_sems.at[k], recv_sem=recv_sems.at[k],
                device_id=to, device_id_type=MESH)

        mine = pltpu.make_async_copy(x_ref, rows(*me), local_sem)
        mine.start()
        first = [copy(0, me, sibling, src=x_ref)]
        first += [copy(1 + j, me, (*chip, c), src=x_ref) for j, chip in enumerate(chips)]
        for cp in first:
            cp.start()
        passed = [copy(4 + j, (*chip, c), sibling) for j, chip in enumerate(chips)]
        for j, chip in enumerate(chips):
            copy(1 + j, (*chip, c), me).wait_recv()
            passed[j].start()
        copy(0, sibling, me).wait_recv()
        for j, chip in enumerate(chips):
            copy(4 + j, (*chip, 1 - c), me).wait_recv()
        for cp in first + passed:
            cp.wait_send()
        mine.wait()

    return pl.pallas_call(
        body, name=name, out_shape=_sds((8 * m, n), blk.dtype),
        in_specs=[pl.BlockSpec(memory_space=space)], out_specs=pl.BlockSpec(memory_space=space),
        scratch_shapes=[pltpu.SemaphoreType.DMA((7,)), pltpu.SemaphoreType.DMA((7,)),
                        pltpu.SemaphoreType.DMA],
        compiler_params=pltpu.CompilerParams(vmem_limit_bytes=VMEM_LIMIT))(blk)


def _hbm_specs(n):
    return [pl.BlockSpec(memory_space=pl.ANY)] * n


def _swap_halves_sibling(name, srcs):
    nw = len(srcs)

    def body(*refs):
        src_refs, dst_refs, send_sems, recv_sems = refs[:nw], refs[nw:2 * nw], refs[2 * nw], refs[2 * nw + 1]
        x, y, c = _place()
        cps = [pltpu.make_async_remote_copy(
            src_ref=src_refs[w].at[pl.ds(0, 4), 1 - c], dst_ref=dst_refs[w],
            send_sem=send_sems.at[w], recv_sem=recv_sems.at[w], device_id=(x, y, 1 - c),
            device_id_type=MESH) for w in range(nw)]
        for cp in cps:
            cp.start()
        for cp in cps:
            cp.wait()

    return pl.pallas_call(
        body, name=name, out_shape=[_sds(s.shape[:1] + s.shape[2:], s.dtype) for s in srcs],
        in_specs=_hbm_specs(nw), out_specs=_hbm_specs(nw),
        scratch_shapes=[pltpu.SemaphoreType.DMA((nw,)), pltpu.SemaphoreType.DMA((nw,))])(*srcs)


def _split_start(name, srcs, lands, copies, per_array, after):
    nw = len(srcs)

    def body(*refs):
        send_sems, recv_sems, token = refs[2 * nw + 1], refs[2 * nw + 2], refs[-1]
        for w in range(nw):
            for k, (s, d, dev) in enumerate(copies(refs[w], refs[nw + w], *_place())):
                pltpu.make_async_remote_copy(
                    src_ref=s, dst_ref=d, send_sem=send_sems.at[w * per_array + k],
                    recv_sem=recv_sems.at[w * per_array + k], device_id=dev, device_id_type=MESH).start()
        token[...] = jnp.zeros_like(token)

    hbm, sem = pl.BlockSpec(memory_space=pltpu.HBM), pl.BlockSpec(memory_space=pltpu.SEMAPHORE)
    arrays = list(srcs) + list(lands)
    out = pl.pallas_call(
        body, name=name,
        out_shape=(pltpu.SemaphoreType.DMA((nw * per_array,)), pltpu.SemaphoreType.DMA((nw * per_array,)),
                   *[pltpu.HBM(a.shape, a.dtype) for a in arrays], _sds((8, 128), F32)),
        in_specs=[hbm] * (2 * nw) + [pl.BlockSpec(memory_space=pl.ANY)],
        out_specs=(sem, sem, *[hbm] * (2 * nw), pl.BlockSpec(memory_space=pltpu.VMEM)),
        input_output_aliases={i: 2 + i for i in range(2 * nw)},
        compiler_params=pltpu.CompilerParams(has_side_effects=pltpu.SideEffectType.DATAFLOW_SIDE_EFFECTING))(
            *[pltpu.with_memory_space_constraint(a, pltpu.HBM) for a in arrays], after)
    return out[0], out[1], out[2:2 + nw], out[2 + nw:2 + 2 * nw], out[-1]


def _split_wait(name, started, after, waits, per_array):
    send_sems, recv_sems, srcs, lands, _ = started
    nw = len(srcs)

    def body(*refs):
        send_sems, recv_sems = refs[2 * nw], refs[2 * nw + 1]
        x, y, c = _place()
        for w in range(nw):
            for k, (s, d) in enumerate(waits(refs[w], refs[nw + w], x, y, c)):
                cp = pltpu.make_async_remote_copy(
                    src_ref=s, dst_ref=d, send_sem=send_sems.at[w * per_array + k],
                    recv_sem=recv_sems.at[w * per_array + k], device_id=(x, y, 1 - c),
                    device_id_type=MESH)
                cp.wait_send()
                cp.wait_recv()

    hbm, sem = pl.BlockSpec(memory_space=pltpu.HBM), pl.BlockSpec(memory_space=pltpu.SEMAPHORE)
    arrays = list(srcs) + list(lands)
    out = pl.pallas_call(
        body, name=name, out_shape=tuple(pltpu.HBM(a.shape, a.dtype) for a in arrays),
        in_specs=[hbm] * (2 * nw) + [sem, sem, pl.BlockSpec(memory_space=pl.ANY)],
        out_specs=tuple([hbm] * (2 * nw)), input_output_aliases={i: i for i in range(2 * nw)},
        compiler_params=pltpu.CompilerParams(has_side_effects=pltpu.SideEffectType.DATAFLOW_SIDE_EFFECTING))(
            *arrays, send_sems, recv_sems, after)
    return list(out[nw:])


def _other_chips(x, y):
    return [(1 - x, y), (x, 1 - y), (1 - x, 1 - y)]


def _gather_sends(src_ref, land_ref, x, y, c):
    to = land_ref.at[2 * x + y, c]
    return [(src_ref, to, (x, y, 1 - c))] + [(src_ref, to, (px, py, c)) for px, py in _other_chips(x, y)]


def _gather_lands(src_ref, land_ref, x, y, c):
    return [(src_ref, land_ref.at[2 * x + y, 1 - c])] + [
        (src_ref, land_ref.at[2 * px + py, c]) for px, py in _other_chips(x, y)]


def _gather_sends_all(src_ref, land_ref, x, y, c):
    to = land_ref.at[2 * x + y, c]
    return [(src_ref, to, (x, y, 1 - c))] + [
        (src_ref, to, (px, py, pc)) for px, py in _other_chips(x, y) for pc in (c, 1 - c)]


def _gather_lands_all(src_ref, land_ref, x, y, c):
    return [(src_ref, land_ref.at[2 * x + y, 1 - c])] + [
        (src_ref, land_ref.at[2 * px + py, pc]) for px, py in _other_chips(x, y) for pc in (c, 1 - c)]


def _scatter_sends(src_ref, land_ref, x, y, c):
    return [(src_ref.at[2 * px + py], land_ref.at[2 * x + y], (px, py, c)) for px, py in _other_chips(x, y)]


def _scatter_lands(src_ref, land_ref, x, y, c):
    return [(src_ref.at[2 * x + y], land_ref.at[2 * px + py]) for px, py in _other_chips(x, y)]


def _forward_sibling(name, lands):
    nw = len(lands)

    def body(*refs):
        land_refs, out_refs, send_sems, recv_sems = refs[:nw], refs[nw:2 * nw], refs[2 * nw], refs[2 * nw + 1]
        x, y, c = _place()
        cps = []
        for w in range(nw):
            cps += [pltpu.make_async_remote_copy(
                src_ref=land_refs[w].at[2 * px + py, c], dst_ref=out_refs[w].at[2 * px + py, c],
                send_sem=send_sems.at[w, j], recv_sem=recv_sems.at[w, j], device_id=(x, y, 1 - c),
                device_id_type=MESH) for j, (px, py) in enumerate(_other_chips(x, y))]
        for cp in cps:
            cp.start()
        for w in range(nw):
            for j, (px, py) in enumerate(_other_chips(x, y)):
                slot = out_refs[w].at[2 * px + py, 1 - c]
                pltpu.make_async_remote_copy(src_ref=slot, dst_ref=slot, send_sem=send_sems.at[w, j],
                                             recv_sem=recv_sems.at[w, j], device_id=(x, y, 1 - c),
                                             device_id_type=MESH).wait_recv()
        for cp in cps:
            cp.wait_send()

    return pl.pallas_call(
        body, name=name, out_shape=[_sds(a.shape, a.dtype) for a in lands],
        in_specs=_hbm_specs(nw), out_specs=_hbm_specs(nw), input_output_aliases={i: i for i in range(nw)},
        scratch_shapes=[pltpu.SemaphoreType.DMA((nw, 3)), pltpu.SemaphoreType.DMA((nw, 3))])(*lands)


def _share_halves(name, halves):
    nw = len(halves)

    def body(*refs):
        in_refs, out_refs, send_sems, recv_sems = refs[:nw], refs[nw:2 * nw], refs[2 * nw], refs[2 * nw + 1]
        x, y, c = _place()
        cps = [pltpu.make_async_remote_copy(
            src_ref=in_refs[w].at[c], dst_ref=out_refs[w].at[c], send_sem=send_sems.at[w],
            recv_sem=recv_sems.at[w], device_id=(x, y, 1 - c), device_id_type=MESH) for w in range(nw)]
        for cp in cps:
            cp.start()
        for w in range(nw):
            slot = out_refs[w].at[1 - c]
            pltpu.make_async_remote_copy(src_ref=slot, dst_ref=slot, send_sem=send_sems.at[w],
                                         recv_sem=recv_sems.at[w], device_id=(x, y, 1 - c),
                                         device_id_type=MESH).wait_recv()
        for cp in cps:
            cp.wait_send()

    return pl.pallas_call(
        body, name=name, out_shape=[_sds(a.shape, a.dtype) for a in halves],
        in_specs=_hbm_specs(nw), out_specs=_hbm_specs(nw), input_output_aliases={i: i for i in range(nw)},
        scratch_shapes=[pltpu.SemaphoreType.DMA((nw,)), pltpu.SemaphoreType.DMA((nw,))])(*halves)


def _place_blocks(name, blks, place):
    nw = len(blks)

    def body(p_ref, *refs):
        for b_ref, o_ref in zip(refs[:nw], refs[nw:]):
            o_ref[...] = b_ref[...]

    return pl.pallas_call(
        body, name=name,
        grid_spec=pltpu.PrefetchScalarGridSpec(
            num_scalar_prefetch=1, grid=(1,),
            in_specs=[pl.BlockSpec(b.shape, lambda i, p: (0, 0)) for b in blks],
            out_specs=[pl.BlockSpec((None, None) + b.shape, lambda i, p: (p[0], p[1], 0, 0)) for b in blks]),
        out_shape=[_sds((4, 2) + b.shape, b.dtype) for b in blks],
        compiler_params=_params(("arbitrary",)))(place, *blks)


def _pair_sum(name, fulls, gots, place):
    nw = len(fulls)

    def body(p_ref, *refs):
        s = pl.program_id(0)
        for a_ref, b_ref, o_ref, l_ref in zip(refs[:nw], refs[nw:2 * nw], refs[2 * nw:3 * nw], refs[3 * nw:]):
            o_ref[...] = (a_ref[...].astype(F32) + b_ref[...].astype(F32)).astype(o_ref.dtype)

            @pl.when(s == p_ref[0])
            def _():
                l_ref[...] = o_ref[...]

    slab = lambda a: pl.BlockSpec((None,) + a.shape[1:], lambda s, p: (s, 0, 0))
    mine = lambda a: pl.BlockSpec((None,) + a.shape[1:], lambda s, p: (p[0], 0, 0))
    out = pl.pallas_call(
        body, name=name,
        grid_spec=pltpu.PrefetchScalarGridSpec(
            num_scalar_prefetch=1, grid=(4,),
            in_specs=[pl.BlockSpec((None, None) + a.shape[2:], lambda s, p: (s, p[1], 0, 0)) for a in fulls]
            + [slab(b) for b in gots],
            out_specs=[slab(b) for b in gots] + [mine(b) for b in gots]),
        out_shape=[_sds(b.shape, BF16) for b in gots] * 2,
        compiler_params=_params(("arbitrary",)))(place, *fulls, *gots)
    return out[:nw], out[nw:]


def _sum4(name, arrs, place):
    nw = len(arrs)

    def body(p_ref, *refs):
        for a_ref, o_ref in zip(refs[:nw], refs[nw:]):
            acc = a_ref[0].astype(F32)
            for s in range(1, 4):
                acc = acc + a_ref[s].astype(F32)
            o_ref[...] = acc

    return pl.pallas_call(
        body, name=name,
        grid_spec=pltpu.PrefetchScalarGridSpec(
            num_scalar_prefetch=1, grid=(1,),
            in_specs=[pl.BlockSpec(a.shape, lambda i, p: (0, 0, 0)) for a in arrs],
            out_specs=[pl.BlockSpec((None,) + a.shape[1:], lambda i, p: (p[1], 0, 0)) for a in arrs]),
        out_shape=[_sds((2,) + a.shape[1:], F32) for a in arrs],
        compiler_params=_params(("arbitrary",)))(place, *arrs)


def _small_update(gathered, params, slots):
    c1 = 1.0 - ADAM_B1 ** ADAM_STEP
    c2 = 1.0 - ADAM_B2 ** ADAM_STEP
    n = len(params)

    def body(g_ref, *refs):
        ins, sum_ref, outs = refs[:3 * n], refs[3 * n], refs[3 * n + 1:]
        g_all = g_ref[0:1, :]
        for d in range(1, 8):
            g_all = g_all + g_ref[d:d + 1, :]
        sum_ref[...] = g_all
        for k, (off, width) in enumerate(slots):
            w_ref, m_ref, v_ref = ins[3 * k:3 * k + 3]
            go_ref, d_ref, mo_ref, vo_ref = outs[4 * k:4 * k + 4]
            g = g_all[:, off:off + width]
            m = ADAM_B1 * m_ref[...] + (1.0 - ADAM_B1) * g
            v = ADAM_B2 * v_ref[...] + (1.0 - ADAM_B2) * (g * g)
            go_ref[...], mo_ref[...], vo_ref[...] = g, m, v
            d_ref[...] = -ADAM_LR * ((m / c1) / (jnp.sqrt(v / c2) + ADAM_EPS) + ADAM_WD * w_ref[...])

    flat = [a for p in params for a in p]
    out = pl.pallas_call(
        body, name="small_update",
        out_shape=[_sds((1, gathered.shape[1]), F32)] + [_sds(p[0].shape, F32) for p in params for _ in range(4)],
        compiler_params=pltpu.CompilerParams(vmem_limit_bytes=VMEM_LIMIT))(gathered, *flat)
    return out[0], [tuple(out[1 + 4 * k:5 + 4 * k]) for k in range(n)]


def _swiglu(ps, es):
    g, u = ps
    return g * _sigmoid(g) * u, g, u


def _swiglu_bwd(ps, es):
    g, u = es[0].astype(F32), es[1].astype(F32)
    sg = _sigmoid(g)
    return ps[0] * u * (sg * (1.0 + g * (1.0 - sg))), ps[0] * (g * sg)


def _merge(ps, es):
    ga, gm = [e.astype(F32) for e in es]
    return (_sigmoid(ga) * ps[0] + _sigmoid(gm) * ps[1],)


def _merge_bwd(ps, es):
    dm, a, b = ps
    ga, gm = [e.astype(F32) for e in es]
    sa, sm = _sigmoid(ga), _sigmoid(gm)
    return dm * sa, dm * sm, dm * a * (sa * (1.0 - sa)), dm * b * (sm * (1.0 - sm))


W_IN_PIECES = (("q", 512), ("kv", 256), ("mqk", 1024), ("mv", 512), ("mo", 512), ("if", 8),
               ("ga", 1024), ("gm", 1024))


def _local_step(x, tgt, pos_col, mod, sp, in_weights, late_weights, ffn_grads, mixer_grads):
    sh_m, sc_m, gate_m, sh_f, sc_f, gate_f = mod
    inv = ROPE_THETA ** (-2.0 * jnp.arange(HEAD_DIM // 2, dtype=F32) / HEAD_DIM)
    cos, sin = _rope_tables(pos_col, jnp.tile(inv, 4).reshape(1, 128))
    W = dict(in_weights(cos))
    h, pa, pqk, pvo, pif, pg = _proj_in(x, sp["g_pre_mix"], sc_m, sh_m, [
        (W["q+kv"], F32, 256), (W["mqk"], F32, 512), (W["mv+mo"], BF16, 512), (W["if"], F32, 128),
        (W["ga+gm"], BF16, 512)])
    ya = _attn_fwd(pa, cos, sin, sp["sinks"])
    qk = _conv_fwd(pqk, sp["conv_w"], sp["conv_b"])
    bcol = jnp.pad(sp["b_if"], ((0, 0), (0, 120)))
    brow = jnp.broadcast_to(sp["b_if"].reshape(8, 1), (8, 128))
    grow = pif[:, :8].T
    hm, cs, ns, ms = _mlstm_fwd(qk, pvo, pif, bcol, grow, brow)
    ym = _mlstm_out(hm, pvo, sp["norm_w"])
    W.update(late_weights(ym))
    w_fg, w_fu, w_fd = W["fg"], W["fu"], W["fd"]
    merged, = _mm("branches", [[(ya, W["ba"])], [(ym, W["bm"])]], [(pg, 0), (pg, 1)], _merge, [BF16],
                  cn=512, nt=True)
    wide, narrow = (D_MODEL, F32), (D_MODEL, BF16)
    mix, x1, h2 = _mm_rows("mix_out", [[(merged, W["out"])]],
                           [x, gate_m, sp["g_post_mix"], sp["g_pre_ffn"], sc_f, sh_f],
                           _res_norm_rows, [wide, wide, narrow], [], cn=512)
    act, gt, up = _mm("ffn_in", [[(h2, w_fg)], [(h2, w_fu)]], [], _swiglu, [BF16] * 3,
                      cn=256, nt=True)
    dy, dff, acc_l, loss = _mm_rows("ffn_down", [[(act, w_fd)]], [x1, tgt, gate_f, sp["g_post_ffn"]],
                                    _final_loss_rows, [wide, narrow], [(8, D_MODEL), (1, 128)], cn=512)

    G = {}
    dgt, dup = _mm("ffn_down_bwd", [[(dff, w_fd)]], [gt, up], _swiglu_bwd, [BF16, BF16],
                   cn=256, nt=True)
    g_fd, = _mm_tn_group("dw_ffn_down", [act], dff, BF16)
    g_fg, = _mm_tn_group("dw_ffn_gate", [dgt], h2, BF16)
    g_fu, = _mm_tn_group("dw_ffn_up", [dup], h2, BF16)
    tie = ffn_grads(g_fg, g_fu, g_fd)
    dx1, dmix, acc_r = _mm_rows(
        "ffn_in_bwd", [[(dgt, w_fg), (dup, w_fu)]],
        [x1, mix, dy, sc_f + tie, gate_m, sp["g_pre_ffn"], sp["g_post_mix"]],
        _res_norm_bwd_rows, [wide, narrow], [(8, D_MODEL)], cn=512, tm=256)
    d_a, d_m, dga, dgm = _mm("mix_out_bwd", [[(dmix, W["out"])], [(ya, W["ba"])], [(ym, W["bm"])]],
                             [(pg, 0), (pg, 1)], _merge_bwd, [BF16] * 4, cn=512, nt=True)
    G["out"], = _mm_tn_group("dw_out", [merged], dmix, BF16)
    dya, = _mm("branch_attn_bwd", [[(d_a, W["ba"])]], [], _first, [F32], cn=512)
    heads = MLSTM_HEADS * MLSTM_HEAD_DIM
    dhm, do_m, acc_n = _mm_rows("branch_mlstm_bwd", [[(d_m, W["bm"])]], [hm, pvo, sp["norm_w"]],
                                _mlstm_out_bwd_rows, [(heads, F32), (heads, BF16)], [(8, heads)], cn=512)
    G["ba"], = _mm_tn_group("dw_branch_attn", [d_a], ya, BF16)
    G["bm"], = _mm_tn_group("dw_branch_mlstm", [d_m], ym, BF16)
    dqk, dv_m, dgc, dgr = _mlstm_bwd(qk, pvo, pif, bcol, grow, brow, cs, ns, ms, dhm)
    dif, acc_g = _gate_bwd(dgc, dgr, pif, bcol)
    du, acc_c = _conv_bwd(pqk, sp["conv_w"], sp["conv_b"], dqk)
    dq_a, dkv, dsink = _attn_bwd(pa, cos, sin, sp["sinks"], dya)
    dproj = {"q": dq_a, "kv": dkv, "mqk": du, "mv": dv_m, "mo": do_m, "if": dif, "ga": dga, "gm": dgm}
    names = [k for k, _ in W_IN_PIECES]
    for part in (names[:4], names[4:]):
        G.update(zip(part, _mm_tn_group("dw_in_from_" + part[0], [dproj[k] for k in part], h, BF16)))
    w_tied = dict(W, **{"if": W["if"] + mixer_grads(G).astype(BF16)})
    dx, acc_p = _mm_rows("proj_bwd", [[(dproj[k], w_tied[k]) for k, _ in W_IN_PIECES]],
                         [x, dx1, sp["g_pre_mix"], sc_m], _pre_norm_bwd_rows, [wide], [(8, D_MODEL)], cn=512)

    small = {
        "mod": jnp.concatenate([acc_p[1], acc_p[0], acc_r[3], acc_r[1], acc_r[0], acc_l[0]]),
        "g_pre_mix": acc_p[2], "g_post_mix": acc_r[4], "b_if": acc_g[0, :8],
        "conv_w": acc_c[:CONV_WIDTH].reshape(-1), "conv_b": acc_c[CONV_WIDTH],
        "sinks": dsink[:, 0], "norm_w": acc_n[0], "g_pre_ffn": acc_r[2], "g_post_ffn": acc_l[1]}
    return loss, dx, small


IN_WIDTH = sum(n for _, n in W_IN_PIECES)
IN_SHARD = IN_WIDTH // 4
IN_SHARD_PAD = -(-IN_SHARD // 32) * 32


def _split_w_in(w_in_t):
    out, off, start = {}, 0, {}
    for k, n in W_IN_PIECES:
        out[k], start[k] = w_in_t[off:off + n], off
        off += n
    out["if"] = jnp.pad(out["if"], ((0, 120), (0, 0)))
    for name, first, last in (("q+kv", "q", "kv"), ("mv+mo", "mv", "mo"), ("ga+gm", "ga", "gm")):
        out[name] = w_in_t[start[first]:start[last] + out[last].shape[0]]
    return out


def _halves(a):
    return a.reshape(4, 2, a.shape[0] // 8, a.shape[1])


SMALL = (("b_ada", 6144), ("g_pre_mix", 1024), ("g_post_mix", 1024), ("b_if", 128), ("conv_w", 4096),
         ("conv_b", 1024), ("sinks", 128), ("norm_w", 512), ("g_pre_ffn", 1024), ("g_post_ffn", 1024))
SMALL_LEN = 8 * 2048


def _pack_small(vals):
    parts = []
    for k, n in SMALL:
        v = vals[k].reshape(-1)
        parts.append(jnp.pad(v, (0, n - v.shape[0])))
    flat = jnp.concatenate(parts)
    return jnp.pad(flat, (0, SMALL_LEN - flat.shape[0]))


def kernel(x, c, positions, w_ada, b_ada, g_pre_mix, g_post_mix, w_in, b_if, conv_w, conv_b, attn_sinks, mlstm_norm_w, w_branch_attn, w_branch_mlstm, w_out, g_pre_ffn, g_post_ffn, w_ffn_gate, w_ffn_up, w_ffn_down, loss_target, m_w_ada, m_b_ada, m_g_pre_mix, m_g_post_mix, m_w_in, m_b_if, m_conv_w, m_conv_b, m_attn_sinks, m_mlstm_norm_w, m_w_branch_attn, m_w_branch_mlstm, m_w_out, m_g_pre_ffn, m_g_post_ffn, m_w_ffn_gate, m_w_ffn_up, m_w_ffn_down, v_w_ada, v_b_ada, v_g_pre_mix, v_g_post_mix, v_w_in, v_b_if, v_conv_w, v_conv_b, v_attn_sinks, v_mlstm_norm_w, v_w_branch_attn, v_w_branch_mlstm, v_w_out, v_g_pre_ffn, v_g_post_ffn, v_w_ffn_gate, v_w_ffn_up, v_w_ffn_down):
    xi, yi, ci = _place()
    chip = 2 * xi + yi
    dev = 2 * chip + ci
    T = x.shape[1]
    ada_cols = w_ada.shape[2]

    place = jnp.stack([chip, ci]).astype(jnp.int32)

    def my_half(a):
        n = a.shape[0] // 2
        return lax.dynamic_slice_in_dim(a, ci * n, n, axis=0).astype(BF16)

    blk = jnp.concatenate([c.reshape(-1), conv_w.reshape(-1)]).reshape(8, 256)
    got = _all_gather8("gather_cond", blk, pltpu.VMEM).reshape(8, 2048)
    c_all = got[:, :D_MODEL].astype(BF16)
    conv_full = got[::2, D_MODEL:].reshape(4, CONV_WIDTH, -1).transpose(1, 0, 2).reshape(CONV_WIDTH, -1)

    b_sh = lax.dynamic_slice_in_dim(b_ada, chip * ada_cols, ada_cols, axis=1)
    mod_part, = _mm("ada_mod", [[(c_all, w_ada[0].astype(BF16))]], [b_sh],
                    lambda ps, es: (ps[0] + es[0],), [F32], cn=512, tm=8)
    mod_all = _all_gather8("gather_mod", mod_part, pltpu.VMEM).reshape(4, 2, 8, ada_cols)[:, 0]
    mod = lax.dynamic_index_in_dim(mod_all, dev, axis=1, keepdims=False).reshape(6, 1, D_MODEL)

    def gather_start(name, blks, after, sends, copies):
        return _split_start(name + "_start", blks, _place_blocks(name + "_place", blks, place),
                            sends, copies, after)

    w_in_t = jnp.pad(w_in[0].T, ((0, IN_SHARD_PAD - IN_SHARD), (0, 0)))
    in_started = gather_start("in_gather", [my_half(w_in_t)], mod, _gather_sends, 4)
    late_keys = ("fg", "fu", "fd", "out", "ba", "bm")
    late_started = gather_start(
        "late_gather",
        [my_half(w_ffn_gate[0].T), my_half(w_ffn_up[0].T), my_half(w_ffn_down[0]), my_half(w_out[0]),
         my_half(w_branch_attn[0].T), my_half(w_branch_mlstm[0].T)], in_started[4], _gather_sends_all, 7)
    mod = mod + (in_started[4][0, 0] + late_started[4][0, 0])

    def in_weights(after):
        g_in, = _forward_sibling("in_gather_forward",
                                 _split_wait("in_gather_wait", in_started, after, _gather_lands, 4))
        return _split_w_in(g_in.reshape(4, IN_SHARD_PAD, D_MODEL)[:, :IN_SHARD].reshape(IN_WIDTH, D_MODEL))

    def late_weights(after):
        lands = _split_wait("late_gather_wait", late_started, after, _gather_lands_all, 7)
        return {k: a.reshape(-1, a.shape[-1]) for k, a in zip(late_keys, lands)}

    sent = {}

    def scatter_start(name, groups):
        pairs, lands = _pair_sum(name + "_pair_sum", groups, _swap_halves_sibling(name + "_pair", groups), place)
        sent[name] = _split_start(name + "_start", pairs, lands, _scatter_sends, 3, pairs[0])
        return sent[name][4][0, 0]

    def ffn_grads(g_fg, g_fu, g_fd):
        return scatter_start("rs_ffn", [_halves(g_fg), _halves(g_fu), _halves(g_fd)])

    def mixer_grads(G):
        g_in_t = jnp.concatenate([G[k][:n] for k, n in W_IN_PIECES]).reshape(4, IN_SHARD, D_MODEL)
        g_in_t = jnp.pad(g_in_t, ((0, 0), (0, IN_SHARD_PAD - IN_SHARD), (0, 0)))
        return scatter_start("rs_mix", [g_in_t.reshape(4, 2, IN_SHARD_PAD // 2, D_MODEL), _halves(G["out"]),
                                        _halves(G["ba"]), _halves(G["bm"])])

    sp = {"g_pre_mix": g_pre_mix, "g_post_mix": g_post_mix, "b_if": b_if, "conv_w": conv_full,
          "conv_b": conv_b, "sinks": attn_sinks, "norm_w": mlstm_norm_w, "g_pre_ffn": g_pre_ffn,
          "g_post_ffn": g_post_ffn}
    loss, dx, small = _local_step(x[0], loss_target[0], positions.reshape(T, 1), [mod[i] for i in range(6)],
                                  sp, in_weights, late_weights, ffn_grads, mixer_grads)

    reds = (_sum4("rs_ffn_chip_sum", _split_wait("rs_ffn_wait", sent["rs_ffn"], dx, _scatter_lands, 3), place)
            + _sum4("rs_mix_chip_sum", _split_wait("rs_mix_wait", sent["rs_mix"], dx, _scatter_lands, 3), place))
    gsh = {k: s.reshape(-1, s.shape[-1])
           for k, s in zip(("fg", "fu", "fd", "w_in", "out", "ba", "bm"), _share_halves("rs_share", reds))}
    gsh["w_in"] = gsh["w_in"][:IN_SHARD]

    small["b_ada"] = small.pop("mod")
    vec = _pack_small(small).reshape(8, 2048)
    g_all = _all_gather8("gather_small", vec, pltpu.VMEM).reshape(8, SMALL_LEN)
    dmod_sh = lax.dynamic_slice_in_dim(g_all[:, :6 * D_MODEL], chip * ada_cols, ada_cols, axis=1)
    g_w_ada, = _mm_tn_group("dw_ada", [c_all], dmod_sh.astype(BF16), F32)

    smalls = {"b_ada": (b_ada, m_b_ada, v_b_ada), "g_pre_mix": (g_pre_mix, m_g_pre_mix, v_g_pre_mix),
              "g_post_mix": (g_post_mix, m_g_post_mix, v_g_post_mix), "b_if": (b_if, m_b_if, v_b_if),
              "conv_b": (conv_b, m_conv_b, v_conv_b), "sinks": (attn_sinks, m_attn_sinks, v_attn_sinks),
              "norm_w": (mlstm_norm_w, m_mlstm_norm_w, v_mlstm_norm_w),
              "g_pre_ffn": (g_pre_ffn, m_g_pre_ffn, v_g_pre_ffn),
              "g_post_ffn": (g_post_ffn, m_g_post_ffn, v_g_post_ffn)}
    offsets, off = {}, 0
    for k, width in SMALL:
        offsets[k], off = off, off + width
    g_sum, updates = _small_update(g_all, list(smalls.values()),
                                   [(offsets[k], t[0].shape[1]) for k, t in smalls.items()])
    g_conv = g_sum[:, offsets["conv_w"]:offsets["conv_w"] + CONV_WIDTH * D_MODEL].reshape(1, CONV_WIDTH, D_MODEL)
    g_conv = lax.dynamic_slice_in_dim(g_conv, chip * conv_w.shape[2], conv_w.shape[2], axis=2)

    res = dict(zip(smalls, updates))
    res["conv_w"] = (g_conv, *[o[None] for o in _adamw("adam_conv_w", conv_w[0], g_conv[0], m_conv_w[0], v_conv_w[0])])
    res["w_ada"] = (g_w_ada[None], *[o[None] for o in _adamw("adam_w_ada", w_ada[0], g_w_ada, m_w_ada[0], v_w_ada[0])])
    bigs = {"w_in": (w_in, m_w_in, v_w_in), "ba": (w_branch_attn, m_w_branch_attn, v_w_branch_attn),
            "bm": (w_branch_mlstm, m_w_branch_mlstm, v_w_branch_mlstm), "out": (w_out, m_w_out, v_w_out),
            "fg": (w_ffn_gate, m_w_ffn_gate, v_w_ffn_gate), "fu": (w_ffn_up, m_w_ffn_up, v_w_ffn_up),
            "fd": (w_ffn_down, m_w_ffn_down, v_w_ffn_down)}
    for k, (w, m, v) in bigs.items():
        if k in ("w_in", "fg", "fu"):
            res[k] = tuple(o.T[None] for o in (gsh[k], *_adamw("adam_" + k, w[0].T, gsh[k], m[0].T, v[0].T)))
        else:
            g = gsh[k].T if k in ("ba", "bm") else gsh[k]
            res[k] = (g[None], *[o[None] for o in _adamw("adam_" + k, w[0], g, m[0], v[0])])

    order = ("w_ada", "b_ada", "g_pre_mix", "g_post_mix", "w_in", "b_if", "conv_w", "conv_b", "sinks",
             "norm_w", "ba", "bm", "out", "g_pre_ffn", "g_post_ffn", "fg", "fu", "fd")
    total = lax.psum(loss[0, 0], ("x", "y", "c"))
    return (total, dx[None], *[res[k][0] for k in order], *[res[k][1] for k in order],
            *[res[k][2] for k in order], *[res[k][3] for k in order])
```

```python
import functools

import jax
import jax.numpy as jnp
from jax import lax
from jax.experimental import pallas as pl
from jax.experimental.pallas import tpu as pltpu

F32, BF16 = jnp.float32, jnp.bfloat16
MESH = pl.DeviceIdType.MESH

D_MODEL = 1024
N_Q_HEADS, N_KV_HEADS, HEAD_DIM, WINDOW = 8, 2, 64, 128
ROPE_THETA = 10000.0
MLSTM_HEADS, MLSTM_HEAD_DIM, MLSTM_CHUNK, CONV_WIDTH = 4, 128, 64, 4
D_FF = 2816
NORM_EPS = 1e-6
ADAM_LR, ADAM_B1, ADAM_B2, ADAM_EPS, ADAM_WD, ADAM_STEP = 0.001, 0.9, 0.999, 1e-08, 0.01, 10

VMEM_LIMIT = 56 * 1024 * 1024
ROW_TILE = 256
MM_TM = 512
MM_TT = 1024
ATTN_BLK = WINDOW
STEP_ROWS = 2 * MLSTM_CHUNK
NEG_INF = float("-inf")


def _params(sem):
    return pltpu.CompilerParams(dimension_semantics=sem, vmem_limit_bytes=VMEM_LIMIT)


def _sds(shape, dtype):
    return jax.ShapeDtypeStruct(shape, dtype)


def _sigmoid(x):
    return 1.0 / (1.0 + jnp.exp(-x))


def _dot(a, b, ca, cb):
    return lax.dot_general(a, b, (((ca,), (cb,)), ((), ())), preferred_element_type=F32)


def _bdot(a, b, ca, cb):
    return lax.dot_general(a, b, (((ca,), (cb,)), ((0,), (0,))), preferred_element_type=F32)


def _bdot_rows(a, b):
    return jnp.stack([_dot(a[h], b[h], 0, 0) for h in range(a.shape[0])])


def _mm(name, prods, extras, epi, out_dtypes, cn, nt=False, tm=MM_TM):
    flat = [ab for p in prods for ab in p]
    counts = [len(p) for p in prods]
    M = flat[0][0].shape[0]
    N = flat[0][1].shape[0 if nt else 1]
    tm = min(tm, M)
    n_in = 2 * len(flat) + len(extras)

    def body(*refs):
        ins, outs = refs[:n_in], refs[n_in:]
        for j in range(N // cn):
            cols = slice(j * cn, (j + 1) * cn)
            k, ps = 0, []
            for cnt in counts:
                acc = None
                for _ in range(cnt):
                    b = ins[k + 1][cols, :] if nt else ins[k + 1][:, cols]
                    d = _dot(ins[k][...], b, 1, 1 if nt else 0)
                    acc = d if acc is None else acc + d
                    k += 2
                ps.append(acc)
            res = epi(ps, [r[:, cols] for r in ins[k:]])
            for o, r in zip(outs, res):
                o[:, cols] = r.astype(o.dtype)

    in_specs, args = [], []
    for a, b in flat:
        in_specs.append(pl.BlockSpec((tm, a.shape[1]), lambda i: (i, 0)))
        in_specs.append(pl.BlockSpec(b.shape, lambda i: (0, 0), pipeline_mode=pl.Buffered(1)))
        args += [a, b]
    for e in extras:
        e, off = e if isinstance(e, tuple) else (e, 0)
        rows = 1 if e.shape[0] == 1 else tm
        in_specs.append(pl.BlockSpec((rows, N), lambda i, off=off, rows=rows: (0 if rows == 1 else i, off)))
        args.append(e)
    return pl.pallas_call(
        body, name=name, grid=(M // tm,), in_specs=in_specs,
        out_specs=[pl.BlockSpec((tm, N), lambda i: (i, 0)) for _ in out_dtypes],
        out_shape=[_sds((M, N), dt) for dt in out_dtypes],
        compiler_params=_params(("parallel",)))(*args)


def _mm_rows(name, prods, extras, epi, outs, accs, cn, nt=False, tm=MM_TM):
    flat = [ab for p in prods for ab in p]
    counts = [len(p) for p in prods]
    M = flat[0][0].shape[0]
    N = flat[0][1].shape[0 if nt else 1]
    tm = min(tm, M)
    n_mm, n_in, n_out = 2 * len(flat), 2 * len(flat) + len(extras), len(outs)

    def body(*refs):
        ins, out_refs, acc_refs = refs[:n_in], refs[n_in:n_in + n_out], refs[n_in + n_out:]

        @pl.when(pl.program_id(0) == 0)
        def _():
            for a in acc_refs:
                a[...] = jnp.zeros_like(a)

        chunks = [[] for _ in counts]
        for j in range(N // cn):
            cols = slice(j * cn, (j + 1) * cn)
            k = 0
            for p, cnt in enumerate(counts):
                acc = None
                for _ in range(cnt):
                    b = ins[k + 1][cols, :] if nt else ins[k + 1][:, cols]
                    d = _dot(ins[k][...], b, 1, 1 if nt else 0)
                    acc = d if acc is None else acc + d
                    k += 2
                chunks[p].append(acc)
        ps = [c[0] if len(c) == 1 else jnp.concatenate(c, axis=1) for c in chunks]
        res, incs = epi(ps, [r[...] for r in ins[n_mm:]])
        for o, r in zip(out_refs, res):
            o[...] = r.astype(o.dtype)
        for a, inc in zip(acc_refs, incs):
            a[...] += inc

    in_specs, args = [], []
    for a, b in flat:
        in_specs.append(pl.BlockSpec((tm, a.shape[1]), lambda i: (i, 0)))
        in_specs.append(pl.BlockSpec(b.shape, lambda i: (0, 0), pipeline_mode=pl.Buffered(1)))
        args += [a, b]
    for e in extras:
        rows = 1 if e.shape[0] == 1 else tm
        in_specs.append(pl.BlockSpec((rows, e.shape[1]), lambda i, rows=rows: (0 if rows == 1 else i, 0)))
        args.append(e)
    return pl.pallas_call(
        body, name=name, grid=(M // tm,), in_specs=in_specs,
        out_specs=[pl.BlockSpec((tm, w), lambda i: (i, 0)) for w, _ in outs]
        + [pl.BlockSpec(s, lambda i: (0, 0)) for s in accs],
        out_shape=[_sds((M, w), dt) for w, dt in outs] + [_sds(s, F32) for s in accs],
        compiler_params=_params(("arbitrary",)))(*args)


def _mm_tn_group(name, pieces, b, out_dtype, tt=MM_TT):
    T, N = b.shape
    tt = min(tt, T)
    steps, n = T // tt, len(pieces)

    def body(*refs):
        a_refs, b_ref, out_refs, accs = refs[:n], refs[n], refs[n + 1:2 * n + 1], refs[2 * n + 1:]
        t = pl.program_id(0)

        @pl.when(t == 0)
        def _():
            for acc in accs:
                acc[...] = jnp.zeros_like(acc)

        for a_ref, acc in zip(a_refs, accs):
            acc[...] += _dot(a_ref[...], b_ref[...], 0, 0)

        @pl.when(t == steps - 1)
        def _():
            for o_ref, acc in zip(out_refs, accs):
                o_ref[...] = acc[...].astype(o_ref.dtype)

    return pl.pallas_call(
        body, name=name, grid=(steps,),
        in_specs=[pl.BlockSpec((tt, a.shape[1]), lambda t: (t, 0)) for a in pieces]
        + [pl.BlockSpec((tt, N), lambda t: (t, 0))],
        out_specs=[pl.BlockSpec((a.shape[1], N), lambda t: (0, 0)) for a in pieces],
        out_shape=[_sds((a.shape[1], N), out_dtype) for a in pieces],
        scratch_shapes=[pltpu.VMEM((a.shape[1], N), F32) for a in pieces],
        compiler_params=_params(("arbitrary",)))(*pieces, b)


def _first(ps, es):
    return (ps[0],)


def _rows(name, body, ins, out_shapes, T, tr=ROW_TILE):
    tr = min(tr, T)

    def spec(shape):
        if shape[0] == T:
            return pl.BlockSpec((tr,) + tuple(shape[1:]), lambda i: (i,) + (0,) * (len(shape) - 1))
        return pl.BlockSpec(tuple(shape), lambda i: (0,) * len(shape))

    return pl.pallas_call(
        body, name=name, grid=(T // tr,),
        in_specs=[spec(a.shape) for a in ins], out_specs=[spec(s.shape) for s in out_shapes],
        out_shape=out_shapes, compiler_params=_params(("arbitrary",)))(*ins)


def _rms(x):
    r = lax.rsqrt(jnp.mean(x * x, axis=-1, keepdims=True) + NORM_EPS)
    return x * r, r


def _rms_bwd(dxn, xn, r):
    return r * (dxn - xn * jnp.mean(dxn * xn, axis=-1, keepdims=True))


def _colsum(v):
    return jnp.sum(v, axis=0, keepdims=True)


def _proj_in(x, g, sc, sh, groups):
    T = x.shape[0]
    tm = min(MM_TM, T)
    ng = len(groups)

    def body(x_ref, g_ref, sc_ref, sh_ref, *rest):
        w_refs, h_ref, out_refs = rest[:ng], rest[ng], rest[ng + 1:]
        xn, _ = _rms(x_ref[...])
        h = (xn * g_ref[...] * (1.0 + sc_ref[...]) + sh_ref[...]).astype(BF16)
        h_ref[...] = h
        for w_ref, o_ref, (w, _, cn) in zip(w_refs, out_refs, groups):
            for j in range(w.shape[0] // cn):
                cols = slice(j * cn, (j + 1) * cn)
                o_ref[:, cols] = _dot(h, w_ref[cols, :], 1, 1).astype(o_ref.dtype)

    row = pl.BlockSpec((1, D_MODEL), lambda i: (0, 0))
    tile = lambda w: pl.BlockSpec((tm, w), lambda i: (i, 0))
    return pl.pallas_call(
        body, name="proj_in", grid=(T // tm,),
        in_specs=[tile(D_MODEL), row, row, row] + [
            pl.BlockSpec(w.shape, lambda i: (0, 0), pipeline_mode=pl.Buffered(1)) for w, _, _ in groups],
        out_specs=[tile(D_MODEL)] + [tile(w.shape[0]) for w, _, _ in groups],
        out_shape=[_sds((T, D_MODEL), BF16)] + [_sds((T, w.shape[0]), dt) for w, dt, _ in groups],
        compiler_params=_params(("parallel",)))(x, g, sc, sh, *[w for w, _, _ in groups])


def _acc_rows(rows):
    w = rows[0].shape[1]
    return jnp.concatenate(rows + [jnp.zeros((8 - len(rows), w), F32)], axis=0)


def _res_norm_rows(ps, es):
    mix = ps[0]
    x, gate, gp, g2, sc, sh = es
    mh, _ = _rms(mix)
    x1 = x + gate * (mh * gp)
    xn, _ = _rms(x1)
    return [mix, x1, xn * g2 * (1.0 + sc) + sh], []


def _final_loss_rows(ps, es):
    x1, tgt, gate, gp = es
    fh, r = _rms(ps[0])
    e = x1 + gate * (fh * gp) - tgt
    loss = 0.5 * jnp.sum(jnp.mean(e * e, axis=-1, keepdims=True))
    dy = e * (1.0 / D_MODEL)
    acc = _acc_rows([_colsum(dy * fh * gp), _colsum(dy * gate * fh)])
    return [dy, _rms_bwd(dy * gate * gp, fh, r)], [acc, jnp.full((1, 128), loss, F32)]


def _res_norm_bwd_rows(ps, es):
    dh = ps[0]
    x1, mix, dy, sc, gate, g2, gp = es
    xn, r1 = _rms(x1)
    rows = [_colsum(dh * xn * g2), _colsum(dh), _colsum(dh * (1.0 + sc) * xn)]
    dx1 = dy + _rms_bwd(dh * (1.0 + sc) * g2, xn, r1)
    mh, rm = _rms(mix)
    rows += [_colsum(dx1 * mh * gp), _colsum(dx1 * gate * mh)]
    return [dx1, _rms_bwd(dx1 * gate * gp, mh, rm)], [_acc_rows(rows)]


def _pre_norm_bwd_rows(ps, es):
    dh = ps[0]
    x, dx1, g, sc = es
    xn, r = _rms(x)
    rows = [_colsum(dh * xn * g), _colsum(dh), _colsum(dh * (1.0 + sc) * xn)]
    return [dx1 + _rms_bwd(dh * (1.0 + sc) * g, xn, r)], [_acc_rows(rows)]


def _rope_tables(pos_col, inv_freq):
    T = pos_col.shape[0]

    def body(p_ref, f_ref, c_ref, s_ref):
        ang = p_ref[...].astype(F32) * f_ref[...]
        lane = lax.broadcasted_iota(jnp.int32, ang.shape, 1)
        c_ref[...] = jnp.cos(ang)
        s_ref[...] = jnp.where(lane % HEAD_DIM < HEAD_DIM // 2, -1.0, 1.0) * jnp.sin(ang)

    return _rows("rope_tables", body, [pos_col, inv_freq],
                 [_sds((T, 128), F32), _sds((T, 128), F32)], T, tr=512)


def _swap_halves(t):
    W = t.shape[1]
    lane = lax.broadcasted_iota(jnp.int32, t.shape, 1)
    half = HEAD_DIM // 2
    return jnp.where(lane % HEAD_DIM < half, pltpu.roll(t, W - half, 1), pltpu.roll(t, half, 1))


def _widen(c, W):
    return c if W == 128 else jnp.concatenate([c] * (W // 128), axis=1)


def _rope(t, c, s):
    W = t.shape[1]
    return t * _widen(c, W) + _swap_halves(t) * _widen(s, W)


def _unrope(dy, c, s):
    W = dy.shape[1]
    return dy * _widen(c, W) + _swap_halves(dy * _widen(s, W))


def _attn_mask(n):
    qi = lax.broadcasted_iota(jnp.int32, (ATTN_BLK, 2 * ATTN_BLK), 0)
    kj = lax.broadcasted_iota(jnp.int32, (ATTN_BLK, 2 * ATTN_BLK), 1)
    rel = kj - ATTN_BLK
    return (rel <= qi) & (qi - rel < WINDOW) & ((n > 0) | (kj >= ATTN_BLK))


def _attn_load(cur, prv, cc, sc, cp, sp):
    x, xp = cur[...], prv[...]
    q = _rope(x[:, :512], cc[...], sc[...]) * (HEAD_DIM ** -0.5)
    k = jnp.concatenate([_rope(xp[:, 512:640], cp[...], sp[...]),
                         _rope(x[:, 512:640], cc[...], sc[...])], axis=0)
    v = jnp.concatenate([xp[:, 640:768], x[:, 640:768]], axis=0)
    return q, k, v


ROLLED = tuple(h for h in range(N_Q_HEADS) if h % 2 != h // (N_Q_HEADS // N_KV_HEADS))


def _pair_heads(t):
    half = lax.broadcasted_iota(jnp.int32, (ATTN_BLK, 128), 1) // HEAD_DIM
    return jnp.stack([jnp.where(half == h % 2, t[:, 128 * (h // 2):128 * (h // 2) + 128], 0.0)
                      for h in range(N_Q_HEADS)])


def _kv_heads(t):
    half = lax.broadcasted_iota(jnp.int32, t.shape, 1) // HEAD_DIM
    tr = pltpu.roll(t, HEAD_DIM, 1)
    return jnp.stack([jnp.where(half == h % 2, tr if h in ROLLED else t, 0.0)
                      for h in range(N_Q_HEADS)])


def _sink_column(snk):
    return jnp.stack([jnp.full((1, 1), snk[0, h], F32) for h in range(N_Q_HEADS)])


def _attn_probs(qh, kh, mask, sink):
    s = jnp.where(mask, _bdot(qh, kh, 2, 2), NEG_INF)
    m = jnp.maximum(jnp.max(s, axis=-1, keepdims=True), sink)
    p = jnp.exp(s - m)
    es = jnp.exp(sink - m)
    rl = 1.0 / (jnp.sum(p, axis=-1, keepdims=True) + es)
    return p, es, rl


def _attn_specs(order):
    blk = lambda w: pl.BlockSpec((ATTN_BLK, w), lambda s: (order(s), 0))
    prv = lambda w: pl.BlockSpec((ATTN_BLK, w), lambda s: (jnp.maximum(order(s) - 1, 0), 0))
    return [blk(768), prv(768), blk(128), blk(128), prv(128), prv(128),
            pl.BlockSpec(memory_space=pltpu.SMEM)]


def _attn_fwd(pa, cos, sin, sinks):
    T = pa.shape[0]
    nb = T // ATTN_BLK

    def body(cur, prv, cc, sc, cp, sp, snk, y_ref):
        n = pl.program_id(0)
        q, k, v = _attn_load(cur, prv, cc, sc, cp, sp)
        mask = _attn_mask(n)
        half_q = lax.broadcasted_iota(jnp.int32, (ATTN_BLK, 128), 1) // HEAD_DIM
        half_k = lax.broadcasted_iota(jnp.int32, k.shape, 1) // HEAD_DIM
        moved = (pltpu.roll(k, HEAD_DIM, 1), pltpu.roll(v, HEAD_DIM, 1))
        for pair in range(N_Q_HEADS // 2):
            o = None
            for a in range(2):
                h = 2 * pair + a
                ku, vu = moved if h in ROLLED else (k, v)
                qh = jnp.where(half_q == a, q[:, 128 * pair:128 * pair + 128], 0.0).astype(BF16)
                kh = jnp.where(half_k == a, ku, 0.0).astype(BF16)
                vh = jnp.where(half_k == a, vu, 0.0).astype(BF16)
                s = jnp.where(mask, _dot(qh, kh, 1, 1), NEG_INF)
                m = jnp.maximum(jnp.max(s, axis=-1, keepdims=True), snk[0, h])
                p = jnp.exp(s - m)
                rl = 1.0 / (jnp.sum(p, axis=-1, keepdims=True) + jnp.exp(snk[0, h] - m))
                oh = _dot(p.astype(BF16), vh, 1, 0) * rl
                o = oh if o is None else o + oh
            y_ref[:, 128 * pair:128 * pair + 128] = o.astype(BF16)

    return pl.pallas_call(
        body, name="attn_fwd", grid=(nb,), in_specs=_attn_specs(lambda s: s),
        out_specs=pl.BlockSpec((ATTN_BLK, 512), lambda n: (n, 0)),
        out_shape=_sds((T, 512), BF16), compiler_params=_params(("parallel",)))(
            pa, pa, cos, sin, cos, sin, sinks)


def _attn_bwd(pa, cos, sin, sinks, dy):
    T = pa.shape[0]
    nb = T // ATTN_BLK
    rev = lambda s: nb - 1 - s

    def body(cur, prv, cc, sc, cp, sp, snk, dy_ref, dq_ref, dkv_ref, dsink_ref, carry):
        n = rev(pl.program_id(0))

        @pl.when(pl.program_id(0) == 0)
        def _():
            dsink_ref[...] = jnp.zeros_like(dsink_ref)
            carry[...] = jnp.zeros_like(carry)

        q, k, v = _attn_load(cur, prv, cc, sc, cp, sp)
        qh, kh, vh = _pair_heads(q).astype(BF16), _kv_heads(k).astype(BF16), _kv_heads(v).astype(BF16)
        p, es, rl = _attn_probs(qh, kh, _attn_mask(n), _sink_column(snk))
        pn = p * rl
        do = _pair_heads(dy_ref[...]).astype(BF16)
        dp = _bdot(do, vh, 2, 2)
        delta = jnp.sum(pn * dp, axis=-1, keepdims=True)
        ds = (pn * (dp - delta)).astype(BF16)
        dsink = es * rl * delta
        dq = _bdot(ds, kh, 2, 1) * (HEAD_DIM ** -0.5)
        dkh = _bdot_rows(ds, qh)
        dvh = _bdot_rows(pn.astype(BF16), do)

        def fold(t):
            same = [t[h] for h in range(N_Q_HEADS) if h not in ROLLED]
            moved = [t[h] for h in ROLLED]
            return sum(same[1:], same[0]) + pltpu.roll(sum(moved[1:], moved[0]), HEAD_DIM, 1)

        dk, dv = fold(dkh), fold(dvh)
        for h in range(N_Q_HEADS):
            dsink_ref[h:h + 1, :] += -jnp.sum(dsink[h])
        for pair in range(N_Q_HEADS // 2):
            dq_ref[:, 128 * pair:128 * pair + 128] = _unrope(
                dq[2 * pair] + dq[2 * pair + 1], cc[...], sc[...]).astype(BF16)
        dkv_ref[:, 0:128] = _unrope(dk[ATTN_BLK:] + carry[:, 0:128], cc[...], sc[...]).astype(BF16)
        dkv_ref[:, 128:256] = (dv[ATTN_BLK:] + carry[:, 128:256]).astype(BF16)
        carry[:, 0:128] = dk[:ATTN_BLK]
        carry[:, 128:256] = dv[:ATTN_BLK]

    blk = lambda w: pl.BlockSpec((ATTN_BLK, w), lambda s: (rev(s), 0))
    return pl.pallas_call(
        body, name="attn_bwd", grid=(nb,), in_specs=_attn_specs(rev) + [blk(512)],
        out_specs=[blk(512), blk(256), pl.BlockSpec((8, 128), lambda s: (0, 0))],
        out_shape=[_sds((T, 512), BF16), _sds((T, 256), BF16), _sds((8, 128), F32)],
        scratch_shapes=[pltpu.VMEM((ATTN_BLK, 256), F32)],
        compiler_params=_params(("arbitrary",)))(pa, pa, cos, sin, cos, sin, sinks, dy)


CONV_COLS = 2 * MLSTM_HEADS * MLSTM_HEAD_DIM


def _conv_pre(cur_ref, halo_ref, w_ref, b_ref, i, tr):
    xx = jnp.concatenate([jnp.where(i > 0, halo_ref[...], 0.0), cur_ref[...]], axis=0)
    taps = [(pltpu.roll(xx, CONV_WIDTH - 1 - j, 0) if j < CONV_WIDTH - 1 else xx)[8:8 + tr]
            for j in range(CONV_WIDTH)]
    pre = b_ref[...]
    for j in range(CONV_WIDTH):
        pre = pre + taps[j] * w_ref[j:j + 1, :]
    return pre, taps


def _conv_specs(T, tr):
    return [pl.BlockSpec((tr, CONV_COLS), lambda i: (i, 0)),
            pl.BlockSpec((8, CONV_COLS), lambda i: (jnp.maximum(i * (tr // 8) - 1, 0), 0)),
            pl.BlockSpec((CONV_WIDTH, CONV_COLS), lambda i: (0, 0)),
            pl.BlockSpec((1, CONV_COLS), lambda i: (0, 0))]


def _conv_fwd(pm, w, b):
    T = pm.shape[0]
    tr = min(ROW_TILE, T)

    def body(cur_ref, halo_ref, w_ref, b_ref, o_ref):
        pre, _ = _conv_pre(cur_ref, halo_ref, w_ref, b_ref, pl.program_id(0), tr)
        o_ref[...] = pre * _sigmoid(pre)

    return pl.pallas_call(
        body, name="conv_fwd", grid=(T // tr,), in_specs=_conv_specs(T, tr),
        out_specs=pl.BlockSpec((tr, CONV_COLS), lambda i: (i, 0)),
        out_shape=_sds((T, CONV_COLS), F32), compiler_params=_params(("parallel",)))(pm, pm, w, b)


def _conv_bwd(pqk, w, b, dqk):
    T = pqk.shape[0]
    tr = min(ROW_TILE, T)
    nt = T // tr

    def body(cur_ref, prev_ref, next_ref, w_ref, b_ref, d_ref, dnext_ref, du_ref, acc_ref):
        i = pl.program_id(0)

        @pl.when(i == 0)
        def _():
            acc_ref[...] = jnp.zeros_like(acc_ref)

        last = i == nt - 1
        xx = jnp.concatenate([jnp.where(i > 0, prev_ref[...], 0.0), cur_ref[...],
                              jnp.where(last, 0.0, next_ref[...])], axis=0)
        taps = [(pltpu.roll(xx, CONV_WIDTH - 1 - j, 0) if j < CONV_WIDTH - 1 else xx)[8:16 + tr]
                for j in range(CONV_WIDTH)]
        pre = b_ref[...]
        for j in range(CONV_WIDTH):
            pre = pre + taps[j] * w_ref[j:j + 1, :]
        sg = _sigmoid(pre)
        dd = jnp.concatenate([d_ref[...], jnp.where(last, 0.0, dnext_ref[...])], axis=0)
        dpre = dd * (sg * (1.0 + pre * (1.0 - sg)))
        for j in range(CONV_WIDTH):
            acc_ref[j:j + 1, :] += _colsum(dpre[:tr] * taps[j][:tr])
        acc_ref[CONV_WIDTH:CONV_WIDTH + 1, :] += _colsum(dpre[:tr])
        du = dpre[:tr] * w_ref[CONV_WIDTH - 1:CONV_WIDTH, :]
        for j in range(CONV_WIDTH - 1):
            k = CONV_WIDTH - 1 - j
            du = du + pltpu.roll(dpre, tr + 8 - k, 0)[:tr] * w_ref[j:j + 1, :]
        du_ref[...] = du.astype(BF16)

    tile = pl.BlockSpec((tr, CONV_COLS), lambda i: (i, 0))
    after = pl.BlockSpec((8, CONV_COLS), lambda i: (jnp.minimum((i + 1) * (tr // 8), T // 8 - 1), 0))
    before = pl.BlockSpec((8, CONV_COLS), lambda i: (jnp.maximum(i * (tr // 8) - 1, 0), 0))
    return pl.pallas_call(
        body, name="conv_bwd", grid=(nt,),
        in_specs=[tile, before, after, pl.BlockSpec((CONV_WIDTH, CONV_COLS), lambda i: (0, 0)),
                  pl.BlockSpec((1, CONV_COLS), lambda i: (0, 0)), tile, after],
        out_specs=[tile, pl.BlockSpec((8, CONV_COLS), lambda i: (0, 0))],
        out_shape=[_sds((T, CONV_COLS), BF16), _sds((8, CONV_COLS), F32)],
        compiler_params=_params(("arbitrary",)))(pqk, pqk, pqk, w, b, dqk, dqk)


def _log_sigmoid(x):
    return jnp.minimum(x, 0.0) - jnp.log1p(jnp.exp(-jnp.abs(x)))


def _chunk_cumsum(x, axis):
    idx = lax.broadcasted_iota(jnp.int32, x.shape, axis) % MLSTM_CHUNK
    k = 1
    while k < MLSTM_CHUNK:
        x = x + jnp.where(idx >= k, pltpu.roll(x, k, axis), 0.0)
        k *= 2
    return x


def _chunk_rev_cumsum(x, axis):
    n = x.shape[axis]
    idx = lax.broadcasted_iota(jnp.int32, x.shape, axis) % MLSTM_CHUNK
    k = 1
    while k < MLSTM_CHUNK:
        x = x + jnp.where(idx < MLSTM_CHUNK - k, pltpu.roll(x, n - k, axis), 0.0)
        k *= 2
    return x


def _mlstm_gates(gc_ref, bc_ref, gr_ref, br_ref):
    gc = gc_ref[...] + bc_ref[...]
    gr = gr_ref[...] + br_ref[...]
    return gc, _chunk_cumsum(_log_sigmoid(gc), 0), gr, _chunk_cumsum(_log_sigmoid(gr), 1)


def _heads(ref, base=0):
    D = MLSTM_HEAD_DIM
    return jnp.stack([ref[:, base + D * h:base + D * h + D] for h in range(MLSTM_HEADS)])


def _mlstm_inputs(q_ref, k_ref, v_ref, gc, bc, gr, br):
    H = MLSTM_HEADS
    q, v = _heads(q_ref), _heads(v_ref)
    ks = _heads(k_ref) * (MLSTM_HEAD_DIM ** -0.5)
    return dict(
        q=q, ks=ks, qb=q.astype(BF16), kb=ks.astype(BF16), vb=v.astype(BF16),
        b_col=jnp.stack([bc[:, H + h:H + h + 1] for h in range(H)]),
        i_col=jnp.stack([gc[:, h:h + 1] for h in range(H)]),
        b_row=jnp.stack([br[H + h:H + h + 1, :] for h in range(H)]),
        i_row=jnp.stack([gr[h:h + 1, :] for h in range(H)]))


def _mlstm_head(f, c_prev, n_prev, m_prev):
    L = MLSTM_CHUNK
    q, qb = f["q"], f["qb"]
    t = lax.broadcasted_iota(jnp.int32, (1, 2 * L, 2 * L), 1)
    s = lax.broadcasted_iota(jnp.int32, (1, 2 * L, 2 * L), 2)
    mask = (t // L == s // L) & (s <= t)
    d = jnp.where(mask, f["b_col"] - f["b_row"] + f["i_row"], NEG_INF)
    row = lax.broadcasted_iota(jnp.int32, (1, 2 * L, 1), 1)
    inter = f["b_col"] + jnp.where(row < L, m_prev[0], m_prev[1])
    m_t = jnp.maximum(inter, jnp.max(d, axis=-1, keepdims=True))
    w_intra = jnp.exp(d - m_t)
    w_inter = jnp.exp(inter - m_t)
    sc = _bdot(qb, f["kb"], 2, 2) * w_intra
    qc = jnp.concatenate([_bdot(qb[:, :L], c_prev[0].astype(BF16), 2, 1),
                          _bdot(qb[:, L:], c_prev[1].astype(BF16), 2, 1)], axis=1)
    qn = jnp.concatenate([jnp.sum(q[:, :L] * n_prev[0], axis=-1, keepdims=True),
                          jnp.sum(q[:, L:] * n_prev[1], axis=-1, keepdims=True)], axis=1)
    num = _bdot(sc.astype(BF16), f["vb"], 2, 1) + w_inter * qc
    den = jnp.sum(sc, axis=-1, keepdims=True) + w_inter * qn
    return dict(f, w_intra=w_intra, w_inter=w_inter, sc=sc, qc=qc, qn=qn, num=num, den=den,
                floor=jnp.exp(-m_t))


def _mlstm_update(f, ch, c, n, m):
    L = MLSTM_CHUNK
    rows = slice(L * ch, L * ch + L)
    b_col = f["b_col"][:, rows]
    g_last = b_col[:, L - 1:L]
    a_col = g_last - b_col + f["i_col"][:, rows]
    m_new = jnp.maximum(g_last + m, jnp.max(a_col, axis=1, keepdims=True))
    decay = jnp.exp(g_last + m - m_new)
    e_a = jnp.exp(a_col - m_new)
    kw = f["ks"][:, rows] * e_a
    c_new = decay * c + _bdot_rows(kw.astype(BF16), f["vb"][:, rows])
    n_new = decay * n + jnp.sum(kw, axis=1, keepdims=True)
    return c_new, n_new, m_new, decay, e_a, kw


def _mlstm_specs(T, order):
    blk = lambda w, col: pl.BlockSpec((STEP_ROWS, w), lambda s: (order(s), col))
    return [blk(512, 0), blk(512, 1), blk(512, 0), blk(128, 0),
            pl.BlockSpec((1, 128), lambda s: (0, 0)),
            pl.BlockSpec((8, STEP_ROWS), lambda s: (0, order(s))),
            pl.BlockSpec((8, 128), lambda s: (0, 0))]


def _lanes(m):
    return jnp.broadcast_to(m, m.shape[:-1] + (128,))


def _mlstm_fwd(qk, pm, gcol, bcol, grow, brow, norm_w):
    T = qk.shape[0]
    steps = T // STEP_ROWS
    H, D = MLSTM_HEADS, MLSTM_HEAD_DIM

    def body(q_ref, k_ref, v_ref, gc_ref, bc_ref, gr_ref, br_ref, o_ref, w_ref,
             h_ref, y_ref, cs_ref, ns_ref, ms_ref, c_st, n_st, m_st):
        @pl.when(pl.program_id(0) == 0)
        def _():
            c_st[...] = jnp.zeros_like(c_st)
            n_st[...] = jnp.zeros_like(n_st)
            m_st[...] = jnp.zeros_like(m_st)

        f = _mlstm_inputs(q_ref, k_ref, v_ref, *_mlstm_gates(gc_ref, bc_ref, gr_ref, br_ref))
        c0, n0, m0 = c_st[...], n_st[...], m_st[:, :, 0:1]
        c1, n1, m1, _, _, _ = _mlstm_update(f, 0, c0, n0, m0)
        c2, n2, m2, _, _, _ = _mlstm_update(f, 1, c1, n1, m1)
        f = _mlstm_head(f, (c0, c1), (n0, n1), (m0, m1))
        h = f["num"] / jnp.maximum(jnp.abs(f["den"]), f["floor"])
        hn, _ = _head_norm(h)
        w = jnp.stack([w_ref[:, D * hd:D * hd + D] for hd in range(H)])
        y = _sigmoid(_heads(o_ref).astype(F32)) * hn * w
        for hd in range(H):
            h_ref[:, D * hd:D * hd + D] = h[hd]
            y_ref[:, D * hd:D * hd + D] = y[hd].astype(BF16)
        cs_ref[0], cs_ref[1] = c0, c1
        ns_ref[0], ns_ref[1] = n0, n1
        ms_ref[0], ms_ref[1] = _lanes(m0), _lanes(m1)
        c_st[...], n_st[...], m_st[...] = c2, n2, _lanes(m2)

    vec = pl.BlockSpec((2, H, 1, 128), lambda s: (s, 0, 0, 0))
    rows = pl.BlockSpec((STEP_ROWS, 512), lambda s: (s, 0))
    return pl.pallas_call(
        body, name="mlstm_fwd", grid=(steps,),
        in_specs=_mlstm_specs(T, lambda s: s) + [pl.BlockSpec((STEP_ROWS, 512), lambda s: (s, 1)),
                                                 pl.BlockSpec((1, 512), lambda s: (0, 0))],
        out_specs=[rows, rows, pl.BlockSpec((2, H, 128, 128), lambda s: (s, 0, 0, 0)), vec, vec],
        out_shape=[_sds((T, 512), F32), _sds((T, 512), BF16), _sds((2 * steps, H, 128, 128), F32),
                   _sds((2 * steps, H, 1, 128), F32), _sds((2 * steps, H, 1, 128), F32)],
        scratch_shapes=[pltpu.VMEM((H, 128, 128), F32), pltpu.VMEM((H, 1, 128), F32),
                        pltpu.VMEM((H, 1, 128), F32)],
        compiler_params=_params(("arbitrary",)))(qk, qk, pm, gcol, bcol, grow, brow, pm, norm_w)


def _mlstm_bwd(qk, pm, gcol, bcol, grow, brow, cs, ns, ms, dh):
    T = qk.shape[0]
    steps = T // STEP_ROWS
    H, L, D = MLSTM_HEADS, MLSTM_CHUNK, MLSTM_HEAD_DIM
    rev = lambda s: steps - 1 - s

    def body(q_ref, k_ref, v_ref, gc_ref, bc_ref, gr_ref, br_ref, cs_ref, ns_ref, ms_ref, dh_ref,
             dqk_ref, dv_ref, dgc_ref, dgr_ref, dc_st, dn_st):
        @pl.when(pl.program_id(0) == 0)
        def _():
            dc_st[...] = jnp.zeros_like(dc_st)
            dn_st[...] = jnp.zeros_like(dn_st)

        f = _mlstm_inputs(q_ref, k_ref, v_ref, *_mlstm_gates(gc_ref, bc_ref, gr_ref, br_ref))
        c_prev = (cs_ref[0], cs_ref[1])
        n_prev = (ns_ref[0], ns_ref[1])
        m_prev = (ms_ref[0, :, :, 0:1], ms_ref[1, :, :, 0:1])
        f = _mlstm_head(f, c_prev, n_prev, m_prev)
        big = jnp.abs(f["den"]) > f["floor"]
        rden = 1.0 / jnp.where(big, jnp.abs(f["den"]), f["floor"])
        dnum = _heads(dh_ref) * rden
        hdh = jnp.sum(f["num"] * dnum, axis=-1, keepdims=True)
        dden = jnp.where(big, -hdh * rden * jnp.sign(f["den"]), 0.0)
        dnum_b = dnum.astype(BF16)
        dsc = _bdot(dnum_b, f["vb"], 2, 2) + dden
        g = dsc * f["sc"]
        dv = _bdot_rows(f["sc"].astype(BF16), dnum_b)
        dqk_ = (dsc * f["w_intra"]).astype(BF16)
        dq = _bdot(dqk_, f["kb"], 2, 1)
        dks = _bdot_rows(dqk_, f["qb"])
        wdn = f["w_inter"] * dnum
        wdn_b = wdn.astype(BF16)
        wdd = f["w_inter"] * dden
        u = jnp.sum(f["qc"] * wdn, axis=-1, keepdims=True) + wdd * f["qn"]
        dks_s, dv_s, z_s, dg_s = [None, None], [None, None], [None, None], [None, None]
        dcn, dnn = dc_st[...], dn_st[...]
        for ch in (1, 0):
            rows = slice(L * ch, L * ch + L)
            _, _, _, decay, e_a, kw = _mlstm_update(f, ch, c_prev[ch], n_prev[ch], m_prev[ch])
            dcn_b = dcn.astype(BF16)
            dkw = _bdot(f["vb"][:, rows], dcn_b, 2, 2) + dnn
            dks_s[ch] = e_a * dkw
            dv_s[ch] = _bdot(kw.astype(BF16), dcn_b, 2, 1)
            z_s[ch] = e_a * jnp.sum(f["ks"][:, rows] * dkw, axis=-1, keepdims=True)
            dg_s[ch] = jnp.sum(z_s[ch], axis=1, keepdims=True) + decay * (
                jnp.sum(c_prev[ch] * dcn, axis=(1, 2), keepdims=True)
                + jnp.sum(n_prev[ch] * dnn, axis=(1, 2), keepdims=True))
            dcn = decay * dcn + _bdot_rows(f["qb"][:, rows], wdn_b[:, rows])
            dnn = decay * dnn + jnp.sum(wdd[:, rows] * f["q"][:, rows], axis=1, keepdims=True)
        dc_st[...], dn_st[...] = dcn, dnn
        dq = dq + jnp.concatenate(
            [_bdot(wdn_b[:, :L], c_prev[0].astype(BF16), 2, 2) + wdd[:, :L] * n_prev[0],
             _bdot(wdn_b[:, L:], c_prev[1].astype(BF16), 2, 2) + wdd[:, L:] * n_prev[1]], axis=1)
        dks = (dks + jnp.concatenate(dks_s, axis=1)) * (D ** -0.5)
        dv = dv + jnp.concatenate(dv_s, axis=1)
        z = jnp.concatenate(z_s, axis=1)
        row = lax.broadcasted_iota(jnp.int32, (1, STEP_ROWS, 1), 1)
        dg_col = jnp.where(row == L - 1, dg_s[0], 0.0) + jnp.where(row == 2 * L - 1, dg_s[1], 0.0)
        db_col = jnp.sum(g, axis=-1, keepdims=True) + u - z + dg_col
        g_row = jnp.sum(g, axis=1, keepdims=True)
        lane = lax.broadcasted_iota(jnp.int32, (STEP_ROWS, 128), 1)
        sub = lax.broadcasted_iota(jnp.int32, (8, STEP_ROWS), 0)
        dgc = jnp.zeros((STEP_ROWS, 128), F32)
        dgr = jnp.zeros((8, STEP_ROWS), F32)
        for hd in range(H):
            dgc = dgc + jnp.where(lane == hd, z[hd], 0.0) + jnp.where(lane == H + hd, db_col[hd], 0.0)
            dgr = dgr + jnp.where(sub == hd, g_row[hd], 0.0) - jnp.where(sub == H + hd, g_row[hd], 0.0)
            dqk_ref[:, D * hd:D * hd + D] = dq[hd]
            dqk_ref[:, H * D + D * hd:H * D + D * hd + D] = dks[hd]
            dv_ref[:, D * hd:D * hd + D] = dv[hd].astype(BF16)
        dgc_ref[...] = dgc
        dgr_ref[...] = dgr

    return pl.pallas_call(
        body, name="mlstm_bwd", grid=(steps,),
        in_specs=_mlstm_specs(T, rev) + [
            pl.BlockSpec((2, H, 128, 128), lambda s: (rev(s), 0, 0, 0)),
            pl.BlockSpec((2, H, 1, 128), lambda s: (rev(s), 0, 0, 0)),
            pl.BlockSpec((2, H, 1, 128), lambda s: (rev(s), 0, 0, 0)),
            pl.BlockSpec((STEP_ROWS, 512), lambda s: (rev(s), 0))],
        out_specs=[pl.BlockSpec((STEP_ROWS, 1024), lambda s: (rev(s), 0)),
                   pl.BlockSpec((STEP_ROWS, 512), lambda s: (rev(s), 0)),
                   pl.BlockSpec((STEP_ROWS, 128), lambda s: (rev(s), 0)),
                   pl.BlockSpec((8, STEP_ROWS), lambda s: (0, rev(s)))],
        out_shape=[_sds((T, 1024), F32), _sds((T, 512), BF16), _sds((T, 128), F32), _sds((8, T), F32)],
        scratch_shapes=[pltpu.VMEM((H, 128, 128), F32), pltpu.VMEM((H, 1, 128), F32)],
        compiler_params=_params(("arbitrary",)))(qk, qk, pm, gcol, bcol, grow, brow, cs, ns, ms, dh)


def _rows_to_lanes(x):
    eye = (lax.broadcasted_iota(jnp.int32, (8, 128), 0)
           == lax.broadcasted_iota(jnp.int32, (8, 128), 1)).astype(BF16)
    out, rest = None, x
    for _ in range(3):
        piece = rest.astype(BF16)
        rest = rest - piece.astype(F32)
        t = _dot(piece, eye, 0, 0)
        out = t if out is None else out + t
    return out


def _gate_bwd(dgc, dgr, gcol, bcol):
    T = dgc.shape[0]
    tr = min(ROW_TILE, T)

    def body(a_ref, b_ref, g_ref, bias_ref, o_ref, acc_ref):
        i = pl.program_id(0)

        @pl.when(i == 0)
        def _():
            acc_ref[...] = jnp.zeros_like(acc_ref)

        d = a_ref[...] + _rows_to_lanes(b_ref[:, pl.ds(pl.multiple_of(i * tr, 128), tr)])
        lane = lax.broadcasted_iota(jnp.int32, d.shape, 1)
        is_f = (lane >= MLSTM_HEADS) & (lane < 2 * MLSTM_HEADS)
        dlogf = _chunk_rev_cumsum(jnp.where(is_f, d, 0.0), 0)
        out = jnp.where(is_f, dlogf * _sigmoid(-(g_ref[...] + bias_ref[...])), d)
        o_ref[...] = out.astype(BF16)
        acc_ref[0:1, :] += _colsum(out)

    return _rows("gate_bwd", body, [dgc, dgr, gcol, bcol],
                 [_sds((T, 128), BF16), _sds((8, 128), F32)], T, tr=tr)


def _head_norm(h, mu_axis=-1):
    mu = jnp.mean(h, axis=-1, keepdims=True)
    hc = h - mu
    r = lax.rsqrt(jnp.mean(hc * hc, axis=-1, keepdims=True) + NORM_EPS)
    return hc * r, r


def _mlstm_out_bwd_rows(ps, es):
    hm, vo, w_all = es
    D, width = MLSTM_HEAD_DIM, MLSTM_HEADS * MLSTM_HEAD_DIM
    dhs, dos, dws = [], [], []
    for hd in range(MLSTM_HEADS):
        cols = slice(D * hd, D * hd + D)
        hn, r = _head_norm(hm[:, cols])
        sg = _sigmoid(vo[:, width + D * hd:width + D * hd + D].astype(F32))
        dy, w = ps[0][:, cols], w_all[:, cols]
        dos.append(dy * hn * w * sg * (1.0 - sg))
        dyn = dy * sg
        dws.append(_colsum(dyn * hn))
        dhn = dyn * w
        dhs.append(r * (dhn - jnp.mean(dhn, axis=-1, keepdims=True)
                        - hn * jnp.mean(dhn * hn, axis=-1, keepdims=True)))
    cat = lambda parts: jnp.concatenate(parts, axis=1)
    return [cat(dhs), cat(dos)], [_acc_rows([cat(dws)])]


ADAM_TILE_ELEMS = 256 * 1024


def _adamw(name, w, g, m, v):
    R, C = w.shape
    fits = [t for t in range(8, R + 1, 8) if R % t == 0 and t * C <= ADAM_TILE_ELEMS]
    if fits or R * C <= ADAM_TILE_ELEMS:
        tr = fits[-1] if fits else R
        spec, grid = pl.BlockSpec((tr, C), lambda i: (i, 0)), (R // tr,)
    else:
        spec, grid = pl.BlockSpec((R, 128), lambda i: (0, i)), (C // 128,)
    c1 = 1.0 - ADAM_B1 ** ADAM_STEP
    c2 = 1.0 - ADAM_B2 ** ADAM_STEP

    def body(w_ref, g_ref, m_ref, v_ref, d_ref, mo_ref, vo_ref):
        g = g_ref[...]
        m = ADAM_B1 * m_ref[...] + (1.0 - ADAM_B1) * g
        v = ADAM_B2 * v_ref[...] + (1.0 - ADAM_B2) * (g * g)
        mo_ref[...] = m
        vo_ref[...] = v
        d_ref[...] = -ADAM_LR * ((m / c1) / (jnp.sqrt(v / c2) + ADAM_EPS) + ADAM_WD * w_ref[...])

    return pl.pallas_call(
        body, name=name, grid=grid, in_specs=[spec] * 4, out_specs=[spec] * 3,
        out_shape=[_sds((R, C), F32)] * 3, compiler_params=_params(("parallel",)))(w, g, m, v)


def _place():
    return lax.axis_index("x"), lax.axis_index("y"), lax.axis_index("c")


def _all_gather8(name, blk, space):
    m, n = blk.shape

    def body(x_ref, out_ref, send_sems, recv_sems, local_sem):
        x, y, c = _place()
        me, sibling = (x, y, c), (x, y, 1 - c)
        chips = [(1 - x, y), (x, 1 - y), (1 - x, 1 - y)]

        def rows(px, py, pc):
            return out_ref.at[pl.ds((4 * px + 2 * py + pc) * m, m), :]

        def copy(k, block, to, src=None):
            return pltpu.make_async_remote_copy(
                src_ref=rows(*block) if src is None else src, dst_ref=rows(*block),
                send_sem=send_sems.at[k], recv_sem=recv_sems.at[k],
                device_id=to, device_id_type=MESH)

        mine = pltpu.make_async_copy(x_ref, rows(*me), local_sem)
        mine.start()
        first = [copy(0, me, sibling, src=x_ref)]
        first += [copy(1 + j, me, (*chip, c), src=x_ref) for j, chip in enumerate(chips)]
        for cp in first:
            cp.start()
        passed = [copy(4 + j, (*chip, c), sibling) for j, chip in enumerate(chips)]
        for j, chip in enumerate(chips):
            copy(1 + j, (*chip, c), me).wait_recv()
            passed[j].start()
        copy(0, sibling, me).wait_recv()
        for j, chip in enumerate(chips):
            copy(4 + j, (*chip, 1 - c), me).wait_recv()
        for cp in first + passed:
            cp.wait_send()
        mine.wait()

    return pl.pallas_call(
        body, name=name, out_shape=_sds((8 * m, n), blk.dtype),
        in_specs=[pl.BlockSpec(memory_space=space)], out_specs=pl.BlockSpec(memory_space=space),
        scratch_shapes=[pltpu.SemaphoreType.DMA((7,)), pltpu.SemaphoreType.DMA((7,)),
                        pltpu.SemaphoreType.DMA],
        compiler_params=pltpu.CompilerParams(vmem_limit_bytes=VMEM_LIMIT))(blk)


def _hbm_specs(n):
    return [pl.BlockSpec(memory_space=pl.ANY)] * n


def _swap_halves_sibling(name, srcs):
    nw = len(srcs)

    def body(*refs):
        src_refs, dst_refs, send_sems, recv_sems = refs[:nw], refs[nw:2 * nw], refs[2 * nw], refs[2 * nw + 1]
        x, y, c = _place()
        cps = [pltpu.make_async_remote_copy(
            src_ref=src_refs[w].at[pl.ds(0, 4), 1 - c], dst_ref=dst_refs[w],
            send_sem=send_sems.at[w], recv_sem=recv_sems.at[w], device_id=(x, y, 1 - c),
            device_id_type=MESH) for w in range(nw)]
        for cp in cps:
            cp.start()
        for cp in cps:
            cp.wait()

    return pl.pallas_call(
        body, name=name, out_shape=[_sds(s.shape[:1] + s.shape[2:], s.dtype) for s in srcs],
        in_specs=_hbm_specs(nw), out_specs=_hbm_specs(nw),
        scratch_shapes=[pltpu.SemaphoreType.DMA((nw,)), pltpu.SemaphoreType.DMA((nw,))])(*srcs)


def _split_start(name, srcs, lands, copies, per_array, after):
    nw = len(srcs)

    def body(*refs):
        send_sems, recv_sems, token = refs[2 * nw + 1], refs[2 * nw + 2], refs[-1]
        for w in range(nw):
            for k, (s, d, dev) in enumerate(copies(refs[w], refs[nw + w], *_place())):
                pltpu.make_async_remote_copy(
                    src_ref=s, dst_ref=d, send_sem=send_sems.at[w * per_array + k],
                    recv_sem=recv_sems.at[w * per_array + k], device_id=dev, device_id_type=MESH).start()
        token[...] = jnp.zeros_like(token)

    hbm, sem = pl.BlockSpec(memory_space=pltpu.HBM), pl.BlockSpec(memory_space=pltpu.SEMAPHORE)
    arrays = list(srcs) + list(lands)
    out = pl.pallas_call(
        body, name=name,
        out_shape=(pltpu.SemaphoreType.DMA((nw * per_array,)), pltpu.SemaphoreType.DMA((nw * per_array,)),
                   *[pltpu.HBM(a.shape, a.dtype) for a in arrays], _sds((8, 128), F32)),
        in_specs=[hbm] * (2 * nw) + [pl.BlockSpec(memory_space=pl.ANY)],
        out_specs=(sem, sem, *[hbm] * (2 * nw), pl.BlockSpec(memory_space=pltpu.VMEM)),
        input_output_aliases={i: 2 + i for i in range(2 * nw)},
        compiler_params=pltpu.CompilerParams(has_side_effects=pltpu.SideEffectType.DATAFLOW_SIDE_EFFECTING))(
            *[pltpu.with_memory_space_constraint(a, pltpu.HBM) for a in arrays], after)
    return out[0], out[1], out[2:2 + nw], out[2 + nw:2 + 2 * nw], out[-1]


def _split_wait(name, started, after, waits, per_array):
    send_sems, recv_sems, srcs, lands, _ = started
    nw = len(srcs)

    def body(*refs):
        send_sems, recv_sems = refs[2 * nw], refs[2 * nw + 1]
        x, y, c = _place()
        for w in range(nw):
            for k, (s, d) in enumerate(waits(refs[w], refs[nw + w], x, y, c)):
                cp = pltpu.make_async_remote_copy(
                    src_ref=s, dst_ref=d, send_sem=send_sems.at[w * per_array + k],
                    recv_sem=recv_sems.at[w * per_array + k], device_id=(x, y, 1 - c),
                    device_id_type=MESH)
                cp.wait_send()
                cp.wait_recv()

    hbm, sem = pl.BlockSpec(memory_space=pltpu.HBM), pl.BlockSpec(memory_space=pltpu.SEMAPHORE)
    arrays = list(srcs) + list(lands)
    out = pl.pallas_call(
        body, name=name, out_shape=tuple(pltpu.HBM(a.shape, a.dtype) for a in arrays),
        in_specs=[hbm] * (2 * nw) + [sem, sem, pl.BlockSpec(memory_space=pl.ANY)],
        out_specs=tuple([hbm] * (2 * nw)), input_output_aliases={i: i for i in range(2 * nw)},
        compiler_params=pltpu.CompilerParams(has_side_effects=pltpu.SideEffectType.DATAFLOW_SIDE_EFFECTING))(
            *arrays, send_sems, recv_sems, after)
    return list(out[nw:])


def _other_chips(x, y):
    return [(1 - x, y), (x, 1 - y), (1 - x, 1 - y)]


def _gather_sends(src_ref, land_ref, x, y, c):
    to = land_ref.at[2 * x + y, c]
    return [(src_ref, to, (x, y, 1 - c))] + [(src_ref, to, (px, py, c)) for px, py in _other_chips(x, y)]


def _gather_lands(src_ref, land_ref, x, y, c):
    return [(src_ref, land_ref.at[2 * x + y, 1 - c])] + [
        (src_ref, land_ref.at[2 * px + py, c]) for px, py in _other_chips(x, y)]


def _gather_sends_all(src_ref, land_ref, x, y, c):
    to = land_ref.at[2 * x + y, c]
    return [(src_ref, to, (x, y, 1 - c))] + [
        (src_ref, to, (px, py, pc)) for px, py in _other_chips(x, y) for pc in (c, 1 - c)]


def _gather_lands_all(src_ref, land_ref, x, y, c):
    return [(src_ref, land_ref.at[2 * x + y, 1 - c])] + [
        (src_ref, land_ref.at[2 * px + py, pc]) for px, py in _other_chips(x, y) for pc in (c, 1 - c)]


def _scatter_sends(src_ref, land_ref, x, y, c):
    return [(src_ref.at[2 * px + py], land_ref.at[2 * x + y], (px, py, c)) for px, py in _other_chips(x, y)]


def _scatter_lands(src_ref, land_ref, x, y, c):
    return [(src_ref.at[2 * x + y], land_ref.at[2 * px + py]) for px, py in _other_chips(x, y)]


def _forward_sibling(name, lands):
    nw = len(lands)

    def body(*refs):
        land_refs, out_refs, send_sems, recv_sems = refs[:nw], refs[nw:2 * nw], refs[2 * nw], refs[2 * nw + 1]
        x, y, c = _place()
        cps = []
        for w in range(nw):
            cps += [pltpu.make_async_remote_copy(
                src_ref=land_refs[w].at[2 * px + py, c], dst_ref=out_refs[w].at[2 * px + py, c],
                send_sem=send_sems.at[w, j], recv_sem=recv_sems.at[w, j], device_id=(x, y, 1 - c),
                device_id_type=MESH) for j, (px, py) in enumerate(_other_chips(x, y))]
        for cp in cps:
            cp.start()
        for w in range(nw):
            for j, (px, py) in enumerate(_other_chips(x, y)):
                slot = out_refs[w].at[2 * px + py, 1 - c]
                pltpu.make_async_remote_copy(src_ref=slot, dst_ref=slot, send_sem=send_sems.at[w, j],
                                             recv_sem=recv_sems.at[w, j], device_id=(x, y, 1 - c),
                                             device_id_type=MESH).wait_recv()
        for cp in cps:
            cp.wait_send()

    return pl.pallas_call(
        body, name=name, out_shape=[_sds(a.shape, a.dtype) for a in lands],
        in_specs=_hbm_specs(nw), out_specs=_hbm_specs(nw), input_output_aliases={i: i for i in range(nw)},
        scratch_shapes=[pltpu.SemaphoreType.DMA((nw, 3)), pltpu.SemaphoreType.DMA((nw, 3))])(*lands)


def _share_halves(name, halves):
    nw = len(halves)

    def body(*refs):
        in_refs, out_refs, send_sems, recv_sems = refs[:nw], refs[nw:2 * nw], refs[2 * nw], refs[2 * nw + 1]
        x, y, c = _place()
        cps = [pltpu.make_async_remote_copy(
            src_ref=in_refs[w].at[c], dst_ref=out_refs[w].at[c], send_sem=send_sems.at[w],
            recv_sem=recv_sems.at[w], device_id=(x, y, 1 - c), device_id_type=MESH) for w in range(nw)]
        for cp in cps:
            cp.start()
        for w in range(nw):
            slot = out_refs[w].at[1 - c]
            pltpu.make_async_remote_copy(src_ref=slot, dst_ref=slot, send_sem=send_sems.at[w],
                                         recv_sem=recv_sems.at[w], device_id=(x, y, 1 - c),
                                         device_id_type=MESH).wait_recv()
        for cp in cps:
            cp.wait_send()

    return pl.pallas_call(
        body, name=name, out_shape=[_sds(a.shape, a.dtype) for a in halves],
        in_specs=_hbm_specs(nw), out_specs=_hbm_specs(nw), input_output_aliases={i: i for i in range(nw)},
        scratch_shapes=[pltpu.SemaphoreType.DMA((nw,)), pltpu.SemaphoreType.DMA((nw,))])(*halves)


def _place_blocks(name, blks, place):
    nw = len(blks)

    def body(p_ref, *refs):
        for b_ref, o_ref in zip(refs[:nw], refs[nw:]):
            o_ref[...] = b_ref[...]

    return pl.pallas_call(
        body, name=name,
        grid_spec=pltpu.PrefetchScalarGridSpec(
            num_scalar_prefetch=1, grid=(1,),
            in_specs=[pl.BlockSpec(b.shape, lambda i, p: (0, 0)) for b in blks],
            out_specs=[pl.BlockSpec((None, None) + b.shape, lambda i, p: (p[0], p[1], 0, 0)) for b in blks]),
        out_shape=[_sds((4, 2) + b.shape, b.dtype) for b in blks],
        compiler_params=_params(("arbitrary",)))(place, *blks)


def _pair_sum(name, fulls, gots, place):
    nw = len(fulls)

    def body(p_ref, *refs):
        s = pl.program_id(0)
        for a_ref, b_ref, o_ref, l_ref in zip(refs[:nw], refs[nw:2 * nw], refs[2 * nw:3 * nw], refs[3 * nw:]):
            o_ref[...] = (a_ref[...].astype(F32) + b_ref[...].astype(F32)).astype(o_ref.dtype)

            @pl.when(s == p_ref[0])
            def _():
                l_ref[...] = o_ref[...]

    slab = lambda a: pl.BlockSpec((None,) + a.shape[1:], lambda s, p: (s, 0, 0))
    mine = lambda a: pl.BlockSpec((None,) + a.shape[1:], lambda s, p: (p[0], 0, 0))
    out = pl.pallas_call(
        body, name=name,
        grid_spec=pltpu.PrefetchScalarGridSpec(
            num_scalar_prefetch=1, grid=(4,),
            in_specs=[pl.BlockSpec((None, None) + a.shape[2:], lambda s, p: (s, p[1], 0, 0)) for a in fulls]
            + [slab(b) for b in gots],
            out_specs=[slab(b) for b in gots] + [mine(b) for b in gots]),
        out_shape=[_sds(b.shape, BF16) for b in gots] * 2,
        compiler_params=_params(("arbitrary",)))(place, *fulls, *gots)
    return out[:nw], out[nw:]


def _sum4(name, arrs, place):
    nw = len(arrs)

    def body(p_ref, *refs):
        for a_ref, o_ref in zip(refs[:nw], refs[nw:]):
            acc = a_ref[0].astype(F32)
            for s in range(1, 4):
                acc = acc + a_ref[s].astype(F32)
            o_ref[...] = acc

    return pl.pallas_call(
        body, name=name,
        grid_spec=pltpu.PrefetchScalarGridSpec(
            num_scalar_prefetch=1, grid=(1,),
            in_specs=[pl.BlockSpec(a.shape, lambda i, p: (0, 0, 0)) for a in arrs],
            out_specs=[pl.BlockSpec((None,) + a.shape[1:], lambda i, p: (p[1], 0, 0)) for a in arrs]),
        out_shape=[_sds((2,) + a.shape[1:], F32) for a in arrs],
        compiler_params=_params(("arbitrary",)))(place, *arrs)


def _small_update(gathered, params, slots):
    c1 = 1.0 - ADAM_B1 ** ADAM_STEP
    c2 = 1.0 - ADAM_B2 ** ADAM_STEP
    n = len(params)

    def body(g_ref, *refs):
        ins, sum_ref, outs = refs[:3 * n], refs[3 * n], refs[3 * n + 1:]
        g_all = g_ref[0:1, :]
        for d in range(1, 8):
            g_all = g_all + g_ref[d:d + 1, :]
        sum_ref[...] = g_all
        for k, (off, width) in enumerate(slots):
            w_ref, m_ref, v_ref = ins[3 * k:3 * k + 3]
            go_ref, d_ref, mo_ref, vo_ref = outs[4 * k:4 * k + 4]
            g = g_all[:, off:off + width]
            m = ADAM_B1 * m_ref[...] + (1.0 - ADAM_B1) * g
            v = ADAM_B2 * v_ref[...] + (1.0 - ADAM_B2) * (g * g)
            go_ref[...], mo_ref[...], vo_ref[...] = g, m, v
            d_ref[...] = -ADAM_LR * ((m / c1) / (jnp.sqrt(v / c2) + ADAM_EPS) + ADAM_WD * w_ref[...])

    flat = [a for p in params for a in p]
    out = pl.pallas_call(
        body, name="small_update",
        out_shape=[_sds((1, gathered.shape[1]), F32)] + [_sds(p[0].shape, F32) for p in params for _ in range(4)],
        compiler_params=pltpu.CompilerParams(vmem_limit_bytes=VMEM_LIMIT))(gathered, *flat)
    return out[0], [tuple(out[1 + 4 * k:5 + 4 * k]) for k in range(n)]


def _swiglu(ps, es):
    g, u = ps
    return g * _sigmoid(g) * u, g, u


def _swiglu_bwd(ps, es):
    g, u = es[0].astype(F32), es[1].astype(F32)
    sg = _sigmoid(g)
    return ps[0] * u * (sg * (1.0 + g * (1.0 - sg))), ps[0] * (g * sg)


def _merge(ps, es):
    ga, gm = [e.astype(F32) for e in es]
    return (_sigmoid(ga) * ps[0] + _sigmoid(gm) * ps[1],)


def _merge_bwd(ps, es):
    dm, a, b = ps
    ga, gm = [e.astype(F32) for e in es]
    sa, sm = _sigmoid(ga), _sigmoid(gm)
    return dm * sa, dm * sm, dm * a * (sa * (1.0 - sa)), dm * b * (sm * (1.0 - sm))


W_IN_PIECES = (("q", 512), ("kv", 256), ("mqk", 1024), ("mv", 512), ("mo", 512), ("if", 8),
               ("ga", 1024), ("gm", 1024))


def _local_step(x, tgt, pos_col, mod, sp, in_weights, late_weights, ffn_grads, mixer_grads):
    sh_m, sc_m, gate_m, sh_f, sc_f, gate_f = mod
    inv = ROPE_THETA ** (-2.0 * jnp.arange(HEAD_DIM // 2, dtype=F32) / HEAD_DIM)
    cos, sin = _rope_tables(pos_col, jnp.tile(inv, 4).reshape(1, 128))
    W = dict(in_weights(cos))
    h, pa, pqk, pvo, pif, pg = _proj_in(x, sp["g_pre_mix"], sc_m, sh_m, [
        (W["q+kv"], F32, 256), (W["mqk"], F32, 512), (W["mv+mo"], BF16, 512), (W["if"], F32, 128),
        (W["ga+gm"], BF16, 512)])
    ya = _attn_fwd(pa, cos, sin, sp["sinks"])
    qk = _conv_fwd(pqk, sp["conv_w"], sp["conv_b"])
    bcol = jnp.pad(sp["b_if"], ((0, 0), (0, 120)))
    brow = jnp.broadcast_to(sp["b_if"].reshape(8, 1), (8, 128))
    grow = pif[:, :8].T
    hm, ym, cs, ns, ms = _mlstm_fwd(qk, pvo, pif, bcol, grow, brow, sp["norm_w"])
    W.update(late_weights(ym))
    w_fg, w_fu, w_fd = W["fg"], W["fu"], W["fd"]
    merged, = _mm("branches", [[(ya, W["ba"])], [(ym, W["bm"])]], [(pg, 0), (pg, 1)], _merge, [BF16],
                  cn=512, nt=True)
    wide, narrow = (D_MODEL, F32), (D_MODEL, BF16)
    mix, x1, h2 = _mm_rows("mix_out", [[(merged, W["out"])]],
                           [x, gate_m, sp["g_post_mix"], sp["g_pre_ffn"], sc_f, sh_f],
                           _res_norm_rows, [wide, wide, narrow], [], cn=512)
    act, gt, up = _mm("ffn_in", [[(h2, w_fg)], [(h2, w_fu)]], [], _swiglu, [BF16] * 3,
                      cn=256, nt=True)
    dy, dff, acc_l, loss = _mm_rows("ffn_down", [[(act, w_fd)]], [x1, tgt, gate_f, sp["g_post_ffn"]],
                                    _final_loss_rows, [wide, narrow], [(8, D_MODEL), (1, 128)], cn=512)

    G = {}
    dgt, dup = _mm("ffn_down_bwd", [[(dff, w_fd)]], [gt, up], _swiglu_bwd, [BF16, BF16],
                   cn=256, nt=True)
    g_fd, = _mm_tn_group("dw_ffn_down", [act], dff, BF16)
    g_fg, = _mm_tn_group("dw_ffn_gate", [dgt], h2, BF16)
    g_fu, = _mm_tn_group("dw_ffn_up", [dup], h2, BF16)
    tie = ffn_grads(g_fg, g_fu, g_fd)
    dx1, dmix, acc_r = _mm_rows(
        "ffn_in_bwd", [[(dgt, w_fg), (dup, w_fu)]],
        [x1, mix, dy, sc_f + tie, gate_m, sp["g_pre_ffn"], sp["g_post_mix"]],
        _res_norm_bwd_rows, [wide, narrow], [(8, D_MODEL)], cn=512, tm=256)
    d_a, d_m, dga, dgm = _mm("mix_out_bwd", [[(dmix, W["out"])], [(ya, W["ba"])], [(ym, W["bm"])]],
                             [(pg, 0), (pg, 1)], _merge_bwd, [BF16] * 4, cn=512, nt=True)
    G["out"], = _mm_tn_group("dw_out", [merged], dmix, BF16)
    dya, = _mm("branch_attn_bwd", [[(d_a, W["ba"])]], [], _first, [F32], cn=512)
    heads = MLSTM_HEADS * MLSTM_HEAD_DIM
    dhm, do_m, acc_n = _mm_rows("branch_mlstm_bwd", [[(d_m, W["bm"])]], [hm, pvo, sp["norm_w"]],
                                _mlstm_out_bwd_rows, [(heads, F32), (heads, BF16)], [(8, heads)], cn=512)
    G["ba"], = _mm_tn_group("dw_branch_attn", [d_a], ya, BF16)
    G["bm"], = _mm_tn_group("dw_branch_mlstm", [d_m], ym, BF16)
    dqk, dv_m, dgc, dgr = _mlstm_bwd(qk, pvo, pif, bcol, grow, brow, cs, ns, ms, dhm)
    dif, acc_g = _gate_bwd(dgc, dgr, pif, bcol)
    du, acc_c = _conv_bwd(pqk, sp["conv_w"], sp["conv_b"], dqk)
    dq_a, dkv, dsink = _attn_bwd(pa, cos, sin, sp["sinks"], dya)
    dproj = {"q": dq_a, "kv": dkv, "mqk": du, "mv": dv_m, "mo": do_m, "if": dif, "ga": dga, "gm": dgm}
    names = [k for k, _ in W_IN_PIECES]
    for part in (names[:4], names[4:]):
        G.update(zip(part, _mm_tn_group("dw_in_from_" + part[0], [dproj[k] for k in part], h, BF16)))
    w_tied = dict(W, **{"if": W["if"] + mixer_grads(G).astype(BF16)})
    dx, acc_p = _mm_rows("proj_bwd", [[(dproj[k], w_tied[k]) for k, _ in W_IN_PIECES]],
                         [x, dx1, sp["g_pre_mix"], sc_m], _pre_norm_bwd_rows, [wide], [(8, D_MODEL)], cn=512)

    small = {
        "mod": jnp.concatenate([acc_p[1], acc_p[0], acc_r[3], acc_r[1], acc_r[0], acc_l[0]]),
        "g_pre_mix": acc_p[2], "g_post_mix": acc_r[4], "b_if": acc_g[0, :8],
        "conv_w": acc_c[:CONV_WIDTH].reshape(-1), "conv_b": acc_c[CONV_WIDTH],
        "sinks": dsink[:, 0], "norm_w": acc_n[0], "g_pre_ffn": acc_r[2], "g_post_ffn": acc_l[1]}
    return loss, dx, small


IN_WIDTH = sum(n for _, n in W_IN_PIECES)
IN_SHARD = IN_WIDTH // 4
IN_SHARD_PAD = -(-IN_SHARD // 32) * 32


def _split_w_in(w_in_t):
    out, off, start = {}, 0, {}
    for k, n in W_IN_PIECES:
        out[k], start[k] = w_in_t[off:off + n], off
        off += n
    out["if"] = jnp.pad(out["if"], ((0, 120), (0, 0)))
    for name, first, last in (("q+kv", "q", "kv"), ("mv+mo", "mv", "mo"), ("ga+gm", "ga", "gm")):
        out[name] = w_in_t[start[first]:start[last] + out[last].shape[0]]
    return out


def _halves(a):
    return a.reshape(4, 2, a.shape[0] // 8, a.shape[1])


SMALL = (("b_ada", 6144), ("g_pre_mix", 1024), ("g_post_mix", 1024), ("b_if", 128), ("conv_w", 4096),
         ("conv_b", 1024), ("sinks", 128), ("norm_w", 512), ("g_pre_ffn", 1024), ("g_post_ffn", 1024))
SMALL_LEN = 8 * 2048


def _pack_small(vals):
    parts = []
    for k, n in SMALL:
        v = vals[k].reshape(-1)
        parts.append(jnp.pad(v, (0, n - v.shape[0])))
    flat = jnp.concatenate(parts)
    return jnp.pad(flat, (0, SMALL_LEN - flat.shape[0]))


def kernel(x, c, positions, w_ada, b_ada, g_pre_mix, g_post_mix, w_in, b_if, conv_w, conv_b, attn_sinks, mlstm_norm_w, w_branch_attn, w_branch_mlstm, w_out, g_pre_ffn, g_post_ffn, w_ffn_gate, w_ffn_up, w_ffn_down, loss_target, m_w_ada, m_b_ada, m_g_pre_mix, m_g_post_mix, m_w_in, m_b_if, m_conv_w, m_conv_b, m_attn_sinks, m_mlstm_norm_w, m_w_branch_attn, m_w_branch_mlstm, m_w_out, m_g_pre_ffn, m_g_post_ffn, m_w_ffn_gate, m_w_ffn_up, m_w_ffn_down, v_w_ada, v_b_ada, v_g_pre_mix, v_g_post_mix, v_w_in, v_b_if, v_conv_w, v_conv_b, v_attn_sinks, v_mlstm_norm_w, v_w_branch_attn, v_w_branch_mlstm, v_w_out, v_g_pre_ffn, v_g_post_ffn, v_w_ffn_gate, v_w_ffn_up, v_w_ffn_down):
    xi, yi, ci = _place()
    chip = 2 * xi + yi
    dev = 2 * chip + ci
    T = x.shape[1]
    ada_cols = w_ada.shape[2]

    place = jnp.stack([chip, ci]).astype(jnp.int32)

    def my_half(a):
        n = a.shape[0] // 2
        return lax.dynamic_slice_in_dim(a, ci * n, n, axis=0).astype(BF16)

    blk = jnp.concatenate([c.reshape(-1), conv_w.reshape(-1)]).reshape(8, 256)
    got = _all_gather8("gather_cond", blk, pltpu.VMEM).reshape(8, 2048)
    c_all = got[:, :D_MODEL].astype(BF16)
    conv_full = got[::2, D_MODEL:].reshape(4, CONV_WIDTH, -1).transpose(1, 0, 2).reshape(CONV_WIDTH, -1)

    b_sh = lax.dynamic_slice_in_dim(b_ada, chip * ada_cols, ada_cols, axis=1)
    mod_part, = _mm("ada_mod", [[(c_all, w_ada[0].astype(BF16))]], [b_sh],
                    lambda ps, es: (ps[0] + es[0],), [F32], cn=512, tm=8)
    mod_all = _all_gather8("gather_mod", mod_part, pltpu.VMEM).reshape(4, 2, 8, ada_cols)[:, 0]
    mod = lax.dynamic_index_in_dim(mod_all, dev, axis=1, keepdims=False).reshape(6, 1, D_MODEL)

    def gather_start(name, blks, after, sends, copies):
        return _split_start(name + "_start", blks, _place_blocks(name + "_place", blks, place),
                            sends, copies, after)

    w_in_t = jnp.pad(w_in[0].T, ((0, IN_SHARD_PAD - IN_SHARD), (0, 0)))
    in_started = gather_start("in_gather", [my_half(w_in_t)], mod, _gather_sends, 4)
    late_keys = ("fg", "fu", "fd", "out", "ba", "bm")
    late_started = gather_start(
        "late_gather",
        [my_half(w_ffn_gate[0].T), my_half(w_ffn_up[0].T), my_half(w_ffn_down[0]), my_half(w_out[0]),
         my_half(w_branch_attn[0].T), my_half(w_branch_mlstm[0].T)], in_started[4], _gather_sends_all, 7)
    mod = mod + (in_started[4][0, 0] + late_started[4][0, 0])

    def in_weights(after):
        g_in, = _forward_sibling("in_gather_forward",
                                 _split_wait("in_gather_wait", in_started, after, _gather_lands, 4))
        return _split_w_in(g_in.reshape(4, IN_SHARD_PAD, D_MODEL)[:, :IN_SHARD].reshape(IN_WIDTH, D_MODEL))

    def late_weights(after):
        lands = _split_wait("late_gather_wait", late_started, after, _gather_lands_all, 7)
        return {k: a.reshape(-1, a.shape[-1]) for k, a in zip(late_keys, lands)}

    sent = {}

    def scatter_start(name, groups):
        pairs, lands = _pair_sum(name + "_pair_sum", groups, _swap_halves_sibling(name + "_pair", groups), place)
        sent[name] = _split_start(name + "_start", pairs, lands, _scatter_sends, 3, pairs[0])
        return sent[name][4][0, 0]

    def ffn_grads(g_fg, g_fu, g_fd):
        return scatter_start("rs_ffn", [_halves(g_fg), _halves(g_fu), _halves(g_fd)])

    def mixer_grads(G):
        g_in_t = jnp.concatenate([G[k][:n] for k, n in W_IN_PIECES]).reshape(4, IN_SHARD, D_MODEL)
        g_in_t = jnp.pad(g_in_t, ((0, 0), (0, IN_SHARD_PAD - IN_SHARD), (0, 0)))
        return scatter_start("rs_mix", [g_in_t.reshape(4, 2, IN_SHARD_PAD // 2, D_MODEL), _halves(G["out"]),
                                        _halves(G["ba"]), _halves(G["bm"])])

    sp = {"g_pre_mix": g_pre_mix, "g_post_mix": g_post_mix, "b_if": b_if, "conv_w": conv_full,
          "conv_b": conv_b, "sinks": attn_sinks, "norm_w": mlstm_norm_w, "g_pre_ffn": g_pre_ffn,
          "g_post_ffn": g_post_ffn}
    loss, dx, small = _local_step(x[0], loss_target[0], positions.reshape(T, 1), [mod[i] for i in range(6)],
                                  sp, in_weights, late_weights, ffn_grads, mixer_grads)

    reds = (_sum4("rs_ffn_chip_sum", _split_wait("rs_ffn_wait", sent["rs_ffn"], dx, _scatter_lands, 3), place)
            + _sum4("rs_mix_chip_sum", _split_wait("rs_mix_wait", sent["rs_mix"], dx, _scatter_lands, 3), place))
    gsh = {k: s.reshape(-1, s.shape[-1])
           for k, s in zip(("fg", "fu", "fd", "w_in", "out", "ba", "bm"), _share_halves("rs_share", reds))}
    gsh["w_in"] = gsh["w_in"][:IN_SHARD]

    small["b_ada"] = small.pop("mod")
    vec = _pack_small(small).reshape(8, 2048)
    g_all = _all_gather8("gather_small", vec, pltpu.VMEM).reshape(8, SMALL_LEN)
    dmod_sh = lax.dynamic_slice_in_dim(g_all[:, :6 * D_MODEL], chip * ada_cols, ada_cols, axis=1)
    g_w_ada, = _mm_tn_group("dw_ada", [c_all], dmod_sh.astype(BF16), F32)

    smalls = {"b_ada": (b_ada, m_b_ada, v_b_ada), "g_pre_mix": (g_pre_mix, m_g_pre_mix, v_g_pre_mix),
              "g_post_mix": (g_post_mix, m_g_post_mix, v_g_post_mix), "b_if": (b_if, m_b_if, v_b_if),
              "conv_b": (conv_b, m_conv_b, v_conv_b), "sinks": (attn_sinks, m_attn_sinks, v_attn_sinks),
              "norm_w": (mlstm_norm_w, m_mlstm_norm_w, v_mlstm_norm_w),
              "g_pre_ffn": (g_pre_ffn, m_g_pre_ffn, v_g_pre_ffn),
              "g_post_ffn": (g_post_ffn, m_g_post_ffn, v_g_post_ffn)}
    offsets, off = {}, 0
    for k, width in SMALL:
        offsets[k], off = off, off + width
    g_sum, updates = _small_update(g_all, list(smalls.values()),
                                   [(offsets[k], t[0].shape[1]) for k, t in smalls.items()])
    g_conv = g_sum[:, offsets["conv_w"]:offsets["conv_w"] + CONV_WIDTH * D_MODEL].reshape(1, CONV_WIDTH, D_MODEL)
    g_conv = lax.dynamic_slice_in_dim(g_conv, chip * conv_w.shape[2], conv_w.shape[2], axis=2)

    res = dict(zip(smalls, updates))
    res["conv_w"] = (g_conv, *[o[None] for o in _adamw("adam_conv_w", conv_w[0], g_conv[0], m_conv_w[0], v_conv_w[0])])
    res["w_ada"] = (g_w_ada[None], *[o[None] for o in _adamw("adam_w_ada", w_ada[0], g_w_ada, m_w_ada[0], v_w_ada[0])])
    bigs = {"w_in": (w_in, m_w_in, v_w_in), "ba": (w_branch_attn, m_w_branch_attn, v_w_branch_attn),
            "bm": (w_branch_mlstm, m_w_branch_mlstm, v_w_branch_mlstm), "out": (w_out, m_w_out, v_w_out),
            "fg": (w_ffn_gate, m_w_ffn_gate, v_w_ffn_gate), "fu": (w_ffn_up, m_w_ffn_up, v_w_ffn_up),
            "fd": (w_ffn_down, m_w_ffn_down, v_w_ffn_down)}
    for k, (w, m, v) in bigs.items():
        if k in ("w_in", "fg", "fu"):
            res[k] = tuple(o.T[None] for o in (gsh[k], *_adamw("adam_" + k, w[0].T, gsh[k], m[0].T, v[0].T)))
        else:
            g = gsh[k].T if k in ("ba", "bm") else gsh[k]
            res[k] = (g[None], *[o[None] for o in _adamw("adam_" + k, w[0], g, m[0], v[0])])

    order = ("w_ada", "b_ada", "g_pre_mix", "g_post_mix", "w_in", "b_if", "conv_w", "conv_b", "sinks",
             "norm_w", "ba", "bm", "out", "g_pre_ffn", "g_post_ffn", "fg", "fu", "fd")
    total = lax.psum(loss[0, 0], ("x", "y", "c"))
    return (total, dx[None], *[res[k][0] for k in order], *[res[k][1] for k in order],
            *[res[k][2] for k in order], *[res[k][3] for k in order])
```

```python
import functools

import jax
import jax.numpy as jnp
from jax import lax
from jax.experimental import pallas as pl
from jax.experimental.pallas import tpu as pltpu

F32, BF16 = jnp.float32, jnp.bfloat16
MESH = pl.DeviceIdType.MESH

D_MODEL = 1024
N_Q_HEADS, N_KV_HEADS, HEAD_DIM, WINDOW = 8, 2, 64, 128
ROPE_THETA = 10000.0
MLSTM_HEADS, MLSTM_HEAD_DIM, MLSTM_CHUNK, CONV_WIDTH = 4, 128, 64, 4
D_FF = 2816
NORM_EPS = 1e-6
ADAM_LR, ADAM_B1, ADAM_B2, ADAM_EPS, ADAM_WD, ADAM_STEP = 0.001, 0.9, 0.999, 1e-08, 0.01, 10

VMEM_LIMIT = 56 * 1024 * 1024
ROW_TILE = 256
MM_TM = 512
MM_TT = 1024
ATTN_BLK = WINDOW
STEP_ROWS = 2 * MLSTM_CHUNK
NEG_INF = float("-inf")


def _params(sem):
    return pltpu.CompilerParams(dimension_semantics=sem, vmem_limit_bytes=VMEM_LIMIT)


def _sds(shape, dtype):
    return jax.ShapeDtypeStruct(shape, dtype)


def _sigmoid(x):
    return 1.0 / (1.0 + jnp.exp(-x))


def _dot(a, b, ca, cb):
    return lax.dot_general(a, b, (((ca,), (cb,)), ((), ())), preferred_element_type=F32)


def _bdot(a, b, ca, cb):
    return lax.dot_general(a, b, (((ca,), (cb,)), ((0,), (0,))), preferred_element_type=F32)


def _bdot_rows(a, b):
    return jnp.stack([_dot(a[h], b[h], 0, 0) for h in range(a.shape[0])])


def _mm(name, prods, extras, epi, out_dtypes, cn, nt=False, tm=MM_TM):
    flat = [ab for p in prods for ab in p]
    counts = [len(p) for p in prods]
    M = flat[0][0].shape[0]
    N = flat[0][1].shape[0 if nt else 1]
    tm = min(tm, M)
    n_in = 2 * len(flat) + len(extras)

    def body(*refs):
        ins, outs = refs[:n_in], refs[n_in:]
        for j in range(N // cn):
            cols = slice(j * cn, (j + 1) * cn)
            k, ps = 0, []
            for cnt in counts:
                acc = None
                for _ in range(cnt):
                    b = ins[k + 1][cols, :] if nt else ins[k + 1][:, cols]
                    d = _dot(ins[k][...], b, 1, 1 if nt else 0)
                    acc = d if acc is None else acc + d
                    k += 2
                ps.append(acc)
            res = epi(ps, [r[:, cols] for r in ins[k:]])
            for o, r in zip(outs, res):
                o[:, cols] = r.astype(o.dtype)

    in_specs, args = [], []
    for a, b in flat:
        in_specs.append(pl.BlockSpec((tm, a.shape[1]), lambda i: (i, 0)))
        in_specs.append(pl.BlockSpec(b.shape, lambda i: (0, 0), pipeline_mode=pl.Buffered(1)))
        args += [a, b]
    for e in extras:
        e, off = e if isinstance(e, tuple) else (e, 0)
        rows = 1 if e.shape[0] == 1 else tm
        in_specs.append(pl.BlockSpec((rows, N), lambda i, off=off, rows=rows: (0 if rows == 1 else i, off)))
        args.append(e)
    return pl.pallas_call(
        body, name=name, grid=(M // tm,), in_specs=in_specs,
        out_specs=[pl.BlockSpec((tm, N), lambda i: (i, 0)) for _ in out_dtypes],
        out_shape=[_sds((M, N), dt) for dt in out_dtypes],
        compiler_params=_params(("parallel",)))(*args)


def _mm_rows(name, prods, extras, epi, outs, accs, cn, nt=False, tm=MM_TM):
    flat = [ab for p in prods for ab in p]
    counts = [len(p) for p in prods]
    M = flat[0][0].shape[0]
    N = flat[0][1].shape[0 if nt else 1]
    tm = min(tm, M)
    n_mm, n_in, n_out = 2 * len(flat), 2 * len(flat) + len(extras), len(outs)

    def body(*refs):
        ins, out_refs, acc_refs = refs[:n_in], refs[n_in:n_in + n_out], refs[n_in + n_out:]

        @pl.when(pl.program_id(0) == 0)
        def _():
            for a in acc_refs:
                a[...] = jnp.zeros_like(a)

        chunks = [[] for _ in counts]
        for j in range(N // cn):
            cols = slice(j * cn, (j + 1) * cn)
            k = 0
            for p, cnt in enumerate(counts):
                acc = None
                for _ in range(cnt):
                    b = ins[k + 1][cols, :] if nt else ins[k + 1][:, cols]
                    d = _dot(ins[k][...], b, 1, 1 if nt else 0)
                    acc = d if acc is None else acc + d
                    k += 2
                chunks[p].append(acc)
        ps = [c[0] if len(c) == 1 else jnp.concatenate(c, axis=1) for c in chunks]
        res, incs = epi(ps, [r[...] for r in ins[n_mm:]])
        for o, r in zip(out_refs, res):
            o[...] = r.astype(o.dtype)
        for a, inc in zip(acc_refs, incs):
            a[...] += inc

    in_specs, args = [], []
    for a, b in flat:
        in_specs.append(pl.BlockSpec((tm, a.shape[1]), lambda i: (i, 0)))
        in_specs.append(pl.BlockSpec(b.shape, lambda i: (0, 0), pipeline_mode=pl.Buffered(1)))
        args += [a, b]
    for e in extras:
        rows = 1 if e.shape[0] == 1 else tm
        in_specs.append(pl.BlockSpec((rows, e.shape[1]), lambda i, rows=rows: (0 if rows == 1 else i, 0)))
        args.append(e)
    return pl.pallas_call(
        body, name=name, grid=(M // tm,), in_specs=in_specs,
        out_specs=[pl.BlockSpec((tm, w), lambda i: (i, 0)) for w, _ in outs]
        + [pl.BlockSpec(s, lambda i: (0, 0)) for s in accs],
        out_shape=[_sds((M, w), dt) for w, dt in outs] + [_sds(s, F32) for s in accs],
        compiler_params=_params(("arbitrary",)))(*args)


def _mm_tn_group(name, pieces, b, out_dtype, tt=MM_TT):
    T, N = b.shape
    tt = min(tt, T)
    steps, n = T // tt, len(pieces)

    def body(*refs):
        a_refs, b_ref, out_refs, accs = refs[:n], refs[n], refs[n + 1:2 * n + 1], refs[2 * n + 1:]
        t = pl.program_id(0)

        @pl.when(t == 0)
        def _():
            for acc in accs:
                acc[...] = jnp.zeros_like(acc)

        for a_ref, acc in zip(a_refs, accs):
            acc[...] += _dot(a_ref[...], b_ref[...], 0, 0)

        @pl.when(t == steps - 1)
        def _():
            for o_ref, acc in zip(out_refs, accs):
                o_ref[...] = acc[...].astype(o_ref.dtype)

    return pl.pallas_call(
        body, name=name, grid=(steps,),
        in_specs=[pl.BlockSpec((tt, a.shape[1]), lambda t: (t, 0)) for a in pieces]
        + [pl.BlockSpec((tt, N), lambda t: (t, 0))],
        out_specs=[pl.BlockSpec((a.shape[1], N), lambda t: (0, 0)) for a in pieces],
        out_shape=[_sds((a.shape[1], N), out_dtype) for a in pieces],
        scratch_shapes=[pltpu.VMEM((a.shape[1], N), F32) for a in pieces],
        compiler_params=_params(("arbitrary",)))(*pieces, b)


def _first(ps, es):
    return (ps[0],)


def _rows(name, body, ins, out_shapes, T, tr=ROW_TILE):
    tr = min(tr, T)

    def spec(shape):
        if shape[0] == T:
            return pl.BlockSpec((tr,) + tuple(shape[1:]), lambda i: (i,) + (0,) * (len(shape) - 1))
        return pl.BlockSpec(tuple(shape), lambda i: (0,) * len(shape))

    return pl.pallas_call(
        body, name=name, grid=(T // tr,),
        in_specs=[spec(a.shape) for a in ins], out_specs=[spec(s.shape) for s in out_shapes],
        out_shape=out_shapes, compiler_params=_params(("arbitrary",)))(*ins)


def _rms(x):
    r = lax.rsqrt(jnp.mean(x * x, axis=-1, keepdims=True) + NORM_EPS)
    return x * r, r


def _rms_bwd(dxn, xn, r):
    return r * (dxn - xn * jnp.mean(dxn * xn, axis=-1, keepdims=True))


def _colsum(v):
    return jnp.sum(v, axis=0, keepdims=True)


def _proj_in(x, g, sc, sh, groups):
    T = x.shape[0]
    tm = min(MM_TM, T)
    ng = len(groups)

    def body(x_ref, g_ref, sc_ref, sh_ref, *rest):
        w_refs, h_ref, out_refs = rest[:ng], rest[ng], rest[ng + 1:]
        xn, _ = _rms(x_ref[...])
        h = (xn * g_ref[...] * (1.0 + sc_ref[...]) + sh_ref[...]).astype(BF16)
        h_ref[...] = h
        for w_ref, o_ref, (w, _, cn) in zip(w_refs, out_refs, groups):
            for j in range(w.shape[0] // cn):
                cols = slice(j * cn, (j + 1) * cn)
                o_ref[:, cols] = _dot(h, w_ref[cols, :], 1, 1).astype(o_ref.dtype)

    row = pl.BlockSpec((1, D_MODEL), lambda i: (0, 0))
    tile = lambda w: pl.BlockSpec((tm, w), lambda i: (i, 0))
    return pl.pallas_call(
        body, name="proj_in", grid=(T // tm,),
        in_specs=[tile(D_MODEL), row, row, row] + [
            pl.BlockSpec(w.shape, lambda i: (0, 0), pipeline_mode=pl.Buffered(1)) for w, _, _ in groups],
        out_specs=[tile(D_MODEL)] + [tile(w.shape[0]) for w, _, _ in groups],
        out_shape=[_sds((T, D_MODEL), BF16)] + [_sds((T, w.shape[0]), dt) for w, dt, _ in groups],
        compiler_params=_params(("parallel",)))(x, g, sc, sh, *[w for w, _, _ in groups])


def _acc_rows(rows):
    w = rows[0].shape[1]
    return jnp.concatenate(rows + [jnp.zeros((8 - len(rows), w), F32)], axis=0)


def _res_norm_rows(ps, es):
    mix = ps[0]
    x, gate, gp, g2, sc, sh = es
    mh, _ = _rms(mix)
    x1 = x + gate * (mh * gp)
    xn, _ = _rms(x1)
    return [mix, x1, xn * g2 * (1.0 + sc) + sh], []


def _final_loss_rows(ps, es):
    x1, tgt, gate, gp = es
    fh, r = _rms(ps[0])
    e = x1 + gate * (fh * gp) - tgt
    loss = 0.5 * jnp.sum(jnp.mean(e * e, axis=-1, keepdims=True))
    dy = e * (1.0 / D_MODEL)
    acc = _acc_rows([_colsum(dy * fh * gp), _colsum(dy * gate * fh)])
    return [dy, _rms_bwd(dy * gate * gp, fh, r)], [acc, jnp.full((1, 128), loss, F32)]


def _res_norm_bwd_rows(ps, es):
    dh = ps[0]
    x1, mix, dy, sc, gate, g2, gp = es
    xn, r1 = _rms(x1)
    rows = [_colsum(dh * xn * g2), _colsum(dh), _colsum(dh * (1.0 + sc) * xn)]
    dx1 = dy + _rms_bwd(dh * (1.0 + sc) * g2, xn, r1)
    mh, rm = _rms(mix)
    rows += [_colsum(dx1 * mh * gp), _colsum(dx1 * gate * mh)]
    return [dx1, _rms_bwd(dx1 * gate * gp, mh, rm)], [_acc_rows(rows)]


def _pre_norm_bwd_rows(ps, es):
    dh = ps[0]
    x, dx1, g, sc = es
    xn, r = _rms(x)
    rows = [_colsum(dh * xn * g), _colsum(dh), _colsum(dh * (1.0 + sc) * xn)]
    return [dx1 + _rms_bwd(dh * (1.0 + sc) * g, xn, r)], [_acc_rows(rows)]


def _rope_tables(pos_col, inv_freq):
    T = pos_col.shape[0]
    half = HEAD_DIM // 2
    groups = 128 // half
    tr = min(512, T)
    rows = tr // groups

    def spread(z):
        lane = lax.broadcasted_iota(jnp.int32, z.shape, 1)
        width = half
        while width < 128:
            z = jnp.where(lane % (2 * width) < width, z, pltpu.roll(z, width, 1))
            width *= 2
        return z

    def body(p_ref, f_ref, c_ref, s_ref):
        lane = lax.broadcasted_iota(jnp.int32, (rows, 128), 1)
        pos = p_ref[0:rows, :].astype(F32)
        for g in range(1, groups):
            pos = jnp.where(lane // half == g, p_ref[rows * g:rows * g + rows, :].astype(F32), pos)
        ang = pos * f_ref[...]
        cos, sin = jnp.cos(ang), jnp.sin(ang)
        sign = jnp.where(lane % HEAD_DIM < half, -1.0, 1.0)
        for g in range(groups):
            shift = (128 - half * g) % 128
            at = slice(rows * g, rows * g + rows)
            c_ref[at, :] = spread(pltpu.roll(cos, shift, 1) if shift else cos)
            s_ref[at, :] = sign * spread(pltpu.roll(sin, shift, 1) if shift else sin)

    return _rows("rope_tables", body, [pos_col, inv_freq],
                 [_sds((T, 128), F32), _sds((T, 128), F32)], T, tr=tr)


def _swap_halves(t):
    W = t.shape[1]
    lane = lax.broadcasted_iota(jnp.int32, t.shape, 1)
    half = HEAD_DIM // 2
    return jnp.where(lane % HEAD_DIM < half, pltpu.roll(t, W - half, 1), pltpu.roll(t, half, 1))


def _widen(c, W):
    return c if W == 128 else jnp.concatenate([c] * (W // 128), axis=1)


def _rope(t, c, s):
    W = t.shape[1]
    return t * _widen(c, W) + _swap_halves(t) * _widen(s, W)


def _unrope(dy, c, s):
    W = dy.shape[1]
    return dy * _widen(c, W) + _swap_halves(dy * _widen(s, W))


def _attn_mask(n):
    qi = lax.broadcasted_iota(jnp.int32, (ATTN_BLK, 2 * ATTN_BLK), 0)
    kj = lax.broadcasted_iota(jnp.int32, (ATTN_BLK, 2 * ATTN_BLK), 1)
    rel = kj - ATTN_BLK
    return (rel <= qi) & (qi - rel < WINDOW) & ((n > 0) | (kj >= ATTN_BLK))


def _attn_load(cur, prv, cc, sc, cp, sp):
    x, xp = cur[...], prv[...]
    q = _rope(x[:, :512], cc[...], sc[...]) * (HEAD_DIM ** -0.5)
    k = jnp.concatenate([_rope(xp[:, 512:640], cp[...], sp[...]),
                         _rope(x[:, 512:640], cc[...], sc[...])], axis=0)
    v = jnp.concatenate([xp[:, 640:768], x[:, 640:768]], axis=0)
    return q, k, v


ROLLED = tuple(h for h in range(N_Q_HEADS) if h % 2 != h // (N_Q_HEADS // N_KV_HEADS))


def _pair_heads(t):
    half = lax.broadcasted_iota(jnp.int32, (ATTN_BLK, 128), 1) // HEAD_DIM
    return jnp.stack([jnp.where(half == h % 2, t[:, 128 * (h // 2):128 * (h // 2) + 128], 0.0)
                      for h in range(N_Q_HEADS)])


def _kv_heads(t):
    half = lax.broadcasted_iota(jnp.int32, t.shape, 1) // HEAD_DIM
    tr = pltpu.roll(t, HEAD_DIM, 1)
    return jnp.stack([jnp.where(half == h % 2, tr if h in ROLLED else t, 0.0)
                      for h in range(N_Q_HEADS)])


def _sink_column(snk):
    return jnp.stack([jnp.full((1, 1), snk[0, h], F32) for h in range(N_Q_HEADS)])


def _attn_probs(qh, kh, mask, sink):
    s = jnp.where(mask, _bdot(qh, kh, 2, 2), NEG_INF)
    m = jnp.maximum(jnp.max(s, axis=-1, keepdims=True), sink)
    p = jnp.exp(s - m)
    es = jnp.exp(sink - m)
    rl = 1.0 / (jnp.sum(p, axis=-1, keepdims=True) + es)
    return p, es, rl


def _attn_specs(order):
    blk = lambda w: pl.BlockSpec((ATTN_BLK, w), lambda s: (order(s), 0))
    prv = lambda w: pl.BlockSpec((ATTN_BLK, w), lambda s: (jnp.maximum(order(s) - 1, 0), 0))
    return [blk(768), prv(768), blk(128), blk(128), prv(128), prv(128),
            pl.BlockSpec(memory_space=pltpu.SMEM)]


def _attn_fwd(pa, cos, sin, sinks):
    T = pa.shape[0]
    nb = T // ATTN_BLK

    def body(cur, prv, cc, sc, cp, sp, snk, y_ref):
        n = pl.program_id(0)
        q, k, v = _attn_load(cur, prv, cc, sc, cp, sp)
        mask = _attn_mask(n)
        half_q = lax.broadcasted_iota(jnp.int32, (ATTN_BLK, 128), 1) // HEAD_DIM
        half_k = lax.broadcasted_iota(jnp.int32, k.shape, 1) // HEAD_DIM
        moved = (pltpu.roll(k, HEAD_DIM, 1), pltpu.roll(v, HEAD_DIM, 1))
        for pair in range(N_Q_HEADS // 2):
            o = None
            for a in range(2):
                h = 2 * pair + a
                ku, vu = moved if h in ROLLED else (k, v)
                qh = jnp.where(half_q == a, q[:, 128 * pair:128 * pair + 128], 0.0).astype(BF16)
                kh = jnp.where(half_k == a, ku, 0.0).astype(BF16)
                vh = jnp.where(half_k == a, vu, 0.0).astype(BF16)
                s = jnp.where(mask, _dot(qh, kh, 1, 1), NEG_INF)
                m = jnp.maximum(jnp.max(s, axis=-1, keepdims=True), snk[0, h])
                p = jnp.exp(s - m)
                rl = 1.0 / (jnp.sum(p, axis=-1, keepdims=True) + jnp.exp(snk[0, h] - m))
                oh = _dot(p.astype(BF16), vh, 1, 0) * rl
                o = oh if o is None else o + oh
            y_ref[:, 128 * pair:128 * pair + 128] = o.astype(BF16)

    return pl.pallas_call(
        body, name="attn_fwd", grid=(nb,), in_specs=_attn_specs(lambda s: s),
        out_specs=pl.BlockSpec((ATTN_BLK, 512), lambda n: (n, 0)),
        out_shape=_sds((T, 512), BF16), compiler_params=_params(("parallel",)))(
            pa, pa, cos, sin, cos, sin, sinks)


def _attn_bwd(pa, cos, sin, sinks, dy):
    T = pa.shape[0]
    nb = T // ATTN_BLK
    rev = lambda s: nb - 1 - s

    def body(cur, prv, cc, sc, cp, sp, snk, dy_ref, dq_ref, dkv_ref, dsink_ref, carry):
        n = rev(pl.program_id(0))

        @pl.when(pl.program_id(0) == 0)
        def _():
            dsink_ref[...] = jnp.zeros_like(dsink_ref)
            carry[...] = jnp.zeros_like(carry)

        q, k, v = _attn_load(cur, prv, cc, sc, cp, sp)
        qh, kh, vh = _pair_heads(q).astype(BF16), _kv_heads(k).astype(BF16), _kv_heads(v).astype(BF16)
        p, es, rl = _attn_probs(qh, kh, _attn_mask(n), _sink_column(snk))
        pn = p * rl
        do = _pair_heads(dy_ref[...]).astype(BF16)
        dp = _bdot(do, vh, 2, 2)
        delta = jnp.sum(pn * dp, axis=-1, keepdims=True)
        ds = (pn * (dp - delta)).astype(BF16)
        dsink = es * rl * delta
        dq = _bdot(ds, kh, 2, 1) * (HEAD_DIM ** -0.5)
        dkh = _bdot_rows(ds, qh)
        dvh = _bdot_rows(pn.astype(BF16), do)

        def fold(t):
            same = [t[h] for h in range(N_Q_HEADS) if h not in ROLLED]
            moved = [t[h] for h in ROLLED]
            return sum(same[1:], same[0]) + pltpu.roll(sum(moved[1:], moved[0]), HEAD_DIM, 1)

        dk, dv = fold(dkh), fold(dvh)
        for h in range(N_Q_HEADS):
            dsink_ref[h:h + 1, :] += -jnp.sum(dsink[h])
        for pair in range(N_Q_HEADS // 2):
            dq_ref[:, 128 * pair:128 * pair + 128] = _unrope(
                dq[2 * pair] + dq[2 * pair + 1], cc[...], sc[...]).astype(BF16)
        dkv_ref[:, 0:128] = _unrope(dk[ATTN_BLK:] + carry[:, 0:128], cc[...], sc[...]).astype(BF16)
        dkv_ref[:, 128:256] = (dv[ATTN_BLK:] + carry[:, 128:256]).astype(BF16)
        carry[:, 0:128] = dk[:ATTN_BLK]
        carry[:, 128:256] = dv[:ATTN_BLK]

    blk = lambda w: pl.BlockSpec((ATTN_BLK, w), lambda s: (rev(s), 0))
    return pl.pallas_call(
        body, name="attn_bwd", grid=(nb,), in_specs=_attn_specs(rev) + [blk(512)],
        out_specs=[blk(512), blk(256), pl.BlockSpec((8, 128), lambda s: (0, 0))],
        out_shape=[_sds((T, 512), BF16), _sds((T, 256), BF16), _sds((8, 128), F32)],
        scratch_shapes=[pltpu.VMEM((ATTN_BLK, 256), F32)],
        compiler_params=_params(("arbitrary",)))(pa, pa, cos, sin, cos, sin, sinks, dy)


CONV_COLS = 2 * MLSTM_HEADS * MLSTM_HEAD_DIM


def _conv_pre(cur_ref, halo_ref, w_ref, b_ref, i, tr):
    xx = jnp.concatenate([jnp.where(i > 0, halo_ref[...], 0.0), cur_ref[...]], axis=0)
    taps = [(pltpu.roll(xx, CONV_WIDTH - 1 - j, 0) if j < CONV_WIDTH - 1 else xx)[8:8 + tr]
            for j in range(CONV_WIDTH)]
    pre = b_ref[...]
    for j in range(CONV_WIDTH):
        pre = pre + taps[j] * w_ref[j:j + 1, :]
    return pre, taps


def _conv_specs(T, tr):
    return [pl.BlockSpec((tr, CONV_COLS), lambda i: (i, 0)),
            pl.BlockSpec((8, CONV_COLS), lambda i: (jnp.maximum(i * (tr // 8) - 1, 0), 0)),
            pl.BlockSpec((CONV_WIDTH, CONV_COLS), lambda i: (0, 0)),
            pl.BlockSpec((1, CONV_COLS), lambda i: (0, 0))]


def _conv_fwd(pm, w, b):
    T = pm.shape[0]
    tr = min(ROW_TILE, T)

    def body(cur_ref, halo_ref, w_ref, b_ref, o_ref):
        pre, _ = _conv_pre(cur_ref, halo_ref, w_ref, b_ref, pl.program_id(0), tr)
        o_ref[...] = pre * _sigmoid(pre)

    return pl.pallas_call(
        body, name="conv_fwd", grid=(T // tr,), in_specs=_conv_specs(T, tr),
        out_specs=pl.BlockSpec((tr, CONV_COLS), lambda i: (i, 0)),
        out_shape=_sds((T, CONV_COLS), F32), compiler_params=_params(("parallel",)))(pm, pm, w, b)


def _conv_bwd(pqk, w, b, dqk):
    T = pqk.shape[0]
    tr = min(ROW_TILE, T)
    nt = T // tr

    def body(cur_ref, prev_ref, next_ref, w_ref, b_ref, d_ref, dnext_ref, du_ref, acc_ref):
        i = pl.program_id(0)

        @pl.when(i == 0)
        def _():
            acc_ref[...] = jnp.zeros_like(acc_ref)

        last = i == nt - 1
        xx = jnp.concatenate([jnp.where(i > 0, prev_ref[...], 0.0), cur_ref[...],
                              jnp.where(last, 0.0, next_ref[...])], axis=0)
        taps = [(pltpu.roll(xx, CONV_WIDTH - 1 - j, 0) if j < CONV_WIDTH - 1 else xx)[8:16 + tr]
                for j in range(CONV_WIDTH)]
        pre = b_ref[...]
        for j in range(CONV_WIDTH):
            pre = pre + taps[j] * w_ref[j:j + 1, :]
        sg = _sigmoid(pre)
        dd = jnp.concatenate([d_ref[...], jnp.where(last, 0.0, dnext_ref[...])], axis=0)
        dpre = dd * (sg * (1.0 + pre * (1.0 - sg)))
        for j in range(CONV_WIDTH):
            acc_ref[j:j + 1, :] += _colsum(dpre[:tr] * taps[j][:tr])
        acc_ref[CONV_WIDTH:CONV_WIDTH + 1, :] += _colsum(dpre[:tr])
        du = dpre[:tr] * w_ref[CONV_WIDTH - 1:CONV_WIDTH, :]
        for j in range(CONV_WIDTH - 1):
            k = CONV_WIDTH - 1 - j
            du = du + pltpu.roll(dpre, tr + 8 - k, 0)[:tr] * w_ref[j:j + 1, :]
        du_ref[...] = du.astype(BF16)

    tile = pl.BlockSpec((tr, CONV_COLS), lambda i: (i, 0))
    after = pl.BlockSpec((8, CONV_COLS), lambda i: (jnp.minimum((i + 1) * (tr // 8), T // 8 - 1), 0))
    before = pl.BlockSpec((8, CONV_COLS), lambda i: (jnp.maximum(i * (tr // 8) - 1, 0), 0))
    return pl.pallas_call(
        body, name="conv_bwd", grid=(nt,),
        in_specs=[tile, before, after, pl.BlockSpec((CONV_WIDTH, CONV_COLS), lambda i: (0, 0)),
                  pl.BlockSpec((1, CONV_COLS), lambda i: (0, 0)), tile, after],
        out_specs=[tile, pl.BlockSpec((8, CONV_COLS), lambda i: (0, 0))],
        out_shape=[_sds((T, CONV_COLS), BF16), _sds((8, CONV_COLS), F32)],
        compiler_params=_params(("arbitrary",)))(pqk, pqk, pqk, w, b, dqk, dqk)


def _log_sigmoid(x):
    return jnp.minimum(x, 0.0) - jnp.log1p(jnp.exp(-jnp.abs(x)))


def _chunk_cumsum(x, axis):
    idx = lax.broadcasted_iota(jnp.int32, x.shape, axis) % MLSTM_CHUNK
    k = 1
    while k < MLSTM_CHUNK:
        x = x + jnp.where(idx >= k, pltpu.roll(x, k, axis), 0.0)
        k *= 2
    return x


def _chunk_rev_cumsum(x, axis):
    n = x.shape[axis]
    idx = lax.broadcasted_iota(jnp.int32, x.shape, axis) % MLSTM_CHUNK
    k = 1
    while k < MLSTM_CHUNK:
        x = x + jnp.where(idx < MLSTM_CHUNK - k, pltpu.roll(x, n - k, axis), 0.0)
        k *= 2
    return x


def _mlstm_gates(gc_ref, bc_ref, gr_ref, br_ref):
    gc = gc_ref[...] + bc_ref[...]
    gr = gr_ref[...] + br_ref[...]
    return gc, _chunk_cumsum(_log_sigmoid(gc), 0), gr, _chunk_cumsum(_log_sigmoid(gr), 1)


def _heads(ref, base=0):
    D = MLSTM_HEAD_DIM
    return jnp.stack([ref[:, base + D * h:base + D * h + D] for h in range(MLSTM_HEADS)])


def _mlstm_inputs(q_ref, k_ref, v_ref, gc, bc, gr, br):
    H = MLSTM_HEADS
    q, v = _heads(q_ref), _heads(v_ref)
    ks = _heads(k_ref) * (MLSTM_HEAD_DIM ** -0.5)
    return dict(
        q=q, ks=ks, qb=q.astype(BF16), kb=ks.astype(BF16), vb=v.astype(BF16),
        b_col=jnp.stack([bc[:, H + h:H + h + 1] for h in range(H)]),
        i_col=jnp.stack([gc[:, h:h + 1] for h in range(H)]),
        b_row=jnp.stack([br[H + h:H + h + 1, :] for h in range(H)]),
        i_row=jnp.stack([gr[h:h + 1, :] for h in range(H)]))


def _mlstm_head(f, c_prev, n_prev, m_prev):
    L = MLSTM_CHUNK
    q, qb = f["q"], f["qb"]
    t = lax.broadcasted_iota(jnp.int32, (1, 2 * L, 2 * L), 1)
    s = lax.broadcasted_iota(jnp.int32, (1, 2 * L, 2 * L), 2)
    mask = (t // L == s // L) & (s <= t)
    d = jnp.where(mask, f["b_col"] - f["b_row"] + f["i_row"], NEG_INF)
    row = lax.broadcasted_iota(jnp.int32, (1, 2 * L, 1), 1)
    inter = f["b_col"] + jnp.where(row < L, m_prev[0], m_prev[1])
    m_t = jnp.maximum(inter, jnp.max(d, axis=-1, keepdims=True))
    w_intra = jnp.exp(d - m_t)
    w_inter = jnp.exp(inter - m_t)
    sc = _bdot(qb, f["kb"], 2, 2) * w_intra
    qc = jnp.concatenate([_bdot(qb[:, :L], c_prev[0].astype(BF16), 2, 1),
                          _bdot(qb[:, L:], c_prev[1].astype(BF16), 2, 1)], axis=1)
    qn = jnp.concatenate([jnp.sum(q[:, :L] * n_prev[0], axis=-1, keepdims=True),
                          jnp.sum(q[:, L:] * n_prev[1], axis=-1, keepdims=True)], axis=1)
    num = _bdot(sc.astype(BF16), f["vb"], 2, 1) + w_inter * qc
    den = jnp.sum(sc, axis=-1, keepdims=True) + w_inter * qn
    return dict(f, w_intra=w_intra, w_inter=w_inter, sc=sc, qc=qc, qn=qn, num=num, den=den,
                floor=jnp.exp(-m_t))


def _mlstm_update(f, ch, c, n, m):
    L = MLSTM_CHUNK
    rows = slice(L * ch, L * ch + L)
    b_col = f["b_col"][:, rows]
    g_last = b_col[:, L - 1:L]
    a_col = g_last - b_col + f["i_col"][:, rows]
    m_new = jnp.maximum(g_last + m, jnp.max(a_col, axis=1, keepdims=True))
    decay = jnp.exp(g_last + m - m_new)
    e_a = jnp.exp(a_col - m_new)
    kw = f["ks"][:, rows] * e_a
    c_new = decay * c + _bdot_rows(kw.astype(BF16), f["vb"][:, rows])
    n_new = decay * n + jnp.sum(kw, axis=1, keepdims=True)
    return c_new, n_new, m_new, decay, e_a, kw


def _mlstm_specs(T, order):
    blk = lambda w, col: pl.BlockSpec((STEP_ROWS, w), lambda s: (order(s), col))
    return [blk(512, 0), blk(512, 1), blk(512, 0), blk(128, 0),
            pl.BlockSpec((1, 128), lambda s: (0, 0)),
            pl.BlockSpec((8, STEP_ROWS), lambda s: (0, order(s))),
            pl.BlockSpec((8, 128), lambda s: (0, 0))]


def _lanes(m):
    return jnp.broadcast_to(m, m.shape[:-1] + (128,))


def _mlstm_fwd(qk, pm, gcol, bcol, grow, brow, norm_w):
    T = qk.shape[0]
    steps = T // STEP_ROWS
    H, D = MLSTM_HEADS, MLSTM_HEAD_DIM

    def body(q_ref, k_ref, v_ref, gc_ref, bc_ref, gr_ref, br_ref, o_ref, w_ref,
             h_ref, y_ref, cs_ref, ns_ref, ms_ref, c_st, n_st, m_st):
        @pl.when(pl.program_id(0) == 0)
        def _():
            c_st[...] = jnp.zeros_like(c_st)
            n_st[...] = jnp.zeros_like(n_st)
            m_st[...] = jnp.zeros_like(m_st)

        f = _mlstm_inputs(q_ref, k_ref, v_ref, *_mlstm_gates(gc_ref, bc_ref, gr_ref, br_ref))
        c0, n0, m0 = c_st[...], n_st[...], m_st[:, :, 0:1]
        c1, n1, m1, _, _, _ = _mlstm_update(f, 0, c0, n0, m0)
        c2, n2, m2, _, _, _ = _mlstm_update(f, 1, c1, n1, m1)
        f = _mlstm_head(f, (c0, c1), (n0, n1), (m0, m1))
        h = f["num"] / jnp.maximum(jnp.abs(f["den"]), f["floor"])
        hn, _ = _head_norm(h)
        w = jnp.stack([w_ref[:, D * hd:D * hd + D] for hd in range(H)])
        y = _sigmoid(_heads(o_ref).astype(F32)) * hn * w
        for hd in range(H):
            h_ref[:, D * hd:D * hd + D] = h[hd]
            y_ref[:, D * hd:D * hd + D] = y[hd].astype(BF16)
        cs_ref[0], cs_ref[1] = c0, c1
        ns_ref[0], ns_ref[1] = n0, n1
        ms_ref[0], ms_ref[1] = _lanes(m0), _lanes(m1)
        c_st[...], n_st[...], m_st[...] = c2, n2, _lanes(m2)

    vec = pl.BlockSpec((2, H, 1, 128), lambda s: (s, 0, 0, 0))
    rows = pl.BlockSpec((STEP_ROWS, 512), lambda s: (s, 0))
    return pl.pallas_call(
        body, name="mlstm_fwd", grid=(steps,),
        in_specs=_mlstm_specs(T, lambda s: s) + [pl.BlockSpec((STEP_ROWS, 512), lambda s: (s, 1)),
                                                 pl.BlockSpec((1, 512), lambda s: (0, 0))],
        out_specs=[rows, rows, pl.BlockSpec((2, H, 128, 128), lambda s: (s, 0, 0, 0)), vec, vec],
        out_shape=[_sds((T, 512), F32), _sds((T, 512), BF16), _sds((2 * steps, H, 128, 128), F32),
                   _sds((2 * steps, H, 1, 128), F32), _sds((2 * steps, H, 1, 128), F32)],
        scratch_shapes=[pltpu.VMEM((H, 128, 128), F32), pltpu.VMEM((H, 1, 128), F32),
                        pltpu.VMEM((H, 1, 128), F32)],
        compiler_params=_params(("arbitrary",)))(qk, qk, pm, gcol, bcol, grow, brow, pm, norm_w)


def _mlstm_bwd(qk, pm, gcol, bcol, grow, brow, cs, ns, ms, dh):
    T = qk.shape[0]
    steps = T // STEP_ROWS
    H, L, D = MLSTM_HEADS, MLSTM_CHUNK, MLSTM_HEAD_DIM
    rev = lambda s: steps - 1 - s

    def body(q_ref, k_ref, v_ref, gc_ref, bc_ref, gr_ref, br_ref, cs_ref, ns_ref, ms_ref, dh_ref,
             dqk_ref, dv_ref, dgc_ref, dgr_ref, dc_st, dn_st):
        @pl.when(pl.program_id(0) == 0)
        def _():
            dc_st[...] = jnp.zeros_like(dc_st)
            dn_st[...] = jnp.zeros_like(dn_st)

        f = _mlstm_inputs(q_ref, k_ref, v_ref, *_mlstm_gates(gc_ref, bc_ref, gr_ref, br_ref))
        c_prev = (cs_ref[0], cs_ref[1])
        n_prev = (ns_ref[0], ns_ref[1])
        m_prev = (ms_ref[0, :, :, 0:1], ms_ref[1, :, :, 0:1])
        f = _mlstm_head(f, c_prev, n_prev, m_prev)
        big = jnp.abs(f["den"]) > f["floor"]
        rden = 1.0 / jnp.where(big, jnp.abs(f["den"]), f["floor"])
        dnum = _heads(dh_ref) * rden
        hdh = jnp.sum(f["num"] * dnum, axis=-1, keepdims=True)
        dden = jnp.where(big, -hdh * rden * jnp.sign(f["den"]), 0.0)
        dnum_b = dnum.astype(BF16)
        dsc = _bdot(dnum_b, f["vb"], 2, 2) + dden
        g = dsc * f["sc"]
        dv = _bdot_rows(f["sc"].astype(BF16), dnum_b)
        dqk_ = (dsc * f["w_intra"]).astype(BF16)
        dq = _bdot(dqk_, f["kb"], 2, 1)
        dks = _bdot_rows(dqk_, f["qb"])
        wdn = f["w_inter"] * dnum
        wdn_b = wdn.astype(BF16)
        wdd = f["w_inter"] * dden
        u = jnp.sum(f["qc"] * wdn, axis=-1, keepdims=True) + wdd * f["qn"]
        dks_s, dv_s, z_s, dg_s = [None, None], [None, None], [None, None], [None, None]
        dcn, dnn = dc_st[...], dn_st[...]
        for ch in (1, 0):
            rows = slice(L * ch, L * ch + L)
            _, _, _, decay, e_a, kw = _mlstm_update(f, ch, c_prev[ch], n_prev[ch], m_prev[ch])
            dcn_b = dcn.astype(BF16)
            dkw = _bdot(f["vb"][:, rows], dcn_b, 2, 2) + dnn
            dks_s[ch] = e_a * dkw
            dv_s[ch] = _bdot(kw.astype(BF16), dcn_b, 2, 1)
            z_s[ch] = e_a * jnp.sum(f["ks"][:, rows] * dkw, axis=-1, keepdims=True)
            dg_s[ch] = jnp.sum(z_s[ch], axis=1, keepdims=True) + decay * (
                jnp.sum(c_prev[ch] * dcn, axis=(1, 2), keepdims=True)
                + jnp.sum(n_prev[ch] * dnn, axis=(1, 2), keepdims=True))
            dcn = decay * dcn + _bdot_rows(f["qb"][:, rows], wdn_b[:, rows])
            dnn = decay * dnn + jnp.sum(wdd[:, rows] * f["q"][:, rows], axis=1, keepdims=True)
        dc_st[...], dn_st[...] = dcn, dnn
        dq = dq + jnp.concatenate(
            [_bdot(wdn_b[:, :L], c_prev[0].astype(BF16), 2, 2) + wdd[:, :L] * n_prev[0],
             _bdot(wdn_b[:, L:], c_prev[1].astype(BF16), 2, 2) + wdd[:, L:] * n_prev[1]], axis=1)
        dks = (dks + jnp.concatenate(dks_s, axis=1)) * (D ** -0.5)
        dv = dv + jnp.concatenate(dv_s, axis=1)
        z = jnp.concatenate(z_s, axis=1)
        row = lax.broadcasted_iota(jnp.int32, (1, STEP_ROWS, 1), 1)
        dg_col = jnp.where(row == L - 1, dg_s[0], 0.0) + jnp.where(row == 2 * L - 1, dg_s[1], 0.0)
        db_col = jnp.sum(g, axis=-1, keepdims=True) + u - z + dg_col
        g_row = jnp.sum(g, axis=1, keepdims=True)
        lane = lax.broadcasted_iota(jnp.int32, (STEP_ROWS, 128), 1)
        sub = lax.broadcasted_iota(jnp.int32, (8, STEP_ROWS), 0)
        dgc = jnp.zeros((STEP_ROWS, 128), F32)
        dgr = jnp.zeros((8, STEP_ROWS), F32)
        for hd in range(H):
            dgc = dgc + jnp.where(lane == hd, z[hd], 0.0) + jnp.where(lane == H + hd, db_col[hd], 0.0)
            dgr = dgr + jnp.where(sub == hd, g_row[hd], 0.0) - jnp.where(sub == H + hd, g_row[hd], 0.0)
            dqk_ref[:, D * hd:D * hd + D] = dq[hd]
            dqk_ref[:, H * D + D * hd:H * D + D * hd + D] = dks[hd]
            dv_ref[:, D * hd:D * hd + D] = dv[hd].astype(BF16)
        dgc_ref[...] = dgc
        dgr_ref[...] = dgr

    return pl.pallas_call(
        body, name="mlstm_bwd", grid=(steps,),
        in_specs=_mlstm_specs(T, rev) + [
            pl.BlockSpec((2, H, 128, 128), lambda s: (rev(s), 0, 0, 0)),
            pl.BlockSpec((2, H, 1, 128), lambda s: (rev(s), 0, 0, 0)),
            pl.BlockSpec((2, H, 1, 128), lambda s: (rev(s), 0, 0, 0)),
            pl.BlockSpec((STEP_ROWS, 512), lambda s: (rev(s), 0))],
        out_specs=[pl.BlockSpec((STEP_ROWS, 1024), lambda s: (rev(s), 0)),
                   pl.BlockSpec((STEP_ROWS, 512), lambda s: (rev(s), 0)),
                   pl.BlockSpec((STEP_ROWS, 128), lambda s: (rev(s), 0)),
                   pl.BlockSpec((8, STEP_ROWS), lambda s: (0, rev(s)))],
        out_shape=[_sds((T, 1024), F32), _sds((T, 512), BF16), _sds((T, 128), F32), _sds((8, T), F32)],
        scratch_shapes=[pltpu.VMEM((H, 128, 128), F32), pltpu.VMEM((H, 1, 128), F32)],
        compiler_params=_params(("arbitrary",)))(qk, qk, pm, gcol, bcol, grow, brow, cs, ns, ms, dh)


def _rows_to_lanes(x):
    eye = (lax.broadcasted_iota(jnp.int32, (8, 128), 0)
           == lax.broadcasted_iota(jnp.int32, (8, 128), 1)).astype(BF16)
    out, rest = None, x
    for _ in range(3):
        piece = rest.astype(BF16)
        rest = rest - piece.astype(F32)
        t = _dot(piece, eye, 0, 0)
        out = t if out is None else out + t
    return out


def _gate_bwd(dgc, dgr, gcol, bcol):
    T = dgc.shape[0]
    tr = min(ROW_TILE, T)

    def body(a_ref, b_ref, g_ref, bias_ref, o_ref, acc_ref):
        i = pl.program_id(0)

        @pl.when(i == 0)
        def _():
            acc_ref[...] = jnp.zeros_like(acc_ref)

        d = a_ref[...] + _rows_to_lanes(b_ref[:, pl.ds(pl.multiple_of(i * tr, 128), tr)])
        lane = lax.broadcasted_iota(jnp.int32, d.shape, 1)
        is_f = (lane >= MLSTM_HEADS) & (lane < 2 * MLSTM_HEADS)
        dlogf = _chunk_rev_cumsum(jnp.where(is_f, d, 0.0), 0)
        out = jnp.where(is_f, dlogf * _sigmoid(-(g_ref[...] + bias_ref[...])), d)
        o_ref[...] = out.astype(BF16)
        acc_ref[0:1, :] += _colsum(out)

    return _rows("gate_bwd", body, [dgc, dgr, gcol, bcol],
                 [_sds((T, 128), BF16), _sds((8, 128), F32)], T, tr=tr)


def _head_norm(h, mu_axis=-1):
    mu = jnp.mean(h, axis=-1, keepdims=True)
    hc = h - mu
    r = lax.rsqrt(jnp.mean(hc * hc, axis=-1, keepdims=True) + NORM_EPS)
    return hc * r, r


def _mlstm_out_bwd_rows(ps, es):
    hm, vo, w_all = es
    D, width = MLSTM_HEAD_DIM, MLSTM_HEADS * MLSTM_HEAD_DIM
    dhs, dos, dws = [], [], []
    for hd in range(MLSTM_HEADS):
        cols = slice(D * hd, D * hd + D)
        hn, r = _head_norm(hm[:, cols])
        sg = _sigmoid(vo[:, width + D * hd:width + D * hd + D].astype(F32))
        dy, w = ps[0][:, cols], w_all[:, cols]
        dos.append(dy * hn * w * sg * (1.0 - sg))
        dyn = dy * sg
        dws.append(_colsum(dyn * hn))
        dhn = dyn * w
        dhs.append(r * (dhn - jnp.mean(dhn, axis=-1, keepdims=True)
                        - hn * jnp.mean(dhn * hn, axis=-1, keepdims=True)))
    cat = lambda parts: jnp.concatenate(parts, axis=1)
    return [cat(dhs), cat(dos)], [_acc_rows([cat(dws)])]


ADAM_TILE_ELEMS = 256 * 1024


def _adamw(name, w, g, m, v):
    R, C = w.shape
    fits = [t for t in range(8, R + 1, 8) if R % t == 0 and t * C <= ADAM_TILE_ELEMS]
    if fits or R * C <= ADAM_TILE_ELEMS:
        tr = fits[-1] if fits else R
        spec, grid = pl.BlockSpec((tr, C), lambda i: (i, 0)), (R // tr,)
    else:
        spec, grid = pl.BlockSpec((R, 128), lambda i: (0, i)), (C // 128,)
    c1 = 1.0 - ADAM_B1 ** ADAM_STEP
    c2 = 1.0 - ADAM_B2 ** ADAM_STEP

    def body(w_ref, g_ref, m_ref, v_ref, d_ref, mo_ref, vo_ref):
        g = g_ref[...]
        m = ADAM_B1 * m_ref[...] + (1.0 - ADAM_B1) * g
        v = ADAM_B2 * v_ref[...] + (1.0 - ADAM_B2) * (g * g)
        mo_ref[...] = m
        vo_ref[...] = v
        d_ref[...] = -ADAM_LR * ((m / c1) / (jnp.sqrt(v / c2) + ADAM_EPS) + ADAM_WD * w_ref[...])

    return pl.pallas_call(
        body, name=name, grid=grid, in_specs=[spec] * 4, out_specs=[spec] * 3,
        out_shape=[_sds((R, C), F32)] * 3, compiler_params=_params(("parallel",)))(w, g, m, v)


def _place():
    return lax.axis_index("x"), lax.axis_index("y"), lax.axis_index("c")


def _all_gather8(name, blk, space):
    m, n = blk.shape

    def body(x_ref, out_ref, send_sems, recv_sems, local_sem):
        x, y, c = _place()
        me, sibling = (x, y, c), (x, y, 1 - c)
        chips = [(1 - x, y), (x, 1 - y), (1 - x, 1 - y)]

        def rows(px, py, pc):
            return out_ref.at[pl.ds((4 * px + 2 * py + pc) * m, m), :]

        def copy(k, block, to, src=None):
            return pltpu.make_async_remote_copy(
                src_ref=rows(*block) if src is None else src, dst_ref=rows(*block),
                send_sem=send_sems.at[k], recv_sem=recv_sems.at[k],
                device_id=to, device_id_type=MESH)

        mine = pltpu.make_async_copy(x_ref, rows(*me), local_sem)
        mine.start()
        first = [copy(0, me, sibling, src=x_ref)]
        first += [copy(1 + j, me, (*chip, c), src=x_ref) for j, chip in enumerate(chips)]
        for cp in first:
            cp.start()
        passed = [copy(4 + j, (*chip, c), sibling) for j, chip in enumerate(chips)]
        for j, chip in enumerate(chips):
            copy(1 + j, (*chip, c), me).wait_recv()
            passed[j].start()
        copy(0, sibling, me).wait_recv()
        for j, chip in enumerate(chips):
            copy(4 + j, (*chip, 1 - c), me).wait_recv()
        for cp in first + passed:
            cp.wait_send()
        mine.wait()

    return pl.pallas_call(
        body, name=name, out_shape=_sds((8 * m, n), blk.dtype),
        in_specs=[pl.BlockSpec(memory_space=space)], out_specs=pl.BlockSpec(memory_space=space),
        scratch_shapes=[pltpu.SemaphoreType.DMA((7,)), pltpu.SemaphoreType.DMA((7,)),
                        pltpu.SemaphoreType.DMA],
        compiler_params=pltpu.CompilerParams(vmem_limit_bytes=VMEM_LIMIT))(blk)


def _hbm_specs(n):
    return [pl.BlockSpec(memory_space=pl.ANY)] * n


def _swap_halves_sibling(name, srcs):
    nw = len(srcs)

    def body(*refs):
        src_refs, dst_refs, send_sems, recv_sems = refs[:nw], refs[nw:2 * nw], refs[2 * nw], refs[2 * nw + 1]
        x, y, c = _place()
        cps = [pltpu.make_async_remote_copy(
            src_ref=src_refs[w].at[pl.ds(0, 4), 1 - c], dst_ref=dst_refs[w],
            send_sem=send_sems.at[w], recv_sem=recv_sems.at[w], device_id=(x, y, 1 - c),
            device_id_type=MESH) for w in range(nw)]
        for cp in cps:
            cp.start()
        for cp in cps:
            cp.wait()

    return pl.pallas_call(
        body, name=name, out_shape=[_sds(s.shape[:1] + s.shape[2:], s.dtype) for s in srcs],
        in_specs=_hbm_specs(nw), out_specs=_hbm_specs(nw),
        scratch_shapes=[pltpu.SemaphoreType.DMA((nw,)), pltpu.SemaphoreType.DMA((nw,))])(*srcs)


def _split_start(name, srcs, lands, copies, per_array, after):
    nw = len(srcs)

    def body(*refs):
        send_sems, recv_sems, token = refs[2 * nw + 1], refs[2 * nw + 2], refs[-1]
        for w in range(nw):
            for k, (s, d, dev) in enumerate(copies(refs[w], refs[nw + w], *_place())):
                pltpu.make_async_remote_copy(
                    src_ref=s, dst_ref=d, send_sem=send_sems.at[w * per_array + k],
                    recv_sem=recv_sems.at[w * per_array + k], device_id=dev, device_id_type=MESH).start()
        token[...] = jnp.zeros_like(token)

    hbm, sem = pl.BlockSpec(memory_space=pltpu.HBM), pl.BlockSpec(memory_space=pltpu.SEMAPHORE)
    arrays = list(srcs) + list(lands)
    out = pl.pallas_call(
        body, name=name,
        out_shape=(pltpu.SemaphoreType.DMA((nw * per_array,)), pltpu.SemaphoreType.DMA((nw * per_array,)),
                   *[pltpu.HBM(a.shape, a.dtype) for a in arrays], _sds((8, 128), F32)),
        in_specs=[hbm] * (2 * nw) + [pl.BlockSpec(memory_space=pl.ANY)],
        out_specs=(sem, sem, *[hbm] * (2 * nw), pl.BlockSpec(memory_space=pltpu.VMEM)),
        input_output_aliases={i: 2 + i for i in range(2 * nw)},
        compiler_params=pltpu.CompilerParams(has_side_effects=pltpu.SideEffectType.DATAFLOW_SIDE_EFFECTING))(
            *[pltpu.with_memory_space_constraint(a, pltpu.HBM) for a in arrays], after)
    return out[0], out[1], out[2:2 + nw], out[2 + nw:2 + 2 * nw], out[-1]


def _split_wait(name, started, after, waits, per_array):
    send_sems, recv_sems, srcs, lands, _ = started
    nw = len(srcs)

    def body(*refs):
        send_sems, recv_sems = refs[2 * nw], refs[2 * nw + 1]
        x, y, c = _place()
        for w in range(nw):
            for k, (s, d) in enumerate(waits(refs[w], refs[nw + w], x, y, c)):
                cp = pltpu.make_async_remote_copy(
                    src_ref=s, dst_ref=d, send_sem=send_sems.at[w * per_array + k],
                    recv_sem=recv_sems.at[w * per_array + k], device_id=(x, y, 1 - c),
                    device_id_type=MESH)
                cp.wait_send()
                cp.wait_recv()

    hbm, sem = pl.BlockSpec(memory_space=pltpu.HBM), pl.BlockSpec(memory_space=pltpu.SEMAPHORE)
    arrays = list(srcs) + list(lands)
    out = pl.pallas_call(
        body, name=name, out_shape=tuple(pltpu.HBM(a.shape, a.dtype) for a in arrays),
        in_specs=[hbm] * (2 * nw) + [sem, sem, pl.BlockSpec(memory_space=pl.ANY)],
        out_specs=tuple([hbm] * (2 * nw)), input_output_aliases={i: i for i in range(2 * nw)},
        compiler_params=pltpu.CompilerParams(has_side_effects=pltpu.SideEffectType.DATAFLOW_SIDE_EFFECTING))(
            *arrays, send_sems, recv_sems, after)
    return list(out[nw:])


def _other_chips(x, y):
    return [(1 - x, y), (x, 1 - y), (1 - x, 1 - y)]


def _gather_sends(src_ref, land_ref, x, y, c):
    to = land_ref.at[2 * x + y, c]
    return [(src_ref, to, (x, y, 1 - c))] + [(src_ref, to, (px, py, c)) for px, py in _other_chips(x, y)]


def _gather_lands(src_ref, land_ref, x, y, c):
    return [(src_ref, land_ref.at[2 * x + y, 1 - c])] + [
        (src_ref, land_ref.at[2 * px + py, c]) for px, py in _other_chips(x, y)]


def _gather_sends_all(src_ref, land_ref, x, y, c):
    to = land_ref.at[2 * x + y, c]
    return [(src_ref, to, (x, y, 1 - c))] + [
        (src_ref, to, (px, py, pc)) for px, py in _other_chips(x, y) for pc in (c, 1 - c)]


def _gather_lands_all(src_ref, land_ref, x, y, c):
    return [(src_ref, land_ref.at[2 * x + y, 1 - c])] + [
        (src_ref, land_ref.at[2 * px + py, pc]) for px, py in _other_chips(x, y) for pc in (c, 1 - c)]


def _scatter_sends(src_ref, land_ref, x, y, c):
    return [(src_ref.at[2 * px + py], land_ref.at[2 * x + y], (px, py, c)) for px, py in _other_chips(x, y)]


def _scatter_lands(src_ref, land_ref, x, y, c):
    return [(src_ref.at[2 * x + y], land_ref.at[2 * px + py]) for px, py in _other_chips(x, y)]


def _forward_sibling(name, lands):
    nw = len(lands)

    def body(*refs):
        land_refs, out_refs, send_sems, recv_sems = refs[:nw], refs[nw:2 * nw], refs[2 * nw], refs[2 * nw + 1]
        x, y, c = _place()
        cps = []
        for w in range(nw):
            cps += [pltpu.make_async_remote_copy(
                src_ref=land_refs[w].at[2 * px + py, c], dst_ref=out_refs[w].at[2 * px + py, c],
                send_sem=send_sems.at[w, j], recv_sem=recv_sems.at[w, j], device_id=(x, y, 1 - c),
                device_id_type=MESH) for j, (px, py) in enumerate(_other_chips(x, y))]
        for cp in cps:
            cp.start()
        for w in range(nw):
            for j, (px, py) in enumerate(_other_chips(x, y)):
                slot = out_refs[w].at[2 * px + py, 1 - c]
                pltpu.make_async_remote_copy(src_ref=slot, dst_ref=slot, send_sem=send_sems.at[w, j],
                                             recv_sem=recv_sems.at[w, j], device_id=(x, y, 1 - c),
                                             device_id_type=MESH).wait_recv()
        for cp in cps:
            cp.wait_send()

    return pl.pallas_call(
        body, name=name, out_shape=[_sds(a.shape, a.dtype) for a in lands],
        in_specs=_hbm_specs(nw), out_specs=_hbm_specs(nw), input_output_aliases={i: i for i in range(nw)},
        scratch_shapes=[pltpu.SemaphoreType.DMA((nw, 3)), pltpu.SemaphoreType.DMA((nw, 3))])(*lands)


def _share_halves(name, halves):
    nw = len(halves)

    def body(*refs):
        in_refs, out_refs, send_sems, recv_sems = refs[:nw], refs[nw:2 * nw], refs[2 * nw], refs[2 * nw + 1]
        x, y, c = _place()
        cps = [pltpu.make_async_remote_copy(
            src_ref=in_refs[w].at[c], dst_ref=out_refs[w].at[c], send_sem=send_sems.at[w],
            recv_sem=recv_sems.at[w], device_id=(x, y, 1 - c), device_id_type=MESH) for w in range(nw)]
        for cp in cps:
            cp.start()
        for w in range(nw):
            slot = out_refs[w].at[1 - c]
            pltpu.make_async_remote_copy(src_ref=slot, dst_ref=slot, send_sem=send_sems.at[w],
                                         recv_sem=recv_sems.at[w], device_id=(x, y, 1 - c),
                                         device_id_type=MESH).wait_recv()
        for cp in cps:
            cp.wait_send()

    return pl.pallas_call(
        body, name=name, out_shape=[_sds(a.shape, a.dtype) for a in halves],
        in_specs=_hbm_specs(nw), out_specs=_hbm_specs(nw), input_output_aliases={i: i for i in range(nw)},
        scratch_shapes=[pltpu.SemaphoreType.DMA((nw,)), pltpu.SemaphoreType.DMA((nw,))])(*halves)


def _place_blocks(name, blks, place):
    nw = len(blks)

    def body(p_ref, *refs):
        for b_ref, o_ref in zip(refs[:nw], refs[nw:]):
            o_ref[...] = b_ref[...]

    return pl.pallas_call(
        body, name=name,
        grid_spec=pltpu.PrefetchScalarGridSpec(
            num_scalar_prefetch=1, grid=(1,),
            in_specs=[pl.BlockSpec(b.shape, lambda i, p: (0, 0)) for b in blks],
            out_specs=[pl.BlockSpec((None, None) + b.shape, lambda i, p: (p[0], p[1], 0, 0)) for b in blks]),
        out_shape=[_sds((4, 2) + b.shape, b.dtype) for b in blks],
        compiler_params=_params(("arbitrary",)))(place, *blks)


def _pair_sum(name, fulls, gots, place):
    nw = len(fulls)

    def body(p_ref, *refs):
        s = pl.program_id(0)
        for a_ref, b_ref, o_ref, l_ref in zip(refs[:nw], refs[nw:2 * nw], refs[2 * nw:3 * nw], refs[3 * nw:]):
            o_ref[...] = (a_ref[...].astype(F32) + b_ref[...].astype(F32)).astype(o_ref.dtype)

            @pl.when(s == p_ref[0])
            def _():
                l_ref[...] = o_ref[...]

    slab = lambda a: pl.BlockSpec((None,) + a.shape[1:], lambda s, p: (s, 0, 0))
    mine = lambda a: pl.BlockSpec((None,) + a.shape[1:], lambda s, p: (p[0], 0, 0))
    out = pl.pallas_call(
        body, name=name,
        grid_spec=pltpu.PrefetchScalarGridSpec(
            num_scalar_prefetch=1, grid=(4,),
            in_specs=[pl.BlockSpec((None, None) + a.shape[2:], lambda s, p: (s, p[1], 0, 0)) for a in fulls]
            + [slab(b) for b in gots],
            out_specs=[slab(b) for b in gots] + [mine(b) for b in gots]),
        out_shape=[_sds(b.shape, BF16) for b in gots] * 2,
        compiler_params=_params(("arbitrary",)))(place, *fulls, *gots)
    return out[:nw], out[nw:]


def _sum4(name, arrs, place):
    nw = len(arrs)

    def body(p_ref, *refs):
        for a_ref, o_ref in zip(refs[:nw], refs[nw:]):
            acc = a_ref[0].astype(F32)
            for s in range(1, 4):
                acc = acc + a_ref[s].astype(F32)
            o_ref[...] = acc

    return pl.pallas_call(
        body, name=name,
        grid_spec=pltpu.PrefetchScalarGridSpec(
            num_scalar_prefetch=1, grid=(1,),
            in_specs=[pl.BlockSpec(a.shape, lambda i, p: (0, 0, 0)) for a in arrs],
            out_specs=[pl.BlockSpec((None,) + a.shape[1:], lambda i, p: (p[1], 0, 0)) for a in arrs]),
        out_shape=[_sds((2,) + a.shape[1:], F32) for a in arrs],
        compiler_params=_params(("arbitrary",)))(place, *arrs)


def _small_update(gathered, params, slots):
    c1 = 1.0 - ADAM_B1 ** ADAM_STEP
    c2 = 1.0 - ADAM_B2 ** ADAM_STEP
    n = len(params)

    def body(g_ref, *refs):
        ins, sum_ref, outs = refs[:3 * n], refs[3 * n], refs[3 * n + 1:]
        g_all = g_ref[0:1, :]
        for d in range(1, 8):
            g_all = g_all + g_ref[d:d + 1, :]
        sum_ref[...] = g_all
        for k, (off, width) in enumerate(slots):
            w_ref, m_ref, v_ref = ins[3 * k:3 * k + 3]
            go_ref, d_ref, mo_ref, vo_ref = outs[4 * k:4 * k + 4]
            g = g_all[:, off:off + width]
            m = ADAM_B1 * m_ref[...] + (1.0 - ADAM_B1) * g
            v = ADAM_B2 * v_ref[...] + (1.0 - ADAM_B2) * (g * g)
            go_ref[...], mo_ref[...], vo_ref[...] = g, m, v
            d_ref[...] = -ADAM_LR * ((m / c1) / (jnp.sqrt(v / c2) + ADAM_EPS) + ADAM_WD * w_ref[...])

    flat = [a for p in params for a in p]
    out = pl.pallas_call(
        body, name="small_update",
        out_shape=[_sds((1, gathered.shape[1]), F32)] + [_sds(p[0].shape, F32) for p in params for _ in range(4)],
        compiler_params=pltpu.CompilerParams(vmem_limit_bytes=VMEM_LIMIT))(gathered, *flat)
    return out[0], [tuple(out[1 + 4 * k:5 + 4 * k]) for k in range(n)]


def _swiglu(ps, es):
    g, u = ps
    return g * _sigmoid(g) * u, g, u


def _swiglu_bwd(ps, es):
    g, u = es[0].astype(F32), es[1].astype(F32)
    sg = _sigmoid(g)
    return ps[0] * u * (sg * (1.0 + g * (1.0 - sg))), ps[0] * (g * sg)


def _merge(ps, es):
    ga, gm = [e.astype(F32) for e in es]
    return (_sigmoid(ga) * ps[0] + _sigmoid(gm) * ps[1],)


def _merge_bwd(ps, es):
    dm, a, b = ps
    ga, gm = [e.astype(F32) for e in es]
    sa, sm = _sigmoid(ga), _sigmoid(gm)
    return dm * sa, dm * sm, dm * a * (sa * (1.0 - sa)), dm * b * (sm * (1.0 - sm))


W_IN_PIECES = (("q", 512), ("kv", 256), ("mqk", 1024), ("mv", 512), ("mo", 512), ("if", 8),
               ("ga", 1024), ("gm", 1024))


def _local_step(x, tgt, pos_col, mod, sp, in_weights, late_weights, ffn_grads, mixer_grads):
    sh_m, sc_m, gate_m, sh_f, sc_f, gate_f = mod
    inv = ROPE_THETA ** (-2.0 * jnp.arange(HEAD_DIM // 2, dtype=F32) / HEAD_DIM)
    cos, sin = _rope_tables(pos_col, jnp.tile(inv, 4).reshape(1, 128))
    W = dict(in_weights(cos))
    h, pa, pqk, pvo, pif, pg = _proj_in(x, sp["g_pre_mix"], sc_m, sh_m, [
        (W["q+kv"], F32, 256), (W["mqk"], F32, 512), (W["mv+mo"], BF16, 512), (W["if"], F32, 128),
        (W["ga+gm"], BF16, 512)])
    ya = _attn_fwd(pa, cos, sin, sp["sinks"])
    qk = _conv_fwd(pqk, sp["conv_w"], sp["conv_b"])
    bcol = jnp.pad(sp["b_if"], ((0, 0), (0, 120)))
    brow = jnp.broadcast_to(sp["b_if"].reshape(8, 1), (8, 128))
    grow = pif[:, :8].T
    hm, ym, cs, ns, ms = _mlstm_fwd(qk, pvo, pif, bcol, grow, brow, sp["norm_w"])
    W.update(late_weights(ym))
    w_fg, w_fu, w_fd = W["fg"], W["fu"], W["fd"]
    merged, = _mm("branches", [[(ya, W["ba"])], [(ym, W["bm"])]], [(pg, 0), (pg, 1)], _merge, [BF16],
                  cn=512, nt=True)
    wide, narrow = (D_MODEL, F32), (D_MODEL, BF16)
    mix, x1, h2 = _mm_rows("mix_out", [[(merged, W["out"])]],
                           [x, gate_m, sp["g_post_mix"], sp["g_pre_ffn"], sc_f, sh_f],
                           _res_norm_rows, [wide, wide, narrow], [], cn=512)
    act, gt, up = _mm("ffn_in", [[(h2, w_fg)], [(h2, w_fu)]], [], _swiglu, [BF16] * 3,
                      cn=256, nt=True)
    dy, dff, acc_l, loss = _mm_rows("ffn_down", [[(act, w_fd)]], [x1, tgt, gate_f, sp["g_post_ffn"]],
                                    _final_loss_rows, [wide, narrow], [(8, D_MODEL), (1, 128)], cn=512)

    G = {}
    dgt, dup = _mm("ffn_down_bwd", [[(dff, w_fd)]], [gt, up], _swiglu_bwd, [BF16, BF16],
                   cn=256, nt=True)
    g_fd, = _mm_tn_group("dw_ffn_down", [act], dff, BF16)
    g_fg, = _mm_tn_group("dw_ffn_gate", [dgt], h2, BF16)
    g_fu, = _mm_tn_group("dw_ffn_up", [dup], h2, BF16)
    tie = ffn_grads(g_fg, g_fu, g_fd)
    dx1, dmix, acc_r = _mm_rows(
        "ffn_in_bwd", [[(dgt, w_fg), (dup, w_fu)]],
        [x1, mix, dy, sc_f + tie, gate_m, sp["g_pre_ffn"], sp["g_post_mix"]],
        _res_norm_bwd_rows, [wide, narrow], [(8, D_MODEL)], cn=512, tm=256)
    d_a, d_m, dga, dgm = _mm("mix_out_bwd", [[(dmix, W["out"])], [(ya, W["ba"])], [(ym, W["bm"])]],
                             [(pg, 0), (pg, 1)], _merge_bwd, [BF16] * 4, cn=512, nt=True)
    G["out"], = _mm_tn_group("dw_out", [merged], dmix, BF16)
    dya, = _mm("branch_attn_bwd", [[(d_a, W["ba"])]], [], _first, [F32], cn=512)
    heads = MLSTM_HEADS * MLSTM_HEAD_DIM
    dhm, do_m, acc_n = _mm_rows("branch_mlstm_bwd", [[(d_m, W["bm"])]], [hm, pvo, sp["norm_w"]],
                                _mlstm_out_bwd_rows, [(heads, F32), (heads, BF16)], [(8, heads)], cn=512)
    G["ba"], = _mm_tn_group("dw_branch_attn", [d_a], ya, BF16)
    G["bm"], = _mm_tn_group("dw_branch_mlstm", [d_m], ym, BF16)
    dqk, dv_m, dgc, dgr = _mlstm_bwd(qk, pvo, pif, bcol, grow, brow, cs, ns, ms, dhm)
    dif, acc_g = _gate_bwd(dgc, dgr, pif, bcol)
    du, acc_c = _conv_bwd(pqk, sp["conv_w"], sp["conv_b"], dqk)
    dq_a, dkv, dsink = _attn_bwd(pa, cos, sin, sp["sinks"], dya)
    dproj = {"q": dq_a, "kv": dkv, "mqk": du, "mv": dv_m, "mo": do_m, "if": dif, "ga": dga, "gm": dgm}
    names = [k for k, _ in W_IN_PIECES]
    for part in (names[:4], names[4:]):
        G.update(zip(part, _mm_tn_group("dw_in_from_" + part[0], [dproj[k] for k in part], h, BF16)))
    w_tied = dict(W, **{"if": W["if"] + mixer_grads(G).astype(BF16)})
    dx, acc_p = _mm_rows("proj_bwd", [[(dproj[k], w_tied[k]) for k, _ in W_IN_PIECES]],
                         [x, dx1, sp["g_pre_mix"], sc_m], _pre_norm_bwd_rows, [wide], [(8, D_MODEL)], cn=512)

    small = {
        "mod": jnp.concatenate([acc_p[1], acc_p[0], acc_r[3], acc_r[1], acc_r[0], acc_l[0]]),
        "g_pre_mix": acc_p[2], "g_post_mix": acc_r[4], "b_if": acc_g[0, :8],
        "conv_w": acc_c[:CONV_WIDTH].reshape(-1), "conv_b": acc_c[CONV_WIDTH],
        "sinks": dsink[:, 0], "norm_w": acc_n[0], "g_pre_ffn": acc_r[2], "g_post_ffn": acc_l[1]}
    return loss, dx, small


IN_WIDTH = sum(n for _, n in W_IN_PIECES)
IN_SHARD = IN_WIDTH // 4
IN_SHARD_PAD = -(-IN_SHARD // 32) * 32


def _split_w_in(w_in_t):
    out, off, start = {}, 0, {}
    for k, n in W_IN_PIECES:
        out[k], start[k] = w_in_t[off:off + n], off
        off += n
    out["if"] = jnp.pad(out["if"], ((0, 120), (0, 0)))
    for name, first, last in (("q+kv", "q", "kv"), ("mv+mo", "mv", "mo"), ("ga+gm", "ga", "gm")):
        out[name] = w_in_t[start[first]:start[last] + out[last].shape[0]]
    return out


def _halves(a):
    return a.reshape(4, 2, a.shape[0] // 8, a.shape[1])


SMALL = (("b_ada", 6144), ("g_pre_mix", 1024), ("g_post_mix", 1024), ("b_if", 128), ("conv_w", 4096),
         ("conv_b", 1024), ("sinks", 128), ("norm_w", 512), ("g_pre_ffn", 1024), ("g_post_ffn", 1024))
SMALL_LEN = 8 * 2048


def _pack_small(vals):
    parts = []
    for k, n in SMALL:
        v = vals[k].reshape(-1)
        parts.append(jnp.pad(v, (0, n - v.shape[0])))
    flat = jnp.concatenate(parts)
    return jnp.pad(flat, (0, SMALL_LEN - flat.shape[0]))


def kernel(x, c, positions, w_ada, b_ada, g_pre_mix, g_post_mix, w_in, b_if, conv_w, conv_b, attn_sinks, mlstm_norm_w, w_branch_attn, w_branch_mlstm, w_out, g_pre_ffn, g_post_ffn, w_ffn_gate, w_ffn_up, w_ffn_down, loss_target, m_w_ada, m_b_ada, m_g_pre_mix, m_g_post_mix, m_w_in, m_b_if, m_conv_w, m_conv_b, m_attn_sinks, m_mlstm_norm_w, m_w_branch_attn, m_w_branch_mlstm, m_w_out, m_g_pre_ffn, m_g_post_ffn, m_w_ffn_gate, m_w_ffn_up, m_w_ffn_down, v_w_ada, v_b_ada, v_g_pre_mix, v_g_post_mix, v_w_in, v_b_if, v_conv_w, v_conv_b, v_attn_sinks, v_mlstm_norm_w, v_w_branch_attn, v_w_branch_mlstm, v_w_out, v_g_pre_ffn, v_g_post_ffn, v_w_ffn_gate, v_w_ffn_up, v_w_ffn_down):
    xi, yi, ci = _place()
    chip = 2 * xi + yi
    dev = 2 * chip + ci
    T = x.shape[1]
    ada_cols = w_ada.shape[2]

    place = jnp.stack([chip, ci]).astype(jnp.int32)

    def my_half(a):
        n = a.shape[0] // 2
        return lax.dynamic_slice_in_dim(a, ci * n, n, axis=0).astype(BF16)

    blk = jnp.concatenate([c.reshape(-1), conv_w.reshape(-1)]).reshape(8, 256)
    got = _all_gather8("gather_cond", blk, pltpu.VMEM).reshape(8, 2048)
    c_all = got[:, :D_MODEL].astype(BF16)
    conv_full = got[::2, D_MODEL:].reshape(4, CONV_WIDTH, -1).transpose(1, 0, 2).reshape(CONV_WIDTH, -1)

    b_sh = lax.dynamic_slice_in_dim(b_ada, chip * ada_cols, ada_cols, axis=1)
    mod_part, = _mm("ada_mod", [[(c_all, w_ada[0].astype(BF16))]], [b_sh],
                    lambda ps, es: (ps[0] + es[0],), [F32], cn=512, tm=8)
    mod_all = _all_gather8("gather_mod", mod_part, pltpu.VMEM).reshape(4, 2, 8, ada_cols)[:, 0]
    mod = lax.dynamic_index_in_dim(mod_all, dev, axis=1, keepdims=False).reshape(6, 1, D_MODEL)

    def gather_start(name, blks, after, sends, copies):
        return _split_start(name + "_start", blks, _place_blocks(name + "_place", blks, place),
                            sends, copies, after)

    w_in_t = jnp.pad(w_in[0].T, ((0, IN_SHARD_PAD - IN_SHARD), (0, 0)))
    in_started = gather_start("in_gather", [my_half(w_in_t)], mod, _gather_sends, 4)
    late_keys = ("fg", "fu", "fd", "out", "ba", "bm")
    late_started = gather_start(
        "late_gather",
        [my_half(w_ffn_gate[0].T), my_half(w_ffn_up[0].T), my_half(w_ffn_down[0]), my_half(w_out[0]),
         my_half(w_branch_attn[0].T), my_half(w_branch_mlstm[0].T)], in_started[4], _gather_sends_all, 7)
    mod = mod + (in_started[4][0, 0] + late_started[4][0, 0])

    def in_weights(after):
        g_in, = _forward_sibling("in_gather_forward",
                                 _split_wait("in_gather_wait", in_started, after, _gather_lands, 4))
        return _split_w_in(g_in.reshape(4, IN_SHARD_PAD, D_MODEL)[:, :IN_SHARD].reshape(IN_WIDTH, D_MODEL))

    def late_weights(after):
        lands = _split_wait("late_gather_wait", late_started, after, _gather_lands_all, 7)
        return {k: a.reshape(-1, a.shape[-1]) for k, a in zip(late_keys, lands)}

    sent = {}

    def scatter_start(name, groups):
        pairs, lands = _pair_sum(name + "_pair_sum", groups, _swap_halves_sibling(name + "_pair", groups), place)
        sent[name] = _split_start(name + "_start", pairs, lands, _scatter_sends, 3, pairs[0])
        return sent[name][4][0, 0]

    def ffn_grads(g_fg, g_fu, g_fd):
        return scatter_start("rs_ffn", [_halves(g_fg), _halves(g_fu), _halves(g_fd)])

    def mixer_grads(G):
        g_in_t = jnp.concatenate([G[k][:n] for k, n in W_IN_PIECES]).reshape(4, IN_SHARD, D_MODEL)
        g_in_t = jnp.pad(g_in_t, ((0, 0), (0, IN_SHARD_PAD - IN_SHARD), (0, 0)))
        return scatter_start("rs_mix", [g_in_t.reshape(4, 2, IN_SHARD_PAD // 2, D_MODEL), _halves(G["out"]),
                                        _halves(G["ba"]), _halves(G["bm"])])

    sp = {"g_pre_mix": g_pre_mix, "g_post_mix": g_post_mix, "b_if": b_if, "conv_w": conv_full,
          "conv_b": conv_b, "sinks": attn_sinks, "norm_w": mlstm_norm_w, "g_pre_ffn": g_pre_ffn,
          "g_post_ffn": g_post_ffn}
    loss, dx, small = _local_step(x[0], loss_target[0], positions.reshape(T, 1), [mod[i] for i in range(6)],
                                  sp, in_weights, late_weights, ffn_grads, mixer_grads)

    reds = (_sum4("rs_ffn_chip_sum", _split_wait("rs_ffn_wait", sent["rs_ffn"], dx, _scatter_lands, 3), place)
            + _sum4("rs_mix_chip_sum", _split_wait("rs_mix_wait", sent["rs_mix"], dx, _scatter_lands, 3), place))
    gsh = {k: s.reshape(-1, s.shape[-1])
           for k, s in zip(("fg", "fu", "fd", "w_in", "out", "ba", "bm"), _share_halves("rs_share", reds))}
    gsh["w_in"] = gsh["w_in"][:IN_SHARD]

    small["b_ada"] = small.pop("mod")
    vec = _pack_small(small).reshape(8, 2048)
    g_all = _all_gather8("gather_small", vec, pltpu.VMEM).reshape(8, SMALL_LEN)
    dmod_sh = lax.dynamic_slice_in_dim(g_all[:, :6 * D_MODEL], chip * ada_cols, ada_cols, axis=1)
    g_w_ada, = _mm_tn_group("dw_ada", [c_all], dmod_sh.astype(BF16), F32)

    smalls = {"b_ada": (b_ada, m_b_ada, v_b_ada), "g_pre_mix": (g_pre_mix, m_g_pre_mix, v_g_pre_mix),
              "g_post_mix": (g_post_mix, m_g_post_mix, v_g_post_mix), "b_if": (b_if, m_b_if, v_b_if),
              "conv_b": (conv_b, m_conv_b, v_conv_b), "sinks": (attn_sinks, m_attn_sinks, v_attn_sinks),
              "norm_w": (mlstm_norm_w, m_mlstm_norm_w, v_mlstm_norm_w),
              "g_pre_ffn": (g_pre_ffn, m_g_pre_ffn, v_g_pre_ffn),
              "g_post_ffn": (g_post_ffn, m_g_post_ffn, v_g_post_ffn)}
    offsets, off = {}, 0
    for k, width in SMALL:
        offsets[k], off = off, off + width
    g_sum, updates = _small_update(g_all, list(smalls.values()),
                                   [(offsets[k], t[0].shape[1]) for k, t in smalls.items()])
    g_conv = g_sum[:, offsets["conv_w"]:offsets["conv_w"] + CONV_WIDTH * D_MODEL].reshape(1, CONV_WIDTH, D_MODEL)
    g_conv = lax.dynamic_slice_in_dim(g_conv, chip * conv_w.shape[2], conv_w.shape[2], axis=2)

    res = dict(zip(smalls, updates))
    res["conv_w"] = (g_conv, *[o[None] for o in _adamw("adam_conv_w", conv_w[0], g_conv[0], m_conv_w[0], v_conv_w[0])])
    res["w_ada"] = (g_w_ada[None], *[o[None] for o in _adamw("adam_w_ada", w_ada[0], g_w_ada, m_w_ada[0], v_w_ada[0])])
    bigs = {"w_in": (w_in, m_w_in, v_w_in), "ba": (w_branch_attn, m_w_branch_attn, v_w_branch_attn),
            "bm": (w_branch_mlstm, m_w_branch_mlstm, v_w_branch_mlstm), "out": (w_out, m_w_out, v_w_out),
            "fg": (w_ffn_gate, m_w_ffn_gate, v_w_ffn_gate), "fu": (w_ffn_up, m_w_ffn_up, v_w_ffn_up),
            "fd": (w_ffn_down, m_w_ffn_down, v_w_ffn_down)}
    for k, (w, m, v) in bigs.items():
        if k in ("w_in", "fg", "fu"):
            res[k] = tuple(o.T[None] for o in (gsh[k], *_adamw("adam_" + k, w[0].T, gsh[k], m[0].T, v[0].T)))
        else:
            g = gsh[k].T if k in ("ba", "bm") else gsh[k]
            res[k] = (g[None], *[o[None] for o in _adamw("adam_" + k, w[0], g, m[0], v[0])])

    order = ("w_ada", "b_ada", "g_pre_mix", "g_post_mix", "w_in", "b_if", "conv_w", "conv_b", "sinks",
             "norm_w", "ba", "bm", "out", "g_pre_ffn", "g_post_ffn", "fg", "fu", "fd")
    total = lax.psum(loss[0, 0], ("x", "y", "c"))
    return (total, dx[None], *[res[k][0] for k in order], *[res[k][1] for k in order],
            *[res[k][2] for k in order], *[res[k][3] for k in order])
```

```python
import functools

import jax
import jax.numpy as jnp
from jax import lax
from jax.experimental import pallas as pl
from jax.experimental.pallas import tpu as pltpu

F32, BF16 = jnp.float32, jnp.bfloat16
MESH = pl.DeviceIdType.MESH

D_MODEL = 1024
N_Q_HEADS, N_KV_HEADS, HEAD_DIM, WINDOW = 8, 2, 64, 128
ROPE_THETA = 10000.0
MLSTM_HEADS, MLSTM_HEAD_DIM, MLSTM_CHUNK, CONV_WIDTH = 4, 128, 64, 4
D_FF = 2816
NORM_EPS = 1e-6
ADAM_LR, ADAM_B1, ADAM_B2, ADAM_EPS, ADAM_WD, ADAM_STEP = 0.001, 0.9, 0.999, 1e-08, 0.01, 10

VMEM_LIMIT = 56 * 1024 * 1024
ROW_TILE = 256
MM_TM = 512
MM_TT = 1024
ATTN_BLK = WINDOW
STEP_ROWS = 2 * MLSTM_CHUNK
NEG_INF = float("-inf")


def _params(sem):
    return pltpu.CompilerParams(dimension_semantics=sem, vmem_limit_bytes=VMEM_LIMIT)


def _sds(shape, dtype):
    return jax.ShapeDtypeStruct(shape, dtype)


def _sigmoid(x):
    return 1.0 / (1.0 + jnp.exp(-x))


def _dot(a, b, ca, cb):
    return lax.dot_general(a, b, (((ca,), (cb,)), ((), ())), preferred_element_type=F32)


def _bdot(a, b, ca, cb):
    return lax.dot_general(a, b, (((ca,), (cb,)), ((0,), (0,))), preferred_element_type=F32)


def _bdot_rows(a, b):
    return jnp.stack([_dot(a[h], b[h], 0, 0) for h in range(a.shape[0])])


def _mm(name, prods, extras, epi, out_dtypes, cn, nt=False, tm=MM_TM):
    flat = [ab for p in prods for ab in p]
    counts = [len(p) for p in prods]
    M = flat[0][0].shape[0]
    N = flat[0][1].shape[0 if nt else 1]
    tm = min(tm, M)
    n_in = 2 * len(flat) + len(extras)

    def body(*refs):
        ins, outs = refs[:n_in], refs[n_in:]
        for j in range(N // cn):
            cols = slice(j * cn, (j + 1) * cn)
            k, ps = 0, []
            for cnt in counts:
                acc = None
                for _ in range(cnt):
                    b = ins[k + 1][cols, :] if nt else ins[k + 1][:, cols]
                    d = _dot(ins[k][...], b, 1, 1 if nt else 0)
                    acc = d if acc is None else acc + d
                    k += 2
                ps.append(acc)
            res = epi(ps, [r[:, cols] for r in ins[k:]])
            for o, r in zip(outs, res):
                o[:, cols] = r.astype(o.dtype)

    in_specs, args = [], []
    for a, b in flat:
        in_specs.append(pl.BlockSpec((tm, a.shape[1]), lambda i: (i, 0)))
        in_specs.append(pl.BlockSpec(b.shape, lambda i: (0, 0), pipeline_mode=pl.Buffered(1)))
        args += [a, b]
    for e in extras:
        e, off = e if isinstance(e, tuple) else (e, 0)
        rows = 1 if e.shape[0] == 1 else tm
        in_specs.append(pl.BlockSpec((rows, N), lambda i, off=off, rows=rows: (0 if rows == 1 else i, off)))
        args.append(e)
    return pl.pallas_call(
        body, name=name, grid=(M // tm,), in_specs=in_specs,
        out_specs=[pl.BlockSpec((tm, N), lambda i: (i, 0)) for _ in out_dtypes],
        out_shape=[_sds((M, N), dt) for dt in out_dtypes],
        compiler_params=_params(("parallel",)))(*args)


def _mm_rows(name, prods, extras, epi, outs, accs, cn, nt=False, tm=MM_TM):
    flat = [ab for p in prods for ab in p]
    counts = [len(p) for p in prods]
    M = flat[0][0].shape[0]
    N = flat[0][1].shape[0 if nt else 1]
    tm = min(tm, M)
    n_mm, n_in, n_out = 2 * len(flat), 2 * len(flat) + len(extras), len(outs)

    def body(*refs):
        ins, out_refs, acc_refs = refs[:n_in], refs[n_in:n_in + n_out], refs[n_in + n_out:]

        @pl.when(pl.program_id(0) == 0)
        def _():
            for a in acc_refs:
                a[...] = jnp.zeros_like(a)

        chunks = [[] for _ in counts]
        for j in range(N // cn):
            cols = slice(j * cn, (j + 1) * cn)
            k = 0
            for p, cnt in enumerate(counts):
                acc = None
                for _ in range(cnt):
                    b = ins[k + 1][cols, :] if nt else ins[k + 1][:, cols]
                    d = _dot(ins[k][...], b, 1, 1 if nt else 0)
                    acc = d if acc is None else acc + d
                    k += 2
                chunks[p].append(acc)
        ps = [c[0] if len(c) == 1 else jnp.concatenate(c, axis=1) for c in chunks]
        res, incs = epi(ps, [r[...] for r in ins[n_mm:]])
        for o, r in zip(out_refs, res):
            o[...] = r.astype(o.dtype)
        for a, inc in zip(acc_refs, incs):
            a[...] += inc

    in_specs, args = [], []
    for a, b in flat:
        in_specs.append(pl.BlockSpec((tm, a.shape[1]), lambda i: (i, 0)))
        in_specs.append(pl.BlockSpec(b.shape, lambda i: (0, 0), pipeline_mode=pl.Buffered(1)))
        args += [a, b]
    for e in extras:
        rows = 1 if e.shape[0] == 1 else tm
        in_specs.append(pl.BlockSpec((rows, e.shape[1]), lambda i, rows=rows: (0 if rows == 1 else i, 0)))
        args.append(e)
    return pl.pallas_call(
        body, name=name, grid=(M // tm,), in_specs=in_specs,
        out_specs=[pl.BlockSpec((tm, w), lambda i: (i, 0)) for w, _ in outs]
        + [pl.BlockSpec(s, lambda i: (0, 0)) for s in accs],
        out_shape=[_sds((M, w), dt) for w, dt in outs] + [_sds(s, F32) for s in accs],
        compiler_params=_params(("arbitrary",)))(*args)


def _mm_tn_group(name, pieces, b, out_dtype, tt=MM_TT):
    T, N = b.shape
    tt = min(tt, T)
    steps, n = T // tt, len(pieces)

    def body(*refs):
        a_refs, b_ref, out_refs, accs = refs[:n], refs[n], refs[n + 1:2 * n + 1], refs[2 * n + 1:]
        t = pl.program_id(0)

        @pl.when(t == 0)
        def _():
            for acc in accs:
                acc[...] = jnp.zeros_like(acc)

        for a_ref, acc in zip(a_refs, accs):
            acc[...] += _dot(a_ref[...], b_ref[...], 0, 0)

        @pl.when(t == steps - 1)
        def _():
            for o_ref, acc in zip(out_refs, accs):
                o_ref[...] = acc[...].astype(o_ref.dtype)

    return pl.pallas_call(
        body, name=name, grid=(steps,),
        in_specs=[pl.BlockSpec((tt, a.shape[1]), lambda t: (t, 0)) for a in pieces]
        + [pl.BlockSpec((tt, N), lambda t: (t, 0))],
        out_specs=[pl.BlockSpec((a.shape[1], N), lambda t: (0, 0)) for a in pieces],
        out_shape=[_sds((a.shape[1], N), out_dtype) for a in pieces],
        scratch_shapes=[pltpu.VMEM((a.shape[1], N), F32) for a in pieces],
        compiler_params=_params(("arbitrary",)))(*pieces, b)


def _first(ps, es):
    return (ps[0],)


def _rows(name, body, ins, out_shapes, T, tr=ROW_TILE):
    tr = min(tr, T)

    def spec(shape):
        if shape[0] == T:
            return pl.BlockSpec((tr,) + tuple(shape[1:]), lambda i: (i,) + (0,) * (len(shape) - 1))
        return pl.BlockSpec(tuple(shape), lambda i: (0,) * len(shape))

    return pl.pallas_call(
        body, name=name, grid=(T // tr,),
        in_specs=[spec(a.shape) for a in ins], out_specs=[spec(s.shape) for s in out_shapes],
        out_shape=out_shapes, compiler_params=_params(("arbitrary",)))(*ins)


def _rms(x):
    r = lax.rsqrt(jnp.mean(x * x, axis=-1, keepdims=True) + NORM_EPS)
    return x * r, r


def _rms_bwd(dxn, xn, r):
    return r * (dxn - xn * jnp.mean(dxn * xn, axis=-1, keepdims=True))


def _colsum(v):
    return jnp.sum(v, axis=0, keepdims=True)


def _proj_in(x, g, sc, sh, groups):
    T = x.shape[0]
    tm = min(MM_TM, T)
    ng = len(groups)

    def body(x_ref, g_ref, sc_ref, sh_ref, *rest):
        w_refs, h_ref, out_refs = rest[:ng], rest[ng], rest[ng + 1:]
        xn, _ = _rms(x_ref[...])
        h = (xn * g_ref[...] * (1.0 + sc_ref[...]) + sh_ref[...]).astype(BF16)
        h_ref[...] = h
        for w_ref, o_ref, (w, _, cn) in zip(w_refs, out_refs, groups):
            for j in range(w.shape[0] // cn):
                cols = slice(j * cn, (j + 1) * cn)
                o_ref[:, cols] = _dot(h, w_ref[cols, :], 1, 1).astype(o_ref.dtype)

    row = pl.BlockSpec((1, D_MODEL), lambda i: (0, 0))
    tile = lambda w: pl.BlockSpec((tm, w), lambda i: (i, 0))
    return pl.pallas_call(
        body, name="proj_in", grid=(T // tm,),
        in_specs=[tile(D_MODEL), row, row, row] + [
            pl.BlockSpec(w.shape, lambda i: (0, 0), pipeline_mode=pl.Buffered(1)) for w, _, _ in groups],
        out_specs=[tile(D_MODEL)] + [tile(w.shape[0]) for w, _, _ in groups],
        out_shape=[_sds((T, D_MODEL), BF16)] + [_sds((T, w.shape[0]), dt) for w, dt, _ in groups],
        compiler_params=_params(("parallel",)))(x, g, sc, sh, *[w for w, _, _ in groups])


def _acc_rows(rows):
    w = rows[0].shape[1]
    return jnp.concatenate(rows + [jnp.zeros((8 - len(rows), w), F32)], axis=0)


def _res_norm_rows(ps, es):
    mix = ps[0]
    x, gate, gp, g2, sc, sh = es
    mh, _ = _rms(mix)
    x1 = x + gate * (mh * gp)
    xn, _ = _rms(x1)
    return [mix, x1, xn * g2 * (1.0 + sc) + sh], []


def _final_loss_rows(ps, es):
    x1, tgt, gate, gp = es
    fh, r = _rms(ps[0])
    e = x1 + gate * (fh * gp) - tgt
    loss = 0.5 * jnp.sum(jnp.mean(e * e, axis=-1, keepdims=True))
    dy = e * (1.0 / D_MODEL)
    acc = _acc_rows([_colsum(dy * fh * gp), _colsum(dy * gate * fh)])
    return [dy, _rms_bwd(dy * gate * gp, fh, r)], [acc, jnp.full((1, 128), loss, F32)]


def _res_norm_bwd_rows(ps, es):
    dh = ps[0]
    x1, mix, dy, sc, gate, g2, gp = es
    xn, r1 = _rms(x1)
    rows = [_colsum(dh * xn * g2), _colsum(dh), _colsum(dh * (1.0 + sc) * xn)]
    dx1 = dy + _rms_bwd(dh * (1.0 + sc) * g2, xn, r1)
    mh, rm = _rms(mix)
    rows += [_colsum(dx1 * mh * gp), _colsum(dx1 * gate * mh)]
    return [dx1, _rms_bwd(dx1 * gate * gp, mh, rm)], [_acc_rows(rows)]


def _pre_norm_bwd_rows(ps, es):
    dh = ps[0]
    x, dx1, g, sc = es
    xn, r = _rms(x)
    rows = [_colsum(dh * xn * g), _colsum(dh), _colsum(dh * (1.0 + sc) * xn)]
    return [dx1 + _rms_bwd(dh * (1.0 + sc) * g, xn, r)], [_acc_rows(rows)]


def _rope_tables(pos_col, inv_freq):
    T = pos_col.shape[0]

    def body(p_ref, f_ref, c_ref, s_ref):
        ang = p_ref[...].astype(F32) * f_ref[...]
        lane = lax.broadcasted_iota(jnp.int32, ang.shape, 1)
        c_ref[...] = jnp.cos(ang)
        s_ref[...] = jnp.where(lane % HEAD_DIM < HEAD_DIM // 2, -1.0, 1.0) * jnp.sin(ang)

    return _rows("rope_tables", body, [pos_col, inv_freq],
                 [_sds((T, 128), F32), _sds((T, 128), F32)], T, tr=512)


def _swap_halves(t):
    W = t.shape[1]
    lane = lax.broadcasted_iota(jnp.int32, t.shape, 1)
    half = HEAD_DIM // 2
    return jnp.where(lane % HEAD_DIM < half, pltpu.roll(t, W - half, 1), pltpu.roll(t, half, 1))


def _widen(c, W):
    return c if W == 128 else jnp.concatenate([c] * (W // 128), axis=1)


def _rope(t, c, s):
    W = t.shape[1]
    return t * _widen(c, W) + _swap_halves(t) * _widen(s, W)


def _unrope(dy, c, s):
    W = dy.shape[1]
    return dy * _widen(c, W) + _swap_halves(dy * _widen(s, W))


def _attn_mask(n):
    qi = lax.broadcasted_iota(jnp.int32, (ATTN_BLK, 2 * ATTN_BLK), 0)
    kj = lax.broadcasted_iota(jnp.int32, (ATTN_BLK, 2 * ATTN_BLK), 1)
    rel = kj - ATTN_BLK
    return (rel <= qi) & (qi - rel < WINDOW) & ((n > 0) | (kj >= ATTN_BLK))


def _attn_load(cur, prv, cc, sc, cp, sp):
    x, xp = cur[...], prv[...]
    q = _rope(x[:, :512], cc[...], sc[...]) * (HEAD_DIM ** -0.5)
    k = jnp.concatenate([_rope(xp[:, 512:640], cp[...], sp[...]),
                         _rope(x[:, 512:640], cc[...], sc[...])], axis=0)
    v = jnp.concatenate([xp[:, 640:768], x[:, 640:768]], axis=0)
    return q, k, v


ROLLED = tuple(h for h in range(N_Q_HEADS) if h % 2 != h // (N_Q_HEADS // N_KV_HEADS))


def _pair_heads(t):
    half = lax.broadcasted_iota(jnp.int32, (ATTN_BLK, 128), 1) // HEAD_DIM
    return jnp.stack([jnp.where(half == h % 2, t[:, 128 * (h // 2):128 * (h // 2) + 128], 0.0)
                      for h in range(N_Q_HEADS)])


def _kv_heads(t):
    half = lax.broadcasted_iota(jnp.int32, t.shape, 1) // HEAD_DIM
    tr = pltpu.roll(t, HEAD_DIM, 1)
    return jnp.stack([jnp.where(half == h % 2, tr if h in ROLLED else t, 0.0)
                      for h in range(N_Q_HEADS)])


def _sink_column(snk):
    return jnp.stack([jnp.full((1, 1), snk[0, h], F32) for h in range(N_Q_HEADS)])


def _attn_probs(qh, kh, mask, sink):
    s = jnp.where(mask, _bdot(qh, kh, 2, 2), NEG_INF)
    m = jnp.maximum(jnp.max(s, axis=-1, keepdims=True), sink)
    p = jnp.exp(s - m)
    es = jnp.exp(sink - m)
    rl = 1.0 / (jnp.sum(p, axis=-1, keepdims=True) + es)
    return p, es, rl


def _attn_specs(order):
    blk = lambda w: pl.BlockSpec((ATTN_BLK, w), lambda s: (order(s), 0))
    prv = lambda w: pl.BlockSpec((ATTN_BLK, w), lambda s: (jnp.maximum(order(s) - 1, 0), 0))
    return [blk(768), prv(768), blk(128), blk(128), prv(128), prv(128),
            pl.BlockSpec(memory_space=pltpu.SMEM)]


def _attn_fwd(pa, cos, sin, sinks):
    T = pa.shape[0]
    nb = T // ATTN_BLK

    def body(cur, prv, cc, sc, cp, sp, snk, y_ref):
        n = pl.program_id(0)
        q, k, v = _attn_load(cur, prv, cc, sc, cp, sp)
        mask = _attn_mask(n)
        half_q = lax.broadcasted_iota(jnp.int32, (ATTN_BLK, 128), 1) // HEAD_DIM
        half_k = lax.broadcasted_iota(jnp.int32, k.shape, 1) // HEAD_DIM
        moved = (pltpu.roll(k, HEAD_DIM, 1), pltpu.roll(v, HEAD_DIM, 1))
        for pair in range(N_Q_HEADS // 2):
            o = None
            for a in range(2):
                h = 2 * pair + a
                ku, vu = moved if h in ROLLED else (k, v)
                qh = jnp.where(half_q == a, q[:, 128 * pair:128 * pair + 128], 0.0).astype(BF16)
                kh = jnp.where(half_k == a, ku, 0.0).astype(BF16)
                vh = jnp.where(half_k == a, vu, 0.0).astype(BF16)
                s = jnp.where(mask, _dot(qh, kh, 1, 1), NEG_INF)
                m = jnp.maximum(jnp.max(s, axis=-1, keepdims=True), snk[0, h])
                p = jnp.exp(s - m)
                rl = 1.0 / (jnp.sum(p, axis=-1, keepdims=True) + jnp.exp(snk[0, h] - m))
                oh = _dot(p.astype(BF16), vh, 1, 0) * rl
                o = oh if o is None else o + oh
            y_ref[:, 128 * pair:128 * pair + 128] = o.astype(BF16)

    return pl.pallas_call(
        body, name="attn_fwd", grid=(nb,), in_specs=_attn_specs(lambda s: s),
        out_specs=pl.BlockSpec((ATTN_BLK, 512), lambda n: (n, 0)),
        out_shape=_sds((T, 512), BF16), compiler_params=_params(("parallel",)))(
            pa, pa, cos, sin, cos, sin, sinks)


def _attn_bwd(pa, cos, sin, sinks, dy):
    T = pa.shape[0]
    nb = T // ATTN_BLK
    rev = lambda s: nb - 1 - s

    def body(cur, prv, cc, sc, cp, sp, snk, dy_ref, dq_ref, dkv_ref, dsink_ref, carry):
        n = rev(pl.program_id(0))

        @pl.when(pl.program_id(0) == 0)
        def _():
            dsink_ref[...] = jnp.zeros_like(dsink_ref)
            carry[...] = jnp.zeros_like(carry)

        q, k, v = _attn_load(cur, prv, cc, sc, cp, sp)
        qh, kh, vh = _pair_heads(q).astype(BF16), _kv_heads(k).astype(BF16), _kv_heads(v).astype(BF16)
        p, es, rl = _attn_probs(qh, kh, _attn_mask(n), _sink_column(snk))
        pn = p * rl
        do = _pair_heads(dy_ref[...]).astype(BF16)
        dp = _bdot(do, vh, 2, 2)
        delta = jnp.sum(pn * dp, axis=-1, keepdims=True)
        ds = (pn * (dp - delta)).astype(BF16)
        dsink = es * rl * delta
        dq = _bdot(ds, kh, 2, 1) * (HEAD_DIM ** -0.5)
        dkh = _bdot_rows(ds, qh)
        dvh = _bdot_rows(pn.astype(BF16), do)

        def fold(t):
            same = [t[h] for h in range(N_Q_HEADS) if h not in ROLLED]
            moved = [t[h] for h in ROLLED]
            return sum(same[1:], same[0]) + pltpu.roll(sum(moved[1:], moved[0]), HEAD_DIM, 1)

        dk, dv = fold(dkh), fold(dvh)
        for h in range(N_Q_HEADS):
            dsink_ref[h:h + 1, :] += -jnp.sum(dsink[h])
        for pair in range(N_Q_HEADS // 2):
            dq_ref[:, 128 * pair:128 * pair + 128] = _unrope(
                dq[2 * pair] + dq[2 * pair + 1], cc[...], sc[...]).astype(BF16)
        dkv_ref[:, 0:128] = _unrope(dk[ATTN_BLK:] + carry[:, 0:128], cc[...], sc[...]).astype(BF16)
        dkv_ref[:, 128:256] = (dv[ATTN_BLK:] + carry[:, 128:256]).astype(BF16)
        carry[:, 0:128] = dk[:ATTN_BLK]
        carry[:, 128:256] = dv[:ATTN_BLK]

    blk = lambda w: pl.BlockSpec((ATTN_BLK, w), lambda s: (rev(s), 0))
    return pl.pallas_call(
        body, name="attn_bwd", grid=(nb,), in_specs=_attn_specs(rev) + [blk(512)],
        out_specs=[blk(512), blk(256), pl.BlockSpec((8, 128), lambda s: (0, 0))],
        out_shape=[_sds((T, 512), BF16), _sds((T, 256), BF16), _sds((8, 128), F32)],
        scratch_shapes=[pltpu.VMEM((ATTN_BLK, 256), F32)],
        compiler_params=_params(("arbitrary",)))(pa, pa, cos, sin, cos, sin, sinks, dy)


CONV_COLS = 2 * MLSTM_HEADS * MLSTM_HEAD_DIM


def _conv_pre(cur_ref, halo_ref, w_ref, b_ref, i, tr):
    xx = jnp.concatenate([jnp.where(i > 0, halo_ref[...], 0.0), cur_ref[...]], axis=0)
    taps = [(pltpu.roll(xx, CONV_WIDTH - 1 - j, 0) if j < CONV_WIDTH - 1 else xx)[8:8 + tr]
            for j in range(CONV_WIDTH)]
    pre = b_ref[...]
    for j in range(CONV_WIDTH):
        pre = pre + taps[j] * w_ref[j:j + 1, :]
    return pre, taps


def _conv_specs(T, tr):
    return [pl.BlockSpec((tr, CONV_COLS), lambda i: (i, 0)),
            pl.BlockSpec((8, CONV_COLS), lambda i: (jnp.maximum(i * (tr // 8) - 1, 0), 0)),
            pl.BlockSpec((CONV_WIDTH, CONV_COLS), lambda i: (0, 0)),
            pl.BlockSpec((1, CONV_COLS), lambda i: (0, 0))]


def _conv_fwd(pm, w, b):
    T = pm.shape[0]
    tr = min(ROW_TILE, T)

    def body(cur_ref, halo_ref, w_ref, b_ref, o_ref):
        pre, _ = _conv_pre(cur_ref, halo_ref, w_ref, b_ref, pl.program_id(0), tr)
        o_ref[...] = pre * _sigmoid(pre)

    return pl.pallas_call(
        body, name="conv_fwd", grid=(T // tr,), in_specs=_conv_specs(T, tr),
        out_specs=pl.BlockSpec((tr, CONV_COLS), lambda i: (i, 0)),
        out_shape=_sds((T, CONV_COLS), F32), compiler_params=_params(("parallel",)))(pm, pm, w, b)


def _conv_bwd(pqk, w, b, dqk):
    T = pqk.shape[0]
    tr = min(ROW_TILE, T)
    nt = T // tr

    def body(cur_ref, prev_ref, next_ref, w_ref, b_ref, d_ref, dnext_ref, du_ref, acc_ref):
        i = pl.program_id(0)

        @pl.when(i == 0)
        def _():
            acc_ref[...] = jnp.zeros_like(acc_ref)

        last = i == nt - 1
        xx = jnp.concatenate([jnp.where(i > 0, prev_ref[...], 0.0), cur_ref[...],
                              jnp.where(last, 0.0, next_ref[...])], axis=0)
        taps = [(pltpu.roll(xx, CONV_WIDTH - 1 - j, 0) if j < CONV_WIDTH - 1 else xx)[8:16 + tr]
                for j in range(CONV_WIDTH)]
        pre = b_ref[...]
        for j in range(CONV_WIDTH):
            pre = pre + taps[j] * w_ref[j:j + 1, :]
        sg = _sigmoid(pre)
        dd = jnp.concatenate([d_ref[...], jnp.where(last, 0.0, dnext_ref[...])], axis=0)
        dpre = dd * (sg * (1.0 + pre * (1.0 - sg)))
        for j in range(CONV_WIDTH):
            acc_ref[j:j + 1, :] += _colsum(dpre[:tr] * taps[j][:tr])
        acc_ref[CONV_WIDTH:CONV_WIDTH + 1, :] += _colsum(dpre[:tr])
        du = dpre[:tr] * w_ref[CONV_WIDTH - 1:CONV_WIDTH, :]
        for j in range(CONV_WIDTH - 1):
            k = CONV_WIDTH - 1 - j
            du = du + pltpu.roll(dpre, tr + 8 - k, 0)[:tr] * w_ref[j:j + 1, :]
        du_ref[...] = du.astype(BF16)

    tile = pl.BlockSpec((tr, CONV_COLS), lambda i: (i, 0))
    after = pl.BlockSpec((8, CONV_COLS), lambda i: (jnp.minimum((i + 1) * (tr // 8), T // 8 - 1), 0))
    before = pl.BlockSpec((8, CONV_COLS), lambda i: (jnp.maximum(i * (tr // 8) - 1, 0), 0))
    return pl.pallas_call(
        body, name="conv_bwd", grid=(nt,),
        in_specs=[tile, before, after, pl.BlockSpec((CONV_WIDTH, CONV_COLS), lambda i: (0, 0)),
                  pl.BlockSpec((1, CONV_COLS), lambda i: (0, 0)), tile, after],
        out_specs=[tile, pl.BlockSpec((8, CONV_COLS), lambda i: (0, 0))],
        out_shape=[_sds((T, CONV_COLS), BF16), _sds((8, CONV_COLS), F32)],
        compiler_params=_params(("arbitrary",)))(pqk, pqk, pqk, w, b, dqk, dqk)


def _log_sigmoid(x):
    return jnp.minimum(x, 0.0) - jnp.log1p(jnp.exp(-jnp.abs(x)))


def _chunk_cumsum(x, axis):
    idx = lax.broadcasted_iota(jnp.int32, x.shape, axis) % MLSTM_CHUNK
    k = 1
    while k < MLSTM_CHUNK:
        x = x + jnp.where(idx >= k, pltpu.roll(x, k, axis), 0.0)
        k *= 2
    return x


def _chunk_rev_cumsum(x, axis):
    n = x.shape[axis]
    idx = lax.broadcasted_iota(jnp.int32, x.shape, axis) % MLSTM_CHUNK
    k = 1
    while k < MLSTM_CHUNK:
        x = x + jnp.where(idx < MLSTM_CHUNK - k, pltpu.roll(x, n - k, axis), 0.0)
        k *= 2
    return x


def _mlstm_gates(gc_ref, bc_ref, gr_ref, br_ref):
    gc = gc_ref[...] + bc_ref[...]
    gr = gr_ref[...] + br_ref[...]
    return gc, _chunk_cumsum(_log_sigmoid(gc), 0), gr, _chunk_cumsum(_log_sigmoid(gr), 1)


def _heads(ref, base=0):
    D = MLSTM_HEAD_DIM
    return jnp.stack([ref[:, base + D * h:base + D * h + D] for h in range(MLSTM_HEADS)])


def _mlstm_inputs(q_ref, k_ref, v_ref, gc, bc, gr, br):
    H = MLSTM_HEADS
    q, v = _heads(q_ref), _heads(v_ref)
    ks = _heads(k_ref) * (MLSTM_HEAD_DIM ** -0.5)
    return dict(
        q=q, ks=ks, qb=q.astype(BF16), kb=ks.astype(BF16), vb=v.astype(BF16),
        b_col=jnp.stack([bc[:, H + h:H + h + 1] for h in range(H)]),
        i_col=jnp.stack([gc[:, h:h + 1] for h in range(H)]),
        b_row=jnp.stack([br[H + h:H + h + 1, :] for h in range(H)]),
        i_row=jnp.stack([gr[h:h + 1, :] for h in range(H)]))


def _mlstm_head(f, c_prev, n_prev, m_prev):
    L = MLSTM_CHUNK
    q, qb = f["q"], f["qb"]
    t = lax.broadcasted_iota(jnp.int32, (1, 2 * L, 2 * L), 1)
    s = lax.broadcasted_iota(jnp.int32, (1, 2 * L, 2 * L), 2)
    mask = (t // L == s // L) & (s <= t)
    d = jnp.where(mask, f["b_col"] - f["b_row"] + f["i_row"], NEG_INF)
    row = lax.broadcasted_iota(jnp.int32, (1, 2 * L, 1), 1)
    inter = f["b_col"] + jnp.where(row < L, m_prev[0], m_prev[1])
    m_t = jnp.maximum(inter, jnp.max(d, axis=-1, keepdims=True))
    w_intra = jnp.exp(d - m_t)
    w_inter = jnp.exp(inter - m_t)
    sc = _bdot(qb, f["kb"], 2, 2) * w_intra
    qc = jnp.concatenate([_bdot(qb[:, :L], c_prev[0].astype(BF16), 2, 1),
                          _bdot(qb[:, L:], c_prev[1].astype(BF16), 2, 1)], axis=1)
    qn = jnp.concatenate([jnp.sum(q[:, :L] * n_prev[0], axis=-1, keepdims=True),
                          jnp.sum(q[:, L:] * n_prev[1], axis=-1, keepdims=True)], axis=1)
    num = _bdot(sc.astype(BF16), f["vb"], 2, 1) + w_inter * qc
    den = jnp.sum(sc, axis=-1, keepdims=True) + w_inter * qn
    return dict(f, w_intra=w_intra, w_inter=w_inter, sc=sc, qc=qc, qn=qn, num=num, den=den,
                floor=jnp.exp(-m_t))


def _mlstm_update(f, ch, c, n, m):
    L = MLSTM_CHUNK
    rows = slice(L * ch, L * ch + L)
    b_col = f["b_col"][:, rows]
    g_last = b_col[:, L - 1:L]
    a_col = g_last - b_col + f["i_col"][:, rows]
    m_new = jnp.maximum(g_last + m, jnp.max(a_col, axis=1, keepdims=True))
    decay = jnp.exp(g_last + m - m_new)
    e_a = jnp.exp(a_col - m_new)
    kw = f["ks"][:, rows] * e_a
    c_new = decay * c + _bdot_rows(kw.astype(BF16), f["vb"][:, rows])
    n_new = decay * n + jnp.sum(kw, axis=1, keepdims=True)
    return c_new, n_new, m_new, decay, e_a, kw


def _mlstm_specs(T, order):
    blk = lambda w, col: pl.BlockSpec((STEP_ROWS, w), lambda s: (order(s), col))
    return [blk(512, 0), blk(512, 1), blk(512, 0), blk(128, 0),
            pl.BlockSpec((1, 128), lambda s: (0, 0)),
            pl.BlockSpec((8, STEP_ROWS), lambda s: (0, order(s))),
            pl.BlockSpec((8, 128), lambda s: (0, 0))]


def _lanes(m):
    return jnp.broadcast_to(m, m.shape[:-1] + (128,))


def _mlstm_fwd(qk, pm, gcol, bcol, grow, brow, norm_w):
    T = qk.shape[0]
    steps = T // STEP_ROWS
    H, D = MLSTM_HEADS, MLSTM_HEAD_DIM

    def body(q_ref, k_ref, v_ref, gc_ref, bc_ref, gr_ref, br_ref, o_ref, w_ref,
             h_ref, y_ref, cs_ref, ns_ref, ms_ref, c_st, n_st, m_st):
        @pl.when(pl.program_id(0) == 0)
        def _():
            c_st[...] = jnp.zeros_like(c_st)
            n_st[...] = jnp.zeros_like(n_st)
            m_st[...] = jnp.zeros_like(m_st)

        f = _mlstm_inputs(q_ref, k_ref, v_ref, *_mlstm_gates(gc_ref, bc_ref, gr_ref, br_ref))
        c0, n0, m0 = c_st[...], n_st[...], m_st[:, :, 0:1]
        c1, n1, m1, _, _, _ = _mlstm_update(f, 0, c0, n0, m0)
        c2, n2, m2, _, _, _ = _mlstm_update(f, 1, c1, n1, m1)
        f = _mlstm_head(f, (c0, c1), (n0, n1), (m0, m1))
        h = f["num"] / jnp.maximum(jnp.abs(f["den"]), f["floor"])
        hn, _ = _head_norm(h)
        w = jnp.stack([w_ref[:, D * hd:D * hd + D] for hd in range(H)])
        y = _sigmoid(_heads(o_ref).astype(F32)) * hn * w
        for hd in range(H):
            h_ref[:, D * hd:D * hd + D] = h[hd]
            y_ref[:, D * hd:D * hd + D] = y[hd].astype(BF16)
        cs_ref[0], cs_ref[1] = c0, c1
        ns_ref[0], ns_ref[1] = n0, n1
        ms_ref[0], ms_ref[1] = _lanes(m0), _lanes(m1)
        c_st[...], n_st[...], m_st[...] = c2, n2, _lanes(m2)

    vec = pl.BlockSpec((2, H, 1, 128), lambda s: (s, 0, 0, 0))
    rows = pl.BlockSpec((STEP_ROWS, 512), lambda s: (s, 0))
    return pl.pallas_call(
        body, name="mlstm_fwd", grid=(steps,),
        in_specs=_mlstm_specs(T, lambda s: s) + [pl.BlockSpec((STEP_ROWS, 512), lambda s: (s, 1)),
                                                 pl.BlockSpec((1, 512), lambda s: (0, 0))],
        out_specs=[rows, rows, pl.BlockSpec((2, H, 128, 128), lambda s: (s, 0, 0, 0)), vec, vec],
        out_shape=[_sds((T, 512), F32), _sds((T, 512), BF16), _sds((2 * steps, H, 128, 128), F32),
                   _sds((2 * steps, H, 1, 128), F32), _sds((2 * steps, H, 1, 128), F32)],
        scratch_shapes=[pltpu.VMEM((H, 128, 128), F32), pltpu.VMEM((H, 1, 128), F32),
                        pltpu.VMEM((H, 1, 128), F32)],
        compiler_params=_params(("arbitrary",)))(qk, qk, pm, gcol, bcol, grow, brow, pm, norm_w)


def _mlstm_bwd(qk, pm, gcol, bcol, grow, brow, cs, ns, ms, dh):
    T = qk.shape[0]
    steps = T // STEP_ROWS
    H, L, D = MLSTM_HEADS, MLSTM_CHUNK, MLSTM_HEAD_DIM
    rev = lambda s: steps - 1 - s

    def body(q_ref, k_ref, v_ref, gc_ref, bc_ref, gr_ref, br_ref, cs_ref, ns_ref, ms_ref, dh_ref,
             dqk_ref, dv_ref, dgc_ref, dgr_ref, dc_st, dn_st):
        @pl.when(pl.program_id(0) == 0)
        def _():
            dc_st[...] = jnp.zeros_like(dc_st)
            dn_st[...] = jnp.zeros_like(dn_st)

        f = _mlstm_inputs(q_ref, k_ref, v_ref, *_mlstm_gates(gc_ref, bc_ref, gr_ref, br_ref))
        c_prev = (cs_ref[0], cs_ref[1])
        n_prev = (ns_ref[0], ns_ref[1])
        m_prev = (ms_ref[0, :, :, 0:1], ms_ref[1, :, :, 0:1])
        f = _mlstm_head(f, c_prev, n_prev, m_prev)
        big = jnp.abs(f["den"]) > f["floor"]
        rden = 1.0 / jnp.where(big, jnp.abs(f["den"]), f["floor"])
        dnum = _heads(dh_ref) * rden
        hdh = jnp.sum(f["num"] * dnum, axis=-1, keepdims=True)
        dden = jnp.where(big, -hdh * rden * jnp.sign(f["den"]), 0.0)
        dnum_b = dnum.astype(BF16)
        dsc = _bdot(dnum_b, f["vb"], 2, 2) + dden
        g = dsc * f["sc"]
        dv = _bdot_rows(f["sc"].astype(BF16), dnum_b)
        dqk_ = (dsc * f["w_intra"]).astype(BF16)
        dq = _bdot(dqk_, f["kb"], 2, 1)
        dks = _bdot_rows(dqk_, f["qb"])
        wdn = f["w_inter"] * dnum
        wdn_b = wdn.astype(BF16)
        wdd = f["w_inter"] * dden
        u = jnp.sum(f["qc"] * wdn, axis=-1, keepdims=True) + wdd * f["qn"]
        dks_s, dv_s, z_s, dg_s = [None, None], [None, None], [None, None], [None, None]
        dcn, dnn = dc_st[...], dn_st[...]
        for ch in (1, 0):
            rows = slice(L * ch, L * ch + L)
            _, _, _, decay, e_a, kw = _mlstm_update(f, ch, c_prev[ch], n_prev[ch], m_prev[ch])
            dcn_b = dcn.astype(BF16)
            dkw = _bdot(f["vb"][:, rows], dcn_b, 2, 2) + dnn
            dks_s[ch] = e_a * dkw
            dv_s[ch] = _bdot(kw.astype(BF16), dcn_b, 2, 1)
            z_s[ch] = e_a * jnp.sum(f["ks"][:, rows] * dkw, axis=-1, keepdims=True)
            dg_s[ch] = jnp.sum(z_s[ch], axis=1, keepdims=True) + decay * (
                jnp.sum(c_prev[ch] * dcn, axis=(1, 2), keepdims=True)
                + jnp.sum(n_prev[ch] * dnn, axis=(1, 2), keepdims=True))
            dcn = decay * dcn + _bdot_rows(f["qb"][:, rows], wdn_b[:, rows])
            dnn = decay * dnn + jnp.sum(wdd[:, rows] * f["q"][:, rows], axis=1, keepdims=True)
        dc_st[...], dn_st[...] = dcn, dnn
        dq = dq + jnp.concatenate(
            [_bdot(wdn_b[:, :L], c_prev[0].astype(BF16), 2, 2) + wdd[:, :L] * n_prev[0],
             _bdot(wdn_b[:, L:], c_prev[1].astype(BF16), 2, 2) + wdd[:, L:] * n_prev[1]], axis=1)
        dks = (dks + jnp.concatenate(dks_s, axis=1)) * (D ** -0.5)
        dv = dv + jnp.concatenate(dv_s, axis=1)
        z = jnp.concatenate(z_s, axis=1)
        row = lax.broadcasted_iota(jnp.int32, (1, STEP_ROWS, 1), 1)
        dg_col = jnp.where(row == L - 1, dg_s[0], 0.0) + jnp.where(row == 2 * L - 1, dg_s[1], 0.0)
        db_col = jnp.sum(g, axis=-1, keepdims=True) + u - z + dg_col
        g_row = jnp.sum(g, axis=1, keepdims=True)
        lane = lax.broadcasted_iota(jnp.int32, (STEP_ROWS, 128), 1)
        sub = lax.broadcasted_iota(jnp.int32, (8, STEP_ROWS), 0)
        dgc = jnp.zeros((STEP_ROWS, 128), F32)
        dgr = jnp.zeros((8, STEP_ROWS), F32)
        for hd in range(H):
            dgc = dgc + jnp.where(lane == hd, z[hd], 0.0) + jnp.where(lane == H + hd, db_col[hd], 0.0)
            dgr = dgr + jnp.where(sub == hd, g_row[hd], 0.0) - jnp.where(sub == H + hd, g_row[hd], 0.0)
            dqk_ref[:, D * hd:D * hd + D] = dq[hd]
            dqk_ref[:, H * D + D * hd:H * D + D * hd + D] = dks[hd]
            dv_ref[:, D * hd:D * hd + D] = dv[hd].astype(BF16)
        dgc_ref[...] = dgc
        dgr_ref[...] = dgr

    return pl.pallas_call(
        body, name="mlstm_bwd", grid=(steps,),
        in_specs=_mlstm_specs(T, rev) + [
            pl.BlockSpec((2, H, 128, 128), lambda s: (rev(s), 0, 0, 0)),
            pl.BlockSpec((2, H, 1, 128), lambda s: (rev(s), 0, 0, 0)),
            pl.BlockSpec((2, H, 1, 128), lambda s: (rev(s), 0, 0, 0)),
            pl.BlockSpec((STEP_ROWS, 512), lambda s: (rev(s), 0))],
        out_specs=[pl.BlockSpec((STEP_ROWS, 1024), lambda s: (rev(s), 0)),
                   pl.BlockSpec((STEP_ROWS, 512), lambda s: (rev(s), 0)),
                   pl.BlockSpec((STEP_ROWS, 128), lambda s: (rev(s), 0)),
                   pl.BlockSpec((8, STEP_ROWS), lambda s: (0, rev(s)))],
        out_shape=[_sds((T, 1024), F32), _sds((T, 512), BF16), _sds((T, 128), F32), _sds((8, T), F32)],
        scratch_shapes=[pltpu.VMEM((H, 128, 128), F32), pltpu.VMEM((H, 1, 128), F32)],
        compiler_params=_params(("arbitrary",)))(qk, qk, pm, gcol, bcol, grow, brow, cs, ns, ms, dh)


def _rows_to_lanes(x):
    eye = (lax.broadcasted_iota(jnp.int32, (8, 128), 0)
           == lax.broadcasted_iota(jnp.int32, (8, 128), 1)).astype(BF16)
    out, rest = None, x
    for _ in range(3):
        piece = rest.astype(BF16)
        rest = rest - piece.astype(F32)
        t = _dot(piece, eye, 0, 0)
        out = t if out is None else out + t
    return out


def _gate_bwd(dgc, dgr, gcol, bcol):
    T = dgc.shape[0]
    tr = min(ROW_TILE, T)

    def body(a_ref, b_ref, g_ref, bias_ref, o_ref, acc_ref):
        i = pl.program_id(0)

        @pl.when(i == 0)
        def _():
            acc_ref[...] = jnp.zeros_like(acc_ref)

        d = a_ref[...] + _rows_to_lanes(b_ref[:, pl.ds(pl.multiple_of(i * tr, 128), tr)])
        lane = lax.broadcasted_iota(jnp.int32, d.shape, 1)
        is_f = (lane >= MLSTM_HEADS) & (lane < 2 * MLSTM_HEADS)
        dlogf = _chunk_rev_cumsum(jnp.where(is_f, d, 0.0), 0)
        out = jnp.where(is_f, dlogf * _sigmoid(-(g_ref[...] + bias_ref[...])), d)
        o_ref[...] = out.astype(BF16)
        acc_ref[0:1, :] += _colsum(out)

    return _rows("gate_bwd", body, [dgc, dgr, gcol, bcol],
                 [_sds((T, 128), BF16), _sds((8, 128), F32)], T, tr=tr)


def _head_norm(h, mu_axis=-1):
    mu = jnp.mean(h, axis=-1, keepdims=True)
    hc = h - mu
    r = lax.rsqrt(jnp.mean(hc * hc, axis=-1, keepdims=True) + NORM_EPS)
    return hc * r, r


def _mlstm_out_bwd_rows(ps, es):
    hm, vo, w_all = es
    D, width = MLSTM_HEAD_DIM, MLSTM_HEADS * MLSTM_HEAD_DIM
    dhs, dos, dws = [], [], []
    for hd in range(MLSTM_HEADS):
        cols = slice(D * hd, D * hd + D)
        hn, r = _head_norm(hm[:, cols])
        sg = _sigmoid(vo[:, width + D * hd:width + D * hd + D].astype(F32))
        dy, w = ps[0][:, cols], w_all[:, cols]
        dos.append(dy * hn * w * sg * (1.0 - sg))
        dyn = dy * sg
        dws.append(_colsum(dyn * hn))
        dhn = dyn * w
        dhs.append(r * (dhn - jnp.mean(dhn, axis=-1, keepdims=True)
                        - hn * jnp.mean(dhn * hn, axis=-1, keepdims=True)))
    cat = lambda parts: jnp.concatenate(parts, axis=1)
    return [cat(dhs), cat(dos)], [_acc_rows([cat(dws)])]


ADAM_TILE_ELEMS = 64 * 1024


def _adamw(name, w, g, m, v):
    R, C = w.shape
    fits = [t for t in range(8, R + 1, 8) if R % t == 0 and t * C <= ADAM_TILE_ELEMS]
    if fits or R * C <= ADAM_TILE_ELEMS:
        tr = fits[-1] if fits else R
        spec, grid = pl.BlockSpec((tr, C), lambda i: (i, 0)), (R // tr,)
    else:
        spec, grid = pl.BlockSpec((R, 128), lambda i: (0, i)), (C // 128,)
    c1 = 1.0 - ADAM_B1 ** ADAM_STEP
    c2 = 1.0 - ADAM_B2 ** ADAM_STEP

    def body(w_ref, g_ref, m_ref, v_ref, d_ref, mo_ref, vo_ref):
        g = g_ref[...]
        m = ADAM_B1 * m_ref[...] + (1.0 - ADAM_B1) * g
        v = ADAM_B2 * v_ref[...] + (1.0 - ADAM_B2) * (g * g)
        mo_ref[...] = m
        vo_ref[...] = v
        d_ref[...] = -ADAM_LR * ((m / c1) / (jnp.sqrt(v / c2) + ADAM_EPS) + ADAM_WD * w_ref[...])

    return pl.pallas_call(
        body, name=name, grid=grid, in_specs=[spec] * 4, out_specs=[spec] * 3,
        out_shape=[_sds((R, C), F32)] * 3, compiler_params=_params(("parallel",)))(w, g, m, v)


def _place():
    return lax.axis_index("x"), lax.axis_index("y"), lax.axis_index("c")


def _all_gather8(name, blk, space):
    m, n = blk.shape

    def body(x_ref, out_ref, send_sems, recv_sems, local_sem):
        x, y, c = _place()
        me, sibling = (x, y, c), (x, y, 1 - c)
        chips = [(1 - x, y), (x, 1 - y), (1 - x, 1 - y)]

        def rows(px, py, pc):
            return out_ref.at[pl.ds((4 * px + 2 * py + pc) * m, m), :]

        def copy(k, block, to, src=None):
            return pltpu.make_async_remote_copy(
                src_ref=rows(*block) if src is None else src, dst_ref=rows(*block),
                send_sem=send_sems.at[k], recv_sem=recv_sems.at[k],
                device_id=to, device_id_type=MESH)

        mine = pltpu.make_async_copy(x_ref, rows(*me), local_sem)
        mine.start()
        first = [copy(0, me, sibling, src=x_ref)]
        first += [copy(1 + j, me, (*chip, c), src=x_ref) for j, chip in enumerate(chips)]
        for cp in first:
            cp.start()
        passed = [copy(4 + j, (*chip, c), sibling) for j, chip in enumerate(chips)]
        for j, chip in enumerate(chips):
            copy(1 + j, (*chip, c), me).wait_recv()
            passed[j].start()
        copy(0, sibling, me).wait_recv()
        for j, chip in enumerate(chips):
            copy(4 + j, (*chip, 1 - c), me).wait_recv()
        for cp in first + passed:
            cp.wait_send()
        mine.wait()

    return pl.pallas_call(
        body, name=name, out_shape=_sds((8 * m, n), blk.dtype),
        in_specs=[pl.BlockSpec(memory_space=space)], out_specs=pl.BlockSpec(memory_space=space),
        scratch_shapes=[pltpu.SemaphoreType.DMA((7,)), pltpu.SemaphoreType.DMA((7,)),
                        pltpu.SemaphoreType.DMA],
        compiler_params=pltpu.CompilerParams(vmem_limit_bytes=VMEM_LIMIT))(blk)


def _hbm_specs(n):
    return [pl.BlockSpec(memory_space=pl.ANY)] * n


def _swap_halves_sibling(name, srcs):
    nw = len(srcs)

    def body(*refs):
        src_refs, dst_refs, send_sems, recv_sems = refs[:nw], refs[nw:2 * nw], refs[2 * nw], refs[2 * nw + 1]
        x, y, c = _place()
        cps = [pltpu.make_async_remote_copy(
            src_ref=src_refs[w].at[pl.ds(0, 4), 1 - c], dst_ref=dst_refs[w],
            send_sem=send_sems.at[w], recv_sem=recv_sems.at[w], device_id=(x, y, 1 - c),
            device_id_type=MESH) for w in range(nw)]
        for cp in cps:
            cp.start()
        for cp in cps:
            cp.wait()

    return pl.pallas_call(
        body, name=name, out_shape=[_sds(s.shape[:1] + s.shape[2:], s.dtype) for s in srcs],
        in_specs=_hbm_specs(nw), out_specs=_hbm_specs(nw),
        scratch_shapes=[pltpu.SemaphoreType.DMA((nw,)), pltpu.SemaphoreType.DMA((nw,))])(*srcs)


def _split_start(name, srcs, lands, copies, per_array, after):
    nw = len(srcs)

    def body(*refs):
        send_sems, recv_sems, token = refs[2 * nw + 1], refs[2 * nw + 2], refs[-1]
        for w in range(nw):
            for k, (s, d, dev) in enumerate(copies(refs[w], refs[nw + w], *_place())):
                pltpu.make_async_remote_copy(
                    src_ref=s, dst_ref=d, send_sem=send_sems.at[w * per_array + k],
                    recv_sem=recv_sems.at[w * per_array + k], device_id=dev, device_id_type=MESH).start()
        token[...] = jnp.zeros_like(token)

    hbm, sem = pl.BlockSpec(memory_space=pltpu.HBM), pl.BlockSpec(memory_space=pltpu.SEMAPHORE)
    arrays = list(srcs) + list(lands)
    out = pl.pallas_call(
        body, name=name,
        out_shape=(pltpu.SemaphoreType.DMA((nw * per_array,)), pltpu.SemaphoreType.DMA((nw * per_array,)),
                   *[pltpu.HBM(a.shape, a.dtype) for a in arrays], _sds((8, 128), F32)),
        in_specs=[hbm] * (2 * nw) + [pl.BlockSpec(memory_space=pl.ANY)],
        out_specs=(sem, sem, *[hbm] * (2 * nw), pl.BlockSpec(memory_space=pltpu.VMEM)),
        input_output_aliases={i: 2 + i for i in range(2 * nw)},
        compiler_params=pltpu.CompilerParams(has_side_effects=pltpu.SideEffectType.DATAFLOW_SIDE_EFFECTING))(
            *[pltpu.with_memory_space_constraint(a, pltpu.HBM) for a in arrays], after)
    return out[0], out[1], out[2:2 + nw], out[2 + nw:2 + 2 * nw], out[-1]


def _split_wait(name, started, after, waits, per_array):
    send_sems, recv_sems, srcs, lands, _ = started
    nw = len(srcs)

    def body(*refs):
        send_sems, recv_sems = refs[2 * nw], refs[2 * nw + 1]
        x, y, c = _place()
        for w in range(nw):
            for k, (s, d) in enumerate(waits(refs[w], refs[nw + w], x, y, c)):
                cp = pltpu.make_async_remote_copy(
                    src_ref=s, dst_ref=d, send_sem=send_sems.at[w * per_array + k],
                    recv_sem=recv_sems.at[w * per_array + k], device_id=(x, y, 1 - c),
                    device_id_type=MESH)
                cp.wait_send()
                cp.wait_recv()

    hbm, sem = pl.BlockSpec(memory_space=pltpu.HBM), pl.BlockSpec(memory_space=pltpu.SEMAPHORE)
    arrays = list(srcs) + list(lands)
    out = pl.pallas_call(
        body, name=name, out_shape=tuple(pltpu.HBM(a.shape, a.dtype) for a in arrays),
        in_specs=[hbm] * (2 * nw) + [sem, sem, pl.BlockSpec(memory_space=pl.ANY)],
        out_specs=tuple([hbm] * (2 * nw)), input_output_aliases={i: i for i in range(2 * nw)},
        compiler_params=pltpu.CompilerParams(has_side_effects=pltpu.SideEffectType.DATAFLOW_SIDE_EFFECTING))(
            *arrays, send_sems, recv_sems, after)
    return list(out[nw:])


def _other_chips(x, y):
    return [(1 - x, y), (x, 1 - y), (1 - x, 1 - y)]


def _gather_sends(src_ref, land_ref, x, y, c):
    to = land_ref.at[2 * x + y, c]
    return [(src_ref, to, (x, y, 1 - c))] + [(src_ref, to, (px, py, c)) for px, py in _other_chips(x, y)]


def _gather_lands(src_ref, land_ref, x, y, c):
    return [(src_ref, land_ref.at[2 * x + y, 1 - c])] + [
        (src_ref, land_ref.at[2 * px + py, c]) for px, py in _other_chips(x, y)]


def _gather_sends_all(src_ref, land_ref, x, y, c):
    to = land_ref.at[2 * x + y, c]
    return [(src_ref, to, (x, y, 1 - c))] + [
        (src_ref, to, (px, py, pc)) for px, py in _other_chips(x, y) for pc in (c, 1 - c)]


def _gather_lands_all(src_ref, land_ref, x, y, c):
    return [(src_ref, land_ref.at[2 * x + y, 1 - c])] + [
        (src_ref, land_ref.at[2 * px + py, pc]) for px, py in _other_chips(x, y) for pc in (c, 1 - c)]


def _scatter_sends(src_ref, land_ref, x, y, c):
    return [(src_ref.at[2 * px + py], land_ref.at[2 * x + y], (px, py, c)) for px, py in _other_chips(x, y)]


def _scatter_lands(src_ref, land_ref, x, y, c):
    return [(src_ref.at[2 * x + y], land_ref.at[2 * px + py]) for px, py in _other_chips(x, y)]


def _forward_sibling(name, lands):
    nw = len(lands)

    def body(*refs):
        land_refs, out_refs, send_sems, recv_sems = refs[:nw], refs[nw:2 * nw], refs[2 * nw], refs[2 * nw + 1]
        x, y, c = _place()
        cps = []
        for w in range(nw):
            cps += [pltpu.make_async_remote_copy(
                src_ref=land_refs[w].at[2 * px + py, c], dst_ref=out_refs[w].at[2 * px + py, c],
                send_sem=send_sems.at[w, j], recv_sem=recv_sems.at[w, j], device_id=(x, y, 1 - c),
                device_id_type=MESH) for j, (px, py) in enumerate(_other_chips(x, y))]
        for cp in cps:
            cp.start()
        for w in range(nw):
            for j, (px, py) in enumerate(_other_chips(x, y)):
                slot = out_refs[w].at[2 * px + py, 1 - c]
                pltpu.make_async_remote_copy(src_ref=slot, dst_ref=slot, send_sem=send_sems.at[w, j],
                                             recv_sem=recv_sems.at[w, j], device_id=(x, y, 1 - c),
                                             device_id_type=MESH).wait_recv()
        for cp in cps:
            cp.wait_send()

    return pl.pallas_call(
        body, name=name, out_shape=[_sds(a.shape, a.dtype) for a in lands],
        in_specs=_hbm_specs(nw), out_specs=_hbm_specs(nw), input_output_aliases={i: i for i in range(nw)},
        scratch_shapes=[pltpu.SemaphoreType.DMA((nw, 3)), pltpu.SemaphoreType.DMA((nw, 3))])(*lands)


def _share_halves(name, halves):
    nw = len(halves)

    def body(*refs):
        in_refs, out_refs, send_sems, recv_sems = refs[:nw], refs[nw:2 * nw], refs[2 * nw], refs[2 * nw + 1]
        x, y, c = _place()
        cps = [pltpu.make_async_remote_copy(
            src_ref=in_refs[w].at[c], dst_ref=out_refs[w].at[c], send_sem=send_sems.at[w],
            recv_sem=recv_sems.at[w], device_id=(x, y, 1 - c), device_id_type=MESH) for w in range(nw)]
        for cp in cps:
            cp.start()
        for w in range(nw):
            slot = out_refs[w].at[1 - c]
            pltpu.make_async_remote_copy(src_ref=slot, dst_ref=slot, send_sem=send_sems.at[w],
                                         recv_sem=recv_sems.at[w], device_id=(x, y, 1 - c),
                                         device_id_type=MESH).wait_recv()
        for cp in cps:
            cp.wait_send()

    return pl.pallas_call(
        body, name=name, out_shape=[_sds(a.shape, a.dtype) for a in halves],
        in_specs=_hbm_specs(nw), out_specs=_hbm_specs(nw), input_output_aliases={i: i for i in range(nw)},
        scratch_shapes=[pltpu.SemaphoreType.DMA((nw,)), pltpu.SemaphoreType.DMA((nw,))])(*halves)


def _place_blocks(name, blks, place):
    nw = len(blks)

    def body(p_ref, *refs):
        for b_ref, o_ref in zip(refs[:nw], refs[nw:]):
            o_ref[...] = b_ref[...]

    return pl.pallas_call(
        body, name=name,
        grid_spec=pltpu.PrefetchScalarGridSpec(
            num_scalar_prefetch=1, grid=(1,),
            in_specs=[pl.BlockSpec(b.shape, lambda i, p: (0, 0)) for b in blks],
            out_specs=[pl.BlockSpec((None, None) + b.shape, lambda i, p: (p[0], p[1], 0, 0)) for b in blks]),
        out_shape=[_sds((4, 2) + b.shape, b.dtype) for b in blks],
        compiler_params=_params(("arbitrary",)))(place, *blks)


def _pair_sum(name, fulls, gots, place):
    nw = len(fulls)

    def body(p_ref, *refs):
        s = pl.program_id(0)
        for a_ref, b_ref, o_ref, l_ref in zip(refs[:nw], refs[nw:2 * nw], refs[2 * nw:3 * nw], refs[3 * nw:]):
            o_ref[...] = (a_ref[...].astype(F32) + b_ref[...].astype(F32)).astype(o_ref.dtype)

            @pl.when(s == p_ref[0])
            def _():
                l_ref[...] = o_ref[...]

    slab = lambda a: pl.BlockSpec((None,) + a.shape[1:], lambda s, p: (s, 0, 0))
    mine = lambda a: pl.BlockSpec((None,) + a.shape[1:], lambda s, p: (p[0], 0, 0))
    out = pl.pallas_call(
        body, name=name,
        grid_spec=pltpu.PrefetchScalarGridSpec(
            num_scalar_prefetch=1, grid=(4,),
            in_specs=[pl.BlockSpec((None, None) + a.shape[2:], lambda s, p: (s, p[1], 0, 0)) for a in fulls]
            + [slab(b) for b in gots],
            out_specs=[slab(b) for b in gots] + [mine(b) for b in gots]),
        out_shape=[_sds(b.shape, BF16) for b in gots] * 2,
        compiler_params=_params(("arbitrary",)))(place, *fulls, *gots)
    return out[:nw], out[nw:]


def _sum4(name, arrs, place):
    nw = len(arrs)

    def body(p_ref, *refs):
        for a_ref, o_ref in zip(refs[:nw], refs[nw:]):
            acc = a_ref[0].astype(F32)
            for s in range(1, 4):
                acc = acc + a_ref[s].astype(F32)
            o_ref[...] = acc

    return pl.pallas_call(
        body, name=name,
        grid_spec=pltpu.PrefetchScalarGridSpec(
            num_scalar_prefetch=1, grid=(1,),
            in_specs=[pl.BlockSpec(a.shape, lambda i, p: (0, 0, 0)) for a in arrs],
            out_specs=[pl.BlockSpec((None,) + a.shape[1:], lambda i, p: (p[1], 0, 0)) for a in arrs]),
        out_shape=[_sds((2,) + a.shape[1:], F32) for a in arrs],
        compiler_params=_params(("arbitrary",)))(place, *arrs)


def _small_update(gathered, params, slots):
    c1 = 1.0 - ADAM_B1 ** ADAM_STEP
    c2 = 1.0 - ADAM_B2 ** ADAM_STEP
    n = len(params)

    def body(g_ref, *refs):
        ins, sum_ref, outs = refs[:3 * n], refs[3 * n], refs[3 * n + 1:]
        g_all = g_ref[0:1, :]
        for d in range(1, 8):
            g_all = g_all + g_ref[d:d + 1, :]
        sum_ref[...] = g_all
        for k, (off, width) in enumerate(slots):
            w_ref, m_ref, v_ref = ins[3 * k:3 * k + 3]
            go_ref, d_ref, mo_ref, vo_ref = outs[4 * k:4 * k + 4]
            g = g_all[:, off:off + width]
            m = ADAM_B1 * m_ref[...] + (1.0 - ADAM_B1) * g
            v = ADAM_B2 * v_ref[...] + (1.0 - ADAM_B2) * (g * g)
            go_ref[...], mo_ref[...], vo_ref[...] = g, m, v
            d_ref[...] = -ADAM_LR * ((m / c1) / (jnp.sqrt(v / c2) + ADAM_EPS) + ADAM_WD * w_ref[...])

    flat = [a for p in params for a in p]
    out = pl.pallas_call(
        body, name="small_update",
        out_shape=[_sds((1, gathered.shape[1]), F32)] + [_sds(p[0].shape, F32) for p in params for _ in range(4)],
        compiler_params=pltpu.CompilerParams(vmem_limit_bytes=VMEM_LIMIT))(gathered, *flat)
    return out[0], [tuple(out[1 + 4 * k:5 + 4 * k]) for k in range(n)]


def _swiglu(ps, es):
    g, u = ps
    return g * _sigmoid(g) * u, g, u


def _swiglu_bwd(ps, es):
    g, u = es[0].astype(F32), es[1].astype(F32)
    sg = _sigmoid(g)
    return ps[0] * u * (sg * (1.0 + g * (1.0 - sg))), ps[0] * (g * sg)


def _merge(ps, es):
    ga, gm = [e.astype(F32) for e in es]
    return (_sigmoid(ga) * ps[0] + _sigmoid(gm) * ps[1],)


def _merge_bwd(ps, es):
    dm, a, b = ps
    ga, gm = [e.astype(F32) for e in es]
    sa, sm = _sigmoid(ga), _sigmoid(gm)
    return dm * sa, dm * sm, dm * a * (sa * (1.0 - sa)), dm * b * (sm * (1.0 - sm))


W_IN_PIECES = (("q", 512), ("kv", 256), ("mqk", 1024), ("mv", 512), ("mo", 512), ("if", 8),
               ("ga", 1024), ("gm", 1024))


def _local_step(x, tgt, pos_col, mod, sp, in_weights, late_weights, ffn_grads, mixer_grads):
    sh_m, sc_m, gate_m, sh_f, sc_f, gate_f = mod
    inv = ROPE_THETA ** (-2.0 * jnp.arange(HEAD_DIM // 2, dtype=F32) / HEAD_DIM)
    cos, sin = _rope_tables(pos_col, jnp.tile(inv, 4).reshape(1, 128))
    W = dict(in_weights(cos))
    h, pa, pqk, pvo, pif, pg = _proj_in(x, sp["g_pre_mix"], sc_m, sh_m, [
        (W["q+kv"], F32, 256), (W["mqk"], F32, 512), (W["mv+mo"], BF16, 512), (W["if"], F32, 128),
        (W["ga+gm"], BF16, 512)])
    ya = _attn_fwd(pa, cos, sin, sp["sinks"])
    qk = _conv_fwd(pqk, sp["conv_w"], sp["conv_b"])
    bcol = jnp.pad(sp["b_if"], ((0, 0), (0, 120)))
    brow = jnp.broadcast_to(sp["b_if"].reshape(8, 1), (8, 128))
    grow = pif[:, :8].T
    hm, ym, cs, ns, ms = _mlstm_fwd(qk, pvo, pif, bcol, grow, brow, sp["norm_w"])
    W.update(late_weights(ym))
    w_fg, w_fu, w_fd = W["fg"], W["fu"], W["fd"]
    merged, = _mm("branches", [[(ya, W["ba"])], [(ym, W["bm"])]], [(pg, 0), (pg, 1)], _merge, [BF16],
                  cn=512, nt=True)
    wide, narrow = (D_MODEL, F32), (D_MODEL, BF16)
    mix, x1, h2 = _mm_rows("mix_out", [[(merged, W["out"])]],
                           [x, gate_m, sp["g_post_mix"], sp["g_pre_ffn"], sc_f, sh_f],
                           _res_norm_rows, [wide, wide, narrow], [], cn=512)
    act, gt, up = _mm("ffn_in", [[(h2, w_fg)], [(h2, w_fu)]], [], _swiglu, [BF16] * 3,
                      cn=256, nt=True)
    dy, dff, acc_l, loss = _mm_rows("ffn_down", [[(act, w_fd)]], [x1, tgt, gate_f, sp["g_post_ffn"]],
                                    _final_loss_rows, [wide, narrow], [(8, D_MODEL), (1, 128)], cn=512)

    G = {}
    dgt, dup = _mm("ffn_down_bwd", [[(dff, w_fd)]], [gt, up], _swiglu_bwd, [BF16, BF16],
                   cn=256, nt=True)
    g_fd, = _mm_tn_group("dw_ffn_down", [act], dff, BF16)
    g_fg, = _mm_tn_group("dw_ffn_gate", [dgt], h2, BF16)
    g_fu, = _mm_tn_group("dw_ffn_up", [dup], h2, BF16)
    tie = ffn_grads(g_fg, g_fu, g_fd)
    dx1, dmix, acc_r = _mm_rows(
        "ffn_in_bwd", [[(dgt, w_fg), (dup, w_fu)]],
        [x1, mix, dy, sc_f + tie, gate_m, sp["g_pre_ffn"], sp["g_post_mix"]],
        _res_norm_bwd_rows, [wide, narrow], [(8, D_MODEL)], cn=512, tm=256)
    d_a, d_m, dga, dgm = _mm("mix_out_bwd", [[(dmix, W["out"])], [(ya, W["ba"])], [(ym, W["bm"])]],
                             [(pg, 0), (pg, 1)], _merge_bwd, [BF16] * 4, cn=512, nt=True)
    G["out"], = _mm_tn_group("dw_out", [merged], dmix, BF16)
    dya, = _mm("branch_attn_bwd", [[(d_a, W["ba"])]], [], _first, [F32], cn=512)
    heads = MLSTM_HEADS * MLSTM_HEAD_DIM
    dhm, do_m, acc_n = _mm_rows("branch_mlstm_bwd", [[(d_m, W["bm"])]], [hm, pvo, sp["norm_w"]],
                                _mlstm_out_bwd_rows, [(heads, F32), (heads, BF16)], [(8, heads)], cn=512)
    G["ba"], = _mm_tn_group("dw_branch_attn", [d_a], ya, BF16)
    G["bm"], = _mm_tn_group("dw_branch_mlstm", [d_m], ym, BF16)
    dqk, dv_m, dgc, dgr = _mlstm_bwd(qk, pvo, pif, bcol, grow, brow, cs, ns, ms, dhm)
    dif, acc_g = _gate_bwd(dgc, dgr, pif, bcol)
    du, acc_c = _conv_bwd(pqk, sp["conv_w"], sp["conv_b"], dqk)
    dq_a, dkv, dsink = _attn_bwd(pa, cos, sin, sp["sinks"], dya)
    dproj = {"q": dq_a, "kv": dkv, "mqk": du, "mv": dv_m, "mo": do_m, "if": dif, "ga": dga, "gm": dgm}
    names = [k for k, _ in W_IN_PIECES]
    for part in (names[:4], names[4:]):
        G.update(zip(part, _mm_tn_group("dw_in_from_" + part[0], [dproj[k] for k in part], h, BF16)))
    w_tied = dict(W, **{"if": W["if"] + mixer_grads(G).astype(BF16)})
    dx, acc_p = _mm_rows("proj_bwd", [[(dproj[k], w_tied[k]) for k, _ in W_IN_PIECES]],
                         [x, dx1, sp["g_pre_mix"], sc_m], _pre_norm_bwd_rows, [wide], [(8, D_MODEL)], cn=512)

    small = {
        "mod": jnp.concatenate([acc_p[1], acc_p[0], acc_r[3], acc_r[1], acc_r[0], acc_l[0]]),
        "g_pre_mix": acc_p[2], "g_post_mix": acc_r[4], "b_if": acc_g[0, :8],
        "conv_w": acc_c[:CONV_WIDTH].reshape(-1), "conv_b": acc_c[CONV_WIDTH],
        "sinks": dsink[:, 0], "norm_w": acc_n[0], "g_pre_ffn": acc_r[2], "g_post_ffn": acc_l[1]}
    return loss, dx, small


IN_WIDTH = sum(n for _, n in W_IN_PIECES)
IN_SHARD = IN_WIDTH // 4
IN_SHARD_PAD = -(-IN_SHARD // 32) * 32


def _split_w_in(w_in_t):
    out, off, start = {}, 0, {}
    for k, n in W_IN_PIECES:
        out[k], start[k] = w_in_t[off:off + n], off
        off += n
    out["if"] = jnp.pad(out["if"], ((0, 120), (0, 0)))
    for name, first, last in (("q+kv", "q", "kv"), ("mv+mo", "mv", "mo"), ("ga+gm", "ga", "gm")):
        out[name] = w_in_t[start[first]:start[last] + out[last].shape[0]]
    return out


def _halves(a):
    return a.reshape(4, 2, a.shape[0] // 8, a.shape[1])


SMALL = (("b_ada", 6144), ("g_pre_mix", 1024), ("g_post_mix", 1024), ("b_if", 128), ("conv_w", 4096),
         ("conv_b", 1024), ("sinks", 128), ("norm_w", 512), ("g_pre_ffn", 1024), ("g_post_ffn", 1024))
SMALL_LEN = 8 * 2048


def _pack_small(vals):
    parts = []
    for k, n in SMALL:
        v = vals[k].reshape(-1)
        parts.append(jnp.pad(v, (0, n - v.shape[0])))
    flat = jnp.concatenate(parts)
    return jnp.pad(flat, (0, SMALL_LEN - flat.shape[0]))


def kernel(x, c, positions, w_ada, b_ada, g_pre_mix, g_post_mix, w_in, b_if, conv_w, conv_b, attn_sinks, mlstm_norm_w, w_branch_attn, w_branch_mlstm, w_out, g_pre_ffn, g_post_ffn, w_ffn_gate, w_ffn_up, w_ffn_down, loss_target, m_w_ada, m_b_ada, m_g_pre_mix, m_g_post_mix, m_w_in, m_b_if, m_conv_w, m_conv_b, m_attn_sinks, m_mlstm_norm_w, m_w_branch_attn, m_w_branch_mlstm, m_w_out, m_g_pre_ffn, m_g_post_ffn, m_w_ffn_gate, m_w_ffn_up, m_w_ffn_down, v_w_ada, v_b_ada, v_g_pre_mix, v_g_post_mix, v_w_in, v_b_if, v_conv_w, v_conv_b, v_attn_sinks, v_mlstm_norm_w, v_w_branch_attn, v_w_branch_mlstm, v_w_out, v_g_pre_ffn, v_g_post_ffn, v_w_ffn_gate, v_w_ffn_up, v_w_ffn_down):
    xi, yi, ci = _place()
    chip = 2 * xi + yi
    dev = 2 * chip + ci
    T = x.shape[1]
    ada_cols = w_ada.shape[2]

    place = jnp.stack([chip, ci]).astype(jnp.int32)

    def my_half(a):
        n = a.shape[0] // 2
        return lax.dynamic_slice_in_dim(a, ci * n, n, axis=0).astype(BF16)

    blk = jnp.concatenate([c.reshape(-1), conv_w.reshape(-1)]).reshape(8, 256)
    got = _all_gather8("gather_cond", blk, pltpu.VMEM).reshape(8, 2048)
    c_all = got[:, :D_MODEL].astype(BF16)
    conv_full = got[::2, D_MODEL:].reshape(4, CONV_WIDTH, -1).transpose(1, 0, 2).reshape(CONV_WIDTH, -1)

    b_sh = lax.dynamic_slice_in_dim(b_ada, chip * ada_cols, ada_cols, axis=1)
    mod_part, = _mm("ada_mod", [[(c_all, w_ada[0].astype(BF16))]], [b_sh],
                    lambda ps, es: (ps[0] + es[0],), [F32], cn=512, tm=8)
    mod_all = _all_gather8("gather_mod", mod_part, pltpu.VMEM).reshape(4, 2, 8, ada_cols)[:, 0]
    mod = lax.dynamic_index_in_dim(mod_all, dev, axis=1, keepdims=False).reshape(6, 1, D_MODEL)

    def gather_start(name, blks, after, sends, copies):
        return _split_start(name + "_start", blks, _place_blocks(name + "_place", blks, place),
                            sends, copies, after)

    w_in_t = jnp.pad(w_in[0].T, ((0, IN_SHARD_PAD - IN_SHARD), (0, 0)))
    in_started = gather_start("in_gather", [my_half(w_in_t)], mod, _gather_sends, 4)
    late_keys = ("fg", "fu", "fd", "out", "ba", "bm")
    late_started = gather_start(
        "late_gather",
        [my_half(w_ffn_gate[0].T), my_half(w_ffn_up[0].T), my_half(w_ffn_down[0]), my_half(w_out[0]),
         my_half(w_branch_attn[0].T), my_half(w_branch_mlstm[0].T)], in_started[4], _gather_sends_all, 7)
    mod = mod + (in_started[4][0, 0] + late_started[4][0, 0])

    def in_weights(after):
        g_in, = _forward_sibling("in_gather_forward",
                                 _split_wait("in_gather_wait", in_started, after, _gather_lands, 4))
        return _split_w_in(g_in.reshape(4, IN_SHARD_PAD, D_MODEL)[:, :IN_SHARD].reshape(IN_WIDTH, D_MODEL))

    def late_weights(after):
        lands = _split_wait("late_gather_wait", late_started, after, _gather_lands_all, 7)
        return {k: a.reshape(-1, a.shape[-1]) for k, a in zip(late_keys, lands)}

    sent = {}

    def scatter_start(name, groups):
        pairs, lands = _pair_sum(name + "_pair_sum", groups, _swap_halves_sibling(name + "_pair", groups), place)
        sent[name] = _split_start(name + "_start", pairs, lands, _scatter_sends, 3, pairs[0])
        return sent[name][4][0, 0]

    def ffn_grads(g_fg, g_fu, g_fd):
        return scatter_start("rs_ffn", [_halves(g_fg), _halves(g_fu), _halves(g_fd)])

    def mixer_grads(G):
        g_in_t = jnp.concatenate([G[k][:n] for k, n in W_IN_PIECES]).reshape(4, IN_SHARD, D_MODEL)
        g_in_t = jnp.pad(g_in_t, ((0, 0), (0, IN_SHARD_PAD - IN_SHARD), (0, 0)))
        return scatter_start("rs_mix", [g_in_t.reshape(4, 2, IN_SHARD_PAD // 2, D_MODEL), _halves(G["out"]),
                                        _halves(G["ba"]), _halves(G["bm"])])

    sp = {"g_pre_mix": g_pre_mix, "g_post_mix": g_post_mix, "b_if": b_if, "conv_w": conv_full,
          "conv_b": conv_b, "sinks": attn_sinks, "norm_w": mlstm_norm_w, "g_pre_ffn": g_pre_ffn,
          "g_post_ffn": g_post_ffn}
    loss, dx, small = _local_step(x[0], loss_target[0], positions.reshape(T, 1), [mod[i] for i in range(6)],
                                  sp, in_weights, late_weights, ffn_grads, mixer_grads)

    reds = (_sum4("rs_ffn_chip_sum", _split_wait("rs_ffn_wait", sent["rs_ffn"], dx, _scatter_lands, 3), place)
            + _sum4("rs_mix_chip_sum", _split_wait("rs_mix_wait", sent["rs_mix"], dx, _scatter_lands, 3), place))
    gsh = {k: s.reshape(-1, s.shape[-1])
           for k, s in zip(("fg", "fu", "fd", "w_in", "out", "ba", "bm"), _share_halves("rs_share", reds))}
    gsh["w_in"] = gsh["w_in"][:IN_SHARD]

    small["b_ada"] = small.pop("mod")
    vec = _pack_small(small).reshape(8, 2048)
    g_all = _all_gather8("gather_small", vec, pltpu.VMEM).reshape(8, SMALL_LEN)
    dmod_sh = lax.dynamic_slice_in_dim(g_all[:, :6 * D_MODEL], chip * ada_cols, ada_cols, axis=1)
    g_w_ada, = _mm_tn_group("dw_ada", [c_all], dmod_sh.astype(BF16), F32)

    smalls = {"b_ada": (b_ada, m_b_ada, v_b_ada), "g_pre_mix": (g_pre_mix, m_g_pre_mix, v_g_pre_mix),
              "g_post_mix": (g_post_mix, m_g_post_mix, v_g_post_mix), "b_if": (b_if, m_b_if, v_b_if),
              "conv_b": (conv_b, m_conv_b, v_conv_b), "sinks": (attn_sinks, m_attn_sinks, v_attn_sinks),
              "norm_w": (mlstm_norm_w, m_mlstm_norm_w, v_mlstm_norm_w),
              "g_pre_ffn": (g_pre_ffn, m_g_pre_ffn, v_g_pre_ffn),
              "g_post_ffn": (g_post_ffn, m_g_post_ffn, v_g_post_ffn)}
    offsets, off = {}, 0
    for k, width in SMALL:
        offsets[k], off = off, off + width
    g_sum, updates = _small_update(g_all, list(smalls.values()),
                                   [(offsets[k], t[0].shape[1]) for k, t in smalls.items()])
    g_conv = g_sum[:, offsets["conv_w"]:offsets["conv_w"] + CONV_WIDTH * D_MODEL].reshape(1, CONV_WIDTH, D_MODEL)
    g_conv = lax.dynamic_slice_in_dim(g_conv, chip * conv_w.shape[2], conv_w.shape[2], axis=2)

    res = dict(zip(smalls, updates))
    res["conv_w"] = (g_conv, *[o[None] for o in _adamw("adam_conv_w", conv_w[0], g_conv[0], m_conv_w[0], v_conv_w[0])])
    res["w_ada"] = (g_w_ada[None], *[o[None] for o in _adamw("adam_w_ada", w_ada[0], g_w_ada, m_w_ada[0], v_w_ada[0])])
    bigs = {"w_in": (w_in, m_w_in, v_w_in), "ba": (w_branch_attn, m_w_branch_attn, v_w_branch_attn),
            "bm": (w_branch_mlstm, m_w_branch_mlstm, v_w_branch_mlstm), "out": (w_out, m_w_out, v_w_out),
            "fg": (w_ffn_gate, m_w_ffn_gate, v_w_ffn_gate), "fu": (w_ffn_up, m_w_ffn_up, v_w_ffn_up),
            "fd": (w_ffn_down, m_w_ffn_down, v_w_ffn_down)}
    for k, (w, m, v) in bigs.items():
        if k in ("w_in", "fg", "fu"):
            res[k] = tuple(o.T[None] for o in (gsh[k], *_adamw("adam_" + k, w[0].T, gsh[k], m[0].T, v[0].T)))
        else:
            g = gsh[k].T if k in ("ba", "bm") else gsh[k]
            res[k] = (g[None], *[o[None] for o in _adamw("adam_" + k, w[0], g, m[0], v[0])])

    order = ("w_ada", "b_ada", "g_pre_mix", "g_post_mix", "w_in", "b_if", "conv_w", "conv_b", "sinks",
             "norm_w", "ba", "bm", "out", "g_pre_ffn", "g_post_ffn", "fg", "fu", "fd")
    total = lax.psum(loss[0, 0], ("x", "y", "c"))
    return (total, dx[None], *[res[k][0] for k in order], *[res[k][1] for k in order],
            *[res[k][2] for k in order], *[res[k][3] for k in order])
```

```python
import functools

import jax
import jax.numpy as jnp
from jax import lax
from jax.experimental import pallas as pl
from jax.experimental.pallas import tpu as pltpu

F32, BF16 = jnp.float32, jnp.bfloat16
MESH = pl.DeviceIdType.MESH

D_MODEL = 1024
N_Q_HEADS, N_KV_HEADS, HEAD_DIM, WINDOW = 8, 2, 64, 128
ROPE_THETA = 10000.0
MLSTM_HEADS, MLSTM_HEAD_DIM, MLSTM_CHUNK, CONV_WIDTH = 4, 128, 64, 4
D_FF = 2816
NORM_EPS = 1e-6
ADAM_LR, ADAM_B1, ADAM_B2, ADAM_EPS, ADAM_WD, ADAM_STEP = 0.001, 0.9, 0.999, 1e-08, 0.01, 10

VMEM_LIMIT = 56 * 1024 * 1024
ROW_TILE = 256
MM_TM = 512
MM_TT = 1024
ATTN_BLK = WINDOW
STEP_ROWS = 2 * MLSTM_CHUNK
NEG_INF = float("-inf")


def _params(sem):
    return pltpu.CompilerParams(dimension_semantics=sem, vmem_limit_bytes=VMEM_LIMIT)


def _sds(shape, dtype):
    return jax.ShapeDtypeStruct(shape, dtype)


def _sigmoid(x):
    return 1.0 / (1.0 + jnp.exp(-x))


def _dot(a, b, ca, cb):
    return lax.dot_general(a, b, (((ca,), (cb,)), ((), ())), preferred_element_type=F32)


def _bdot(a, b, ca, cb):
    return lax.dot_general(a, b, (((ca,), (cb,)), ((0,), (0,))), preferred_element_type=F32)


def _bdot_rows(a, b):
    return jnp.stack([_dot(a[h], b[h], 0, 0) for h in range(a.shape[0])])


def _mm(name, prods, extras, epi, out_dtypes, cn, nt=False, tm=MM_TM):
    flat = [ab for p in prods for ab in p]
    counts = [len(p) for p in prods]
    M = flat[0][0].shape[0]
    N = flat[0][1].shape[0 if nt else 1]
    tm = min(tm, M)
    n_in = 2 * len(flat) + len(extras)

    def body(*refs):
        ins, outs = refs[:n_in], refs[n_in:]
        for j in range(N // cn):
            cols = slice(j * cn, (j + 1) * cn)
            k, ps = 0, []
            for cnt in counts:
                acc = None
                for _ in range(cnt):
                    b = ins[k + 1][cols, :] if nt else ins[k + 1][:, cols]
                    d = _dot(ins[k][...], b, 1, 1 if nt else 0)
                    acc = d if acc is None else acc + d
                    k += 2
                ps.append(acc)
            res = epi(ps, [r[:, cols] for r in ins[k:]])
            for o, r in zip(outs, res):
                o[:, cols] = r.astype(o.dtype)

    in_specs, args = [], []
    for a, b in flat:
        in_specs.append(pl.BlockSpec((tm, a.shape[1]), lambda i: (i, 0)))
        in_specs.append(pl.BlockSpec(b.shape, lambda i: (0, 0), pipeline_mode=pl.Buffered(1)))
        args += [a, b]
    for e in extras:
        e, off = e if isinstance(e, tuple) else (e, 0)
        rows = 1 if e.shape[0] == 1 else tm
        in_specs.append(pl.BlockSpec((rows, N), lambda i, off=off, rows=rows: (0 if rows == 1 else i, off)))
        args.append(e)
    return pl.pallas_call(
        body, name=name, grid=(M // tm,), in_specs=in_specs,
        out_specs=[pl.BlockSpec((tm, N), lambda i: (i, 0)) for _ in out_dtypes],
        out_shape=[_sds((M, N), dt) for dt in out_dtypes],
        compiler_params=_params(("parallel",)))(*args)


def _mm_rows(name, prods, extras, epi, outs, accs, cn, nt=False, tm=MM_TM):
    flat = [ab for p in prods for ab in p]
    counts = [len(p) for p in prods]
    M = flat[0][0].shape[0]
    N = flat[0][1].shape[0 if nt else 1]
    tm = min(tm, M)
    n_mm, n_in, n_out = 2 * len(flat), 2 * len(flat) + len(extras), len(outs)

    def body(*refs):
        ins, out_refs, acc_refs = refs[:n_in], refs[n_in:n_in + n_out], refs[n_in + n_out:]

        @pl.when(pl.program_id(0) == 0)
        def _():
            for a in acc_refs:
                a[...] = jnp.zeros_like(a)

        chunks = [[] for _ in counts]
        for j in range(N // cn):
            cols = slice(j * cn, (j + 1) * cn)
            k = 0
            for p, cnt in enumerate(counts):
                acc = None
                for _ in range(cnt):
                    b = ins[k + 1][cols, :] if nt else ins[k + 1][:, cols]
                    d = _dot(ins[k][...], b, 1, 1 if nt else 0)
                    acc = d if acc is None else acc + d
                    k += 2
                chunks[p].append(acc)
        ps = [c[0] if len(c) == 1 else jnp.concatenate(c, axis=1) for c in chunks]
        res, incs = epi(ps, [r[...] for r in ins[n_mm:]])
        for o, r in zip(out_refs, res):
            o[...] = r.astype(o.dtype)
        for a, inc in zip(acc_refs, incs):
            a[...] += inc

    in_specs, args = [], []
    for a, b in flat:
        in_specs.append(pl.BlockSpec((tm, a.shape[1]), lambda i: (i, 0)))
        in_specs.append(pl.BlockSpec(b.shape, lambda i: (0, 0), pipeline_mode=pl.Buffered(1)))
        args += [a, b]
    for e in extras:
        rows = 1 if e.shape[0] == 1 else tm
        in_specs.append(pl.BlockSpec((rows, e.shape[1]), lambda i, rows=rows: (0 if rows == 1 else i, 0)))
        args.append(e)
    return pl.pallas_call(
        body, name=name, grid=(M // tm,), in_specs=in_specs,
        out_specs=[pl.BlockSpec((tm, w), lambda i: (i, 0)) for w, _ in outs]
        + [pl.BlockSpec(s, lambda i: (0, 0)) for s in accs],
        out_shape=[_sds((M, w), dt) for w, dt in outs] + [_sds(s, F32) for s in accs],
        compiler_params=_params(("arbitrary",)))(*args)


def _mm_tn_group(name, pieces, b, out_dtype, tt=MM_TT):
    T, N = b.shape
    tt = min(tt, T)
    steps, n = T // tt, len(pieces)

    def body(*refs):
        a_refs, b_ref, out_refs, accs = refs[:n], refs[n], refs[n + 1:2 * n + 1], refs[2 * n + 1:]
        t = pl.program_id(0)

        @pl.when(t == 0)
        def _():
            for acc in accs:
                acc[...] = jnp.zeros_like(acc)

        for a_ref, acc in zip(a_refs, accs):
            acc[...] += _dot(a_ref[...], b_ref[...], 0, 0)

        @pl.when(t == steps - 1)
        def _():
            for o_ref, acc in zip(out_refs, accs):
                o_ref[...] = acc[...].astype(o_ref.dtype)

    return pl.pallas_call(
        body, name=name, grid=(steps,),
        in_specs=[pl.BlockSpec((tt, a.shape[1]), lambda t: (t, 0)) for a in pieces]
        + [pl.BlockSpec((tt, N), lambda t: (t, 0))],
        out_specs=[pl.BlockSpec((a.shape[1], N), lambda t: (0, 0)) for a in pieces],
        out_shape=[_sds((a.shape[1], N), out_dtype) for a in pieces],
        scratch_shapes=[pltpu.VMEM((a.shape[1], N), F32) for a in pieces],
        compiler_params=_params(("arbitrary",)))(*pieces, b)


def _first(ps, es):
    return (ps[0],)


def _rows(name, body, ins, out_shapes, T, tr=ROW_TILE):
    tr = min(tr, T)

    def spec(shape):
        if shape[0] == T:
            return pl.BlockSpec((tr,) + tuple(shape[1:]), lambda i: (i,) + (0,) * (len(shape) - 1))
        return pl.BlockSpec(tuple(shape), lambda i: (0,) * len(shape))

    return pl.pallas_call(
        body, name=name, grid=(T // tr,),
        in_specs=[spec(a.shape) for a in ins], out_specs=[spec(s.shape) for s in out_shapes],
        out_shape=out_shapes, compiler_params=_params(("arbitrary",)))(*ins)


def _rms(x):
    r = lax.rsqrt(jnp.mean(x * x, axis=-1, keepdims=True) + NORM_EPS)
    return x * r, r


def _rms_bwd(dxn, xn, r):
    return r * (dxn - xn * jnp.mean(dxn * xn, axis=-1, keepdims=True))


def _colsum(v):
    return jnp.sum(v, axis=0, keepdims=True)


def _proj_in(x, g, sc, sh, groups):
    T = x.shape[0]
    tm = min(MM_TM, T)
    ng = len(groups)

    def body(x_ref, g_ref, sc_ref, sh_ref, *rest):
        w_refs, h_ref, out_refs = rest[:ng], rest[ng], rest[ng + 1:]
        xn, _ = _rms(x_ref[...])
        h = (xn * g_ref[...] * (1.0 + sc_ref[...]) + sh_ref[...]).astype(BF16)
        h_ref[...] = h
        for w_ref, o_ref, (w, _, cn) in zip(w_refs, out_refs, groups):
            for j in range(w.shape[0] // cn):
                cols = slice(j * cn, (j + 1) * cn)
                o_ref[:, cols] = _dot(h, w_ref[cols, :], 1, 1).astype(o_ref.dtype)

    row = pl.BlockSpec((1, D_MODEL), lambda i: (0, 0))
    tile = lambda w: pl.BlockSpec((tm, w), lambda i: (i, 0))
    return pl.pallas_call(
        body, name="proj_in", grid=(T // tm,),
        in_specs=[tile(D_MODEL), row, row, row] + [
            pl.BlockSpec(w.shape, lambda i: (0, 0), pipeline_mode=pl.Buffered(1)) for w, _, _ in groups],
        out_specs=[tile(D_MODEL)] + [tile(w.shape[0]) for w, _, _ in groups],
        out_shape=[_sds((T, D_MODEL), BF16)] + [_sds((T, w.shape[0]), dt) for w, dt, _ in groups],
        compiler_params=_params(("parallel",)))(x, g, sc, sh, *[w for w, _, _ in groups])


def _acc_rows(rows):
    w = rows[0].shape[1]
    return jnp.concatenate(rows + [jnp.zeros((8 - len(rows), w), F32)], axis=0)


def _res_norm_rows(ps, es):
    mix = ps[0]
    x, gate, gp, g2, sc, sh = es
    mh, _ = _rms(mix)
    x1 = x + gate * (mh * gp)
    xn, _ = _rms(x1)
    return [mix, x1, xn * g2 * (1.0 + sc) + sh], []


def _final_loss_rows(ps, es):
    x1, tgt, gate, gp = es
    fh, r = _rms(ps[0])
    e = x1 + gate * (fh * gp) - tgt
    loss = 0.5 * jnp.sum(jnp.mean(e * e, axis=-1, keepdims=True))
    dy = e * (1.0 / D_MODEL)
    acc = _acc_rows([_colsum(dy * fh * gp), _colsum(dy * gate * fh)])
    return [dy, _rms_bwd(dy * gate * gp, fh, r)], [acc, jnp.full((1, 128), loss, F32)]


def _res_norm_bwd_rows(ps, es):
    dh = ps[0]
    x1, mix, dy, sc, gate, g2, gp = es
    xn, r1 = _rms(x1)
    rows = [_colsum(dh * xn * g2), _colsum(dh), _colsum(dh * (1.0 + sc) * xn)]
    dx1 = dy + _rms_bwd(dh * (1.0 + sc) * g2, xn, r1)
    mh, rm = _rms(mix)
    rows += [_colsum(dx1 * mh * gp), _colsum(dx1 * gate * mh)]
    return [dx1, _rms_bwd(dx1 * gate * gp, mh, rm)], [_acc_rows(rows)]


def _pre_norm_bwd_rows(ps, es):
    dh = ps[0]
    x, dx1, g, sc = es
    xn, r = _rms(x)
    rows = [_colsum(dh * xn * g), _colsum(dh), _colsum(dh * (1.0 + sc) * xn)]
    return [dx1 + _rms_bwd(dh * (1.0 + sc) * g, xn, r)], [_acc_rows(rows)]


def _rope_tables(pos_col, inv_freq):
    T = pos_col.shape[0]

    def body(p_ref, f_ref, c_ref, s_ref):
        ang = p_ref[...].astype(F32) * f_ref[...]
        lane = lax.broadcasted_iota(jnp.int32, ang.shape, 1)
        c_ref[...] = jnp.cos(ang)
        s_ref[...] = jnp.where(lane % HEAD_DIM < HEAD_DIM // 2, -1.0, 1.0) * jnp.sin(ang)

    return _rows("rope_tables", body, [pos_col, inv_freq],
                 [_sds((T, 128), F32), _sds((T, 128), F32)], T, tr=512)


def _swap_halves(t):
    W = t.shape[1]
    lane = lax.broadcasted_iota(jnp.int32, t.shape, 1)
    half = HEAD_DIM // 2
    return jnp.where(lane % HEAD_DIM < half, pltpu.roll(t, W - half, 1), pltpu.roll(t, half, 1))


def _widen(c, W):
    return c if W == 128 else jnp.concatenate([c] * (W // 128), axis=1)


def _rope(t, c, s):
    W = t.shape[1]
    return t * _widen(c, W) + _swap_halves(t) * _widen(s, W)


def _unrope(dy, c, s):
    W = dy.shape[1]
    return dy * _widen(c, W) + _swap_halves(dy * _widen(s, W))


def _attn_mask(n):
    qi = lax.broadcasted_iota(jnp.int32, (ATTN_BLK, 2 * ATTN_BLK), 0)
    kj = lax.broadcasted_iota(jnp.int32, (ATTN_BLK, 2 * ATTN_BLK), 1)
    rel = kj - ATTN_BLK
    return (rel <= qi) & (qi - rel < WINDOW) & ((n > 0) | (kj >= ATTN_BLK))


def _attn_load(cur, prv, cc, sc, cp, sp):
    x, xp = cur[...], prv[...]
    q = _rope(x[:, :512], cc[...], sc[...]) * (HEAD_DIM ** -0.5)
    k = jnp.concatenate([_rope(xp[:, 512:640], cp[...], sp[...]),
                         _rope(x[:, 512:640], cc[...], sc[...])], axis=0)
    v = jnp.concatenate([xp[:, 640:768], x[:, 640:768]], axis=0)
    return q, k, v


ROLLED = tuple(h for h in range(N_Q_HEADS) if h % 2 != h // (N_Q_HEADS // N_KV_HEADS))


def _pair_heads(t):
    half = lax.broadcasted_iota(jnp.int32, (ATTN_BLK, 128), 1) // HEAD_DIM
    return jnp.stack([jnp.where(half == h % 2, t[:, 128 * (h // 2):128 * (h // 2) + 128], 0.0)
                      for h in range(N_Q_HEADS)])


def _kv_heads(t):
    half = lax.broadcasted_iota(jnp.int32, t.shape, 1) // HEAD_DIM
    tr = pltpu.roll(t, HEAD_DIM, 1)
    return jnp.stack([jnp.where(half == h % 2, tr if h in ROLLED else t, 0.0)
                      for h in range(N_Q_HEADS)])


def _sink_column(snk):
    return jnp.stack([jnp.full((1, 1), snk[0, h], F32) for h in range(N_Q_HEADS)])


def _attn_probs(qh, kh, mask, sink):
    s = jnp.where(mask, _bdot(qh, kh, 2, 2), NEG_INF)
    m = jnp.maximum(jnp.max(s, axis=-1, keepdims=True), sink)
    p = jnp.exp(s - m)
    es = jnp.exp(sink - m)
    rl = 1.0 / (jnp.sum(p, axis=-1, keepdims=True) + es)
    return p, es, rl


def _attn_specs(order):
    blk = lambda w: pl.BlockSpec((ATTN_BLK, w), lambda s: (order(s), 0))
    prv = lambda w: pl.BlockSpec((ATTN_BLK, w), lambda s: (jnp.maximum(order(s) - 1, 0), 0))
    return [blk(768), prv(768), blk(128), blk(128), prv(128), prv(128),
            pl.BlockSpec(memory_space=pltpu.SMEM)]


def _attn_fwd(pa, cos, sin, sinks):
    T = pa.shape[0]
    nb = T // ATTN_BLK

    def body(cur, prv, cc, sc, cp, sp, snk, y_ref):
        n = pl.program_id(0)
        q, k, v = _attn_load(cur, prv, cc, sc, cp, sp)
        mask = _attn_mask(n)
        half_q = lax.broadcasted_iota(jnp.int32, (ATTN_BLK, 128), 1) // HEAD_DIM
        half_k = lax.broadcasted_iota(jnp.int32, k.shape, 1) // HEAD_DIM
        moved = (pltpu.roll(k, HEAD_DIM, 1), pltpu.roll(v, HEAD_DIM, 1))
        for pair in range(N_Q_HEADS // 2):
            o = None
            for a in range(2):
                h = 2 * pair + a
                ku, vu = moved if h in ROLLED else (k, v)
                qh = jnp.where(half_q == a, q[:, 128 * pair:128 * pair + 128], 0.0).astype(BF16)
                kh = jnp.where(half_k == a, ku, 0.0).astype(BF16)
                vh = jnp.where(half_k == a, vu, 0.0).astype(BF16)
                s = jnp.where(mask, _dot(qh, kh, 1, 1), NEG_INF)
                m = jnp.maximum(jnp.max(s, axis=-1, keepdims=True), snk[0, h])
                p = jnp.exp(s - m)
                rl = 1.0 / (jnp.sum(p, axis=-1, keepdims=True) + jnp.exp(snk[0, h] - m))
                oh = _dot(p.astype(BF16), vh, 1, 0) * rl
                o = oh if o is None else o + oh
            y_ref[:, 128 * pair:128 * pair + 128] = o.astype(BF16)

    return pl.pallas_call(
        body, name="attn_fwd", grid=(nb,), in_specs=_attn_specs(lambda s: s),
        out_specs=pl.BlockSpec((ATTN_BLK, 512), lambda n: (n, 0)),
        out_shape=_sds((T, 512), BF16), compiler_params=_params(("parallel",)))(
            pa, pa, cos, sin, cos, sin, sinks)


def _attn_bwd(pa, cos, sin, sinks, dy):
    T = pa.shape[0]
    nb = T // ATTN_BLK
    rev = lambda s: nb - 1 - s

    def body(cur, prv, cc, sc, cp, sp, snk, dy_ref, dq_ref, dkv_ref, dsink_ref, carry):
        n = rev(pl.program_id(0))

        @pl.when(pl.program_id(0) == 0)
        def _():
            dsink_ref[...] = jnp.zeros_like(dsink_ref)
            carry[...] = jnp.zeros_like(carry)

        q, k, v = _attn_load(cur, prv, cc, sc, cp, sp)
        qh, kh, vh = _pair_heads(q).astype(BF16), _kv_heads(k).astype(BF16), _kv_heads(v).astype(BF16)
        p, es, rl = _attn_probs(qh, kh, _attn_mask(n), _sink_column(snk))
        pn = p * rl
        do = _pair_heads(dy_ref[...]).astype(BF16)
        dp = _bdot(do, vh, 2, 2)
        delta = jnp.sum(pn * dp, axis=-1, keepdims=True)
        ds = (pn * (dp - delta)).astype(BF16)
        dsink = es * rl * delta
        dq = _bdot(ds, kh, 2, 1) * (HEAD_DIM ** -0.5)
        dkh = _bdot_rows(ds, qh)
        dvh = _bdot_rows(pn.astype(BF16), do)

        def fold(t):
            same = [t[h] for h in range(N_Q_HEADS) if h not in ROLLED]
            moved = [t[h] for h in ROLLED]
            return sum(same[1:], same[0]) + pltpu.roll(sum(moved[1:], moved[0]), HEAD_DIM, 1)

        dk, dv = fold(dkh), fold(dvh)
        for h in range(N_Q_HEADS):
            dsink_ref[h:h + 1, :] += -jnp.sum(dsink[h])
        for pair in range(N_Q_HEADS // 2):
            dq_ref[:, 128 * pair:128 * pair + 128] = _unrope(
                dq[2 * pair] + dq[2 * pair + 1], cc[...], sc[...]).astype(BF16)
        dkv_ref[:, 0:128] = _unrope(dk[ATTN_BLK:] + carry[:, 0:128], cc[...], sc[...]).astype(BF16)
        dkv_ref[:, 128:256] = (dv[ATTN_BLK:] + carry[:, 128:256]).astype(BF16)
        carry[:, 0:128] = dk[:ATTN_BLK]
        carry[:, 128:256] = dv[:ATTN_BLK]

    blk = lambda w: pl.BlockSpec((ATTN_BLK, w), lambda s: (rev(s), 0))
    return pl.pallas_call(
        body, name="attn_bwd", grid=(nb,), in_specs=_attn_specs(rev) + [blk(512)],
        out_specs=[blk(512), blk(256), pl.BlockSpec((8, 128), lambda s: (0, 0))],
        out_shape=[_sds((T, 512), BF16), _sds((T, 256), BF16), _sds((8, 128), F32)],
        scratch_shapes=[pltpu.VMEM((ATTN_BLK, 256), F32)],
        compiler_params=_params(("arbitrary",)))(pa, pa, cos, sin, cos, sin, sinks, dy)


CONV_COLS = 2 * MLSTM_HEADS * MLSTM_HEAD_DIM


def _conv_pre(cur_ref, halo_ref, w_ref, b_ref, i, tr):
    xx = jnp.concatenate([jnp.where(i > 0, halo_ref[...], 0.0), cur_ref[...]], axis=0)
    taps = [(pltpu.roll(xx, CONV_WIDTH - 1 - j, 0) if j < CONV_WIDTH - 1 else xx)[8:8 + tr]
            for j in range(CONV_WIDTH)]
    pre = b_ref[...]
    for j in range(CONV_WIDTH):
        pre = pre + taps[j] * w_ref[j:j + 1, :]
    return pre, taps


def _conv_specs(T, tr):
    return [pl.BlockSpec((tr, CONV_COLS), lambda i: (i, 0)),
            pl.BlockSpec((8, CONV_COLS), lambda i: (jnp.maximum(i * (tr // 8) - 1, 0), 0)),
            pl.BlockSpec((CONV_WIDTH, CONV_COLS), lambda i: (0, 0)),
            pl.BlockSpec((1, CONV_COLS), lambda i: (0, 0))]


def _conv_fwd(pm, w, b):
    T = pm.shape[0]
    tr = min(ROW_TILE, T)

    def body(cur_ref, halo_ref, w_ref, b_ref, o_ref):
        pre, _ = _conv_pre(cur_ref, halo_ref, w_ref, b_ref, pl.program_id(0), tr)
        o_ref[...] = pre * _sigmoid(pre)

    return pl.pallas_call(
        body, name="conv_fwd", grid=(T // tr,), in_specs=_conv_specs(T, tr),
        out_specs=pl.BlockSpec((tr, CONV_COLS), lambda i: (i, 0)),
        out_shape=_sds((T, CONV_COLS), F32), compiler_params=_params(("parallel",)))(pm, pm, w, b)


def _conv_bwd(pqk, w, b, dqk):
    T = pqk.shape[0]
    tr = min(ROW_TILE, T)
    nt = T // tr

    def body(cur_ref, prev_ref, next_ref, w_ref, b_ref, d_ref, dnext_ref, du_ref, acc_ref):
        i = pl.program_id(0)

        @pl.when(i == 0)
        def _():
            acc_ref[...] = jnp.zeros_like(acc_ref)

        last = i == nt - 1
        xx = jnp.concatenate([jnp.where(i > 0, prev_ref[...], 0.0), cur_ref[...],
                              jnp.where(last, 0.0, next_ref[...])], axis=0)
        taps = [(pltpu.roll(xx, CONV_WIDTH - 1 - j, 0) if j < CONV_WIDTH - 1 else xx)[8:16 + tr]
                for j in range(CONV_WIDTH)]
        pre = b_ref[...]
        for j in range(CONV_WIDTH):
            pre = pre + taps[j] * w_ref[j:j + 1, :]
        sg = _sigmoid(pre)
        dd = jnp.concatenate([d_ref[...], jnp.where(last, 0.0, dnext_ref[...])], axis=0)
        dpre = dd * (sg * (1.0 + pre * (1.0 - sg)))
        for j in range(CONV_WIDTH):
            acc_ref[j:j + 1, :] += _colsum(dpre[:tr] * taps[j][:tr])
        acc_ref[CONV_WIDTH:CONV_WIDTH + 1, :] += _colsum(dpre[:tr])
        du = dpre[:tr] * w_ref[CONV_WIDTH - 1:CONV_WIDTH, :]
        for j in range(CONV_WIDTH - 1):
            k = CONV_WIDTH - 1 - j
            du = du + pltpu.roll(dpre, tr + 8 - k, 0)[:tr] * w_ref[j:j + 1, :]
        du_ref[...] = du.astype(BF16)

    tile = pl.BlockSpec((tr, CONV_COLS), lambda i: (i, 0))
    after = pl.BlockSpec((8, CONV_COLS), lambda i: (jnp.minimum((i + 1) * (tr // 8), T // 8 - 1), 0))
    before = pl.BlockSpec((8, CONV_COLS), lambda i: (jnp.maximum(i * (tr // 8) - 1, 0), 0))
    return pl.pallas_call(
        body, name="conv_bwd", grid=(nt,),
        in_specs=[tile, before, after, pl.BlockSpec((CONV_WIDTH, CONV_COLS), lambda i: (0, 0)),
                  pl.BlockSpec((1, CONV_COLS), lambda i: (0, 0)), tile, after],
        out_specs=[tile, pl.BlockSpec((8, CONV_COLS), lambda i: (0, 0))],
        out_shape=[_sds((T, CONV_COLS), BF16), _sds((8, CONV_COLS), F32)],
        compiler_params=_params(("arbitrary",)))(pqk, pqk, pqk, w, b, dqk, dqk)


def _log_sigmoid(x):
    return jnp.minimum(x, 0.0) - jnp.log1p(jnp.exp(-jnp.abs(x)))


def _chunk_cumsum(x, axis):
    idx = lax.broadcasted_iota(jnp.int32, x.shape, axis) % MLSTM_CHUNK
    k = 1
    while k < MLSTM_CHUNK:
        x = x + jnp.where(idx >= k, pltpu.roll(x, k, axis), 0.0)
        k *= 2
    return x


def _chunk_rev_cumsum(x, axis):
    n = x.shape[axis]
    idx = lax.broadcasted_iota(jnp.int32, x.shape, axis) % MLSTM_CHUNK
    k = 1
    while k < MLSTM_CHUNK:
        x = x + jnp.where(idx < MLSTM_CHUNK - k, pltpu.roll(x, n - k, axis), 0.0)
        k *= 2
    return x


def _mlstm_gates(gc_ref, bc_ref, gr_ref, br_ref):
    gc = gc_ref[...] + bc_ref[...]
    gr = gr_ref[...] + br_ref[...]
    return gc, _chunk_cumsum(_log_sigmoid(gc), 0), gr, _chunk_cumsum(_log_sigmoid(gr), 1)


def _heads(ref, base=0):
    D = MLSTM_HEAD_DIM
    return jnp.stack([ref[:, base + D * h:base + D * h + D] for h in range(MLSTM_HEADS)])


def _mlstm_inputs(q_ref, k_ref, v_ref, gc, bc, gr, br):
    H = MLSTM_HEADS
    q, v = _heads(q_ref), _heads(v_ref)
    ks = _heads(k_ref) * (MLSTM_HEAD_DIM ** -0.5)
    return dict(
        q=q, ks=ks, qb=q.astype(BF16), kb=ks.astype(BF16), vb=v.astype(BF16),
        b_col=jnp.stack([bc[:, H + h:H + h + 1] for h in range(H)]),
        i_col=jnp.stack([gc[:, h:h + 1] for h in range(H)]),
        b_row=jnp.stack([br[H + h:H + h + 1, :] for h in range(H)]),
        i_row=jnp.stack([gr[h:h + 1, :] for h in range(H)]))


def _mlstm_head(f, c_prev, n_prev, m_prev):
    L = MLSTM_CHUNK
    q, qb = f["q"], f["qb"]
    t = lax.broadcasted_iota(jnp.int32, (1, 2 * L, 2 * L), 1)
    s = lax.broadcasted_iota(jnp.int32, (1, 2 * L, 2 * L), 2)
    mask = (t // L == s // L) & (s <= t)
    d = jnp.where(mask, f["b_col"] - f["b_row"] + f["i_row"], NEG_INF)
    row = lax.broadcasted_iota(jnp.int32, (1, 2 * L, 1), 1)
    inter = f["b_col"] + jnp.where(row < L, m_prev[0], m_prev[1])
    m_t = jnp.maximum(inter, jnp.max(d, axis=-1, keepdims=True))
    w_intra = jnp.exp(d - m_t)
    w_inter = jnp.exp(inter - m_t)
    sc = _bdot(qb, f["kb"], 2, 2) * w_intra
    qc = jnp.concatenate([_bdot(qb[:, :L], c_prev[0].astype(BF16), 2, 1),
                          _bdot(qb[:, L:], c_prev[1].astype(BF16), 2, 1)], axis=1)
    qn = jnp.concatenate([jnp.sum(q[:, :L] * n_prev[0], axis=-1, keepdims=True),
                          jnp.sum(q[:, L:] * n_prev[1], axis=-1, keepdims=True)], axis=1)
    num = _bdot(sc.astype(BF16), f["vb"], 2, 1) + w_inter * qc
    den = jnp.sum(sc, axis=-1, keepdims=True) + w_inter * qn
    return dict(f, w_intra=w_intra, w_inter=w_inter, sc=sc, qc=qc, qn=qn, num=num, den=den,
                floor=jnp.exp(-m_t))


def _mlstm_update(f, ch, c, n, m):
    L = MLSTM_CHUNK
    rows = slice(L * ch, L * ch + L)
    b_col = f["b_col"][:, rows]
    g_last = b_col[:, L - 1:L]
    a_col = g_last - b_col + f["i_col"][:, rows]
    m_new = jnp.maximum(g_last + m, jnp.max(a_col, axis=1, keepdims=True))
    decay = jnp.exp(g_last + m - m_new)
    e_a = jnp.exp(a_col - m_new)
    kw = f["ks"][:, rows] * e_a
    c_new = decay * c + _bdot_rows(kw.astype(BF16), f["vb"][:, rows])
    n_new = decay * n + jnp.sum(kw, axis=1, keepdims=True)
    return c_new, n_new, m_new, decay, e_a, kw


def _mlstm_specs(T, order):
    blk = lambda w, col: pl.BlockSpec((STEP_ROWS, w), lambda s: (order(s), col))
    return [blk(512, 0), blk(512, 1), blk(512, 0), blk(128, 0),
            pl.BlockSpec((1, 128), lambda s: (0, 0)),
            pl.BlockSpec((8, STEP_ROWS), lambda s: (0, order(s))),
            pl.BlockSpec((8, 128), lambda s: (0, 0))]


def _lanes(m):
    return jnp.broadcast_to(m, m.shape[:-1] + (128,))


def _mlstm_fwd(qk, pm, gcol, bcol, grow, brow, norm_w):
    T = qk.shape[0]
    steps = T // STEP_ROWS
    H, D = MLSTM_HEADS, MLSTM_HEAD_DIM

    def body(q_ref, k_ref, v_ref, gc_ref, bc_ref, gr_ref, br_ref, o_ref, w_ref,
             h_ref, y_ref, cs_ref, ns_ref, ms_ref, c_st, n_st, m_st):
        @pl.when(pl.program_id(0) == 0)
        def _():
            c_st[...] = jnp.zeros_like(c_st)
            n_st[...] = jnp.zeros_like(n_st)
            m_st[...] = jnp.zeros_like(m_st)

        f = _mlstm_inputs(q_ref, k_ref, v_ref, *_mlstm_gates(gc_ref, bc_ref, gr_ref, br_ref))
        c0, n0, m0 = c_st[...], n_st[...], m_st[:, :, 0:1]
        c1, n1, m1, _, _, _ = _mlstm_update(f, 0, c0, n0, m0)
        c2, n2, m2, _, _, _ = _mlstm_update(f, 1, c1, n1, m1)
        f = _mlstm_head(f, (c0, c1), (n0, n1), (m0, m1))
        h = f["num"] / jnp.maximum(jnp.abs(f["den"]), f["floor"])
        hn, _ = _head_norm(h)
        w = jnp.stack([w_ref[:, D * hd:D * hd + D] for hd in range(H)])
        y = _sigmoid(_heads(o_ref).astype(F32)) * hn * w
        for hd in range(H):
            h_ref[:, D * hd:D * hd + D] = h[hd]
            y_ref[:, D * hd:D * hd + D] = y[hd].astype(BF16)
        cs_ref[0], cs_ref[1] = c0, c1
        ns_ref[0], ns_ref[1] = n0, n1
        ms_ref[0], ms_ref[1] = _lanes(m0), _lanes(m1)
        c_st[...], n_st[...], m_st[...] = c2, n2, _lanes(m2)

    vec = pl.BlockSpec((2, H, 1, 128), lambda s: (s, 0, 0, 0))
    rows = pl.BlockSpec((STEP_ROWS, 512), lambda s: (s, 0))
    return pl.pallas_call(
        body, name="mlstm_fwd", grid=(steps,),
        in_specs=_mlstm_specs(T, lambda s: s) + [pl.BlockSpec((STEP_ROWS, 512), lambda s: (s, 1)),
                                                 pl.BlockSpec((1, 512), lambda s: (0, 0))],
        out_specs=[rows, rows, pl.BlockSpec((2, H, 128, 128), lambda s: (s, 0, 0, 0)), vec, vec],
        out_shape=[_sds((T, 512), F32), _sds((T, 512), BF16), _sds((2 * steps, H, 128, 128), F32),
                   _sds((2 * steps, H, 1, 128), F32), _sds((2 * steps, H, 1, 128), F32)],
        scratch_shapes=[pltpu.VMEM((H, 128, 128), F32), pltpu.VMEM((H, 1, 128), F32),
                        pltpu.VMEM((H, 1, 128), F32)],
        compiler_params=_params(("arbitrary",)))(qk, qk, pm, gcol, bcol, grow, brow, pm, norm_w)


def _mlstm_bwd(qk, pm, gcol, bcol, grow, brow, cs, ns, ms, dh):
    T = qk.shape[0]
    steps = T // STEP_ROWS
    H, L, D = MLSTM_HEADS, MLSTM_CHUNK, MLSTM_HEAD_DIM
    rev = lambda s: steps - 1 - s

    def body(q_ref, k_ref, v_ref, gc_ref, bc_ref, gr_ref, br_ref, cs_ref, ns_ref, ms_ref, dh_ref,
             dqk_ref, dv_ref, dgc_ref, dgr_ref, dc_st, dn_st):
        @pl.when(pl.program_id(0) == 0)
        def _():
            dc_st[...] = jnp.zeros_like(dc_st)
            dn_st[...] = jnp.zeros_like(dn_st)

        f = _mlstm_inputs(q_ref, k_ref, v_ref, *_mlstm_gates(gc_ref, bc_ref, gr_ref, br_ref))
        c_prev = (cs_ref[0], cs_ref[1])
        n_prev = (ns_ref[0], ns_ref[1])
        m_prev = (ms_ref[0, :, :, 0:1], ms_ref[1, :, :, 0:1])
        f = _mlstm_head(f, c_prev, n_prev, m_prev)
        big = jnp.abs(f["den"]) > f["floor"]
        rden = 1.0 / jnp.where(big, jnp.abs(f["den"]), f["floor"])
        dnum = _heads(dh_ref) * rden
        hdh = jnp.sum(f["num"] * dnum, axis=-1, keepdims=True)
        dden = jnp.where(big, -hdh * rden * jnp.sign(f["den"]), 0.0)
        dnum_b = dnum.astype(BF16)
        dsc = _bdot(dnum_b, f["vb"], 2, 2) + dden
        g = dsc * f["sc"]
        dv = _bdot_rows(f["sc"].astype(BF16), dnum_b)
        dqk_ = (dsc * f["w_intra"]).astype(BF16)
        dq = _bdot(dqk_, f["kb"], 2, 1)
        dks = _bdot_rows(dqk_, f["qb"])
        wdn = f["w_inter"] * dnum
        wdn_b = wdn.astype(BF16)
        wdd = f["w_inter"] * dden
        u = jnp.sum(f["qc"] * wdn, axis=-1, keepdims=True) + wdd * f["qn"]
        dks_s, dv_s, z_s, dg_s = [None, None], [None, None], [None, None], [None, None]
        dcn, dnn = dc_st[...], dn_st[...]
        for ch in (1, 0):
            rows = slice(L * ch, L * ch + L)
            _, _, _, decay, e_a, kw = _mlstm_update(f, ch, c_prev[ch], n_prev[ch], m_prev[ch])
            dcn_b = dcn.astype(BF16)
            dkw = _bdot(f["vb"][:, rows], dcn_b, 2, 2) + dnn
            dks_s[ch] = e_a * dkw
            dv_s[ch] = _bdot(kw.astype(BF16), dcn_b, 2, 1)
            z_s[ch] = e_a * jnp.sum(f["ks"][:, rows] * dkw, axis=-1, keepdims=True)
            dg_s[ch] = jnp.sum(z_s[ch], axis=1, keepdims=True) + decay * (
                jnp.sum(c_prev[ch] * dcn, axis=(1, 2), keepdims=True)
                + jnp.sum(n_prev[ch] * dnn, axis=(1, 2), keepdims=True))
            dcn = decay * dcn + _bdot_rows(f["qb"][:, rows], wdn_b[:, rows])
            dnn = decay * dnn + jnp.sum(wdd[:, rows] * f["q"][:, rows], axis=1, keepdims=True)
        dc_st[...], dn_st[...] = dcn, dnn
        dq = dq + jnp.concatenate(
            [_bdot(wdn_b[:, :L], c_prev[0].astype(BF16), 2, 2) + wdd[:, :L] * n_prev[0],
             _bdot(wdn_b[:, L:], c_prev[1].astype(BF16), 2, 2) + wdd[:, L:] * n_prev[1]], axis=1)
        dks = (dks + jnp.concatenate(dks_s, axis=1)) * (D ** -0.5)
        dv = dv + jnp.concatenate(dv_s, axis=1)
        z = jnp.concatenate(z_s, axis=1)
        row = lax.broadcasted_iota(jnp.int32, (1, STEP_ROWS, 1), 1)
        dg_col = jnp.where(row == L - 1, dg_s[0], 0.0) + jnp.where(row == 2 * L - 1, dg_s[1], 0.0)
        db_col = jnp.sum(g, axis=-1, keepdims=True) + u - z + dg_col
        g_row = jnp.sum(g, axis=1, keepdims=True)
        lane = lax.broadcasted_iota(jnp.int32, (STEP_ROWS, 128), 1)
        sub = lax.broadcasted_iota(jnp.int32, (8, STEP_ROWS), 0)
        dgc = jnp.zeros((STEP_ROWS, 128), F32)
        dgr = jnp.zeros((8, STEP_ROWS), F32)
        for hd in range(H):
            dgc = dgc + jnp.where(lane == hd, z[hd], 0.0) + jnp.where(lane == H + hd, db_col[hd], 0.0)
            dgr = dgr + jnp.where(sub == hd, g_row[hd], 0.0) - jnp.where(sub == H + hd, g_row[hd], 0.0)
            dqk_ref[:, D * hd:D * hd + D] = dq[hd]
            dqk_ref[:, H * D + D * hd:H * D + D * hd + D] = dks[hd]
            dv_ref[:, D * hd:D * hd + D] = dv[hd].astype(BF16)
        dgc_ref[...] = dgc
        dgr_ref[...] = dgr

    return pl.pallas_call(
        body, name="mlstm_bwd", grid=(steps,),
        in_specs=_mlstm_specs(T, rev) + [
            pl.BlockSpec((2, H, 128, 128), lambda s: (rev(s), 0, 0, 0)),
            pl.BlockSpec((2, H, 1, 128), lambda s: (rev(s), 0, 0, 0)),
            pl.BlockSpec((2, H, 1, 128), lambda s: (rev(s), 0, 0, 0)),
            pl.BlockSpec((STEP_ROWS, 512), lambda s: (rev(s), 0))],
        out_specs=[pl.BlockSpec((STEP_ROWS, 1024), lambda s: (rev(s), 0)),
                   pl.BlockSpec((STEP_ROWS, 512), lambda s: (rev(s), 0)),
                   pl.BlockSpec((STEP_ROWS, 128), lambda s: (rev(s), 0)),
                   pl.BlockSpec((8, STEP_ROWS), lambda s: (0, rev(s)))],
        out_shape=[_sds((T, 1024), F32), _sds((T, 512), BF16), _sds((T, 128), F32), _sds((8, T), F32)],
        scratch_shapes=[pltpu.VMEM((H, 128, 128), F32), pltpu.VMEM((H, 1, 128), F32)],
        compiler_params=_params(("arbitrary",)))(qk, qk, pm, gcol, bcol, grow, brow, cs, ns, ms, dh)


def _rows_to_lanes(x):
    eye = (lax.broadcasted_iota(jnp.int32, (8, 128), 0)
           == lax.broadcasted_iota(jnp.int32, (8, 128), 1)).astype(BF16)
    out, rest = None, x
    for _ in range(3):
        piece = rest.astype(BF16)
        rest = rest - piece.astype(F32)
        t = _dot(piece, eye, 0, 0)
        out = t if out is None else out + t
    return out


def _gate_bwd(dgc, dgr, gcol, bcol):
    T = dgc.shape[0]
    tr = min(ROW_TILE, T)

    def body(a_ref, b_ref, g_ref, bias_ref, o_ref, acc_ref):
        i = pl.program_id(0)

        @pl.when(i == 0)
        def _():
            acc_ref[...] = jnp.zeros_like(acc_ref)

        d = a_ref[...] + _rows_to_lanes(b_ref[:, pl.ds(pl.multiple_of(i * tr, 128), tr)])
        lane = lax.broadcasted_iota(jnp.int32, d.shape, 1)
        is_f = (lane >= MLSTM_HEADS) & (lane < 2 * MLSTM_HEADS)
        dlogf = _chunk_rev_cumsum(jnp.where(is_f, d, 0.0), 0)
        out = jnp.where(is_f, dlogf * _sigmoid(-(g_ref[...] + bias_ref[...])), d)
        o_ref[...] = out.astype(BF16)
        acc_ref[0:1, :] += _colsum(out)

    return _rows("gate_bwd", body, [dgc, dgr, gcol, bcol],
                 [_sds((T, 128), BF16), _sds((8, 128), F32)], T, tr=tr)


def _head_norm(h, mu_axis=-1):
    mu = jnp.mean(h, axis=-1, keepdims=True)
    hc = h - mu
    r = lax.rsqrt(jnp.mean(hc * hc, axis=-1, keepdims=True) + NORM_EPS)
    return hc * r, r


def _mlstm_out_bwd_rows(ps, es):
    hm, vo, w_all = es
    D, width = MLSTM_HEAD_DIM, MLSTM_HEADS * MLSTM_HEAD_DIM
    dhs, dos, dws = [], [], []
    for hd in range(MLSTM_HEADS):
        cols = slice(D * hd, D * hd + D)
        hn, r = _head_norm(hm[:, cols])
        sg = _sigmoid(vo[:, width + D * hd:width + D * hd + D].astype(F32))
        dy, w = ps[0][:, cols], w_all[:, cols]
        dos.append(dy * hn * w * sg * (1.0 - sg))
        dyn = dy * sg
        dws.append(_colsum(dyn * hn))
        dhn = dyn * w
        dhs.append(r * (dhn - jnp.mean(dhn, axis=-1, keepdims=True)
                        - hn * jnp.mean(dhn * hn, axis=-1, keepdims=True)))
    cat = lambda parts: jnp.concatenate(parts, axis=1)
    return [cat(dhs), cat(dos)], [_acc_rows([cat(dws)])]


ADAM_TILE_ELEMS = 512 * 1024


def _adamw(name, w, g, m, v):
    R, C = w.shape
    fits = [t for t in range(8, R + 1, 8) if R % t == 0 and t * C <= ADAM_TILE_ELEMS]
    if fits or R * C <= ADAM_TILE_ELEMS:
        tr = fits[-1] if fits else R
        spec, grid = pl.BlockSpec((tr, C), lambda i: (i, 0)), (R // tr,)
    else:
        spec, grid = pl.BlockSpec((R, 128), lambda i: (0, i)), (C // 128,)
    c1 = 1.0 - ADAM_B1 ** ADAM_STEP
    c2 = 1.0 - ADAM_B2 ** ADAM_STEP

    def body(w_ref, g_ref, m_ref, v_ref, d_ref, mo_ref, vo_ref):
        g = g_ref[...]
        m = ADAM_B1 * m_ref[...] + (1.0 - ADAM_B1) * g
        v = ADAM_B2 * v_ref[...] + (1.0 - ADAM_B2) * (g * g)
        mo_ref[...] = m
        vo_ref[...] = v
        d_ref[...] = -ADAM_LR * ((m / c1) / (jnp.sqrt(v / c2) + ADAM_EPS) + ADAM_WD * w_ref[...])

    return pl.pallas_call(
        body, name=name, grid=grid, in_specs=[spec] * 4, out_specs=[spec] * 3,
        out_shape=[_sds((R, C), F32)] * 3, compiler_params=_params(("parallel",)))(w, g, m, v)


def _place():
    return lax.axis_index("x"), lax.axis_index("y"), lax.axis_index("c")


def _all_gather8(name, blk, space):
    m, n = blk.shape

    def body(x_ref, out_ref, send_sems, recv_sems, local_sem):
        x, y, c = _place()
        me, sibling = (x, y, c), (x, y, 1 - c)
        chips = [(1 - x, y), (x, 1 - y), (1 - x, 1 - y)]

        def rows(px, py, pc):
            return out_ref.at[pl.ds((4 * px + 2 * py + pc) * m, m), :]

        def copy(k, block, to, src=None):
            return pltpu.make_async_remote_copy(
                src_ref=rows(*block) if src is None else src, dst_ref=rows(*block),
                send_sem=send_sems.at[k], recv_sem=recv_sems.at[k],
                device_id=to, device_id_type=MESH)

        mine = pltpu.make_async_copy(x_ref, rows(*me), local_sem)
        mine.start()
        first = [copy(0, me, sibling, src=x_ref)]
        first += [copy(1 + j, me, (*chip, c), src=x_ref) for j, chip in enumerate(chips)]
        for cp in first:
            cp.start()
        passed = [copy(4 + j, (*chip, c), sibling) for j, chip in enumerate(chips)]
        for j, chip in enumerate(chips):
            copy(1 + j, (*chip, c), me).wait_recv()
            passed[j].start()
        copy(0, sibling, me).wait_recv()
        for j, chip in enumerate(chips):
            copy(4 + j, (*chip, 1 - c), me).wait_recv()
        for cp in first + passed:
            cp.wait_send()
        mine.wait()

    return pl.pallas_call(
        body, name=name, out_shape=_sds((8 * m, n), blk.dtype),
        in_specs=[pl.BlockSpec(memory_space=space)], out_specs=pl.BlockSpec(memory_space=space),
        scratch_shapes=[pltpu.SemaphoreType.DMA((7,)), pltpu.SemaphoreType.DMA((7,)),
                        pltpu.SemaphoreType.DMA],
        compiler_params=pltpu.CompilerParams(vmem_limit_bytes=VMEM_LIMIT))(blk)


def _hbm_specs(n):
    return [pl.BlockSpec(memory_space=pl.ANY)] * n


def _swap_halves_sibling(name, srcs):
    nw = len(srcs)

    def body(*refs):
        src_refs, dst_refs, send_sems, recv_sems = refs[:nw], refs[nw:2 * nw], refs[2 * nw], refs[2 * nw + 1]
        x, y, c = _place()
        cps = [pltpu.make_async_remote_copy(
            src_ref=src_refs[w].at[pl.ds(0, 4), 1 - c], dst_ref=dst_refs[w],
            send_sem=send_sems.at[w], recv_sem=recv_sems.at[w], device_id=(x, y, 1 - c),
            device_id_type=MESH) for w in range(nw)]
        for cp in cps:
            cp.start()
        for cp in cps:
            cp.wait()

    return pl.pallas_call(
        body, name=name, out_shape=[_sds(s.shape[:1] + s.shape[2:], s.dtype) for s in srcs],
        in_specs=_hbm_specs(nw), out_specs=_hbm_specs(nw),
        scratch_shapes=[pltpu.SemaphoreType.DMA((nw,)), pltpu.SemaphoreType.DMA((nw,))])(*srcs)


def _split_start(name, srcs, lands, copies, per_array, after):
    nw = len(srcs)

    def body(*refs):
        send_sems, recv_sems, token = refs[2 * nw + 1], refs[2 * nw + 2], refs[-1]
        for w in range(nw):
            for k, (s, d, dev) in enumerate(copies(refs[w], refs[nw + w], *_place())):
                pltpu.make_async_remote_copy(
                    src_ref=s, dst_ref=d, send_sem=send_sems.at[w * per_array + k],
                    recv_sem=recv_sems.at[w * per_array + k], device_id=dev, device_id_type=MESH).start()
        token[...] = jnp.zeros_like(token)

    hbm, sem = pl.BlockSpec(memory_space=pltpu.HBM), pl.BlockSpec(memory_space=pltpu.SEMAPHORE)
    arrays = list(srcs) + list(lands)
    out = pl.pallas_call(
        body, name=name,
        out_shape=(pltpu.SemaphoreType.DMA((nw * per_array,)), pltpu.SemaphoreType.DMA((nw * per_array,)),
                   *[pltpu.HBM(a.shape, a.dtype) for a in arrays], _sds((8, 128), F32)),
        in_specs=[hbm] * (2 * nw) + [pl.BlockSpec(memory_space=pl.ANY)],
        out_specs=(sem, sem, *[hbm] * (2 * nw), pl.BlockSpec(memory_space=pltpu.VMEM)),
        input_output_aliases={i: 2 + i for i in range(2 * nw)},
        compiler_params=pltpu.CompilerParams(has_side_effects=pltpu.SideEffectType.DATAFLOW_SIDE_EFFECTING))(
            *[pltpu.with_memory_space_constraint(a, pltpu.HBM) for a in arrays], after)
    return out[0], out[1], out[2:2 + nw], out[2 + nw:2 + 2 * nw], out[-1]


def _split_wait(name, started, after, waits, per_array):
    send_sems, recv_sems, srcs, lands, _ = started
    nw = len(srcs)

    def body(*refs):
        send_sems, recv_sems = refs[2 * nw], refs[2 * nw + 1]
        x, y, c = _place()
        for w in range(nw):
            for k, (s, d) in enumerate(waits(refs[w], refs[nw + w], x, y, c)):
                cp = pltpu.make_async_remote_copy(
                    src_ref=s, dst_ref=d, send_sem=send_sems.at[w * per_array + k],
                    recv_sem=recv_sems.at[w * per_array + k], device_id=(x, y, 1 - c),
                    device_id_type=MESH)
                cp.wait_send()
                cp.wait_recv()

    hbm, sem = pl.BlockSpec(memory_space=pltpu.HBM), pl.BlockSpec(memory_space=pltpu.SEMAPHORE)
    arrays = list(srcs) + list(lands)
    out = pl.pallas_call(
        body, name=name, out_shape=tuple(pltpu.HBM(a.shape, a.dtype) for a in arrays),
        in_specs=[hbm] * (2 * nw) + [sem, sem, pl.BlockSpec(memory_space=pl.ANY)],
        out_specs=tuple([hbm] * (2 * nw)), input_output_aliases={i: i for i in range(2 * nw)},
        compiler_params=pltpu.CompilerParams(has_side_effects=pltpu.SideEffectType.DATAFLOW_SIDE_EFFECTING))(
            *arrays, send_sems, recv_sems, after)
    return list(out[nw:])


def _other_chips(x, y):
    return [(1 - x, y), (x, 1 - y), (1 - x, 1 - y)]


def _gather_sends(src_ref, land_ref, x, y, c):
    to = land_ref.at[2 * x + y, c]
    return [(src_ref, to, (x, y, 1 - c))] + [(src_ref, to, (px, py, c)) for px, py in _other_chips(x, y)]


def _gather_lands(src_ref, land_ref, x, y, c):
    return [(src_ref, land_ref.at[2 * x + y, 1 - c])] + [
        (src_ref, land_ref.at[2 * px + py, c]) for px, py in _other_chips(x, y)]


def _gather_sends_all(src_ref, land_ref, x, y, c):
    to = land_ref.at[2 * x + y, c]
    return [(src_ref, to, (x, y, 1 - c))] + [
        (src_ref, to, (px, py, pc)) for px, py in _other_chips(x, y) for pc in (c, 1 - c)]


def _gather_lands_all(src_ref, land_ref, x, y, c):
    return [(src_ref, land_ref.at[2 * x + y, 1 - c])] + [
        (src_ref, land_ref.at[2 * px + py, pc]) for px, py in _other_chips(x, y) for pc in (c, 1 - c)]


def _scatter_sends(src_ref, land_ref, x, y, c):
    return [(src_ref.at[2 * px + py], land_ref.at[2 * x + y], (px, py, c)) for px, py in _other_chips(x, y)]


def _scatter_lands(src_ref, land_ref, x, y, c):
    return [(src_ref.at[2 * x + y], land_ref.at[2 * px + py]) for px, py in _other_chips(x, y)]


def _forward_sibling(name, lands):
    nw = len(lands)

    def body(*refs):
        land_refs, out_refs, send_sems, recv_sems = refs[:nw], refs[nw:2 * nw], refs[2 * nw], refs[2 * nw + 1]
        x, y, c = _place()
        cps = []
        for w in range(nw):
            cps += [pltpu.make_async_remote_copy(
                src_ref=land_refs[w].at[2 * px + py, c], dst_ref=out_refs[w].at[2 * px + py, c],
                send_sem=send_sems.at[w, j], recv_sem=recv_sems.at[w, j], device_id=(x, y, 1 - c),
                device_id_type=MESH) for j, (px, py) in enumerate(_other_chips(x, y))]
        for cp in cps:
            cp.start()
        for w in range(nw):
            for j, (px, py) in enumerate(_other_chips(x, y)):
                slot = out_refs[w].at[2 * px + py, 1 - c]
                pltpu.make_async_remote_copy(src_ref=slot, dst_ref=slot, send_sem=send_sems.at[w, j],
                                             recv_sem=recv_sems.at[w, j], device_id=(x, y, 1 - c),
                                             device_id_type=MESH).wait_recv()
        for cp in cps:
            cp.wait_send()

    return pl.pallas_call(
        body, name=name, out_shape=[_sds(a.shape, a.dtype) for a in lands],
        in_specs=_hbm_specs(nw), out_specs=_hbm_specs(nw), input_output_aliases={i: i for i in range(nw)},
        scratch_shapes=[pltpu.SemaphoreType.DMA((nw, 3)), pltpu.SemaphoreType.DMA((nw, 3))])(*lands)


def _share_halves(name, halves):
    nw = len(halves)

    def body(*refs):
        in_refs, out_refs, send_sems, recv_sems = refs[:nw], refs[nw:2 * nw], refs[2 * nw], refs[2 * nw + 1]
        x, y, c = _place()
        cps = [pltpu.make_async_remote_copy(
            src_ref=in_refs[w].at[c], dst_ref=out_refs[w].at[c], send_sem=send_sems.at[w],
            recv_sem=recv_sems.at[w], device_id=(x, y, 1 - c), device_id_type=MESH) for w in range(nw)]
        for cp in cps:
            cp.start()
        for w in range(nw):
            slot = out_refs[w].at[1 - c]
            pltpu.make_async_remote_copy(src_ref=slot, dst_ref=slot, send_sem=send_sems.at[w],
                                         recv_sem=recv_sems.at[w], device_id=(x, y, 1 - c),
                                         device_id_type=MESH).wait_recv()
        for cp in cps:
            cp.wait_send()

    return pl.pallas_call(
        body, name=name, out_shape=[_sds(a.shape, a.dtype) for a in halves],
        in_specs=_hbm_specs(nw), out_specs=_hbm_specs(nw), input_output_aliases={i: i for i in range(nw)},
        scratch_shapes=[pltpu.SemaphoreType.DMA((nw,)), pltpu.SemaphoreType.DMA((nw,))])(*halves)


def _place_blocks(name, blks, place):
    nw = len(blks)

    def body(p_ref, *refs):
        for b_ref, o_ref in zip(refs[:nw], refs[nw:]):
            o_ref[...] = b_ref[...]

    return pl.pallas_call(
        body, name=name,
        grid_spec=pltpu.PrefetchScalarGridSpec(
            num_scalar_prefetch=1, grid=(1,),
            in_specs=[pl.BlockSpec(b.shape, lambda i, p: (0, 0)) for b in blks],
            out_specs=[pl.BlockSpec((None, None) + b.shape, lambda i, p: (p[0], p[1], 0, 0)) for b in blks]),
        out_shape=[_sds((4, 2) + b.shape, b.dtype) for b in blks],
        compiler_params=_params(("arbitrary",)))(place, *blks)


def _pair_sum(name, fulls, gots, place):
    nw = len(fulls)

    def body(p_ref, *refs):
        s = pl.program_id(0)
        for a_ref, b_ref, o_ref, l_ref in zip(refs[:nw], refs[nw:2 * nw], refs[2 * nw:3 * nw], refs[3 * nw:]):
            o_ref[...] = (a_ref[...].astype(F32) + b_ref[...].astype(F32)).astype(o_ref.dtype)

            @pl.when(s == p_ref[0])
            def _():
                l_ref[...] = o_ref[...]

    slab = lambda a: pl.BlockSpec((None,) + a.shape[1:], lambda s, p: (s, 0, 0))
    mine = lambda a: pl.BlockSpec((None,) + a.shape[1:], lambda s, p: (p[0], 0, 0))
    out = pl.pallas_call(
        body, name=name,
        grid_spec=pltpu.PrefetchScalarGridSpec(
            num_scalar_prefetch=1, grid=(4,),
            in_specs=[pl.BlockSpec((None, None) + a.shape[2:], lambda s, p: (s, p[1], 0, 0)) for a in fulls]
            + [slab(b) for b in gots],
            out_specs=[slab(b) for b in gots] + [mine(b) for b in gots]),
        out_shape=[_sds(b.shape, BF16) for b in gots] * 2,
        compiler_params=_params(("arbitrary",)))(place, *fulls, *gots)
    return out[:nw], out[nw:]


def _sum4(name, arrs, place):
    nw = len(arrs)

    def body(p_ref, *refs):
        for a_ref, o_ref in zip(refs[:nw], refs[nw:]):
            acc = a_ref[0].astype(F32)
            for s in range(1, 4):
                acc = acc + a_ref[s].astype(F32)
            o_ref[...] = acc

    return pl.pallas_call(
        body, name=name,
        grid_spec=pltpu.PrefetchScalarGridSpec(
            num_scalar_prefetch=1, grid=(1,),
            in_specs=[pl.BlockSpec(a.shape, lambda i, p: (0, 0, 0)) for a in arrs],
            out_specs=[pl.BlockSpec((None,) + a.shape[1:], lambda i, p: (p[1], 0, 0)) for a in arrs]),
        out_shape=[_sds((2,) + a.shape[1:], F32) for a in arrs],
        compiler_params=_params(("arbitrary",)))(place, *arrs)


def _small_update(gathered, params, slots):
    c1 = 1.0 - ADAM_B1 ** ADAM_STEP
    c2 = 1.0 - ADAM_B2 ** ADAM_STEP
    n = len(params)

    def body(g_ref, *refs):
        ins, sum_ref, outs = refs[:3 * n], refs[3 * n], refs[3 * n + 1:]
        g_all = g_ref[0:1, :]
        for d in range(1, 8):
            g_all = g_all + g_ref[d:d + 1, :]
        sum_ref[...] = g_all
        for k, (off, width) in enumerate(slots):
            w_ref, m_ref, v_ref = ins[3 * k:3 * k + 3]
            go_ref, d_ref, mo_ref, vo_ref = outs[4 * k:4 * k + 4]
            g = g_all[:, off:off + width]
            m = ADAM_B1 * m_ref[...] + (1.0 - ADAM_B1) * g
            v = ADAM_B2 * v_ref[...] + (1.0 - ADAM_B2) * (g * g)
            go_ref[...], mo_ref[...], vo_ref[...] = g, m, v
            d_ref[...] = -ADAM_LR * ((m / c1) / (jnp.sqrt(v / c2) + ADAM_EPS) + ADAM_WD * w_ref[...])

    flat = [a for p in params for a in p]
    out = pl.pallas_call(
        body, name="small_update",
        out_shape=[_sds((1, gathered.shape[1]), F32)] + [_sds(p[0].shape, F32) for p in params for _ in range(4)],
        compiler_params=pltpu.CompilerParams(vmem_limit_bytes=VMEM_LIMIT))(gathered, *flat)
    return out[0], [tuple(out[1 + 4 * k:5 + 4 * k]) for k in range(n)]


def _swiglu(ps, es):
    g, u = ps
    return g * _sigmoid(g) * u, g, u


def _swiglu_bwd(ps, es):
    g, u = es[0].astype(F32), es[1].astype(F32)
    sg = _sigmoid(g)
    return ps[0] * u * (sg * (1.0 + g * (1.0 - sg))), ps[0] * (g * sg)


def _merge(ps, es):
    ga, gm = [e.astype(F32) for e in es]
    return (_sigmoid(ga) * ps[0] + _sigmoid(gm) * ps[1],)


def _merge_bwd(ps, es):
    dm, a, b = ps
    ga, gm = [e.astype(F32) for e in es]
    sa, sm = _sigmoid(ga), _sigmoid(gm)
    return dm * sa, dm * sm, dm * a * (sa * (1.0 - sa)), dm * b * (sm * (1.0 - sm))


W_IN_PIECES = (("q", 512), ("kv", 256), ("mqk", 1024), ("mv", 512), ("mo", 512), ("if", 8),
               ("ga", 1024), ("gm", 1024))


def _local_step(x, tgt, pos_col, mod, sp, in_weights, late_weights, ffn_grads, mixer_grads):
    sh_m, sc_m, gate_m, sh_f, sc_f, gate_f = mod
    inv = ROPE_THETA ** (-2.0 * jnp.arange(HEAD_DIM // 2, dtype=F32) / HEAD_DIM)
    cos, sin = _rope_tables(pos_col, jnp.tile(inv, 4).reshape(1, 128))
    W = dict(in_weights(cos))
    h, pa, pqk, pvo, pif, pg = _proj_in(x, sp["g_pre_mix"], sc_m, sh_m, [
        (W["q+kv"], F32, 256), (W["mqk"], F32, 512), (W["mv+mo"], BF16, 512), (W["if"], F32, 128),
        (W["ga+gm"], BF16, 512)])
    ya = _attn_fwd(pa, cos, sin, sp["sinks"])
    qk = _conv_fwd(pqk, sp["conv_w"], sp["conv_b"])
    bcol = jnp.pad(sp["b_if"], ((0, 0), (0, 120)))
    brow = jnp.broadcast_to(sp["b_if"].reshape(8, 1), (8, 128))
    grow = pif[:, :8].T
    hm, ym, cs, ns, ms = _mlstm_fwd(qk, pvo, pif, bcol, grow, brow, sp["norm_w"])
    W.update(late_weights(ym))
    w_fg, w_fu, w_fd = W["fg"], W["fu"], W["fd"]
    merged, = _mm("branches", [[(ya, W["ba"])], [(ym, W["bm"])]], [(pg, 0), (pg, 1)], _merge, [BF16],
                  cn=512, nt=True)
    wide, narrow = (D_MODEL, F32), (D_MODEL, BF16)
    mix, x1, h2 = _mm_rows("mix_out", [[(merged, W["out"])]],
                           [x, gate_m, sp["g_post_mix"], sp["g_pre_ffn"], sc_f, sh_f],
                           _res_norm_rows, [wide, wide, narrow], [], cn=512)
    act, gt, up = _mm("ffn_in", [[(h2, w_fg)], [(h2, w_fu)]], [], _swiglu, [BF16] * 3,
                      cn=256, nt=True)
    dy, dff, acc_l, loss = _mm_rows("ffn_down", [[(act, w_fd)]], [x1, tgt, gate_f, sp["g_post_ffn"]],
                                    _final_loss_rows, [wide, narrow], [(8, D_MODEL), (1, 128)], cn=512)

    G = {}
    dgt, dup = _mm("ffn_down_bwd", [[(dff, w_fd)]], [gt, up], _swiglu_bwd, [BF16, BF16],
                   cn=256, nt=True)
    g_fd, = _mm_tn_group("dw_ffn_down", [act], dff, BF16)
    g_fg, = _mm_tn_group("dw_ffn_gate", [dgt], h2, BF16)
    g_fu, = _mm_tn_group("dw_ffn_up", [dup], h2, BF16)
    tie = ffn_grads(g_fg, g_fu, g_fd)
    dx1, dmix, acc_r = _mm_rows(
        "ffn_in_bwd", [[(dgt, w_fg), (dup, w_fu)]],
        [x1, mix, dy, sc_f + tie, gate_m, sp["g_pre_ffn"], sp["g_post_mix"]],
        _res_norm_bwd_rows, [wide, narrow], [(8, D_MODEL)], cn=512, tm=256)
    d_a, d_m, dga, dgm = _mm("mix_out_bwd", [[(dmix, W["out"])], [(ya, W["ba"])], [(ym, W["bm"])]],
                             [(pg, 0), (pg, 1)], _merge_bwd, [BF16] * 4, cn=512, nt=True)
    G["out"], = _mm_tn_group("dw_out", [merged], dmix, BF16)
    dya, = _mm("branch_attn_bwd", [[(d_a, W["ba"])]], [], _first, [F32], cn=512)
    heads = MLSTM_HEADS * MLSTM_HEAD_DIM
    dhm, do_m, acc_n = _mm_rows("branch_mlstm_bwd", [[(d_m, W["bm"])]], [hm, pvo, sp["norm_w"]],
                                _mlstm_out_bwd_rows, [(heads, F32), (heads, BF16)], [(8, heads)], cn=512)
    G["ba"], = _mm_tn_group("dw_branch_attn", [d_a], ya, BF16)
    G["bm"], = _mm_tn_group("dw_branch_mlstm", [d_m], ym, BF16)
    dqk, dv_m, dgc, dgr = _mlstm_bwd(qk, pvo, pif, bcol, grow, brow, cs, ns, ms, dhm)
    dif, acc_g = _gate_bwd(dgc, dgr, pif, bcol)
    du, acc_c = _conv_bwd(pqk, sp["conv_w"], sp["conv_b"], dqk)
    dq_a, dkv, dsink = _attn_bwd(pa, cos, sin, sp["sinks"], dya)
    dproj = {"q": dq_a, "kv": dkv, "mqk": du, "mv": dv_m, "mo": do_m, "if": dif, "ga": dga, "gm": dgm}
    names = [k for k, _ in W_IN_PIECES]
    for part in (names[:4], names[4:]):
        G.update(zip(part, _mm_tn_group("dw_in_from_" + part[0], [dproj[k] for k in part], h, BF16)))
    w_tied = dict(W, **{"if": W["if"] + mixer_grads(G).astype(BF16)})
    dx, acc_p = _mm_rows("proj_bwd", [[(dproj[k], w_tied[k]) for k, _ in W_IN_PIECES]],
                         [x, dx1, sp["g_pre_mix"], sc_m], _pre_norm_bwd_rows, [wide], [(8, D_MODEL)], cn=512)

    small = {
        "mod": jnp.concatenate([acc_p[1], acc_p[0], acc_r[3], acc_r[1], acc_r[0], acc_l[0]]),
        "g_pre_mix": acc_p[2], "g_post_mix": acc_r[4], "b_if": acc_g[0, :8],
        "conv_w": acc_c[:CONV_WIDTH].reshape(-1), "conv_b": acc_c[CONV_WIDTH],
        "sinks": dsink[:, 0], "norm_w": acc_n[0], "g_pre_ffn": acc_r[2], "g_post_ffn": acc_l[1]}
    return loss, dx, small


IN_WIDTH = sum(n for _, n in W_IN_PIECES)
IN_SHARD = IN_WIDTH // 4
IN_SHARD_PAD = -(-IN_SHARD // 32) * 32


def _split_w_in(w_in_t):
    out, off, start = {}, 0, {}
    for k, n in W_IN_PIECES:
        out[k], start[k] = w_in_t[off:off + n], off
        off += n
    out["if"] = jnp.pad(out["if"], ((0, 120), (0, 0)))
    for name, first, last in (("q+kv", "q", "kv"), ("mv+mo", "mv", "mo"), ("ga+gm", "ga", "gm")):
        out[name] = w_in_t[start[first]:start[last] + out[last].shape[0]]
    return out


def _halves(a):
    return a.reshape(4, 2, a.shape[0] // 8, a.shape[1])


SMALL = (("b_ada", 6144), ("g_pre_mix", 1024), ("g_post_mix", 1024), ("b_if", 128), ("conv_w", 4096),
         ("conv_b", 1024), ("sinks", 128), ("norm_w", 512), ("g_pre_ffn", 1024), ("g_post_ffn", 1024))
SMALL_LEN = 8 * 2048


def _pack_small(vals):
    parts = []
    for k, n in SMALL:
        v = vals[k].reshape(-1)
        parts.append(jnp.pad(v, (0, n - v.shape[0])))
    flat = jnp.concatenate(parts)
    return jnp.pad(flat, (0, SMALL_LEN - flat.shape[0]))


def kernel(x, c, positions, w_ada, b_ada, g_pre_mix, g_post_mix, w_in, b_if, conv_w, conv_b, attn_sinks, mlstm_norm_w, w_branch_attn, w_branch_mlstm, w_out, g_pre_ffn, g_post_ffn, w_ffn_gate, w_ffn_up, w_ffn_down, loss_target, m_w_ada, m_b_ada, m_g_pre_mix, m_g_post_mix, m_w_in, m_b_if, m_conv_w, m_conv_b, m_attn_sinks, m_mlstm_norm_w, m_w_branch_attn, m_w_branch_mlstm, m_w_out, m_g_pre_ffn, m_g_post_ffn, m_w_ffn_gate, m_w_ffn_up, m_w_ffn_down, v_w_ada, v_b_ada, v_g_pre_mix, v_g_post_mix, v_w_in, v_b_if, v_conv_w, v_conv_b, v_attn_sinks, v_mlstm_norm_w, v_w_branch_attn, v_w_branch_mlstm, v_w_out, v_g_pre_ffn, v_g_post_ffn, v_w_ffn_gate, v_w_ffn_up, v_w_ffn_down):
    xi, yi, ci = _place()
    chip = 2 * xi + yi
    dev = 2 * chip + ci
    T = x.shape[1]
    ada_cols = w_ada.shape[2]

    place = jnp.stack([chip, ci]).astype(jnp.int32)

    def my_half(a):
        n = a.shape[0] // 2
        return lax.dynamic_slice_in_dim(a, ci * n, n, axis=0).astype(BF16)

    blk = jnp.concatenate([c.reshape(-1), conv_w.reshape(-1)]).reshape(8, 256)
    got = _all_gather8("gather_cond", blk, pltpu.VMEM).reshape(8, 2048)
    c_all = got[:, :D_MODEL].astype(BF16)
    conv_full = got[::2, D_MODEL:].reshape(4, CONV_WIDTH, -1).transpose(1, 0, 2).reshape(CONV_WIDTH, -1)

    b_sh = lax.dynamic_slice_in_dim(b_ada, chip * ada_cols, ada_cols, axis=1)
    mod_part, = _mm("ada_mod", [[(c_all, w_ada[0].astype(BF16))]], [b_sh],
                    lambda ps, es: (ps[0] + es[0],), [F32], cn=512, tm=8)
    mod_all = _all_gather8("gather_mod", mod_part, pltpu.VMEM).reshape(4, 2, 8, ada_cols)[:, 0]
    mod = lax.dynamic_index_in_dim(mod_all, dev, axis=1, keepdims=False).reshape(6, 1, D_MODEL)

    def gather_start(name, blks, after, sends, copies):
        return _split_start(name + "_start", blks, _place_blocks(name + "_place", blks, place),
                            sends, copies, after)

    w_in_t = jnp.pad(w_in[0].T, ((0, IN_SHARD_PAD - IN_SHARD), (0, 0)))
    in_started = gather_start("in_gather", [my_half(w_in_t)], mod, _gather_sends, 4)
    late_keys = ("fg", "fu", "fd", "out", "ba", "bm")
    late_started = gather_start(
        "late_gather",
        [my_half(w_ffn_gate[0].T), my_half(w_ffn_up[0].T), my_half(w_ffn_down[0]), my_half(w_out[0]),
         my_half(w_branch_attn[0].T), my_half(w_branch_mlstm[0].T)], in_started[4], _gather_sends_all, 7)
    mod = mod + (in_started[4][0, 0] + late_started[4][0, 0])

    def in_weights(after):
        g_in, = _forward_sibling("in_gather_forward",
                                 _split_wait("in_gather_wait", in_started, after, _gather_lands, 4))
        return _split_w_in(g_in.reshape(4, IN_SHARD_PAD, D_MODEL)[:, :IN_SHARD].reshape(IN_WIDTH, D_MODEL))

    def late_weights(after):
        lands = _split_wait("late_gather_wait", late_started, after, _gather_lands_all, 7)
        return {k: a.reshape(-1, a.shape[-1]) for k, a in zip(late_keys, lands)}

    sent = {}

    def scatter_start(name, groups):
        pairs, lands = _pair_sum(name + "_pair_sum", groups, _swap_halves_sibling(name + "_pair", groups), place)
        sent[name] = _split_start(name + "_start", pairs, lands, _scatter_sends, 3, pairs[0])
        return sent[name][4][0, 0]

    def ffn_grads(g_fg, g_fu, g_fd):
        return scatter_start("rs_ffn", [_halves(g_fg), _halves(g_fu), _halves(g_fd)])

    def mixer_grads(G):
        g_in_t = jnp.concatenate([G[k][:n] for k, n in W_IN_PIECES]).reshape(4, IN_SHARD, D_MODEL)
        g_in_t = jnp.pad(g_in_t, ((0, 0), (0, IN_SHARD_PAD - IN_SHARD), (0, 0)))
        return scatter_start("rs_mix", [g_in_t.reshape(4, 2, IN_SHARD_PAD // 2, D_MODEL), _halves(G["out"]),
                                        _halves(G["ba"]), _halves(G["bm"])])

    sp = {"g_pre_mix": g_pre_mix, "g_post_mix": g_post_mix, "b_if": b_if, "conv_w": conv_full,
          "conv_b": conv_b, "sinks": attn_sinks, "norm_w": mlstm_norm_w, "g_pre_ffn": g_pre_ffn,
          "g_post_ffn": g_post_ffn}
    loss, dx, small = _local_step(x[0], loss_target[0], positions.reshape(T, 1), [mod[i] for i in range(6)],
                                  sp, in_weights, late_weights, ffn_grads, mixer_grads)

    reds = (_sum4("rs_ffn_chip_sum", _split_wait("rs_ffn_wait", sent["rs_ffn"], dx, _scatter_lands, 3), place)
            + _sum4("rs_mix_chip_sum", _split_wait("rs_mix_wait", sent["rs_mix"], dx, _scatter_lands, 3), place))
    gsh = {k: s.reshape(-1, s.shape[-1])
           for k, s in zip(("fg", "fu", "fd", "w_in", "out", "ba", "bm"), _share_halves("rs_share", reds))}
    gsh["w_in"] = gsh["w_in"][:IN_SHARD]

    small["b_ada"] = small.pop("mod")
    vec = _pack_small(small).reshape(8, 2048)
    g_all = _all_gather8("gather_small", vec, pltpu.VMEM).reshape(8, SMALL_LEN)
    dmod_sh = lax.dynamic_slice_in_dim(g_all[:, :6 * D_MODEL], chip * ada_cols, ada_cols, axis=1)
    g_w_ada, = _mm_tn_group("dw_ada", [c_all], dmod_sh.astype(BF16), F32)

    smalls = {"b_ada": (b_ada, m_b_ada, v_b_ada), "g_pre_mix": (g_pre_mix, m_g_pre_mix, v_g_pre_mix),
              "g_post_mix": (g_post_mix, m_g_post_mix, v_g_post_mix), "b_if": (b_if, m_b_if, v_b_if),
              "conv_b": (conv_b, m_conv_b, v_conv_b), "sinks": (attn_sinks, m_attn_sinks, v_attn_sinks),
              "norm_w": (mlstm_norm_w, m_mlstm_norm_w, v_mlstm_norm_w),
              "g_pre_ffn": (g_pre_ffn, m_g_pre_ffn, v_g_pre_ffn),
              "g_post_ffn": (g_post_ffn, m_g_post_ffn, v_g_post_ffn)}
    offsets, off = {}, 0
    for k, width in SMALL:
        offsets[k], off = off, off + width
    g_sum, updates = _small_update(g_all, list(smalls.values()),
                                   [(offsets[k], t[0].shape[1]) for k, t in smalls.items()])
    g_conv = g_sum[:, offsets["conv_w"]:offsets["conv_w"] + CONV_WIDTH * D_MODEL].reshape(1, CONV_WIDTH, D_MODEL)
    g_conv = lax.dynamic_slice_in_dim(g_conv, chip * conv_w.shape[2], conv_w.shape[2], axis=2)

    res = dict(zip(smalls, updates))
    res["conv_w"] = (g_conv, *[o[None] for o in _adamw("adam_conv_w", conv_w[0], g_conv[0], m_conv_w[0], v_conv_w[0])])
    res["w_ada"] = (g_w_ada[None], *[o[None] for o in _adamw("adam_w_ada", w_ada[0], g_w_ada, m_w_ada[0], v_w_ada[0])])
    bigs = {"w_in": (w_in, m_w_in, v_w_in), "ba": (w_branch_attn, m_w_branch_attn, v_w_branch_attn),
            "bm": (w_branch_mlstm, m_w_branch_mlstm, v_w_branch_mlstm), "out": (w_out, m_w_out, v_w_out),
            "fg": (w_ffn_gate, m_w_ffn_gate, v_w_ffn_gate), "fu": (w_ffn_up, m_w_ffn_up, v_w_ffn_up),
            "fd": (w_ffn_down, m_w_ffn_down, v_w_ffn_down)}
    for k, (w, m, v) in bigs.items():
        if k in ("w_in", "fg", "fu"):
            res[k] = tuple(o.T[None] for o in (gsh[k], *_adamw("adam_" + k, w[0].T, gsh[k], m[0].T, v[0].T)))
        else:
            g = gsh[k].T if k in ("ba", "bm") else gsh[k]
            res[k] = (g[None], *[o[None] for o in _adamw("adam_" + k, w[0], g, m[0], v[0])])

    order = ("w_ada", "b_ada", "g_pre_mix", "g_post_mix", "w_in", "b_if", "conv_w", "conv_b", "sinks",
             "norm_w", "ba", "bm", "out", "g_pre_ffn", "g_post_ffn", "fg", "fu", "fd")
    total = lax.psum(loss[0, 0], ("x", "y", "c"))
    return (total, dx[None], *[res[k][0] for k in order], *[res[k][1] for k in order],
            *[res[k][2] for k in order], *[res[k][3] for k in order])
```

```python
import functools

import jax
import jax.numpy as jnp
from jax import lax
from jax.experimental import pallas as pl
from jax.experimental.pallas import tpu as pltpu

F32, BF16 = jnp.float32, jnp.bfloat16
MESH = pl.DeviceIdType.MESH

D_MODEL = 1024
N_Q_HEADS, N_KV_HEADS, HEAD_DIM, WINDOW = 8, 2, 64, 128
ROPE_THETA = 10000.0
MLSTM_HEADS, MLSTM_HEAD_DIM, MLSTM_CHUNK, CONV_WIDTH = 4, 128, 64, 4
D_FF = 2816
NORM_EPS = 1e-6
ADAM_LR, ADAM_B1, ADAM_B2, ADAM_EPS, ADAM_WD, ADAM_STEP = 0.001, 0.9, 0.999, 1e-08, 0.01, 10

VMEM_LIMIT = 56 * 1024 * 1024
ROW_TILE = 256
MM_TM = 512
MM_TT = 1024
ATTN_BLK = WINDOW
STEP_ROWS = 2 * MLSTM_CHUNK
NEG_INF = float("-inf")


def _params(sem):
    return pltpu.CompilerParams(dimension_semantics=sem, vmem_limit_bytes=VMEM_LIMIT)


def _sds(shape, dtype):
    return jax.ShapeDtypeStruct(shape, dtype)


def _sigmoid(x):
    return 1.0 / (1.0 + jnp.exp(-x))


def _dot(a, b, ca, cb):
    return lax.dot_general(a, b, (((ca,), (cb,)), ((), ())), preferred_element_type=F32)


def _bdot(a, b, ca, cb):
    return lax.dot_general(a, b, (((ca,), (cb,)), ((0,), (0,))), preferred_element_type=F32)


def _bdot_rows(a, b):
    return jnp.stack([_dot(a[h], b[h], 0, 0) for h in range(a.shape[0])])


def _mm(name, prods, extras, epi, out_dtypes, cn, nt=False, tm=MM_TM):
    flat = [ab for p in prods for ab in p]
    counts = [len(p) for p in prods]
    M = flat[0][0].shape[0]
    N = flat[0][1].shape[0 if nt else 1]
    tm = min(tm, M)
    n_in = 2 * len(flat) + len(extras)

    def body(*refs):
        ins, outs = refs[:n_in], refs[n_in:]
        for j in range(N // cn):
            cols = slice(j * cn, (j + 1) * cn)
            k, ps = 0, []
            for cnt in counts:
                acc = None
                for _ in range(cnt):
                    b = ins[k + 1][cols, :] if nt else ins[k + 1][:, cols]
                    d = _dot(ins[k][...], b, 1, 1 if nt else 0)
                    acc = d if acc is None else acc + d
                    k += 2
                ps.append(acc)
            res = epi(ps, [r[:, cols] for r in ins[k:]])
            for o, r in zip(outs, res):
                o[:, cols] = r.astype(o.dtype)

    in_specs, args = [], []
    for a, b in flat:
        in_specs.append(pl.BlockSpec((tm, a.shape[1]), lambda i: (i, 0)))
        in_specs.append(pl.BlockSpec(b.shape, lambda i: (0, 0), pipeline_mode=pl.Buffered(1)))
        args += [a, b]
    for e in extras:
        e, off = e if isinstance(e, tuple) else (e, 0)
        rows = 1 if e.shape[0] == 1 else tm
        in_specs.append(pl.BlockSpec((rows, N), lambda i, off=off, rows=rows: (0 if rows == 1 else i, off)))
        args.append(e)
    return pl.pallas_call(
        body, name=name, grid=(M // tm,), in_specs=in_specs,
        out_specs=[pl.BlockSpec((tm, N), lambda i: (i, 0)) for _ in out_dtypes],
        out_shape=[_sds((M, N), dt) for dt in out_dtypes],
        compiler_params=_params(("parallel",)))(*args)


def _mm_rows(name, prods, extras, epi, outs, accs, cn, nt=False, tm=MM_TM):
    flat = [ab for p in prods for ab in p]
    counts = [len(p) for p in prods]
    M = flat[0][0].shape[0]
    N = flat[0][1].shape[0 if nt else 1]
    tm = min(tm, M)
    n_mm, n_in, n_out = 2 * len(flat), 2 * len(flat) + len(extras), len(outs)

    def body(*refs):
        ins, out_refs, acc_refs = refs[:n_in], refs[n_in:n_in + n_out], refs[n_in + n_out:]

        @pl.when(pl.program_id(0) == 0)
        def _():
            for a in acc_refs:
                a[...] = jnp.zeros_like(a)

        chunks = [[] for _ in counts]
        for j in range(N // cn):
            cols = slice(j * cn, (j + 1) * cn)
            k = 0
            for p, cnt in enumerate(counts):
                acc = None
                for _ in range(cnt):
                    b = ins[k + 1][cols, :] if nt else ins[k + 1][:, cols]
                    d = _dot(ins[k][...], b, 1, 1 if nt else 0)
                    acc = d if acc is None else acc + d
                    k += 2
                chunks[p].append(acc)
        ps = [c[0] if len(c) == 1 else jnp.concatenate(c, axis=1) for c in chunks]
        res, incs = epi(ps, [r[...] for r in ins[n_mm:]])
        for o, r in zip(out_refs, res):
            o[...] = r.astype(o.dtype)
        for a, inc in zip(acc_refs, incs):
            a[...] += inc

    in_specs, args = [], []
    for a, b in flat:
        in_specs.append(pl.BlockSpec((tm, a.shape[1]), lambda i: (i, 0)))
        in_specs.append(pl.BlockSpec(b.shape, lambda i: (0, 0), pipeline_mode=pl.Buffered(1)))
        args += [a, b]
    for e in extras:
        rows = 1 if e.shape[0] == 1 else tm
        in_specs.append(pl.BlockSpec((rows, e.shape[1]), lambda i, rows=rows: (0 if rows == 1 else i, 0)))
        args.append(e)
    return pl.pallas_call(
        body, name=name, grid=(M // tm,), in_specs=in_specs,
        out_specs=[pl.BlockSpec((tm, w), lambda i: (i, 0)) for w, _ in outs]
        + [pl.BlockSpec(s, lambda i: (0, 0)) for s in accs],
        out_shape=[_sds((M, w), dt) for w, dt in outs] + [_sds(s, F32) for s in accs],
        compiler_params=_params(("arbitrary",)))(*args)


def _mm_tn_group(name, pieces, b, out_dtype, tt=MM_TT):
    T, N = b.shape
    tt = min(tt, T)
    steps, n = T // tt, len(pieces)

    def body(*refs):
        a_refs, b_ref, out_refs, accs = refs[:n], refs[n], refs[n + 1:2 * n + 1], refs[2 * n + 1:]
        t = pl.program_id(0)

        @pl.when(t == 0)
        def _():
            for acc in accs:
                acc[...] = jnp.zeros_like(acc)

        for a_ref, acc in zip(a_refs, accs):
            acc[...] += _dot(a_ref[...], b_ref[...], 0, 0)

        @pl.when(t == steps - 1)
        def _():
            for o_ref, acc in zip(out_refs, accs):
                o_ref[...] = acc[...].astype(o_ref.dtype)

    return pl.pallas_call(
        body, name=name, grid=(steps,),
        in_specs=[pl.BlockSpec((tt, a.shape[1]), lambda t: (t, 0)) for a in pieces]
        + [pl.BlockSpec((tt, N), lambda t: (t, 0))],
        out_specs=[pl.BlockSpec((a.shape[1], N), lambda t: (0, 0)) for a in pieces],
        out_shape=[_sds((a.shape[1], N), out_dtype) for a in pieces],
        scratch_shapes=[pltpu.VMEM((a.shape[1], N), F32) for a in pieces],
        compiler_params=_params(("arbitrary",)))(*pieces, b)


def _first(ps, es):
    return (ps[0],)


def _rows(name, body, ins, out_shapes, T, tr=ROW_TILE):
    tr = min(tr, T)

    def spec(shape):
        if shape[0] == T:
            return pl.BlockSpec((tr,) + tuple(shape[1:]), lambda i: (i,) + (0,) * (len(shape) - 1))
        return pl.BlockSpec(tuple(shape), lambda i: (0,) * len(shape))

    return pl.pallas_call(
        body, name=name, grid=(T // tr,),
        in_specs=[spec(a.shape) for a in ins], out_specs=[spec(s.shape) for s in out_shapes],
        out_shape=out_shapes, compiler_params=_params(("arbitrary",)))(*ins)


def _rms(x):
    r = lax.rsqrt(jnp.mean(x * x, axis=-1, keepdims=True) + NORM_EPS)
    return x * r, r


def _rms_bwd(dxn, xn, r):
    return r * (dxn - xn * jnp.mean(dxn * xn, axis=-1, keepdims=True))


def _colsum(v):
    return jnp.sum(v, axis=0, keepdims=True)


def _proj_in(x, g, sc, sh, groups):
    T = x.shape[0]
    tm = min(MM_TM, T)
    ng = len(groups)

    def body(x_ref, g_ref, sc_ref, sh_ref, *rest):
        w_refs, h_ref, out_refs = rest[:ng], rest[ng], rest[ng + 1:]
        xn, _ = _rms(x_ref[...])
        h = (xn * g_ref[...] * (1.0 + sc_ref[...]) + sh_ref[...]).astype(BF16)
        h_ref[...] = h
        for w_ref, o_ref, (w, _, cn) in zip(w_refs, out_refs, groups):
            for j in range(w.shape[0] // cn):
                cols = slice(j * cn, (j + 1) * cn)
                o_ref[:, cols] = _dot(h, w_ref[cols, :], 1, 1).astype(o_ref.dtype)

    row = pl.BlockSpec((1, D_MODEL), lambda i: (0, 0))
    tile = lambda w: pl.BlockSpec((tm, w), lambda i: (i, 0))
    return pl.pallas_call(
        body, name="proj_in", grid=(T // tm,),
        in_specs=[tile(D_MODEL), row, row, row] + [
            pl.BlockSpec(w.shape, lambda i: (0, 0), pipeline_mode=pl.Buffered(1)) for w, _, _ in groups],
        out_specs=[tile(D_MODEL)] + [tile(w.shape[0]) for w, _, _ in groups],
        out_shape=[_sds((T, D_MODEL), BF16)] + [_sds((T, w.shape[0]), dt) for w, dt, _ in groups],
        compiler_params=_params(("parallel",)))(x, g, sc, sh, *[w for w, _, _ in groups])


def _acc_rows(rows):
    w = rows[0].shape[1]
    return jnp.concatenate(rows + [jnp.zeros((8 - len(rows), w), F32)], axis=0)


def _res_norm_rows(ps, es):
    mix = ps[0]
    x, gate, gp, g2, sc, sh = es
    mh, _ = _rms(mix)
    x1 = x + gate * (mh * gp)
    xn, _ = _rms(x1)
    return [mix, x1, xn * g2 * (1.0 + sc) + sh], []


def _final_loss_rows(ps, es):
    x1, tgt, gate, gp = es
    fh, r = _rms(ps[0])
    e = x1 + gate * (fh * gp) - tgt
    loss = 0.5 * jnp.sum(jnp.mean(e * e, axis=-1, keepdims=True))
    dy = e * (1.0 / D_MODEL)
    acc = _acc_rows([_colsum(dy * fh * gp), _colsum(dy * gate * fh)])
    return [dy, _rms_bwd(dy * gate * gp, fh, r)], [acc, jnp.full((1, 128), loss, F32)]


def _res_norm_bwd_rows(ps, es):
    dh = ps[0]
    x1, mix, dy, sc, gate, g2, gp = es
    xn, r1 = _rms(x1)
    rows = [_colsum(dh * xn * g2), _colsum(dh), _colsum(dh * (1.0 + sc) * xn)]
    dx1 = dy + _rms_bwd(dh * (1.0 + sc) * g2, xn, r1)
    mh, rm = _rms(mix)
    rows += [_colsum(dx1 * mh * gp), _colsum(dx1 * gate * mh)]
    return [dx1, _rms_bwd(dx1 * gate * gp, mh, rm)], [_acc_rows(rows)]


def _pre_norm_bwd_rows(ps, es):
    dh = ps[0]
    x, dx1, g, sc = es
    xn, r = _rms(x)
    rows = [_colsum(dh * xn * g), _colsum(dh), _colsum(dh * (1.0 + sc) * xn)]
    return [dx1 + _rms_bwd(dh * (1.0 + sc) * g, xn, r)], [_acc_rows(rows)]


def _rope_tables(pos_col, inv_freq):
    T = pos_col.shape[0]

    def body(p_ref, f_ref, c_ref, s_ref):
        ang = p_ref[...].astype(F32) * f_ref[...]
        lane = lax.broadcasted_iota(jnp.int32, ang.shape, 1)
        c_ref[...] = jnp.cos(ang)
        s_ref[...] = jnp.where(lane % HEAD_DIM < HEAD_DIM // 2, -1.0, 1.0) * jnp.sin(ang)

    return _rows("rope_tables", body, [pos_col, inv_freq],
                 [_sds((T, 128), F32), _sds((T, 128), F32)], T, tr=512)


def _swap_halves(t):
    W = t.shape[1]
    lane = lax.broadcasted_iota(jnp.int32, t.shape, 1)
    half = HEAD_DIM // 2
    return jnp.where(lane % HEAD_DIM < half, pltpu.roll(t, W - half, 1), pltpu.roll(t, half, 1))


def _widen(c, W):
    return c if W == 128 else jnp.concatenate([c] * (W // 128), axis=1)


def _rope(t, c, s):
    W = t.shape[1]
    return t * _widen(c, W) + _swap_halves(t) * _widen(s, W)


def _unrope(dy, c, s):
    W = dy.shape[1]
    return dy * _widen(c, W) + _swap_halves(dy * _widen(s, W))


def _attn_mask(n):
    qi = lax.broadcasted_iota(jnp.int32, (ATTN_BLK, 2 * ATTN_BLK), 0)
    kj = lax.broadcasted_iota(jnp.int32, (ATTN_BLK, 2 * ATTN_BLK), 1)
    rel = kj - ATTN_BLK
    return (rel <= qi) & (qi - rel < WINDOW) & ((n > 0) | (kj >= ATTN_BLK))


def _attn_load(cur, prv, cc, sc, cp, sp):
    x, xp = cur[...], prv[...]
    q = _rope(x[:, :512], cc[...], sc[...]) * (HEAD_DIM ** -0.5)
    k = jnp.concatenate([_rope(xp[:, 512:640], cp[...], sp[...]),
                         _rope(x[:, 512:640], cc[...], sc[...])], axis=0)
    v = jnp.concatenate([xp[:, 640:768], x[:, 640:768]], axis=0)
    return q, k, v


ROLLED = tuple(h for h in range(N_Q_HEADS) if h % 2 != h // (N_Q_HEADS // N_KV_HEADS))


def _pair_heads(t):
    half = lax.broadcasted_iota(jnp.int32, (ATTN_BLK, 128), 1) // HEAD_DIM
    return jnp.stack([jnp.where(half == h % 2, t[:, 128 * (h // 2):128 * (h // 2) + 128], 0.0)
                      for h in range(N_Q_HEADS)])


def _kv_heads(t):
    half = lax.broadcasted_iota(jnp.int32, t.shape, 1) // HEAD_DIM
    tr = pltpu.roll(t, HEAD_DIM, 1)
    return jnp.stack([jnp.where(half == h % 2, tr if h in ROLLED else t, 0.0)
                      for h in range(N_Q_HEADS)])


def _sink_column(snk):
    return jnp.stack([jnp.full((1, 1), snk[0, h], F32) for h in range(N_Q_HEADS)])


def _attn_probs(qh, kh, mask, sink):
    s = jnp.where(mask, _bdot(qh, kh, 2, 2), NEG_INF)
    m = jnp.maximum(jnp.max(s, axis=-1, keepdims=True), sink)
    p = jnp.exp(s - m)
    es = jnp.exp(sink - m)
    rl = 1.0 / (jnp.sum(p, axis=-1, keepdims=True) + es)
    return p, es, rl


def _attn_specs(order):
    blk = lambda w: pl.BlockSpec((ATTN_BLK, w), lambda s: (order(s), 0))
    prv = lambda w: pl.BlockSpec((ATTN_BLK, w), lambda s: (jnp.maximum(order(s) - 1, 0), 0))
    return [blk(768), prv(768), blk(128), blk(128), prv(128), prv(128),
            pl.BlockSpec(memory_space=pltpu.SMEM)]


def _attn_fwd(pa, cos, sin, sinks):
    T = pa.shape[0]
    nb = T // ATTN_BLK

    def body(cur, prv, cc, sc, cp, sp, snk, y_ref):
        n = pl.program_id(0)
        q, k, v = _attn_load(cur, prv, cc, sc, cp, sp)
        mask = _attn_mask(n)
        half_q = lax.broadcasted_iota(jnp.int32, (ATTN_BLK, 128), 1) // HEAD_DIM
        half_k = lax.broadcasted_iota(jnp.int32, k.shape, 1) // HEAD_DIM
        moved = (pltpu.roll(k, HEAD_DIM, 1), pltpu.roll(v, HEAD_DIM, 1))
        for pair in range(N_Q_HEADS // 2):
            o = None
            for a in range(2):
                h = 2 * pair + a
                ku, vu = moved if h in ROLLED else (k, v)
                qh = jnp.where(half_q == a, q[:, 128 * pair:128 * pair + 128], 0.0).astype(BF16)
                kh = jnp.where(half_k == a, ku, 0.0).astype(BF16)
                vh = jnp.where(half_k == a, vu, 0.0).astype(BF16)
                s = jnp.where(mask, _dot(qh, kh, 1, 1), NEG_INF)
                m = jnp.maximum(jnp.max(s, axis=-1, keepdims=True), snk[0, h])
                p = jnp.exp(s - m)
                rl = 1.0 / (jnp.sum(p, axis=-1, keepdims=True) + jnp.exp(snk[0, h] - m))
                oh = _dot(p.astype(BF16), vh, 1, 0) * rl
                o = oh if o is None else o + oh
            y_ref[:, 128 * pair:128 * pair + 128] = o.astype(BF16)

    return pl.pallas_call(
        body, name="attn_fwd", grid=(nb,), in_specs=_attn_specs(lambda s: s),
        out_specs=pl.BlockSpec((ATTN_BLK, 512), lambda n: (n, 0)),
        out_shape=_sds((T, 512), BF16), compiler_params=_params(("parallel",)))(
            pa, pa, cos, sin, cos, sin, sinks)


def _attn_bwd(pa, cos, sin, sinks, dy):
    T = pa.shape[0]
    nb = T // ATTN_BLK
    rev = lambda s: nb - 1 - s

    def body(cur, prv, cc, sc, cp, sp, snk, dy_ref, dq_ref, dkv_ref, dsink_ref, carry):
        n = rev(pl.program_id(0))

        @pl.when(pl.program_id(0) == 0)
        def _():
            dsink_ref[...] = jnp.zeros_like(dsink_ref)
            carry[...] = jnp.zeros_like(carry)

        q, k, v = _attn_load(cur, prv, cc, sc, cp, sp)
        qh, kh, vh = _pair_heads(q).astype(BF16), _kv_heads(k).astype(BF16), _kv_heads(v).astype(BF16)
        p, es, rl = _attn_probs(qh, kh, _attn_mask(n), _sink_column(snk))
        pn = p * rl
        do = _pair_heads(dy_ref[...]).astype(BF16)
        dp = _bdot(do, vh, 2, 2)
        delta = jnp.sum(pn * dp, axis=-1, keepdims=True)
        ds = (pn * (dp - delta)).astype(BF16)
        dsink = es * rl * delta
        dq = _bdot(ds, kh, 2, 1) * (HEAD_DIM ** -0.5)
        dkh = _bdot_rows(ds, qh)
        dvh = _bdot_rows(pn.astype(BF16), do)

        def fold(t):
            same = [t[h] for h in range(N_Q_HEADS) if h not in ROLLED]
            moved = [t[h] for h in ROLLED]
            return sum(same[1:], same[0]) + pltpu.roll(sum(moved[1:], moved[0]), HEAD_DIM, 1)

        dk, dv = fold(dkh), fold(dvh)
        for h in range(N_Q_HEADS):
            dsink_ref[h:h + 1, :] += -jnp.sum(dsink[h])
        for pair in range(N_Q_HEADS // 2):
            dq_ref[:, 128 * pair:128 * pair + 128] = _unrope(
                dq[2 * pair] + dq[2 * pair + 1], cc[...], sc[...]).astype(BF16)
        dkv_ref[:, 0:128] = _unrope(dk[ATTN_BLK:] + carry[:, 0:128], cc[...], sc[...]).astype(BF16)
        dkv_ref[:, 128:256] = (dv[ATTN_BLK:] + carry[:, 128:256]).astype(BF16)
        carry[:, 0:128] = dk[:ATTN_BLK]
        carry[:, 128:256] = dv[:ATTN_BLK]

    blk = lambda w: pl.BlockSpec((ATTN_BLK, w), lambda s: (rev(s), 0))
    return pl.pallas_call(
        body, name="attn_bwd", grid=(nb,), in_specs=_attn_specs(rev) + [blk(512)],
        out_specs=[blk(512), blk(256), pl.BlockSpec((8, 128), lambda s: (0, 0))],
        out_shape=[_sds((T, 512), BF16), _sds((T, 256), BF16), _sds((8, 128), F32)],
        scratch_shapes=[pltpu.VMEM((ATTN_BLK, 256), F32)],
        compiler_params=_params(("arbitrary",)))(pa, pa, cos, sin, cos, sin, sinks, dy)


CONV_COLS = 2 * MLSTM_HEADS * MLSTM_HEAD_DIM


def _conv_pre(cur_ref, halo_ref, w_ref, b_ref, i, tr):
    xx = jnp.concatenate([jnp.where(i > 0, halo_ref[...], 0.0), cur_ref[...]], axis=0)
    taps = [(pltpu.roll(xx, CONV_WIDTH - 1 - j, 0) if j < CONV_WIDTH - 1 else xx)[8:8 + tr]
            for j in range(CONV_WIDTH)]
    pre = b_ref[...]
    for j in range(CONV_WIDTH):
        pre = pre + taps[j] * w_ref[j:j + 1, :]
    return pre, taps


def _conv_specs(T, tr):
    return [pl.BlockSpec((tr, CONV_COLS), lambda i: (i, 0)),
            pl.BlockSpec((8, CONV_COLS), lambda i: (jnp.maximum(i * (tr // 8) - 1, 0), 0)),
            pl.BlockSpec((CONV_WIDTH, CONV_COLS), lambda i: (0, 0)),
            pl.BlockSpec((1, CONV_COLS), lambda i: (0, 0))]


def _conv_fwd(pm, w, b):
    T = pm.shape[0]
    tr = min(ROW_TILE, T)

    def body(cur_ref, halo_ref, w_ref, b_ref, o_ref):
        pre, _ = _conv_pre(cur_ref, halo_ref, w_ref, b_ref, pl.program_id(0), tr)
        o_ref[...] = pre * _sigmoid(pre)

    return pl.pallas_call(
        body, name="conv_fwd", grid=(T // tr,), in_specs=_conv_specs(T, tr),
        out_specs=pl.BlockSpec((tr, CONV_COLS), lambda i: (i, 0)),
        out_shape=_sds((T, CONV_COLS), F32), compiler_params=_params(("parallel",)))(pm, pm, w, b)


def _conv_bwd(pqk, w, b, dqk):
    T = pqk.shape[0]
    tr = min(ROW_TILE, T)
    nt = T // tr

    def body(cur_ref, prev_ref, next_ref, w_ref, b_ref, d_ref, dnext_ref, du_ref, acc_ref):
        i = pl.program_id(0)

        @pl.when(i == 0)
        def _():
            acc_ref[...] = jnp.zeros_like(acc_ref)

        last = i == nt - 1
        xx = jnp.concatenate([jnp.where(i > 0, prev_ref[...], 0.0), cur_ref[...],
                              jnp.where(last, 0.0, next_ref[...])], axis=0)
        taps = [(pltpu.roll(xx, CONV_WIDTH - 1 - j, 0) if j < CONV_WIDTH - 1 else xx)[8:16 + tr]
                for j in range(CONV_WIDTH)]
        pre = b_ref[...]
        for j in range(CONV_WIDTH):
            pre = pre + taps[j] * w_ref[j:j + 1, :]
        sg = _sigmoid(pre)
        dd = jnp.concatenate([d_ref[...], jnp.where(last, 0.0, dnext_ref[...])], axis=0)
        dpre = dd * (sg * (1.0 + pre * (1.0 - sg)))
        for j in range(CONV_WIDTH):
            acc_ref[j:j + 1, :] += _colsum(dpre[:tr] * taps[j][:tr])
        acc_ref[CONV_WIDTH:CONV_WIDTH + 1, :] += _colsum(dpre[:tr])
        du = dpre[:tr] * w_ref[CONV_WIDTH - 1:CONV_WIDTH, :]
        for j in range(CONV_WIDTH - 1):
            k = CONV_WIDTH - 1 - j
            du = du + pltpu.roll(dpre, tr + 8 - k, 0)[:tr] * w_ref[j:j + 1, :]
        du_ref[...] = du.astype(BF16)

    tile = pl.BlockSpec((tr, CONV_COLS), lambda i: (i, 0))
    after = pl.BlockSpec((8, CONV_COLS), lambda i: (jnp.minimum((i + 1) * (tr // 8), T // 8 - 1), 0))
    before = pl.BlockSpec((8, CONV_COLS), lambda i: (jnp.maximum(i * (tr // 8) - 1, 0), 0))
    return pl.pallas_call(
        body, name="conv_bwd", grid=(nt,),
        in_specs=[tile, before, after, pl.BlockSpec((CONV_WIDTH, CONV_COLS), lambda i: (0, 0)),
                  pl.BlockSpec((1, CONV_COLS), lambda i: (0, 0)), tile, after],
        out_specs=[tile, pl.BlockSpec((8, CONV_COLS), lambda i: (0, 0))],
        out_shape=[_sds((T, CONV_COLS), BF16), _sds((8, CONV_COLS), F32)],
        compiler_params=_params(("arbitrary",)))(pqk, pqk, pqk, w, b, dqk, dqk)


def _log_sigmoid(x):
    return jnp.minimum(x, 0.0) - jnp.log1p(jnp.exp(-jnp.abs(x)))


def _chunk_cumsum(x, axis):
    idx = lax.broadcasted_iota(jnp.int32, x.shape, axis) % MLSTM_CHUNK
    k = 1
    while k < MLSTM_CHUNK:
        x = x + jnp.where(idx >= k, pltpu.roll(x, k, axis), 0.0)
        k *= 2
    return x


def _chunk_rev_cumsum(x, axis):
    n = x.shape[axis]
    idx = lax.broadcasted_iota(jnp.int32, x.shape, axis) % MLSTM_CHUNK
    k = 1
    while k < MLSTM_CHUNK:
        x = x + jnp.where(idx < MLSTM_CHUNK - k, pltpu.roll(x, n - k, axis), 0.0)
        k *= 2
    return x


def _mlstm_gates(gc_ref, bc_ref, gr_ref, br_ref):
    gc = gc_ref[...] + bc_ref[...]
    gr = gr_ref[...] + br_ref[...]
    return gc, _chunk_cumsum(_log_sigmoid(gc), 0), gr, _chunk_cumsum(_log_sigmoid(gr), 1)


def _heads(ref, base=0):
    D = MLSTM_HEAD_DIM
    return jnp.stack([ref[:, base + D * h:base + D * h + D] for h in range(MLSTM_HEADS)])


def _mlstm_inputs(q_ref, k_ref, v_ref, gc, bc, gr, br):
    H = MLSTM_HEADS
    q, v = _heads(q_ref), _heads(v_ref)
    ks = _heads(k_ref) * (MLSTM_HEAD_DIM ** -0.5)
    return dict(
        q=q, ks=ks, qb=q.astype(BF16), kb=ks.astype(BF16), vb=v.astype(BF16),
        b_col=jnp.stack([bc[:, H + h:H + h + 1] for h in range(H)]),
        i_col=jnp.stack([gc[:, h:h + 1] for h in range(H)]),
        b_row=jnp.stack([br[H + h:H + h + 1, :] for h in range(H)]),
        i_row=jnp.stack([gr[h:h + 1, :] for h in range(H)]))


def _mlstm_head(f, c_prev, n_prev, m_prev):
    L = MLSTM_CHUNK
    q, qb = f["q"], f["qb"]
    t = lax.broadcasted_iota(jnp.int32, (1, 2 * L, 2 * L), 1)
    s = lax.broadcasted_iota(jnp.int32, (1, 2 * L, 2 * L), 2)
    mask = (t // L == s // L) & (s <= t)
    d = jnp.where(mask, f["b_col"] - f["b_row"] + f["i_row"], NEG_INF)
    row = lax.broadcasted_iota(jnp.int32, (1, 2 * L, 1), 1)
    inter = f["b_col"] + jnp.where(row < L, m_prev[0], m_prev[1])
    m_t = jnp.maximum(inter, jnp.max(d, axis=-1, keepdims=True))
    w_intra = jnp.exp(d - m_t)
    w_inter = jnp.exp(inter - m_t)
    sc = _bdot(qb, f["kb"], 2, 2) * w_intra
    qc = jnp.concatenate([_bdot(qb[:, :L], c_prev[0].astype(BF16), 2, 1),
                          _bdot(qb[:, L:], c_prev[1].astype(BF16), 2, 1)], axis=1)
    qn = jnp.concatenate([jnp.sum(q[:, :L] * n_prev[0], axis=-1, keepdims=True),
                          jnp.sum(q[:, L:] * n_prev[1], axis=-1, keepdims=True)], axis=1)
    num = _bdot(sc.astype(BF16), f["vb"], 2, 1) + w_inter * qc
    den = jnp.sum(sc, axis=-1, keepdims=True) + w_inter * qn
    return dict(f, w_intra=w_intra, w_inter=w_inter, sc=sc, qc=qc, qn=qn, num=num, den=den,
                floor=jnp.exp(-m_t))


def _mlstm_update(f, ch, c, n, m):
    L = MLSTM_CHUNK
    rows = slice(L * ch, L * ch + L)
    b_col = f["b_col"][:, rows]
    g_last = b_col[:, L - 1:L]
    a_col = g_last - b_col + f["i_col"][:, rows]
    m_new = jnp.maximum(g_last + m, jnp.max(a_col, axis=1, keepdims=True))
    decay = jnp.exp(g_last + m - m_new)
    e_a = jnp.exp(a_col - m_new)
    kw = f["ks"][:, rows] * e_a
    c_new = decay * c + _bdot_rows(kw.astype(BF16), f["vb"][:, rows])
    n_new = decay * n + jnp.sum(kw, axis=1, keepdims=True)
    return c_new, n_new, m_new, decay, e_a, kw


def _mlstm_specs(T, order):
    blk = lambda w, col: pl.BlockSpec((STEP_ROWS, w), lambda s: (order(s), col))
    return [blk(512, 0), blk(512, 1), blk(512, 0), blk(128, 0),
            pl.BlockSpec((1, 128), lambda s: (0, 0)),
            pl.BlockSpec((8, STEP_ROWS), lambda s: (0, order(s))),
            pl.BlockSpec((8, 128), lambda s: (0, 0))]


def _lanes(m):
    return jnp.broadcast_to(m, m.shape[:-1] + (128,))


def _mlstm_fwd(qk, pm, gcol, bcol, grow, brow, norm_w):
    T = qk.shape[0]
    steps = T // STEP_ROWS
    H, D = MLSTM_HEADS, MLSTM_HEAD_DIM

    def body(q_ref, k_ref, v_ref, gc_ref, bc_ref, gr_ref, br_ref, o_ref, w_ref,
             h_ref, y_ref, cs_ref, ns_ref, ms_ref, c_st, n_st, m_st):
        @pl.when(pl.program_id(0) == 0)
        def _():
            c_st[...] = jnp.zeros_like(c_st)
            n_st[...] = jnp.zeros_like(n_st)
            m_st[...] = jnp.zeros_like(m_st)

        f = _mlstm_inputs(q_ref, k_ref, v_ref, *_mlstm_gates(gc_ref, bc_ref, gr_ref, br_ref))
        c0, n0, m0 = c_st[...], n_st[...], m_st[:, :, 0:1]
        c1, n1, m1, _, _, _ = _mlstm_update(f, 0, c0, n0, m0)
        c2, n2, m2, _, _, _ = _mlstm_update(f, 1, c1, n1, m1)
        f = _mlstm_head(f, (c0, c1), (n0, n1), (m0, m1))
        h = f["num"] / jnp.maximum(jnp.abs(f["den"]), f["floor"])
        hn, _ = _head_norm(h)
        w = jnp.stack([w_ref[:, D * hd:D * hd + D] for hd in range(H)])
        y = _sigmoid(_heads(o_ref).astype(F32)) * hn * w
        for hd in range(H):
            h_ref[:, D * hd:D * hd + D] = h[hd]
            y_ref[:, D * hd:D * hd + D] = y[hd].astype(BF16)
        cs_ref[0], cs_ref[1] = c0, c1
        ns_ref[0], ns_ref[1] = n0, n1
        ms_ref[0], ms_ref[1] = _lanes(m0), _lanes(m1)
        c_st[...], n_st[...], m_st[...] = c2, n2, _lanes(m2)

    vec = pl.BlockSpec((2, H, 1, 128), lambda s: (s, 0, 0, 0))
    rows = pl.BlockSpec((STEP_ROWS, 512), lambda s: (s, 0))
    return pl.pallas_call(
        body, name="mlstm_fwd", grid=(steps,),
        in_specs=_mlstm_specs(T, lambda s: s) + [pl.BlockSpec((STEP_ROWS, 512), lambda s: (s, 1)),
                                                 pl.BlockSpec((1, 512), lambda s: (0, 0))],
        out_specs=[rows, rows, pl.BlockSpec((2, H, 128, 128), lambda s: (s, 0, 0, 0)), vec, vec],
        out_shape=[_sds((T, 512), F32), _sds((T, 512), BF16), _sds((2 * steps, H, 128, 128), F32),
                   _sds((2 * steps, H, 1, 128), F32), _sds((2 * steps, H, 1, 128), F32)],
        scratch_shapes=[pltpu.VMEM((H, 128, 128), F32), pltpu.VMEM((H, 1, 128), F32),
                        pltpu.VMEM((H, 1, 128), F32)],
        compiler_params=_params(("arbitrary",)))(qk, qk, pm, gcol, bcol, grow, brow, pm, norm_w)


def _mlstm_bwd(qk, pm, gcol, bcol, grow, brow, cs, ns, ms, dh):
    T = qk.shape[0]
    steps = T // STEP_ROWS
    H, L, D = MLSTM_HEADS, MLSTM_CHUNK, MLSTM_HEAD_DIM
    rev = lambda s: steps - 1 - s

    def body(q_ref, k_ref, v_ref, gc_ref, bc_ref, gr_ref, br_ref, cs_ref, ns_ref, ms_ref, dh_ref,
             dqk_ref, dv_ref, dgc_ref, dgr_ref, dc_st, dn_st):
        @pl.when(pl.program_id(0) == 0)
        def _():
            dc_st[...] = jnp.zeros_like(dc_st)
            dn_st[...] = jnp.zeros_like(dn_st)

        f = _mlstm_inputs(q_ref, k_ref, v_ref, *_mlstm_gates(gc_ref, bc_ref, gr_ref, br_ref))
        c_prev = (cs_ref[0], cs_ref[1])
        n_prev = (ns_ref[0], ns_ref[1])
        m_prev = (ms_ref[0, :, :, 0:1], ms_ref[1, :, :, 0:1])
        f = _mlstm_head(f, c_prev, n_prev, m_prev)
        big = jnp.abs(f["den"]) > f["floor"]
        rden = 1.0 / jnp.where(big, jnp.abs(f["den"]), f["floor"])
        dnum = _heads(dh_ref) * rden
        hdh = jnp.sum(f["num"] * dnum, axis=-1, keepdims=True)
        dden = jnp.where(big, -hdh * rden * jnp.sign(f["den"]), 0.0)
        dnum_b = dnum.astype(BF16)
        dsc = _bdot(dnum_b, f["vb"], 2, 2) + dden
        g = dsc * f["sc"]
        dv = _bdot_rows(f["sc"].astype(BF16), dnum_b)
        dqk_ = (dsc * f["w_intra"]).astype(BF16)
        dq = _bdot(dqk_, f["kb"], 2, 1)
        dks = _bdot_rows(dqk_, f["qb"])
        wdn = f["w_inter"] * dnum
        wdn_b = wdn.astype(BF16)
        wdd = f["w_inter"] * dden
        u = jnp.sum(f["qc"] * wdn, axis=-1, keepdims=True) + wdd * f["qn"]
        dks_s, dv_s, z_s, dg_s = [None, None], [None, None], [None, None], [None, None]
        dcn, dnn = dc_st[...], dn_st[...]
        for ch in (1, 0):
            rows = slice(L * ch, L * ch + L)
            _, _, _, decay, e_a, kw = _mlstm_update(f, ch, c_prev[ch], n_prev[ch], m_prev[ch])
            dcn_b = dcn.astype(BF16)
            dkw = _bdot(f["vb"][:, rows], dcn_b, 2, 2) + dnn
            dks_s[ch] = e_a * dkw
            dv_s[ch] = _bdot(kw.astype(BF16), dcn_b, 2, 1)
            z_s[ch] = e_a * jnp.sum(f["ks"][:, rows] * dkw, axis=-1, keepdims=True)
            dg_s[ch] = jnp.sum(z_s[ch], axis=1, keepdims=True) + decay * (
                jnp.sum(c_prev[ch] * dcn, axis=(1, 2), keepdims=True)
                + jnp.sum(n_prev[ch] * dnn, axis=(1, 2), keepdims=True))
            dcn = decay * dcn + _bdot_rows(f["qb"][:, rows], wdn_b[:, rows])
            dnn = decay * dnn + jnp.sum(wdd[:, rows] * f["q"][:, rows], axis=1, keepdims=True)
        dc_st[...], dn_st[...] = dcn, dnn
        dq = dq + jnp.concatenate(
            [_bdot(wdn_b[:, :L], c_prev[0].astype(BF16), 2, 2) + wdd[:, :L] * n_prev[0],
             _bdot(wdn_b[:, L:], c_prev[1].astype(BF16), 2, 2) + wdd[:, L:] * n_prev[1]], axis=1)
        dks = (dks + jnp.concatenate(dks_s, axis=1)) * (D ** -0.5)
        dv = dv + jnp.concatenate(dv_s, axis=1)
        z = jnp.concatenate(z_s, axis=1)
        row = lax.broadcasted_iota(jnp.int32, (1, STEP_ROWS, 1), 1)
        dg_col = jnp.where(row == L - 1, dg_s[0], 0.0) + jnp.where(row == 2 * L - 1, dg_s[1], 0.0)
        db_col = jnp.sum(g, axis=-1, keepdims=True) + u - z + dg_col
        g_row = jnp.sum(g, axis=1, keepdims=True)
        lane = lax.broadcasted_iota(jnp.int32, (STEP_ROWS, 128), 1)
        sub = lax.broadcasted_iota(jnp.int32, (8, STEP_ROWS), 0)
        dgc = jnp.zeros((STEP_ROWS, 128), F32)
        dgr = jnp.zeros((8, STEP_ROWS), F32)
        for hd in range(H):
            dgc = dgc + jnp.where(lane == hd, z[hd], 0.0) + jnp.where(lane == H + hd, db_col[hd], 0.0)
            dgr = dgr + jnp.where(sub == hd, g_row[hd], 0.0) - jnp.where(sub == H + hd, g_row[hd], 0.0)
            dqk_ref[:, D * hd:D * hd + D] = dq[hd]
            dqk_ref[:, H * D + D * hd:H * D + D * hd + D] = dks[hd]
            dv_ref[:, D * hd:D * hd + D] = dv[hd].astype(BF16)
        dgc_ref[...] = dgc
        dgr_ref[...] = dgr

    return pl.pallas_call(
        body, name="mlstm_bwd", grid=(steps,),
        in_specs=_mlstm_specs(T, rev) + [
            pl.BlockSpec((2, H, 128, 128), lambda s: (rev(s), 0, 0, 0)),
            pl.BlockSpec((2, H, 1, 128), lambda s: (rev(s), 0, 0, 0)),
            pl.BlockSpec((2, H, 1, 128), lambda s: (rev(s), 0, 0, 0)),
            pl.BlockSpec((STEP_ROWS, 512), lambda s: (rev(s), 0))],
        out_specs=[pl.BlockSpec((STEP_ROWS, 1024), lambda s: (rev(s), 0)),
                   pl.BlockSpec((STEP_ROWS, 512), lambda s: (rev(s), 0)),
                   pl.BlockSpec((STEP_ROWS, 128), lambda s: (rev(s), 0)),
                   pl.BlockSpec((8, STEP_ROWS), lambda s: (0, rev(s)))],
        out_shape=[_sds((T, 1024), F32), _sds((T, 512), BF16), _sds((T, 128), F32), _sds((8, T), F32)],
        scratch_shapes=[pltpu.VMEM((H, 128, 128), F32), pltpu.VMEM((H, 1, 128), F32)],
        compiler_params=_params(("arbitrary",)))(qk, qk, pm, gcol, bcol, grow, brow, cs, ns, ms, dh)


def _rows_to_lanes(x):
    eye = (lax.broadcasted_iota(jnp.int32, (8, 128), 0)
           == lax.broadcasted_iota(jnp.int32, (8, 128), 1)).astype(BF16)
    out, rest = None, x
    for _ in range(3):
        piece = rest.astype(BF16)
        rest = rest - piece.astype(F32)
        t = _dot(piece, eye, 0, 0)
        out = t if out is None else out + t
    return out


def _gate_bwd(dgc, dgr, gcol, bcol):
    T = dgc.shape[0]
    tr = min(ROW_TILE, T)

    def body(a_ref, b_ref, g_ref, bias_ref, o_ref, acc_ref):
        i = pl.program_id(0)

        @pl.when(i == 0)
        def _():
            acc_ref[...] = jnp.zeros_like(acc_ref)

        d = a_ref[...] + _rows_to_lanes(b_ref[:, pl.ds(pl.multiple_of(i * tr, 128), tr)])
        lane = lax.broadcasted_iota(jnp.int32, d.shape, 1)
        is_f = (lane >= MLSTM_HEADS) & (lane < 2 * MLSTM_HEADS)
        dlogf = _chunk_rev_cumsum(jnp.where(is_f, d, 0.0), 0)
        out = jnp.where(is_f, dlogf * _sigmoid(-(g_ref[...] + bias_ref[...])), d)
        o_ref[...] = out.astype(BF16)
        acc_ref[0:1, :] += _colsum(out)

    return _rows("gate_bwd", body, [dgc, dgr, gcol, bcol],
                 [_sds((T, 128), BF16), _sds((8, 128), F32)], T, tr=tr)


def _head_norm(h, mu_axis=-1):
    mu = jnp.mean(h, axis=-1, keepdims=True)
    hc = h - mu
    r = lax.rsqrt(jnp.mean(hc * hc, axis=-1, keepdims=True) + NORM_EPS)
    return hc * r, r


def _mlstm_out_bwd_rows(ps, es):
    hm, vo, w_all = es
    D, width = MLSTM_HEAD_DIM, MLSTM_HEADS * MLSTM_HEAD_DIM
    dhs, dos, dws = [], [], []
    for hd in range(MLSTM_HEADS):
        cols = slice(D * hd, D * hd + D)
        hn, r = _head_norm(hm[:, cols])
        sg = _sigmoid(vo[:, width + D * hd:width + D * hd + D].astype(F32))
        dy, w = ps[0][:, cols], w_all[:, cols]
        dos.append(dy * hn * w * sg * (1.0 - sg))
        dyn = dy * sg
        dws.append(_colsum(dyn * hn))
        dhn = dyn * w
        dhs.append(r * (dhn - jnp.mean(dhn, axis=-1, keepdims=True)
                        - hn * jnp.mean(dhn * hn, axis=-1, keepdims=True)))
    cat = lambda parts: jnp.concatenate(parts, axis=1)
    return [cat(dhs), cat(dos)], [_acc_rows([cat(dws)])]


ADAM_TILE_ELEMS = 256 * 1024


def _adamw(name, w, g, m, v):
    R, C = w.shape
    fits = [t for t in range(8, R + 1, 8) if R % t == 0 and t * C <= ADAM_TILE_ELEMS]
    if fits or R * C <= ADAM_TILE_ELEMS:
        tr = fits[-1] if fits else R
        spec, grid = pl.BlockSpec((tr, C), lambda i: (i, 0)), (R // tr,)
    else:
        spec, grid = pl.BlockSpec((R, 128), lambda i: (0, i)), (C // 128,)
    c1 = 1.0 - ADAM_B1 ** ADAM_STEP
    c2 = 1.0 - ADAM_B2 ** ADAM_STEP

    def body(w_ref, g_ref, m_ref, v_ref, d_ref, mo_ref, vo_ref):
        g = g_ref[...]
        m = ADAM_B1 * m_ref[...] + (1.0 - ADAM_B1) * g
        v = ADAM_B2 * v_ref[...] + (1.0 - ADAM_B2) * (g * g)
        mo_ref[...] = m
        vo_ref[...] = v
        d_ref[...] = -ADAM_LR * ((m / c1) / (jnp.sqrt(v / c2) + ADAM_EPS) + ADAM_WD * w_ref[...])

    return pl.pallas_call(
        body, name=name, grid=grid, in_specs=[spec] * 4, out_specs=[spec] * 3,
        out_shape=[_sds((R, C), F32)] * 3, compiler_params=_params(("parallel",)))(w, g, m, v)


def _place():
    return lax.axis_index("x"), lax.axis_index("y"), lax.axis_index("c")


def _all_gather8(name, blk, space):
    m, n = blk.shape

    def body(x_ref, out_ref, send_sems, recv_sems, local_sem):
        x, y, c = _place()
        me, sibling = (x, y, c), (x, y, 1 - c)
        chips = [(1 - x, y), (x, 1 - y), (1 - x, 1 - y)]

        def rows(px, py, pc):
            return out_ref.at[pl.ds((4 * px + 2 * py + pc) * m, m), :]

        def copy(k, block, to, src=None):
            return pltpu.make_async_remote_copy(
                src_ref=rows(*block) if src is None else src, dst_ref=rows(*block),
                send_sem=send_sems.at[k], recv_sem=recv_sems.at[k],
                device_id=to, device_id_type=MESH)

        mine = pltpu.make_async_copy(x_ref, rows(*me), local_sem)
        mine.start()
        first = [copy(0, me, sibling, src=x_ref)]
        first += [copy(1 + j, me, (*chip, c), src=x_ref) for j, chip in enumerate(chips)]
        for cp in first:
            cp.start()
        passed = [copy(4 + j, (*chip, c), sibling) for j, chip in enumerate(chips)]
        for j, chip in enumerate(chips):
            copy(1 + j, (*chip, c), me).wait_recv()
            passed[j].start()
        copy(0, sibling, me).wait_recv()
        for j, chip in enumerate(chips):
            copy(4 + j, (*chip, 1 - c), me).wait_recv()
        for cp in first + passed:
            cp.wait_send()
        mine.wait()

    return pl.pallas_call(
        body, name=name, out_shape=_sds((8 * m, n), blk.dtype),
        in_specs=[pl.BlockSpec(memory_space=space)], out_specs=pl.BlockSpec(memory_space=space),
        scratch_shapes=[pltpu.SemaphoreType.DMA((7,)), pltpu.SemaphoreType.DMA((7,)),
                        pltpu.SemaphoreType.DMA],
        compiler_params=pltpu.CompilerParams(vmem_limit_bytes=VMEM_LIMIT))(blk)


def _hbm_specs(n):
    return [pl.BlockSpec(memory_space=pl.ANY)] * n


def _swap_halves_sibling(name, srcs):
    nw = len(srcs)

    def body(*refs):
        src_refs, dst_refs, send_sems, recv_sems = refs[:nw], refs[nw:2 * nw], refs[2 * nw], refs[2 * nw + 1]
        x, y, c = _place()
        cps = [pltpu.make_async_remote_copy(
            src_ref=src_refs[w].at[pl.ds(0, 4), 1 - c], dst_ref=dst_refs[w],
            send_sem=send_sems.at[w], recv_sem=recv_sems.at[w], device_id=(x, y, 1 - c),
            device_id_type=MESH) for w in range(nw)]
        for cp in cps:
            cp.start()
        for cp in cps:
            cp.wait()

    return pl.pallas_call(
        body, name=name, out_shape=[_sds(s.shape[:1] + s.shape[2:], s.dtype) for s in srcs],
        in_specs=_hbm_specs(nw), out_specs=_hbm_specs(nw),
        scratch_shapes=[pltpu.SemaphoreType.DMA((nw,)), pltpu.SemaphoreType.DMA((nw,))])(*srcs)


def _split_start(name, srcs, lands, copies, per_array, after):
    nw = len(srcs)

    def body(*refs):
        send_sems, recv_sems, token = refs[2 * nw + 1], refs[2 * nw + 2], refs[-1]
        for w in range(nw):
            for k, (s, d, dev) in enumerate(copies(refs[w], refs[nw + w], *_place())):
                pltpu.make_async_remote_copy(
                    src_ref=s, dst_ref=d, send_sem=send_sems.at[w * per_array + k],
                    recv_sem=recv_sems.at[w * per_array + k], device_id=dev, device_id_type=MESH).start()
        token[...] = jnp.zeros_like(token)

    hbm, sem = pl.BlockSpec(memory_space=pltpu.HBM), pl.BlockSpec(memory_space=pltpu.SEMAPHORE)
    arrays = list(srcs) + list(lands)
    out = pl.pallas_call(
        body, name=name,
        out_shape=(pltpu.SemaphoreType.DMA((nw * per_array,)), pltpu.SemaphoreType.DMA((nw * per_array,)),
                   *[pltpu.HBM(a.shape, a.dtype) for a in arrays], _sds((8, 128), F32)),
        in_specs=[hbm] * (2 * nw) + [pl.BlockSpec(memory_space=pl.ANY)],
        out_specs=(sem, sem, *[hbm] * (2 * nw), pl.BlockSpec(memory_space=pltpu.VMEM)),
        input_output_aliases={i: 2 + i for i in range(2 * nw)},
        compiler_params=pltpu.CompilerParams(has_side_effects=pltpu.SideEffectType.DATAFLOW_SIDE_EFFECTING))(
            *[pltpu.with_memory_space_constraint(a, pltpu.HBM) for a in arrays], after)
    return out[0], out[1], out[2:2 + nw], out[2 + nw:2 + 2 * nw], out[-1]


def _split_wait(name, started, after, waits, per_array, with_srcs=False):
    send_sems, recv_sems, srcs, lands, _ = started
    nw = len(srcs)

    def body(*refs):
        send_sems, recv_sems = refs[2 * nw], refs[2 * nw + 1]
        x, y, c = _place()
        for w in range(nw):
            for k, (s, d) in enumerate(waits(refs[w], refs[nw + w], x, y, c)):
                cp = pltpu.make_async_remote_copy(
                    src_ref=s, dst_ref=d, send_sem=send_sems.at[w * per_array + k],
                    recv_sem=recv_sems.at[w * per_array + k], device_id=(x, y, 1 - c),
                    device_id_type=MESH)
                cp.wait_send()
                cp.wait_recv()

    hbm, sem = pl.BlockSpec(memory_space=pltpu.HBM), pl.BlockSpec(memory_space=pltpu.SEMAPHORE)
    arrays = list(srcs) + list(lands)
    out = pl.pallas_call(
        body, name=name, out_shape=tuple(pltpu.HBM(a.shape, a.dtype) for a in arrays),
        in_specs=[hbm] * (2 * nw) + [sem, sem, pl.BlockSpec(memory_space=pl.ANY)],
        out_specs=tuple([hbm] * (2 * nw)), input_output_aliases={i: i for i in range(2 * nw)},
        compiler_params=pltpu.CompilerParams(has_side_effects=pltpu.SideEffectType.DATAFLOW_SIDE_EFFECTING))(
            *arrays, send_sems, recv_sems, after)
    return (list(out[:nw]), list(out[nw:])) if with_srcs else list(out[nw:])


def _sibling_sends(src_ref, land_ref, x, y, c):
    return [(src_ref.at[pl.ds(0, 4), 1 - c], land_ref, (x, y, 1 - c))]


def _sibling_lands(src_ref, land_ref, x, y, c):
    return [(src_ref.at[pl.ds(0, 4), 1 - c], land_ref)]


def _other_chips(x, y):
    return [(1 - x, y), (x, 1 - y), (1 - x, 1 - y)]


def _gather_sends(src_ref, land_ref, x, y, c):
    to = land_ref.at[2 * x + y, c]
    return [(src_ref, to, (x, y, 1 - c))] + [(src_ref, to, (px, py, c)) for px, py in _other_chips(x, y)]


def _gather_lands(src_ref, land_ref, x, y, c):
    return [(src_ref, land_ref.at[2 * x + y, 1 - c])] + [
        (src_ref, land_ref.at[2 * px + py, c]) for px, py in _other_chips(x, y)]


def _gather_sends_all(src_ref, land_ref, x, y, c):
    to = land_ref.at[2 * x + y, c]
    return [(src_ref, to, (x, y, 1 - c))] + [
        (src_ref, to, (px, py, pc)) for px, py in _other_chips(x, y) for pc in (c, 1 - c)]


def _gather_lands_all(src_ref, land_ref, x, y, c):
    return [(src_ref, land_ref.at[2 * x + y, 1 - c])] + [
        (src_ref, land_ref.at[2 * px + py, pc]) for px, py in _other_chips(x, y) for pc in (c, 1 - c)]


def _scatter_sends(src_ref, land_ref, x, y, c):
    return [(src_ref.at[2 * px + py], land_ref.at[2 * x + y], (px, py, c)) for px, py in _other_chips(x, y)]


def _scatter_lands(src_ref, land_ref, x, y, c):
    return [(src_ref.at[2 * x + y], land_ref.at[2 * px + py]) for px, py in _other_chips(x, y)]


def _forward_sibling(name, lands):
    nw = len(lands)

    def body(*refs):
        land_refs, out_refs, send_sems, recv_sems = refs[:nw], refs[nw:2 * nw], refs[2 * nw], refs[2 * nw + 1]
        x, y, c = _place()
        cps = []
        for w in range(nw):
            cps += [pltpu.make_async_remote_copy(
                src_ref=land_refs[w].at[2 * px + py, c], dst_ref=out_refs[w].at[2 * px + py, c],
                send_sem=send_sems.at[w, j], recv_sem=recv_sems.at[w, j], device_id=(x, y, 1 - c),
                device_id_type=MESH) for j, (px, py) in enumerate(_other_chips(x, y))]
        for cp in cps:
            cp.start()
        for w in range(nw):
            for j, (px, py) in enumerate(_other_chips(x, y)):
                slot = out_refs[w].at[2 * px + py, 1 - c]
                pltpu.make_async_remote_copy(src_ref=slot, dst_ref=slot, send_sem=send_sems.at[w, j],
                                             recv_sem=recv_sems.at[w, j], device_id=(x, y, 1 - c),
                                             device_id_type=MESH).wait_recv()
        for cp in cps:
            cp.wait_send()

    return pl.pallas_call(
        body, name=name, out_shape=[_sds(a.shape, a.dtype) for a in lands],
        in_specs=_hbm_specs(nw), out_specs=_hbm_specs(nw), input_output_aliases={i: i for i in range(nw)},
        scratch_shapes=[pltpu.SemaphoreType.DMA((nw, 3)), pltpu.SemaphoreType.DMA((nw, 3))])(*lands)


def _share_halves(name, halves):
    nw = len(halves)

    def body(*refs):
        in_refs, out_refs, send_sems, recv_sems = refs[:nw], refs[nw:2 * nw], refs[2 * nw], refs[2 * nw + 1]
        x, y, c = _place()
        cps = [pltpu.make_async_remote_copy(
            src_ref=in_refs[w].at[c], dst_ref=out_refs[w].at[c], send_sem=send_sems.at[w],
            recv_sem=recv_sems.at[w], device_id=(x, y, 1 - c), device_id_type=MESH) for w in range(nw)]
        for cp in cps:
            cp.start()
        for w in range(nw):
            slot = out_refs[w].at[1 - c]
            pltpu.make_async_remote_copy(src_ref=slot, dst_ref=slot, send_sem=send_sems.at[w],
                                         recv_sem=recv_sems.at[w], device_id=(x, y, 1 - c),
                                         device_id_type=MESH).wait_recv()
        for cp in cps:
            cp.wait_send()

    return pl.pallas_call(
        body, name=name, out_shape=[_sds(a.shape, a.dtype) for a in halves],
        in_specs=_hbm_specs(nw), out_specs=_hbm_specs(nw), input_output_aliases={i: i for i in range(nw)},
        scratch_shapes=[pltpu.SemaphoreType.DMA((nw,)), pltpu.SemaphoreType.DMA((nw,))])(*halves)


def _place_blocks(name, blks, place):
    nw = len(blks)

    def body(p_ref, *refs):
        for b_ref, o_ref in zip(refs[:nw], refs[nw:]):
            o_ref[...] = b_ref[...]

    return pl.pallas_call(
        body, name=name,
        grid_spec=pltpu.PrefetchScalarGridSpec(
            num_scalar_prefetch=1, grid=(1,),
            in_specs=[pl.BlockSpec(b.shape, lambda i, p: (0, 0)) for b in blks],
            out_specs=[pl.BlockSpec((None, None) + b.shape, lambda i, p: (p[0], p[1], 0, 0)) for b in blks]),
        out_shape=[_sds((4, 2) + b.shape, b.dtype) for b in blks],
        compiler_params=_params(("arbitrary",)))(place, *blks)


def _pair_sum(name, fulls, gots, place):
    nw = len(fulls)

    def body(p_ref, *refs):
        s = pl.program_id(0)
        for a_ref, b_ref, o_ref, l_ref in zip(refs[:nw], refs[nw:2 * nw], refs[2 * nw:3 * nw], refs[3 * nw:]):
            o_ref[...] = (a_ref[...].astype(F32) + b_ref[...].astype(F32)).astype(o_ref.dtype)

            @pl.when(s == p_ref[0])
            def _():
                l_ref[...] = o_ref[...]

    slab = lambda a: pl.BlockSpec((None,) + a.shape[1:], lambda s, p: (s, 0, 0))
    mine = lambda a: pl.BlockSpec((None,) + a.shape[1:], lambda s, p: (p[0], 0, 0))
    out = pl.pallas_call(
        body, name=name,
        grid_spec=pltpu.PrefetchScalarGridSpec(
            num_scalar_prefetch=1, grid=(4,),
            in_specs=[pl.BlockSpec((None, None) + a.shape[2:], lambda s, p: (s, p[1], 0, 0)) for a in fulls]
            + [slab(b) for b in gots],
            out_specs=[slab(b) for b in gots] + [mine(b) for b in gots]),
        out_shape=[_sds(b.shape, BF16) for b in gots] * 2,
        compiler_params=_params(("arbitrary",)))(place, *fulls, *gots)
    return out[:nw], out[nw:]


def _sum4(name, arrs, place):
    nw = len(arrs)

    def body(p_ref, *refs):
        for a_ref, o_ref in zip(refs[:nw], refs[nw:]):
            acc = a_ref[0].astype(F32)
            for s in range(1, 4):
                acc = acc + a_ref[s].astype(F32)
            o_ref[...] = acc

    return pl.pallas_call(
        body, name=name,
        grid_spec=pltpu.PrefetchScalarGridSpec(
            num_scalar_prefetch=1, grid=(1,),
            in_specs=[pl.BlockSpec(a.shape, lambda i, p: (0, 0, 0)) for a in arrs],
            out_specs=[pl.BlockSpec((None,) + a.shape[1:], lambda i, p: (p[1], 0, 0)) for a in arrs]),
        out_shape=[_sds((2,) + a.shape[1:], F32) for a in arrs],
        compiler_params=_params(("arbitrary",)))(place, *arrs)


def _small_update(gathered, params, slots):
    c1 = 1.0 - ADAM_B1 ** ADAM_STEP
    c2 = 1.0 - ADAM_B2 ** ADAM_STEP
    n = len(params)

    def body(g_ref, *refs):
        ins, sum_ref, outs = refs[:3 * n], refs[3 * n], refs[3 * n + 1:]
        g_all = g_ref[0:1, :]
        for d in range(1, 8):
            g_all = g_all + g_ref[d:d + 1, :]
        sum_ref[...] = g_all
        for k, (off, width) in enumerate(slots):
            w_ref, m_ref, v_ref = ins[3 * k:3 * k + 3]
            go_ref, d_ref, mo_ref, vo_ref = outs[4 * k:4 * k + 4]
            g = g_all[:, off:off + width]
            m = ADAM_B1 * m_ref[...] + (1.0 - ADAM_B1) * g
            v = ADAM_B2 * v_ref[...] + (1.0 - ADAM_B2) * (g * g)
            go_ref[...], mo_ref[...], vo_ref[...] = g, m, v
            d_ref[...] = -ADAM_LR * ((m / c1) / (jnp.sqrt(v / c2) + ADAM_EPS) + ADAM_WD * w_ref[...])

    flat = [a for p in params for a in p]
    out = pl.pallas_call(
        body, name="small_update",
        out_shape=[_sds((1, gathered.shape[1]), F32)] + [_sds(p[0].shape, F32) for p in params for _ in range(4)],
        compiler_params=pltpu.CompilerParams(vmem_limit_bytes=VMEM_LIMIT))(gathered, *flat)
    return out[0], [tuple(out[1 + 4 * k:5 + 4 * k]) for k in range(n)]


def _swiglu(ps, es):
    g, u = ps
    return g * _sigmoid(g) * u, g, u


def _swiglu_bwd(ps, es):
    g, u = es[0].astype(F32), es[1].astype(F32)
    sg = _sigmoid(g)
    return ps[0] * u * (sg * (1.0 + g * (1.0 - sg))), ps[0] * (g * sg)


def _merge(ps, es):
    ga, gm = [e.astype(F32) for e in es]
    return (_sigmoid(ga) * ps[0] + _sigmoid(gm) * ps[1],)


def _merge_bwd(ps, es):
    dm, a, b = ps
    ga, gm = [e.astype(F32) for e in es]
    sa, sm = _sigmoid(ga), _sigmoid(gm)
    return dm * sa, dm * sm, dm * a * (sa * (1.0 - sa)), dm * b * (sm * (1.0 - sm))


W_IN_PIECES = (("q", 512), ("kv", 256), ("mqk", 1024), ("mv", 512), ("mo", 512), ("if", 8),
               ("ga", 1024), ("gm", 1024))


def _local_step(x, tgt, pos_col, mod, sp, in_weights, late_weights, ffn_grads, mixer_grads,
                ffn_grads_late=lambda after: 0.0):
    sh_m, sc_m, gate_m, sh_f, sc_f, gate_f = mod
    inv = ROPE_THETA ** (-2.0 * jnp.arange(HEAD_DIM // 2, dtype=F32) / HEAD_DIM)
    cos, sin = _rope_tables(pos_col, jnp.tile(inv, 4).reshape(1, 128))
    W = dict(in_weights(cos))
    h, pa, pqk, pvo, pif, pg = _proj_in(x, sp["g_pre_mix"], sc_m, sh_m, [
        (W["q+kv"], F32, 256), (W["mqk"], F32, 512), (W["mv+mo"], BF16, 512), (W["if"], F32, 128),
        (W["ga+gm"], BF16, 512)])
    ya = _attn_fwd(pa, cos, sin, sp["sinks"])
    qk = _conv_fwd(pqk, sp["conv_w"], sp["conv_b"])
    bcol = jnp.pad(sp["b_if"], ((0, 0), (0, 120)))
    brow = jnp.broadcast_to(sp["b_if"].reshape(8, 1), (8, 128))
    grow = pif[:, :8].T
    hm, ym, cs, ns, ms = _mlstm_fwd(qk, pvo, pif, bcol, grow, brow, sp["norm_w"])
    W.update(late_weights(ym))
    w_fg, w_fu, w_fd = W["fg"], W["fu"], W["fd"]
    merged, = _mm("branches", [[(ya, W["ba"])], [(ym, W["bm"])]], [(pg, 0), (pg, 1)], _merge, [BF16],
                  cn=512, nt=True)
    wide, narrow = (D_MODEL, F32), (D_MODEL, BF16)
    mix, x1, h2 = _mm_rows("mix_out", [[(merged, W["out"])]],
                           [x, gate_m, sp["g_post_mix"], sp["g_pre_ffn"], sc_f, sh_f],
                           _res_norm_rows, [wide, wide, narrow], [], cn=512)
    act, gt, up = _mm("ffn_in", [[(h2, w_fg)], [(h2, w_fu)]], [], _swiglu, [BF16] * 3,
                      cn=256, nt=True)
    dy, dff, acc_l, loss = _mm_rows("ffn_down", [[(act, w_fd)]], [x1, tgt, gate_f, sp["g_post_ffn"]],
                                    _final_loss_rows, [wide, narrow], [(8, D_MODEL), (1, 128)], cn=512)

    G = {}
    dgt, dup = _mm("ffn_down_bwd", [[(dff, w_fd)]], [gt, up], _swiglu_bwd, [BF16, BF16],
                   cn=256, nt=True)
    g_fd, = _mm_tn_group("dw_ffn_down", [act], dff, BF16)
    g_fg, = _mm_tn_group("dw_ffn_gate", [dgt], h2, BF16)
    g_fu, = _mm_tn_group("dw_ffn_up", [dup], h2, BF16)
    tie = ffn_grads(g_fg, g_fu, g_fd)
    dx1, dmix, acc_r = _mm_rows(
        "ffn_in_bwd", [[(dgt, w_fg), (dup, w_fu)]],
        [x1, mix, dy, sc_f + tie, gate_m, sp["g_pre_ffn"], sp["g_post_mix"]],
        _res_norm_bwd_rows, [wide, narrow], [(8, D_MODEL)], cn=512, tm=256)
    d_a, d_m, dga, dgm = _mm("mix_out_bwd", [[(dmix, W["out"])], [(ya, W["ba"])], [(ym, W["bm"])]],
                             [(pg, 0), (pg, 1)], _merge_bwd, [BF16] * 4, cn=512, nt=True)
    G["out"], = _mm_tn_group("dw_out", [merged], dmix, BF16)
    dya, = _mm("branch_attn_bwd", [[(d_a, W["ba"])]], [], _first, [F32], cn=512)
    heads = MLSTM_HEADS * MLSTM_HEAD_DIM
    dhm, do_m, acc_n = _mm_rows("branch_mlstm_bwd", [[(d_m, W["bm"])]],
                                [hm, pvo, sp["norm_w"] + ffn_grads_late(dya)],
                                _mlstm_out_bwd_rows, [(heads, F32), (heads, BF16)], [(8, heads)], cn=512)
    G["ba"], = _mm_tn_group("dw_branch_attn", [d_a], ya, BF16)
    G["bm"], = _mm_tn_group("dw_branch_mlstm", [d_m], ym, BF16)
    dqk, dv_m, dgc, dgr = _mlstm_bwd(qk, pvo, pif, bcol, grow, brow, cs, ns, ms, dhm)
    dif, acc_g = _gate_bwd(dgc, dgr, pif, bcol)
    du, acc_c = _conv_bwd(pqk, sp["conv_w"], sp["conv_b"], dqk)
    dq_a, dkv, dsink = _attn_bwd(pa, cos, sin, sp["sinks"], dya)
    dproj = {"q": dq_a, "kv": dkv, "mqk": du, "mv": dv_m, "mo": do_m, "if": dif, "ga": dga, "gm": dgm}
    names = [k for k, _ in W_IN_PIECES]
    for part in (names[:4], names[4:]):
        G.update(zip(part, _mm_tn_group("dw_in_from_" + part[0], [dproj[k] for k in part], h, BF16)))
    w_tied = dict(W, **{"if": W["if"] + mixer_grads(G).astype(BF16)})
    dx, acc_p = _mm_rows("proj_bwd", [[(dproj[k], w_tied[k]) for k, _ in W_IN_PIECES]],
                         [x, dx1, sp["g_pre_mix"], sc_m], _pre_norm_bwd_rows, [wide], [(8, D_MODEL)], cn=512)

    small = {
        "mod": jnp.concatenate([acc_p[1], acc_p[0], acc_r[3], acc_r[1], acc_r[0], acc_l[0]]),
        "g_pre_mix": acc_p[2], "g_post_mix": acc_r[4], "b_if": acc_g[0, :8],
        "conv_w": acc_c[:CONV_WIDTH].reshape(-1), "conv_b": acc_c[CONV_WIDTH],
        "sinks": dsink[:, 0], "norm_w": acc_n[0], "g_pre_ffn": acc_r[2], "g_post_ffn": acc_l[1]}
    return loss, dx, small


IN_WIDTH = sum(n for _, n in W_IN_PIECES)
IN_SHARD = IN_WIDTH // 4
IN_SHARD_PAD = -(-IN_SHARD // 32) * 32


def _split_w_in(w_in_t):
    out, off, start = {}, 0, {}
    for k, n in W_IN_PIECES:
        out[k], start[k] = w_in_t[off:off + n], off
        off += n
    out["if"] = jnp.pad(out["if"], ((0, 120), (0, 0)))
    for name, first, last in (("q+kv", "q", "kv"), ("mv+mo", "mv", "mo"), ("ga+gm", "ga", "gm")):
        out[name] = w_in_t[start[first]:start[last] + out[last].shape[0]]
    return out


def _halves(a):
    return a.reshape(4, 2, a.shape[0] // 8, a.shape[1])


SMALL = (("b_ada", 6144), ("g_pre_mix", 1024), ("g_post_mix", 1024), ("b_if", 128), ("conv_w", 4096),
         ("conv_b", 1024), ("sinks", 128), ("norm_w", 512), ("g_pre_ffn", 1024), ("g_post_ffn", 1024))
SMALL_LEN = 8 * 2048


def _pack_small(vals):
    parts = []
    for k, n in SMALL:
        v = vals[k].reshape(-1)
        parts.append(jnp.pad(v, (0, n - v.shape[0])))
    flat = jnp.concatenate(parts)
    return jnp.pad(flat, (0, SMALL_LEN - flat.shape[0]))


def kernel(x, c, positions, w_ada, b_ada, g_pre_mix, g_post_mix, w_in, b_if, conv_w, conv_b, attn_sinks, mlstm_norm_w, w_branch_attn, w_branch_mlstm, w_out, g_pre_ffn, g_post_ffn, w_ffn_gate, w_ffn_up, w_ffn_down, loss_target, m_w_ada, m_b_ada, m_g_pre_mix, m_g_post_mix, m_w_in, m_b_if, m_conv_w, m_conv_b, m_attn_sinks, m_mlstm_norm_w, m_w_branch_attn, m_w_branch_mlstm, m_w_out, m_g_pre_ffn, m_g_post_ffn, m_w_ffn_gate, m_w_ffn_up, m_w_ffn_down, v_w_ada, v_b_ada, v_g_pre_mix, v_g_post_mix, v_w_in, v_b_if, v_conv_w, v_conv_b, v_attn_sinks, v_mlstm_norm_w, v_w_branch_attn, v_w_branch_mlstm, v_w_out, v_g_pre_ffn, v_g_post_ffn, v_w_ffn_gate, v_w_ffn_up, v_w_ffn_down):
    xi, yi, ci = _place()
    chip = 2 * xi + yi
    dev = 2 * chip + ci
    T = x.shape[1]
    ada_cols = w_ada.shape[2]

    place = jnp.stack([chip, ci]).astype(jnp.int32)

    def my_half(a):
        n = a.shape[0] // 2
        return lax.dynamic_slice_in_dim(a, ci * n, n, axis=0).astype(BF16)

    blk = jnp.concatenate([c.reshape(-1), conv_w.reshape(-1)]).reshape(8, 256)
    got = _all_gather8("gather_cond", blk, pltpu.VMEM).reshape(8, 2048)
    c_all = got[:, :D_MODEL].astype(BF16)
    conv_full = got[::2, D_MODEL:].reshape(4, CONV_WIDTH, -1).transpose(1, 0, 2).reshape(CONV_WIDTH, -1)

    b_sh = lax.dynamic_slice_in_dim(b_ada, chip * ada_cols, ada_cols, axis=1)
    mod_part, = _mm("ada_mod", [[(c_all, w_ada[0].astype(BF16))]], [b_sh],
                    lambda ps, es: (ps[0] + es[0],), [F32], cn=512, tm=8)
    mod_all = _all_gather8("gather_mod", mod_part, pltpu.VMEM).reshape(4, 2, 8, ada_cols)[:, 0]
    mod = lax.dynamic_index_in_dim(mod_all, dev, axis=1, keepdims=False).reshape(6, 1, D_MODEL)

    def gather_start(name, blks, after, sends, copies):
        return _split_start(name + "_start", blks, _place_blocks(name + "_place", blks, place),
                            sends, copies, after)

    w_in_t = jnp.pad(w_in[0].T, ((0, IN_SHARD_PAD - IN_SHARD), (0, 0)))
    in_started = gather_start("in_gather", [my_half(w_in_t)], mod, _gather_sends, 4)
    late_keys = ("fg", "fu", "fd", "out", "ba", "bm")
    late_started = gather_start(
        "late_gather",
        [my_half(w_ffn_gate[0].T), my_half(w_ffn_up[0].T), my_half(w_ffn_down[0]), my_half(w_out[0]),
         my_half(w_branch_attn[0].T), my_half(w_branch_mlstm[0].T)], in_started[4], _gather_sends_all, 7)
    mod = mod + (in_started[4][0, 0] + late_started[4][0, 0])

    def in_weights(after):
        g_in, = _forward_sibling("in_gather_forward",
                                 _split_wait("in_gather_wait", in_started, after, _gather_lands, 4))
        return _split_w_in(g_in.reshape(4, IN_SHARD_PAD, D_MODEL)[:, :IN_SHARD].reshape(IN_WIDTH, D_MODEL))

    def late_weights(after):
        lands = _split_wait("late_gather_wait", late_started, after, _gather_lands_all, 7)
        return {k: a.reshape(-1, a.shape[-1]) for k, a in zip(late_keys, lands)}

    sent = {}

    def scatter_start(name, groups):
        pairs, lands = _pair_sum(name + "_pair_sum", groups, _swap_halves_sibling(name + "_pair", groups), place)
        sent[name] = _split_start(name + "_start", pairs, lands, _scatter_sends, 3, pairs[0])
        return sent[name][4][0, 0]

    def ffn_grads(g_fg, g_fu, g_fd):
        groups = [_halves(g_fg), _halves(g_fu), _halves(g_fd)]
        lands = [jnp.zeros(g.shape[:1] + g.shape[2:], g.dtype) for g in groups]
        sent["rs_ffn_pair"] = _split_start("rs_ffn_pair_start", groups, lands, _sibling_sends, 1, g_fd)
        return sent["rs_ffn_pair"][4][0, 0]

    def ffn_grads_late(after):
        groups, gots = _split_wait("rs_ffn_pair_wait", sent["rs_ffn_pair"], after, _sibling_lands, 1,
                                   with_srcs=True)
        pairs, lands = _pair_sum("rs_ffn_pair_sum", groups, gots, place)
        sent["rs_ffn"] = _split_start("rs_ffn_start", pairs, lands, _scatter_sends, 3, pairs[0])
        return sent["rs_ffn"][4][0, 0]

    def mixer_grads(G):
        g_in_t = jnp.concatenate([G[k][:n] for k, n in W_IN_PIECES]).reshape(4, IN_SHARD, D_MODEL)
        g_in_t = jnp.pad(g_in_t, ((0, 0), (0, IN_SHARD_PAD - IN_SHARD), (0, 0)))
        return scatter_start("rs_mix", [g_in_t.reshape(4, 2, IN_SHARD_PAD // 2, D_MODEL), _halves(G["out"]),
                                        _halves(G["ba"]), _halves(G["bm"])])

    sp = {"g_pre_mix": g_pre_mix, "g_post_mix": g_post_mix, "b_if": b_if, "conv_w": conv_full,
          "conv_b": conv_b, "sinks": attn_sinks, "norm_w": mlstm_norm_w, "g_pre_ffn": g_pre_ffn,
          "g_post_ffn": g_post_ffn}
    loss, dx, small = _local_step(x[0], loss_target[0], positions.reshape(T, 1), [mod[i] for i in range(6)],
                                  sp, in_weights, late_weights, ffn_grads, mixer_grads, ffn_grads_late)

    reds = (_sum4("rs_ffn_chip_sum", _split_wait("rs_ffn_wait", sent["rs_ffn"], dx, _scatter_lands, 3), place)
            + _sum4("rs_mix_chip_sum", _split_wait("rs_mix_wait", sent["rs_mix"], dx, _scatter_lands, 3), place))
    gsh = {k: s.reshape(-1, s.shape[-1])
           for k, s in zip(("fg", "fu", "fd", "w_in", "out", "ba", "bm"), _share_halves("rs_share", reds))}
    gsh["w_in"] = gsh["w_in"][:IN_SHARD]

    small["b_ada"] = small.pop("mod")
    vec = _pack_small(small).reshape(8, 2048)
    g_all = _all_gather8("gather_small", vec, pltpu.VMEM).reshape(8, SMALL_LEN)
    dmod_sh = lax.dynamic_slice_in_dim(g_all[:, :6 * D_MODEL], chip * ada_cols, ada_cols, axis=1)
    g_w_ada, = _mm_tn_group("dw_ada", [c_all], dmod_sh.astype(BF16), F32)

    smalls = {"b_ada": (b_ada, m_b_ada, v_b_ada), "g_pre_mix": (g_pre_mix, m_g_pre_mix, v_g_pre_mix),
              "g_post_mix": (g_post_mix, m_g_post_mix, v_g_post_mix), "b_if": (b_if, m_b_if, v_b_if),
              "conv_b": (conv_b, m_conv_b, v_conv_b), "sinks": (attn_sinks, m_attn_sinks, v_attn_sinks),
              "norm_w": (mlstm_norm_w, m_mlstm_norm_w, v_mlstm_norm_w),
              "g_pre_ffn": (g_pre_ffn, m_g_pre_ffn, v_g_pre_ffn),
              "g_post_ffn": (g_post_ffn, m_g_post_ffn, v_g_post_ffn)}
    offsets, off = {}, 0
    for k, width in SMALL:
        offsets[k], off = off, off + width
    g_sum, updates = _small_update(g_all, list(smalls.values()),
                                   [(offsets[k], t[0].shape[1]) for k, t in smalls.items()])
    g_conv = g_sum[:, offsets["conv_w"]:offsets["conv_w"] + CONV_WIDTH * D_MODEL].reshape(1, CONV_WIDTH, D_MODEL)
    g_conv = lax.dynamic_slice_in_dim(g_conv, chip * conv_w.shape[2], conv_w.shape[2], axis=2)

    res = dict(zip(smalls, updates))
    res["conv_w"] = (g_conv, *[o[None] for o in _adamw("adam_conv_w", conv_w[0], g_conv[0], m_conv_w[0], v_conv_w[0])])
    res["w_ada"] = (g_w_ada[None], *[o[None] for o in _adamw("adam_w_ada", w_ada[0], g_w_ada, m_w_ada[0], v_w_ada[0])])
    bigs = {"w_in": (w_in, m_w_in, v_w_in), "ba": (w_branch_attn, m_w_branch_attn, v_w_branch_attn),
            "bm": (w_branch_mlstm, m_w_branch_mlstm, v_w_branch_mlstm), "out": (w_out, m_w_out, v_w_out),
            "fg": (w_ffn_gate, m_w_ffn_gate, v_w_ffn_gate), "fu": (w_ffn_up, m_w_ffn_up, v_w_ffn_up),
            "fd": (w_ffn_down, m_w_ffn_down, v_w_ffn_down)}
    for k, (w, m, v) in bigs.items():
        if k in ("w_in", "fg", "fu"):
            res[k] = tuple(o.T[None] for o in (gsh[k], *_adamw("adam_" + k, w[0].T, gsh[k], m[0].T, v[0].T)))
        else:
            g = gsh[k].T if k in ("ba", "bm") else gsh[k]
            res[k] = (g[None], *[o[None] for o in _adamw("adam_" + k, w[0], g, m[0], v[0])])

    order = ("w_ada", "b_ada", "g_pre_mix", "g_post_mix", "w_in", "b_if", "conv_w", "conv_b", "sinks",
             "norm_w", "ba", "bm", "out", "g_pre_ffn", "g_post_ffn", "fg", "fu", "fd")
    total = lax.psum(loss[0, 0], ("x", "y", "c"))
    return (total, dx[None], *[res[k][0] for k in order], *[res[k][1] for k in order],
            *[res[k][2] for k in order], *[res[k][3] for k in order])
```
